```python
import jax, jax.numpy as jnp
from jax import lax
import numpy as np

D_MODEL = 1024
BATCH = 8
SEQ = 8192
DEPTH = 1

N_META = 16
D_MIX = D_MODEL
GLA_WIDTH = D_MIX // 2
GLA_HEADS = 4
GLA_DV = GLA_WIDTH // GLA_HEADS
GLA_DK = GLA_DV // 2
GLA_KEY_WIDTH = GLA_HEADS * GLA_DK
GLA_GATE_RANK = 16
GLA_TAU = 16.0
GLA_CHUNK = 64
SWA_WIDTH = D_MIX - GLA_WIDTH
SWA_HEAD_DIM = 64
SWA_Q_HEADS = SWA_WIDTH // SWA_HEAD_DIM
SWA_KV_HEADS = 2
SWA_GROUP = SWA_Q_HEADS // SWA_KV_HEADS
SWA_KV_WIDTH = SWA_KV_HEADS * SWA_HEAD_DIM
WINDOW = 128
SWA_BLOCK = 128
ROPE_THETA = 10000.0
D_FF = 256 * ((8 * D_MODEL // 3 + 255) // 256)
NORM_EPS = 1e-6
NEG_INF = -1e30

IN_SPLITS = (GLA_KEY_WIDTH, GLA_KEY_WIDTH, GLA_WIDTH, GLA_WIDTH, GLA_GATE_RANK,
             SWA_WIDTH, SWA_KV_WIDTH, SWA_KV_WIDTH)
D_IN = sum(IN_SPLITS)

kernel_name = "hybrid_gla_swa_macaron_layer"


def rms_norm(x, w):
    xf = x.astype(jnp.float32)
    y = xf * lax.rsqrt(jnp.mean(xf * xf, axis=-1, keepdims=True) + NORM_EPS)
    return (y * w.astype(jnp.float32)).astype(x.dtype)


def swiglu(x, w_gate, w_up, w_down):
    return (jax.nn.silu(x @ w_gate) * (x @ w_up)) @ w_down


def rope(x, pos):
    hd = x.shape[-1]
    inv_freq = 1.0 / (ROPE_THETA ** (jnp.arange(0, hd, 2, dtype=jnp.float32) / hd))
    ang = pos.astype(jnp.float32)[:, None] * inv_freq[None, :]
    ang = jnp.concatenate([ang, ang], axis=-1)[:, None, :]
    xf = x.astype(jnp.float32)
    x1, x2 = jnp.split(xf, 2, axis=-1)
    rot = jnp.concatenate([-x2, x1], axis=-1)
    return (xf * jnp.cos(ang) + rot * jnp.sin(ang)).astype(x.dtype)


def gla_chunked(q, k, v, log_a):
    B, L, H, dk = q.shape
    dv = v.shape[-1]
    C = GLA_CHUNK
    pad = (-L) % C
    n = (L + pad) // C

    def to_chunks(t):
        t = jnp.pad(t, ((0, 0), (pad, 0), (0, 0), (0, 0)))
        return t.reshape(B, n, C, H, t.shape[-1]).transpose(1, 0, 3, 2, 4).astype(jnp.float32)

    qc = to_chunks(q) * (dk ** -0.5)
    kc = to_chunks(k)
    vc = to_chunks(v)
    bc = jnp.cumsum(to_chunks(log_a), axis=3)
    causal = jnp.tril(jnp.ones((C, C), dtype=bool))[None, None, :, :, None]

    def step(S, inp):
        qi, ki, vi, bi = inp
        diff = bi[:, :, :, None, :] - bi[:, :, None, :, :]
        decay = jnp.exp(jnp.where(causal, diff, -jnp.inf))
        attn = jnp.einsum('bhid,bhjd,bhijd->bhij', qi, ki, decay)
        o = jnp.einsum('bhij,bhjv->bhiv', attn, vi) + \
            jnp.einsum('bhid,bhdv->bhiv', qi * jnp.exp(bi), S)
        b_last = bi[:, :, -1:, :]
        S = jnp.exp(b_last[:, :, 0, :])[..., None] * S + \
            jnp.einsum('bhjd,bhjv->bhdv', ki * jnp.exp(b_last - bi), vi)
        return S, o

    S0 = jnp.zeros((B, H, dk, dv), jnp.float32)
    _, o = lax.scan(step, S0, (qc, kc, vc, bc))
    o = o.transpose(1, 0, 3, 2, 4).reshape(B, n * C, H, dv)[:, pad:]
    return o.astype(v.dtype)


def swa_with_sinks(q, k, v, sinks):
    B, L, HQ, hd = q.shape
    T = SWA_BLOCK
    KV, G = SWA_KV_HEADS, SWA_GROUP
    pad = (-L) % T
    Lp = L + pad
    nb = Lp // T
    qb = jnp.pad(q, ((0, 0), (pad, 0), (0, 0), (0, 0))).reshape(B, nb, T, KV, G, hd)
    kp = jnp.pad(k, ((0, 0), (pad + T, 0), (0, 0), (0, 0))).reshape(B, nb + 1, T, KV, hd)
    vp = jnp.pad(v, ((0, 0), (pad + T, 0), (0, 0), (0, 0))).reshape(B, nb + 1, T, KV, hd)
    k_band = jnp.concatenate([kp[:, :-1], kp[:, 1:]], axis=2)
    v_band = jnp.concatenate([vp[:, :-1], vp[:, 1:]], axis=2)
    k_meta = k[:, :N_META]
    v_meta = v[:, :N_META]

    qpos = (jnp.arange(Lp) - pad).reshape(nb, T)
    kpos_all = (jnp.arange(Lp + T) - pad - T).reshape(nb + 1, T)
    kpos = jnp.concatenate([kpos_all[:-1], kpos_all[1:]], axis=1)
    dq = qpos[:, :, None]
    dk_ = kpos[:, None, :]
    band_mask = (dk_ >= N_META) & (dk_ <= dq) & (dq - dk_ < WINDOW)
    meta_mask = jnp.arange(N_META)[None, None, :] <= dq
    mask = jnp.concatenate([band_mask, meta_mask], axis=-1)

    scale = hd ** -0.5
    s_band = jnp.einsum('bntkgd,bnskd->bnkgts', qb, k_band).astype(jnp.float32)
    s_meta = jnp.einsum('bntkgd,bmkd->bnkgtm', qb, k_meta).astype(jnp.float32)
    s = jnp.concatenate([s_band, s_meta], axis=-1) * scale
    s = jnp.where(mask[None, :, None, None], s, NEG_INF)
    sink = sinks.astype(jnp.float32).reshape(KV, G)[None, None, :, :, None, None]
    m = jnp.maximum(jnp.max(s, axis=-1, keepdims=True), sink)
    p = jnp.exp(s - m)
    p = p / (jnp.sum(p, axis=-1, keepdims=True) + jnp.exp(sink - m))
    p = p.astype(v.dtype)
    o = jnp.einsum('bnkgts,bnskd->bntkgd', p[..., :2 * T], v_band) + \
        jnp.einsum('bnkgtm,bmkd->bntkgd', p[..., 2 * T:], v_meta)
    return o.reshape(B, Lp, HQ, hd)[:, pad:]


def hybrid_mixer(h, w_in, gla_w_a2, gla_b_a, gla_out_norm, swa_sinks, swa_out_norm, w_out):
    B, L, _ = h.shape
    pos = jnp.arange(L, dtype=jnp.int32)
    proj = h @ w_in
    offsets = [int(o) for o in np.cumsum(IN_SPLITS)[:-1]]
    gq, gk, gv, gg, ga, sq, sk, sv = jnp.split(proj, offsets, axis=-1)

    log_a = jax.nn.log_sigmoid((ga @ gla_w_a2 + gla_b_a).astype(jnp.float32)) / GLA_TAU
    o_gla = gla_chunked(gq.reshape(B, L, GLA_HEADS, GLA_DK),
                        gk.reshape(B, L, GLA_HEADS, GLA_DK),
                        gv.reshape(B, L, GLA_HEADS, GLA_DV),
                        log_a.reshape(B, L, GLA_HEADS, GLA_DK))
    o_gla = rms_norm(o_gla, gla_out_norm) * jax.nn.silu(gg.reshape(B, L, GLA_HEADS, GLA_DV))
    o_gla = o_gla.reshape(B, L, GLA_WIDTH)

    q = rope(sq.reshape(B, L, SWA_Q_HEADS, SWA_HEAD_DIM), pos)
    k = rope(sk.reshape(B, L, SWA_KV_HEADS, SWA_HEAD_DIM), pos)
    v = sv.reshape(B, L, SWA_KV_HEADS, SWA_HEAD_DIM)
    o_swa = swa_with_sinks(q, k, v, swa_sinks).reshape(B, L, SWA_WIDTH)
    o_swa = rms_norm(o_swa, swa_out_norm)

    return jnp.concatenate([o_gla, o_swa], axis=-1) @ w_out


def _fwd_setup_inputs(seed: int = 0) -> dict:
    key = jax.random.key(seed)
    ks = jax.random.split(key, 24)
    f32 = jnp.float32
    nrm = lambda k, shape, s: jax.random.normal(k, shape, f32) * s
    gain = lambda k, shape: 1.0 + 0.05 * jax.random.normal(k, shape, f32)
    Dp = DEPTH
    return {
        "x": jax.random.normal(ks[0], (BATCH, SEQ, D_MODEL), f32),
        "meta_tokens": nrm(ks[1], (N_META, D_MODEL), 1.0),
        "ffn1_pre_norm": gain(ks[2], (Dp, D_MODEL)),
        "ffn1_w_gate": nrm(ks[3], (Dp, D_MODEL, D_FF), D_MODEL ** -0.5),
        "ffn1_w_up": nrm(ks[4], (Dp, D_MODEL, D_FF), D_MODEL ** -0.5),
        "ffn1_w_down": nrm(ks[5], (Dp, D_FF, D_MODEL), D_FF ** -0.5),
        "ffn1_post_norm": gain(ks[6], (Dp, D_MODEL)),
        "mix_pre_norm": gain(ks[7], (Dp, D_MODEL)),
        "w_in": nrm(ks[8], (Dp, D_MODEL, D_IN), D_MODEL ** -0.5),
        "gla_w_a2": nrm(ks[9], (Dp, GLA_GATE_RANK, GLA_KEY_WIDTH), GLA_GATE_RANK ** -0.5),
        "gla_b_a": nrm(ks[10], (Dp, GLA_KEY_WIDTH), 0.1),
        "gla_out_norm": gain(ks[11], (Dp, GLA_DV)),
        "swa_sinks": nrm(ks[12], (Dp, SWA_Q_HEADS), 0.5),
        "swa_out_norm": gain(ks[13], (Dp, SWA_WIDTH)),
        "w_out": nrm(ks[14], (Dp, D_MIX, D_MODEL), D_MIX ** -0.5),
        "mix_post_norm": gain(ks[15], (Dp, D_MODEL)),
        "ffn2_pre_norm": gain(ks[16], (Dp, D_MODEL)),
        "ffn2_w_gate": nrm(ks[17], (Dp, D_MODEL, D_FF), D_MODEL ** -0.5),
        "ffn2_w_up": nrm(ks[18], (Dp, D_MODEL, D_FF), D_MODEL ** -0.5),
        "ffn2_w_down": nrm(ks[19], (Dp, D_FF, D_MODEL), D_FF ** -0.5),
        "ffn2_post_norm": gain(ks[20], (Dp, D_MODEL)),
    }


def _fwd_reference(x, meta_tokens, ffn1_pre_norm, ffn1_w_gate, ffn1_w_up, ffn1_w_down, ffn1_post_norm,
              mix_pre_norm, w_in, gla_w_a2, gla_b_a, gla_out_norm, swa_sinks, swa_out_norm, w_out,
              mix_post_norm, ffn2_pre_norm, ffn2_w_gate, ffn2_w_up, ffn2_w_down, ffn2_post_norm):
    B = x.shape[0]
    meta = jnp.broadcast_to(meta_tokens.astype(x.dtype)[None], (B, N_META, x.shape[-1]))
    h = jnp.concatenate([meta, x], axis=1)
    for l in range(DEPTH):
        f = swiglu(rms_norm(h, ffn1_pre_norm[l]), ffn1_w_gate[l], ffn1_w_up[l], ffn1_w_down[l])
        h = h + 0.5 * rms_norm(f, ffn1_post_norm[l])
        m = hybrid_mixer(rms_norm(h, mix_pre_norm[l]), w_in[l], gla_w_a2[l], gla_b_a[l],
                         gla_out_norm[l], swa_sinks[l], swa_out_norm[l], w_out[l])
        h = h + rms_norm(m, mix_post_norm[l])
        f = swiglu(rms_norm(h, ffn2_pre_norm[l]), ffn2_w_gate[l], ffn2_w_up[l], ffn2_w_down[l])
        h = h + 0.5 * rms_norm(f, ffn2_post_norm[l])
    return h[:, N_META:]


import jax as _jax
import jax.numpy as _jnp

TWIN_FORMAT = 'train_step'
FWD_PARAMS = ['x', 'meta_tokens', 'ffn1_pre_norm', 'ffn1_w_gate', 'ffn1_w_up', 'ffn1_w_down', 'ffn1_post_norm', 'mix_pre_norm', 'w_in', 'gla_w_a2', 'gla_b_a', 'gla_out_norm', 'swa_sinks', 'swa_out_norm', 'w_out', 'mix_post_norm', 'ffn2_pre_norm', 'ffn2_w_gate', 'ffn2_w_up', 'ffn2_w_down', 'ffn2_post_norm']
TWIN_WEIGHTS = ['meta_tokens', 'ffn1_pre_norm', 'ffn1_w_gate', 'ffn1_w_up', 'ffn1_w_down', 'ffn1_post_norm', 'mix_pre_norm', 'w_in', 'gla_w_a2', 'gla_b_a', 'gla_out_norm', 'swa_sinks', 'swa_out_norm', 'w_out', 'mix_post_norm', 'ffn2_pre_norm', 'ffn2_w_gate', 'ffn2_w_up', 'ffn2_w_down', 'ffn2_post_norm']
TWIN_DIFF_INPUT = 'x'
TWIN_INPUTS = ['x', 'meta_tokens', 'ffn1_pre_norm', 'ffn1_w_gate', 'ffn1_w_up', 'ffn1_w_down', 'ffn1_post_norm', 'mix_pre_norm', 'w_in', 'gla_w_a2', 'gla_b_a', 'gla_out_norm', 'swa_sinks', 'swa_out_norm', 'w_out', 'mix_post_norm', 'ffn2_pre_norm', 'ffn2_w_gate', 'ffn2_w_up', 'ffn2_w_down', 'ffn2_post_norm', 'loss_target', 'm_meta_tokens', 'm_ffn1_pre_norm', 'm_ffn1_w_gate', 'm_ffn1_w_up', 'm_ffn1_w_down', 'm_ffn1_post_norm', 'm_mix_pre_norm', 'm_w_in', 'm_gla_w_a2', 'm_gla_b_a', 'm_gla_out_norm', 'm_swa_sinks', 'm_swa_out_norm', 'm_w_out', 'm_mix_post_norm', 'm_ffn2_pre_norm', 'm_ffn2_w_gate', 'm_ffn2_w_up', 'm_ffn2_w_down', 'm_ffn2_post_norm', 'v_meta_tokens', 'v_ffn1_pre_norm', 'v_ffn1_w_gate', 'v_ffn1_w_up', 'v_ffn1_w_down', 'v_ffn1_post_norm', 'v_mix_pre_norm', 'v_w_in', 'v_gla_w_a2', 'v_gla_b_a', 'v_gla_out_norm', 'v_swa_sinks', 'v_swa_out_norm', 'v_w_out', 'v_mix_post_norm', 'v_ffn2_pre_norm', 'v_ffn2_w_gate', 'v_ffn2_w_up', 'v_ffn2_w_down', 'v_ffn2_post_norm']
TWIN_OUTPUTS = ['loss', 'grad_x', 'grad_meta_tokens', 'grad_ffn1_pre_norm', 'grad_ffn1_w_gate', 'grad_ffn1_w_up', 'grad_ffn1_w_down', 'grad_ffn1_post_norm', 'grad_mix_pre_norm', 'grad_w_in', 'grad_gla_w_a2', 'grad_gla_b_a', 'grad_gla_out_norm', 'grad_swa_sinks', 'grad_swa_out_norm', 'grad_w_out', 'grad_mix_post_norm', 'grad_ffn2_pre_norm', 'grad_ffn2_w_gate', 'grad_ffn2_w_up', 'grad_ffn2_w_down', 'grad_ffn2_post_norm', 'delta_meta_tokens', 'delta_ffn1_pre_norm', 'delta_ffn1_w_gate', 'delta_ffn1_w_up', 'delta_ffn1_w_down', 'delta_ffn1_post_norm', 'delta_mix_pre_norm', 'delta_w_in', 'delta_gla_w_a2', 'delta_gla_b_a', 'delta_gla_out_norm', 'delta_swa_sinks', 'delta_swa_out_norm', 'delta_w_out', 'delta_mix_post_norm', 'delta_ffn2_pre_norm', 'delta_ffn2_w_gate', 'delta_ffn2_w_up', 'delta_ffn2_w_down', 'delta_ffn2_post_norm', 'new_m_meta_tokens', 'new_m_ffn1_pre_norm', 'new_m_ffn1_w_gate', 'new_m_ffn1_w_up', 'new_m_ffn1_w_down', 'new_m_ffn1_post_norm', 'new_m_mix_pre_norm', 'new_m_w_in', 'new_m_gla_w_a2', 'new_m_gla_b_a', 'new_m_gla_out_norm', 'new_m_swa_sinks', 'new_m_swa_out_norm', 'new_m_w_out', 'new_m_mix_post_norm', 'new_m_ffn2_pre_norm', 'new_m_ffn2_w_gate', 'new_m_ffn2_w_up', 'new_m_ffn2_w_down', 'new_m_ffn2_post_norm', 'new_v_meta_tokens', 'new_v_ffn1_pre_norm', 'new_v_ffn1_w_gate', 'new_v_ffn1_w_up', 'new_v_ffn1_w_down', 'new_v_ffn1_post_norm', 'new_v_mix_pre_norm', 'new_v_w_in', 'new_v_gla_w_a2', 'new_v_gla_b_a', 'new_v_gla_out_norm', 'new_v_swa_sinks', 'new_v_swa_out_norm', 'new_v_w_out', 'new_v_mix_post_norm', 'new_v_ffn2_pre_norm', 'new_v_ffn2_w_gate', 'new_v_ffn2_w_up', 'new_v_ffn2_w_down', 'new_v_ffn2_post_norm']
TWIN_LEAF_KINDS = {'loss': 'loss', 'grad_x': 'grad_x', 'grad_meta_tokens': 'grad_w', 'grad_ffn1_pre_norm': 'grad_w', 'grad_ffn1_w_gate': 'grad_w', 'grad_ffn1_w_up': 'grad_w', 'grad_ffn1_w_down': 'grad_w', 'grad_ffn1_post_norm': 'grad_w', 'grad_mix_pre_norm': 'grad_w', 'grad_w_in': 'grad_w', 'grad_gla_w_a2': 'grad_w', 'grad_gla_b_a': 'grad_w', 'grad_gla_out_norm': 'grad_w', 'grad_swa_sinks': 'grad_w', 'grad_swa_out_norm': 'grad_w', 'grad_w_out': 'grad_w', 'grad_mix_post_norm': 'grad_w', 'grad_ffn2_pre_norm': 'grad_w', 'grad_ffn2_w_gate': 'grad_w', 'grad_ffn2_w_up': 'grad_w', 'grad_ffn2_w_down': 'grad_w', 'grad_ffn2_post_norm': 'grad_w', 'delta_meta_tokens': 'delta_w', 'delta_ffn1_pre_norm': 'delta_w', 'delta_ffn1_w_gate': 'delta_w', 'delta_ffn1_w_up': 'delta_w', 'delta_ffn1_w_down': 'delta_w', 'delta_ffn1_post_norm': 'delta_w', 'delta_mix_pre_norm': 'delta_w', 'delta_w_in': 'delta_w', 'delta_gla_w_a2': 'delta_w', 'delta_gla_b_a': 'delta_w', 'delta_gla_out_norm': 'delta_w', 'delta_swa_sinks': 'delta_w', 'delta_swa_out_norm': 'delta_w', 'delta_w_out': 'delta_w', 'delta_mix_post_norm': 'delta_w', 'delta_ffn2_pre_norm': 'delta_w', 'delta_ffn2_w_gate': 'delta_w', 'delta_ffn2_w_up': 'delta_w', 'delta_ffn2_w_down': 'delta_w', 'delta_ffn2_post_norm': 'delta_w', 'new_m_meta_tokens': 'new_m', 'new_m_ffn1_pre_norm': 'new_m', 'new_m_ffn1_w_gate': 'new_m', 'new_m_ffn1_w_up': 'new_m', 'new_m_ffn1_w_down': 'new_m', 'new_m_ffn1_post_norm': 'new_m', 'new_m_mix_pre_norm': 'new_m', 'new_m_w_in': 'new_m', 'new_m_gla_w_a2': 'new_m', 'new_m_gla_b_a': 'new_m', 'new_m_gla_out_norm': 'new_m', 'new_m_swa_sinks': 'new_m', 'new_m_swa_out_norm': 'new_m', 'new_m_w_out': 'new_m', 'new_m_mix_post_norm': 'new_m', 'new_m_ffn2_pre_norm': 'new_m', 'new_m_ffn2_w_gate': 'new_m', 'new_m_ffn2_w_up': 'new_m', 'new_m_ffn2_w_down': 'new_m', 'new_m_ffn2_post_norm': 'new_m', 'new_v_meta_tokens': 'new_v', 'new_v_ffn1_pre_norm': 'new_v', 'new_v_ffn1_w_gate': 'new_v', 'new_v_ffn1_w_up': 'new_v', 'new_v_ffn1_w_down': 'new_v', 'new_v_ffn1_post_norm': 'new_v', 'new_v_mix_pre_norm': 'new_v', 'new_v_w_in': 'new_v', 'new_v_gla_w_a2': 'new_v', 'new_v_gla_b_a': 'new_v', 'new_v_gla_out_norm': 'new_v', 'new_v_swa_sinks': 'new_v', 'new_v_swa_out_norm': 'new_v', 'new_v_w_out': 'new_v', 'new_v_mix_post_norm': 'new_v', 'new_v_ffn2_pre_norm': 'new_v', 'new_v_ffn2_w_gate': 'new_v', 'new_v_ffn2_w_up': 'new_v', 'new_v_ffn2_w_down': 'new_v', 'new_v_ffn2_post_norm': 'new_v'}


def _forward(args):
    return _fwd_reference(*[args[k] for k in FWD_PARAMS])


def _output_shape():
    def fwd():
        inp = _fwd_setup_inputs(0)
        return _fwd_reference(*[inp[k] for k in FWD_PARAMS])
    out = _jax.eval_shape(fwd)
    return out.shape, out.dtype

N_MICROBATCH = 1
ADAM_LR = 0.001
ADAM_B1 = 0.9
ADAM_B2 = 0.999
ADAM_EPS = 1e-08
ADAM_WD = 0.01
ADAM_STEP = 10
PER_EXAMPLE_BATCH_AXIS = {'x': 0, 'loss_target': 0}
SHARED_INPUTS = []
_WEIGHT_DTYPES = {'meta_tokens': _jnp.float32, 'ffn1_pre_norm': _jnp.float32, 'ffn1_w_gate': _jnp.float32, 'ffn1_w_up': _jnp.float32, 'ffn1_w_down': _jnp.float32, 'ffn1_post_norm': _jnp.float32, 'mix_pre_norm': _jnp.float32, 'w_in': _jnp.float32, 'gla_w_a2': _jnp.float32, 'gla_b_a': _jnp.float32, 'gla_out_norm': _jnp.float32, 'swa_sinks': _jnp.float32, 'swa_out_norm': _jnp.float32, 'w_out': _jnp.float32, 'mix_post_norm': _jnp.float32, 'ffn2_pre_norm': _jnp.float32, 'ffn2_w_gate': _jnp.float32, 'ffn2_w_up': _jnp.float32, 'ffn2_w_down': _jnp.float32, 'ffn2_post_norm': _jnp.float32}
MOMENT_SCALE = {'meta_tokens': 1.165851e-01, 'ffn1_pre_norm': 7.098563e-01, 'ffn1_w_gate': 3.244999e-01, 'ffn1_w_up': 3.381774e-01, 'ffn1_w_down': 5.602205e-01, 'ffn1_post_norm': 1.582845e+01, 'mix_pre_norm': 1.122564e+00, 'w_in': 6.715630e-01, 'gla_w_a2': 5.647932e-02, 'gla_b_a': 2.604027e-01, 'gla_out_norm': 6.416309e-01, 'swa_sinks': 8.396018e-02, 'swa_out_norm': 1.055757e+00, 'w_out': 7.216851e-01, 'mix_post_norm': 6.375266e+01, 'ffn2_pre_norm': 5.861080e-01, 'ffn2_w_gate': 1.915540e-01, 'ffn2_w_up': 3.245769e-01, 'ffn2_w_down': 5.416460e-01, 'ffn2_post_norm': 1.608948e+01}


def _to_microbatches(a, axis):
    t = _jnp.moveaxis(a, axis, 0)
    t = t.reshape((N_MICROBATCH, t.shape[0] // N_MICROBATCH) + t.shape[1:])
    return _jnp.moveaxis(t, 1, axis + 1)


def setup_inputs(seed: int = 0) -> dict:
    inp = _fwd_setup_inputs(seed)
    key = _jax.random.fold_in(_jax.random.key(seed), 7919)
    shape, _ = _output_shape()
    out = dict(inp)
    out["loss_target"] = _jax.random.normal(_jax.random.fold_in(key, 0), shape, _jnp.float32)
    for i, name in enumerate(TWIN_WEIGHTS):
        w = inp[name].astype(_jnp.float32)
        if MOMENT_SCALE is None:
            s = _jnp.sqrt(_jnp.mean(_jnp.square(w)) + 1e-30)
        else:
            s = MOMENT_SCALE[name]
        km, kv = _jax.random.split(_jax.random.fold_in(key, i + 1))
        out[name] = w
        out["m_" + name] = s * _jax.random.normal(km, w.shape, _jnp.float32)
        out["v_" + name] = (s * s) * _jax.random.uniform(kv, w.shape, _jnp.float32, 0.5, 1.5)
    if N_MICROBATCH > 1:
        for name, axis in PER_EXAMPLE_BATCH_AXIS.items():
            out[name] = _to_microbatches(out[name], axis)
    return {'x': out['x'], 'meta_tokens': out['meta_tokens'], 'ffn1_pre_norm': out['ffn1_pre_norm'], 'ffn1_w_gate': out['ffn1_w_gate'], 'ffn1_w_up': out['ffn1_w_up'], 'ffn1_w_down': out['ffn1_w_down'], 'ffn1_post_norm': out['ffn1_post_norm'], 'mix_pre_norm': out['mix_pre_norm'], 'w_in': out['w_in'], 'gla_w_a2': out['gla_w_a2'], 'gla_b_a': out['gla_b_a'], 'gla_out_norm': out['gla_out_norm'], 'swa_sinks': out['swa_sinks'], 'swa_out_norm': out['swa_out_norm'], 'w_out': out['w_out'], 'mix_post_norm': out['mix_post_norm'], 'ffn2_pre_norm': out['ffn2_pre_norm'], 'ffn2_w_gate': out['ffn2_w_gate'], 'ffn2_w_up': out['ffn2_w_up'], 'ffn2_w_down': out['ffn2_w_down'], 'ffn2_post_norm': out['ffn2_post_norm'], 'loss_target': out['loss_target'], 'm_meta_tokens': out['m_meta_tokens'], 'm_ffn1_pre_norm': out['m_ffn1_pre_norm'], 'm_ffn1_w_gate': out['m_ffn1_w_gate'], 'm_ffn1_w_up': out['m_ffn1_w_up'], 'm_ffn1_w_down': out['m_ffn1_w_down'], 'm_ffn1_post_norm': out['m_ffn1_post_norm'], 'm_mix_pre_norm': out['m_mix_pre_norm'], 'm_w_in': out['m_w_in'], 'm_gla_w_a2': out['m_gla_w_a2'], 'm_gla_b_a': out['m_gla_b_a'], 'm_gla_out_norm': out['m_gla_out_norm'], 'm_swa_sinks': out['m_swa_sinks'], 'm_swa_out_norm': out['m_swa_out_norm'], 'm_w_out': out['m_w_out'], 'm_mix_post_norm': out['m_mix_post_norm'], 'm_ffn2_pre_norm': out['m_ffn2_pre_norm'], 'm_ffn2_w_gate': out['m_ffn2_w_gate'], 'm_ffn2_w_up': out['m_ffn2_w_up'], 'm_ffn2_w_down': out['m_ffn2_w_down'], 'm_ffn2_post_norm': out['m_ffn2_post_norm'], 'v_meta_tokens': out['v_meta_tokens'], 'v_ffn1_pre_norm': out['v_ffn1_pre_norm'], 'v_ffn1_w_gate': out['v_ffn1_w_gate'], 'v_ffn1_w_up': out['v_ffn1_w_up'], 'v_ffn1_w_down': out['v_ffn1_w_down'], 'v_ffn1_post_norm': out['v_ffn1_post_norm'], 'v_mix_pre_norm': out['v_mix_pre_norm'], 'v_w_in': out['v_w_in'], 'v_gla_w_a2': out['v_gla_w_a2'], 'v_gla_b_a': out['v_gla_b_a'], 'v_gla_out_norm': out['v_gla_out_norm'], 'v_swa_sinks': out['v_swa_sinks'], 'v_swa_out_norm': out['v_swa_out_norm'], 'v_w_out': out['v_w_out'], 'v_mix_post_norm': out['v_mix_post_norm'], 'v_ffn2_pre_norm': out['v_ffn2_pre_norm'], 'v_ffn2_w_gate': out['v_ffn2_w_gate'], 'v_ffn2_w_up': out['v_ffn2_w_up'], 'v_ffn2_w_down': out['v_ffn2_w_down'], 'v_ffn2_post_norm': out['v_ffn2_post_norm']}


def _loss(weights, diff, rest, loss_target):
    with _jax.named_scope("forward"):
        args = {**rest, TWIN_DIFF_INPUT: diff, **{k: w.astype(_WEIGHT_DTYPES[k]) for k, w in weights.items()}}
        y = _forward(args)
    with _jax.named_scope("loss_head"):
        err = _jnp.square(y.astype(_jnp.float32) - loss_target)
        return 0.5 * _jnp.sum(_jnp.mean(err, axis=-1)) if err.ndim else 0.5 * err


def _adamw(w, g, m, v):
    m = ADAM_B1 * m + (1.0 - ADAM_B1) * g
    v = ADAM_B2 * v + (1.0 - ADAM_B2) * _jnp.square(g)
    m_hat = m / (1.0 - ADAM_B1 ** ADAM_STEP)
    v_hat = v / (1.0 - ADAM_B2 ** ADAM_STEP)
    delta = -ADAM_LR * (m_hat / (_jnp.sqrt(v_hat) + ADAM_EPS) + ADAM_WD * w)
    return delta, m, v


def reference(x, meta_tokens, ffn1_pre_norm, ffn1_w_gate, ffn1_w_up, ffn1_w_down, ffn1_post_norm, mix_pre_norm, w_in, gla_w_a2, gla_b_a, gla_out_norm, swa_sinks, swa_out_norm, w_out, mix_post_norm, ffn2_pre_norm, ffn2_w_gate, ffn2_w_up, ffn2_w_down, ffn2_post_norm, loss_target, m_meta_tokens, m_ffn1_pre_norm, m_ffn1_w_gate, m_ffn1_w_up, m_ffn1_w_down, m_ffn1_post_norm, m_mix_pre_norm, m_w_in, m_gla_w_a2, m_gla_b_a, m_gla_out_norm, m_swa_sinks, m_swa_out_norm, m_w_out, m_mix_post_norm, m_ffn2_pre_norm, m_ffn2_w_gate, m_ffn2_w_up, m_ffn2_w_down, m_ffn2_post_norm, v_meta_tokens, v_ffn1_pre_norm, v_ffn1_w_gate, v_ffn1_w_up, v_ffn1_w_down, v_ffn1_post_norm, v_mix_pre_norm, v_w_in, v_gla_w_a2, v_gla_b_a, v_gla_out_norm, v_swa_sinks, v_swa_out_norm, v_w_out, v_mix_post_norm, v_ffn2_pre_norm, v_ffn2_w_gate, v_ffn2_w_up, v_ffn2_w_down, v_ffn2_post_norm):
    given = dict(x=x, meta_tokens=meta_tokens, ffn1_pre_norm=ffn1_pre_norm, ffn1_w_gate=ffn1_w_gate, ffn1_w_up=ffn1_w_up, ffn1_w_down=ffn1_w_down, ffn1_post_norm=ffn1_post_norm, mix_pre_norm=mix_pre_norm, w_in=w_in, gla_w_a2=gla_w_a2, gla_b_a=gla_b_a, gla_out_norm=gla_out_norm, swa_sinks=swa_sinks, swa_out_norm=swa_out_norm, w_out=w_out, mix_post_norm=mix_post_norm, ffn2_pre_norm=ffn2_pre_norm, ffn2_w_gate=ffn2_w_gate, ffn2_w_up=ffn2_w_up, ffn2_w_down=ffn2_w_down, ffn2_post_norm=ffn2_post_norm, loss_target=loss_target, m_meta_tokens=m_meta_tokens, m_ffn1_pre_norm=m_ffn1_pre_norm, m_ffn1_w_gate=m_ffn1_w_gate, m_ffn1_w_up=m_ffn1_w_up, m_ffn1_w_down=m_ffn1_w_down, m_ffn1_post_norm=m_ffn1_post_norm, m_mix_pre_norm=m_mix_pre_norm, m_w_in=m_w_in, m_gla_w_a2=m_gla_w_a2, m_gla_b_a=m_gla_b_a, m_gla_out_norm=m_gla_out_norm, m_swa_sinks=m_swa_sinks, m_swa_out_norm=m_swa_out_norm, m_w_out=m_w_out, m_mix_post_norm=m_mix_post_norm, m_ffn2_pre_norm=m_ffn2_pre_norm, m_ffn2_w_gate=m_ffn2_w_gate, m_ffn2_w_up=m_ffn2_w_up, m_ffn2_w_down=m_ffn2_w_down, m_ffn2_post_norm=m_ffn2_post_norm, v_meta_tokens=v_meta_tokens, v_ffn1_pre_norm=v_ffn1_pre_norm, v_ffn1_w_gate=v_ffn1_w_gate, v_ffn1_w_up=v_ffn1_w_up, v_ffn1_w_down=v_ffn1_w_down, v_ffn1_post_norm=v_ffn1_post_norm, v_mix_pre_norm=v_mix_pre_norm, v_w_in=v_w_in, v_gla_w_a2=v_gla_w_a2, v_gla_b_a=v_gla_b_a, v_gla_out_norm=v_gla_out_norm, v_swa_sinks=v_swa_sinks, v_swa_out_norm=v_swa_out_norm, v_w_out=v_w_out, v_mix_post_norm=v_mix_post_norm, v_ffn2_pre_norm=v_ffn2_pre_norm, v_ffn2_w_gate=v_ffn2_w_gate, v_ffn2_w_up=v_ffn2_w_up, v_ffn2_w_down=v_ffn2_w_down, v_ffn2_post_norm=v_ffn2_post_norm)
    weights = {n: given[n] for n in TWIN_WEIGHTS}
    shared = {n: given[n] for n in SHARED_INPUTS}
    per_example = {n: given[n] for n in ['x']}
    grad_fn = _jax.value_and_grad(_loss, argnums=(0, 1))

    def one_microbatch(ex, loss_target):
        ex = dict(ex)
        diff = ex.pop(TWIN_DIFF_INPUT)
        return grad_fn(weights, diff, {**shared, **ex}, loss_target)

    if N_MICROBATCH == 1:
        loss, (grad_w, grad_x) = one_microbatch(per_example, given["loss_target"])
    else:
        def body(carry, xs):
            loss_sum, grad_sum = carry
            l_k, (gw_k, gx_k) = one_microbatch(xs[0], xs[1])
            with _jax.named_scope("update"):
                return (loss_sum + l_k, _jax.tree.map(_jnp.add, grad_sum, gw_k)), gx_k

        init = (_jnp.zeros((), _jnp.float32), _jax.tree.map(_jnp.zeros_like, weights))
        (loss, grad_w), grad_x = _jax.lax.scan(body, init, (per_example, given["loss_target"]))
    with _jax.named_scope("update"):
        delta_w, new_m, new_v = {}, {}, {}
        for n in TWIN_WEIGHTS:
            delta_w[n], new_m[n], new_v[n] = _adamw(weights[n], grad_w[n], given["m_" + n], given["v_" + n])
    return (loss, grad_x, *[grad_w[n] for n in TWIN_WEIGHTS], *[delta_w[n] for n in TWIN_WEIGHTS],
            *[new_m[n] for n in TWIN_WEIGHTS], *[new_v[n] for n in TWIN_WEIGHTS])
```

```python
import functools

import jax
import jax.numpy as jnp
from jax import lax
from jax.experimental import pallas as pl
from jax.experimental.pallas import tpu as pltpu

F32, BF16 = jnp.float32, jnp.bfloat16

D_MODEL = 1024
D_FF = 2816
N_META = 16
BLK = 128
PAD_ROWS = BLK - N_META
GLA_DK = 64
SWA_HD = 64
SWA_HEADS = 8
GLA_TAU = 16.0
NORM_EPS = 1e-6
NEG_INF = -1e30
ROPE_THETA = 10000.0
P_GQ, P_GK, P_GV, P_GG, P_SQ, P_SK, P_SV, P_GA, P_END = 0, 256, 512, 1024, 1536, 2048, 2176, 2304, 2432
D_IN = 2320
IN_SPLITS = (256, 256, 512, 512, 16, 512, 128, 128)
FF_TILE = 256
N_DEV = 8
MESH = pl.DeviceIdType.MESH

ADAM_LR, ADAM_B1, ADAM_B2, ADAM_EPS, ADAM_WD, ADAM_STEP = 0.001, 0.9, 0.999, 1e-08, 0.01, 10

V7X_VMEM_BYTES = 64 << 20
VMEM_SPEC = pl.BlockSpec(memory_space=pltpu.VMEM)
SMEM_SPEC = pl.BlockSpec(memory_space=pltpu.SMEM)
ANY_SPEC = pl.BlockSpec(memory_space=pl.ANY)


def _params(semantics, vmem_mb=56):
    return pltpu.CompilerParams(dimension_semantics=semantics, vmem_limit_bytes=vmem_mb << 20)


def _row_tile(rows):
    return 320 if rows % 320 == 0 else BLK


def _nn(a, b):
    return lax.dot_general(a, b, (((1,), (0,)), ((), ())), preferred_element_type=F32)


def _nt(a, b):
    return lax.dot_general(a, b, (((1,), (1,)), ((), ())), preferred_element_type=F32)


def _tn(a, b):
    return lax.dot_general(a, b, (((0,), (0,)), ((), ())), preferred_element_type=F32)


def _rms(x):
    r = lax.rsqrt(jnp.mean(x * x, axis=-1, keepdims=True) + NORM_EPS)
    return x * r, r


def _rms_bwd(xn, r, w, dy):
    g = dy * w
    return r * (g - xn * jnp.mean(g * xn, axis=-1, keepdims=True))


def _sigmoid(x):
    return 1.0 / (1.0 + jnp.exp(-x))


def _colsum(x):
    return jnp.sum(x, axis=0, keepdims=True)


def _split_bf16(x):
    hi = x.astype(BF16)
    lo = (x - hi.astype(F32)).astype(BF16)
    return hi, lo


def _tri(lower):
    r = lax.broadcasted_iota(jnp.int32, (BLK, BLK), 0)
    c = lax.broadcasted_iota(jnp.int32, (BLK, BLK), 1)
    return (r >= c) if lower else (c >= r)


def _half_mask(width, half):
    lane = lax.broadcasted_iota(jnp.int32, (1, width), 1)
    return ((lane % 128) < 64) if half == 0 else ((lane % 128) >= 64)


def _rot_half(x):
    w = x.shape[-1]
    lane = lax.broadcasted_iota(jnp.int32, (1, w), 1)
    return jnp.where((lane % SWA_HD) < SWA_HD // 2, -pltpu.roll(x, w - SWA_HD // 2, 1), pltpu.roll(x, SWA_HD // 2, 1))


def _row_spec(tm, cols):
    return pl.BlockSpec((tm, cols), lambda i: (i, 0))


def _acc_spec(cols):
    return pl.BlockSpec((8, cols), lambda i: (0, 0))


def _acc_add(ref, first, value):
    @pl.when(first)
    def _():
        ref[...] = jnp.zeros_like(ref)
    ref[0:1, :] += value


def _ffn_fwd(h, gpre, wg_t, wu_t, wd, gpost, tgt=None):
    rows = h.shape[0]
    tm = _row_tile(rows)
    nf = D_FF // FF_TILE
    with_loss = tgt is not None

    def body(*refs):
        if with_loss:
            (h_ref, gpre_ref, wg_ref, wu_ref, wd_ref, gpost_ref, t_ref,
             ho_ref, a_ref, b_ref, f_ref, dy_ref, loss_ref, acc) = refs
        else:
            (h_ref, gpre_ref, wg_ref, wu_ref, wd_ref, gpost_ref, ho_ref, a_ref, b_ref, f_ref, acc) = refs
        i = pl.program_id(0)
        h_in = h_ref[...]
        hn, _ = _rms(h_in)
        n16 = (hn * gpre_ref[...]).astype(BF16)
        for j in range(nf):
            cols = slice(j * FF_TILE, (j + 1) * FF_TILE)
            a = _nt(n16, wg_ref[cols, :])
            b = _nt(n16, wu_ref[cols, :])
            a_ref[:, cols] = a.astype(BF16)
            b_ref[:, cols] = b.astype(BF16)
            s16 = (a * _sigmoid(a) * b).astype(BF16)
            part = _nn(s16, wd_ref[cols, :])
            if j == 0:
                acc[...] = part
            else:
                acc[...] += part
        f = acc[...]
        f_ref[...] = f
        fn, _ = _rms(f)
        y = h_in + 0.5 * (fn * gpost_ref[...])
        ho_ref[...] = y
        if with_loss:
            row = i * tm + lax.broadcasted_iota(jnp.int32, (tm, 1), 0)
            err = jnp.where(row >= BLK, y - t_ref[...], 0.0)
            dy_ref[...] = err * (1.0 / D_MODEL)
            part = 0.5 * jnp.sum(jnp.sum(err * err, axis=-1, keepdims=True) * (1.0 / D_MODEL), axis=0, keepdims=True)

            @pl.when(i == 0)
            def _():
                loss_ref[...] = jnp.zeros_like(loss_ref)
            loss_ref[...] += part

    row_f32 = _row_spec(tm, D_MODEL)
    in_specs = [row_f32, VMEM_SPEC, VMEM_SPEC, VMEM_SPEC, VMEM_SPEC, VMEM_SPEC]
    out_specs = [row_f32, _row_spec(tm, D_FF), _row_spec(tm, D_FF), row_f32]
    out_shape = [jax.ShapeDtypeStruct((rows, D_MODEL), F32), jax.ShapeDtypeStruct((rows, D_FF), BF16),
                 jax.ShapeDtypeStruct((rows, D_FF), BF16), jax.ShapeDtypeStruct((rows, D_MODEL), F32)]
    args = [h, gpre, wg_t, wu_t, wd, gpost]
    if with_loss:
        in_specs.append(row_f32)
        args.append(tgt)
        out_specs += [row_f32, pl.BlockSpec((8, 128), lambda i: (0, 0))]
        out_shape += [jax.ShapeDtypeStruct((rows, D_MODEL), F32), jax.ShapeDtypeStruct((8, 128), F32)]
    return pl.pallas_call(
        body, name="ffn_fwd_loss" if with_loss else "ffn_fwd", grid=(rows // tm,),
        in_specs=in_specs, out_specs=out_specs, out_shape=out_shape,
        scratch_shapes=[pltpu.VMEM((tm, D_MODEL), F32)],
        compiler_params=_params(("arbitrary",)),
    )(*args)


def _ffn_bwd_act(dh_out, h, a, b, f, gpre, gpost, wg_t, wu_t, wd, name):
    rows = h.shape[0]
    tm = _row_tile(rows)
    nf = D_FF // FF_TILE

    def body(dho_ref, h_ref, a_ref, b_ref, f_ref, gpre_ref, gpost_ref, wg_ref, wu_ref, wd_ref,
             dh_ref, da_ref, db_ref, df_ref, n_ref, dgpre_ref, dgpost_ref, acc):
        first = pl.program_id(0) == 0
        dho = dho_ref[...]
        drr = 0.5 * dho
        fn, rf = _rms(f_ref[...])
        _acc_add(dgpost_ref, first, _colsum(drr * fn))
        df16 = _rms_bwd(fn, rf, gpost_ref[...], drr).astype(BF16)
        df_ref[...] = df16
        hn, rh = _rms(h_ref[...])
        n_ref[...] = (hn * gpre_ref[...]).astype(BF16)
        for j in range(nf):
            cols = slice(j * FF_TILE, (j + 1) * FF_TILE)
            ds = _nt(df16, wd_ref[cols, :])
            av = a_ref[:, cols].astype(F32)
            bv = b_ref[:, cols].astype(F32)
            sg = _sigmoid(av)
            db16 = (ds * (av * sg)).astype(BF16)
            da16 = (ds * bv * (sg * (1.0 + av * (1.0 - sg)))).astype(BF16)
            da_ref[:, cols] = da16
            db_ref[:, cols] = db16
            part = _nn(da16, wg_ref[cols, :]) + _nn(db16, wu_ref[cols, :])
            if j == 0:
                acc[...] = part
            else:
                acc[...] += part
        dn = acc[...]
        _acc_add(dgpre_ref, first, _colsum(dn * hn))
        dh_ref[...] = dho + _rms_bwd(hn, rh, gpre_ref[...], dn)

    row_f32 = _row_spec(tm, D_MODEL)
    row_ff = _row_spec(tm, D_FF)
    return pl.pallas_call(
        body, name=name, grid=(rows // tm,),
        in_specs=[row_f32, row_f32, row_ff, row_ff, row_f32, VMEM_SPEC, VMEM_SPEC, VMEM_SPEC, VMEM_SPEC, VMEM_SPEC],
        out_specs=[row_f32, row_ff, row_ff, row_f32, row_f32, _acc_spec(D_MODEL), _acc_spec(D_MODEL)],
        out_shape=[jax.ShapeDtypeStruct((rows, D_MODEL), F32), jax.ShapeDtypeStruct((rows, D_FF), BF16),
                   jax.ShapeDtypeStruct((rows, D_FF), BF16), jax.ShapeDtypeStruct((rows, D_MODEL), BF16),
                   jax.ShapeDtypeStruct((rows, D_MODEL), BF16), jax.ShapeDtypeStruct((8, D_MODEL), F32),
                   jax.ShapeDtypeStruct((8, D_MODEL), F32)],
        scratch_shapes=[pltpu.VMEM((tm, D_MODEL), F32)],
        compiler_params=_params(("arbitrary",)),
    )(dh_out, h, a, b, f, gpre, gpost, wg_t, wu_t, wd)


def _ffn_bwd_weights(a, b, da, db, df, n, name):
    rows = a.shape[0]
    tm = 256 if rows % 256 == 0 else BLK
    tf = D_FF // 2
    nr = rows // tm

    def body(a_ref, b_ref, da_ref, db_ref, df_ref, n_ref, dwg_ref, dwu_ref, dwd_ref):
        first = pl.program_id(1) == 0
        av = a_ref[...].astype(F32)
        s16 = (av * _sigmoid(av) * b_ref[...].astype(F32)).astype(BF16)
        n16 = n_ref[...]
        pg = _tn(da_ref[...], n16)
        pu = _tn(db_ref[...], n16)
        pd = _tn(s16, df_ref[...])

        @pl.when(first)
        def _():
            dwg_ref[...] = pg
            dwu_ref[...] = pu
            dwd_ref[...] = pd

        @pl.when(jnp.logical_not(first))
        def _():
            dwg_ref[...] += pg
            dwu_ref[...] += pu
            dwd_ref[...] += pd

    ff_spec = pl.BlockSpec((tm, tf), lambda j, i: (i, j))
    d_spec = pl.BlockSpec((tm, D_MODEL), lambda j, i: (i, 0))
    w_spec = pl.BlockSpec((tf, D_MODEL), lambda j, i: (j, 0))
    w_shape = jax.ShapeDtypeStruct((D_FF, D_MODEL), F32)
    return pl.pallas_call(
        body, name=name, grid=(2, nr),
        in_specs=[ff_spec, ff_spec, ff_spec, ff_spec, d_spec, d_spec],
        out_specs=[w_spec, w_spec, w_spec], out_shape=[w_shape, w_shape, w_shape],
        compiler_params=_params(("arbitrary", "arbitrary")),
    )(a, b, da, db, df, n)


def _chunk_cumsum(x, lower):
    tri = jnp.where(_tri(lower), 1.0, 0.0).astype(BF16)
    hi, lo = _split_bf16(x)
    return _nn(tri, hi) + _nn(tri, lo)


def _mix_in(h, g, win_p, wa2_p, b_a, cos, sin):
    rows = h.shape[0]
    tm = _row_tile(rows)
    tm = tm if tm % BLK == 0 else BLK

    def body(h_ref, g_ref, win_ref, wa2_ref, ba_ref, cos_ref, sin_ref,
             gq_ref, gk_ref, gv_ref, gg_ref, sq_ref, sk_ref, sv_ref, ga_ref, loga_ref, bc_ref, n_ref):
        hn, _ = _rms(h_ref[...])
        n16 = (hn * g_ref[...]).astype(BF16)
        n_ref[...] = n16
        proj = _nt(n16, win_ref[...])
        gq_ref[...] = proj[:, P_GQ:P_GK]
        gk_ref[...] = proj[:, P_GK:P_GV]
        gv_ref[...] = proj[:, P_GV:P_GG].astype(BF16)
        gg_ref[...] = proj[:, P_GG:P_SQ]
        c1, s1 = cos_ref[...], sin_ref[...]
        c4 = jnp.concatenate([c1, c1, c1, c1], axis=1)
        s4 = jnp.concatenate([s1, s1, s1, s1], axis=1)
        sq = proj[:, P_SQ:P_SK]
        sk = proj[:, P_SK:P_SV]
        sq_ref[...] = (sq * c4 + _rot_half(sq) * s4).astype(BF16)
        sk_ref[...] = (sk * c1 + _rot_half(sk) * s1).astype(BF16)
        sv_ref[...] = proj[:, P_SV:P_GA].astype(BF16)
        ga = proj[:, P_GA:P_END]
        ga_ref[...] = ga
        z = _nn(ga, wa2_ref[...]) + ba_ref[...]
        loga = (jnp.minimum(z, 0.0) - jnp.log(1.0 + jnp.exp(-jnp.abs(z)))) * (1.0 / GLA_TAU)
        loga_ref[...] = loga
        for c in range(tm // BLK):
            rs = slice(c * BLK, (c + 1) * BLK)
            bc_ref[rs, :] = _chunk_cumsum(loga[rs, :], True)

    f32 = lambda c: jax.ShapeDtypeStruct((rows, c), F32)
    b16 = lambda c: jax.ShapeDtypeStruct((rows, c), BF16)
    rs = lambda c: _row_spec(tm, c)
    return pl.pallas_call(
        body, name="mix_in", grid=(rows // tm,),
        in_specs=[rs(D_MODEL), VMEM_SPEC, VMEM_SPEC, VMEM_SPEC, VMEM_SPEC, rs(128), rs(128)],
        out_specs=[rs(256), rs(256), rs(512), rs(512), rs(512), rs(128), rs(128), rs(128), rs(256), rs(256), rs(D_MODEL)],
        out_shape=[f32(256), f32(256), b16(512), f32(512), b16(512), b16(128), b16(128), f32(128), f32(256), f32(256),
                   b16(D_MODEL)],
        compiler_params=_params(("arbitrary",)),
    )(h, g, win_p, wa2_p, b_a, cos, sin)


def _gla_factors(q, k, bc):
    bm = bc[BLK // 2 - 1:BLK // 2, :]
    bl = bc[BLK - 1:BLK, :]
    e_q, e_k, e_qe, e_kd = jnp.exp(bc - bm), jnp.exp(bm - bc), jnp.exp(bc), jnp.exp(bl - bc)
    return (q * e_q, k * e_k, q * e_qe, k * e_kd), (e_q, e_k, e_qe, e_kd), jnp.exp(bl)


def _gla_fwd(gq, gk, gv, gg, bc, wgn):
    rows = gq.shape[0]
    nc = rows // BLK
    scale = GLA_DK ** -0.5

    def body(q_ref, k_ref, v_ref, gg_ref, bc_ref, wgn_ref, o_ref, cat_ref, sp_ref, st):
        @pl.when(pl.program_id(0) == 0)
        def _():
            st[...] = jnp.zeros_like(st)
        low = _tri(True)
        wgn_v = wgn_ref[...]
        for p in range(2):
            sl = slice(128 * p, 128 * p + 128)
            (qt, kt, qe, kd), _, ebl = _gla_factors(q_ref[:, sl] * scale, k_ref[:, sl], bc_ref[:, sl])
            s_prev = st[p]
            sp_ref[0, p] = s_prev
            s16 = s_prev.astype(BF16)
            qt16 = qt.astype(BF16)
            s_new = s_prev * ebl
            for hh in range(2):
                hs = slice(128 * (2 * p + hh), 128 * (2 * p + hh) + 128)
                lm = _half_mask(128, hh)
                vh = v_ref[:, hs]
                pm = jnp.where(low, _nt(qt16, jnp.where(lm, kt, 0.0).astype(BF16)), 0.0)
                o = _nn(pm.astype(BF16), vh) + _nt(jnp.where(lm, qe, 0.0).astype(BF16), s16)
                s_new = s_new + _tn(vh, jnp.where(lm, kd, 0.0).astype(BF16))
                o_ref[:, hs] = o
                on, _ = _rms(o)
                gate = gg_ref[:, hs]
                cat_ref[:, hs] = (on * wgn_v * (gate * _sigmoid(gate))).astype(BF16)
            st[p] = s_new

    rs = lambda c: _row_spec(BLK, c)
    return pl.pallas_call(
        body, name="gla_fwd", grid=(nc,),
        in_specs=[rs(256), rs(256), rs(512), rs(512), rs(256), VMEM_SPEC],
        out_specs=[rs(512), rs(512), pl.BlockSpec((1, 2, 128, 128), lambda i: (i, 0, 0, 0))],
        out_shape=[jax.ShapeDtypeStruct((rows, 512), F32), jax.ShapeDtypeStruct((rows, 512), BF16),
                   jax.ShapeDtypeStruct((nc, 2, 128, 128), F32)],
        scratch_shapes=[pltpu.VMEM((2, 128, 128), F32)],
        compiler_params=_params(("arbitrary",)),
    )(gq, gk, gv, gg, bc, wgn)


def _gla_bwd(dcat, o_all, gq, gk, gv, gg, bc, sp, wgn):
    rows = gq.shape[0]
    nc = rows // BLK
    scale = GLA_DK ** -0.5

    def body(dc_ref, o_ref, q_ref, k_ref, v_ref, gg_ref, bc_ref, sp_ref, wgn_ref,
             dq_ref, dk_ref, dv_ref, dgg_ref, dla_ref, dwgn_ref, dst):
        first = pl.program_id(0) == 0

        @pl.when(first)
        def _():
            dst[...] = jnp.zeros_like(dst)
        low, upp = _tri(True), _tri(False)
        last_row = lax.broadcasted_iota(jnp.int32, (BLK, 1), 0) == BLK - 1
        wgn_v = wgn_ref[...]
        dwgn = jnp.zeros((1, 128), F32)
        for p in range(2):
            sl = slice(128 * p, 128 * p + 128)
            (qt, kt, qe, kd), (e_q, e_k, e_qe, e_kd), ebl = _gla_factors(
                q_ref[:, sl] * scale, k_ref[:, sl], bc_ref[:, sl])
            s_prev = sp_ref[0, p]
            s16 = s_prev.astype(BF16)
            ds_next = dst[p]
            ds16 = ds_next.astype(BF16)
            qt16 = qt.astype(BF16)
            ds_new = ds_next * ebl
            dqt = jnp.zeros((BLK, 128), F32)
            dkt = jnp.zeros((BLK, 128), F32)
            dqe = jnp.zeros((BLK, 128), F32)
            dkd = jnp.zeros((BLK, 128), F32)
            for hh in range(2):
                hs = slice(128 * (2 * p + hh), 128 * (2 * p + hh) + 128)
                lm = _half_mask(128, hh)
                on, ro = _rms(o_ref[:, hs])
                gate = gg_ref[:, hs]
                sg = _sigmoid(gate)
                si = gate * sg
                dog = dc_ref[:, hs]
                dwgn = dwgn + _colsum(dog * si * on)
                dgg_ref[:, hs] = dog * (on * wgn_v) * (sg * (1.0 + gate * (1.0 - sg)))
                do16 = _rms_bwd(on, ro, wgn_v, dog * si).astype(BF16)
                vh = v_ref[:, hs]
                ktm16 = jnp.where(lm, kt, 0.0).astype(BF16)
                qtm16 = jnp.where(lm, qt, 0.0).astype(BF16)
                qem16 = jnp.where(lm, qe, 0.0).astype(BF16)
                kdm16 = jnp.where(lm, kd, 0.0).astype(BF16)
                p_t = jnp.where(upp, _nt(ktm16, qt16), 0.0)
                dp_t = jnp.where(upp, _nt(vh, do16), 0.0)
                dp = jnp.where(low, _nt(do16, vh), 0.0)
                dv_ref[:, hs] = _nn(p_t.astype(BF16), do16) + _nt(kdm16, ds16)
                dqt = dqt + _nn(dp.astype(BF16), ktm16)
                dkt = dkt + _nn(dp_t.astype(BF16), qtm16)
                dqe = dqe + jnp.where(lm, _nn(do16, s16), 0.0)
                dkd = dkd + jnp.where(lm, _nn(vh, ds16), 0.0)
                ds_new = ds_new + _tn(do16, qem16)
            debl = _colsum(ds_next * s_prev)
            dq_ref[:, sl] = (dqt * e_q + dqe * e_qe) * scale
            dk_ref[:, sl] = dkt * e_k + dkd * e_kd
            dkd_kd = dkd * kd
            db = dqt * qt - dkt * kt + dqe * qe - dkd_kd
            db = jnp.where(last_row, db + (_colsum(dkd_kd) + debl * ebl), db)
            dla_ref[:, sl] = _chunk_cumsum(db, False)
            dst[p] = ds_new
        _acc_add(dwgn_ref, first, dwgn)

    rev = lambda c: pl.BlockSpec((BLK, c), lambda i: (nc - 1 - i, 0))
    f32 = lambda c: jax.ShapeDtypeStruct((rows, c), F32)
    return pl.pallas_call(
        body, name="gla_bwd", grid=(nc,),
        in_specs=[rev(512), rev(512), rev(256), rev(256), rev(512), rev(512), rev(256),
                  pl.BlockSpec((1, 2, 128, 128), lambda i: (nc - 1 - i, 0, 0, 0)), VMEM_SPEC],
        out_specs=[rev(256), rev(256), rev(512), rev(512), rev(256), _acc_spec(128)],
        out_shape=[f32(256), f32(256), f32(512), f32(512), f32(256), jax.ShapeDtypeStruct((8, 128), F32)],
        scratch_shapes=[pltpu.VMEM((2, 128, 128), F32)],
        compiler_params=_params(("arbitrary",)),
    )(dcat, o_all, gq, gk, gv, gg, bc, sp, wgn)


def _swa_mask(i):
    t = lax.broadcasted_iota(jnp.int32, (BLK, 3 * BLK), 0)
    c = lax.broadcasted_iota(jnp.int32, (BLK, 3 * BLK), 1)
    qpos = i * BLK + t - PAD_ROWS
    seg = c // BLK
    cc = c % BLK
    kpos = jnp.where(seg == 0, (i - 1) * BLK + cc, i * BLK + cc) - PAD_ROWS
    band = (seg < 2) & (kpos >= N_META) & (kpos <= qpos) & (qpos - kpos < BLK)
    midx = cc - PAD_ROWS
    meta = (seg == 2) & (midx >= 0) & (midx <= qpos)
    return band | meta


def _swa_keys(k_ref, i):
    prev = pl.multiple_of(jnp.maximum(i - 1, 0) * BLK, BLK)
    own = pl.multiple_of(i * BLK, BLK)
    return jnp.concatenate([k_ref[pl.ds(prev, BLK), :], k_ref[pl.ds(own, BLK), :], k_ref[0:BLK, :]], axis=0), prev, own


def _place(x, kv):
    if kv == 0:
        lo = jnp.where(_half_mask(128, 0), x, jnp.zeros_like(x))
        return lo, pltpu.roll(lo, 64, 1)
    hi = jnp.where(_half_mask(128, 1), x, jnp.zeros_like(x))
    return pltpu.roll(hi, 64, 1), hi


def _swa_fwd(sq, sk, sv, sinks, wn):
    rows = sq.shape[0]
    nb = rows // BLK
    scale = SWA_HD ** -0.5

    def body(q_ref, k_ref, v_ref, sink_ref, wn_ref, o_ref, cat_ref, lse_ref):
        i = pl.program_id(0)
        mask = _swa_mask(i)
        keys, _, _ = _swa_keys(k_ref, i)
        vals, _, _ = _swa_keys(v_ref, i)
        kz = (_place(keys, 0), _place(keys, 1))
        vz = (_place(vals, 0), _place(vals, 1))
        outs = []
        for pr in range(4):
            q_pair = q_ref[:, 128 * pr:128 * pr + 128]
            o_pair = jnp.zeros((BLK, 128), F32)
            for half in range(2):
                hd = 2 * pr + half
                kv = hd // 4
                sink = sink_ref[0, hd]
                s = jnp.where(mask, _nt(q_pair, kz[kv][half]) * scale, NEG_INF)
                m = jnp.maximum(jnp.max(s, axis=-1, keepdims=True), sink)
                e = jnp.exp(s - m)
                den = jnp.sum(e, axis=-1, keepdims=True) + jnp.exp(sink - m)
                lse_ref[:, hd:hd + 1] = m + jnp.log(den)
                o_pair = o_pair + _nn((e / den).astype(BF16), vz[kv][half])
            outs.append(o_pair)
        o = jnp.concatenate(outs, axis=1)
        o_ref[...] = o
        on, _ = _rms(o)
        cat_ref[...] = (on * wn_ref[...]).astype(BF16)

    return pl.pallas_call(
        body, name="swa_fwd", grid=(nb,),
        in_specs=[_row_spec(BLK, 512), VMEM_SPEC, VMEM_SPEC, SMEM_SPEC, VMEM_SPEC],
        out_specs=[_row_spec(BLK, 512), _row_spec(BLK, 512), _row_spec(BLK, SWA_HEADS)],
        out_shape=[jax.ShapeDtypeStruct((rows, 512), F32), jax.ShapeDtypeStruct((rows, 512), BF16),
                   jax.ShapeDtypeStruct((rows, SWA_HEADS), F32)],
        compiler_params=_params(("arbitrary",)),
    )(sq, sk, sv, sinks, wn)


def _swa_bwd(dcat, o_all, sq, sk, sv, lse, sinks, wn):
    rows = sq.shape[0]
    nb = rows // BLK
    scale = SWA_HD ** -0.5

    def body(dc_ref, o_ref, q_ref, k_ref, v_ref, lse_ref, sink_ref, wn_ref, dq_ref, dk_ref, dv_ref, dsink_ref, dwn_ref):
        i = pl.program_id(0)
        first = i == 0

        @pl.when(first)
        def _():
            dk_ref[...] = jnp.zeros_like(dk_ref)
            dv_ref[...] = jnp.zeros_like(dv_ref)
        mask = _swa_mask(i)
        keys, prev, own = _swa_keys(k_ref, i)
        vals, _, _ = _swa_keys(v_ref, i)
        kz = (_place(keys, 0), _place(keys, 1))
        vz = (_place(vals, 0), _place(vals, 1))
        o = o_ref[...]
        on, ro = _rms(o)
        dc = dc_ref[...]
        _acc_add(dwn_ref, first, _colsum(dc * on))
        do = _rms_bwd(on, ro, wn_ref[...], dc)
        do_o = do * o
        lane8 = lax.broadcasted_iota(jnp.int32, (1, 128), 1)
        dsink = jnp.zeros((1, 128), F32)
        dk_acc = jnp.zeros((3 * BLK, 128), F32)
        dv_acc = jnp.zeros((3 * BLK, 128), F32)
        dqs = []
        for pr in range(4):
            ps = slice(128 * pr, 128 * pr + 128)
            q_pair = q_ref[:, ps]
            do16 = do[:, ps].astype(BF16)
            dq_pair = jnp.zeros((BLK, 128), F32)
            for half in range(2):
                hd = 2 * pr + half
                kv = hd // 4
                hm = _half_mask(128, half)
                sink = sink_ref[0, hd]
                lse_h = lse_ref[:, hd:hd + 1]
                delta = jnp.sum(jnp.where(hm, do_o[:, ps], 0.0), axis=-1, keepdims=True)
                s = jnp.where(mask, _nt(q_pair, kz[kv][half]) * scale, NEG_INF)
                prob = jnp.exp(s - lse_h)
                dsink = dsink + jnp.where(lane8 == hd, -jnp.sum(jnp.exp(sink - lse_h) * delta), 0.0)
                dp = _nt(do16, vz[kv][half])
                ds16 = (prob * (dp - delta) * scale).astype(BF16)
                dq_pair = dq_pair + _nn(ds16, kz[kv][half])
                dkz = jnp.where(hm, _tn(ds16, q_pair), 0.0)
                dvz = jnp.where(hm, _tn(prob.astype(BF16), do16), 0.0)
                if half != kv:
                    dkz = pltpu.roll(dkz, 64, 1)
                    dvz = pltpu.roll(dvz, 64, 1)
                dk_acc = dk_acc + dkz
                dv_acc = dv_acc + dvz
            dqs.append(dq_pair)
        dq_ref[...] = jnp.concatenate(dqs, axis=1)
        _acc_add(dsink_ref, first, dsink)
        for ref, acc in ((dk_ref, dk_acc), (dv_ref, dv_acc)):
            ref[pl.ds(prev, BLK), :] += acc[0:BLK]
            ref[pl.ds(own, BLK), :] += acc[BLK:2 * BLK]
            ref[0:BLK, :] += acc[2 * BLK:3 * BLK]

    full = pl.BlockSpec((rows, 128), lambda i: (0, 0))
    return pl.pallas_call(
        body, name="swa_bwd", grid=(nb,),
        in_specs=[_row_spec(BLK, 512), _row_spec(BLK, 512), _row_spec(BLK, 512), VMEM_SPEC, VMEM_SPEC,
                  _row_spec(BLK, SWA_HEADS), SMEM_SPEC, VMEM_SPEC],
        out_specs=[_row_spec(BLK, 512), full, full, _acc_spec(128), _acc_spec(512)],
        out_shape=[jax.ShapeDtypeStruct((rows, 512), F32), jax.ShapeDtypeStruct((rows, 128), F32),
                   jax.ShapeDtypeStruct((rows, 128), F32), jax.ShapeDtypeStruct((8, 128), F32),
                   jax.ShapeDtypeStruct((8, 512), F32)],
        compiler_params=_params(("arbitrary",)),
    )(dcat, o_all, sq, sk, sv, lse, sinks, wn)


def _mix_out(h, cat_g, cat_s, wout, gpost):
    rows = h.shape[0]
    tm = _row_tile(rows)

    def body(h_ref, cg_ref, cs_ref, w_ref, g_ref, ho_ref, m_ref):
        m = _nn(cg_ref[...], w_ref[0:512, :]) + _nn(cs_ref[...], w_ref[512:1024, :])
        m_ref[...] = m
        mn, _ = _rms(m)
        ho_ref[...] = h_ref[...] + mn * g_ref[...]

    row_f32 = _row_spec(tm, D_MODEL)
    return pl.pallas_call(
        body, name="mix_out", grid=(rows // tm,),
        in_specs=[row_f32, _row_spec(tm, 512), _row_spec(tm, 512), VMEM_SPEC, VMEM_SPEC],
        out_specs=[row_f32, row_f32],
        out_shape=[jax.ShapeDtypeStruct((rows, D_MODEL), F32), jax.ShapeDtypeStruct((rows, D_MODEL), F32)],
        compiler_params=_params(("arbitrary",)),
    )(h, cat_g, cat_s, wout, gpost)


def _mix_out_bwd(dh, m, cat_g, cat_s, wout, gpost):
    rows = dh.shape[0]
    tm = _row_tile(rows)

    def body(dh_ref, m_ref, cg_ref, cs_ref, w_ref, g_ref, dcg_ref, dcs_ref, dw_ref, dg_ref):
        first = pl.program_id(0) == 0
        dhv = dh_ref[...]
        mn, rm = _rms(m_ref[...])
        _acc_add(dg_ref, first, _colsum(dhv * mn))
        dm16 = _rms_bwd(mn, rm, g_ref[...], dhv).astype(BF16)
        dcat = _nt(dm16, w_ref[...])
        dcg_ref[...] = dcat[:, 0:512]
        dcs_ref[...] = dcat[:, 512:1024]
        pg = _tn(cg_ref[...], dm16)
        ps = _tn(cs_ref[...], dm16)

        @pl.when(first)
        def _():
            dw_ref[0:512, :] = pg
            dw_ref[512:1024, :] = ps

        @pl.when(jnp.logical_not(first))
        def _():
            dw_ref[0:512, :] += pg
            dw_ref[512:1024, :] += ps

    row_f32 = _row_spec(tm, D_MODEL)
    return pl.pallas_call(
        body, name="mix_out_bwd", grid=(rows // tm,),
        in_specs=[row_f32, row_f32, _row_spec(tm, 512), _row_spec(tm, 512), VMEM_SPEC, VMEM_SPEC],
        out_specs=[_row_spec(tm, 512), _row_spec(tm, 512), pl.BlockSpec((D_MODEL, D_MODEL), lambda i: (0, 0)),
                   _acc_spec(D_MODEL)],
        out_shape=[jax.ShapeDtypeStruct((rows, 512), F32), jax.ShapeDtypeStruct((rows, 512), F32),
                   jax.ShapeDtypeStruct((D_MODEL, D_MODEL), F32), jax.ShapeDtypeStruct((8, D_MODEL), F32)],
        compiler_params=_params(("arbitrary",)),
    )(dh, m, cat_g, cat_s, wout, gpost)


def _mix_in_bwd(dh_out, h, n16, g, win_p, wa2_p, cos, sin, loga, ga, dgq, dgk, dgv, dgg, dsq, dsk, dsv, dloga):
    rows = h.shape[0]
    tm = _row_tile(rows)

    def body(dho_ref, h_ref, n_ref, g_ref, win_ref, wa2_ref, cos_ref, sin_ref, loga_ref, ga_ref,
             dgq_ref, dgk_ref, dgv_ref, dgg_ref, dsq_ref, dsk_ref, dsv_ref, dla_ref,
             dh_ref, dwin_ref, dwa2_ref, dg_ref, dba_ref):
        first = pl.program_id(0) == 0
        dz = dla_ref[...] * (1.0 / GLA_TAU) * (1.0 - jnp.exp(GLA_TAU * loga_ref[...]))
        _acc_add(dba_ref, first, _colsum(dz))
        dga = _nt(dz, wa2_ref[...])
        pa = _tn(ga_ref[...], dz)
        c1, s1 = cos_ref[...], sin_ref[...]
        c4 = jnp.concatenate([c1, c1, c1, c1], axis=1)
        s4 = jnp.concatenate([s1, s1, s1, s1], axis=1)
        dq_r, dk_r = dsq_ref[...], dsk_ref[...]
        dsq = dq_r * c4 - _rot_half(dq_r * s4)
        dsk = dk_r * c1 - _rot_half(dk_r * s1)
        dproj16 = jnp.concatenate(
            [dgq_ref[...], dgk_ref[...], dgv_ref[...], dgg_ref[...], dsq, dsk, dsv_ref[...], dga], axis=1).astype(BF16)
        dn = _nn(dproj16, win_ref[...])
        pw = _tn(dproj16, n_ref[...])

        @pl.when(first)
        def _():
            dwin_ref[...] = pw
            dwa2_ref[...] = pa

        @pl.when(jnp.logical_not(first))
        def _():
            dwin_ref[...] += pw
            dwa2_ref[...] += pa
        hn, rh = _rms(h_ref[...])
        _acc_add(dg_ref, first, _colsum(dn * hn))
        dh_ref[...] = dho_ref[...] + _rms_bwd(hn, rh, g_ref[...], dn)

    rs = lambda c: _row_spec(tm, c)
    return pl.pallas_call(
        body, name="mix_in_bwd", grid=(rows // tm,),
        in_specs=[rs(D_MODEL), rs(D_MODEL), rs(D_MODEL), VMEM_SPEC, VMEM_SPEC, VMEM_SPEC, rs(128), rs(128), rs(256), rs(128),
                  rs(256), rs(256), rs(512), rs(512), rs(512), rs(128), rs(128), rs(256)],
        out_specs=[rs(D_MODEL), pl.BlockSpec((P_END, D_MODEL), lambda i: (0, 0)), pl.BlockSpec((128, 256), lambda i: (0, 0)),
                   _acc_spec(D_MODEL), _acc_spec(256)],
        out_shape=[jax.ShapeDtypeStruct((rows, D_MODEL), F32), jax.ShapeDtypeStruct((P_END, D_MODEL), F32),
                   jax.ShapeDtypeStruct((128, 256), F32), jax.ShapeDtypeStruct((8, D_MODEL), F32),
                   jax.ShapeDtypeStruct((8, 256), F32)],
        compiler_params=_params(("arbitrary",)),
    )(dh_out, h, n16, g, win_p, wa2_p, cos, sin, loga, ga, dgq, dgk, dgv, dgg, dsq, dsk, dsv, dloga)


def _rope_tables(rows):
    pos = (jnp.arange(rows, dtype=jnp.int32) - PAD_ROWS).astype(F32)
    inv_freq = 1.0 / (ROPE_THETA ** (jnp.arange(0, SWA_HD, 2, dtype=F32) / SWA_HD))
    ang = pos[:, None] * inv_freq[None, :]
    ang = jnp.concatenate([ang, ang, ang, ang], axis=-1)
    return jnp.cos(ang), jnp.sin(ang)


def _local_step(h0, tgt, w):
    rows = h0.shape[0]
    cos, sin = _rope_tables(rows)
    h1, a1, b1, f1 = _ffn_fwd(h0, w["ffn1_pre"], w["wg1"], w["wu1"], w["wd1"], w["ffn1_post"])
    gq, gk, gv, gg, sq, sk, sv, ga, loga, bc, n2 = _mix_in(h1, w["mix_pre"], w["win"], w["wa2"], w["b_a"], cos, sin)
    o_g, cat_g, sp = _gla_fwd(gq, gk, gv, gg, bc, w["gla_norm"])
    o_s, cat_s, lse = _swa_fwd(sq, sk, sv, w["sinks"], w["swa_norm"])
    h2, m = _mix_out(h1, cat_g, cat_s, w["wout"], w["mix_post"])
    h3, a2, b2, f2, dy, loss = _ffn_fwd(h2, w["ffn2_pre"], w["wg2"], w["wu2"], w["wd2"], w["ffn2_post"], tgt)
    del h3
    g = {}
    dh2, da, db, df, n3, g["ffn2_pre"], g["ffn2_post"] = _ffn_bwd_act(
        dy, h2, a2, b2, f2, w["ffn2_pre"], w["ffn2_post"], w["wg2"], w["wu2"], w["wd2"], "ffn2_bwd_act")
    g["wg2"], g["wu2"], g["wd2"] = _ffn_bwd_weights(a2, b2, da, db, df, n3, "ffn2_bwd_w")
    dcg, dcs, g["wout"], g["mix_post"] = _mix_out_bwd(dh2, m, cat_g, cat_s, w["wout"], w["mix_post"])
    dsq, dsk, dsv, g["sinks"], g["swa_norm"] = _swa_bwd(dcs, o_s, sq, sk, sv, lse, w["sinks"], w["swa_norm"])
    dgq, dgk, dgv, dgg, dloga, g["gla_norm"] = _gla_bwd(dcg, o_g, gq, gk, gv, gg, bc, sp, w["gla_norm"])
    dh1, g["win"], g["wa2"], g["mix_pre"], g["b_a"] = _mix_in_bwd(
        dh2, h1, n2, w["mix_pre"], w["win"], w["wa2"], cos, sin, loga, ga, dgq, dgk, dgv, dgg, dsq, dsk, dsv, dloga)
    dh0, da, db, df, n1, g["ffn1_pre"], g["ffn1_post"] = _ffn_bwd_act(
        dh1, h0, a1, b1, f1, w["ffn1_pre"], w["ffn1_post"], w["wg1"], w["wu1"], w["wd1"], "ffn1_bwd_act")
    g["wg1"], g["wu1"], g["wd1"] = _ffn_bwd_weights(a1, b1, da, db, df, n1, "ffn1_bwd_w")
    return loss[0, 0], dh0, g


def _win_pad_rows(win_t):
    pad = jnp.zeros((P_END - P_GA - 16, win_t.shape[1]), win_t.dtype)
    return jnp.concatenate([win_t[0:1536], win_t[1552:2320], win_t[1536:1552], pad], axis=0)


def _win_unpad_rows(win_p):
    return jnp.concatenate([win_p[0:1536], win_p[P_GA:P_GA + 16], win_p[1536:P_GA]], axis=0)


def _place_on_mesh():
    return lax.axis_index("x"), lax.axis_index("y"), lax.axis_index("c")


def _dev_index(px, py, pc):
    return 4 * px + 2 * py + pc


def _other_devices(x, y, c):
    flip = lambda v, f: 1 - v if f else v
    return [(flip(x, fx), flip(y, fy), flip(c, fc)) for fx in (0, 1) for fy in (0, 1) for fc in (0, 1)][1:]


def _all_gather(shards):
    n = len(shards)

    def body(*refs):
        ins, outs = refs[:n], refs[n:2 * n]
        send_sems, recv_sems, local_sems = refs[2 * n:]
        x, y, c = _place_on_mesh()
        me, sibling = (x, y, c), (x, y, 1 - c)
        chips = [(1 - x, y), (x, 1 - y), (1 - x, 1 - y)]

        def rows(k, px, py, pc):
            r = ins[k].shape[0]
            return outs[k].at[pl.ds(pl.multiple_of(_dev_index(px, py, pc) * r, 8), r), :]

        def copy(k, slot, block, to, src=None):
            return pltpu.make_async_remote_copy(
                src_ref=rows(k, *block) if src is None else src, dst_ref=rows(k, *block),
                send_sem=send_sems.at[k, slot], recv_sem=recv_sems.at[k, slot], device_id=to, device_id_type=MESH)

        local = [pltpu.make_async_copy(ins[k], rows(k, *me), local_sems.at[k]) for k in range(n)]
        sends = []
        for k in range(n):
            local[k].start()
            sends.append(copy(k, 0, me, sibling, src=ins[k]))
            sends += [copy(k, 1 + j, me, (*chip, c), src=ins[k]) for j, chip in enumerate(chips)]
        for cp in sends:
            cp.start()
        for k in range(n):
            for j, chip in enumerate(chips):
                copy(k, 1 + j, (*chip, c), me).wait_recv()
                passed = copy(k, 4 + j, (*chip, c), sibling)
                passed.start()
                sends.append(passed)
        for k in range(n):
            copy(k, 0, sibling, me).wait_recv()
            for j, chip in enumerate(chips):
                copy(k, 4 + j, (*chip, 1 - c), me).wait_recv()
        for cp in sends:
            cp.wait_send()
        for cp in local:
            cp.wait()

    return pl.pallas_call(
        body, name="all_gather_weights",
        in_specs=[ANY_SPEC] * n, out_specs=[ANY_SPEC] * n,
        out_shape=[jax.ShapeDtypeStruct((N_DEV * s.shape[0], s.shape[1]), s.dtype) for s in shards],
        scratch_shapes=[pltpu.SemaphoreType.DMA((n, 7)), pltpu.SemaphoreType.DMA((n, 7)), pltpu.SemaphoreType.DMA((n,))],
    )(*shards)


def _scatter_partials(parts):
    n = len(parts)

    def body(*refs):
        ins, outs = refs[:n], refs[n:2 * n]
        send_sems, recv_sems, local_sems = refs[2 * n:]
        x, y, c = _place_on_mesh()
        me = _dev_index(x, y, c)
        peers = _other_devices(x, y, c)

        def rows(ref, k, d):
            r = ins[k].shape[0] // N_DEV
            return ref.at[pl.ds(pl.multiple_of(d * r, 8), r), :]

        def copy(k, f, peer):
            return pltpu.make_async_remote_copy(
                src_ref=rows(ins[k], k, _dev_index(*peer)), dst_ref=rows(outs[k], k, me),
                send_sem=send_sems.at[k, f], recv_sem=recv_sems.at[k, f], device_id=peer, device_id_type=MESH)

        def arrival(k, f, peer):
            return pltpu.make_async_remote_copy(
                src_ref=rows(ins[k], k, me), dst_ref=rows(outs[k], k, _dev_index(*peer)),
                send_sem=send_sems.at[k, f], recv_sem=recv_sems.at[k, f], device_id=peer, device_id_type=MESH)

        local = [pltpu.make_async_copy(rows(ins[k], k, me), rows(outs[k], k, me), local_sems.at[k]) for k in range(n)]
        sends = [copy(k, f, peer) for k in range(n) for f, peer in enumerate(peers)]
        for cp in local + sends:
            cp.start()
        for k in range(n):
            for f, peer in enumerate(peers):
                arrival(k, f, peer).wait_recv()
        for cp in sends:
            cp.wait_send()
        for cp in local:
            cp.wait()

    return pl.pallas_call(
        body, name="scatter_grad_partials",
        in_specs=[ANY_SPEC] * n, out_specs=[ANY_SPEC] * n,
        out_shape=[jax.ShapeDtypeStruct(p.shape, p.dtype) for p in parts],
        scratch_shapes=[pltpu.SemaphoreType.DMA((n, 7)), pltpu.SemaphoreType.DMA((n, 7)), pltpu.SemaphoreType.DMA((n,))],
    )(*parts)


def _sum_partials(parts):
    n = len(parts)

    def body(*refs):
        ins, outs = refs[:n], refs[n:]
        first = pl.program_id(0) == 0
        for i_ref, o_ref in zip(ins, outs):
            v = i_ref[...].astype(F32)

            @pl.when(first)
            def _():
                o_ref[...] = v

            @pl.when(jnp.logical_not(first))
            def _():
                o_ref[...] += v

    shapes = [(p.shape[0] // N_DEV, p.shape[1]) for p in parts]
    return pl.pallas_call(
        body, name="sum_grad_partials", grid=(N_DEV,),
        in_specs=[pl.BlockSpec(s, lambda j: (j, 0)) for s in shapes],
        out_specs=[pl.BlockSpec(s, lambda j: (0, 0)) for s in shapes],
        out_shape=[jax.ShapeDtypeStruct(s, F32) for s in shapes],
        compiler_params=_params(("arbitrary",)),
    )(*parts)


def _all_reduce_small(slab):
    rows, cols = slab.shape

    def body(x_ref, o_ref, gathered, send_sems, recv_sems):
        x, y, c = _place_on_mesh()
        me = _dev_index(x, y, c)
        peers = _other_devices(x, y, c)

        def copy(f, peer):
            return pltpu.make_async_remote_copy(
                src_ref=x_ref, dst_ref=gathered.at[me], send_sem=send_sems.at[f], recv_sem=recv_sems.at[f],
                device_id=peer, device_id_type=MESH)

        def arrival(f, peer):
            return pltpu.make_async_remote_copy(
                src_ref=x_ref, dst_ref=gathered.at[_dev_index(*peer)], send_sem=send_sems.at[f], recv_sem=recv_sems.at[f],
                device_id=peer, device_id_type=MESH)

        sends = [copy(f, peer) for f, peer in enumerate(peers)]
        for cp in sends:
            cp.start()
        gathered[me] = x_ref[...]
        for f, peer in enumerate(peers):
            arrival(f, peer).wait_recv()
        for cp in sends:
            cp.wait_send()
        total = gathered[0]
        for d in range(1, N_DEV):
            total = total + gathered[d]
        o_ref[...] = total

    return pl.pallas_call(
        body, name="all_reduce_small",
        in_specs=[VMEM_SPEC], out_specs=VMEM_SPEC, out_shape=jax.ShapeDtypeStruct((rows, cols), F32),
        scratch_shapes=[pltpu.VMEM((N_DEV, rows, cols), F32), pltpu.SemaphoreType.DMA((7,)), pltpu.SemaphoreType.DMA((7,))],
    )(slab)


def _adamw(ws, gs, ms, vs, name):
    n = len(ws)
    c1 = 1.0 / (1.0 - ADAM_B1 ** ADAM_STEP)
    c2 = 1.0 / (1.0 - ADAM_B2 ** ADAM_STEP)

    def body(*refs):
        w_r, g_r, m_r, v_r = refs[:n], refs[n:2 * n], refs[2 * n:3 * n], refs[3 * n:4 * n]
        d_o, m_o, v_o = refs[4 * n:5 * n], refs[5 * n:6 * n], refs[6 * n:7 * n]
        for k in range(n):
            g = g_r[k][...]
            m = ADAM_B1 * m_r[k][...] + (1.0 - ADAM_B1) * g
            v = ADAM_B2 * v_r[k][...] + (1.0 - ADAM_B2) * (g * g)
            m_o[k][...] = m
            v_o[k][...] = v
            d_o[k][...] = -ADAM_LR * ((m * c1) / (jnp.sqrt(v * c2) + ADAM_EPS) + ADAM_WD * w_r[k][...])

    shapes = [jax.ShapeDtypeStruct(w.shape, F32) for w in ws]
    outs = pl.pallas_call(
        body, name=name, in_specs=[VMEM_SPEC] * (4 * n), out_specs=[VMEM_SPEC] * (3 * n), out_shape=shapes * 3,
        compiler_params=pltpu.CompilerParams(vmem_limit_bytes=56 << 20),
    )(*ws, *gs, *ms, *vs)
    return outs[:n], outs[n:2 * n], outs[2 * n:]


WEIGHT_NAMES = ("meta_tokens", "ffn1_pre_norm", "ffn1_w_gate", "ffn1_w_up", "ffn1_w_down", "ffn1_post_norm", "mix_pre_norm",
                "w_in", "gla_w_a2", "gla_b_a", "gla_out_norm", "swa_sinks", "swa_out_norm", "w_out", "mix_post_norm",
                "ffn2_pre_norm", "ffn2_w_gate", "ffn2_w_up", "ffn2_w_down", "ffn2_post_norm")
WIN_SHARD = D_IN // N_DEV
WIN_SHARD_PAD = 304
SLAB_VECTORS = ("ffn1_pre", "ffn1_post", "mix_pre", "mix_post", "ffn2_pre", "ffn2_post")
SLAB_ROWS = 32


def kernel(x, meta_tokens, ffn1_pre_norm, ffn1_w_gate, ffn1_w_up, ffn1_w_down, ffn1_post_norm, mix_pre_norm, w_in, gla_w_a2, gla_b_a, gla_out_norm, swa_sinks, swa_out_norm, w_out, mix_post_norm, ffn2_pre_norm, ffn2_w_gate, ffn2_w_up, ffn2_w_down, ffn2_post_norm, loss_target, m_meta_tokens, m_ffn1_pre_norm, m_ffn1_w_gate, m_ffn1_w_up, m_ffn1_w_down, m_ffn1_post_norm, m_mix_pre_norm, m_w_in, m_gla_w_a2, m_gla_b_a, m_gla_out_norm, m_swa_sinks, m_swa_out_norm, m_w_out, m_mix_post_norm, m_ffn2_pre_norm, m_ffn2_w_gate, m_ffn2_w_up, m_ffn2_w_down, m_ffn2_post_norm, v_meta_tokens, v_ffn1_pre_norm, v_ffn1_w_gate, v_ffn1_w_up, v_ffn1_w_down, v_ffn1_post_norm, v_mix_pre_norm, v_w_in, v_gla_w_a2, v_gla_b_a, v_gla_out_norm, v_swa_sinks, v_swa_out_norm, v_w_out, v_mix_post_norm, v_ffn2_pre_norm, v_ffn2_w_gate, v_ffn2_w_up, v_ffn2_w_down, v_ffn2_post_norm):
    given = dict(locals())
    W = {n: given[n] for n in WEIGHT_NAMES}
    M = {n: given["m_" + n] for n in WEIGHT_NAMES}
    V = {n: given["v_" + n] for n in WEIGHT_NAMES}
    dev = _dev_index(*_place_on_mesh())

    def t16(w):
        return w[0].T.astype(BF16)
    win_shard = jnp.pad(t16(W["w_in"]), ((0, WIN_SHARD_PAD - WIN_SHARD), (0, 0)))
    small = jnp.concatenate([W["meta_tokens"], jnp.pad(W["gla_w_a2"][0], ((0, 0), (0, 96)))], axis=0)
    shards = [t16(W["ffn1_w_gate"]), t16(W["ffn1_w_up"]), W["ffn1_w_down"][0].astype(BF16), win_shard,
              W["w_out"][0].astype(BF16), t16(W["ffn2_w_gate"]), t16(W["ffn2_w_up"]), W["ffn2_w_down"][0].astype(BF16), small]
    wg1, wu1, wd1, win_g, wout, wg2, wu2, wd2, small_g = _all_gather(shards)
    win_t = win_g.reshape(N_DEV, WIN_SHARD_PAD, D_MODEL)[:, :WIN_SHARD].reshape(D_IN, D_MODEL)
    small_g = small_g.reshape(N_DEV, 32, 128)
    meta_full = small_g[:, :N_META].transpose(1, 0, 2).reshape(N_META, D_MODEL)
    wa2_full = small_g[:, N_META:, :32].transpose(1, 0, 2).reshape(16, 256)
    w = dict(
        ffn1_pre=W["ffn1_pre_norm"], ffn1_post=W["ffn1_post_norm"], mix_pre=W["mix_pre_norm"], mix_post=W["mix_post_norm"],
        ffn2_pre=W["ffn2_pre_norm"], ffn2_post=W["ffn2_post_norm"], b_a=W["gla_b_a"], gla_norm=W["gla_out_norm"],
        sinks=W["swa_sinks"], swa_norm=W["swa_out_norm"], wg1=wg1, wu1=wu1, wd1=wd1, wg2=wg2, wu2=wu2, wd2=wd2,
        win=_win_pad_rows(win_t), wout=wout, wa2=jnp.pad(wa2_full, ((0, 112), (0, 0))))

    front = jnp.zeros((PAD_ROWS, D_MODEL), F32)
    h0 = jnp.concatenate([front, meta_full, x[0]], axis=0)
    tgt = jnp.concatenate([jnp.zeros((BLK, D_MODEL), F32), loss_target[0]], axis=0)
    loss, dh0, g = _local_step(h0, tgt, w)
    loss = lax.psum(loss, ("x", "y", "c"))
    grad_x = dh0[BLK:][None]

    dwin = _win_unpad_rows(g["win"]).reshape(N_DEV, WIN_SHARD, D_MODEL)
    dwin = jnp.pad(dwin, ((0, 0), (0, WIN_SHARD_PAD - WIN_SHARD), (0, 0))).reshape(N_DEV * WIN_SHARD_PAD, D_MODEL)
    parts = [p.astype(BF16) for p in (g["wg1"], g["wu1"], g["wd1"], dwin, g["wout"], g["wg2"], g["wu2"], g["wd2"])]
    sums = _sum_partials(_scatter_partials(parts))
    big = dict(ffn1_w_gate=sums[0].T[None], ffn1_w_up=sums[1].T[None], ffn1_w_down=sums[2][None],
               w_in=sums[3][:WIN_SHARD].T[None], w_out=sums[4][None],
               ffn2_w_gate=sums[5].T[None], ffn2_w_up=sums[6].T[None], ffn2_w_down=sums[7][None])

    packed = jnp.concatenate([g["b_a"][0:1], g["gla_norm"][0:1], g["sinks"][0:1], g["swa_norm"][0:1]], axis=1)
    slab = jnp.concatenate([g[k][0:1] for k in SLAB_VECTORS] + [packed, jnp.zeros((1, D_MODEL), F32),
                           g["wa2"][:16].reshape(4, D_MODEL), jnp.zeros((4, D_MODEL), F32), dh0[PAD_ROWS:BLK]], axis=0)
    tot = _all_reduce_small(slab)
    small_grads = dict(
        ffn1_pre_norm=tot[0:1], ffn1_post_norm=tot[1:2], mix_pre_norm=tot[2:3], mix_post_norm=tot[3:4],
        ffn2_pre_norm=tot[4:5], ffn2_post_norm=tot[5:6], gla_b_a=tot[6:7, 0:256], gla_out_norm=tot[6:7, 256:384],
        swa_sinks=tot[6:7, 384:392], swa_out_norm=tot[6:7, 512:1024],
        gla_w_a2=lax.dynamic_slice_in_dim(tot[8:12].reshape(16, 256), dev * 32, 32, axis=1)[None],
        meta_tokens=lax.dynamic_slice_in_dim(tot[16:32], dev * 128, 128, axis=1))
    grads = {**big, **small_grads}

    delta, new_m, new_v = {}, {}, {}
    for n in big:
        d_, m_, v_ = _adamw([W[n][0]], [grads[n][0]], [M[n][0]], [V[n][0]], "adamw_" + n)
        delta[n], new_m[n], new_v[n] = d_[0][None], m_[0][None], v_[0][None]
    names = [n for n in WEIGHT_NAMES if n not in big]
    two_d = lambda a: a.reshape(-1, a.shape[-1])
    d_, m_, v_ = _adamw([two_d(W[n]) for n in names], [two_d(grads[n]) for n in names],
                        [two_d(M[n]) for n in names], [two_d(V[n]) for n in names], "adamw_small")
    for k, n in enumerate(names):
        delta[n], new_m[n], new_v[n] = d_[k].reshape(W[n].shape), m_[k].reshape(W[n].shape), v_[k].reshape(W[n].shape)
    return (loss, grad_x, *[grads[n] for n in WEIGHT_NAMES], *[delta[n] for n in WEIGHT_NAMES],
            *[new_m[n] for n in WEIGHT_NAMES], *[new_v[n] for n in WEIGHT_NAMES])
```

```python
import functools

import jax
import jax.numpy as jnp
from jax import lax
from jax.experimental import pallas as pl
from jax.experimental.pallas import tpu as pltpu

F32, BF16 = jnp.float32, jnp.bfloat16

D_MODEL = 1024
D_FF = 2816
N_META = 16
BLK = 128
PAD_ROWS = BLK - N_META
GLA_DK = 64
SWA_HD = 64
SWA_HEADS = 8
GLA_TAU = 16.0
NORM_EPS = 1e-6
NEG_INF = -1e30
ROPE_THETA = 10000.0
P_GQ, P_GK, P_GV, P_GG, P_SQ, P_SK, P_SV, P_GA, P_END = 0, 256, 512, 1024, 1536, 2048, 2176, 2304, 2432
D_IN = 2320
IN_SPLITS = (256, 256, 512, 512, 16, 512, 128, 128)
FF_TILE = 2816
WGRAD_TILE = 256
N_DEV = 8
MESH = pl.DeviceIdType.MESH

ADAM_LR, ADAM_B1, ADAM_B2, ADAM_EPS, ADAM_WD, ADAM_STEP = 0.001, 0.9, 0.999, 1e-08, 0.01, 10

V7X_VMEM_BYTES = 64 << 20
VMEM_SPEC = pl.BlockSpec(memory_space=pltpu.VMEM)
SMEM_SPEC = pl.BlockSpec(memory_space=pltpu.SMEM)
ANY_SPEC = pl.BlockSpec(memory_space=pl.ANY)


def _params(semantics, vmem_mb=56):
    return pltpu.CompilerParams(dimension_semantics=semantics, vmem_limit_bytes=vmem_mb << 20)


def _row_tile(rows):
    return 320 if rows % 320 == 0 else BLK


def _nn(a, b):
    return lax.dot_general(a, b, (((1,), (0,)), ((), ())), preferred_element_type=F32)


def _nt(a, b):
    return lax.dot_general(a, b, (((1,), (1,)), ((), ())), preferred_element_type=F32)


def _tn(a, b):
    return lax.dot_general(a, b, (((0,), (0,)), ((), ())), preferred_element_type=F32)


def _rms(x):
    r = lax.rsqrt(jnp.mean(x * x, axis=-1, keepdims=True) + NORM_EPS)
    return x * r, r


def _rms_bwd(xn, r, w, dy):
    g = dy * w
    return r * (g - xn * jnp.mean(g * xn, axis=-1, keepdims=True))


def _sigmoid(x):
    return 1.0 / (1.0 + jnp.exp(-x))


def _colsum(x):
    return jnp.sum(x, axis=0, keepdims=True)


def _split_bf16(x):
    hi = x.astype(BF16)
    lo = (x - hi.astype(F32)).astype(BF16)
    return hi, lo


def _tri(lower):
    r = lax.broadcasted_iota(jnp.int32, (BLK, BLK), 0)
    c = lax.broadcasted_iota(jnp.int32, (BLK, BLK), 1)
    return (r >= c) if lower else (c >= r)


def _half_mask(width, half):
    lane = lax.broadcasted_iota(jnp.int32, (1, width), 1)
    return ((lane % 128) < 64) if half == 0 else ((lane % 128) >= 64)


def _rot_half(x):
    w = x.shape[-1]
    lane = lax.broadcasted_iota(jnp.int32, (1, w), 1)
    return jnp.where((lane % SWA_HD) < SWA_HD // 2, -pltpu.roll(x, w - SWA_HD // 2, 1), pltpu.roll(x, SWA_HD // 2, 1))


def _row_spec(tm, cols):
    return pl.BlockSpec((tm, cols), lambda i: (i, 0))


def _acc_spec(cols):
    return pl.BlockSpec((8, cols), lambda i: (0, 0))


def _acc_add(ref, first, value):
    @pl.when(first)
    def _():
        ref[...] = jnp.zeros_like(ref)
    ref[0:1, :] += value


def _ffn_fwd(h, gpre, wg_t, wu_t, wd, gpost, tgt=None):
    rows = h.shape[0]
    tm = _row_tile(rows)
    nf = D_FF // FF_TILE
    with_loss = tgt is not None

    def body(*refs):
        if with_loss:
            (h_ref, gpre_ref, wg_ref, wu_ref, wd_ref, gpost_ref, t_ref,
             ho_ref, a_ref, b_ref, f_ref, dy_ref, loss_ref, acc) = refs
        else:
            (h_ref, gpre_ref, wg_ref, wu_ref, wd_ref, gpost_ref, ho_ref, a_ref, b_ref, f_ref, acc) = refs
        i = pl.program_id(0)
        h_in = h_ref[...]
        hn, _ = _rms(h_in)
        n16 = (hn * gpre_ref[...]).astype(BF16)
        for j in range(nf):
            cols = slice(j * FF_TILE, (j + 1) * FF_TILE)
            a = _nt(n16, wg_ref[cols, :])
            b = _nt(n16, wu_ref[cols, :])
            a_ref[:, cols] = a.astype(BF16)
            b_ref[:, cols] = b.astype(BF16)
            s16 = (a * _sigmoid(a) * b).astype(BF16)
            part = _nn(s16, wd_ref[cols, :])
            if j == 0:
                acc[...] = part
            else:
                acc[...] += part
        f = acc[...]
        f_ref[...] = f
        fn, _ = _rms(f)
        y = h_in + 0.5 * (fn * gpost_ref[...])
        ho_ref[...] = y
        if with_loss:
            row = i * tm + lax.broadcasted_iota(jnp.int32, (tm, 1), 0)
            err = jnp.where(row >= BLK, y - t_ref[...], 0.0)
            dy_ref[...] = err * (1.0 / D_MODEL)
            part = 0.5 * jnp.sum(jnp.sum(err * err, axis=-1, keepdims=True) * (1.0 / D_MODEL), axis=0, keepdims=True)

            @pl.when(i == 0)
            def _():
                loss_ref[...] = jnp.zeros_like(loss_ref)
            loss_ref[...] += part

    row_f32 = _row_spec(tm, D_MODEL)
    in_specs = [row_f32, VMEM_SPEC, VMEM_SPEC, VMEM_SPEC, VMEM_SPEC, VMEM_SPEC]
    out_specs = [row_f32, _row_spec(tm, D_FF), _row_spec(tm, D_FF), row_f32]
    out_shape = [jax.ShapeDtypeStruct((rows, D_MODEL), F32), jax.ShapeDtypeStruct((rows, D_FF), BF16),
                 jax.ShapeDtypeStruct((rows, D_FF), BF16), jax.ShapeDtypeStruct((rows, D_MODEL), F32)]
    args = [h, gpre, wg_t, wu_t, wd, gpost]
    if with_loss:
        in_specs.append(row_f32)
        args.append(tgt)
        out_specs += [row_f32, pl.BlockSpec((8, 128), lambda i: (0, 0))]
        out_shape += [jax.ShapeDtypeStruct((rows, D_MODEL), F32), jax.ShapeDtypeStruct((8, 128), F32)]
    return pl.pallas_call(
        body, name="ffn_fwd_loss" if with_loss else "ffn_fwd", grid=(rows // tm,),
        in_specs=in_specs, out_specs=out_specs, out_shape=out_shape,
        scratch_shapes=[pltpu.VMEM((tm, D_MODEL), F32)],
        compiler_params=_params(("arbitrary",)),
    )(*args)


def _ffn_bwd_act(dh_out, h, a, b, f, gpre, gpost, wg_t, wu_t, wd, name):
    rows = h.shape[0]
    tm = _row_tile(rows)
    nf = D_FF // FF_TILE

    def body(dho_ref, h_ref, a_ref, b_ref, f_ref, gpre_ref, gpost_ref, wg_ref, wu_ref, wd_ref,
             dh_ref, da_ref, db_ref, df_ref, n_ref, dgpre_ref, dgpost_ref, acc):
        first = pl.program_id(0) == 0
        dho = dho_ref[...]
        drr = 0.5 * dho
        fn, rf = _rms(f_ref[...])
        _acc_add(dgpost_ref, first, _colsum(drr * fn))
        df16 = _rms_bwd(fn, rf, gpost_ref[...], drr).astype(BF16)
        df_ref[...] = df16
        hn, rh = _rms(h_ref[...])
        n_ref[...] = (hn * gpre_ref[...]).astype(BF16)
        for j in range(nf):
            cols = slice(j * FF_TILE, (j + 1) * FF_TILE)
            ds = _nt(df16, wd_ref[cols, :])
            av = a_ref[:, cols].astype(F32)
            bv = b_ref[:, cols].astype(F32)
            sg = _sigmoid(av)
            db16 = (ds * (av * sg)).astype(BF16)
            da16 = (ds * bv * (sg * (1.0 + av * (1.0 - sg)))).astype(BF16)
            da_ref[:, cols] = da16
            db_ref[:, cols] = db16
            part = _nn(da16, wg_ref[cols, :]) + _nn(db16, wu_ref[cols, :])
            if j == 0:
                acc[...] = part
            else:
                acc[...] += part
        dn = acc[...]
        _acc_add(dgpre_ref, first, _colsum(dn * hn))
        dh_ref[...] = dho + _rms_bwd(hn, rh, gpre_ref[...], dn)

    row_f32 = _row_spec(tm, D_MODEL)
    row_ff = _row_spec(tm, D_FF)
    return pl.pallas_call(
        body, name=name, grid=(rows // tm,),
        in_specs=[row_f32, row_f32, row_ff, row_ff, row_f32, VMEM_SPEC, VMEM_SPEC, VMEM_SPEC, VMEM_SPEC, VMEM_SPEC],
        out_specs=[row_f32, row_ff, row_ff, row_f32, row_f32, _acc_spec(D_MODEL), _acc_spec(D_MODEL)],
        out_shape=[jax.ShapeDtypeStruct((rows, D_MODEL), F32), jax.ShapeDtypeStruct((rows, D_FF), BF16),
                   jax.ShapeDtypeStruct((rows, D_FF), BF16), jax.ShapeDtypeStruct((rows, D_MODEL), BF16),
                   jax.ShapeDtypeStruct((rows, D_MODEL), BF16), jax.ShapeDtypeStruct((8, D_MODEL), F32),
                   jax.ShapeDtypeStruct((8, D_MODEL), F32)],
        scratch_shapes=[pltpu.VMEM((tm, D_MODEL), F32)],
        compiler_params=_params(("arbitrary",)),
    )(dh_out, h, a, b, f, gpre, gpost, wg_t, wu_t, wd)


def _ffn_bwd_weights(a, b, da, db, df, n, name):
    rows = a.shape[0]
    tm = 1664 if rows % 1664 == 0 else BLK
    tf = WGRAD_TILE
    nr = rows // tm

    def body(a_ref, b_ref, da_ref, db_ref, df_ref, n_ref, dwg_ref, dwu_ref, dwd_ref, acc_g, acc_u, acc_d):
        i = pl.program_id(1)
        av = a_ref[...].astype(F32)
        s16 = (av * _sigmoid(av) * b_ref[...].astype(F32)).astype(BF16)
        n16 = n_ref[...]
        pg = _tn(da_ref[...], n16)
        pu = _tn(db_ref[...], n16)
        pd = _tn(s16, df_ref[...])

        @pl.when(i == 0)
        def _():
            acc_g[...] = pg
            acc_u[...] = pu
            acc_d[...] = pd

        @pl.when(i > 0)
        def _():
            acc_g[...] += pg
            acc_u[...] += pu
            acc_d[...] += pd

        @pl.when(i == nr - 1)
        def _():
            dwg_ref[...] = acc_g[...].astype(BF16)
            dwu_ref[...] = acc_u[...].astype(BF16)
            dwd_ref[...] = acc_d[...].astype(BF16)

    ff_spec = pl.BlockSpec((tm, tf), lambda j, i: (i, j))
    d_spec = pl.BlockSpec((tm, D_MODEL), lambda j, i: (i, 0))
    w_spec = pl.BlockSpec((tf, D_MODEL), lambda j, i: (j, 0))
    w_shape = jax.ShapeDtypeStruct((D_FF, D_MODEL), BF16)
    return pl.pallas_call(
        body, name=name, grid=(D_FF // tf, nr),
        in_specs=[ff_spec, ff_spec, ff_spec, ff_spec, d_spec, d_spec],
        out_specs=[w_spec, w_spec, w_spec], out_shape=[w_shape, w_shape, w_shape],
        scratch_shapes=[pltpu.VMEM((tf, D_MODEL), F32)] * 3,
        compiler_params=_params(("arbitrary", "arbitrary")),
    )(a, b, da, db, df, n)


def _chunk_cumsum(x, lower):
    tri = jnp.where(_tri(lower), 1.0, 0.0).astype(BF16)
    hi, lo = _split_bf16(x)
    return _nn(tri, hi) + _nn(tri, lo)


def _mix_in(h, g, win_p, wa2_p, b_a, cos, sin):
    rows = h.shape[0]
    tm = _row_tile(rows)
    tm = tm if tm % BLK == 0 else BLK

    def body(h_ref, g_ref, win_ref, wa2_ref, ba_ref, cos_ref, sin_ref,
             gq_ref, gk_ref, gv_ref, gg_ref, sq_ref, sk_ref, sv_ref, ga_ref, loga_ref, bc_ref, n_ref):
        hn, _ = _rms(h_ref[...])
        n16 = (hn * g_ref[...]).astype(BF16)
        n_ref[...] = n16
        proj = _nt(n16, win_ref[...])
        gq_ref[...] = proj[:, P_GQ:P_GK]
        gk_ref[...] = proj[:, P_GK:P_GV]
        gv_ref[...] = proj[:, P_GV:P_GG].astype(BF16)
        gg_ref[...] = proj[:, P_GG:P_SQ]
        c1, s1 = cos_ref[...], sin_ref[...]
        c4 = jnp.concatenate([c1, c1, c1, c1], axis=1)
        s4 = jnp.concatenate([s1, s1, s1, s1], axis=1)
        sq = proj[:, P_SQ:P_SK]
        sk = proj[:, P_SK:P_SV]
        sq_ref[...] = (sq * c4 + _rot_half(sq) * s4).astype(BF16)
        sk_ref[...] = (sk * c1 + _rot_half(sk) * s1).astype(BF16)
        sv_ref[...] = proj[:, P_SV:P_GA].astype(BF16)
        ga = proj[:, P_GA:P_END]
        ga_ref[...] = ga
        z = _nn(ga, wa2_ref[...]) + ba_ref[...]
        loga = (jnp.minimum(z, 0.0) - jnp.log(1.0 + jnp.exp(-jnp.abs(z)))) * (1.0 / GLA_TAU)
        loga_ref[...] = loga
        for c in range(tm // BLK):
            rs = slice(c * BLK, (c + 1) * BLK)
            bc_ref[rs, :] = _chunk_cumsum(loga[rs, :], True)

    f32 = lambda c: jax.ShapeDtypeStruct((rows, c), F32)
    b16 = lambda c: jax.ShapeDtypeStruct((rows, c), BF16)
    rs = lambda c: _row_spec(tm, c)
    return pl.pallas_call(
        body, name="mix_in", grid=(rows // tm,),
        in_specs=[rs(D_MODEL), VMEM_SPEC, VMEM_SPEC, VMEM_SPEC, VMEM_SPEC, rs(128), rs(128)],
        out_specs=[rs(256), rs(256), rs(512), rs(512), rs(512), rs(128), rs(128), rs(128), rs(256), rs(256), rs(D_MODEL)],
        out_shape=[f32(256), f32(256), b16(512), f32(512), b16(512), b16(128), b16(128), f32(128), f32(256), f32(256),
                   b16(D_MODEL)],
        compiler_params=_params(("arbitrary",)),
    )(h, g, win_p, wa2_p, b_a, cos, sin)


def _gla_factors(q, k, bc):
    bm = bc[BLK // 2 - 1:BLK // 2, :]
    bl = bc[BLK - 1:BLK, :]
    e_q, e_k, e_qe, e_kd = jnp.exp(bc - bm), jnp.exp(bm - bc), jnp.exp(bc), jnp.exp(bl - bc)
    return (q * e_q, k * e_k, q * e_qe, k * e_kd), (e_q, e_k, e_qe, e_kd), jnp.exp(bl)


def _gla_fwd(gq, gk, gv, gg, bc, wgn):
    rows = gq.shape[0]
    nc = rows // BLK
    scale = GLA_DK ** -0.5

    def body(q_ref, k_ref, v_ref, gg_ref, bc_ref, wgn_ref, o_ref, cat_ref, sp_ref, st):
        @pl.when(pl.program_id(0) == 0)
        def _():
            st[...] = jnp.zeros_like(st)
        low = _tri(True)
        wgn_v = wgn_ref[...]
        for p in range(2):
            sl = slice(128 * p, 128 * p + 128)
            (qt, kt, qe, kd), _, ebl = _gla_factors(q_ref[:, sl] * scale, k_ref[:, sl], bc_ref[:, sl])
            s_prev = st[p]
            sp_ref[0, p] = s_prev
            s16 = s_prev.astype(BF16)
            qt16 = qt.astype(BF16)
            s_new = s_prev * ebl
            for hh in range(2):
                hs = slice(128 * (2 * p + hh), 128 * (2 * p + hh) + 128)
                lm = _half_mask(128, hh)
                vh = v_ref[:, hs]
                pm = jnp.where(low, _nt(qt16, jnp.where(lm, kt, 0.0).astype(BF16)), 0.0)
                o = _nn(pm.astype(BF16), vh) + _nt(jnp.where(lm, qe, 0.0).astype(BF16), s16)
                s_new = s_new + _tn(vh, jnp.where(lm, kd, 0.0).astype(BF16))
                o_ref[:, hs] = o
                on, _ = _rms(o)
                gate = gg_ref[:, hs]
                cat_ref[:, hs] = (on * wgn_v * (gate * _sigmoid(gate))).astype(BF16)
            st[p] = s_new

    rs = lambda c: _row_spec(BLK, c)
    return pl.pallas_call(
        body, name="gla_fwd", grid=(nc,),
        in_specs=[rs(256), rs(256), rs(512), rs(512), rs(256), VMEM_SPEC],
        out_specs=[rs(512), rs(512), pl.BlockSpec((1, 2, 128, 128), lambda i: (i, 0, 0, 0))],
        out_shape=[jax.ShapeDtypeStruct((rows, 512), F32), jax.ShapeDtypeStruct((rows, 512), BF16),
                   jax.ShapeDtypeStruct((nc, 2, 128, 128), F32)],
        scratch_shapes=[pltpu.VMEM((2, 128, 128), F32)],
        compiler_params=_params(("arbitrary",)),
    )(gq, gk, gv, gg, bc, wgn)


def _gla_bwd(dcat, o_all, gq, gk, gv, gg, bc, sp, wgn):
    rows = gq.shape[0]
    nc = rows // BLK
    scale = GLA_DK ** -0.5

    def body(dc_ref, o_ref, q_ref, k_ref, v_ref, gg_ref, bc_ref, sp_ref, wgn_ref,
             dq_ref, dk_ref, dv_ref, dgg_ref, dla_ref, dwgn_ref, dst):
        first = pl.program_id(0) == 0

        @pl.when(first)
        def _():
            dst[...] = jnp.zeros_like(dst)
        low, upp = _tri(True), _tri(False)
        last_row = lax.broadcasted_iota(jnp.int32, (BLK, 1), 0) == BLK - 1
        wgn_v = wgn_ref[...]
        dwgn = jnp.zeros((1, 128), F32)
        for p in range(2):
            sl = slice(128 * p, 128 * p + 128)
            (qt, kt, qe, kd), (e_q, e_k, e_qe, e_kd), ebl = _gla_factors(
                q_ref[:, sl] * scale, k_ref[:, sl], bc_ref[:, sl])
            s_prev = sp_ref[0, p]
            s16 = s_prev.astype(BF16)
            ds_next = dst[p]
            ds16 = ds_next.astype(BF16)
            qt16 = qt.astype(BF16)
            ds_new = ds_next * ebl
            dqt = jnp.zeros((BLK, 128), F32)
            dkt = jnp.zeros((BLK, 128), F32)
            dqe = jnp.zeros((BLK, 128), F32)
            dkd = jnp.zeros((BLK, 128), F32)
            for hh in range(2):
                hs = slice(128 * (2 * p + hh), 128 * (2 * p + hh) + 128)
                lm = _half_mask(128, hh)
                on, ro = _rms(o_ref[:, hs])
                gate = gg_ref[:, hs]
                sg = _sigmoid(gate)
                si = gate * sg
                dog = dc_ref[:, hs]
                dwgn = dwgn + _colsum(dog * si * on)
                dgg_ref[:, hs] = dog * (on * wgn_v) * (sg * (1.0 + gate * (1.0 - sg)))
                do16 = _rms_bwd(on, ro, wgn_v, dog * si).astype(BF16)
                vh = v_ref[:, hs]
                ktm16 = jnp.where(lm, kt, 0.0).astype(BF16)
                qtm16 = jnp.where(lm, qt, 0.0).astype(BF16)
                qem16 = jnp.where(lm, qe, 0.0).astype(BF16)
                kdm16 = jnp.where(lm, kd, 0.0).astype(BF16)
                p_t = jnp.where(upp, _nt(ktm16, qt16), 0.0)
                dp_t = jnp.where(upp, _nt(vh, do16), 0.0)
                dp = jnp.where(low, _nt(do16, vh), 0.0)
                dv_ref[:, hs] = _nn(p_t.astype(BF16), do16) + _nt(kdm16, ds16)
                dqt = dqt + _nn(dp.astype(BF16), ktm16)
                dkt = dkt + _nn(dp_t.astype(BF16), qtm16)
                dqe = dqe + jnp.where(lm, _nn(do16, s16), 0.0)
                dkd = dkd + jnp.where(lm, _nn(vh, ds16), 0.0)
                ds_new = ds_new + _tn(do16, qem16)
            debl = _colsum(ds_next * s_prev)
            dq_ref[:, sl] = (dqt * e_q + dqe * e_qe) * scale
            dk_ref[:, sl] = dkt * e_k + dkd * e_kd
            dkd_kd = dkd * kd
            db = dqt * qt - dkt * kt + dqe * qe - dkd_kd
            db = jnp.where(last_row, db + (_colsum(dkd_kd) + debl * ebl), db)
            dla_ref[:, sl] = _chunk_cumsum(db, False)
            dst[p] = ds_new
        _acc_add(dwgn_ref, first, dwgn)

    rev = lambda c: pl.BlockSpec((BLK, c), lambda i: (nc - 1 - i, 0))
    f32 = lambda c: jax.ShapeDtypeStruct((rows, c), F32)
    return pl.pallas_call(
        body, name="gla_bwd", grid=(nc,),
        in_specs=[rev(512), rev(512), rev(256), rev(256), rev(512), rev(512), rev(256),
                  pl.BlockSpec((1, 2, 128, 128), lambda i: (nc - 1 - i, 0, 0, 0)), VMEM_SPEC],
        out_specs=[rev(256), rev(256), rev(512), rev(512), rev(256), _acc_spec(128)],
        out_shape=[f32(256), f32(256), f32(512), f32(512), f32(256), jax.ShapeDtypeStruct((8, 128), F32)],
        scratch_shapes=[pltpu.VMEM((2, 128, 128), F32)],
        compiler_params=_params(("arbitrary",)),
    )(dcat, o_all, gq, gk, gv, gg, bc, sp, wgn)


def _swa_mask(i):
    t = lax.broadcasted_iota(jnp.int32, (BLK, 3 * BLK), 0)
    c = lax.broadcasted_iota(jnp.int32, (BLK, 3 * BLK), 1)
    qpos = i * BLK + t - PAD_ROWS
    seg = c // BLK
    cc = c % BLK
    kpos = jnp.where(seg == 0, (i - 1) * BLK + cc, i * BLK + cc) - PAD_ROWS
    band = (seg < 2) & (kpos >= N_META) & (kpos <= qpos) & (qpos - kpos < BLK)
    midx = cc - PAD_ROWS
    meta = (seg == 2) & (midx >= 0) & (midx <= qpos)
    return band | meta


def _swa_keys(k_ref, i):
    prev = pl.multiple_of(jnp.maximum(i - 1, 0) * BLK, BLK)
    own = pl.multiple_of(i * BLK, BLK)
    return jnp.concatenate([k_ref[pl.ds(prev, BLK), :], k_ref[pl.ds(own, BLK), :], k_ref[0:BLK, :]], axis=0), prev, own


def _place(x, kv):
    if kv == 0:
        lo = jnp.where(_half_mask(128, 0), x, jnp.zeros_like(x))
        return lo, pltpu.roll(lo, 64, 1)
    hi = jnp.where(_half_mask(128, 1), x, jnp.zeros_like(x))
    return pltpu.roll(hi, 64, 1), hi


def _swa_fwd(sq, sk, sv, sinks, wn):
    rows = sq.shape[0]
    nb = rows // BLK
    scale = SWA_HD ** -0.5

    def body(q_ref, k_ref, v_ref, sink_ref, wn_ref, o_ref, cat_ref, lse_ref):
        i = pl.program_id(0)
        mask = _swa_mask(i)
        keys, _, _ = _swa_keys(k_ref, i)
        vals, _, _ = _swa_keys(v_ref, i)
        kz = (_place(keys, 0), _place(keys, 1))
        vz = (_place(vals, 0), _place(vals, 1))
        outs = []
        for pr in range(4):
            q_pair = q_ref[:, 128 * pr:128 * pr + 128]
            o_pair = jnp.zeros((BLK, 128), F32)
            for half in range(2):
                hd = 2 * pr + half
                kv = hd // 4
                sink = sink_ref[0, hd]
                s = jnp.where(mask, _nt(q_pair, kz[kv][half]) * scale, NEG_INF)
                m = jnp.maximum(jnp.max(s, axis=-1, keepdims=True), sink)
                e = jnp.exp(s - m)
                den = jnp.sum(e, axis=-1, keepdims=True) + jnp.exp(sink - m)
                lse_ref[:, hd:hd + 1] = m + jnp.log(den)
                o_pair = o_pair + _nn((e / den).astype(BF16), vz[kv][half])
            outs.append(o_pair)
        o = jnp.concatenate(outs, axis=1)
        o_ref[...] = o
        on, _ = _rms(o)
        cat_ref[...] = (on * wn_ref[...]).astype(BF16)

    return pl.pallas_call(
        body, name="swa_fwd", grid=(nb,),
        in_specs=[_row_spec(BLK, 512), VMEM_SPEC, VMEM_SPEC, SMEM_SPEC, VMEM_SPEC],
        out_specs=[_row_spec(BLK, 512), _row_spec(BLK, 512), _row_spec(BLK, SWA_HEADS)],
        out_shape=[jax.ShapeDtypeStruct((rows, 512), F32), jax.ShapeDtypeStruct((rows, 512), BF16),
                   jax.ShapeDtypeStruct((rows, SWA_HEADS), F32)],
        compiler_params=_params(("arbitrary",)),
    )(sq, sk, sv, sinks, wn)


def _swa_bwd(dcat, o_all, sq, sk, sv, lse, sinks, wn):
    rows = sq.shape[0]
    nb = rows // BLK
    scale = SWA_HD ** -0.5

    def body(dc_ref, o_ref, q_ref, k_ref, v_ref, lse_ref, sink_ref, wn_ref, dq_ref, dk_ref, dv_ref, dsink_ref, dwn_ref):
        i = pl.program_id(0)
        first = i == 0

        @pl.when(first)
        def _():
            dk_ref[...] = jnp.zeros_like(dk_ref)
            dv_ref[...] = jnp.zeros_like(dv_ref)
        mask = _swa_mask(i)
        keys, prev, own = _swa_keys(k_ref, i)
        vals, _, _ = _swa_keys(v_ref, i)
        kz = (_place(keys, 0), _place(keys, 1))
        vz = (_place(vals, 0), _place(vals, 1))
        o = o_ref[...]
        on, ro = _rms(o)
        dc = dc_ref[...]
        _acc_add(dwn_ref, first, _colsum(dc * on))
        do = _rms_bwd(on, ro, wn_ref[...], dc)
        do_o = do * o
        lane8 = lax.broadcasted_iota(jnp.int32, (1, 128), 1)
        dsink = jnp.zeros((1, 128), F32)
        dk_acc = jnp.zeros((3 * BLK, 128), F32)
        dv_acc = jnp.zeros((3 * BLK, 128), F32)
        dqs = []
        for pr in range(4):
            ps = slice(128 * pr, 128 * pr + 128)
            q_pair = q_ref[:, ps]
            do16 = do[:, ps].astype(BF16)
            dq_pair = jnp.zeros((BLK, 128), F32)
            for half in range(2):
                hd = 2 * pr + half
                kv = hd // 4
                hm = _half_mask(128, half)
                sink = sink_ref[0, hd]
                lse_h = lse_ref[:, hd:hd + 1]
                delta = jnp.sum(jnp.where(hm, do_o[:, ps], 0.0), axis=-1, keepdims=True)
                s = jnp.where(mask, _nt(q_pair, kz[kv][half]) * scale, NEG_INF)
                prob = jnp.exp(s - lse_h)
                dsink = dsink + jnp.where(lane8 == hd, -jnp.sum(jnp.exp(sink - lse_h) * delta), 0.0)
                dp = _nt(do16, vz[kv][half])
                ds16 = (prob * (dp - delta) * scale).astype(BF16)
                dq_pair = dq_pair + _nn(ds16, kz[kv][half])
                dkz = jnp.where(hm, _tn(ds16, q_pair), 0.0)
                dvz = jnp.where(hm, _tn(prob.astype(BF16), do16), 0.0)
                if half != kv:
                    dkz = pltpu.roll(dkz, 64, 1)
                    dvz = pltpu.roll(dvz, 64, 1)
                dk_acc = dk_acc + dkz
                dv_acc = dv_acc + dvz
            dqs.append(dq_pair)
        dq_ref[...] = jnp.concatenate(dqs, axis=1)
        _acc_add(dsink_ref, first, dsink)
        for ref, acc in ((dk_ref, dk_acc), (dv_ref, dv_acc)):
            ref[pl.ds(prev, BLK), :] += acc[0:BLK]
            ref[pl.ds(own, BLK), :] += acc[BLK:2 * BLK]
            ref[0:BLK, :] += acc[2 * BLK:3 * BLK]

    full = pl.BlockSpec((rows, 128), lambda i: (0, 0))
    return pl.pallas_call(
        body, name="swa_bwd", grid=(nb,),
        in_specs=[_row_spec(BLK, 512), _row_spec(BLK, 512), _row_spec(BLK, 512), VMEM_SPEC, VMEM_SPEC,
                  _row_spec(BLK, SWA_HEADS), SMEM_SPEC, VMEM_SPEC],
        out_specs=[_row_spec(BLK, 512), full, full, _acc_spec(128), _acc_spec(512)],
        out_shape=[jax.ShapeDtypeStruct((rows, 512), F32), jax.ShapeDtypeStruct((rows, 128), F32),
                   jax.ShapeDtypeStruct((rows, 128), F32), jax.ShapeDtypeStruct((8, 128), F32),
                   jax.ShapeDtypeStruct((8, 512), F32)],
        compiler_params=_params(("arbitrary",)),
    )(dcat, o_all, sq, sk, sv, lse, sinks, wn)


def _mix_out(h, cat_g, cat_s, wout, gpost):
    rows = h.shape[0]
    tm = _row_tile(rows)

    def body(h_ref, cg_ref, cs_ref, w_ref, g_ref, ho_ref, m_ref):
        m = _nn(cg_ref[...], w_ref[0:512, :]) + _nn(cs_ref[...], w_ref[512:1024, :])
        m_ref[...] = m
        mn, _ = _rms(m)
        ho_ref[...] = h_ref[...] + mn * g_ref[...]

    row_f32 = _row_spec(tm, D_MODEL)
    return pl.pallas_call(
        body, name="mix_out", grid=(rows // tm,),
        in_specs=[row_f32, _row_spec(tm, 512), _row_spec(tm, 512), VMEM_SPEC, VMEM_SPEC],
        out_specs=[row_f32, row_f32],
        out_shape=[jax.ShapeDtypeStruct((rows, D_MODEL), F32), jax.ShapeDtypeStruct((rows, D_MODEL), F32)],
        compiler_params=_params(("arbitrary",)),
    )(h, cat_g, cat_s, wout, gpost)


def _mix_out_bwd(dh, m, cat_g, cat_s, wout, gpost):
    rows = dh.shape[0]
    tm = _row_tile(rows)

    def body(dh_ref, m_ref, cg_ref, cs_ref, w_ref, g_ref, dcg_ref, dcs_ref, dw_ref, dg_ref):
        first = pl.program_id(0) == 0
        dhv = dh_ref[...]
        mn, rm = _rms(m_ref[...])
        _acc_add(dg_ref, first, _colsum(dhv * mn))
        dm16 = _rms_bwd(mn, rm, g_ref[...], dhv).astype(BF16)
        dcat = _nt(dm16, w_ref[...])
        dcg_ref[...] = dcat[:, 0:512]
        dcs_ref[...] = dcat[:, 512:1024]
        pg = _tn(cg_ref[...], dm16)
        ps = _tn(cs_ref[...], dm16)

        @pl.when(first)
        def _():
            dw_ref[0:512, :] = pg
            dw_ref[512:1024, :] = ps

        @pl.when(jnp.logical_not(first))
        def _():
            dw_ref[0:512, :] += pg
            dw_ref[512:1024, :] += ps

    row_f32 = _row_spec(tm, D_MODEL)
    return pl.pallas_call(
        body, name="mix_out_bwd", grid=(rows // tm,),
        in_specs=[row_f32, row_f32, _row_spec(tm, 512), _row_spec(tm, 512), VMEM_SPEC, VMEM_SPEC],
        out_specs=[_row_spec(tm, 512), _row_spec(tm, 512), pl.BlockSpec((D_MODEL, D_MODEL), lambda i: (0, 0)),
                   _acc_spec(D_MODEL)],
        out_shape=[jax.ShapeDtypeStruct((rows, 512), F32), jax.ShapeDtypeStruct((rows, 512), F32),
                   jax.ShapeDtypeStruct((D_MODEL, D_MODEL), F32), jax.ShapeDtypeStruct((8, D_MODEL), F32)],
        compiler_params=_params(("arbitrary",)),
    )(dh, m, cat_g, cat_s, wout, gpost)


def _mix_in_bwd(dh_out, h, n16, g, win_p, wa2_p, cos, sin, loga, ga, dgq, dgk, dgv, dgg, dsq, dsk, dsv, dloga):
    rows = h.shape[0]
    tm = _row_tile(rows)

    def body(dho_ref, h_ref, n_ref, g_ref, win_ref, wa2_ref, cos_ref, sin_ref, loga_ref, ga_ref,
             dgq_ref, dgk_ref, dgv_ref, dgg_ref, dsq_ref, dsk_ref, dsv_ref, dla_ref,
             dh_ref, dwin_ref, dwa2_ref, dg_ref, dba_ref):
        first = pl.program_id(0) == 0
        dz = dla_ref[...] * (1.0 / GLA_TAU) * (1.0 - jnp.exp(GLA_TAU * loga_ref[...]))
        _acc_add(dba_ref, first, _colsum(dz))
        dga = _nt(dz, wa2_ref[...])
        pa = _tn(ga_ref[...], dz)
        c1, s1 = cos_ref[...], sin_ref[...]
        c4 = jnp.concatenate([c1, c1, c1, c1], axis=1)
        s4 = jnp.concatenate([s1, s1, s1, s1], axis=1)
        dq_r, dk_r = dsq_ref[...], dsk_ref[...]
        dsq = dq_r * c4 - _rot_half(dq_r * s4)
        dsk = dk_r * c1 - _rot_half(dk_r * s1)
        dproj16 = jnp.concatenate(
            [dgq_ref[...], dgk_ref[...], dgv_ref[...], dgg_ref[...], dsq, dsk, dsv_ref[...], dga], axis=1).astype(BF16)
        dn = _nn(dproj16, win_ref[...])
        pw = _tn(dproj16, n_ref[...])

        @pl.when(first)
        def _():
            dwin_ref[...] = pw
            dwa2_ref[...] = pa

        @pl.when(jnp.logical_not(first))
        def _():
            dwin_ref[...] += pw
            dwa2_ref[...] += pa
        hn, rh = _rms(h_ref[...])
        _acc_add(dg_ref, first, _colsum(dn * hn))
        dh_ref[...] = dho_ref[...] + _rms_bwd(hn, rh, g_ref[...], dn)

    rs = lambda c: _row_spec(tm, c)
    return pl.pallas_call(
        body, name="mix_in_bwd", grid=(rows // tm,),
        in_specs=[rs(D_MODEL), rs(D_MODEL), rs(D_MODEL), VMEM_SPEC, VMEM_SPEC, VMEM_SPEC, rs(128), rs(128), rs(256), rs(128),
                  rs(256), rs(256), rs(512), rs(512), rs(512), rs(128), rs(128), rs(256)],
        out_specs=[rs(D_MODEL), pl.BlockSpec((P_END, D_MODEL), lambda i: (0, 0)), pl.BlockSpec((128, 256), lambda i: (0, 0)),
                   _acc_spec(D_MODEL), _acc_spec(256)],
        out_shape=[jax.ShapeDtypeStruct((rows, D_MODEL), F32), jax.ShapeDtypeStruct((P_END, D_MODEL), F32),
                   jax.ShapeDtypeStruct((128, 256), F32), jax.ShapeDtypeStruct((8, D_MODEL), F32),
                   jax.ShapeDtypeStruct((8, 256), F32)],
        compiler_params=_params(("arbitrary",)),
    )(dh_out, h, n16, g, win_p, wa2_p, cos, sin, loga, ga, dgq, dgk, dgv, dgg, dsq, dsk, dsv, dloga)


def _rope_tables(rows):
    pos = (jnp.arange(rows, dtype=jnp.int32) - PAD_ROWS).astype(F32)
    inv_freq = 1.0 / (ROPE_THETA ** (jnp.arange(0, SWA_HD, 2, dtype=F32) / SWA_HD))
    ang = pos[:, None] * inv_freq[None, :]
    ang = jnp.concatenate([ang, ang, ang, ang], axis=-1)
    return jnp.cos(ang), jnp.sin(ang)


def _local_step(h0, tgt, w):
    rows = h0.shape[0]
    cos, sin = _rope_tables(rows)
    h1, a1, b1, f1 = _ffn_fwd(h0, w["ffn1_pre"], w["wg1"], w["wu1"], w["wd1"], w["ffn1_post"])
    gq, gk, gv, gg, sq, sk, sv, ga, loga, bc, n2 = _mix_in(h1, w["mix_pre"], w["win"], w["wa2"], w["b_a"], cos, sin)
    o_g, cat_g, sp = _gla_fwd(gq, gk, gv, gg, bc, w["gla_norm"])
    o_s, cat_s, lse = _swa_fwd(sq, sk, sv, w["sinks"], w["swa_norm"])
    h2, m = _mix_out(h1, cat_g, cat_s, w["wout"], w["mix_post"])
    h3, a2, b2, f2, dy, loss = _ffn_fwd(h2, w["ffn2_pre"], w["wg2"], w["wu2"], w["wd2"], w["ffn2_post"], tgt)
    del h3
    g = {}
    dh2, da, db, df, n3, g["ffn2_pre"], g["ffn2_post"] = _ffn_bwd_act(
        dy, h2, a2, b2, f2, w["ffn2_pre"], w["ffn2_post"], w["wg2"], w["wu2"], w["wd2"], "ffn2_bwd_act")
    g["wg2"], g["wu2"], g["wd2"] = _ffn_bwd_weights(a2, b2, da, db, df, n3, "ffn2_bwd_w")
    dcg, dcs, g["wout"], g["mix_post"] = _mix_out_bwd(dh2, m, cat_g, cat_s, w["wout"], w["mix_post"])
    dsq, dsk, dsv, g["sinks"], g["swa_norm"] = _swa_bwd(dcs, o_s, sq, sk, sv, lse, w["sinks"], w["swa_norm"])
    dgq, dgk, dgv, dgg, dloga, g["gla_norm"] = _gla_bwd(dcg, o_g, gq, gk, gv, gg, bc, sp, w["gla_norm"])
    dh1, g["win"], g["wa2"], g["mix_pre"], g["b_a"] = _mix_in_bwd(
        dh2, h1, n2, w["mix_pre"], w["win"], w["wa2"], cos, sin, loga, ga, dgq, dgk, dgv, dgg, dsq, dsk, dsv, dloga)
    dh0, da, db, df, n1, g["ffn1_pre"], g["ffn1_post"] = _ffn_bwd_act(
        dh1, h0, a1, b1, f1, w["ffn1_pre"], w["ffn1_post"], w["wg1"], w["wu1"], w["wd1"], "ffn1_bwd_act")
    g["wg1"], g["wu1"], g["wd1"] = _ffn_bwd_weights(a1, b1, da, db, df, n1, "ffn1_bwd_w")
    return loss[0, 0], dh0, g


def _win_pad_rows(win_t):
    pad = jnp.zeros((P_END - P_GA - 16, win_t.shape[1]), win_t.dtype)
    return jnp.concatenate([win_t[0:1536], win_t[1552:2320], win_t[1536:1552], pad], axis=0)


def _win_unpad_rows(win_p):
    return jnp.concatenate([win_p[0:1536], win_p[P_GA:P_GA + 16], win_p[1536:P_GA]], axis=0)


def _place_on_mesh():
    return lax.axis_index("x"), lax.axis_index("y"), lax.axis_index("c")


def _dev_index(px, py, pc):
    return 4 * px + 2 * py + pc


def _other_devices(x, y, c):
    flip = lambda v, f: 1 - v if f else v
    return [(flip(x, fx), flip(y, fy), flip(c, fc)) for fx in (0, 1) for fy in (0, 1) for fc in (0, 1)][1:]


def _all_gather(shards):
    n = len(shards)

    def body(*refs):
        ins, outs = refs[:n], refs[n:2 * n]
        send_sems, recv_sems, local_sems = refs[2 * n:]
        x, y, c = _place_on_mesh()
        me, sibling = (x, y, c), (x, y, 1 - c)
        chips = [(1 - x, y), (x, 1 - y), (1 - x, 1 - y)]

        def rows(k, px, py, pc):
            r = ins[k].shape[0]
            return outs[k].at[pl.ds(pl.multiple_of(_dev_index(px, py, pc) * r, 8), r), :]

        def copy(k, slot, block, to, src=None):
            return pltpu.make_async_remote_copy(
                src_ref=rows(k, *block) if src is None else src, dst_ref=rows(k, *block),
                send_sem=send_sems.at[k, slot], recv_sem=recv_sems.at[k, slot], device_id=to, device_id_type=MESH)

        local = [pltpu.make_async_copy(ins[k], rows(k, *me), local_sems.at[k]) for k in range(n)]
        sends = []
        for k in range(n):
            local[k].start()
            sends.append(copy(k, 0, me, sibling, src=ins[k]))
            sends += [copy(k, 1 + j, me, (*chip, c), src=ins[k]) for j, chip in enumerate(chips)]
        for cp in sends:
            cp.start()
        for k in range(n):
            for j, chip in enumerate(chips):
                copy(k, 1 + j, (*chip, c), me).wait_recv()
                passed = copy(k, 4 + j, (*chip, c), sibling)
                passed.start()
                sends.append(passed)
        for k in range(n):
            copy(k, 0, sibling, me).wait_recv()
            for j, chip in enumerate(chips):
                copy(k, 4 + j, (*chip, 1 - c), me).wait_recv()
        for cp in sends:
            cp.wait_send()
        for cp in local:
            cp.wait()

    return pl.pallas_call(
        body, name="all_gather_weights",
        in_specs=[ANY_SPEC] * n, out_specs=[ANY_SPEC] * n,
        out_shape=[jax.ShapeDtypeStruct((N_DEV * s.shape[0], s.shape[1]), s.dtype) for s in shards],
        scratch_shapes=[pltpu.SemaphoreType.DMA((n, 7)), pltpu.SemaphoreType.DMA((n, 7)), pltpu.SemaphoreType.DMA((n,))],
    )(*shards)


def _scatter_partials(parts):
    n = len(parts)

    def body(*refs):
        ins, outs = refs[:n], refs[n:2 * n]
        send_sems, recv_sems, local_sems = refs[2 * n:]
        x, y, c = _place_on_mesh()
        me = _dev_index(x, y, c)
        peers = _other_devices(x, y, c)

        def rows(ref, k, d):
            r = ins[k].shape[0] // N_DEV
            return ref.at[pl.ds(pl.multiple_of(d * r, 8), r), :]

        def copy(k, f, peer):
            return pltpu.make_async_remote_copy(
                src_ref=rows(ins[k], k, _dev_index(*peer)), dst_ref=rows(outs[k], k, me),
                send_sem=send_sems.at[k, f], recv_sem=recv_sems.at[k, f], device_id=peer, device_id_type=MESH)

        def arrival(k, f, peer):
            return pltpu.make_async_remote_copy(
                src_ref=rows(ins[k], k, me), dst_ref=rows(outs[k], k, _dev_index(*peer)),
                send_sem=send_sems.at[k, f], recv_sem=recv_sems.at[k, f], device_id=peer, device_id_type=MESH)

        local = [pltpu.make_async_copy(rows(ins[k], k, me), rows(outs[k], k, me), local_sems.at[k]) for k in range(n)]
        sends = [copy(k, f, peer) for k in range(n) for f, peer in enumerate(peers)]
        for cp in local + sends:
            cp.start()
        for k in range(n):
            for f, peer in enumerate(peers):
                arrival(k, f, peer).wait_recv()
        for cp in sends:
            cp.wait_send()
        for cp in local:
            cp.wait()

    return pl.pallas_call(
        body, name="scatter_grad_partials",
        in_specs=[ANY_SPEC] * n, out_specs=[ANY_SPEC] * n,
        out_shape=[jax.ShapeDtypeStruct(p.shape, p.dtype) for p in parts],
        scratch_shapes=[pltpu.SemaphoreType.DMA((n, 7)), pltpu.SemaphoreType.DMA((n, 7)), pltpu.SemaphoreType.DMA((n,))],
    )(*parts)


def _sum_partials(parts):
    n = len(parts)

    def body(*refs):
        ins, outs = refs[:n], refs[n:]
        first = pl.program_id(0) == 0
        for i_ref, o_ref in zip(ins, outs):
            v = i_ref[...].astype(F32)

            @pl.when(first)
            def _():
                o_ref[...] = v

            @pl.when(jnp.logical_not(first))
            def _():
                o_ref[...] += v

    shapes = [(p.shape[0] // N_DEV, p.shape[1]) for p in parts]
    return pl.pallas_call(
        body, name="sum_grad_partials", grid=(N_DEV,),
        in_specs=[pl.BlockSpec(s, lambda j: (j, 0)) for s in shapes],
        out_specs=[pl.BlockSpec(s, lambda j: (0, 0)) for s in shapes],
        out_shape=[jax.ShapeDtypeStruct(s, F32) for s in shapes],
        compiler_params=_params(("arbitrary",)),
    )(*parts)


def _all_reduce_small(slab):
    rows, cols = slab.shape

    def body(x_ref, o_ref, gathered, send_sems, recv_sems):
        x, y, c = _place_on_mesh()
        me = _dev_index(x, y, c)
        peers = _other_devices(x, y, c)

        def copy(f, peer):
            return pltpu.make_async_remote_copy(
                src_ref=x_ref, dst_ref=gathered.at[me], send_sem=send_sems.at[f], recv_sem=recv_sems.at[f],
                device_id=peer, device_id_type=MESH)

        def arrival(f, peer):
            return pltpu.make_async_remote_copy(
                src_ref=x_ref, dst_ref=gathered.at[_dev_index(*peer)], send_sem=send_sems.at[f], recv_sem=recv_sems.at[f],
                device_id=peer, device_id_type=MESH)

        sends = [copy(f, peer) for f, peer in enumerate(peers)]
        for cp in sends:
            cp.start()
        gathered[me] = x_ref[...]
        for f, peer in enumerate(peers):
            arrival(f, peer).wait_recv()
        for cp in sends:
            cp.wait_send()
        total = gathered[0]
        for d in range(1, N_DEV):
            total = total + gathered[d]
        o_ref[...] = total

    return pl.pallas_call(
        body, name="all_reduce_small",
        in_specs=[VMEM_SPEC], out_specs=VMEM_SPEC, out_shape=jax.ShapeDtypeStruct((rows, cols), F32),
        scratch_shapes=[pltpu.VMEM((N_DEV, rows, cols), F32), pltpu.SemaphoreType.DMA((7,)), pltpu.SemaphoreType.DMA((7,))],
    )(slab)


def _adamw(ws, gs, ms, vs, name):
    n = len(ws)
    c1 = 1.0 / (1.0 - ADAM_B1 ** ADAM_STEP)
    c2 = 1.0 / (1.0 - ADAM_B2 ** ADAM_STEP)

    def body(*refs):
        w_r, g_r, m_r, v_r = refs[:n], refs[n:2 * n], refs[2 * n:3 * n], refs[3 * n:4 * n]
        d_o, m_o, v_o = refs[4 * n:5 * n], refs[5 * n:6 * n], refs[6 * n:7 * n]
        for k in range(n):
            g = g_r[k][...]
            m = ADAM_B1 * m_r[k][...] + (1.0 - ADAM_B1) * g
            v = ADAM_B2 * v_r[k][...] + (1.0 - ADAM_B2) * (g * g)
            m_o[k][...] = m
            v_o[k][...] = v
            d_o[k][...] = -ADAM_LR * ((m * c1) / (jnp.sqrt(v * c2) + ADAM_EPS) + ADAM_WD * w_r[k][...])

    shapes = [jax.ShapeDtypeStruct(w.shape, F32) for w in ws]
    outs = pl.pallas_call(
        body, name=name, in_specs=[VMEM_SPEC] * (4 * n), out_specs=[VMEM_SPEC] * (3 * n), out_shape=shapes * 3,
        compiler_params=pltpu.CompilerParams(vmem_limit_bytes=56 << 20),
    )(*ws, *gs, *ms, *vs)
    return outs[:n], outs[n:2 * n], outs[2 * n:]


WEIGHT_NAMES = ("meta_tokens", "ffn1_pre_norm", "ffn1_w_gate", "ffn1_w_up", "ffn1_w_down", "ffn1_post_norm", "mix_pre_norm",
                "w_in", "gla_w_a2", "gla_b_a", "gla_out_norm", "swa_sinks", "swa_out_norm", "w_out", "mix_post_norm",
                "ffn2_pre_norm", "ffn2_w_gate", "ffn2_w_up", "ffn2_w_down", "ffn2_post_norm")
WIN_SHARD = D_IN // N_DEV
WIN_SHARD_PAD = 304
SLAB_VECTORS = ("ffn1_pre", "ffn1_post", "mix_pre", "mix_post", "ffn2_pre", "ffn2_post")
SLAB_ROWS = 32


def kernel(x, meta_tokens, ffn1_pre_norm, ffn1_w_gate, ffn1_w_up, ffn1_w_down, ffn1_post_norm, mix_pre_norm, w_in, gla_w_a2, gla_b_a, gla_out_norm, swa_sinks, swa_out_norm, w_out, mix_post_norm, ffn2_pre_norm, ffn2_w_gate, ffn2_w_up, ffn2_w_down, ffn2_post_norm, loss_target, m_meta_tokens, m_ffn1_pre_norm, m_ffn1_w_gate, m_ffn1_w_up, m_ffn1_w_down, m_ffn1_post_norm, m_mix_pre_norm, m_w_in, m_gla_w_a2, m_gla_b_a, m_gla_out_norm, m_swa_sinks, m_swa_out_norm, m_w_out, m_mix_post_norm, m_ffn2_pre_norm, m_ffn2_w_gate, m_ffn2_w_up, m_ffn2_w_down, m_ffn2_post_norm, v_meta_tokens, v_ffn1_pre_norm, v_ffn1_w_gate, v_ffn1_w_up, v_ffn1_w_down, v_ffn1_post_norm, v_mix_pre_norm, v_w_in, v_gla_w_a2, v_gla_b_a, v_gla_out_norm, v_swa_sinks, v_swa_out_norm, v_w_out, v_mix_post_norm, v_ffn2_pre_norm, v_ffn2_w_gate, v_ffn2_w_up, v_ffn2_w_down, v_ffn2_post_norm):
    given = dict(locals())
    W = {n: given[n] for n in WEIGHT_NAMES}
    M = {n: given["m_" + n] for n in WEIGHT_NAMES}
    V = {n: given["v_" + n] for n in WEIGHT_NAMES}
    dev = _dev_index(*_place_on_mesh())

    def t16(w):
        return w[0].T.astype(BF16)
    win_shard = jnp.pad(t16(W["w_in"]), ((0, WIN_SHARD_PAD - WIN_SHARD), (0, 0)))
    small = jnp.concatenate([W["meta_tokens"], jnp.pad(W["gla_w_a2"][0], ((0, 0), (0, 96)))], axis=0)
    shards = [t16(W["ffn1_w_gate"]), t16(W["ffn1_w_up"]), W["ffn1_w_down"][0].astype(BF16), win_shard,
              W["w_out"][0].astype(BF16), t16(W["ffn2_w_gate"]), t16(W["ffn2_w_up"]), W["ffn2_w_down"][0].astype(BF16), small]
    wg1, wu1, wd1, win_g, wout, wg2, wu2, wd2, small_g = _all_gather(shards)
    win_t = win_g.reshape(N_DEV, WIN_SHARD_PAD, D_MODEL)[:, :WIN_SHARD].reshape(D_IN, D_MODEL)
    small_g = small_g.reshape(N_DEV, 32, 128)
    meta_full = small_g[:, :N_META].transpose(1, 0, 2).reshape(N_META, D_MODEL)
    wa2_full = small_g[:, N_META:, :32].transpose(1, 0, 2).reshape(16, 256)
    w = dict(
        ffn1_pre=W["ffn1_pre_norm"], ffn1_post=W["ffn1_post_norm"], mix_pre=W["mix_pre_norm"], mix_post=W["mix_post_norm"],
        ffn2_pre=W["ffn2_pre_norm"], ffn2_post=W["ffn2_post_norm"], b_a=W["gla_b_a"], gla_norm=W["gla_out_norm"],
        sinks=W["swa_sinks"], swa_norm=W["swa_out_norm"], wg1=wg1, wu1=wu1, wd1=wd1, wg2=wg2, wu2=wu2, wd2=wd2,
        win=_win_pad_rows(win_t), wout=wout, wa2=jnp.pad(wa2_full, ((0, 112), (0, 0))))

    front = jnp.zeros((PAD_ROWS, D_MODEL), F32)
    h0 = jnp.concatenate([front, meta_full, x[0]], axis=0)
    tgt = jnp.concatenate([jnp.zeros((BLK, D_MODEL), F32), loss_target[0]], axis=0)
    loss, dh0, g = _local_step(h0, tgt, w)
    loss = lax.psum(loss, ("x", "y", "c"))
    grad_x = dh0[BLK:][None]

    dwin = _win_unpad_rows(g["win"]).reshape(N_DEV, WIN_SHARD, D_MODEL)
    dwin = jnp.pad(dwin, ((0, 0), (0, WIN_SHARD_PAD - WIN_SHARD), (0, 0))).reshape(N_DEV * WIN_SHARD_PAD, D_MODEL)
    parts = [p.astype(BF16) for p in (g["wg1"], g["wu1"], g["wd1"], dwin, g["wout"], g["wg2"], g["wu2"], g["wd2"])]
    sums = _sum_partials(_scatter_partials(parts))
    big = dict(ffn1_w_gate=sums[0].T[None], ffn1_w_up=sums[1].T[None], ffn1_w_down=sums[2][None],
               w_in=sums[3][:WIN_SHARD].T[None], w_out=sums[4][None],
               ffn2_w_gate=sums[5].T[None], ffn2_w_up=sums[6].T[None], ffn2_w_down=sums[7][None])

    packed = jnp.concatenate([g["b_a"][0:1], g["gla_norm"][0:1], g["sinks"][0:1], g["swa_norm"][0:1]], axis=1)
    slab = jnp.concatenate([g[k][0:1] for k in SLAB_VECTORS] + [packed, jnp.zeros((1, D_MODEL), F32),
                           g["wa2"][:16].reshape(4, D_MODEL), jnp.zeros((4, D_MODEL), F32), dh0[PAD_ROWS:BLK]], axis=0)
    tot = _all_reduce_small(slab)
    small_grads = dict(
        ffn1_pre_norm=tot[0:1], ffn1_post_norm=tot[1:2], mix_pre_norm=tot[2:3], mix_post_norm=tot[3:4],
        ffn2_pre_norm=tot[4:5], ffn2_post_norm=tot[5:6], gla_b_a=tot[6:7, 0:256], gla_out_norm=tot[6:7, 256:384],
        swa_sinks=tot[6:7, 384:392], swa_out_norm=tot[6:7, 512:1024],
        gla_w_a2=lax.dynamic_slice_in_dim(tot[8:12].reshape(16, 256), dev * 32, 32, axis=1)[None],
        meta_tokens=lax.dynamic_slice_in_dim(tot[16:32], dev * 128, 128, axis=1))
    grads = {**big, **small_grads}

    delta, new_m, new_v = {}, {}, {}
    for n in big:
        d_, m_, v_ = _adamw([W[n][0]], [grads[n][0]], [M[n][0]], [V[n][0]], "adamw_" + n)
        delta[n], new_m[n], new_v[n] = d_[0][None], m_[0][None], v_[0][None]
    names = [n for n in WEIGHT_NAMES if n not in big]
    two_d = lambda a: a.reshape(-1, a.shape[-1])
    d_, m_, v_ = _adamw([two_d(W[n]) for n in names], [two_d(grads[n]) for n in names],
                        [two_d(M[n]) for n in names], [two_d(V[n]) for n in names], "adamw_small")
    for k, n in enumerate(names):
        delta[n], new_m[n], new_v[n] = d_[k].reshape(W[n].shape), m_[k].reshape(W[n].shape), v_[k].reshape(W[n].shape)
    return (loss, grad_x, *[grads[n] for n in WEIGHT_NAMES], *[delta[n] for n in WEIGHT_NAMES],
            *[new_m[n] for n in WEIGHT_NAMES], *[new_v[n] for n in WEIGHT_NAMES])
```

```python
import functools

import jax
import jax.numpy as jnp
from jax import lax
from jax.experimental import pallas as pl
from jax.experimental.pallas import tpu as pltpu

F32, BF16 = jnp.float32, jnp.bfloat16

D_MODEL = 1024
D_FF = 2816
N_META = 16
BLK = 128
PAD_ROWS = BLK - N_META
GLA_DK = 64
SWA_HD = 64
SWA_HEADS = 8
GLA_TAU = 16.0
NORM_EPS = 1e-6
NEG_INF = -1e30
ROPE_THETA = 10000.0
P_GQ, P_GK, P_GV, P_GG, P_SQ, P_SK, P_SV, P_GA, P_END = 0, 256, 512, 1024, 1536, 2048, 2176, 2304, 2432
D_IN = 2320
IN_SPLITS = (256, 256, 512, 512, 16, 512, 128, 128)
FF_TILE = 2816
WGRAD_TILE_MAX = 2432
N_DEV = 8
MESH = pl.DeviceIdType.MESH

ADAM_LR, ADAM_B1, ADAM_B2, ADAM_EPS, ADAM_WD, ADAM_STEP = 0.001, 0.9, 0.999, 1e-08, 0.01, 10

V7X_VMEM_BYTES = 64 << 20
VMEM_SPEC = pl.BlockSpec(memory_space=pltpu.VMEM)
SMEM_SPEC = pl.BlockSpec(memory_space=pltpu.SMEM)
ANY_SPEC = pl.BlockSpec(memory_space=pl.ANY)


def _params(semantics, vmem_mb=56):
    return pltpu.CompilerParams(dimension_semantics=semantics, vmem_limit_bytes=vmem_mb << 20)


def _row_tile(rows):
    return 320 if rows % 320 == 0 else BLK


def _nn(a, b):
    return lax.dot_general(a, b, (((1,), (0,)), ((), ())), preferred_element_type=F32)


def _nt(a, b):
    return lax.dot_general(a, b, (((1,), (1,)), ((), ())), preferred_element_type=F32)


def _tn(a, b):
    return lax.dot_general(a, b, (((0,), (0,)), ((), ())), preferred_element_type=F32)


def _rms(x):
    r = lax.rsqrt(jnp.mean(x * x, axis=-1, keepdims=True) + NORM_EPS)
    return x * r, r


def _rms_bwd(xn, r, w, dy):
    g = dy * w
    return r * (g - xn * jnp.mean(g * xn, axis=-1, keepdims=True))


def _sigmoid(x):
    return 1.0 / (1.0 + jnp.exp(-x))


def _colsum(x):
    return jnp.sum(x, axis=0, keepdims=True)


def _split_bf16(x):
    hi = x.astype(BF16)
    lo = (x - hi.astype(F32)).astype(BF16)
    return hi, lo


def _tri(lower):
    r = lax.broadcasted_iota(jnp.int32, (BLK, BLK), 0)
    c = lax.broadcasted_iota(jnp.int32, (BLK, BLK), 1)
    return (r >= c) if lower else (c >= r)


def _half_mask(width, half):
    lane = lax.broadcasted_iota(jnp.int32, (1, width), 1)
    return ((lane % 128) < 64) if half == 0 else ((lane % 128) >= 64)


def _rot_half(x):
    w = x.shape[-1]
    lane = lax.broadcasted_iota(jnp.int32, (1, w), 1)
    return jnp.where((lane % SWA_HD) < SWA_HD // 2, -pltpu.roll(x, w - SWA_HD // 2, 1), pltpu.roll(x, SWA_HD // 2, 1))


def _row_spec(tm, cols):
    return pl.BlockSpec((tm, cols), lambda i: (i, 0))


def _acc_spec(cols):
    return pl.BlockSpec((8, cols), lambda i: (0, 0))


def _acc_add(ref, first, value):
    @pl.when(first)
    def _():
        ref[...] = jnp.zeros_like(ref)
    ref[0:1, :] += value


def _ffn_fwd(h, gpre, wg_t, wu_t, wd, gpost, tgt=None):
    rows = h.shape[0]
    tm = _row_tile(rows)
    nf = D_FF // FF_TILE
    with_loss = tgt is not None

    def body(*refs):
        if with_loss:
            (h_ref, gpre_ref, wg_ref, wu_ref, wd_ref, gpost_ref, t_ref,
             ho_ref, a_ref, b_ref, f_ref, dy_ref, loss_ref, acc) = refs
        else:
            (h_ref, gpre_ref, wg_ref, wu_ref, wd_ref, gpost_ref, ho_ref, a_ref, b_ref, f_ref, acc) = refs
        i = pl.program_id(0)
        h_in = h_ref[...]
        hn, _ = _rms(h_in)
        n16 = (hn * gpre_ref[...]).astype(BF16)
        for j in range(nf):
            cols = slice(j * FF_TILE, (j + 1) * FF_TILE)
            a = _nt(n16, wg_ref[cols, :])
            b = _nt(n16, wu_ref[cols, :])
            a_ref[:, cols] = a.astype(BF16)
            b_ref[:, cols] = b.astype(BF16)
            s16 = (a * _sigmoid(a) * b).astype(BF16)
            part = _nn(s16, wd_ref[cols, :])
            if j == 0:
                acc[...] = part
            else:
                acc[...] += part
        f = acc[...]
        f_ref[...] = f
        fn, _ = _rms(f)
        y = h_in + 0.5 * (fn * gpost_ref[...])
        ho_ref[...] = y
        if with_loss:
            row = i * tm + lax.broadcasted_iota(jnp.int32, (tm, 1), 0)
            err = jnp.where(row >= BLK, y - t_ref[...], 0.0)
            dy_ref[...] = err * (1.0 / D_MODEL)
            part = 0.5 * jnp.sum(jnp.sum(err * err, axis=-1, keepdims=True) * (1.0 / D_MODEL), axis=0, keepdims=True)

            @pl.when(i == 0)
            def _():
                loss_ref[...] = jnp.zeros_like(loss_ref)
            loss_ref[...] += part

    row_f32 = _row_spec(tm, D_MODEL)
    in_specs = [row_f32, VMEM_SPEC, VMEM_SPEC, VMEM_SPEC, VMEM_SPEC, VMEM_SPEC]
    out_specs = [row_f32, _row_spec(tm, D_FF), _row_spec(tm, D_FF), row_f32]
    out_shape = [jax.ShapeDtypeStruct((rows, D_MODEL), F32), jax.ShapeDtypeStruct((rows, D_FF), BF16),
                 jax.ShapeDtypeStruct((rows, D_FF), BF16), jax.ShapeDtypeStruct((rows, D_MODEL), F32)]
    args = [h, gpre, wg_t, wu_t, wd, gpost]
    if with_loss:
        in_specs.append(row_f32)
        args.append(tgt)
        out_specs += [row_f32, pl.BlockSpec((8, 128), lambda i: (0, 0))]
        out_shape += [jax.ShapeDtypeStruct((rows, D_MODEL), F32), jax.ShapeDtypeStruct((8, 128), F32)]
    return pl.pallas_call(
        body, name="ffn_fwd_loss" if with_loss else "ffn_fwd", grid=(rows // tm,),
        in_specs=in_specs, out_specs=out_specs, out_shape=out_shape,
        scratch_shapes=[pltpu.VMEM((tm, D_MODEL), F32)],
        compiler_params=_params(("arbitrary",)),
    )(*args)


def _ffn_bwd_act(dh_out, h, a, b, f, gpre, gpost, wg_t, wu_t, wd, name):
    rows = h.shape[0]
    tm = _row_tile(rows)
    nf = D_FF // FF_TILE

    def body(dho_ref, h_ref, a_ref, b_ref, f_ref, gpre_ref, gpost_ref, wg_ref, wu_ref, wd_ref,
             dh_ref, da_ref, db_ref, df_ref, n_ref, dgpre_ref, dgpost_ref, acc):
        first = pl.program_id(0) == 0
        dho = dho_ref[...]
        drr = 0.5 * dho
        fn, rf = _rms(f_ref[...])
        _acc_add(dgpost_ref, first, _colsum(drr * fn))
        df16 = _rms_bwd(fn, rf, gpost_ref[...], drr).astype(BF16)
        df_ref[...] = df16
        hn, rh = _rms(h_ref[...])
        n_ref[...] = (hn * gpre_ref[...]).astype(BF16)
        for j in range(nf):
            cols = slice(j * FF_TILE, (j + 1) * FF_TILE)
            ds = _nt(df16, wd_ref[cols, :])
            av = a_ref[:, cols].astype(F32)
            bv = b_ref[:, cols].astype(F32)
            sg = _sigmoid(av)
            db16 = (ds * (av * sg)).astype(BF16)
            da16 = (ds * bv * (sg * (1.0 + av * (1.0 - sg)))).astype(BF16)
            da_ref[:, cols] = da16
            db_ref[:, cols] = db16
            part = _nn(da16, wg_ref[cols, :]) + _nn(db16, wu_ref[cols, :])
            if j == 0:
                acc[...] = part
            else:
                acc[...] += part
        dn = acc[...]
        _acc_add(dgpre_ref, first, _colsum(dn * hn))
        dh_ref[...] = dho + _rms_bwd(hn, rh, gpre_ref[...], dn)

    row_f32 = _row_spec(tm, D_MODEL)
    row_ff = _row_spec(tm, D_FF)
    return pl.pallas_call(
        body, name=name, grid=(rows // tm,),
        in_specs=[row_f32, row_f32, row_ff, row_ff, row_f32, VMEM_SPEC, VMEM_SPEC, VMEM_SPEC, VMEM_SPEC, VMEM_SPEC],
        out_specs=[row_f32, row_ff, row_ff, row_f32, row_f32, _acc_spec(D_MODEL), _acc_spec(D_MODEL)],
        out_shape=[jax.ShapeDtypeStruct((rows, D_MODEL), F32), jax.ShapeDtypeStruct((rows, D_FF), BF16),
                   jax.ShapeDtypeStruct((rows, D_FF), BF16), jax.ShapeDtypeStruct((rows, D_MODEL), BF16),
                   jax.ShapeDtypeStruct((rows, D_MODEL), BF16), jax.ShapeDtypeStruct((8, D_MODEL), F32),
                   jax.ShapeDtypeStruct((8, D_MODEL), F32)],
        scratch_shapes=[pltpu.VMEM((tm, D_MODEL), F32)],
        compiler_params=_params(("arbitrary",)),
    )(dh_out, h, a, b, f, gpre, gpost, wg_t, wu_t, wd)


def _wgrad(lhs, rhs, name, gate=None):
    rows, width = lhs.shape
    tm = 1664 if rows % 1664 == 0 else BLK
    tf = width // 2 if width > WGRAD_TILE_MAX else width
    nr = rows // tm
    gated = gate is not None

    def body(*refs):
        if gated:
            g_ref, l_ref, r_ref, o_ref, acc = refs
            gv = g_ref[...].astype(F32)
            lv = (gv * _sigmoid(gv) * l_ref[...].astype(F32)).astype(BF16)
        else:
            l_ref, r_ref, o_ref, acc = refs
            lv = l_ref[...]
        i = pl.program_id(1)
        part = _tn(lv, r_ref[...])

        @pl.when(i == 0)
        def _():
            acc[...] = part

        @pl.when(i > 0)
        def _():
            acc[...] += part

        @pl.when(i == nr - 1)
        def _():
            o_ref[...] = acc[...].astype(BF16)

    l_spec = pl.BlockSpec((tm, tf), lambda j, i: (i, j))
    r_spec = pl.BlockSpec((tm, D_MODEL), lambda j, i: (i, 0))
    return pl.pallas_call(
        body, name=name, grid=(width // tf, nr),
        in_specs=([l_spec] if gated else []) + [l_spec, r_spec],
        out_specs=pl.BlockSpec((tf, D_MODEL), lambda j, i: (j, 0)),
        out_shape=jax.ShapeDtypeStruct((width, D_MODEL), BF16),
        scratch_shapes=[pltpu.VMEM((tf, D_MODEL), F32)],
        compiler_params=_params(("arbitrary", "arbitrary")),
    )(*([gate] if gated else []), lhs, rhs)


def _chunk_cumsum(x, lower):
    tri = jnp.where(_tri(lower), 1.0, 0.0).astype(BF16)
    hi, lo = _split_bf16(x)
    return _nn(tri, hi) + _nn(tri, lo)


def _mix_in(h, g, win_p, wa2_p, b_a, cos, sin):
    rows = h.shape[0]
    tm = _row_tile(rows)
    tm = tm if tm % BLK == 0 else BLK

    def body(h_ref, g_ref, win_ref, wa2_ref, ba_ref, cos_ref, sin_ref,
             gq_ref, gk_ref, gv_ref, gg_ref, sq_ref, sk_ref, sv_ref, ga_ref, loga_ref, bc_ref, n_ref):
        hn, _ = _rms(h_ref[...])
        n16 = (hn * g_ref[...]).astype(BF16)
        n_ref[...] = n16
        proj = _nt(n16, win_ref[...])
        gq_ref[...] = proj[:, P_GQ:P_GK]
        gk_ref[...] = proj[:, P_GK:P_GV]
        gv_ref[...] = proj[:, P_GV:P_GG].astype(BF16)
        gg_ref[...] = proj[:, P_GG:P_SQ]
        c1, s1 = cos_ref[...], sin_ref[...]
        c4 = jnp.concatenate([c1, c1, c1, c1], axis=1)
        s4 = jnp.concatenate([s1, s1, s1, s1], axis=1)
        sq = proj[:, P_SQ:P_SK]
        sk = proj[:, P_SK:P_SV]
        sq_ref[...] = (sq * c4 + _rot_half(sq) * s4).astype(BF16)
        sk_ref[...] = (sk * c1 + _rot_half(sk) * s1).astype(BF16)
        sv_ref[...] = proj[:, P_SV:P_GA].astype(BF16)
        ga = proj[:, P_GA:P_END]
        ga_ref[...] = ga
        z = _nn(ga, wa2_ref[...]) + ba_ref[...]
        loga = (jnp.minimum(z, 0.0) - jnp.log(1.0 + jnp.exp(-jnp.abs(z)))) * (1.0 / GLA_TAU)
        loga_ref[...] = loga
        for c in range(tm // BLK):
            rs = slice(c * BLK, (c + 1) * BLK)
            bc_ref[rs, :] = _chunk_cumsum(loga[rs, :], True)

    f32 = lambda c: jax.ShapeDtypeStruct((rows, c), F32)
    b16 = lambda c: jax.ShapeDtypeStruct((rows, c), BF16)
    rs = lambda c: _row_spec(tm, c)
    return pl.pallas_call(
        body, name="mix_in", grid=(rows // tm,),
        in_specs=[rs(D_MODEL), VMEM_SPEC, VMEM_SPEC, VMEM_SPEC, VMEM_SPEC, rs(128), rs(128)],
        out_specs=[rs(256), rs(256), rs(512), rs(512), rs(512), rs(128), rs(128), rs(128), rs(256), rs(256), rs(D_MODEL)],
        out_shape=[f32(256), f32(256), b16(512), f32(512), b16(512), b16(128), b16(128), f32(128), f32(256), f32(256),
                   b16(D_MODEL)],
        compiler_params=_params(("arbitrary",)),
    )(h, g, win_p, wa2_p, b_a, cos, sin)


def _gla_factors(q, k, bc):
    bm = bc[BLK // 2 - 1:BLK // 2, :]
    bl = bc[BLK - 1:BLK, :]
    e_q, e_k, e_qe, e_kd = jnp.exp(bc - bm), jnp.exp(bm - bc), jnp.exp(bc), jnp.exp(bl - bc)
    return (q * e_q, k * e_k, q * e_qe, k * e_kd), (e_q, e_k, e_qe, e_kd), jnp.exp(bl)


def _gla_fwd(gq, gk, gv, gg, bc, wgn):
    rows = gq.shape[0]
    nc = rows // BLK
    scale = GLA_DK ** -0.5

    def body(q_ref, k_ref, v_ref, gg_ref, bc_ref, wgn_ref, o_ref, cat_ref, sp_ref, st):
        @pl.when(pl.program_id(0) == 0)
        def _():
            st[...] = jnp.zeros_like(st)
        low = _tri(True)
        wgn_v = wgn_ref[...]
        for p in range(2):
            sl = slice(128 * p, 128 * p + 128)
            (qt, kt, qe, kd), _, ebl = _gla_factors(q_ref[:, sl] * scale, k_ref[:, sl], bc_ref[:, sl])
            s_prev = st[p]
            sp_ref[0, p] = s_prev
            s16 = s_prev.astype(BF16)
            qt16 = qt.astype(BF16)
            s_new = s_prev * ebl
            for hh in range(2):
                hs = slice(128 * (2 * p + hh), 128 * (2 * p + hh) + 128)
                lm = _half_mask(128, hh)
                vh = v_ref[:, hs]
                pm = jnp.where(low, _nt(qt16, jnp.where(lm, kt, 0.0).astype(BF16)), 0.0)
                o = _nn(pm.astype(BF16), vh) + _nt(jnp.where(lm, qe, 0.0).astype(BF16), s16)
                s_new = s_new + _tn(vh, jnp.where(lm, kd, 0.0).astype(BF16))
                o_ref[:, hs] = o
                on, _ = _rms(o)
                gate = gg_ref[:, hs]
                cat_ref[:, hs] = (on * wgn_v * (gate * _sigmoid(gate))).astype(BF16)
            st[p] = s_new

    rs = lambda c: _row_spec(BLK, c)
    return pl.pallas_call(
        body, name="gla_fwd", grid=(nc,),
        in_specs=[rs(256), rs(256), rs(512), rs(512), rs(256), VMEM_SPEC],
        out_specs=[rs(512), rs(512), pl.BlockSpec((1, 2, 128, 128), lambda i: (i, 0, 0, 0))],
        out_shape=[jax.ShapeDtypeStruct((rows, 512), F32), jax.ShapeDtypeStruct((rows, 512), BF16),
                   jax.ShapeDtypeStruct((nc, 2, 128, 128), F32)],
        scratch_shapes=[pltpu.VMEM((2, 128, 128), F32)],
        compiler_params=_params(("arbitrary",)),
    )(gq, gk, gv, gg, bc, wgn)


def _gla_bwd(dcat, o_all, gq, gk, gv, gg, bc, sp, wgn):
    rows = gq.shape[0]
    nc = rows // BLK
    scale = GLA_DK ** -0.5

    def body(dc_ref, o_ref, q_ref, k_ref, v_ref, gg_ref, bc_ref, sp_ref, wgn_ref,
             dq_ref, dk_ref, dv_ref, dgg_ref, dla_ref, dwgn_ref, dst):
        first = pl.program_id(0) == 0

        @pl.when(first)
        def _():
            dst[...] = jnp.zeros_like(dst)
        low, upp = _tri(True), _tri(False)
        last_row = lax.broadcasted_iota(jnp.int32, (BLK, 1), 0) == BLK - 1
        wgn_v = wgn_ref[...]
        dwgn = jnp.zeros((1, 128), F32)
        for p in range(2):
            sl = slice(128 * p, 128 * p + 128)
            (qt, kt, qe, kd), (e_q, e_k, e_qe, e_kd), ebl = _gla_factors(
                q_ref[:, sl] * scale, k_ref[:, sl], bc_ref[:, sl])
            s_prev = sp_ref[0, p]
            s16 = s_prev.astype(BF16)
            ds_next = dst[p]
            ds16 = ds_next.astype(BF16)
            qt16 = qt.astype(BF16)
            ds_new = ds_next * ebl
            dqt = jnp.zeros((BLK, 128), F32)
            dkt = jnp.zeros((BLK, 128), F32)
            dqe = jnp.zeros((BLK, 128), F32)
            dkd = jnp.zeros((BLK, 128), F32)
            for hh in range(2):
                hs = slice(128 * (2 * p + hh), 128 * (2 * p + hh) + 128)
                lm = _half_mask(128, hh)
                on, ro = _rms(o_ref[:, hs])
                gate = gg_ref[:, hs]
                sg = _sigmoid(gate)
                si = gate * sg
                dog = dc_ref[:, hs]
                dwgn = dwgn + _colsum(dog * si * on)
                dgg_ref[:, hs] = dog * (on * wgn_v) * (sg * (1.0 + gate * (1.0 - sg)))
                do16 = _rms_bwd(on, ro, wgn_v, dog * si).astype(BF16)
                vh = v_ref[:, hs]
                ktm16 = jnp.where(lm, kt, 0.0).astype(BF16)
                qtm16 = jnp.where(lm, qt, 0.0).astype(BF16)
                qem16 = jnp.where(lm, qe, 0.0).astype(BF16)
                kdm16 = jnp.where(lm, kd, 0.0).astype(BF16)
                p_t = jnp.where(upp, _nt(ktm16, qt16), 0.0)
                dp_t = jnp.where(upp, _nt(vh, do16), 0.0)
                dp = jnp.where(low, _nt(do16, vh), 0.0)
                dv_ref[:, hs] = _nn(p_t.astype(BF16), do16) + _nt(kdm16, ds16)
                dqt = dqt + _nn(dp.astype(BF16), ktm16)
                dkt = dkt + _nn(dp_t.astype(BF16), qtm16)
                dqe = dqe + jnp.where(lm, _nn(do16, s16), 0.0)
                dkd = dkd + jnp.where(lm, _nn(vh, ds16), 0.0)
                ds_new = ds_new + _tn(do16, qem16)
            debl = _colsum(ds_next * s_prev)
            dq_ref[:, sl] = (dqt * e_q + dqe * e_qe) * scale
            dk_ref[:, sl] = dkt * e_k + dkd * e_kd
            dkd_kd = dkd * kd
            db = dqt * qt - dkt * kt + dqe * qe - dkd_kd
            db = jnp.where(last_row, db + (_colsum(dkd_kd) + debl * ebl), db)
            dla_ref[:, sl] = _chunk_cumsum(db, False)
            dst[p] = ds_new
        _acc_add(dwgn_ref, first, dwgn)

    rev = lambda c: pl.BlockSpec((BLK, c), lambda i: (nc - 1 - i, 0))
    f32 = lambda c: jax.ShapeDtypeStruct((rows, c), F32)
    return pl.pallas_call(
        body, name="gla_bwd", grid=(nc,),
        in_specs=[rev(512), rev(512), rev(256), rev(256), rev(512), rev(512), rev(256),
                  pl.BlockSpec((1, 2, 128, 128), lambda i: (nc - 1 - i, 0, 0, 0)), VMEM_SPEC],
        out_specs=[rev(256), rev(256), rev(512), rev(512), rev(256), _acc_spec(128)],
        out_shape=[f32(256), f32(256), f32(512), f32(512), f32(256), jax.ShapeDtypeStruct((8, 128), F32)],
        scratch_shapes=[pltpu.VMEM((2, 128, 128), F32)],
        compiler_params=_params(("arbitrary",)),
    )(dcat, o_all, gq, gk, gv, gg, bc, sp, wgn)


def _swa_mask(i):
    t = lax.broadcasted_iota(jnp.int32, (BLK, 3 * BLK), 0)
    c = lax.broadcasted_iota(jnp.int32, (BLK, 3 * BLK), 1)
    qpos = i * BLK + t - PAD_ROWS
    seg = c // BLK
    cc = c % BLK
    kpos = jnp.where(seg == 0, (i - 1) * BLK + cc, i * BLK + cc) - PAD_ROWS
    band = (seg < 2) & (kpos >= N_META) & (kpos <= qpos) & (qpos - kpos < BLK)
    midx = cc - PAD_ROWS
    meta = (seg == 2) & (midx >= 0) & (midx <= qpos)
    return band | meta


def _swa_keys(k_ref, i):
    prev = pl.multiple_of(jnp.maximum(i - 1, 0) * BLK, BLK)
    own = pl.multiple_of(i * BLK, BLK)
    return jnp.concatenate([k_ref[pl.ds(prev, BLK), :], k_ref[pl.ds(own, BLK), :], k_ref[0:BLK, :]], axis=0), prev, own


def _place(x, kv):
    if kv == 0:
        lo = jnp.where(_half_mask(128, 0), x, jnp.zeros_like(x))
        return lo, pltpu.roll(lo, 64, 1)
    hi = jnp.where(_half_mask(128, 1), x, jnp.zeros_like(x))
    return pltpu.roll(hi, 64, 1), hi


def _swa_fwd(sq, sk, sv, sinks, wn):
    rows = sq.shape[0]
    nb = rows // BLK
    scale = SWA_HD ** -0.5

    def body(q_ref, k_ref, v_ref, sink_ref, wn_ref, o_ref, cat_ref, lse_ref):
        i = pl.program_id(0)
        mask = _swa_mask(i)
        keys, _, _ = _swa_keys(k_ref, i)
        vals, _, _ = _swa_keys(v_ref, i)
        kz = (_place(keys, 0), _place(keys, 1))
        vz = (_place(vals, 0), _place(vals, 1))
        outs = []
        for pr in range(4):
            q_pair = q_ref[:, 128 * pr:128 * pr + 128]
            o_pair = jnp.zeros((BLK, 128), F32)
            for half in range(2):
                hd = 2 * pr + half
                kv = hd // 4
                sink = sink_ref[0, hd]
                s = jnp.where(mask, _nt(q_pair, kz[kv][half]) * scale, NEG_INF)
                m = jnp.maximum(jnp.max(s, axis=-1, keepdims=True), sink)
                e = jnp.exp(s - m)
                den = jnp.sum(e, axis=-1, keepdims=True) + jnp.exp(sink - m)
                lse_ref[:, hd:hd + 1] = m + jnp.log(den)
                o_pair = o_pair + _nn((e / den).astype(BF16), vz[kv][half])
            outs.append(o_pair)
        o = jnp.concatenate(outs, axis=1)
        o_ref[...] = o
        on, _ = _rms(o)
        cat_ref[...] = (on * wn_ref[...]).astype(BF16)

    return pl.pallas_call(
        body, name="swa_fwd", grid=(nb,),
        in_specs=[_row_spec(BLK, 512), VMEM_SPEC, VMEM_SPEC, SMEM_SPEC, VMEM_SPEC],
        out_specs=[_row_spec(BLK, 512), _row_spec(BLK, 512), _row_spec(BLK, SWA_HEADS)],
        out_shape=[jax.ShapeDtypeStruct((rows, 512), F32), jax.ShapeDtypeStruct((rows, 512), BF16),
                   jax.ShapeDtypeStruct((rows, SWA_HEADS), F32)],
        compiler_params=_params(("arbitrary",)),
    )(sq, sk, sv, sinks, wn)


def _swa_bwd(dcat, o_all, sq, sk, sv, lse, sinks, wn):
    rows = sq.shape[0]
    nb = rows // BLK
    scale = SWA_HD ** -0.5

    def body(dc_ref, o_ref, q_ref, k_ref, v_ref, lse_ref, sink_ref, wn_ref, dq_ref, dk_ref, dv_ref, dsink_ref, dwn_ref):
        i = pl.program_id(0)
        first = i == 0

        @pl.when(first)
        def _():
            dk_ref[...] = jnp.zeros_like(dk_ref)
            dv_ref[...] = jnp.zeros_like(dv_ref)
        mask = _swa_mask(i)
        keys, prev, own = _swa_keys(k_ref, i)
        vals, _, _ = _swa_keys(v_ref, i)
        kz = (_place(keys, 0), _place(keys, 1))
        vz = (_place(vals, 0), _place(vals, 1))
        o = o_ref[...]
        on, ro = _rms(o)
        dc = dc_ref[...]
        _acc_add(dwn_ref, first, _colsum(dc * on))
        do = _rms_bwd(on, ro, wn_ref[...], dc)
        do_o = do * o
        lane8 = lax.broadcasted_iota(jnp.int32, (1, 128), 1)
        dsink = jnp.zeros((1, 128), F32)
        dk_acc = jnp.zeros((3 * BLK, 128), F32)
        dv_acc = jnp.zeros((3 * BLK, 128), F32)
        dqs = []
        for pr in range(4):
            ps = slice(128 * pr, 128 * pr + 128)
            q_pair = q_ref[:, ps]
            do16 = do[:, ps].astype(BF16)
            dq_pair = jnp.zeros((BLK, 128), F32)
            for half in range(2):
                hd = 2 * pr + half
                kv = hd // 4
                hm = _half_mask(128, half)
                sink = sink_ref[0, hd]
                lse_h = lse_ref[:, hd:hd + 1]
                delta = jnp.sum(jnp.where(hm, do_o[:, ps], 0.0), axis=-1, keepdims=True)
                s = jnp.where(mask, _nt(q_pair, kz[kv][half]) * scale, NEG_INF)
                prob = jnp.exp(s - lse_h)
                dsink = dsink + jnp.where(lane8 == hd, -jnp.sum(jnp.exp(sink - lse_h) * delta), 0.0)
                dp = _nt(do16, vz[kv][half])
                ds16 = (prob * (dp - delta) * scale).astype(BF16)
                dq_pair = dq_pair + _nn(ds16, kz[kv][half])
                dkz = jnp.where(hm, _tn(ds16, q_pair), 0.0)
                dvz = jnp.where(hm, _tn(prob.astype(BF16), do16), 0.0)
                if half != kv:
                    dkz = pltpu.roll(dkz, 64, 1)
                    dvz = pltpu.roll(dvz, 64, 1)
                dk_acc = dk_acc + dkz
                dv_acc = dv_acc + dvz
            dqs.append(dq_pair)
        dq_ref[...] = jnp.concatenate(dqs, axis=1)
        _acc_add(dsink_ref, first, dsink)
        for ref, acc in ((dk_ref, dk_acc), (dv_ref, dv_acc)):
            ref[pl.ds(prev, BLK), :] += acc[0:BLK]
            ref[pl.ds(own, BLK), :] += acc[BLK:2 * BLK]
            ref[0:BLK, :] += acc[2 * BLK:3 * BLK]

    full = pl.BlockSpec((rows, 128), lambda i: (0, 0))
    return pl.pallas_call(
        body, name="swa_bwd", grid=(nb,),
        in_specs=[_row_spec(BLK, 512), _row_spec(BLK, 512), _row_spec(BLK, 512), VMEM_SPEC, VMEM_SPEC,
                  _row_spec(BLK, SWA_HEADS), SMEM_SPEC, VMEM_SPEC],
        out_specs=[_row_spec(BLK, 512), full, full, _acc_spec(128), _acc_spec(512)],
        out_shape=[jax.ShapeDtypeStruct((rows, 512), F32), jax.ShapeDtypeStruct((rows, 128), F32),
                   jax.ShapeDtypeStruct((rows, 128), F32), jax.ShapeDtypeStruct((8, 128), F32),
                   jax.ShapeDtypeStruct((8, 512), F32)],
        compiler_params=_params(("arbitrary",)),
    )(dcat, o_all, sq, sk, sv, lse, sinks, wn)


def _mix_out(h, cat_g, cat_s, wout, gpost):
    rows = h.shape[0]
    tm = _row_tile(rows)

    def body(h_ref, cg_ref, cs_ref, w_ref, g_ref, ho_ref, m_ref):
        m = _nn(cg_ref[...], w_ref[0:512, :]) + _nn(cs_ref[...], w_ref[512:1024, :])
        m_ref[...] = m
        mn, _ = _rms(m)
        ho_ref[...] = h_ref[...] + mn * g_ref[...]

    row_f32 = _row_spec(tm, D_MODEL)
    return pl.pallas_call(
        body, name="mix_out", grid=(rows // tm,),
        in_specs=[row_f32, _row_spec(tm, 512), _row_spec(tm, 512), VMEM_SPEC, VMEM_SPEC],
        out_specs=[row_f32, row_f32],
        out_shape=[jax.ShapeDtypeStruct((rows, D_MODEL), F32), jax.ShapeDtypeStruct((rows, D_MODEL), F32)],
        compiler_params=_params(("arbitrary",)),
    )(h, cat_g, cat_s, wout, gpost)


def _mix_out_bwd(dh, m, wout, gpost):
    rows = dh.shape[0]
    tm = _row_tile(rows)

    def body(dh_ref, m_ref, w_ref, g_ref, dcg_ref, dcs_ref, dm_ref, dg_ref):
        first = pl.program_id(0) == 0
        dhv = dh_ref[...]
        mn, rm = _rms(m_ref[...])
        _acc_add(dg_ref, first, _colsum(dhv * mn))
        dm16 = _rms_bwd(mn, rm, g_ref[...], dhv).astype(BF16)
        dm_ref[...] = dm16
        dcat = _nt(dm16, w_ref[...])
        dcg_ref[...] = dcat[:, 0:512]
        dcs_ref[...] = dcat[:, 512:1024]

    row_f32 = _row_spec(tm, D_MODEL)
    return pl.pallas_call(
        body, name="mix_out_bwd", grid=(rows // tm,),
        in_specs=[row_f32, row_f32, VMEM_SPEC, VMEM_SPEC],
        out_specs=[_row_spec(tm, 512), _row_spec(tm, 512), row_f32, _acc_spec(D_MODEL)],
        out_shape=[jax.ShapeDtypeStruct((rows, 512), F32), jax.ShapeDtypeStruct((rows, 512), F32),
                   jax.ShapeDtypeStruct((rows, D_MODEL), BF16), jax.ShapeDtypeStruct((8, D_MODEL), F32)],
        compiler_params=_params(("arbitrary",)),
    )(dh, m, wout, gpost)


def _mix_in_bwd(dh_out, h, g, win_p, wa2_p, cos, sin, loga, ga, dgq, dgk, dgv, dgg, dsq, dsk, dsv, dloga):
    rows = h.shape[0]
    tm = _row_tile(rows)

    def body(dho_ref, h_ref, g_ref, win_ref, wa2_ref, cos_ref, sin_ref, loga_ref, ga_ref,
             dgq_ref, dgk_ref, dgv_ref, dgg_ref, dsq_ref, dsk_ref, dsv_ref, dla_ref,
             dh_ref, dproj_ref, dwa2_ref, dg_ref, dba_ref):
        first = pl.program_id(0) == 0
        dz = dla_ref[...] * (1.0 / GLA_TAU) * (1.0 - jnp.exp(GLA_TAU * loga_ref[...]))
        _acc_add(dba_ref, first, _colsum(dz))
        dga = _nt(dz, wa2_ref[...])
        pa = _tn(ga_ref[...], dz)
        c1, s1 = cos_ref[...], sin_ref[...]
        c4 = jnp.concatenate([c1, c1, c1, c1], axis=1)
        s4 = jnp.concatenate([s1, s1, s1, s1], axis=1)
        dq_r, dk_r = dsq_ref[...], dsk_ref[...]
        dsq = dq_r * c4 - _rot_half(dq_r * s4)
        dsk = dk_r * c1 - _rot_half(dk_r * s1)
        dproj16 = jnp.concatenate(
            [dgq_ref[...], dgk_ref[...], dgv_ref[...], dgg_ref[...], dsq, dsk, dsv_ref[...], dga], axis=1).astype(BF16)
        dproj_ref[...] = dproj16
        dn = _nn(dproj16, win_ref[...])

        @pl.when(first)
        def _():
            dwa2_ref[...] = pa

        @pl.when(jnp.logical_not(first))
        def _():
            dwa2_ref[...] += pa
        hn, rh = _rms(h_ref[...])
        _acc_add(dg_ref, first, _colsum(dn * hn))
        dh_ref[...] = dho_ref[...] + _rms_bwd(hn, rh, g_ref[...], dn)

    rs = lambda c: _row_spec(tm, c)
    return pl.pallas_call(
        body, name="mix_in_bwd", grid=(rows // tm,),
        in_specs=[rs(D_MODEL), rs(D_MODEL), VMEM_SPEC, VMEM_SPEC, VMEM_SPEC, rs(128), rs(128), rs(256), rs(128),
                  rs(256), rs(256), rs(512), rs(512), rs(512), rs(128), rs(128), rs(256)],
        out_specs=[rs(D_MODEL), rs(P_END), pl.BlockSpec((128, 256), lambda i: (0, 0)), _acc_spec(D_MODEL), _acc_spec(256)],
        out_shape=[jax.ShapeDtypeStruct((rows, D_MODEL), F32), jax.ShapeDtypeStruct((rows, P_END), BF16),
                   jax.ShapeDtypeStruct((128, 256), F32), jax.ShapeDtypeStruct((8, D_MODEL), F32),
                   jax.ShapeDtypeStruct((8, 256), F32)],
        compiler_params=_params(("arbitrary",)),
    )(dh_out, h, g, win_p, wa2_p, cos, sin, loga, ga, dgq, dgk, dgv, dgg, dsq, dsk, dsv, dloga)


def _rope_tables(rows):
    pos = (jnp.arange(rows, dtype=jnp.int32) - PAD_ROWS).astype(F32)
    inv_freq = 1.0 / (ROPE_THETA ** (jnp.arange(0, SWA_HD, 2, dtype=F32) / SWA_HD))
    ang = pos[:, None] * inv_freq[None, :]
    ang = jnp.concatenate([ang, ang, ang, ang], axis=-1)
    return jnp.cos(ang), jnp.sin(ang)


def _local_step(h0, tgt, w, late_weights=None, on_grads=None):
    rows = h0.shape[0]
    cos, sin = _rope_tables(rows)
    g = {}

    def tell(group, names):
        for nm in names:
            g[nm] = grads_now[nm]
        return 0.0 if on_grads is None else on_grads(group, {nm: grads_now[nm] for nm in names})

    h1, a1, b1, f1 = _ffn_fwd(h0, w["ffn1_pre"], w["wg1"], w["wu1"], w["wd1"], w["ffn1_post"])
    gq, gk, gv, gg, sq, sk, sv, ga, loga, bc, n2 = _mix_in(h1, w["mix_pre"], w["win"], w["wa2"], w["b_a"], cos, sin)
    o_g, cat_g, sp = _gla_fwd(gq, gk, gv, gg, bc, w["gla_norm"])
    o_s, cat_s, lse = _swa_fwd(sq, sk, sv, w["sinks"], w["swa_norm"])
    if late_weights is not None:
        w = {**w, **late_weights(lse)}
    h2, m = _mix_out(h1, cat_g, cat_s, w["wout"], w["mix_post"])
    h3, a2, b2, f2, dy, loss = _ffn_fwd(h2, w["ffn2_pre"], w["wg2"], w["wu2"], w["wd2"], w["ffn2_post"], tgt)
    del h3
    dh2, da, db, df, n3, g["ffn2_pre"], g["ffn2_post"] = _ffn_bwd_act(
        dy, h2, a2, b2, f2, w["ffn2_pre"], w["ffn2_post"], w["wg2"], w["wu2"], w["wd2"], "ffn2_bwd_act")
    grads_now = dict(wd2=_wgrad(b2, df, "ffn2_wgrad_down", gate=a2), wg2=_wgrad(da, n3, "ffn2_wgrad_gate"),
                     wu2=_wgrad(db, n3, "ffn2_wgrad_up"))
    tok = tell("ffn2", ("wd2", "wg2", "wu2"))
    dcg, dcs, dm, g["mix_post"] = _mix_out_bwd(dh2, m, w["wout"], w["mix_post"] + tok)
    dsq, dsk, dsv, g["sinks"], g["swa_norm"] = _swa_bwd(dcs, o_s, sq, sk, sv, lse, w["sinks"], w["swa_norm"])
    dgq, dgk, dgv, dgg, dloga, g["gla_norm"] = _gla_bwd(dcg, o_g, gq, gk, gv, gg, bc, sp, w["gla_norm"])
    dh1, dproj, g["wa2"], g["mix_pre"], g["b_a"] = _mix_in_bwd(
        dh2, h1, w["mix_pre"], w["win"], w["wa2"], cos, sin, loga, ga, dgq, dgk, dgv, dgg, dsq, dsk, dsv, dloga)
    grads_now = dict(wout=jnp.concatenate([_wgrad(cat_g, dm, "wout_wgrad_gla"), _wgrad(cat_s, dm, "wout_wgrad_swa")], axis=0),
                     win=_wgrad(dproj, n2, "win_wgrad"))
    tok = tell("mix", ("wout", "win"))
    dh0, da, db, df, n1, g["ffn1_pre"], g["ffn1_post"] = _ffn_bwd_act(
        dh1, h0, a1, b1, f1, w["ffn1_pre"] + tok, w["ffn1_post"], w["wg1"], w["wu1"], w["wd1"], "ffn1_bwd_act")
    grads_now = dict(wd1=_wgrad(b1, df, "ffn1_wgrad_down", gate=a1))
    tell("ffn1_down", ("wd1",))
    grads_now = dict(wg1=_wgrad(da, n1, "ffn1_wgrad_gate"))
    tell("ffn1_gate", ("wg1",))
    grads_now = dict(wu1=_wgrad(db, n1, "ffn1_wgrad_up"))
    tell("ffn1_up", ("wu1",))
    return loss[0, 0], dh0, g


def _win_pad_rows(win_t):
    pad = jnp.zeros((P_END - P_GA - 16, win_t.shape[1]), win_t.dtype)
    return jnp.concatenate([win_t[0:1536], win_t[1552:2320], win_t[1536:1552], pad], axis=0)


def _win_unpad_rows(win_p):
    return jnp.concatenate([win_p[0:1536], win_p[P_GA:P_GA + 16], win_p[1536:P_GA]], axis=0)


def _place_on_mesh():
    return lax.axis_index("x"), lax.axis_index("y"), lax.axis_index("c")


def _dev_index(px, py, pc):
    return 4 * px + 2 * py + pc


def _other_devices(x, y, c):
    flip = lambda v, f: 1 - v if f else v
    return [(flip(x, fx), flip(y, fy), flip(c, fc)) for fx in (0, 1) for fy in (0, 1) for fc in (0, 1)][1:]


def _all_gather(shards):
    n = len(shards)

    def body(*refs):
        ins, outs = refs[:n], refs[n:2 * n]
        send_sems, recv_sems, local_sems = refs[2 * n:]
        x, y, c = _place_on_mesh()
        me, sibling = (x, y, c), (x, y, 1 - c)
        chips = [(1 - x, y), (x, 1 - y), (1 - x, 1 - y)]

        def rows(k, px, py, pc):
            r = ins[k].shape[0]
            return outs[k].at[pl.ds(pl.multiple_of(_dev_index(px, py, pc) * r, 8), r), :]

        def copy(k, slot, block, to, src=None):
            return pltpu.make_async_remote_copy(
                src_ref=rows(k, *block) if src is None else src, dst_ref=rows(k, *block),
                send_sem=send_sems.at[k, slot], recv_sem=recv_sems.at[k, slot], device_id=to, device_id_type=MESH)

        local = [pltpu.make_async_copy(ins[k], rows(k, *me), local_sems.at[k]) for k in range(n)]
        sends = []
        for k in range(n):
            local[k].start()
            sends.append(copy(k, 0, me, sibling, src=ins[k]))
            sends += [copy(k, 1 + j, me, (*chip, c), src=ins[k]) for j, chip in enumerate(chips)]
        for cp in sends:
            cp.start()
        for k in range(n):
            for j, chip in enumerate(chips):
                copy(k, 1 + j, (*chip, c), me).wait_recv()
                passed = copy(k, 4 + j, (*chip, c), sibling)
                passed.start()
                sends.append(passed)
        for k in range(n):
            copy(k, 0, sibling, me).wait_recv()
            for j, chip in enumerate(chips):
                copy(k, 4 + j, (*chip, 1 - c), me).wait_recv()
        for cp in sends:
            cp.wait_send()
        for cp in local:
            cp.wait()

    return pl.pallas_call(
        body, name="all_gather_weights",
        in_specs=[ANY_SPEC] * n, out_specs=[ANY_SPEC] * n,
        out_shape=[jax.ShapeDtypeStruct((N_DEV * s.shape[0], s.shape[1]), s.dtype) for s in shards],
        scratch_shapes=[pltpu.SemaphoreType.DMA((n, 7)), pltpu.SemaphoreType.DMA((n, 7)), pltpu.SemaphoreType.DMA((n,))],
    )(*shards)


HBM_SPEC = pl.BlockSpec(memory_space=pltpu.HBM)
SEM_SPEC = pl.BlockSpec(memory_space=pltpu.SEMAPHORE)
DATAFLOW = pltpu.SideEffectType.DATAFLOW_SIDE_EFFECTING


def _exchange_copies(srcs, lands, send_sems, recv_sems, scatter, arriving):
    x, y, c = _place_on_mesh()
    me = _dev_index(x, y, c)
    out = []
    for k, (src, land) in enumerate(zip(srcs, lands)):
        r = land.shape[0] // N_DEV

        def block(ref, d):
            return ref.at[pl.ds(pl.multiple_of(d * r, 8), r), :]

        for f, peer in enumerate(_other_devices(x, y, c)):
            mine, his = (_dev_index(*peer), me) if arriving else (me, _dev_index(*peer))
            out.append(pltpu.make_async_remote_copy(
                src_ref=block(src, his) if scatter else src, dst_ref=block(land, mine),
                send_sem=send_sems.at[7 * k + f], recv_sem=recv_sems.at[7 * k + f], device_id=peer, device_id_type=MESH))
    return out


def _exchange_start(srcs, lands, scatter, name):
    n = len(srcs)

    def body(*refs):
        send_sems, recv_sems = refs[2 * n], refs[2 * n + 1]
        for going in _exchange_copies(refs[:n], refs[n:2 * n], send_sems, recv_sems, scatter, False):
            going.start()
        refs[-1][...] = jnp.zeros_like(refs[-1])

    both = list(srcs) + list(lands)
    outs = pl.pallas_call(
        body, name=name,
        out_shape=(pltpu.SemaphoreType.DMA((7 * n,)), pltpu.SemaphoreType.DMA((7 * n,)),
                   *[pltpu.HBM(a.shape, a.dtype) for a in both], jax.ShapeDtypeStruct((8, 128), F32)),
        in_specs=[HBM_SPEC] * (2 * n), out_specs=(SEM_SPEC, SEM_SPEC, *[HBM_SPEC] * (2 * n), VMEM_SPEC),
        input_output_aliases={i: 2 + i for i in range(2 * n)},
        compiler_params=pltpu.CompilerParams(has_side_effects=DATAFLOW),
    )(*[pltpu.with_memory_space_constraint(a, pltpu.HBM) for a in both])
    return outs[0], outs[1], outs[2:2 + n], outs[2 + n:2 + 2 * n], outs[-1]


def _exchange_wait(started, scatter, after, name):
    send_sems, recv_sems, srcs, lands, _ = started
    n = len(srcs)

    def body(*refs):
        args = (refs[:n], refs[n:2 * n], refs[2 * n], refs[2 * n + 1], scatter)
        for going in _exchange_copies(*args, False):
            going.wait_send()
        for coming in _exchange_copies(*args, True):
            coming.wait_recv()

    both = list(srcs) + list(lands)
    outs = pl.pallas_call(
        body, name=name, out_shape=[pltpu.HBM(a.shape, a.dtype) for a in both],
        in_specs=[HBM_SPEC] * (2 * n) + [SEM_SPEC, SEM_SPEC, ANY_SPEC], out_specs=[HBM_SPEC] * (2 * n),
        input_output_aliases={i: i for i in range(2 * n)},
        compiler_params=pltpu.CompilerParams(has_side_effects=DATAFLOW),
    )(*both, send_sems, recv_sems, after)
    return outs[n:]


def _own_block_placed(block, rows, dev):
    zone = lax.empty((N_DEV * rows, block.shape[1]), block.dtype)
    return lax.dynamic_update_slice(zone, block, (dev * rows, 0))


def _sum_partials(parts, name):
    n = len(parts)

    def body(*refs):
        ins, outs = refs[:n], refs[n:]
        first = pl.program_id(0) == 0
        for i_ref, o_ref in zip(ins, outs):
            v = i_ref[...].astype(F32)

            @pl.when(first)
            def _():
                o_ref[...] = v

            @pl.when(jnp.logical_not(first))
            def _():
                o_ref[...] += v

    shapes = [(p.shape[0] // N_DEV, p.shape[1]) for p in parts]
    return pl.pallas_call(
        body, name=name, grid=(N_DEV,),
        in_specs=[pl.BlockSpec(s, lambda j: (j, 0)) for s in shapes],
        out_specs=[pl.BlockSpec(s, lambda j: (0, 0)) for s in shapes],
        out_shape=[jax.ShapeDtypeStruct(s, F32) for s in shapes],
        compiler_params=_params(("arbitrary",)),
    )(*parts)


def _all_reduce_small(slab):
    rows, cols = slab.shape

    def body(x_ref, o_ref, gathered, send_sems, recv_sems):
        x, y, c = _place_on_mesh()
        me = _dev_index(x, y, c)
        peers = _other_devices(x, y, c)

        def copy(f, peer):
            return pltpu.make_async_remote_copy(
                src_ref=x_ref, dst_ref=gathered.at[me], send_sem=send_sems.at[f], recv_sem=recv_sems.at[f],
                device_id=peer, device_id_type=MESH)

        def arrival(f, peer):
            return pltpu.make_async_remote_copy(
                src_ref=x_ref, dst_ref=gathered.at[_dev_index(*peer)], send_sem=send_sems.at[f], recv_sem=recv_sems.at[f],
                device_id=peer, device_id_type=MESH)

        sends = [copy(f, peer) for f, peer in enumerate(peers)]
        for cp in sends:
            cp.start()
        gathered[me] = x_ref[...]
        for f, peer in enumerate(peers):
            arrival(f, peer).wait_recv()
        for cp in sends:
            cp.wait_send()
        total = gathered[0]
        for d in range(1, N_DEV):
            total = total + gathered[d]
        o_ref[...] = total

    return pl.pallas_call(
        body, name="all_reduce_small",
        in_specs=[VMEM_SPEC], out_specs=VMEM_SPEC, out_shape=jax.ShapeDtypeStruct((rows, cols), F32),
        scratch_shapes=[pltpu.VMEM((N_DEV, rows, cols), F32), pltpu.SemaphoreType.DMA((7,)), pltpu.SemaphoreType.DMA((7,))],
    )(slab)


def _adamw(ws, gs, ms, vs, name):
    n = len(ws)
    c1 = 1.0 / (1.0 - ADAM_B1 ** ADAM_STEP)
    c2 = 1.0 / (1.0 - ADAM_B2 ** ADAM_STEP)

    def body(*refs):
        w_r, g_r, m_r, v_r = refs[:n], refs[n:2 * n], refs[2 * n:3 * n], refs[3 * n:4 * n]
        d_o, m_o, v_o = refs[4 * n:5 * n], refs[5 * n:6 * n], refs[6 * n:7 * n]
        for k in range(n):
            g = g_r[k][...]
            m = ADAM_B1 * m_r[k][...] + (1.0 - ADAM_B1) * g
            v = ADAM_B2 * v_r[k][...] + (1.0 - ADAM_B2) * (g * g)
            m_o[k][...] = m
            v_o[k][...] = v
            d_o[k][...] = -ADAM_LR * ((m * c1) / (jnp.sqrt(v * c2) + ADAM_EPS) + ADAM_WD * w_r[k][...])

    shapes = [jax.ShapeDtypeStruct(w.shape, F32) for w in ws]
    outs = pl.pallas_call(
        body, name=name, in_specs=[VMEM_SPEC] * (4 * n), out_specs=[VMEM_SPEC] * (3 * n), out_shape=shapes * 3,
        compiler_params=pltpu.CompilerParams(vmem_limit_bytes=56 << 20),
    )(*ws, *gs, *ms, *vs)
    return outs[:n], outs[n:2 * n], outs[2 * n:]


WEIGHT_NAMES = ("meta_tokens", "ffn1_pre_norm", "ffn1_w_gate", "ffn1_w_up", "ffn1_w_down", "ffn1_post_norm", "mix_pre_norm",
                "w_in", "gla_w_a2", "gla_b_a", "gla_out_norm", "swa_sinks", "swa_out_norm", "w_out", "mix_post_norm",
                "ffn2_pre_norm", "ffn2_w_gate", "ffn2_w_up", "ffn2_w_down", "ffn2_post_norm")
WIN_SHARD = D_IN // N_DEV
WIN_SHARD_PAD = 304
SLAB_VECTORS = ("ffn1_pre", "ffn1_post", "mix_pre", "mix_post", "ffn2_pre", "ffn2_post")
SLAB_ROWS = 32


def kernel(x, meta_tokens, ffn1_pre_norm, ffn1_w_gate, ffn1_w_up, ffn1_w_down, ffn1_post_norm, mix_pre_norm, w_in, gla_w_a2, gla_b_a, gla_out_norm, swa_sinks, swa_out_norm, w_out, mix_post_norm, ffn2_pre_norm, ffn2_w_gate, ffn2_w_up, ffn2_w_down, ffn2_post_norm, loss_target, m_meta_tokens, m_ffn1_pre_norm, m_ffn1_w_gate, m_ffn1_w_up, m_ffn1_w_down, m_ffn1_post_norm, m_mix_pre_norm, m_w_in, m_gla_w_a2, m_gla_b_a, m_gla_out_norm, m_swa_sinks, m_swa_out_norm, m_w_out, m_mix_post_norm, m_ffn2_pre_norm, m_ffn2_w_gate, m_ffn2_w_up, m_ffn2_w_down, m_ffn2_post_norm, v_meta_tokens, v_ffn1_pre_norm, v_ffn1_w_gate, v_ffn1_w_up, v_ffn1_w_down, v_ffn1_post_norm, v_mix_pre_norm, v_w_in, v_gla_w_a2, v_gla_b_a, v_gla_out_norm, v_swa_sinks, v_swa_out_norm, v_w_out, v_mix_post_norm, v_ffn2_pre_norm, v_ffn2_w_gate, v_ffn2_w_up, v_ffn2_w_down, v_ffn2_post_norm):
    given = dict(locals())
    W = {n: given[n] for n in WEIGHT_NAMES}
    M = {n: given["m_" + n] for n in WEIGHT_NAMES}
    V = {n: given["v_" + n] for n in WEIGHT_NAMES}
    dev = _dev_index(*_place_on_mesh())

    def t16(w):
        return w[0].T.astype(BF16)

    win_shard = jnp.pad(t16(W["w_in"]), ((0, WIN_SHARD_PAD - WIN_SHARD), (0, 0)))
    small = jnp.concatenate([W["meta_tokens"], jnp.pad(W["gla_w_a2"][0], ((0, 0), (0, 96)))], axis=0)
    wg1, wu1, wd1, win_g, small_g = _all_gather(
        [t16(W["ffn1_w_gate"]), t16(W["ffn1_w_up"]), W["ffn1_w_down"][0].astype(BF16), win_shard, small])
    late_shards = [W["w_out"][0].astype(BF16), t16(W["ffn2_w_gate"]), t16(W["ffn2_w_up"]), W["ffn2_w_down"][0].astype(BF16)]
    late = _exchange_start(late_shards, [_own_block_placed(s, s.shape[0], dev) for s in late_shards], False,
                           "gather_late_weights_start")

    def late_weights(after):
        wout, wg2, wu2, wd2 = _exchange_wait(late, False, after, "gather_late_weights_wait")
        return dict(wout=wout, wg2=wg2, wu2=wu2, wd2=wd2)

    win_t = win_g.reshape(N_DEV, WIN_SHARD_PAD, D_MODEL)[:, :WIN_SHARD].reshape(D_IN, D_MODEL)
    small_g = small_g.reshape(N_DEV, 32, 128)
    meta_full = small_g[:, :N_META].transpose(1, 0, 2).reshape(N_META, D_MODEL)
    wa2_full = small_g[:, N_META:, :32].transpose(1, 0, 2).reshape(16, 256)
    w = dict(
        ffn1_pre=W["ffn1_pre_norm"] + late[4][0, 0], ffn1_post=W["ffn1_post_norm"], mix_pre=W["mix_pre_norm"],
        mix_post=W["mix_post_norm"], ffn2_pre=W["ffn2_pre_norm"], ffn2_post=W["ffn2_post_norm"], b_a=W["gla_b_a"],
        gla_norm=W["gla_out_norm"], sinks=W["swa_sinks"], swa_norm=W["swa_out_norm"], wg1=wg1, wu1=wu1, wd1=wd1,
        win=_win_pad_rows(win_t), wa2=jnp.pad(wa2_full, ((0, 112), (0, 0))))

    in_flight = []

    def on_grads(group, grads):
        parts = []
        for nm, p in grads.items():
            if nm == "win":
                p = _win_unpad_rows(p).reshape(N_DEV, WIN_SHARD, D_MODEL)
                p = jnp.pad(p, ((0, 0), (0, WIN_SHARD_PAD - WIN_SHARD), (0, 0))).reshape(N_DEV * WIN_SHARD_PAD, D_MODEL)
            parts.append(p)
        lands = [_own_block_placed(lax.dynamic_slice_in_dim(p, dev * (p.shape[0] // N_DEV), p.shape[0] // N_DEV, axis=0),
                                   p.shape[0] // N_DEV, dev) for p in parts]
        started = _exchange_start(parts, lands, True, "scatter_" + group + "_start")
        in_flight.append((group, list(grads), started))
        return started[4][0, 0]

    front = jnp.zeros((PAD_ROWS, D_MODEL), F32)
    h0 = jnp.concatenate([front, meta_full, x[0]], axis=0)
    tgt = jnp.concatenate([jnp.zeros((BLK, D_MODEL), F32), loss_target[0]], axis=0)
    loss, dh0, g = _local_step(h0, tgt, w, late_weights, on_grads)
    grad_x = dh0[BLK:][None]

    packed = jnp.concatenate([g["b_a"][0:1], g["gla_norm"][0:1], g["sinks"][0:1], g["swa_norm"][0:1]], axis=1)
    slab = jnp.concatenate([g[k][0:1] for k in SLAB_VECTORS] + [packed, jnp.full((1, D_MODEL), loss, F32),
                           g["wa2"][:16].reshape(4, D_MODEL), jnp.zeros((4, D_MODEL), F32), dh0[PAD_ROWS:BLK]], axis=0)
    tot = _all_reduce_small(slab)
    loss = tot[7, 0]
    small_grads = dict(
        ffn1_pre_norm=tot[0:1], ffn1_post_norm=tot[1:2], mix_pre_norm=tot[2:3], mix_post_norm=tot[3:4],
        ffn2_pre_norm=tot[4:5], ffn2_post_norm=tot[5:6], gla_b_a=tot[6:7, 0:256], gla_out_norm=tot[6:7, 256:384],
        swa_sinks=tot[6:7, 384:392], swa_out_norm=tot[6:7, 512:1024],
        gla_w_a2=lax.dynamic_slice_in_dim(tot[8:12].reshape(16, 256), dev * 32, 32, axis=1)[None],
        meta_tokens=lax.dynamic_slice_in_dim(tot[16:32], dev * 128, 128, axis=1))

    last_started = in_flight[-1][2][4]
    sums = {}
    for group, names, started in in_flight:
        lands = _exchange_wait(started, True, last_started, "scatter_" + group + "_wait")
        for nm, s in zip(names, _sum_partials(lands, "sum_" + group)):
            sums[nm] = s
    big = dict(ffn1_w_gate=sums["wg1"].T[None], ffn1_w_up=sums["wu1"].T[None], ffn1_w_down=sums["wd1"][None],
               w_in=sums["win"][:WIN_SHARD].T[None], w_out=sums["wout"][None],
               ffn2_w_gate=sums["wg2"].T[None], ffn2_w_up=sums["wu2"].T[None], ffn2_w_down=sums["wd2"][None])
    grads = {**big, **small_grads}

    delta, new_m, new_v = {}, {}, {}
    for n in big:
        d_, m_, v_ = _adamw([W[n][0]], [grads[n][0]], [M[n][0]], [V[n][0]], "adamw_" + n)
        delta[n], new_m[n], new_v[n] = d_[0][None], m_[0][None], v_[0][None]
    names = [n for n in WEIGHT_NAMES if n not in big]
    two_d = lambda a: a.reshape(-1, a.shape[-1])
    d_, m_, v_ = _adamw([two_d(W[n]) for n in names], [two_d(grads[n]) for n in names],
                        [two_d(M[n]) for n in names], [two_d(V[n]) for n in names], "adamw_small")
    for k, n in enumerate(names):
        delta[n], new_m[n], new_v[n] = d_[k].reshape(W[n].shape), m_[k].reshape(W[n].shape), v_[k].reshape(W[n].shape)
    return (loss, grad_x, *[grads[n] for n in WEIGHT_NAMES], *[delta[n] for n in WEIGHT_NAMES],
            *[new_m[n] for n in WEIGHT_NAMES], *[new_v[n] for n in WEIGHT_NAMES])
```

```python
import functools

import jax
import jax.numpy as jnp
from jax import lax
from jax.experimental import pallas as pl
from jax.experimental.pallas import tpu as pltpu

F32, BF16 = jnp.float32, jnp.bfloat16

D_MODEL = 1024
D_FF = 2816
N_META = 16
BLK = 128
PAD_ROWS = BLK - N_META
GLA_DK = 64
SWA_HD = 64
SWA_HEADS = 8
GLA_TAU = 16.0
NORM_EPS = 1e-6
NEG_INF = -1e30
ROPE_THETA = 10000.0
P_GQ, P_GK, P_GV, P_GG, P_SQ, P_SK, P_SV, P_GA, P_END = 0, 256, 512, 1024, 1536, 2048, 2176, 2304, 2432
D_IN = 2320
IN_SPLITS = (256, 256, 512, 512, 16, 512, 128, 128)
FF_TILE = 2816
WGRAD_TILE_MAX = 2432
N_DEV = 8
MESH = pl.DeviceIdType.MESH

ADAM_LR, ADAM_B1, ADAM_B2, ADAM_EPS, ADAM_WD, ADAM_STEP = 0.001, 0.9, 0.999, 1e-08, 0.01, 10

V7X_VMEM_BYTES = 64 << 20
VMEM_SPEC = pl.BlockSpec(memory_space=pltpu.VMEM)
SMEM_SPEC = pl.BlockSpec(memory_space=pltpu.SMEM)
ANY_SPEC = pl.BlockSpec(memory_space=pl.ANY)


def _params(semantics, vmem_mb=56):
    return pltpu.CompilerParams(dimension_semantics=semantics, vmem_limit_bytes=vmem_mb << 20)


def _row_tile(rows):
    return 320 if rows % 320 == 0 else BLK


def _nn(a, b):
    return lax.dot_general(a, b, (((1,), (0,)), ((), ())), preferred_element_type=F32)


def _nt(a, b):
    return lax.dot_general(a, b, (((1,), (1,)), ((), ())), preferred_element_type=F32)


def _tn(a, b):
    return lax.dot_general(a, b, (((0,), (0,)), ((), ())), preferred_element_type=F32)


def _rms(x):
    r = lax.rsqrt(jnp.mean(x * x, axis=-1, keepdims=True) + NORM_EPS)
    return x * r, r


def _rms_bwd(xn, r, w, dy):
    g = dy * w
    return r * (g - xn * jnp.mean(g * xn, axis=-1, keepdims=True))


def _sigmoid(x):
    return 1.0 / (1.0 + jnp.exp(-x))


def _colsum(x):
    return jnp.sum(x, axis=0, keepdims=True)


def _split_bf16(x):
    hi = x.astype(BF16)
    lo = (x - hi.astype(F32)).astype(BF16)
    return hi, lo


def _tri(lower):
    r = lax.broadcasted_iota(jnp.int32, (BLK, BLK), 0)
    c = lax.broadcasted_iota(jnp.int32, (BLK, BLK), 1)
    return (r >= c) if lower else (c >= r)


def _half_mask(width, half):
    lane = lax.broadcasted_iota(jnp.int32, (1, width), 1)
    return ((lane % 128) < 64) if half == 0 else ((lane % 128) >= 64)


def _rot_half(x):
    w = x.shape[-1]
    lane = lax.broadcasted_iota(jnp.int32, (1, w), 1)
    return jnp.where((lane % SWA_HD) < SWA_HD // 2, -pltpu.roll(x, w - SWA_HD // 2, 1), pltpu.roll(x, SWA_HD // 2, 1))


def _row_spec(tm, cols):
    return pl.BlockSpec((tm, cols), lambda i: (i, 0))


def _acc_spec(cols):
    return pl.BlockSpec((8, cols), lambda i: (0, 0))


def _acc_add(ref, first, value):
    @pl.when(first)
    def _():
        ref[...] = jnp.zeros_like(ref)
    ref[0:1, :] += value


def _ffn_fwd(h, gpre, wg_t, wu_t, wd, gpost, tgt=None):
    rows = h.shape[0]
    tm = _row_tile(rows)
    nf = D_FF // FF_TILE
    with_loss = tgt is not None

    def body(*refs):
        if with_loss:
            (h_ref, gpre_ref, wg_ref, wu_ref, wd_ref, gpost_ref, t_ref,
             ho_ref, a_ref, b_ref, f_ref, dy_ref, loss_ref, acc) = refs
        else:
            (h_ref, gpre_ref, wg_ref, wu_ref, wd_ref, gpost_ref, ho_ref, a_ref, b_ref, f_ref, acc) = refs
        i = pl.program_id(0)
        h_in = h_ref[...]
        hn, _ = _rms(h_in)
        n16 = (hn * gpre_ref[...]).astype(BF16)
        for j in range(nf):
            cols = slice(j * FF_TILE, (j + 1) * FF_TILE)
            a = _nt(n16, wg_ref[cols, :])
            b = _nt(n16, wu_ref[cols, :])
            a_ref[:, cols] = a.astype(BF16)
            b_ref[:, cols] = b.astype(BF16)
            s16 = (a * _sigmoid(a) * b).astype(BF16)
            part = _nn(s16, wd_ref[cols, :])
            if j == 0:
                acc[...] = part
            else:
                acc[...] += part
        f = acc[...]
        f_ref[...] = f
        fn, _ = _rms(f)
        y = h_in + 0.5 * (fn * gpost_ref[...])
        ho_ref[...] = y
        if with_loss:
            row = i * tm + lax.broadcasted_iota(jnp.int32, (tm, 1), 0)
            err = jnp.where(row >= BLK, y - t_ref[...], 0.0)
            dy_ref[...] = err * (1.0 / D_MODEL)
            part = 0.5 * jnp.sum(jnp.sum(err * err, axis=-1, keepdims=True) * (1.0 / D_MODEL), axis=0, keepdims=True)

            @pl.when(i == 0)
            def _():
                loss_ref[...] = jnp.zeros_like(loss_ref)
            loss_ref[...] += part

    row_f32 = _row_spec(tm, D_MODEL)
    in_specs = [row_f32, VMEM_SPEC, VMEM_SPEC, VMEM_SPEC, VMEM_SPEC, VMEM_SPEC]
    out_specs = [row_f32, _row_spec(tm, D_FF), _row_spec(tm, D_FF), row_f32]
    out_shape = [jax.ShapeDtypeStruct((rows, D_MODEL), F32), jax.ShapeDtypeStruct((rows, D_FF), BF16),
                 jax.ShapeDtypeStruct((rows, D_FF), BF16), jax.ShapeDtypeStruct((rows, D_MODEL), F32)]
    args = [h, gpre, wg_t, wu_t, wd, gpost]
    if with_loss:
        in_specs.append(row_f32)
        args.append(tgt)
        out_specs += [row_f32, pl.BlockSpec((8, 128), lambda i: (0, 0))]
        out_shape += [jax.ShapeDtypeStruct((rows, D_MODEL), F32), jax.ShapeDtypeStruct((8, 128), F32)]
    return pl.pallas_call(
        body, name="ffn_fwd_loss" if with_loss else "ffn_fwd", grid=(rows // tm,),
        in_specs=in_specs, out_specs=out_specs, out_shape=out_shape,
        scratch_shapes=[pltpu.VMEM((tm, D_MODEL), F32)],
        compiler_params=_params(("arbitrary",)),
    )(*args)


def _ffn_bwd_act(dh_out, h, a, b, f, gpre, gpost, wg_t, wu_t, wd, name):
    rows = h.shape[0]
    tm = _row_tile(rows)
    nf = D_FF // FF_TILE

    def body(dho_ref, h_ref, a_ref, b_ref, f_ref, gpre_ref, gpost_ref, wg_ref, wu_ref, wd_ref,
             dh_ref, da_ref, db_ref, df_ref, n_ref, dgpre_ref, dgpost_ref, acc):
        first = pl.program_id(0) == 0
        dho = dho_ref[...]
        drr = 0.5 * dho
        fn, rf = _rms(f_ref[...])
        _acc_add(dgpost_ref, first, _colsum(drr * fn))
        df16 = _rms_bwd(fn, rf, gpost_ref[...], drr).astype(BF16)
        df_ref[...] = df16
        hn, rh = _rms(h_ref[...])
        n_ref[...] = (hn * gpre_ref[...]).astype(BF16)
        for j in range(nf):
            cols = slice(j * FF_TILE, (j + 1) * FF_TILE)
            ds = _nt(df16, wd_ref[cols, :])
            av = a_ref[:, cols].astype(F32)
            bv = b_ref[:, cols].astype(F32)
            sg = _sigmoid(av)
            db16 = (ds * (av * sg)).astype(BF16)
            da16 = (ds * bv * (sg * (1.0 + av * (1.0 - sg)))).astype(BF16)
            da_ref[:, cols] = da16
            db_ref[:, cols] = db16
            part = _nn(da16, wg_ref[cols, :]) + _nn(db16, wu_ref[cols, :])
            if j == 0:
                acc[...] = part
            else:
                acc[...] += part
        dn = acc[...]
        _acc_add(dgpre_ref, first, _colsum(dn * hn))
        dh_ref[...] = dho + _rms_bwd(hn, rh, gpre_ref[...], dn)

    row_f32 = _row_spec(tm, D_MODEL)
    row_ff = _row_spec(tm, D_FF)
    return pl.pallas_call(
        body, name=name, grid=(rows // tm,),
        in_specs=[row_f32, row_f32, row_ff, row_ff, row_f32, VMEM_SPEC, VMEM_SPEC, VMEM_SPEC, VMEM_SPEC, VMEM_SPEC],
        out_specs=[row_f32, row_ff, row_ff, row_f32, row_f32, _acc_spec(D_MODEL), _acc_spec(D_MODEL)],
        out_shape=[jax.ShapeDtypeStruct((rows, D_MODEL), F32), jax.ShapeDtypeStruct((rows, D_FF), BF16),
                   jax.ShapeDtypeStruct((rows, D_FF), BF16), jax.ShapeDtypeStruct((rows, D_MODEL), BF16),
                   jax.ShapeDtypeStruct((rows, D_MODEL), BF16), jax.ShapeDtypeStruct((8, D_MODEL), F32),
                   jax.ShapeDtypeStruct((8, D_MODEL), F32)],
        scratch_shapes=[pltpu.VMEM((tm, D_MODEL), F32)],
        compiler_params=_params(("arbitrary",)),
    )(dh_out, h, a, b, f, gpre, gpost, wg_t, wu_t, wd)


def _wgrad(lhs, rhs, name, gate=None):
    rows, width = lhs.shape
    tm = 1664 if rows % 1664 == 0 else BLK
    tf = width // 2 if width > WGRAD_TILE_MAX else width
    nr = rows // tm
    gated = gate is not None

    def body(*refs):
        if gated:
            g_ref, l_ref, r_ref, o_ref, acc = refs
            gv = g_ref[...].astype(F32)
            lv = (gv * _sigmoid(gv) * l_ref[...].astype(F32)).astype(BF16)
        else:
            l_ref, r_ref, o_ref, acc = refs
            lv = l_ref[...]
        i = pl.program_id(1)
        part = _tn(lv, r_ref[...])

        @pl.when(i == 0)
        def _():
            acc[...] = part

        @pl.when(i > 0)
        def _():
            acc[...] += part

        @pl.when(i == nr - 1)
        def _():
            o_ref[...] = acc[...].astype(BF16)

    l_spec = pl.BlockSpec((tm, tf), lambda j, i: (i, j))
    r_spec = pl.BlockSpec((tm, D_MODEL), lambda j, i: (i, 0))
    return pl.pallas_call(
        body, name=name, grid=(width // tf, nr),
        in_specs=([l_spec] if gated else []) + [l_spec, r_spec],
        out_specs=pl.BlockSpec((tf, D_MODEL), lambda j, i: (j, 0)),
        out_shape=jax.ShapeDtypeStruct((width, D_MODEL), BF16),
        scratch_shapes=[pltpu.VMEM((tf, D_MODEL), F32)],
        compiler_params=_params(("arbitrary", "arbitrary")),
    )(*([gate] if gated else []), lhs, rhs)


def _chunk_cumsum(x, lower):
    tri = jnp.where(_tri(lower), 1.0, 0.0).astype(BF16)
    hi, lo = _split_bf16(x)
    return _nn(tri, hi) + _nn(tri, lo)


def _mix_in(h, g, win_p, wa2_p, b_a, cos, sin):
    rows = h.shape[0]
    tm = _row_tile(rows)
    tm = tm if tm % BLK == 0 else BLK

    def body(h_ref, g_ref, win_ref, wa2_ref, ba_ref, cos_ref, sin_ref,
             gq_ref, gk_ref, gv_ref, gg_ref, sq_ref, sk_ref, sv_ref, ga_ref, loga_ref, bc_ref, n_ref):
        hn, _ = _rms(h_ref[...])
        n16 = (hn * g_ref[...]).astype(BF16)
        n_ref[...] = n16
        proj = _nt(n16, win_ref[...])
        gq_ref[...] = proj[:, P_GQ:P_GK]
        gk_ref[...] = proj[:, P_GK:P_GV]
        gv_ref[...] = proj[:, P_GV:P_GG].astype(BF16)
        gg_ref[...] = proj[:, P_GG:P_SQ]
        c1, s1 = cos_ref[...], sin_ref[...]
        c4 = jnp.concatenate([c1, c1, c1, c1], axis=1)
        s4 = jnp.concatenate([s1, s1, s1, s1], axis=1)
        sq = proj[:, P_SQ:P_SK]
        sk = proj[:, P_SK:P_SV]
        sq_ref[...] = (sq * c4 + _rot_half(sq) * s4).astype(BF16)
        sk_ref[...] = (sk * c1 + _rot_half(sk) * s1).astype(BF16)
        sv_ref[...] = proj[:, P_SV:P_GA].astype(BF16)
        ga = proj[:, P_GA:P_END]
        ga_ref[...] = ga
        z = _nn(ga, wa2_ref[...]) + ba_ref[...]
        loga = (jnp.minimum(z, 0.0) - jnp.log(1.0 + jnp.exp(-jnp.abs(z)))) * (1.0 / GLA_TAU)
        loga_ref[...] = loga
        for c in range(tm // BLK):
            rs = slice(c * BLK, (c + 1) * BLK)
            bc_ref[rs, :] = _chunk_cumsum(loga[rs, :], True)

    f32 = lambda c: jax.ShapeDtypeStruct((rows, c), F32)
    b16 = lambda c: jax.ShapeDtypeStruct((rows, c), BF16)
    rs = lambda c: _row_spec(tm, c)
    return pl.pallas_call(
        body, name="mix_in", grid=(rows // tm,),
        in_specs=[rs(D_MODEL), VMEM_SPEC, VMEM_SPEC, VMEM_SPEC, VMEM_SPEC, rs(128), rs(128)],
        out_specs=[rs(256), rs(256), rs(512), rs(512), rs(512), rs(128), rs(128), rs(128), rs(256), rs(256), rs(D_MODEL)],
        out_shape=[f32(256), f32(256), b16(512), f32(512), b16(512), b16(128), b16(128), f32(128), f32(256), f32(256),
                   b16(D_MODEL)],
        compiler_params=_params(("arbitrary",)),
    )(h, g, win_p, wa2_p, b_a, cos, sin)


def _gla_factors(q, k, bc):
    bm = bc[BLK // 2 - 1:BLK // 2, :]
    bl = bc[BLK - 1:BLK, :]
    e_q, e_k, e_qe, e_kd = jnp.exp(bc - bm), jnp.exp(bm - bc), jnp.exp(bc), jnp.exp(bl - bc)
    return (q * e_q, k * e_k, q * e_qe, k * e_kd), (e_q, e_k, e_qe, e_kd), jnp.exp(bl)


def _gla_fwd(gq, gk, gv, gg, bc, wgn):
    rows = gq.shape[0]
    nc = rows // BLK
    scale = GLA_DK ** -0.5

    def body(q_ref, k_ref, v_ref, gg_ref, bc_ref, wgn_ref, o_ref, cat_ref, sp_ref, st):
        @pl.when(pl.program_id(0) == 0)
        def _():
            st[...] = jnp.zeros_like(st)
        low = _tri(True)
        wgn_v = wgn_ref[...]
        for p in range(2):
            sl = slice(128 * p, 128 * p + 128)
            (qt, kt, qe, kd), _, ebl = _gla_factors(q_ref[:, sl] * scale, k_ref[:, sl], bc_ref[:, sl])
            s_prev = st[p]
            sp_ref[0, p] = s_prev
            s16 = s_prev.astype(BF16)
            qt16 = qt.astype(BF16)
            s_new = s_prev * ebl
            for hh in range(2):
                hs = slice(128 * (2 * p + hh), 128 * (2 * p + hh) + 128)
                lm = _half_mask(128, hh)
                vh = v_ref[:, hs]
                pm = jnp.where(low, _nt(qt16, jnp.where(lm, kt, 0.0).astype(BF16)), 0.0)
                o = _nn(pm.astype(BF16), vh) + _nt(jnp.where(lm, qe, 0.0).astype(BF16), s16)
                s_new = s_new + _tn(vh, jnp.where(lm, kd, 0.0).astype(BF16))
                o_ref[:, hs] = o
                on, _ = _rms(o)
                gate = gg_ref[:, hs]
                cat_ref[:, hs] = (on * wgn_v * (gate * _sigmoid(gate))).astype(BF16)
            st[p] = s_new

    rs = lambda c: _row_spec(BLK, c)
    return pl.pallas_call(
        body, name="gla_fwd", grid=(nc,),
        in_specs=[rs(256), rs(256), rs(512), rs(512), rs(256), VMEM_SPEC],
        out_specs=[rs(512), rs(512), pl.BlockSpec((1, 2, 128, 128), lambda i: (i, 0, 0, 0))],
        out_shape=[jax.ShapeDtypeStruct((rows, 512), F32), jax.ShapeDtypeStruct((rows, 512), BF16),
                   jax.ShapeDtypeStruct((nc, 2, 128, 128), F32)],
        scratch_shapes=[pltpu.VMEM((2, 128, 128), F32)],
        compiler_params=_params(("arbitrary",)),
    )(gq, gk, gv, gg, bc, wgn)


def _gla_bwd(dcat, o_all, gq, gk, gv, gg, bc, sp, wgn):
    rows = gq.shape[0]
    nc = rows // BLK
    scale = GLA_DK ** -0.5

    def body(dc_ref, o_ref, q_ref, k_ref, v_ref, gg_ref, bc_ref, sp_ref, wgn_ref,
             dq_ref, dk_ref, dv_ref, dgg_ref, dla_ref, dwgn_ref, dst):
        first = pl.program_id(0) == 0

        @pl.when(first)
        def _():
            dst[...] = jnp.zeros_like(dst)
        low, upp = _tri(True), _tri(False)
        last_row = lax.broadcasted_iota(jnp.int32, (BLK, 1), 0) == BLK - 1
        wgn_v = wgn_ref[...]
        dwgn = jnp.zeros((1, 128), F32)
        for p in range(2):
            sl = slice(128 * p, 128 * p + 128)
            (qt, kt, qe, kd), (e_q, e_k, e_qe, e_kd), ebl = _gla_factors(
                q_ref[:, sl] * scale, k_ref[:, sl], bc_ref[:, sl])
            s_prev = sp_ref[0, p]
            s16 = s_prev.astype(BF16)
            ds_next = dst[p]
            ds16 = ds_next.astype(BF16)
            qt16 = qt.astype(BF16)
            ds_new = ds_next * ebl
            dqt = jnp.zeros((BLK, 128), F32)
            dkt = jnp.zeros((BLK, 128), F32)
            dqe = jnp.zeros((BLK, 128), F32)
            dkd = jnp.zeros((BLK, 128), F32)
            for hh in range(2):
                hs = slice(128 * (2 * p + hh), 128 * (2 * p + hh) + 128)
                lm = _half_mask(128, hh)
                on, ro = _rms(o_ref[:, hs])
                gate = gg_ref[:, hs]
                sg = _sigmoid(gate)
                si = gate * sg
                dog = dc_ref[:, hs]
                dwgn = dwgn + _colsum(dog * si * on)
                dgg_ref[:, hs] = dog * (on * wgn_v) * (sg * (1.0 + gate * (1.0 - sg)))
                do16 = _rms_bwd(on, ro, wgn_v, dog * si).astype(BF16)
                vh = v_ref[:, hs]
                ktm16 = jnp.where(lm, kt, 0.0).astype(BF16)
                qtm16 = jnp.where(lm, qt, 0.0).astype(BF16)
                qem16 = jnp.where(lm, qe, 0.0).astype(BF16)
                kdm16 = jnp.where(lm, kd, 0.0).astype(BF16)
                p_t = jnp.where(upp, _nt(ktm16, qt16), 0.0)
                dp_t = jnp.where(upp, _nt(vh, do16), 0.0)
                dp = jnp.where(low, _nt(do16, vh), 0.0)
                dv_ref[:, hs] = _nn(p_t.astype(BF16), do16) + _nt(kdm16, ds16)
                dqt = dqt + _nn(dp.astype(BF16), ktm16)
                dkt = dkt + _nn(dp_t.astype(BF16), qtm16)
                dqe = dqe + jnp.where(lm, _nn(do16, s16), 0.0)
                dkd = dkd + jnp.where(lm, _nn(vh, ds16), 0.0)
                ds_new = ds_new + _tn(do16, qem16)
            debl = _colsum(ds_next * s_prev)
            dq_ref[:, sl] = (dqt * e_q + dqe * e_qe) * scale
            dk_ref[:, sl] = dkt * e_k + dkd * e_kd
            dkd_kd = dkd * kd
            db = dqt * qt - dkt * kt + dqe * qe - dkd_kd
            db = jnp.where(last_row, db + (_colsum(dkd_kd) + debl * ebl), db)
            dla_ref[:, sl] = _chunk_cumsum(db, False)
            dst[p] = ds_new
        _acc_add(dwgn_ref, first, dwgn)

    rev = lambda c: pl.BlockSpec((BLK, c), lambda i: (nc - 1 - i, 0))
    f32 = lambda c: jax.ShapeDtypeStruct((rows, c), F32)
    return pl.pallas_call(
        body, name="gla_bwd", grid=(nc,),
        in_specs=[rev(512), rev(512), rev(256), rev(256), rev(512), rev(512), rev(256),
                  pl.BlockSpec((1, 2, 128, 128), lambda i: (nc - 1 - i, 0, 0, 0)), VMEM_SPEC],
        out_specs=[rev(256), rev(256), rev(512), rev(512), rev(256), _acc_spec(128)],
        out_shape=[f32(256), f32(256), f32(512), f32(512), f32(256), jax.ShapeDtypeStruct((8, 128), F32)],
        scratch_shapes=[pltpu.VMEM((2, 128, 128), F32)],
        compiler_params=_params(("arbitrary",)),
    )(dcat, o_all, gq, gk, gv, gg, bc, sp, wgn)


def _swa_mask(i):
    t = lax.broadcasted_iota(jnp.int32, (BLK, 3 * BLK), 0)
    c = lax.broadcasted_iota(jnp.int32, (BLK, 3 * BLK), 1)
    qpos = i * BLK + t - PAD_ROWS
    seg = c // BLK
    cc = c % BLK
    kpos = jnp.where(seg == 0, (i - 1) * BLK + cc, i * BLK + cc) - PAD_ROWS
    band = (seg < 2) & (kpos >= N_META) & (kpos <= qpos) & (qpos - kpos < BLK)
    midx = cc - PAD_ROWS
    meta = (seg == 2) & (midx >= 0) & (midx <= qpos)
    return band | meta


def _swa_keys(k_ref, i):
    prev = pl.multiple_of(jnp.maximum(i - 1, 0) * BLK, BLK)
    own = pl.multiple_of(i * BLK, BLK)
    return jnp.concatenate([k_ref[pl.ds(prev, BLK), :], k_ref[pl.ds(own, BLK), :], k_ref[0:BLK, :]], axis=0), prev, own


def _place(x, kv):
    if kv == 0:
        lo = jnp.where(_half_mask(128, 0), x, jnp.zeros_like(x))
        return lo, pltpu.roll(lo, 64, 1)
    hi = jnp.where(_half_mask(128, 1), x, jnp.zeros_like(x))
    return pltpu.roll(hi, 64, 1), hi


def _swa_fwd(sq, sk, sv, sinks, wn):
    rows = sq.shape[0]
    nb = rows // BLK
    scale = SWA_HD ** -0.5

    def body(q_ref, k_ref, v_ref, sink_ref, wn_ref, o_ref, cat_ref, lse_ref):
        i = pl.program_id(0)
        mask = _swa_mask(i)
        keys, _, _ = _swa_keys(k_ref, i)
        vals, _, _ = _swa_keys(v_ref, i)
        kz = (_place(keys, 0), _place(keys, 1))
        vz = (_place(vals, 0), _place(vals, 1))
        outs = []
        for pr in range(4):
            q_pair = q_ref[:, 128 * pr:128 * pr + 128]
            o_pair = jnp.zeros((BLK, 128), F32)
            for half in range(2):
                hd = 2 * pr + half
                kv = hd // 4
                sink = sink_ref[0, hd]
                s = jnp.where(mask, _nt(q_pair, kz[kv][half]) * scale, NEG_INF)
                m = jnp.maximum(jnp.max(s, axis=-1, keepdims=True), sink)
                e = jnp.exp(s - m)
                den = jnp.sum(e, axis=-1, keepdims=True) + jnp.exp(sink - m)
                lse_ref[:, hd:hd + 1] = m + jnp.log(den)
                o_pair = o_pair + _nn((e / den).astype(BF16), vz[kv][half])
            outs.append(o_pair)
        o = jnp.concatenate(outs, axis=1)
        o_ref[...] = o
        on, _ = _rms(o)
        cat_ref[...] = (on * wn_ref[...]).astype(BF16)

    return pl.pallas_call(
        body, name="swa_fwd", grid=(nb,),
        in_specs=[_row_spec(BLK, 512), VMEM_SPEC, VMEM_SPEC, SMEM_SPEC, VMEM_SPEC],
        out_specs=[_row_spec(BLK, 512), _row_spec(BLK, 512), _row_spec(BLK, SWA_HEADS)],
        out_shape=[jax.ShapeDtypeStruct((rows, 512), F32), jax.ShapeDtypeStruct((rows, 512), BF16),
                   jax.ShapeDtypeStruct((rows, SWA_HEADS), F32)],
        compiler_params=_params(("arbitrary",)),
    )(sq, sk, sv, sinks, wn)


def _swa_bwd(dcat, o_all, sq, sk, sv, lse, sinks, wn):
    rows = sq.shape[0]
    nb = rows // BLK
    scale = SWA_HD ** -0.5

    def body(dc_ref, o_ref, q_ref, k_ref, v_ref, lse_ref, sink_ref, wn_ref, dq_ref, dk_ref, dv_ref, dsink_ref, dwn_ref):
        i = pl.program_id(0)
        first = i == 0

        @pl.when(first)
        def _():
            dk_ref[...] = jnp.zeros_like(dk_ref)
            dv_ref[...] = jnp.zeros_like(dv_ref)
        mask = _swa_mask(i)
        keys, prev, own = _swa_keys(k_ref, i)
        vals, _, _ = _swa_keys(v_ref, i)
        kz = (_place(keys, 0), _place(keys, 1))
        vz = (_place(vals, 0), _place(vals, 1))
        o = o_ref[...]
        on, ro = _rms(o)
        dc = dc_ref[...]
        _acc_add(dwn_ref, first, _colsum(dc * on))
        do = _rms_bwd(on, ro, wn_ref[...], dc)
        do_o = do * o
        lane8 = lax.broadcasted_iota(jnp.int32, (1, 128), 1)
        dsink = jnp.zeros((1, 128), F32)
        dk_acc = jnp.zeros((3 * BLK, 128), F32)
        dv_acc = jnp.zeros((3 * BLK, 128), F32)
        dqs = []
        for pr in range(4):
            ps = slice(128 * pr, 128 * pr + 128)
            q_pair = q_ref[:, ps]
            do16 = do[:, ps].astype(BF16)
            dq_pair = jnp.zeros((BLK, 128), F32)
            for half in range(2):
                hd = 2 * pr + half
                kv = hd // 4
                hm = _half_mask(128, half)
                sink = sink_ref[0, hd]
                lse_h = lse_ref[:, hd:hd + 1]
                delta = jnp.sum(jnp.where(hm, do_o[:, ps], 0.0), axis=-1, keepdims=True)
                s = jnp.where(mask, _nt(q_pair, kz[kv][half]) * scale, NEG_INF)
                prob = jnp.exp(s - lse_h)
                dsink = dsink + jnp.where(lane8 == hd, -jnp.sum(jnp.exp(sink - lse_h) * delta), 0.0)
                dp = _nt(do16, vz[kv][half])
                ds16 = (prob * (dp - delta) * scale).astype(BF16)
                dq_pair = dq_pair + _nn(ds16, kz[kv][half])
                dkz = jnp.where(hm, _tn(ds16, q_pair), 0.0)
                dvz = jnp.where(hm, _tn(prob.astype(BF16), do16), 0.0)
                if half != kv:
                    dkz = pltpu.roll(dkz, 64, 1)
                    dvz = pltpu.roll(dvz, 64, 1)
                dk_acc = dk_acc + dkz
                dv_acc = dv_acc + dvz
            dqs.append(dq_pair)
        dq_ref[...] = jnp.concatenate(dqs, axis=1)
        _acc_add(dsink_ref, first, dsink)
        for ref, acc in ((dk_ref, dk_acc), (dv_ref, dv_acc)):
            ref[pl.ds(prev, BLK), :] += acc[0:BLK]
            ref[pl.ds(own, BLK), :] += acc[BLK:2 * BLK]
            ref[0:BLK, :] += acc[2 * BLK:3 * BLK]

    full = pl.BlockSpec((rows, 128), lambda i: (0, 0))
    return pl.pallas_call(
        body, name="swa_bwd", grid=(nb,),
        in_specs=[_row_spec(BLK, 512), _row_spec(BLK, 512), _row_spec(BLK, 512), VMEM_SPEC, VMEM_SPEC,
                  _row_spec(BLK, SWA_HEADS), SMEM_SPEC, VMEM_SPEC],
        out_specs=[_row_spec(BLK, 512), full, full, _acc_spec(128), _acc_spec(512)],
        out_shape=[jax.ShapeDtypeStruct((rows, 512), F32), jax.ShapeDtypeStruct((rows, 128), F32),
                   jax.ShapeDtypeStruct((rows, 128), F32), jax.ShapeDtypeStruct((8, 128), F32),
                   jax.ShapeDtypeStruct((8, 512), F32)],
        compiler_params=_params(("arbitrary",)),
    )(dcat, o_all, sq, sk, sv, lse, sinks, wn)


def _mix_out(h, cat_g, cat_s, wout, gpost):
    rows = h.shape[0]
    tm = _row_tile(rows)

    def body(h_ref, cg_ref, cs_ref, w_ref, g_ref, ho_ref, m_ref):
        m = _nn(cg_ref[...], w_ref[0:512, :]) + _nn(cs_ref[...], w_ref[512:1024, :])
        m_ref[...] = m
        mn, _ = _rms(m)
        ho_ref[...] = h_ref[...] + mn * g_ref[...]

    row_f32 = _row_spec(tm, D_MODEL)
    return pl.pallas_call(
        body, name="mix_out", grid=(rows // tm,),
        in_specs=[row_f32, _row_spec(tm, 512), _row_spec(tm, 512), VMEM_SPEC, VMEM_SPEC],
        out_specs=[row_f32, row_f32],
        out_shape=[jax.ShapeDtypeStruct((rows, D_MODEL), F32), jax.ShapeDtypeStruct((rows, D_MODEL), F32)],
        compiler_params=_params(("arbitrary",)),
    )(h, cat_g, cat_s, wout, gpost)


def _mix_out_bwd(dh, m, wout, gpost):
    rows = dh.shape[0]
    tm = _row_tile(rows)

    def body(dh_ref, m_ref, w_ref, g_ref, dcg_ref, dcs_ref, dm_ref, dg_ref):
        first = pl.program_id(0) == 0
        dhv = dh_ref[...]
        mn, rm = _rms(m_ref[...])
        _acc_add(dg_ref, first, _colsum(dhv * mn))
        dm16 = _rms_bwd(mn, rm, g_ref[...], dhv).astype(BF16)
        dm_ref[...] = dm16
        dcat = _nt(dm16, w_ref[...])
        dcg_ref[...] = dcat[:, 0:512]
        dcs_ref[...] = dcat[:, 512:1024]

    row_f32 = _row_spec(tm, D_MODEL)
    return pl.pallas_call(
        body, name="mix_out_bwd", grid=(rows // tm,),
        in_specs=[row_f32, row_f32, VMEM_SPEC, VMEM_SPEC],
        out_specs=[_row_spec(tm, 512), _row_spec(tm, 512), row_f32, _acc_spec(D_MODEL)],
        out_shape=[jax.ShapeDtypeStruct((rows, 512), F32), jax.ShapeDtypeStruct((rows, 512), F32),
                   jax.ShapeDtypeStruct((rows, D_MODEL), BF16), jax.ShapeDtypeStruct((8, D_MODEL), F32)],
        compiler_params=_params(("arbitrary",)),
    )(dh, m, wout, gpost)


def _mix_in_bwd(dh_out, h, g, win_p, wa2_p, cos, sin, loga, ga, dgq, dgk, dgv, dgg, dsq, dsk, dsv, dloga):
    rows = h.shape[0]
    tm = _row_tile(rows)

    def body(dho_ref, h_ref, g_ref, win_ref, wa2_ref, cos_ref, sin_ref, loga_ref, ga_ref,
             dgq_ref, dgk_ref, dgv_ref, dgg_ref, dsq_ref, dsk_ref, dsv_ref, dla_ref,
             dh_ref, dproj_ref, dwa2_ref, dg_ref, dba_ref):
        first = pl.program_id(0) == 0
        dz = dla_ref[...] * (1.0 / GLA_TAU) * (1.0 - jnp.exp(GLA_TAU * loga_ref[...]))
        _acc_add(dba_ref, first, _colsum(dz))
        dga = _nt(dz, wa2_ref[...])
        pa = _tn(ga_ref[...], dz)
        c1, s1 = cos_ref[...], sin_ref[...]
        c4 = jnp.concatenate([c1, c1, c1, c1], axis=1)
        s4 = jnp.concatenate([s1, s1, s1, s1], axis=1)
        dq_r, dk_r = dsq_ref[...], dsk_ref[...]
        dsq = dq_r * c4 - _rot_half(dq_r * s4)
        dsk = dk_r * c1 - _rot_half(dk_r * s1)
        dproj16 = jnp.concatenate(
            [dgq_ref[...], dgk_ref[...], dgv_ref[...], dgg_ref[...], dsq, dsk, dsv_ref[...], dga], axis=1).astype(BF16)
        dproj_ref[...] = dproj16
        dn = _nn(dproj16, win_ref[...])

        @pl.when(first)
        def _():
            dwa2_ref[...] = pa

        @pl.when(jnp.logical_not(first))
        def _():
            dwa2_ref[...] += pa
        hn, rh = _rms(h_ref[...])
        _acc_add(dg_ref, first, _colsum(dn * hn))
        dh_ref[...] = dho_ref[...] + _rms_bwd(hn, rh, g_ref[...], dn)

    rs = lambda c: _row_spec(tm, c)
    return pl.pallas_call(
        body, name="mix_in_bwd", grid=(rows // tm,),
        in_specs=[rs(D_MODEL), rs(D_MODEL), VMEM_SPEC, VMEM_SPEC, VMEM_SPEC, rs(128), rs(128), rs(256), rs(128),
                  rs(256), rs(256), rs(512), rs(512), rs(512), rs(128), rs(128), rs(256)],
        out_specs=[rs(D_MODEL), rs(P_END), pl.BlockSpec((128, 256), lambda i: (0, 0)), _acc_spec(D_MODEL), _acc_spec(256)],
        out_shape=[jax.ShapeDtypeStruct((rows, D_MODEL), F32), jax.ShapeDtypeStruct((rows, P_END), BF16),
                   jax.ShapeDtypeStruct((128, 256), F32), jax.ShapeDtypeStruct((8, D_MODEL), F32),
                   jax.ShapeDtypeStruct((8, 256), F32)],
        compiler_params=_params(("arbitrary",)),
    )(dh_out, h, g, win_p, wa2_p, cos, sin, loga, ga, dgq, dgk, dgv, dgg, dsq, dsk, dsv, dloga)


def _rope_tables(rows):
    pos = (jnp.arange(rows, dtype=jnp.int32) - PAD_ROWS).astype(F32)
    inv_freq = 1.0 / (ROPE_THETA ** (jnp.arange(0, SWA_HD, 2, dtype=F32) / SWA_HD))
    ang = pos[:, None] * inv_freq[None, :]
    ang = jnp.concatenate([ang, ang, ang, ang], axis=-1)
    return jnp.cos(ang), jnp.sin(ang)


def _local_step(h0, tgt, w, late_weights=None, on_grads=None):
    rows = h0.shape[0]
    cos, sin = _rope_tables(rows)
    g = {}

    def tell(group, names):
        for nm in names:
            g[nm] = grads_now[nm]
        return 0.0 if on_grads is None else on_grads(group, {nm: grads_now[nm] for nm in names})

    h1, a1, b1, f1 = _ffn_fwd(h0, w["ffn1_pre"], w["wg1"], w["wu1"], w["wd1"], w["ffn1_post"])
    if late_weights is not None:
        w = {**w, **late_weights("win", f1)}
    gq, gk, gv, gg, sq, sk, sv, ga, loga, bc, n2 = _mix_in(h1, w["mix_pre"], w["win"], w["wa2"], w["b_a"], cos, sin)
    o_g, cat_g, sp = _gla_fwd(gq, gk, gv, gg, bc, w["gla_norm"])
    o_s, cat_s, lse = _swa_fwd(sq, sk, sv, w["sinks"], w["swa_norm"])
    if late_weights is not None:
        w = {**w, **late_weights("rest", lse)}
    h2, m = _mix_out(h1, cat_g, cat_s, w["wout"], w["mix_post"])
    h3, a2, b2, f2, dy, loss = _ffn_fwd(h2, w["ffn2_pre"], w["wg2"], w["wu2"], w["wd2"], w["ffn2_post"], tgt)
    del h3
    dh2, da, db, df, n3, g["ffn2_pre"], g["ffn2_post"] = _ffn_bwd_act(
        dy, h2, a2, b2, f2, w["ffn2_pre"], w["ffn2_post"], w["wg2"], w["wu2"], w["wd2"], "ffn2_bwd_act")
    grads_now = dict(wd2=_wgrad(b2, df, "ffn2_wgrad_down", gate=a2), wg2=_wgrad(da, n3, "ffn2_wgrad_gate"),
                     wu2=_wgrad(db, n3, "ffn2_wgrad_up"))
    tok = tell("ffn2", ("wd2", "wg2", "wu2"))
    dcg, dcs, dm, g["mix_post"] = _mix_out_bwd(dh2, m, w["wout"], w["mix_post"] + tok)
    dsq, dsk, dsv, g["sinks"], g["swa_norm"] = _swa_bwd(dcs, o_s, sq, sk, sv, lse, w["sinks"], w["swa_norm"])
    dgq, dgk, dgv, dgg, dloga, g["gla_norm"] = _gla_bwd(dcg, o_g, gq, gk, gv, gg, bc, sp, w["gla_norm"])
    dh1, dproj, g["wa2"], g["mix_pre"], g["b_a"] = _mix_in_bwd(
        dh2, h1, w["mix_pre"], w["win"], w["wa2"], cos, sin, loga, ga, dgq, dgk, dgv, dgg, dsq, dsk, dsv, dloga)
    grads_now = dict(wout=jnp.concatenate([_wgrad(cat_g, dm, "wout_wgrad_gla"), _wgrad(cat_s, dm, "wout_wgrad_swa")], axis=0),
                     win=_wgrad(dproj, n2, "win_wgrad"))
    tok = tell("mix", ("wout", "win"))
    dh0, da, db, df, n1, g["ffn1_pre"], g["ffn1_post"] = _ffn_bwd_act(
        dh1, h0, a1, b1, f1, w["ffn1_pre"] + tok, w["ffn1_post"], w["wg1"], w["wu1"], w["wd1"], "ffn1_bwd_act")
    grads_now = dict(wd1=_wgrad(b1, df, "ffn1_wgrad_down", gate=a1))
    tell("ffn1_down", ("wd1",))
    grads_now = dict(wg1=_wgrad(da, n1, "ffn1_wgrad_gate"))
    tell("ffn1_gate", ("wg1",))
    grads_now = dict(wu1=_wgrad(db, n1, "ffn1_wgrad_up"))
    tell("ffn1_up", ("wu1",))
    return loss[0, 0], dh0, g


def _win_pad_rows(win_t):
    pad = jnp.zeros((P_END - P_GA - 16, win_t.shape[1]), win_t.dtype)
    return jnp.concatenate([win_t[0:1536], win_t[1552:2320], win_t[1536:1552], pad], axis=0)


def _win_unpad_rows(win_p):
    return jnp.concatenate([win_p[0:1536], win_p[P_GA:P_GA + 16], win_p[1536:P_GA]], axis=0)


def _place_on_mesh():
    return lax.axis_index("x"), lax.axis_index("y"), lax.axis_index("c")


def _dev_index(px, py, pc):
    return 4 * px + 2 * py + pc


def _other_devices(x, y, c):
    flip = lambda v, f: 1 - v if f else v
    return [(flip(x, fx), flip(y, fy), flip(c, fc)) for fx in (0, 1) for fy in (0, 1) for fc in (0, 1)][1:]


def _all_gather(shards):
    n = len(shards)

    def body(*refs):
        ins, outs = refs[:n], refs[n:2 * n]
        zeros_ref, send_sems, recv_sems, local_sems = refs[2 * n:]
        zeros_ref[...] = jnp.zeros_like(zeros_ref)
        x, y, c = _place_on_mesh()
        me, sibling = (x, y, c), (x, y, 1 - c)
        chips = [(1 - x, y), (x, 1 - y), (1 - x, 1 - y)]

        def rows(k, px, py, pc):
            r = ins[k].shape[0]
            return outs[k].at[pl.ds(pl.multiple_of(_dev_index(px, py, pc) * r, 8), r), :]

        def copy(k, slot, block, to, src=None):
            return pltpu.make_async_remote_copy(
                src_ref=rows(k, *block) if src is None else src, dst_ref=rows(k, *block),
                send_sem=send_sems.at[k, slot], recv_sem=recv_sems.at[k, slot], device_id=to, device_id_type=MESH)

        local = [pltpu.make_async_copy(ins[k], rows(k, *me), local_sems.at[k]) for k in range(n)]
        sends = []
        for k in range(n):
            local[k].start()
            sends.append(copy(k, 0, me, sibling, src=ins[k]))
            sends += [copy(k, 1 + j, me, (*chip, c), src=ins[k]) for j, chip in enumerate(chips)]
        for cp in sends:
            cp.start()
        for k in range(n):
            for j, chip in enumerate(chips):
                copy(k, 1 + j, (*chip, c), me).wait_recv()
                passed = copy(k, 4 + j, (*chip, c), sibling)
                passed.start()
                sends.append(passed)
        for k in range(n):
            copy(k, 0, sibling, me).wait_recv()
            for j, chip in enumerate(chips):
                copy(k, 4 + j, (*chip, 1 - c), me).wait_recv()
        for cp in sends:
            cp.wait_send()
        for cp in local:
            cp.wait()

    return pl.pallas_call(
        body, name="all_gather_weights",
        in_specs=[ANY_SPEC] * n, out_specs=[ANY_SPEC] * n + [VMEM_SPEC],
        out_shape=[jax.ShapeDtypeStruct((N_DEV * s.shape[0], s.shape[1]), s.dtype) for s in shards]
        + [jax.ShapeDtypeStruct((8, 128), F32)],
        scratch_shapes=[pltpu.SemaphoreType.DMA((n, 7)), pltpu.SemaphoreType.DMA((n, 7)), pltpu.SemaphoreType.DMA((n,))],
    )(*shards)


HBM_SPEC = pl.BlockSpec(memory_space=pltpu.HBM)
SEM_SPEC = pl.BlockSpec(memory_space=pltpu.SEMAPHORE)
DATAFLOW = pltpu.SideEffectType.DATAFLOW_SIDE_EFFECTING


def _exchange_copies(srcs, lands, send_sems, recv_sems, scatter, arriving):
    x, y, c = _place_on_mesh()
    me = _dev_index(x, y, c)
    out = []
    for k, (src, land) in enumerate(zip(srcs, lands)):
        r = land.shape[0] // N_DEV

        def block(ref, d):
            return ref.at[pl.ds(pl.multiple_of(d * r, 8), r), :]

        for f, peer in enumerate(_other_devices(x, y, c)):
            mine, his = (_dev_index(*peer), me) if arriving else (me, _dev_index(*peer))
            out.append(pltpu.make_async_remote_copy(
                src_ref=block(src, his) if scatter else src, dst_ref=block(land, mine),
                send_sem=send_sems.at[7 * k + f], recv_sem=recv_sems.at[7 * k + f], device_id=peer, device_id_type=MESH))
    return out


def _exchange_start(srcs, lands, scatter, name):
    n = len(srcs)

    def body(*refs):
        send_sems, recv_sems = refs[2 * n], refs[2 * n + 1]
        for going in _exchange_copies(refs[:n], refs[n:2 * n], send_sems, recv_sems, scatter, False):
            going.start()
        refs[-1][...] = jnp.zeros_like(refs[-1])

    both = list(srcs) + list(lands)
    outs = pl.pallas_call(
        body, name=name,
        out_shape=(pltpu.SemaphoreType.DMA((7 * n,)), pltpu.SemaphoreType.DMA((7 * n,)),
                   *[pltpu.HBM(a.shape, a.dtype) for a in both], jax.ShapeDtypeStruct((8, 128), F32)),
        in_specs=[HBM_SPEC] * (2 * n), out_specs=(SEM_SPEC, SEM_SPEC, *[HBM_SPEC] * (2 * n), VMEM_SPEC),
        input_output_aliases={i: 2 + i for i in range(2 * n)},
        compiler_params=pltpu.CompilerParams(has_side_effects=DATAFLOW),
    )(*[pltpu.with_memory_space_constraint(a, pltpu.HBM) for a in both])
    return outs[0], outs[1], outs[2:2 + n], outs[2 + n:2 + 2 * n], outs[-1]


def _exchange_wait(started, scatter, after, name):
    send_sems, recv_sems, srcs, lands, _ = started
    n = len(srcs)

    def body(*refs):
        args = (refs[:n], refs[n:2 * n], refs[2 * n], refs[2 * n + 1], scatter)
        for going in _exchange_copies(*args, False):
            going.wait_send()
        for coming in _exchange_copies(*args, True):
            coming.wait_recv()

    both = list(srcs) + list(lands)
    outs = pl.pallas_call(
        body, name=name, out_shape=[pltpu.HBM(a.shape, a.dtype) for a in both],
        in_specs=[HBM_SPEC] * (2 * n) + [SEM_SPEC, SEM_SPEC, ANY_SPEC], out_specs=[HBM_SPEC] * (2 * n),
        input_output_aliases={i: i for i in range(2 * n)},
        compiler_params=pltpu.CompilerParams(has_side_effects=DATAFLOW),
    )(*both, send_sems, recv_sems, after)
    return outs[n:]


def _own_block_placed(block, rows, dev):
    zone = lax.empty((N_DEV * rows, block.shape[1]), block.dtype)
    return lax.dynamic_update_slice(zone, block, (dev * rows, 0))


def _sum_partials(parts, name):
    n = len(parts)

    def body(*refs):
        ins, outs = refs[:n], refs[n:]
        first = pl.program_id(0) == 0
        for i_ref, o_ref in zip(ins, outs):
            v = i_ref[...].astype(F32)

            @pl.when(first)
            def _():
                o_ref[...] = v

            @pl.when(jnp.logical_not(first))
            def _():
                o_ref[...] += v

    shapes = [(p.shape[0] // N_DEV, p.shape[1]) for p in parts]
    return pl.pallas_call(
        body, name=name, grid=(N_DEV,),
        in_specs=[pl.BlockSpec(s, lambda j: (j, 0)) for s in shapes],
        out_specs=[pl.BlockSpec(s, lambda j: (0, 0)) for s in shapes],
        out_shape=[jax.ShapeDtypeStruct(s, F32) for s in shapes],
        compiler_params=_params(("arbitrary",)),
    )(*parts)


def _all_reduce_small(slab):
    rows, cols = slab.shape

    def body(x_ref, o_ref, gathered, send_sems, recv_sems):
        x, y, c = _place_on_mesh()
        me = _dev_index(x, y, c)
        peers = _other_devices(x, y, c)

        def copy(f, peer):
            return pltpu.make_async_remote_copy(
                src_ref=x_ref, dst_ref=gathered.at[me], send_sem=send_sems.at[f], recv_sem=recv_sems.at[f],
                device_id=peer, device_id_type=MESH)

        def arrival(f, peer):
            return pltpu.make_async_remote_copy(
                src_ref=x_ref, dst_ref=gathered.at[_dev_index(*peer)], send_sem=send_sems.at[f], recv_sem=recv_sems.at[f],
                device_id=peer, device_id_type=MESH)

        sends = [copy(f, peer) for f, peer in enumerate(peers)]
        for cp in sends:
            cp.start()
        gathered[me] = x_ref[...]
        for f, peer in enumerate(peers):
            arrival(f, peer).wait_recv()
        for cp in sends:
            cp.wait_send()
        total = gathered[0]
        for d in range(1, N_DEV):
            total = total + gathered[d]
        o_ref[...] = total

    return pl.pallas_call(
        body, name="all_reduce_small",
        in_specs=[VMEM_SPEC], out_specs=VMEM_SPEC, out_shape=jax.ShapeDtypeStruct((rows, cols), F32),
        scratch_shapes=[pltpu.VMEM((N_DEV, rows, cols), F32), pltpu.SemaphoreType.DMA((7,)), pltpu.SemaphoreType.DMA((7,))],
    )(slab)


def _adamw(ws, gs, ms, vs, name):
    n = len(ws)
    c1 = 1.0 / (1.0 - ADAM_B1 ** ADAM_STEP)
    c2 = 1.0 / (1.0 - ADAM_B2 ** ADAM_STEP)

    def body(*refs):
        w_r, g_r, m_r, v_r = refs[:n], refs[n:2 * n], refs[2 * n:3 * n], refs[3 * n:4 * n]
        d_o, m_o, v_o = refs[4 * n:5 * n], refs[5 * n:6 * n], refs[6 * n:7 * n]
        for k in range(n):
            g = g_r[k][...]
            m = ADAM_B1 * m_r[k][...] + (1.0 - ADAM_B1) * g
            v = ADAM_B2 * v_r[k][...] + (1.0 - ADAM_B2) * (g * g)
            m_o[k][...] = m
            v_o[k][...] = v
            d_o[k][...] = -ADAM_LR * ((m * c1) / (jnp.sqrt(v * c2) + ADAM_EPS) + ADAM_WD * w_r[k][...])

    shapes = [jax.ShapeDtypeStruct(w.shape, F32) for w in ws]
    outs = pl.pallas_call(
        body, name=name, in_specs=[VMEM_SPEC] * (4 * n), out_specs=[VMEM_SPEC] * (3 * n), out_shape=shapes * 3,
        compiler_params=pltpu.CompilerParams(vmem_limit_bytes=56 << 20),
    )(*ws, *gs, *ms, *vs)
    return outs[:n], outs[n:2 * n], outs[2 * n:]


WEIGHT_NAMES = ("meta_tokens", "ffn1_pre_norm", "ffn1_w_gate", "ffn1_w_up", "ffn1_w_down", "ffn1_post_norm", "mix_pre_norm",
                "w_in", "gla_w_a2", "gla_b_a", "gla_out_norm", "swa_sinks", "swa_out_norm", "w_out", "mix_post_norm",
                "ffn2_pre_norm", "ffn2_w_gate", "ffn2_w_up", "ffn2_w_down", "ffn2_post_norm")
WIN_SHARD = D_IN // N_DEV
WIN_SHARD_PAD = 304
SLAB_VECTORS = ("ffn1_pre", "ffn1_post", "mix_pre", "mix_post", "ffn2_pre", "ffn2_post")
SLAB_ROWS = 32


def kernel(x, meta_tokens, ffn1_pre_norm, ffn1_w_gate, ffn1_w_up, ffn1_w_down, ffn1_post_norm, mix_pre_norm, w_in, gla_w_a2, gla_b_a, gla_out_norm, swa_sinks, swa_out_norm, w_out, mix_post_norm, ffn2_pre_norm, ffn2_w_gate, ffn2_w_up, ffn2_w_down, ffn2_post_norm, loss_target, m_meta_tokens, m_ffn1_pre_norm, m_ffn1_w_gate, m_ffn1_w_up, m_ffn1_w_down, m_ffn1_post_norm, m_mix_pre_norm, m_w_in, m_gla_w_a2, m_gla_b_a, m_gla_out_norm, m_swa_sinks, m_swa_out_norm, m_w_out, m_mix_post_norm, m_ffn2_pre_norm, m_ffn2_w_gate, m_ffn2_w_up, m_ffn2_w_down, m_ffn2_post_norm, v_meta_tokens, v_ffn1_pre_norm, v_ffn1_w_gate, v_ffn1_w_up, v_ffn1_w_down, v_ffn1_post_norm, v_mix_pre_norm, v_w_in, v_gla_w_a2, v_gla_b_a, v_gla_out_norm, v_swa_sinks, v_swa_out_norm, v_w_out, v_mix_post_norm, v_ffn2_pre_norm, v_ffn2_w_gate, v_ffn2_w_up, v_ffn2_w_down, v_ffn2_post_norm):
    given = dict(locals())
    W = {n: given[n] for n in WEIGHT_NAMES}
    M = {n: given["m_" + n] for n in WEIGHT_NAMES}
    V = {n: given["v_" + n] for n in WEIGHT_NAMES}
    dev = _dev_index(*_place_on_mesh())

    def t16(w):
        return w[0].T.astype(BF16)

    small = jnp.concatenate([W["meta_tokens"], jnp.pad(W["gla_w_a2"][0], ((0, 0), (0, 96)))], axis=0)
    wg1, wu1, wd1, small_g, gathered_zeros = _all_gather(
        [t16(W["ffn1_w_gate"]), t16(W["ffn1_w_up"]), W["ffn1_w_down"][0].astype(BF16), small])
    def after_zero(shard, zeros):
        return shard + zeros[0:1, 0:1].astype(shard.dtype)
    win_shard = jnp.pad(t16(W["w_in"]), ((0, WIN_SHARD_PAD - WIN_SHARD), (0, 0)))
    win_shard = after_zero(win_shard, gathered_zeros)
    mid = _exchange_start([win_shard], [_own_block_placed(win_shard, WIN_SHARD_PAD, dev)], False, "gather_w_in_start")
    late_shards = [after_zero(W["w_out"][0].astype(BF16), mid[4]), t16(W["ffn2_w_gate"]), t16(W["ffn2_w_up"]),
                   W["ffn2_w_down"][0].astype(BF16)]
    late = _exchange_start(late_shards, [_own_block_placed(s, s.shape[0], dev) for s in late_shards], False,
                           "gather_late_weights_start")

    def late_weights(what, after):
        if what == "win":
            win_g, = _exchange_wait(mid, False, after, "gather_w_in_wait")
            win_t = win_g.reshape(N_DEV, WIN_SHARD_PAD, D_MODEL)[:, :WIN_SHARD].reshape(D_IN, D_MODEL)
            return dict(win=_win_pad_rows(win_t))
        wout, wg2, wu2, wd2 = _exchange_wait(late, False, after, "gather_late_weights_wait")
        return dict(wout=wout, wg2=wg2, wu2=wu2, wd2=wd2)

    small_g = small_g.reshape(N_DEV, 32, 128)
    meta_full = small_g[:, :N_META].transpose(1, 0, 2).reshape(N_META, D_MODEL)
    wa2_full = small_g[:, N_META:, :32].transpose(1, 0, 2).reshape(16, 256)
    w = dict(
        ffn1_pre=W["ffn1_pre_norm"] + late[4][0, 0], ffn1_post=W["ffn1_post_norm"], mix_pre=W["mix_pre_norm"],
        mix_post=W["mix_post_norm"], ffn2_pre=W["ffn2_pre_norm"], ffn2_post=W["ffn2_post_norm"], b_a=W["gla_b_a"],
        gla_norm=W["gla_out_norm"], sinks=W["swa_sinks"], swa_norm=W["swa_out_norm"], wg1=wg1, wu1=wu1, wd1=wd1,
        wa2=jnp.pad(wa2_full, ((0, 112), (0, 0))))

    in_flight = []

    def on_grads(group, grads):
        parts = []
        for nm, p in grads.items():
            if nm == "win":
                p = _win_unpad_rows(p).reshape(N_DEV, WIN_SHARD, D_MODEL)
                p = jnp.pad(p, ((0, 0), (0, WIN_SHARD_PAD - WIN_SHARD), (0, 0))).reshape(N_DEV * WIN_SHARD_PAD, D_MODEL)
            parts.append(p)
        lands = [_own_block_placed(lax.dynamic_slice_in_dim(p, dev * (p.shape[0] // N_DEV), p.shape[0] // N_DEV, axis=0),
                                   p.shape[0] // N_DEV, dev) for p in parts]
        started = _exchange_start(parts, lands, True, "scatter_" + group + "_start")
        in_flight.append((group, list(grads), started))
        return started[4][0, 0]

    front = jnp.zeros((PAD_ROWS, D_MODEL), F32)
    h0 = jnp.concatenate([front, meta_full, x[0]], axis=0)
    tgt = jnp.concatenate([jnp.zeros((BLK, D_MODEL), F32), loss_target[0]], axis=0)
    loss, dh0, g = _local_step(h0, tgt, w, late_weights, on_grads)
    grad_x = dh0[BLK:][None]

    packed = jnp.concatenate([g["b_a"][0:1], g["gla_norm"][0:1], g["sinks"][0:1], g["swa_norm"][0:1]], axis=1)
    slab = jnp.concatenate([g[k][0:1] for k in SLAB_VECTORS] + [packed, jnp.full((1, D_MODEL), loss, F32),
                           g["wa2"][:16].reshape(4, D_MODEL), jnp.zeros((4, D_MODEL), F32), dh0[PAD_ROWS:BLK]], axis=0)
    tot = _all_reduce_small(slab)
    loss = tot[7, 0]
    small_grads = dict(
        ffn1_pre_norm=tot[0:1], ffn1_post_norm=tot[1:2], mix_pre_norm=tot[2:3], mix_post_norm=tot[3:4],
        ffn2_pre_norm=tot[4:5], ffn2_post_norm=tot[5:6], gla_b_a=tot[6:7, 0:256], gla_out_norm=tot[6:7, 256:384],
        swa_sinks=tot[6:7, 384:392], swa_out_norm=tot[6:7, 512:1024],
        gla_w_a2=lax.dynamic_slice_in_dim(tot[8:12].reshape(16, 256), dev * 32, 32, axis=1)[None],
        meta_tokens=lax.dynamic_slice_in_dim(tot[16:32], dev * 128, 128, axis=1))

    last_started = in_flight[-1][2][4]
    sums = {}
    for group, names, started in in_flight:
        lands = _exchange_wait(started, True, last_started, "scatter_" + group + "_wait")
        for nm, s in zip(names, _sum_partials(lands, "sum_" + group)):
            sums[nm] = s
    big = dict(ffn1_w_gate=sums["wg1"].T[None], ffn1_w_up=sums["wu1"].T[None], ffn1_w_down=sums["wd1"][None],
               w_in=sums["win"][:WIN_SHARD].T[None], w_out=sums["wout"][None],
               ffn2_w_gate=sums["wg2"].T[None], ffn2_w_up=sums["wu2"].T[None], ffn2_w_down=sums["wd2"][None])
    grads = {**big, **small_grads}

    delta, new_m, new_v = {}, {}, {}
    for n in big:
        d_, m_, v_ = _adamw([W[n][0]], [grads[n][0]], [M[n][0]], [V[n][0]], "adamw_" + n)
        delta[n], new_m[n], new_v[n] = d_[0][None], m_[0][None], v_[0][None]
    names = [n for n in WEIGHT_NAMES if n not in big]
    two_d = lambda a: a.reshape(-1, a.shape[-1])
    d_, m_, v_ = _adamw([two_d(W[n]) for n in names], [two_d(grads[n]) for n in names],
                        [two_d(M[n]) for n in names], [two_d(V[n]) for n in names], "adamw_small")
    for k, n in enumerate(names):
        delta[n], new_m[n], new_v[n] = d_[k].reshape(W[n].shape), m_[k].reshape(W[n].shape), v_[k].reshape(W[n].shape)
    return (loss, grad_x, *[grads[n] for n in WEIGHT_NAMES], *[delta[n] for n in WEIGHT_NAMES],
            *[new_m[n] for n in WEIGHT_NAMES], *[new_v[n] for n in WEIGHT_NAMES])
```

```python
import functools

import jax
import jax.numpy as jnp
from jax import lax
from jax.experimental import pallas as pl
from jax.experimental.pallas import tpu as pltpu

F32, BF16 = jnp.float32, jnp.bfloat16

D_MODEL = 1024
D_FF = 2816
N_META = 16
BLK = 128
PAD_ROWS = BLK - N_META
GLA_DK = 64
SWA_HD = 64
SWA_HEADS = 8
GLA_TAU = 16.0
NORM_EPS = 1e-6
NEG_INF = -1e30
ROPE_THETA = 10000.0
P_GQ, P_GK, P_GV, P_GG, P_SQ, P_SK, P_SV, P_GA, P_END = 0, 256, 512, 1024, 1536, 2048, 2176, 2304, 2432
D_IN = 2320
IN_SPLITS = (256, 256, 512, 512, 16, 512, 128, 128)
FF_TILE = 2816
WGRAD_TILE_MAX = 2432
N_DEV = 8
MESH = pl.DeviceIdType.MESH

ADAM_LR, ADAM_B1, ADAM_B2, ADAM_EPS, ADAM_WD, ADAM_STEP = 0.001, 0.9, 0.999, 1e-08, 0.01, 10

V7X_VMEM_BYTES = 64 << 20
VMEM_SPEC = pl.BlockSpec(memory_space=pltpu.VMEM)
SMEM_SPEC = pl.BlockSpec(memory_space=pltpu.SMEM)
ANY_SPEC = pl.BlockSpec(memory_space=pl.ANY)


def _params(semantics, vmem_mb=56):
    return pltpu.CompilerParams(dimension_semantics=semantics, vmem_limit_bytes=vmem_mb << 20)


def _row_tile(rows):
    return 320 if rows % 320 == 0 else BLK


def _nn(a, b):
    return lax.dot_general(a, b, (((1,), (0,)), ((), ())), preferred_element_type=F32)


def _nt(a, b):
    return lax.dot_general(a, b, (((1,), (1,)), ((), ())), preferred_element_type=F32)


def _tn(a, b):
    return lax.dot_general(a, b, (((0,), (0,)), ((), ())), preferred_element_type=F32)


def _rms(x):
    r = lax.rsqrt(jnp.mean(x * x, axis=-1, keepdims=True) + NORM_EPS)
    return x * r, r


def _rms_bwd(xn, r, w, dy):
    g = dy * w
    return r * (g - xn * jnp.mean(g * xn, axis=-1, keepdims=True))


def _sigmoid(x):
    return 1.0 / (1.0 + jnp.exp(-x))


def _colsum(x):
    return jnp.sum(x, axis=0, keepdims=True)


def _split_bf16(x):
    hi = x.astype(BF16)
    lo = (x - hi.astype(F32)).astype(BF16)
    return hi, lo


def _tri(lower):
    r = lax.broadcasted_iota(jnp.int32, (BLK, BLK), 0)
    c = lax.broadcasted_iota(jnp.int32, (BLK, BLK), 1)
    return (r >= c) if lower else (c >= r)


def _half_mask(width, half):
    lane = lax.broadcasted_iota(jnp.int32, (1, width), 1)
    return ((lane % 128) < 64) if half == 0 else ((lane % 128) >= 64)


def _rot_half(x):
    w = x.shape[-1]
    lane = lax.broadcasted_iota(jnp.int32, (1, w), 1)
    return jnp.where((lane % SWA_HD) < SWA_HD // 2, -pltpu.roll(x, w - SWA_HD // 2, 1), pltpu.roll(x, SWA_HD // 2, 1))


def _row_spec(tm, cols):
    return pl.BlockSpec((tm, cols), lambda i: (i, 0))


def _acc_spec(cols):
    return pl.BlockSpec((8, cols), lambda i: (0, 0))


def _acc_add(ref, first, value):
    @pl.when(first)
    def _():
        ref[...] = jnp.zeros_like(ref)
    ref[0:1, :] += value


def _ffn_fwd(h, gpre, wg_t, wu_t, wd, gpost, tgt=None):
    rows = h.shape[0]
    tm = _row_tile(rows)
    nf = D_FF // FF_TILE
    with_loss = tgt is not None

    def body(*refs):
        if with_loss:
            (h_ref, gpre_ref, wg_ref, wu_ref, wd_ref, gpost_ref, t_ref,
             ho_ref, a_ref, b_ref, f_ref, dy_ref, loss_ref, acc) = refs
        else:
            (h_ref, gpre_ref, wg_ref, wu_ref, wd_ref, gpost_ref, ho_ref, a_ref, b_ref, f_ref, acc) = refs
        i = pl.program_id(0)
        h_in = h_ref[...]
        hn, _ = _rms(h_in)
        n16 = (hn * gpre_ref[...]).astype(BF16)
        for j in range(nf):
            cols = slice(j * FF_TILE, (j + 1) * FF_TILE)
            a = _nt(n16, wg_ref[cols, :])
            b = _nt(n16, wu_ref[cols, :])
            a_ref[:, cols] = a.astype(BF16)
            b_ref[:, cols] = b.astype(BF16)
            s16 = (a * _sigmoid(a) * b).astype(BF16)
            part = _nn(s16, wd_ref[cols, :])
            if j == 0:
                acc[...] = part
            else:
                acc[...] += part
        f = acc[...]
        f_ref[...] = f
        fn, _ = _rms(f)
        y = h_in + 0.5 * (fn * gpost_ref[...])
        ho_ref[...] = y
        if with_loss:
            row = i * tm + lax.broadcasted_iota(jnp.int32, (tm, 1), 0)
            err = jnp.where(row >= BLK, y - t_ref[...], 0.0)
            dy_ref[...] = err * (1.0 / D_MODEL)
            part = 0.5 * jnp.sum(jnp.sum(err * err, axis=-1, keepdims=True) * (1.0 / D_MODEL), axis=0, keepdims=True)

            @pl.when(i == 0)
            def _():
                loss_ref[...] = jnp.zeros_like(loss_ref)
            loss_ref[...] += part

    row_f32 = _row_spec(tm, D_MODEL)
    in_specs = [row_f32, VMEM_SPEC, VMEM_SPEC, VMEM_SPEC, VMEM_SPEC, VMEM_SPEC]
    out_specs = [row_f32, _row_spec(tm, D_FF), _row_spec(tm, D_FF), row_f32]
    out_shape = [jax.ShapeDtypeStruct((rows, D_MODEL), F32), jax.ShapeDtypeStruct((rows, D_FF), BF16),
                 jax.ShapeDtypeStruct((rows, D_FF), BF16), jax.ShapeDtypeStruct((rows, D_MODEL), F32)]
    args = [h, gpre, wg_t, wu_t, wd, gpost]
    if with_loss:
        in_specs.append(row_f32)
        args.append(tgt)
        out_specs += [row_f32, pl.BlockSpec((8, 128), lambda i: (0, 0))]
        out_shape += [jax.ShapeDtypeStruct((rows, D_MODEL), F32), jax.ShapeDtypeStruct((8, 128), F32)]
    return pl.pallas_call(
        body, name="ffn_fwd_loss" if with_loss else "ffn_fwd", grid=(rows // tm,),
        in_specs=in_specs, out_specs=out_specs, out_shape=out_shape,
        scratch_shapes=[pltpu.VMEM((tm, D_MODEL), F32)],
        compiler_params=_params(("arbitrary",)),
    )(*args)


def _ffn_bwd_act(dh_out, h, a, b, f, gpre, gpost, wg_t, wu_t, wd, name):
    rows = h.shape[0]
    tm = _row_tile(rows)
    nf = D_FF // FF_TILE

    def body(dho_ref, h_ref, a_ref, b_ref, f_ref, gpre_ref, gpost_ref, wg_ref, wu_ref, wd_ref,
             dh_ref, da_ref, db_ref, df_ref, n_ref, dgpre_ref, dgpost_ref, acc):
        first = pl.program_id(0) == 0
        dho = dho_ref[...]
        drr = 0.5 * dho
        fn, rf = _rms(f_ref[...])
        _acc_add(dgpost_ref, first, _colsum(drr * fn))
        df16 = _rms_bwd(fn, rf, gpost_ref[...], drr).astype(BF16)
        df_ref[...] = df16
        hn, rh = _rms(h_ref[...])
        n_ref[...] = (hn * gpre_ref[...]).astype(BF16)
        for j in range(nf):
            cols = slice(j * FF_TILE, (j + 1) * FF_TILE)
            ds = _nt(df16, wd_ref[cols, :])
            av = a_ref[:, cols].astype(F32)
            bv = b_ref[:, cols].astype(F32)
            sg = _sigmoid(av)
            db16 = (ds * (av * sg)).astype(BF16)
            da16 = (ds * bv * (sg * (1.0 + av * (1.0 - sg)))).astype(BF16)
            da_ref[:, cols] = da16
            db_ref[:, cols] = db16
            part = _nn(da16, wg_ref[cols, :]) + _nn(db16, wu_ref[cols, :])
            if j == 0:
                acc[...] = part
            else:
                acc[...] += part
        dn = acc[...]
        _acc_add(dgpre_ref, first, _colsum(dn * hn))
        dh_ref[...] = dho + _rms_bwd(hn, rh, gpre_ref[...], dn)

    row_f32 = _row_spec(tm, D_MODEL)
    row_ff = _row_spec(tm, D_FF)
    return pl.pallas_call(
        body, name=name, grid=(rows // tm,),
        in_specs=[row_f32, row_f32, row_ff, row_ff, row_f32, VMEM_SPEC, VMEM_SPEC, VMEM_SPEC, VMEM_SPEC, VMEM_SPEC],
        out_specs=[row_f32, row_ff, row_ff, row_f32, row_f32, _acc_spec(D_MODEL), _acc_spec(D_MODEL)],
        out_shape=[jax.ShapeDtypeStruct((rows, D_MODEL), F32), jax.ShapeDtypeStruct((rows, D_FF), BF16),
                   jax.ShapeDtypeStruct((rows, D_FF), BF16), jax.ShapeDtypeStruct((rows, D_MODEL), BF16),
                   jax.ShapeDtypeStruct((rows, D_MODEL), BF16), jax.ShapeDtypeStruct((8, D_MODEL), F32),
                   jax.ShapeDtypeStruct((8, D_MODEL), F32)],
        scratch_shapes=[pltpu.VMEM((tm, D_MODEL), F32)],
        compiler_params=_params(("arbitrary",)),
    )(dh_out, h, a, b, f, gpre, gpost, wg_t, wu_t, wd)


def _wgrad(lhs, rhs, name, gate=None):
    rows, width = lhs.shape
    tm = 1664 if rows % 1664 == 0 else BLK
    tf = width // 2 if width > WGRAD_TILE_MAX else width
    nr = rows // tm
    gated = gate is not None

    def body(*refs):
        if gated:
            g_ref, l_ref, r_ref, o_ref, acc = refs
            gv = g_ref[...].astype(F32)
            lv = (gv * _sigmoid(gv) * l_ref[...].astype(F32)).astype(BF16)
        else:
            l_ref, r_ref, o_ref, acc = refs
            lv = l_ref[...]
        i = pl.program_id(1)
        part = _tn(lv, r_ref[...])

        @pl.when(i == 0)
        def _():
            acc[...] = part

        @pl.when(i > 0)
        def _():
            acc[...] += part

        @pl.when(i == nr - 1)
        def _():
            o_ref[...] = acc[...].astype(BF16)

    l_spec = pl.BlockSpec((tm, tf), lambda j, i: (i, j))
    r_spec = pl.BlockSpec((tm, D_MODEL), lambda j, i: (i, 0))
    return pl.pallas_call(
        body, name=name, grid=(width // tf, nr),
        in_specs=([l_spec] if gated else []) + [l_spec, r_spec],
        out_specs=pl.BlockSpec((tf, D_MODEL), lambda j, i: (j, 0)),
        out_shape=jax.ShapeDtypeStruct((width, D_MODEL), BF16),
        scratch_shapes=[pltpu.VMEM((tf, D_MODEL), F32)],
        compiler_params=_params(("arbitrary", "arbitrary")),
    )(*([gate] if gated else []), lhs, rhs)


def _chunk_cumsum(x, lower):
    tri = jnp.where(_tri(lower), 1.0, 0.0).astype(BF16)
    hi, lo = _split_bf16(x)
    return _nn(tri, hi) + _nn(tri, lo)


def _mix_in(h, g, win_p, wa2_p, b_a, cos, sin):
    rows = h.shape[0]
    tm = 640 if rows % 640 == 0 else BLK

    def body(h_ref, g_ref, win_ref, wa2_ref, ba_ref, cos_ref, sin_ref,
             gq_ref, gk_ref, gv_ref, gg_ref, sq_ref, sk_ref, sv_ref, ga_ref, loga_ref, bc_ref, n_ref):
        hn, _ = _rms(h_ref[...])
        n16 = (hn * g_ref[...]).astype(BF16)
        n_ref[...] = n16
        proj = _nt(n16, win_ref[...])
        gq_ref[...] = proj[:, P_GQ:P_GK]
        gk_ref[...] = proj[:, P_GK:P_GV]
        gv_ref[...] = proj[:, P_GV:P_GG].astype(BF16)
        gg_ref[...] = proj[:, P_GG:P_SQ]
        c1, s1 = cos_ref[...], sin_ref[...]
        c4 = jnp.concatenate([c1, c1, c1, c1], axis=1)
        s4 = jnp.concatenate([s1, s1, s1, s1], axis=1)
        sq = proj[:, P_SQ:P_SK]
        sk = proj[:, P_SK:P_SV]
        sq_ref[...] = (sq * c4 + _rot_half(sq) * s4).astype(BF16)
        sk_ref[...] = (sk * c1 + _rot_half(sk) * s1).astype(BF16)
        sv_ref[...] = proj[:, P_SV:P_GA].astype(BF16)
        ga = proj[:, P_GA:P_END]
        ga_ref[...] = ga
        z = _nn(ga, wa2_ref[...]) + ba_ref[...]
        loga = (jnp.minimum(z, 0.0) - jnp.log(1.0 + jnp.exp(-jnp.abs(z)))) * (1.0 / GLA_TAU)
        loga_ref[...] = loga
        for c in range(tm // BLK):
            rs = slice(c * BLK, (c + 1) * BLK)
            bc_ref[rs, :] = _chunk_cumsum(loga[rs, :], True)

    f32 = lambda c: jax.ShapeDtypeStruct((rows, c), F32)
    b16 = lambda c: jax.ShapeDtypeStruct((rows, c), BF16)
    rs = lambda c: _row_spec(tm, c)
    return pl.pallas_call(
        body, name="mix_in", grid=(rows // tm,),
        in_specs=[rs(D_MODEL), VMEM_SPEC, VMEM_SPEC, VMEM_SPEC, VMEM_SPEC, rs(128), rs(128)],
        out_specs=[rs(256), rs(256), rs(512), rs(512), rs(512), rs(128), rs(128), rs(128), rs(256), rs(256), rs(D_MODEL)],
        out_shape=[f32(256), f32(256), b16(512), f32(512), b16(512), b16(128), b16(128), f32(128), f32(256), f32(256),
                   b16(D_MODEL)],
        compiler_params=_params(("arbitrary",)),
    )(h, g, win_p, wa2_p, b_a, cos, sin)


def _gla_factors(q, k, bc):
    bm = bc[BLK // 2 - 1:BLK // 2, :]
    bl = bc[BLK - 1:BLK, :]
    e_q, e_k, e_qe, e_kd = jnp.exp(bc - bm), jnp.exp(bm - bc), jnp.exp(bc), jnp.exp(bl - bc)
    return (q * e_q, k * e_k, q * e_qe, k * e_kd), (e_q, e_k, e_qe, e_kd), jnp.exp(bl)


def _gla_fwd(gq, gk, gv, gg, bc, wgn):
    rows = gq.shape[0]
    nc = rows // BLK
    scale = GLA_DK ** -0.5

    def body(q_ref, k_ref, v_ref, gg_ref, bc_ref, wgn_ref, o_ref, cat_ref, sp_ref, st):
        @pl.when(pl.program_id(0) == 0)
        def _():
            st[...] = jnp.zeros_like(st)
        low = _tri(True)
        wgn_v = wgn_ref[...]
        for p in range(2):
            sl = slice(128 * p, 128 * p + 128)
            (qt, kt, qe, kd), _, ebl = _gla_factors(q_ref[:, sl] * scale, k_ref[:, sl], bc_ref[:, sl])
            s_prev = st[p]
            sp_ref[0, p] = s_prev
            s16 = s_prev.astype(BF16)
            qt16 = qt.astype(BF16)
            s_new = s_prev * ebl
            for hh in range(2):
                hs = slice(128 * (2 * p + hh), 128 * (2 * p + hh) + 128)
                lm = _half_mask(128, hh)
                vh = v_ref[:, hs]
                pm = jnp.where(low, _nt(qt16, jnp.where(lm, kt, 0.0).astype(BF16)), 0.0)
                o = _nn(pm.astype(BF16), vh) + _nt(jnp.where(lm, qe, 0.0).astype(BF16), s16)
                s_new = s_new + _tn(vh, jnp.where(lm, kd, 0.0).astype(BF16))
                o_ref[:, hs] = o
                on, _ = _rms(o)
                gate = gg_ref[:, hs]
                cat_ref[:, hs] = (on * wgn_v * (gate * _sigmoid(gate))).astype(BF16)
            st[p] = s_new

    rs = lambda c: _row_spec(BLK, c)
    return pl.pallas_call(
        body, name="gla_fwd", grid=(nc,),
        in_specs=[rs(256), rs(256), rs(512), rs(512), rs(256), VMEM_SPEC],
        out_specs=[rs(512), rs(512), pl.BlockSpec((1, 2, 128, 128), lambda i: (i, 0, 0, 0))],
        out_shape=[jax.ShapeDtypeStruct((rows, 512), F32), jax.ShapeDtypeStruct((rows, 512), BF16),
                   jax.ShapeDtypeStruct((nc, 2, 128, 128), F32)],
        scratch_shapes=[pltpu.VMEM((2, 128, 128), F32)],
        compiler_params=_params(("arbitrary",)),
    )(gq, gk, gv, gg, bc, wgn)


def _gla_bwd(dcat, o_all, gq, gk, gv, gg, bc, sp, wgn):
    rows = gq.shape[0]
    nc = rows // BLK
    scale = GLA_DK ** -0.5

    def body(dc_ref, o_ref, q_ref, k_ref, v_ref, gg_ref, bc_ref, sp_ref, wgn_ref,
             dq_ref, dk_ref, dv_ref, dgg_ref, dla_ref, dwgn_ref, dst):
        first = pl.program_id(0) == 0

        @pl.when(first)
        def _():
            dst[...] = jnp.zeros_like(dst)
        low, upp = _tri(True), _tri(False)
        last_row = lax.broadcasted_iota(jnp.int32, (BLK, 1), 0) == BLK - 1
        wgn_v = wgn_ref[...]
        dwgn = jnp.zeros((1, 128), F32)
        for p in range(2):
            sl = slice(128 * p, 128 * p + 128)
            (qt, kt, qe, kd), (e_q, e_k, e_qe, e_kd), ebl = _gla_factors(
                q_ref[:, sl] * scale, k_ref[:, sl], bc_ref[:, sl])
            s_prev = sp_ref[0, p]
            s16 = s_prev.astype(BF16)
            ds_next = dst[p]
            ds16 = ds_next.astype(BF16)
            qt16 = qt.astype(BF16)
            ds_new = ds_next * ebl
            dqt = jnp.zeros((BLK, 128), F32)
            dkt = jnp.zeros((BLK, 128), F32)
            dqe = jnp.zeros((BLK, 128), F32)
            dkd = jnp.zeros((BLK, 128), F32)
            for hh in range(2):
                hs = slice(128 * (2 * p + hh), 128 * (2 * p + hh) + 128)
                lm = _half_mask(128, hh)
                on, ro = _rms(o_ref[:, hs])
                gate = gg_ref[:, hs]
                sg = _sigmoid(gate)
                si = gate * sg
                dog = dc_ref[:, hs]
                dwgn = dwgn + _colsum(dog * si * on)
                dgg_ref[:, hs] = dog * (on * wgn_v) * (sg * (1.0 + gate * (1.0 - sg)))
                do16 = _rms_bwd(on, ro, wgn_v, dog * si).astype(BF16)
                vh = v_ref[:, hs]
                ktm16 = jnp.where(lm, kt, 0.0).astype(BF16)
                qtm16 = jnp.where(lm, qt, 0.0).astype(BF16)
                qem16 = jnp.where(lm, qe, 0.0).astype(BF16)
                kdm16 = jnp.where(lm, kd, 0.0).astype(BF16)
                p_t = jnp.where(upp, _nt(ktm16, qt16), 0.0)
                dp_t = jnp.where(upp, _nt(vh, do16), 0.0)
                dp = jnp.where(low, _nt(do16, vh), 0.0)
                dv_ref[:, hs] = _nn(p_t.astype(BF16), do16) + _nt(kdm16, ds16)
                dqt = dqt + _nn(dp.astype(BF16), ktm16)
                dkt = dkt + _nn(dp_t.astype(BF16), qtm16)
                dqe = dqe + jnp.where(lm, _nn(do16, s16), 0.0)
                dkd = dkd + jnp.where(lm, _nn(vh, ds16), 0.0)
                ds_new = ds_new + _tn(do16, qem16)
            debl = _colsum(ds_next * s_prev)
            dq_ref[:, sl] = (dqt * e_q + dqe * e_qe) * scale
            dk_ref[:, sl] = dkt * e_k + dkd * e_kd
            dkd_kd = dkd * kd
            db = dqt * qt - dkt * kt + dqe * qe - dkd_kd
            db = jnp.where(last_row, db + (_colsum(dkd_kd) + debl * ebl), db)
            dla_ref[:, sl] = _chunk_cumsum(db, False)
            dst[p] = ds_new
        _acc_add(dwgn_ref, first, dwgn)

    rev = lambda c: pl.BlockSpec((BLK, c), lambda i: (nc - 1 - i, 0))
    f32 = lambda c: jax.ShapeDtypeStruct((rows, c), F32)
    return pl.pallas_call(
        body, name="gla_bwd", grid=(nc,),
        in_specs=[rev(512), rev(512), rev(256), rev(256), rev(512), rev(512), rev(256),
                  pl.BlockSpec((1, 2, 128, 128), lambda i: (nc - 1 - i, 0, 0, 0)), VMEM_SPEC],
        out_specs=[rev(256), rev(256), rev(512), rev(512), rev(256), _acc_spec(128)],
        out_shape=[f32(256), f32(256), f32(512), f32(512), f32(256), jax.ShapeDtypeStruct((8, 128), F32)],
        scratch_shapes=[pltpu.VMEM((2, 128, 128), F32)],
        compiler_params=_params(("arbitrary",)),
    )(dcat, o_all, gq, gk, gv, gg, bc, sp, wgn)


def _swa_mask(i):
    t = lax.broadcasted_iota(jnp.int32, (BLK, 3 * BLK), 0)
    c = lax.broadcasted_iota(jnp.int32, (BLK, 3 * BLK), 1)
    qpos = i * BLK + t - PAD_ROWS
    seg = c // BLK
    cc = c % BLK
    kpos = jnp.where(seg == 0, (i - 1) * BLK + cc, i * BLK + cc) - PAD_ROWS
    band = (seg < 2) & (kpos >= N_META) & (kpos <= qpos) & (qpos - kpos < BLK)
    midx = cc - PAD_ROWS
    meta = (seg == 2) & (midx >= 0) & (midx <= qpos)
    return band | meta


def _swa_keys(k_ref, i):
    prev = pl.multiple_of(jnp.maximum(i - 1, 0) * BLK, BLK)
    own = pl.multiple_of(i * BLK, BLK)
    return jnp.concatenate([k_ref[pl.ds(prev, BLK), :], k_ref[pl.ds(own, BLK), :], k_ref[0:BLK, :]], axis=0), prev, own


def _place(x, kv):
    if kv == 0:
        lo = jnp.where(_half_mask(128, 0), x, jnp.zeros_like(x))
        return lo, pltpu.roll(lo, 64, 1)
    hi = jnp.where(_half_mask(128, 1), x, jnp.zeros_like(x))
    return pltpu.roll(hi, 64, 1), hi


def _swa_fwd(sq, sk, sv, sinks, wn):
    rows = sq.shape[0]
    nb = rows // BLK
    scale = SWA_HD ** -0.5

    def body(q_ref, k_ref, v_ref, sink_ref, wn_ref, o_ref, cat_ref, lse_ref):
        i = pl.program_id(0)
        mask = _swa_mask(i)
        keys, _, _ = _swa_keys(k_ref, i)
        vals, _, _ = _swa_keys(v_ref, i)
        kz = (_place(keys, 0), _place(keys, 1))
        vz = (_place(vals, 0), _place(vals, 1))
        outs = []
        for pr in range(4):
            q_pair = q_ref[:, 128 * pr:128 * pr + 128]
            o_pair = jnp.zeros((BLK, 128), F32)
            for half in range(2):
                hd = 2 * pr + half
                kv = hd // 4
                sink = sink_ref[0, hd]
                s = jnp.where(mask, _nt(q_pair, kz[kv][half]) * scale, NEG_INF)
                m = jnp.maximum(jnp.max(s, axis=-1, keepdims=True), sink)
                e = jnp.exp(s - m)
                den = jnp.sum(e, axis=-1, keepdims=True) + jnp.exp(sink - m)
                lse_ref[:, hd:hd + 1] = m + jnp.log(den)
                o_pair = o_pair + _nn((e / den).astype(BF16), vz[kv][half])
            outs.append(o_pair)
        o = jnp.concatenate(outs, axis=1)
        o_ref[...] = o
        on, _ = _rms(o)
        cat_ref[...] = (on * wn_ref[...]).astype(BF16)

    return pl.pallas_call(
        body, name="swa_fwd", grid=(nb,),
        in_specs=[_row_spec(BLK, 512), VMEM_SPEC, VMEM_SPEC, SMEM_SPEC, VMEM_SPEC],
        out_specs=[_row_spec(BLK, 512), _row_spec(BLK, 512), _row_spec(BLK, SWA_HEADS)],
        out_shape=[jax.ShapeDtypeStruct((rows, 512), F32), jax.ShapeDtypeStruct((rows, 512), BF16),
                   jax.ShapeDtypeStruct((rows, SWA_HEADS), F32)],
        compiler_params=_params(("arbitrary",)),
    )(sq, sk, sv, sinks, wn)


def _swa_bwd(dcat, o_all, sq, sk, sv, lse, sinks, wn):
    rows = sq.shape[0]
    nb = rows // BLK
    scale = SWA_HD ** -0.5

    def body(dc_ref, o_ref, q_ref, k_ref, v_ref, lse_ref, sink_ref, wn_ref, dq_ref, dk_ref, dv_ref, dsink_ref, dwn_ref):
        i = pl.program_id(0)
        first = i == 0

        @pl.when(first)
        def _():
            dk_ref[...] = jnp.zeros_like(dk_ref)
            dv_ref[...] = jnp.zeros_like(dv_ref)
        mask = _swa_mask(i)
        keys, prev, own = _swa_keys(k_ref, i)
        vals, _, _ = _swa_keys(v_ref, i)
        kz = (_place(keys, 0), _place(keys, 1))
        vz = (_place(vals, 0), _place(vals, 1))
        o = o_ref[...]
        on, ro = _rms(o)
        dc = dc_ref[...]
        _acc_add(dwn_ref, first, _colsum(dc * on))
        do = _rms_bwd(on, ro, wn_ref[...], dc)
        do_o = do * o
        lane8 = lax.broadcasted_iota(jnp.int32, (1, 128), 1)
        dsink = jnp.zeros((1, 128), F32)
        dk_acc = jnp.zeros((3 * BLK, 128), F32)
        dv_acc = jnp.zeros((3 * BLK, 128), F32)
        dqs = []
        for pr in range(4):
            ps = slice(128 * pr, 128 * pr + 128)
            q_pair = q_ref[:, ps]
            do16 = do[:, ps].astype(BF16)
            dq_pair = jnp.zeros((BLK, 128), F32)
            for half in range(2):
                hd = 2 * pr + half
                kv = hd // 4
                hm = _half_mask(128, half)
                sink = sink_ref[0, hd]
                lse_h = lse_ref[:, hd:hd + 1]
                delta = jnp.sum(jnp.where(hm, do_o[:, ps], 0.0), axis=-1, keepdims=True)
                s = jnp.where(mask, _nt(q_pair, kz[kv][half]) * scale, NEG_INF)
                prob = jnp.exp(s - lse_h)
                dsink = dsink + jnp.where(lane8 == hd, -jnp.sum(jnp.exp(sink - lse_h) * delta), 0.0)
                dp = _nt(do16, vz[kv][half])
                ds16 = (prob * (dp - delta) * scale).astype(BF16)
                dq_pair = dq_pair + _nn(ds16, kz[kv][half])
                dkz = jnp.where(hm, _tn(ds16, q_pair), 0.0)
                dvz = jnp.where(hm, _tn(prob.astype(BF16), do16), 0.0)
                if half != kv:
                    dkz = pltpu.roll(dkz, 64, 1)
                    dvz = pltpu.roll(dvz, 64, 1)
                dk_acc = dk_acc + dkz
                dv_acc = dv_acc + dvz
            dqs.append(dq_pair)
        dq_ref[...] = jnp.concatenate(dqs, axis=1)
        _acc_add(dsink_ref, first, dsink)
        for ref, acc in ((dk_ref, dk_acc), (dv_ref, dv_acc)):
            ref[pl.ds(prev, BLK), :] += acc[0:BLK]
            ref[pl.ds(own, BLK), :] += acc[BLK:2 * BLK]
            ref[0:BLK, :] += acc[2 * BLK:3 * BLK]

    full = pl.BlockSpec((rows, 128), lambda i: (0, 0))
    return pl.pallas_call(
        body, name="swa_bwd", grid=(nb,),
        in_specs=[_row_spec(BLK, 512), _row_spec(BLK, 512), _row_spec(BLK, 512), VMEM_SPEC, VMEM_SPEC,
                  _row_spec(BLK, SWA_HEADS), SMEM_SPEC, VMEM_SPEC],
        out_specs=[_row_spec(BLK, 512), full, full, _acc_spec(128), _acc_spec(512)],
        out_shape=[jax.ShapeDtypeStruct((rows, 512), F32), jax.ShapeDtypeStruct((rows, 128), F32),
                   jax.ShapeDtypeStruct((rows, 128), F32), jax.ShapeDtypeStruct((8, 128), F32),
                   jax.ShapeDtypeStruct((8, 512), F32)],
        compiler_params=_params(("arbitrary",)),
    )(dcat, o_all, sq, sk, sv, lse, sinks, wn)


def _mix_out(h, cat_g, cat_s, wout, gpost):
    rows = h.shape[0]
    tm = _row_tile(rows)

    def body(h_ref, cg_ref, cs_ref, w_ref, g_ref, ho_ref, m_ref):
        m = _nn(cg_ref[...], w_ref[0:512, :]) + _nn(cs_ref[...], w_ref[512:1024, :])
        m_ref[...] = m
        mn, _ = _rms(m)
        ho_ref[...] = h_ref[...] + mn * g_ref[...]

    row_f32 = _row_spec(tm, D_MODEL)
    return pl.pallas_call(
        body, name="mix_out", grid=(rows // tm,),
        in_specs=[row_f32, _row_spec(tm, 512), _row_spec(tm, 512), VMEM_SPEC, VMEM_SPEC],
        out_specs=[row_f32, row_f32],
        out_shape=[jax.ShapeDtypeStruct((rows, D_MODEL), F32), jax.ShapeDtypeStruct((rows, D_MODEL), F32)],
        compiler_params=_params(("arbitrary",)),
    )(h, cat_g, cat_s, wout, gpost)


def _mix_out_bwd(dh, m, wout, gpost):
    rows = dh.shape[0]
    tm = _row_tile(rows)

    def body(dh_ref, m_ref, w_ref, g_ref, dcg_ref, dcs_ref, dm_ref, dg_ref):
        first = pl.program_id(0) == 0
        dhv = dh_ref[...]
        mn, rm = _rms(m_ref[...])
        _acc_add(dg_ref, first, _colsum(dhv * mn))
        dm16 = _rms_bwd(mn, rm, g_ref[...], dhv).astype(BF16)
        dm_ref[...] = dm16
        dcat = _nt(dm16, w_ref[...])
        dcg_ref[...] = dcat[:, 0:512]
        dcs_ref[...] = dcat[:, 512:1024]

    row_f32 = _row_spec(tm, D_MODEL)
    return pl.pallas_call(
        body, name="mix_out_bwd", grid=(rows // tm,),
        in_specs=[row_f32, row_f32, VMEM_SPEC, VMEM_SPEC],
        out_specs=[_row_spec(tm, 512), _row_spec(tm, 512), row_f32, _acc_spec(D_MODEL)],
        out_shape=[jax.ShapeDtypeStruct((rows, 512), F32), jax.ShapeDtypeStruct((rows, 512), F32),
                   jax.ShapeDtypeStruct((rows, D_MODEL), BF16), jax.ShapeDtypeStruct((8, D_MODEL), F32)],
        compiler_params=_params(("arbitrary",)),
    )(dh, m, wout, gpost)


def _mix_in_bwd(dh_out, h, g, win_p, wa2_p, cos, sin, loga, ga, dgq, dgk, dgv, dgg, dsq, dsk, dsv, dloga):
    rows = h.shape[0]
    tm = _row_tile(rows)

    def body(dho_ref, h_ref, g_ref, win_ref, wa2_ref, cos_ref, sin_ref, loga_ref, ga_ref,
             dgq_ref, dgk_ref, dgv_ref, dgg_ref, dsq_ref, dsk_ref, dsv_ref, dla_ref,
             dh_ref, dproj_ref, dwa2_ref, dg_ref, dba_ref):
        first = pl.program_id(0) == 0
        dz = dla_ref[...] * (1.0 / GLA_TAU) * (1.0 - jnp.exp(GLA_TAU * loga_ref[...]))
        _acc_add(dba_ref, first, _colsum(dz))
        dga = _nt(dz, wa2_ref[...])
        pa = _tn(ga_ref[...], dz)
        c1, s1 = cos_ref[...], sin_ref[...]
        c4 = jnp.concatenate([c1, c1, c1, c1], axis=1)
        s4 = jnp.concatenate([s1, s1, s1, s1], axis=1)
        dq_r, dk_r = dsq_ref[...], dsk_ref[...]
        dsq = dq_r * c4 - _rot_half(dq_r * s4)
        dsk = dk_r * c1 - _rot_half(dk_r * s1)
        dproj16 = jnp.concatenate(
            [dgq_ref[...], dgk_ref[...], dgv_ref[...], dgg_ref[...], dsq, dsk, dsv_ref[...], dga], axis=1).astype(BF16)
        dproj_ref[...] = dproj16
        dn = _nn(dproj16, win_ref[...])

        @pl.when(first)
        def _():
            dwa2_ref[...] = pa

        @pl.when(jnp.logical_not(first))
        def _():
            dwa2_ref[...] += pa
        hn, rh = _rms(h_ref[...])
        _acc_add(dg_ref, first, _colsum(dn * hn))
        dh_ref[...] = dho_ref[...] + _rms_bwd(hn, rh, g_ref[...], dn)

    rs = lambda c: _row_spec(tm, c)
    return pl.pallas_call(
        body, name="mix_in_bwd", grid=(rows // tm,),
        in_specs=[rs(D_MODEL), rs(D_MODEL), VMEM_SPEC, VMEM_SPEC, VMEM_SPEC, rs(128), rs(128), rs(256), rs(128),
                  rs(256), rs(256), rs(512), rs(512), rs(512), rs(128), rs(128), rs(256)],
        out_specs=[rs(D_MODEL), rs(P_END), pl.BlockSpec((128, 256), lambda i: (0, 0)), _acc_spec(D_MODEL), _acc_spec(256)],
        out_shape=[jax.ShapeDtypeStruct((rows, D_MODEL), F32), jax.ShapeDtypeStruct((rows, P_END), BF16),
                   jax.ShapeDtypeStruct((128, 256), F32), jax.ShapeDtypeStruct((8, D_MODEL), F32),
                   jax.ShapeDtypeStruct((8, 256), F32)],
        compiler_params=_params(("arbitrary",)),
    )(dh_out, h, g, win_p, wa2_p, cos, sin, loga, ga, dgq, dgk, dgv, dgg, dsq, dsk, dsv, dloga)


def _rope_tables(rows):
    pos = (jnp.arange(rows, dtype=jnp.int32) - PAD_ROWS).astype(F32)
    inv_freq = 1.0 / (ROPE_THETA ** (jnp.arange(0, SWA_HD, 2, dtype=F32) / SWA_HD))
    ang = pos[:, None] * inv_freq[None, :]
    ang = jnp.concatenate([ang, ang, ang, ang], axis=-1)
    return jnp.cos(ang), jnp.sin(ang)


def _local_step(h0, tgt, w, late_weights=None, on_grads=None):
    rows = h0.shape[0]
    cos, sin = _rope_tables(rows)
    g = {}

    def tell(group, names):
        for nm in names:
            g[nm] = grads_now[nm]
        return 0.0 if on_grads is None else on_grads(group, {nm: grads_now[nm] for nm in names})

    h1, a1, b1, f1 = _ffn_fwd(h0, w["ffn1_pre"], w["wg1"], w["wu1"], w["wd1"], w["ffn1_post"])
    if late_weights is not None:
        w = {**w, **late_weights("win", f1)}
    gq, gk, gv, gg, sq, sk, sv, ga, loga, bc, n2 = _mix_in(h1, w["mix_pre"], w["win"], w["wa2"], w["b_a"], cos, sin)
    o_g, cat_g, sp = _gla_fwd(gq, gk, gv, gg, bc, w["gla_norm"])
    o_s, cat_s, lse = _swa_fwd(sq, sk, sv, w["sinks"], w["swa_norm"])
    if late_weights is not None:
        w = {**w, **late_weights("rest", lse)}
    h2, m = _mix_out(h1, cat_g, cat_s, w["wout"], w["mix_post"])
    h3, a2, b2, f2, dy, loss = _ffn_fwd(h2, w["ffn2_pre"], w["wg2"], w["wu2"], w["wd2"], w["ffn2_post"], tgt)
    del h3
    dh2, da, db, df, n3, g["ffn2_pre"], g["ffn2_post"] = _ffn_bwd_act(
        dy, h2, a2, b2, f2, w["ffn2_pre"], w["ffn2_post"], w["wg2"], w["wu2"], w["wd2"], "ffn2_bwd_act")
    grads_now = dict(wd2=_wgrad(b2, df, "ffn2_wgrad_down", gate=a2), wg2=_wgrad(da, n3, "ffn2_wgrad_gate"),
                     wu2=_wgrad(db, n3, "ffn2_wgrad_up"))
    tok = tell("ffn2", ("wd2", "wg2", "wu2"))
    dcg, dcs, dm, g["mix_post"] = _mix_out_bwd(dh2, m, w["wout"], w["mix_post"] + tok)
    dsq, dsk, dsv, g["sinks"], g["swa_norm"] = _swa_bwd(dcs, o_s, sq, sk, sv, lse, w["sinks"], w["swa_norm"])
    dgq, dgk, dgv, dgg, dloga, g["gla_norm"] = _gla_bwd(dcg, o_g, gq, gk, gv, gg, bc, sp, w["gla_norm"])
    dh1, dproj, g["wa2"], g["mix_pre"], g["b_a"] = _mix_in_bwd(
        dh2, h1, w["mix_pre"], w["win"], w["wa2"], cos, sin, loga, ga, dgq, dgk, dgv, dgg, dsq, dsk, dsv, dloga)
    grads_now = dict(wout=jnp.concatenate([_wgrad(cat_g, dm, "wout_wgrad_gla"), _wgrad(cat_s, dm, "wout_wgrad_swa")], axis=0),
                     win=_wgrad(dproj, n2, "win_wgrad"))
    tok = tell("mix", ("wout", "win"))
    dh0, da, db, df, n1, g["ffn1_pre"], g["ffn1_post"] = _ffn_bwd_act(
        dh1, h0, a1, b1, f1, w["ffn1_pre"] + tok, w["ffn1_post"], w["wg1"], w["wu1"], w["wd1"], "ffn1_bwd_act")
    grads_now = dict(wd1=_wgrad(b1, df, "ffn1_wgrad_down", gate=a1))
    tell("ffn1_down", ("wd1",))
    grads_now = dict(wg1=_wgrad(da, n1, "ffn1_wgrad_gate"))
    tell("ffn1_gate", ("wg1",))
    grads_now = dict(wu1=_wgrad(db, n1, "ffn1_wgrad_up"))
    tell("ffn1_up", ("wu1",))
    return loss[0, 0], dh0, g


def _win_pad_rows(win_t):
    pad = jnp.zeros((P_END - P_GA - 16, win_t.shape[1]), win_t.dtype)
    return jnp.concatenate([win_t[0:1536], win_t[1552:2320], win_t[1536:1552], pad], axis=0)


def _win_unpad_rows(win_p):
    return jnp.concatenate([win_p[0:1536], win_p[P_GA:P_GA + 16], win_p[1536:P_GA]], axis=0)


def _place_on_mesh():
    return lax.axis_index("x"), lax.axis_index("y"), lax.axis_index("c")


def _dev_index(px, py, pc):
    return 4 * px + 2 * py + pc


def _other_devices(x, y, c):
    flip = lambda v, f: 1 - v if f else v
    return [(flip(x, fx), flip(y, fy), flip(c, fc)) for fx in (0, 1) for fy in (0, 1) for fc in (0, 1)][1:]


def _all_gather(shards):
    n = len(shards)

    def body(*refs):
        ins, outs = refs[:n], refs[n:2 * n]
        zeros_ref, send_sems, recv_sems, local_sems = refs[2 * n:]
        zeros_ref[...] = jnp.zeros_like(zeros_ref)
        x, y, c = _place_on_mesh()
        me, sibling = (x, y, c), (x, y, 1 - c)
        chips = [(1 - x, y), (x, 1 - y), (1 - x, 1 - y)]

        def rows(k, px, py, pc):
            r = ins[k].shape[0]
            return outs[k].at[pl.ds(pl.multiple_of(_dev_index(px, py, pc) * r, 8), r), :]

        def copy(k, slot, block, to, src=None):
            return pltpu.make_async_remote_copy(
                src_ref=rows(k, *block) if src is None else src, dst_ref=rows(k, *block),
                send_sem=send_sems.at[k, slot], recv_sem=recv_sems.at[k, slot], device_id=to, device_id_type=MESH)

        local = [pltpu.make_async_copy(ins[k], rows(k, *me), local_sems.at[k]) for k in range(n)]
        sends = []
        for k in range(n):
            local[k].start()
            sends.append(copy(k, 0, me, sibling, src=ins[k]))
            sends += [copy(k, 1 + j, me, (*chip, c), src=ins[k]) for j, chip in enumerate(chips)]
        for cp in sends:
            cp.start()
        for k in range(n):
            for j, chip in enumerate(chips):
                copy(k, 1 + j, (*chip, c), me).wait_recv()
                passed = copy(k, 4 + j, (*chip, c), sibling)
                passed.start()
                sends.append(passed)
        for k in range(n):
            copy(k, 0, sibling, me).wait_recv()
            for j, chip in enumerate(chips):
                copy(k, 4 + j, (*chip, 1 - c), me).wait_recv()
        for cp in sends:
            cp.wait_send()
        for cp in local:
            cp.wait()

    return pl.pallas_call(
        body, name="all_gather_weights",
        in_specs=[ANY_SPEC] * n, out_specs=[ANY_SPEC] * n + [VMEM_SPEC],
        out_shape=[jax.ShapeDtypeStruct((N_DEV * s.shape[0], s.shape[1]), s.dtype) for s in shards]
        + [jax.ShapeDtypeStruct((8, 128), F32)],
        scratch_shapes=[pltpu.SemaphoreType.DMA((n, 7)), pltpu.SemaphoreType.DMA((n, 7)), pltpu.SemaphoreType.DMA((n,))],
    )(*shards)


HBM_SPEC = pl.BlockSpec(memory_space=pltpu.HBM)
SEM_SPEC = pl.BlockSpec(memory_space=pltpu.SEMAPHORE)
DATAFLOW = pltpu.SideEffectType.DATAFLOW_SIDE_EFFECTING


def _exchange_copies(srcs, lands, send_sems, recv_sems, scatter, arriving):
    x, y, c = _place_on_mesh()
    me = _dev_index(x, y, c)
    out = []
    for k, (src, land) in enumerate(zip(srcs, lands)):
        r = land.shape[0] // N_DEV

        def block(ref, d):
            return ref.at[pl.ds(pl.multiple_of(d * r, 8), r), :]

        for f, peer in enumerate(_other_devices(x, y, c)):
            mine, his = (_dev_index(*peer), me) if arriving else (me, _dev_index(*peer))
            out.append(pltpu.make_async_remote_copy(
                src_ref=block(src, his) if scatter else src, dst_ref=block(land, mine),
                send_sem=send_sems.at[7 * k + f], recv_sem=recv_sems.at[7 * k + f], device_id=peer, device_id_type=MESH))
    return out


def _exchange_start(srcs, scatter, name):
    n = len(srcs)
    lands = [lax.empty(s.shape if scatter else (N_DEV * s.shape[0], s.shape[1]), s.dtype) for s in srcs]

    def body(*refs):
        send_sems, recv_sems = refs[2 * n], refs[2 * n + 1]
        for going in _exchange_copies(refs[:n], refs[n:2 * n], send_sems, recv_sems, scatter, False):
            going.start()
        refs[-1][...] = jnp.zeros_like(refs[-1])

    both = list(srcs) + list(lands)
    outs = pl.pallas_call(
        body, name=name,
        out_shape=(pltpu.SemaphoreType.DMA((7 * n,)), pltpu.SemaphoreType.DMA((7 * n,)),
                   *[pltpu.HBM(a.shape, a.dtype) for a in both], jax.ShapeDtypeStruct((8, 128), F32)),
        in_specs=[HBM_SPEC] * (2 * n), out_specs=(SEM_SPEC, SEM_SPEC, *[HBM_SPEC] * (2 * n), VMEM_SPEC),
        input_output_aliases={i: 2 + i for i in range(2 * n)},
        compiler_params=pltpu.CompilerParams(has_side_effects=DATAFLOW),
    )(*[pltpu.with_memory_space_constraint(a, pltpu.HBM) for a in both])
    return outs[0], outs[1], outs[2:2 + n], outs[2 + n:2 + 2 * n], outs[-1]


def _exchange_wait(started, scatter, after, name, dev):
    send_sems, recv_sems, srcs, lands, _ = started
    n = len(srcs)

    def body(*refs):
        args = (refs[:n], refs[n:2 * n], refs[2 * n], refs[2 * n + 1], scatter)
        for going in _exchange_copies(*args, False):
            going.wait_send()
        for coming in _exchange_copies(*args, True):
            coming.wait_recv()

    both = list(srcs) + list(lands)
    outs = pl.pallas_call(
        body, name=name, out_shape=[pltpu.HBM(a.shape, a.dtype) for a in both],
        in_specs=[HBM_SPEC] * (2 * n) + [SEM_SPEC, SEM_SPEC, ANY_SPEC], out_specs=[HBM_SPEC] * (2 * n),
        input_output_aliases={i: i for i in range(2 * n)},
        compiler_params=pltpu.CompilerParams(has_side_effects=DATAFLOW),
    )(*both, send_sems, recv_sems, after)
    done = []
    for src, land in zip(outs[:n], outs[n:]):
        r = land.shape[0] // N_DEV
        own = lax.dynamic_slice_in_dim(src, dev * r, r, axis=0) if scatter else src
        done.append(lax.dynamic_update_slice(land, own, (dev * r, 0)))
    return done


def _sum_partials(parts, name):
    n = len(parts)

    def body(*refs):
        ins, outs = refs[:n], refs[n:]
        first = pl.program_id(0) == 0
        for i_ref, o_ref in zip(ins, outs):
            v = i_ref[...].astype(F32)

            @pl.when(first)
            def _():
                o_ref[...] = v

            @pl.when(jnp.logical_not(first))
            def _():
                o_ref[...] += v

    shapes = [(p.shape[0] // N_DEV, p.shape[1]) for p in parts]
    return pl.pallas_call(
        body, name=name, grid=(N_DEV,),
        in_specs=[pl.BlockSpec(s, lambda j: (j, 0)) for s in shapes],
        out_specs=[pl.BlockSpec(s, lambda j: (0, 0)) for s in shapes],
        out_shape=[jax.ShapeDtypeStruct(s, F32) for s in shapes],
        compiler_params=_params(("arbitrary",)),
    )(*parts)


def _all_reduce_small(slab):
    rows, cols = slab.shape

    def body(x_ref, o_ref, gathered, send_sems, recv_sems):
        x, y, c = _place_on_mesh()
        me = _dev_index(x, y, c)
        peers = _other_devices(x, y, c)

        def copy(f, peer):
            return pltpu.make_async_remote_copy(
                src_ref=x_ref, dst_ref=gathered.at[me], send_sem=send_sems.at[f], recv_sem=recv_sems.at[f],
                device_id=peer, device_id_type=MESH)

        def arrival(f, peer):
            return pltpu.make_async_remote_copy(
                src_ref=x_ref, dst_ref=gathered.at[_dev_index(*peer)], send_sem=send_sems.at[f], recv_sem=recv_sems.at[f],
                device_id=peer, device_id_type=MESH)

        sends = [copy(f, peer) for f, peer in enumerate(peers)]
        for cp in sends:
            cp.start()
        gathered[me] = x_ref[...]
        for f, peer in enumerate(peers):
            arrival(f, peer).wait_recv()
        for cp in sends:
            cp.wait_send()
        total = gathered[0]
        for d in range(1, N_DEV):
            total = total + gathered[d]
        o_ref[...] = total

    return pl.pallas_call(
        body, name="all_reduce_small",
        in_specs=[VMEM_SPEC], out_specs=VMEM_SPEC, out_shape=jax.ShapeDtypeStruct((rows, cols), F32),
        scratch_shapes=[pltpu.VMEM((N_DEV, rows, cols), F32), pltpu.SemaphoreType.DMA((7,)), pltpu.SemaphoreType.DMA((7,))],
    )(slab)


def _adamw(ws, gs, ms, vs, name):
    n = len(ws)
    c1 = 1.0 / (1.0 - ADAM_B1 ** ADAM_STEP)
    c2 = 1.0 / (1.0 - ADAM_B2 ** ADAM_STEP)

    def body(*refs):
        w_r, g_r, m_r, v_r = refs[:n], refs[n:2 * n], refs[2 * n:3 * n], refs[3 * n:4 * n]
        d_o, m_o, v_o = refs[4 * n:5 * n], refs[5 * n:6 * n], refs[6 * n:7 * n]
        for k in range(n):
            g = g_r[k][...]
            m = ADAM_B1 * m_r[k][...] + (1.0 - ADAM_B1) * g
            v = ADAM_B2 * v_r[k][...] + (1.0 - ADAM_B2) * (g * g)
            m_o[k][...] = m
            v_o[k][...] = v
            d_o[k][...] = -ADAM_LR * ((m * c1) / (jnp.sqrt(v * c2) + ADAM_EPS) + ADAM_WD * w_r[k][...])

    shapes = [jax.ShapeDtypeStruct(w.shape, F32) for w in ws]
    outs = pl.pallas_call(
        body, name=name, in_specs=[VMEM_SPEC] * (4 * n), out_specs=[VMEM_SPEC] * (3 * n), out_shape=shapes * 3,
        compiler_params=pltpu.CompilerParams(vmem_limit_bytes=56 << 20),
    )(*ws, *gs, *ms, *vs)
    return outs[:n], outs[n:2 * n], outs[2 * n:]


WEIGHT_NAMES = ("meta_tokens", "ffn1_pre_norm", "ffn1_w_gate", "ffn1_w_up", "ffn1_w_down", "ffn1_post_norm", "mix_pre_norm",
                "w_in", "gla_w_a2", "gla_b_a", "gla_out_norm", "swa_sinks", "swa_out_norm", "w_out", "mix_post_norm",
                "ffn2_pre_norm", "ffn2_w_gate", "ffn2_w_up", "ffn2_w_down", "ffn2_post_norm")
WIN_SHARD = D_IN // N_DEV
WIN_SHARD_PAD = 304
SLAB_VECTORS = ("ffn1_pre", "ffn1_post", "mix_pre", "mix_post", "ffn2_pre", "ffn2_post")
SLAB_ROWS = 32


def kernel(x, meta_tokens, ffn1_pre_norm, ffn1_w_gate, ffn1_w_up, ffn1_w_down, ffn1_post_norm, mix_pre_norm, w_in, gla_w_a2, gla_b_a, gla_out_norm, swa_sinks, swa_out_norm, w_out, mix_post_norm, ffn2_pre_norm, ffn2_w_gate, ffn2_w_up, ffn2_w_down, ffn2_post_norm, loss_target, m_meta_tokens, m_ffn1_pre_norm, m_ffn1_w_gate, m_ffn1_w_up, m_ffn1_w_down, m_ffn1_post_norm, m_mix_pre_norm, m_w_in, m_gla_w_a2, m_gla_b_a, m_gla_out_norm, m_swa_sinks, m_swa_out_norm, m_w_out, m_mix_post_norm, m_ffn2_pre_norm, m_ffn2_w_gate, m_ffn2_w_up, m_ffn2_w_down, m_ffn2_post_norm, v_meta_tokens, v_ffn1_pre_norm, v_ffn1_w_gate, v_ffn1_w_up, v_ffn1_w_down, v_ffn1_post_norm, v_mix_pre_norm, v_w_in, v_gla_w_a2, v_gla_b_a, v_gla_out_norm, v_swa_sinks, v_swa_out_norm, v_w_out, v_mix_post_norm, v_ffn2_pre_norm, v_ffn2_w_gate, v_ffn2_w_up, v_ffn2_w_down, v_ffn2_post_norm):
    given = dict(locals())
    W = {n: given[n] for n in WEIGHT_NAMES}
    M = {n: given["m_" + n] for n in WEIGHT_NAMES}
    V = {n: given["v_" + n] for n in WEIGHT_NAMES}
    dev = _dev_index(*_place_on_mesh())

    def t16(w):
        return w[0].T.astype(BF16)

    small = jnp.concatenate([W["meta_tokens"], jnp.pad(W["gla_w_a2"][0], ((0, 0), (0, 96)))], axis=0)
    wg1, wu1, wd1, small_g, gathered_zeros = _all_gather(
        [t16(W["ffn1_w_gate"]), t16(W["ffn1_w_up"]), W["ffn1_w_down"][0].astype(BF16), small])
    def after_zero(shard, zeros):
        return shard + zeros[0:1, 0:1].astype(shard.dtype)
    win_shard = jnp.pad(t16(W["w_in"]), ((0, WIN_SHARD_PAD - WIN_SHARD), (0, 0)))
    win_shard = after_zero(win_shard, gathered_zeros)
    mid = _exchange_start([win_shard], False, "gather_w_in_start")
    late_shards = [after_zero(W["w_out"][0].astype(BF16), mid[4]), t16(W["ffn2_w_gate"]), t16(W["ffn2_w_up"]),
                   W["ffn2_w_down"][0].astype(BF16)]
    late = _exchange_start(late_shards, False, "gather_late_weights_start")

    def late_weights(what, after):
        if what == "win":
            win_g, = _exchange_wait(mid, False, after, "gather_w_in_wait", dev)
            win_t = win_g.reshape(N_DEV, WIN_SHARD_PAD, D_MODEL)[:, :WIN_SHARD].reshape(D_IN, D_MODEL)
            return dict(win=_win_pad_rows(win_t))
        wout, wg2, wu2, wd2 = _exchange_wait(late, False, after, "gather_late_weights_wait", dev)
        return dict(wout=wout, wg2=wg2, wu2=wu2, wd2=wd2)

    small_g = small_g.reshape(N_DEV, 32, 128)
    meta_full = small_g[:, :N_META].transpose(1, 0, 2).reshape(N_META, D_MODEL)
    wa2_full = small_g[:, N_META:, :32].transpose(1, 0, 2).reshape(16, 256)
    w = dict(
        ffn1_pre=W["ffn1_pre_norm"] + late[4][0, 0], ffn1_post=W["ffn1_post_norm"], mix_pre=W["mix_pre_norm"],
        mix_post=W["mix_post_norm"], ffn2_pre=W["ffn2_pre_norm"], ffn2_post=W["ffn2_post_norm"], b_a=W["gla_b_a"],
        gla_norm=W["gla_out_norm"], sinks=W["swa_sinks"], swa_norm=W["swa_out_norm"], wg1=wg1, wu1=wu1, wd1=wd1,
        wa2=jnp.pad(wa2_full, ((0, 112), (0, 0))))

    in_flight = []

    def on_grads(group, grads):
        parts = []
        for nm, p in grads.items():
            if nm == "win":
                p = _win_unpad_rows(p).reshape(N_DEV, WIN_SHARD, D_MODEL)
                p = jnp.pad(p, ((0, 0), (0, WIN_SHARD_PAD - WIN_SHARD), (0, 0))).reshape(N_DEV * WIN_SHARD_PAD, D_MODEL)
            parts.append(p)
        started = _exchange_start(parts, True, "scatter_" + group + "_start")
        in_flight.append((group, list(grads), started))
        return started[4][0, 0]

    front = jnp.zeros((PAD_ROWS, D_MODEL), F32)
    h0 = jnp.concatenate([front, meta_full, x[0]], axis=0)
    tgt = jnp.concatenate([jnp.zeros((BLK, D_MODEL), F32), loss_target[0]], axis=0)
    loss, dh0, g = _local_step(h0, tgt, w, late_weights, on_grads)
    grad_x = dh0[BLK:][None]

    packed = jnp.concatenate([g["b_a"][0:1], g["gla_norm"][0:1], g["sinks"][0:1], g["swa_norm"][0:1]], axis=1)
    slab = jnp.concatenate([g[k][0:1] for k in SLAB_VECTORS] + [packed, jnp.full((1, D_MODEL), loss, F32),
                           g["wa2"][:16].reshape(4, D_MODEL), jnp.zeros((4, D_MODEL), F32), dh0[PAD_ROWS:BLK]], axis=0)
    tot = _all_reduce_small(slab)
    loss = tot[7, 0]
    small_grads = dict(
        ffn1_pre_norm=tot[0:1], ffn1_post_norm=tot[1:2], mix_pre_norm=tot[2:3], mix_post_norm=tot[3:4],
        ffn2_pre_norm=tot[4:5], ffn2_post_norm=tot[5:6], gla_b_a=tot[6:7, 0:256], gla_out_norm=tot[6:7, 256:384],
        swa_sinks=tot[6:7, 384:392], swa_out_norm=tot[6:7, 512:1024],
        gla_w_a2=lax.dynamic_slice_in_dim(tot[8:12].reshape(16, 256), dev * 32, 32, axis=1)[None],
        meta_tokens=lax.dynamic_slice_in_dim(tot[16:32], dev * 128, 128, axis=1))

    last_started = in_flight[-1][2][4]
    sums = {}
    for group, names, started in in_flight:
        lands = _exchange_wait(started, True, last_started, "scatter_" + group + "_wait", dev)
        for nm, s in zip(names, _sum_partials(lands, "sum_" + group)):
            sums[nm] = s
    big = dict(ffn1_w_gate=sums["wg1"].T[None], ffn1_w_up=sums["wu1"].T[None], ffn1_w_down=sums["wd1"][None],
               w_in=sums["win"][:WIN_SHARD].T[None], w_out=sums["wout"][None],
               ffn2_w_gate=sums["wg2"].T[None], ffn2_w_up=sums["wu2"].T[None], ffn2_w_down=sums["wd2"][None])
    grads = {**big, **small_grads}

    delta, new_m, new_v = {}, {}, {}
    for n in big:
        d_, m_, v_ = _adamw([W[n][0]], [grads[n][0]], [M[n][0]], [V[n][0]], "adamw_" + n)
        delta[n], new_m[n], new_v[n] = d_[0][None], m_[0][None], v_[0][None]
    names = [n for n in WEIGHT_NAMES if n not in big]
    two_d = lambda a: a.reshape(-1, a.shape[-1])
    d_, m_, v_ = _adamw([two_d(W[n]) for n in names], [two_d(grads[n]) for n in names],
                        [two_d(M[n]) for n in names], [two_d(V[n]) for n in names], "adamw_small")
    for k, n in enumerate(names):
        delta[n], new_m[n], new_v[n] = d_[k].reshape(W[n].shape), m_[k].reshape(W[n].shape), v_[k].reshape(W[n].shape)
    return (loss, grad_x, *[grads[n] for n in WEIGHT_NAMES], *[delta[n] for n in WEIGHT_NAMES],
            *[new_m[n] for n in WEIGHT_NAMES], *[new_v[n] for n in WEIGHT_NAMES])
```

```python
import functools

import jax
import jax.numpy as jnp
from jax import lax
from jax.experimental import pallas as pl
from jax.experimental.pallas import tpu as pltpu

F32, BF16 = jnp.float32, jnp.bfloat16

D_MODEL = 1024
D_FF = 2816
N_META = 16
BLK = 128
PAD_ROWS = BLK - N_META
GLA_DK = 64
SWA_HD = 64
SWA_HEADS = 8
GLA_TAU = 16.0
NORM_EPS = 1e-6
NEG_INF = -1e30
ROPE_THETA = 10000.0
P_GQ, P_GK, P_GV, P_GG, P_SQ, P_SK, P_SV, P_GA, P_END = 0, 256, 512, 1024, 1536, 2048, 2176, 2304, 2432
D_IN = 2320
IN_SPLITS = (256, 256, 512, 512, 16, 512, 128, 128)
FF_TILE = 2816
WGRAD_TILE_MAX = 2432
N_DEV = 8
MESH = pl.DeviceIdType.MESH

ADAM_LR, ADAM_B1, ADAM_B2, ADAM_EPS, ADAM_WD, ADAM_STEP = 0.001, 0.9, 0.999, 1e-08, 0.01, 10

V7X_VMEM_BYTES = 64 << 20
VMEM_SPEC = pl.BlockSpec(memory_space=pltpu.VMEM)
SMEM_SPEC = pl.BlockSpec(memory_space=pltpu.SMEM)
ANY_SPEC = pl.BlockSpec(memory_space=pl.ANY)


def _params(semantics, vmem_mb=56):
    return pltpu.CompilerParams(dimension_semantics=semantics, vmem_limit_bytes=vmem_mb << 20)


def _row_tile(rows):
    return 320 if rows % 320 == 0 else BLK


def _nn(a, b):
    return lax.dot_general(a, b, (((1,), (0,)), ((), ())), preferred_element_type=F32)


def _nt(a, b):
    return lax.dot_general(a, b, (((1,), (1,)), ((), ())), preferred_element_type=F32)


def _tn(a, b):
    return lax.dot_general(a, b, (((0,), (0,)), ((), ())), preferred_element_type=F32)


def _rms(x):
    r = lax.rsqrt(jnp.mean(x * x, axis=-1, keepdims=True) + NORM_EPS)
    return x * r, r


def _rms_bwd(xn, r, w, dy):
    g = dy * w
    return r * (g - xn * jnp.mean(g * xn, axis=-1, keepdims=True))


def _sigmoid(x):
    return 1.0 / (1.0 + jnp.exp(-x))


def _colsum(x):
    return jnp.sum(x, axis=0, keepdims=True)


def _split_bf16(x):
    hi = x.astype(BF16)
    lo = (x - hi.astype(F32)).astype(BF16)
    return hi, lo


def _tri(lower):
    r = lax.broadcasted_iota(jnp.int32, (BLK, BLK), 0)
    c = lax.broadcasted_iota(jnp.int32, (BLK, BLK), 1)
    return (r >= c) if lower else (c >= r)


def _half_mask(width, half):
    lane = lax.broadcasted_iota(jnp.int32, (1, width), 1)
    return ((lane % 128) < 64) if half == 0 else ((lane % 128) >= 64)


def _rot_half(x):
    w = x.shape[-1]
    lane = lax.broadcasted_iota(jnp.int32, (1, w), 1)
    return jnp.where((lane % SWA_HD) < SWA_HD // 2, -pltpu.roll(x, w - SWA_HD // 2, 1), pltpu.roll(x, SWA_HD // 2, 1))


def _row_spec(tm, cols):
    return pl.BlockSpec((tm, cols), lambda i: (i, 0))


def _acc_spec(cols):
    return pl.BlockSpec((8, cols), lambda i: (0, 0))


def _acc_add(ref, first, value):
    @pl.when(first)
    def _():
        ref[...] = jnp.zeros_like(ref)
    ref[0:1, :] += value


def _ffn_fwd(h, gpre, wg_t, wu_t, wd, gpost, tgt=None):
    rows = h.shape[0]
    tm = _row_tile(rows)
    nf = D_FF // FF_TILE
    with_loss = tgt is not None

    def body(*refs):
        if with_loss:
            (h_ref, gpre_ref, wg_ref, wu_ref, wd_ref, gpost_ref, t_ref,
             ho_ref, a_ref, b_ref, f_ref, dy_ref, loss_ref, acc) = refs
        else:
            (h_ref, gpre_ref, wg_ref, wu_ref, wd_ref, gpost_ref, ho_ref, a_ref, b_ref, f_ref, acc) = refs
        i = pl.program_id(0)
        h_in = h_ref[...]
        hn, _ = _rms(h_in)
        n16 = (hn * gpre_ref[...]).astype(BF16)
        for j in range(nf):
            cols = slice(j * FF_TILE, (j + 1) * FF_TILE)
            a = _nt(n16, wg_ref[cols, :])
            b = _nt(n16, wu_ref[cols, :])
            a_ref[:, cols] = a.astype(BF16)
            b_ref[:, cols] = b.astype(BF16)
            s16 = (a * _sigmoid(a) * b).astype(BF16)
            part = _nn(s16, wd_ref[cols, :])
            if j == 0:
                acc[...] = part
            else:
                acc[...] += part
        f = acc[...]
        f_ref[...] = f
        fn, _ = _rms(f)
        y = h_in + 0.5 * (fn * gpost_ref[...])
        ho_ref[...] = y
        if with_loss:
            row = i * tm + lax.broadcasted_iota(jnp.int32, (tm, 1), 0)
            err = jnp.where(row >= BLK, y - t_ref[...], 0.0)
            dy_ref[...] = err * (1.0 / D_MODEL)
            part = 0.5 * jnp.sum(jnp.sum(err * err, axis=-1, keepdims=True) * (1.0 / D_MODEL), axis=0, keepdims=True)

            @pl.when(i == 0)
            def _():
                loss_ref[...] = jnp.zeros_like(loss_ref)
            loss_ref[...] += part

    row_f32 = _row_spec(tm, D_MODEL)
    in_specs = [row_f32, VMEM_SPEC, VMEM_SPEC, VMEM_SPEC, VMEM_SPEC, VMEM_SPEC]
    out_specs = [row_f32, _row_spec(tm, D_FF), _row_spec(tm, D_FF), row_f32]
    out_shape = [jax.ShapeDtypeStruct((rows, D_MODEL), F32), jax.ShapeDtypeStruct((rows, D_FF), BF16),
                 jax.ShapeDtypeStruct((rows, D_FF), BF16), jax.ShapeDtypeStruct((rows, D_MODEL), F32)]
    args = [h, gpre, wg_t, wu_t, wd, gpost]
    if with_loss:
        in_specs.append(row_f32)
        args.append(tgt)
        out_specs += [row_f32, pl.BlockSpec((8, 128), lambda i: (0, 0))]
        out_shape += [jax.ShapeDtypeStruct((rows, D_MODEL), F32), jax.ShapeDtypeStruct((8, 128), F32)]
    return pl.pallas_call(
        body, name="ffn_fwd_loss" if with_loss else "ffn_fwd", grid=(rows // tm,),
        in_specs=in_specs, out_specs=out_specs, out_shape=out_shape,
        scratch_shapes=[pltpu.VMEM((tm, D_MODEL), F32)],
        compiler_params=_params(("arbitrary",)),
    )(*args)


def _ffn_bwd_act(dh_out, h, a, b, f, gpre, gpost, wg_t, wu_t, wd, name):
    rows = h.shape[0]
    tm = _row_tile(rows)
    nf = D_FF // FF_TILE

    def body(dho_ref, h_ref, a_ref, b_ref, f_ref, gpre_ref, gpost_ref, wg_ref, wu_ref, wd_ref,
             dh_ref, da_ref, db_ref, df_ref, n_ref, dgpre_ref, dgpost_ref, acc):
        first = pl.program_id(0) == 0
        dho = dho_ref[...]
        drr = 0.5 * dho
        fn, rf = _rms(f_ref[...])
        _acc_add(dgpost_ref, first, _colsum(drr * fn))
        df16 = _rms_bwd(fn, rf, gpost_ref[...], drr).astype(BF16)
        df_ref[...] = df16
        hn, rh = _rms(h_ref[...])
        n_ref[...] = (hn * gpre_ref[...]).astype(BF16)
        for j in range(nf):
            cols = slice(j * FF_TILE, (j + 1) * FF_TILE)
            ds = _nt(df16, wd_ref[cols, :])
            av = a_ref[:, cols].astype(F32)
            bv = b_ref[:, cols].astype(F32)
            sg = _sigmoid(av)
            db16 = (ds * (av * sg)).astype(BF16)
            da16 = (ds * bv * (sg * (1.0 + av * (1.0 - sg)))).astype(BF16)
            da_ref[:, cols] = da16
            db_ref[:, cols] = db16
            part = _nn(da16, wg_ref[cols, :]) + _nn(db16, wu_ref[cols, :])
            if j == 0:
                acc[...] = part
            else:
                acc[...] += part
        dn = acc[...]
        _acc_add(dgpre_ref, first, _colsum(dn * hn))
        dh_ref[...] = dho + _rms_bwd(hn, rh, gpre_ref[...], dn)

    row_f32 = _row_spec(tm, D_MODEL)
    row_ff = _row_spec(tm, D_FF)
    return pl.pallas_call(
        body, name=name, grid=(rows // tm,),
        in_specs=[row_f32, row_f32, row_ff, row_ff, row_f32, VMEM_SPEC, VMEM_SPEC, VMEM_SPEC, VMEM_SPEC, VMEM_SPEC],
        out_specs=[row_f32, row_ff, row_ff, row_f32, row_f32, _acc_spec(D_MODEL), _acc_spec(D_MODEL)],
        out_shape=[jax.ShapeDtypeStruct((rows, D_MODEL), F32), jax.ShapeDtypeStruct((rows, D_FF), BF16),
                   jax.ShapeDtypeStruct((rows, D_FF), BF16), jax.ShapeDtypeStruct((rows, D_MODEL), BF16),
                   jax.ShapeDtypeStruct((rows, D_MODEL), BF16), jax.ShapeDtypeStruct((8, D_MODEL), F32),
                   jax.ShapeDtypeStruct((8, D_MODEL), F32)],
        scratch_shapes=[pltpu.VMEM((tm, D_MODEL), F32)],
        compiler_params=_params(("arbitrary",)),
    )(dh_out, h, a, b, f, gpre, gpost, wg_t, wu_t, wd)


def _wgrad(lhs, rhs, name, gate=None):
    rows, width = lhs.shape
    tm = 1664 if rows % 1664 == 0 else BLK
    tf = width // 2 if width > WGRAD_TILE_MAX else width
    nr = rows // tm
    gated = gate is not None

    def body(*refs):
        if gated:
            g_ref, l_ref, r_ref, o_ref, acc = refs
            gv = g_ref[...].astype(F32)
            lv = (gv * _sigmoid(gv) * l_ref[...].astype(F32)).astype(BF16)
        else:
            l_ref, r_ref, o_ref, acc = refs
            lv = l_ref[...]
        i = pl.program_id(1)
        part = _tn(lv, r_ref[...])

        @pl.when(i == 0)
        def _():
            acc[...] = part

        @pl.when(i > 0)
        def _():
            acc[...] += part

        @pl.when(i == nr - 1)
        def _():
            o_ref[...] = acc[...].astype(BF16)

    l_spec = pl.BlockSpec((tm, tf), lambda j, i: (i, j))
    r_spec = pl.BlockSpec((tm, D_MODEL), lambda j, i: (i, 0))
    return pl.pallas_call(
        body, name=name, grid=(width // tf, nr),
        in_specs=([l_spec] if gated else []) + [l_spec, r_spec],
        out_specs=pl.BlockSpec((tf, D_MODEL), lambda j, i: (j, 0)),
        out_shape=jax.ShapeDtypeStruct((width, D_MODEL), BF16),
        scratch_shapes=[pltpu.VMEM((tf, D_MODEL), F32)],
        compiler_params=_params(("arbitrary", "arbitrary")),
    )(*([gate] if gated else []), lhs, rhs)


def _chunk_cumsum(x, lower):
    tri = jnp.where(_tri(lower), 1.0, 0.0).astype(BF16)
    hi, lo = _split_bf16(x)
    return _nn(tri, hi) + _nn(tri, lo)


def _mix_in(h, g, win_p, wa2_p, b_a, cos, sin):
    rows = h.shape[0]
    tm = 640 if rows % 640 == 0 else BLK

    def body(h_ref, g_ref, win_ref, wa2_ref, ba_ref, cos_ref, sin_ref,
             gq_ref, gk_ref, gv_ref, gg_ref, sq_ref, sk_ref, sv_ref, ga_ref, loga_ref, bc_ref, n_ref):
        hn, _ = _rms(h_ref[...])
        n16 = (hn * g_ref[...]).astype(BF16)
        n_ref[...] = n16
        proj = _nt(n16, win_ref[...])
        gq_ref[...] = proj[:, P_GQ:P_GK]
        gk_ref[...] = proj[:, P_GK:P_GV]
        gv_ref[...] = proj[:, P_GV:P_GG].astype(BF16)
        gg_ref[...] = proj[:, P_GG:P_SQ]
        c1, s1 = cos_ref[...], sin_ref[...]
        c4 = jnp.concatenate([c1, c1, c1, c1], axis=1)
        s4 = jnp.concatenate([s1, s1, s1, s1], axis=1)
        sq = proj[:, P_SQ:P_SK]
        sk = proj[:, P_SK:P_SV]
        sq_ref[...] = (sq * c4 + _rot_half(sq) * s4).astype(BF16)
        sk_ref[...] = (sk * c1 + _rot_half(sk) * s1).astype(BF16)
        sv_ref[...] = proj[:, P_SV:P_GA].astype(BF16)
        ga = proj[:, P_GA:P_END]
        ga_ref[...] = ga
        z = _nn(ga, wa2_ref[...]) + ba_ref[...]
        loga = (jnp.minimum(z, 0.0) - jnp.log(1.0 + jnp.exp(-jnp.abs(z)))) * (1.0 / GLA_TAU)
        loga_ref[...] = loga
        for c in range(tm // BLK):
            rs = slice(c * BLK, (c + 1) * BLK)
            bc_ref[rs, :] = _chunk_cumsum(loga[rs, :], True)

    f32 = lambda c: jax.ShapeDtypeStruct((rows, c), F32)
    b16 = lambda c: jax.ShapeDtypeStruct((rows, c), BF16)
    rs = lambda c: _row_spec(tm, c)
    return pl.pallas_call(
        body, name="mix_in", grid=(rows // tm,),
        in_specs=[rs(D_MODEL), VMEM_SPEC, VMEM_SPEC, VMEM_SPEC, VMEM_SPEC, rs(128), rs(128)],
        out_specs=[rs(256), rs(256), rs(512), rs(512), rs(512), rs(128), rs(128), rs(128), rs(256), rs(256), rs(D_MODEL)],
        out_shape=[f32(256), f32(256), b16(512), f32(512), b16(512), b16(128), b16(128), f32(128), f32(256), f32(256),
                   b16(D_MODEL)],
        compiler_params=_params(("arbitrary",)),
    )(h, g, win_p, wa2_p, b_a, cos, sin)


def _gla_factors(q, k, bc):
    bm = bc[BLK // 2 - 1:BLK // 2, :]
    bl = bc[BLK - 1:BLK, :]
    e_q, e_k, e_qe, e_kd = jnp.exp(bc - bm), jnp.exp(bm - bc), jnp.exp(bc), jnp.exp(bl - bc)
    return (q * e_q, k * e_k, q * e_qe, k * e_kd), (e_q, e_k, e_qe, e_kd), jnp.exp(bl)


def _gla_fwd(gq, gk, gv, gg, bc, wgn):
    rows = gq.shape[0]
    nc = rows // BLK
    scale = GLA_DK ** -0.5

    def body(q_ref, k_ref, v_ref, gg_ref, bc_ref, wgn_ref, o_ref, cat_ref, sp_ref, st):
        @pl.when(pl.program_id(0) == 0)
        def _():
            st[...] = jnp.zeros_like(st)
        low = _tri(True)
        wgn_v = wgn_ref[...]
        for p in range(2):
            sl = slice(128 * p, 128 * p + 128)
            (qt, kt, qe, kd), _, ebl = _gla_factors(q_ref[:, sl] * scale, k_ref[:, sl], bc_ref[:, sl])
            s_prev = st[p]
            sp_ref[0, p] = s_prev
            s16 = s_prev.astype(BF16)
            qt16 = qt.astype(BF16)
            s_new = s_prev * ebl
            for hh in range(2):
                hs = slice(128 * (2 * p + hh), 128 * (2 * p + hh) + 128)
                lm = _half_mask(128, hh)
                vh = v_ref[:, hs]
                pm = jnp.where(low, _nt(qt16, jnp.where(lm, kt, 0.0).astype(BF16)), 0.0)
                o = _nn(pm.astype(BF16), vh) + _nt(jnp.where(lm, qe, 0.0).astype(BF16), s16)
                s_new = s_new + _tn(vh, jnp.where(lm, kd, 0.0).astype(BF16))
                o_ref[:, hs] = o
                on, _ = _rms(o)
                gate = gg_ref[:, hs]
                cat_ref[:, hs] = (on * wgn_v * (gate * _sigmoid(gate))).astype(BF16)
            st[p] = s_new

    rs = lambda c: _row_spec(BLK, c)
    return pl.pallas_call(
        body, name="gla_fwd", grid=(nc,),
        in_specs=[rs(256), rs(256), rs(512), rs(512), rs(256), VMEM_SPEC],
        out_specs=[rs(512), rs(512), pl.BlockSpec((1, 2, 128, 128), lambda i: (i, 0, 0, 0))],
        out_shape=[jax.ShapeDtypeStruct((rows, 512), F32), jax.ShapeDtypeStruct((rows, 512), BF16),
                   jax.ShapeDtypeStruct((nc, 2, 128, 128), F32)],
        scratch_shapes=[pltpu.VMEM((2, 128, 128), F32)],
        compiler_params=_params(("arbitrary",)),
    )(gq, gk, gv, gg, bc, wgn)


def _gla_bwd(dcat, o_all, gq, gk, gv, gg, bc, sp, wgn):
    rows = gq.shape[0]
    nc = rows // BLK
    scale = GLA_DK ** -0.5

    def body(dc_ref, o_ref, q_ref, k_ref, v_ref, gg_ref, bc_ref, sp_ref, wgn_ref,
             dq_ref, dk_ref, dv_ref, dgg_ref, dla_ref, dwgn_ref, dst):
        first = pl.program_id(0) == 0

        @pl.when(first)
        def _():
            dst[...] = jnp.zeros_like(dst)
        low, upp = _tri(True), _tri(False)
        last_row = lax.broadcasted_iota(jnp.int32, (BLK, 1), 0) == BLK - 1
        wgn_v = wgn_ref[...]
        dwgn = jnp.zeros((1, 128), F32)
        for p in range(2):
            sl = slice(128 * p, 128 * p + 128)
            (qt, kt, qe, kd), (e_q, e_k, e_qe, e_kd), ebl = _gla_factors(
                q_ref[:, sl] * scale, k_ref[:, sl], bc_ref[:, sl])
            s_prev = sp_ref[0, p]
            s16 = s_prev.astype(BF16)
            ds_next = dst[p]
            ds16 = ds_next.astype(BF16)
            qt16 = qt.astype(BF16)
            ds_new = ds_next * ebl
            dqt = jnp.zeros((BLK, 128), F32)
            dkt = jnp.zeros((BLK, 128), F32)
            dqe = jnp.zeros((BLK, 128), F32)
            dkd = jnp.zeros((BLK, 128), F32)
            for hh in range(2):
                hs = slice(128 * (2 * p + hh), 128 * (2 * p + hh) + 128)
                lm = _half_mask(128, hh)
                on, ro = _rms(o_ref[:, hs])
                gate = gg_ref[:, hs]
                sg = _sigmoid(gate)
                si = gate * sg
                dog = dc_ref[:, hs]
                dwgn = dwgn + _colsum(dog * si * on)
                dgg_ref[:, hs] = dog * (on * wgn_v) * (sg * (1.0 + gate * (1.0 - sg)))
                do16 = _rms_bwd(on, ro, wgn_v, dog * si).astype(BF16)
                vh = v_ref[:, hs]
                ktm16 = jnp.where(lm, kt, 0.0).astype(BF16)
                qtm16 = jnp.where(lm, qt, 0.0).astype(BF16)
                qem16 = jnp.where(lm, qe, 0.0).astype(BF16)
                kdm16 = jnp.where(lm, kd, 0.0).astype(BF16)
                p_t = jnp.where(upp, _nt(ktm16, qt16), 0.0)
                dp_t = jnp.where(upp, _nt(vh, do16), 0.0)
                dp = jnp.where(low, _nt(do16, vh), 0.0)
                dv_ref[:, hs] = _nn(p_t.astype(BF16), do16) + _nt(kdm16, ds16)
                dqt = dqt + _nn(dp.astype(BF16), ktm16)
                dkt = dkt + _nn(dp_t.astype(BF16), qtm16)
                dqe = dqe + jnp.where(lm, _nn(do16, s16), 0.0)
                dkd = dkd + jnp.where(lm, _nn(vh, ds16), 0.0)
                ds_new = ds_new + _tn(do16, qem16)
            debl = _colsum(ds_next * s_prev)
            dq_ref[:, sl] = (dqt * e_q + dqe * e_qe) * scale
            dk_ref[:, sl] = dkt * e_k + dkd * e_kd
            dkd_kd = dkd * kd
            db = dqt * qt - dkt * kt + dqe * qe - dkd_kd
            db = jnp.where(last_row, db + (_colsum(dkd_kd) + debl * ebl), db)
            dla_ref[:, sl] = _chunk_cumsum(db, False)
            dst[p] = ds_new
        _acc_add(dwgn_ref, first, dwgn)

    rev = lambda c: pl.BlockSpec((BLK, c), lambda i: (nc - 1 - i, 0))
    f32 = lambda c: jax.ShapeDtypeStruct((rows, c), F32)
    return pl.pallas_call(
        body, name="gla_bwd", grid=(nc,),
        in_specs=[rev(512), rev(512), rev(256), rev(256), rev(512), rev(512), rev(256),
                  pl.BlockSpec((1, 2, 128, 128), lambda i: (nc - 1 - i, 0, 0, 0)), VMEM_SPEC],
        out_specs=[rev(256), rev(256), rev(512), rev(512), rev(256), _acc_spec(128)],
        out_shape=[f32(256), f32(256), f32(512), f32(512), f32(256), jax.ShapeDtypeStruct((8, 128), F32)],
        scratch_shapes=[pltpu.VMEM((2, 128, 128), F32)],
        compiler_params=_params(("arbitrary",)),
    )(dcat, o_all, gq, gk, gv, gg, bc, sp, wgn)


def _swa_masks(i):
    t = lax.broadcasted_iota(jnp.int32, (BLK, BLK), 0)
    c = lax.broadcasted_iota(jnp.int32, (BLK, BLK), 1)
    own_side = c <= t
    band_ok = i >= jnp.where(own_side, 1, 2)
    meta_ok = (c % N_META) <= jnp.where(i >= 1, N_META, t - PAD_ROWS)
    return own_side, band_ok, meta_ok, c // N_META


def _swa_blocks(ref, i):
    prev = pl.multiple_of(jnp.maximum(i - 1, 0) * BLK, BLK)
    own = pl.multiple_of(i * BLK, BLK)
    return ref[pl.ds(prev, BLK), :], ref[pl.ds(own, BLK), :], prev, own


def _swa_meta_operand(ref):
    blk = ref[0:BLK, :]
    swapped = pltpu.roll(blk, 64, 1)
    lo = jnp.where(_half_mask(128, 0), blk, swapped)
    hi = jnp.where(_half_mask(128, 1), blk, swapped)
    meta = jnp.concatenate([lo, lo, hi, hi], axis=1)[PAD_ROWS:BLK, :]
    tiled = jnp.concatenate([meta] * SWA_HEADS, axis=0)
    j = lax.broadcasted_iota(jnp.int32, tiled.shape, 0)
    lane = lax.broadcasted_iota(jnp.int32, tiled.shape, 1)
    return jnp.where(j // N_META == lane // SWA_HD, tiled, jnp.zeros_like(tiled))


def _swa_meta_fold(acc):
    out = jnp.zeros((N_META, 128), F32)
    for hd in range(SWA_HEADS):
        half, kv = hd % 2, hd // 4
        piece = acc[N_META * hd:N_META * (hd + 1), 128 * (hd // 2):128 * (hd // 2) + 128]
        piece = jnp.where(_half_mask(128, half), piece, 0.0)
        out = out + (piece if half == kv else pltpu.roll(piece, 64, 1))
    return out


def _by_head(group, per_head):
    out = jnp.zeros((BLK, BLK), F32)
    for hd, v in enumerate(per_head):
        out = jnp.where(group == hd, v, out)
    return out


def _place(x, kv):
    if kv == 0:
        lo = jnp.where(_half_mask(128, 0), x, jnp.zeros_like(x))
        return lo, pltpu.roll(lo, 64, 1)
    hi = jnp.where(_half_mask(128, 1), x, jnp.zeros_like(x))
    return pltpu.roll(hi, 64, 1), hi


def _swa_fwd(sq, sk, sv, sinks, wn):
    rows = sq.shape[0]
    nb = rows // BLK
    scale = SWA_HD ** -0.5

    def body(q_ref, k_ref, v_ref, sink_ref, wn_ref, o_ref, cat_ref, lse_ref, kp, vp):
        i = pl.program_id(0)

        @pl.when(i == 0)
        def _():
            kp[...] = _swa_meta_operand(k_ref)
            vp[...] = _swa_meta_operand(v_ref)
        own_side, band_ok, meta_ok, group = _swa_masks(i)
        k_prev, k_own, _, _ = _swa_blocks(k_ref, i)
        v_prev, v_own, _, _ = _swa_blocks(v_ref, i)
        kz_own, kz_prev = (_place(k_own, 0), _place(k_own, 1)), (_place(k_prev, 0), _place(k_prev, 1))
        vz_own, vz_prev = (_place(v_own, 0), _place(v_own, 1)), (_place(v_prev, 0), _place(v_prev, 1))
        q_all = q_ref[...]
        s_meta = jnp.where(meta_ok, _nt(q_all, kp[...]) * scale, NEG_INF)
        s_band, m = [], []
        for hd in range(SWA_HEADS):
            kv, half = hd // 4, hd % 2
            q_pair = q_all[:, 128 * (hd // 2):128 * (hd // 2) + 128]
            s = jnp.where(own_side, _nt(q_pair, kz_own[kv][half]), _nt(q_pair, kz_prev[kv][half])) * scale
            s = jnp.where(band_ok, s, NEG_INF)
            top = jnp.maximum(jnp.max(s, axis=-1, keepdims=True),
                              jnp.max(jnp.where(group == hd, s_meta, NEG_INF), axis=-1, keepdims=True))
            s_band.append(s)
            m.append(jnp.maximum(top, sink_ref[0, hd]))
        e_meta = jnp.exp(s_meta - _by_head(group, m))
        o_meta = _nn(e_meta.astype(BF16), vp[...])
        outs = []
        for pr in range(4):
            o_pair = o_meta[:, 128 * pr:128 * pr + 128]
            rden = []
            for half in range(2):
                hd = 2 * pr + half
                kv = hd // 4
                e = jnp.exp(s_band[hd] - m[hd])
                den = (jnp.sum(e, axis=-1, keepdims=True)
                       + jnp.sum(jnp.where(group == hd, e_meta, 0.0), axis=-1, keepdims=True)
                       + jnp.exp(sink_ref[0, hd] - m[hd]))
                lse_ref[:, hd:hd + 1] = m[hd] + jnp.log(den)
                rden.append(1.0 / den)
                o_pair = (o_pair + _nn(jnp.where(own_side, e, 0.0).astype(BF16), vz_own[kv][half])
                          + _nn(jnp.where(own_side, 0.0, e).astype(BF16), vz_prev[kv][half]))
            outs.append(o_pair * jnp.where(_half_mask(128, 0), rden[0], rden[1]))
        o = jnp.concatenate(outs, axis=1)
        o_ref[...] = o
        on, _ = _rms(o)
        cat_ref[...] = (on * wn_ref[...]).astype(BF16)

    return pl.pallas_call(
        body, name="swa_fwd", grid=(nb,),
        in_specs=[_row_spec(BLK, 512), VMEM_SPEC, VMEM_SPEC, SMEM_SPEC, VMEM_SPEC],
        out_specs=[_row_spec(BLK, 512), _row_spec(BLK, 512), _row_spec(BLK, SWA_HEADS)],
        out_shape=[jax.ShapeDtypeStruct((rows, 512), F32), jax.ShapeDtypeStruct((rows, 512), BF16),
                   jax.ShapeDtypeStruct((rows, SWA_HEADS), F32)],
        scratch_shapes=[pltpu.VMEM((BLK, 512), BF16), pltpu.VMEM((BLK, 512), BF16)],
        compiler_params=_params(("arbitrary",)),
    )(sq, sk, sv, sinks, wn)


def _swa_bwd(dcat, o_all, sq, sk, sv, lse, sinks, wn):
    rows = sq.shape[0]
    nb = rows // BLK
    scale = SWA_HD ** -0.5

    def body(dc_ref, o_ref, q_ref, k_ref, v_ref, lse_ref, sink_ref, wn_ref, dq_ref, dk_ref, dv_ref, dsink_ref, dwn_ref,
             kp, vp, dkp, dvp):
        i = pl.program_id(0)
        first = i == 0

        @pl.when(first)
        def _():
            dk_ref[...] = jnp.zeros_like(dk_ref)
            dv_ref[...] = jnp.zeros_like(dv_ref)
            dkp[...] = jnp.zeros_like(dkp)
            dvp[...] = jnp.zeros_like(dvp)
            kp[...] = _swa_meta_operand(k_ref)
            vp[...] = _swa_meta_operand(v_ref)
        own_side, band_ok, meta_ok, group = _swa_masks(i)
        k_prev, k_own, prev, own = _swa_blocks(k_ref, i)
        v_prev, v_own, _, _ = _swa_blocks(v_ref, i)
        kz_own, kz_prev = (_place(k_own, 0), _place(k_own, 1)), (_place(k_prev, 0), _place(k_prev, 1))
        vz_own, vz_prev = (_place(v_own, 0), _place(v_own, 1)), (_place(v_prev, 0), _place(v_prev, 1))
        o = o_ref[...]
        on, ro = _rms(o)
        dc = dc_ref[...]
        _acc_add(dwn_ref, first, _colsum(dc * on))
        do = _rms_bwd(on, ro, wn_ref[...], dc)
        do_o = do * o
        do16 = do.astype(BF16)
        q_all = q_ref[...]
        lse = [lse_ref[:, hd:hd + 1] for hd in range(SWA_HEADS)]
        delta = [jnp.sum(jnp.where(_half_mask(128, hd % 2), do_o[:, 128 * (hd // 2):128 * (hd // 2) + 128], 0.0),
                         axis=-1, keepdims=True) for hd in range(SWA_HEADS)]
        s_meta = jnp.where(meta_ok, _nt(q_all, kp[...]) * scale, NEG_INF)
        p_meta = jnp.exp(s_meta - _by_head(group, lse))
        ds_meta16 = (p_meta * (_nt(do16, vp[...]) - _by_head(group, delta)) * scale).astype(BF16)
        dq_meta = _nn(ds_meta16, kp[...])
        dkp[...] += _tn(ds_meta16, q_all)
        dvp[...] += _tn(p_meta.astype(BF16), do16)
        lane8 = lax.broadcasted_iota(jnp.int32, (1, 128), 1)
        dsink = jnp.zeros((1, 128), F32)
        dk_own, dk_prev = jnp.zeros((BLK, 128), F32), jnp.zeros((BLK, 128), F32)
        dv_own, dv_prev = jnp.zeros((BLK, 128), F32), jnp.zeros((BLK, 128), F32)
        dqs = []
        for pr in range(4):
            ps = slice(128 * pr, 128 * pr + 128)
            q_pair = q_all[:, ps]
            do_pair = do16[:, ps]
            dq_pair = dq_meta[:, ps]
            for half in range(2):
                hd = 2 * pr + half
                kv = hd // 4
                hm = _half_mask(128, half)

                def back(x):
                    x = jnp.where(hm, x, 0.0)
                    return x if half == kv else pltpu.roll(x, 64, 1)
                s = jnp.where(own_side, _nt(q_pair, kz_own[kv][half]), _nt(q_pair, kz_prev[kv][half])) * scale
                prob = jnp.exp(jnp.where(band_ok, s, NEG_INF) - lse[hd])
                dsink = dsink + jnp.where(lane8 == hd, -jnp.sum(jnp.exp(sink_ref[0, hd] - lse[hd]) * delta[hd]), 0.0)
                dp = jnp.where(own_side, _nt(do_pair, vz_own[kv][half]), _nt(do_pair, vz_prev[kv][half]))
                ds = prob * (dp - delta[hd]) * scale
                ds_own, ds_prev = jnp.where(own_side, ds, 0.0).astype(BF16), jnp.where(own_side, 0.0, ds).astype(BF16)
                p_own, p_prev = jnp.where(own_side, prob, 0.0).astype(BF16), jnp.where(own_side, 0.0, prob).astype(BF16)
                dq_pair = dq_pair + _nn(ds_own, kz_own[kv][half]) + _nn(ds_prev, kz_prev[kv][half])
                dk_own, dk_prev = dk_own + back(_tn(ds_own, q_pair)), dk_prev + back(_tn(ds_prev, q_pair))
                dv_own, dv_prev = dv_own + back(_tn(p_own, do_pair)), dv_prev + back(_tn(p_prev, do_pair))
            dqs.append(dq_pair)
        dq_ref[...] = jnp.concatenate(dqs, axis=1)
        _acc_add(dsink_ref, first, dsink)
        dk_ref[pl.ds(prev, BLK), :] += dk_prev
        dk_ref[pl.ds(own, BLK), :] += dk_own
        dv_ref[pl.ds(prev, BLK), :] += dv_prev
        dv_ref[pl.ds(own, BLK), :] += dv_own

        @pl.when(i == nb - 1)
        def _():
            dk_ref[PAD_ROWS:BLK, :] += _swa_meta_fold(dkp[...])
            dv_ref[PAD_ROWS:BLK, :] += _swa_meta_fold(dvp[...])

    full = pl.BlockSpec((rows, 128), lambda i: (0, 0))
    return pl.pallas_call(
        body, name="swa_bwd", grid=(nb,),
        in_specs=[_row_spec(BLK, 512), _row_spec(BLK, 512), _row_spec(BLK, 512), VMEM_SPEC, VMEM_SPEC,
                  _row_spec(BLK, SWA_HEADS), SMEM_SPEC, VMEM_SPEC],
        out_specs=[_row_spec(BLK, 512), full, full, _acc_spec(128), _acc_spec(512)],
        out_shape=[jax.ShapeDtypeStruct((rows, 512), F32), jax.ShapeDtypeStruct((rows, 128), F32),
                   jax.ShapeDtypeStruct((rows, 128), F32), jax.ShapeDtypeStruct((8, 128), F32),
                   jax.ShapeDtypeStruct((8, 512), F32)],
        scratch_shapes=[pltpu.VMEM((BLK, 512), BF16), pltpu.VMEM((BLK, 512), BF16),
                        pltpu.VMEM((BLK, 512), F32), pltpu.VMEM((BLK, 512), F32)],
        compiler_params=_params(("arbitrary",)),
    )(dcat, o_all, sq, sk, sv, lse, sinks, wn)


def _mix_out(h, cat_g, cat_s, wout, gpost):
    rows = h.shape[0]
    tm = _row_tile(rows)

    def body(h_ref, cg_ref, cs_ref, w_ref, g_ref, ho_ref, m_ref):
        m = _nn(cg_ref[...], w_ref[0:512, :]) + _nn(cs_ref[...], w_ref[512:1024, :])
        m_ref[...] = m
        mn, _ = _rms(m)
        ho_ref[...] = h_ref[...] + mn * g_ref[...]

    row_f32 = _row_spec(tm, D_MODEL)
    return pl.pallas_call(
        body, name="mix_out", grid=(rows // tm,),
        in_specs=[row_f32, _row_spec(tm, 512), _row_spec(tm, 512), VMEM_SPEC, VMEM_SPEC],
        out_specs=[row_f32, row_f32],
        out_shape=[jax.ShapeDtypeStruct((rows, D_MODEL), F32), jax.ShapeDtypeStruct((rows, D_MODEL), F32)],
        compiler_params=_params(("arbitrary",)),
    )(h, cat_g, cat_s, wout, gpost)


def _mix_out_bwd(dh, m, wout, gpost):
    rows = dh.shape[0]
    tm = _row_tile(rows)

    def body(dh_ref, m_ref, w_ref, g_ref, dcg_ref, dcs_ref, dm_ref, dg_ref):
        first = pl.program_id(0) == 0
        dhv = dh_ref[...]
        mn, rm = _rms(m_ref[...])
        _acc_add(dg_ref, first, _colsum(dhv * mn))
        dm16 = _rms_bwd(mn, rm, g_ref[...], dhv).astype(BF16)
        dm_ref[...] = dm16
        dcat = _nt(dm16, w_ref[...])
        dcg_ref[...] = dcat[:, 0:512]
        dcs_ref[...] = dcat[:, 512:1024]

    row_f32 = _row_spec(tm, D_MODEL)
    return pl.pallas_call(
        body, name="mix_out_bwd", grid=(rows // tm,),
        in_specs=[row_f32, row_f32, VMEM_SPEC, VMEM_SPEC],
        out_specs=[_row_spec(tm, 512), _row_spec(tm, 512), row_f32, _acc_spec(D_MODEL)],
        out_shape=[jax.ShapeDtypeStruct((rows, 512), F32), jax.ShapeDtypeStruct((rows, 512), F32),
                   jax.ShapeDtypeStruct((rows, D_MODEL), BF16), jax.ShapeDtypeStruct((8, D_MODEL), F32)],
        compiler_params=_params(("arbitrary",)),
    )(dh, m, wout, gpost)


def _mix_in_bwd(dh_out, h, g, win_p, wa2_p, cos, sin, loga, ga, dgq, dgk, dgv, dgg, dsq, dsk, dsv, dloga):
    rows = h.shape[0]
    tm = _row_tile(rows)

    def body(dho_ref, h_ref, g_ref, win_ref, wa2_ref, cos_ref, sin_ref, loga_ref, ga_ref,
             dgq_ref, dgk_ref, dgv_ref, dgg_ref, dsq_ref, dsk_ref, dsv_ref, dla_ref,
             dh_ref, dproj_ref, dwa2_ref, dg_ref, dba_ref):
        first = pl.program_id(0) == 0
        dz = dla_ref[...] * (1.0 / GLA_TAU) * (1.0 - jnp.exp(GLA_TAU * loga_ref[...]))
        _acc_add(dba_ref, first, _colsum(dz))
        dga = _nt(dz, wa2_ref[...])
        pa = _tn(ga_ref[...], dz)
        c1, s1 = cos_ref[...], sin_ref[...]
        c4 = jnp.concatenate([c1, c1, c1, c1], axis=1)
        s4 = jnp.concatenate([s1, s1, s1, s1], axis=1)
        dq_r, dk_r = dsq_ref[...], dsk_ref[...]
        dsq = dq_r * c4 - _rot_half(dq_r * s4)
        dsk = dk_r * c1 - _rot_half(dk_r * s1)
        dproj16 = jnp.concatenate(
            [dgq_ref[...], dgk_ref[...], dgv_ref[...], dgg_ref[...], dsq, dsk, dsv_ref[...], dga], axis=1).astype(BF16)
        dproj_ref[...] = dproj16
        dn = _nn(dproj16, win_ref[...])

        @pl.when(first)
        def _():
            dwa2_ref[...] = pa

        @pl.when(jnp.logical_not(first))
        def _():
            dwa2_ref[...] += pa
        hn, rh = _rms(h_ref[...])
        _acc_add(dg_ref, first, _colsum(dn * hn))
        dh_ref[...] = dho_ref[...] + _rms_bwd(hn, rh, g_ref[...], dn)

    rs = lambda c: _row_spec(tm, c)
    return pl.pallas_call(
        body, name="mix_in_bwd", grid=(rows // tm,),
        in_specs=[rs(D_MODEL), rs(D_MODEL), VMEM_SPEC, VMEM_SPEC, VMEM_SPEC, rs(128), rs(128), rs(256), rs(128),
                  rs(256), rs(256), rs(512), rs(512), rs(512), rs(128), rs(128), rs(256)],
        out_specs=[rs(D_MODEL), rs(P_END), pl.BlockSpec((128, 256), lambda i: (0, 0)), _acc_spec(D_MODEL), _acc_spec(256)],
        out_shape=[jax.ShapeDtypeStruct((rows, D_MODEL), F32), jax.ShapeDtypeStruct((rows, P_END), BF16),
                   jax.ShapeDtypeStruct((128, 256), F32), jax.ShapeDtypeStruct((8, D_MODEL), F32),
                   jax.ShapeDtypeStruct((8, 256), F32)],
        compiler_params=_params(("arbitrary",)),
    )(dh_out, h, g, win_p, wa2_p, cos, sin, loga, ga, dgq, dgk, dgv, dgg, dsq, dsk, dsv, dloga)


def _rope_tables(rows):
    pos = (jnp.arange(rows, dtype=jnp.int32) - PAD_ROWS).astype(F32)
    inv_freq = 1.0 / (ROPE_THETA ** (jnp.arange(0, SWA_HD, 2, dtype=F32) / SWA_HD))
    ang = pos[:, None] * inv_freq[None, :]
    ang = jnp.concatenate([ang, ang, ang, ang], axis=-1)
    return jnp.cos(ang), jnp.sin(ang)


def _local_step(h0, tgt, w, late_weights=None, on_grads=None):
    rows = h0.shape[0]
    cos, sin = _rope_tables(rows)
    g = {}

    def tell(group, names):
        for nm in names:
            g[nm] = grads_now[nm]
        return 0.0 if on_grads is None else on_grads(group, {nm: grads_now[nm] for nm in names})

    h1, a1, b1, f1 = _ffn_fwd(h0, w["ffn1_pre"], w["wg1"], w["wu1"], w["wd1"], w["ffn1_post"])
    if late_weights is not None:
        w = {**w, **late_weights("win", f1)}
    gq, gk, gv, gg, sq, sk, sv, ga, loga, bc, n2 = _mix_in(h1, w["mix_pre"], w["win"], w["wa2"], w["b_a"], cos, sin)
    o_g, cat_g, sp = _gla_fwd(gq, gk, gv, gg, bc, w["gla_norm"])
    o_s, cat_s, lse = _swa_fwd(sq, sk, sv, w["sinks"], w["swa_norm"])
    if late_weights is not None:
        w = {**w, **late_weights("rest", lse)}
    h2, m = _mix_out(h1, cat_g, cat_s, w["wout"], w["mix_post"])
    h3, a2, b2, f2, dy, loss = _ffn_fwd(h2, w["ffn2_pre"], w["wg2"], w["wu2"], w["wd2"], w["ffn2_post"], tgt)
    del h3
    dh2, da, db, df, n3, g["ffn2_pre"], g["ffn2_post"] = _ffn_bwd_act(
        dy, h2, a2, b2, f2, w["ffn2_pre"], w["ffn2_post"], w["wg2"], w["wu2"], w["wd2"], "ffn2_bwd_act")
    grads_now = dict(wd2=_wgrad(b2, df, "ffn2_wgrad_down", gate=a2), wg2=_wgrad(da, n3, "ffn2_wgrad_gate"),
                     wu2=_wgrad(db, n3, "ffn2_wgrad_up"))
    tok = tell("ffn2", ("wd2", "wg2", "wu2"))
    dcg, dcs, dm, g["mix_post"] = _mix_out_bwd(dh2, m, w["wout"], w["mix_post"] + tok)
    dsq, dsk, dsv, g["sinks"], g["swa_norm"] = _swa_bwd(dcs, o_s, sq, sk, sv, lse, w["sinks"], w["swa_norm"])
    dgq, dgk, dgv, dgg, dloga, g["gla_norm"] = _gla_bwd(dcg, o_g, gq, gk, gv, gg, bc, sp, w["gla_norm"])
    dh1, dproj, g["wa2"], g["mix_pre"], g["b_a"] = _mix_in_bwd(
        dh2, h1, w["mix_pre"], w["win"], w["wa2"], cos, sin, loga, ga, dgq, dgk, dgv, dgg, dsq, dsk, dsv, dloga)
    grads_now = dict(wout=jnp.concatenate([_wgrad(cat_g, dm, "wout_wgrad_gla"), _wgrad(cat_s, dm, "wout_wgrad_swa")], axis=0),
                     win=_wgrad(dproj, n2, "win_wgrad"))
    tok = tell("mix", ("wout", "win"))
    dh0, da, db, df, n1, g["ffn1_pre"], g["ffn1_post"] = _ffn_bwd_act(
        dh1, h0, a1, b1, f1, w["ffn1_pre"] + tok, w["ffn1_post"], w["wg1"], w["wu1"], w["wd1"], "ffn1_bwd_act")
    grads_now = dict(wd1=_wgrad(b1, df, "ffn1_wgrad_down", gate=a1))
    tell("ffn1_down", ("wd1",))
    grads_now = dict(wg1=_wgrad(da, n1, "ffn1_wgrad_gate"))
    tell("ffn1_gate", ("wg1",))
    grads_now = dict(wu1=_wgrad(db, n1, "ffn1_wgrad_up"))
    tell("ffn1_up", ("wu1",))
    return loss[0, 0], dh0, g


def _win_pad_rows(win_t):
    pad = jnp.zeros((P_END - P_GA - 16, win_t.shape[1]), win_t.dtype)
    return jnp.concatenate([win_t[0:1536], win_t[1552:2320], win_t[1536:1552], pad], axis=0)


def _win_unpad_rows(win_p):
    return jnp.concatenate([win_p[0:1536], win_p[P_GA:P_GA + 16], win_p[1536:P_GA]], axis=0)


def _place_on_mesh():
    return lax.axis_index("x"), lax.axis_index("y"), lax.axis_index("c")


def _dev_index(px, py, pc):
    return 4 * px + 2 * py + pc


def _other_devices(x, y, c):
    flip = lambda v, f: 1 - v if f else v
    return [(flip(x, fx), flip(y, fy), flip(c, fc)) for fx in (0, 1) for fy in (0, 1) for fc in (0, 1)][1:]


def _all_gather(shards):
    n = len(shards)

    def body(*refs):
        ins, outs = refs[:n], refs[n:2 * n]
        zeros_ref, send_sems, recv_sems, local_sems = refs[2 * n:]
        zeros_ref[...] = jnp.zeros_like(zeros_ref)
        x, y, c = _place_on_mesh()
        me, sibling = (x, y, c), (x, y, 1 - c)
        chips = [(1 - x, y), (x, 1 - y), (1 - x, 1 - y)]

        def rows(k, px, py, pc):
            r = ins[k].shape[0]
            return outs[k].at[pl.ds(pl.multiple_of(_dev_index(px, py, pc) * r, 8), r), :]

        def copy(k, slot, block, to, src=None):
            return pltpu.make_async_remote_copy(
                src_ref=rows(k, *block) if src is None else src, dst_ref=rows(k, *block),
                send_sem=send_sems.at[k, slot], recv_sem=recv_sems.at[k, slot], device_id=to, device_id_type=MESH)

        local = [pltpu.make_async_copy(ins[k], rows(k, *me), local_sems.at[k]) for k in range(n)]
        sends = []
        for k in range(n):
            local[k].start()
            sends.append(copy(k, 0, me, sibling, src=ins[k]))
            sends += [copy(k, 1 + j, me, (*chip, c), src=ins[k]) for j, chip in enumerate(chips)]
        for cp in sends:
            cp.start()
        for k in range(n):
            for j, chip in enumerate(chips):
                copy(k, 1 + j, (*chip, c), me).wait_recv()
                passed = copy(k, 4 + j, (*chip, c), sibling)
                passed.start()
                sends.append(passed)
        for k in range(n):
            copy(k, 0, sibling, me).wait_recv()
            for j, chip in enumerate(chips):
                copy(k, 4 + j, (*chip, 1 - c), me).wait_recv()
        for cp in sends:
            cp.wait_send()
        for cp in local:
            cp.wait()

    return pl.pallas_call(
        body, name="all_gather_weights",
        in_specs=[ANY_SPEC] * n, out_specs=[ANY_SPEC] * n + [VMEM_SPEC],
        out_shape=[jax.ShapeDtypeStruct((N_DEV * s.shape[0], s.shape[1]), s.dtype) for s in shards]
        + [jax.ShapeDtypeStruct((8, 128), F32)],
        scratch_shapes=[pltpu.SemaphoreType.DMA((n, 7)), pltpu.SemaphoreType.DMA((n, 7)), pltpu.SemaphoreType.DMA((n,))],
    )(*shards)


HBM_SPEC = pl.BlockSpec(memory_space=pltpu.HBM)
SEM_SPEC = pl.BlockSpec(memory_space=pltpu.SEMAPHORE)
DATAFLOW = pltpu.SideEffectType.DATAFLOW_SIDE_EFFECTING


def _exchange_copies(srcs, lands, send_sems, recv_sems, own_sems, scatter, arriving):
    x, y, c = _place_on_mesh()
    me = _dev_index(x, y, c)
    remote, local = [], []
    for k, (src, land) in enumerate(zip(srcs, lands)):
        r = land.shape[0] // N_DEV

        def block(ref, d):
            return ref.at[pl.ds(pl.multiple_of(d * r, 8), r), :]

        for f, peer in enumerate(_other_devices(x, y, c)):
            mine, his = (_dev_index(*peer), me) if arriving else (me, _dev_index(*peer))
            remote.append(pltpu.make_async_remote_copy(
                src_ref=block(src, his) if scatter else src, dst_ref=block(land, mine),
                send_sem=send_sems.at[7 * k + f], recv_sem=recv_sems.at[7 * k + f], device_id=peer, device_id_type=MESH))
        local.append(pltpu.make_async_copy(block(src, me) if scatter else src, block(land, me), own_sems.at[k]))
    return remote, local


def _exchange_start(srcs, scatter, name):
    n = len(srcs)
    lands = [lax.empty(s.shape if scatter else (N_DEV * s.shape[0], s.shape[1]), s.dtype) for s in srcs]

    def body(*refs):
        remote, local = _exchange_copies(refs[:n], refs[n:2 * n], *refs[2 * n:2 * n + 3], scatter, False)
        for cp in remote + local:
            cp.start()
        refs[-1][...] = jnp.zeros_like(refs[-1])

    both = list(srcs) + list(lands)
    outs = pl.pallas_call(
        body, name=name,
        out_shape=(pltpu.SemaphoreType.DMA((7 * n,)), pltpu.SemaphoreType.DMA((7 * n,)), pltpu.SemaphoreType.DMA((n,)),
                   *[pltpu.HBM(a.shape, a.dtype) for a in both], jax.ShapeDtypeStruct((8, 128), F32)),
        in_specs=[HBM_SPEC] * (2 * n), out_specs=(SEM_SPEC, SEM_SPEC, SEM_SPEC, *[HBM_SPEC] * (2 * n), VMEM_SPEC),
        input_output_aliases={i: 3 + i for i in range(2 * n)},
        compiler_params=pltpu.CompilerParams(has_side_effects=DATAFLOW),
    )(*[pltpu.with_memory_space_constraint(a, pltpu.HBM) for a in both])
    return outs[0:3], outs[3:3 + n], outs[3 + n:3 + 2 * n], outs[-1]


def _exchange_wait(started, scatter, after, name):
    sems, srcs, lands, _ = started
    n = len(srcs)

    def body(*refs):
        args = (refs[:n], refs[n:2 * n], *refs[2 * n:2 * n + 3], scatter)
        going, local = _exchange_copies(*args, False)
        for cp in going:
            cp.wait_send()
        for cp in local:
            cp.wait()
        for cp in _exchange_copies(*args, True)[0]:
            cp.wait_recv()

    both = list(srcs) + list(lands)
    outs = pl.pallas_call(
        body, name=name, out_shape=[pltpu.HBM(a.shape, a.dtype) for a in both],
        in_specs=[HBM_SPEC] * (2 * n) + [SEM_SPEC, SEM_SPEC, SEM_SPEC, ANY_SPEC], out_specs=[HBM_SPEC] * (2 * n),
        input_output_aliases={i: i for i in range(2 * n)},
        compiler_params=pltpu.CompilerParams(has_side_effects=DATAFLOW),
    )(*both, *sems, after)
    return outs[n:]


def _sum_partials(parts, name):
    n = len(parts)

    def body(*refs):
        ins, outs = refs[:n], refs[n:]
        first = pl.program_id(0) == 0
        for i_ref, o_ref in zip(ins, outs):
            v = i_ref[...].astype(F32)

            @pl.when(first)
            def _():
                o_ref[...] = v

            @pl.when(jnp.logical_not(first))
            def _():
                o_ref[...] += v

    shapes = [(p.shape[0] // N_DEV, p.shape[1]) for p in parts]
    return pl.pallas_call(
        body, name=name, grid=(N_DEV,),
        in_specs=[pl.BlockSpec(s, lambda j: (j, 0)) for s in shapes],
        out_specs=[pl.BlockSpec(s, lambda j: (0, 0)) for s in shapes],
        out_shape=[jax.ShapeDtypeStruct(s, F32) for s in shapes],
        compiler_params=_params(("arbitrary",)),
    )(*parts)


def _all_reduce_small(slab):
    rows, cols = slab.shape

    def body(x_ref, o_ref, gathered, send_sems, recv_sems):
        x, y, c = _place_on_mesh()
        me = _dev_index(x, y, c)
        peers = _other_devices(x, y, c)

        def copy(f, peer):
            return pltpu.make_async_remote_copy(
                src_ref=x_ref, dst_ref=gathered.at[me], send_sem=send_sems.at[f], recv_sem=recv_sems.at[f],
                device_id=peer, device_id_type=MESH)

        def arrival(f, peer):
            return pltpu.make_async_remote_copy(
                src_ref=x_ref, dst_ref=gathered.at[_dev_index(*peer)], send_sem=send_sems.at[f], recv_sem=recv_sems.at[f],
                device_id=peer, device_id_type=MESH)

        sends = [copy(f, peer) for f, peer in enumerate(peers)]
        for cp in sends:
            cp.start()
        gathered[me] = x_ref[...]
        for f, peer in enumerate(peers):
            arrival(f, peer).wait_recv()
        for cp in sends:
            cp.wait_send()
        total = gathered[0]
        for d in range(1, N_DEV):
            total = total + gathered[d]
        o_ref[...] = total

    return pl.pallas_call(
        body, name="all_reduce_small",
        in_specs=[VMEM_SPEC], out_specs=VMEM_SPEC, out_shape=jax.ShapeDtypeStruct((rows, cols), F32),
        scratch_shapes=[pltpu.VMEM((N_DEV, rows, cols), F32), pltpu.SemaphoreType.DMA((7,)), pltpu.SemaphoreType.DMA((7,))],
    )(slab)


def _adamw(ws, gs, ms, vs, name):
    n = len(ws)
    c1 = 1.0 / (1.0 - ADAM_B1 ** ADAM_STEP)
    c2 = 1.0 / (1.0 - ADAM_B2 ** ADAM_STEP)

    def body(*refs):
        w_r, g_r, m_r, v_r = refs[:n], refs[n:2 * n], refs[2 * n:3 * n], refs[3 * n:4 * n]
        d_o, m_o, v_o = refs[4 * n:5 * n], refs[5 * n:6 * n], refs[6 * n:7 * n]
        for k in range(n):
            g = g_r[k][...]
            m = ADAM_B1 * m_r[k][...] + (1.0 - ADAM_B1) * g
            v = ADAM_B2 * v_r[k][...] + (1.0 - ADAM_B2) * (g * g)
            m_o[k][...] = m
            v_o[k][...] = v
            d_o[k][...] = -ADAM_LR * ((m * c1) / (jnp.sqrt(v * c2) + ADAM_EPS) + ADAM_WD * w_r[k][...])

    shapes = [jax.ShapeDtypeStruct(w.shape, F32) for w in ws]
    outs = pl.pallas_call(
        body, name=name, in_specs=[VMEM_SPEC] * (4 * n), out_specs=[VMEM_SPEC] * (3 * n), out_shape=shapes * 3,
        compiler_params=pltpu.CompilerParams(vmem_limit_bytes=56 << 20),
    )(*ws, *gs, *ms, *vs)
    return outs[:n], outs[n:2 * n], outs[2 * n:]


WEIGHT_NAMES = ("meta_tokens", "ffn1_pre_norm", "ffn1_w_gate", "ffn1_w_up", "ffn1_w_down", "ffn1_post_norm", "mix_pre_norm",
                "w_in", "gla_w_a2", "gla_b_a", "gla_out_norm", "swa_sinks", "swa_out_norm", "w_out", "mix_post_norm",
                "ffn2_pre_norm", "ffn2_w_gate", "ffn2_w_up", "ffn2_w_down", "ffn2_post_norm")
WIN_SHARD = D_IN // N_DEV
WIN_SHARD_PAD = 304
SLAB_VECTORS = ("ffn1_pre", "ffn1_post", "mix_pre", "mix_post", "ffn2_pre", "ffn2_post")
SLAB_ROWS = 32


def kernel(x, meta_tokens, ffn1_pre_norm, ffn1_w_gate, ffn1_w_up, ffn1_w_down, ffn1_post_norm, mix_pre_norm, w_in, gla_w_a2, gla_b_a, gla_out_norm, swa_sinks, swa_out_norm, w_out, mix_post_norm, ffn2_pre_norm, ffn2_w_gate, ffn2_w_up, ffn2_w_down, ffn2_post_norm, loss_target, m_meta_tokens, m_ffn1_pre_norm, m_ffn1_w_gate, m_ffn1_w_up, m_ffn1_w_down, m_ffn1_post_norm, m_mix_pre_norm, m_w_in, m_gla_w_a2, m_gla_b_a, m_gla_out_norm, m_swa_sinks, m_swa_out_norm, m_w_out, m_mix_post_norm, m_ffn2_pre_norm, m_ffn2_w_gate, m_ffn2_w_up, m_ffn2_w_down, m_ffn2_post_norm, v_meta_tokens, v_ffn1_pre_norm, v_ffn1_w_gate, v_ffn1_w_up, v_ffn1_w_down, v_ffn1_post_norm, v_mix_pre_norm, v_w_in, v_gla_w_a2, v_gla_b_a, v_gla_out_norm, v_swa_sinks, v_swa_out_norm, v_w_out, v_mix_post_norm, v_ffn2_pre_norm, v_ffn2_w_gate, v_ffn2_w_up, v_ffn2_w_down, v_ffn2_post_norm):
    given = dict(locals())
    W = {n: given[n] for n in WEIGHT_NAMES}
    M = {n: given["m_" + n] for n in WEIGHT_NAMES}
    V = {n: given["v_" + n] for n in WEIGHT_NAMES}
    dev = _dev_index(*_place_on_mesh())

    def t16(w):
        return w[0].T.astype(BF16)

    small = jnp.concatenate([W["meta_tokens"], jnp.pad(W["gla_w_a2"][0], ((0, 0), (0, 96)))], axis=0)
    wg1, wu1, wd1, small_g, gathered_zeros = _all_gather(
        [t16(W["ffn1_w_gate"]), t16(W["ffn1_w_up"]), W["ffn1_w_down"][0].astype(BF16), small])
    def after_zero(shard, zeros):
        return shard + zeros[0:1, 0:1].astype(shard.dtype)
    win_shard = jnp.pad(t16(W["w_in"]), ((0, WIN_SHARD_PAD - WIN_SHARD), (0, 0)))
    win_shard = after_zero(win_shard, gathered_zeros)
    mid = _exchange_start([win_shard], False, "gather_w_in_start")
    late_shards = [after_zero(W["w_out"][0].astype(BF16), mid[3]), t16(W["ffn2_w_gate"]), t16(W["ffn2_w_up"]),
                   W["ffn2_w_down"][0].astype(BF16)]
    late = _exchange_start(late_shards, False, "gather_late_weights_start")

    def late_weights(what, after):
        if what == "win":
            win_g, = _exchange_wait(mid, False, after, "gather_w_in_wait")
            win_t = win_g.reshape(N_DEV, WIN_SHARD_PAD, D_MODEL)[:, :WIN_SHARD].reshape(D_IN, D_MODEL)
            return dict(win=_win_pad_rows(win_t))
        wout, wg2, wu2, wd2 = _exchange_wait(late, False, after, "gather_late_weights_wait")
        return dict(wout=wout, wg2=wg2, wu2=wu2, wd2=wd2)

    small_g = small_g.reshape(N_DEV, 32, 128)
    meta_full = small_g[:, :N_META].transpose(1, 0, 2).reshape(N_META, D_MODEL)
    wa2_full = small_g[:, N_META:, :32].transpose(1, 0, 2).reshape(16, 256)
    w = dict(
        ffn1_pre=W["ffn1_pre_norm"] + late[3][0, 0], ffn1_post=W["ffn1_post_norm"], mix_pre=W["mix_pre_norm"],
        mix_post=W["mix_post_norm"], ffn2_pre=W["ffn2_pre_norm"], ffn2_post=W["ffn2_post_norm"], b_a=W["gla_b_a"],
        gla_norm=W["gla_out_norm"], sinks=W["swa_sinks"], swa_norm=W["swa_out_norm"], wg1=wg1, wu1=wu1, wd1=wd1,
        wa2=jnp.pad(wa2_full, ((0, 112), (0, 0))))

    in_flight = []

    def on_grads(group, grads):
        parts = []
        for nm, p in grads.items():
            if nm == "win":
                p = _win_unpad_rows(p).reshape(N_DEV, WIN_SHARD, D_MODEL)
                p = jnp.pad(p, ((0, 0), (0, WIN_SHARD_PAD - WIN_SHARD), (0, 0))).reshape(N_DEV * WIN_SHARD_PAD, D_MODEL)
            parts.append(p)
        started = _exchange_start(parts, True, "scatter_" + group + "_start")
        in_flight.append((group, list(grads), started))
        return started[3][0, 0]

    front = jnp.zeros((PAD_ROWS, D_MODEL), F32)
    h0 = jnp.concatenate([front, meta_full, x[0]], axis=0)
    tgt = jnp.concatenate([jnp.zeros((BLK, D_MODEL), F32), loss_target[0]], axis=0)
    loss, dh0, g = _local_step(h0, tgt, w, late_weights, on_grads)
    grad_x = dh0[BLK:][None]

    packed = jnp.concatenate([g["b_a"][0:1], g["gla_norm"][0:1], g["sinks"][0:1], g["swa_norm"][0:1]], axis=1)
    slab = jnp.concatenate([g[k][0:1] for k in SLAB_VECTORS] + [packed, jnp.full((1, D_MODEL), loss, F32),
                           g["wa2"][:16].reshape(4, D_MODEL), jnp.zeros((4, D_MODEL), F32), dh0[PAD_ROWS:BLK]], axis=0)
    tot = _all_reduce_small(slab)
    loss = tot[7, 0]
    small_grads = dict(
        ffn1_pre_norm=tot[0:1], ffn1_post_norm=tot[1:2], mix_pre_norm=tot[2:3], mix_post_norm=tot[3:4],
        ffn2_pre_norm=tot[4:5], ffn2_post_norm=tot[5:6], gla_b_a=tot[6:7, 0:256], gla_out_norm=tot[6:7, 256:384],
        swa_sinks=tot[6:7, 384:392], swa_out_norm=tot[6:7, 512:1024],
        gla_w_a2=lax.dynamic_slice_in_dim(tot[8:12].reshape(16, 256), dev * 32, 32, axis=1)[None],
        meta_tokens=lax.dynamic_slice_in_dim(tot[16:32], dev * 128, 128, axis=1))

    last_started = in_flight[-1][2][3]
    sums = {}
    for group, names, started in in_flight:
        lands = _exchange_wait(started, True, last_started, "scatter_" + group + "_wait")
        for nm, s in zip(names, _sum_partials(lands, "sum_" + group)):
            sums[nm] = s
    big = dict(ffn1_w_gate=("wg1", True), ffn1_w_up=("wu1", True), ffn1_w_down=("wd1", False), w_in=("win", True),
               w_out=("wout", False), ffn2_w_gate=("wg2", True), ffn2_w_up=("wu2", True), ffn2_w_down=("wd2", False))
    grads = dict(small_grads)
    delta, new_m, new_v = {}, {}, {}
    for n, (short, transposed) in big.items():
        to_slab = (lambda a: a[0].T) if transposed else (lambda a: a[0])
        from_slab = (lambda a: a.T[None]) if transposed else (lambda a: a[None])
        g_slab = sums[short][:WIN_SHARD] if short == "win" else sums[short]
        d_, m_, v_ = _adamw([to_slab(W[n])], [g_slab], [to_slab(M[n])], [to_slab(V[n])], "adamw_" + n)
        grads[n], delta[n], new_m[n], new_v[n] = from_slab(g_slab), from_slab(d_[0]), from_slab(m_[0]), from_slab(v_[0])
    names = [n for n in WEIGHT_NAMES if n not in big]
    two_d = lambda a: a.reshape(-1, a.shape[-1])
    d_, m_, v_ = _adamw([two_d(W[n]) for n in names], [two_d(grads[n]) for n in names],
                        [two_d(M[n]) for n in names], [two_d(V[n]) for n in names], "adamw_small")
    for k, n in enumerate(names):
        delta[n], new_m[n], new_v[n] = d_[k].reshape(W[n].shape), m_[k].reshape(W[n].shape), v_[k].reshape(W[n].shape)
    return (loss, grad_x, *[grads[n] for n in WEIGHT_NAMES], *[delta[n] for n in WEIGHT_NAMES],
            *[new_m[n] for n in WEIGHT_NAMES], *[new_v[n] for n in WEIGHT_NAMES])
```

```python
import functools

import jax
import jax.numpy as jnp
from jax import lax
from jax.experimental import pallas as pl
from jax.experimental.pallas import tpu as pltpu

F32, BF16 = jnp.float32, jnp.bfloat16

D_MODEL = 1024
D_FF = 2816
N_META = 16
BLK = 128
PAD_ROWS = BLK - N_META
GLA_DK = 64
SWA_HD = 64
SWA_HEADS = 8
GLA_TAU = 16.0
NORM_EPS = 1e-6
NEG_INF = -1e30
ROPE_THETA = 10000.0
P_GQ, P_GK, P_GV, P_GG, P_SQ, P_SK, P_SV, P_GA, P_END = 0, 256, 512, 1024, 1536, 2048, 2176, 2304, 2432
D_IN = 2320
IN_SPLITS = (256, 256, 512, 512, 16, 512, 128, 128)
FF_TILE = 2816
WGRAD_TILE_MAX = 2432
N_DEV = 8
MESH = pl.DeviceIdType.MESH

ADAM_LR, ADAM_B1, ADAM_B2, ADAM_EPS, ADAM_WD, ADAM_STEP = 0.001, 0.9, 0.999, 1e-08, 0.01, 10

V7X_VMEM_BYTES = 64 << 20
VMEM_SPEC = pl.BlockSpec(memory_space=pltpu.VMEM)
SMEM_SPEC = pl.BlockSpec(memory_space=pltpu.SMEM)
ANY_SPEC = pl.BlockSpec(memory_space=pl.ANY)


def _params(semantics, vmem_mb=56):
    return pltpu.CompilerParams(dimension_semantics=semantics, vmem_limit_bytes=vmem_mb << 20)


def _row_tile(rows):
    return 320 if rows % 320 == 0 else BLK


def _nn(a, b):
    return lax.dot_general(a, b, (((1,), (0,)), ((), ())), preferred_element_type=F32)


def _nt(a, b):
    return lax.dot_general(a, b, (((1,), (1,)), ((), ())), preferred_element_type=F32)


def _tn(a, b):
    return lax.dot_general(a, b, (((0,), (0,)), ((), ())), preferred_element_type=F32)


def _rms(x):
    r = lax.rsqrt(jnp.mean(x * x, axis=-1, keepdims=True) + NORM_EPS)
    return x * r, r


def _rms_bwd(xn, r, w, dy):
    g = dy * w
    return r * (g - xn * jnp.mean(g * xn, axis=-1, keepdims=True))


def _sigmoid(x):
    return 1.0 / (1.0 + jnp.exp(-x))


def _colsum(x):
    return jnp.sum(x, axis=0, keepdims=True)


def _split_bf16(x):
    hi = x.astype(BF16)
    lo = (x - hi.astype(F32)).astype(BF16)
    return hi, lo


def _tri(lower):
    r = lax.broadcasted_iota(jnp.int32, (BLK, BLK), 0)
    c = lax.broadcasted_iota(jnp.int32, (BLK, BLK), 1)
    return (r >= c) if lower else (c >= r)


def _half_mask(width, half):
    lane = lax.broadcasted_iota(jnp.int32, (1, width), 1)
    return ((lane % 128) < 64) if half == 0 else ((lane % 128) >= 64)


def _rot_half(x):
    w = x.shape[-1]
    lane = lax.broadcasted_iota(jnp.int32, (1, w), 1)
    return jnp.where((lane % SWA_HD) < SWA_HD // 2, -pltpu.roll(x, w - SWA_HD // 2, 1), pltpu.roll(x, SWA_HD // 2, 1))


def _row_spec(tm, cols):
    return pl.BlockSpec((tm, cols), lambda i: (i, 0))


def _acc_spec(cols):
    return pl.BlockSpec((8, cols), lambda i: (0, 0))


def _acc_add(ref, first, value):
    @pl.when(first)
    def _():
        ref[...] = jnp.zeros_like(ref)
    ref[0:1, :] += value


def _ffn_fwd(h, gpre, wg_t, wu_t, wd, gpost, tgt=None):
    rows = h.shape[0]
    tm = _row_tile(rows)
    nf = D_FF // FF_TILE
    with_loss = tgt is not None

    def body(*refs):
        if with_loss:
            (h_ref, gpre_ref, wg_ref, wu_ref, wd_ref, gpost_ref, t_ref,
             ho_ref, a_ref, b_ref, f_ref, dy_ref, loss_ref, acc) = refs
        else:
            (h_ref, gpre_ref, wg_ref, wu_ref, wd_ref, gpost_ref, ho_ref, a_ref, b_ref, f_ref, acc) = refs
        i = pl.program_id(0)
        h_in = h_ref[...]
        hn, _ = _rms(h_in)
        n16 = (hn * gpre_ref[...]).astype(BF16)
        for j in range(nf):
            cols = slice(j * FF_TILE, (j + 1) * FF_TILE)
            a = _nt(n16, wg_ref[cols, :])
            b = _nt(n16, wu_ref[cols, :])
            a_ref[:, cols] = a.astype(BF16)
            b_ref[:, cols] = b.astype(BF16)
            s16 = (a * _sigmoid(a) * b).astype(BF16)
            part = _nn(s16, wd_ref[cols, :])
            if j == 0:
                acc[...] = part
            else:
                acc[...] += part
        f = acc[...]
        f_ref[...] = f
        fn, _ = _rms(f)
        y = h_in + 0.5 * (fn * gpost_ref[...])
        ho_ref[...] = y
        if with_loss:
            row = i * tm + lax.broadcasted_iota(jnp.int32, (tm, 1), 0)
            err = jnp.where(row >= BLK, y - t_ref[...], 0.0)
            dy_ref[...] = err * (1.0 / D_MODEL)
            part = 0.5 * jnp.sum(jnp.sum(err * err, axis=-1, keepdims=True) * (1.0 / D_MODEL), axis=0, keepdims=True)

            @pl.when(i == 0)
            def _():
                loss_ref[...] = jnp.zeros_like(loss_ref)
            loss_ref[...] += part

    row_f32 = _row_spec(tm, D_MODEL)
    in_specs = [row_f32, VMEM_SPEC, VMEM_SPEC, VMEM_SPEC, VMEM_SPEC, VMEM_SPEC]
    out_specs = [row_f32, _row_spec(tm, D_FF), _row_spec(tm, D_FF), row_f32]
    out_shape = [jax.ShapeDtypeStruct((rows, D_MODEL), F32), jax.ShapeDtypeStruct((rows, D_FF), BF16),
                 jax.ShapeDtypeStruct((rows, D_FF), BF16), jax.ShapeDtypeStruct((rows, D_MODEL), F32)]
    args = [h, gpre, wg_t, wu_t, wd, gpost]
    if with_loss:
        in_specs.append(row_f32)
        args.append(tgt)
        out_specs += [row_f32, pl.BlockSpec((8, 128), lambda i: (0, 0))]
        out_shape += [jax.ShapeDtypeStruct((rows, D_MODEL), F32), jax.ShapeDtypeStruct((8, 128), F32)]
    return pl.pallas_call(
        body, name="ffn_fwd_loss" if with_loss else "ffn_fwd", grid=(rows // tm,),
        in_specs=in_specs, out_specs=out_specs, out_shape=out_shape,
        scratch_shapes=[pltpu.VMEM((tm, D_MODEL), F32)],
        compiler_params=_params(("arbitrary",)),
    )(*args)


def _ffn_bwd_act(dh_out, h, a, b, f, gpre, gpost, wg_t, wu_t, wd, name):
    rows = h.shape[0]
    tm = _row_tile(rows)
    nf = D_FF // FF_TILE

    def body(dho_ref, h_ref, a_ref, b_ref, f_ref, gpre_ref, gpost_ref, wg_ref, wu_ref, wd_ref,
             dh_ref, da_ref, db_ref, df_ref, n_ref, dgpre_ref, dgpost_ref, acc):
        first = pl.program_id(0) == 0
        dho = dho_ref[...]
        drr = 0.5 * dho
        fn, rf = _rms(f_ref[...])
        _acc_add(dgpost_ref, first, _colsum(drr * fn))
        df16 = _rms_bwd(fn, rf, gpost_ref[...], drr).astype(BF16)
        df_ref[...] = df16
        hn, rh = _rms(h_ref[...])
        n_ref[...] = (hn * gpre_ref[...]).astype(BF16)
        for j in range(nf):
            cols = slice(j * FF_TILE, (j + 1) * FF_TILE)
            ds = _nt(df16, wd_ref[cols, :])
            av = a_ref[:, cols].astype(F32)
            bv = b_ref[:, cols].astype(F32)
            sg = _sigmoid(av)
            db16 = (ds * (av * sg)).astype(BF16)
            da16 = (ds * bv * (sg * (1.0 + av * (1.0 - sg)))).astype(BF16)
            da_ref[:, cols] = da16
            db_ref[:, cols] = db16
            part = _nn(da16, wg_ref[cols, :]) + _nn(db16, wu_ref[cols, :])
            if j == 0:
                acc[...] = part
            else:
                acc[...] += part
        dn = acc[...]
        _acc_add(dgpre_ref, first, _colsum(dn * hn))
        dh_ref[...] = dho + _rms_bwd(hn, rh, gpre_ref[...], dn)

    row_f32 = _row_spec(tm, D_MODEL)
    row_ff = _row_spec(tm, D_FF)
    return pl.pallas_call(
        body, name=name, grid=(rows // tm,),
        in_specs=[row_f32, row_f32, row_ff, row_ff, row_f32, VMEM_SPEC, VMEM_SPEC, VMEM_SPEC, VMEM_SPEC, VMEM_SPEC],
        out_specs=[row_f32, row_ff, row_ff, row_f32, row_f32, _acc_spec(D_MODEL), _acc_spec(D_MODEL)],
        out_shape=[jax.ShapeDtypeStruct((rows, D_MODEL), F32), jax.ShapeDtypeStruct((rows, D_FF), BF16),
                   jax.ShapeDtypeStruct((rows, D_FF), BF16), jax.ShapeDtypeStruct((rows, D_MODEL), BF16),
                   jax.ShapeDtypeStruct((rows, D_MODEL), BF16), jax.ShapeDtypeStruct((8, D_MODEL), F32),
                   jax.ShapeDtypeStruct((8, D_MODEL), F32)],
        scratch_shapes=[pltpu.VMEM((tm, D_MODEL), F32)],
        compiler_params=_params(("arbitrary",)),
    )(dh_out, h, a, b, f, gpre, gpost, wg_t, wu_t, wd)


def _wgrad(lhs, rhs, name, gate=None):
    rows, width = lhs.shape
    tm = 1664 if rows % 1664 == 0 else BLK
    tf = width // 2 if width > WGRAD_TILE_MAX else width
    nr = rows // tm
    gated = gate is not None

    def body(*refs):
        if gated:
            g_ref, l_ref, r_ref, o_ref, acc = refs
            gv = g_ref[...].astype(F32)
            lv = (gv * _sigmoid(gv) * l_ref[...].astype(F32)).astype(BF16)
        else:
            l_ref, r_ref, o_ref, acc = refs
            lv = l_ref[...]
        i = pl.program_id(1)
        part = _tn(lv, r_ref[...])

        @pl.when(i == 0)
        def _():
            acc[...] = part

        @pl.when(i > 0)
        def _():
            acc[...] += part

        @pl.when(i == nr - 1)
        def _():
            o_ref[...] = acc[...].astype(BF16)

    l_spec = pl.BlockSpec((tm, tf), lambda j, i: (i, j))
    r_spec = pl.BlockSpec((tm, D_MODEL), lambda j, i: (i, 0))
    return pl.pallas_call(
        body, name=name, grid=(width // tf, nr),
        in_specs=([l_spec] if gated else []) + [l_spec, r_spec],
        out_specs=pl.BlockSpec((tf, D_MODEL), lambda j, i: (j, 0)),
        out_shape=jax.ShapeDtypeStruct((width, D_MODEL), BF16),
        scratch_shapes=[pltpu.VMEM((tf, D_MODEL), F32)],
        compiler_params=_params(("arbitrary", "arbitrary")),
    )(*([gate] if gated else []), lhs, rhs)


def _chunk_cumsum(x, lower):
    tri = jnp.where(_tri(lower), 1.0, 0.0).astype(BF16)
    hi, lo = _split_bf16(x)
    return _nn(tri, hi) + _nn(tri, lo)


def _mix_in(h, g, win_p, wa2_p, b_a, cos, sin):
    rows = h.shape[0]
    tm = 640 if rows % 640 == 0 else BLK

    def body(h_ref, g_ref, win_ref, wa2_ref, ba_ref, cos_ref, sin_ref,
             gq_ref, gk_ref, gv_ref, gg_ref, sq_ref, sk_ref, sv_ref, ga_ref, loga_ref, bc_ref, n_ref):
        hn, _ = _rms(h_ref[...])
        n16 = (hn * g_ref[...]).astype(BF16)
        n_ref[...] = n16
        proj = _nt(n16, win_ref[...])
        gq_ref[...] = proj[:, P_GQ:P_GK]
        gk_ref[...] = proj[:, P_GK:P_GV]
        gv_ref[...] = proj[:, P_GV:P_GG].astype(BF16)
        gg_ref[...] = proj[:, P_GG:P_SQ]
        c1, s1 = cos_ref[...], sin_ref[...]
        c4 = jnp.concatenate([c1, c1, c1, c1], axis=1)
        s4 = jnp.concatenate([s1, s1, s1, s1], axis=1)
        sq = proj[:, P_SQ:P_SK]
        sk = proj[:, P_SK:P_SV]
        sq_ref[...] = (sq * c4 + _rot_half(sq) * s4).astype(BF16)
        sk_ref[...] = (sk * c1 + _rot_half(sk) * s1).astype(BF16)
        sv_ref[...] = proj[:, P_SV:P_GA].astype(BF16)
        ga = proj[:, P_GA:P_END]
        ga_ref[...] = ga
        z = _nn(ga, wa2_ref[...]) + ba_ref[...]
        loga = (jnp.minimum(z, 0.0) - jnp.log(1.0 + jnp.exp(-jnp.abs(z)))) * (1.0 / GLA_TAU)
        loga_ref[...] = loga
        for c in range(tm // BLK):
            rs = slice(c * BLK, (c + 1) * BLK)
            bc_ref[rs, :] = _chunk_cumsum(loga[rs, :], True)

    f32 = lambda c: jax.ShapeDtypeStruct((rows, c), F32)
    b16 = lambda c: jax.ShapeDtypeStruct((rows, c), BF16)
    rs = lambda c: _row_spec(tm, c)
    return pl.pallas_call(
        body, name="mix_in", grid=(rows // tm,),
        in_specs=[rs(D_MODEL), VMEM_SPEC, VMEM_SPEC, VMEM_SPEC, VMEM_SPEC, rs(128), rs(128)],
        out_specs=[rs(256), rs(256), rs(512), rs(512), rs(512), rs(128), rs(128), rs(128), rs(256), rs(256), rs(D_MODEL)],
        out_shape=[f32(256), f32(256), b16(512), f32(512), b16(512), b16(128), b16(128), f32(128), f32(256), f32(256),
                   b16(D_MODEL)],
        compiler_params=_params(("arbitrary",)),
    )(h, g, win_p, wa2_p, b_a, cos, sin)


def _gla_factors(q, k, bc):
    bm = bc[BLK // 2 - 1:BLK // 2, :]
    bl = bc[BLK - 1:BLK, :]
    e_q, e_k, e_qe, e_kd = jnp.exp(bc - bm), jnp.exp(bm - bc), jnp.exp(bc), jnp.exp(bl - bc)
    return (q * e_q, k * e_k, q * e_qe, k * e_kd), (e_q, e_k, e_qe, e_kd), jnp.exp(bl)


def _gla_fwd(gq, gk, gv, gg, bc, wgn):
    rows = gq.shape[0]
    nc = rows // BLK
    scale = GLA_DK ** -0.5

    def body(q_ref, k_ref, v_ref, gg_ref, bc_ref, wgn_ref, o_ref, cat_ref, sp_ref, st):
        @pl.when(pl.program_id(0) == 0)
        def _():
            st[...] = jnp.zeros_like(st)
        low = _tri(True)
        wgn_v = wgn_ref[...]
        for p in range(2):
            sl = slice(128 * p, 128 * p + 128)
            (qt, kt, qe, kd), _, ebl = _gla_factors(q_ref[:, sl] * scale, k_ref[:, sl], bc_ref[:, sl])
            s_prev = st[p]
            sp_ref[0, p] = s_prev
            s16 = s_prev.astype(BF16)
            qt16 = qt.astype(BF16)
            s_new = s_prev * ebl
            for hh in range(2):
                hs = slice(128 * (2 * p + hh), 128 * (2 * p + hh) + 128)
                lm = _half_mask(128, hh)
                vh = v_ref[:, hs]
                pm = jnp.where(low, _nt(qt16, jnp.where(lm, kt, 0.0).astype(BF16)), 0.0)
                o = _nn(pm.astype(BF16), vh) + _nt(jnp.where(lm, qe, 0.0).astype(BF16), s16)
                s_new = s_new + _tn(vh, jnp.where(lm, kd, 0.0).astype(BF16))
                o_ref[:, hs] = o
                on, _ = _rms(o)
                gate = gg_ref[:, hs]
                cat_ref[:, hs] = (on * wgn_v * (gate * _sigmoid(gate))).astype(BF16)
            st[p] = s_new

    rs = lambda c: _row_spec(BLK, c)
    return pl.pallas_call(
        body, name="gla_fwd", grid=(nc,),
        in_specs=[rs(256), rs(256), rs(512), rs(512), rs(256), VMEM_SPEC],
        out_specs=[rs(512), rs(512), pl.BlockSpec((1, 2, 128, 128), lambda i: (i, 0, 0, 0))],
        out_shape=[jax.ShapeDtypeStruct((rows, 512), F32), jax.ShapeDtypeStruct((rows, 512), BF16),
                   jax.ShapeDtypeStruct((nc, 2, 128, 128), F32)],
        scratch_shapes=[pltpu.VMEM((2, 128, 128), F32)],
        compiler_params=_params(("arbitrary",)),
    )(gq, gk, gv, gg, bc, wgn)


def _gla_bwd(dcat, o_all, gq, gk, gv, gg, bc, sp, wgn):
    rows = gq.shape[0]
    nc = rows // BLK
    scale = GLA_DK ** -0.5

    def body(dc_ref, o_ref, q_ref, k_ref, v_ref, gg_ref, bc_ref, sp_ref, wgn_ref,
             dq_ref, dk_ref, dv_ref, dgg_ref, dla_ref, dwgn_ref, dst):
        first = pl.program_id(0) == 0

        @pl.when(first)
        def _():
            dst[...] = jnp.zeros_like(dst)
        low, upp = _tri(True), _tri(False)
        last_row = lax.broadcasted_iota(jnp.int32, (BLK, 1), 0) == BLK - 1
        wgn_v = wgn_ref[...]
        dwgn = jnp.zeros((1, 128), F32)
        for p in range(2):
            sl = slice(128 * p, 128 * p + 128)
            (qt, kt, qe, kd), (e_q, e_k, e_qe, e_kd), ebl = _gla_factors(
                q_ref[:, sl] * scale, k_ref[:, sl], bc_ref[:, sl])
            s_prev = sp_ref[0, p]
            s16 = s_prev.astype(BF16)
            ds_next = dst[p]
            ds16 = ds_next.astype(BF16)
            qt16 = qt.astype(BF16)
            ds_new = ds_next * ebl
            dqt = jnp.zeros((BLK, 128), F32)
            dkt = jnp.zeros((BLK, 128), F32)
            dqe = jnp.zeros((BLK, 128), F32)
            dkd = jnp.zeros((BLK, 128), F32)
            for hh in range(2):
                hs = slice(128 * (2 * p + hh), 128 * (2 * p + hh) + 128)
                lm = _half_mask(128, hh)
                on, ro = _rms(o_ref[:, hs])
                gate = gg_ref[:, hs]
                sg = _sigmoid(gate)
                si = gate * sg
                dog = dc_ref[:, hs]
                dwgn = dwgn + _colsum(dog * si * on)
                dgg_ref[:, hs] = dog * (on * wgn_v) * (sg * (1.0 + gate * (1.0 - sg)))
                do16 = _rms_bwd(on, ro, wgn_v, dog * si).astype(BF16)
                vh = v_ref[:, hs]
                ktm16 = jnp.where(lm, kt, 0.0).astype(BF16)
                qtm16 = jnp.where(lm, qt, 0.0).astype(BF16)
                qem16 = jnp.where(lm, qe, 0.0).astype(BF16)
                kdm16 = jnp.where(lm, kd, 0.0).astype(BF16)
                p_t = jnp.where(upp, _nt(ktm16, qt16), 0.0)
                dp_t = jnp.where(upp, _nt(vh, do16), 0.0)
                dp = jnp.where(low, _nt(do16, vh), 0.0)
                dv_ref[:, hs] = _nn(p_t.astype(BF16), do16) + _nt(kdm16, ds16)
                dqt = dqt + _nn(dp.astype(BF16), ktm16)
                dkt = dkt + _nn(dp_t.astype(BF16), qtm16)
                dqe = dqe + jnp.where(lm, _nn(do16, s16), 0.0)
                dkd = dkd + jnp.where(lm, _nn(vh, ds16), 0.0)
                ds_new = ds_new + _tn(do16, qem16)
            debl = _colsum(ds_next * s_prev)
            dq_ref[:, sl] = (dqt * e_q + dqe * e_qe) * scale
            dk_ref[:, sl] = dkt * e_k + dkd * e_kd
            dkd_kd = dkd * kd
            db = dqt * qt - dkt * kt + dqe * qe - dkd_kd
            db = jnp.where(last_row, db + (_colsum(dkd_kd) + debl * ebl), db)
            dla_ref[:, sl] = _chunk_cumsum(db, False)
            dst[p] = ds_new
        _acc_add(dwgn_ref, first, dwgn)

    rev = lambda c: pl.BlockSpec((BLK, c), lambda i: (nc - 1 - i, 0))
    f32 = lambda c: jax.ShapeDtypeStruct((rows, c), F32)
    return pl.pallas_call(
        body, name="gla_bwd", grid=(nc,),
        in_specs=[rev(512), rev(512), rev(256), rev(256), rev(512), rev(512), rev(256),
                  pl.BlockSpec((1, 2, 128, 128), lambda i: (nc - 1 - i, 0, 0, 0)), VMEM_SPEC],
        out_specs=[rev(256), rev(256), rev(512), rev(512), rev(256), _acc_spec(128)],
        out_shape=[f32(256), f32(256), f32(512), f32(512), f32(256), jax.ShapeDtypeStruct((8, 128), F32)],
        scratch_shapes=[pltpu.VMEM((2, 128, 128), F32)],
        compiler_params=_params(("arbitrary",)),
    )(dcat, o_all, gq, gk, gv, gg, bc, sp, wgn)


def _swa_masks(i):
    t = lax.broadcasted_iota(jnp.int32, (BLK, BLK), 0)
    c = lax.broadcasted_iota(jnp.int32, (BLK, BLK), 1)
    own_side = c <= t
    band_ok = i >= jnp.where(own_side, 1, 2)
    meta_ok = (c % N_META) <= jnp.where(i >= 1, N_META, t - PAD_ROWS)
    return own_side, band_ok, meta_ok, c // N_META


def _swa_blocks(ref, i):
    prev = pl.multiple_of(jnp.maximum(i - 1, 0) * BLK, BLK)
    own = pl.multiple_of(i * BLK, BLK)
    return jnp.concatenate([ref[pl.ds(prev, BLK), :], ref[pl.ds(own, BLK), :]], axis=0), prev, own


def _swa_meta_operand(ref):
    blk = ref[0:BLK, :]
    swapped = pltpu.roll(blk, 64, 1)
    lo = jnp.where(_half_mask(128, 0), blk, swapped)
    hi = jnp.where(_half_mask(128, 1), blk, swapped)
    meta = jnp.concatenate([lo, lo, hi, hi], axis=1)[PAD_ROWS:BLK, :]
    tiled = jnp.concatenate([meta] * SWA_HEADS, axis=0)
    j = lax.broadcasted_iota(jnp.int32, tiled.shape, 0)
    lane = lax.broadcasted_iota(jnp.int32, tiled.shape, 1)
    return jnp.where(j // N_META == lane // SWA_HD, tiled, jnp.zeros_like(tiled))


def _swa_meta_fold(acc):
    out = jnp.zeros((N_META, 128), F32)
    for hd in range(SWA_HEADS):
        half, kv = hd % 2, hd // 4
        piece = acc[N_META * hd:N_META * (hd + 1), 128 * (hd // 2):128 * (hd // 2) + 128]
        piece = jnp.where(_half_mask(128, half), piece, 0.0)
        out = out + (piece if half == kv else pltpu.roll(piece, 64, 1))
    return out


def _by_head(group, per_head):
    out = jnp.zeros((BLK, BLK), F32)
    for hd, v in enumerate(per_head):
        out = jnp.where(group == hd, v, out)
    return out


def _place(x, kv):
    if kv == 0:
        lo = jnp.where(_half_mask(128, 0), x, jnp.zeros_like(x))
        return lo, pltpu.roll(lo, 64, 1)
    hi = jnp.where(_half_mask(128, 1), x, jnp.zeros_like(x))
    return pltpu.roll(hi, 64, 1), hi


def _swa_fwd(sq, sk, sv, sinks, wn):
    rows = sq.shape[0]
    nb = rows // BLK
    scale = SWA_HD ** -0.5

    def body(q_ref, k_ref, v_ref, sink_ref, wn_ref, o_ref, cat_ref, lse_ref, kp, vp):
        i = pl.program_id(0)

        @pl.when(i == 0)
        def _():
            kp[...] = _swa_meta_operand(k_ref)
            vp[...] = _swa_meta_operand(v_ref)
        own_side, band_ok, meta_ok, group = _swa_masks(i)
        k2, _, _ = _swa_blocks(k_ref, i)
        v2, _, _ = _swa_blocks(v_ref, i)
        kz = (_place(k2, 0), _place(k2, 1))
        vz = (_place(v2, 0), _place(v2, 1))
        q_all = q_ref[...]
        s_meta = jnp.where(meta_ok, _nt(q_all, kp[...]) * scale, NEG_INF)
        s_band, m = [], []
        for hd in range(SWA_HEADS):
            kv, half = hd // 4, hd % 2
            q_pair = q_all[:, 128 * (hd // 2):128 * (hd // 2) + 128]
            s2 = _nt(q_pair, kz[kv][half])
            s = jnp.where(band_ok, jnp.where(own_side, s2[:, BLK:], s2[:, :BLK]) * scale, NEG_INF)
            top = jnp.maximum(jnp.max(s, axis=-1, keepdims=True),
                              jnp.max(jnp.where(group == hd, s_meta, NEG_INF), axis=-1, keepdims=True))
            s_band.append(s)
            m.append(jnp.maximum(top, sink_ref[0, hd]))
        e_meta = jnp.exp(s_meta - _by_head(group, m))
        o_meta = _nn(e_meta.astype(BF16), vp[...])
        outs = []
        for pr in range(4):
            o_pair = o_meta[:, 128 * pr:128 * pr + 128]
            rden = []
            for half in range(2):
                hd = 2 * pr + half
                kv = hd // 4
                e = jnp.exp(s_band[hd] - m[hd])
                den = (jnp.sum(e, axis=-1, keepdims=True)
                       + jnp.sum(jnp.where(group == hd, e_meta, 0.0), axis=-1, keepdims=True)
                       + jnp.exp(sink_ref[0, hd] - m[hd]))
                lse_ref[:, hd:hd + 1] = m[hd] + jnp.log(den)
                rden.append(1.0 / den)
                e2 = jnp.concatenate([jnp.where(own_side, 0.0, e), jnp.where(own_side, e, 0.0)], axis=1).astype(BF16)
                o_pair = o_pair + _nn(e2, vz[kv][half])
            outs.append(o_pair * jnp.where(_half_mask(128, 0), rden[0], rden[1]))
        o = jnp.concatenate(outs, axis=1)
        o_ref[...] = o
        on, _ = _rms(o)
        cat_ref[...] = (on * wn_ref[...]).astype(BF16)

    return pl.pallas_call(
        body, name="swa_fwd", grid=(nb,),
        in_specs=[_row_spec(BLK, 512), VMEM_SPEC, VMEM_SPEC, SMEM_SPEC, VMEM_SPEC],
        out_specs=[_row_spec(BLK, 512), _row_spec(BLK, 512), _row_spec(BLK, SWA_HEADS)],
        out_shape=[jax.ShapeDtypeStruct((rows, 512), F32), jax.ShapeDtypeStruct((rows, 512), BF16),
                   jax.ShapeDtypeStruct((rows, SWA_HEADS), F32)],
        scratch_shapes=[pltpu.VMEM((BLK, 512), BF16), pltpu.VMEM((BLK, 512), BF16)],
        compiler_params=_params(("arbitrary",)),
    )(sq, sk, sv, sinks, wn)


def _swa_bwd(dcat, o_all, sq, sk, sv, lse, sinks, wn):
    rows = sq.shape[0]
    nb = rows // BLK
    scale = SWA_HD ** -0.5

    def body(dc_ref, o_ref, q_ref, k_ref, v_ref, lse_ref, sink_ref, wn_ref, dq_ref, dk_ref, dv_ref, dsink_ref, dwn_ref,
             kp, vp, dkp, dvp):
        i = pl.program_id(0)
        first = i == 0

        @pl.when(first)
        def _():
            dk_ref[...] = jnp.zeros_like(dk_ref)
            dv_ref[...] = jnp.zeros_like(dv_ref)
            dkp[...] = jnp.zeros_like(dkp)
            dvp[...] = jnp.zeros_like(dvp)
            kp[...] = _swa_meta_operand(k_ref)
            vp[...] = _swa_meta_operand(v_ref)
        own_side, band_ok, meta_ok, group = _swa_masks(i)
        k2, prev, own = _swa_blocks(k_ref, i)
        v2, _, _ = _swa_blocks(v_ref, i)
        kz = (_place(k2, 0), _place(k2, 1))
        vz = (_place(v2, 0), _place(v2, 1))
        o = o_ref[...]
        on, ro = _rms(o)
        dc = dc_ref[...]
        _acc_add(dwn_ref, first, _colsum(dc * on))
        do = _rms_bwd(on, ro, wn_ref[...], dc)
        do_o = do * o
        do16 = do.astype(BF16)
        q_all = q_ref[...]
        lse = [lse_ref[:, hd:hd + 1] for hd in range(SWA_HEADS)]
        delta = [jnp.sum(jnp.where(_half_mask(128, hd % 2), do_o[:, 128 * (hd // 2):128 * (hd // 2) + 128], 0.0),
                         axis=-1, keepdims=True) for hd in range(SWA_HEADS)]
        s_meta = jnp.where(meta_ok, _nt(q_all, kp[...]) * scale, NEG_INF)
        p_meta = jnp.exp(s_meta - _by_head(group, lse))
        ds_meta16 = (p_meta * (_nt(do16, vp[...]) - _by_head(group, delta)) * scale).astype(BF16)
        dq_meta = _nn(ds_meta16, kp[...])
        dkp[...] += _tn(ds_meta16, q_all)
        dvp[...] += _tn(p_meta.astype(BF16), do16)
        lane8 = lax.broadcasted_iota(jnp.int32, (1, 128), 1)
        dsink = jnp.zeros((1, 128), F32)
        dk2 = [[jnp.zeros((2 * BLK, 128), F32) for _ in range(2)] for _ in range(2)]
        dv2 = [[jnp.zeros((2 * BLK, 128), F32) for _ in range(2)] for _ in range(2)]
        dqs = []
        for pr in range(4):
            ps = slice(128 * pr, 128 * pr + 128)
            q_pair = q_all[:, ps]
            do_pair = do16[:, ps]
            dq_pair = dq_meta[:, ps]
            for half in range(2):
                hd = 2 * pr + half
                kv = hd // 4

                def window(x2):
                    return jnp.where(own_side, x2[:, BLK:], x2[:, :BLK])

                def unwindow(x):
                    return jnp.concatenate([jnp.where(own_side, 0.0, x), jnp.where(own_side, x, 0.0)], axis=1).astype(BF16)
                s = jnp.where(band_ok, window(_nt(q_pair, kz[kv][half])) * scale, NEG_INF)
                prob = jnp.exp(s - lse[hd])
                dsink = dsink + jnp.where(lane8 == hd, -jnp.sum(jnp.exp(sink_ref[0, hd] - lse[hd]) * delta[hd]), 0.0)
                ds2 = unwindow(prob * (window(_nt(do_pair, vz[kv][half])) - delta[hd]) * scale)
                dq_pair = dq_pair + _nn(ds2, kz[kv][half])
                dk2[kv][half] = dk2[kv][half] + _tn(ds2, q_pair)
                dv2[kv][half] = dv2[kv][half] + _tn(unwindow(prob), do_pair)
            dqs.append(dq_pair)
        dq_ref[...] = jnp.concatenate(dqs, axis=1)
        _acc_add(dsink_ref, first, dsink)
        for ref, acc2 in ((dk_ref, dk2), (dv_ref, dv2)):
            tot = jnp.zeros((2 * BLK, 128), F32)
            for kv in range(2):
                for half in range(2):
                    part = jnp.where(_half_mask(128, half), acc2[kv][half], 0.0)
                    tot = tot + (part if half == kv else pltpu.roll(part, 64, 1))
            ref[pl.ds(prev, BLK), :] += tot[:BLK]
            ref[pl.ds(own, BLK), :] += tot[BLK:]

        @pl.when(i == nb - 1)
        def _():
            dk_ref[PAD_ROWS:BLK, :] += _swa_meta_fold(dkp[...])
            dv_ref[PAD_ROWS:BLK, :] += _swa_meta_fold(dvp[...])

    full = pl.BlockSpec((rows, 128), lambda i: (0, 0))
    return pl.pallas_call(
        body, name="swa_bwd", grid=(nb,),
        in_specs=[_row_spec(BLK, 512), _row_spec(BLK, 512), _row_spec(BLK, 512), VMEM_SPEC, VMEM_SPEC,
                  _row_spec(BLK, SWA_HEADS), SMEM_SPEC, VMEM_SPEC],
        out_specs=[_row_spec(BLK, 512), full, full, _acc_spec(128), _acc_spec(512)],
        out_shape=[jax.ShapeDtypeStruct((rows, 512), F32), jax.ShapeDtypeStruct((rows, 128), F32),
                   jax.ShapeDtypeStruct((rows, 128), F32), jax.ShapeDtypeStruct((8, 128), F32),
                   jax.ShapeDtypeStruct((8, 512), F32)],
        scratch_shapes=[pltpu.VMEM((BLK, 512), BF16), pltpu.VMEM((BLK, 512), BF16),
                        pltpu.VMEM((BLK, 512), F32), pltpu.VMEM((BLK, 512), F32)],
        compiler_params=_params(("arbitrary",)),
    )(dcat, o_all, sq, sk, sv, lse, sinks, wn)


def _mix_out(h, cat_g, cat_s, wout, gpost):
    rows = h.shape[0]
    tm = _row_tile(rows)

    def body(h_ref, cg_ref, cs_ref, w_ref, g_ref, ho_ref, m_ref):
        m = _nn(cg_ref[...], w_ref[0:512, :]) + _nn(cs_ref[...], w_ref[512:1024, :])
        m_ref[...] = m
        mn, _ = _rms(m)
        ho_ref[...] = h_ref[...] + mn * g_ref[...]

    row_f32 = _row_spec(tm, D_MODEL)
    return pl.pallas_call(
        body, name="mix_out", grid=(rows // tm,),
        in_specs=[row_f32, _row_spec(tm, 512), _row_spec(tm, 512), VMEM_SPEC, VMEM_SPEC],
        out_specs=[row_f32, row_f32],
        out_shape=[jax.ShapeDtypeStruct((rows, D_MODEL), F32), jax.ShapeDtypeStruct((rows, D_MODEL), F32)],
        compiler_params=_params(("arbitrary",)),
    )(h, cat_g, cat_s, wout, gpost)


def _mix_out_bwd(dh, m, wout, gpost):
    rows = dh.shape[0]
    tm = _row_tile(rows)

    def body(dh_ref, m_ref, w_ref, g_ref, dcg_ref, dcs_ref, dm_ref, dg_ref):
        first = pl.program_id(0) == 0
        dhv = dh_ref[...]
        mn, rm = _rms(m_ref[...])
        _acc_add(dg_ref, first, _colsum(dhv * mn))
        dm16 = _rms_bwd(mn, rm, g_ref[...], dhv).astype(BF16)
        dm_ref[...] = dm16
        dcat = _nt(dm16, w_ref[...])
        dcg_ref[...] = dcat[:, 0:512]
        dcs_ref[...] = dcat[:, 512:1024]

    row_f32 = _row_spec(tm, D_MODEL)
    return pl.pallas_call(
        body, name="mix_out_bwd", grid=(rows // tm,),
        in_specs=[row_f32, row_f32, VMEM_SPEC, VMEM_SPEC],
        out_specs=[_row_spec(tm, 512), _row_spec(tm, 512), row_f32, _acc_spec(D_MODEL)],
        out_shape=[jax.ShapeDtypeStruct((rows, 512), F32), jax.ShapeDtypeStruct((rows, 512), F32),
                   jax.ShapeDtypeStruct((rows, D_MODEL), BF16), jax.ShapeDtypeStruct((8, D_MODEL), F32)],
        compiler_params=_params(("arbitrary",)),
    )(dh, m, wout, gpost)


def _mix_in_bwd(dh_out, h, g, win_p, wa2_p, cos, sin, loga, ga, dgq, dgk, dgv, dgg, dsq, dsk, dsv, dloga):
    rows = h.shape[0]
    tm = _row_tile(rows)

    def body(dho_ref, h_ref, g_ref, win_ref, wa2_ref, cos_ref, sin_ref, loga_ref, ga_ref,
             dgq_ref, dgk_ref, dgv_ref, dgg_ref, dsq_ref, dsk_ref, dsv_ref, dla_ref,
             dh_ref, dproj_ref, dwa2_ref, dg_ref, dba_ref):
        first = pl.program_id(0) == 0
        dz = dla_ref[...] * (1.0 / GLA_TAU) * (1.0 - jnp.exp(GLA_TAU * loga_ref[...]))
        _acc_add(dba_ref, first, _colsum(dz))
        dga = _nt(dz, wa2_ref[...])
        pa = _tn(ga_ref[...], dz)
        c1, s1 = cos_ref[...], sin_ref[...]
        c4 = jnp.concatenate([c1, c1, c1, c1], axis=1)
        s4 = jnp.concatenate([s1, s1, s1, s1], axis=1)
        dq_r, dk_r = dsq_ref[...], dsk_ref[...]
        dsq = dq_r * c4 - _rot_half(dq_r * s4)
        dsk = dk_r * c1 - _rot_half(dk_r * s1)
        dproj16 = jnp.concatenate(
            [dgq_ref[...], dgk_ref[...], dgv_ref[...], dgg_ref[...], dsq, dsk, dsv_ref[...], dga], axis=1).astype(BF16)
        dproj_ref[...] = dproj16
        dn = _nn(dproj16, win_ref[...])

        @pl.when(first)
        def _():
            dwa2_ref[...] = pa

        @pl.when(jnp.logical_not(first))
        def _():
            dwa2_ref[...] += pa
        hn, rh = _rms(h_ref[...])
        _acc_add(dg_ref, first, _colsum(dn * hn))
        dh_ref[...] = dho_ref[...] + _rms_bwd(hn, rh, g_ref[...], dn)

    rs = lambda c: _row_spec(tm, c)
    return pl.pallas_call(
        body, name="mix_in_bwd", grid=(rows // tm,),
        in_specs=[rs(D_MODEL), rs(D_MODEL), VMEM_SPEC, VMEM_SPEC, VMEM_SPEC, rs(128), rs(128), rs(256), rs(128),
                  rs(256), rs(256), rs(512), rs(512), rs(512), rs(128), rs(128), rs(256)],
        out_specs=[rs(D_MODEL), rs(P_END), pl.BlockSpec((128, 256), lambda i: (0, 0)), _acc_spec(D_MODEL), _acc_spec(256)],
        out_shape=[jax.ShapeDtypeStruct((rows, D_MODEL), F32), jax.ShapeDtypeStruct((rows, P_END), BF16),
                   jax.ShapeDtypeStruct((128, 256), F32), jax.ShapeDtypeStruct((8, D_MODEL), F32),
                   jax.ShapeDtypeStruct((8, 256), F32)],
        compiler_params=_params(("arbitrary",)),
    )(dh_out, h, g, win_p, wa2_p, cos, sin, loga, ga, dgq, dgk, dgv, dgg, dsq, dsk, dsv, dloga)


def _rope_tables(rows):
    pos = (jnp.arange(rows, dtype=jnp.int32) - PAD_ROWS).astype(F32)
    inv_freq = 1.0 / (ROPE_THETA ** (jnp.arange(0, SWA_HD, 2, dtype=F32) / SWA_HD))
    ang = pos[:, None] * inv_freq[None, :]
    ang = jnp.concatenate([ang, ang, ang, ang], axis=-1)
    return jnp.cos(ang), jnp.sin(ang)


def _local_step(h0, tgt, w, late_weights=None, on_grads=None):
    rows = h0.shape[0]
    cos, sin = _rope_tables(rows)
    g = {}

    def tell(group, names):
        for nm in names:
            g[nm] = grads_now[nm]
        return 0.0 if on_grads is None else on_grads(group, {nm: grads_now[nm] for nm in names})

    h1, a1, b1, f1 = _ffn_fwd(h0, w["ffn1_pre"], w["wg1"], w["wu1"], w["wd1"], w["ffn1_post"])
    if late_weights is not None:
        w = {**w, **late_weights("win", f1)}
    gq, gk, gv, gg, sq, sk, sv, ga, loga, bc, n2 = _mix_in(h1, w["mix_pre"], w["win"], w["wa2"], w["b_a"], cos, sin)
    o_g, cat_g, sp = _gla_fwd(gq, gk, gv, gg, bc, w["gla_norm"])
    o_s, cat_s, lse = _swa_fwd(sq, sk, sv, w["sinks"], w["swa_norm"])
    if late_weights is not None:
        w = {**w, **late_weights("rest", lse)}
    h2, m = _mix_out(h1, cat_g, cat_s, w["wout"], w["mix_post"])
    h3, a2, b2, f2, dy, loss = _ffn_fwd(h2, w["ffn2_pre"], w["wg2"], w["wu2"], w["wd2"], w["ffn2_post"], tgt)
    del h3
    dh2, da, db, df, n3, g["ffn2_pre"], g["ffn2_post"] = _ffn_bwd_act(
        dy, h2, a2, b2, f2, w["ffn2_pre"], w["ffn2_post"], w["wg2"], w["wu2"], w["wd2"], "ffn2_bwd_act")
    grads_now = dict(wd2=_wgrad(b2, df, "ffn2_wgrad_down", gate=a2), wg2=_wgrad(da, n3, "ffn2_wgrad_gate"),
                     wu2=_wgrad(db, n3, "ffn2_wgrad_up"))
    tok = tell("ffn2", ("wd2", "wg2", "wu2"))
    dcg, dcs, dm, g["mix_post"] = _mix_out_bwd(dh2, m, w["wout"], w["mix_post"] + tok)
    dsq, dsk, dsv, g["sinks"], g["swa_norm"] = _swa_bwd(dcs, o_s, sq, sk, sv, lse, w["sinks"], w["swa_norm"])
    dgq, dgk, dgv, dgg, dloga, g["gla_norm"] = _gla_bwd(dcg, o_g, gq, gk, gv, gg, bc, sp, w["gla_norm"])
    dh1, dproj, g["wa2"], g["mix_pre"], g["b_a"] = _mix_in_bwd(
        dh2, h1, w["mix_pre"], w["win"], w["wa2"], cos, sin, loga, ga, dgq, dgk, dgv, dgg, dsq, dsk, dsv, dloga)
    grads_now = dict(wout=jnp.concatenate([_wgrad(cat_g, dm, "wout_wgrad_gla"), _wgrad(cat_s, dm, "wout_wgrad_swa")], axis=0),
                     win=_wgrad(dproj, n2, "win_wgrad"))
    tok = tell("mix", ("wout", "win"))
    dh0, da, db, df, n1, g["ffn1_pre"], g["ffn1_post"] = _ffn_bwd_act(
        dh1, h0, a1, b1, f1, w["ffn1_pre"] + tok, w["ffn1_post"], w["wg1"], w["wu1"], w["wd1"], "ffn1_bwd_act")
    grads_now = dict(wd1=_wgrad(b1, df, "ffn1_wgrad_down", gate=a1))
    tell("ffn1_down", ("wd1",))
    grads_now = dict(wg1=_wgrad(da, n1, "ffn1_wgrad_gate"))
    tell("ffn1_gate", ("wg1",))
    grads_now = dict(wu1=_wgrad(db, n1, "ffn1_wgrad_up"))
    tell("ffn1_up", ("wu1",))
    return loss[0, 0], dh0, g


def _win_pad_rows(win_t):
    pad = jnp.zeros((P_END - P_GA - 16, win_t.shape[1]), win_t.dtype)
    return jnp.concatenate([win_t[0:1536], win_t[1552:2320], win_t[1536:1552], pad], axis=0)


def _win_unpad_rows(win_p):
    return jnp.concatenate([win_p[0:1536], win_p[P_GA:P_GA + 16], win_p[1536:P_GA]], axis=0)


def _place_on_mesh():
    return lax.axis_index("x"), lax.axis_index("y"), lax.axis_index("c")


def _dev_index(px, py, pc):
    return 4 * px + 2 * py + pc


def _other_devices(x, y, c):
    flip = lambda v, f: 1 - v if f else v
    return [(flip(x, fx), flip(y, fy), flip(c, fc)) for fx in (0, 1) for fy in (0, 1) for fc in (0, 1)][1:]


def _all_gather(shards):
    n = len(shards)

    def body(*refs):
        ins, outs = refs[:n], refs[n:2 * n]
        zeros_ref, send_sems, recv_sems, local_sems = refs[2 * n:]
        zeros_ref[...] = jnp.zeros_like(zeros_ref)
        x, y, c = _place_on_mesh()
        me, sibling = (x, y, c), (x, y, 1 - c)
        chips = [(1 - x, y), (x, 1 - y), (1 - x, 1 - y)]

        def rows(k, px, py, pc):
            r = ins[k].shape[0]
            return outs[k].at[pl.ds(pl.multiple_of(_dev_index(px, py, pc) * r, 8), r), :]

        def copy(k, slot, block, to, src=None):
            return pltpu.make_async_remote_copy(
                src_ref=rows(k, *block) if src is None else src, dst_ref=rows(k, *block),
                send_sem=send_sems.at[k, slot], recv_sem=recv_sems.at[k, slot], device_id=to, device_id_type=MESH)

        local = [pltpu.make_async_copy(ins[k], rows(k, *me), local_sems.at[k]) for k in range(n)]
        sends = []
        for k in range(n):
            local[k].start()
            sends.append(copy(k, 0, me, sibling, src=ins[k]))
            sends += [copy(k, 1 + j, me, (*chip, c), src=ins[k]) for j, chip in enumerate(chips)]
        for cp in sends:
            cp.start()
        for k in range(n):
            for j, chip in enumerate(chips):
                copy(k, 1 + j, (*chip, c), me).wait_recv()
                passed = copy(k, 4 + j, (*chip, c), sibling)
                passed.start()
                sends.append(passed)
        for k in range(n):
            copy(k, 0, sibling, me).wait_recv()
            for j, chip in enumerate(chips):
                copy(k, 4 + j, (*chip, 1 - c), me).wait_recv()
        for cp in sends:
            cp.wait_send()
        for cp in local:
            cp.wait()

    return pl.pallas_call(
        body, name="all_gather_weights",
        in_specs=[ANY_SPEC] * n, out_specs=[ANY_SPEC] * n + [VMEM_SPEC],
        out_shape=[jax.ShapeDtypeStruct((N_DEV * s.shape[0], s.shape[1]), s.dtype) for s in shards]
        + [jax.ShapeDtypeStruct((8, 128), F32)],
        scratch_shapes=[pltpu.SemaphoreType.DMA((n, 7)), pltpu.SemaphoreType.DMA((n, 7)), pltpu.SemaphoreType.DMA((n,))],
    )(*shards)


HBM_SPEC = pl.BlockSpec(memory_space=pltpu.HBM)
SEM_SPEC = pl.BlockSpec(memory_space=pltpu.SEMAPHORE)
DATAFLOW = pltpu.SideEffectType.DATAFLOW_SIDE_EFFECTING


def _exchange_copies(srcs, lands, send_sems, recv_sems, own_sems, scatter, arriving):
    x, y, c = _place_on_mesh()
    me = _dev_index(x, y, c)
    remote, local = [], []
    for k, (src, land) in enumerate(zip(srcs, lands)):
        r = land.shape[0] // N_DEV

        def block(ref, d):
            return ref.at[pl.ds(pl.multiple_of(d * r, 8), r), :]

        for f, peer in enumerate(_other_devices(x, y, c)):
            mine, his = (_dev_index(*peer), me) if arriving else (me, _dev_index(*peer))
            remote.append(pltpu.make_async_remote_copy(
                src_ref=block(src, his) if scatter else src, dst_ref=block(land, mine),
                send_sem=send_sems.at[7 * k + f], recv_sem=recv_sems.at[7 * k + f], device_id=peer, device_id_type=MESH))
        local.append(pltpu.make_async_copy(block(src, me) if scatter else src, block(land, me), own_sems.at[k]))
    return remote, local


def _exchange_start(srcs, scatter, name):
    n = len(srcs)
    lands = [lax.empty(s.shape if scatter else (N_DEV * s.shape[0], s.shape[1]), s.dtype) for s in srcs]

    def body(*refs):
        remote, local = _exchange_copies(refs[:n], refs[n:2 * n], *refs[2 * n:2 * n + 3], scatter, False)
        for cp in remote + local:
            cp.start()
        refs[-1][...] = jnp.zeros_like(refs[-1])

    both = list(srcs) + list(lands)
    outs = pl.pallas_call(
        body, name=name,
        out_shape=(pltpu.SemaphoreType.DMA((7 * n,)), pltpu.SemaphoreType.DMA((7 * n,)), pltpu.SemaphoreType.DMA((n,)),
                   *[pltpu.HBM(a.shape, a.dtype) for a in both], jax.ShapeDtypeStruct((8, 128), F32)),
        in_specs=[HBM_SPEC] * (2 * n), out_specs=(SEM_SPEC, SEM_SPEC, SEM_SPEC, *[HBM_SPEC] * (2 * n), VMEM_SPEC),
        input_output_aliases={i: 3 + i for i in range(2 * n)},
        compiler_params=pltpu.CompilerParams(has_side_effects=DATAFLOW),
    )(*[pltpu.with_memory_space_constraint(a, pltpu.HBM) for a in both])
    return outs[0:3], outs[3:3 + n], outs[3 + n:3 + 2 * n], outs[-1]


def _exchange_wait(started, scatter, after, name):
    sems, srcs, lands, _ = started
    n = len(srcs)

    def body(*refs):
        args = (refs[:n], refs[n:2 * n], *refs[2 * n:2 * n + 3], scatter)
        going, local = _exchange_copies(*args, False)
        for cp in going:
            cp.wait_send()
        for cp in local:
            cp.wait()
        for cp in _exchange_copies(*args, True)[0]:
            cp.wait_recv()

    both = list(srcs) + list(lands)
    outs = pl.pallas_call(
        body, name=name, out_shape=[pltpu.HBM(a.shape, a.dtype) for a in both],
        in_specs=[HBM_SPEC] * (2 * n) + [SEM_SPEC, SEM_SPEC, SEM_SPEC, ANY_SPEC], out_specs=[HBM_SPEC] * (2 * n),
        input_output_aliases={i: i for i in range(2 * n)},
        compiler_params=pltpu.CompilerParams(has_side_effects=DATAFLOW),
    )(*both, *sems, after)
    return outs[n:]


def _sum_partials(parts, name):
    n = len(parts)

    def body(*refs):
        ins, outs = refs[:n], refs[n:]
        first = pl.program_id(0) == 0
        for i_ref, o_ref in zip(ins, outs):
            v = i_ref[...].astype(F32)

            @pl.when(first)
            def _():
                o_ref[...] = v

            @pl.when(jnp.logical_not(first))
            def _():
                o_ref[...] += v

    shapes = [(p.shape[0] // N_DEV, p.shape[1]) for p in parts]
    return pl.pallas_call(
        body, name=name, grid=(N_DEV,),
        in_specs=[pl.BlockSpec(s, lambda j: (j, 0)) for s in shapes],
        out_specs=[pl.BlockSpec(s, lambda j: (0, 0)) for s in shapes],
        out_shape=[jax.ShapeDtypeStruct(s, F32) for s in shapes],
        compiler_params=_params(("arbitrary",)),
    )(*parts)


def _all_reduce_small(slab):
    rows, cols = slab.shape

    def body(x_ref, o_ref, gathered, send_sems, recv_sems):
        x, y, c = _place_on_mesh()
        me = _dev_index(x, y, c)
        peers = _other_devices(x, y, c)

        def copy(f, peer):
            return pltpu.make_async_remote_copy(
                src_ref=x_ref, dst_ref=gathered.at[me], send_sem=send_sems.at[f], recv_sem=recv_sems.at[f],
                device_id=peer, device_id_type=MESH)

        def arrival(f, peer):
            return pltpu.make_async_remote_copy(
                src_ref=x_ref, dst_ref=gathered.at[_dev_index(*peer)], send_sem=send_sems.at[f], recv_sem=recv_sems.at[f],
                device_id=peer, device_id_type=MESH)

        sends = [copy(f, peer) for f, peer in enumerate(peers)]
        for cp in sends:
            cp.start()
        gathered[me] = x_ref[...]
        for f, peer in enumerate(peers):
            arrival(f, peer).wait_recv()
        for cp in sends:
            cp.wait_send()
        total = gathered[0]
        for d in range(1, N_DEV):
            total = total + gathered[d]
        o_ref[...] = total

    return pl.pallas_call(
        body, name="all_reduce_small",
        in_specs=[VMEM_SPEC], out_specs=VMEM_SPEC, out_shape=jax.ShapeDtypeStruct((rows, cols), F32),
        scratch_shapes=[pltpu.VMEM((N_DEV, rows, cols), F32), pltpu.SemaphoreType.DMA((7,)), pltpu.SemaphoreType.DMA((7,))],
    )(slab)


def _adamw(ws, gs, ms, vs, name):
    n = len(ws)
    c1 = 1.0 / (1.0 - ADAM_B1 ** ADAM_STEP)
    c2 = 1.0 / (1.0 - ADAM_B2 ** ADAM_STEP)

    def body(*refs):
        w_r, g_r, m_r, v_r = refs[:n], refs[n:2 * n], refs[2 * n:3 * n], refs[3 * n:4 * n]
        d_o, m_o, v_o = refs[4 * n:5 * n], refs[5 * n:6 * n], refs[6 * n:7 * n]
        for k in range(n):
            g = g_r[k][...]
            m = ADAM_B1 * m_r[k][...] + (1.0 - ADAM_B1) * g
            v = ADAM_B2 * v_r[k][...] + (1.0 - ADAM_B2) * (g * g)
            m_o[k][...] = m
            v_o[k][...] = v
            d_o[k][...] = -ADAM_LR * ((m * c1) / (jnp.sqrt(v * c2) + ADAM_EPS) + ADAM_WD * w_r[k][...])

    shapes = [jax.ShapeDtypeStruct(w.shape, F32) for w in ws]
    outs = pl.pallas_call(
        body, name=name, in_specs=[VMEM_SPEC] * (4 * n), out_specs=[VMEM_SPEC] * (3 * n), out_shape=shapes * 3,
        compiler_params=pltpu.CompilerParams(vmem_limit_bytes=56 << 20),
    )(*ws, *gs, *ms, *vs)
    return outs[:n], outs[n:2 * n], outs[2 * n:]


WEIGHT_NAMES = ("meta_tokens", "ffn1_pre_norm", "ffn1_w_gate", "ffn1_w_up", "ffn1_w_down", "ffn1_post_norm", "mix_pre_norm",
                "w_in", "gla_w_a2", "gla_b_a", "gla_out_norm", "swa_sinks", "swa_out_norm", "w_out", "mix_post_norm",
                "ffn2_pre_norm", "ffn2_w_gate", "ffn2_w_up", "ffn2_w_down", "ffn2_post_norm")
WIN_SHARD = D_IN // N_DEV
WIN_SHARD_PAD = 304
SLAB_VECTORS = ("ffn1_pre", "ffn1_post", "mix_pre", "mix_post", "ffn2_pre", "ffn2_post")
SLAB_ROWS = 32


def kernel(x, meta_tokens, ffn1_pre_norm, ffn1_w_gate, ffn1_w_up, ffn1_w_down, ffn1_post_norm, mix_pre_norm, w_in, gla_w_a2, gla_b_a, gla_out_norm, swa_sinks, swa_out_norm, w_out, mix_post_norm, ffn2_pre_norm, ffn2_w_gate, ffn2_w_up, ffn2_w_down, ffn2_post_norm, loss_target, m_meta_tokens, m_ffn1_pre_norm, m_ffn1_w_gate, m_ffn1_w_up, m_ffn1_w_down, m_ffn1_post_norm, m_mix_pre_norm, m_w_in, m_gla_w_a2, m_gla_b_a, m_gla_out_norm, m_swa_sinks, m_swa_out_norm, m_w_out, m_mix_post_norm, m_ffn2_pre_norm, m_ffn2_w_gate, m_ffn2_w_up, m_ffn2_w_down, m_ffn2_post_norm, v_meta_tokens, v_ffn1_pre_norm, v_ffn1_w_gate, v_ffn1_w_up, v_ffn1_w_down, v_ffn1_post_norm, v_mix_pre_norm, v_w_in, v_gla_w_a2, v_gla_b_a, v_gla_out_norm, v_swa_sinks, v_swa_out_norm, v_w_out, v_mix_post_norm, v_ffn2_pre_norm, v_ffn2_w_gate, v_ffn2_w_up, v_ffn2_w_down, v_ffn2_post_norm):
    given = dict(locals())
    W = {n: given[n] for n in WEIGHT_NAMES}
    M = {n: given["m_" + n] for n in WEIGHT_NAMES}
    V = {n: given["v_" + n] for n in WEIGHT_NAMES}
    dev = _dev_index(*_place_on_mesh())

    def t16(w):
        return w[0].T.astype(BF16)

    small = jnp.concatenate([W["meta_tokens"], jnp.pad(W["gla_w_a2"][0], ((0, 0), (0, 96)))], axis=0)
    wg1, wu1, wd1, small_g, gathered_zeros = _all_gather(
        [t16(W["ffn1_w_gate"]), t16(W["ffn1_w_up"]), W["ffn1_w_down"][0].astype(BF16), small])
    def after_zero(shard, zeros):
        return shard + zeros[0:1, 0:1].astype(shard.dtype)
    win_shard = jnp.pad(t16(W["w_in"]), ((0, WIN_SHARD_PAD - WIN_SHARD), (0, 0)))
    win_shard = after_zero(win_shard, gathered_zeros)
    mid = _exchange_start([win_shard], False, "gather_w_in_start")
    late_shards = [after_zero(W["w_out"][0].astype(BF16), mid[3]), t16(W["ffn2_w_gate"]), t16(W["ffn2_w_up"]),
                   W["ffn2_w_down"][0].astype(BF16)]
    late = _exchange_start(late_shards, False, "gather_late_weights_start")

    def late_weights(what, after):
        if what == "win":
            win_g, = _exchange_wait(mid, False, after, "gather_w_in_wait")
            win_t = win_g.reshape(N_DEV, WIN_SHARD_PAD, D_MODEL)[:, :WIN_SHARD].reshape(D_IN, D_MODEL)
            return dict(win=_win_pad_rows(win_t))
        wout, wg2, wu2, wd2 = _exchange_wait(late, False, after, "gather_late_weights_wait")
        return dict(wout=wout, wg2=wg2, wu2=wu2, wd2=wd2)

    small_g = small_g.reshape(N_DEV, 32, 128)
    meta_full = small_g[:, :N_META].transpose(1, 0, 2).reshape(N_META, D_MODEL)
    wa2_full = small_g[:, N_META:, :32].transpose(1, 0, 2).reshape(16, 256)
    w = dict(
        ffn1_pre=W["ffn1_pre_norm"] + late[3][0, 0], ffn1_post=W["ffn1_post_norm"], mix_pre=W["mix_pre_norm"],
        mix_post=W["mix_post_norm"], ffn2_pre=W["ffn2_pre_norm"], ffn2_post=W["ffn2_post_norm"], b_a=W["gla_b_a"],
        gla_norm=W["gla_out_norm"], sinks=W["swa_sinks"], swa_norm=W["swa_out_norm"], wg1=wg1, wu1=wu1, wd1=wd1,
        wa2=jnp.pad(wa2_full, ((0, 112), (0, 0))))

    in_flight = []

    def on_grads(group, grads):
        parts = []
        for nm, p in grads.items():
            if nm == "win":
                p = _win_unpad_rows(p).reshape(N_DEV, WIN_SHARD, D_MODEL)
                p = jnp.pad(p, ((0, 0), (0, WIN_SHARD_PAD - WIN_SHARD), (0, 0))).reshape(N_DEV * WIN_SHARD_PAD, D_MODEL)
            parts.append(p)
        started = _exchange_start(parts, True, "scatter_" + group + "_start")
        in_flight.append((group, list(grads), started))
        return started[3][0, 0]

    front = jnp.zeros((PAD_ROWS, D_MODEL), F32)
    h0 = jnp.concatenate([front, meta_full, x[0]], axis=0)
    tgt = jnp.concatenate([jnp.zeros((BLK, D_MODEL), F32), loss_target[0]], axis=0)
    loss, dh0, g = _local_step(h0, tgt, w, late_weights, on_grads)
    grad_x = dh0[BLK:][None]

    packed = jnp.concatenate([g["b_a"][0:1], g["gla_norm"][0:1], g["sinks"][0:1], g["swa_norm"][0:1]], axis=1)
    slab = jnp.concatenate([g[k][0:1] for k in SLAB_VECTORS] + [packed, jnp.full((1, D_MODEL), loss, F32),
                           g["wa2"][:16].reshape(4, D_MODEL), jnp.zeros((4, D_MODEL), F32), dh0[PAD_ROWS:BLK]], axis=0)
    tot = _all_reduce_small(slab)
    loss = tot[7, 0]
    small_grads = dict(
        ffn1_pre_norm=tot[0:1], ffn1_post_norm=tot[1:2], mix_pre_norm=tot[2:3], mix_post_norm=tot[3:4],
        ffn2_pre_norm=tot[4:5], ffn2_post_norm=tot[5:6], gla_b_a=tot[6:7, 0:256], gla_out_norm=tot[6:7, 256:384],
        swa_sinks=tot[6:7, 384:392], swa_out_norm=tot[6:7, 512:1024],
        gla_w_a2=lax.dynamic_slice_in_dim(tot[8:12].reshape(16, 256), dev * 32, 32, axis=1)[None],
        meta_tokens=lax.dynamic_slice_in_dim(tot[16:32], dev * 128, 128, axis=1))

    big = dict(wg1=("ffn1_w_gate", True), wu1=("ffn1_w_up", True), wd1=("ffn1_w_down", False), win=("w_in", True),
               wout=("w_out", False), wg2=("ffn2_w_gate", True), wu2=("ffn2_w_up", True), wd2=("ffn2_w_down", False))
    grads = dict(small_grads)
    delta, new_m, new_v = {}, {}, {}
    names = [n for n in WEIGHT_NAMES if n not in [full for full, _ in big.values()]]
    two_d = lambda a: a.reshape(-1, a.shape[-1])
    d_, m_, v_ = _adamw([two_d(W[n]) for n in names], [two_d(grads[n]) for n in names],
                        [two_d(M[n]) for n in names], [two_d(V[n]) for n in names], "adamw_small")
    for k, n in enumerate(names):
        delta[n], new_m[n], new_v[n] = d_[k].reshape(W[n].shape), m_[k].reshape(W[n].shape), v_[k].reshape(W[n].shape)

    before_wait = d_[0] + in_flight[-1][2][3][0, 0]
    for group, shorts, started in in_flight:
        lands = _exchange_wait(started, True, before_wait, "scatter_" + group + "_wait")
        for short, g_slab in zip(shorts, _sum_partials(lands, "sum_" + group)):
            n, transposed = big[short]
            to_slab = (lambda a: a[0].T) if transposed else (lambda a: a[0])
            from_slab = (lambda a: a.T[None]) if transposed else (lambda a: a[None])
            g_slab = g_slab[:WIN_SHARD] if short == "win" else g_slab
            d_, m_, v_ = _adamw([to_slab(W[n])], [g_slab], [to_slab(M[n])], [to_slab(V[n])], "adamw_" + n)
            grads[n], delta[n], new_m[n], new_v[n] = from_slab(g_slab), from_slab(d_[0]), from_slab(m_[0]), from_slab(v_[0])
            before_wait = d_[0]
    return (loss, grad_x, *[grads[n] for n in WEIGHT_NAMES], *[delta[n] for n in WEIGHT_NAMES],
            *[new_m[n] for n in WEIGHT_NAMES], *[new_v[n] for n in WEIGHT_NAMES])
```

```python
import functools

import jax
import jax.numpy as jnp
from jax import lax
from jax.experimental import pallas as pl
from jax.experimental.pallas import tpu as pltpu

F32, BF16 = jnp.float32, jnp.bfloat16

D_MODEL = 1024
D_FF = 2816
N_META = 16
BLK = 128
PAD_ROWS = BLK - N_META
GLA_DK = 64
SWA_HD = 64
SWA_HEADS = 8
GLA_TAU = 16.0
NORM_EPS = 1e-6
NEG_INF = -1e30
ROPE_THETA = 10000.0
P_GQ, P_GK, P_GV, P_GG, P_SQ, P_SK, P_SV, P_GA, P_END = 0, 256, 512, 1024, 1536, 2048, 2176, 2304, 2432
D_IN = 2320
IN_SPLITS = (256, 256, 512, 512, 16, 512, 128, 128)
FF_TILE = 2816
WGRAD_TILE_MAX = 2432
N_DEV = 8
MESH = pl.DeviceIdType.MESH

ADAM_LR, ADAM_B1, ADAM_B2, ADAM_EPS, ADAM_WD, ADAM_STEP = 0.001, 0.9, 0.999, 1e-08, 0.01, 10

V7X_VMEM_BYTES = 64 << 20
VMEM_SPEC = pl.BlockSpec(memory_space=pltpu.VMEM)
SMEM_SPEC = pl.BlockSpec(memory_space=pltpu.SMEM)
ANY_SPEC = pl.BlockSpec(memory_space=pl.ANY)


def _params(semantics, vmem_mb=56):
    return pltpu.CompilerParams(dimension_semantics=semantics, vmem_limit_bytes=vmem_mb << 20)


def _row_tile(rows):
    return 320 if rows % 320 == 0 else BLK


def _nn(a, b):
    return lax.dot_general(a, b, (((1,), (0,)), ((), ())), preferred_element_type=F32)


def _nt(a, b):
    return lax.dot_general(a, b, (((1,), (1,)), ((), ())), preferred_element_type=F32)


def _tn(a, b):
    return lax.dot_general(a, b, (((0,), (0,)), ((), ())), preferred_element_type=F32)


def _rms(x):
    r = lax.rsqrt(jnp.mean(x * x, axis=-1, keepdims=True) + NORM_EPS)
    return x * r, r


def _rms_bwd(xn, r, w, dy):
    g = dy * w
    return r * (g - xn * jnp.mean(g * xn, axis=-1, keepdims=True))


def _sigmoid(x):
    return 1.0 / (1.0 + jnp.exp(-x))


def _colsum(x):
    return jnp.sum(x, axis=0, keepdims=True)


def _split_bf16(x):
    hi = x.astype(BF16)
    lo = (x - hi.astype(F32)).astype(BF16)
    return hi, lo


def _tri(lower):
    r = lax.broadcasted_iota(jnp.int32, (BLK, BLK), 0)
    c = lax.broadcasted_iota(jnp.int32, (BLK, BLK), 1)
    return (r >= c) if lower else (c >= r)


def _half_mask(width, half):
    lane = lax.broadcasted_iota(jnp.int32, (1, width), 1)
    return ((lane % 128) < 64) if half == 0 else ((lane % 128) >= 64)


def _rot_half(x):
    w = x.shape[-1]
    lane = lax.broadcasted_iota(jnp.int32, (1, w), 1)
    return jnp.where((lane % SWA_HD) < SWA_HD // 2, -pltpu.roll(x, w - SWA_HD // 2, 1), pltpu.roll(x, SWA_HD // 2, 1))


def _row_spec(tm, cols):
    return pl.BlockSpec((tm, cols), lambda i: (i, 0))


def _acc_spec(cols):
    return pl.BlockSpec((8, cols), lambda i: (0, 0))


def _acc_add(ref, first, value):
    @pl.when(first)
    def _():
        ref[...] = jnp.zeros_like(ref)
    ref[0:1, :] += value


def _ffn_fwd(h, gpre, wg_t, wu_t, wd, gpost, tgt=None):
    rows = h.shape[0]
    tm = _row_tile(rows)
    nf = D_FF // FF_TILE
    with_loss = tgt is not None

    def body(*refs):
        if with_loss:
            (h_ref, gpre_ref, wg_ref, wu_ref, wd_ref, gpost_ref, t_ref,
             ho_ref, a_ref, b_ref, f_ref, dy_ref, loss_ref, acc) = refs
        else:
            (h_ref, gpre_ref, wg_ref, wu_ref, wd_ref, gpost_ref, ho_ref, a_ref, b_ref, f_ref, acc) = refs
        i = pl.program_id(0)
        h_in = h_ref[...]
        hn, _ = _rms(h_in)
        n16 = (hn * gpre_ref[...]).astype(BF16)
        for j in range(nf):
            cols = slice(j * FF_TILE, (j + 1) * FF_TILE)
            a = _nt(n16, wg_ref[cols, :])
            b = _nt(n16, wu_ref[cols, :])
            a_ref[:, cols] = a.astype(BF16)
            b_ref[:, cols] = b.astype(BF16)
            s16 = (a * _sigmoid(a) * b).astype(BF16)
            part = _nn(s16, wd_ref[cols, :])
            if j == 0:
                acc[...] = part
            else:
                acc[...] += part
        f = acc[...]
        f_ref[...] = f
        fn, _ = _rms(f)
        y = h_in + 0.5 * (fn * gpost_ref[...])
        ho_ref[...] = y
        if with_loss:
            row = i * tm + lax.broadcasted_iota(jnp.int32, (tm, 1), 0)
            err = jnp.where(row >= BLK, y - t_ref[...], 0.0)
            dy_ref[...] = err * (1.0 / D_MODEL)
            part = 0.5 * jnp.sum(jnp.sum(err * err, axis=-1, keepdims=True) * (1.0 / D_MODEL), axis=0, keepdims=True)

            @pl.when(i == 0)
            def _():
                loss_ref[...] = jnp.zeros_like(loss_ref)
            loss_ref[...] += part

    row_f32 = _row_spec(tm, D_MODEL)
    in_specs = [row_f32, VMEM_SPEC, VMEM_SPEC, VMEM_SPEC, VMEM_SPEC, VMEM_SPEC]
    out_specs = [row_f32, _row_spec(tm, D_FF), _row_spec(tm, D_FF), row_f32]
    out_shape = [jax.ShapeDtypeStruct((rows, D_MODEL), F32), jax.ShapeDtypeStruct((rows, D_FF), BF16),
                 jax.ShapeDtypeStruct((rows, D_FF), BF16), jax.ShapeDtypeStruct((rows, D_MODEL), F32)]
    args = [h, gpre, wg_t, wu_t, wd, gpost]
    if with_loss:
        in_specs.append(row_f32)
        args.append(tgt)
        out_specs += [row_f32, pl.BlockSpec((8, 128), lambda i: (0, 0))]
        out_shape += [jax.ShapeDtypeStruct((rows, D_MODEL), F32), jax.ShapeDtypeStruct((8, 128), F32)]
    return pl.pallas_call(
        body, name="ffn_fwd_loss" if with_loss else "ffn_fwd", grid=(rows // tm,),
        in_specs=in_specs, out_specs=out_specs, out_shape=out_shape,
        scratch_shapes=[pltpu.VMEM((tm, D_MODEL), F32)],
        compiler_params=_params(("arbitrary",)),
    )(*args)


def _ffn_bwd_act(dh_out, h, a, b, f, gpre, gpost, wg_t, wu_t, wd, name):
    rows = h.shape[0]
    tm = _row_tile(rows)
    nf = D_FF // FF_TILE

    def body(dho_ref, h_ref, a_ref, b_ref, f_ref, gpre_ref, gpost_ref, wg_ref, wu_ref, wd_ref,
             dh_ref, da_ref, db_ref, df_ref, n_ref, dgpre_ref, dgpost_ref, acc):
        first = pl.program_id(0) == 0
        dho = dho_ref[...]
        drr = 0.5 * dho
        fn, rf = _rms(f_ref[...])
        _acc_add(dgpost_ref, first, _colsum(drr * fn))
        df16 = _rms_bwd(fn, rf, gpost_ref[...], drr).astype(BF16)
        df_ref[...] = df16
        hn, rh = _rms(h_ref[...])
        n_ref[...] = (hn * gpre_ref[...]).astype(BF16)
        for j in range(nf):
            cols = slice(j * FF_TILE, (j + 1) * FF_TILE)
            ds = _nt(df16, wd_ref[cols, :])
            av = a_ref[:, cols].astype(F32)
            bv = b_ref[:, cols].astype(F32)
            sg = _sigmoid(av)
            db16 = (ds * (av * sg)).astype(BF16)
            da16 = (ds * bv * (sg * (1.0 + av * (1.0 - sg)))).astype(BF16)
            da_ref[:, cols] = da16
            db_ref[:, cols] = db16
            part = _nn(da16, wg_ref[cols, :]) + _nn(db16, wu_ref[cols, :])
            if j == 0:
                acc[...] = part
            else:
                acc[...] += part
        dn = acc[...]
        _acc_add(dgpre_ref, first, _colsum(dn * hn))
        dh_ref[...] = dho + _rms_bwd(hn, rh, gpre_ref[...], dn)

    row_f32 = _row_spec(tm, D_MODEL)
    row_ff = _row_spec(tm, D_FF)
    return pl.pallas_call(
        body, name=name, grid=(rows // tm,),
        in_specs=[row_f32, row_f32, row_ff, row_ff, row_f32, VMEM_SPEC, VMEM_SPEC, VMEM_SPEC, VMEM_SPEC, VMEM_SPEC],
        out_specs=[row_f32, row_ff, row_ff, row_f32, row_f32, _acc_spec(D_MODEL), _acc_spec(D_MODEL)],
        out_shape=[jax.ShapeDtypeStruct((rows, D_MODEL), F32), jax.ShapeDtypeStruct((rows, D_FF), BF16),
                   jax.ShapeDtypeStruct((rows, D_FF), BF16), jax.ShapeDtypeStruct((rows, D_MODEL), BF16),
                   jax.ShapeDtypeStruct((rows, D_MODEL), BF16), jax.ShapeDtypeStruct((8, D_MODEL), F32),
                   jax.ShapeDtypeStruct((8, D_MODEL), F32)],
        scratch_shapes=[pltpu.VMEM((tm, D_MODEL), F32)],
        compiler_params=_params(("arbitrary",)),
    )(dh_out, h, a, b, f, gpre, gpost, wg_t, wu_t, wd)


def _wgrad(lhs, rhs, name, gate=None):
    rows, width = lhs.shape
    tm = 1664 if rows % 1664 == 0 else BLK
    tf = width // 2 if width > WGRAD_TILE_MAX else width
    nr = rows // tm
    gated = gate is not None

    def body(*refs):
        if gated:
            g_ref, l_ref, r_ref, o_ref, acc = refs
            gv = g_ref[...].astype(F32)
            lv = (gv * _sigmoid(gv) * l_ref[...].astype(F32)).astype(BF16)
        else:
            l_ref, r_ref, o_ref, acc = refs
            lv = l_ref[...]
        i = pl.program_id(1)
        part = _tn(lv, r_ref[...])

        @pl.when(i == 0)
        def _():
            acc[...] = part

        @pl.when(i > 0)
        def _():
            acc[...] += part

        @pl.when(i == nr - 1)
        def _():
            o_ref[...] = acc[...].astype(BF16)

    l_spec = pl.BlockSpec((tm, tf), lambda j, i: (i, j))
    r_spec = pl.BlockSpec((tm, D_MODEL), lambda j, i: (i, 0))
    return pl.pallas_call(
        body, name=name, grid=(width // tf, nr),
        in_specs=([l_spec] if gated else []) + [l_spec, r_spec],
        out_specs=pl.BlockSpec((tf, D_MODEL), lambda j, i: (j, 0)),
        out_shape=jax.ShapeDtypeStruct((width, D_MODEL), BF16),
        scratch_shapes=[pltpu.VMEM((tf, D_MODEL), F32)],
        compiler_params=_params(("arbitrary", "arbitrary")),
    )(*([gate] if gated else []), lhs, rhs)


def _chunk_cumsum(x, lower):
    tri = jnp.where(_tri(lower), 1.0, 0.0).astype(BF16)
    hi, lo = _split_bf16(x)
    return _nn(tri, hi) + _nn(tri, lo)


def _mix_in(h, g, win_p, wa2_p, b_a, cos, sin):
    rows = h.shape[0]
    tm = 640 if rows % 640 == 0 else BLK

    def body(h_ref, g_ref, win_ref, wa2_ref, ba_ref, cos_ref, sin_ref,
             gq_ref, gk_ref, gv_ref, gg_ref, sq_ref, sk_ref, sv_ref, ga_ref, loga_ref, bc_ref, n_ref):
        hn, _ = _rms(h_ref[...])
        n16 = (hn * g_ref[...]).astype(BF16)
        n_ref[...] = n16
        proj = _nt(n16, win_ref[...])
        gq_ref[...] = proj[:, P_GQ:P_GK]
        gk_ref[...] = proj[:, P_GK:P_GV]
        gv_ref[...] = proj[:, P_GV:P_GG].astype(BF16)
        gg_ref[...] = proj[:, P_GG:P_SQ]
        c1, s1 = cos_ref[...], sin_ref[...]
        c4 = jnp.concatenate([c1, c1, c1, c1], axis=1)
        s4 = jnp.concatenate([s1, s1, s1, s1], axis=1)
        sq = proj[:, P_SQ:P_SK]
        sk = proj[:, P_SK:P_SV]
        sq_ref[...] = (sq * c4 + _rot_half(sq) * s4).astype(BF16)
        sk_ref[...] = (sk * c1 + _rot_half(sk) * s1).astype(BF16)
        sv_ref[...] = proj[:, P_SV:P_GA].astype(BF16)
        ga = proj[:, P_GA:P_END]
        ga_ref[...] = ga
        z = _nn(ga, wa2_ref[...]) + ba_ref[...]
        loga = (jnp.minimum(z, 0.0) - jnp.log(1.0 + jnp.exp(-jnp.abs(z)))) * (1.0 / GLA_TAU)
        loga_ref[...] = loga
        for c in range(tm // BLK):
            rs = slice(c * BLK, (c + 1) * BLK)
            bc_ref[rs, :] = _chunk_cumsum(loga[rs, :], True)

    f32 = lambda c: jax.ShapeDtypeStruct((rows, c), F32)
    b16 = lambda c: jax.ShapeDtypeStruct((rows, c), BF16)
    rs = lambda c: _row_spec(tm, c)
    return pl.pallas_call(
        body, name="mix_in", grid=(rows // tm,),
        in_specs=[rs(D_MODEL), VMEM_SPEC, VMEM_SPEC, VMEM_SPEC, VMEM_SPEC, rs(128), rs(128)],
        out_specs=[rs(256), rs(256), rs(512), rs(512), rs(512), rs(128), rs(128), rs(128), rs(256), rs(256), rs(D_MODEL)],
        out_shape=[f32(256), f32(256), b16(512), f32(512), b16(512), b16(128), b16(128), f32(128), f32(256), f32(256),
                   b16(D_MODEL)],
        compiler_params=_params(("arbitrary",)),
    )(h, g, win_p, wa2_p, b_a, cos, sin)


def _gla_factors(q, k, bc):
    bm = bc[BLK // 2 - 1:BLK // 2, :]
    bl = bc[BLK - 1:BLK, :]
    e_q, e_k, e_qe, e_kd = jnp.exp(bc - bm), jnp.exp(bm - bc), jnp.exp(bc), jnp.exp(bl - bc)
    return (q * e_q, k * e_k, q * e_qe, k * e_kd), (e_q, e_k, e_qe, e_kd), jnp.exp(bl)


def _gla_fwd(gq, gk, gv, gg, bc, wgn):
    rows = gq.shape[0]
    nc = rows // BLK
    scale = GLA_DK ** -0.5

    def body(q_ref, k_ref, v_ref, gg_ref, bc_ref, wgn_ref, o_ref, cat_ref, sp_ref, st):
        @pl.when(pl.program_id(0) == 0)
        def _():
            st[...] = jnp.zeros_like(st)
        low = _tri(True)
        wgn_v = wgn_ref[...]
        for p in range(2):
            sl = slice(128 * p, 128 * p + 128)
            (qt, kt, qe, kd), _, ebl = _gla_factors(q_ref[:, sl] * scale, k_ref[:, sl], bc_ref[:, sl])
            s_prev = st[p]
            sp_ref[0, p] = s_prev
            s16 = s_prev.astype(BF16)
            qt16 = qt.astype(BF16)
            s_new = s_prev * ebl
            for hh in range(2):
                hs = slice(128 * (2 * p + hh), 128 * (2 * p + hh) + 128)
                lm = _half_mask(128, hh)
                vh = v_ref[:, hs]
                pm = jnp.where(low, _nt(qt16, jnp.where(lm, kt, 0.0).astype(BF16)), 0.0)
                o = _nn(pm.astype(BF16), vh) + _nt(jnp.where(lm, qe, 0.0).astype(BF16), s16)
                s_new = s_new + _tn(vh, jnp.where(lm, kd, 0.0).astype(BF16))
                o_ref[:, hs] = o
                on, _ = _rms(o)
                gate = gg_ref[:, hs]
                cat_ref[:, hs] = (on * wgn_v * (gate * _sigmoid(gate))).astype(BF16)
            st[p] = s_new

    rs = lambda c: _row_spec(BLK, c)
    return pl.pallas_call(
        body, name="gla_fwd", grid=(nc,),
        in_specs=[rs(256), rs(256), rs(512), rs(512), rs(256), VMEM_SPEC],
        out_specs=[rs(512), rs(512), pl.BlockSpec((1, 2, 128, 128), lambda i: (i, 0, 0, 0))],
        out_shape=[jax.ShapeDtypeStruct((rows, 512), F32), jax.ShapeDtypeStruct((rows, 512), BF16),
                   jax.ShapeDtypeStruct((nc, 2, 128, 128), F32)],
        scratch_shapes=[pltpu.VMEM((2, 128, 128), F32)],
        compiler_params=_params(("arbitrary",)),
    )(gq, gk, gv, gg, bc, wgn)


def _gla_bwd(dcat, o_all, gq, gk, gv, gg, bc, sp, wgn):
    rows = gq.shape[0]
    nc = rows // BLK
    scale = GLA_DK ** -0.5

    def body(dc_ref, o_ref, q_ref, k_ref, v_ref, gg_ref, bc_ref, sp_ref, wgn_ref,
             dq_ref, dk_ref, dv_ref, dgg_ref, dla_ref, dwgn_ref, dst):
        first = pl.program_id(0) == 0

        @pl.when(first)
        def _():
            dst[...] = jnp.zeros_like(dst)
        low, upp = _tri(True), _tri(False)
        last_row = lax.broadcasted_iota(jnp.int32, (BLK, 1), 0) == BLK - 1
        wgn_v = wgn_ref[...]
        dwgn = jnp.zeros((1, 128), F32)
        for p in range(2):
            sl = slice(128 * p, 128 * p + 128)
            (qt, kt, qe, kd), (e_q, e_k, e_qe, e_kd), ebl = _gla_factors(
                q_ref[:, sl] * scale, k_ref[:, sl], bc_ref[:, sl])
            s_prev = sp_ref[0, p]
            s16 = s_prev.astype(BF16)
            ds_next = dst[p]
            ds16 = ds_next.astype(BF16)
            qt16 = qt.astype(BF16)
            ds_new = ds_next * ebl
            dqt = jnp.zeros((BLK, 128), F32)
            dkt = jnp.zeros((BLK, 128), F32)
            dqe = jnp.zeros((BLK, 128), F32)
            dkd = jnp.zeros((BLK, 128), F32)
            for hh in range(2):
                hs = slice(128 * (2 * p + hh), 128 * (2 * p + hh) + 128)
                lm = _half_mask(128, hh)
                on, ro = _rms(o_ref[:, hs])
                gate = gg_ref[:, hs]
                sg = _sigmoid(gate)
                si = gate * sg
                dog = dc_ref[:, hs]
                dwgn = dwgn + _colsum(dog * si * on)
                dgg_ref[:, hs] = dog * (on * wgn_v) * (sg * (1.0 + gate * (1.0 - sg)))
                do16 = _rms_bwd(on, ro, wgn_v, dog * si).astype(BF16)
                vh = v_ref[:, hs]
                ktm16 = jnp.where(lm, kt, 0.0).astype(BF16)
                qtm16 = jnp.where(lm, qt, 0.0).astype(BF16)
                qem16 = jnp.where(lm, qe, 0.0).astype(BF16)
                kdm16 = jnp.where(lm, kd, 0.0).astype(BF16)
                p_t = jnp.where(upp, _nt(ktm16, qt16), 0.0)
                dp_t = jnp.where(upp, _nt(vh, do16), 0.0)
                dp = jnp.where(low, _nt(do16, vh), 0.0)
                dv_ref[:, hs] = _nn(p_t.astype(BF16), do16) + _nt(kdm16, ds16)
                dqt = dqt + _nn(dp.astype(BF16), ktm16)
                dkt = dkt + _nn(dp_t.astype(BF16), qtm16)
                dqe = dqe + jnp.where(lm, _nn(do16, s16), 0.0)
                dkd = dkd + jnp.where(lm, _nn(vh, ds16), 0.0)
                ds_new = ds_new + _tn(do16, qem16)
            debl = _colsum(ds_next * s_prev)
            dq_ref[:, sl] = (dqt * e_q + dqe * e_qe) * scale
            dk_ref[:, sl] = dkt * e_k + dkd * e_kd
            dkd_kd = dkd * kd
            db = dqt * qt - dkt * kt + dqe * qe - dkd_kd
            db = jnp.where(last_row, db + (_colsum(dkd_kd) + debl * ebl), db)
            dla_ref[:, sl] = _chunk_cumsum(db, False)
            dst[p] = ds_new
        _acc_add(dwgn_ref, first, dwgn)

    rev = lambda c: pl.BlockSpec((BLK, c), lambda i: (nc - 1 - i, 0))
    f32 = lambda c: jax.ShapeDtypeStruct((rows, c), F32)
    return pl.pallas_call(
        body, name="gla_bwd", grid=(nc,),
        in_specs=[rev(512), rev(512), rev(256), rev(256), rev(512), rev(512), rev(256),
                  pl.BlockSpec((1, 2, 128, 128), lambda i: (nc - 1 - i, 0, 0, 0)), VMEM_SPEC],
        out_specs=[rev(256), rev(256), rev(512), rev(512), rev(256), _acc_spec(128)],
        out_shape=[f32(256), f32(256), f32(512), f32(512), f32(256), jax.ShapeDtypeStruct((8, 128), F32)],
        scratch_shapes=[pltpu.VMEM((2, 128, 128), F32)],
        compiler_params=_params(("arbitrary",)),
    )(dcat, o_all, gq, gk, gv, gg, bc, sp, wgn)


def _swa_masks(i):
    t = lax.broadcasted_iota(jnp.int32, (BLK, BLK), 0)
    c = lax.broadcasted_iota(jnp.int32, (BLK, BLK), 1)
    own_side = c <= t
    band_ok = i >= jnp.where(own_side, 1, 2)
    meta_ok = (c % N_META) <= jnp.where(i >= 1, N_META, t - PAD_ROWS)
    return own_side, band_ok, meta_ok, c // N_META


def _swa_blocks(ref, i):
    prev = pl.multiple_of(jnp.maximum(i - 1, 0) * BLK, BLK)
    own = pl.multiple_of(i * BLK, BLK)
    return jnp.concatenate([ref[pl.ds(prev, BLK), :], ref[pl.ds(own, BLK), :]], axis=0), prev, own


def _swa_meta_operand(ref):
    blk = ref[0:BLK, :]
    swapped = pltpu.roll(blk, 64, 1)
    lo = jnp.where(_half_mask(128, 0), blk, swapped)
    hi = jnp.where(_half_mask(128, 1), blk, swapped)
    meta = jnp.concatenate([lo, lo, hi, hi], axis=1)[PAD_ROWS:BLK, :]
    tiled = jnp.concatenate([meta] * SWA_HEADS, axis=0)
    j = lax.broadcasted_iota(jnp.int32, tiled.shape, 0)
    lane = lax.broadcasted_iota(jnp.int32, tiled.shape, 1)
    return jnp.where(j // N_META == lane // SWA_HD, tiled, jnp.zeros_like(tiled))


def _swa_meta_fold(acc):
    out = jnp.zeros((N_META, 128), F32)
    for hd in range(SWA_HEADS):
        half, kv = hd % 2, hd // 4
        piece = acc[N_META * hd:N_META * (hd + 1), 128 * (hd // 2):128 * (hd // 2) + 128]
        piece = jnp.where(_half_mask(128, half), piece, 0.0)
        out = out + (piece if half == kv else pltpu.roll(piece, 64, 1))
    return out


def _by_head(group, per_head):
    out = jnp.zeros((BLK, BLK), F32)
    for hd, v in enumerate(per_head):
        out = jnp.where(group == hd, v, out)
    return out


def _place(x, kv):
    if kv == 0:
        lo = jnp.where(_half_mask(128, 0), x, jnp.zeros_like(x))
        return lo, pltpu.roll(lo, 64, 1)
    hi = jnp.where(_half_mask(128, 1), x, jnp.zeros_like(x))
    return pltpu.roll(hi, 64, 1), hi


def _swa_fwd(sq, sk, sv, sinks, wn):
    rows = sq.shape[0]
    nb = rows // BLK
    scale = SWA_HD ** -0.5

    def body(q_ref, k_ref, v_ref, sink_ref, wn_ref, o_ref, cat_ref, lse_ref, kp, vp):
        i = pl.program_id(0)

        @pl.when(i == 0)
        def _():
            kp[...] = _swa_meta_operand(k_ref)
            vp[...] = _swa_meta_operand(v_ref)
        own_side, band_ok, meta_ok, group = _swa_masks(i)
        k2, _, _ = _swa_blocks(k_ref, i)
        v2, _, _ = _swa_blocks(v_ref, i)
        kz = (_place(k2, 0), _place(k2, 1))
        vz = (_place(v2, 0), _place(v2, 1))
        q_all = q_ref[...]
        s_meta = jnp.where(meta_ok, _nt(q_all, kp[...]) * scale, NEG_INF)
        s_band, m = [], []
        for hd in range(SWA_HEADS):
            kv, half = hd // 4, hd % 2
            q_pair = q_all[:, 128 * (hd // 2):128 * (hd // 2) + 128]
            s2 = _nt(q_pair, kz[kv][half])
            s = jnp.where(band_ok, jnp.where(own_side, s2[:, BLK:], s2[:, :BLK]) * scale, NEG_INF)
            top = jnp.maximum(jnp.max(s, axis=-1, keepdims=True),
                              jnp.max(jnp.where(group == hd, s_meta, NEG_INF), axis=-1, keepdims=True))
            s_band.append(s)
            m.append(jnp.maximum(top, sink_ref[0, hd]))
        e_meta = jnp.exp(s_meta - _by_head(group, m))
        o_meta = _nn(e_meta.astype(BF16), vp[...])
        outs = []
        for pr in range(4):
            o_pair = o_meta[:, 128 * pr:128 * pr + 128]
            rden = []
            for half in range(2):
                hd = 2 * pr + half
                kv = hd // 4
                e = jnp.exp(s_band[hd] - m[hd])
                den = (jnp.sum(e, axis=-1, keepdims=True)
                       + jnp.sum(jnp.where(group == hd, e_meta, 0.0), axis=-1, keepdims=True)
                       + jnp.exp(sink_ref[0, hd] - m[hd]))
                lse_ref[:, hd:hd + 1] = m[hd] + jnp.log(den)
                rden.append(1.0 / den)
                e2 = jnp.concatenate([jnp.where(own_side, 0.0, e), jnp.where(own_side, e, 0.0)], axis=1).astype(BF16)
                o_pair = o_pair + _nn(e2, vz[kv][half])
            outs.append(o_pair * jnp.where(_half_mask(128, 0), rden[0], rden[1]))
        o = jnp.concatenate(outs, axis=1)
        o_ref[...] = o
        on, _ = _rms(o)
        cat_ref[...] = (on * wn_ref[...]).astype(BF16)

    return pl.pallas_call(
        body, name="swa_fwd", grid=(nb,),
        in_specs=[_row_spec(BLK, 512), VMEM_SPEC, VMEM_SPEC, SMEM_SPEC, VMEM_SPEC],
        out_specs=[_row_spec(BLK, 512), _row_spec(BLK, 512), _row_spec(BLK, SWA_HEADS)],
        out_shape=[jax.ShapeDtypeStruct((rows, 512), F32), jax.ShapeDtypeStruct((rows, 512), BF16),
                   jax.ShapeDtypeStruct((rows, SWA_HEADS), F32)],
        scratch_shapes=[pltpu.VMEM((BLK, 512), BF16), pltpu.VMEM((BLK, 512), BF16)],
        compiler_params=_params(("arbitrary",)),
    )(sq, sk, sv, sinks, wn)


def _swa_bwd(dcat, o_all, sq, sk, sv, lse, sinks, wn):
    rows = sq.shape[0]
    nb = rows // BLK
    scale = SWA_HD ** -0.5

    def body(dc_ref, o_ref, q_ref, k_ref, v_ref, lse_ref, sink_ref, wn_ref, dq_ref, dk_ref, dv_ref, dsink_ref, dwn_ref,
             kp, vp, dkp, dvp):
        i = pl.program_id(0)
        first = i == 0

        @pl.when(first)
        def _():
            dk_ref[...] = jnp.zeros_like(dk_ref)
            dv_ref[...] = jnp.zeros_like(dv_ref)
            dkp[...] = jnp.zeros_like(dkp)
            dvp[...] = jnp.zeros_like(dvp)
            kp[...] = _swa_meta_operand(k_ref)
            vp[...] = _swa_meta_operand(v_ref)
        own_side, band_ok, meta_ok, group = _swa_masks(i)
        k2, prev, own = _swa_blocks(k_ref, i)
        v2, _, _ = _swa_blocks(v_ref, i)
        kz = (_place(k2, 0), _place(k2, 1))
        vz = (_place(v2, 0), _place(v2, 1))
        o = o_ref[...]
        on, ro = _rms(o)
        dc = dc_ref[...]
        _acc_add(dwn_ref, first, _colsum(dc * on))
        do = _rms_bwd(on, ro, wn_ref[...], dc)
        do_o = do * o
        do16 = do.astype(BF16)
        q_all = q_ref[...]
        lse = [lse_ref[:, hd:hd + 1] for hd in range(SWA_HEADS)]
        delta = [jnp.sum(jnp.where(_half_mask(128, hd % 2), do_o[:, 128 * (hd // 2):128 * (hd // 2) + 128], 0.0),
                         axis=-1, keepdims=True) for hd in range(SWA_HEADS)]
        s_meta = jnp.where(meta_ok, _nt(q_all, kp[...]) * scale, NEG_INF)
        p_meta = jnp.exp(s_meta - _by_head(group, lse))
        ds_meta16 = (p_meta * (_nt(do16, vp[...]) - _by_head(group, delta)) * scale).astype(BF16)
        dq_meta = _nn(ds_meta16, kp[...])
        dkp[...] += _tn(ds_meta16, q_all)
        dvp[...] += _tn(p_meta.astype(BF16), do16)
        lane8 = lax.broadcasted_iota(jnp.int32, (1, 128), 1)
        dsink = jnp.zeros((1, 128), F32)
        dk2 = [[jnp.zeros((2 * BLK, 128), F32) for _ in range(2)] for _ in range(2)]
        dv2 = [[jnp.zeros((2 * BLK, 128), F32) for _ in range(2)] for _ in range(2)]
        dqs = []
        for pr in range(4):
            ps = slice(128 * pr, 128 * pr + 128)
            q_pair = q_all[:, ps]
            do_pair = do16[:, ps]
            dq_pair = dq_meta[:, ps]
            for half in range(2):
                hd = 2 * pr + half
                kv = hd // 4

                def window(x2):
                    return jnp.where(own_side, x2[:, BLK:], x2[:, :BLK])

                def unwindow(x):
                    return jnp.concatenate([jnp.where(own_side, 0.0, x), jnp.where(own_side, x, 0.0)], axis=1).astype(BF16)
                s = jnp.where(band_ok, window(_nt(q_pair, kz[kv][half])) * scale, NEG_INF)
                prob = jnp.exp(s - lse[hd])
                dsink = dsink + jnp.where(lane8 == hd, -jnp.sum(jnp.exp(sink_ref[0, hd] - lse[hd]) * delta[hd]), 0.0)
                ds2 = unwindow(prob * (window(_nt(do_pair, vz[kv][half])) - delta[hd]) * scale)
                dq_pair = dq_pair + _nn(ds2, kz[kv][half])
                dk2[kv][half] = dk2[kv][half] + _tn(ds2, q_pair)
                dv2[kv][half] = dv2[kv][half] + _tn(unwindow(prob), do_pair)
            dqs.append(dq_pair)
        dq_ref[...] = jnp.concatenate(dqs, axis=1)
        _acc_add(dsink_ref, first, dsink)
        for ref, acc2 in ((dk_ref, dk2), (dv_ref, dv2)):
            tot = jnp.zeros((2 * BLK, 128), F32)
            for kv in range(2):
                for half in range(2):
                    part = jnp.where(_half_mask(128, half), acc2[kv][half], 0.0)
                    tot = tot + (part if half == kv else pltpu.roll(part, 64, 1))
            ref[pl.ds(prev, BLK), :] += tot[:BLK]
            ref[pl.ds(own, BLK), :] += tot[BLK:]

        @pl.when(i == nb - 1)
        def _():
            dk_ref[PAD_ROWS:BLK, :] += _swa_meta_fold(dkp[...])
            dv_ref[PAD_ROWS:BLK, :] += _swa_meta_fold(dvp[...])

    full = pl.BlockSpec((rows, 128), lambda i: (0, 0))
    return pl.pallas_call(
        body, name="swa_bwd", grid=(nb,),
        in_specs=[_row_spec(BLK, 512), _row_spec(BLK, 512), _row_spec(BLK, 512), VMEM_SPEC, VMEM_SPEC,
                  _row_spec(BLK, SWA_HEADS), SMEM_SPEC, VMEM_SPEC],
        out_specs=[_row_spec(BLK, 512), full, full, _acc_spec(128), _acc_spec(512)],
        out_shape=[jax.ShapeDtypeStruct((rows, 512), F32), jax.ShapeDtypeStruct((rows, 128), F32),
                   jax.ShapeDtypeStruct((rows, 128), F32), jax.ShapeDtypeStruct((8, 128), F32),
                   jax.ShapeDtypeStruct((8, 512), F32)],
        scratch_shapes=[pltpu.VMEM((BLK, 512), BF16), pltpu.VMEM((BLK, 512), BF16),
                        pltpu.VMEM((BLK, 512), F32), pltpu.VMEM((BLK, 512), F32)],
        compiler_params=_params(("arbitrary",)),
    )(dcat, o_all, sq, sk, sv, lse, sinks, wn)


def _mix_out(h, cat_g, cat_s, wout, gpost):
    rows = h.shape[0]
    tm = _row_tile(rows)

    def body(h_ref, cg_ref, cs_ref, w_ref, g_ref, ho_ref, m_ref):
        m = _nn(cg_ref[...], w_ref[0:512, :]) + _nn(cs_ref[...], w_ref[512:1024, :])
        m_ref[...] = m
        mn, _ = _rms(m)
        ho_ref[...] = h_ref[...] + mn * g_ref[...]

    row_f32 = _row_spec(tm, D_MODEL)
    return pl.pallas_call(
        body, name="mix_out", grid=(rows // tm,),
        in_specs=[row_f32, _row_spec(tm, 512), _row_spec(tm, 512), VMEM_SPEC, VMEM_SPEC],
        out_specs=[row_f32, row_f32],
        out_shape=[jax.ShapeDtypeStruct((rows, D_MODEL), F32), jax.ShapeDtypeStruct((rows, D_MODEL), F32)],
        compiler_params=_params(("arbitrary",)),
    )(h, cat_g, cat_s, wout, gpost)


def _mix_out_bwd(dh, m, wout, gpost):
    rows = dh.shape[0]
    tm = _row_tile(rows)

    def body(dh_ref, m_ref, w_ref, g_ref, dcg_ref, dcs_ref, dm_ref, dg_ref):
        first = pl.program_id(0) == 0
        dhv = dh_ref[...]
        mn, rm = _rms(m_ref[...])
        _acc_add(dg_ref, first, _colsum(dhv * mn))
        dm16 = _rms_bwd(mn, rm, g_ref[...], dhv).astype(BF16)
        dm_ref[...] = dm16
        dcat = _nt(dm16, w_ref[...])
        dcg_ref[...] = dcat[:, 0:512]
        dcs_ref[...] = dcat[:, 512:1024]

    row_f32 = _row_spec(tm, D_MODEL)
    return pl.pallas_call(
        body, name="mix_out_bwd", grid=(rows // tm,),
        in_specs=[row_f32, row_f32, VMEM_SPEC, VMEM_SPEC],
        out_specs=[_row_spec(tm, 512), _row_spec(tm, 512), row_f32, _acc_spec(D_MODEL)],
        out_shape=[jax.ShapeDtypeStruct((rows, 512), F32), jax.ShapeDtypeStruct((rows, 512), F32),
                   jax.ShapeDtypeStruct((rows, D_MODEL), BF16), jax.ShapeDtypeStruct((8, D_MODEL), F32)],
        compiler_params=_params(("arbitrary",)),
    )(dh, m, wout, gpost)


def _mix_in_bwd(dh_out, h, g, win_p, wa2_p, cos, sin, loga, ga, dgq, dgk, dgv, dgg, dsq, dsk, dsv, dloga):
    rows = h.shape[0]
    tm = _row_tile(rows)

    def body(dho_ref, h_ref, g_ref, win_ref, wa2_ref, cos_ref, sin_ref, loga_ref, ga_ref,
             dgq_ref, dgk_ref, dgv_ref, dgg_ref, dsq_ref, dsk_ref, dsv_ref, dla_ref,
             dh_ref, dproj_ref, dwa2_ref, dg_ref, dba_ref):
        first = pl.program_id(0) == 0
        dz = dla_ref[...] * (1.0 / GLA_TAU) * (1.0 - jnp.exp(GLA_TAU * loga_ref[...]))
        _acc_add(dba_ref, first, _colsum(dz))
        dga = _nt(dz, wa2_ref[...])
        pa = _tn(ga_ref[...], dz)
        c1, s1 = cos_ref[...], sin_ref[...]
        c4 = jnp.concatenate([c1, c1, c1, c1], axis=1)
        s4 = jnp.concatenate([s1, s1, s1, s1], axis=1)
        dq_r, dk_r = dsq_ref[...], dsk_ref[...]
        dsq = dq_r * c4 - _rot_half(dq_r * s4)
        dsk = dk_r * c1 - _rot_half(dk_r * s1)
        dproj16 = jnp.concatenate(
            [dgq_ref[...], dgk_ref[...], dgv_ref[...], dgg_ref[...], dsq, dsk, dsv_ref[...], dga], axis=1).astype(BF16)
        dproj_ref[...] = dproj16
        dn = _nn(dproj16, win_ref[...])

        @pl.when(first)
        def _():
            dwa2_ref[...] = pa

        @pl.when(jnp.logical_not(first))
        def _():
            dwa2_ref[...] += pa
        hn, rh = _rms(h_ref[...])
        _acc_add(dg_ref, first, _colsum(dn * hn))
        dh_ref[...] = dho_ref[...] + _rms_bwd(hn, rh, g_ref[...], dn)

    rs = lambda c: _row_spec(tm, c)
    return pl.pallas_call(
        body, name="mix_in_bwd", grid=(rows // tm,),
        in_specs=[rs(D_MODEL), rs(D_MODEL), VMEM_SPEC, VMEM_SPEC, VMEM_SPEC, rs(128), rs(128), rs(256), rs(128),
                  rs(256), rs(256), rs(512), rs(512), rs(512), rs(128), rs(128), rs(256)],
        out_specs=[rs(D_MODEL), rs(P_END), pl.BlockSpec((128, 256), lambda i: (0, 0)), _acc_spec(D_MODEL), _acc_spec(256)],
        out_shape=[jax.ShapeDtypeStruct((rows, D_MODEL), F32), jax.ShapeDtypeStruct((rows, P_END), BF16),
                   jax.ShapeDtypeStruct((128, 256), F32), jax.ShapeDtypeStruct((8, D_MODEL), F32),
                   jax.ShapeDtypeStruct((8, 256), F32)],
        compiler_params=_params(("arbitrary",)),
    )(dh_out, h, g, win_p, wa2_p, cos, sin, loga, ga, dgq, dgk, dgv, dgg, dsq, dsk, dsv, dloga)


def _rope_tables(rows):
    pos = (jnp.arange(rows, dtype=jnp.int32) - PAD_ROWS).astype(F32)
    inv_freq = 1.0 / (ROPE_THETA ** (jnp.arange(0, SWA_HD, 2, dtype=F32) / SWA_HD))
    ang = pos[:, None] * inv_freq[None, :]
    ang = jnp.concatenate([ang, ang, ang, ang], axis=-1)
    return jnp.cos(ang), jnp.sin(ang)


def _local_step(h0, tgt, w, late_weights=None, on_grads=None):
    rows = h0.shape[0]
    cos, sin = _rope_tables(rows)
    g = {}

    def tell(group, names):
        for nm in names:
            g[nm] = grads_now[nm]
        return 0.0 if on_grads is None else on_grads(group, {nm: grads_now[nm] for nm in names})

    h1, a1, b1, f1 = _ffn_fwd(h0, w["ffn1_pre"], w["wg1"], w["wu1"], w["wd1"], w["ffn1_post"])
    if late_weights is not None:
        w = {**w, **late_weights("win", f1)}
    gq, gk, gv, gg, sq, sk, sv, ga, loga, bc, n2 = _mix_in(h1, w["mix_pre"], w["win"], w["wa2"], w["b_a"], cos, sin)
    o_g, cat_g, sp = _gla_fwd(gq, gk, gv, gg, bc, w["gla_norm"])
    o_s, cat_s, lse = _swa_fwd(sq, sk, sv, w["sinks"], w["swa_norm"])
    if late_weights is not None:
        w = {**w, **late_weights("rest", lse)}
    h2, m = _mix_out(h1, cat_g, cat_s, w["wout"], w["mix_post"])
    h3, a2, b2, f2, dy, loss = _ffn_fwd(h2, w["ffn2_pre"], w["wg2"], w["wu2"], w["wd2"], w["ffn2_post"], tgt)
    del h3
    dh2, da, db, df, n3, g["ffn2_pre"], g["ffn2_post"] = _ffn_bwd_act(
        dy, h2, a2, b2, f2, w["ffn2_pre"], w["ffn2_post"], w["wg2"], w["wu2"], w["wd2"], "ffn2_bwd_act")
    grads_now = dict(wd2=_wgrad(b2, df, "ffn2_wgrad_down", gate=a2), wg2=_wgrad(da, n3, "ffn2_wgrad_gate"),
                     wu2=_wgrad(db, n3, "ffn2_wgrad_up"))
    tok = tell("ffn2", ("wd2", "wg2", "wu2"))
    dcg, dcs, dm, g["mix_post"] = _mix_out_bwd(dh2, m, w["wout"], w["mix_post"] + tok)
    dsq, dsk, dsv, g["sinks"], g["swa_norm"] = _swa_bwd(dcs, o_s, sq, sk, sv, lse, w["sinks"], w["swa_norm"])
    dgq, dgk, dgv, dgg, dloga, g["gla_norm"] = _gla_bwd(dcg, o_g, gq, gk, gv, gg, bc, sp, w["gla_norm"])
    dh1, dproj, g["wa2"], g["mix_pre"], g["b_a"] = _mix_in_bwd(
        dh2, h1, w["mix_pre"], w["win"], w["wa2"], cos, sin, loga, ga, dgq, dgk, dgv, dgg, dsq, dsk, dsv, dloga)
    grads_now = dict(wout=jnp.concatenate([_wgrad(cat_g, dm, "wout_wgrad_gla"), _wgrad(cat_s, dm, "wout_wgrad_swa")], axis=0),
                     win=_wgrad(dproj, n2, "win_wgrad"))
    tok = tell("mix", ("wout", "win"))
    dh0, da, db, df, n1, g["ffn1_pre"], g["ffn1_post"] = _ffn_bwd_act(
        dh1, h0, a1, b1, f1, w["ffn1_pre"] + tok, w["ffn1_post"], w["wg1"], w["wu1"], w["wd1"], "ffn1_bwd_act")
    grads_now = dict(wd1=_wgrad(b1, df, "ffn1_wgrad_down", gate=a1))
    tell("ffn1_down", ("wd1",))
    grads_now = dict(wg1=_wgrad(da, n1, "ffn1_wgrad_gate"))
    tell("ffn1_gate", ("wg1",))
    grads_now = dict(wu1=_wgrad(db, n1, "ffn1_wgrad_up"))
    tell("ffn1_up", ("wu1",))
    return loss[0, 0], dh0, g


def _win_pad_rows(win_t):
    pad = jnp.zeros((P_END - P_GA - 16, win_t.shape[1]), win_t.dtype)
    return jnp.concatenate([win_t[0:1536], win_t[1552:2320], win_t[1536:1552], pad], axis=0)


def _win_unpad_rows(win_p):
    return jnp.concatenate([win_p[0:1536], win_p[P_GA:P_GA + 16], win_p[1536:P_GA]], axis=0)


def _place_on_mesh():
    return lax.axis_index("x"), lax.axis_index("y"), lax.axis_index("c")


def _dev_index(px, py, pc):
    return 4 * px + 2 * py + pc


def _other_devices(x, y, c):
    flip = lambda v, f: 1 - v if f else v
    return [(flip(x, fx), flip(y, fy), flip(c, fc)) for fx in (0, 1) for fy in (0, 1) for fc in (0, 1)][1:]


def _all_gather(shards):
    n = len(shards)

    def body(*refs):
        ins, outs = refs[:n], refs[n:2 * n]
        zeros_ref, send_sems, recv_sems, local_sems = refs[2 * n:]
        zeros_ref[...] = jnp.zeros_like(zeros_ref)
        x, y, c = _place_on_mesh()
        me, sibling = (x, y, c), (x, y, 1 - c)
        chips = [(1 - x, y), (x, 1 - y), (1 - x, 1 - y)]

        def rows(k, px, py, pc):
            r = ins[k].shape[0]
            return outs[k].at[pl.ds(pl.multiple_of(_dev_index(px, py, pc) * r, 8), r), :]

        def copy(k, slot, block, to, src=None):
            return pltpu.make_async_remote_copy(
                src_ref=rows(k, *block) if src is None else src, dst_ref=rows(k, *block),
                send_sem=send_sems.at[k, slot], recv_sem=recv_sems.at[k, slot], device_id=to, device_id_type=MESH)

        local = [pltpu.make_async_copy(ins[k], rows(k, *me), local_sems.at[k]) for k in range(n)]
        sends = []
        for k in range(n):
            local[k].start()
            sends.append(copy(k, 0, me, sibling, src=ins[k]))
            sends += [copy(k, 1 + j, me, (*chip, c), src=ins[k]) for j, chip in enumerate(chips)]
        for cp in sends:
            cp.start()
        for k in range(n):
            for j, chip in enumerate(chips):
                copy(k, 1 + j, (*chip, c), me).wait_recv()
                passed = copy(k, 4 + j, (*chip, c), sibling)
                passed.start()
                sends.append(passed)
        for k in range(n):
            copy(k, 0, sibling, me).wait_recv()
            for j, chip in enumerate(chips):
                copy(k, 4 + j, (*chip, 1 - c), me).wait_recv()
        for cp in sends:
            cp.wait_send()
        for cp in local:
            cp.wait()

    return pl.pallas_call(
        body, name="all_gather_weights",
        in_specs=[ANY_SPEC] * n, out_specs=[ANY_SPEC] * n + [VMEM_SPEC],
        out_shape=[jax.ShapeDtypeStruct((N_DEV * s.shape[0], s.shape[1]), s.dtype) for s in shards]
        + [jax.ShapeDtypeStruct((8, 128), F32)],
        scratch_shapes=[pltpu.SemaphoreType.DMA((n, 7)), pltpu.SemaphoreType.DMA((n, 7)), pltpu.SemaphoreType.DMA((n,))],
    )(*shards)


HBM_SPEC = pl.BlockSpec(memory_space=pltpu.HBM)
SEM_SPEC = pl.BlockSpec(memory_space=pltpu.SEMAPHORE)
DATAFLOW = pltpu.SideEffectType.DATAFLOW_SIDE_EFFECTING


GATHER, SCATTER, SCATTER_CHIPS = "gather", "scatter", "scatter among chips"


def _exchange_peers(kind):
    x, y, c = _place_on_mesh()
    if kind == SCATTER_CHIPS:
        peers = [(1 - x, y, c), (x, 1 - y, c), (1 - x, 1 - y, c)]
        return peers, [2 * p[0] + p[1] for p in peers], 2 * x + y, 4
    peers = _other_devices(x, y, c)
    return peers, [_dev_index(*p) for p in peers], _dev_index(x, y, c), N_DEV


def _exchange_copies(srcs, lands, send_sems, recv_sems, own_sems, kind, arriving):
    peers, theirs, me, blocks = _exchange_peers(kind)
    remote, local = [], []
    for k, (src, land) in enumerate(zip(srcs, lands)):
        r = land.shape[0] // blocks

        def block(ref, d):
            return ref.at[pl.ds(pl.multiple_of(d * r, 8), r), :]

        for f, (peer, him) in enumerate(zip(peers, theirs)):
            mine, his = (him, me) if arriving else (me, him)
            sem = len(peers) * k + f
            remote.append(pltpu.make_async_remote_copy(
                src_ref=src if kind == GATHER else block(src, his), dst_ref=block(land, mine),
                send_sem=send_sems.at[sem], recv_sem=recv_sems.at[sem], device_id=peer, device_id_type=MESH))
        local.append(pltpu.make_async_copy(src if kind == GATHER else block(src, me), block(land, me), own_sems.at[k]))
    return remote, local


def _exchange_start(srcs, kind, name):
    n = len(srcs)
    lands = [lax.empty((N_DEV * s.shape[0], s.shape[1]) if kind == GATHER else s.shape, s.dtype) for s in srcs]
    sems = (3 if kind == SCATTER_CHIPS else 7) * n

    def body(*refs):
        remote, local = _exchange_copies(refs[:n], refs[n:2 * n], *refs[2 * n:2 * n + 3], kind, False)
        for cp in remote + local:
            cp.start()
        refs[-1][...] = jnp.zeros_like(refs[-1])

    both = list(srcs) + list(lands)
    outs = pl.pallas_call(
        body, name=name,
        out_shape=(pltpu.SemaphoreType.DMA((sems,)), pltpu.SemaphoreType.DMA((sems,)), pltpu.SemaphoreType.DMA((n,)),
                   *[pltpu.HBM(a.shape, a.dtype) for a in both], jax.ShapeDtypeStruct((8, 128), F32)),
        in_specs=[HBM_SPEC] * (2 * n), out_specs=(SEM_SPEC, SEM_SPEC, SEM_SPEC, *[HBM_SPEC] * (2 * n), VMEM_SPEC),
        input_output_aliases={i: 3 + i for i in range(2 * n)},
        compiler_params=pltpu.CompilerParams(has_side_effects=DATAFLOW),
    )(*[pltpu.with_memory_space_constraint(a, pltpu.HBM) for a in both])
    return outs[0:3], outs[3:3 + n], outs[3 + n:3 + 2 * n], outs[-1]


def _exchange_wait(started, kind, after, name):
    sems, srcs, lands, _ = started
    n = len(srcs)

    def body(*refs):
        args = (refs[:n], refs[n:2 * n], *refs[2 * n:2 * n + 3], kind)
        going, local = _exchange_copies(*args, False)
        for cp in going:
            cp.wait_send()
        for cp in local:
            cp.wait()
        for cp in _exchange_copies(*args, True)[0]:
            cp.wait_recv()

    both = list(srcs) + list(lands)
    outs = pl.pallas_call(
        body, name=name, out_shape=[pltpu.HBM(a.shape, a.dtype) for a in both],
        in_specs=[HBM_SPEC] * (2 * n) + [SEM_SPEC, SEM_SPEC, SEM_SPEC, ANY_SPEC], out_specs=[HBM_SPEC] * (2 * n),
        input_output_aliases={i: i for i in range(2 * n)},
        compiler_params=pltpu.CompilerParams(has_side_effects=DATAFLOW),
    )(*both, *sems, after)
    return outs[n:]


def _sibling_reduce(part, name):
    r, cols = part.shape[0] // N_DEV, part.shape[1]

    def swap(p_ref, got_ref, send_sems, recv_sems):
        x, y, c = _place_on_mesh()
        copies = [pltpu.make_async_remote_copy(
            src_ref=p_ref.at[pl.ds(pl.multiple_of((2 * j + 1 - c) * r, 8), r), :], dst_ref=got_ref.at[pl.ds(j * r, r), :],
            send_sem=send_sems.at[j], recv_sem=recv_sems.at[j], device_id=(x, y, 1 - c), device_id_type=MESH)
            for j in range(4)]
        for cp in copies:
            cp.start()
        for cp in copies:
            cp.wait()

    got = pl.pallas_call(
        swap, name=name + "_swap", in_specs=[ANY_SPEC], out_specs=ANY_SPEC,
        out_shape=jax.ShapeDtypeStruct((4 * r, cols), part.dtype),
        scratch_shapes=[pltpu.SemaphoreType.DMA((4,)), pltpu.SemaphoreType.DMA((4,))],
    )(part)

    def add(c_ref, mine_ref, got_ref, o_ref):
        del c_ref
        o_ref[...] = (mine_ref[...].astype(F32) + got_ref[...].astype(F32)).astype(o_ref.dtype)

    core = lax.axis_index("c").astype(jnp.int32).reshape(1)
    return pl.pallas_call(
        add, name=name + "_add",
        grid_spec=pltpu.PrefetchScalarGridSpec(
            num_scalar_prefetch=1, grid=(4,),
            in_specs=[pl.BlockSpec((r, cols), lambda j, c_ref: (2 * j + c_ref[0], 0)),
                      pl.BlockSpec((r, cols), lambda j, c_ref: (j, 0))],
            out_specs=pl.BlockSpec((r, cols), lambda j, c_ref: (j, 0))),
        out_shape=jax.ShapeDtypeStruct((4 * r, cols), part.dtype),
        compiler_params=_params(("arbitrary",)),
    )(core, part, got)


def _sum_partials(parts, name, blocks=N_DEV):
    n = len(parts)

    def body(*refs):
        ins, outs = refs[:n], refs[n:]
        first = pl.program_id(0) == 0
        for i_ref, o_ref in zip(ins, outs):
            v = i_ref[...].astype(F32)

            @pl.when(first)
            def _():
                o_ref[...] = v

            @pl.when(jnp.logical_not(first))
            def _():
                o_ref[...] += v

    shapes = [(p.shape[0] // blocks, p.shape[1]) for p in parts]
    return pl.pallas_call(
        body, name=name, grid=(blocks,),
        in_specs=[pl.BlockSpec(s, lambda j: (j, 0)) for s in shapes],
        out_specs=[pl.BlockSpec(s, lambda j: (0, 0)) for s in shapes],
        out_shape=[jax.ShapeDtypeStruct(s, F32) for s in shapes],
        compiler_params=_params(("arbitrary",)),
    )(*parts)


def _all_reduce_small(slab):
    rows, cols = slab.shape

    def body(x_ref, o_ref, gathered, send_sems, recv_sems):
        x, y, c = _place_on_mesh()
        me = _dev_index(x, y, c)
        peers = _other_devices(x, y, c)

        def copy(f, peer):
            return pltpu.make_async_remote_copy(
                src_ref=x_ref, dst_ref=gathered.at[me], send_sem=send_sems.at[f], recv_sem=recv_sems.at[f],
                device_id=peer, device_id_type=MESH)

        def arrival(f, peer):
            return pltpu.make_async_remote_copy(
                src_ref=x_ref, dst_ref=gathered.at[_dev_index(*peer)], send_sem=send_sems.at[f], recv_sem=recv_sems.at[f],
                device_id=peer, device_id_type=MESH)

        sends = [copy(f, peer) for f, peer in enumerate(peers)]
        for cp in sends:
            cp.start()
        gathered[me] = x_ref[...]
        for f, peer in enumerate(peers):
            arrival(f, peer).wait_recv()
        for cp in sends:
            cp.wait_send()
        total = gathered[0]
        for d in range(1, N_DEV):
            total = total + gathered[d]
        o_ref[...] = total

    return pl.pallas_call(
        body, name="all_reduce_small",
        in_specs=[VMEM_SPEC], out_specs=VMEM_SPEC, out_shape=jax.ShapeDtypeStruct((rows, cols), F32),
        scratch_shapes=[pltpu.VMEM((N_DEV, rows, cols), F32), pltpu.SemaphoreType.DMA((7,)), pltpu.SemaphoreType.DMA((7,))],
    )(slab)


def _adamw(ws, gs, ms, vs, name):
    n = len(ws)
    c1 = 1.0 / (1.0 - ADAM_B1 ** ADAM_STEP)
    c2 = 1.0 / (1.0 - ADAM_B2 ** ADAM_STEP)

    def body(*refs):
        w_r, g_r, m_r, v_r = refs[:n], refs[n:2 * n], refs[2 * n:3 * n], refs[3 * n:4 * n]
        d_o, m_o, v_o = refs[4 * n:5 * n], refs[5 * n:6 * n], refs[6 * n:7 * n]
        for k in range(n):
            g = g_r[k][...]
            m = ADAM_B1 * m_r[k][...] + (1.0 - ADAM_B1) * g
            v = ADAM_B2 * v_r[k][...] + (1.0 - ADAM_B2) * (g * g)
            m_o[k][...] = m
            v_o[k][...] = v
            d_o[k][...] = -ADAM_LR * ((m * c1) / (jnp.sqrt(v * c2) + ADAM_EPS) + ADAM_WD * w_r[k][...])

    shapes = [jax.ShapeDtypeStruct(w.shape, F32) for w in ws]
    outs = pl.pallas_call(
        body, name=name, in_specs=[VMEM_SPEC] * (4 * n), out_specs=[VMEM_SPEC] * (3 * n), out_shape=shapes * 3,
        compiler_params=pltpu.CompilerParams(vmem_limit_bytes=56 << 20),
    )(*ws, *gs, *ms, *vs)
    return outs[:n], outs[n:2 * n], outs[2 * n:]


WEIGHT_NAMES = ("meta_tokens", "ffn1_pre_norm", "ffn1_w_gate", "ffn1_w_up", "ffn1_w_down", "ffn1_post_norm", "mix_pre_norm",
                "w_in", "gla_w_a2", "gla_b_a", "gla_out_norm", "swa_sinks", "swa_out_norm", "w_out", "mix_post_norm",
                "ffn2_pre_norm", "ffn2_w_gate", "ffn2_w_up", "ffn2_w_down", "ffn2_post_norm")
WIN_SHARD = D_IN // N_DEV
WIN_SHARD_PAD = 304
SLAB_VECTORS = ("ffn1_pre", "ffn1_post", "mix_pre", "mix_post", "ffn2_pre", "ffn2_post")
SLAB_ROWS = 32


def kernel(x, meta_tokens, ffn1_pre_norm, ffn1_w_gate, ffn1_w_up, ffn1_w_down, ffn1_post_norm, mix_pre_norm, w_in, gla_w_a2, gla_b_a, gla_out_norm, swa_sinks, swa_out_norm, w_out, mix_post_norm, ffn2_pre_norm, ffn2_w_gate, ffn2_w_up, ffn2_w_down, ffn2_post_norm, loss_target, m_meta_tokens, m_ffn1_pre_norm, m_ffn1_w_gate, m_ffn1_w_up, m_ffn1_w_down, m_ffn1_post_norm, m_mix_pre_norm, m_w_in, m_gla_w_a2, m_gla_b_a, m_gla_out_norm, m_swa_sinks, m_swa_out_norm, m_w_out, m_mix_post_norm, m_ffn2_pre_norm, m_ffn2_w_gate, m_ffn2_w_up, m_ffn2_w_down, m_ffn2_post_norm, v_meta_tokens, v_ffn1_pre_norm, v_ffn1_w_gate, v_ffn1_w_up, v_ffn1_w_down, v_ffn1_post_norm, v_mix_pre_norm, v_w_in, v_gla_w_a2, v_gla_b_a, v_gla_out_norm, v_swa_sinks, v_swa_out_norm, v_w_out, v_mix_post_norm, v_ffn2_pre_norm, v_ffn2_w_gate, v_ffn2_w_up, v_ffn2_w_down, v_ffn2_post_norm):
    given = dict(locals())
    W = {n: given[n] for n in WEIGHT_NAMES}
    M = {n: given["m_" + n] for n in WEIGHT_NAMES}
    V = {n: given["v_" + n] for n in WEIGHT_NAMES}
    dev = _dev_index(*_place_on_mesh())

    def t16(w):
        return w[0].T.astype(BF16)

    small = jnp.concatenate([W["meta_tokens"], jnp.pad(W["gla_w_a2"][0], ((0, 0), (0, 96)))], axis=0)
    wg1, wu1, wd1, small_g, gathered_zeros = _all_gather(
        [t16(W["ffn1_w_gate"]), t16(W["ffn1_w_up"]), W["ffn1_w_down"][0].astype(BF16), small])
    def after_zero(shard, zeros):
        return shard + zeros[0:1, 0:1].astype(shard.dtype)
    win_shard = jnp.pad(t16(W["w_in"]), ((0, WIN_SHARD_PAD - WIN_SHARD), (0, 0)))
    win_shard = after_zero(win_shard, gathered_zeros)
    mid = _exchange_start([win_shard], GATHER, "gather_w_in_start")
    late_shards = [after_zero(W["w_out"][0].astype(BF16), mid[3]), t16(W["ffn2_w_gate"]), t16(W["ffn2_w_up"]),
                   W["ffn2_w_down"][0].astype(BF16)]
    late = _exchange_start(late_shards, GATHER, "gather_late_weights_start")

    def late_weights(what, after):
        if what == "win":
            win_g, = _exchange_wait(mid, GATHER, after, "gather_w_in_wait")
            win_t = win_g.reshape(N_DEV, WIN_SHARD_PAD, D_MODEL)[:, :WIN_SHARD].reshape(D_IN, D_MODEL)
            return dict(win=_win_pad_rows(win_t))
        wout, wg2, wu2, wd2 = _exchange_wait(late, GATHER, after, "gather_late_weights_wait")
        return dict(wout=wout, wg2=wg2, wu2=wu2, wd2=wd2)

    small_g = small_g.reshape(N_DEV, 32, 128)
    meta_full = small_g[:, :N_META].transpose(1, 0, 2).reshape(N_META, D_MODEL)
    wa2_full = small_g[:, N_META:, :32].transpose(1, 0, 2).reshape(16, 256)
    w = dict(
        ffn1_pre=W["ffn1_pre_norm"] + late[3][0, 0], ffn1_post=W["ffn1_post_norm"], mix_pre=W["mix_pre_norm"],
        mix_post=W["mix_post_norm"], ffn2_pre=W["ffn2_pre_norm"], ffn2_post=W["ffn2_post_norm"], b_a=W["gla_b_a"],
        gla_norm=W["gla_out_norm"], sinks=W["swa_sinks"], swa_norm=W["swa_out_norm"], wg1=wg1, wu1=wu1, wd1=wd1,
        wa2=jnp.pad(wa2_full, ((0, 112), (0, 0))))

    in_flight = []

    def on_grads(group, grads):
        parts = []
        for nm, p in grads.items():
            if nm == "win":
                p = _win_unpad_rows(p).reshape(N_DEV, WIN_SHARD, D_MODEL)
                p = jnp.pad(p, ((0, 0), (0, WIN_SHARD_PAD - WIN_SHARD), (0, 0))).reshape(N_DEV * WIN_SHARD_PAD, D_MODEL)
            parts.append(p)
        kind = SCATTER_CHIPS if group.startswith("ffn1") else SCATTER
        if kind == SCATTER_CHIPS:
            parts = [_sibling_reduce(p, "pair_" + group) for p in parts]
        started = _exchange_start(parts, kind, "scatter_" + group + "_start")
        in_flight.append((group, list(grads), started, kind))
        return started[3][0, 0]

    front = jnp.zeros((PAD_ROWS, D_MODEL), F32)
    h0 = jnp.concatenate([front, meta_full, x[0]], axis=0)
    tgt = jnp.concatenate([jnp.zeros((BLK, D_MODEL), F32), loss_target[0]], axis=0)
    loss, dh0, g = _local_step(h0, tgt, w, late_weights, on_grads)
    grad_x = dh0[BLK:][None]

    packed = jnp.concatenate([g["b_a"][0:1], g["gla_norm"][0:1], g["sinks"][0:1], g["swa_norm"][0:1]], axis=1)
    slab = jnp.concatenate([g[k][0:1] for k in SLAB_VECTORS] + [packed, jnp.full((1, D_MODEL), loss, F32),
                           g["wa2"][:16].reshape(4, D_MODEL), jnp.zeros((4, D_MODEL), F32), dh0[PAD_ROWS:BLK]], axis=0)
    tot = _all_reduce_small(slab)
    loss = tot[7, 0]
    small_grads = dict(
        ffn1_pre_norm=tot[0:1], ffn1_post_norm=tot[1:2], mix_pre_norm=tot[2:3], mix_post_norm=tot[3:4],
        ffn2_pre_norm=tot[4:5], ffn2_post_norm=tot[5:6], gla_b_a=tot[6:7, 0:256], gla_out_norm=tot[6:7, 256:384],
        swa_sinks=tot[6:7, 384:392], swa_out_norm=tot[6:7, 512:1024],
        gla_w_a2=lax.dynamic_slice_in_dim(tot[8:12].reshape(16, 256), dev * 32, 32, axis=1)[None],
        meta_tokens=lax.dynamic_slice_in_dim(tot[16:32], dev * 128, 128, axis=1))

    big = dict(wg1=("ffn1_w_gate", True), wu1=("ffn1_w_up", True), wd1=("ffn1_w_down", False), win=("w_in", True),
               wout=("w_out", False), wg2=("ffn2_w_gate", True), wu2=("ffn2_w_up", True), wd2=("ffn2_w_down", False))
    grads = dict(small_grads)
    delta, new_m, new_v = {}, {}, {}
    names = [n for n in WEIGHT_NAMES if n not in [full for full, _ in big.values()]]
    two_d = lambda a: a.reshape(-1, a.shape[-1])
    d_, m_, v_ = _adamw([two_d(W[n]) for n in names], [two_d(grads[n]) for n in names],
                        [two_d(M[n]) for n in names], [two_d(V[n]) for n in names], "adamw_small")
    for k, n in enumerate(names):
        delta[n], new_m[n], new_v[n] = d_[k].reshape(W[n].shape), m_[k].reshape(W[n].shape), v_[k].reshape(W[n].shape)

    before_wait = d_[0] + in_flight[-1][2][3][0, 0]
    for group, shorts, started, kind in in_flight:
        lands = _exchange_wait(started, kind, before_wait, "scatter_" + group + "_wait")
        for short, g_slab in zip(shorts, _sum_partials(lands, "sum_" + group, 4 if kind == SCATTER_CHIPS else N_DEV)):
            n, transposed = big[short]
            to_slab = (lambda a: a[0].T) if transposed else (lambda a: a[0])
            from_slab = (lambda a: a.T[None]) if transposed else (lambda a: a[None])
            g_slab = g_slab[:WIN_SHARD] if short == "win" else g_slab
            d_, m_, v_ = _adamw([to_slab(W[n])], [g_slab], [to_slab(M[n])], [to_slab(V[n])], "adamw_" + n)
            grads[n], delta[n], new_m[n], new_v[n] = from_slab(g_slab), from_slab(d_[0]), from_slab(m_[0]), from_slab(v_[0])
            before_wait = d_[0]
    return (loss, grad_x, *[grads[n] for n in WEIGHT_NAMES], *[delta[n] for n in WEIGHT_NAMES],
            *[new_m[n] for n in WEIGHT_NAMES], *[new_v[n] for n in WEIGHT_NAMES])
```

```python
import functools

import jax
import jax.numpy as jnp
from jax import lax
from jax.experimental import pallas as pl
from jax.experimental.pallas import tpu as pltpu

F32, BF16 = jnp.float32, jnp.bfloat16

D_MODEL = 1024
D_FF = 2816
N_META = 16
BLK = 128
PAD_ROWS = BLK - N_META
GLA_DK = 64
SWA_HD = 64
SWA_HEADS = 8
GLA_TAU = 16.0
NORM_EPS = 1e-6
NEG_INF = -1e30
ROPE_THETA = 10000.0
P_GQ, P_GK, P_GV, P_GG, P_SQ, P_SK, P_SV, P_GA, P_END = 0, 256, 512, 1024, 1536, 2048, 2176, 2304, 2432
D_IN = 2320
IN_SPLITS = (256, 256, 512, 512, 16, 512, 128, 128)
FF_TILE = 2816
WGRAD_TILE_MAX = 2432
N_DEV = 8
MESH = pl.DeviceIdType.MESH

ADAM_LR, ADAM_B1, ADAM_B2, ADAM_EPS, ADAM_WD, ADAM_STEP = 0.001, 0.9, 0.999, 1e-08, 0.01, 10

V7X_VMEM_BYTES = 64 << 20
VMEM_SPEC = pl.BlockSpec(memory_space=pltpu.VMEM)
SMEM_SPEC = pl.BlockSpec(memory_space=pltpu.SMEM)
ANY_SPEC = pl.BlockSpec(memory_space=pl.ANY)


def _params(semantics, vmem_mb=56):
    return pltpu.CompilerParams(dimension_semantics=semantics, vmem_limit_bytes=vmem_mb << 20)


def _row_tile(rows):
    return 320 if rows % 320 == 0 else BLK


def _nn(a, b):
    return lax.dot_general(a, b, (((1,), (0,)), ((), ())), preferred_element_type=F32)


def _nt(a, b):
    return lax.dot_general(a, b, (((1,), (1,)), ((), ())), preferred_element_type=F32)


def _tn(a, b):
    return lax.dot_general(a, b, (((0,), (0,)), ((), ())), preferred_element_type=F32)


def _rms(x):
    r = lax.rsqrt(jnp.mean(x * x, axis=-1, keepdims=True) + NORM_EPS)
    return x * r, r


def _rms_bwd(xn, r, w, dy):
    g = dy * w
    return r * (g - xn * jnp.mean(g * xn, axis=-1, keepdims=True))


def _sigmoid(x):
    return 1.0 / (1.0 + jnp.exp(-x))


def _colsum(x):
    return jnp.sum(x, axis=0, keepdims=True)


def _split_bf16(x):
    hi = x.astype(BF16)
    lo = (x - hi.astype(F32)).astype(BF16)
    return hi, lo


def _tri(lower):
    r = lax.broadcasted_iota(jnp.int32, (BLK, BLK), 0)
    c = lax.broadcasted_iota(jnp.int32, (BLK, BLK), 1)
    return (r >= c) if lower else (c >= r)


def _half_mask(width, half):
    lane = lax.broadcasted_iota(jnp.int32, (1, width), 1)
    return ((lane % 128) < 64) if half == 0 else ((lane % 128) >= 64)


def _rot_half(x):
    w = x.shape[-1]
    lane = lax.broadcasted_iota(jnp.int32, (1, w), 1)
    return jnp.where((lane % SWA_HD) < SWA_HD // 2, -pltpu.roll(x, w - SWA_HD // 2, 1), pltpu.roll(x, SWA_HD // 2, 1))


def _row_spec(tm, cols):
    return pl.BlockSpec((tm, cols), lambda i: (i, 0))


def _acc_spec(cols):
    return pl.BlockSpec((8, cols), lambda i: (0, 0))


def _acc_add(ref, first, value):
    @pl.when(first)
    def _():
        ref[...] = jnp.zeros_like(ref)
    ref[0:1, :] += value


def _behind_front(ref, i, tm, front):
    blk = ref[...]
    return jnp.where(i == 0, jnp.concatenate([front, blk[0:tm - BLK]], axis=0), blk)


def _ffn_fwd(h, gpre, wg_t, wu_t, wd, gpost, tgt=None, front=None):
    with_loss, with_front = tgt is not None, front is not None
    rows = h.shape[0] + (BLK if with_front else 0)
    tm = _row_tile(rows)
    nf = D_FF // FF_TILE

    def body(*refs):
        refs = list(refs)
        h_ref, gpre_ref, wg_ref, wu_ref, wd_ref, gpost_ref = refs[:6]
        del refs[:6]
        front_ref = refs.pop(0) if with_front else None
        t_ref = refs.pop(0) if with_loss else None
        h0_ref = refs.pop(0) if with_front else None
        ho_ref, a_ref, b_ref, f_ref = refs[:4]
        dy_ref, loss_ref = refs[4:6] if with_loss else (None, None)
        acc = refs[-1]
        i = pl.program_id(0)
        if with_front:
            h_in = _behind_front(h_ref, i, tm, front_ref[...])
            h0_ref[...] = h_in
        else:
            h_in = h_ref[...]
        hn, _ = _rms(h_in)
        n16 = (hn * gpre_ref[...]).astype(BF16)
        for j in range(nf):
            cols = slice(j * FF_TILE, (j + 1) * FF_TILE)
            a = _nt(n16, wg_ref[cols, :])
            b = _nt(n16, wu_ref[cols, :])
            a_ref[:, cols] = a.astype(BF16)
            b_ref[:, cols] = b.astype(BF16)
            s16 = (a * _sigmoid(a) * b).astype(BF16)
            part = _nn(s16, wd_ref[cols, :])
            if j == 0:
                acc[...] = part
            else:
                acc[...] += part
        f = acc[...]
        f_ref[...] = f
        fn, _ = _rms(f)
        y = h_in + 0.5 * (fn * gpost_ref[...])
        ho_ref[...] = y
        if with_loss:
            row = i * tm + lax.broadcasted_iota(jnp.int32, (tm, 1), 0)
            err = jnp.where(row >= BLK, y - _behind_front(t_ref, i, tm, jnp.zeros((BLK, D_MODEL), F32)), 0.0)
            dy_ref[...] = err * (1.0 / D_MODEL)
            part = 0.5 * jnp.sum(jnp.sum(err * err, axis=-1, keepdims=True) * (1.0 / D_MODEL), axis=0, keepdims=True)

            @pl.when(i == 0)
            def _():
                loss_ref[...] = jnp.zeros_like(loss_ref)
            loss_ref[...] += part

    row_f32 = _row_spec(tm, D_MODEL)
    behind = pl.BlockSpec((pl.Element(tm), pl.Element(D_MODEL)),
                          lambda i: (pl.multiple_of(jnp.maximum(i * tm - BLK, 0), 64), 0))
    in_specs = [behind if with_front else row_f32, VMEM_SPEC, VMEM_SPEC, VMEM_SPEC, VMEM_SPEC, VMEM_SPEC]
    out_specs = [row_f32, _row_spec(tm, D_FF), _row_spec(tm, D_FF), row_f32]
    out_shape = [jax.ShapeDtypeStruct((rows, D_MODEL), F32), jax.ShapeDtypeStruct((rows, D_FF), BF16),
                 jax.ShapeDtypeStruct((rows, D_FF), BF16), jax.ShapeDtypeStruct((rows, D_MODEL), F32)]
    args = [h, gpre, wg_t, wu_t, wd, gpost]
    if with_front:
        in_specs.append(VMEM_SPEC)
        args.append(front)
        out_specs.insert(0, row_f32)
        out_shape.insert(0, jax.ShapeDtypeStruct((rows, D_MODEL), F32))
    if with_loss:
        in_specs.append(behind)
        args.append(tgt)
        out_specs += [row_f32, pl.BlockSpec((8, 128), lambda i: (0, 0))]
        out_shape += [jax.ShapeDtypeStruct((rows, D_MODEL), F32), jax.ShapeDtypeStruct((8, 128), F32)]
    return pl.pallas_call(
        body, name="ffn_fwd_loss" if with_loss else "ffn_fwd", grid=(rows // tm,),
        in_specs=in_specs, out_specs=out_specs, out_shape=out_shape,
        scratch_shapes=[pltpu.VMEM((tm, D_MODEL), F32)],
        compiler_params=_params(("arbitrary",)),
    )(*args)


def _ffn_bwd_act(dh_out, h, a, b, f, gpre, gpost, wg_t, wu_t, wd, name):
    rows = h.shape[0]
    tm = _row_tile(rows)
    nf = D_FF // FF_TILE

    def body(dho_ref, h_ref, a_ref, b_ref, f_ref, gpre_ref, gpost_ref, wg_ref, wu_ref, wd_ref,
             dh_ref, da_ref, db_ref, df_ref, n_ref, dgpre_ref, dgpost_ref, acc):
        first = pl.program_id(0) == 0
        dho = dho_ref[...]
        drr = 0.5 * dho
        fn, rf = _rms(f_ref[...])
        _acc_add(dgpost_ref, first, _colsum(drr * fn))
        df16 = _rms_bwd(fn, rf, gpost_ref[...], drr).astype(BF16)
        df_ref[...] = df16
        hn, rh = _rms(h_ref[...])
        n_ref[...] = (hn * gpre_ref[...]).astype(BF16)
        for j in range(nf):
            cols = slice(j * FF_TILE, (j + 1) * FF_TILE)
            ds = _nt(df16, wd_ref[cols, :])
            av = a_ref[:, cols].astype(F32)
            bv = b_ref[:, cols].astype(F32)
            sg = _sigmoid(av)
            db16 = (ds * (av * sg)).astype(BF16)
            da16 = (ds * bv * (sg * (1.0 + av * (1.0 - sg)))).astype(BF16)
            da_ref[:, cols] = da16
            db_ref[:, cols] = db16
            part = _nn(da16, wg_ref[cols, :]) + _nn(db16, wu_ref[cols, :])
            if j == 0:
                acc[...] = part
            else:
                acc[...] += part
        dn = acc[...]
        _acc_add(dgpre_ref, first, _colsum(dn * hn))
        dh_ref[...] = dho + _rms_bwd(hn, rh, gpre_ref[...], dn)

    row_f32 = _row_spec(tm, D_MODEL)
    row_ff = _row_spec(tm, D_FF)
    return pl.pallas_call(
        body, name=name, grid=(rows // tm,),
        in_specs=[row_f32, row_f32, row_ff, row_ff, row_f32, VMEM_SPEC, VMEM_SPEC, VMEM_SPEC, VMEM_SPEC, VMEM_SPEC],
        out_specs=[row_f32, row_ff, row_ff, row_f32, row_f32, _acc_spec(D_MODEL), _acc_spec(D_MODEL)],
        out_shape=[jax.ShapeDtypeStruct((rows, D_MODEL), F32), jax.ShapeDtypeStruct((rows, D_FF), BF16),
                   jax.ShapeDtypeStruct((rows, D_FF), BF16), jax.ShapeDtypeStruct((rows, D_MODEL), BF16),
                   jax.ShapeDtypeStruct((rows, D_MODEL), BF16), jax.ShapeDtypeStruct((8, D_MODEL), F32),
                   jax.ShapeDtypeStruct((8, D_MODEL), F32)],
        scratch_shapes=[pltpu.VMEM((tm, D_MODEL), F32)],
        compiler_params=_params(("arbitrary",)),
    )(dh_out, h, a, b, f, gpre, gpost, wg_t, wu_t, wd)


def _wgrad(lhs, rhs, name, gate=None):
    rows, width = lhs.shape
    tm = 1664 if rows % 1664 == 0 else BLK
    tf = width // 2 if width > WGRAD_TILE_MAX else width
    nr = rows // tm
    gated = gate is not None

    def body(*refs):
        if gated:
            g_ref, l_ref, r_ref, o_ref, acc = refs
            gv = g_ref[...].astype(F32)
            lv = (gv * _sigmoid(gv) * l_ref[...].astype(F32)).astype(BF16)
        else:
            l_ref, r_ref, o_ref, acc = refs
            lv = l_ref[...]
        i = pl.program_id(1)
        part = _tn(lv, r_ref[...])

        @pl.when(i == 0)
        def _():
            acc[...] = part

        @pl.when(i > 0)
        def _():
            acc[...] += part

        @pl.when(i == nr - 1)
        def _():
            o_ref[...] = acc[...].astype(BF16)

    l_spec = pl.BlockSpec((tm, tf), lambda j, i: (i, j))
    r_spec = pl.BlockSpec((tm, D_MODEL), lambda j, i: (i, 0))
    return pl.pallas_call(
        body, name=name, grid=(width // tf, nr),
        in_specs=([l_spec] if gated else []) + [l_spec, r_spec],
        out_specs=pl.BlockSpec((tf, D_MODEL), lambda j, i: (j, 0)),
        out_shape=jax.ShapeDtypeStruct((width, D_MODEL), BF16),
        scratch_shapes=[pltpu.VMEM((tf, D_MODEL), F32)],
        compiler_params=_params(("arbitrary", "arbitrary")),
    )(*([gate] if gated else []), lhs, rhs)


def _chunk_cumsum(x, lower):
    tri = jnp.where(_tri(lower), 1.0, 0.0).astype(BF16)
    hi, lo = _split_bf16(x)
    return _nn(tri, hi) + _nn(tri, lo)


def _mix_in(h, g, win_p, wa2_p, b_a, cos, sin):
    rows = h.shape[0]
    tm = 640 if rows % 640 == 0 else BLK

    def body(h_ref, g_ref, win_ref, wa2_ref, ba_ref, cos_ref, sin_ref,
             gq_ref, gk_ref, gv_ref, gg_ref, sq_ref, sk_ref, sv_ref, ga_ref, loga_ref, bc_ref, n_ref):
        hn, _ = _rms(h_ref[...])
        n16 = (hn * g_ref[...]).astype(BF16)
        n_ref[...] = n16
        proj = _nt(n16, win_ref[...])
        gq_ref[...] = proj[:, P_GQ:P_GK]
        gk_ref[...] = proj[:, P_GK:P_GV]
        gv_ref[...] = proj[:, P_GV:P_GG].astype(BF16)
        gg_ref[...] = proj[:, P_GG:P_SQ]
        c1, s1 = cos_ref[...], sin_ref[...]
        c4 = jnp.concatenate([c1, c1, c1, c1], axis=1)
        s4 = jnp.concatenate([s1, s1, s1, s1], axis=1)
        sq = proj[:, P_SQ:P_SK]
        sk = proj[:, P_SK:P_SV]
        sq_ref[...] = (sq * c4 + _rot_half(sq) * s4).astype(BF16)
        sk_ref[...] = (sk * c1 + _rot_half(sk) * s1).astype(BF16)
        sv_ref[...] = proj[:, P_SV:P_GA].astype(BF16)
        ga = proj[:, P_GA:P_END]
        ga_ref[...] = ga
        z = _nn(ga, wa2_ref[...]) + ba_ref[...]
        loga = (jnp.minimum(z, 0.0) - jnp.log(1.0 + jnp.exp(-jnp.abs(z)))) * (1.0 / GLA_TAU)
        loga_ref[...] = loga
        for c in range(tm // BLK):
            rs = slice(c * BLK, (c + 1) * BLK)
            bc_ref[rs, :] = _chunk_cumsum(loga[rs, :], True)

    f32 = lambda c: jax.ShapeDtypeStruct((rows, c), F32)
    b16 = lambda c: jax.ShapeDtypeStruct((rows, c), BF16)
    rs = lambda c: _row_spec(tm, c)
    return pl.pallas_call(
        body, name="mix_in", grid=(rows // tm,),
        in_specs=[rs(D_MODEL), VMEM_SPEC, VMEM_SPEC, VMEM_SPEC, VMEM_SPEC, rs(128), rs(128)],
        out_specs=[rs(256), rs(256), rs(512), rs(512), rs(512), rs(128), rs(128), rs(128), rs(256), rs(256), rs(D_MODEL)],
        out_shape=[f32(256), f32(256), b16(512), f32(512), b16(512), b16(128), b16(128), f32(128), f32(256), f32(256),
                   b16(D_MODEL)],
        compiler_params=_params(("arbitrary",)),
    )(h, g, win_p, wa2_p, b_a, cos, sin)


def _gla_factors(q, k, bc):
    bm = bc[BLK // 2 - 1:BLK // 2, :]
    bl = bc[BLK - 1:BLK, :]
    e_q, e_k, e_qe, e_kd = jnp.exp(bc - bm), jnp.exp(bm - bc), jnp.exp(bc), jnp.exp(bl - bc)
    return (q * e_q, k * e_k, q * e_qe, k * e_kd), (e_q, e_k, e_qe, e_kd), jnp.exp(bl)


def _gla_fwd(gq, gk, gv, gg, bc, wgn):
    rows = gq.shape[0]
    nc = rows // BLK
    scale = GLA_DK ** -0.5

    def body(q_ref, k_ref, v_ref, gg_ref, bc_ref, wgn_ref, o_ref, cat_ref, sp_ref, st):
        @pl.when(pl.program_id(0) == 0)
        def _():
            st[...] = jnp.zeros_like(st)
        low = _tri(True)
        wgn_v = wgn_ref[...]
        for p in range(2):
            sl = slice(128 * p, 128 * p + 128)
            (qt, kt, qe, kd), _, ebl = _gla_factors(q_ref[:, sl] * scale, k_ref[:, sl], bc_ref[:, sl])
            s_prev = st[p]
            sp_ref[0, p] = s_prev
            s16 = s_prev.astype(BF16)
            qt16 = qt.astype(BF16)
            s_new = s_prev * ebl
            for hh in range(2):
                hs = slice(128 * (2 * p + hh), 128 * (2 * p + hh) + 128)
                lm = _half_mask(128, hh)
                vh = v_ref[:, hs]
                pm = jnp.where(low, _nt(qt16, jnp.where(lm, kt, 0.0).astype(BF16)), 0.0)
                o = _nn(pm.astype(BF16), vh) + _nt(jnp.where(lm, qe, 0.0).astype(BF16), s16)
                s_new = s_new + _tn(vh, jnp.where(lm, kd, 0.0).astype(BF16))
                o_ref[:, hs] = o
                on, _ = _rms(o)
                gate = gg_ref[:, hs]
                cat_ref[:, hs] = (on * wgn_v * (gate * _sigmoid(gate))).astype(BF16)
            st[p] = s_new

    rs = lambda c: _row_spec(BLK, c)
    return pl.pallas_call(
        body, name="gla_fwd", grid=(nc,),
        in_specs=[rs(256), rs(256), rs(512), rs(512), rs(256), VMEM_SPEC],
        out_specs=[rs(512), rs(512), pl.BlockSpec((1, 2, 128, 128), lambda i: (i, 0, 0, 0))],
        out_shape=[jax.ShapeDtypeStruct((rows, 512), F32), jax.ShapeDtypeStruct((rows, 512), BF16),
                   jax.ShapeDtypeStruct((nc, 2, 128, 128), F32)],
        scratch_shapes=[pltpu.VMEM((2, 128, 128), F32)],
        compiler_params=_params(("arbitrary",)),
    )(gq, gk, gv, gg, bc, wgn)


def _gla_bwd(dcat, o_all, gq, gk, gv, gg, bc, sp, wgn):
    rows = gq.shape[0]
    nc = rows // BLK
    scale = GLA_DK ** -0.5

    def body(dc_ref, o_ref, q_ref, k_ref, v_ref, gg_ref, bc_ref, sp_ref, wgn_ref,
             dq_ref, dk_ref, dv_ref, dgg_ref, dla_ref, dwgn_ref, dst):
        first = pl.program_id(0) == 0

        @pl.when(first)
        def _():
            dst[...] = jnp.zeros_like(dst)
        low, upp = _tri(True), _tri(False)
        last_row = lax.broadcasted_iota(jnp.int32, (BLK, 1), 0) == BLK - 1
        wgn_v = wgn_ref[...]
        dwgn = jnp.zeros((1, 128), F32)
        for p in range(2):
            sl = slice(128 * p, 128 * p + 128)
            (qt, kt, qe, kd), (e_q, e_k, e_qe, e_kd), ebl = _gla_factors(
                q_ref[:, sl] * scale, k_ref[:, sl], bc_ref[:, sl])
            s_prev = sp_ref[0, p]
            s16 = s_prev.astype(BF16)
            ds_next = dst[p]
            ds16 = ds_next.astype(BF16)
            qt16 = qt.astype(BF16)
            ds_new = ds_next * ebl
            dqt = jnp.zeros((BLK, 128), F32)
            dkt = jnp.zeros((BLK, 128), F32)
            dqe = jnp.zeros((BLK, 128), F32)
            dkd = jnp.zeros((BLK, 128), F32)
            for hh in range(2):
                hs = slice(128 * (2 * p + hh), 128 * (2 * p + hh) + 128)
                lm = _half_mask(128, hh)
                on, ro = _rms(o_ref[:, hs])
                gate = gg_ref[:, hs]
                sg = _sigmoid(gate)
                si = gate * sg
                dog = dc_ref[:, hs]
                dwgn = dwgn + _colsum(dog * si * on)
                dgg_ref[:, hs] = dog * (on * wgn_v) * (sg * (1.0 + gate * (1.0 - sg)))
                do16 = _rms_bwd(on, ro, wgn_v, dog * si).astype(BF16)
                vh = v_ref[:, hs]
                ktm16 = jnp.where(lm, kt, 0.0).astype(BF16)
                qtm16 = jnp.where(lm, qt, 0.0).astype(BF16)
                qem16 = jnp.where(lm, qe, 0.0).astype(BF16)
                kdm16 = jnp.where(lm, kd, 0.0).astype(BF16)
                p_t = jnp.where(upp, _nt(ktm16, qt16), 0.0)
                dp_t = jnp.where(upp, _nt(vh, do16), 0.0)
                dp = jnp.where(low, _nt(do16, vh), 0.0)
                dv_ref[:, hs] = _nn(p_t.astype(BF16), do16) + _nt(kdm16, ds16)
                dqt = dqt + _nn(dp.astype(BF16), ktm16)
                dkt = dkt + _nn(dp_t.astype(BF16), qtm16)
                dqe = dqe + jnp.where(lm, _nn(do16, s16), 0.0)
                dkd = dkd + jnp.where(lm, _nn(vh, ds16), 0.0)
                ds_new = ds_new + _tn(do16, qem16)
            debl = _colsum(ds_next * s_prev)
            dq_ref[:, sl] = (dqt * e_q + dqe * e_qe) * scale
            dk_ref[:, sl] = dkt * e_k + dkd * e_kd
            dkd_kd = dkd * kd
            db = dqt * qt - dkt * kt + dqe * qe - dkd_kd
            db = jnp.where(last_row, db + (_colsum(dkd_kd) + debl * ebl), db)
            dla_ref[:, sl] = _chunk_cumsum(db, False)
            dst[p] = ds_new
        _acc_add(dwgn_ref, first, dwgn)

    rev = lambda c: pl.BlockSpec((BLK, c), lambda i: (nc - 1 - i, 0))
    f32 = lambda c: jax.ShapeDtypeStruct((rows, c), F32)
    return pl.pallas_call(
        body, name="gla_bwd", grid=(nc,),
        in_specs=[rev(512), rev(512), rev(256), rev(256), rev(512), rev(512), rev(256),
                  pl.BlockSpec((1, 2, 128, 128), lambda i: (nc - 1 - i, 0, 0, 0)), VMEM_SPEC],
        out_specs=[rev(256), rev(256), rev(512), rev(512), rev(256), _acc_spec(128)],
        out_shape=[f32(256), f32(256), f32(512), f32(512), f32(256), jax.ShapeDtypeStruct((8, 128), F32)],
        scratch_shapes=[pltpu.VMEM((2, 128, 128), F32)],
        compiler_params=_params(("arbitrary",)),
    )(dcat, o_all, gq, gk, gv, gg, bc, sp, wgn)


def _swa_masks(i):
    t = lax.broadcasted_iota(jnp.int32, (BLK, BLK), 0)
    c = lax.broadcasted_iota(jnp.int32, (BLK, BLK), 1)
    own_side = c <= t
    band_ok = i >= jnp.where(own_side, 1, 2)
    meta_ok = (c % N_META) <= jnp.where(i >= 1, N_META, t - PAD_ROWS)
    return own_side, band_ok, meta_ok, c // N_META


def _swa_blocks(ref, i):
    prev = pl.multiple_of(jnp.maximum(i - 1, 0) * BLK, BLK)
    own = pl.multiple_of(i * BLK, BLK)
    return jnp.concatenate([ref[pl.ds(prev, BLK), :], ref[pl.ds(own, BLK), :]], axis=0), prev, own


def _swa_meta_operand(ref):
    blk = ref[0:BLK, :]
    swapped = pltpu.roll(blk, 64, 1)
    lo = jnp.where(_half_mask(128, 0), blk, swapped)
    hi = jnp.where(_half_mask(128, 1), blk, swapped)
    meta = jnp.concatenate([lo, lo, hi, hi], axis=1)[PAD_ROWS:BLK, :]
    tiled = jnp.concatenate([meta] * SWA_HEADS, axis=0)
    j = lax.broadcasted_iota(jnp.int32, tiled.shape, 0)
    lane = lax.broadcasted_iota(jnp.int32, tiled.shape, 1)
    return jnp.where(j // N_META == lane // SWA_HD, tiled, jnp.zeros_like(tiled))


def _swa_meta_fold(acc):
    out = jnp.zeros((N_META, 128), F32)
    for hd in range(SWA_HEADS):
        half, kv = hd % 2, hd // 4
        piece = acc[N_META * hd:N_META * (hd + 1), 128 * (hd // 2):128 * (hd // 2) + 128]
        piece = jnp.where(_half_mask(128, half), piece, 0.0)
        out = out + (piece if half == kv else pltpu.roll(piece, 64, 1))
    return out


def _by_head(group, per_head):
    out = jnp.zeros((BLK, BLK), F32)
    for hd, v in enumerate(per_head):
        out = jnp.where(group == hd, v, out)
    return out


def _place(x, kv):
    if kv == 0:
        lo = jnp.where(_half_mask(128, 0), x, jnp.zeros_like(x))
        return lo, pltpu.roll(lo, 64, 1)
    hi = jnp.where(_half_mask(128, 1), x, jnp.zeros_like(x))
    return pltpu.roll(hi, 64, 1), hi


def _swa_fwd(sq, sk, sv, sinks, wn):
    rows = sq.shape[0]
    nb = rows // BLK
    scale = SWA_HD ** -0.5

    def body(q_ref, k_ref, v_ref, sink_ref, wn_ref, o_ref, cat_ref, lse_ref, kp, vp):
        i = pl.program_id(0)

        @pl.when(i == 0)
        def _():
            kp[...] = _swa_meta_operand(k_ref)
            vp[...] = _swa_meta_operand(v_ref)
        own_side, band_ok, meta_ok, group = _swa_masks(i)
        k2, _, _ = _swa_blocks(k_ref, i)
        v2, _, _ = _swa_blocks(v_ref, i)
        kz = (_place(k2, 0), _place(k2, 1))
        vz = (_place(v2, 0), _place(v2, 1))
        q_all = q_ref[...]
        s_meta = jnp.where(meta_ok, _nt(q_all, kp[...]) * scale, NEG_INF)
        s_band, m = [], []
        for hd in range(SWA_HEADS):
            kv, half = hd // 4, hd % 2
            q_pair = q_all[:, 128 * (hd // 2):128 * (hd // 2) + 128]
            s2 = _nt(q_pair, kz[kv][half])
            s = jnp.where(band_ok, jnp.where(own_side, s2[:, BLK:], s2[:, :BLK]) * scale, NEG_INF)
            top = jnp.maximum(jnp.max(s, axis=-1, keepdims=True),
                              jnp.max(jnp.where(group == hd, s_meta, NEG_INF), axis=-1, keepdims=True))
            s_band.append(s)
            m.append(jnp.maximum(top, sink_ref[0, hd]))
        e_meta = jnp.exp(s_meta - _by_head(group, m))
        o_meta = _nn(e_meta.astype(BF16), vp[...])
        outs = []
        for pr in range(4):
            o_pair = o_meta[:, 128 * pr:128 * pr + 128]
            rden = []
            for half in range(2):
                hd = 2 * pr + half
                kv = hd // 4
                e = jnp.exp(s_band[hd] - m[hd])
                den = (jnp.sum(e, axis=-1, keepdims=True)
                       + jnp.sum(jnp.where(group == hd, e_meta, 0.0), axis=-1, keepdims=True)
                       + jnp.exp(sink_ref[0, hd] - m[hd]))
                lse_ref[:, hd:hd + 1] = m[hd] + jnp.log(den)
                rden.append(1.0 / den)
                e2 = jnp.concatenate([jnp.where(own_side, 0.0, e), jnp.where(own_side, e, 0.0)], axis=1).astype(BF16)
                o_pair = o_pair + _nn(e2, vz[kv][half])
            outs.append(o_pair * jnp.where(_half_mask(128, 0), rden[0], rden[1]))
        o = jnp.concatenate(outs, axis=1)
        o_ref[...] = o
        on, _ = _rms(o)
        cat_ref[...] = (on * wn_ref[...]).astype(BF16)

    return pl.pallas_call(
        body, name="swa_fwd", grid=(nb,),
        in_specs=[_row_spec(BLK, 512), VMEM_SPEC, VMEM_SPEC, SMEM_SPEC, VMEM_SPEC],
        out_specs=[_row_spec(BLK, 512), _row_spec(BLK, 512), _row_spec(BLK, SWA_HEADS)],
        out_shape=[jax.ShapeDtypeStruct((rows, 512), F32), jax.ShapeDtypeStruct((rows, 512), BF16),
                   jax.ShapeDtypeStruct((rows, SWA_HEADS), F32)],
        scratch_shapes=[pltpu.VMEM((BLK, 512), BF16), pltpu.VMEM((BLK, 512), BF16)],
        compiler_params=_params(("arbitrary",)),
    )(sq, sk, sv, sinks, wn)


def _swa_bwd(dcat, o_all, sq, sk, sv, lse, sinks, wn):
    rows = sq.shape[0]
    nb = rows // BLK
    scale = SWA_HD ** -0.5

    def body(dc_ref, o_ref, q_ref, k_ref, v_ref, lse_ref, sink_ref, wn_ref, dq_ref, dk_ref, dv_ref, dsink_ref, dwn_ref,
             kp, vp, dkp, dvp):
        i = pl.program_id(0)
        first = i == 0

        @pl.when(first)
        def _():
            dk_ref[...] = jnp.zeros_like(dk_ref)
            dv_ref[...] = jnp.zeros_like(dv_ref)
            dkp[...] = jnp.zeros_like(dkp)
            dvp[...] = jnp.zeros_like(dvp)
            kp[...] = _swa_meta_operand(k_ref)
            vp[...] = _swa_meta_operand(v_ref)
        own_side, band_ok, meta_ok, group = _swa_masks(i)
        k2, prev, own = _swa_blocks(k_ref, i)
        v2, _, _ = _swa_blocks(v_ref, i)
        kz = (_place(k2, 0), _place(k2, 1))
        vz = (_place(v2, 0), _place(v2, 1))
        o = o_ref[...]
        on, ro = _rms(o)
        dc = dc_ref[...]
        _acc_add(dwn_ref, first, _colsum(dc * on))
        do = _rms_bwd(on, ro, wn_ref[...], dc)
        do_o = do * o
        do16 = do.astype(BF16)
        q_all = q_ref[...]
        lse = [lse_ref[:, hd:hd + 1] for hd in range(SWA_HEADS)]
        delta = [jnp.sum(jnp.where(_half_mask(128, hd % 2), do_o[:, 128 * (hd // 2):128 * (hd // 2) + 128], 0.0),
                         axis=-1, keepdims=True) for hd in range(SWA_HEADS)]
        s_meta = jnp.where(meta_ok, _nt(q_all, kp[...]) * scale, NEG_INF)
        p_meta = jnp.exp(s_meta - _by_head(group, lse))
        ds_meta16 = (p_meta * (_nt(do16, vp[...]) - _by_head(group, delta)) * scale).astype(BF16)
        dq_meta = _nn(ds_meta16, kp[...])
        dkp[...] += _tn(ds_meta16, q_all)
        dvp[...] += _tn(p_meta.astype(BF16), do16)
        lane8 = lax.broadcasted_iota(jnp.int32, (1, 128), 1)
        dsink = jnp.zeros((1, 128), F32)
        dk2 = [[jnp.zeros((2 * BLK, 128), F32) for _ in range(2)] for _ in range(2)]
        dv2 = [[jnp.zeros((2 * BLK, 128), F32) for _ in range(2)] for _ in range(2)]
        dqs = []
        for pr in range(4):
            ps = slice(128 * pr, 128 * pr + 128)
            q_pair = q_all[:, ps]
            do_pair = do16[:, ps]
            dq_pair = dq_meta[:, ps]
            for half in range(2):
                hd = 2 * pr + half
                kv = hd // 4

                def window(x2):
                    return jnp.where(own_side, x2[:, BLK:], x2[:, :BLK])

                def unwindow(x):
                    return jnp.concatenate([jnp.where(own_side, 0.0, x), jnp.where(own_side, x, 0.0)], axis=1).astype(BF16)
                s = jnp.where(band_ok, window(_nt(q_pair, kz[kv][half])) * scale, NEG_INF)
                prob = jnp.exp(s - lse[hd])
                dsink = dsink + jnp.where(lane8 == hd, -jnp.sum(jnp.exp(sink_ref[0, hd] - lse[hd]) * delta[hd]), 0.0)
                ds2 = unwindow(prob * (window(_nt(do_pair, vz[kv][half])) - delta[hd]) * scale)
                dq_pair = dq_pair + _nn(ds2, kz[kv][half])
                dk2[kv][half] = dk2[kv][half] + _tn(ds2, q_pair)
                dv2[kv][half] = dv2[kv][half] + _tn(unwindow(prob), do_pair)
            dqs.append(dq_pair)
        dq_ref[...] = jnp.concatenate(dqs, axis=1)
        _acc_add(dsink_ref, first, dsink)
        for ref, acc2 in ((dk_ref, dk2), (dv_ref, dv2)):
            tot = jnp.zeros((2 * BLK, 128), F32)
            for kv in range(2):
                for half in range(2):
                    part = jnp.where(_half_mask(128, half), acc2[kv][half], 0.0)
                    tot = tot + (part if half == kv else pltpu.roll(part, 64, 1))
            ref[pl.ds(prev, BLK), :] += tot[:BLK]
            ref[pl.ds(own, BLK), :] += tot[BLK:]

        @pl.when(i == nb - 1)
        def _():
            dk_ref[PAD_ROWS:BLK, :] += _swa_meta_fold(dkp[...])
            dv_ref[PAD_ROWS:BLK, :] += _swa_meta_fold(dvp[...])

    full = pl.BlockSpec((rows, 128), lambda i: (0, 0))
    return pl.pallas_call(
        body, name="swa_bwd", grid=(nb,),
        in_specs=[_row_spec(BLK, 512), _row_spec(BLK, 512), _row_spec(BLK, 512), VMEM_SPEC, VMEM_SPEC,
                  _row_spec(BLK, SWA_HEADS), SMEM_SPEC, VMEM_SPEC],
        out_specs=[_row_spec(BLK, 512), full, full, _acc_spec(128), _acc_spec(512)],
        out_shape=[jax.ShapeDtypeStruct((rows, 512), F32), jax.ShapeDtypeStruct((rows, 128), F32),
                   jax.ShapeDtypeStruct((rows, 128), F32), jax.ShapeDtypeStruct((8, 128), F32),
                   jax.ShapeDtypeStruct((8, 512), F32)],
        scratch_shapes=[pltpu.VMEM((BLK, 512), BF16), pltpu.VMEM((BLK, 512), BF16),
                        pltpu.VMEM((BLK, 512), F32), pltpu.VMEM((BLK, 512), F32)],
        compiler_params=_params(("arbitrary",)),
    )(dcat, o_all, sq, sk, sv, lse, sinks, wn)


def _mix_out(h, cat_g, cat_s, wout, gpost):
    rows = h.shape[0]
    tm = _row_tile(rows)

    def body(h_ref, cg_ref, cs_ref, w_ref, g_ref, ho_ref, m_ref):
        m = _nn(cg_ref[...], w_ref[0:512, :]) + _nn(cs_ref[...], w_ref[512:1024, :])
        m_ref[...] = m
        mn, _ = _rms(m)
        ho_ref[...] = h_ref[...] + mn * g_ref[...]

    row_f32 = _row_spec(tm, D_MODEL)
    return pl.pallas_call(
        body, name="mix_out", grid=(rows // tm,),
        in_specs=[row_f32, _row_spec(tm, 512), _row_spec(tm, 512), VMEM_SPEC, VMEM_SPEC],
        out_specs=[row_f32, row_f32],
        out_shape=[jax.ShapeDtypeStruct((rows, D_MODEL), F32), jax.ShapeDtypeStruct((rows, D_MODEL), F32)],
        compiler_params=_params(("arbitrary",)),
    )(h, cat_g, cat_s, wout, gpost)


def _mix_out_bwd(dh, m, wout, gpost):
    rows = dh.shape[0]
    tm = _row_tile(rows)

    def body(dh_ref, m_ref, w_ref, g_ref, dcg_ref, dcs_ref, dm_ref, dg_ref):
        first = pl.program_id(0) == 0
        dhv = dh_ref[...]
        mn, rm = _rms(m_ref[...])
        _acc_add(dg_ref, first, _colsum(dhv * mn))
        dm16 = _rms_bwd(mn, rm, g_ref[...], dhv).astype(BF16)
        dm_ref[...] = dm16
        dcat = _nt(dm16, w_ref[...])
        dcg_ref[...] = dcat[:, 0:512]
        dcs_ref[...] = dcat[:, 512:1024]

    row_f32 = _row_spec(tm, D_MODEL)
    return pl.pallas_call(
        body, name="mix_out_bwd", grid=(rows // tm,),
        in_specs=[row_f32, row_f32, VMEM_SPEC, VMEM_SPEC],
        out_specs=[_row_spec(tm, 512), _row_spec(tm, 512), row_f32, _acc_spec(D_MODEL)],
        out_shape=[jax.ShapeDtypeStruct((rows, 512), F32), jax.ShapeDtypeStruct((rows, 512), F32),
                   jax.ShapeDtypeStruct((rows, D_MODEL), BF16), jax.ShapeDtypeStruct((8, D_MODEL), F32)],
        compiler_params=_params(("arbitrary",)),
    )(dh, m, wout, gpost)


def _mix_in_bwd(dh_out, h, g, win_p, wa2_p, cos, sin, loga, ga, dgq, dgk, dgv, dgg, dsq, dsk, dsv, dloga):
    rows = h.shape[0]
    tm = _row_tile(rows)

    def body(dho_ref, h_ref, g_ref, win_ref, wa2_ref, cos_ref, sin_ref, loga_ref, ga_ref,
             dgq_ref, dgk_ref, dgv_ref, dgg_ref, dsq_ref, dsk_ref, dsv_ref, dla_ref,
             dh_ref, dproj_ref, dwa2_ref, dg_ref, dba_ref):
        first = pl.program_id(0) == 0
        dz = dla_ref[...] * (1.0 / GLA_TAU) * (1.0 - jnp.exp(GLA_TAU * loga_ref[...]))
        _acc_add(dba_ref, first, _colsum(dz))
        dga = _nt(dz, wa2_ref[...])
        pa = _tn(ga_ref[...], dz)
        c1, s1 = cos_ref[...], sin_ref[...]
        c4 = jnp.concatenate([c1, c1, c1, c1], axis=1)
        s4 = jnp.concatenate([s1, s1, s1, s1], axis=1)
        dq_r, dk_r = dsq_ref[...], dsk_ref[...]
        dsq = dq_r * c4 - _rot_half(dq_r * s4)
        dsk = dk_r * c1 - _rot_half(dk_r * s1)
        dproj16 = jnp.concatenate(
            [dgq_ref[...], dgk_ref[...], dgv_ref[...], dgg_ref[...], dsq, dsk, dsv_ref[...], dga], axis=1).astype(BF16)
        dproj_ref[...] = dproj16
        dn = _nn(dproj16, win_ref[...])

        @pl.when(first)
        def _():
            dwa2_ref[...] = pa

        @pl.when(jnp.logical_not(first))
        def _():
            dwa2_ref[...] += pa
        hn, rh = _rms(h_ref[...])
        _acc_add(dg_ref, first, _colsum(dn * hn))
        dh_ref[...] = dho_ref[...] + _rms_bwd(hn, rh, g_ref[...], dn)

    rs = lambda c: _row_spec(tm, c)
    return pl.pallas_call(
        body, name="mix_in_bwd", grid=(rows // tm,),
        in_specs=[rs(D_MODEL), rs(D_MODEL), VMEM_SPEC, VMEM_SPEC, VMEM_SPEC, rs(128), rs(128), rs(256), rs(128),
                  rs(256), rs(256), rs(512), rs(512), rs(512), rs(128), rs(128), rs(256)],
        out_specs=[rs(D_MODEL), rs(P_END), pl.BlockSpec((128, 256), lambda i: (0, 0)), _acc_spec(D_MODEL), _acc_spec(256)],
        out_shape=[jax.ShapeDtypeStruct((rows, D_MODEL), F32), jax.ShapeDtypeStruct((rows, P_END), BF16),
                   jax.ShapeDtypeStruct((128, 256), F32), jax.ShapeDtypeStruct((8, D_MODEL), F32),
                   jax.ShapeDtypeStruct((8, 256), F32)],
        compiler_params=_params(("arbitrary",)),
    )(dh_out, h, g, win_p, wa2_p, cos, sin, loga, ga, dgq, dgk, dgv, dgg, dsq, dsk, dsv, dloga)


def _rope_tables(rows):
    pos = (jnp.arange(rows, dtype=jnp.int32) - PAD_ROWS).astype(F32)
    inv_freq = 1.0 / (ROPE_THETA ** (jnp.arange(0, SWA_HD, 2, dtype=F32) / SWA_HD))
    ang = pos[:, None] * inv_freq[None, :]
    ang = jnp.concatenate([ang, ang, ang, ang], axis=-1)
    return jnp.cos(ang), jnp.sin(ang)


def _local_step(x, tgt, front, w, late_weights=None, on_grads=None):
    cos, sin = _rope_tables(x.shape[0] + BLK)
    g = {}

    def tell(group, names):
        for nm in names:
            g[nm] = grads_now[nm]
        return 0.0 if on_grads is None else on_grads(group, {nm: grads_now[nm] for nm in names})

    h0, h1, a1, b1, f1 = _ffn_fwd(x, w["ffn1_pre"], w["wg1"], w["wu1"], w["wd1"], w["ffn1_post"], front=front)
    if late_weights is not None:
        w = {**w, **late_weights("win", f1)}
    gq, gk, gv, gg, sq, sk, sv, ga, loga, bc, n2 = _mix_in(h1, w["mix_pre"], w["win"], w["wa2"], w["b_a"], cos, sin)
    o_g, cat_g, sp = _gla_fwd(gq, gk, gv, gg, bc, w["gla_norm"])
    o_s, cat_s, lse = _swa_fwd(sq, sk, sv, w["sinks"], w["swa_norm"])
    if late_weights is not None:
        w = {**w, **late_weights("rest", lse)}
    h2, m = _mix_out(h1, cat_g, cat_s, w["wout"], w["mix_post"])
    h3, a2, b2, f2, dy, loss = _ffn_fwd(h2, w["ffn2_pre"], w["wg2"], w["wu2"], w["wd2"], w["ffn2_post"], tgt)
    del h3
    dh2, da, db, df, n3, g["ffn2_pre"], g["ffn2_post"] = _ffn_bwd_act(
        dy, h2, a2, b2, f2, w["ffn2_pre"], w["ffn2_post"], w["wg2"], w["wu2"], w["wd2"], "ffn2_bwd_act")
    grads_now = dict(wd2=_wgrad(b2, df, "ffn2_wgrad_down", gate=a2), wg2=_wgrad(da, n3, "ffn2_wgrad_gate"),
                     wu2=_wgrad(db, n3, "ffn2_wgrad_up"))
    tok = tell("ffn2", ("wd2", "wg2", "wu2"))
    dcg, dcs, dm, g["mix_post"] = _mix_out_bwd(dh2, m, w["wout"], w["mix_post"] + tok)
    dsq, dsk, dsv, g["sinks"], g["swa_norm"] = _swa_bwd(dcs, o_s, sq, sk, sv, lse, w["sinks"], w["swa_norm"])
    dgq, dgk, dgv, dgg, dloga, g["gla_norm"] = _gla_bwd(dcg, o_g, gq, gk, gv, gg, bc, sp, w["gla_norm"])
    dh1, dproj, g["wa2"], g["mix_pre"], g["b_a"] = _mix_in_bwd(
        dh2, h1, w["mix_pre"], w["win"], w["wa2"], cos, sin, loga, ga, dgq, dgk, dgv, dgg, dsq, dsk, dsv, dloga)
    grads_now = dict(wout=jnp.concatenate([_wgrad(cat_g, dm, "wout_wgrad_gla"), _wgrad(cat_s, dm, "wout_wgrad_swa")], axis=0),
                     win=_wgrad(dproj, n2, "win_wgrad"))
    tok = tell("mix", ("wout", "win"))
    dh0, da, db, df, n1, g["ffn1_pre"], g["ffn1_post"] = _ffn_bwd_act(
        dh1, h0, a1, b1, f1, w["ffn1_pre"] + tok, w["ffn1_post"], w["wg1"], w["wu1"], w["wd1"], "ffn1_bwd_act")
    grads_now = dict(wd1=_wgrad(b1, df, "ffn1_wgrad_down", gate=a1))
    tell("ffn1_down", ("wd1",))
    grads_now = dict(wg1=_wgrad(da, n1, "ffn1_wgrad_gate"))
    tell("ffn1_gate", ("wg1",))
    grads_now = dict(wu1=_wgrad(db, n1, "ffn1_wgrad_up"))
    tell("ffn1_up", ("wu1",))
    return loss[0, 0], dh0, g


def _win_pad_rows(win_t):
    pad = jnp.zeros((P_END - P_GA - 16, win_t.shape[1]), win_t.dtype)
    return jnp.concatenate([win_t[0:1536], win_t[1552:2320], win_t[1536:1552], pad], axis=0)


def _win_unpad_rows(win_p):
    return jnp.concatenate([win_p[0:1536], win_p[P_GA:P_GA + 16], win_p[1536:P_GA]], axis=0)


def _place_on_mesh():
    return lax.axis_index("x"), lax.axis_index("y"), lax.axis_index("c")


def _dev_index(px, py, pc):
    return 4 * px + 2 * py + pc


def _other_devices(x, y, c):
    flip = lambda v, f: 1 - v if f else v
    return [(flip(x, fx), flip(y, fy), flip(c, fc)) for fx in (0, 1) for fy in (0, 1) for fc in (0, 1)][1:]


def _all_gather(shards):
    n = len(shards)

    def body(*refs):
        ins, outs = refs[:n], refs[n:2 * n]
        zeros_ref, send_sems, recv_sems, local_sems = refs[2 * n:]
        zeros_ref[...] = jnp.zeros_like(zeros_ref)
        x, y, c = _place_on_mesh()
        me, sibling = (x, y, c), (x, y, 1 - c)
        chips = [(1 - x, y), (x, 1 - y), (1 - x, 1 - y)]

        def rows(k, px, py, pc):
            r = ins[k].shape[0]
            return outs[k].at[pl.ds(pl.multiple_of(_dev_index(px, py, pc) * r, 8), r), :]

        def copy(k, slot, block, to, src=None):
            return pltpu.make_async_remote_copy(
                src_ref=rows(k, *block) if src is None else src, dst_ref=rows(k, *block),
                send_sem=send_sems.at[k, slot], recv_sem=recv_sems.at[k, slot], device_id=to, device_id_type=MESH)

        local = [pltpu.make_async_copy(ins[k], rows(k, *me), local_sems.at[k]) for k in range(n)]
        sends = []
        for k in range(n):
            local[k].start()
            sends.append(copy(k, 0, me, sibling, src=ins[k]))
            sends += [copy(k, 1 + j, me, (*chip, c), src=ins[k]) for j, chip in enumerate(chips)]
        for cp in sends:
            cp.start()
        for k in range(n):
            for j, chip in enumerate(chips):
                copy(k, 1 + j, (*chip, c), me).wait_recv()
                passed = copy(k, 4 + j, (*chip, c), sibling)
                passed.start()
                sends.append(passed)
        for k in range(n):
            copy(k, 0, sibling, me).wait_recv()
            for j, chip in enumerate(chips):
                copy(k, 4 + j, (*chip, 1 - c), me).wait_recv()
        for cp in sends:
            cp.wait_send()
        for cp in local:
            cp.wait()

    return pl.pallas_call(
        body, name="all_gather_weights",
        in_specs=[ANY_SPEC] * n, out_specs=[ANY_SPEC] * n + [VMEM_SPEC],
        out_shape=[jax.ShapeDtypeStruct((N_DEV * s.shape[0], s.shape[1]), s.dtype) for s in shards]
        + [jax.ShapeDtypeStruct((8, 128), F32)],
        scratch_shapes=[pltpu.SemaphoreType.DMA((n, 7)), pltpu.SemaphoreType.DMA((n, 7)), pltpu.SemaphoreType.DMA((n,))],
    )(*shards)


HBM_SPEC = pl.BlockSpec(memory_space=pltpu.HBM)
SEM_SPEC = pl.BlockSpec(memory_space=pltpu.SEMAPHORE)
DATAFLOW = pltpu.SideEffectType.DATAFLOW_SIDE_EFFECTING


GATHER, SCATTER, SCATTER_CHIPS = "gather", "scatter", "scatter among chips"


def _exchange_peers(kind):
    x, y, c = _place_on_mesh()
    if kind == SCATTER_CHIPS:
        peers = [(1 - x, y, c), (x, 1 - y, c), (1 - x, 1 - y, c)]
        return peers, [2 * p[0] + p[1] for p in peers], 2 * x + y, 4
    peers = _other_devices(x, y, c)
    return peers, [_dev_index(*p) for p in peers], _dev_index(x, y, c), N_DEV


def _exchange_copies(srcs, lands, send_sems, recv_sems, own_sems, kind, arriving):
    peers, theirs, me, blocks = _exchange_peers(kind)
    remote, local = [], []
    for k, (src, land) in enumerate(zip(srcs, lands)):
        r = land.shape[0] // blocks

        def block(ref, d):
            return ref.at[pl.ds(pl.multiple_of(d * r, 8), r), :]

        for f, (peer, him) in enumerate(zip(peers, theirs)):
            mine, his = (him, me) if arriving else (me, him)
            sem = len(peers) * k + f
            remote.append(pltpu.make_async_remote_copy(
                src_ref=src if kind == GATHER else block(src, his), dst_ref=block(land, mine),
                send_sem=send_sems.at[sem], recv_sem=recv_sems.at[sem], device_id=peer, device_id_type=MESH))
        local.append(pltpu.make_async_copy(src if kind == GATHER else block(src, me), block(land, me), own_sems.at[k]))
    return remote, local


def _exchange_start(srcs, kind, name):
    n = len(srcs)
    lands = [lax.empty((N_DEV * s.shape[0], s.shape[1]) if kind == GATHER else s.shape, s.dtype) for s in srcs]
    sems = (3 if kind == SCATTER_CHIPS else 7) * n

    def body(*refs):
        remote, local = _exchange_copies(refs[:n], refs[n:2 * n], *refs[2 * n:2 * n + 3], kind, False)
        for cp in remote + local:
            cp.start()
        refs[-1][...] = jnp.zeros_like(refs[-1])

    both = list(srcs) + list(lands)
    outs = pl.pallas_call(
        body, name=name,
        out_shape=(pltpu.SemaphoreType.DMA((sems,)), pltpu.SemaphoreType.DMA((sems,)), pltpu.SemaphoreType.DMA((n,)),
                   *[pltpu.HBM(a.shape, a.dtype) for a in both], jax.ShapeDtypeStruct((8, 128), F32)),
        in_specs=[HBM_SPEC] * (2 * n), out_specs=(SEM_SPEC, SEM_SPEC, SEM_SPEC, *[HBM_SPEC] * (2 * n), VMEM_SPEC),
        input_output_aliases={i: 3 + i for i in range(2 * n)},
        compiler_params=pltpu.CompilerParams(has_side_effects=DATAFLOW),
    )(*[pltpu.with_memory_space_constraint(a, pltpu.HBM) for a in both])
    return outs[0:3], outs[3:3 + n], outs[3 + n:3 + 2 * n], outs[-1]


def _exchange_wait(started, kind, after, name):
    sems, srcs, lands, _ = started
    n = len(srcs)

    def body(*refs):
        args = (refs[:n], refs[n:2 * n], *refs[2 * n:2 * n + 3], kind)
        going, local = _exchange_copies(*args, False)
        for cp in going:
            cp.wait_send()
        for cp in local:
            cp.wait()
        for cp in _exchange_copies(*args, True)[0]:
            cp.wait_recv()

    both = list(srcs) + list(lands)
    outs = pl.pallas_call(
        body, name=name, out_shape=[pltpu.HBM(a.shape, a.dtype) for a in both],
        in_specs=[HBM_SPEC] * (2 * n) + [SEM_SPEC, SEM_SPEC, SEM_SPEC, ANY_SPEC], out_specs=[HBM_SPEC] * (2 * n),
        input_output_aliases={i: i for i in range(2 * n)},
        compiler_params=pltpu.CompilerParams(has_side_effects=DATAFLOW),
    )(*both, *sems, after)
    return outs[n:]


def _sibling_reduce(part, name):
    r, cols = part.shape[0] // N_DEV, part.shape[1]

    def swap(p_ref, got_ref, send_sems, recv_sems):
        x, y, c = _place_on_mesh()
        copies = [pltpu.make_async_remote_copy(
            src_ref=p_ref.at[pl.ds(pl.multiple_of((2 * j + 1 - c) * r, 8), r), :], dst_ref=got_ref.at[pl.ds(j * r, r), :],
            send_sem=send_sems.at[j], recv_sem=recv_sems.at[j], device_id=(x, y, 1 - c), device_id_type=MESH)
            for j in range(4)]
        for cp in copies:
            cp.start()
        for cp in copies:
            cp.wait()

    got = pl.pallas_call(
        swap, name=name + "_swap", in_specs=[ANY_SPEC], out_specs=ANY_SPEC,
        out_shape=jax.ShapeDtypeStruct((4 * r, cols), part.dtype),
        scratch_shapes=[pltpu.SemaphoreType.DMA((4,)), pltpu.SemaphoreType.DMA((4,))],
    )(part)

    def add(c_ref, mine_ref, got_ref, o_ref):
        del c_ref
        o_ref[...] = (mine_ref[...].astype(F32) + got_ref[...].astype(F32)).astype(o_ref.dtype)

    core = lax.axis_index("c").astype(jnp.int32).reshape(1)
    return pl.pallas_call(
        add, name=name + "_add",
        grid_spec=pltpu.PrefetchScalarGridSpec(
            num_scalar_prefetch=1, grid=(4,),
            in_specs=[pl.BlockSpec((r, cols), lambda j, c_ref: (2 * j + c_ref[0], 0)),
                      pl.BlockSpec((r, cols), lambda j, c_ref: (j, 0))],
            out_specs=pl.BlockSpec((r, cols), lambda j, c_ref: (j, 0))),
        out_shape=jax.ShapeDtypeStruct((4 * r, cols), part.dtype),
        compiler_params=_params(("arbitrary",)),
    )(core, part, got)


def _sum_partials(parts, name, blocks=N_DEV):
    n = len(parts)

    def body(*refs):
        ins, outs = refs[:n], refs[n:]
        first = pl.program_id(0) == 0
        for i_ref, o_ref in zip(ins, outs):
            v = i_ref[...].astype(F32)

            @pl.when(first)
            def _():
                o_ref[...] = v

            @pl.when(jnp.logical_not(first))
            def _():
                o_ref[...] += v

    shapes = [(p.shape[0] // blocks, p.shape[1]) for p in parts]
    return pl.pallas_call(
        body, name=name, grid=(blocks,),
        in_specs=[pl.BlockSpec(s, lambda j: (j, 0)) for s in shapes],
        out_specs=[pl.BlockSpec(s, lambda j: (0, 0)) for s in shapes],
        out_shape=[jax.ShapeDtypeStruct(s, F32) for s in shapes],
        compiler_params=_params(("arbitrary",)),
    )(*parts)


def _all_reduce_small(slab):
    rows, cols = slab.shape

    def body(x_ref, o_ref, gathered, send_sems, recv_sems):
        x, y, c = _place_on_mesh()
        me = _dev_index(x, y, c)
        peers = _other_devices(x, y, c)

        def copy(f, peer):
            return pltpu.make_async_remote_copy(
                src_ref=x_ref, dst_ref=gathered.at[me], send_sem=send_sems.at[f], recv_sem=recv_sems.at[f],
                device_id=peer, device_id_type=MESH)

        def arrival(f, peer):
            return pltpu.make_async_remote_copy(
                src_ref=x_ref, dst_ref=gathered.at[_dev_index(*peer)], send_sem=send_sems.at[f], recv_sem=recv_sems.at[f],
                device_id=peer, device_id_type=MESH)

        sends = [copy(f, peer) for f, peer in enumerate(peers)]
        for cp in sends:
            cp.start()
        gathered[me] = x_ref[...]
        for f, peer in enumerate(peers):
            arrival(f, peer).wait_recv()
        for cp in sends:
            cp.wait_send()
        total = gathered[0]
        for d in range(1, N_DEV):
            total = total + gathered[d]
        o_ref[...] = total

    return pl.pallas_call(
        body, name="all_reduce_small",
        in_specs=[VMEM_SPEC], out_specs=VMEM_SPEC, out_shape=jax.ShapeDtypeStruct((rows, cols), F32),
        scratch_shapes=[pltpu.VMEM((N_DEV, rows, cols), F32), pltpu.SemaphoreType.DMA((7,)), pltpu.SemaphoreType.DMA((7,))],
    )(slab)


def _adamw(ws, gs, ms, vs, name):
    n = len(ws)
    c1 = 1.0 / (1.0 - ADAM_B1 ** ADAM_STEP)
    c2 = 1.0 / (1.0 - ADAM_B2 ** ADAM_STEP)

    def body(*refs):
        w_r, g_r, m_r, v_r = refs[:n], refs[n:2 * n], refs[2 * n:3 * n], refs[3 * n:4 * n]
        d_o, m_o, v_o = refs[4 * n:5 * n], refs[5 * n:6 * n], refs[6 * n:7 * n]
        for k in range(n):
            g = g_r[k][...]
            m = ADAM_B1 * m_r[k][...] + (1.0 - ADAM_B1) * g
            v = ADAM_B2 * v_r[k][...] + (1.0 - ADAM_B2) * (g * g)
            m_o[k][...] = m
            v_o[k][...] = v
            d_o[k][...] = -ADAM_LR * ((m * c1) / (jnp.sqrt(v * c2) + ADAM_EPS) + ADAM_WD * w_r[k][...])

    shapes = [jax.ShapeDtypeStruct(w.shape, F32) for w in ws]
    outs = pl.pallas_call(
        body, name=name, in_specs=[VMEM_SPEC] * (4 * n), out_specs=[VMEM_SPEC] * (3 * n), out_shape=shapes * 3,
        compiler_params=pltpu.CompilerParams(vmem_limit_bytes=56 << 20),
    )(*ws, *gs, *ms, *vs)
    return outs[:n], outs[n:2 * n], outs[2 * n:]


WEIGHT_NAMES = ("meta_tokens", "ffn1_pre_norm", "ffn1_w_gate", "ffn1_w_up", "ffn1_w_down", "ffn1_post_norm", "mix_pre_norm",
                "w_in", "gla_w_a2", "gla_b_a", "gla_out_norm", "swa_sinks", "swa_out_norm", "w_out", "mix_post_norm",
                "ffn2_pre_norm", "ffn2_w_gate", "ffn2_w_up", "ffn2_w_down", "ffn2_post_norm")
WIN_SHARD = D_IN // N_DEV
WIN_SHARD_PAD = 304
SLAB_VECTORS = ("ffn1_pre", "ffn1_post", "mix_pre", "mix_post", "ffn2_pre", "ffn2_post")
SLAB_ROWS = 32


def kernel(x, meta_tokens, ffn1_pre_norm, ffn1_w_gate, ffn1_w_up, ffn1_w_down, ffn1_post_norm, mix_pre_norm, w_in, gla_w_a2, gla_b_a, gla_out_norm, swa_sinks, swa_out_norm, w_out, mix_post_norm, ffn2_pre_norm, ffn2_w_gate, ffn2_w_up, ffn2_w_down, ffn2_post_norm, loss_target, m_meta_tokens, m_ffn1_pre_norm, m_ffn1_w_gate, m_ffn1_w_up, m_ffn1_w_down, m_ffn1_post_norm, m_mix_pre_norm, m_w_in, m_gla_w_a2, m_gla_b_a, m_gla_out_norm, m_swa_sinks, m_swa_out_norm, m_w_out, m_mix_post_norm, m_ffn2_pre_norm, m_ffn2_w_gate, m_ffn2_w_up, m_ffn2_w_down, m_ffn2_post_norm, v_meta_tokens, v_ffn1_pre_norm, v_ffn1_w_gate, v_ffn1_w_up, v_ffn1_w_down, v_ffn1_post_norm, v_mix_pre_norm, v_w_in, v_gla_w_a2, v_gla_b_a, v_gla_out_norm, v_swa_sinks, v_swa_out_norm, v_w_out, v_mix_post_norm, v_ffn2_pre_norm, v_ffn2_w_gate, v_ffn2_w_up, v_ffn2_w_down, v_ffn2_post_norm):
    given = dict(locals())
    W = {n: given[n] for n in WEIGHT_NAMES}
    M = {n: given["m_" + n] for n in WEIGHT_NAMES}
    V = {n: given["v_" + n] for n in WEIGHT_NAMES}
    dev = _dev_index(*_place_on_mesh())

    def t16(w):
        return w[0].T.astype(BF16)

    small = jnp.concatenate([W["meta_tokens"], jnp.pad(W["gla_w_a2"][0], ((0, 0), (0, 96)))], axis=0)
    wg1, wu1, wd1, small_g, gathered_zeros = _all_gather(
        [t16(W["ffn1_w_gate"]), t16(W["ffn1_w_up"]), W["ffn1_w_down"][0].astype(BF16), small])
    def after_zero(shard, zeros):
        return shard + zeros[0:1, 0:1].astype(shard.dtype)
    win_shard = jnp.pad(t16(W["w_in"]), ((0, WIN_SHARD_PAD - WIN_SHARD), (0, 0)))
    win_shard = after_zero(win_shard, gathered_zeros)
    mid = _exchange_start([win_shard], GATHER, "gather_w_in_start")
    late_shards = [after_zero(W["w_out"][0].astype(BF16), mid[3]), t16(W["ffn2_w_gate"]), t16(W["ffn2_w_up"]),
                   W["ffn2_w_down"][0].astype(BF16)]
    late = _exchange_start(late_shards, GATHER, "gather_late_weights_start")

    def late_weights(what, after):
        if what == "win":
            win_g, = _exchange_wait(mid, GATHER, after, "gather_w_in_wait")
            win_t = win_g.reshape(N_DEV, WIN_SHARD_PAD, D_MODEL)[:, :WIN_SHARD].reshape(D_IN, D_MODEL)
            return dict(win=_win_pad_rows(win_t))
        wout, wg2, wu2, wd2 = _exchange_wait(late, GATHER, after, "gather_late_weights_wait")
        return dict(wout=wout, wg2=wg2, wu2=wu2, wd2=wd2)

    small_g = small_g.reshape(N_DEV, 32, 128)
    meta_full = small_g[:, :N_META].transpose(1, 0, 2).reshape(N_META, D_MODEL)
    wa2_full = small_g[:, N_META:, :32].transpose(1, 0, 2).reshape(16, 256)
    w = dict(
        ffn1_pre=W["ffn1_pre_norm"] + late[3][0, 0], ffn1_post=W["ffn1_post_norm"], mix_pre=W["mix_pre_norm"],
        mix_post=W["mix_post_norm"], ffn2_pre=W["ffn2_pre_norm"], ffn2_post=W["ffn2_post_norm"], b_a=W["gla_b_a"],
        gla_norm=W["gla_out_norm"], sinks=W["swa_sinks"], swa_norm=W["swa_out_norm"], wg1=wg1, wu1=wu1, wd1=wd1,
        wa2=jnp.pad(wa2_full, ((0, 112), (0, 0))))

    in_flight = []

    def on_grads(group, grads):
        parts = []
        for nm, p in grads.items():
            if nm == "win":
                p = _win_unpad_rows(p).reshape(N_DEV, WIN_SHARD, D_MODEL)
                p = jnp.pad(p, ((0, 0), (0, WIN_SHARD_PAD - WIN_SHARD), (0, 0))).reshape(N_DEV * WIN_SHARD_PAD, D_MODEL)
            parts.append(p)
        kind = SCATTER_CHIPS if group.startswith("ffn1") else SCATTER
        if kind == SCATTER_CHIPS:
            parts = [_sibling_reduce(p, "pair_" + group) for p in parts]
        started = _exchange_start(parts, kind, "scatter_" + group + "_start")
        in_flight.append((group, list(grads), started, kind))
        return started[3][0, 0]

    front = jnp.concatenate([jnp.zeros((PAD_ROWS, D_MODEL), F32), meta_full], axis=0)
    loss, dh0, g = _local_step(x[0], loss_target[0], front, w, late_weights, on_grads)
    grad_x = dh0[BLK:][None]

    packed = jnp.concatenate([g["b_a"][0:1], g["gla_norm"][0:1], g["sinks"][0:1], g["swa_norm"][0:1]], axis=1)
    slab = jnp.concatenate([g[k][0:1] for k in SLAB_VECTORS] + [packed, jnp.full((1, D_MODEL), loss, F32),
                           g["wa2"][:16].reshape(4, D_MODEL), jnp.zeros((4, D_MODEL), F32), dh0[PAD_ROWS:BLK]], axis=0)
    tot = _all_reduce_small(slab)
    loss = tot[7, 0]
    small_grads = dict(
        ffn1_pre_norm=tot[0:1], ffn1_post_norm=tot[1:2], mix_pre_norm=tot[2:3], mix_post_norm=tot[3:4],
        ffn2_pre_norm=tot[4:5], ffn2_post_norm=tot[5:6], gla_b_a=tot[6:7, 0:256], gla_out_norm=tot[6:7, 256:384],
        swa_sinks=tot[6:7, 384:392], swa_out_norm=tot[6:7, 512:1024],
        gla_w_a2=lax.dynamic_slice_in_dim(tot[8:12].reshape(16, 256), dev * 32, 32, axis=1)[None],
        meta_tokens=lax.dynamic_slice_in_dim(tot[16:32], dev * 128, 128, axis=1))

    big = dict(wg1=("ffn1_w_gate", True), wu1=("ffn1_w_up", True), wd1=("ffn1_w_down", False), win=("w_in", True),
               wout=("w_out", False), wg2=("ffn2_w_gate", True), wu2=("ffn2_w_up", True), wd2=("ffn2_w_down", False))
    grads = dict(small_grads)
    delta, new_m, new_v = {}, {}, {}
    names = [n for n in WEIGHT_NAMES if n not in [full for full, _ in big.values()]]
    two_d = lambda a: a.reshape(-1, a.shape[-1])
    d_, m_, v_ = _adamw([two_d(W[n]) for n in names], [two_d(grads[n]) for n in names],
                        [two_d(M[n]) for n in names], [two_d(V[n]) for n in names], "adamw_small")
    for k, n in enumerate(names):
        delta[n], new_m[n], new_v[n] = d_[k].reshape(W[n].shape), m_[k].reshape(W[n].shape), v_[k].reshape(W[n].shape)

    before_wait = d_[0] + in_flight[-1][2][3][0, 0]
    for group, shorts, started, kind in in_flight:
        lands = _exchange_wait(started, kind, before_wait, "scatter_" + group + "_wait")
        for short, g_slab in zip(shorts, _sum_partials(lands, "sum_" + group, 4 if kind == SCATTER_CHIPS else N_DEV)):
            n, transposed = big[short]
            to_slab = (lambda a: a[0].T) if transposed else (lambda a: a[0])
            from_slab = (lambda a: a.T[None]) if transposed else (lambda a: a[None])
            g_slab = g_slab[:WIN_SHARD] if short == "win" else g_slab
            d_, m_, v_ = _adamw([to_slab(W[n])], [g_slab], [to_slab(M[n])], [to_slab(V[n])], "adamw_" + n)
            grads[n], delta[n], new_m[n], new_v[n] = from_slab(g_slab), from_slab(d_[0]), from_slab(m_[0]), from_slab(v_[0])
            before_wait = d_[0]
    return (loss, grad_x, *[grads[n] for n in WEIGHT_NAMES], *[delta[n] for n in WEIGHT_NAMES],
            *[new_m[n] for n in WEIGHT_NAMES], *[new_v[n] for n in WEIGHT_NAMES])
```

```python
import functools

import jax
import jax.numpy as jnp
from jax import lax
from jax.experimental import pallas as pl
from jax.experimental.pallas import tpu as pltpu

F32, BF16 = jnp.float32, jnp.bfloat16

D_MODEL = 1024
D_FF = 2816
N_META = 16
BLK = 128
PAD_ROWS = BLK - N_META
GLA_DK = 64
SWA_HD = 64
SWA_HEADS = 8
GLA_TAU = 16.0
NORM_EPS = 1e-6
NEG_INF = -1e30
ROPE_THETA = 10000.0
P_GQ, P_GK, P_GV, P_GG, P_SQ, P_SK, P_SV, P_GA, P_END = 0, 256, 512, 1024, 1536, 2048, 2176, 2304, 2432
D_IN = 2320
IN_SPLITS = (256, 256, 512, 512, 16, 512, 128, 128)
FF_TILE = 2816
WGRAD_TILE_MAX = 2432
N_DEV = 8
MESH = pl.DeviceIdType.MESH

ADAM_LR, ADAM_B1, ADAM_B2, ADAM_EPS, ADAM_WD, ADAM_STEP = 0.001, 0.9, 0.999, 1e-08, 0.01, 10

V7X_VMEM_BYTES = 64 << 20
VMEM_SPEC = pl.BlockSpec(memory_space=pltpu.VMEM)
SMEM_SPEC = pl.BlockSpec(memory_space=pltpu.SMEM)
ANY_SPEC = pl.BlockSpec(memory_space=pl.ANY)


def _params(semantics, vmem_mb=56):
    return pltpu.CompilerParams(dimension_semantics=semantics, vmem_limit_bytes=vmem_mb << 20)


def _row_tile(rows):
    return 320 if rows % 320 == 0 else BLK


def _nn(a, b):
    return lax.dot_general(a, b, (((1,), (0,)), ((), ())), preferred_element_type=F32)


def _nt(a, b):
    return lax.dot_general(a, b, (((1,), (1,)), ((), ())), preferred_element_type=F32)


def _tn(a, b):
    return lax.dot_general(a, b, (((0,), (0,)), ((), ())), preferred_element_type=F32)


def _rms(x):
    r = lax.rsqrt(jnp.mean(x * x, axis=-1, keepdims=True) + NORM_EPS)
    return x * r, r


def _rms_bwd(xn, r, w, dy):
    g = dy * w
    return r * (g - xn * jnp.mean(g * xn, axis=-1, keepdims=True))


def _sigmoid(x):
    return 1.0 / (1.0 + jnp.exp(-x))


def _colsum(x):
    return jnp.sum(x, axis=0, keepdims=True)


def _split_bf16(x):
    hi = x.astype(BF16)
    lo = (x - hi.astype(F32)).astype(BF16)
    return hi, lo


def _tri(lower):
    r = lax.broadcasted_iota(jnp.int32, (BLK, BLK), 0)
    c = lax.broadcasted_iota(jnp.int32, (BLK, BLK), 1)
    return (r >= c) if lower else (c >= r)


def _half_mask(width, half):
    lane = lax.broadcasted_iota(jnp.int32, (1, width), 1)
    return ((lane % 128) < 64) if half == 0 else ((lane % 128) >= 64)


def _rot_half(x):
    w = x.shape[-1]
    lane = lax.broadcasted_iota(jnp.int32, (1, w), 1)
    return jnp.where((lane % SWA_HD) < SWA_HD // 2, -pltpu.roll(x, w - SWA_HD // 2, 1), pltpu.roll(x, SWA_HD // 2, 1))


def _row_spec(tm, cols):
    return pl.BlockSpec((tm, cols), lambda i: (i, 0))


def _acc_spec(cols):
    return pl.BlockSpec((8, cols), lambda i: (0, 0))


def _acc_add(ref, first, value):
    @pl.when(first)
    def _():
        ref[...] = jnp.zeros_like(ref)
    ref[0:1, :] += value


def _behind_front(ref, i, tm, front):
    blk = ref[...]
    return jnp.where(i == 0, jnp.concatenate([front, blk[0:tm - BLK]], axis=0), blk)


def _ffn_fwd(h, gpre, wg_t, wu_t, wd, gpost, tgt=None, front=None):
    with_loss, with_front = tgt is not None, front is not None
    rows = h.shape[0] + (BLK if with_front else 0)
    tm = _row_tile(rows)
    nf = D_FF // FF_TILE

    def body(*refs):
        refs = list(refs)
        h_ref, gpre_ref, wg_ref, wu_ref, wd_ref, gpost_ref = refs[:6]
        del refs[:6]
        front_ref = refs.pop(0) if with_front else None
        t_ref = refs.pop(0) if with_loss else None
        h0_ref = refs.pop(0) if with_front else None
        ho_ref, a_ref, b_ref, f_ref = refs[:4]
        dy_ref, loss_ref = refs[4:6] if with_loss else (None, None)
        acc = refs[-1]
        i = pl.program_id(0)
        if with_front:
            h_in = _behind_front(h_ref, i, tm, front_ref[...])
            h0_ref[...] = h_in
        else:
            h_in = h_ref[...]
        hn, _ = _rms(h_in)
        n16 = (hn * gpre_ref[...]).astype(BF16)
        for j in range(nf):
            cols = slice(j * FF_TILE, (j + 1) * FF_TILE)
            a = _nt(n16, wg_ref[cols, :])
            b = _nt(n16, wu_ref[cols, :])
            a_ref[:, cols] = a.astype(BF16)
            b_ref[:, cols] = b.astype(BF16)
            s16 = (a * _sigmoid(a) * b).astype(BF16)
            part = _nn(s16, wd_ref[cols, :])
            if j == 0:
                acc[...] = part
            else:
                acc[...] += part
        f = acc[...]
        f_ref[...] = f
        fn, _ = _rms(f)
        y = h_in + 0.5 * (fn * gpost_ref[...])
        ho_ref[...] = y
        if with_loss:
            row = i * tm + lax.broadcasted_iota(jnp.int32, (tm, 1), 0)
            err = jnp.where(row >= BLK, y - _behind_front(t_ref, i, tm, jnp.zeros((BLK, D_MODEL), F32)), 0.0)
            dy_ref[...] = err * (1.0 / D_MODEL)
            part = 0.5 * jnp.sum(jnp.sum(err * err, axis=-1, keepdims=True) * (1.0 / D_MODEL), axis=0, keepdims=True)

            @pl.when(i == 0)
            def _():
                loss_ref[...] = jnp.zeros_like(loss_ref)
            loss_ref[...] += part

    row_f32 = _row_spec(tm, D_MODEL)
    behind = pl.BlockSpec((pl.Element(tm), pl.Element(D_MODEL)),
                          lambda i: (pl.multiple_of(jnp.maximum(i * tm - BLK, 0), 64), 0))
    in_specs = [behind if with_front else row_f32, VMEM_SPEC, VMEM_SPEC, VMEM_SPEC, VMEM_SPEC, VMEM_SPEC]
    out_specs = [row_f32, _row_spec(tm, D_FF), _row_spec(tm, D_FF), row_f32]
    out_shape = [jax.ShapeDtypeStruct((rows, D_MODEL), F32), jax.ShapeDtypeStruct((rows, D_FF), BF16),
                 jax.ShapeDtypeStruct((rows, D_FF), BF16), jax.ShapeDtypeStruct((rows, D_MODEL), F32)]
    args = [h, gpre, wg_t, wu_t, wd, gpost]
    if with_front:
        in_specs.append(VMEM_SPEC)
        args.append(front)
        out_specs.insert(0, row_f32)
        out_shape.insert(0, jax.ShapeDtypeStruct((rows, D_MODEL), F32))
    if with_loss:
        in_specs.append(behind)
        args.append(tgt)
        out_specs += [row_f32, pl.BlockSpec((8, 128), lambda i: (0, 0))]
        out_shape += [jax.ShapeDtypeStruct((rows, D_MODEL), F32), jax.ShapeDtypeStruct((8, 128), F32)]
    return pl.pallas_call(
        body, name="ffn_fwd_loss" if with_loss else "ffn_fwd", grid=(rows // tm,),
        in_specs=in_specs, out_specs=out_specs, out_shape=out_shape,
        scratch_shapes=[pltpu.VMEM((tm, D_MODEL), F32)],
        compiler_params=_params(("arbitrary",)),
    )(*args)


def _ffn_bwd_act(dh_out, h, a, b, f, gpre, gpost, wg_t, wu_t, wd, name):
    rows = h.shape[0]
    tm = _row_tile(rows)
    nf = D_FF // FF_TILE

    def body(dho_ref, h_ref, a_ref, b_ref, f_ref, gpre_ref, gpost_ref, wg_ref, wu_ref, wd_ref,
             dh_ref, da_ref, db_ref, df_ref, n_ref, dgpre_ref, dgpost_ref, acc):
        first = pl.program_id(0) == 0
        dho = dho_ref[...]
        drr = 0.5 * dho
        fn, rf = _rms(f_ref[...])
        _acc_add(dgpost_ref, first, _colsum(drr * fn))
        df16 = _rms_bwd(fn, rf, gpost_ref[...], drr).astype(BF16)
        df_ref[...] = df16
        hn, rh = _rms(h_ref[...])
        n_ref[...] = (hn * gpre_ref[...]).astype(BF16)
        for j in range(nf):
            cols = slice(j * FF_TILE, (j + 1) * FF_TILE)
            ds = _nt(df16, wd_ref[cols, :])
            av = a_ref[:, cols].astype(F32)
            bv = b_ref[:, cols].astype(F32)
            sg = _sigmoid(av)
            db16 = (ds * (av * sg)).astype(BF16)
            da16 = (ds * bv * (sg * (1.0 + av * (1.0 - sg)))).astype(BF16)
            da_ref[:, cols] = da16
            db_ref[:, cols] = db16
            part = _nn(da16, wg_ref[cols, :]) + _nn(db16, wu_ref[cols, :])
            if j == 0:
                acc[...] = part
            else:
                acc[...] += part
        dn = acc[...]
        _acc_add(dgpre_ref, first, _colsum(dn * hn))
        dh_ref[...] = dho + _rms_bwd(hn, rh, gpre_ref[...], dn)

    row_f32 = _row_spec(tm, D_MODEL)
    row_ff = _row_spec(tm, D_FF)
    return pl.pallas_call(
        body, name=name, grid=(rows // tm,),
        in_specs=[row_f32, row_f32, row_ff, row_ff, row_f32, VMEM_SPEC, VMEM_SPEC, VMEM_SPEC, VMEM_SPEC, VMEM_SPEC],
        out_specs=[row_f32, row_ff, row_ff, row_f32, row_f32, _acc_spec(D_MODEL), _acc_spec(D_MODEL)],
        out_shape=[jax.ShapeDtypeStruct((rows, D_MODEL), F32), jax.ShapeDtypeStruct((rows, D_FF), BF16),
                   jax.ShapeDtypeStruct((rows, D_FF), BF16), jax.ShapeDtypeStruct((rows, D_MODEL), BF16),
                   jax.ShapeDtypeStruct((rows, D_MODEL), BF16), jax.ShapeDtypeStruct((8, D_MODEL), F32),
                   jax.ShapeDtypeStruct((8, D_MODEL), F32)],
        scratch_shapes=[pltpu.VMEM((tm, D_MODEL), F32)],
        compiler_params=_params(("arbitrary",)),
    )(dh_out, h, a, b, f, gpre, gpost, wg_t, wu_t, wd)


def _wgrad(lhs, rhs, name, gate=None):
    rows, width = lhs.shape
    tm = rows if rows % 1664 == 0 else BLK
    tf = 256 if width % 256 == 0 else 128
    nr = rows // tm
    gated = gate is not None

    def body(*refs):
        if gated:
            g_ref, l_ref, r_ref, o_ref, acc = refs
            gv = g_ref[...].astype(F32)
            lv = (gv * _sigmoid(gv) * l_ref[...].astype(F32)).astype(BF16)
        else:
            l_ref, r_ref, o_ref, acc = refs
            lv = l_ref[...]
        i = pl.program_id(1)
        part = _tn(lv, r_ref[...])

        @pl.when(i == 0)
        def _():
            acc[...] = part

        @pl.when(i > 0)
        def _():
            acc[...] += part

        @pl.when(i == nr - 1)
        def _():
            o_ref[...] = acc[...].astype(BF16)

    l_spec = pl.BlockSpec((tm, tf), lambda j, i: (i, j))
    r_spec = pl.BlockSpec((tm, D_MODEL), lambda j, i: (i, 0))
    return pl.pallas_call(
        body, name=name, grid=(width // tf, nr),
        in_specs=([l_spec] if gated else []) + [l_spec, r_spec],
        out_specs=pl.BlockSpec((tf, D_MODEL), lambda j, i: (j, 0)),
        out_shape=jax.ShapeDtypeStruct((width, D_MODEL), BF16),
        scratch_shapes=[pltpu.VMEM((tf, D_MODEL), F32)],
        compiler_params=_params(("arbitrary", "arbitrary")),
    )(*([gate] if gated else []), lhs, rhs)


def _chunk_cumsum(x, lower):
    tri = jnp.where(_tri(lower), 1.0, 0.0).astype(BF16)
    hi, lo = _split_bf16(x)
    return _nn(tri, hi) + _nn(tri, lo)


def _mix_in(h, g, win_p, wa2_p, b_a, cos, sin):
    rows = h.shape[0]
    tm = 640 if rows % 640 == 0 else BLK

    def body(h_ref, g_ref, win_ref, wa2_ref, ba_ref, cos_ref, sin_ref,
             gq_ref, gk_ref, gv_ref, gg_ref, sq_ref, sk_ref, sv_ref, ga_ref, loga_ref, bc_ref, n_ref):
        hn, _ = _rms(h_ref[...])
        n16 = (hn * g_ref[...]).astype(BF16)
        n_ref[...] = n16
        proj = _nt(n16, win_ref[...])
        gq_ref[...] = proj[:, P_GQ:P_GK]
        gk_ref[...] = proj[:, P_GK:P_GV]
        gv_ref[...] = proj[:, P_GV:P_GG].astype(BF16)
        gg_ref[...] = proj[:, P_GG:P_SQ]
        c1, s1 = cos_ref[...], sin_ref[...]
        c4 = jnp.concatenate([c1, c1, c1, c1], axis=1)
        s4 = jnp.concatenate([s1, s1, s1, s1], axis=1)
        sq = proj[:, P_SQ:P_SK]
        sk = proj[:, P_SK:P_SV]
        sq_ref[...] = (sq * c4 + _rot_half(sq) * s4).astype(BF16)
        sk_ref[...] = (sk * c1 + _rot_half(sk) * s1).astype(BF16)
        sv_ref[...] = proj[:, P_SV:P_GA].astype(BF16)
        ga = proj[:, P_GA:P_END]
        ga_ref[...] = ga
        z = _nn(ga, wa2_ref[...]) + ba_ref[...]
        loga = (jnp.minimum(z, 0.0) - jnp.log(1.0 + jnp.exp(-jnp.abs(z)))) * (1.0 / GLA_TAU)
        loga_ref[...] = loga
        for c in range(tm // BLK):
            rs = slice(c * BLK, (c + 1) * BLK)
            bc_ref[rs, :] = _chunk_cumsum(loga[rs, :], True)

    f32 = lambda c: jax.ShapeDtypeStruct((rows, c), F32)
    b16 = lambda c: jax.ShapeDtypeStruct((rows, c), BF16)
    rs = lambda c: _row_spec(tm, c)
    return pl.pallas_call(
        body, name="mix_in", grid=(rows // tm,),
        in_specs=[rs(D_MODEL), VMEM_SPEC, VMEM_SPEC, VMEM_SPEC, VMEM_SPEC, rs(128), rs(128)],
        out_specs=[rs(256), rs(256), rs(512), rs(512), rs(512), rs(128), rs(128), rs(128), rs(256), rs(256), rs(D_MODEL)],
        out_shape=[f32(256), f32(256), b16(512), f32(512), b16(512), b16(128), b16(128), f32(128), f32(256), f32(256),
                   b16(D_MODEL)],
        compiler_params=_params(("arbitrary",)),
    )(h, g, win_p, wa2_p, b_a, cos, sin)


def _gla_factors(q, k, bc):
    bm = bc[BLK // 2 - 1:BLK // 2, :]
    bl = bc[BLK - 1:BLK, :]
    e_q, e_k, e_qe, e_kd = jnp.exp(bc - bm), jnp.exp(bm - bc), jnp.exp(bc), jnp.exp(bl - bc)
    return (q * e_q, k * e_k, q * e_qe, k * e_kd), (e_q, e_k, e_qe, e_kd), jnp.exp(bl)


def _gla_fwd(gq, gk, gv, gg, bc, wgn):
    rows = gq.shape[0]
    nc = rows // BLK
    scale = GLA_DK ** -0.5

    def body(q_ref, k_ref, v_ref, gg_ref, bc_ref, wgn_ref, o_ref, cat_ref, sp_ref, st):
        @pl.when(pl.program_id(0) == 0)
        def _():
            st[...] = jnp.zeros_like(st)
        low = _tri(True)
        wgn_v = wgn_ref[...]
        for p in range(2):
            sl = slice(128 * p, 128 * p + 128)
            (qt, kt, qe, kd), _, ebl = _gla_factors(q_ref[:, sl] * scale, k_ref[:, sl], bc_ref[:, sl])
            s_prev = st[p]
            sp_ref[0, p] = s_prev
            s16 = s_prev.astype(BF16)
            qt16 = qt.astype(BF16)
            s_new = s_prev * ebl
            for hh in range(2):
                hs = slice(128 * (2 * p + hh), 128 * (2 * p + hh) + 128)
                lm = _half_mask(128, hh)
                vh = v_ref[:, hs]
                pm = jnp.where(low, _nt(qt16, jnp.where(lm, kt, 0.0).astype(BF16)), 0.0)
                o = _nn(pm.astype(BF16), vh) + _nt(jnp.where(lm, qe, 0.0).astype(BF16), s16)
                s_new = s_new + _tn(vh, jnp.where(lm, kd, 0.0).astype(BF16))
                o_ref[:, hs] = o
                on, _ = _rms(o)
                gate = gg_ref[:, hs]
                cat_ref[:, hs] = (on * wgn_v * (gate * _sigmoid(gate))).astype(BF16)
            st[p] = s_new

    rs = lambda c: _row_spec(BLK, c)
    return pl.pallas_call(
        body, name="gla_fwd", grid=(nc,),
        in_specs=[rs(256), rs(256), rs(512), rs(512), rs(256), VMEM_SPEC],
        out_specs=[rs(512), rs(512), pl.BlockSpec((1, 2, 128, 128), lambda i: (i, 0, 0, 0))],
        out_shape=[jax.ShapeDtypeStruct((rows, 512), F32), jax.ShapeDtypeStruct((rows, 512), BF16),
                   jax.ShapeDtypeStruct((nc, 2, 128, 128), F32)],
        scratch_shapes=[pltpu.VMEM((2, 128, 128), F32)],
        compiler_params=_params(("arbitrary",)),
    )(gq, gk, gv, gg, bc, wgn)


def _gla_bwd(dcat, o_all, gq, gk, gv, gg, bc, sp, wgn):
    rows = gq.shape[0]
    nc = rows // BLK
    scale = GLA_DK ** -0.5

    def body(dc_ref, o_ref, q_ref, k_ref, v_ref, gg_ref, bc_ref, sp_ref, wgn_ref,
             dq_ref, dk_ref, dv_ref, dgg_ref, dla_ref, dwgn_ref, dst):
        first = pl.program_id(0) == 0

        @pl.when(first)
        def _():
            dst[...] = jnp.zeros_like(dst)
        low, upp = _tri(True), _tri(False)
        last_row = lax.broadcasted_iota(jnp.int32, (BLK, 1), 0) == BLK - 1
        wgn_v = wgn_ref[...]
        dwgn = jnp.zeros((1, 128), F32)
        for p in range(2):
            sl = slice(128 * p, 128 * p + 128)
            (qt, kt, qe, kd), (e_q, e_k, e_qe, e_kd), ebl = _gla_factors(
                q_ref[:, sl] * scale, k_ref[:, sl], bc_ref[:, sl])
            s_prev = sp_ref[0, p]
            s16 = s_prev.astype(BF16)
            ds_next = dst[p]
            ds16 = ds_next.astype(BF16)
            qt16 = qt.astype(BF16)
            ds_new = ds_next * ebl
            dqt = jnp.zeros((BLK, 128), F32)
            dkt = jnp.zeros((BLK, 128), F32)
            dqe = jnp.zeros((BLK, 128), F32)
            dkd = jnp.zeros((BLK, 128), F32)
            for hh in range(2):
                hs = slice(128 * (2 * p + hh), 128 * (2 * p + hh) + 128)
                lm = _half_mask(128, hh)
                on, ro = _rms(o_ref[:, hs])
                gate = gg_ref[:, hs]
                sg = _sigmoid(gate)
                si = gate * sg
                dog = dc_ref[:, hs]
                dwgn = dwgn + _colsum(dog * si * on)
                dgg_ref[:, hs] = dog * (on * wgn_v) * (sg * (1.0 + gate * (1.0 - sg)))
                do16 = _rms_bwd(on, ro, wgn_v, dog * si).astype(BF16)
                vh = v_ref[:, hs]
                ktm16 = jnp.where(lm, kt, 0.0).astype(BF16)
                qtm16 = jnp.where(lm, qt, 0.0).astype(BF16)
                qem16 = jnp.where(lm, qe, 0.0).astype(BF16)
                kdm16 = jnp.where(lm, kd, 0.0).astype(BF16)
                p_t = jnp.where(upp, _nt(ktm16, qt16), 0.0)
                dp_t = jnp.where(upp, _nt(vh, do16), 0.0)
                dp = jnp.where(low, _nt(do16, vh), 0.0)
                dv_ref[:, hs] = _nn(p_t.astype(BF16), do16) + _nt(kdm16, ds16)
                dqt = dqt + _nn(dp.astype(BF16), ktm16)
                dkt = dkt + _nn(dp_t.astype(BF16), qtm16)
                dqe = dqe + jnp.where(lm, _nn(do16, s16), 0.0)
                dkd = dkd + jnp.where(lm, _nn(vh, ds16), 0.0)
                ds_new = ds_new + _tn(do16, qem16)
            debl = _colsum(ds_next * s_prev)
            dq_ref[:, sl] = (dqt * e_q + dqe * e_qe) * scale
            dk_ref[:, sl] = dkt * e_k + dkd * e_kd
            dkd_kd = dkd * kd
            db = dqt * qt - dkt * kt + dqe * qe - dkd_kd
            db = jnp.where(last_row, db + (_colsum(dkd_kd) + debl * ebl), db)
            dla_ref[:, sl] = _chunk_cumsum(db, False)
            dst[p] = ds_new
        _acc_add(dwgn_ref, first, dwgn)

    rev = lambda c: pl.BlockSpec((BLK, c), lambda i: (nc - 1 - i, 0))
    f32 = lambda c: jax.ShapeDtypeStruct((rows, c), F32)
    return pl.pallas_call(
        body, name="gla_bwd", grid=(nc,),
        in_specs=[rev(512), rev(512), rev(256), rev(256), rev(512), rev(512), rev(256),
                  pl.BlockSpec((1, 2, 128, 128), lambda i: (nc - 1 - i, 0, 0, 0)), VMEM_SPEC],
        out_specs=[rev(256), rev(256), rev(512), rev(512), rev(256), _acc_spec(128)],
        out_shape=[f32(256), f32(256), f32(512), f32(512), f32(256), jax.ShapeDtypeStruct((8, 128), F32)],
        scratch_shapes=[pltpu.VMEM((2, 128, 128), F32)],
        compiler_params=_params(("arbitrary",)),
    )(dcat, o_all, gq, gk, gv, gg, bc, sp, wgn)


def _swa_masks(i):
    t = lax.broadcasted_iota(jnp.int32, (BLK, BLK), 0)
    c = lax.broadcasted_iota(jnp.int32, (BLK, BLK), 1)
    own_side = c <= t
    band_ok = i >= jnp.where(own_side, 1, 2)
    meta_ok = (c % N_META) <= jnp.where(i >= 1, N_META, t - PAD_ROWS)
    return own_side, band_ok, meta_ok, c // N_META


def _swa_blocks(ref, i):
    prev = pl.multiple_of(jnp.maximum(i - 1, 0) * BLK, BLK)
    own = pl.multiple_of(i * BLK, BLK)
    return jnp.concatenate([ref[pl.ds(prev, BLK), :], ref[pl.ds(own, BLK), :]], axis=0), prev, own


def _swa_meta_operand(ref):
    blk = ref[0:BLK, :]
    swapped = pltpu.roll(blk, 64, 1)
    lo = jnp.where(_half_mask(128, 0), blk, swapped)
    hi = jnp.where(_half_mask(128, 1), blk, swapped)
    meta = jnp.concatenate([lo, lo, hi, hi], axis=1)[PAD_ROWS:BLK, :]
    tiled = jnp.concatenate([meta] * SWA_HEADS, axis=0)
    j = lax.broadcasted_iota(jnp.int32, tiled.shape, 0)
    lane = lax.broadcasted_iota(jnp.int32, tiled.shape, 1)
    return jnp.where(j // N_META == lane // SWA_HD, tiled, jnp.zeros_like(tiled))


def _swa_meta_fold(acc):
    out = jnp.zeros((N_META, 128), F32)
    for hd in range(SWA_HEADS):
        half, kv = hd % 2, hd // 4
        piece = acc[N_META * hd:N_META * (hd + 1), 128 * (hd // 2):128 * (hd // 2) + 128]
        piece = jnp.where(_half_mask(128, half), piece, 0.0)
        out = out + (piece if half == kv else pltpu.roll(piece, 64, 1))
    return out


def _by_head(group, per_head):
    out = jnp.zeros((BLK, BLK), F32)
    for hd, v in enumerate(per_head):
        out = jnp.where(group == hd, v, out)
    return out


def _place(x, kv):
    if kv == 0:
        lo = jnp.where(_half_mask(128, 0), x, jnp.zeros_like(x))
        return lo, pltpu.roll(lo, 64, 1)
    hi = jnp.where(_half_mask(128, 1), x, jnp.zeros_like(x))
    return pltpu.roll(hi, 64, 1), hi


def _swa_fwd(sq, sk, sv, sinks, wn):
    rows = sq.shape[0]
    nb = rows // BLK
    scale = SWA_HD ** -0.5

    def body(q_ref, k_ref, v_ref, sink_ref, wn_ref, o_ref, cat_ref, lse_ref, kp, vp):
        i = pl.program_id(0)

        @pl.when(i == 0)
        def _():
            kp[...] = _swa_meta_operand(k_ref)
            vp[...] = _swa_meta_operand(v_ref)
        own_side, band_ok, meta_ok, group = _swa_masks(i)
        k2, _, _ = _swa_blocks(k_ref, i)
        v2, _, _ = _swa_blocks(v_ref, i)
        kz = (_place(k2, 0), _place(k2, 1))
        vz = (_place(v2, 0), _place(v2, 1))
        q_all = q_ref[...]
        s_meta = jnp.where(meta_ok, _nt(q_all, kp[...]) * scale, NEG_INF)
        s_band, m = [], []
        for hd in range(SWA_HEADS):
            kv, half = hd // 4, hd % 2
            q_pair = q_all[:, 128 * (hd // 2):128 * (hd // 2) + 128]
            s2 = _nt(q_pair, kz[kv][half])
            s = jnp.where(band_ok, jnp.where(own_side, s2[:, BLK:], s2[:, :BLK]) * scale, NEG_INF)
            top = jnp.maximum(jnp.max(s, axis=-1, keepdims=True),
                              jnp.max(jnp.where(group == hd, s_meta, NEG_INF), axis=-1, keepdims=True))
            s_band.append(s)
            m.append(jnp.maximum(top, sink_ref[0, hd]))
        e_meta = jnp.exp(s_meta - _by_head(group, m))
        o_meta = _nn(e_meta.astype(BF16), vp[...])
        outs = []
        for pr in range(4):
            o_pair = o_meta[:, 128 * pr:128 * pr + 128]
            rden = []
            for half in range(2):
                hd = 2 * pr + half
                kv = hd // 4
                e = jnp.exp(s_band[hd] - m[hd])
                den = (jnp.sum(e, axis=-1, keepdims=True)
                       + jnp.sum(jnp.where(group == hd, e_meta, 0.0), axis=-1, keepdims=True)
                       + jnp.exp(sink_ref[0, hd] - m[hd]))
                lse_ref[:, hd:hd + 1] = m[hd] + jnp.log(den)
                rden.append(1.0 / den)
                e2 = jnp.concatenate([jnp.where(own_side, 0.0, e), jnp.where(own_side, e, 0.0)], axis=1).astype(BF16)
                o_pair = o_pair + _nn(e2, vz[kv][half])
            outs.append(o_pair * jnp.where(_half_mask(128, 0), rden[0], rden[1]))
        o = jnp.concatenate(outs, axis=1)
        o_ref[...] = o
        on, _ = _rms(o)
        cat_ref[...] = (on * wn_ref[...]).astype(BF16)

    return pl.pallas_call(
        body, name="swa_fwd", grid=(nb,),
        in_specs=[_row_spec(BLK, 512), VMEM_SPEC, VMEM_SPEC, SMEM_SPEC, VMEM_SPEC],
        out_specs=[_row_spec(BLK, 512), _row_spec(BLK, 512), _row_spec(BLK, SWA_HEADS)],
        out_shape=[jax.ShapeDtypeStruct((rows, 512), F32), jax.ShapeDtypeStruct((rows, 512), BF16),
                   jax.ShapeDtypeStruct((rows, SWA_HEADS), F32)],
        scratch_shapes=[pltpu.VMEM((BLK, 512), BF16), pltpu.VMEM((BLK, 512), BF16)],
        compiler_params=_params(("arbitrary",)),
    )(sq, sk, sv, sinks, wn)


def _swa_bwd(dcat, o_all, sq, sk, sv, lse, sinks, wn):
    rows = sq.shape[0]
    nb = rows // BLK
    scale = SWA_HD ** -0.5

    def body(dc_ref, o_ref, q_ref, k_ref, v_ref, lse_ref, sink_ref, wn_ref, dq_ref, dk_ref, dv_ref, dsink_ref, dwn_ref,
             kp, vp, dkp, dvp):
        i = pl.program_id(0)
        first = i == 0

        @pl.when(first)
        def _():
            dk_ref[...] = jnp.zeros_like(dk_ref)
            dv_ref[...] = jnp.zeros_like(dv_ref)
            dkp[...] = jnp.zeros_like(dkp)
            dvp[...] = jnp.zeros_like(dvp)
            kp[...] = _swa_meta_operand(k_ref)
            vp[...] = _swa_meta_operand(v_ref)
        own_side, band_ok, meta_ok, group = _swa_masks(i)
        k2, prev, own = _swa_blocks(k_ref, i)
        v2, _, _ = _swa_blocks(v_ref, i)
        kz = (_place(k2, 0), _place(k2, 1))
        vz = (_place(v2, 0), _place(v2, 1))
        o = o_ref[...]
        on, ro = _rms(o)
        dc = dc_ref[...]
        _acc_add(dwn_ref, first, _colsum(dc * on))
        do = _rms_bwd(on, ro, wn_ref[...], dc)
        do_o = do * o
        do16 = do.astype(BF16)
        q_all = q_ref[...]
        lse = [lse_ref[:, hd:hd + 1] for hd in range(SWA_HEADS)]
        delta = [jnp.sum(jnp.where(_half_mask(128, hd % 2), do_o[:, 128 * (hd // 2):128 * (hd // 2) + 128], 0.0),
                         axis=-1, keepdims=True) for hd in range(SWA_HEADS)]
        s_meta = jnp.where(meta_ok, _nt(q_all, kp[...]) * scale, NEG_INF)
        p_meta = jnp.exp(s_meta - _by_head(group, lse))
        ds_meta16 = (p_meta * (_nt(do16, vp[...]) - _by_head(group, delta)) * scale).astype(BF16)
        dq_meta = _nn(ds_meta16, kp[...])
        dkp[...] += _tn(ds_meta16, q_all)
        dvp[...] += _tn(p_meta.astype(BF16), do16)
        lane8 = lax.broadcasted_iota(jnp.int32, (1, 128), 1)
        dsink = jnp.zeros((1, 128), F32)
        dk2 = [[jnp.zeros((2 * BLK, 128), F32) for _ in range(2)] for _ in range(2)]
        dv2 = [[jnp.zeros((2 * BLK, 128), F32) for _ in range(2)] for _ in range(2)]
        dqs = []
        for pr in range(4):
            ps = slice(128 * pr, 128 * pr + 128)
            q_pair = q_all[:, ps]
            do_pair = do16[:, ps]
            dq_pair = dq_meta[:, ps]
            for half in range(2):
                hd = 2 * pr + half
                kv = hd // 4

                def window(x2):
                    return jnp.where(own_side, x2[:, BLK:], x2[:, :BLK])

                def unwindow(x):
                    return jnp.concatenate([jnp.where(own_side, 0.0, x), jnp.where(own_side, x, 0.0)], axis=1).astype(BF16)
                s = jnp.where(band_ok, window(_nt(q_pair, kz[kv][half])) * scale, NEG_INF)
                prob = jnp.exp(s - lse[hd])
                dsink = dsink + jnp.where(lane8 == hd, -jnp.sum(jnp.exp(sink_ref[0, hd] - lse[hd]) * delta[hd]), 0.0)
                ds2 = unwindow(prob * (window(_nt(do_pair, vz[kv][half])) - delta[hd]) * scale)
                dq_pair = dq_pair + _nn(ds2, kz[kv][half])
                dk2[kv][half] = dk2[kv][half] + _tn(ds2, q_pair)
                dv2[kv][half] = dv2[kv][half] + _tn(unwindow(prob), do_pair)
            dqs.append(dq_pair)
        dq_ref[...] = jnp.concatenate(dqs, axis=1)
        _acc_add(dsink_ref, first, dsink)
        for ref, acc2 in ((dk_ref, dk2), (dv_ref, dv2)):
            tot = jnp.zeros((2 * BLK, 128), F32)
            for kv in range(2):
                for half in range(2):
                    part = jnp.where(_half_mask(128, half), acc2[kv][half], 0.0)
                    tot = tot + (part if half == kv else pltpu.roll(part, 64, 1))
            ref[pl.ds(prev, BLK), :] += tot[:BLK]
            ref[pl.ds(own, BLK), :] += tot[BLK:]

        @pl.when(i == nb - 1)
        def _():
            dk_ref[PAD_ROWS:BLK, :] += _swa_meta_fold(dkp[...])
            dv_ref[PAD_ROWS:BLK, :] += _swa_meta_fold(dvp[...])

    full = pl.BlockSpec((rows, 128), lambda i: (0, 0))
    return pl.pallas_call(
        body, name="swa_bwd", grid=(nb,),
        in_specs=[_row_spec(BLK, 512), _row_spec(BLK, 512), _row_spec(BLK, 512), VMEM_SPEC, VMEM_SPEC,
                  _row_spec(BLK, SWA_HEADS), SMEM_SPEC, VMEM_SPEC],
        out_specs=[_row_spec(BLK, 512), full, full, _acc_spec(128), _acc_spec(512)],
        out_shape=[jax.ShapeDtypeStruct((rows, 512), F32), jax.ShapeDtypeStruct((rows, 128), F32),
                   jax.ShapeDtypeStruct((rows, 128), F32), jax.ShapeDtypeStruct((8, 128), F32),
                   jax.ShapeDtypeStruct((8, 512), F32)],
        scratch_shapes=[pltpu.VMEM((BLK, 512), BF16), pltpu.VMEM((BLK, 512), BF16),
                        pltpu.VMEM((BLK, 512), F32), pltpu.VMEM((BLK, 512), F32)],
        compiler_params=_params(("arbitrary",)),
    )(dcat, o_all, sq, sk, sv, lse, sinks, wn)


def _mix_out(h, cat_g, cat_s, wout, gpost):
    rows = h.shape[0]
    tm = _row_tile(rows)

    def body(h_ref, cg_ref, cs_ref, w_ref, g_ref, ho_ref, m_ref):
        m = _nn(cg_ref[...], w_ref[0:512, :]) + _nn(cs_ref[...], w_ref[512:1024, :])
        m_ref[...] = m
        mn, _ = _rms(m)
        ho_ref[...] = h_ref[...] + mn * g_ref[...]

    row_f32 = _row_spec(tm, D_MODEL)
    return pl.pallas_call(
        body, name="mix_out", grid=(rows // tm,),
        in_specs=[row_f32, _row_spec(tm, 512), _row_spec(tm, 512), VMEM_SPEC, VMEM_SPEC],
        out_specs=[row_f32, row_f32],
        out_shape=[jax.ShapeDtypeStruct((rows, D_MODEL), F32), jax.ShapeDtypeStruct((rows, D_MODEL), F32)],
        compiler_params=_params(("arbitrary",)),
    )(h, cat_g, cat_s, wout, gpost)


def _mix_out_bwd(dh, m, wout, gpost):
    rows = dh.shape[0]
    tm = _row_tile(rows)

    def body(dh_ref, m_ref, w_ref, g_ref, dcg_ref, dcs_ref, dm_ref, dg_ref):
        first = pl.program_id(0) == 0
        dhv = dh_ref[...]
        mn, rm = _rms(m_ref[...])
        _acc_add(dg_ref, first, _colsum(dhv * mn))
        dm16 = _rms_bwd(mn, rm, g_ref[...], dhv).astype(BF16)
        dm_ref[...] = dm16
        dcat = _nt(dm16, w_ref[...])
        dcg_ref[...] = dcat[:, 0:512]
        dcs_ref[...] = dcat[:, 512:1024]

    row_f32 = _row_spec(tm, D_MODEL)
    return pl.pallas_call(
        body, name="mix_out_bwd", grid=(rows // tm,),
        in_specs=[row_f32, row_f32, VMEM_SPEC, VMEM_SPEC],
        out_specs=[_row_spec(tm, 512), _row_spec(tm, 512), row_f32, _acc_spec(D_MODEL)],
        out_shape=[jax.ShapeDtypeStruct((rows, 512), F32), jax.ShapeDtypeStruct((rows, 512), F32),
                   jax.ShapeDtypeStruct((rows, D_MODEL), BF16), jax.ShapeDtypeStruct((8, D_MODEL), F32)],
        compiler_params=_params(("arbitrary",)),
    )(dh, m, wout, gpost)


def _mix_in_bwd(dh_out, h, g, win_p, wa2_p, cos, sin, loga, ga, dgq, dgk, dgv, dgg, dsq, dsk, dsv, dloga):
    rows = h.shape[0]
    tm = _row_tile(rows)

    def body(dho_ref, h_ref, g_ref, win_ref, wa2_ref, cos_ref, sin_ref, loga_ref, ga_ref,
             dgq_ref, dgk_ref, dgv_ref, dgg_ref, dsq_ref, dsk_ref, dsv_ref, dla_ref,
             dh_ref, dproj_ref, dwa2_ref, dg_ref, dba_ref):
        first = pl.program_id(0) == 0
        dz = dla_ref[...] * (1.0 / GLA_TAU) * (1.0 - jnp.exp(GLA_TAU * loga_ref[...]))
        _acc_add(dba_ref, first, _colsum(dz))
        dga = _nt(dz, wa2_ref[...])
        pa = _tn(ga_ref[...], dz)
        c1, s1 = cos_ref[...], sin_ref[...]
        c4 = jnp.concatenate([c1, c1, c1, c1], axis=1)
        s4 = jnp.concatenate([s1, s1, s1, s1], axis=1)
        dq_r, dk_r = dsq_ref[...], dsk_ref[...]
        dsq = dq_r * c4 - _rot_half(dq_r * s4)
        dsk = dk_r * c1 - _rot_half(dk_r * s1)
        dproj16 = jnp.concatenate(
            [dgq_ref[...], dgk_ref[...], dgv_ref[...], dgg_ref[...], dsq, dsk, dsv_ref[...], dga], axis=1).astype(BF16)
        dproj_ref[...] = dproj16
        dn = _nn(dproj16, win_ref[...])

        @pl.when(first)
        def _():
            dwa2_ref[...] = pa

        @pl.when(jnp.logical_not(first))
        def _():
            dwa2_ref[...] += pa
        hn, rh = _rms(h_ref[...])
        _acc_add(dg_ref, first, _colsum(dn * hn))
        dh_ref[...] = dho_ref[...] + _rms_bwd(hn, rh, g_ref[...], dn)

    rs = lambda c: _row_spec(tm, c)
    return pl.pallas_call(
        body, name="mix_in_bwd", grid=(rows // tm,),
        in_specs=[rs(D_MODEL), rs(D_MODEL), VMEM_SPEC, VMEM_SPEC, VMEM_SPEC, rs(128), rs(128), rs(256), rs(128),
                  rs(256), rs(256), rs(512), rs(512), rs(512), rs(128), rs(128), rs(256)],
        out_specs=[rs(D_MODEL), rs(P_END), pl.BlockSpec((128, 256), lambda i: (0, 0)), _acc_spec(D_MODEL), _acc_spec(256)],
        out_shape=[jax.ShapeDtypeStruct((rows, D_MODEL), F32), jax.ShapeDtypeStruct((rows, P_END), BF16),
                   jax.ShapeDtypeStruct((128, 256), F32), jax.ShapeDtypeStruct((8, D_MODEL), F32),
                   jax.ShapeDtypeStruct((8, 256), F32)],
        compiler_params=_params(("arbitrary",)),
    )(dh_out, h, g, win_p, wa2_p, cos, sin, loga, ga, dgq, dgk, dgv, dgg, dsq, dsk, dsv, dloga)


def _rope_tables(rows):
    pos = (jnp.arange(rows, dtype=jnp.int32) - PAD_ROWS).astype(F32)
    inv_freq = 1.0 / (ROPE_THETA ** (jnp.arange(0, SWA_HD, 2, dtype=F32) / SWA_HD))
    ang = pos[:, None] * inv_freq[None, :]
    ang = jnp.concatenate([ang, ang, ang, ang], axis=-1)
    return jnp.cos(ang), jnp.sin(ang)


def _local_step(x, tgt, front, w, late_weights=None, on_grads=None):
    cos, sin = _rope_tables(x.shape[0] + BLK)
    g = {}

    def tell(group, names):
        for nm in names:
            g[nm] = grads_now[nm]
        return 0.0 if on_grads is None else on_grads(group, {nm: grads_now[nm] for nm in names})

    h0, h1, a1, b1, f1 = _ffn_fwd(x, w["ffn1_pre"], w["wg1"], w["wu1"], w["wd1"], w["ffn1_post"], front=front)
    if late_weights is not None:
        w = {**w, **late_weights("win", f1)}
    gq, gk, gv, gg, sq, sk, sv, ga, loga, bc, n2 = _mix_in(h1, w["mix_pre"], w["win"], w["wa2"], w["b_a"], cos, sin)
    o_g, cat_g, sp = _gla_fwd(gq, gk, gv, gg, bc, w["gla_norm"])
    o_s, cat_s, lse = _swa_fwd(sq, sk, sv, w["sinks"], w["swa_norm"])
    if late_weights is not None:
        w = {**w, **late_weights("rest", lse)}
    h2, m = _mix_out(h1, cat_g, cat_s, w["wout"], w["mix_post"])
    h3, a2, b2, f2, dy, loss = _ffn_fwd(h2, w["ffn2_pre"], w["wg2"], w["wu2"], w["wd2"], w["ffn2_post"], tgt)
    del h3
    dh2, da, db, df, n3, g["ffn2_pre"], g["ffn2_post"] = _ffn_bwd_act(
        dy, h2, a2, b2, f2, w["ffn2_pre"], w["ffn2_post"], w["wg2"], w["wu2"], w["wd2"], "ffn2_bwd_act")
    grads_now = dict(wd2=_wgrad(b2, df, "ffn2_wgrad_down", gate=a2), wg2=_wgrad(da, n3, "ffn2_wgrad_gate"),
                     wu2=_wgrad(db, n3, "ffn2_wgrad_up"))
    tok = tell("ffn2", ("wd2", "wg2", "wu2"))
    dcg, dcs, dm, g["mix_post"] = _mix_out_bwd(dh2, m, w["wout"], w["mix_post"] + tok)
    dsq, dsk, dsv, g["sinks"], g["swa_norm"] = _swa_bwd(dcs, o_s, sq, sk, sv, lse, w["sinks"], w["swa_norm"])
    dgq, dgk, dgv, dgg, dloga, g["gla_norm"] = _gla_bwd(dcg, o_g, gq, gk, gv, gg, bc, sp, w["gla_norm"])
    dh1, dproj, g["wa2"], g["mix_pre"], g["b_a"] = _mix_in_bwd(
        dh2, h1, w["mix_pre"], w["win"], w["wa2"], cos, sin, loga, ga, dgq, dgk, dgv, dgg, dsq, dsk, dsv, dloga)
    grads_now = dict(wout=jnp.concatenate([_wgrad(cat_g, dm, "wout_wgrad_gla"), _wgrad(cat_s, dm, "wout_wgrad_swa")], axis=0),
                     win=_wgrad(dproj, n2, "win_wgrad"))
    tok = tell("mix", ("wout", "win"))
    dh0, da, db, df, n1, g["ffn1_pre"], g["ffn1_post"] = _ffn_bwd_act(
        dh1, h0, a1, b1, f1, w["ffn1_pre"] + tok, w["ffn1_post"], w["wg1"], w["wu1"], w["wd1"], "ffn1_bwd_act")
    grads_now = dict(wd1=_wgrad(b1, df, "ffn1_wgrad_down", gate=a1))
    tell("ffn1_down", ("wd1",))
    grads_now = dict(wg1=_wgrad(da, n1, "ffn1_wgrad_gate"))
    tell("ffn1_gate", ("wg1",))
    grads_now = dict(wu1=_wgrad(db, n1, "ffn1_wgrad_up"))
    tell("ffn1_up", ("wu1",))
    return loss[0, 0], dh0, g


def _win_pad_rows(win_t):
    pad = jnp.zeros((P_END - P_GA - 16, win_t.shape[1]), win_t.dtype)
    return jnp.concatenate([win_t[0:1536], win_t[1552:2320], win_t[1536:1552], pad], axis=0)


def _win_unpad_rows(win_p):
    return jnp.concatenate([win_p[0:1536], win_p[P_GA:P_GA + 16], win_p[1536:P_GA]], axis=0)


def _place_on_mesh():
    return lax.axis_index("x"), lax.axis_index("y"), lax.axis_index("c")


def _dev_index(px, py, pc):
    return 4 * px + 2 * py + pc


def _other_devices(x, y, c):
    flip = lambda v, f: 1 - v if f else v
    return [(flip(x, fx), flip(y, fy), flip(c, fc)) for fx in (0, 1) for fy in (0, 1) for fc in (0, 1)][1:]


def _all_gather(shards):
    n = len(shards)

    def body(*refs):
        ins, outs = refs[:n], refs[n:2 * n]
        zeros_ref, send_sems, recv_sems, local_sems = refs[2 * n:]
        zeros_ref[...] = jnp.zeros_like(zeros_ref)
        x, y, c = _place_on_mesh()
        me, sibling = (x, y, c), (x, y, 1 - c)
        chips = [(1 - x, y), (x, 1 - y), (1 - x, 1 - y)]

        def rows(k, px, py, pc):
            r = ins[k].shape[0]
            return outs[k].at[pl.ds(pl.multiple_of(_dev_index(px, py, pc) * r, 8), r), :]

        def copy(k, slot, block, to, src=None):
            return pltpu.make_async_remote_copy(
                src_ref=rows(k, *block) if src is None else src, dst_ref=rows(k, *block),
                send_sem=send_sems.at[k, slot], recv_sem=recv_sems.at[k, slot], device_id=to, device_id_type=MESH)

        local = [pltpu.make_async_copy(ins[k], rows(k, *me), local_sems.at[k]) for k in range(n)]
        sends = []
        for k in range(n):
            local[k].start()
            sends.append(copy(k, 0, me, sibling, src=ins[k]))
            sends += [copy(k, 1 + j, me, (*chip, c), src=ins[k]) for j, chip in enumerate(chips)]
        for cp in sends:
            cp.start()
        for k in range(n):
            for j, chip in enumerate(chips):
                copy(k, 1 + j, (*chip, c), me).wait_recv()
                passed = copy(k, 4 + j, (*chip, c), sibling)
                passed.start()
                sends.append(passed)
        for k in range(n):
            copy(k, 0, sibling, me).wait_recv()
            for j, chip in enumerate(chips):
                copy(k, 4 + j, (*chip, 1 - c), me).wait_recv()
        for cp in sends:
            cp.wait_send()
        for cp in local:
            cp.wait()

    return pl.pallas_call(
        body, name="all_gather_weights",
        in_specs=[ANY_SPEC] * n, out_specs=[ANY_SPEC] * n + [VMEM_SPEC],
        out_shape=[jax.ShapeDtypeStruct((N_DEV * s.shape[0], s.shape[1]), s.dtype) for s in shards]
        + [jax.ShapeDtypeStruct((8, 128), F32)],
        scratch_shapes=[pltpu.SemaphoreType.DMA((n, 7)), pltpu.SemaphoreType.DMA((n, 7)), pltpu.SemaphoreType.DMA((n,))],
    )(*shards)


HBM_SPEC = pl.BlockSpec(memory_space=pltpu.HBM)
SEM_SPEC = pl.BlockSpec(memory_space=pltpu.SEMAPHORE)
DATAFLOW = pltpu.SideEffectType.DATAFLOW_SIDE_EFFECTING


GATHER, SCATTER, SCATTER_CHIPS = "gather", "scatter", "scatter among chips"


def _exchange_peers(kind):
    x, y, c = _place_on_mesh()
    if kind == SCATTER_CHIPS:
        peers = [(1 - x, y, c), (x, 1 - y, c), (1 - x, 1 - y, c)]
        return peers, [2 * p[0] + p[1] for p in peers], 2 * x + y, 4
    peers = _other_devices(x, y, c)
    return peers, [_dev_index(*p) for p in peers], _dev_index(x, y, c), N_DEV


def _exchange_copies(srcs, lands, send_sems, recv_sems, own_sems, kind, arriving):
    peers, theirs, me, blocks = _exchange_peers(kind)
    remote, local = [], []
    for k, (src, land) in enumerate(zip(srcs, lands)):
        r = land.shape[0] // blocks

        def block(ref, d):
            return ref.at[pl.ds(pl.multiple_of(d * r, 8), r), :]

        for f, (peer, him) in enumerate(zip(peers, theirs)):
            mine, his = (him, me) if arriving else (me, him)
            sem = len(peers) * k + f
            remote.append(pltpu.make_async_remote_copy(
                src_ref=src if kind == GATHER else block(src, his), dst_ref=block(land, mine),
                send_sem=send_sems.at[sem], recv_sem=recv_sems.at[sem], device_id=peer, device_id_type=MESH))
        local.append(pltpu.make_async_copy(src if kind == GATHER else block(src, me), block(land, me), own_sems.at[k]))
    return remote, local


def _exchange_start(srcs, kind, name):
    n = len(srcs)
    lands = [lax.empty((N_DEV * s.shape[0], s.shape[1]) if kind == GATHER else s.shape, s.dtype) for s in srcs]
    sems = (3 if kind == SCATTER_CHIPS else 7) * n

    def body(*refs):
        remote, local = _exchange_copies(refs[:n], refs[n:2 * n], *refs[2 * n:2 * n + 3], kind, False)
        for cp in remote + local:
            cp.start()
        refs[-1][...] = jnp.zeros_like(refs[-1])

    both = list(srcs) + list(lands)
    outs = pl.pallas_call(
        body, name=name,
        out_shape=(pltpu.SemaphoreType.DMA((sems,)), pltpu.SemaphoreType.DMA((sems,)), pltpu.SemaphoreType.DMA((n,)),
                   *[pltpu.HBM(a.shape, a.dtype) for a in both], jax.ShapeDtypeStruct((8, 128), F32)),
        in_specs=[HBM_SPEC] * (2 * n), out_specs=(SEM_SPEC, SEM_SPEC, SEM_SPEC, *[HBM_SPEC] * (2 * n), VMEM_SPEC),
        input_output_aliases={i: 3 + i for i in range(2 * n)},
        compiler_params=pltpu.CompilerParams(has_side_effects=DATAFLOW),
    )(*[pltpu.with_memory_space_constraint(a, pltpu.HBM) for a in both])
    return outs[0:3], outs[3:3 + n], outs[3 + n:3 + 2 * n], outs[-1]


def _exchange_wait(started, kind, after, name):
    sems, srcs, lands, _ = started
    n = len(srcs)

    def body(*refs):
        args = (refs[:n], refs[n:2 * n], *refs[2 * n:2 * n + 3], kind)
        going, local = _exchange_copies(*args, False)
        for cp in going:
            cp.wait_send()
        for cp in local:
            cp.wait()
        for cp in _exchange_copies(*args, True)[0]:
            cp.wait_recv()

    both = list(srcs) + list(lands)
    outs = pl.pallas_call(
        body, name=name, out_shape=[pltpu.HBM(a.shape, a.dtype) for a in both],
        in_specs=[HBM_SPEC] * (2 * n) + [SEM_SPEC, SEM_SPEC, SEM_SPEC, ANY_SPEC], out_specs=[HBM_SPEC] * (2 * n),
        input_output_aliases={i: i for i in range(2 * n)},
        compiler_params=pltpu.CompilerParams(has_side_effects=DATAFLOW),
    )(*both, *sems, after)
    return outs[n:]


def _sibling_reduce(part, name):
    r, cols = part.shape[0] // N_DEV, part.shape[1]

    def swap(p_ref, got_ref, send_sems, recv_sems):
        x, y, c = _place_on_mesh()
        copies = [pltpu.make_async_remote_copy(
            src_ref=p_ref.at[pl.ds(pl.multiple_of((2 * j + 1 - c) * r, 8), r), :], dst_ref=got_ref.at[pl.ds(j * r, r), :],
            send_sem=send_sems.at[j], recv_sem=recv_sems.at[j], device_id=(x, y, 1 - c), device_id_type=MESH)
            for j in range(4)]
        for cp in copies:
            cp.start()
        for cp in copies:
            cp.wait()

    got = pl.pallas_call(
        swap, name=name + "_swap", in_specs=[ANY_SPEC], out_specs=ANY_SPEC,
        out_shape=jax.ShapeDtypeStruct((4 * r, cols), part.dtype),
        scratch_shapes=[pltpu.SemaphoreType.DMA((4,)), pltpu.SemaphoreType.DMA((4,))],
    )(part)

    def add(c_ref, mine_ref, got_ref, o_ref):
        del c_ref
        o_ref[...] = (mine_ref[...].astype(F32) + got_ref[...].astype(F32)).astype(o_ref.dtype)

    core = lax.axis_index("c").astype(jnp.int32).reshape(1)
    return pl.pallas_call(
        add, name=name + "_add",
        grid_spec=pltpu.PrefetchScalarGridSpec(
            num_scalar_prefetch=1, grid=(4,),
            in_specs=[pl.BlockSpec((r, cols), lambda j, c_ref: (2 * j + c_ref[0], 0)),
                      pl.BlockSpec((r, cols), lambda j, c_ref: (j, 0))],
            out_specs=pl.BlockSpec((r, cols), lambda j, c_ref: (j, 0))),
        out_shape=jax.ShapeDtypeStruct((4 * r, cols), part.dtype),
        compiler_params=_params(("arbitrary",)),
    )(core, part, got)


def _sum_partials(parts, name, blocks=N_DEV):
    n = len(parts)

    def body(*refs):
        ins, outs = refs[:n], refs[n:]
        first = pl.program_id(0) == 0
        for i_ref, o_ref in zip(ins, outs):
            v = i_ref[...].astype(F32)

            @pl.when(first)
            def _():
                o_ref[...] = v

            @pl.when(jnp.logical_not(first))
            def _():
                o_ref[...] += v

    shapes = [(p.shape[0] // blocks, p.shape[1]) for p in parts]
    return pl.pallas_call(
        body, name=name, grid=(blocks,),
        in_specs=[pl.BlockSpec(s, lambda j: (j, 0)) for s in shapes],
        out_specs=[pl.BlockSpec(s, lambda j: (0, 0)) for s in shapes],
        out_shape=[jax.ShapeDtypeStruct(s, F32) for s in shapes],
        compiler_params=_params(("arbitrary",)),
    )(*parts)


def _all_reduce_small(slab):
    rows, cols = slab.shape

    def body(x_ref, o_ref, gathered, send_sems, recv_sems):
        x, y, c = _place_on_mesh()
        me = _dev_index(x, y, c)
        peers = _other_devices(x, y, c)

        def copy(f, peer):
            return pltpu.make_async_remote_copy(
                src_ref=x_ref, dst_ref=gathered.at[me], send_sem=send_sems.at[f], recv_sem=recv_sems.at[f],
                device_id=peer, device_id_type=MESH)

        def arrival(f, peer):
            return pltpu.make_async_remote_copy(
                src_ref=x_ref, dst_ref=gathered.at[_dev_index(*peer)], send_sem=send_sems.at[f], recv_sem=recv_sems.at[f],
                device_id=peer, device_id_type=MESH)

        sends = [copy(f, peer) for f, peer in enumerate(peers)]
        for cp in sends:
            cp.start()
        gathered[me] = x_ref[...]
        for f, peer in enumerate(peers):
            arrival(f, peer).wait_recv()
        for cp in sends:
            cp.wait_send()
        total = gathered[0]
        for d in range(1, N_DEV):
            total = total + gathered[d]
        o_ref[...] = total

    return pl.pallas_call(
        body, name="all_reduce_small",
        in_specs=[VMEM_SPEC], out_specs=VMEM_SPEC, out_shape=jax.ShapeDtypeStruct((rows, cols), F32),
        scratch_shapes=[pltpu.VMEM((N_DEV, rows, cols), F32), pltpu.SemaphoreType.DMA((7,)), pltpu.SemaphoreType.DMA((7,))],
    )(slab)


def _adamw(ws, gs, ms, vs, name):
    n = len(ws)
    c1 = 1.0 / (1.0 - ADAM_B1 ** ADAM_STEP)
    c2 = 1.0 / (1.0 - ADAM_B2 ** ADAM_STEP)

    def body(*refs):
        w_r, g_r, m_r, v_r = refs[:n], refs[n:2 * n], refs[2 * n:3 * n], refs[3 * n:4 * n]
        d_o, m_o, v_o = refs[4 * n:5 * n], refs[5 * n:6 * n], refs[6 * n:7 * n]
        for k in range(n):
            g = g_r[k][...]
            m = ADAM_B1 * m_r[k][...] + (1.0 - ADAM_B1) * g
            v = ADAM_B2 * v_r[k][...] + (1.0 - ADAM_B2) * (g * g)
            m_o[k][...] = m
            v_o[k][...] = v
            d_o[k][...] = -ADAM_LR * ((m * c1) / (jnp.sqrt(v * c2) + ADAM_EPS) + ADAM_WD * w_r[k][...])

    shapes = [jax.ShapeDtypeStruct(w.shape, F32) for w in ws]
    outs = pl.pallas_call(
        body, name=name, in_specs=[VMEM_SPEC] * (4 * n), out_specs=[VMEM_SPEC] * (3 * n), out_shape=shapes * 3,
        compiler_params=pltpu.CompilerParams(vmem_limit_bytes=56 << 20),
    )(*ws, *gs, *ms, *vs)
    return outs[:n], outs[n:2 * n], outs[2 * n:]


WEIGHT_NAMES = ("meta_tokens", "ffn1_pre_norm", "ffn1_w_gate", "ffn1_w_up", "ffn1_w_down", "ffn1_post_norm", "mix_pre_norm",
                "w_in", "gla_w_a2", "gla_b_a", "gla_out_norm", "swa_sinks", "swa_out_norm", "w_out", "mix_post_norm",
                "ffn2_pre_norm", "ffn2_w_gate", "ffn2_w_up", "ffn2_w_down", "ffn2_post_norm")
WIN_SHARD = D_IN // N_DEV
WIN_SHARD_PAD = 304
SLAB_VECTORS = ("ffn1_pre", "ffn1_post", "mix_pre", "mix_post", "ffn2_pre", "ffn2_post")
SLAB_ROWS = 32


def kernel(x, meta_tokens, ffn1_pre_norm, ffn1_w_gate, ffn1_w_up, ffn1_w_down, ffn1_post_norm, mix_pre_norm, w_in, gla_w_a2, gla_b_a, gla_out_norm, swa_sinks, swa_out_norm, w_out, mix_post_norm, ffn2_pre_norm, ffn2_w_gate, ffn2_w_up, ffn2_w_down, ffn2_post_norm, loss_target, m_meta_tokens, m_ffn1_pre_norm, m_ffn1_w_gate, m_ffn1_w_up, m_ffn1_w_down, m_ffn1_post_norm, m_mix_pre_norm, m_w_in, m_gla_w_a2, m_gla_b_a, m_gla_out_norm, m_swa_sinks, m_swa_out_norm, m_w_out, m_mix_post_norm, m_ffn2_pre_norm, m_ffn2_w_gate, m_ffn2_w_up, m_ffn2_w_down, m_ffn2_post_norm, v_meta_tokens, v_ffn1_pre_norm, v_ffn1_w_gate, v_ffn1_w_up, v_ffn1_w_down, v_ffn1_post_norm, v_mix_pre_norm, v_w_in, v_gla_w_a2, v_gla_b_a, v_gla_out_norm, v_swa_sinks, v_swa_out_norm, v_w_out, v_mix_post_norm, v_ffn2_pre_norm, v_ffn2_w_gate, v_ffn2_w_up, v_ffn2_w_down, v_ffn2_post_norm):
    given = dict(locals())
    W = {n: given[n] for n in WEIGHT_NAMES}
    M = {n: given["m_" + n] for n in WEIGHT_NAMES}
    V = {n: given["v_" + n] for n in WEIGHT_NAMES}
    dev = _dev_index(*_place_on_mesh())

    def t16(w):
        return w[0].T.astype(BF16)

    small = jnp.concatenate([W["meta_tokens"], jnp.pad(W["gla_w_a2"][0], ((0, 0), (0, 96)))], axis=0)
    wg1, wu1, wd1, small_g, gathered_zeros = _all_gather(
        [t16(W["ffn1_w_gate"]), t16(W["ffn1_w_up"]), W["ffn1_w_down"][0].astype(BF16), small])
    def after_zero(shard, zeros):
        return shard + zeros[0:1, 0:1].astype(shard.dtype)
    win_shard = jnp.pad(t16(W["w_in"]), ((0, WIN_SHARD_PAD - WIN_SHARD), (0, 0)))
    win_shard = after_zero(win_shard, gathered_zeros)
    mid = _exchange_start([win_shard], GATHER, "gather_w_in_start")
    late_shards = [after_zero(W["w_out"][0].astype(BF16), mid[3]), t16(W["ffn2_w_gate"]), t16(W["ffn2_w_up"]),
                   W["ffn2_w_down"][0].astype(BF16)]
    late = _exchange_start(late_shards, GATHER, "gather_late_weights_start")

    def late_weights(what, after):
        if what == "win":
            win_g, = _exchange_wait(mid, GATHER, after, "gather_w_in_wait")
            win_t = win_g.reshape(N_DEV, WIN_SHARD_PAD, D_MODEL)[:, :WIN_SHARD].reshape(D_IN, D_MODEL)
            return dict(win=_win_pad_rows(win_t))
        wout, wg2, wu2, wd2 = _exchange_wait(late, GATHER, after, "gather_late_weights_wait")
        return dict(wout=wout, wg2=wg2, wu2=wu2, wd2=wd2)

    small_g = small_g.reshape(N_DEV, 32, 128)
    meta_full = small_g[:, :N_META].transpose(1, 0, 2).reshape(N_META, D_MODEL)
    wa2_full = small_g[:, N_META:, :32].transpose(1, 0, 2).reshape(16, 256)
    w = dict(
        ffn1_pre=W["ffn1_pre_norm"] + late[3][0, 0], ffn1_post=W["ffn1_post_norm"], mix_pre=W["mix_pre_norm"],
        mix_post=W["mix_post_norm"], ffn2_pre=W["ffn2_pre_norm"], ffn2_post=W["ffn2_post_norm"], b_a=W["gla_b_a"],
        gla_norm=W["gla_out_norm"], sinks=W["swa_sinks"], swa_norm=W["swa_out_norm"], wg1=wg1, wu1=wu1, wd1=wd1,
        wa2=jnp.pad(wa2_full, ((0, 112), (0, 0))))

    in_flight = []

    def on_grads(group, grads):
        parts = []
        for nm, p in grads.items():
            if nm == "win":
                p = _win_unpad_rows(p).reshape(N_DEV, WIN_SHARD, D_MODEL)
                p = jnp.pad(p, ((0, 0), (0, WIN_SHARD_PAD - WIN_SHARD), (0, 0))).reshape(N_DEV * WIN_SHARD_PAD, D_MODEL)
            parts.append(p)
        kind = SCATTER_CHIPS if group.startswith("ffn1") else SCATTER
        if kind == SCATTER_CHIPS:
            parts = [_sibling_reduce(p, "pair_" + group) for p in parts]
        started = _exchange_start(parts, kind, "scatter_" + group + "_start")
        in_flight.append((group, list(grads), started, kind))
        return started[3][0, 0]

    front = jnp.concatenate([jnp.zeros((PAD_ROWS, D_MODEL), F32), meta_full], axis=0)
    loss, dh0, g = _local_step(x[0], loss_target[0], front, w, late_weights, on_grads)
    grad_x = dh0[BLK:][None]

    packed = jnp.concatenate([g["b_a"][0:1], g["gla_norm"][0:1], g["sinks"][0:1], g["swa_norm"][0:1]], axis=1)
    slab = jnp.concatenate([g[k][0:1] for k in SLAB_VECTORS] + [packed, jnp.full((1, D_MODEL), loss, F32),
                           g["wa2"][:16].reshape(4, D_MODEL), jnp.zeros((4, D_MODEL), F32), dh0[PAD_ROWS:BLK]], axis=0)
    tot = _all_reduce_small(slab)
    loss = tot[7, 0]
    small_grads = dict(
        ffn1_pre_norm=tot[0:1], ffn1_post_norm=tot[1:2], mix_pre_norm=tot[2:3], mix_post_norm=tot[3:4],
        ffn2_pre_norm=tot[4:5], ffn2_post_norm=tot[5:6], gla_b_a=tot[6:7, 0:256], gla_out_norm=tot[6:7, 256:384],
        swa_sinks=tot[6:7, 384:392], swa_out_norm=tot[6:7, 512:1024],
        gla_w_a2=lax.dynamic_slice_in_dim(tot[8:12].reshape(16, 256), dev * 32, 32, axis=1)[None],
        meta_tokens=lax.dynamic_slice_in_dim(tot[16:32], dev * 128, 128, axis=1))

    big = dict(wg1=("ffn1_w_gate", True), wu1=("ffn1_w_up", True), wd1=("ffn1_w_down", False), win=("w_in", True),
               wout=("w_out", False), wg2=("ffn2_w_gate", True), wu2=("ffn2_w_up", True), wd2=("ffn2_w_down", False))
    grads = dict(small_grads)
    delta, new_m, new_v = {}, {}, {}
    names = [n for n in WEIGHT_NAMES if n not in [full for full, _ in big.values()]]
    two_d = lambda a: a.reshape(-1, a.shape[-1])
    d_, m_, v_ = _adamw([two_d(W[n]) for n in names], [two_d(grads[n]) for n in names],
                        [two_d(M[n]) for n in names], [two_d(V[n]) for n in names], "adamw_small")
    for k, n in enumerate(names):
        delta[n], new_m[n], new_v[n] = d_[k].reshape(W[n].shape), m_[k].reshape(W[n].shape), v_[k].reshape(W[n].shape)

    before_wait = d_[0] + in_flight[-1][2][3][0, 0]
    for group, shorts, started, kind in in_flight:
        lands = _exchange_wait(started, kind, before_wait, "scatter_" + group + "_wait")
        for short, g_slab in zip(shorts, _sum_partials(lands, "sum_" + group, 4 if kind == SCATTER_CHIPS else N_DEV)):
            n, transposed = big[short]
            to_slab = (lambda a: a[0].T) if transposed else (lambda a: a[0])
            from_slab = (lambda a: a.T[None]) if transposed else (lambda a: a[None])
            g_slab = g_slab[:WIN_SHARD] if short == "win" else g_slab
            d_, m_, v_ = _adamw([to_slab(W[n])], [g_slab], [to_slab(M[n])], [to_slab(V[n])], "adamw_" + n)
            grads[n], delta[n], new_m[n], new_v[n] = from_slab(g_slab), from_slab(d_[0]), from_slab(m_[0]), from_slab(v_[0])
            before_wait = d_[0]
    return (loss, grad_x, *[grads[n] for n in WEIGHT_NAMES], *[delta[n] for n in WEIGHT_NAMES],
            *[new_m[n] for n in WEIGHT_NAMES], *[new_v[n] for n in WEIGHT_NAMES])
```

```python
import functools

import jax
import jax.numpy as jnp
from jax import lax
from jax.experimental import pallas as pl
from jax.experimental.pallas import tpu as pltpu

F32, BF16 = jnp.float32, jnp.bfloat16

D_MODEL = 1024
D_FF = 2816
N_META = 16
BLK = 128
PAD_ROWS = BLK - N_META
GLA_DK = 64
SWA_HD = 64
SWA_HEADS = 8
GLA_TAU = 16.0
NORM_EPS = 1e-6
NEG_INF = -1e30
ROPE_THETA = 10000.0
P_GQ, P_GK, P_GV, P_GG, P_SQ, P_SK, P_SV, P_GA, P_END = 0, 256, 512, 1024, 1536, 2048, 2176, 2304, 2432
D_IN = 2320
IN_SPLITS = (256, 256, 512, 512, 16, 512, 128, 128)
FF_TILE = 2816
WGRAD_TILE_MAX = 2432
N_DEV = 8
MESH = pl.DeviceIdType.MESH

ADAM_LR, ADAM_B1, ADAM_B2, ADAM_EPS, ADAM_WD, ADAM_STEP = 0.001, 0.9, 0.999, 1e-08, 0.01, 10

V7X_VMEM_BYTES = 64 << 20
VMEM_SPEC = pl.BlockSpec(memory_space=pltpu.VMEM)
SMEM_SPEC = pl.BlockSpec(memory_space=pltpu.SMEM)
ANY_SPEC = pl.BlockSpec(memory_space=pl.ANY)


def _params(semantics, vmem_mb=56):
    return pltpu.CompilerParams(dimension_semantics=semantics, vmem_limit_bytes=vmem_mb << 20)


def _row_tile(rows):
    return 320 if rows % 320 == 0 else BLK


def _nn(a, b):
    return lax.dot_general(a, b, (((1,), (0,)), ((), ())), preferred_element_type=F32)


def _nt(a, b):
    return lax.dot_general(a, b, (((1,), (1,)), ((), ())), preferred_element_type=F32)


def _tn(a, b):
    return lax.dot_general(a, b, (((0,), (0,)), ((), ())), preferred_element_type=F32)


def _rms(x):
    r = lax.rsqrt(jnp.mean(x * x, axis=-1, keepdims=True) + NORM_EPS)
    return x * r, r


def _rms_bwd(xn, r, w, dy):
    g = dy * w
    return r * (g - xn * jnp.mean(g * xn, axis=-1, keepdims=True))


def _sigmoid(x):
    return 1.0 / (1.0 + jnp.exp(-x))


def _colsum(x):
    return jnp.sum(x, axis=0, keepdims=True)


def _split_bf16(x):
    hi = x.astype(BF16)
    lo = (x - hi.astype(F32)).astype(BF16)
    return hi, lo


def _tri(lower):
    r = lax.broadcasted_iota(jnp.int32, (BLK, BLK), 0)
    c = lax.broadcasted_iota(jnp.int32, (BLK, BLK), 1)
    return (r >= c) if lower else (c >= r)


def _half_mask(width, half):
    lane = lax.broadcasted_iota(jnp.int32, (1, width), 1)
    return ((lane % 128) < 64) if half == 0 else ((lane % 128) >= 64)


def _rot_half(x):
    w = x.shape[-1]
    lane = lax.broadcasted_iota(jnp.int32, (1, w), 1)
    return jnp.where((lane % SWA_HD) < SWA_HD // 2, -pltpu.roll(x, w - SWA_HD // 2, 1), pltpu.roll(x, SWA_HD // 2, 1))


def _row_spec(tm, cols):
    return pl.BlockSpec((tm, cols), lambda i: (i, 0))


def _acc_spec(cols):
    return pl.BlockSpec((8, cols), lambda i: (0, 0))


def _acc_add(ref, first, value):
    @pl.when(first)
    def _():
        ref[...] = jnp.zeros_like(ref)
    ref[0:1, :] += value


def _behind_front(ref, i, tm, front):
    blk = ref[...]
    return jnp.where(i == 0, jnp.concatenate([front, blk[0:tm - BLK]], axis=0), blk)


def _ffn_fwd(h, gpre, wg_t, wu_t, wd, gpost, tgt=None, front=None):
    with_loss, with_front = tgt is not None, front is not None
    rows = h.shape[0] + (BLK if with_front else 0)
    tm = _row_tile(rows)
    nf = D_FF // FF_TILE

    def body(*refs):
        refs = list(refs)
        h_ref, gpre_ref, wg_ref, wu_ref, wd_ref, gpost_ref = refs[:6]
        del refs[:6]
        front_ref = refs.pop(0) if with_front else None
        t_ref = refs.pop(0) if with_loss else None
        h0_ref = refs.pop(0) if with_front else None
        ho_ref, a_ref, b_ref, s_ref, f_ref = refs[:5]
        dy_ref, loss_ref = refs[5:7] if with_loss else (None, None)
        acc = refs[-1]
        i = pl.program_id(0)
        if with_front:
            h_in = _behind_front(h_ref, i, tm, front_ref[...])
            h0_ref[...] = h_in
        else:
            h_in = h_ref[...]
        hn, _ = _rms(h_in)
        n16 = (hn * gpre_ref[...]).astype(BF16)
        for j in range(nf):
            cols = slice(j * FF_TILE, (j + 1) * FF_TILE)
            a = _nt(n16, wg_ref[cols, :])
            b = _nt(n16, wu_ref[cols, :])
            a_ref[:, cols] = a.astype(BF16)
            b_ref[:, cols] = b.astype(BF16)
            s16 = (a * _sigmoid(a) * b).astype(BF16)
            s_ref[:, cols] = s16
            part = _nn(s16, wd_ref[cols, :])
            if j == 0:
                acc[...] = part
            else:
                acc[...] += part
        f = acc[...]
        f_ref[...] = f
        fn, _ = _rms(f)
        y = h_in + 0.5 * (fn * gpost_ref[...])
        ho_ref[...] = y
        if with_loss:
            row = i * tm + lax.broadcasted_iota(jnp.int32, (tm, 1), 0)
            err = jnp.where(row >= BLK, y - _behind_front(t_ref, i, tm, jnp.zeros((BLK, D_MODEL), F32)), 0.0)
            dy_ref[...] = err * (1.0 / D_MODEL)
            part = 0.5 * jnp.sum(jnp.sum(err * err, axis=-1, keepdims=True) * (1.0 / D_MODEL), axis=0, keepdims=True)

            @pl.when(i == 0)
            def _():
                loss_ref[...] = jnp.zeros_like(loss_ref)
            loss_ref[...] += part

    row_f32 = _row_spec(tm, D_MODEL)
    behind = pl.BlockSpec((pl.Element(tm), pl.Element(D_MODEL)),
                          lambda i: (pl.multiple_of(jnp.maximum(i * tm - BLK, 0), 64), 0))
    in_specs = [behind if with_front else row_f32, VMEM_SPEC, VMEM_SPEC, VMEM_SPEC, VMEM_SPEC, VMEM_SPEC]
    out_specs = [row_f32, _row_spec(tm, D_FF), _row_spec(tm, D_FF), _row_spec(tm, D_FF), row_f32]
    out_shape = [jax.ShapeDtypeStruct((rows, D_MODEL), F32), jax.ShapeDtypeStruct((rows, D_FF), BF16),
                 jax.ShapeDtypeStruct((rows, D_FF), BF16), jax.ShapeDtypeStruct((rows, D_FF), BF16),
                 jax.ShapeDtypeStruct((rows, D_MODEL), F32)]
    args = [h, gpre, wg_t, wu_t, wd, gpost]
    if with_front:
        in_specs.append(VMEM_SPEC)
        args.append(front)
        out_specs.insert(0, row_f32)
        out_shape.insert(0, jax.ShapeDtypeStruct((rows, D_MODEL), F32))
    if with_loss:
        in_specs.append(behind)
        args.append(tgt)
        out_specs += [row_f32, pl.BlockSpec((8, 128), lambda i: (0, 0))]
        out_shape += [jax.ShapeDtypeStruct((rows, D_MODEL), F32), jax.ShapeDtypeStruct((8, 128), F32)]
    return pl.pallas_call(
        body, name="ffn_fwd_loss" if with_loss else "ffn_fwd", grid=(rows // tm,),
        in_specs=in_specs, out_specs=out_specs, out_shape=out_shape,
        scratch_shapes=[pltpu.VMEM((tm, D_MODEL), F32)],
        compiler_params=_params(("arbitrary",)),
    )(*args)


def _ffn_bwd_act(dh_out, h, a, b, f, gpre, gpost, wg_t, wu_t, wd, name):
    rows = h.shape[0]
    tm = _row_tile(rows)
    nf = D_FF // FF_TILE

    def body(dho_ref, h_ref, a_ref, b_ref, f_ref, gpre_ref, gpost_ref, wg_ref, wu_ref, wd_ref,
             dh_ref, da_ref, db_ref, df_ref, n_ref, dgpre_ref, dgpost_ref, acc):
        first = pl.program_id(0) == 0
        dho = dho_ref[...]
        drr = 0.5 * dho
        fn, rf = _rms(f_ref[...])
        _acc_add(dgpost_ref, first, _colsum(drr * fn))
        df16 = _rms_bwd(fn, rf, gpost_ref[...], drr).astype(BF16)
        df_ref[...] = df16
        hn, rh = _rms(h_ref[...])
        n_ref[...] = (hn * gpre_ref[...]).astype(BF16)
        for j in range(nf):
            cols = slice(j * FF_TILE, (j + 1) * FF_TILE)
            ds = _nt(df16, wd_ref[cols, :])
            av = a_ref[:, cols].astype(F32)
            bv = b_ref[:, cols].astype(F32)
            sg = _sigmoid(av)
            db16 = (ds * (av * sg)).astype(BF16)
            da16 = (ds * bv * (sg * (1.0 + av * (1.0 - sg)))).astype(BF16)
            da_ref[:, cols] = da16
            db_ref[:, cols] = db16
            part = _nn(da16, wg_ref[cols, :]) + _nn(db16, wu_ref[cols, :])
            if j == 0:
                acc[...] = part
            else:
                acc[...] += part
        dn = acc[...]
        _acc_add(dgpre_ref, first, _colsum(dn * hn))
        dh_ref[...] = dho + _rms_bwd(hn, rh, gpre_ref[...], dn)

    row_f32 = _row_spec(tm, D_MODEL)
    row_ff = _row_spec(tm, D_FF)
    return pl.pallas_call(
        body, name=name, grid=(rows // tm,),
        in_specs=[row_f32, row_f32, row_ff, row_ff, row_f32, VMEM_SPEC, VMEM_SPEC, VMEM_SPEC, VMEM_SPEC, VMEM_SPEC],
        out_specs=[row_f32, row_ff, row_ff, row_f32, row_f32, _acc_spec(D_MODEL), _acc_spec(D_MODEL)],
        out_shape=[jax.ShapeDtypeStruct((rows, D_MODEL), F32), jax.ShapeDtypeStruct((rows, D_FF), BF16),
                   jax.ShapeDtypeStruct((rows, D_FF), BF16), jax.ShapeDtypeStruct((rows, D_MODEL), BF16),
                   jax.ShapeDtypeStruct((rows, D_MODEL), BF16), jax.ShapeDtypeStruct((8, D_MODEL), F32),
                   jax.ShapeDtypeStruct((8, D_MODEL), F32)],
        scratch_shapes=[pltpu.VMEM((tm, D_MODEL), F32)],
        compiler_params=_params(("arbitrary",)),
    )(dh_out, h, a, b, f, gpre, gpost, wg_t, wu_t, wd)


def _wgrad(lhs, rhs, name):
    rows, width = lhs.shape
    tm = rows if rows % 1664 == 0 else BLK
    tf = 256 if width % 256 == 0 else 128
    nr = rows // tm

    def body(l_ref, r_ref, o_ref, acc):
        i = pl.program_id(1)
        part = _tn(l_ref[...], r_ref[...])

        @pl.when(i == 0)
        def _():
            acc[...] = part

        @pl.when(i > 0)
        def _():
            acc[...] += part

        @pl.when(i == nr - 1)
        def _():
            o_ref[...] = acc[...].astype(BF16)

    l_spec = pl.BlockSpec((tm, tf), lambda j, i: (i, j))
    r_spec = pl.BlockSpec((tm, D_MODEL), lambda j, i: (i, 0))
    return pl.pallas_call(
        body, name=name, grid=(width // tf, nr),
        in_specs=[l_spec, r_spec],
        out_specs=pl.BlockSpec((tf, D_MODEL), lambda j, i: (j, 0)),
        out_shape=jax.ShapeDtypeStruct((width, D_MODEL), BF16),
        scratch_shapes=[pltpu.VMEM((tf, D_MODEL), F32)],
        compiler_params=_params(("arbitrary", "arbitrary")),
    )(lhs, rhs)


def _chunk_cumsum(x, lower):
    tri = jnp.where(_tri(lower), 1.0, 0.0).astype(BF16)
    hi, lo = _split_bf16(x)
    return _nn(tri, hi) + _nn(tri, lo)


def _mix_in(h, g, win_p, wa2_p, b_a, cos, sin):
    rows = h.shape[0]
    tm = 640 if rows % 640 == 0 else BLK

    def body(h_ref, g_ref, win_ref, wa2_ref, ba_ref, cos_ref, sin_ref,
             gq_ref, gk_ref, gv_ref, gg_ref, sq_ref, sk_ref, sv_ref, ga_ref, loga_ref, bc_ref, n_ref):
        hn, _ = _rms(h_ref[...])
        n16 = (hn * g_ref[...]).astype(BF16)
        n_ref[...] = n16
        proj = _nt(n16, win_ref[...])
        gq_ref[...] = proj[:, P_GQ:P_GK]
        gk_ref[...] = proj[:, P_GK:P_GV]
        gv_ref[...] = proj[:, P_GV:P_GG].astype(BF16)
        gg_ref[...] = proj[:, P_GG:P_SQ]
        c1, s1 = cos_ref[...], sin_ref[...]
        c4 = jnp.concatenate([c1, c1, c1, c1], axis=1)
        s4 = jnp.concatenate([s1, s1, s1, s1], axis=1)
        sq = proj[:, P_SQ:P_SK]
        sk = proj[:, P_SK:P_SV]
        sq_ref[...] = (sq * c4 + _rot_half(sq) * s4).astype(BF16)
        sk_ref[...] = (sk * c1 + _rot_half(sk) * s1).astype(BF16)
        sv_ref[...] = proj[:, P_SV:P_GA].astype(BF16)
        ga = proj[:, P_GA:P_END]
        ga_ref[...] = ga
        z = _nn(ga, wa2_ref[...]) + ba_ref[...]
        loga = (jnp.minimum(z, 0.0) - jnp.log(1.0 + jnp.exp(-jnp.abs(z)))) * (1.0 / GLA_TAU)
        loga_ref[...] = loga
        for c in range(tm // BLK):
            rs = slice(c * BLK, (c + 1) * BLK)
            bc_ref[rs, :] = _chunk_cumsum(loga[rs, :], True)

    f32 = lambda c: jax.ShapeDtypeStruct((rows, c), F32)
    b16 = lambda c: jax.ShapeDtypeStruct((rows, c), BF16)
    rs = lambda c: _row_spec(tm, c)
    return pl.pallas_call(
        body, name="mix_in", grid=(rows // tm,),
        in_specs=[rs(D_MODEL), VMEM_SPEC, VMEM_SPEC, VMEM_SPEC, VMEM_SPEC, rs(128), rs(128)],
        out_specs=[rs(256), rs(256), rs(512), rs(512), rs(512), rs(128), rs(128), rs(128), rs(256), rs(256), rs(D_MODEL)],
        out_shape=[f32(256), f32(256), b16(512), f32(512), b16(512), b16(128), b16(128), f32(128), f32(256), f32(256),
                   b16(D_MODEL)],
        compiler_params=_params(("arbitrary",)),
    )(h, g, win_p, wa2_p, b_a, cos, sin)


def _gla_factors(q, k, bc):
    bm = bc[BLK // 2 - 1:BLK // 2, :]
    bl = bc[BLK - 1:BLK, :]
    e_q, e_k, e_qe, e_kd = jnp.exp(bc - bm), jnp.exp(bm - bc), jnp.exp(bc), jnp.exp(bl - bc)
    return (q * e_q, k * e_k, q * e_qe, k * e_kd), (e_q, e_k, e_qe, e_kd), jnp.exp(bl)


def _gla_fwd(gq, gk, gv, gg, bc, wgn):
    rows = gq.shape[0]
    nc = rows // BLK
    scale = GLA_DK ** -0.5

    def body(q_ref, k_ref, v_ref, gg_ref, bc_ref, wgn_ref, o_ref, cat_ref, sp_ref, st):
        @pl.when(pl.program_id(0) == 0)
        def _():
            st[...] = jnp.zeros_like(st)
        low = _tri(True)
        wgn_v = wgn_ref[...]
        for p in range(2):
            sl = slice(128 * p, 128 * p + 128)
            (qt, kt, qe, kd), _, ebl = _gla_factors(q_ref[:, sl] * scale, k_ref[:, sl], bc_ref[:, sl])
            s_prev = st[p]
            sp_ref[0, p] = s_prev
            s16 = s_prev.astype(BF16)
            qt16 = qt.astype(BF16)
            s_new = s_prev * ebl
            for hh in range(2):
                hs = slice(128 * (2 * p + hh), 128 * (2 * p + hh) + 128)
                lm = _half_mask(128, hh)
                vh = v_ref[:, hs]
                pm = jnp.where(low, _nt(qt16, jnp.where(lm, kt, 0.0).astype(BF16)), 0.0)
                o = _nn(pm.astype(BF16), vh) + _nt(jnp.where(lm, qe, 0.0).astype(BF16), s16)
                s_new = s_new + _tn(vh, jnp.where(lm, kd, 0.0).astype(BF16))
                o_ref[:, hs] = o
                on, _ = _rms(o)
                gate = gg_ref[:, hs]
                cat_ref[:, hs] = (on * wgn_v * (gate * _sigmoid(gate))).astype(BF16)
            st[p] = s_new

    rs = lambda c: _row_spec(BLK, c)
    return pl.pallas_call(
        body, name="gla_fwd", grid=(nc,),
        in_specs=[rs(256), rs(256), rs(512), rs(512), rs(256), VMEM_SPEC],
        out_specs=[rs(512), rs(512), pl.BlockSpec((1, 2, 128, 128), lambda i: (i, 0, 0, 0))],
        out_shape=[jax.ShapeDtypeStruct((rows, 512), F32), jax.ShapeDtypeStruct((rows, 512), BF16),
                   jax.ShapeDtypeStruct((nc, 2, 128, 128), F32)],
        scratch_shapes=[pltpu.VMEM((2, 128, 128), F32)],
        compiler_params=_params(("arbitrary",)),
    )(gq, gk, gv, gg, bc, wgn)


def _gla_bwd(dcat, o_all, gq, gk, gv, gg, bc, sp, wgn):
    rows = gq.shape[0]
    nc = rows // BLK
    scale = GLA_DK ** -0.5

    def body(dc_ref, o_ref, q_ref, k_ref, v_ref, gg_ref, bc_ref, sp_ref, wgn_ref,
             dq_ref, dk_ref, dv_ref, dgg_ref, dla_ref, dwgn_ref, dst):
        first = pl.program_id(0) == 0

        @pl.when(first)
        def _():
            dst[...] = jnp.zeros_like(dst)
        low, upp = _tri(True), _tri(False)
        last_row = lax.broadcasted_iota(jnp.int32, (BLK, 1), 0) == BLK - 1
        wgn_v = wgn_ref[...]
        dwgn = jnp.zeros((1, 128), F32)
        for p in range(2):
            sl = slice(128 * p, 128 * p + 128)
            (qt, kt, qe, kd), (e_q, e_k, e_qe, e_kd), ebl = _gla_factors(
                q_ref[:, sl] * scale, k_ref[:, sl], bc_ref[:, sl])
            s_prev = sp_ref[0, p]
            s16 = s_prev.astype(BF16)
            ds_next = dst[p]
            ds16 = ds_next.astype(BF16)
            qt16 = qt.astype(BF16)
            ds_new = ds_next * ebl
            dqt = jnp.zeros((BLK, 128), F32)
            dkt = jnp.zeros((BLK, 128), F32)
            dqe = jnp.zeros((BLK, 128), F32)
            dkd = jnp.zeros((BLK, 128), F32)
            for hh in range(2):
                hs = slice(128 * (2 * p + hh), 128 * (2 * p + hh) + 128)
                lm = _half_mask(128, hh)
                on, ro = _rms(o_ref[:, hs])
                gate = gg_ref[:, hs]
                sg = _sigmoid(gate)
                si = gate * sg
                dog = dc_ref[:, hs]
                dwgn = dwgn + _colsum(dog * si * on)
                dgg_ref[:, hs] = dog * (on * wgn_v) * (sg * (1.0 + gate * (1.0 - sg)))
                do16 = _rms_bwd(on, ro, wgn_v, dog * si).astype(BF16)
                vh = v_ref[:, hs]
                ktm16 = jnp.where(lm, kt, 0.0).astype(BF16)
                qtm16 = jnp.where(lm, qt, 0.0).astype(BF16)
                qem16 = jnp.where(lm, qe, 0.0).astype(BF16)
                kdm16 = jnp.where(lm, kd, 0.0).astype(BF16)
                p_t = jnp.where(upp, _nt(ktm16, qt16), 0.0)
                dp_t = jnp.where(upp, _nt(vh, do16), 0.0)
                dp = jnp.where(low, _nt(do16, vh), 0.0)
                dv_ref[:, hs] = _nn(p_t.astype(BF16), do16) + _nt(kdm16, ds16)
                dqt = dqt + _nn(dp.astype(BF16), ktm16)
                dkt = dkt + _nn(dp_t.astype(BF16), qtm16)
                dqe = dqe + jnp.where(lm, _nn(do16, s16), 0.0)
                dkd = dkd + jnp.where(lm, _nn(vh, ds16), 0.0)
                ds_new = ds_new + _tn(do16, qem16)
            debl = _colsum(ds_next * s_prev)
            dq_ref[:, sl] = (dqt * e_q + dqe * e_qe) * scale
            dk_ref[:, sl] = dkt * e_k + dkd * e_kd
            dkd_kd = dkd * kd
            db = dqt * qt - dkt * kt + dqe * qe - dkd_kd
            db = jnp.where(last_row, db + (_colsum(dkd_kd) + debl * ebl), db)
            dla_ref[:, sl] = _chunk_cumsum(db, False)
            dst[p] = ds_new
        _acc_add(dwgn_ref, first, dwgn)

    rev = lambda c: pl.BlockSpec((BLK, c), lambda i: (nc - 1 - i, 0))
    f32 = lambda c: jax.ShapeDtypeStruct((rows, c), F32)
    return pl.pallas_call(
        body, name="gla_bwd", grid=(nc,),
        in_specs=[rev(512), rev(512), rev(256), rev(256), rev(512), rev(512), rev(256),
                  pl.BlockSpec((1, 2, 128, 128), lambda i: (nc - 1 - i, 0, 0, 0)), VMEM_SPEC],
        out_specs=[rev(256), rev(256), rev(512), rev(512), rev(256), _acc_spec(128)],
        out_shape=[f32(256), f32(256), f32(512), f32(512), f32(256), jax.ShapeDtypeStruct((8, 128), F32)],
        scratch_shapes=[pltpu.VMEM((2, 128, 128), F32)],
        compiler_params=_params(("arbitrary",)),
    )(dcat, o_all, gq, gk, gv, gg, bc, sp, wgn)


def _swa_masks(i):
    t = lax.broadcasted_iota(jnp.int32, (BLK, BLK), 0)
    c = lax.broadcasted_iota(jnp.int32, (BLK, BLK), 1)
    own_side = c <= t
    band_ok = i >= jnp.where(own_side, 1, 2)
    meta_ok = (c % N_META) <= jnp.where(i >= 1, N_META, t - PAD_ROWS)
    return own_side, band_ok, meta_ok, c // N_META


def _swa_blocks(ref, i):
    prev = pl.multiple_of(jnp.maximum(i - 1, 0) * BLK, BLK)
    own = pl.multiple_of(i * BLK, BLK)
    return jnp.concatenate([ref[pl.ds(prev, BLK), :], ref[pl.ds(own, BLK), :]], axis=0), prev, own


def _swa_meta_operand(ref):
    blk = ref[0:BLK, :]
    swapped = pltpu.roll(blk, 64, 1)
    lo = jnp.where(_half_mask(128, 0), blk, swapped)
    hi = jnp.where(_half_mask(128, 1), blk, swapped)
    meta = jnp.concatenate([lo, lo, hi, hi], axis=1)[PAD_ROWS:BLK, :]
    tiled = jnp.concatenate([meta] * SWA_HEADS, axis=0)
    j = lax.broadcasted_iota(jnp.int32, tiled.shape, 0)
    lane = lax.broadcasted_iota(jnp.int32, tiled.shape, 1)
    return jnp.where(j // N_META == lane // SWA_HD, tiled, jnp.zeros_like(tiled))


def _swa_meta_fold(acc):
    out = jnp.zeros((N_META, 128), F32)
    for hd in range(SWA_HEADS):
        half, kv = hd % 2, hd // 4
        piece = acc[N_META * hd:N_META * (hd + 1), 128 * (hd // 2):128 * (hd // 2) + 128]
        piece = jnp.where(_half_mask(128, half), piece, 0.0)
        out = out + (piece if half == kv else pltpu.roll(piece, 64, 1))
    return out


def _by_head(group, per_head):
    out = jnp.zeros((BLK, BLK), F32)
    for hd, v in enumerate(per_head):
        out = jnp.where(group == hd, v, out)
    return out


def _place(x, kv):
    if kv == 0:
        lo = jnp.where(_half_mask(128, 0), x, jnp.zeros_like(x))
        return lo, pltpu.roll(lo, 64, 1)
    hi = jnp.where(_half_mask(128, 1), x, jnp.zeros_like(x))
    return pltpu.roll(hi, 64, 1), hi


def _swa_fwd(sq, sk, sv, sinks, wn):
    rows = sq.shape[0]
    nb = rows // BLK
    scale = SWA_HD ** -0.5

    def body(q_ref, k_ref, v_ref, sink_ref, wn_ref, o_ref, cat_ref, lse_ref, kp, vp):
        i = pl.program_id(0)

        @pl.when(i == 0)
        def _():
            kp[...] = _swa_meta_operand(k_ref)
            vp[...] = _swa_meta_operand(v_ref)
        own_side, band_ok, meta_ok, group = _swa_masks(i)
        k2, _, _ = _swa_blocks(k_ref, i)
        v2, _, _ = _swa_blocks(v_ref, i)
        kz = (_place(k2, 0), _place(k2, 1))
        vz = (_place(v2, 0), _place(v2, 1))
        q_all = q_ref[...]
        s_meta = jnp.where(meta_ok, _nt(q_all, kp[...]) * scale, NEG_INF)
        s_band, m = [], []
        for hd in range(SWA_HEADS):
            kv, half = hd // 4, hd % 2
            q_pair = q_all[:, 128 * (hd // 2):128 * (hd // 2) + 128]
            s2 = _nt(q_pair, kz[kv][half])
            s = jnp.where(band_ok, jnp.where(own_side, s2[:, BLK:], s2[:, :BLK]) * scale, NEG_INF)
            top = jnp.maximum(jnp.max(s, axis=-1, keepdims=True),
                              jnp.max(jnp.where(group == hd, s_meta, NEG_INF), axis=-1, keepdims=True))
            s_band.append(s)
            m.append(jnp.maximum(top, sink_ref[0, hd]))
        e_meta = jnp.exp(s_meta - _by_head(group, m))
        o_meta = _nn(e_meta.astype(BF16), vp[...])
        outs = []
        for pr in range(4):
            o_pair = o_meta[:, 128 * pr:128 * pr + 128]
            rden = []
            for half in range(2):
                hd = 2 * pr + half
                kv = hd // 4
                e = jnp.exp(s_band[hd] - m[hd])
                den = (jnp.sum(e, axis=-1, keepdims=True)
                       + jnp.sum(jnp.where(group == hd, e_meta, 0.0), axis=-1, keepdims=True)
                       + jnp.exp(sink_ref[0, hd] - m[hd]))
                lse_ref[:, hd:hd + 1] = m[hd] + jnp.log(den)
                rden.append(1.0 / den)
                e2 = jnp.concatenate([jnp.where(own_side, 0.0, e), jnp.where(own_side, e, 0.0)], axis=1).astype(BF16)
                o_pair = o_pair + _nn(e2, vz[kv][half])
            outs.append(o_pair * jnp.where(_half_mask(128, 0), rden[0], rden[1]))
        o = jnp.concatenate(outs, axis=1)
        o_ref[...] = o
        on, _ = _rms(o)
        cat_ref[...] = (on * wn_ref[...]).astype(BF16)

    return pl.pallas_call(
        body, name="swa_fwd", grid=(nb,),
        in_specs=[_row_spec(BLK, 512), VMEM_SPEC, VMEM_SPEC, SMEM_SPEC, VMEM_SPEC],
        out_specs=[_row_spec(BLK, 512), _row_spec(BLK, 512), _row_spec(BLK, SWA_HEADS)],
        out_shape=[jax.ShapeDtypeStruct((rows, 512), F32), jax.ShapeDtypeStruct((rows, 512), BF16),
                   jax.ShapeDtypeStruct((rows, SWA_HEADS), F32)],
        scratch_shapes=[pltpu.VMEM((BLK, 512), BF16), pltpu.VMEM((BLK, 512), BF16)],
        compiler_params=_params(("arbitrary",)),
    )(sq, sk, sv, sinks, wn)


def _swa_bwd(dcat, o_all, sq, sk, sv, lse, sinks, wn):
    rows = sq.shape[0]
    nb = rows // BLK
    scale = SWA_HD ** -0.5

    def body(dc_ref, o_ref, q_ref, k_ref, v_ref, lse_ref, sink_ref, wn_ref, dq_ref, dk_ref, dv_ref, dsink_ref, dwn_ref,
             kp, vp, dkp, dvp):
        i = pl.program_id(0)
        first = i == 0

        @pl.when(first)
        def _():
            dk_ref[...] = jnp.zeros_like(dk_ref)
            dv_ref[...] = jnp.zeros_like(dv_ref)
            dkp[...] = jnp.zeros_like(dkp)
            dvp[...] = jnp.zeros_like(dvp)
            kp[...] = _swa_meta_operand(k_ref)
            vp[...] = _swa_meta_operand(v_ref)
        own_side, band_ok, meta_ok, group = _swa_masks(i)
        k2, prev, own = _swa_blocks(k_ref, i)
        v2, _, _ = _swa_blocks(v_ref, i)
        kz = (_place(k2, 0), _place(k2, 1))
        vz = (_place(v2, 0), _place(v2, 1))
        o = o_ref[...]
        on, ro = _rms(o)
        dc = dc_ref[...]
        _acc_add(dwn_ref, first, _colsum(dc * on))
        do = _rms_bwd(on, ro, wn_ref[...], dc)
        do_o = do * o
        do16 = do.astype(BF16)
        q_all = q_ref[...]
        lse = [lse_ref[:, hd:hd + 1] for hd in range(SWA_HEADS)]
        delta = [jnp.sum(jnp.where(_half_mask(128, hd % 2), do_o[:, 128 * (hd // 2):128 * (hd // 2) + 128], 0.0),
                         axis=-1, keepdims=True) for hd in range(SWA_HEADS)]
        s_meta = jnp.where(meta_ok, _nt(q_all, kp[...]) * scale, NEG_INF)
        p_meta = jnp.exp(s_meta - _by_head(group, lse))
        ds_meta16 = (p_meta * (_nt(do16, vp[...]) - _by_head(group, delta)) * scale).astype(BF16)
        dq_meta = _nn(ds_meta16, kp[...])
        dkp[...] += _tn(ds_meta16, q_all)
        dvp[...] += _tn(p_meta.astype(BF16), do16)
        lane8 = lax.broadcasted_iota(jnp.int32, (1, 128), 1)
        dsink = jnp.zeros((1, 128), F32)
        dk2 = [[jnp.zeros((2 * BLK, 128), F32) for _ in range(2)] for _ in range(2)]
        dv2 = [[jnp.zeros((2 * BLK, 128), F32) for _ in range(2)] for _ in range(2)]
        dqs = []
        for pr in range(4):
            ps = slice(128 * pr, 128 * pr + 128)
            q_pair = q_all[:, ps]
            do_pair = do16[:, ps]
            dq_pair = dq_meta[:, ps]
            for half in range(2):
                hd = 2 * pr + half
                kv = hd // 4

                def window(x2):
                    return jnp.where(own_side, x2[:, BLK:], x2[:, :BLK])

                def unwindow(x):
                    return jnp.concatenate([jnp.where(own_side, 0.0, x), jnp.where(own_side, x, 0.0)], axis=1).astype(BF16)
                s = jnp.where(band_ok, window(_nt(q_pair, kz[kv][half])) * scale, NEG_INF)
                prob = jnp.exp(s - lse[hd])
                dsink = dsink + jnp.where(lane8 == hd, -jnp.sum(jnp.exp(sink_ref[0, hd] - lse[hd]) * delta[hd]), 0.0)
                ds2 = unwindow(prob * (window(_nt(do_pair, vz[kv][half])) - delta[hd]) * scale)
                dq_pair = dq_pair + _nn(ds2, kz[kv][half])
                dk2[kv][half] = dk2[kv][half] + _tn(ds2, q_pair)
                dv2[kv][half] = dv2[kv][half] + _tn(unwindow(prob), do_pair)
            dqs.append(dq_pair)
        dq_ref[...] = jnp.concatenate(dqs, axis=1)
        _acc_add(dsink_ref, first, dsink)
        for ref, acc2 in ((dk_ref, dk2), (dv_ref, dv2)):
            tot = jnp.zeros((2 * BLK, 128), F32)
            for kv in range(2):
                for half in range(2):
                    part = jnp.where(_half_mask(128, half), acc2[kv][half], 0.0)
                    tot = tot + (part if half == kv else pltpu.roll(part, 64, 1))
            ref[pl.ds(prev, BLK), :] += tot[:BLK]
            ref[pl.ds(own, BLK), :] += tot[BLK:]

        @pl.when(i == nb - 1)
        def _():
            dk_ref[PAD_ROWS:BLK, :] += _swa_meta_fold(dkp[...])
            dv_ref[PAD_ROWS:BLK, :] += _swa_meta_fold(dvp[...])

    full = pl.BlockSpec((rows, 128), lambda i: (0, 0))
    return pl.pallas_call(
        body, name="swa_bwd", grid=(nb,),
        in_specs=[_row_spec(BLK, 512), _row_spec(BLK, 512), _row_spec(BLK, 512), VMEM_SPEC, VMEM_SPEC,
                  _row_spec(BLK, SWA_HEADS), SMEM_SPEC, VMEM_SPEC],
        out_specs=[_row_spec(BLK, 512), full, full, _acc_spec(128), _acc_spec(512)],
        out_shape=[jax.ShapeDtypeStruct((rows, 512), F32), jax.ShapeDtypeStruct((rows, 128), F32),
                   jax.ShapeDtypeStruct((rows, 128), F32), jax.ShapeDtypeStruct((8, 128), F32),
                   jax.ShapeDtypeStruct((8, 512), F32)],
        scratch_shapes=[pltpu.VMEM((BLK, 512), BF16), pltpu.VMEM((BLK, 512), BF16),
                        pltpu.VMEM((BLK, 512), F32), pltpu.VMEM((BLK, 512), F32)],
        compiler_params=_params(("arbitrary",)),
    )(dcat, o_all, sq, sk, sv, lse, sinks, wn)


def _mix_out(h, cat_g, cat_s, wout, gpost):
    rows = h.shape[0]
    tm = _row_tile(rows)

    def body(h_ref, cg_ref, cs_ref, w_ref, g_ref, ho_ref, m_ref):
        m = _nn(cg_ref[...], w_ref[0:512, :]) + _nn(cs_ref[...], w_ref[512:1024, :])
        m_ref[...] = m
        mn, _ = _rms(m)
        ho_ref[...] = h_ref[...] + mn * g_ref[...]

    row_f32 = _row_spec(tm, D_MODEL)
    return pl.pallas_call(
        body, name="mix_out", grid=(rows // tm,),
        in_specs=[row_f32, _row_spec(tm, 512), _row_spec(tm, 512), VMEM_SPEC, VMEM_SPEC],
        out_specs=[row_f32, row_f32],
        out_shape=[jax.ShapeDtypeStruct((rows, D_MODEL), F32), jax.ShapeDtypeStruct((rows, D_MODEL), F32)],
        compiler_params=_params(("arbitrary",)),
    )(h, cat_g, cat_s, wout, gpost)


def _mix_out_bwd(dh, m, wout, gpost):
    rows = dh.shape[0]
    tm = _row_tile(rows)

    def body(dh_ref, m_ref, w_ref, g_ref, dcg_ref, dcs_ref, dm_ref, dg_ref):
        first = pl.program_id(0) == 0
        dhv = dh_ref[...]
        mn, rm = _rms(m_ref[...])
        _acc_add(dg_ref, first, _colsum(dhv * mn))
        dm16 = _rms_bwd(mn, rm, g_ref[...], dhv).astype(BF16)
        dm_ref[...] = dm16
        dcat = _nt(dm16, w_ref[...])
        dcg_ref[...] = dcat[:, 0:512]
        dcs_ref[...] = dcat[:, 512:1024]

    row_f32 = _row_spec(tm, D_MODEL)
    return pl.pallas_call(
        body, name="mix_out_bwd", grid=(rows // tm,),
        in_specs=[row_f32, row_f32, VMEM_SPEC, VMEM_SPEC],
        out_specs=[_row_spec(tm, 512), _row_spec(tm, 512), row_f32, _acc_spec(D_MODEL)],
        out_shape=[jax.ShapeDtypeStruct((rows, 512), F32), jax.ShapeDtypeStruct((rows, 512), F32),
                   jax.ShapeDtypeStruct((rows, D_MODEL), BF16), jax.ShapeDtypeStruct((8, D_MODEL), F32)],
        compiler_params=_params(("arbitrary",)),
    )(dh, m, wout, gpost)


def _mix_in_bwd(dh_out, h, g, win_p, wa2_p, cos, sin, loga, ga, dgq, dgk, dgv, dgg, dsq, dsk, dsv, dloga):
    rows = h.shape[0]
    tm = _row_tile(rows)

    def body(dho_ref, h_ref, g_ref, win_ref, wa2_ref, cos_ref, sin_ref, loga_ref, ga_ref,
             dgq_ref, dgk_ref, dgv_ref, dgg_ref, dsq_ref, dsk_ref, dsv_ref, dla_ref,
             dh_ref, dproj_ref, dwa2_ref, dg_ref, dba_ref):
        first = pl.program_id(0) == 0
        dz = dla_ref[...] * (1.0 / GLA_TAU) * (1.0 - jnp.exp(GLA_TAU * loga_ref[...]))
        _acc_add(dba_ref, first, _colsum(dz))
        dga = _nt(dz, wa2_ref[...])
        pa = _tn(ga_ref[...], dz)
        c1, s1 = cos_ref[...], sin_ref[...]
        c4 = jnp.concatenate([c1, c1, c1, c1], axis=1)
        s4 = jnp.concatenate([s1, s1, s1, s1], axis=1)
        dq_r, dk_r = dsq_ref[...], dsk_ref[...]
        dsq = dq_r * c4 - _rot_half(dq_r * s4)
        dsk = dk_r * c1 - _rot_half(dk_r * s1)
        dproj16 = jnp.concatenate(
            [dgq_ref[...], dgk_ref[...], dgv_ref[...], dgg_ref[...], dsq, dsk, dsv_ref[...], dga], axis=1).astype(BF16)
        dproj_ref[...] = dproj16
        dn = _nn(dproj16, win_ref[...])

        @pl.when(first)
        def _():
            dwa2_ref[...] = pa

        @pl.when(jnp.logical_not(first))
        def _():
            dwa2_ref[...] += pa
        hn, rh = _rms(h_ref[...])
        _acc_add(dg_ref, first, _colsum(dn * hn))
        dh_ref[...] = dho_ref[...] + _rms_bwd(hn, rh, g_ref[...], dn)

    rs = lambda c: _row_spec(tm, c)
    return pl.pallas_call(
        body, name="mix_in_bwd", grid=(rows // tm,),
        in_specs=[rs(D_MODEL), rs(D_MODEL), VMEM_SPEC, VMEM_SPEC, VMEM_SPEC, rs(128), rs(128), rs(256), rs(128),
                  rs(256), rs(256), rs(512), rs(512), rs(512), rs(128), rs(128), rs(256)],
        out_specs=[rs(D_MODEL), rs(P_END), pl.BlockSpec((128, 256), lambda i: (0, 0)), _acc_spec(D_MODEL), _acc_spec(256)],
        out_shape=[jax.ShapeDtypeStruct((rows, D_MODEL), F32), jax.ShapeDtypeStruct((rows, P_END), BF16),
                   jax.ShapeDtypeStruct((128, 256), F32), jax.ShapeDtypeStruct((8, D_MODEL), F32),
                   jax.ShapeDtypeStruct((8, 256), F32)],
        compiler_params=_params(("arbitrary",)),
    )(dh_out, h, g, win_p, wa2_p, cos, sin, loga, ga, dgq, dgk, dgv, dgg, dsq, dsk, dsv, dloga)


def _rope_tables(rows):
    pos = (jnp.arange(rows, dtype=jnp.int32) - PAD_ROWS).astype(F32)
    inv_freq = 1.0 / (ROPE_THETA ** (jnp.arange(0, SWA_HD, 2, dtype=F32) / SWA_HD))
    ang = pos[:, None] * inv_freq[None, :]
    ang = jnp.concatenate([ang, ang, ang, ang], axis=-1)
    return jnp.cos(ang), jnp.sin(ang)


def _local_step(x, tgt, front, w, late_weights=None, on_grads=None):
    cos, sin = _rope_tables(x.shape[0] + BLK)
    g = {}

    def tell(group, names):
        for nm in names:
            g[nm] = grads_now[nm]
        return 0.0 if on_grads is None else on_grads(group, {nm: grads_now[nm] for nm in names})

    h0, h1, a1, b1, s1, f1 = _ffn_fwd(x, w["ffn1_pre"], w["wg1"], w["wu1"], w["wd1"], w["ffn1_post"], front=front)
    if late_weights is not None:
        w = {**w, **late_weights("win", f1)}
    gq, gk, gv, gg, sq, sk, sv, ga, loga, bc, n2 = _mix_in(h1, w["mix_pre"], w["win"], w["wa2"], w["b_a"], cos, sin)
    o_g, cat_g, sp = _gla_fwd(gq, gk, gv, gg, bc, w["gla_norm"])
    o_s, cat_s, lse = _swa_fwd(sq, sk, sv, w["sinks"], w["swa_norm"])
    if late_weights is not None:
        w = {**w, **late_weights("rest", lse)}
    h2, m = _mix_out(h1, cat_g, cat_s, w["wout"], w["mix_post"])
    h3, a2, b2, s2, f2, dy, loss = _ffn_fwd(h2, w["ffn2_pre"], w["wg2"], w["wu2"], w["wd2"], w["ffn2_post"], tgt)
    del h3
    dh2, da, db, df, n3, g["ffn2_pre"], g["ffn2_post"] = _ffn_bwd_act(
        dy, h2, a2, b2, f2, w["ffn2_pre"], w["ffn2_post"], w["wg2"], w["wu2"], w["wd2"], "ffn2_bwd_act")
    grads_now = dict(wd2=_wgrad(s2, df, "ffn2_wgrad_down"), wg2=_wgrad(da, n3, "ffn2_wgrad_gate"),
                     wu2=_wgrad(db, n3, "ffn2_wgrad_up"))
    tok = tell("ffn2", ("wd2", "wg2", "wu2"))
    dcg, dcs, dm, g["mix_post"] = _mix_out_bwd(dh2, m, w["wout"], w["mix_post"] + tok)
    dsq, dsk, dsv, g["sinks"], g["swa_norm"] = _swa_bwd(dcs, o_s, sq, sk, sv, lse, w["sinks"], w["swa_norm"])
    dgq, dgk, dgv, dgg, dloga, g["gla_norm"] = _gla_bwd(dcg, o_g, gq, gk, gv, gg, bc, sp, w["gla_norm"])
    dh1, dproj, g["wa2"], g["mix_pre"], g["b_a"] = _mix_in_bwd(
        dh2, h1, w["mix_pre"], w["win"], w["wa2"], cos, sin, loga, ga, dgq, dgk, dgv, dgg, dsq, dsk, dsv, dloga)
    grads_now = dict(wout=jnp.concatenate([_wgrad(cat_g, dm, "wout_wgrad_gla"), _wgrad(cat_s, dm, "wout_wgrad_swa")], axis=0),
                     win=_wgrad(dproj, n2, "win_wgrad"))
    tok = tell("mix", ("wout", "win"))
    dh0, da, db, df, n1, g["ffn1_pre"], g["ffn1_post"] = _ffn_bwd_act(
        dh1, h0, a1, b1, f1, w["ffn1_pre"] + tok, w["ffn1_post"], w["wg1"], w["wu1"], w["wd1"], "ffn1_bwd_act")
    grads_now = dict(wd1=_wgrad(s1, df, "ffn1_wgrad_down"))
    tell("ffn1_down", ("wd1",))
    grads_now = dict(wg1=_wgrad(da, n1, "ffn1_wgrad_gate"))
    tell("ffn1_gate", ("wg1",))
    grads_now = dict(wu1=_wgrad(db, n1, "ffn1_wgrad_up"))
    tell("ffn1_up", ("wu1",))
    return loss[0, 0], dh0, g


def _win_pad_rows(win_t):
    pad = jnp.zeros((P_END - P_GA - 16, win_t.shape[1]), win_t.dtype)
    return jnp.concatenate([win_t[0:1536], win_t[1552:2320], win_t[1536:1552], pad], axis=0)


def _win_unpad_rows(win_p):
    return jnp.concatenate([win_p[0:1536], win_p[P_GA:P_GA + 16], win_p[1536:P_GA]], axis=0)


def _place_on_mesh():
    return lax.axis_index("x"), lax.axis_index("y"), lax.axis_index("c")


def _dev_index(px, py, pc):
    return 4 * px + 2 * py + pc


def _other_devices(x, y, c):
    flip = lambda v, f: 1 - v if f else v
    return [(flip(x, fx), flip(y, fy), flip(c, fc)) for fx in (0, 1) for fy in (0, 1) for fc in (0, 1)][1:]


def _all_gather(shards):
    n = len(shards)

    def body(*refs):
        ins, outs = refs[:n], refs[n:2 * n]
        zeros_ref, send_sems, recv_sems, local_sems = refs[2 * n:]
        zeros_ref[...] = jnp.zeros_like(zeros_ref)
        x, y, c = _place_on_mesh()
        me, sibling = (x, y, c), (x, y, 1 - c)
        chips = [(1 - x, y), (x, 1 - y), (1 - x, 1 - y)]

        def rows(k, px, py, pc):
            r = ins[k].shape[0]
            return outs[k].at[pl.ds(pl.multiple_of(_dev_index(px, py, pc) * r, 8), r), :]

        def copy(k, slot, block, to, src=None):
            return pltpu.make_async_remote_copy(
                src_ref=rows(k, *block) if src is None else src, dst_ref=rows(k, *block),
                send_sem=send_sems.at[k, slot], recv_sem=recv_sems.at[k, slot], device_id=to, device_id_type=MESH)

        local = [pltpu.make_async_copy(ins[k], rows(k, *me), local_sems.at[k]) for k in range(n)]
        sends = []
        for k in range(n):
            local[k].start()
            sends.append(copy(k, 0, me, sibling, src=ins[k]))
            sends += [copy(k, 1 + j, me, (*chip, c), src=ins[k]) for j, chip in enumerate(chips)]
        for cp in sends:
            cp.start()
        for k in range(n):
            for j, chip in enumerate(chips):
                copy(k, 1 + j, (*chip, c), me).wait_recv()
                passed = copy(k, 4 + j, (*chip, c), sibling)
                passed.start()
                sends.append(passed)
        for k in range(n):
            copy(k, 0, sibling, me).wait_recv()
            for j, chip in enumerate(chips):
                copy(k, 4 + j, (*chip, 1 - c), me).wait_recv()
        for cp in sends:
            cp.wait_send()
        for cp in local:
            cp.wait()

    return pl.pallas_call(
        body, name="all_gather_weights",
        in_specs=[ANY_SPEC] * n, out_specs=[ANY_SPEC] * n + [VMEM_SPEC],
        out_shape=[jax.ShapeDtypeStruct((N_DEV * s.shape[0], s.shape[1]), s.dtype) for s in shards]
        + [jax.ShapeDtypeStruct((8, 128), F32)],
        scratch_shapes=[pltpu.SemaphoreType.DMA((n, 7)), pltpu.SemaphoreType.DMA((n, 7)), pltpu.SemaphoreType.DMA((n,))],
    )(*shards)


HBM_SPEC = pl.BlockSpec(memory_space=pltpu.HBM)
SEM_SPEC = pl.BlockSpec(memory_space=pltpu.SEMAPHORE)
DATAFLOW = pltpu.SideEffectType.DATAFLOW_SIDE_EFFECTING


GATHER, SCATTER, SCATTER_CHIPS = "gather", "scatter", "scatter among chips"


def _exchange_peers(kind):
    x, y, c = _place_on_mesh()
    if kind == SCATTER_CHIPS:
        peers = [(1 - x, y, c), (x, 1 - y, c), (1 - x, 1 - y, c)]
        return peers, [2 * p[0] + p[1] for p in peers], 2 * x + y, 4
    peers = _other_devices(x, y, c)
    return peers, [_dev_index(*p) for p in peers], _dev_index(x, y, c), N_DEV


def _exchange_copies(srcs, lands, send_sems, recv_sems, own_sems, kind, arriving):
    peers, theirs, me, blocks = _exchange_peers(kind)
    remote, local = [], []
    for k, (src, land) in enumerate(zip(srcs, lands)):
        r = land.shape[0] // blocks

        def block(ref, d):
            return ref.at[pl.ds(pl.multiple_of(d * r, 8), r), :]

        for f, (peer, him) in enumerate(zip(peers, theirs)):
            mine, his = (him, me) if arriving else (me, him)
            sem = len(peers) * k + f
            remote.append(pltpu.make_async_remote_copy(
                src_ref=src if kind == GATHER else block(src, his), dst_ref=block(land, mine),
                send_sem=send_sems.at[sem], recv_sem=recv_sems.at[sem], device_id=peer, device_id_type=MESH))
        local.append(pltpu.make_async_copy(src if kind == GATHER else block(src, me), block(land, me), own_sems.at[k]))
    return remote, local


def _exchange_start(srcs, kind, name):
    n = len(srcs)
    lands = [lax.empty((N_DEV * s.shape[0], s.shape[1]) if kind == GATHER else s.shape, s.dtype) for s in srcs]
    sems = (3 if kind == SCATTER_CHIPS else 7) * n

    def body(*refs):
        remote, local = _exchange_copies(refs[:n], refs[n:2 * n], *refs[2 * n:2 * n + 3], kind, False)
        for cp in remote + local:
            cp.start()
        refs[-1][...] = jnp.zeros_like(refs[-1])

    both = list(srcs) + list(lands)
    outs = pl.pallas_call(
        body, name=name,
        out_shape=(pltpu.SemaphoreType.DMA((sems,)), pltpu.SemaphoreType.DMA((sems,)), pltpu.SemaphoreType.DMA((n,)),
                   *[pltpu.HBM(a.shape, a.dtype) for a in both], jax.ShapeDtypeStruct((8, 128), F32)),
        in_specs=[HBM_SPEC] * (2 * n), out_specs=(SEM_SPEC, SEM_SPEC, SEM_SPEC, *[HBM_SPEC] * (2 * n), VMEM_SPEC),
        input_output_aliases={i: 3 + i for i in range(2 * n)},
        compiler_params=pltpu.CompilerParams(has_side_effects=DATAFLOW),
    )(*[pltpu.with_memory_space_constraint(a, pltpu.HBM) for a in both])
    return outs[0:3], outs[3:3 + n], outs[3 + n:3 + 2 * n], outs[-1]


def _exchange_wait(started, kind, after, name):
    sems, srcs, lands, _ = started
    n = len(srcs)

    def body(*refs):
        args = (refs[:n], refs[n:2 * n], *refs[2 * n:2 * n + 3], kind)
        going, local = _exchange_copies(*args, False)
        for cp in going:
            cp.wait_send()
        for cp in local:
            cp.wait()
        for cp in _exchange_copies(*args, True)[0]:
            cp.wait_recv()

    both = list(srcs) + list(lands)
    outs = pl.pallas_call(
        body, name=name, out_shape=[pltpu.HBM(a.shape, a.dtype) for a in both],
        in_specs=[HBM_SPEC] * (2 * n) + [SEM_SPEC, SEM_SPEC, SEM_SPEC, ANY_SPEC], out_specs=[HBM_SPEC] * (2 * n),
        input_output_aliases={i: i for i in range(2 * n)},
        compiler_params=pltpu.CompilerParams(has_side_effects=DATAFLOW),
    )(*both, *sems, after)
    return outs[n:]


def _sibling_reduce(part, name):
    r, cols = part.shape[0] // N_DEV, part.shape[1]

    def swap(p_ref, got_ref, send_sems, recv_sems):
        x, y, c = _place_on_mesh()
        copies = [pltpu.make_async_remote_copy(
            src_ref=p_ref.at[pl.ds(pl.multiple_of((2 * j + 1 - c) * r, 8), r), :], dst_ref=got_ref.at[pl.ds(j * r, r), :],
            send_sem=send_sems.at[j], recv_sem=recv_sems.at[j], device_id=(x, y, 1 - c), device_id_type=MESH)
            for j in range(4)]
        for cp in copies:
            cp.start()
        for cp in copies:
            cp.wait()

    got = pl.pallas_call(
        swap, name=name + "_swap", in_specs=[ANY_SPEC], out_specs=ANY_SPEC,
        out_shape=jax.ShapeDtypeStruct((4 * r, cols), part.dtype),
        scratch_shapes=[pltpu.SemaphoreType.DMA((4,)), pltpu.SemaphoreType.DMA((4,))],
    )(part)

    def add(c_ref, mine_ref, got_ref, o_ref):
        del c_ref
        o_ref[...] = (mine_ref[...].astype(F32) + got_ref[...].astype(F32)).astype(o_ref.dtype)

    core = lax.axis_index("c").astype(jnp.int32).reshape(1)
    return pl.pallas_call(
        add, name=name + "_add",
        grid_spec=pltpu.PrefetchScalarGridSpec(
            num_scalar_prefetch=1, grid=(4,),
            in_specs=[pl.BlockSpec((r, cols), lambda j, c_ref: (2 * j + c_ref[0], 0)),
                      pl.BlockSpec((r, cols), lambda j, c_ref: (j, 0))],
            out_specs=pl.BlockSpec((r, cols), lambda j, c_ref: (j, 0))),
        out_shape=jax.ShapeDtypeStruct((4 * r, cols), part.dtype),
        compiler_params=_params(("arbitrary",)),
    )(core, part, got)


def _sum_partials(parts, name, blocks=N_DEV):
    n = len(parts)

    def body(*refs):
        ins, outs = refs[:n], refs[n:]
        first = pl.program_id(0) == 0
        for i_ref, o_ref in zip(ins, outs):
            v = i_ref[...].astype(F32)

            @pl.when(first)
            def _():
                o_ref[...] = v

            @pl.when(jnp.logical_not(first))
            def _():
                o_ref[...] += v

    shapes = [(p.shape[0] // blocks, p.shape[1]) for p in parts]
    return pl.pallas_call(
        body, name=name, grid=(blocks,),
        in_specs=[pl.BlockSpec(s, lambda j: (j, 0)) for s in shapes],
        out_specs=[pl.BlockSpec(s, lambda j: (0, 0)) for s in shapes],
        out_shape=[jax.ShapeDtypeStruct(s, F32) for s in shapes],
        compiler_params=_params(("arbitrary",)),
    )(*parts)


def _all_reduce_small(slab):
    rows, cols = slab.shape

    def body(x_ref, o_ref, gathered, send_sems, recv_sems):
        x, y, c = _place_on_mesh()
        me = _dev_index(x, y, c)
        peers = _other_devices(x, y, c)

        def copy(f, peer):
            return pltpu.make_async_remote_copy(
                src_ref=x_ref, dst_ref=gathered.at[me], send_sem=send_sems.at[f], recv_sem=recv_sems.at[f],
                device_id=peer, device_id_type=MESH)

        def arrival(f, peer):
            return pltpu.make_async_remote_copy(
                src_ref=x_ref, dst_ref=gathered.at[_dev_index(*peer)], send_sem=send_sems.at[f], recv_sem=recv_sems.at[f],
                device_id=peer, device_id_type=MESH)

        sends = [copy(f, peer) for f, peer in enumerate(peers)]
        for cp in sends:
            cp.start()
        gathered[me] = x_ref[...]
        for f, peer in enumerate(peers):
            arrival(f, peer).wait_recv()
        for cp in sends:
            cp.wait_send()
        total = gathered[0]
        for d in range(1, N_DEV):
            total = total + gathered[d]
        o_ref[...] = total

    return pl.pallas_call(
        body, name="all_reduce_small",
        in_specs=[VMEM_SPEC], out_specs=VMEM_SPEC, out_shape=jax.ShapeDtypeStruct((rows, cols), F32),
        scratch_shapes=[pltpu.VMEM((N_DEV, rows, cols), F32), pltpu.SemaphoreType.DMA((7,)), pltpu.SemaphoreType.DMA((7,))],
    )(slab)


def _adamw_update(w, g, m, v):
    m = ADAM_B1 * m + (1.0 - ADAM_B1) * g
    v = ADAM_B2 * v + (1.0 - ADAM_B2) * (g * g)
    m_hat = m * (1.0 / (1.0 - ADAM_B1 ** ADAM_STEP))
    v_hat = v * (1.0 / (1.0 - ADAM_B2 ** ADAM_STEP))
    return -ADAM_LR * (m_hat / (jnp.sqrt(v_hat) + ADAM_EPS) + ADAM_WD * w), m, v


def _sum_adamw(parts, w, m, v, blocks, name):
    shape = w.shape

    def body(p_ref, w_ref, m_ref, v_ref, g_ref, d_ref, mo_ref, vo_ref):
        j = pl.program_id(0)
        part = p_ref[...].astype(F32)

        @pl.when(j == 0)
        def _():
            g_ref[...] = part

        @pl.when(j > 0)
        def _():
            g_ref[...] += part

        @pl.when(j == blocks - 1)
        def _():
            d_ref[...], mo_ref[...], vo_ref[...] = _adamw_update(w_ref[...], g_ref[...], m_ref[...], v_ref[...])

    held = pl.BlockSpec(shape, lambda j: (0, 0))
    return pl.pallas_call(
        body, name=name, grid=(blocks,),
        in_specs=[pl.BlockSpec(shape, lambda j: (j, 0)), held, held, held],
        out_specs=[held] * 4, out_shape=[jax.ShapeDtypeStruct(shape, F32)] * 4,
        compiler_params=_params(("arbitrary",)),
    )(parts, w, m, v)


def _adamw(ws, gs, ms, vs, name):
    n = len(ws)

    def body(*refs):
        w_r, g_r, m_r, v_r = refs[:n], refs[n:2 * n], refs[2 * n:3 * n], refs[3 * n:4 * n]
        d_o, m_o, v_o = refs[4 * n:5 * n], refs[5 * n:6 * n], refs[6 * n:7 * n]
        for k in range(n):
            d_o[k][...], m_o[k][...], v_o[k][...] = _adamw_update(w_r[k][...], g_r[k][...], m_r[k][...], v_r[k][...])

    shapes = [jax.ShapeDtypeStruct(w.shape, F32) for w in ws]
    outs = pl.pallas_call(
        body, name=name, in_specs=[VMEM_SPEC] * (4 * n), out_specs=[VMEM_SPEC] * (3 * n), out_shape=shapes * 3,
        compiler_params=pltpu.CompilerParams(vmem_limit_bytes=56 << 20),
    )(*ws, *gs, *ms, *vs)
    return outs[:n], outs[n:2 * n], outs[2 * n:]


WEIGHT_NAMES = ("meta_tokens", "ffn1_pre_norm", "ffn1_w_gate", "ffn1_w_up", "ffn1_w_down", "ffn1_post_norm", "mix_pre_norm",
                "w_in", "gla_w_a2", "gla_b_a", "gla_out_norm", "swa_sinks", "swa_out_norm", "w_out", "mix_post_norm",
                "ffn2_pre_norm", "ffn2_w_gate", "ffn2_w_up", "ffn2_w_down", "ffn2_post_norm")
WIN_SHARD = D_IN // N_DEV
WIN_SHARD_PAD = 304
SLAB_VECTORS = ("ffn1_pre", "ffn1_post", "mix_pre", "mix_post", "ffn2_pre", "ffn2_post")
SLAB_ROWS = 32


def kernel(x, meta_tokens, ffn1_pre_norm, ffn1_w_gate, ffn1_w_up, ffn1_w_down, ffn1_post_norm, mix_pre_norm, w_in, gla_w_a2, gla_b_a, gla_out_norm, swa_sinks, swa_out_norm, w_out, mix_post_norm, ffn2_pre_norm, ffn2_w_gate, ffn2_w_up, ffn2_w_down, ffn2_post_norm, loss_target, m_meta_tokens, m_ffn1_pre_norm, m_ffn1_w_gate, m_ffn1_w_up, m_ffn1_w_down, m_ffn1_post_norm, m_mix_pre_norm, m_w_in, m_gla_w_a2, m_gla_b_a, m_gla_out_norm, m_swa_sinks, m_swa_out_norm, m_w_out, m_mix_post_norm, m_ffn2_pre_norm, m_ffn2_w_gate, m_ffn2_w_up, m_ffn2_w_down, m_ffn2_post_norm, v_meta_tokens, v_ffn1_pre_norm, v_ffn1_w_gate, v_ffn1_w_up, v_ffn1_w_down, v_ffn1_post_norm, v_mix_pre_norm, v_w_in, v_gla_w_a2, v_gla_b_a, v_gla_out_norm, v_swa_sinks, v_swa_out_norm, v_w_out, v_mix_post_norm, v_ffn2_pre_norm, v_ffn2_w_gate, v_ffn2_w_up, v_ffn2_w_down, v_ffn2_post_norm):
    given = dict(locals())
    W = {n: given[n] for n in WEIGHT_NAMES}
    M = {n: given["m_" + n] for n in WEIGHT_NAMES}
    V = {n: given["v_" + n] for n in WEIGHT_NAMES}
    dev = _dev_index(*_place_on_mesh())

    def t16(w):
        return w[0].T.astype(BF16)

    small = jnp.concatenate([W["meta_tokens"], jnp.pad(W["gla_w_a2"][0], ((0, 0), (0, 96)))], axis=0)
    wg1, wu1, wd1, small_g, gathered_zeros = _all_gather(
        [t16(W["ffn1_w_gate"]), t16(W["ffn1_w_up"]), W["ffn1_w_down"][0].astype(BF16), small])
    def after_zero(shard, zeros):
        return shard + zeros[0:1, 0:1].astype(shard.dtype)
    win_shard = jnp.pad(t16(W["w_in"]), ((0, WIN_SHARD_PAD - WIN_SHARD), (0, 0)))
    win_shard = after_zero(win_shard, gathered_zeros)
    mid = _exchange_start([win_shard], GATHER, "gather_w_in_start")
    late_shards = [after_zero(W["w_out"][0].astype(BF16), mid[3]), t16(W["ffn2_w_gate"]), t16(W["ffn2_w_up"]),
                   W["ffn2_w_down"][0].astype(BF16)]
    late = _exchange_start(late_shards, GATHER, "gather_late_weights_start")

    def late_weights(what, after):
        if what == "win":
            win_g, = _exchange_wait(mid, GATHER, after, "gather_w_in_wait")
            win_t = win_g.reshape(N_DEV, WIN_SHARD_PAD, D_MODEL)[:, :WIN_SHARD].reshape(D_IN, D_MODEL)
            return dict(win=_win_pad_rows(win_t))
        wout, wg2, wu2, wd2 = _exchange_wait(late, GATHER, after, "gather_late_weights_wait")
        return dict(wout=wout, wg2=wg2, wu2=wu2, wd2=wd2)

    small_g = small_g.reshape(N_DEV, 32, 128)
    meta_full = small_g[:, :N_META].transpose(1, 0, 2).reshape(N_META, D_MODEL)
    wa2_full = small_g[:, N_META:, :32].transpose(1, 0, 2).reshape(16, 256)
    w = dict(
        ffn1_pre=W["ffn1_pre_norm"] + late[3][0, 0], ffn1_post=W["ffn1_post_norm"], mix_pre=W["mix_pre_norm"],
        mix_post=W["mix_post_norm"], ffn2_pre=W["ffn2_pre_norm"], ffn2_post=W["ffn2_post_norm"], b_a=W["gla_b_a"],
        gla_norm=W["gla_out_norm"], sinks=W["swa_sinks"], swa_norm=W["swa_out_norm"], wg1=wg1, wu1=wu1, wd1=wd1,
        wa2=jnp.pad(wa2_full, ((0, 112), (0, 0))))

    in_flight = []

    def on_grads(group, grads):
        parts = []
        for nm, p in grads.items():
            if nm == "win":
                p = _win_unpad_rows(p).reshape(N_DEV, WIN_SHARD, D_MODEL)
                p = jnp.pad(p, ((0, 0), (0, WIN_SHARD_PAD - WIN_SHARD), (0, 0))).reshape(N_DEV * WIN_SHARD_PAD, D_MODEL)
            parts.append(p)
        kind = SCATTER_CHIPS if group.startswith("ffn1") else SCATTER
        if kind == SCATTER_CHIPS:
            parts = [_sibling_reduce(p, "pair_" + group) for p in parts]
        started = _exchange_start(parts, kind, "scatter_" + group + "_start")
        in_flight.append((group, list(grads), started, kind))
        return started[3][0, 0]

    front = jnp.concatenate([jnp.zeros((PAD_ROWS, D_MODEL), F32), meta_full], axis=0)
    loss, dh0, g = _local_step(x[0], loss_target[0], front, w, late_weights, on_grads)
    grad_x = dh0[BLK:][None]

    packed = jnp.concatenate([g["b_a"][0:1], g["gla_norm"][0:1], g["sinks"][0:1], g["swa_norm"][0:1]], axis=1)
    slab = jnp.concatenate([g[k][0:1] for k in SLAB_VECTORS] + [packed, jnp.full((1, D_MODEL), loss, F32),
                           g["wa2"][:16].reshape(4, D_MODEL), jnp.zeros((4, D_MODEL), F32), dh0[PAD_ROWS:BLK]], axis=0)
    tot = _all_reduce_small(slab)
    loss = tot[7, 0]
    small_grads = dict(
        ffn1_pre_norm=tot[0:1], ffn1_post_norm=tot[1:2], mix_pre_norm=tot[2:3], mix_post_norm=tot[3:4],
        ffn2_pre_norm=tot[4:5], ffn2_post_norm=tot[5:6], gla_b_a=tot[6:7, 0:256], gla_out_norm=tot[6:7, 256:384],
        swa_sinks=tot[6:7, 384:392], swa_out_norm=tot[6:7, 512:1024],
        gla_w_a2=lax.dynamic_slice_in_dim(tot[8:12].reshape(16, 256), dev * 32, 32, axis=1)[None],
        meta_tokens=lax.dynamic_slice_in_dim(tot[16:32], dev * 128, 128, axis=1))

    big = dict(wg1=("ffn1_w_gate", True), wu1=("ffn1_w_up", True), wd1=("ffn1_w_down", False), win=("w_in", True),
               wout=("w_out", False), wg2=("ffn2_w_gate", True), wu2=("ffn2_w_up", True), wd2=("ffn2_w_down", False))
    grads = dict(small_grads)
    delta, new_m, new_v = {}, {}, {}
    names = [n for n in WEIGHT_NAMES if n not in [full for full, _ in big.values()]]
    two_d = lambda a: a.reshape(-1, a.shape[-1])
    d_, m_, v_ = _adamw([two_d(W[n]) for n in names], [two_d(grads[n]) for n in names],
                        [two_d(M[n]) for n in names], [two_d(V[n]) for n in names], "adamw_small")
    for k, n in enumerate(names):
        delta[n], new_m[n], new_v[n] = d_[k].reshape(W[n].shape), m_[k].reshape(W[n].shape), v_[k].reshape(W[n].shape)

    before_wait = d_[0] + in_flight[-1][2][3][0, 0]
    for group, shorts, started, kind in in_flight:
        lands = _exchange_wait(started, kind, before_wait, "scatter_" + group + "_wait")
        blocks = 4 if kind == SCATTER_CHIPS else N_DEV
        for short, land in zip(shorts, lands):
            n, transposed = big[short]
            to_slab = (lambda a: a[0].T) if transposed else (lambda a: a[0])
            from_slab = (lambda a: a.T[None]) if transposed else (lambda a: a[None])
            if short == "win":
                g_slab = _sum_partials([land], "sum_" + n, blocks)[0][:WIN_SHARD]
                d_, m_, v_ = _adamw([to_slab(W[n])], [g_slab], [to_slab(M[n])], [to_slab(V[n])], "adamw_" + n)
                d_, m_, v_ = d_[0], m_[0], v_[0]
            else:
                g_slab, d_, m_, v_ = _sum_adamw(land, to_slab(W[n]), to_slab(M[n]), to_slab(V[n]), blocks, "adamw_" + n)
            grads[n], delta[n], new_m[n], new_v[n] = from_slab(g_slab), from_slab(d_), from_slab(m_), from_slab(v_)
            before_wait = d_
    return (loss, grad_x, *[grads[n] for n in WEIGHT_NAMES], *[delta[n] for n in WEIGHT_NAMES],
            *[new_m[n] for n in WEIGHT_NAMES], *[new_v[n] for n in WEIGHT_NAMES])
```

```python
import math

import jax
import jax.numpy as jnp
from jax import lax
from jax.experimental import pallas as pl
from jax.experimental.pallas import tpu as pltpu

F32, BF16 = jnp.float32, jnp.bfloat16

D_MODEL = 1024
D_FF = 2816
N_META = 16
BLK = 128
PAD_ROWS = BLK - N_META
GLA_DK = 64
SWA_HD = 64
SWA_HEADS = 8
GLA_TAU = 16.0
NORM_EPS = 1e-6
NEG_INF = -1e30
ROPE_THETA = 10000.0
P_GQ, P_GK, P_GV, P_GG, P_SQ, P_SK, P_SV, P_GA, P_END = 0, 256, 512, 1024, 1536, 2048, 2176, 2304, 2432
D_IN = 2320
IN_SPLITS = (256, 256, 512, 512, 16, 512, 128, 128)
FF_TILE = 2816
WGRAD_TILE_MAX = 2432
N_DEV = 8
MESH = pl.DeviceIdType.MESH

ADAM_LR, ADAM_B1, ADAM_B2, ADAM_EPS, ADAM_WD, ADAM_STEP = 0.001, 0.9, 0.999, 1e-08, 0.01, 10

V7X_VMEM_BYTES = 64 << 20
VMEM_SPEC = pl.BlockSpec(memory_space=pltpu.VMEM)
SMEM_SPEC = pl.BlockSpec(memory_space=pltpu.SMEM)
ANY_SPEC = pl.BlockSpec(memory_space=pl.ANY)


def _params(semantics, vmem_mb=56):
    return pltpu.CompilerParams(dimension_semantics=semantics, vmem_limit_bytes=vmem_mb << 20)


def _row_tile(rows):
    return 416 if rows % 416 == 0 else BLK


def _nn(a, b):
    return lax.dot_general(a, b, (((1,), (0,)), ((), ())), preferred_element_type=F32)


def _nt(a, b):
    return lax.dot_general(a, b, (((1,), (1,)), ((), ())), preferred_element_type=F32)


def _tn(a, b):
    return lax.dot_general(a, b, (((0,), (0,)), ((), ())), preferred_element_type=F32)


def _rms(x):
    r = lax.rsqrt(jnp.mean(x * x, axis=-1, keepdims=True) + NORM_EPS)
    return x * r, r


def _rms_bwd(xn, r, w, dy):
    g = dy * w
    return r * (g - xn * jnp.mean(g * xn, axis=-1, keepdims=True))


def _sigmoid(x):
    return 1.0 / (1.0 + jnp.exp(-x))


def _colsum(x):
    return jnp.sum(x, axis=0, keepdims=True)


def _split_bf16(x):
    hi = x.astype(BF16)
    lo = (x - hi.astype(F32)).astype(BF16)
    return hi, lo


def _tri(lower):
    r = lax.broadcasted_iota(jnp.int32, (BLK, BLK), 0)
    c = lax.broadcasted_iota(jnp.int32, (BLK, BLK), 1)
    return (r >= c) if lower else (c >= r)


def _half_mask(width, half):
    lane = lax.broadcasted_iota(jnp.int32, (1, width), 1)
    return ((lane % 128) < 64) if half == 0 else ((lane % 128) >= 64)


def _rot_half(x):
    w = x.shape[-1]
    lane = lax.broadcasted_iota(jnp.int32, (1, w), 1)
    return jnp.where((lane % SWA_HD) < SWA_HD // 2, -pltpu.roll(x, w - SWA_HD // 2, 1), pltpu.roll(x, SWA_HD // 2, 1))


def _row_spec(tm, cols):
    return pl.BlockSpec((tm, cols), lambda i: (i, 0))


def _acc_spec(cols):
    return pl.BlockSpec((8, cols), lambda i: (0, 0))


def _acc_add(ref, first, value):
    @pl.when(first)
    def _():
        ref[...] = jnp.zeros_like(ref)
    ref[0:1, :] += value


def _behind_front(ref, i, tm, front):
    blk = ref[...]
    return jnp.where(i == 0, jnp.concatenate([front, blk[0:tm - BLK]], axis=0), blk)


def _ffn_fwd(h, gpre, wg_t, wu_t, wd, gpost, tgt=None, front=None):
    with_loss, with_front = tgt is not None, front is not None
    rows = h.shape[0] + (BLK if with_front else 0)
    tm = _row_tile(rows)
    nf = D_FF // FF_TILE

    def body(*refs):
        refs = list(refs)
        h_ref, gpre_ref, wg_ref, wu_ref, wd_ref, gpost_ref = refs[:6]
        del refs[:6]
        front_ref = refs.pop(0) if with_front else None
        t_ref = refs.pop(0) if with_loss else None
        h0_ref = refs.pop(0) if with_front else None
        ho_ref, a_ref, b_ref, s_ref, f_ref = refs[:5]
        dy_ref, loss_ref = refs[5:7] if with_loss else (None, None)
        acc = refs[-1]
        i = pl.program_id(0)
        if with_front:
            h_in = _behind_front(h_ref, i, tm, front_ref[...])
            h0_ref[...] = h_in
        else:
            h_in = h_ref[...]
        hn, _ = _rms(h_in)
        n16 = (hn * gpre_ref[...]).astype(BF16)
        for j in range(nf):
            cols = slice(j * FF_TILE, (j + 1) * FF_TILE)
            a = _nt(n16, wg_ref[cols, :])
            b = _nt(n16, wu_ref[cols, :])
            a_ref[:, cols] = a.astype(BF16)
            b_ref[:, cols] = b.astype(BF16)
            s16 = (a * _sigmoid(a) * b).astype(BF16)
            s_ref[:, cols] = s16
            part = _nn(s16, wd_ref[cols, :])
            if j == 0:
                acc[...] = part
            else:
                acc[...] += part
        f = acc[...]
        f_ref[...] = f
        fn, _ = _rms(f)
        y = h_in + 0.5 * (fn * gpost_ref[...])
        ho_ref[...] = y
        if with_loss:
            row = i * tm + lax.broadcasted_iota(jnp.int32, (tm, 1), 0)
            err = jnp.where(row >= BLK, y - _behind_front(t_ref, i, tm, jnp.zeros((BLK, D_MODEL), F32)), 0.0)
            dy_ref[...] = err * (1.0 / D_MODEL)
            part = 0.5 * jnp.sum(jnp.sum(err * err, axis=-1, keepdims=True) * (1.0 / D_MODEL), axis=0, keepdims=True)

            @pl.when(i == 0)
            def _():
                loss_ref[...] = jnp.zeros_like(loss_ref)
            loss_ref[...] += part

    row_f32 = _row_spec(tm, D_MODEL)
    behind = pl.BlockSpec((pl.Element(tm), pl.Element(D_MODEL)),
                          lambda i: (pl.multiple_of(jnp.maximum(i * tm - BLK, 0), math.gcd(tm, BLK)), 0))
    in_specs = [behind if with_front else row_f32, VMEM_SPEC, VMEM_SPEC, VMEM_SPEC, VMEM_SPEC, VMEM_SPEC]
    out_specs = [row_f32, _row_spec(tm, D_FF), _row_spec(tm, D_FF), _row_spec(tm, D_FF), row_f32]
    out_shape = [jax.ShapeDtypeStruct((rows, D_MODEL), F32), jax.ShapeDtypeStruct((rows, D_FF), BF16),
                 jax.ShapeDtypeStruct((rows, D_FF), BF16), jax.ShapeDtypeStruct((rows, D_FF), BF16),
                 jax.ShapeDtypeStruct((rows, D_MODEL), F32)]
    args = [h, gpre, wg_t, wu_t, wd, gpost]
    if with_front:
        in_specs.append(VMEM_SPEC)
        args.append(front)
        out_specs.insert(0, row_f32)
        out_shape.insert(0, jax.ShapeDtypeStruct((rows, D_MODEL), F32))
    if with_loss:
        in_specs.append(behind)
        args.append(tgt)
        out_specs += [row_f32, pl.BlockSpec((8, 128), lambda i: (0, 0))]
        out_shape += [jax.ShapeDtypeStruct((rows, D_MODEL), F32), jax.ShapeDtypeStruct((8, 128), F32)]
    return pl.pallas_call(
        body, name="ffn_fwd_loss" if with_loss else "ffn_fwd", grid=(rows // tm,),
        in_specs=in_specs, out_specs=out_specs, out_shape=out_shape,
        scratch_shapes=[pltpu.VMEM((tm, D_MODEL), F32)],
        compiler_params=_params(("arbitrary",)),
    )(*args)


def _ffn_bwd_act(dh_out, h, a, b, f, gpre, gpost, wg_t, wu_t, wd, name):
    rows = h.shape[0]
    tm = _row_tile(rows)
    nf = D_FF // FF_TILE

    def body(dho_ref, h_ref, a_ref, b_ref, f_ref, gpre_ref, gpost_ref, wg_ref, wu_ref, wd_ref,
             dh_ref, da_ref, db_ref, df_ref, n_ref, dgpre_ref, dgpost_ref, acc):
        first = pl.program_id(0) == 0
        dho = dho_ref[...]
        drr = 0.5 * dho
        fn, rf = _rms(f_ref[...])
        _acc_add(dgpost_ref, first, _colsum(drr * fn))
        df16 = _rms_bwd(fn, rf, gpost_ref[...], drr).astype(BF16)
        df_ref[...] = df16
        hn, rh = _rms(h_ref[...])
        n_ref[...] = (hn * gpre_ref[...]).astype(BF16)
        for j in range(nf):
            cols = slice(j * FF_TILE, (j + 1) * FF_TILE)
            ds = _nt(df16, wd_ref[cols, :])
            av = a_ref[:, cols].astype(F32)
            bv = b_ref[:, cols].astype(F32)
            sg = _sigmoid(av)
            db16 = (ds * (av * sg)).astype(BF16)
            da16 = (ds * bv * (sg * (1.0 + av * (1.0 - sg)))).astype(BF16)
            da_ref[:, cols] = da16
            db_ref[:, cols] = db16
            part = _nn(da16, wg_ref[cols, :]) + _nn(db16, wu_ref[cols, :])
            if j == 0:
                acc[...] = part
            else:
                acc[...] += part
        dn = acc[...]
        _acc_add(dgpre_ref, first, _colsum(dn * hn))
        dh_ref[...] = dho + _rms_bwd(hn, rh, gpre_ref[...], dn)

    row_f32 = _row_spec(tm, D_MODEL)
    row_ff = _row_spec(tm, D_FF)
    return pl.pallas_call(
        body, name=name, grid=(rows // tm,),
        in_specs=[row_f32, row_f32, row_ff, row_ff, row_f32, VMEM_SPEC, VMEM_SPEC, VMEM_SPEC, VMEM_SPEC, VMEM_SPEC],
        out_specs=[row_f32, row_ff, row_ff, row_f32, row_f32, _acc_spec(D_MODEL), _acc_spec(D_MODEL)],
        out_shape=[jax.ShapeDtypeStruct((rows, D_MODEL), F32), jax.ShapeDtypeStruct((rows, D_FF), BF16),
                   jax.ShapeDtypeStruct((rows, D_FF), BF16), jax.ShapeDtypeStruct((rows, D_MODEL), BF16),
                   jax.ShapeDtypeStruct((rows, D_MODEL), BF16), jax.ShapeDtypeStruct((8, D_MODEL), F32),
                   jax.ShapeDtypeStruct((8, D_MODEL), F32)],
        scratch_shapes=[pltpu.VMEM((tm, D_MODEL), F32)],
        compiler_params=_params(("arbitrary",), vmem_mb=62),
    )(dh_out, h, a, b, f, gpre, gpost, wg_t, wu_t, wd)


def _wgrad(lhs, rhs, name):
    rows, width = lhs.shape
    tm = rows if rows % 1664 == 0 else BLK
    tf = 256 if width % 256 == 0 else 128
    nr = rows // tm

    def body(l_ref, r_ref, o_ref, acc):
        i = pl.program_id(1)
        part = _tn(l_ref[...], r_ref[...])

        @pl.when(i == 0)
        def _():
            acc[...] = part

        @pl.when(i > 0)
        def _():
            acc[...] += part

        @pl.when(i == nr - 1)
        def _():
            o_ref[...] = acc[...].astype(BF16)

    l_spec = pl.BlockSpec((tm, tf), lambda j, i: (i, j))
    r_spec = pl.BlockSpec((tm, D_MODEL), lambda j, i: (i, 0))
    return pl.pallas_call(
        body, name=name, grid=(width // tf, nr),
        in_specs=[l_spec, r_spec],
        out_specs=pl.BlockSpec((tf, D_MODEL), lambda j, i: (j, 0)),
        out_shape=jax.ShapeDtypeStruct((width, D_MODEL), BF16),
        scratch_shapes=[pltpu.VMEM((tf, D_MODEL), F32)],
        compiler_params=_params(("arbitrary", "arbitrary")),
    )(lhs, rhs)


def _chunk_cumsum(x, lower):
    tri = jnp.where(_tri(lower), 1.0, 0.0).astype(BF16)
    hi, lo = _split_bf16(x)
    return _nn(tri, hi) + _nn(tri, lo)


def _mix_in(h, g, win_p, wa2_p, b_a, cos, sin):
    rows = h.shape[0]
    tm = 640 if rows % 640 == 0 else BLK

    def body(h_ref, g_ref, win_ref, wa2_ref, ba_ref, cos_ref, sin_ref,
             gq_ref, gk_ref, gv_ref, gg_ref, sq_ref, sk_ref, sv_ref, ga_ref, loga_ref, bc_ref, n_ref):
        hn, _ = _rms(h_ref[...])
        n16 = (hn * g_ref[...]).astype(BF16)
        n_ref[...] = n16
        proj = _nt(n16, win_ref[...])
        gq_ref[...] = proj[:, P_GQ:P_GK]
        gk_ref[...] = proj[:, P_GK:P_GV]
        gv_ref[...] = proj[:, P_GV:P_GG].astype(BF16)
        gg_ref[...] = proj[:, P_GG:P_SQ]
        c1, s1 = cos_ref[...], sin_ref[...]
        c4 = jnp.concatenate([c1, c1, c1, c1], axis=1)
        s4 = jnp.concatenate([s1, s1, s1, s1], axis=1)
        sq = proj[:, P_SQ:P_SK]
        sk = proj[:, P_SK:P_SV]
        sq_ref[...] = (sq * c4 + _rot_half(sq) * s4).astype(BF16)
        sk_ref[...] = (sk * c1 + _rot_half(sk) * s1).astype(BF16)
        sv_ref[...] = proj[:, P_SV:P_GA].astype(BF16)
        ga = proj[:, P_GA:P_END]
        ga_ref[...] = ga
        z = _nn(ga, wa2_ref[...]) + ba_ref[...]
        loga = (jnp.minimum(z, 0.0) - jnp.log(1.0 + jnp.exp(-jnp.abs(z)))) * (1.0 / GLA_TAU)
        loga_ref[...] = loga
        for c in range(tm // BLK):
            rs = slice(c * BLK, (c + 1) * BLK)
            bc_ref[rs, :] = _chunk_cumsum(loga[rs, :], True)

    f32 = lambda c: jax.ShapeDtypeStruct((rows, c), F32)
    b16 = lambda c: jax.ShapeDtypeStruct((rows, c), BF16)
    rs = lambda c: _row_spec(tm, c)
    return pl.pallas_call(
        body, name="mix_in", grid=(rows // tm,),
        in_specs=[rs(D_MODEL), VMEM_SPEC, VMEM_SPEC, VMEM_SPEC, VMEM_SPEC, rs(128), rs(128)],
        out_specs=[rs(256), rs(256), rs(512), rs(512), rs(512), rs(128), rs(128), rs(128), rs(256), rs(256), rs(D_MODEL)],
        out_shape=[f32(256), f32(256), b16(512), f32(512), b16(512), b16(128), b16(128), f32(128), f32(256), f32(256),
                   b16(D_MODEL)],
        compiler_params=_params(("arbitrary",)),
    )(h, g, win_p, wa2_p, b_a, cos, sin)


def _gla_factors(q, k, bc):
    bm = bc[BLK // 2 - 1:BLK // 2, :]
    bl = bc[BLK - 1:BLK, :]
    e_q, e_k, e_qe, e_kd = jnp.exp(bc - bm), jnp.exp(bm - bc), jnp.exp(bc), jnp.exp(bl - bc)
    return (q * e_q, k * e_k, q * e_qe, k * e_kd), (e_q, e_k, e_qe, e_kd), jnp.exp(bl)


def _gla_fwd(gq, gk, gv, gg, bc, wgn):
    rows = gq.shape[0]
    nc = rows // BLK
    scale = GLA_DK ** -0.5

    def body(q_ref, k_ref, v_ref, gg_ref, bc_ref, wgn_ref, o_ref, cat_ref, sp_ref, st):
        @pl.when(pl.program_id(0) == 0)
        def _():
            st[...] = jnp.zeros_like(st)
        low = _tri(True)
        wgn_v = wgn_ref[...]
        for p in range(2):
            sl = slice(128 * p, 128 * p + 128)
            (qt, kt, qe, kd), _, ebl = _gla_factors(q_ref[:, sl] * scale, k_ref[:, sl], bc_ref[:, sl])
            s_prev = st[p]
            sp_ref[0, p] = s_prev
            s16 = s_prev.astype(BF16)
            qt16 = qt.astype(BF16)
            s_new = s_prev * ebl
            for hh in range(2):
                hs = slice(128 * (2 * p + hh), 128 * (2 * p + hh) + 128)
                lm = _half_mask(128, hh)
                vh = v_ref[:, hs]
                pm = jnp.where(low, _nt(qt16, jnp.where(lm, kt, 0.0).astype(BF16)), 0.0)
                o = _nn(pm.astype(BF16), vh) + _nt(jnp.where(lm, qe, 0.0).astype(BF16), s16)
                s_new = s_new + _tn(vh, jnp.where(lm, kd, 0.0).astype(BF16))
                o_ref[:, hs] = o
                on, _ = _rms(o)
                gate = gg_ref[:, hs]
                cat_ref[:, hs] = (on * wgn_v * (gate * _sigmoid(gate))).astype(BF16)
            st[p] = s_new

    rs = lambda c: _row_spec(BLK, c)
    return pl.pallas_call(
        body, name="gla_fwd", grid=(nc,),
        in_specs=[rs(256), rs(256), rs(512), rs(512), rs(256), VMEM_SPEC],
        out_specs=[rs(512), rs(512), pl.BlockSpec((1, 2, 128, 128), lambda i: (i, 0, 0, 0))],
        out_shape=[jax.ShapeDtypeStruct((rows, 512), F32), jax.ShapeDtypeStruct((rows, 512), BF16),
                   jax.ShapeDtypeStruct((nc, 2, 128, 128), F32)],
        scratch_shapes=[pltpu.VMEM((2, 128, 128), F32)],
        compiler_params=_params(("arbitrary",)),
    )(gq, gk, gv, gg, bc, wgn)


def _gla_bwd(dcat, o_all, gq, gk, gv, gg, bc, sp, wgn):
    rows = gq.shape[0]
    nc = rows // BLK
    scale = GLA_DK ** -0.5

    def body(dc_ref, o_ref, q_ref, k_ref, v_ref, gg_ref, bc_ref, sp_ref, wgn_ref,
             dq_ref, dk_ref, dv_ref, dgg_ref, dla_ref, dwgn_ref, dst):
        first = pl.program_id(0) == 0

        @pl.when(first)
        def _():
            dst[...] = jnp.zeros_like(dst)
        low, upp = _tri(True), _tri(False)
        last_row = lax.broadcasted_iota(jnp.int32, (BLK, 1), 0) == BLK - 1
        wgn_v = wgn_ref[...]
        dwgn = jnp.zeros((1, 128), F32)
        for p in range(2):
            sl = slice(128 * p, 128 * p + 128)
            (qt, kt, qe, kd), (e_q, e_k, e_qe, e_kd), ebl = _gla_factors(
                q_ref[:, sl] * scale, k_ref[:, sl], bc_ref[:, sl])
            s_prev = sp_ref[0, p]
            s16 = s_prev.astype(BF16)
            ds_next = dst[p]
            ds16 = ds_next.astype(BF16)
            qt16 = qt.astype(BF16)
            ds_new = ds_next * ebl
            dqt = jnp.zeros((BLK, 128), F32)
            dkt = jnp.zeros((BLK, 128), F32)
            dqe = jnp.zeros((BLK, 128), F32)
            dkd = jnp.zeros((BLK, 128), F32)
            for hh in range(2):
                hs = slice(128 * (2 * p + hh), 128 * (2 * p + hh) + 128)
                lm = _half_mask(128, hh)
                on, ro = _rms(o_ref[:, hs])
                gate = gg_ref[:, hs]
                sg = _sigmoid(gate)
                si = gate * sg
                dog = dc_ref[:, hs]
                dwgn = dwgn + _colsum(dog * si * on)
                dgg_ref[:, hs] = dog * (on * wgn_v) * (sg * (1.0 + gate * (1.0 - sg)))
                do16 = _rms_bwd(on, ro, wgn_v, dog * si).astype(BF16)
                vh = v_ref[:, hs]
                ktm16 = jnp.where(lm, kt, 0.0).astype(BF16)
                qtm16 = jnp.where(lm, qt, 0.0).astype(BF16)
                qem16 = jnp.where(lm, qe, 0.0).astype(BF16)
                kdm16 = jnp.where(lm, kd, 0.0).astype(BF16)
                p_t = jnp.where(upp, _nt(ktm16, qt16), 0.0)
                dp_t = jnp.where(upp, _nt(vh, do16), 0.0)
                dp = jnp.where(low, _nt(do16, vh), 0.0)
                dv_ref[:, hs] = _nn(p_t.astype(BF16), do16) + _nt(kdm16, ds16)
                dqt = dqt + _nn(dp.astype(BF16), ktm16)
                dkt = dkt + _nn(dp_t.astype(BF16), qtm16)
                dqe = dqe + jnp.where(lm, _nn(do16, s16), 0.0)
                dkd = dkd + jnp.where(lm, _nn(vh, ds16), 0.0)
                ds_new = ds_new + _tn(do16, qem16)
            debl = _colsum(ds_next * s_prev)
            dq_ref[:, sl] = (dqt * e_q + dqe * e_qe) * scale
            dk_ref[:, sl] = dkt * e_k + dkd * e_kd
            dkd_kd = dkd * kd
            db = dqt * qt - dkt * kt + dqe * qe - dkd_kd
            db = jnp.where(last_row, db + (_colsum(dkd_kd) + debl * ebl), db)
            dla_ref[:, sl] = _chunk_cumsum(db, False)
            dst[p] = ds_new
        _acc_add(dwgn_ref, first, dwgn)

    rev = lambda c: pl.BlockSpec((BLK, c), lambda i: (nc - 1 - i, 0))
    f32 = lambda c: jax.ShapeDtypeStruct((rows, c), F32)
    return pl.pallas_call(
        body, name="gla_bwd", grid=(nc,),
        in_specs=[rev(512), rev(512), rev(256), rev(256), rev(512), rev(512), rev(256),
                  pl.BlockSpec((1, 2, 128, 128), lambda i: (nc - 1 - i, 0, 0, 0)), VMEM_SPEC],
        out_specs=[rev(256), rev(256), rev(512), rev(512), rev(256), _acc_spec(128)],
        out_shape=[f32(256), f32(256), f32(512), f32(512), f32(256), jax.ShapeDtypeStruct((8, 128), F32)],
        scratch_shapes=[pltpu.VMEM((2, 128, 128), F32)],
        compiler_params=_params(("arbitrary",)),
    )(dcat, o_all, gq, gk, gv, gg, bc, sp, wgn)


def _swa_masks(i):
    t = lax.broadcasted_iota(jnp.int32, (BLK, BLK), 0)
    c = lax.broadcasted_iota(jnp.int32, (BLK, BLK), 1)
    own_side = c <= t
    band_ok = i >= jnp.where(own_side, 1, 2)
    meta_ok = (c % N_META) <= jnp.where(i >= 1, N_META, t - PAD_ROWS)
    return own_side, band_ok, meta_ok, c // N_META


def _swa_blocks(ref, i):
    prev = pl.multiple_of(jnp.maximum(i - 1, 0) * BLK, BLK)
    own = pl.multiple_of(i * BLK, BLK)
    return jnp.concatenate([ref[pl.ds(prev, BLK), :], ref[pl.ds(own, BLK), :]], axis=0), prev, own


def _swa_meta_operand(ref):
    blk = ref[0:BLK, :]
    swapped = pltpu.roll(blk, 64, 1)
    lo = jnp.where(_half_mask(128, 0), blk, swapped)
    hi = jnp.where(_half_mask(128, 1), blk, swapped)
    meta = jnp.concatenate([lo, lo, hi, hi], axis=1)[PAD_ROWS:BLK, :]
    tiled = jnp.concatenate([meta] * SWA_HEADS, axis=0)
    j = lax.broadcasted_iota(jnp.int32, tiled.shape, 0)
    lane = lax.broadcasted_iota(jnp.int32, tiled.shape, 1)
    return jnp.where(j // N_META == lane // SWA_HD, tiled, jnp.zeros_like(tiled))


def _swa_meta_fold(acc):
    out = jnp.zeros((N_META, 128), F32)
    for hd in range(SWA_HEADS):
        half, kv = hd % 2, hd // 4
        piece = acc[N_META * hd:N_META * (hd + 1), 128 * (hd // 2):128 * (hd // 2) + 128]
        piece = jnp.where(_half_mask(128, half), piece, 0.0)
        out = out + (piece if half == kv else pltpu.roll(piece, 64, 1))
    return out


def _by_head(group, per_head):
    out = jnp.zeros((BLK, BLK), F32)
    for hd, v in enumerate(per_head):
        out = jnp.where(group == hd, v, out)
    return out


def _place(x, kv):
    if kv == 0:
        lo = jnp.where(_half_mask(128, 0), x, jnp.zeros_like(x))
        return lo, pltpu.roll(lo, 64, 1)
    hi = jnp.where(_half_mask(128, 1), x, jnp.zeros_like(x))
    return pltpu.roll(hi, 64, 1), hi


def _swa_fwd(sq, sk, sv, sinks, wn):
    rows = sq.shape[0]
    nb = rows // BLK
    scale = SWA_HD ** -0.5

    def body(q_ref, k_ref, v_ref, sink_ref, wn_ref, o_ref, cat_ref, lse_ref, kp, vp):
        i = pl.program_id(0)

        @pl.when(i == 0)
        def _():
            kp[...] = _swa_meta_operand(k_ref)
            vp[...] = _swa_meta_operand(v_ref)
        own_side, band_ok, meta_ok, group = _swa_masks(i)
        k2, _, _ = _swa_blocks(k_ref, i)
        v2, _, _ = _swa_blocks(v_ref, i)
        kz = (_place(k2, 0), _place(k2, 1))
        vz = (_place(v2, 0), _place(v2, 1))
        q_all = q_ref[...]
        s_meta = jnp.where(meta_ok, _nt(q_all, kp[...]) * scale, NEG_INF)
        s_band, m = [], []
        for hd in range(SWA_HEADS):
            kv, half = hd // 4, hd % 2
            q_pair = q_all[:, 128 * (hd // 2):128 * (hd // 2) + 128]
            s2 = _nt(q_pair, kz[kv][half])
            s = jnp.where(band_ok, jnp.where(own_side, s2[:, BLK:], s2[:, :BLK]) * scale, NEG_INF)
            top = jnp.maximum(jnp.max(s, axis=-1, keepdims=True),
                              jnp.max(jnp.where(group == hd, s_meta, NEG_INF), axis=-1, keepdims=True))
            s_band.append(s)
            m.append(jnp.maximum(top, sink_ref[0, hd]))
        e_meta = jnp.exp(s_meta - _by_head(group, m))
        o_meta = _nn(e_meta.astype(BF16), vp[...])
        outs = []
        for pr in range(4):
            o_pair = o_meta[:, 128 * pr:128 * pr + 128]
            rden = []
            for half in range(2):
                hd = 2 * pr + half
                kv = hd // 4
                e = jnp.exp(s_band[hd] - m[hd])
                den = (jnp.sum(e, axis=-1, keepdims=True)
                       + jnp.sum(jnp.where(group == hd, e_meta, 0.0), axis=-1, keepdims=True)
                       + jnp.exp(sink_ref[0, hd] - m[hd]))
                lse_ref[:, hd:hd + 1] = m[hd] + jnp.log(den)
                rden.append(1.0 / den)
                e2 = jnp.concatenate([jnp.where(own_side, 0.0, e), jnp.where(own_side, e, 0.0)], axis=1).astype(BF16)
                o_pair = o_pair + _nn(e2, vz[kv][half])
            outs.append(o_pair * jnp.where(_half_mask(128, 0), rden[0], rden[1]))
        o = jnp.concatenate(outs, axis=1)
        o_ref[...] = o
        on, _ = _rms(o)
        cat_ref[...] = (on * wn_ref[...]).astype(BF16)

    return pl.pallas_call(
        body, name="swa_fwd", grid=(nb,),
        in_specs=[_row_spec(BLK, 512), VMEM_SPEC, VMEM_SPEC, SMEM_SPEC, VMEM_SPEC],
        out_specs=[_row_spec(BLK, 512), _row_spec(BLK, 512), _row_spec(BLK, SWA_HEADS)],
        out_shape=[jax.ShapeDtypeStruct((rows, 512), F32), jax.ShapeDtypeStruct((rows, 512), BF16),
                   jax.ShapeDtypeStruct((rows, SWA_HEADS), F32)],
        scratch_shapes=[pltpu.VMEM((BLK, 512), BF16), pltpu.VMEM((BLK, 512), BF16)],
        compiler_params=_params(("arbitrary",)),
    )(sq, sk, sv, sinks, wn)


def _swa_bwd(dcat, o_all, sq, sk, sv, lse, sinks, wn):
    rows = sq.shape[0]
    nb = rows // BLK
    scale = SWA_HD ** -0.5

    def body(dc_ref, o_ref, q_ref, k_ref, v_ref, lse_ref, sink_ref, wn_ref, dq_ref, dk_ref, dv_ref, dsink_ref, dwn_ref,
             kp, vp, dkp, dvp):
        i = pl.program_id(0)
        first = i == 0

        @pl.when(first)
        def _():
            dk_ref[...] = jnp.zeros_like(dk_ref)
            dv_ref[...] = jnp.zeros_like(dv_ref)
            dkp[...] = jnp.zeros_like(dkp)
            dvp[...] = jnp.zeros_like(dvp)
            kp[...] = _swa_meta_operand(k_ref)
            vp[...] = _swa_meta_operand(v_ref)
        own_side, band_ok, meta_ok, group = _swa_masks(i)
        k2, prev, own = _swa_blocks(k_ref, i)
        v2, _, _ = _swa_blocks(v_ref, i)
        kz = (_place(k2, 0), _place(k2, 1))
        vz = (_place(v2, 0), _place(v2, 1))
        o = o_ref[...]
        on, ro = _rms(o)
        dc = dc_ref[...]
        _acc_add(dwn_ref, first, _colsum(dc * on))
        do = _rms_bwd(on, ro, wn_ref[...], dc)
        do_o = do * o
        do16 = do.astype(BF16)
        q_all = q_ref[...]
        lse = [lse_ref[:, hd:hd + 1] for hd in range(SWA_HEADS)]
        delta = [jnp.sum(jnp.where(_half_mask(128, hd % 2), do_o[:, 128 * (hd // 2):128 * (hd // 2) + 128], 0.0),
                         axis=-1, keepdims=True) for hd in range(SWA_HEADS)]
        s_meta = jnp.where(meta_ok, _nt(q_all, kp[...]) * scale, NEG_INF)
        p_meta = jnp.exp(s_meta - _by_head(group, lse))
        ds_meta16 = (p_meta * (_nt(do16, vp[...]) - _by_head(group, delta)) * scale).astype(BF16)
        dq_meta = _nn(ds_meta16, kp[...])
        dkp[...] += _tn(ds_meta16, q_all)
        dvp[...] += _tn(p_meta.astype(BF16), do16)
        lane8 = lax.broadcasted_iota(jnp.int32, (1, 128), 1)
        dsink = jnp.zeros((1, 128), F32)
        dk2 = [[jnp.zeros((2 * BLK, 128), F32) for _ in range(2)] for _ in range(2)]
        dv2 = [[jnp.zeros((2 * BLK, 128), F32) for _ in range(2)] for _ in range(2)]
        dqs = []
        for pr in range(4):
            ps = slice(128 * pr, 128 * pr + 128)
            q_pair = q_all[:, ps]
            do_pair = do16[:, ps]
            dq_pair = dq_meta[:, ps]
            for half in range(2):
                hd = 2 * pr + half
                kv = hd // 4

                def window(x2):
                    return jnp.where(own_side, x2[:, BLK:], x2[:, :BLK])

                def unwindow(x):
                    return jnp.concatenate([jnp.where(own_side, 0.0, x), jnp.where(own_side, x, 0.0)], axis=1).astype(BF16)
                s = jnp.where(band_ok, window(_nt(q_pair, kz[kv][half])) * scale, NEG_INF)
                prob = jnp.exp(s - lse[hd])
                dsink = dsink + jnp.where(lane8 == hd, -jnp.sum(jnp.exp(sink_ref[0, hd] - lse[hd]) * delta[hd]), 0.0)
                ds2 = unwindow(prob * (window(_nt(do_pair, vz[kv][half])) - delta[hd]) * scale)
                dq_pair = dq_pair + _nn(ds2, kz[kv][half])
                dk2[kv][half] = dk2[kv][half] + _tn(ds2, q_pair)
                dv2[kv][half] = dv2[kv][half] + _tn(unwindow(prob), do_pair)
            dqs.append(dq_pair)
        dq_ref[...] = jnp.concatenate(dqs, axis=1)
        _acc_add(dsink_ref, first, dsink)
        for ref, acc2 in ((dk_ref, dk2), (dv_ref, dv2)):
            tot = jnp.zeros((2 * BLK, 128), F32)
            for kv in range(2):
                for half in range(2):
                    part = jnp.where(_half_mask(128, half), acc2[kv][half], 0.0)
                    tot = tot + (part if half == kv else pltpu.roll(part, 64, 1))
            ref[pl.ds(prev, BLK), :] += tot[:BLK]
            ref[pl.ds(own, BLK), :] += tot[BLK:]

        @pl.when(i == nb - 1)
        def _():
            dk_ref[PAD_ROWS:BLK, :] += _swa_meta_fold(dkp[...])
            dv_ref[PAD_ROWS:BLK, :] += _swa_meta_fold(dvp[...])

    full = pl.BlockSpec((rows, 128), lambda i: (0, 0))
    return pl.pallas_call(
        body, name="swa_bwd", grid=(nb,),
        in_specs=[_row_spec(BLK, 512), _row_spec(BLK, 512), _row_spec(BLK, 512), VMEM_SPEC, VMEM_SPEC,
                  _row_spec(BLK, SWA_HEADS), SMEM_SPEC, VMEM_SPEC],
        out_specs=[_row_spec(BLK, 512), full, full, _acc_spec(128), _acc_spec(512)],
        out_shape=[jax.ShapeDtypeStruct((rows, 512), F32), jax.ShapeDtypeStruct((rows, 128), F32),
                   jax.ShapeDtypeStruct((rows, 128), F32), jax.ShapeDtypeStruct((8, 128), F32),
                   jax.ShapeDtypeStruct((8, 512), F32)],
        scratch_shapes=[pltpu.VMEM((BLK, 512), BF16), pltpu.VMEM((BLK, 512), BF16),
                        pltpu.VMEM((BLK, 512), F32), pltpu.VMEM((BLK, 512), F32)],
        compiler_params=_params(("arbitrary",)),
    )(dcat, o_all, sq, sk, sv, lse, sinks, wn)


def _mix_out(h, cat_g, cat_s, wout, gpost):
    rows = h.shape[0]
    tm = _row_tile(rows)

    def body(h_ref, cg_ref, cs_ref, w_ref, g_ref, ho_ref, m_ref):
        m = _nn(cg_ref[...], w_ref[0:512, :]) + _nn(cs_ref[...], w_ref[512:1024, :])
        m_ref[...] = m
        mn, _ = _rms(m)
        ho_ref[...] = h_ref[...] + mn * g_ref[...]

    row_f32 = _row_spec(tm, D_MODEL)
    return pl.pallas_call(
        body, name="mix_out", grid=(rows // tm,),
        in_specs=[row_f32, _row_spec(tm, 512), _row_spec(tm, 512), VMEM_SPEC, VMEM_SPEC],
        out_specs=[row_f32, row_f32],
        out_shape=[jax.ShapeDtypeStruct((rows, D_MODEL), F32), jax.ShapeDtypeStruct((rows, D_MODEL), F32)],
        compiler_params=_params(("arbitrary",)),
    )(h, cat_g, cat_s, wout, gpost)


def _mix_out_bwd(dh, m, wout, gpost):
    rows = dh.shape[0]
    tm = _row_tile(rows)

    def body(dh_ref, m_ref, w_ref, g_ref, dcg_ref, dcs_ref, dm_ref, dg_ref):
        first = pl.program_id(0) == 0
        dhv = dh_ref[...]
        mn, rm = _rms(m_ref[...])
        _acc_add(dg_ref, first, _colsum(dhv * mn))
        dm16 = _rms_bwd(mn, rm, g_ref[...], dhv).astype(BF16)
        dm_ref[...] = dm16
        dcat = _nt(dm16, w_ref[...])
        dcg_ref[...] = dcat[:, 0:512]
        dcs_ref[...] = dcat[:, 512:1024]

    row_f32 = _row_spec(tm, D_MODEL)
    return pl.pallas_call(
        body, name="mix_out_bwd", grid=(rows // tm,),
        in_specs=[row_f32, row_f32, VMEM_SPEC, VMEM_SPEC],
        out_specs=[_row_spec(tm, 512), _row_spec(tm, 512), row_f32, _acc_spec(D_MODEL)],
        out_shape=[jax.ShapeDtypeStruct((rows, 512), F32), jax.ShapeDtypeStruct((rows, 512), F32),
                   jax.ShapeDtypeStruct((rows, D_MODEL), BF16), jax.ShapeDtypeStruct((8, D_MODEL), F32)],
        compiler_params=_params(("arbitrary",)),
    )(dh, m, wout, gpost)


def _mix_in_bwd(dh_out, h, g, win_p, wa2_p, cos, sin, loga, ga, dgq, dgk, dgv, dgg, dsq, dsk, dsv, dloga):
    rows = h.shape[0]
    tm = _row_tile(rows)

    def body(dho_ref, h_ref, g_ref, win_ref, wa2_ref, cos_ref, sin_ref, loga_ref, ga_ref,
             dgq_ref, dgk_ref, dgv_ref, dgg_ref, dsq_ref, dsk_ref, dsv_ref, dla_ref,
             dh_ref, dproj_ref, dwa2_ref, dg_ref, dba_ref):
        first = pl.program_id(0) == 0
        dz = dla_ref[...] * (1.0 / GLA_TAU) * (1.0 - jnp.exp(GLA_TAU * loga_ref[...]))
        _acc_add(dba_ref, first, _colsum(dz))
        dga = _nt(dz, wa2_ref[...])
        pa = _tn(ga_ref[...], dz)
        c1, s1 = cos_ref[...], sin_ref[...]
        c4 = jnp.concatenate([c1, c1, c1, c1], axis=1)
        s4 = jnp.concatenate([s1, s1, s1, s1], axis=1)
        dq_r, dk_r = dsq_ref[...], dsk_ref[...]
        dsq = dq_r * c4 - _rot_half(dq_r * s4)
        dsk = dk_r * c1 - _rot_half(dk_r * s1)
        dproj16 = jnp.concatenate(
            [dgq_ref[...], dgk_ref[...], dgv_ref[...], dgg_ref[...], dsq, dsk, dsv_ref[...], dga], axis=1).astype(BF16)
        dproj_ref[...] = dproj16
        dn = _nn(dproj16, win_ref[...])

        @pl.when(first)
        def _():
            dwa2_ref[...] = pa

        @pl.when(jnp.logical_not(first))
        def _():
            dwa2_ref[...] += pa
        hn, rh = _rms(h_ref[...])
        _acc_add(dg_ref, first, _colsum(dn * hn))
        dh_ref[...] = dho_ref[...] + _rms_bwd(hn, rh, g_ref[...], dn)

    rs = lambda c: _row_spec(tm, c)
    return pl.pallas_call(
        body, name="mix_in_bwd", grid=(rows // tm,),
        in_specs=[rs(D_MODEL), rs(D_MODEL), VMEM_SPEC, VMEM_SPEC, VMEM_SPEC, rs(128), rs(128), rs(256), rs(128),
                  rs(256), rs(256), rs(512), rs(512), rs(512), rs(128), rs(128), rs(256)],
        out_specs=[rs(D_MODEL), rs(P_END), pl.BlockSpec((128, 256), lambda i: (0, 0)), _acc_spec(D_MODEL), _acc_spec(256)],
        out_shape=[jax.ShapeDtypeStruct((rows, D_MODEL), F32), jax.ShapeDtypeStruct((rows, P_END), BF16),
                   jax.ShapeDtypeStruct((128, 256), F32), jax.ShapeDtypeStruct((8, D_MODEL), F32),
                   jax.ShapeDtypeStruct((8, 256), F32)],
        compiler_params=_params(("arbitrary",)),
    )(dh_out, h, g, win_p, wa2_p, cos, sin, loga, ga, dgq, dgk, dgv, dgg, dsq, dsk, dsv, dloga)


def _rope_tables(rows):
    pos = (jnp.arange(rows, dtype=jnp.int32) - PAD_ROWS).astype(F32)
    inv_freq = 1.0 / (ROPE_THETA ** (jnp.arange(0, SWA_HD, 2, dtype=F32) / SWA_HD))
    ang = pos[:, None] * inv_freq[None, :]
    ang = jnp.concatenate([ang, ang, ang, ang], axis=-1)
    return jnp.cos(ang), jnp.sin(ang)


def _local_step(x, tgt, front, w, late_weights=None, on_grads=None):
    cos, sin = _rope_tables(x.shape[0] + BLK)
    g = {}

    def tell(group, names):
        for nm in names:
            g[nm] = grads_now[nm]
        return 0.0 if on_grads is None else on_grads(group, {nm: grads_now[nm] for nm in names})

    h0, h1, a1, b1, s1, f1 = _ffn_fwd(x, w["ffn1_pre"], w["wg1"], w["wu1"], w["wd1"], w["ffn1_post"], front=front)
    if late_weights is not None:
        w = {**w, **late_weights("win", f1)}
    gq, gk, gv, gg, sq, sk, sv, ga, loga, bc, n2 = _mix_in(h1, w["mix_pre"], w["win"], w["wa2"], w["b_a"], cos, sin)
    o_g, cat_g, sp = _gla_fwd(gq, gk, gv, gg, bc, w["gla_norm"])
    o_s, cat_s, lse = _swa_fwd(sq, sk, sv, w["sinks"], w["swa_norm"])
    if late_weights is not None:
        w = {**w, **late_weights("rest", lse)}
    h2, m = _mix_out(h1, cat_g, cat_s, w["wout"], w["mix_post"])
    h3, a2, b2, s2, f2, dy, loss = _ffn_fwd(h2, w["ffn2_pre"], w["wg2"], w["wu2"], w["wd2"], w["ffn2_post"], tgt)
    del h3
    dh2, da, db, df, n3, g["ffn2_pre"], g["ffn2_post"] = _ffn_bwd_act(
        dy, h2, a2, b2, f2, w["ffn2_pre"], w["ffn2_post"], w["wg2"], w["wu2"], w["wd2"], "ffn2_bwd_act")
    grads_now = dict(wd2=_wgrad(s2, df, "ffn2_wgrad_down"), wg2=_wgrad(da, n3, "ffn2_wgrad_gate"),
                     wu2=_wgrad(db, n3, "ffn2_wgrad_up"))
    tok = tell("ffn2", ("wd2", "wg2", "wu2"))
    dcg, dcs, dm, g["mix_post"] = _mix_out_bwd(dh2, m, w["wout"], w["mix_post"] + tok)
    dsq, dsk, dsv, g["sinks"], g["swa_norm"] = _swa_bwd(dcs, o_s, sq, sk, sv, lse, w["sinks"], w["swa_norm"])
    dgq, dgk, dgv, dgg, dloga, g["gla_norm"] = _gla_bwd(dcg, o_g, gq, gk, gv, gg, bc, sp, w["gla_norm"])
    dh1, dproj, g["wa2"], g["mix_pre"], g["b_a"] = _mix_in_bwd(
        dh2, h1, w["mix_pre"], w["win"], w["wa2"], cos, sin, loga, ga, dgq, dgk, dgv, dgg, dsq, dsk, dsv, dloga)
    grads_now = dict(wout=jnp.concatenate([_wgrad(cat_g, dm, "wout_wgrad_gla"), _wgrad(cat_s, dm, "wout_wgrad_swa")], axis=0),
                     win=_wgrad(dproj, n2, "win_wgrad"))
    tok = tell("mix", ("wout", "win"))
    dh0, da, db, df, n1, g["ffn1_pre"], g["ffn1_post"] = _ffn_bwd_act(
        dh1, h0, a1, b1, f1, w["ffn1_pre"] + tok, w["ffn1_post"], w["wg1"], w["wu1"], w["wd1"], "ffn1_bwd_act")
    grads_now = dict(wd1=_wgrad(s1, df, "ffn1_wgrad_down"))
    tell("ffn1_down", ("wd1",))
    grads_now = dict(wg1=_wgrad(da, n1, "ffn1_wgrad_gate"))
    tell("ffn1_gate", ("wg1",))
    grads_now = dict(wu1=_wgrad(db, n1, "ffn1_wgrad_up"))
    tell("ffn1_up", ("wu1",))
    return loss[0, 0], dh0, g


def _win_pad_rows(win_t):
    pad = jnp.zeros((P_END - P_GA - 16, win_t.shape[1]), win_t.dtype)
    return jnp.concatenate([win_t[0:1536], win_t[1552:2320], win_t[1536:1552], pad], axis=0)


def _win_unpad_rows(win_p):
    return jnp.concatenate([win_p[0:1536], win_p[P_GA:P_GA + 16], win_p[1536:P_GA]], axis=0)


def _place_on_mesh():
    return lax.axis_index("x"), lax.axis_index("y"), lax.axis_index("c")


def _dev_index(px, py, pc):
    return 4 * px + 2 * py + pc


def _other_devices(x, y, c):
    flip = lambda v, f: 1 - v if f else v
    return [(flip(x, fx), flip(y, fy), flip(c, fc)) for fx in (0, 1) for fy in (0, 1) for fc in (0, 1)][1:]


def _all_gather(shards):
    n = len(shards)

    def body(*refs):
        ins, outs = refs[:n], refs[n:2 * n]
        zeros_ref, send_sems, recv_sems, local_sems = refs[2 * n:]
        zeros_ref[...] = jnp.zeros_like(zeros_ref)
        x, y, c = _place_on_mesh()
        me, sibling = (x, y, c), (x, y, 1 - c)
        chips = [(1 - x, y), (x, 1 - y), (1 - x, 1 - y)]

        def rows(k, px, py, pc):
            r = ins[k].shape[0]
            return outs[k].at[pl.ds(pl.multiple_of(_dev_index(px, py, pc) * r, 8), r), :]

        def copy(k, slot, block, to, src=None):
            return pltpu.make_async_remote_copy(
                src_ref=rows(k, *block) if src is None else src, dst_ref=rows(k, *block),
                send_sem=send_sems.at[k, slot], recv_sem=recv_sems.at[k, slot], device_id=to, device_id_type=MESH)

        local = [pltpu.make_async_copy(ins[k], rows(k, *me), local_sems.at[k]) for k in range(n)]
        sends = []
        for k in range(n):
            local[k].start()
            sends.append(copy(k, 0, me, sibling, src=ins[k]))
            sends += [copy(k, 1 + j, me, (*chip, c), src=ins[k]) for j, chip in enumerate(chips)]
        for cp in sends:
            cp.start()
        for k in range(n):
            for j, chip in enumerate(chips):
                copy(k, 1 + j, (*chip, c), me).wait_recv()
                passed = copy(k, 4 + j, (*chip, c), sibling)
                passed.start()
                sends.append(passed)
        for k in range(n):
            copy(k, 0, sibling, me).wait_recv()
            for j, chip in enumerate(chips):
                copy(k, 4 + j, (*chip, 1 - c), me).wait_recv()
        for cp in sends:
            cp.wait_send()
        for cp in local:
            cp.wait()

    return pl.pallas_call(
        body, name="all_gather_weights",
        in_specs=[ANY_SPEC] * n, out_specs=[ANY_SPEC] * n + [VMEM_SPEC],
        out_shape=[jax.ShapeDtypeStruct((N_DEV * s.shape[0], s.shape[1]), s.dtype) for s in shards]
        + [jax.ShapeDtypeStruct((8, 128), F32)],
        scratch_shapes=[pltpu.SemaphoreType.DMA((n, 7)), pltpu.SemaphoreType.DMA((n, 7)), pltpu.SemaphoreType.DMA((n,))],
    )(*shards)


HBM_SPEC = pl.BlockSpec(memory_space=pltpu.HBM)
SEM_SPEC = pl.BlockSpec(memory_space=pltpu.SEMAPHORE)
DATAFLOW = pltpu.SideEffectType.DATAFLOW_SIDE_EFFECTING


GATHER, SCATTER, SCATTER_CHIPS = "gather", "scatter", "scatter among chips"


def _exchange_peers(kind):
    x, y, c = _place_on_mesh()
    if kind == SCATTER_CHIPS:
        peers = [(1 - x, y, c), (x, 1 - y, c), (1 - x, 1 - y, c)]
        return peers, [2 * p[0] + p[1] for p in peers], 2 * x + y, 4
    peers = _other_devices(x, y, c)
    return peers, [_dev_index(*p) for p in peers], _dev_index(x, y, c), N_DEV


def _exchange_copies(srcs, lands, send_sems, recv_sems, own_sems, kind, arriving):
    peers, theirs, me, blocks = _exchange_peers(kind)
    remote, local = [], []
    for k, (src, land) in enumerate(zip(srcs, lands)):
        r = land.shape[0] // blocks

        def block(ref, d):
            return ref.at[pl.ds(pl.multiple_of(d * r, 8), r), :]

        for f, (peer, him) in enumerate(zip(peers, theirs)):
            mine, his = (him, me) if arriving else (me, him)
            sem = len(peers) * k + f
            remote.append(pltpu.make_async_remote_copy(
                src_ref=src if kind == GATHER else block(src, his), dst_ref=block(land, mine),
                send_sem=send_sems.at[sem], recv_sem=recv_sems.at[sem], device_id=peer, device_id_type=MESH))
        local.append(pltpu.make_async_copy(src if kind == GATHER else block(src, me), block(land, me), own_sems.at[k]))
    return remote, local


def _exchange_start(srcs, kind, name):
    n = len(srcs)
    lands = [lax.empty((N_DEV * s.shape[0], s.shape[1]) if kind == GATHER else s.shape, s.dtype) for s in srcs]
    sems = (3 if kind == SCATTER_CHIPS else 7) * n

    def body(*refs):
        remote, local = _exchange_copies(refs[:n], refs[n:2 * n], *refs[2 * n:2 * n + 3], kind, False)
        for cp in remote + local:
            cp.start()
        refs[-1][...] = jnp.zeros_like(refs[-1])

    both = list(srcs) + list(lands)
    outs = pl.pallas_call(
        body, name=name,
        out_shape=(pltpu.SemaphoreType.DMA((sems,)), pltpu.SemaphoreType.DMA((sems,)), pltpu.SemaphoreType.DMA((n,)),
                   *[pltpu.HBM(a.shape, a.dtype) for a in both], jax.ShapeDtypeStruct((8, 128), F32)),
        in_specs=[HBM_SPEC] * (2 * n), out_specs=(SEM_SPEC, SEM_SPEC, SEM_SPEC, *[HBM_SPEC] * (2 * n), VMEM_SPEC),
        input_output_aliases={i: 3 + i for i in range(2 * n)},
        compiler_params=pltpu.CompilerParams(has_side_effects=DATAFLOW),
    )(*[pltpu.with_memory_space_constraint(a, pltpu.HBM) for a in both])
    return outs[0:3], outs[3:3 + n], outs[3 + n:3 + 2 * n], outs[-1]


def _exchange_wait(started, kind, after, name):
    sems, srcs, lands, _ = started
    n = len(srcs)

    def body(*refs):
        args = (refs[:n], refs[n:2 * n], *refs[2 * n:2 * n + 3], kind)
        going, local = _exchange_copies(*args, False)
        for cp in going:
            cp.wait_send()
        for cp in local:
            cp.wait()
        for cp in _exchange_copies(*args, True)[0]:
            cp.wait_recv()

    both = list(srcs) + list(lands)
    outs = pl.pallas_call(
        body, name=name, out_shape=[pltpu.HBM(a.shape, a.dtype) for a in both],
        in_specs=[HBM_SPEC] * (2 * n) + [SEM_SPEC, SEM_SPEC, SEM_SPEC, ANY_SPEC], out_specs=[HBM_SPEC] * (2 * n),
        input_output_aliases={i: i for i in range(2 * n)},
        compiler_params=pltpu.CompilerParams(has_side_effects=DATAFLOW),
    )(*both, *sems, after)
    return outs[n:]


def _sibling_reduce(part, name):
    r, cols = part.shape[0] // N_DEV, part.shape[1]

    def swap(p_ref, got_ref, send_sems, recv_sems):
        x, y, c = _place_on_mesh()
        copies = [pltpu.make_async_remote_copy(
            src_ref=p_ref.at[pl.ds(pl.multiple_of((2 * j + 1 - c) * r, 8), r), :], dst_ref=got_ref.at[pl.ds(j * r, r), :],
            send_sem=send_sems.at[j], recv_sem=recv_sems.at[j], device_id=(x, y, 1 - c), device_id_type=MESH)
            for j in range(4)]
        for cp in copies:
            cp.start()
        for cp in copies:
            cp.wait()

    got = pl.pallas_call(
        swap, name=name + "_swap", in_specs=[ANY_SPEC], out_specs=ANY_SPEC,
        out_shape=jax.ShapeDtypeStruct((4 * r, cols), part.dtype),
        scratch_shapes=[pltpu.SemaphoreType.DMA((4,)), pltpu.SemaphoreType.DMA((4,))],
    )(part)

    def add(c_ref, mine_ref, got_ref, o_ref):
        del c_ref
        o_ref[...] = (mine_ref[...].astype(F32) + got_ref[...].astype(F32)).astype(o_ref.dtype)

    core = lax.axis_index("c").astype(jnp.int32).reshape(1)
    return pl.pallas_call(
        add, name=name + "_add",
        grid_spec=pltpu.PrefetchScalarGridSpec(
            num_scalar_prefetch=1, grid=(4,),
            in_specs=[pl.BlockSpec((r, cols), lambda j, c_ref: (2 * j + c_ref[0], 0)),
                      pl.BlockSpec((r, cols), lambda j, c_ref: (j, 0))],
            out_specs=pl.BlockSpec((r, cols), lambda j, c_ref: (j, 0))),
        out_shape=jax.ShapeDtypeStruct((4 * r, cols), part.dtype),
        compiler_params=_params(("arbitrary",)),
    )(core, part, got)


def _sum_partials(parts, name, blocks=N_DEV):
    n = len(parts)

    def body(*refs):
        ins, outs = refs[:n], refs[n:]
        first = pl.program_id(0) == 0
        for i_ref, o_ref in zip(ins, outs):
            v = i_ref[...].astype(F32)

            @pl.when(first)
            def _():
                o_ref[...] = v

            @pl.when(jnp.logical_not(first))
            def _():
                o_ref[...] += v

    shapes = [(p.shape[0] // blocks, p.shape[1]) for p in parts]
    return pl.pallas_call(
        body, name=name, grid=(blocks,),
        in_specs=[pl.BlockSpec(s, lambda j: (j, 0)) for s in shapes],
        out_specs=[pl.BlockSpec(s, lambda j: (0, 0)) for s in shapes],
        out_shape=[jax.ShapeDtypeStruct(s, F32) for s in shapes],
        compiler_params=_params(("arbitrary",)),
    )(*parts)


def _all_reduce_small(slab):
    rows, cols = slab.shape

    def body(x_ref, o_ref, gathered, send_sems, recv_sems):
        x, y, c = _place_on_mesh()
        me = _dev_index(x, y, c)
        peers = _other_devices(x, y, c)

        def copy(f, peer):
            return pltpu.make_async_remote_copy(
                src_ref=x_ref, dst_ref=gathered.at[me], send_sem=send_sems.at[f], recv_sem=recv_sems.at[f],
                device_id=peer, device_id_type=MESH)

        def arrival(f, peer):
            return pltpu.make_async_remote_copy(
                src_ref=x_ref, dst_ref=gathered.at[_dev_index(*peer)], send_sem=send_sems.at[f], recv_sem=recv_sems.at[f],
                device_id=peer, device_id_type=MESH)

        sends = [copy(f, peer) for f, peer in enumerate(peers)]
        for cp in sends:
            cp.start()
        gathered[me] = x_ref[...]
        for f, peer in enumerate(peers):
            arrival(f, peer).wait_recv()
        for cp in sends:
            cp.wait_send()
        total = gathered[0]
        for d in range(1, N_DEV):
            total = total + gathered[d]
        o_ref[...] = total

    return pl.pallas_call(
        body, name="all_reduce_small",
        in_specs=[VMEM_SPEC], out_specs=VMEM_SPEC, out_shape=jax.ShapeDtypeStruct((rows, cols), F32),
        scratch_shapes=[pltpu.VMEM((N_DEV, rows, cols), F32), pltpu.SemaphoreType.DMA((7,)), pltpu.SemaphoreType.DMA((7,))],
    )(slab)


def _adamw_update(w, g, m, v):
    m = ADAM_B1 * m + (1.0 - ADAM_B1) * g
    v = ADAM_B2 * v + (1.0 - ADAM_B2) * (g * g)
    m_hat = m * (1.0 / (1.0 - ADAM_B1 ** ADAM_STEP))
    v_hat = v * (1.0 / (1.0 - ADAM_B2 ** ADAM_STEP))
    return -ADAM_LR * (m_hat / (jnp.sqrt(v_hat) + ADAM_EPS) + ADAM_WD * w), m, v


def _sum_adamw(parts, w, m, v, blocks, name):
    shape = w.shape

    def body(p_ref, w_ref, m_ref, v_ref, g_ref, d_ref, mo_ref, vo_ref):
        j = pl.program_id(0)
        part = p_ref[...].astype(F32)

        @pl.when(j == 0)
        def _():
            g_ref[...] = part

        @pl.when(j > 0)
        def _():
            g_ref[...] += part

        @pl.when(j == blocks - 1)
        def _():
            d_ref[...], mo_ref[...], vo_ref[...] = _adamw_update(w_ref[...], g_ref[...], m_ref[...], v_ref[...])

    held = pl.BlockSpec(shape, lambda j: (0, 0))
    return pl.pallas_call(
        body, name=name, grid=(blocks,),
        in_specs=[pl.BlockSpec(shape, lambda j: (j, 0)), held, held, held],
        out_specs=[held] * 4, out_shape=[jax.ShapeDtypeStruct(shape, F32)] * 4,
        compiler_params=_params(("arbitrary",)),
    )(parts, w, m, v)


def _adamw(ws, gs, ms, vs, name):
    n = len(ws)

    def body(*refs):
        w_r, g_r, m_r, v_r = refs[:n], refs[n:2 * n], refs[2 * n:3 * n], refs[3 * n:4 * n]
        d_o, m_o, v_o = refs[4 * n:5 * n], refs[5 * n:6 * n], refs[6 * n:7 * n]
        for k in range(n):
            d_o[k][...], m_o[k][...], v_o[k][...] = _adamw_update(w_r[k][...], g_r[k][...], m_r[k][...], v_r[k][...])

    shapes = [jax.ShapeDtypeStruct(w.shape, F32) for w in ws]
    outs = pl.pallas_call(
        body, name=name, in_specs=[VMEM_SPEC] * (4 * n), out_specs=[VMEM_SPEC] * (3 * n), out_shape=shapes * 3,
        compiler_params=pltpu.CompilerParams(vmem_limit_bytes=56 << 20),
    )(*ws, *gs, *ms, *vs)
    return outs[:n], outs[n:2 * n], outs[2 * n:]


WEIGHT_NAMES = ("meta_tokens", "ffn1_pre_norm", "ffn1_w_gate", "ffn1_w_up", "ffn1_w_down", "ffn1_post_norm", "mix_pre_norm",
                "w_in", "gla_w_a2", "gla_b_a", "gla_out_norm", "swa_sinks", "swa_out_norm", "w_out", "mix_post_norm",
                "ffn2_pre_norm", "ffn2_w_gate", "ffn2_w_up", "ffn2_w_down", "ffn2_post_norm")
WIN_SHARD = D_IN // N_DEV
WIN_SHARD_PAD = 304
SLAB_VECTORS = ("ffn1_pre", "ffn1_post", "mix_pre", "mix_post", "ffn2_pre", "ffn2_post")
SLAB_ROWS = 32


def kernel(x, meta_tokens, ffn1_pre_norm, ffn1_w_gate, ffn1_w_up, ffn1_w_down, ffn1_post_norm, mix_pre_norm, w_in, gla_w_a2, gla_b_a, gla_out_norm, swa_sinks, swa_out_norm, w_out, mix_post_norm, ffn2_pre_norm, ffn2_w_gate, ffn2_w_up, ffn2_w_down, ffn2_post_norm, loss_target, m_meta_tokens, m_ffn1_pre_norm, m_ffn1_w_gate, m_ffn1_w_up, m_ffn1_w_down, m_ffn1_post_norm, m_mix_pre_norm, m_w_in, m_gla_w_a2, m_gla_b_a, m_gla_out_norm, m_swa_sinks, m_swa_out_norm, m_w_out, m_mix_post_norm, m_ffn2_pre_norm, m_ffn2_w_gate, m_ffn2_w_up, m_ffn2_w_down, m_ffn2_post_norm, v_meta_tokens, v_ffn1_pre_norm, v_ffn1_w_gate, v_ffn1_w_up, v_ffn1_w_down, v_ffn1_post_norm, v_mix_pre_norm, v_w_in, v_gla_w_a2, v_gla_b_a, v_gla_out_norm, v_swa_sinks, v_swa_out_norm, v_w_out, v_mix_post_norm, v_ffn2_pre_norm, v_ffn2_w_gate, v_ffn2_w_up, v_ffn2_w_down, v_ffn2_post_norm):
    given = dict(locals())
    W = {n: given[n] for n in WEIGHT_NAMES}
    M = {n: given["m_" + n] for n in WEIGHT_NAMES}
    V = {n: given["v_" + n] for n in WEIGHT_NAMES}
    dev = _dev_index(*_place_on_mesh())

    def t16(w):
        return w[0].T.astype(BF16)

    small = jnp.concatenate([W["meta_tokens"], jnp.pad(W["gla_w_a2"][0], ((0, 0), (0, 96)))], axis=0)
    wg1, wu1, wd1, small_g, gathered_zeros = _all_gather(
        [t16(W["ffn1_w_gate"]), t16(W["ffn1_w_up"]), W["ffn1_w_down"][0].astype(BF16), small])
    def after_zero(shard, zeros):
        return shard + zeros[0:1, 0:1].astype(shard.dtype)
    win_shard = jnp.pad(t16(W["w_in"]), ((0, WIN_SHARD_PAD - WIN_SHARD), (0, 0)))
    win_shard = after_zero(win_shard, gathered_zeros)
    mid = _exchange_start([win_shard], GATHER, "gather_w_in_start")
    late_shards = [after_zero(W["w_out"][0].astype(BF16), mid[3]), t16(W["ffn2_w_gate"]), t16(W["ffn2_w_up"]),
                   W["ffn2_w_down"][0].astype(BF16)]
    late = _exchange_start(late_shards, GATHER, "gather_late_weights_start")

    def late_weights(what, after):
        if what == "win":
            win_g, = _exchange_wait(mid, GATHER, after, "gather_w_in_wait")
            win_t = win_g.reshape(N_DEV, WIN_SHARD_PAD, D_MODEL)[:, :WIN_SHARD].reshape(D_IN, D_MODEL)
            return dict(win=_win_pad_rows(win_t))
        wout, wg2, wu2, wd2 = _exchange_wait(late, GATHER, after, "gather_late_weights_wait")
        return dict(wout=wout, wg2=wg2, wu2=wu2, wd2=wd2)

    small_g = small_g.reshape(N_DEV, 32, 128)
    meta_full = small_g[:, :N_META].transpose(1, 0, 2).reshape(N_META, D_MODEL)
    wa2_full = small_g[:, N_META:, :32].transpose(1, 0, 2).reshape(16, 256)
    w = dict(
        ffn1_pre=W["ffn1_pre_norm"] + late[3][0, 0], ffn1_post=W["ffn1_post_norm"], mix_pre=W["mix_pre_norm"],
        mix_post=W["mix_post_norm"], ffn2_pre=W["ffn2_pre_norm"], ffn2_post=W["ffn2_post_norm"], b_a=W["gla_b_a"],
        gla_norm=W["gla_out_norm"], sinks=W["swa_sinks"], swa_norm=W["swa_out_norm"], wg1=wg1, wu1=wu1, wd1=wd1,
        wa2=jnp.pad(wa2_full, ((0, 112), (0, 0))))

    in_flight = []

    def on_grads(group, grads):
        parts = []
        for nm, p in grads.items():
            if nm == "win":
                p = _win_unpad_rows(p).reshape(N_DEV, WIN_SHARD, D_MODEL)
                p = jnp.pad(p, ((0, 0), (0, WIN_SHARD_PAD - WIN_SHARD), (0, 0))).reshape(N_DEV * WIN_SHARD_PAD, D_MODEL)
            parts.append(p)
        kind = SCATTER_CHIPS if group.startswith("ffn1") else SCATTER
        if kind == SCATTER_CHIPS:
            parts = [_sibling_reduce(p, "pair_" + group) for p in parts]
        started = _exchange_start(parts, kind, "scatter_" + group + "_start")
        in_flight.append((group, list(grads), started, kind))
        return started[3][0, 0]

    front = jnp.concatenate([jnp.zeros((PAD_ROWS, D_MODEL), F32), meta_full], axis=0)
    loss, dh0, g = _local_step(x[0], loss_target[0], front, w, late_weights, on_grads)
    grad_x = dh0[BLK:][None]

    packed = jnp.concatenate([g["b_a"][0:1], g["gla_norm"][0:1], g["sinks"][0:1], g["swa_norm"][0:1]], axis=1)
    slab = jnp.concatenate([g[k][0:1] for k in SLAB_VECTORS] + [packed, jnp.full((1, D_MODEL), loss, F32),
                           g["wa2"][:16].reshape(4, D_MODEL), jnp.zeros((4, D_MODEL), F32), dh0[PAD_ROWS:BLK]], axis=0)
    tot = _all_reduce_small(slab)
    loss = tot[7, 0]
    small_grads = dict(
        ffn1_pre_norm=tot[0:1], ffn1_post_norm=tot[1:2], mix_pre_norm=tot[2:3], mix_post_norm=tot[3:4],
        ffn2_pre_norm=tot[4:5], ffn2_post_norm=tot[5:6], gla_b_a=tot[6:7, 0:256], gla_out_norm=tot[6:7, 256:384],
        swa_sinks=tot[6:7, 384:392], swa_out_norm=tot[6:7, 512:1024],
        gla_w_a2=lax.dynamic_slice_in_dim(tot[8:12].reshape(16, 256), dev * 32, 32, axis=1)[None],
        meta_tokens=lax.dynamic_slice_in_dim(tot[16:32], dev * 128, 128, axis=1))

    big = dict(wg1=("ffn1_w_gate", True), wu1=("ffn1_w_up", True), wd1=("ffn1_w_down", False), win=("w_in", True),
               wout=("w_out", False), wg2=("ffn2_w_gate", True), wu2=("ffn2_w_up", True), wd2=("ffn2_w_down", False))
    grads = dict(small_grads)
    delta, new_m, new_v = {}, {}, {}
    names = [n for n in WEIGHT_NAMES if n not in [full for full, _ in big.values()]]
    two_d = lambda a: a.reshape(-1, a.shape[-1])
    d_, m_, v_ = _adamw([two_d(W[n]) for n in names], [two_d(grads[n]) for n in names],
                        [two_d(M[n]) for n in names], [two_d(V[n]) for n in names], "adamw_small")
    for k, n in enumerate(names):
        delta[n], new_m[n], new_v[n] = d_[k].reshape(W[n].shape), m_[k].reshape(W[n].shape), v_[k].reshape(W[n].shape)

    before_wait = d_[0] + in_flight[-1][2][3][0, 0]
    for group, shorts, started, kind in in_flight:
        lands = _exchange_wait(started, kind, before_wait, "scatter_" + group + "_wait")
        blocks = 4 if kind == SCATTER_CHIPS else N_DEV
        for short, land in zip(shorts, lands):
            n, transposed = big[short]
            to_slab = (lambda a: a[0].T) if transposed else (lambda a: a[0])
            from_slab = (lambda a: a.T[None]) if transposed else (lambda a: a[None])
            if short == "win":
                g_slab = _sum_partials([land], "sum_" + n, blocks)[0][:WIN_SHARD]
                d_, m_, v_ = _adamw([to_slab(W[n])], [g_slab], [to_slab(M[n])], [to_slab(V[n])], "adamw_" + n)
                d_, m_, v_ = d_[0], m_[0], v_[0]
            else:
                g_slab, d_, m_, v_ = _sum_adamw(land, to_slab(W[n]), to_slab(M[n]), to_slab(V[n]), blocks, "adamw_" + n)
            grads[n], delta[n], new_m[n], new_v[n] = from_slab(g_slab), from_slab(d_), from_slab(m_), from_slab(v_)
            before_wait = d_
    return (loss, grad_x, *[grads[n] for n in WEIGHT_NAMES], *[delta[n] for n in WEIGHT_NAMES],
            *[new_m[n] for n in WEIGHT_NAMES], *[new_v[n] for n in WEIGHT_NAMES])
```

```python
import math

import jax
import jax.numpy as jnp
from jax import lax
from jax.experimental import pallas as pl
from jax.experimental.pallas import tpu as pltpu

F32, BF16 = jnp.float32, jnp.bfloat16

D_MODEL = 1024
D_FF = 2816
N_META = 16
BLK = 128
PAD_ROWS = BLK - N_META
GLA_DK = 64
SWA_HD = 64
SWA_HEADS = 8
GLA_TAU = 16.0
NORM_EPS = 1e-6
NEG_INF = -1e30
ROPE_THETA = 10000.0
P_GQ, P_GK, P_GV, P_GG, P_SQ, P_SK, P_SV, P_GA, P_END = 0, 256, 512, 1024, 1536, 2048, 2176, 2304, 2432
D_IN = 2320
IN_SPLITS = (256, 256, 512, 512, 16, 512, 128, 128)
FF_TILE = 2816
WGRAD_TILE_MAX = 2432
N_DEV = 8
MESH = pl.DeviceIdType.MESH

ADAM_LR, ADAM_B1, ADAM_B2, ADAM_EPS, ADAM_WD, ADAM_STEP = 0.001, 0.9, 0.999, 1e-08, 0.01, 10

V7X_VMEM_BYTES = 64 << 20
VMEM_SPEC = pl.BlockSpec(memory_space=pltpu.VMEM)
SMEM_SPEC = pl.BlockSpec(memory_space=pltpu.SMEM)
ANY_SPEC = pl.BlockSpec(memory_space=pl.ANY)


def _params(semantics, vmem_mb=56):
    return pltpu.CompilerParams(dimension_semantics=semantics, vmem_limit_bytes=vmem_mb << 20)


def _row_tile(rows):
    return 416 if rows % 416 == 0 else BLK


def _nn(a, b):
    return lax.dot_general(a, b, (((1,), (0,)), ((), ())), preferred_element_type=F32)


def _nt(a, b):
    return lax.dot_general(a, b, (((1,), (1,)), ((), ())), preferred_element_type=F32)


def _tn(a, b):
    return lax.dot_general(a, b, (((0,), (0,)), ((), ())), preferred_element_type=F32)


def _rms(x):
    r = lax.rsqrt(jnp.mean(x * x, axis=-1, keepdims=True) + NORM_EPS)
    return x * r, r


def _rms_bwd(xn, r, w, dy):
    g = dy * w
    return r * (g - xn * jnp.mean(g * xn, axis=-1, keepdims=True))


def _sigmoid(x):
    return 1.0 / (1.0 + jnp.exp(-x))


def _colsum(x):
    return jnp.sum(x, axis=0, keepdims=True)


def _split_bf16(x):
    hi = x.astype(BF16)
    lo = (x - hi.astype(F32)).astype(BF16)
    return hi, lo


def _tri(lower):
    r = lax.broadcasted_iota(jnp.int32, (BLK, BLK), 0)
    c = lax.broadcasted_iota(jnp.int32, (BLK, BLK), 1)
    return (r >= c) if lower else (c >= r)


def _half_mask(width, half):
    lane = lax.broadcasted_iota(jnp.int32, (1, width), 1)
    return ((lane % 128) < 64) if half == 0 else ((lane % 128) >= 64)


def _rot_half(x):
    w = x.shape[-1]
    lane = lax.broadcasted_iota(jnp.int32, (1, w), 1)
    return jnp.where((lane % SWA_HD) < SWA_HD // 2, -pltpu.roll(x, w - SWA_HD // 2, 1), pltpu.roll(x, SWA_HD // 2, 1))


def _row_spec(tm, cols):
    return pl.BlockSpec((tm, cols), lambda i: (i, 0))


def _acc_spec(cols):
    return pl.BlockSpec((8, cols), lambda i: (0, 0))


def _acc_add(ref, first, value):
    @pl.when(first)
    def _():
        ref[...] = jnp.zeros_like(ref)
    ref[0:1, :] += value


def _behind_front(ref, i, tm, front):
    blk = ref[...]
    return jnp.where(i == 0, jnp.concatenate([front, blk[0:tm - BLK]], axis=0), blk)


def _ffn_fwd(h, gpre, wg_t, wu_t, wd, gpost, tgt=None, front=None):
    with_loss, with_front = tgt is not None, front is not None
    rows = h.shape[0] + (BLK if with_front else 0)
    tm = _row_tile(rows)
    nf = D_FF // FF_TILE

    def body(*refs):
        refs = list(refs)
        h_ref, gpre_ref, wg_ref, wu_ref, wd_ref, gpost_ref = refs[:6]
        del refs[:6]
        front_ref = refs.pop(0) if with_front else None
        t_ref = refs.pop(0) if with_loss else None
        h0_ref = refs.pop(0) if with_front else None
        ho_ref, a_ref, b_ref, s_ref, f_ref = refs[:5]
        dy_ref, loss_ref = refs[5:7] if with_loss else (None, None)
        acc = refs[-1]
        i = pl.program_id(0)
        if with_front:
            h_in = _behind_front(h_ref, i, tm, front_ref[...])
            h0_ref[...] = h_in
        else:
            h_in = h_ref[...]
        hn, _ = _rms(h_in)
        n16 = (hn * gpre_ref[...]).astype(BF16)
        for j in range(nf):
            cols = slice(j * FF_TILE, (j + 1) * FF_TILE)
            a = _nt(n16, wg_ref[cols, :])
            b = _nt(n16, wu_ref[cols, :])
            a_ref[:, cols] = a.astype(BF16)
            b_ref[:, cols] = b.astype(BF16)
            s16 = (a * _sigmoid(a) * b).astype(BF16)
            s_ref[:, cols] = s16
            part = _nn(s16, wd_ref[cols, :])
            if j == 0:
                acc[...] = part
            else:
                acc[...] += part
        f = acc[...]
        f_ref[...] = f
        fn, _ = _rms(f)
        y = h_in + 0.5 * (fn * gpost_ref[...])
        ho_ref[...] = y
        if with_loss:
            row = i * tm + lax.broadcasted_iota(jnp.int32, (tm, 1), 0)
            err = jnp.where(row >= BLK, y - _behind_front(t_ref, i, tm, jnp.zeros((BLK, D_MODEL), F32)), 0.0)
            dy_ref[...] = err * (1.0 / D_MODEL)
            part = 0.5 * jnp.sum(jnp.sum(err * err, axis=-1, keepdims=True) * (1.0 / D_MODEL), axis=0, keepdims=True)

            @pl.when(i == 0)
            def _():
                loss_ref[...] = jnp.zeros_like(loss_ref)
            loss_ref[...] += part

    row_f32 = _row_spec(tm, D_MODEL)
    behind = pl.BlockSpec((pl.Element(tm), pl.Element(D_MODEL)),
                          lambda i: (pl.multiple_of(jnp.maximum(i * tm - BLK, 0), math.gcd(tm, BLK)), 0))
    in_specs = [behind if with_front else row_f32, VMEM_SPEC, VMEM_SPEC, VMEM_SPEC, VMEM_SPEC, VMEM_SPEC]
    out_specs = [row_f32, _row_spec(tm, D_FF), _row_spec(tm, D_FF), _row_spec(tm, D_FF), row_f32]
    out_shape = [jax.ShapeDtypeStruct((rows, D_MODEL), F32), jax.ShapeDtypeStruct((rows, D_FF), BF16),
                 jax.ShapeDtypeStruct((rows, D_FF), BF16), jax.ShapeDtypeStruct((rows, D_FF), BF16),
                 jax.ShapeDtypeStruct((rows, D_MODEL), F32)]
    args = [h, gpre, wg_t, wu_t, wd, gpost]
    if with_front:
        in_specs.append(VMEM_SPEC)
        args.append(front)
        out_specs.insert(0, row_f32)
        out_shape.insert(0, jax.ShapeDtypeStruct((rows, D_MODEL), F32))
    if with_loss:
        in_specs.append(behind)
        args.append(tgt)
        out_specs += [row_f32, pl.BlockSpec((8, 128), lambda i: (0, 0))]
        out_shape += [jax.ShapeDtypeStruct((rows, D_MODEL), F32), jax.ShapeDtypeStruct((8, 128), F32)]
    return pl.pallas_call(
        body, name="ffn_fwd_loss" if with_loss else "ffn_fwd", grid=(rows // tm,),
        in_specs=in_specs, out_specs=out_specs, out_shape=out_shape,
        scratch_shapes=[pltpu.VMEM((tm, D_MODEL), F32)],
        compiler_params=_params(("arbitrary",)),
    )(*args)


def _ffn_bwd_act(dh_out, h, a, b, f, gpre, gpost, wg_t, wu_t, wd, name):
    rows = h.shape[0]
    tm = _row_tile(rows)
    nf = D_FF // FF_TILE

    def body(dho_ref, h_ref, a_ref, b_ref, f_ref, gpre_ref, gpost_ref, wg_ref, wu_ref, wd_ref,
             dh_ref, da_ref, db_ref, df_ref, n_ref, dgpre_ref, dgpost_ref, acc):
        first = pl.program_id(0) == 0
        dho = dho_ref[...]
        drr = 0.5 * dho
        fn, rf = _rms(f_ref[...])
        _acc_add(dgpost_ref, first, _colsum(drr * fn))
        df16 = _rms_bwd(fn, rf, gpost_ref[...], drr).astype(BF16)
        df_ref[...] = df16
        hn, rh = _rms(h_ref[...])
        n_ref[...] = (hn * gpre_ref[...]).astype(BF16)
        for j in range(nf):
            cols = slice(j * FF_TILE, (j + 1) * FF_TILE)
            ds = _nt(df16, wd_ref[cols, :])
            av = a_ref[:, cols].astype(F32)
            bv = b_ref[:, cols].astype(F32)
            sg = _sigmoid(av)
            db16 = (ds * (av * sg)).astype(BF16)
            da16 = (ds * bv * (sg * (1.0 + av * (1.0 - sg)))).astype(BF16)
            da_ref[:, cols] = da16
            db_ref[:, cols] = db16
            part = _nn(da16, wg_ref[cols, :]) + _nn(db16, wu_ref[cols, :])
            if j == 0:
                acc[...] = part
            else:
                acc[...] += part
        dn = acc[...]
        _acc_add(dgpre_ref, first, _colsum(dn * hn))
        dh_ref[...] = dho + _rms_bwd(hn, rh, gpre_ref[...], dn)

    row_f32 = _row_spec(tm, D_MODEL)
    row_ff = _row_spec(tm, D_FF)
    return pl.pallas_call(
        body, name=name, grid=(rows // tm,),
        in_specs=[row_f32, row_f32, row_ff, row_ff, row_f32, VMEM_SPEC, VMEM_SPEC, VMEM_SPEC, VMEM_SPEC, VMEM_SPEC],
        out_specs=[row_f32, row_ff, row_ff, row_f32, row_f32, _acc_spec(D_MODEL), _acc_spec(D_MODEL)],
        out_shape=[jax.ShapeDtypeStruct((rows, D_MODEL), F32), jax.ShapeDtypeStruct((rows, D_FF), BF16),
                   jax.ShapeDtypeStruct((rows, D_FF), BF16), jax.ShapeDtypeStruct((rows, D_MODEL), BF16),
                   jax.ShapeDtypeStruct((rows, D_MODEL), BF16), jax.ShapeDtypeStruct((8, D_MODEL), F32),
                   jax.ShapeDtypeStruct((8, D_MODEL), F32)],
        scratch_shapes=[pltpu.VMEM((tm, D_MODEL), F32)],
        compiler_params=_params(("arbitrary",), vmem_mb=62),
    )(dh_out, h, a, b, f, gpre, gpost, wg_t, wu_t, wd)


def _wgrad(lhs, rhs, name):
    rows, width = lhs.shape
    tm = rows if rows % 1664 == 0 else BLK
    tf = 256 if width % 256 == 0 else 128
    nr = rows // tm

    def body(l_ref, r_ref, o_ref, acc):
        i = pl.program_id(1)
        part = _tn(l_ref[...], r_ref[...])

        @pl.when(i == 0)
        def _():
            acc[...] = part

        @pl.when(i > 0)
        def _():
            acc[...] += part

        @pl.when(i == nr - 1)
        def _():
            o_ref[...] = acc[...].astype(BF16)

    l_spec = pl.BlockSpec((tm, tf), lambda j, i: (i, j))
    r_spec = pl.BlockSpec((tm, D_MODEL), lambda j, i: (i, 0))
    return pl.pallas_call(
        body, name=name, grid=(width // tf, nr),
        in_specs=[l_spec, r_spec],
        out_specs=pl.BlockSpec((tf, D_MODEL), lambda j, i: (j, 0)),
        out_shape=jax.ShapeDtypeStruct((width, D_MODEL), BF16),
        scratch_shapes=[pltpu.VMEM((tf, D_MODEL), F32)],
        compiler_params=_params(("arbitrary", "arbitrary")),
    )(lhs, rhs)


def _chunk_cumsum(x, lower):
    tri = jnp.where(_tri(lower), 1.0, 0.0).astype(BF16)
    hi, lo = _split_bf16(x)
    return _nn(tri, hi) + _nn(tri, lo)


def _mix_in(h, g, win_p, wa2_p, b_a, cos, sin):
    rows = h.shape[0]
    tm = 640 if rows % 640 == 0 else BLK

    def body(h_ref, g_ref, win_ref, wa2_ref, ba_ref, cos_ref, sin_ref,
             gq_ref, gk_ref, gv_ref, gg_ref, sq_ref, sk_ref, sv_ref, ga_ref, loga_ref, bc_ref, n_ref):
        hn, _ = _rms(h_ref[...])
        n16 = (hn * g_ref[...]).astype(BF16)
        n_ref[...] = n16
        proj = _nt(n16, win_ref[...])
        gq_ref[...] = proj[:, P_GQ:P_GK]
        gk_ref[...] = proj[:, P_GK:P_GV]
        gv_ref[...] = proj[:, P_GV:P_GG].astype(BF16)
        gg_ref[...] = proj[:, P_GG:P_SQ]
        c1, s1 = cos_ref[...], sin_ref[...]
        c4 = jnp.concatenate([c1, c1, c1, c1], axis=1)
        s4 = jnp.concatenate([s1, s1, s1, s1], axis=1)
        sq = proj[:, P_SQ:P_SK]
        sk = proj[:, P_SK:P_SV]
        sq_ref[...] = (sq * c4 + _rot_half(sq) * s4).astype(BF16)
        sk_ref[...] = (sk * c1 + _rot_half(sk) * s1).astype(BF16)
        sv_ref[...] = proj[:, P_SV:P_GA].astype(BF16)
        ga = proj[:, P_GA:P_END]
        ga_ref[...] = ga
        z = _nn(ga, wa2_ref[...]) + ba_ref[...]
        loga = (jnp.minimum(z, 0.0) - jnp.log(1.0 + jnp.exp(-jnp.abs(z)))) * (1.0 / GLA_TAU)
        loga_ref[...] = loga
        for c in range(tm // BLK):
            rs = slice(c * BLK, (c + 1) * BLK)
            bc_ref[rs, :] = _chunk_cumsum(loga[rs, :], True)

    f32 = lambda c: jax.ShapeDtypeStruct((rows, c), F32)
    b16 = lambda c: jax.ShapeDtypeStruct((rows, c), BF16)
    rs = lambda c: _row_spec(tm, c)
    return pl.pallas_call(
        body, name="mix_in", grid=(rows // tm,),
        in_specs=[rs(D_MODEL), VMEM_SPEC, VMEM_SPEC, VMEM_SPEC, VMEM_SPEC, rs(128), rs(128)],
        out_specs=[rs(256), rs(256), rs(512), rs(512), rs(512), rs(128), rs(128), rs(128), rs(256), rs(256), rs(D_MODEL)],
        out_shape=[f32(256), f32(256), b16(512), f32(512), b16(512), b16(128), b16(128), f32(128), f32(256), f32(256),
                   b16(D_MODEL)],
        compiler_params=_params(("arbitrary",)),
    )(h, g, win_p, wa2_p, b_a, cos, sin)


def _gla_factors(q, k, bc):
    bm = bc[BLK // 2 - 1:BLK // 2, :]
    bl = bc[BLK - 1:BLK, :]
    e_q, e_k, e_qe, e_kd = jnp.exp(bc - bm), jnp.exp(bm - bc), jnp.exp(bc), jnp.exp(bl - bc)
    return (q * e_q, k * e_k, q * e_qe, k * e_kd), (e_q, e_k, e_qe, e_kd), jnp.exp(bl)


def _gla_fwd(gq, gk, gv, gg, bc, wgn):
    rows = gq.shape[0]
    nc = rows // BLK
    scale = GLA_DK ** -0.5

    def body(q_ref, k_ref, v_ref, gg_ref, bc_ref, wgn_ref, o_ref, cat_ref, sp_ref, st):
        @pl.when(pl.program_id(0) == 0)
        def _():
            st[...] = jnp.zeros_like(st)
        low = _tri(True)
        wgn_v = wgn_ref[...]
        for p in range(2):
            sl = slice(128 * p, 128 * p + 128)
            (qt, kt, qe, kd), _, ebl = _gla_factors(q_ref[:, sl] * scale, k_ref[:, sl], bc_ref[:, sl])
            s_prev = st[p]
            sp_ref[0, p] = s_prev
            s16 = s_prev.astype(BF16)
            qt16 = qt.astype(BF16)
            s_new = s_prev * ebl
            for hh in range(2):
                hs = slice(128 * (2 * p + hh), 128 * (2 * p + hh) + 128)
                lm = _half_mask(128, hh)
                vh = v_ref[:, hs]
                pm = jnp.where(low, _nt(qt16, jnp.where(lm, kt, 0.0).astype(BF16)), 0.0)
                o = _nn(pm.astype(BF16), vh) + _nt(jnp.where(lm, qe, 0.0).astype(BF16), s16)
                s_new = s_new + _tn(vh, jnp.where(lm, kd, 0.0).astype(BF16))
                o_ref[:, hs] = o
                on, _ = _rms(o)
                gate = gg_ref[:, hs]
                cat_ref[:, hs] = (on * wgn_v * (gate * _sigmoid(gate))).astype(BF16)
            st[p] = s_new

    rs = lambda c: _row_spec(BLK, c)
    return pl.pallas_call(
        body, name="gla_fwd", grid=(nc,),
        in_specs=[rs(256), rs(256), rs(512), rs(512), rs(256), VMEM_SPEC],
        out_specs=[rs(512), rs(512), pl.BlockSpec((1, 2, 128, 128), lambda i: (i, 0, 0, 0))],
        out_shape=[jax.ShapeDtypeStruct((rows, 512), F32), jax.ShapeDtypeStruct((rows, 512), BF16),
                   jax.ShapeDtypeStruct((nc, 2, 128, 128), F32)],
        scratch_shapes=[pltpu.VMEM((2, 128, 128), F32)],
        compiler_params=_params(("arbitrary",)),
    )(gq, gk, gv, gg, bc, wgn)


def _gla_bwd(dcat, o_all, gq, gk, gv, gg, bc, sp, wgn):
    rows = gq.shape[0]
    nc = rows // BLK
    scale = GLA_DK ** -0.5

    def body(dc_ref, o_ref, q_ref, k_ref, v_ref, gg_ref, bc_ref, sp_ref, wgn_ref,
             dq_ref, dk_ref, dv_ref, dgg_ref, dla_ref, dwgn_ref, dst):
        first = pl.program_id(0) == 0

        @pl.when(first)
        def _():
            dst[...] = jnp.zeros_like(dst)
        low, upp = _tri(True), _tri(False)
        last_row = lax.broadcasted_iota(jnp.int32, (BLK, 1), 0) == BLK - 1
        wgn_v = wgn_ref[...]
        dwgn = jnp.zeros((1, 128), F32)
        for p in range(2):
            sl = slice(128 * p, 128 * p + 128)
            (qt, kt, qe, kd), (e_q, e_k, e_qe, e_kd), ebl = _gla_factors(
                q_ref[:, sl] * scale, k_ref[:, sl], bc_ref[:, sl])
            s_prev = sp_ref[0, p]
            s16 = s_prev.astype(BF16)
            ds_next = dst[p]
            ds16 = ds_next.astype(BF16)
            qt16 = qt.astype(BF16)
            ds_new = ds_next * ebl
            dqt = jnp.zeros((BLK, 128), F32)
            dkt = jnp.zeros((BLK, 128), F32)
            dqe = jnp.zeros((BLK, 128), F32)
            dkd = jnp.zeros((BLK, 128), F32)
            for hh in range(2):
                hs = slice(128 * (2 * p + hh), 128 * (2 * p + hh) + 128)
                lm = _half_mask(128, hh)
                on, ro = _rms(o_ref[:, hs])
                gate = gg_ref[:, hs]
                sg = _sigmoid(gate)
                si = gate * sg
                dog = dc_ref[:, hs]
                dwgn = dwgn + _colsum(dog * si * on)
                dgg_ref[:, hs] = dog * (on * wgn_v) * (sg * (1.0 + gate * (1.0 - sg)))
                do16 = _rms_bwd(on, ro, wgn_v, dog * si).astype(BF16)
                vh = v_ref[:, hs]
                ktm16 = jnp.where(lm, kt, 0.0).astype(BF16)
                qtm16 = jnp.where(lm, qt, 0.0).astype(BF16)
                qem16 = jnp.where(lm, qe, 0.0).astype(BF16)
                kdm16 = jnp.where(lm, kd, 0.0).astype(BF16)
                p_t = jnp.where(upp, _nt(ktm16, qt16), 0.0)
                dp_t = jnp.where(upp, _nt(vh, do16), 0.0)
                dp = jnp.where(low, _nt(do16, vh), 0.0)
                dv_ref[:, hs] = _nn(p_t.astype(BF16), do16) + _nt(kdm16, ds16)
                dqt = dqt + _nn(dp.astype(BF16), ktm16)
                dkt = dkt + _nn(dp_t.astype(BF16), qtm16)
                dqe = dqe + jnp.where(lm, _nn(do16, s16), 0.0)
                dkd = dkd + jnp.where(lm, _nn(vh, ds16), 0.0)
                ds_new = ds_new + _tn(do16, qem16)
            debl = _colsum(ds_next * s_prev)
            dq_ref[:, sl] = (dqt * e_q + dqe * e_qe) * scale
            dk_ref[:, sl] = dkt * e_k + dkd * e_kd
            dkd_kd = dkd * kd
            db = dqt * qt - dkt * kt + dqe * qe - dkd_kd
            db = jnp.where(last_row, db + (_colsum(dkd_kd) + debl * ebl), db)
            dla_ref[:, sl] = _chunk_cumsum(db, False)
            dst[p] = ds_new
        _acc_add(dwgn_ref, first, dwgn)

    rev = lambda c: pl.BlockSpec((BLK, c), lambda i: (nc - 1 - i, 0))
    f32 = lambda c: jax.ShapeDtypeStruct((rows, c), F32)
    return pl.pallas_call(
        body, name="gla_bwd", grid=(nc,),
        in_specs=[rev(512), rev(512), rev(256), rev(256), rev(512), rev(512), rev(256),
                  pl.BlockSpec((1, 2, 128, 128), lambda i: (nc - 1 - i, 0, 0, 0)), VMEM_SPEC],
        out_specs=[rev(256), rev(256), rev(512), rev(512), rev(256), _acc_spec(128)],
        out_shape=[f32(256), f32(256), f32(512), f32(512), f32(256), jax.ShapeDtypeStruct((8, 128), F32)],
        scratch_shapes=[pltpu.VMEM((2, 128, 128), F32)],
        compiler_params=_params(("arbitrary",)),
    )(dcat, o_all, gq, gk, gv, gg, bc, sp, wgn)


def _swa_masks(i):
    t = lax.broadcasted_iota(jnp.int32, (BLK, BLK), 0)
    c = lax.broadcasted_iota(jnp.int32, (BLK, BLK), 1)
    own_side = c <= t
    band_ok = i >= jnp.where(own_side, 1, 2)
    meta_ok = (c % N_META) <= jnp.where(i >= 1, N_META, t - PAD_ROWS)
    return own_side, band_ok, meta_ok, c // N_META


def _swa_blocks(ref, i):
    prev = pl.multiple_of(jnp.maximum(i - 1, 0) * BLK, BLK)
    own = pl.multiple_of(i * BLK, BLK)
    return jnp.concatenate([ref[pl.ds(prev, BLK), :], ref[pl.ds(own, BLK), :]], axis=0), prev, own


def _swa_meta_operand(ref):
    blk = ref[0:BLK, :]
    swapped = pltpu.roll(blk, 64, 1)
    lo = jnp.where(_half_mask(128, 0), blk, swapped)
    hi = jnp.where(_half_mask(128, 1), blk, swapped)
    meta = jnp.concatenate([lo, lo, hi, hi], axis=1)[PAD_ROWS:BLK, :]
    tiled = jnp.concatenate([meta] * SWA_HEADS, axis=0)
    j = lax.broadcasted_iota(jnp.int32, tiled.shape, 0)
    lane = lax.broadcasted_iota(jnp.int32, tiled.shape, 1)
    return jnp.where(j // N_META == lane // SWA_HD, tiled, jnp.zeros_like(tiled))


def _swa_meta_fold(acc):
    out = jnp.zeros((N_META, 128), F32)
    for hd in range(SWA_HEADS):
        half, kv = hd % 2, hd // 4
        piece = acc[N_META * hd:N_META * (hd + 1), 128 * (hd // 2):128 * (hd // 2) + 128]
        piece = jnp.where(_half_mask(128, half), piece, 0.0)
        out = out + (piece if half == kv else pltpu.roll(piece, 64, 1))
    return out


def _by_head(group, per_head):
    out = jnp.zeros((BLK, BLK), F32)
    for hd, v in enumerate(per_head):
        out = jnp.where(group == hd, v, out)
    return out


def _place(x, kv):
    if kv == 0:
        lo = jnp.where(_half_mask(128, 0), x, jnp.zeros_like(x))
        return lo, pltpu.roll(lo, 64, 1)
    hi = jnp.where(_half_mask(128, 1), x, jnp.zeros_like(x))
    return pltpu.roll(hi, 64, 1), hi


def _swa_fwd(sq, sk, sv, sinks, wn):
    rows = sq.shape[0]
    nb = rows // BLK
    scale = SWA_HD ** -0.5

    def body(q_ref, k_ref, v_ref, sink_ref, wn_ref, o_ref, cat_ref, lse_ref, kp, vp):
        i = pl.program_id(0)

        @pl.when(i == 0)
        def _():
            kp[...] = _swa_meta_operand(k_ref)
            vp[...] = _swa_meta_operand(v_ref)
        own_side, band_ok, meta_ok, group = _swa_masks(i)
        k2, _, _ = _swa_blocks(k_ref, i)
        v2, _, _ = _swa_blocks(v_ref, i)
        kz = (_place(k2, 0), _place(k2, 1))
        vz = (_place(v2, 0), _place(v2, 1))
        q_all = q_ref[...]
        s_meta = jnp.where(meta_ok, _nt(q_all, kp[...]) * scale, NEG_INF)
        s_band, m = [], []
        for hd in range(SWA_HEADS):
            kv, half = hd // 4, hd % 2
            q_pair = q_all[:, 128 * (hd // 2):128 * (hd // 2) + 128]
            s2 = _nt(q_pair, kz[kv][half])
            s = jnp.where(band_ok, jnp.where(own_side, s2[:, BLK:], s2[:, :BLK]) * scale, NEG_INF)
            top = jnp.maximum(jnp.max(s, axis=-1, keepdims=True),
                              jnp.max(jnp.where(group == hd, s_meta, NEG_INF), axis=-1, keepdims=True))
            s_band.append(s)
            m.append(jnp.maximum(top, sink_ref[0, hd]))
        e_meta = jnp.exp(s_meta - _by_head(group, m))
        o_meta = _nn(e_meta.astype(BF16), vp[...])
        outs = []
        for pr in range(4):
            o_pair = o_meta[:, 128 * pr:128 * pr + 128]
            rden = []
            for half in range(2):
                hd = 2 * pr + half
                kv = hd // 4
                e = jnp.exp(s_band[hd] - m[hd])
                den = (jnp.sum(e, axis=-1, keepdims=True)
                       + jnp.sum(jnp.where(group == hd, e_meta, 0.0), axis=-1, keepdims=True)
                       + jnp.exp(sink_ref[0, hd] - m[hd]))
                lse_ref[:, hd:hd + 1] = m[hd] + jnp.log(den)
                rden.append(1.0 / den)
                e2 = jnp.concatenate([jnp.where(own_side, 0.0, e), jnp.where(own_side, e, 0.0)], axis=1).astype(BF16)
                o_pair = o_pair + _nn(e2, vz[kv][half])
            outs.append(o_pair * jnp.where(_half_mask(128, 0), rden[0], rden[1]))
        o = jnp.concatenate(outs, axis=1)
        o_ref[...] = o
        on, _ = _rms(o)
        cat_ref[...] = (on * wn_ref[...]).astype(BF16)

    return pl.pallas_call(
        body, name="swa_fwd", grid=(nb,),
        in_specs=[_row_spec(BLK, 512), VMEM_SPEC, VMEM_SPEC, SMEM_SPEC, VMEM_SPEC],
        out_specs=[_row_spec(BLK, 512), _row_spec(BLK, 512), _row_spec(BLK, SWA_HEADS)],
        out_shape=[jax.ShapeDtypeStruct((rows, 512), F32), jax.ShapeDtypeStruct((rows, 512), BF16),
                   jax.ShapeDtypeStruct((rows, SWA_HEADS), F32)],
        scratch_shapes=[pltpu.VMEM((BLK, 512), BF16), pltpu.VMEM((BLK, 512), BF16)],
        compiler_params=_params(("arbitrary",)),
    )(sq, sk, sv, sinks, wn)


def _swa_bwd(dcat, o_all, sq, sk, sv, lse, sinks, wn):
    rows = sq.shape[0]
    nb = rows // BLK
    scale = SWA_HD ** -0.5

    def body(dc_ref, o_ref, q_ref, k_ref, v_ref, lse_ref, sink_ref, wn_ref, dq_ref, dk_ref, dv_ref, dsink_ref, dwn_ref,
             kp, vp, dkp, dvp):
        i = pl.program_id(0)
        first = i == 0

        @pl.when(first)
        def _():
            dk_ref[...] = jnp.zeros_like(dk_ref)
            dv_ref[...] = jnp.zeros_like(dv_ref)
            dkp[...] = jnp.zeros_like(dkp)
            dvp[...] = jnp.zeros_like(dvp)
            kp[...] = _swa_meta_operand(k_ref)
            vp[...] = _swa_meta_operand(v_ref)
        own_side, band_ok, meta_ok, group = _swa_masks(i)
        k2, prev, own = _swa_blocks(k_ref, i)
        v2, _, _ = _swa_blocks(v_ref, i)
        kz = (_place(k2, 0), _place(k2, 1))
        vz = (_place(v2, 0), _place(v2, 1))
        o = o_ref[...]
        on, ro = _rms(o)
        dc = dc_ref[...]
        _acc_add(dwn_ref, first, _colsum(dc * on))
        do = _rms_bwd(on, ro, wn_ref[...], dc)
        do_o = do * o
        do16 = do.astype(BF16)
        q_all = q_ref[...]
        lse = [lse_ref[:, hd:hd + 1] for hd in range(SWA_HEADS)]
        delta = [jnp.sum(jnp.where(_half_mask(128, hd % 2), do_o[:, 128 * (hd // 2):128 * (hd // 2) + 128], 0.0),
                         axis=-1, keepdims=True) for hd in range(SWA_HEADS)]
        s_meta = jnp.where(meta_ok, _nt(q_all, kp[...]) * scale, NEG_INF)
        p_meta = jnp.exp(s_meta - _by_head(group, lse))
        ds_meta16 = (p_meta * (_nt(do16, vp[...]) - _by_head(group, delta)) * scale).astype(BF16)
        dq_meta = _nn(ds_meta16, kp[...])
        dkp[...] += _tn(ds_meta16, q_all)
        dvp[...] += _tn(p_meta.astype(BF16), do16)
        own2 = jnp.concatenate([own_side.astype(jnp.int32)] * 2, axis=0) > 0
        ok2 = jnp.concatenate([band_ok.astype(jnp.int32)] * 2, axis=0) > 0

        def window(x2):
            return jnp.where(own2, x2[:, BLK:], x2[:, :BLK])

        def unwindow(x):
            return jnp.concatenate([jnp.where(own2, 0.0, x), jnp.where(own2, x, 0.0)], axis=1).astype(BF16)
        lane8 = lax.broadcasted_iota(jnp.int32, (1, 128), 1)
        dsink = jnp.zeros((1, 128), F32)
        dq_pairs = [dq_meta[:, 128 * pr:128 * pr + 128] for pr in range(4)]
        dk2 = [[None, None], [None, None]]
        dv2 = [[None, None], [None, None]]
        for kv in range(2):
            for half in range(2):
                heads, pairs = (4 * kv + half, 4 * kv + 2 + half), (2 * kv, 2 * kv + 1)
                q_s = jnp.concatenate([q_all[:, 128 * pr:128 * pr + 128] for pr in pairs], axis=0)
                do_s = jnp.concatenate([do16[:, 128 * pr:128 * pr + 128] for pr in pairs], axis=0)
                lse_s = jnp.concatenate([lse[hd] for hd in heads], axis=0)
                delta_s = jnp.concatenate([delta[hd] for hd in heads], axis=0)
                s = jnp.where(ok2, window(_nt(q_s, kz[kv][half])) * scale, NEG_INF)
                prob = jnp.exp(s - lse_s)
                for hd in heads:
                    dsink = dsink + jnp.where(lane8 == hd, -jnp.sum(jnp.exp(sink_ref[0, hd] - lse[hd]) * delta[hd]), 0.0)
                ds2 = unwindow(prob * (window(_nt(do_s, vz[kv][half])) - delta_s) * scale)
                dq_s = _nn(ds2, kz[kv][half])
                dq_pairs[pairs[0]] = dq_pairs[pairs[0]] + dq_s[:BLK]
                dq_pairs[pairs[1]] = dq_pairs[pairs[1]] + dq_s[BLK:]
                dk2[kv][half] = _tn(ds2, q_s)
                dv2[kv][half] = _tn(unwindow(prob), do_s)
        dq_ref[...] = jnp.concatenate(dq_pairs, axis=1)
        _acc_add(dsink_ref, first, dsink)
        for ref, acc2 in ((dk_ref, dk2), (dv_ref, dv2)):
            tot = jnp.zeros((2 * BLK, 128), F32)
            for kv in range(2):
                for half in range(2):
                    part = jnp.where(_half_mask(128, half), acc2[kv][half], 0.0)
                    tot = tot + (part if half == kv else pltpu.roll(part, 64, 1))
            ref[pl.ds(prev, BLK), :] += tot[:BLK]
            ref[pl.ds(own, BLK), :] += tot[BLK:]

        @pl.when(i == nb - 1)
        def _():
            dk_ref[PAD_ROWS:BLK, :] += _swa_meta_fold(dkp[...])
            dv_ref[PAD_ROWS:BLK, :] += _swa_meta_fold(dvp[...])

    full = pl.BlockSpec((rows, 128), lambda i: (0, 0))
    return pl.pallas_call(
        body, name="swa_bwd", grid=(nb,),
        in_specs=[_row_spec(BLK, 512), _row_spec(BLK, 512), _row_spec(BLK, 512), VMEM_SPEC, VMEM_SPEC,
                  _row_spec(BLK, SWA_HEADS), SMEM_SPEC, VMEM_SPEC],
        out_specs=[_row_spec(BLK, 512), full, full, _acc_spec(128), _acc_spec(512)],
        out_shape=[jax.ShapeDtypeStruct((rows, 512), F32), jax.ShapeDtypeStruct((rows, 128), F32),
                   jax.ShapeDtypeStruct((rows, 128), F32), jax.ShapeDtypeStruct((8, 128), F32),
                   jax.ShapeDtypeStruct((8, 512), F32)],
        scratch_shapes=[pltpu.VMEM((BLK, 512), BF16), pltpu.VMEM((BLK, 512), BF16),
                        pltpu.VMEM((BLK, 512), F32), pltpu.VMEM((BLK, 512), F32)],
        compiler_params=_params(("arbitrary",)),
    )(dcat, o_all, sq, sk, sv, lse, sinks, wn)


def _mix_out(h, cat_g, cat_s, wout, gpost):
    rows = h.shape[0]
    tm = _row_tile(rows)

    def body(h_ref, cg_ref, cs_ref, w_ref, g_ref, ho_ref, m_ref):
        m = _nn(cg_ref[...], w_ref[0:512, :]) + _nn(cs_ref[...], w_ref[512:1024, :])
        m_ref[...] = m
        mn, _ = _rms(m)
        ho_ref[...] = h_ref[...] + mn * g_ref[...]

    row_f32 = _row_spec(tm, D_MODEL)
    return pl.pallas_call(
        body, name="mix_out", grid=(rows // tm,),
        in_specs=[row_f32, _row_spec(tm, 512), _row_spec(tm, 512), VMEM_SPEC, VMEM_SPEC],
        out_specs=[row_f32, row_f32],
        out_shape=[jax.ShapeDtypeStruct((rows, D_MODEL), F32), jax.ShapeDtypeStruct((rows, D_MODEL), F32)],
        compiler_params=_params(("arbitrary",)),
    )(h, cat_g, cat_s, wout, gpost)


def _mix_out_bwd(dh, m, wout, gpost):
    rows = dh.shape[0]
    tm = _row_tile(rows)

    def body(dh_ref, m_ref, w_ref, g_ref, dcg_ref, dcs_ref, dm_ref, dg_ref):
        first = pl.program_id(0) == 0
        dhv = dh_ref[...]
        mn, rm = _rms(m_ref[...])
        _acc_add(dg_ref, first, _colsum(dhv * mn))
        dm16 = _rms_bwd(mn, rm, g_ref[...], dhv).astype(BF16)
        dm_ref[...] = dm16
        dcat = _nt(dm16, w_ref[...])
        dcg_ref[...] = dcat[:, 0:512]
        dcs_ref[...] = dcat[:, 512:1024]

    row_f32 = _row_spec(tm, D_MODEL)
    return pl.pallas_call(
        body, name="mix_out_bwd", grid=(rows // tm,),
        in_specs=[row_f32, row_f32, VMEM_SPEC, VMEM_SPEC],
        out_specs=[_row_spec(tm, 512), _row_spec(tm, 512), row_f32, _acc_spec(D_MODEL)],
        out_shape=[jax.ShapeDtypeStruct((rows, 512), F32), jax.ShapeDtypeStruct((rows, 512), F32),
                   jax.ShapeDtypeStruct((rows, D_MODEL), BF16), jax.ShapeDtypeStruct((8, D_MODEL), F32)],
        compiler_params=_params(("arbitrary",)),
    )(dh, m, wout, gpost)


def _mix_in_bwd(dh_out, h, g, win_p, wa2_p, cos, sin, loga, ga, dgq, dgk, dgv, dgg, dsq, dsk, dsv, dloga):
    rows = h.shape[0]
    tm = _row_tile(rows)

    def body(dho_ref, h_ref, g_ref, win_ref, wa2_ref, cos_ref, sin_ref, loga_ref, ga_ref,
             dgq_ref, dgk_ref, dgv_ref, dgg_ref, dsq_ref, dsk_ref, dsv_ref, dla_ref,
             dh_ref, dproj_ref, dwa2_ref, dg_ref, dba_ref):
        first = pl.program_id(0) == 0
        dz = dla_ref[...] * (1.0 / GLA_TAU) * (1.0 - jnp.exp(GLA_TAU * loga_ref[...]))
        _acc_add(dba_ref, first, _colsum(dz))
        dga = _nt(dz, wa2_ref[...])
        pa = _tn(ga_ref[...], dz)
        c1, s1 = cos_ref[...], sin_ref[...]
        c4 = jnp.concatenate([c1, c1, c1, c1], axis=1)
        s4 = jnp.concatenate([s1, s1, s1, s1], axis=1)
        dq_r, dk_r = dsq_ref[...], dsk_ref[...]
        dsq = dq_r * c4 - _rot_half(dq_r * s4)
        dsk = dk_r * c1 - _rot_half(dk_r * s1)
        dproj16 = jnp.concatenate(
            [dgq_ref[...], dgk_ref[...], dgv_ref[...], dgg_ref[...], dsq, dsk, dsv_ref[...], dga], axis=1).astype(BF16)
        dproj_ref[...] = dproj16
        dn = _nn(dproj16, win_ref[...])

        @pl.when(first)
        def _():
            dwa2_ref[...] = pa

        @pl.when(jnp.logical_not(first))
        def _():
            dwa2_ref[...] += pa
        hn, rh = _rms(h_ref[...])
        _acc_add(dg_ref, first, _colsum(dn * hn))
        dh_ref[...] = dho_ref[...] + _rms_bwd(hn, rh, g_ref[...], dn)

    rs = lambda c: _row_spec(tm, c)
    return pl.pallas_call(
        body, name="mix_in_bwd", grid=(rows // tm,),
        in_specs=[rs(D_MODEL), rs(D_MODEL), VMEM_SPEC, VMEM_SPEC, VMEM_SPEC, rs(128), rs(128), rs(256), rs(128),
                  rs(256), rs(256), rs(512), rs(512), rs(512), rs(128), rs(128), rs(256)],
        out_specs=[rs(D_MODEL), rs(P_END), pl.BlockSpec((128, 256), lambda i: (0, 0)), _acc_spec(D_MODEL), _acc_spec(256)],
        out_shape=[jax.ShapeDtypeStruct((rows, D_MODEL), F32), jax.ShapeDtypeStruct((rows, P_END), BF16),
                   jax.ShapeDtypeStruct((128, 256), F32), jax.ShapeDtypeStruct((8, D_MODEL), F32),
                   jax.ShapeDtypeStruct((8, 256), F32)],
        compiler_params=_params(("arbitrary",)),
    )(dh_out, h, g, win_p, wa2_p, cos, sin, loga, ga, dgq, dgk, dgv, dgg, dsq, dsk, dsv, dloga)


def _rope_tables(rows):
    pos = (jnp.arange(rows, dtype=jnp.int32) - PAD_ROWS).astype(F32)
    inv_freq = 1.0 / (ROPE_THETA ** (jnp.arange(0, SWA_HD, 2, dtype=F32) / SWA_HD))
    ang = pos[:, None] * inv_freq[None, :]
    ang = jnp.concatenate([ang, ang, ang, ang], axis=-1)
    return jnp.cos(ang), jnp.sin(ang)


def _local_step(x, tgt, front, w, late_weights=None, on_grads=None):
    cos, sin = _rope_tables(x.shape[0] + BLK)
    g = {}

    def tell(group, names):
        for nm in names:
            g[nm] = grads_now[nm]
        return 0.0 if on_grads is None else on_grads(group, {nm: grads_now[nm] for nm in names})

    h0, h1, a1, b1, s1, f1 = _ffn_fwd(x, w["ffn1_pre"], w["wg1"], w["wu1"], w["wd1"], w["ffn1_post"], front=front)
    if late_weights is not None:
        w = {**w, **late_weights("win", f1)}
    gq, gk, gv, gg, sq, sk, sv, ga, loga, bc, n2 = _mix_in(h1, w["mix_pre"], w["win"], w["wa2"], w["b_a"], cos, sin)
    o_g, cat_g, sp = _gla_fwd(gq, gk, gv, gg, bc, w["gla_norm"])
    o_s, cat_s, lse = _swa_fwd(sq, sk, sv, w["sinks"], w["swa_norm"])
    if late_weights is not None:
        w = {**w, **late_weights("rest", lse)}
    h2, m = _mix_out(h1, cat_g, cat_s, w["wout"], w["mix_post"])
    h3, a2, b2, s2, f2, dy, loss = _ffn_fwd(h2, w["ffn2_pre"], w["wg2"], w["wu2"], w["wd2"], w["ffn2_post"], tgt)
    del h3
    dh2, da, db, df, n3, g["ffn2_pre"], g["ffn2_post"] = _ffn_bwd_act(
        dy, h2, a2, b2, f2, w["ffn2_pre"], w["ffn2_post"], w["wg2"], w["wu2"], w["wd2"], "ffn2_bwd_act")
    grads_now = dict(wd2=_wgrad(s2, df, "ffn2_wgrad_down"), wg2=_wgrad(da, n3, "ffn2_wgrad_gate"),
                     wu2=_wgrad(db, n3, "ffn2_wgrad_up"))
    tok = tell("ffn2", ("wd2", "wg2", "wu2"))
    dcg, dcs, dm, g["mix_post"] = _mix_out_bwd(dh2, m, w["wout"], w["mix_post"] + tok)
    dsq, dsk, dsv, g["sinks"], g["swa_norm"] = _swa_bwd(dcs, o_s, sq, sk, sv, lse, w["sinks"], w["swa_norm"])
    dgq, dgk, dgv, dgg, dloga, g["gla_norm"] = _gla_bwd(dcg, o_g, gq, gk, gv, gg, bc, sp, w["gla_norm"])
    dh1, dproj, g["wa2"], g["mix_pre"], g["b_a"] = _mix_in_bwd(
        dh2, h1, w["mix_pre"], w["win"], w["wa2"], cos, sin, loga, ga, dgq, dgk, dgv, dgg, dsq, dsk, dsv, dloga)
    grads_now = dict(wout=jnp.concatenate([_wgrad(cat_g, dm, "wout_wgrad_gla"), _wgrad(cat_s, dm, "wout_wgrad_swa")], axis=0),
                     win=_wgrad(dproj, n2, "win_wgrad"))
    tok = tell("mix", ("wout", "win"))
    dh0, da, db, df, n1, g["ffn1_pre"], g["ffn1_post"] = _ffn_bwd_act(
        dh1, h0, a1, b1, f1, w["ffn1_pre"] + tok, w["ffn1_post"], w["wg1"], w["wu1"], w["wd1"], "ffn1_bwd_act")
    grads_now = dict(wd1=_wgrad(s1, df, "ffn1_wgrad_down"))
    tell("ffn1_down", ("wd1",))
    grads_now = dict(wg1=_wgrad(da, n1, "ffn1_wgrad_gate"))
    tell("ffn1_gate", ("wg1",))
    grads_now = dict(wu1=_wgrad(db, n1, "ffn1_wgrad_up"))
    tell("ffn1_up", ("wu1",))
    return loss[0, 0], dh0, g


def _win_pad_rows(win_t):
    pad = jnp.zeros((P_END - P_GA - 16, win_t.shape[1]), win_t.dtype)
    return jnp.concatenate([win_t[0:1536], win_t[1552:2320], win_t[1536:1552], pad], axis=0)


def _win_unpad_rows(win_p):
    return jnp.concatenate([win_p[0:1536], win_p[P_GA:P_GA + 16], win_p[1536:P_GA]], axis=0)


def _place_on_mesh():
    return lax.axis_index("x"), lax.axis_index("y"), lax.axis_index("c")


def _dev_index(px, py, pc):
    return 4 * px + 2 * py + pc


def _other_devices(x, y, c):
    flip = lambda v, f: 1 - v if f else v
    return [(flip(x, fx), flip(y, fy), flip(c, fc)) for fx in (0, 1) for fy in (0, 1) for fc in (0, 1)][1:]


def _all_gather(shards):
    n = len(shards)

    def body(*refs):
        ins, outs = refs[:n], refs[n:2 * n]
        zeros_ref, send_sems, recv_sems, local_sems = refs[2 * n:]
        zeros_ref[...] = jnp.zeros_like(zeros_ref)
        x, y, c = _place_on_mesh()
        me, sibling = (x, y, c), (x, y, 1 - c)
        chips = [(1 - x, y), (x, 1 - y), (1 - x, 1 - y)]

        def rows(k, px, py, pc):
            r = ins[k].shape[0]
            return outs[k].at[pl.ds(pl.multiple_of(_dev_index(px, py, pc) * r, 8), r), :]

        def copy(k, slot, block, to, src=None):
            return pltpu.make_async_remote_copy(
                src_ref=rows(k, *block) if src is None else src, dst_ref=rows(k, *block),
                send_sem=send_sems.at[k, slot], recv_sem=recv_sems.at[k, slot], device_id=to, device_id_type=MESH)

        local = [pltpu.make_async_copy(ins[k], rows(k, *me), local_sems.at[k]) for k in range(n)]
        sends = []
        for k in range(n):
            local[k].start()
            sends.append(copy(k, 0, me, sibling, src=ins[k]))
            sends += [copy(k, 1 + j, me, (*chip, c), src=ins[k]) for j, chip in enumerate(chips)]
        for cp in sends:
            cp.start()
        for k in range(n):
            for j, chip in enumerate(chips):
                copy(k, 1 + j, (*chip, c), me).wait_recv()
                passed = copy(k, 4 + j, (*chip, c), sibling)
                passed.start()
                sends.append(passed)
        for k in range(n):
            copy(k, 0, sibling, me).wait_recv()
            for j, chip in enumerate(chips):
                copy(k, 4 + j, (*chip, 1 - c), me).wait_recv()
        for cp in sends:
            cp.wait_send()
        for cp in local:
            cp.wait()

    return pl.pallas_call(
        body, name="all_gather_weights",
        in_specs=[ANY_SPEC] * n, out_specs=[ANY_SPEC] * n + [VMEM_SPEC],
        out_shape=[jax.ShapeDtypeStruct((N_DEV * s.shape[0], s.shape[1]), s.dtype) for s in shards]
        + [jax.ShapeDtypeStruct((8, 128), F32)],
        scratch_shapes=[pltpu.SemaphoreType.DMA((n, 7)), pltpu.SemaphoreType.DMA((n, 7)), pltpu.SemaphoreType.DMA((n,))],
    )(*shards)


HBM_SPEC = pl.BlockSpec(memory_space=pltpu.HBM)
SEM_SPEC = pl.BlockSpec(memory_space=pltpu.SEMAPHORE)
DATAFLOW = pltpu.SideEffectType.DATAFLOW_SIDE_EFFECTING


GATHER, SCATTER, SCATTER_CHIPS = "gather", "scatter", "scatter among chips"


def _exchange_peers(kind):
    x, y, c = _place_on_mesh()
    if kind == SCATTER_CHIPS:
        peers = [(1 - x, y, c), (x, 1 - y, c), (1 - x, 1 - y, c)]
        return peers, [2 * p[0] + p[1] for p in peers], 2 * x + y, 4
    peers = _other_devices(x, y, c)
    return peers, [_dev_index(*p) for p in peers], _dev_index(x, y, c), N_DEV


def _exchange_copies(srcs, lands, send_sems, recv_sems, own_sems, kind, arriving):
    peers, theirs, me, blocks = _exchange_peers(kind)
    remote, local = [], []
    for k, (src, land) in enumerate(zip(srcs, lands)):
        r = land.shape[0] // blocks

        def block(ref, d):
            return ref.at[pl.ds(pl.multiple_of(d * r, 8), r), :]

        for f, (peer, him) in enumerate(zip(peers, theirs)):
            mine, his = (him, me) if arriving else (me, him)
            sem = len(peers) * k + f
            remote.append(pltpu.make_async_remote_copy(
                src_ref=src if kind == GATHER else block(src, his), dst_ref=block(land, mine),
                send_sem=send_sems.at[sem], recv_sem=recv_sems.at[sem], device_id=peer, device_id_type=MESH))
        local.append(pltpu.make_async_copy(src if kind == GATHER else block(src, me), block(land, me), own_sems.at[k]))
    return remote, local


def _exchange_start(srcs, kind, name):
    n = len(srcs)
    lands = [lax.empty((N_DEV * s.shape[0], s.shape[1]) if kind == GATHER else s.shape, s.dtype) for s in srcs]
    sems = (3 if kind == SCATTER_CHIPS else 7) * n

    def body(*refs):
        remote, local = _exchange_copies(refs[:n], refs[n:2 * n], *refs[2 * n:2 * n + 3], kind, False)
        for cp in remote + local:
            cp.start()
        refs[-1][...] = jnp.zeros_like(refs[-1])

    both = list(srcs) + list(lands)
    outs = pl.pallas_call(
        body, name=name,
        out_shape=(pltpu.SemaphoreType.DMA((sems,)), pltpu.SemaphoreType.DMA((sems,)), pltpu.SemaphoreType.DMA((n,)),
                   *[pltpu.HBM(a.shape, a.dtype) for a in both], jax.ShapeDtypeStruct((8, 128), F32)),
        in_specs=[HBM_SPEC] * (2 * n), out_specs=(SEM_SPEC, SEM_SPEC, SEM_SPEC, *[HBM_SPEC] * (2 * n), VMEM_SPEC),
        input_output_aliases={i: 3 + i for i in range(2 * n)},
        compiler_params=pltpu.CompilerParams(has_side_effects=DATAFLOW),
    )(*[pltpu.with_memory_space_constraint(a, pltpu.HBM) for a in both])
    return outs[0:3], outs[3:3 + n], outs[3 + n:3 + 2 * n], outs[-1]


def _exchange_wait(started, kind, after, name):
    sems, srcs, lands, _ = started
    n = len(srcs)

    def body(*refs):
        args = (refs[:n], refs[n:2 * n], *refs[2 * n:2 * n + 3], kind)
        going, local = _exchange_copies(*args, False)
        for cp in going:
            cp.wait_send()
        for cp in local:
            cp.wait()
        for cp in _exchange_copies(*args, True)[0]:
            cp.wait_recv()

    both = list(srcs) + list(lands)
    outs = pl.pallas_call(
        body, name=name, out_shape=[pltpu.HBM(a.shape, a.dtype) for a in both],
        in_specs=[HBM_SPEC] * (2 * n) + [SEM_SPEC, SEM_SPEC, SEM_SPEC, ANY_SPEC], out_specs=[HBM_SPEC] * (2 * n),
        input_output_aliases={i: i for i in range(2 * n)},
        compiler_params=pltpu.CompilerParams(has_side_effects=DATAFLOW),
    )(*both, *sems, after)
    return outs[n:]


def _sibling_reduce(part, name):
    r, cols = part.shape[0] // N_DEV, part.shape[1]

    def swap(p_ref, got_ref, send_sems, recv_sems):
        x, y, c = _place_on_mesh()
        copies = [pltpu.make_async_remote_copy(
            src_ref=p_ref.at[pl.ds(pl.multiple_of((2 * j + 1 - c) * r, 8), r), :], dst_ref=got_ref.at[pl.ds(j * r, r), :],
            send_sem=send_sems.at[j], recv_sem=recv_sems.at[j], device_id=(x, y, 1 - c), device_id_type=MESH)
            for j in range(4)]
        for cp in copies:
            cp.start()
        for cp in copies:
            cp.wait()

    got = pl.pallas_call(
        swap, name=name + "_swap", in_specs=[ANY_SPEC], out_specs=ANY_SPEC,
        out_shape=jax.ShapeDtypeStruct((4 * r, cols), part.dtype),
        scratch_shapes=[pltpu.SemaphoreType.DMA((4,)), pltpu.SemaphoreType.DMA((4,))],
    )(part)

    def add(c_ref, mine_ref, got_ref, o_ref):
        del c_ref
        o_ref[...] = (mine_ref[...].astype(F32) + got_ref[...].astype(F32)).astype(o_ref.dtype)

    core = lax.axis_index("c").astype(jnp.int32).reshape(1)
    return pl.pallas_call(
        add, name=name + "_add",
        grid_spec=pltpu.PrefetchScalarGridSpec(
            num_scalar_prefetch=1, grid=(4,),
            in_specs=[pl.BlockSpec((r, cols), lambda j, c_ref: (2 * j + c_ref[0], 0)),
                      pl.BlockSpec((r, cols), lambda j, c_ref: (j, 0))],
            out_specs=pl.BlockSpec((r, cols), lambda j, c_ref: (j, 0))),
        out_shape=jax.ShapeDtypeStruct((4 * r, cols), part.dtype),
        compiler_params=_params(("arbitrary",)),
    )(core, part, got)


def _sum_partials(parts, name, blocks=N_DEV):
    n = len(parts)

    def body(*refs):
        ins, outs = refs[:n], refs[n:]
        first = pl.program_id(0) == 0
        for i_ref, o_ref in zip(ins, outs):
            v = i_ref[...].astype(F32)

            @pl.when(first)
            def _():
                o_ref[...] = v

            @pl.when(jnp.logical_not(first))
            def _():
                o_ref[...] += v

    shapes = [(p.shape[0] // blocks, p.shape[1]) for p in parts]
    return pl.pallas_call(
        body, name=name, grid=(blocks,),
        in_specs=[pl.BlockSpec(s, lambda j: (j, 0)) for s in shapes],
        out_specs=[pl.BlockSpec(s, lambda j: (0, 0)) for s in shapes],
        out_shape=[jax.ShapeDtypeStruct(s, F32) for s in shapes],
        compiler_params=_params(("arbitrary",)),
    )(*parts)


def _all_reduce_small(slab):
    rows, cols = slab.shape

    def body(x_ref, o_ref, gathered, send_sems, recv_sems):
        x, y, c = _place_on_mesh()
        me = _dev_index(x, y, c)
        peers = _other_devices(x, y, c)

        def copy(f, peer):
            return pltpu.make_async_remote_copy(
                src_ref=x_ref, dst_ref=gathered.at[me], send_sem=send_sems.at[f], recv_sem=recv_sems.at[f],
                device_id=peer, device_id_type=MESH)

        def arrival(f, peer):
            return pltpu.make_async_remote_copy(
                src_ref=x_ref, dst_ref=gathered.at[_dev_index(*peer)], send_sem=send_sems.at[f], recv_sem=recv_sems.at[f],
                device_id=peer, device_id_type=MESH)

        sends = [copy(f, peer) for f, peer in enumerate(peers)]
        for cp in sends:
            cp.start()
        gathered[me] = x_ref[...]
        for f, peer in enumerate(peers):
            arrival(f, peer).wait_recv()
        for cp in sends:
            cp.wait_send()
        total = gathered[0]
        for d in range(1, N_DEV):
            total = total + gathered[d]
        o_ref[...] = total

    return pl.pallas_call(
        body, name="all_reduce_small",
        in_specs=[VMEM_SPEC], out_specs=VMEM_SPEC, out_shape=jax.ShapeDtypeStruct((rows, cols), F32),
        scratch_shapes=[pltpu.VMEM((N_DEV, rows, cols), F32), pltpu.SemaphoreType.DMA((7,)), pltpu.SemaphoreType.DMA((7,))],
    )(slab)


def _adamw_update(w, g, m, v):
    m = ADAM_B1 * m + (1.0 - ADAM_B1) * g
    v = ADAM_B2 * v + (1.0 - ADAM_B2) * (g * g)
    m_hat = m * (1.0 / (1.0 - ADAM_B1 ** ADAM_STEP))
    v_hat = v * (1.0 / (1.0 - ADAM_B2 ** ADAM_STEP))
    return -ADAM_LR * (m_hat / (jnp.sqrt(v_hat) + ADAM_EPS) + ADAM_WD * w), m, v


def _sum_adamw(parts, w, m, v, blocks, name):
    shape = w.shape

    def body(p_ref, w_ref, m_ref, v_ref, g_ref, d_ref, mo_ref, vo_ref):
        j = pl.program_id(0)
        part = p_ref[...].astype(F32)

        @pl.when(j == 0)
        def _():
            g_ref[...] = part

        @pl.when(j > 0)
        def _():
            g_ref[...] += part

        @pl.when(j == blocks - 1)
        def _():
            d_ref[...], mo_ref[...], vo_ref[...] = _adamw_update(w_ref[...], g_ref[...], m_ref[...], v_ref[...])

    held = pl.BlockSpec(shape, lambda j: (0, 0))
    return pl.pallas_call(
        body, name=name, grid=(blocks,),
        in_specs=[pl.BlockSpec(shape, lambda j: (j, 0)), held, held, held],
        out_specs=[held] * 4, out_shape=[jax.ShapeDtypeStruct(shape, F32)] * 4,
        compiler_params=_params(("arbitrary",)),
    )(parts, w, m, v)


def _adamw(ws, gs, ms, vs, name):
    n = len(ws)

    def body(*refs):
        w_r, g_r, m_r, v_r = refs[:n], refs[n:2 * n], refs[2 * n:3 * n], refs[3 * n:4 * n]
        d_o, m_o, v_o = refs[4 * n:5 * n], refs[5 * n:6 * n], refs[6 * n:7 * n]
        for k in range(n):
            d_o[k][...], m_o[k][...], v_o[k][...] = _adamw_update(w_r[k][...], g_r[k][...], m_r[k][...], v_r[k][...])

    shapes = [jax.ShapeDtypeStruct(w.shape, F32) for w in ws]
    outs = pl.pallas_call(
        body, name=name, in_specs=[VMEM_SPEC] * (4 * n), out_specs=[VMEM_SPEC] * (3 * n), out_shape=shapes * 3,
        compiler_params=pltpu.CompilerParams(vmem_limit_bytes=56 << 20),
    )(*ws, *gs, *ms, *vs)
    return outs[:n], outs[n:2 * n], outs[2 * n:]


WEIGHT_NAMES = ("meta_tokens", "ffn1_pre_norm", "ffn1_w_gate", "ffn1_w_up", "ffn1_w_down", "ffn1_post_norm", "mix_pre_norm",
                "w_in", "gla_w_a2", "gla_b_a", "gla_out_norm", "swa_sinks", "swa_out_norm", "w_out", "mix_post_norm",
                "ffn2_pre_norm", "ffn2_w_gate", "ffn2_w_up", "ffn2_w_down", "ffn2_post_norm")
WIN_SHARD = D_IN // N_DEV
WIN_SHARD_PAD = 304
SLAB_VECTORS = ("ffn1_pre", "ffn1_post", "mix_pre", "mix_post", "ffn2_pre", "ffn2_post")
SLAB_ROWS = 32


def kernel(x, meta_tokens, ffn1_pre_norm, ffn1_w_gate, ffn1_w_up, ffn1_w_down, ffn1_post_norm, mix_pre_norm, w_in, gla_w_a2, gla_b_a, gla_out_norm, swa_sinks, swa_out_norm, w_out, mix_post_norm, ffn2_pre_norm, ffn2_w_gate, ffn2_w_up, ffn2_w_down, ffn2_post_norm, loss_target, m_meta_tokens, m_ffn1_pre_norm, m_ffn1_w_gate, m_ffn1_w_up, m_ffn1_w_down, m_ffn1_post_norm, m_mix_pre_norm, m_w_in, m_gla_w_a2, m_gla_b_a, m_gla_out_norm, m_swa_sinks, m_swa_out_norm, m_w_out, m_mix_post_norm, m_ffn2_pre_norm, m_ffn2_w_gate, m_ffn2_w_up, m_ffn2_w_down, m_ffn2_post_norm, v_meta_tokens, v_ffn1_pre_norm, v_ffn1_w_gate, v_ffn1_w_up, v_ffn1_w_down, v_ffn1_post_norm, v_mix_pre_norm, v_w_in, v_gla_w_a2, v_gla_b_a, v_gla_out_norm, v_swa_sinks, v_swa_out_norm, v_w_out, v_mix_post_norm, v_ffn2_pre_norm, v_ffn2_w_gate, v_ffn2_w_up, v_ffn2_w_down, v_ffn2_post_norm):
    given = dict(locals())
    W = {n: given[n] for n in WEIGHT_NAMES}
    M = {n: given["m_" + n] for n in WEIGHT_NAMES}
    V = {n: given["v_" + n] for n in WEIGHT_NAMES}
    dev = _dev_index(*_place_on_mesh())

    def t16(w):
        return w[0].T.astype(BF16)

    small = jnp.concatenate([W["meta_tokens"], jnp.pad(W["gla_w_a2"][0], ((0, 0), (0, 96)))], axis=0)
    wg1, wu1, wd1, small_g, gathered_zeros = _all_gather(
        [t16(W["ffn1_w_gate"]), t16(W["ffn1_w_up"]), W["ffn1_w_down"][0].astype(BF16), small])
    def after_zero(shard, zeros):
        return shard + zeros[0:1, 0:1].astype(shard.dtype)
    win_shard = jnp.pad(t16(W["w_in"]), ((0, WIN_SHARD_PAD - WIN_SHARD), (0, 0)))
    win_shard = after_zero(win_shard, gathered_zeros)
    mid = _exchange_start([win_shard], GATHER, "gather_w_in_start")
    late_shards = [after_zero(W["w_out"][0].astype(BF16), mid[3]), t16(W["ffn2_w_gate"]), t16(W["ffn2_w_up"]),
                   W["ffn2_w_down"][0].astype(BF16)]
    late = _exchange_start(late_shards, GATHER, "gather_late_weights_start")

    def late_weights(what, after):
        if what == "win":
            win_g, = _exchange_wait(mid, GATHER, after, "gather_w_in_wait")
            win_t = win_g.reshape(N_DEV, WIN_SHARD_PAD, D_MODEL)[:, :WIN_SHARD].reshape(D_IN, D_MODEL)
            return dict(win=_win_pad_rows(win_t))
        wout, wg2, wu2, wd2 = _exchange_wait(late, GATHER, after, "gather_late_weights_wait")
        return dict(wout=wout, wg2=wg2, wu2=wu2, wd2=wd2)

    small_g = small_g.reshape(N_DEV, 32, 128)
    meta_full = small_g[:, :N_META].transpose(1, 0, 2).reshape(N_META, D_MODEL)
    wa2_full = small_g[:, N_META:, :32].transpose(1, 0, 2).reshape(16, 256)
    w = dict(
        ffn1_pre=W["ffn1_pre_norm"] + late[3][0, 0], ffn1_post=W["ffn1_post_norm"], mix_pre=W["mix_pre_norm"],
        mix_post=W["mix_post_norm"], ffn2_pre=W["ffn2_pre_norm"], ffn2_post=W["ffn2_post_norm"], b_a=W["gla_b_a"],
        gla_norm=W["gla_out_norm"], sinks=W["swa_sinks"], swa_norm=W["swa_out_norm"], wg1=wg1, wu1=wu1, wd1=wd1,
        wa2=jnp.pad(wa2_full, ((0, 112), (0, 0))))

    in_flight = []

    def on_grads(group, grads):
        parts = []
        for nm, p in grads.items():
            if nm == "win":
                p = _win_unpad_rows(p).reshape(N_DEV, WIN_SHARD, D_MODEL)
                p = jnp.pad(p, ((0, 0), (0, WIN_SHARD_PAD - WIN_SHARD), (0, 0))).reshape(N_DEV * WIN_SHARD_PAD, D_MODEL)
            parts.append(p)
        kind = SCATTER_CHIPS if group.startswith("ffn1") else SCATTER
        if kind == SCATTER_CHIPS:
            parts = [_sibling_reduce(p, "pair_" + group) for p in parts]
        started = _exchange_start(parts, kind, "scatter_" + group + "_start")
        in_flight.append((group, list(grads), started, kind))
        return started[3][0, 0]

    front = jnp.concatenate([jnp.zeros((PAD_ROWS, D_MODEL), F32), meta_full], axis=0)
    loss, dh0, g = _local_step(x[0], loss_target[0], front, w, late_weights, on_grads)
    grad_x = dh0[BLK:][None]

    packed = jnp.concatenate([g["b_a"][0:1], g["gla_norm"][0:1], g["sinks"][0:1], g["swa_norm"][0:1]], axis=1)
    slab = jnp.concatenate([g[k][0:1] for k in SLAB_VECTORS] + [packed, jnp.full((1, D_MODEL), loss, F32),
                           g["wa2"][:16].reshape(4, D_MODEL), jnp.zeros((4, D_MODEL), F32), dh0[PAD_ROWS:BLK]], axis=0)
    tot = _all_reduce_small(slab)
    loss = tot[7, 0]
    small_grads = dict(
        ffn1_pre_norm=tot[0:1], ffn1_post_norm=tot[1:2], mix_pre_norm=tot[2:3], mix_post_norm=tot[3:4],
        ffn2_pre_norm=tot[4:5], ffn2_post_norm=tot[5:6], gla_b_a=tot[6:7, 0:256], gla_out_norm=tot[6:7, 256:384],
        swa_sinks=tot[6:7, 384:392], swa_out_norm=tot[6:7, 512:1024],
        gla_w_a2=lax.dynamic_slice_in_dim(tot[8:12].reshape(16, 256), dev * 32, 32, axis=1)[None],
        meta_tokens=lax.dynamic_slice_in_dim(tot[16:32], dev * 128, 128, axis=1))

    big = dict(wg1=("ffn1_w_gate", True), wu1=("ffn1_w_up", True), wd1=("ffn1_w_down", False), win=("w_in", True),
               wout=("w_out", False), wg2=("ffn2_w_gate", True), wu2=("ffn2_w_up", True), wd2=("ffn2_w_down", False))
    grads = dict(small_grads)
    delta, new_m, new_v = {}, {}, {}
    names = [n for n in WEIGHT_NAMES if n not in [full for full, _ in big.values()]]
    two_d = lambda a: a.reshape(-1, a.shape[-1])
    d_, m_, v_ = _adamw([two_d(W[n]) for n in names], [two_d(grads[n]) for n in names],
                        [two_d(M[n]) for n in names], [two_d(V[n]) for n in names], "adamw_small")
    for k, n in enumerate(names):
        delta[n], new_m[n], new_v[n] = d_[k].reshape(W[n].shape), m_[k].reshape(W[n].shape), v_[k].reshape(W[n].shape)

    before_wait = d_[0] + in_flight[-1][2][3][0, 0]
    for group, shorts, started, kind in in_flight:
        lands = _exchange_wait(started, kind, before_wait, "scatter_" + group + "_wait")
        blocks = 4 if kind == SCATTER_CHIPS else N_DEV
        for short, land in zip(shorts, lands):
            n, transposed = big[short]
            to_slab = (lambda a: a[0].T) if transposed else (lambda a: a[0])
            from_slab = (lambda a: a.T[None]) if transposed else (lambda a: a[None])
            if short == "win":
                g_slab = _sum_partials([land], "sum_" + n, blocks)[0][:WIN_SHARD]
                d_, m_, v_ = _adamw([to_slab(W[n])], [g_slab], [to_slab(M[n])], [to_slab(V[n])], "adamw_" + n)
                d_, m_, v_ = d_[0], m_[0], v_[0]
            else:
                g_slab, d_, m_, v_ = _sum_adamw(land, to_slab(W[n]), to_slab(M[n]), to_slab(V[n]), blocks, "adamw_" + n)
            grads[n], delta[n], new_m[n], new_v[n] = from_slab(g_slab), from_slab(d_), from_slab(m_), from_slab(v_)
            before_wait = d_
    return (loss, grad_x, *[grads[n] for n in WEIGHT_NAMES], *[delta[n] for n in WEIGHT_NAMES],
            *[new_m[n] for n in WEIGHT_NAMES], *[new_v[n] for n in WEIGHT_NAMES])
```

```python
import math

import jax
import jax.numpy as jnp
from jax import lax
from jax.experimental import pallas as pl
from jax.experimental.pallas import tpu as pltpu

F32, BF16 = jnp.float32, jnp.bfloat16

D_MODEL = 1024
D_FF = 2816
N_META = 16
BLK = 128
PAD_ROWS = BLK - N_META
GLA_DK = 64
SWA_HD = 64
SWA_HEADS = 8
GLA_TAU = 16.0
NORM_EPS = 1e-6
NEG_INF = -1e30
ROPE_THETA = 10000.0
P_GQ, P_GK, P_GV, P_GG, P_SQ, P_SK, P_SV, P_GA, P_END = 0, 256, 512, 1024, 1536, 2048, 2176, 2304, 2432
D_IN = 2320
IN_SPLITS = (256, 256, 512, 512, 16, 512, 128, 128)
FF_TILE = 2816
WGRAD_TILE_MAX = 2432
N_DEV = 8
MESH = pl.DeviceIdType.MESH

ADAM_LR, ADAM_B1, ADAM_B2, ADAM_EPS, ADAM_WD, ADAM_STEP = 0.001, 0.9, 0.999, 1e-08, 0.01, 10

V7X_VMEM_BYTES = 64 << 20
VMEM_SPEC = pl.BlockSpec(memory_space=pltpu.VMEM)
SMEM_SPEC = pl.BlockSpec(memory_space=pltpu.SMEM)
ANY_SPEC = pl.BlockSpec(memory_space=pl.ANY)


def _params(semantics, vmem_mb=56):
    return pltpu.CompilerParams(dimension_semantics=semantics, vmem_limit_bytes=vmem_mb << 20)


def _row_tile(rows):
    return 416 if rows % 416 == 0 else BLK


def _nn(a, b):
    return lax.dot_general(a, b, (((1,), (0,)), ((), ())), preferred_element_type=F32)


def _nt(a, b):
    return lax.dot_general(a, b, (((1,), (1,)), ((), ())), preferred_element_type=F32)


def _tn(a, b):
    return lax.dot_general(a, b, (((0,), (0,)), ((), ())), preferred_element_type=F32)


def _rms(x):
    r = lax.rsqrt(jnp.mean(x * x, axis=-1, keepdims=True) + NORM_EPS)
    return x * r, r


def _rms_bwd(xn, r, w, dy):
    g = dy * w
    return r * (g - xn * jnp.mean(g * xn, axis=-1, keepdims=True))


def _sigmoid(x):
    return 1.0 / (1.0 + jnp.exp(-x))


def _colsum(x):
    return jnp.sum(x, axis=0, keepdims=True)


def _split_bf16(x):
    hi = x.astype(BF16)
    lo = (x - hi.astype(F32)).astype(BF16)
    return hi, lo


def _tri(lower):
    r = lax.broadcasted_iota(jnp.int32, (BLK, BLK), 0)
    c = lax.broadcasted_iota(jnp.int32, (BLK, BLK), 1)
    return (r >= c) if lower else (c >= r)


def _half_mask(width, half):
    lane = lax.broadcasted_iota(jnp.int32, (1, width), 1)
    return ((lane % 128) < 64) if half == 0 else ((lane % 128) >= 64)


def _rot_half(x):
    w = x.shape[-1]
    lane = lax.broadcasted_iota(jnp.int32, (1, w), 1)
    return jnp.where((lane % SWA_HD) < SWA_HD // 2, -pltpu.roll(x, w - SWA_HD // 2, 1), pltpu.roll(x, SWA_HD // 2, 1))


def _row_spec(tm, cols):
    return pl.BlockSpec((tm, cols), lambda i: (i, 0))


def _acc_spec(cols):
    return pl.BlockSpec((8, cols), lambda i: (0, 0))


def _acc_add(ref, first, value):
    @pl.when(first)
    def _():
        ref[...] = jnp.zeros_like(ref)
    ref[0:1, :] += value


def _behind_front(ref, i, tm, front):
    blk = ref[...]
    return jnp.where(i == 0, jnp.concatenate([front, blk[0:tm - BLK]], axis=0), blk)


def _ffn_fwd(h, gpre, wg_t, wu_t, wd, gpost, tgt=None, front=None):
    with_loss, with_front = tgt is not None, front is not None
    rows = h.shape[0] + (BLK if with_front else 0)
    tm = _row_tile(rows)
    nf = D_FF // FF_TILE

    def body(*refs):
        refs = list(refs)
        h_ref, gpre_ref, wg_ref, wu_ref, wd_ref, gpost_ref = refs[:6]
        del refs[:6]
        front_ref = refs.pop(0) if with_front else None
        t_ref = refs.pop(0) if with_loss else None
        h0_ref = refs.pop(0) if with_front else None
        ho_ref, a_ref, b_ref, s_ref, f_ref = refs[:5]
        dy_ref, loss_ref = refs[5:7] if with_loss else (None, None)
        acc = refs[-1]
        i = pl.program_id(0)
        if with_front:
            h_in = _behind_front(h_ref, i, tm, front_ref[...])
            h0_ref[...] = h_in
        else:
            h_in = h_ref[...]
        hn, _ = _rms(h_in)
        n16 = (hn * gpre_ref[...]).astype(BF16)
        for j in range(nf):
            cols = slice(j * FF_TILE, (j + 1) * FF_TILE)
            a = _nt(n16, wg_ref[cols, :])
            b = _nt(n16, wu_ref[cols, :])
            a_ref[:, cols] = a.astype(BF16)
            b_ref[:, cols] = b.astype(BF16)
            s16 = (a * _sigmoid(a) * b).astype(BF16)
            s_ref[:, cols] = s16
            part = _nn(s16, wd_ref[cols, :])
            if j == 0:
                acc[...] = part
            else:
                acc[...] += part
        f = acc[...]
        f_ref[...] = f
        fn, _ = _rms(f)
        y = h_in + 0.5 * (fn * gpost_ref[...])
        ho_ref[...] = y
        if with_loss:
            row = i * tm + lax.broadcasted_iota(jnp.int32, (tm, 1), 0)
            err = jnp.where(row >= BLK, y - _behind_front(t_ref, i, tm, jnp.zeros((BLK, D_MODEL), F32)), 0.0)
            dy_ref[...] = err * (1.0 / D_MODEL)
            part = 0.5 * jnp.sum(jnp.sum(err * err, axis=-1, keepdims=True) * (1.0 / D_MODEL), axis=0, keepdims=True)

            @pl.when(i == 0)
            def _():
                loss_ref[...] = jnp.zeros_like(loss_ref)
            loss_ref[...] += part

    row_f32 = _row_spec(tm, D_MODEL)
    behind = pl.BlockSpec((pl.Element(tm), pl.Element(D_MODEL)),
                          lambda i: (pl.multiple_of(jnp.maximum(i * tm - BLK, 0), math.gcd(tm, BLK)), 0))
    in_specs = [behind if with_front else row_f32, VMEM_SPEC, VMEM_SPEC, VMEM_SPEC, VMEM_SPEC, VMEM_SPEC]
    out_specs = [row_f32, _row_spec(tm, D_FF), _row_spec(tm, D_FF), _row_spec(tm, D_FF), row_f32]
    out_shape = [jax.ShapeDtypeStruct((rows, D_MODEL), F32), jax.ShapeDtypeStruct((rows, D_FF), BF16),
                 jax.ShapeDtypeStruct((rows, D_FF), BF16), jax.ShapeDtypeStruct((rows, D_FF), BF16),
                 jax.ShapeDtypeStruct((rows, D_MODEL), F32)]
    args = [h, gpre, wg_t, wu_t, wd, gpost]
    if with_front:
        in_specs.append(VMEM_SPEC)
        args.append(front)
        out_specs.insert(0, row_f32)
        out_shape.insert(0, jax.ShapeDtypeStruct((rows, D_MODEL), F32))
    if with_loss:
        in_specs.append(behind)
        args.append(tgt)
        out_specs += [row_f32, pl.BlockSpec((8, 128), lambda i: (0, 0))]
        out_shape += [jax.ShapeDtypeStruct((rows, D_MODEL), F32), jax.ShapeDtypeStruct((8, 128), F32)]
    return pl.pallas_call(
        body, name="ffn_fwd_loss" if with_loss else "ffn_fwd", grid=(rows // tm,),
        in_specs=in_specs, out_specs=out_specs, out_shape=out_shape,
        scratch_shapes=[pltpu.VMEM((tm, D_MODEL), F32)],
        compiler_params=_params(("arbitrary",)),
    )(*args)


def _ffn_bwd_act(dh_out, h, a, b, f, gpre, gpost, wg_t, wu_t, wd, name):
    rows = h.shape[0]
    tm = _row_tile(rows)
    nf = D_FF // FF_TILE

    def body(dho_ref, h_ref, a_ref, b_ref, f_ref, gpre_ref, gpost_ref, wg_ref, wu_ref, wd_ref,
             dh_ref, da_ref, db_ref, df_ref, n_ref, dgpre_ref, dgpost_ref, acc):
        first = pl.program_id(0) == 0
        dho = dho_ref[...]
        drr = 0.5 * dho
        fn, rf = _rms(f_ref[...])
        _acc_add(dgpost_ref, first, _colsum(drr * fn))
        df16 = _rms_bwd(fn, rf, gpost_ref[...], drr).astype(BF16)
        df_ref[...] = df16
        hn, rh = _rms(h_ref[...])
        n_ref[...] = (hn * gpre_ref[...]).astype(BF16)
        for j in range(nf):
            cols = slice(j * FF_TILE, (j + 1) * FF_TILE)
            ds = _nt(df16, wd_ref[cols, :])
            av = a_ref[:, cols].astype(F32)
            bv = b_ref[:, cols].astype(F32)
            sg = _sigmoid(av)
            db16 = (ds * (av * sg)).astype(BF16)
            da16 = (ds * bv * (sg * (1.0 + av * (1.0 - sg)))).astype(BF16)
            da_ref[:, cols] = da16
            db_ref[:, cols] = db16
            part = _nn(da16, wg_ref[cols, :]) + _nn(db16, wu_ref[cols, :])
            if j == 0:
                acc[...] = part
            else:
                acc[...] += part
        dn = acc[...]
        _acc_add(dgpre_ref, first, _colsum(dn * hn))
        dh_ref[...] = dho + _rms_bwd(hn, rh, gpre_ref[...], dn)

    row_f32 = _row_spec(tm, D_MODEL)
    row_ff = _row_spec(tm, D_FF)
    return pl.pallas_call(
        body, name=name, grid=(rows // tm,),
        in_specs=[row_f32, row_f32, row_ff, row_ff, row_f32, VMEM_SPEC, VMEM_SPEC, VMEM_SPEC, VMEM_SPEC, VMEM_SPEC],
        out_specs=[row_f32, row_ff, row_ff, row_f32, row_f32, _acc_spec(D_MODEL), _acc_spec(D_MODEL)],
        out_shape=[jax.ShapeDtypeStruct((rows, D_MODEL), F32), jax.ShapeDtypeStruct((rows, D_FF), BF16),
                   jax.ShapeDtypeStruct((rows, D_FF), BF16), jax.ShapeDtypeStruct((rows, D_MODEL), BF16),
                   jax.ShapeDtypeStruct((rows, D_MODEL), BF16), jax.ShapeDtypeStruct((8, D_MODEL), F32),
                   jax.ShapeDtypeStruct((8, D_MODEL), F32)],
        scratch_shapes=[pltpu.VMEM((tm, D_MODEL), F32)],
        compiler_params=_params(("arbitrary",), vmem_mb=62),
    )(dh_out, h, a, b, f, gpre, gpost, wg_t, wu_t, wd)


def _wgrad(lhs, rhs, name, after=None):
    rows, width = lhs.shape
    tm = rows if rows % 1664 == 0 else BLK
    tf = 256 if width % 256 == 0 else 128
    nr = rows // tm

    def body(l_ref, r_ref, *rest):
        o_ref, acc = rest[-2:]
        i = pl.program_id(1)
        part = _tn(l_ref[...], r_ref[...])

        @pl.when(i == 0)
        def _():
            acc[...] = part

        @pl.when(i > 0)
        def _():
            acc[...] += part

        @pl.when(i == nr - 1)
        def _():
            o_ref[...] = acc[...].astype(BF16)

    l_spec = pl.BlockSpec((tm, tf), lambda j, i: (i, j))
    r_spec = pl.BlockSpec((tm, D_MODEL), lambda j, i: (i, 0))
    return pl.pallas_call(
        body, name=name, grid=(width // tf, nr),
        in_specs=[l_spec, r_spec] + ([] if after is None else [ANY_SPEC]),
        out_specs=pl.BlockSpec((tf, D_MODEL), lambda j, i: (j, 0)),
        out_shape=jax.ShapeDtypeStruct((width, D_MODEL), BF16),
        scratch_shapes=[pltpu.VMEM((tf, D_MODEL), F32)],
        compiler_params=_params(("arbitrary", "arbitrary")),
    )(lhs, rhs, *([] if after is None else [after]))


def _chunk_cumsum(x, lower):
    tri = jnp.where(_tri(lower), 1.0, 0.0).astype(BF16)
    hi, lo = _split_bf16(x)
    return _nn(tri, hi) + _nn(tri, lo)


def _mix_in(h, g, win_p, wa2_p, b_a, cos, sin):
    rows = h.shape[0]
    tm = 640 if rows % 640 == 0 else BLK

    def body(h_ref, g_ref, win_ref, wa2_ref, ba_ref, cos_ref, sin_ref,
             gq_ref, gk_ref, gv_ref, gg_ref, sq_ref, sk_ref, sv_ref, ga_ref, loga_ref, bc_ref, n_ref):
        hn, _ = _rms(h_ref[...])
        n16 = (hn * g_ref[...]).astype(BF16)
        n_ref[...] = n16
        proj = _nt(n16, win_ref[...])
        gq_ref[...] = proj[:, P_GQ:P_GK]
        gk_ref[...] = proj[:, P_GK:P_GV]
        gv_ref[...] = proj[:, P_GV:P_GG].astype(BF16)
        gg_ref[...] = proj[:, P_GG:P_SQ]
        c1, s1 = cos_ref[...], sin_ref[...]
        c4 = jnp.concatenate([c1, c1, c1, c1], axis=1)
        s4 = jnp.concatenate([s1, s1, s1, s1], axis=1)
        sq = proj[:, P_SQ:P_SK]
        sk = proj[:, P_SK:P_SV]
        sq_ref[...] = (sq * c4 + _rot_half(sq) * s4).astype(BF16)
        sk_ref[...] = (sk * c1 + _rot_half(sk) * s1).astype(BF16)
        sv_ref[...] = proj[:, P_SV:P_GA].astype(BF16)
        ga = proj[:, P_GA:P_END]
        ga_ref[...] = ga
        z = _nn(ga, wa2_ref[...]) + ba_ref[...]
        loga = (jnp.minimum(z, 0.0) - jnp.log(1.0 + jnp.exp(-jnp.abs(z)))) * (1.0 / GLA_TAU)
        loga_ref[...] = loga
        for c in range(tm // BLK):
            rs = slice(c * BLK, (c + 1) * BLK)
            bc_ref[rs, :] = _chunk_cumsum(loga[rs, :], True)

    f32 = lambda c: jax.ShapeDtypeStruct((rows, c), F32)
    b16 = lambda c: jax.ShapeDtypeStruct((rows, c), BF16)
    rs = lambda c: _row_spec(tm, c)
    return pl.pallas_call(
        body, name="mix_in", grid=(rows // tm,),
        in_specs=[rs(D_MODEL), VMEM_SPEC, VMEM_SPEC, VMEM_SPEC, VMEM_SPEC, rs(128), rs(128)],
        out_specs=[rs(256), rs(256), rs(512), rs(512), rs(512), rs(128), rs(128), rs(128), rs(256), rs(256), rs(D_MODEL)],
        out_shape=[f32(256), f32(256), b16(512), f32(512), b16(512), b16(128), b16(128), f32(128), f32(256), f32(256),
                   b16(D_MODEL)],
        compiler_params=_params(("arbitrary",)),
    )(h, g, win_p, wa2_p, b_a, cos, sin)


def _gla_factors(q, k, bc):
    bm = bc[BLK // 2 - 1:BLK // 2, :]
    bl = bc[BLK - 1:BLK, :]
    e_q, e_k, e_qe, e_kd = jnp.exp(bc - bm), jnp.exp(bm - bc), jnp.exp(bc), jnp.exp(bl - bc)
    return (q * e_q, k * e_k, q * e_qe, k * e_kd), (e_q, e_k, e_qe, e_kd), jnp.exp(bl)


def _gla_fwd(gq, gk, gv, gg, bc, wgn):
    rows = gq.shape[0]
    nc = rows // BLK
    scale = GLA_DK ** -0.5

    def body(q_ref, k_ref, v_ref, gg_ref, bc_ref, wgn_ref, o_ref, cat_ref, sp_ref, st):
        @pl.when(pl.program_id(0) == 0)
        def _():
            st[...] = jnp.zeros_like(st)
        low = _tri(True)
        wgn_v = wgn_ref[...]
        for p in range(2):
            sl = slice(128 * p, 128 * p + 128)
            (qt, kt, qe, kd), _, ebl = _gla_factors(q_ref[:, sl] * scale, k_ref[:, sl], bc_ref[:, sl])
            s_prev = st[p]
            sp_ref[0, p] = s_prev
            s16 = s_prev.astype(BF16)
            qt16 = qt.astype(BF16)
            s_new = s_prev * ebl
            for hh in range(2):
                hs = slice(128 * (2 * p + hh), 128 * (2 * p + hh) + 128)
                lm = _half_mask(128, hh)
                vh = v_ref[:, hs]
                pm = jnp.where(low, _nt(qt16, jnp.where(lm, kt, 0.0).astype(BF16)), 0.0)
                o = _nn(pm.astype(BF16), vh) + _nt(jnp.where(lm, qe, 0.0).astype(BF16), s16)
                s_new = s_new + _tn(vh, jnp.where(lm, kd, 0.0).astype(BF16))
                o_ref[:, hs] = o
                on, _ = _rms(o)
                gate = gg_ref[:, hs]
                cat_ref[:, hs] = (on * wgn_v * (gate * _sigmoid(gate))).astype(BF16)
            st[p] = s_new

    rs = lambda c: _row_spec(BLK, c)
    return pl.pallas_call(
        body, name="gla_fwd", grid=(nc,),
        in_specs=[rs(256), rs(256), rs(512), rs(512), rs(256), VMEM_SPEC],
        out_specs=[rs(512), rs(512), pl.BlockSpec((1, 2, 128, 128), lambda i: (i, 0, 0, 0))],
        out_shape=[jax.ShapeDtypeStruct((rows, 512), F32), jax.ShapeDtypeStruct((rows, 512), BF16),
                   jax.ShapeDtypeStruct((nc, 2, 128, 128), F32)],
        scratch_shapes=[pltpu.VMEM((2, 128, 128), F32)],
        compiler_params=_params(("arbitrary",)),
    )(gq, gk, gv, gg, bc, wgn)


def _gla_bwd(dcat, o_all, gq, gk, gv, gg, bc, sp, wgn):
    rows = gq.shape[0]
    nc = rows // BLK
    scale = GLA_DK ** -0.5

    def body(dc_ref, o_ref, q_ref, k_ref, v_ref, gg_ref, bc_ref, sp_ref, wgn_ref,
             dq_ref, dk_ref, dv_ref, dgg_ref, dla_ref, dwgn_ref, dst):
        first = pl.program_id(0) == 0

        @pl.when(first)
        def _():
            dst[...] = jnp.zeros_like(dst)
        low, upp = _tri(True), _tri(False)
        last_row = lax.broadcasted_iota(jnp.int32, (BLK, 1), 0) == BLK - 1
        wgn_v = wgn_ref[...]
        dwgn = jnp.zeros((1, 128), F32)
        for p in range(2):
            sl = slice(128 * p, 128 * p + 128)
            (qt, kt, qe, kd), (e_q, e_k, e_qe, e_kd), ebl = _gla_factors(
                q_ref[:, sl] * scale, k_ref[:, sl], bc_ref[:, sl])
            s_prev = sp_ref[0, p]
            s16 = s_prev.astype(BF16)
            ds_next = dst[p]
            ds16 = ds_next.astype(BF16)
            qt16 = qt.astype(BF16)
            ds_new = ds_next * ebl
            dqt = jnp.zeros((BLK, 128), F32)
            dkt = jnp.zeros((BLK, 128), F32)
            dqe = jnp.zeros((BLK, 128), F32)
            dkd = jnp.zeros((BLK, 128), F32)
            for hh in range(2):
                hs = slice(128 * (2 * p + hh), 128 * (2 * p + hh) + 128)
                lm = _half_mask(128, hh)
                on, ro = _rms(o_ref[:, hs])
                gate = gg_ref[:, hs]
                sg = _sigmoid(gate)
                si = gate * sg
                dog = dc_ref[:, hs]
                dwgn = dwgn + _colsum(dog * si * on)
                dgg_ref[:, hs] = dog * (on * wgn_v) * (sg * (1.0 + gate * (1.0 - sg)))
                do16 = _rms_bwd(on, ro, wgn_v, dog * si).astype(BF16)
                vh = v_ref[:, hs]
                ktm16 = jnp.where(lm, kt, 0.0).astype(BF16)
                qtm16 = jnp.where(lm, qt, 0.0).astype(BF16)
                qem16 = jnp.where(lm, qe, 0.0).astype(BF16)
                kdm16 = jnp.where(lm, kd, 0.0).astype(BF16)
                p_t = jnp.where(upp, _nt(ktm16, qt16), 0.0)
                dp_t = jnp.where(upp, _nt(vh, do16), 0.0)
                dp = jnp.where(low, _nt(do16, vh), 0.0)
                dv_ref[:, hs] = _nn(p_t.astype(BF16), do16) + _nt(kdm16, ds16)
                dqt = dqt + _nn(dp.astype(BF16), ktm16)
                dkt = dkt + _nn(dp_t.astype(BF16), qtm16)
                dqe = dqe + jnp.where(lm, _nn(do16, s16), 0.0)
                dkd = dkd + jnp.where(lm, _nn(vh, ds16), 0.0)
                ds_new = ds_new + _tn(do16, qem16)
            debl = _colsum(ds_next * s_prev)
            dq_ref[:, sl] = (dqt * e_q + dqe * e_qe) * scale
            dk_ref[:, sl] = dkt * e_k + dkd * e_kd
            dkd_kd = dkd * kd
            db = dqt * qt - dkt * kt + dqe * qe - dkd_kd
            db = jnp.where(last_row, db + (_colsum(dkd_kd) + debl * ebl), db)
            dla_ref[:, sl] = _chunk_cumsum(db, False)
            dst[p] = ds_new
        _acc_add(dwgn_ref, first, dwgn)

    rev = lambda c: pl.BlockSpec((BLK, c), lambda i: (nc - 1 - i, 0))
    f32 = lambda c: jax.ShapeDtypeStruct((rows, c), F32)
    return pl.pallas_call(
        body, name="gla_bwd", grid=(nc,),
        in_specs=[rev(512), rev(512), rev(256), rev(256), rev(512), rev(512), rev(256),
                  pl.BlockSpec((1, 2, 128, 128), lambda i: (nc - 1 - i, 0, 0, 0)), VMEM_SPEC],
        out_specs=[rev(256), rev(256), rev(512), rev(512), rev(256), _acc_spec(128)],
        out_shape=[f32(256), f32(256), f32(512), f32(512), f32(256), jax.ShapeDtypeStruct((8, 128), F32)],
        scratch_shapes=[pltpu.VMEM((2, 128, 128), F32)],
        compiler_params=_params(("arbitrary",)),
    )(dcat, o_all, gq, gk, gv, gg, bc, sp, wgn)


def _swa_masks(i):
    t = lax.broadcasted_iota(jnp.int32, (BLK, BLK), 0)
    c = lax.broadcasted_iota(jnp.int32, (BLK, BLK), 1)
    own_side = c <= t
    band_ok = i >= jnp.where(own_side, 1, 2)
    meta_ok = (c % N_META) <= jnp.where(i >= 1, N_META, t - PAD_ROWS)
    return own_side, band_ok, meta_ok, c // N_META


def _swa_blocks(ref, i):
    prev = pl.multiple_of(jnp.maximum(i - 1, 0) * BLK, BLK)
    own = pl.multiple_of(i * BLK, BLK)
    return jnp.concatenate([ref[pl.ds(prev, BLK), :], ref[pl.ds(own, BLK), :]], axis=0), prev, own


def _swa_meta_operand(ref):
    blk = ref[0:BLK, :]
    swapped = pltpu.roll(blk, 64, 1)
    lo = jnp.where(_half_mask(128, 0), blk, swapped)
    hi = jnp.where(_half_mask(128, 1), blk, swapped)
    meta = jnp.concatenate([lo, lo, hi, hi], axis=1)[PAD_ROWS:BLK, :]
    tiled = jnp.concatenate([meta] * SWA_HEADS, axis=0)
    j = lax.broadcasted_iota(jnp.int32, tiled.shape, 0)
    lane = lax.broadcasted_iota(jnp.int32, tiled.shape, 1)
    return jnp.where(j // N_META == lane // SWA_HD, tiled, jnp.zeros_like(tiled))


def _swa_meta_fold(acc):
    out = jnp.zeros((N_META, 128), F32)
    for hd in range(SWA_HEADS):
        half, kv = hd % 2, hd // 4
        piece = acc[N_META * hd:N_META * (hd + 1), 128 * (hd // 2):128 * (hd // 2) + 128]
        piece = jnp.where(_half_mask(128, half), piece, 0.0)
        out = out + (piece if half == kv else pltpu.roll(piece, 64, 1))
    return out


def _by_head(group, per_head):
    out = jnp.zeros((BLK, BLK), F32)
    for hd, v in enumerate(per_head):
        out = jnp.where(group == hd, v, out)
    return out


def _place(x, kv):
    if kv == 0:
        lo = jnp.where(_half_mask(128, 0), x, jnp.zeros_like(x))
        return lo, pltpu.roll(lo, 64, 1)
    hi = jnp.where(_half_mask(128, 1), x, jnp.zeros_like(x))
    return pltpu.roll(hi, 64, 1), hi


def _swa_fwd(sq, sk, sv, sinks, wn):
    rows = sq.shape[0]
    nb = rows // BLK
    scale = SWA_HD ** -0.5

    def body(q_ref, k_ref, v_ref, sink_ref, wn_ref, o_ref, cat_ref, lse_ref, kp, vp):
        i = pl.program_id(0)

        @pl.when(i == 0)
        def _():
            kp[...] = _swa_meta_operand(k_ref)
            vp[...] = _swa_meta_operand(v_ref)
        own_side, band_ok, meta_ok, group = _swa_masks(i)
        k2, _, _ = _swa_blocks(k_ref, i)
        v2, _, _ = _swa_blocks(v_ref, i)
        kz = (_place(k2, 0), _place(k2, 1))
        vz = (_place(v2, 0), _place(v2, 1))
        q_all = q_ref[...]
        s_meta = jnp.where(meta_ok, _nt(q_all, kp[...]) * scale, NEG_INF)
        s_band, m = [], []
        for hd in range(SWA_HEADS):
            kv, half = hd // 4, hd % 2
            q_pair = q_all[:, 128 * (hd // 2):128 * (hd // 2) + 128]
            s2 = _nt(q_pair, kz[kv][half])
            s = jnp.where(band_ok, jnp.where(own_side, s2[:, BLK:], s2[:, :BLK]) * scale, NEG_INF)
            top = jnp.maximum(jnp.max(s, axis=-1, keepdims=True),
                              jnp.max(jnp.where(group == hd, s_meta, NEG_INF), axis=-1, keepdims=True))
            s_band.append(s)
            m.append(jnp.maximum(top, sink_ref[0, hd]))
        e_meta = jnp.exp(s_meta - _by_head(group, m))
        o_meta = _nn(e_meta.astype(BF16), vp[...])
        outs = []
        for pr in range(4):
            o_pair = o_meta[:, 128 * pr:128 * pr + 128]
            rden = []
            for half in range(2):
                hd = 2 * pr + half
                kv = hd // 4
                e = jnp.exp(s_band[hd] - m[hd])
                den = (jnp.sum(e, axis=-1, keepdims=True)
                       + jnp.sum(jnp.where(group == hd, e_meta, 0.0), axis=-1, keepdims=True)
                       + jnp.exp(sink_ref[0, hd] - m[hd]))
                lse_ref[:, hd:hd + 1] = m[hd] + jnp.log(den)
                rden.append(1.0 / den)
                e2 = jnp.concatenate([jnp.where(own_side, 0.0, e), jnp.where(own_side, e, 0.0)], axis=1).astype(BF16)
                o_pair = o_pair + _nn(e2, vz[kv][half])
            outs.append(o_pair * jnp.where(_half_mask(128, 0), rden[0], rden[1]))
        o = jnp.concatenate(outs, axis=1)
        o_ref[...] = o
        on, _ = _rms(o)
        cat_ref[...] = (on * wn_ref[...]).astype(BF16)

    return pl.pallas_call(
        body, name="swa_fwd", grid=(nb,),
        in_specs=[_row_spec(BLK, 512), VMEM_SPEC, VMEM_SPEC, SMEM_SPEC, VMEM_SPEC],
        out_specs=[_row_spec(BLK, 512), _row_spec(BLK, 512), _row_spec(BLK, SWA_HEADS)],
        out_shape=[jax.ShapeDtypeStruct((rows, 512), F32), jax.ShapeDtypeStruct((rows, 512), BF16),
                   jax.ShapeDtypeStruct((rows, SWA_HEADS), F32)],
        scratch_shapes=[pltpu.VMEM((BLK, 512), BF16), pltpu.VMEM((BLK, 512), BF16)],
        compiler_params=_params(("arbitrary",)),
    )(sq, sk, sv, sinks, wn)


def _swa_bwd(dcat, o_all, sq, sk, sv, lse, sinks, wn):
    rows = sq.shape[0]
    nb = rows // BLK
    scale = SWA_HD ** -0.5

    def body(dc_ref, o_ref, q_ref, k_ref, v_ref, lse_ref, sink_ref, wn_ref, dq_ref, dk_ref, dv_ref, dsink_ref, dwn_ref,
             kp, vp, dkp, dvp):
        i = pl.program_id(0)
        first = i == 0

        @pl.when(first)
        def _():
            dk_ref[...] = jnp.zeros_like(dk_ref)
            dv_ref[...] = jnp.zeros_like(dv_ref)
            dkp[...] = jnp.zeros_like(dkp)
            dvp[...] = jnp.zeros_like(dvp)
            kp[...] = _swa_meta_operand(k_ref)
            vp[...] = _swa_meta_operand(v_ref)
        own_side, band_ok, meta_ok, group = _swa_masks(i)
        k2, prev, own = _swa_blocks(k_ref, i)
        v2, _, _ = _swa_blocks(v_ref, i)
        kz = (_place(k2, 0), _place(k2, 1))
        vz = (_place(v2, 0), _place(v2, 1))
        o = o_ref[...]
        on, ro = _rms(o)
        dc = dc_ref[...]
        _acc_add(dwn_ref, first, _colsum(dc * on))
        do = _rms_bwd(on, ro, wn_ref[...], dc)
        do_o = do * o
        do16 = do.astype(BF16)
        q_all = q_ref[...]
        lse = [lse_ref[:, hd:hd + 1] for hd in range(SWA_HEADS)]
        delta = [jnp.sum(jnp.where(_half_mask(128, hd % 2), do_o[:, 128 * (hd // 2):128 * (hd // 2) + 128], 0.0),
                         axis=-1, keepdims=True) for hd in range(SWA_HEADS)]
        s_meta = jnp.where(meta_ok, _nt(q_all, kp[...]) * scale, NEG_INF)
        p_meta = jnp.exp(s_meta - _by_head(group, lse))
        ds_meta16 = (p_meta * (_nt(do16, vp[...]) - _by_head(group, delta)) * scale).astype(BF16)
        dq_meta = _nn(ds_meta16, kp[...])
        dkp[...] += _tn(ds_meta16, q_all)
        dvp[...] += _tn(p_meta.astype(BF16), do16)
        own2 = jnp.concatenate([own_side.astype(jnp.int32)] * 2, axis=0) > 0
        ok2 = jnp.concatenate([band_ok.astype(jnp.int32)] * 2, axis=0) > 0

        def window(x2):
            return jnp.where(own2, x2[:, BLK:], x2[:, :BLK])

        def unwindow(x):
            return jnp.concatenate([jnp.where(own2, 0.0, x), jnp.where(own2, x, 0.0)], axis=1).astype(BF16)
        lane8 = lax.broadcasted_iota(jnp.int32, (1, 128), 1)
        dsink = jnp.zeros((1, 128), F32)
        dq_pairs = [dq_meta[:, 128 * pr:128 * pr + 128] for pr in range(4)]
        dk2 = [[None, None], [None, None]]
        dv2 = [[None, None], [None, None]]
        for kv in range(2):
            for half in range(2):
                heads, pairs = (4 * kv + half, 4 * kv + 2 + half), (2 * kv, 2 * kv + 1)
                q_s = jnp.concatenate([q_all[:, 128 * pr:128 * pr + 128] for pr in pairs], axis=0)
                do_s = jnp.concatenate([do16[:, 128 * pr:128 * pr + 128] for pr in pairs], axis=0)
                lse_s = jnp.concatenate([lse[hd] for hd in heads], axis=0)
                delta_s = jnp.concatenate([delta[hd] for hd in heads], axis=0)
                s = jnp.where(ok2, window(_nt(q_s, kz[kv][half])) * scale, NEG_INF)
                prob = jnp.exp(s - lse_s)
                for hd in heads:
                    dsink = dsink + jnp.where(lane8 == hd, -jnp.sum(jnp.exp(sink_ref[0, hd] - lse[hd]) * delta[hd]), 0.0)
                ds2 = unwindow(prob * (window(_nt(do_s, vz[kv][half])) - delta_s) * scale)
                dq_s = _nn(ds2, kz[kv][half])
                dq_pairs[pairs[0]] = dq_pairs[pairs[0]] + dq_s[:BLK]
                dq_pairs[pairs[1]] = dq_pairs[pairs[1]] + dq_s[BLK:]
                dk2[kv][half] = _tn(ds2, q_s)
                dv2[kv][half] = _tn(unwindow(prob), do_s)
        dq_ref[...] = jnp.concatenate(dq_pairs, axis=1)
        _acc_add(dsink_ref, first, dsink)
        for ref, acc2 in ((dk_ref, dk2), (dv_ref, dv2)):
            tot = jnp.zeros((2 * BLK, 128), F32)
            for kv in range(2):
                for half in range(2):
                    part = jnp.where(_half_mask(128, half), acc2[kv][half], 0.0)
                    tot = tot + (part if half == kv else pltpu.roll(part, 64, 1))
            ref[pl.ds(prev, BLK), :] += tot[:BLK]
            ref[pl.ds(own, BLK), :] += tot[BLK:]

        @pl.when(i == nb - 1)
        def _():
            dk_ref[PAD_ROWS:BLK, :] += _swa_meta_fold(dkp[...])
            dv_ref[PAD_ROWS:BLK, :] += _swa_meta_fold(dvp[...])

    full = pl.BlockSpec((rows, 128), lambda i: (0, 0))
    return pl.pallas_call(
        body, name="swa_bwd", grid=(nb,),
        in_specs=[_row_spec(BLK, 512), _row_spec(BLK, 512), _row_spec(BLK, 512), VMEM_SPEC, VMEM_SPEC,
                  _row_spec(BLK, SWA_HEADS), SMEM_SPEC, VMEM_SPEC],
        out_specs=[_row_spec(BLK, 512), full, full, _acc_spec(128), _acc_spec(512)],
        out_shape=[jax.ShapeDtypeStruct((rows, 512), F32), jax.ShapeDtypeStruct((rows, 128), F32),
                   jax.ShapeDtypeStruct((rows, 128), F32), jax.ShapeDtypeStruct((8, 128), F32),
                   jax.ShapeDtypeStruct((8, 512), F32)],
        scratch_shapes=[pltpu.VMEM((BLK, 512), BF16), pltpu.VMEM((BLK, 512), BF16),
                        pltpu.VMEM((BLK, 512), F32), pltpu.VMEM((BLK, 512), F32)],
        compiler_params=_params(("arbitrary",)),
    )(dcat, o_all, sq, sk, sv, lse, sinks, wn)


def _mix_out(h, cat_g, cat_s, wout, gpost):
    rows = h.shape[0]
    tm = _row_tile(rows)

    def body(h_ref, cg_ref, cs_ref, w_ref, g_ref, ho_ref, m_ref):
        m = _nn(cg_ref[...], w_ref[0:512, :]) + _nn(cs_ref[...], w_ref[512:1024, :])
        m_ref[...] = m
        mn, _ = _rms(m)
        ho_ref[...] = h_ref[...] + mn * g_ref[...]

    row_f32 = _row_spec(tm, D_MODEL)
    return pl.pallas_call(
        body, name="mix_out", grid=(rows // tm,),
        in_specs=[row_f32, _row_spec(tm, 512), _row_spec(tm, 512), VMEM_SPEC, VMEM_SPEC],
        out_specs=[row_f32, row_f32],
        out_shape=[jax.ShapeDtypeStruct((rows, D_MODEL), F32), jax.ShapeDtypeStruct((rows, D_MODEL), F32)],
        compiler_params=_params(("arbitrary",)),
    )(h, cat_g, cat_s, wout, gpost)


def _mix_out_bwd(dh, m, wout, gpost):
    rows = dh.shape[0]
    tm = _row_tile(rows)

    def body(dh_ref, m_ref, w_ref, g_ref, dcg_ref, dcs_ref, dm_ref, dg_ref):
        first = pl.program_id(0) == 0
        dhv = dh_ref[...]
        mn, rm = _rms(m_ref[...])
        _acc_add(dg_ref, first, _colsum(dhv * mn))
        dm16 = _rms_bwd(mn, rm, g_ref[...], dhv).astype(BF16)
        dm_ref[...] = dm16
        dcat = _nt(dm16, w_ref[...])
        dcg_ref[...] = dcat[:, 0:512]
        dcs_ref[...] = dcat[:, 512:1024]

    row_f32 = _row_spec(tm, D_MODEL)
    return pl.pallas_call(
        body, name="mix_out_bwd", grid=(rows // tm,),
        in_specs=[row_f32, row_f32, VMEM_SPEC, VMEM_SPEC],
        out_specs=[_row_spec(tm, 512), _row_spec(tm, 512), row_f32, _acc_spec(D_MODEL)],
        out_shape=[jax.ShapeDtypeStruct((rows, 512), F32), jax.ShapeDtypeStruct((rows, 512), F32),
                   jax.ShapeDtypeStruct((rows, D_MODEL), BF16), jax.ShapeDtypeStruct((8, D_MODEL), F32)],
        compiler_params=_params(("arbitrary",)),
    )(dh, m, wout, gpost)


def _mix_in_bwd(dh_out, h, g, win_p, wa2_p, cos, sin, loga, ga, dgq, dgk, dgv, dgg, dsq, dsk, dsv, dloga):
    rows = h.shape[0]
    tm = _row_tile(rows)

    def body(dho_ref, h_ref, g_ref, win_ref, wa2_ref, cos_ref, sin_ref, loga_ref, ga_ref,
             dgq_ref, dgk_ref, dgv_ref, dgg_ref, dsq_ref, dsk_ref, dsv_ref, dla_ref,
             dh_ref, dproj_ref, dwa2_ref, dg_ref, dba_ref):
        first = pl.program_id(0) == 0
        dz = dla_ref[...] * (1.0 / GLA_TAU) * (1.0 - jnp.exp(GLA_TAU * loga_ref[...]))
        _acc_add(dba_ref, first, _colsum(dz))
        dga = _nt(dz, wa2_ref[...])
        pa = _tn(ga_ref[...], dz)
        c1, s1 = cos_ref[...], sin_ref[...]
        c4 = jnp.concatenate([c1, c1, c1, c1], axis=1)
        s4 = jnp.concatenate([s1, s1, s1, s1], axis=1)
        dq_r, dk_r = dsq_ref[...], dsk_ref[...]
        dsq = dq_r * c4 - _rot_half(dq_r * s4)
        dsk = dk_r * c1 - _rot_half(dk_r * s1)
        dproj16 = jnp.concatenate(
            [dgq_ref[...], dgk_ref[...], dgv_ref[...], dgg_ref[...], dsq, dsk, dsv_ref[...], dga], axis=1).astype(BF16)
        dproj_ref[...] = dproj16
        dn = _nn(dproj16, win_ref[...])

        @pl.when(first)
        def _():
            dwa2_ref[...] = pa

        @pl.when(jnp.logical_not(first))
        def _():
            dwa2_ref[...] += pa
        hn, rh = _rms(h_ref[...])
        _acc_add(dg_ref, first, _colsum(dn * hn))
        dh_ref[...] = dho_ref[...] + _rms_bwd(hn, rh, g_ref[...], dn)

    rs = lambda c: _row_spec(tm, c)
    return pl.pallas_call(
        body, name="mix_in_bwd", grid=(rows // tm,),
        in_specs=[rs(D_MODEL), rs(D_MODEL), VMEM_SPEC, VMEM_SPEC, VMEM_SPEC, rs(128), rs(128), rs(256), rs(128),
                  rs(256), rs(256), rs(512), rs(512), rs(512), rs(128), rs(128), rs(256)],
        out_specs=[rs(D_MODEL), rs(P_END), pl.BlockSpec((128, 256), lambda i: (0, 0)), _acc_spec(D_MODEL), _acc_spec(256)],
        out_shape=[jax.ShapeDtypeStruct((rows, D_MODEL), F32), jax.ShapeDtypeStruct((rows, P_END), BF16),
                   jax.ShapeDtypeStruct((128, 256), F32), jax.ShapeDtypeStruct((8, D_MODEL), F32),
                   jax.ShapeDtypeStruct((8, 256), F32)],
        compiler_params=_params(("arbitrary",)),
    )(dh_out, h, g, win_p, wa2_p, cos, sin, loga, ga, dgq, dgk, dgv, dgg, dsq, dsk, dsv, dloga)


def _rope_tables(rows):
    pos = (jnp.arange(rows, dtype=jnp.int32) - PAD_ROWS).astype(F32)
    inv_freq = 1.0 / (ROPE_THETA ** (jnp.arange(0, SWA_HD, 2, dtype=F32) / SWA_HD))
    ang = pos[:, None] * inv_freq[None, :]
    ang = jnp.concatenate([ang, ang, ang, ang], axis=-1)
    return jnp.cos(ang), jnp.sin(ang)


def _local_step(x, tgt, front, w, late_weights=None, on_grads=None):
    cos, sin = _rope_tables(x.shape[0] + BLK)
    g = {}

    def tell(group, names):
        for nm in names:
            g[nm] = grads_now[nm]
        return None if on_grads is None else on_grads(group, {nm: grads_now[nm] for nm in names})

    h0, h1, a1, b1, s1, f1 = _ffn_fwd(x, w["ffn1_pre"], w["wg1"], w["wu1"], w["wd1"], w["ffn1_post"], front=front)
    if late_weights is not None:
        w = {**w, **late_weights("win", f1)}
    gq, gk, gv, gg, sq, sk, sv, ga, loga, bc, n2 = _mix_in(h1, w["mix_pre"], w["win"], w["wa2"], w["b_a"], cos, sin)
    o_g, cat_g, sp = _gla_fwd(gq, gk, gv, gg, bc, w["gla_norm"])
    o_s, cat_s, lse = _swa_fwd(sq, sk, sv, w["sinks"], w["swa_norm"])
    if late_weights is not None:
        w = {**w, **late_weights("rest", lse)}
    h2, m = _mix_out(h1, cat_g, cat_s, w["wout"], w["mix_post"])
    h3, a2, b2, s2, f2, dy, loss = _ffn_fwd(h2, w["ffn2_pre"], w["wg2"], w["wu2"], w["wd2"], w["ffn2_post"], tgt)
    del h3
    dh2, da, db, df, n3, g["ffn2_pre"], g["ffn2_post"] = _ffn_bwd_act(
        dy, h2, a2, b2, f2, w["ffn2_pre"], w["ffn2_post"], w["wg2"], w["wu2"], w["wd2"], "ffn2_bwd_act")
    grads_now = dict(wd2=_wgrad(s2, df, "ffn2_wgrad_down"), wg2=_wgrad(da, n3, "ffn2_wgrad_gate"),
                     wu2=_wgrad(db, n3, "ffn2_wgrad_up"))
    tok = tell("ffn2", ("wd2", "wg2", "wu2"))
    dcg, dcs, dm, g["mix_post"] = _mix_out_bwd(dh2, m, w["wout"], w["mix_post"] + (0.0 if tok is None else tok[0, 0]))
    dsq, dsk, dsv, g["sinks"], g["swa_norm"] = _swa_bwd(dcs, o_s, sq, sk, sv, lse, w["sinks"], w["swa_norm"])
    dgq, dgk, dgv, dgg, dloga, g["gla_norm"] = _gla_bwd(dcg, o_g, gq, gk, gv, gg, bc, sp, w["gla_norm"])
    dh1, dproj, g["wa2"], g["mix_pre"], g["b_a"] = _mix_in_bwd(
        dh2, h1, w["mix_pre"], w["win"], w["wa2"], cos, sin, loga, ga, dgq, dgk, dgv, dgg, dsq, dsk, dsv, dloga)
    dh0, da, db, df, n1, g["ffn1_pre"], g["ffn1_post"] = _ffn_bwd_act(
        dh1, h0, a1, b1, f1, w["ffn1_pre"], w["ffn1_post"], w["wg1"], w["wu1"], w["wd1"], "ffn1_bwd_act")
    grads_now = dict(wd1=_wgrad(s1, df, "ffn1_wgrad_down"))
    tell("ffn1_down", ("wd1",))
    grads_now = dict(wg1=_wgrad(da, n1, "ffn1_wgrad_gate"))
    tell("ffn1_gate", ("wg1",))
    grads_now = dict(wu1=_wgrad(db, n1, "ffn1_wgrad_up"))
    tok = tell("ffn1_up", ("wu1",))
    grads_now = dict(win=_wgrad(dproj, n2, "win_wgrad", after=tok),
                     wout=jnp.concatenate([_wgrad(cat_g, dm, "wout_wgrad_gla", after=tok),
                                           _wgrad(cat_s, dm, "wout_wgrad_swa", after=tok)], axis=0))
    tell("mix", ("wout", "win"))
    return loss[0, 0], dh0, g


def _win_pad_rows(win_t):
    pad = jnp.zeros((P_END - P_GA - 16, win_t.shape[1]), win_t.dtype)
    return jnp.concatenate([win_t[0:1536], win_t[1552:2320], win_t[1536:1552], pad], axis=0)


def _win_unpad_rows(win_p):
    return jnp.concatenate([win_p[0:1536], win_p[P_GA:P_GA + 16], win_p[1536:P_GA]], axis=0)


def _place_on_mesh():
    return lax.axis_index("x"), lax.axis_index("y"), lax.axis_index("c")


def _dev_index(px, py, pc):
    return 4 * px + 2 * py + pc


def _other_devices(x, y, c):
    flip = lambda v, f: 1 - v if f else v
    return [(flip(x, fx), flip(y, fy), flip(c, fc)) for fx in (0, 1) for fy in (0, 1) for fc in (0, 1)][1:]


def _all_gather(shards):
    n = len(shards)

    def body(*refs):
        ins, outs = refs[:n], refs[n:2 * n]
        zeros_ref, send_sems, recv_sems, local_sems = refs[2 * n:]
        zeros_ref[...] = jnp.zeros_like(zeros_ref)
        x, y, c = _place_on_mesh()
        me, sibling = (x, y, c), (x, y, 1 - c)
        chips = [(1 - x, y), (x, 1 - y), (1 - x, 1 - y)]

        def rows(k, px, py, pc):
            r = ins[k].shape[0]
            return outs[k].at[pl.ds(pl.multiple_of(_dev_index(px, py, pc) * r, 8), r), :]

        def copy(k, slot, block, to, src=None):
            return pltpu.make_async_remote_copy(
                src_ref=rows(k, *block) if src is None else src, dst_ref=rows(k, *block),
                send_sem=send_sems.at[k, slot], recv_sem=recv_sems.at[k, slot], device_id=to, device_id_type=MESH)

        local = [pltpu.make_async_copy(ins[k], rows(k, *me), local_sems.at[k]) for k in range(n)]
        sends = []
        for k in range(n):
            local[k].start()
            sends.append(copy(k, 0, me, sibling, src=ins[k]))
            sends += [copy(k, 1 + j, me, (*chip, c), src=ins[k]) for j, chip in enumerate(chips)]
        for cp in sends:
            cp.start()
        for k in range(n):
            for j, chip in enumerate(chips):
                copy(k, 1 + j, (*chip, c), me).wait_recv()
                passed = copy(k, 4 + j, (*chip, c), sibling)
                passed.start()
                sends.append(passed)
        for k in range(n):
            copy(k, 0, sibling, me).wait_recv()
            for j, chip in enumerate(chips):
                copy(k, 4 + j, (*chip, 1 - c), me).wait_recv()
        for cp in sends:
            cp.wait_send()
        for cp in local:
            cp.wait()

    return pl.pallas_call(
        body, name="all_gather_weights",
        in_specs=[ANY_SPEC] * n, out_specs=[ANY_SPEC] * n + [VMEM_SPEC],
        out_shape=[jax.ShapeDtypeStruct((N_DEV * s.shape[0], s.shape[1]), s.dtype) for s in shards]
        + [jax.ShapeDtypeStruct((8, 128), F32)],
        scratch_shapes=[pltpu.SemaphoreType.DMA((n, 7)), pltpu.SemaphoreType.DMA((n, 7)), pltpu.SemaphoreType.DMA((n,))],
    )(*shards)


HBM_SPEC = pl.BlockSpec(memory_space=pltpu.HBM)
SEM_SPEC = pl.BlockSpec(memory_space=pltpu.SEMAPHORE)
DATAFLOW = pltpu.SideEffectType.DATAFLOW_SIDE_EFFECTING


GATHER, SCATTER, SCATTER_CHIPS = "gather", "scatter", "scatter among chips"


def _exchange_peers(kind):
    x, y, c = _place_on_mesh()
    if kind == SCATTER_CHIPS:
        peers = [(1 - x, y, c), (x, 1 - y, c), (1 - x, 1 - y, c)]
        return peers, [2 * p[0] + p[1] for p in peers], 2 * x + y, 4
    peers = _other_devices(x, y, c)
    return peers, [_dev_index(*p) for p in peers], _dev_index(x, y, c), N_DEV


def _exchange_copies(srcs, lands, send_sems, recv_sems, own_sems, kind, arriving):
    peers, theirs, me, blocks = _exchange_peers(kind)
    remote, local = [], []
    for k, (src, land) in enumerate(zip(srcs, lands)):
        r = land.shape[0] // blocks

        def block(ref, d):
            return ref.at[pl.ds(pl.multiple_of(d * r, 8), r), :]

        for f, (peer, him) in enumerate(zip(peers, theirs)):
            mine, his = (him, me) if arriving else (me, him)
            sem = len(peers) * k + f
            remote.append(pltpu.make_async_remote_copy(
                src_ref=src if kind == GATHER else block(src, his), dst_ref=block(land, mine),
                send_sem=send_sems.at[sem], recv_sem=recv_sems.at[sem], device_id=peer, device_id_type=MESH))
        local.append(pltpu.make_async_copy(src if kind == GATHER else block(src, me), block(land, me), own_sems.at[k]))
    return remote, local


def _exchange_start(srcs, kind, name):
    n = len(srcs)
    lands = [lax.empty((N_DEV * s.shape[0], s.shape[1]) if kind == GATHER else s.shape, s.dtype) for s in srcs]
    sems = (3 if kind == SCATTER_CHIPS else 7) * n

    def body(*refs):
        remote, local = _exchange_copies(refs[:n], refs[n:2 * n], *refs[2 * n:2 * n + 3], kind, False)
        for cp in remote + local:
            cp.start()
        refs[-1][...] = jnp.zeros_like(refs[-1])

    both = list(srcs) + list(lands)
    outs = pl.pallas_call(
        body, name=name,
        out_shape=(pltpu.SemaphoreType.DMA((sems,)), pltpu.SemaphoreType.DMA((sems,)), pltpu.SemaphoreType.DMA((n,)),
                   *[pltpu.HBM(a.shape, a.dtype) for a in both], jax.ShapeDtypeStruct((8, 128), F32)),
        in_specs=[HBM_SPEC] * (2 * n), out_specs=(SEM_SPEC, SEM_SPEC, SEM_SPEC, *[HBM_SPEC] * (2 * n), VMEM_SPEC),
        input_output_aliases={i: 3 + i for i in range(2 * n)},
        compiler_params=pltpu.CompilerParams(has_side_effects=DATAFLOW),
    )(*[pltpu.with_memory_space_constraint(a, pltpu.HBM) for a in both])
    return outs[0:3], outs[3:3 + n], outs[3 + n:3 + 2 * n], outs[-1]


def _exchange_wait(started, kind, after, name):
    sems, srcs, lands, _ = started
    n = len(srcs)

    def body(*refs):
        args = (refs[:n], refs[n:2 * n], *refs[2 * n:2 * n + 3], kind)
        going, local = _exchange_copies(*args, False)
        for cp in going:
            cp.wait_send()
        for cp in local:
            cp.wait()
        for cp in _exchange_copies(*args, True)[0]:
            cp.wait_recv()

    both = list(srcs) + list(lands)
    outs = pl.pallas_call(
        body, name=name, out_shape=[pltpu.HBM(a.shape, a.dtype) for a in both],
        in_specs=[HBM_SPEC] * (2 * n) + [SEM_SPEC, SEM_SPEC, SEM_SPEC, ANY_SPEC], out_specs=[HBM_SPEC] * (2 * n),
        input_output_aliases={i: i for i in range(2 * n)},
        compiler_params=pltpu.CompilerParams(has_side_effects=DATAFLOW),
    )(*both, *sems, after)
    return outs[n:]


def _sibling_reduce(part, name):
    r, cols = part.shape[0] // N_DEV, part.shape[1]

    def swap(p_ref, got_ref, send_sems, recv_sems):
        x, y, c = _place_on_mesh()
        copies = [pltpu.make_async_remote_copy(
            src_ref=p_ref.at[pl.ds(pl.multiple_of((2 * j + 1 - c) * r, 8), r), :], dst_ref=got_ref.at[pl.ds(j * r, r), :],
            send_sem=send_sems.at[j], recv_sem=recv_sems.at[j], device_id=(x, y, 1 - c), device_id_type=MESH)
            for j in range(4)]
        for cp in copies:
            cp.start()
        for cp in copies:
            cp.wait()

    got = pl.pallas_call(
        swap, name=name + "_swap", in_specs=[ANY_SPEC], out_specs=ANY_SPEC,
        out_shape=jax.ShapeDtypeStruct((4 * r, cols), part.dtype),
        scratch_shapes=[pltpu.SemaphoreType.DMA((4,)), pltpu.SemaphoreType.DMA((4,))],
    )(part)

    def add(c_ref, mine_ref, got_ref, o_ref):
        del c_ref
        o_ref[...] = (mine_ref[...].astype(F32) + got_ref[...].astype(F32)).astype(o_ref.dtype)

    core = lax.axis_index("c").astype(jnp.int32).reshape(1)
    return pl.pallas_call(
        add, name=name + "_add",
        grid_spec=pltpu.PrefetchScalarGridSpec(
            num_scalar_prefetch=1, grid=(4,),
            in_specs=[pl.BlockSpec((r, cols), lambda j, c_ref: (2 * j + c_ref[0], 0)),
                      pl.BlockSpec((r, cols), lambda j, c_ref: (j, 0))],
            out_specs=pl.BlockSpec((r, cols), lambda j, c_ref: (j, 0))),
        out_shape=jax.ShapeDtypeStruct((4 * r, cols), part.dtype),
        compiler_params=_params(("arbitrary",)),
    )(core, part, got)


def _sum_partials(parts, name, blocks=N_DEV):
    n = len(parts)

    def body(*refs):
        ins, outs = refs[:n], refs[n:]
        first = pl.program_id(0) == 0
        for i_ref, o_ref in zip(ins, outs):
            v = i_ref[...].astype(F32)

            @pl.when(first)
            def _():
                o_ref[...] = v

            @pl.when(jnp.logical_not(first))
            def _():
                o_ref[...] += v

    shapes = [(p.shape[0] // blocks, p.shape[1]) for p in parts]
    return pl.pallas_call(
        body, name=name, grid=(blocks,),
        in_specs=[pl.BlockSpec(s, lambda j: (j, 0)) for s in shapes],
        out_specs=[pl.BlockSpec(s, lambda j: (0, 0)) for s in shapes],
        out_shape=[jax.ShapeDtypeStruct(s, F32) for s in shapes],
        compiler_params=_params(("arbitrary",)),
    )(*parts)


def _all_reduce_small(slab):
    rows, cols = slab.shape

    def body(x_ref, o_ref, gathered, send_sems, recv_sems):
        x, y, c = _place_on_mesh()
        me = _dev_index(x, y, c)
        peers = _other_devices(x, y, c)

        def copy(f, peer):
            return pltpu.make_async_remote_copy(
                src_ref=x_ref, dst_ref=gathered.at[me], send_sem=send_sems.at[f], recv_sem=recv_sems.at[f],
                device_id=peer, device_id_type=MESH)

        def arrival(f, peer):
            return pltpu.make_async_remote_copy(
                src_ref=x_ref, dst_ref=gathered.at[_dev_index(*peer)], send_sem=send_sems.at[f], recv_sem=recv_sems.at[f],
                device_id=peer, device_id_type=MESH)

        sends = [copy(f, peer) for f, peer in enumerate(peers)]
        for cp in sends:
            cp.start()
        gathered[me] = x_ref[...]
        for f, peer in enumerate(peers):
            arrival(f, peer).wait_recv()
        for cp in sends:
            cp.wait_send()
        total = gathered[0]
        for d in range(1, N_DEV):
            total = total + gathered[d]
        o_ref[...] = total

    return pl.pallas_call(
        body, name="all_reduce_small",
        in_specs=[VMEM_SPEC], out_specs=VMEM_SPEC, out_shape=jax.ShapeDtypeStruct((rows, cols), F32),
        scratch_shapes=[pltpu.VMEM((N_DEV, rows, cols), F32), pltpu.SemaphoreType.DMA((7,)), pltpu.SemaphoreType.DMA((7,))],
    )(slab)


def _adamw_update(w, g, m, v):
    m = ADAM_B1 * m + (1.0 - ADAM_B1) * g
    v = ADAM_B2 * v + (1.0 - ADAM_B2) * (g * g)
    m_hat = m * (1.0 / (1.0 - ADAM_B1 ** ADAM_STEP))
    v_hat = v * (1.0 / (1.0 - ADAM_B2 ** ADAM_STEP))
    return -ADAM_LR * (m_hat / (jnp.sqrt(v_hat) + ADAM_EPS) + ADAM_WD * w), m, v


def _sum_adamw(parts, w, m, v, blocks, name):
    shape = w.shape

    def body(p_ref, w_ref, m_ref, v_ref, g_ref, d_ref, mo_ref, vo_ref):
        j = pl.program_id(0)
        part = p_ref[...].astype(F32)

        @pl.when(j == 0)
        def _():
            g_ref[...] = part

        @pl.when(j > 0)
        def _():
            g_ref[...] += part

        @pl.when(j == blocks - 1)
        def _():
            d_ref[...], mo_ref[...], vo_ref[...] = _adamw_update(w_ref[...], g_ref[...], m_ref[...], v_ref[...])

    held = pl.BlockSpec(shape, lambda j: (0, 0))
    return pl.pallas_call(
        body, name=name, grid=(blocks,),
        in_specs=[pl.BlockSpec(shape, lambda j: (j, 0)), held, held, held],
        out_specs=[held] * 4, out_shape=[jax.ShapeDtypeStruct(shape, F32)] * 4,
        compiler_params=_params(("arbitrary",)),
    )(parts, w, m, v)


def _adamw(ws, gs, ms, vs, name):
    n = len(ws)

    def body(*refs):
        w_r, g_r, m_r, v_r = refs[:n], refs[n:2 * n], refs[2 * n:3 * n], refs[3 * n:4 * n]
        d_o, m_o, v_o = refs[4 * n:5 * n], refs[5 * n:6 * n], refs[6 * n:7 * n]
        for k in range(n):
            d_o[k][...], m_o[k][...], v_o[k][...] = _adamw_update(w_r[k][...], g_r[k][...], m_r[k][...], v_r[k][...])

    shapes = [jax.ShapeDtypeStruct(w.shape, F32) for w in ws]
    outs = pl.pallas_call(
        body, name=name, in_specs=[VMEM_SPEC] * (4 * n), out_specs=[VMEM_SPEC] * (3 * n), out_shape=shapes * 3,
        compiler_params=pltpu.CompilerParams(vmem_limit_bytes=56 << 20),
    )(*ws, *gs, *ms, *vs)
    return outs[:n], outs[n:2 * n], outs[2 * n:]


WEIGHT_NAMES = ("meta_tokens", "ffn1_pre_norm", "ffn1_w_gate", "ffn1_w_up", "ffn1_w_down", "ffn1_post_norm", "mix_pre_norm",
                "w_in", "gla_w_a2", "gla_b_a", "gla_out_norm", "swa_sinks", "swa_out_norm", "w_out", "mix_post_norm",
                "ffn2_pre_norm", "ffn2_w_gate", "ffn2_w_up", "ffn2_w_down", "ffn2_post_norm")
WIN_SHARD = D_IN // N_DEV
WIN_SHARD_PAD = 304
SLAB_VECTORS = ("ffn1_pre", "ffn1_post", "mix_pre", "mix_post", "ffn2_pre", "ffn2_post")
SLAB_ROWS = 32


def kernel(x, meta_tokens, ffn1_pre_norm, ffn1_w_gate, ffn1_w_up, ffn1_w_down, ffn1_post_norm, mix_pre_norm, w_in, gla_w_a2, gla_b_a, gla_out_norm, swa_sinks, swa_out_norm, w_out, mix_post_norm, ffn2_pre_norm, ffn2_w_gate, ffn2_w_up, ffn2_w_down, ffn2_post_norm, loss_target, m_meta_tokens, m_ffn1_pre_norm, m_ffn1_w_gate, m_ffn1_w_up, m_ffn1_w_down, m_ffn1_post_norm, m_mix_pre_norm, m_w_in, m_gla_w_a2, m_gla_b_a, m_gla_out_norm, m_swa_sinks, m_swa_out_norm, m_w_out, m_mix_post_norm, m_ffn2_pre_norm, m_ffn2_w_gate, m_ffn2_w_up, m_ffn2_w_down, m_ffn2_post_norm, v_meta_tokens, v_ffn1_pre_norm, v_ffn1_w_gate, v_ffn1_w_up, v_ffn1_w_down, v_ffn1_post_norm, v_mix_pre_norm, v_w_in, v_gla_w_a2, v_gla_b_a, v_gla_out_norm, v_swa_sinks, v_swa_out_norm, v_w_out, v_mix_post_norm, v_ffn2_pre_norm, v_ffn2_w_gate, v_ffn2_w_up, v_ffn2_w_down, v_ffn2_post_norm):
    given = dict(locals())
    W = {n: given[n] for n in WEIGHT_NAMES}
    M = {n: given["m_" + n] for n in WEIGHT_NAMES}
    V = {n: given["v_" + n] for n in WEIGHT_NAMES}
    dev = _dev_index(*_place_on_mesh())

    def t16(w):
        return w[0].T.astype(BF16)

    small = jnp.concatenate([W["meta_tokens"], jnp.pad(W["gla_w_a2"][0], ((0, 0), (0, 96)))], axis=0)
    wg1, wu1, wd1, small_g, gathered_zeros = _all_gather(
        [t16(W["ffn1_w_gate"]), t16(W["ffn1_w_up"]), W["ffn1_w_down"][0].astype(BF16), small])
    def after_zero(shard, zeros):
        return shard + zeros[0:1, 0:1].astype(shard.dtype)
    win_shard = jnp.pad(t16(W["w_in"]), ((0, WIN_SHARD_PAD - WIN_SHARD), (0, 0)))
    win_shard = after_zero(win_shard, gathered_zeros)
    mid = _exchange_start([win_shard], GATHER, "gather_w_in_start")
    late_shards = [after_zero(W["w_out"][0].astype(BF16), mid[3]), t16(W["ffn2_w_gate"]), t16(W["ffn2_w_up"]),
                   W["ffn2_w_down"][0].astype(BF16)]
    late = _exchange_start(late_shards, GATHER, "gather_late_weights_start")

    def late_weights(what, after):
        if what == "win":
            win_g, = _exchange_wait(mid, GATHER, after, "gather_w_in_wait")
            win_t = win_g.reshape(N_DEV, WIN_SHARD_PAD, D_MODEL)[:, :WIN_SHARD].reshape(D_IN, D_MODEL)
            return dict(win=_win_pad_rows(win_t))
        wout, wg2, wu2, wd2 = _exchange_wait(late, GATHER, after, "gather_late_weights_wait")
        return dict(wout=wout, wg2=wg2, wu2=wu2, wd2=wd2)

    small_g = small_g.reshape(N_DEV, 32, 128)
    meta_full = small_g[:, :N_META].transpose(1, 0, 2).reshape(N_META, D_MODEL)
    wa2_full = small_g[:, N_META:, :32].transpose(1, 0, 2).reshape(16, 256)
    w = dict(
        ffn1_pre=W["ffn1_pre_norm"] + late[3][0, 0], ffn1_post=W["ffn1_post_norm"], mix_pre=W["mix_pre_norm"],
        mix_post=W["mix_post_norm"], ffn2_pre=W["ffn2_pre_norm"], ffn2_post=W["ffn2_post_norm"], b_a=W["gla_b_a"],
        gla_norm=W["gla_out_norm"], sinks=W["swa_sinks"], swa_norm=W["swa_out_norm"], wg1=wg1, wu1=wu1, wd1=wd1,
        wa2=jnp.pad(wa2_full, ((0, 112), (0, 0))))

    in_flight = []

    def on_grads(group, grads):
        parts = []
        for nm, p in grads.items():
            if nm == "win":
                p = _win_unpad_rows(p).reshape(N_DEV, WIN_SHARD, D_MODEL)
                p = jnp.pad(p, ((0, 0), (0, WIN_SHARD_PAD - WIN_SHARD), (0, 0))).reshape(N_DEV * WIN_SHARD_PAD, D_MODEL)
            parts.append(p)
        kind = SCATTER if group == "ffn2" else SCATTER_CHIPS
        if kind == SCATTER_CHIPS:
            parts = [_sibling_reduce(p, "pair_" + group + "_" + nm) for nm, p in zip(grads, parts)]
        started = _exchange_start(parts, kind, "scatter_" + group + "_start")
        in_flight.append((group, list(grads), started, kind))
        return started[3]

    front = jnp.concatenate([jnp.zeros((PAD_ROWS, D_MODEL), F32), meta_full], axis=0)
    loss, dh0, g = _local_step(x[0], loss_target[0], front, w, late_weights, on_grads)
    grad_x = dh0[BLK:][None]

    packed = jnp.concatenate([g["b_a"][0:1], g["gla_norm"][0:1], g["sinks"][0:1], g["swa_norm"][0:1]], axis=1)
    slab = jnp.concatenate([g[k][0:1] for k in SLAB_VECTORS] + [packed, jnp.full((1, D_MODEL), loss, F32),
                           g["wa2"][:16].reshape(4, D_MODEL), jnp.zeros((4, D_MODEL), F32), dh0[PAD_ROWS:BLK]], axis=0)
    tot = _all_reduce_small(slab)
    loss = tot[7, 0]
    small_grads = dict(
        ffn1_pre_norm=tot[0:1], ffn1_post_norm=tot[1:2], mix_pre_norm=tot[2:3], mix_post_norm=tot[3:4],
        ffn2_pre_norm=tot[4:5], ffn2_post_norm=tot[5:6], gla_b_a=tot[6:7, 0:256], gla_out_norm=tot[6:7, 256:384],
        swa_sinks=tot[6:7, 384:392], swa_out_norm=tot[6:7, 512:1024],
        gla_w_a2=lax.dynamic_slice_in_dim(tot[8:12].reshape(16, 256), dev * 32, 32, axis=1)[None],
        meta_tokens=lax.dynamic_slice_in_dim(tot[16:32], dev * 128, 128, axis=1))

    big = dict(wg1=("ffn1_w_gate", True), wu1=("ffn1_w_up", True), wd1=("ffn1_w_down", False), win=("w_in", True),
               wout=("w_out", False), wg2=("ffn2_w_gate", True), wu2=("ffn2_w_up", True), wd2=("ffn2_w_down", False))
    grads = dict(small_grads)
    delta, new_m, new_v = {}, {}, {}
    names = [n for n in WEIGHT_NAMES if n not in [full for full, _ in big.values()]]
    two_d = lambda a: a.reshape(-1, a.shape[-1])
    d_, m_, v_ = _adamw([two_d(W[n]) for n in names], [two_d(grads[n]) for n in names],
                        [two_d(M[n]) for n in names], [two_d(V[n]) for n in names], "adamw_small")
    for k, n in enumerate(names):
        delta[n], new_m[n], new_v[n] = d_[k].reshape(W[n].shape), m_[k].reshape(W[n].shape), v_[k].reshape(W[n].shape)

    before_wait = d_[0] + in_flight[-1][2][3][0, 0]
    for group, shorts, started, kind in in_flight:
        lands = _exchange_wait(started, kind, before_wait, "scatter_" + group + "_wait")
        blocks = 4 if kind == SCATTER_CHIPS else N_DEV
        for short, land in zip(shorts, lands):
            n, transposed = big[short]
            to_slab = (lambda a: a[0].T) if transposed else (lambda a: a[0])
            from_slab = (lambda a: a.T[None]) if transposed else (lambda a: a[None])
            if short == "win":
                g_slab = _sum_partials([land], "sum_" + n, blocks)[0][:WIN_SHARD]
                d_, m_, v_ = _adamw([to_slab(W[n])], [g_slab], [to_slab(M[n])], [to_slab(V[n])], "adamw_" + n)
                d_, m_, v_ = d_[0], m_[0], v_[0]
            else:
                g_slab, d_, m_, v_ = _sum_adamw(land, to_slab(W[n]), to_slab(M[n]), to_slab(V[n]), blocks, "adamw_" + n)
            grads[n], delta[n], new_m[n], new_v[n] = from_slab(g_slab), from_slab(d_), from_slab(m_), from_slab(v_)
            before_wait = d_
    return (loss, grad_x, *[grads[n] for n in WEIGHT_NAMES], *[delta[n] for n in WEIGHT_NAMES],
            *[new_m[n] for n in WEIGHT_NAMES], *[new_v[n] for n in WEIGHT_NAMES])
```

```python
import math

import jax
import jax.numpy as jnp
from jax import lax
from jax.experimental import pallas as pl
from jax.experimental.pallas import tpu as pltpu

F32, BF16 = jnp.float32, jnp.bfloat16

D_MODEL = 1024
D_FF = 2816
N_META = 16
BLK = 128
PAD_ROWS = BLK - N_META
GLA_DK = 64
SWA_HD = 64
SWA_HEADS = 8
GLA_TAU = 16.0
NORM_EPS = 1e-6
NEG_INF = -1e30
ROPE_THETA = 10000.0
P_GQ, P_GK, P_GV, P_GG, P_SQ, P_SK, P_SV, P_GA, P_END = 0, 256, 512, 1024, 1536, 2048, 2176, 2304, 2432
D_IN = 2320
IN_SPLITS = (256, 256, 512, 512, 16, 512, 128, 128)
FF_TILE = 2816
WGRAD_TILE_MAX = 2432
N_DEV = 8
MESH = pl.DeviceIdType.MESH

ADAM_LR, ADAM_B1, ADAM_B2, ADAM_EPS, ADAM_WD, ADAM_STEP = 0.001, 0.9, 0.999, 1e-08, 0.01, 10

V7X_VMEM_BYTES = 64 << 20
VMEM_SPEC = pl.BlockSpec(memory_space=pltpu.VMEM)
SMEM_SPEC = pl.BlockSpec(memory_space=pltpu.SMEM)
ANY_SPEC = pl.BlockSpec(memory_space=pl.ANY)


def _params(semantics, vmem_mb=56):
    return pltpu.CompilerParams(dimension_semantics=semantics, vmem_limit_bytes=vmem_mb << 20)


def _row_tile(rows):
    return 416 if rows % 416 == 0 else BLK


def _nn(a, b):
    return lax.dot_general(a, b, (((1,), (0,)), ((), ())), preferred_element_type=F32)


def _nt(a, b):
    return lax.dot_general(a, b, (((1,), (1,)), ((), ())), preferred_element_type=F32)


def _tn(a, b):
    return lax.dot_general(a, b, (((0,), (0,)), ((), ())), preferred_element_type=F32)


def _rms(x):
    r = lax.rsqrt(jnp.mean(x * x, axis=-1, keepdims=True) + NORM_EPS)
    return x * r, r


def _rms_bwd(xn, r, w, dy):
    g = dy * w
    return r * (g - xn * jnp.mean(g * xn, axis=-1, keepdims=True))


def _sigmoid(x):
    return 1.0 / (1.0 + jnp.exp(-x))


def _colsum(x):
    return jnp.sum(x, axis=0, keepdims=True)


def _split_bf16(x):
    hi = x.astype(BF16)
    lo = (x - hi.astype(F32)).astype(BF16)
    return hi, lo


def _tri(lower):
    r = lax.broadcasted_iota(jnp.int32, (BLK, BLK), 0)
    c = lax.broadcasted_iota(jnp.int32, (BLK, BLK), 1)
    return (r >= c) if lower else (c >= r)


def _half_mask(width, half):
    lane = lax.broadcasted_iota(jnp.int32, (1, width), 1)
    return ((lane % 128) < 64) if half == 0 else ((lane % 128) >= 64)


def _rot_half(x):
    w = x.shape[-1]
    lane = lax.broadcasted_iota(jnp.int32, (1, w), 1)
    return jnp.where((lane % SWA_HD) < SWA_HD // 2, -pltpu.roll(x, w - SWA_HD // 2, 1), pltpu.roll(x, SWA_HD // 2, 1))


def _row_spec(tm, cols):
    return pl.BlockSpec((tm, cols), lambda i: (i, 0))


def _acc_spec(cols):
    return pl.BlockSpec((8, cols), lambda i: (0, 0))


def _acc_add(ref, first, value):
    @pl.when(first)
    def _():
        ref[...] = jnp.zeros_like(ref)
    ref[0:1, :] += value


def _behind_front(ref, i, tm, front):
    blk = ref[...]
    return jnp.where(i == 0, jnp.concatenate([front, blk[0:tm - BLK]], axis=0), blk)


def _ffn_fwd(h, gpre, wg_t, wu_t, wd, gpost, tgt=None, front=None):
    with_loss, with_front = tgt is not None, front is not None
    rows = h.shape[0] + (BLK if with_front else 0)
    tm = _row_tile(rows)
    nf = D_FF // FF_TILE

    def body(*refs):
        refs = list(refs)
        h_ref, gpre_ref, wg_ref, wu_ref, wd_ref, gpost_ref = refs[:6]
        del refs[:6]
        front_ref = refs.pop(0) if with_front else None
        t_ref = refs.pop(0) if with_loss else None
        h0_ref = refs.pop(0) if with_front else None
        ho_ref, a_ref, b_ref, s_ref, f_ref = refs[:5]
        dy_ref, loss_ref = refs[5:7] if with_loss else (None, None)
        acc = refs[-1]
        i = pl.program_id(0)
        if with_front:
            h_in = _behind_front(h_ref, i, tm, front_ref[...])
            h0_ref[...] = h_in
        else:
            h_in = h_ref[...]
        hn, _ = _rms(h_in)
        n16 = (hn * gpre_ref[...]).astype(BF16)
        for j in range(nf):
            cols = slice(j * FF_TILE, (j + 1) * FF_TILE)
            a = _nt(n16, wg_ref[cols, :])
            b = _nt(n16, wu_ref[cols, :])
            a_ref[:, cols] = a.astype(BF16)
            b_ref[:, cols] = b.astype(BF16)
            s16 = (a * _sigmoid(a) * b).astype(BF16)
            s_ref[:, cols] = s16
            part = _nn(s16, wd_ref[cols, :])
            if j == 0:
                acc[...] = part
            else:
                acc[...] += part
        f = acc[...]
        f_ref[...] = f
        fn, _ = _rms(f)
        y = h_in + 0.5 * (fn * gpost_ref[...])
        ho_ref[...] = y
        if with_loss:
            row = i * tm + lax.broadcasted_iota(jnp.int32, (tm, 1), 0)
            err = jnp.where(row >= BLK, y - _behind_front(t_ref, i, tm, jnp.zeros((BLK, D_MODEL), F32)), 0.0)
            dy_ref[...] = err * (1.0 / D_MODEL)
            part = 0.5 * jnp.sum(jnp.sum(err * err, axis=-1, keepdims=True) * (1.0 / D_MODEL), axis=0, keepdims=True)

            @pl.when(i == 0)
            def _():
                loss_ref[...] = jnp.zeros_like(loss_ref)
            loss_ref[...] += part

    row_f32 = _row_spec(tm, D_MODEL)
    behind = pl.BlockSpec((pl.Element(tm), pl.Element(D_MODEL)),
                          lambda i: (pl.multiple_of(jnp.maximum(i * tm - BLK, 0), math.gcd(tm, BLK)), 0))
    in_specs = [behind if with_front else row_f32, VMEM_SPEC, VMEM_SPEC, VMEM_SPEC, VMEM_SPEC, VMEM_SPEC]
    out_specs = [row_f32, _row_spec(tm, D_FF), _row_spec(tm, D_FF), _row_spec(tm, D_FF), row_f32]
    out_shape = [jax.ShapeDtypeStruct((rows, D_MODEL), F32), jax.ShapeDtypeStruct((rows, D_FF), BF16),
                 jax.ShapeDtypeStruct((rows, D_FF), BF16), jax.ShapeDtypeStruct((rows, D_FF), BF16),
                 jax.ShapeDtypeStruct((rows, D_MODEL), F32)]
    args = [h, gpre, wg_t, wu_t, wd, gpost]
    if with_front:
        in_specs.append(VMEM_SPEC)
        args.append(front)
        out_specs.insert(0, row_f32)
        out_shape.insert(0, jax.ShapeDtypeStruct((rows, D_MODEL), F32))
    if with_loss:
        in_specs.append(behind)
        args.append(tgt)
        out_specs += [row_f32, pl.BlockSpec((8, 128), lambda i: (0, 0))]
        out_shape += [jax.ShapeDtypeStruct((rows, D_MODEL), F32), jax.ShapeDtypeStruct((8, 128), F32)]
    return pl.pallas_call(
        body, name="ffn_fwd_loss" if with_loss else "ffn_fwd", grid=(rows // tm,),
        in_specs=in_specs, out_specs=out_specs, out_shape=out_shape,
        scratch_shapes=[pltpu.VMEM((tm, D_MODEL), F32)],
        compiler_params=_params(("arbitrary",)),
    )(*args)


def _ffn_bwd_act(dh_out, h, a, b, f, gpre, gpost, wg_t, wu_t, wd, name):
    rows = h.shape[0]
    tm = _row_tile(rows)
    nf = D_FF // FF_TILE

    def body(dho_ref, h_ref, a_ref, b_ref, f_ref, gpre_ref, gpost_ref, wg_ref, wu_ref, wd_ref,
             dh_ref, da_ref, db_ref, df_ref, n_ref, dgpre_ref, dgpost_ref, acc):
        first = pl.program_id(0) == 0
        dho = dho_ref[...]
        drr = 0.5 * dho
        fn, rf = _rms(f_ref[...])
        _acc_add(dgpost_ref, first, _colsum(drr * fn))
        df16 = _rms_bwd(fn, rf, gpost_ref[...], drr).astype(BF16)
        df_ref[...] = df16
        hn, rh = _rms(h_ref[...])
        n_ref[...] = (hn * gpre_ref[...]).astype(BF16)
        for j in range(nf):
            cols = slice(j * FF_TILE, (j + 1) * FF_TILE)
            ds = _nt(df16, wd_ref[cols, :])
            av = a_ref[:, cols].astype(F32)
            bv = b_ref[:, cols].astype(F32)
            sg = _sigmoid(av)
            db16 = (ds * (av * sg)).astype(BF16)
            da16 = (ds * bv * (sg * (1.0 + av * (1.0 - sg)))).astype(BF16)
            da_ref[:, cols] = da16
            db_ref[:, cols] = db16
            part = _nn(da16, wg_ref[cols, :]) + _nn(db16, wu_ref[cols, :])
            if j == 0:
                acc[...] = part
            else:
                acc[...] += part
        dn = acc[...]
        _acc_add(dgpre_ref, first, _colsum(dn * hn))
        dh_ref[...] = dho + _rms_bwd(hn, rh, gpre_ref[...], dn)

    row_f32 = _row_spec(tm, D_MODEL)
    row_ff = _row_spec(tm, D_FF)
    return pl.pallas_call(
        body, name=name, grid=(rows // tm,),
        in_specs=[row_f32, row_f32, row_ff, row_ff, row_f32, VMEM_SPEC, VMEM_SPEC, VMEM_SPEC, VMEM_SPEC, VMEM_SPEC],
        out_specs=[row_f32, row_ff, row_ff, row_f32, row_f32, _acc_spec(D_MODEL), _acc_spec(D_MODEL)],
        out_shape=[jax.ShapeDtypeStruct((rows, D_MODEL), F32), jax.ShapeDtypeStruct((rows, D_FF), BF16),
                   jax.ShapeDtypeStruct((rows, D_FF), BF16), jax.ShapeDtypeStruct((rows, D_MODEL), BF16),
                   jax.ShapeDtypeStruct((rows, D_MODEL), BF16), jax.ShapeDtypeStruct((8, D_MODEL), F32),
                   jax.ShapeDtypeStruct((8, D_MODEL), F32)],
        scratch_shapes=[pltpu.VMEM((tm, D_MODEL), F32)],
        compiler_params=_params(("arbitrary",), vmem_mb=62),
    )(dh_out, h, a, b, f, gpre, gpost, wg_t, wu_t, wd)


def _wgrad(lhs, rhs, name, after=None):
    rows, width = lhs.shape
    tm = rows if rows % 1664 == 0 else BLK
    tf = 256 if width % 256 == 0 else 128
    nr = rows // tm

    def body(l_ref, r_ref, *rest):
        o_ref, acc = rest[-2:]
        i = pl.program_id(1)
        part = _tn(l_ref[...], r_ref[...])

        @pl.when(i == 0)
        def _():
            acc[...] = part

        @pl.when(i > 0)
        def _():
            acc[...] += part

        @pl.when(i == nr - 1)
        def _():
            o_ref[...] = acc[...].astype(BF16)

    l_spec = pl.BlockSpec((tm, tf), lambda j, i: (i, j))
    r_spec = pl.BlockSpec((tm, D_MODEL), lambda j, i: (i, 0))
    return pl.pallas_call(
        body, name=name, grid=(width // tf, nr),
        in_specs=[l_spec, r_spec] + ([] if after is None else [ANY_SPEC]),
        out_specs=pl.BlockSpec((tf, D_MODEL), lambda j, i: (j, 0)),
        out_shape=jax.ShapeDtypeStruct((width, D_MODEL), BF16),
        scratch_shapes=[pltpu.VMEM((tf, D_MODEL), F32)],
        compiler_params=_params(("arbitrary", "arbitrary")),
    )(lhs, rhs, *([] if after is None else [after]))


def _chunk_cumsum(x, lower):
    tri = jnp.where(_tri(lower), 1.0, 0.0).astype(BF16)
    hi, lo = _split_bf16(x)
    return _nn(tri, hi) + _nn(tri, lo)


def _mix_in(h, g, win_p, wa2_p, b_a, cos, sin):
    rows = h.shape[0]
    tm = 640 if rows % 640 == 0 else BLK

    def body(h_ref, g_ref, win_ref, wa2_ref, ba_ref, cos_ref, sin_ref,
             gq_ref, gk_ref, gv_ref, gg_ref, sq_ref, sk_ref, sv_ref, ga_ref, loga_ref, bc_ref, n_ref):
        hn, _ = _rms(h_ref[...])
        n16 = (hn * g_ref[...]).astype(BF16)
        n_ref[...] = n16
        proj = _nt(n16, win_ref[...])
        gq_ref[...] = proj[:, P_GQ:P_GK]
        gk_ref[...] = proj[:, P_GK:P_GV]
        gv_ref[...] = proj[:, P_GV:P_GG].astype(BF16)
        gg_ref[...] = proj[:, P_GG:P_SQ]
        c1, s1 = cos_ref[...], sin_ref[...]
        c4 = jnp.concatenate([c1, c1, c1, c1], axis=1)
        s4 = jnp.concatenate([s1, s1, s1, s1], axis=1)
        sq = proj[:, P_SQ:P_SK]
        sk = proj[:, P_SK:P_SV]
        sq_ref[...] = (sq * c4 + _rot_half(sq) * s4).astype(BF16)
        sk_ref[...] = (sk * c1 + _rot_half(sk) * s1).astype(BF16)
        sv_ref[...] = proj[:, P_SV:P_GA].astype(BF16)
        ga = proj[:, P_GA:P_END]
        ga_ref[...] = ga
        z = _nn(ga, wa2_ref[...]) + ba_ref[...]
        loga = (jnp.minimum(z, 0.0) - jnp.log(1.0 + jnp.exp(-jnp.abs(z)))) * (1.0 / GLA_TAU)
        loga_ref[...] = loga
        for c in range(tm // BLK):
            rs = slice(c * BLK, (c + 1) * BLK)
            bc_ref[rs, :] = _chunk_cumsum(loga[rs, :], True)

    f32 = lambda c: jax.ShapeDtypeStruct((rows, c), F32)
    b16 = lambda c: jax.ShapeDtypeStruct((rows, c), BF16)
    rs = lambda c: _row_spec(tm, c)
    return pl.pallas_call(
        body, name="mix_in", grid=(rows // tm,),
        in_specs=[rs(D_MODEL), VMEM_SPEC, VMEM_SPEC, VMEM_SPEC, VMEM_SPEC, rs(128), rs(128)],
        out_specs=[rs(256), rs(256), rs(512), rs(512), rs(512), rs(128), rs(128), rs(128), rs(256), rs(256), rs(D_MODEL)],
        out_shape=[f32(256), f32(256), b16(512), f32(512), b16(512), b16(128), b16(128), f32(128), f32(256), f32(256),
                   b16(D_MODEL)],
        compiler_params=_params(("arbitrary",)),
    )(h, g, win_p, wa2_p, b_a, cos, sin)


def _gla_factors(q, k, bc):
    bm = bc[BLK // 2 - 1:BLK // 2, :]
    bl = bc[BLK - 1:BLK, :]
    e_q, e_k, e_qe, e_kd = jnp.exp(bc - bm), jnp.exp(bm - bc), jnp.exp(bc), jnp.exp(bl - bc)
    return (q * e_q, k * e_k, q * e_qe, k * e_kd), (e_q, e_k, e_qe, e_kd), jnp.exp(bl)


def _gla_fwd(gq, gk, gv, gg, bc, wgn):
    rows = gq.shape[0]
    nc = rows // BLK
    scale = GLA_DK ** -0.5

    def body(q_ref, k_ref, v_ref, gg_ref, bc_ref, wgn_ref, o_ref, cat_ref, sp_ref, st):
        @pl.when(pl.program_id(0) == 0)
        def _():
            st[...] = jnp.zeros_like(st)
        low = _tri(True)
        wgn_v = wgn_ref[...]
        for p in range(2):
            sl = slice(128 * p, 128 * p + 128)
            (qt, kt, qe, kd), _, ebl = _gla_factors(q_ref[:, sl] * scale, k_ref[:, sl], bc_ref[:, sl])
            s_prev = st[p]
            sp_ref[0, p] = s_prev
            s16 = s_prev.astype(BF16)
            qt16 = qt.astype(BF16)
            s_new = s_prev * ebl
            for hh in range(2):
                hs = slice(128 * (2 * p + hh), 128 * (2 * p + hh) + 128)
                lm = _half_mask(128, hh)
                vh = v_ref[:, hs]
                pm = jnp.where(low, _nt(qt16, jnp.where(lm, kt, 0.0).astype(BF16)), 0.0)
                o = _nn(pm.astype(BF16), vh) + _nt(jnp.where(lm, qe, 0.0).astype(BF16), s16)
                s_new = s_new + _tn(vh, jnp.where(lm, kd, 0.0).astype(BF16))
                o_ref[:, hs] = o
                on, _ = _rms(o)
                gate = gg_ref[:, hs]
                cat_ref[:, hs] = (on * wgn_v * (gate * _sigmoid(gate))).astype(BF16)
            st[p] = s_new

    rs = lambda c: _row_spec(BLK, c)
    return pl.pallas_call(
        body, name="gla_fwd", grid=(nc,),
        in_specs=[rs(256), rs(256), rs(512), rs(512), rs(256), VMEM_SPEC],
        out_specs=[rs(512), rs(512), pl.BlockSpec((1, 2, 128, 128), lambda i: (i, 0, 0, 0))],
        out_shape=[jax.ShapeDtypeStruct((rows, 512), F32), jax.ShapeDtypeStruct((rows, 512), BF16),
                   jax.ShapeDtypeStruct((nc, 2, 128, 128), F32)],
        scratch_shapes=[pltpu.VMEM((2, 128, 128), F32)],
        compiler_params=_params(("arbitrary",)),
    )(gq, gk, gv, gg, bc, wgn)


def _gla_bwd(dcat, o_all, gq, gk, gv, gg, bc, sp, wgn):
    rows = gq.shape[0]
    nc = rows // BLK
    scale = GLA_DK ** -0.5

    def body(dc_ref, o_ref, q_ref, k_ref, v_ref, gg_ref, bc_ref, sp_ref, wgn_ref,
             dq_ref, dk_ref, dv_ref, dgg_ref, dla_ref, dwgn_ref, dst):
        first = pl.program_id(0) == 0

        @pl.when(first)
        def _():
            dst[...] = jnp.zeros_like(dst)
        low, upp = _tri(True), _tri(False)
        last_row = lax.broadcasted_iota(jnp.int32, (BLK, 1), 0) == BLK - 1
        wgn_v = wgn_ref[...]
        dwgn = jnp.zeros((1, 128), F32)
        for p in range(2):
            sl = slice(128 * p, 128 * p + 128)
            (qt, kt, qe, kd), (e_q, e_k, e_qe, e_kd), ebl = _gla_factors(
                q_ref[:, sl] * scale, k_ref[:, sl], bc_ref[:, sl])
            s_prev = sp_ref[0, p]
            s16 = s_prev.astype(BF16)
            ds_next = dst[p]
            ds16 = ds_next.astype(BF16)
            qt16 = qt.astype(BF16)
            ds_new = ds_next * ebl
            dqt = jnp.zeros((BLK, 128), F32)
            dkt = jnp.zeros((BLK, 128), F32)
            dqe = jnp.zeros((BLK, 128), F32)
            dkd = jnp.zeros((BLK, 128), F32)
            for hh in range(2):
                hs = slice(128 * (2 * p + hh), 128 * (2 * p + hh) + 128)
                lm = _half_mask(128, hh)
                on, ro = _rms(o_ref[:, hs])
                gate = gg_ref[:, hs]
                sg = _sigmoid(gate)
                si = gate * sg
                dog = dc_ref[:, hs]
                dwgn = dwgn + _colsum(dog * si * on)
                dgg_ref[:, hs] = dog * (on * wgn_v) * (sg * (1.0 + gate * (1.0 - sg)))
                do16 = _rms_bwd(on, ro, wgn_v, dog * si).astype(BF16)
                vh = v_ref[:, hs]
                ktm16 = jnp.where(lm, kt, 0.0).astype(BF16)
                qtm16 = jnp.where(lm, qt, 0.0).astype(BF16)
                qem16 = jnp.where(lm, qe, 0.0).astype(BF16)
                kdm16 = jnp.where(lm, kd, 0.0).astype(BF16)
                p_t = jnp.where(upp, _nt(ktm16, qt16), 0.0)
                dp_t = jnp.where(upp, _nt(vh, do16), 0.0)
                dp = jnp.where(low, _nt(do16, vh), 0.0)
                dv_ref[:, hs] = _nn(p_t.astype(BF16), do16) + _nt(kdm16, ds16)
                dqt = dqt + _nn(dp.astype(BF16), ktm16)
                dkt = dkt + _nn(dp_t.astype(BF16), qtm16)
                dqe = dqe + jnp.where(lm, _nn(do16, s16), 0.0)
                dkd = dkd + jnp.where(lm, _nn(vh, ds16), 0.0)
                ds_new = ds_new + _tn(do16, qem16)
            debl = _colsum(ds_next * s_prev)
            dq_ref[:, sl] = (dqt * e_q + dqe * e_qe) * scale
            dk_ref[:, sl] = dkt * e_k + dkd * e_kd
            dkd_kd = dkd * kd
            db = dqt * qt - dkt * kt + dqe * qe - dkd_kd
            db = jnp.where(last_row, db + (_colsum(dkd_kd) + debl * ebl), db)
            dla_ref[:, sl] = _chunk_cumsum(db, False)
            dst[p] = ds_new
        _acc_add(dwgn_ref, first, dwgn)

    rev = lambda c: pl.BlockSpec((BLK, c), lambda i: (nc - 1 - i, 0))
    f32 = lambda c: jax.ShapeDtypeStruct((rows, c), F32)
    return pl.pallas_call(
        body, name="gla_bwd", grid=(nc,),
        in_specs=[rev(512), rev(512), rev(256), rev(256), rev(512), rev(512), rev(256),
                  pl.BlockSpec((1, 2, 128, 128), lambda i: (nc - 1 - i, 0, 0, 0)), VMEM_SPEC],
        out_specs=[rev(256), rev(256), rev(512), rev(512), rev(256), _acc_spec(128)],
        out_shape=[f32(256), f32(256), f32(512), f32(512), f32(256), jax.ShapeDtypeStruct((8, 128), F32)],
        scratch_shapes=[pltpu.VMEM((2, 128, 128), F32)],
        compiler_params=_params(("arbitrary",)),
    )(dcat, o_all, gq, gk, gv, gg, bc, sp, wgn)


def _swa_masks(i):
    t = lax.broadcasted_iota(jnp.int32, (BLK, BLK), 0)
    c = lax.broadcasted_iota(jnp.int32, (BLK, BLK), 1)
    own_side = c <= t
    band_ok = i >= jnp.where(own_side, 1, 2)
    meta_ok = (c % N_META) <= jnp.where(i >= 1, N_META, t - PAD_ROWS)
    return own_side, band_ok, meta_ok, c // N_META


def _swa_blocks(ref, i):
    prev = pl.multiple_of(jnp.maximum(i - 1, 0) * BLK, BLK)
    own = pl.multiple_of(i * BLK, BLK)
    return jnp.concatenate([ref[pl.ds(prev, BLK), :], ref[pl.ds(own, BLK), :]], axis=0), prev, own


def _swa_meta_operand(ref):
    blk = ref[0:BLK, :]
    swapped = pltpu.roll(blk, 64, 1)
    lo = jnp.where(_half_mask(128, 0), blk, swapped)
    hi = jnp.where(_half_mask(128, 1), blk, swapped)
    meta = jnp.concatenate([lo, lo, hi, hi], axis=1)[PAD_ROWS:BLK, :]
    tiled = jnp.concatenate([meta] * SWA_HEADS, axis=0)
    j = lax.broadcasted_iota(jnp.int32, tiled.shape, 0)
    lane = lax.broadcasted_iota(jnp.int32, tiled.shape, 1)
    return jnp.where(j // N_META == lane // SWA_HD, tiled, jnp.zeros_like(tiled))


def _swa_meta_fold(acc):
    out = jnp.zeros((N_META, 128), F32)
    for hd in range(SWA_HEADS):
        half, kv = hd % 2, hd // 4
        piece = acc[N_META * hd:N_META * (hd + 1), 128 * (hd // 2):128 * (hd // 2) + 128]
        piece = jnp.where(_half_mask(128, half), piece, 0.0)
        out = out + (piece if half == kv else pltpu.roll(piece, 64, 1))
    return out


def _by_head(group, per_head):
    out = jnp.zeros((BLK, BLK), F32)
    for hd, v in enumerate(per_head):
        out = jnp.where(group == hd, v, out)
    return out


def _place(x, kv):
    if kv == 0:
        lo = jnp.where(_half_mask(128, 0), x, jnp.zeros_like(x))
        return lo, pltpu.roll(lo, 64, 1)
    hi = jnp.where(_half_mask(128, 1), x, jnp.zeros_like(x))
    return pltpu.roll(hi, 64, 1), hi


def _swa_fwd(sq, sk, sv, sinks, wn):
    rows = sq.shape[0]
    nb = rows // BLK
    scale = SWA_HD ** -0.5

    def body(q_ref, k_ref, v_ref, sink_ref, wn_ref, o_ref, cat_ref, lse_ref, kp, vp):
        i = pl.program_id(0)

        @pl.when(i == 0)
        def _():
            kp[...] = _swa_meta_operand(k_ref)
            vp[...] = _swa_meta_operand(v_ref)
        own_side, band_ok, meta_ok, group = _swa_masks(i)
        k2, _, _ = _swa_blocks(k_ref, i)
        v2, _, _ = _swa_blocks(v_ref, i)
        kz = (_place(k2, 0), _place(k2, 1))
        vz = (_place(v2, 0), _place(v2, 1))
        q_all = q_ref[...]
        s_meta = jnp.where(meta_ok, _nt(q_all, kp[...]) * scale, NEG_INF)
        s_band, m = [], []
        for hd in range(SWA_HEADS):
            kv, half = hd // 4, hd % 2
            q_pair = q_all[:, 128 * (hd // 2):128 * (hd // 2) + 128]
            s2 = _nt(q_pair, kz[kv][half])
            s = jnp.where(band_ok, jnp.where(own_side, s2[:, BLK:], s2[:, :BLK]) * scale, NEG_INF)
            top = jnp.maximum(jnp.max(s, axis=-1, keepdims=True),
                              jnp.max(jnp.where(group == hd, s_meta, NEG_INF), axis=-1, keepdims=True))
            s_band.append(s)
            m.append(jnp.maximum(top, sink_ref[0, hd]))
        e_meta = jnp.exp(s_meta - _by_head(group, m))
        o_meta = _nn(e_meta.astype(BF16), vp[...])
        outs = []
        for pr in range(4):
            o_pair = o_meta[:, 128 * pr:128 * pr + 128]
            rden = []
            for half in range(2):
                hd = 2 * pr + half
                kv = hd // 4
                e = jnp.exp(s_band[hd] - m[hd])
                den = (jnp.sum(e, axis=-1, keepdims=True)
                       + jnp.sum(jnp.where(group == hd, e_meta, 0.0), axis=-1, keepdims=True)
                       + jnp.exp(sink_ref[0, hd] - m[hd]))
                lse_ref[:, hd:hd + 1] = m[hd] + jnp.log(den)
                rden.append(1.0 / den)
                e2 = jnp.concatenate([jnp.where(own_side, 0.0, e), jnp.where(own_side, e, 0.0)], axis=1).astype(BF16)
                o_pair = o_pair + _nn(e2, vz[kv][half])
            outs.append(o_pair * jnp.where(_half_mask(128, 0), rden[0], rden[1]))
        o = jnp.concatenate(outs, axis=1)
        o_ref[...] = o
        on, _ = _rms(o)
        cat_ref[...] = (on * wn_ref[...]).astype(BF16)

    return pl.pallas_call(
        body, name="swa_fwd", grid=(nb,),
        in_specs=[_row_spec(BLK, 512), VMEM_SPEC, VMEM_SPEC, SMEM_SPEC, VMEM_SPEC],
        out_specs=[_row_spec(BLK, 512), _row_spec(BLK, 512), _row_spec(BLK, SWA_HEADS)],
        out_shape=[jax.ShapeDtypeStruct((rows, 512), F32), jax.ShapeDtypeStruct((rows, 512), BF16),
                   jax.ShapeDtypeStruct((rows, SWA_HEADS), F32)],
        scratch_shapes=[pltpu.VMEM((BLK, 512), BF16), pltpu.VMEM((BLK, 512), BF16)],
        compiler_params=_params(("arbitrary",)),
    )(sq, sk, sv, sinks, wn)


def _swa_bwd(dcat, o_all, sq, sk, sv, lse, sinks, wn):
    rows = sq.shape[0]
    nb = rows // BLK
    scale = SWA_HD ** -0.5

    def body(dc_ref, o_ref, q_ref, k_ref, v_ref, lse_ref, sink_ref, wn_ref, dq_ref, dk_ref, dv_ref, dsink_ref, dwn_ref,
             kp, vp, dkp, dvp):
        i = pl.program_id(0)
        first = i == 0

        @pl.when(first)
        def _():
            dk_ref[...] = jnp.zeros_like(dk_ref)
            dv_ref[...] = jnp.zeros_like(dv_ref)
            dkp[...] = jnp.zeros_like(dkp)
            dvp[...] = jnp.zeros_like(dvp)
            kp[...] = _swa_meta_operand(k_ref)
            vp[...] = _swa_meta_operand(v_ref)
        own_side, band_ok, meta_ok, group = _swa_masks(i)
        k2, prev, own = _swa_blocks(k_ref, i)
        v2, _, _ = _swa_blocks(v_ref, i)
        kz = (_place(k2, 0), _place(k2, 1))
        vz = (_place(v2, 0), _place(v2, 1))
        o = o_ref[...]
        on, ro = _rms(o)
        dc = dc_ref[...]
        _acc_add(dwn_ref, first, _colsum(dc * on))
        do = _rms_bwd(on, ro, wn_ref[...], dc)
        do_o = do * o
        do16 = do.astype(BF16)
        q_all = q_ref[...]
        lse = [lse_ref[:, hd:hd + 1] for hd in range(SWA_HEADS)]
        delta = [jnp.sum(jnp.where(_half_mask(128, hd % 2), do_o[:, 128 * (hd // 2):128 * (hd // 2) + 128], 0.0),
                         axis=-1, keepdims=True) for hd in range(SWA_HEADS)]
        s_meta = jnp.where(meta_ok, _nt(q_all, kp[...]) * scale, NEG_INF)
        p_meta = jnp.exp(s_meta - _by_head(group, lse))
        ds_meta16 = (p_meta * (_nt(do16, vp[...]) - _by_head(group, delta)) * scale).astype(BF16)
        dq_meta = _nn(ds_meta16, kp[...])
        dkp[...] += _tn(ds_meta16, q_all)
        dvp[...] += _tn(p_meta.astype(BF16), do16)
        own2 = jnp.concatenate([own_side.astype(jnp.int32)] * 2, axis=0) > 0
        ok2 = jnp.concatenate([band_ok.astype(jnp.int32)] * 2, axis=0) > 0

        def window(x2):
            return jnp.where(own2, x2[:, BLK:], x2[:, :BLK])

        def unwindow(x):
            return jnp.concatenate([jnp.where(own2, 0.0, x), jnp.where(own2, x, 0.0)], axis=1).astype(BF16)
        lane8 = lax.broadcasted_iota(jnp.int32, (1, 128), 1)
        dsink = jnp.zeros((1, 128), F32)
        dq_pairs = [dq_meta[:, 128 * pr:128 * pr + 128] for pr in range(4)]
        dk2 = [[None, None], [None, None]]
        dv2 = [[None, None], [None, None]]
        for kv in range(2):
            for half in range(2):
                heads, pairs = (4 * kv + half, 4 * kv + 2 + half), (2 * kv, 2 * kv + 1)
                q_s = jnp.concatenate([q_all[:, 128 * pr:128 * pr + 128] for pr in pairs], axis=0)
                do_s = jnp.concatenate([do16[:, 128 * pr:128 * pr + 128] for pr in pairs], axis=0)
                lse_s = jnp.concatenate([lse[hd] for hd in heads], axis=0)
                delta_s = jnp.concatenate([delta[hd] for hd in heads], axis=0)
                s = jnp.where(ok2, window(_nt(q_s, kz[kv][half])) * scale, NEG_INF)
                prob = jnp.exp(s - lse_s)
                for hd in heads:
                    dsink = dsink + jnp.where(lane8 == hd, -jnp.sum(jnp.exp(sink_ref[0, hd] - lse[hd]) * delta[hd]), 0.0)
                ds2 = unwindow(prob * (window(_nt(do_s, vz[kv][half])) - delta_s) * scale)
                dq_s = _nn(ds2, kz[kv][half])
                dq_pairs[pairs[0]] = dq_pairs[pairs[0]] + dq_s[:BLK]
                dq_pairs[pairs[1]] = dq_pairs[pairs[1]] + dq_s[BLK:]
                dk2[kv][half] = _tn(ds2, q_s)
                dv2[kv][half] = _tn(unwindow(prob), do_s)
        dq_ref[...] = jnp.concatenate(dq_pairs, axis=1)
        _acc_add(dsink_ref, first, dsink)
        for ref, acc2 in ((dk_ref, dk2), (dv_ref, dv2)):
            tot = jnp.zeros((2 * BLK, 128), F32)
            for kv in range(2):
                for half in range(2):
                    part = jnp.where(_half_mask(128, half), acc2[kv][half], 0.0)
                    tot = tot + (part if half == kv else pltpu.roll(part, 64, 1))
            ref[pl.ds(prev, BLK), :] += tot[:BLK]
            ref[pl.ds(own, BLK), :] += tot[BLK:]

        @pl.when(i == nb - 1)
        def _():
            dk_ref[PAD_ROWS:BLK, :] += _swa_meta_fold(dkp[...])
            dv_ref[PAD_ROWS:BLK, :] += _swa_meta_fold(dvp[...])

    full = pl.BlockSpec((rows, 128), lambda i: (0, 0))
    return pl.pallas_call(
        body, name="swa_bwd", grid=(nb,),
        in_specs=[_row_spec(BLK, 512), _row_spec(BLK, 512), _row_spec(BLK, 512), VMEM_SPEC, VMEM_SPEC,
                  _row_spec(BLK, SWA_HEADS), SMEM_SPEC, VMEM_SPEC],
        out_specs=[_row_spec(BLK, 512), full, full, _acc_spec(128), _acc_spec(512)],
        out_shape=[jax.ShapeDtypeStruct((rows, 512), F32), jax.ShapeDtypeStruct((rows, 128), F32),
                   jax.ShapeDtypeStruct((rows, 128), F32), jax.ShapeDtypeStruct((8, 128), F32),
                   jax.ShapeDtypeStruct((8, 512), F32)],
        scratch_shapes=[pltpu.VMEM((BLK, 512), BF16), pltpu.VMEM((BLK, 512), BF16),
                        pltpu.VMEM((BLK, 512), F32), pltpu.VMEM((BLK, 512), F32)],
        compiler_params=_params(("arbitrary",)),
    )(dcat, o_all, sq, sk, sv, lse, sinks, wn)


def _mix_out(h, cat_g, cat_s, wout, gpost):
    rows = h.shape[0]
    tm = _row_tile(rows)

    def body(h_ref, cg_ref, cs_ref, w_ref, g_ref, ho_ref, m_ref):
        m = _nn(cg_ref[...], w_ref[0:512, :]) + _nn(cs_ref[...], w_ref[512:1024, :])
        m_ref[...] = m
        mn, _ = _rms(m)
        ho_ref[...] = h_ref[...] + mn * g_ref[...]

    row_f32 = _row_spec(tm, D_MODEL)
    return pl.pallas_call(
        body, name="mix_out", grid=(rows // tm,),
        in_specs=[row_f32, _row_spec(tm, 512), _row_spec(tm, 512), VMEM_SPEC, VMEM_SPEC],
        out_specs=[row_f32, row_f32],
        out_shape=[jax.ShapeDtypeStruct((rows, D_MODEL), F32), jax.ShapeDtypeStruct((rows, D_MODEL), F32)],
        compiler_params=_params(("arbitrary",)),
    )(h, cat_g, cat_s, wout, gpost)


def _mix_out_bwd(dh, m, wout, gpost):
    rows = dh.shape[0]
    tm = _row_tile(rows)

    def body(dh_ref, m_ref, w_ref, g_ref, dcg_ref, dcs_ref, dm_ref, dg_ref):
        first = pl.program_id(0) == 0
        dhv = dh_ref[...]
        mn, rm = _rms(m_ref[...])
        _acc_add(dg_ref, first, _colsum(dhv * mn))
        dm16 = _rms_bwd(mn, rm, g_ref[...], dhv).astype(BF16)
        dm_ref[...] = dm16
        dcat = _nt(dm16, w_ref[...])
        dcg_ref[...] = dcat[:, 0:512]
        dcs_ref[...] = dcat[:, 512:1024]

    row_f32 = _row_spec(tm, D_MODEL)
    return pl.pallas_call(
        body, name="mix_out_bwd", grid=(rows // tm,),
        in_specs=[row_f32, row_f32, VMEM_SPEC, VMEM_SPEC],
        out_specs=[_row_spec(tm, 512), _row_spec(tm, 512), row_f32, _acc_spec(D_MODEL)],
        out_shape=[jax.ShapeDtypeStruct((rows, 512), F32), jax.ShapeDtypeStruct((rows, 512), F32),
                   jax.ShapeDtypeStruct((rows, D_MODEL), BF16), jax.ShapeDtypeStruct((8, D_MODEL), F32)],
        compiler_params=_params(("arbitrary",)),
    )(dh, m, wout, gpost)


def _mix_in_bwd(dh_out, h, g, win_p, wa2_p, cos, sin, loga, ga, dgq, dgk, dgv, dgg, dsq, dsk, dsv, dloga):
    rows = h.shape[0]
    tm = _row_tile(rows)

    def body(dho_ref, h_ref, g_ref, win_ref, wa2_ref, cos_ref, sin_ref, loga_ref, ga_ref,
             dgq_ref, dgk_ref, dgv_ref, dgg_ref, dsq_ref, dsk_ref, dsv_ref, dla_ref,
             dh_ref, dproj_ref, dwa2_ref, dg_ref, dba_ref):
        first = pl.program_id(0) == 0
        dz = dla_ref[...] * (1.0 / GLA_TAU) * (1.0 - jnp.exp(GLA_TAU * loga_ref[...]))
        _acc_add(dba_ref, first, _colsum(dz))
        dga = _nt(dz, wa2_ref[...])
        pa = _tn(ga_ref[...], dz)
        c1, s1 = cos_ref[...], sin_ref[...]
        c4 = jnp.concatenate([c1, c1, c1, c1], axis=1)
        s4 = jnp.concatenate([s1, s1, s1, s1], axis=1)
        dq_r, dk_r = dsq_ref[...], dsk_ref[...]
        dsq = dq_r * c4 - _rot_half(dq_r * s4)
        dsk = dk_r * c1 - _rot_half(dk_r * s1)
        dproj16 = jnp.concatenate(
            [dgq_ref[...], dgk_ref[...], dgv_ref[...], dgg_ref[...], dsq, dsk, dsv_ref[...], dga], axis=1).astype(BF16)
        dproj_ref[...] = dproj16
        dn = _nn(dproj16, win_ref[...])

        @pl.when(first)
        def _():
            dwa2_ref[...] = pa

        @pl.when(jnp.logical_not(first))
        def _():
            dwa2_ref[...] += pa
        hn, rh = _rms(h_ref[...])
        _acc_add(dg_ref, first, _colsum(dn * hn))
        dh_ref[...] = dho_ref[...] + _rms_bwd(hn, rh, g_ref[...], dn)

    rs = lambda c: _row_spec(tm, c)
    return pl.pallas_call(
        body, name="mix_in_bwd", grid=(rows // tm,),
        in_specs=[rs(D_MODEL), rs(D_MODEL), VMEM_SPEC, VMEM_SPEC, VMEM_SPEC, rs(128), rs(128), rs(256), rs(128),
                  rs(256), rs(256), rs(512), rs(512), rs(512), rs(128), rs(128), rs(256)],
        out_specs=[rs(D_MODEL), rs(P_END), pl.BlockSpec((128, 256), lambda i: (0, 0)), _acc_spec(D_MODEL), _acc_spec(256)],
        out_shape=[jax.ShapeDtypeStruct((rows, D_MODEL), F32), jax.ShapeDtypeStruct((rows, P_END), BF16),
                   jax.ShapeDtypeStruct((128, 256), F32), jax.ShapeDtypeStruct((8, D_MODEL), F32),
                   jax.ShapeDtypeStruct((8, 256), F32)],
        compiler_params=_params(("arbitrary",)),
    )(dh_out, h, g, win_p, wa2_p, cos, sin, loga, ga, dgq, dgk, dgv, dgg, dsq, dsk, dsv, dloga)


def _rope_tables(rows):
    pos = (jnp.arange(rows, dtype=jnp.int32) - PAD_ROWS).astype(F32)
    inv_freq = 1.0 / (ROPE_THETA ** (jnp.arange(0, SWA_HD, 2, dtype=F32) / SWA_HD))
    ang = pos[:, None] * inv_freq[None, :]
    ang = jnp.concatenate([ang, ang, ang, ang], axis=-1)
    return jnp.cos(ang), jnp.sin(ang)


def _local_step(x, tgt, front, w, late_weights=None, on_grads=None):
    cos, sin = _rope_tables(x.shape[0] + BLK)
    g = {}

    def tell(group, names):
        for nm in names:
            g[nm] = grads_now[nm]
        return None if on_grads is None else on_grads(group, {nm: grads_now[nm] for nm in names})

    h0, h1, a1, b1, s1, f1 = _ffn_fwd(x, w["ffn1_pre"], w["wg1"], w["wu1"], w["wd1"], w["ffn1_post"], front=front)
    if late_weights is not None:
        w = {**w, **late_weights("win", f1)}
    gq, gk, gv, gg, sq, sk, sv, ga, loga, bc, n2 = _mix_in(h1, w["mix_pre"], w["win"], w["wa2"], w["b_a"], cos, sin)
    o_g, cat_g, sp = _gla_fwd(gq, gk, gv, gg, bc, w["gla_norm"])
    o_s, cat_s, lse = _swa_fwd(sq, sk, sv, w["sinks"], w["swa_norm"])
    if late_weights is not None:
        w = {**w, **late_weights("rest", lse)}
    h2, m = _mix_out(h1, cat_g, cat_s, w["wout"], w["mix_post"])
    h3, a2, b2, s2, f2, dy, loss = _ffn_fwd(h2, w["ffn2_pre"], w["wg2"], w["wu2"], w["wd2"], w["ffn2_post"], tgt)
    del h3
    dh2, da, db, df, n3, g["ffn2_pre"], g["ffn2_post"] = _ffn_bwd_act(
        dy, h2, a2, b2, f2, w["ffn2_pre"], w["ffn2_post"], w["wg2"], w["wu2"], w["wd2"], "ffn2_bwd_act")
    grads_now = dict(wd2=_wgrad(s2, df, "ffn2_wgrad_down"), wg2=_wgrad(da, n3, "ffn2_wgrad_gate"),
                     wu2=_wgrad(db, n3, "ffn2_wgrad_up"))
    tok = tell("ffn2", ("wd2", "wg2", "wu2"))
    dcg, dcs, dm, g["mix_post"] = _mix_out_bwd(dh2, m, w["wout"], w["mix_post"] + (0.0 if tok is None else tok[0, 0]))
    dsq, dsk, dsv, g["sinks"], g["swa_norm"] = _swa_bwd(dcs, o_s, sq, sk, sv, lse, w["sinks"], w["swa_norm"])
    dgq, dgk, dgv, dgg, dloga, g["gla_norm"] = _gla_bwd(dcg, o_g, gq, gk, gv, gg, bc, sp, w["gla_norm"])
    dh1, dproj, g["wa2"], g["mix_pre"], g["b_a"] = _mix_in_bwd(
        dh2, h1, w["mix_pre"], w["win"], w["wa2"], cos, sin, loga, ga, dgq, dgk, dgv, dgg, dsq, dsk, dsv, dloga)
    dh0, da, db, df, n1, g["ffn1_pre"], g["ffn1_post"] = _ffn_bwd_act(
        dh1, h0, a1, b1, f1, w["ffn1_pre"], w["ffn1_post"], w["wg1"], w["wu1"], w["wd1"], "ffn1_bwd_act")
    grads_now = dict(wd1=_wgrad(s1, df, "ffn1_wgrad_down"))
    tok = tell("ffn1_down", ("wd1",))
    grads_now = dict(wg1=_wgrad(da, n1, "ffn1_wgrad_gate", after=tok))
    tok = tell("ffn1_gate", ("wg1",))
    grads_now = dict(wu1=_wgrad(db, n1, "ffn1_wgrad_up", after=tok))
    tok = tell("ffn1_up", ("wu1",))
    grads_now = dict(win=_wgrad(dproj, n2, "win_wgrad", after=tok),
                     wout=jnp.concatenate([_wgrad(cat_g, dm, "wout_wgrad_gla", after=tok),
                                           _wgrad(cat_s, dm, "wout_wgrad_swa", after=tok)], axis=0))
    tell("mix", ("wout", "win"))
    return loss[0, 0], dh0, g


def _win_pad_rows(win_t):
    pad = jnp.zeros((P_END - P_GA - 16, win_t.shape[1]), win_t.dtype)
    return jnp.concatenate([win_t[0:1536], win_t[1552:2320], win_t[1536:1552], pad], axis=0)


def _win_unpad_rows(win_p):
    return jnp.concatenate([win_p[0:1536], win_p[P_GA:P_GA + 16], win_p[1536:P_GA]], axis=0)


def _place_on_mesh():
    return lax.axis_index("x"), lax.axis_index("y"), lax.axis_index("c")


def _dev_index(px, py, pc):
    return 4 * px + 2 * py + pc


def _other_devices(x, y, c):
    flip = lambda v, f: 1 - v if f else v
    return [(flip(x, fx), flip(y, fy), flip(c, fc)) for fx in (0, 1) for fy in (0, 1) for fc in (0, 1)][1:]


def _all_gather(shards):
    n = len(shards)

    def body(*refs):
        ins, outs = refs[:n], refs[n:2 * n]
        zeros_ref, send_sems, recv_sems, local_sems = refs[2 * n:]
        zeros_ref[...] = jnp.zeros_like(zeros_ref)
        x, y, c = _place_on_mesh()
        me, sibling = (x, y, c), (x, y, 1 - c)
        chips = [(1 - x, y), (x, 1 - y), (1 - x, 1 - y)]

        def rows(k, px, py, pc):
            r = ins[k].shape[0]
            return outs[k].at[pl.ds(pl.multiple_of(_dev_index(px, py, pc) * r, 8), r), :]

        def copy(k, slot, block, to, src=None):
            return pltpu.make_async_remote_copy(
                src_ref=rows(k, *block) if src is None else src, dst_ref=rows(k, *block),
                send_sem=send_sems.at[k, slot], recv_sem=recv_sems.at[k, slot], device_id=to, device_id_type=MESH)

        local = [pltpu.make_async_copy(ins[k], rows(k, *me), local_sems.at[k]) for k in range(n)]
        sends = []
        for k in range(n):
            local[k].start()
            sends.append(copy(k, 0, me, sibling, src=ins[k]))
            sends += [copy(k, 1 + j, me, (*chip, c), src=ins[k]) for j, chip in enumerate(chips)]
        for cp in sends:
            cp.start()
        for k in range(n):
            for j, chip in enumerate(chips):
                copy(k, 1 + j, (*chip, c), me).wait_recv()
                passed = copy(k, 4 + j, (*chip, c), sibling)
                passed.start()
                sends.append(passed)
        for k in range(n):
            copy(k, 0, sibling, me).wait_recv()
            for j, chip in enumerate(chips):
                copy(k, 4 + j, (*chip, 1 - c), me).wait_recv()
        for cp in sends:
            cp.wait_send()
        for cp in local:
            cp.wait()

    return pl.pallas_call(
        body, name="all_gather_weights",
        in_specs=[ANY_SPEC] * n, out_specs=[ANY_SPEC] * n + [VMEM_SPEC],
        out_shape=[jax.ShapeDtypeStruct((N_DEV * s.shape[0], s.shape[1]), s.dtype) for s in shards]
        + [jax.ShapeDtypeStruct((8, 128), F32)],
        scratch_shapes=[pltpu.SemaphoreType.DMA((n, 7)), pltpu.SemaphoreType.DMA((n, 7)), pltpu.SemaphoreType.DMA((n,))],
    )(*shards)


HBM_SPEC = pl.BlockSpec(memory_space=pltpu.HBM)
SEM_SPEC = pl.BlockSpec(memory_space=pltpu.SEMAPHORE)
DATAFLOW = pltpu.SideEffectType.DATAFLOW_SIDE_EFFECTING


GATHER, SCATTER, SCATTER_CHIPS = "gather", "scatter", "scatter among chips"


def _exchange_peers(kind):
    x, y, c = _place_on_mesh()
    if kind == SCATTER_CHIPS:
        peers = [(1 - x, y, c), (x, 1 - y, c), (1 - x, 1 - y, c)]
        return peers, [2 * p[0] + p[1] for p in peers], 2 * x + y, 4
    peers = _other_devices(x, y, c)
    return peers, [_dev_index(*p) for p in peers], _dev_index(x, y, c), N_DEV


def _exchange_copies(srcs, lands, send_sems, recv_sems, own_sems, kind, arriving):
    peers, theirs, me, blocks = _exchange_peers(kind)
    remote, local = [], []
    for k, (src, land) in enumerate(zip(srcs, lands)):
        r = land.shape[0] // blocks

        def block(ref, d):
            return ref.at[pl.ds(pl.multiple_of(d * r, 8), r), :]

        for f, (peer, him) in enumerate(zip(peers, theirs)):
            mine, his = (him, me) if arriving else (me, him)
            sem = len(peers) * k + f
            remote.append(pltpu.make_async_remote_copy(
                src_ref=src if kind == GATHER else block(src, his), dst_ref=block(land, mine),
                send_sem=send_sems.at[sem], recv_sem=recv_sems.at[sem], device_id=peer, device_id_type=MESH))
        local.append(pltpu.make_async_copy(src if kind == GATHER else block(src, me), block(land, me), own_sems.at[k]))
    return remote, local


def _exchange_start(srcs, kind, name):
    n = len(srcs)
    lands = [lax.empty((N_DEV * s.shape[0], s.shape[1]) if kind == GATHER else s.shape, s.dtype) for s in srcs]
    sems = (3 if kind == SCATTER_CHIPS else 7) * n

    def body(*refs):
        remote, local = _exchange_copies(refs[:n], refs[n:2 * n], *refs[2 * n:2 * n + 3], kind, False)
        for cp in remote + local:
            cp.start()
        refs[-1][...] = jnp.zeros_like(refs[-1])

    both = list(srcs) + list(lands)
    outs = pl.pallas_call(
        body, name=name,
        out_shape=(pltpu.SemaphoreType.DMA((sems,)), pltpu.SemaphoreType.DMA((sems,)), pltpu.SemaphoreType.DMA((n,)),
                   *[pltpu.HBM(a.shape, a.dtype) for a in both], jax.ShapeDtypeStruct((8, 128), F32)),
        in_specs=[HBM_SPEC] * (2 * n), out_specs=(SEM_SPEC, SEM_SPEC, SEM_SPEC, *[HBM_SPEC] * (2 * n), VMEM_SPEC),
        input_output_aliases={i: 3 + i for i in range(2 * n)},
        compiler_params=pltpu.CompilerParams(has_side_effects=DATAFLOW),
    )(*[pltpu.with_memory_space_constraint(a, pltpu.HBM) for a in both])
    return outs[0:3], outs[3:3 + n], outs[3 + n:3 + 2 * n], outs[-1]


def _exchange_wait(started, kind, after, name):
    sems, srcs, lands, _ = started
    n = len(srcs)

    def body(*refs):
        args = (refs[:n], refs[n:2 * n], *refs[2 * n:2 * n + 3], kind)
        going, local = _exchange_copies(*args, False)
        for cp in going:
            cp.wait_send()
        for cp in local:
            cp.wait()
        for cp in _exchange_copies(*args, True)[0]:
            cp.wait_recv()

    both = list(srcs) + list(lands)
    outs = pl.pallas_call(
        body, name=name, out_shape=[pltpu.HBM(a.shape, a.dtype) for a in both],
        in_specs=[HBM_SPEC] * (2 * n) + [SEM_SPEC, SEM_SPEC, SEM_SPEC, ANY_SPEC], out_specs=[HBM_SPEC] * (2 * n),
        input_output_aliases={i: i for i in range(2 * n)},
        compiler_params=pltpu.CompilerParams(has_side_effects=DATAFLOW),
    )(*both, *sems, after)
    return outs[n:]


def _sibling_reduce(part, name):
    r, cols = part.shape[0] // N_DEV, part.shape[1]

    def swap(p_ref, got_ref, send_sems, recv_sems):
        x, y, c = _place_on_mesh()
        copies = [pltpu.make_async_remote_copy(
            src_ref=p_ref.at[pl.ds(pl.multiple_of((2 * j + 1 - c) * r, 8), r), :], dst_ref=got_ref.at[pl.ds(j * r, r), :],
            send_sem=send_sems.at[j], recv_sem=recv_sems.at[j], device_id=(x, y, 1 - c), device_id_type=MESH)
            for j in range(4)]
        for cp in copies:
            cp.start()
        for cp in copies:
            cp.wait()

    got = pl.pallas_call(
        swap, name=name + "_swap", in_specs=[ANY_SPEC], out_specs=ANY_SPEC,
        out_shape=jax.ShapeDtypeStruct((4 * r, cols), part.dtype),
        scratch_shapes=[pltpu.SemaphoreType.DMA((4,)), pltpu.SemaphoreType.DMA((4,))],
    )(part)

    def add(c_ref, mine_ref, got_ref, o_ref):
        del c_ref
        o_ref[...] = (mine_ref[...].astype(F32) + got_ref[...].astype(F32)).astype(o_ref.dtype)

    core = lax.axis_index("c").astype(jnp.int32).reshape(1)
    return pl.pallas_call(
        add, name=name + "_add",
        grid_spec=pltpu.PrefetchScalarGridSpec(
            num_scalar_prefetch=1, grid=(4,),
            in_specs=[pl.BlockSpec((r, cols), lambda j, c_ref: (2 * j + c_ref[0], 0)),
                      pl.BlockSpec((r, cols), lambda j, c_ref: (j, 0))],
            out_specs=pl.BlockSpec((r, cols), lambda j, c_ref: (j, 0))),
        out_shape=jax.ShapeDtypeStruct((4 * r, cols), part.dtype),
        compiler_params=_params(("arbitrary",)),
    )(core, part, got)


def _sum_partials(parts, name, blocks=N_DEV):
    n = len(parts)

    def body(*refs):
        ins, outs = refs[:n], refs[n:]
        first = pl.program_id(0) == 0
        for i_ref, o_ref in zip(ins, outs):
            v = i_ref[...].astype(F32)

            @pl.when(first)
            def _():
                o_ref[...] = v

            @pl.when(jnp.logical_not(first))
            def _():
                o_ref[...] += v

    shapes = [(p.shape[0] // blocks, p.shape[1]) for p in parts]
    return pl.pallas_call(
        body, name=name, grid=(blocks,),
        in_specs=[pl.BlockSpec(s, lambda j: (j, 0)) for s in shapes],
        out_specs=[pl.BlockSpec(s, lambda j: (0, 0)) for s in shapes],
        out_shape=[jax.ShapeDtypeStruct(s, F32) for s in shapes],
        compiler_params=_params(("arbitrary",)),
    )(*parts)


def _all_reduce_small(slab):
    rows, cols = slab.shape

    def body(x_ref, o_ref, gathered, send_sems, recv_sems):
        x, y, c = _place_on_mesh()
        me = _dev_index(x, y, c)
        peers = _other_devices(x, y, c)

        def copy(f, peer):
            return pltpu.make_async_remote_copy(
                src_ref=x_ref, dst_ref=gathered.at[me], send_sem=send_sems.at[f], recv_sem=recv_sems.at[f],
                device_id=peer, device_id_type=MESH)

        def arrival(f, peer):
            return pltpu.make_async_remote_copy(
                src_ref=x_ref, dst_ref=gathered.at[_dev_index(*peer)], send_sem=send_sems.at[f], recv_sem=recv_sems.at[f],
                device_id=peer, device_id_type=MESH)

        sends = [copy(f, peer) for f, peer in enumerate(peers)]
        for cp in sends:
            cp.start()
        gathered[me] = x_ref[...]
        for f, peer in enumerate(peers):
            arrival(f, peer).wait_recv()
        for cp in sends:
            cp.wait_send()
        total = gathered[0]
        for d in range(1, N_DEV):
            total = total + gathered[d]
        o_ref[...] = total

    return pl.pallas_call(
        body, name="all_reduce_small",
        in_specs=[VMEM_SPEC], out_specs=VMEM_SPEC, out_shape=jax.ShapeDtypeStruct((rows, cols), F32),
        scratch_shapes=[pltpu.VMEM((N_DEV, rows, cols), F32), pltpu.SemaphoreType.DMA((7,)), pltpu.SemaphoreType.DMA((7,))],
    )(slab)


def _adamw_update(w, g, m, v):
    m = ADAM_B1 * m + (1.0 - ADAM_B1) * g
    v = ADAM_B2 * v + (1.0 - ADAM_B2) * (g * g)
    m_hat = m * (1.0 / (1.0 - ADAM_B1 ** ADAM_STEP))
    v_hat = v * (1.0 / (1.0 - ADAM_B2 ** ADAM_STEP))
    return -ADAM_LR * (m_hat / (jnp.sqrt(v_hat) + ADAM_EPS) + ADAM_WD * w), m, v


def _sum_adamw(parts, w, m, v, blocks, name):
    shape = w.shape

    def body(p_ref, w_ref, m_ref, v_ref, g_ref, d_ref, mo_ref, vo_ref):
        j = pl.program_id(0)
        part = p_ref[...].astype(F32)

        @pl.when(j == 0)
        def _():
            g_ref[...] = part

        @pl.when(j > 0)
        def _():
            g_ref[...] += part

        @pl.when(j == blocks - 1)
        def _():
            d_ref[...], mo_ref[...], vo_ref[...] = _adamw_update(w_ref[...], g_ref[...], m_ref[...], v_ref[...])

    held = pl.BlockSpec(shape, lambda j: (0, 0))
    return pl.pallas_call(
        body, name=name, grid=(blocks,),
        in_specs=[pl.BlockSpec(shape, lambda j: (j, 0)), held, held, held],
        out_specs=[held] * 4, out_shape=[jax.ShapeDtypeStruct(shape, F32)] * 4,
        compiler_params=_params(("arbitrary",)),
    )(parts, w, m, v)


def _adamw(ws, gs, ms, vs, name):
    n = len(ws)

    def body(*refs):
        w_r, g_r, m_r, v_r = refs[:n], refs[n:2 * n], refs[2 * n:3 * n], refs[3 * n:4 * n]
        d_o, m_o, v_o = refs[4 * n:5 * n], refs[5 * n:6 * n], refs[6 * n:7 * n]
        for k in range(n):
            d_o[k][...], m_o[k][...], v_o[k][...] = _adamw_update(w_r[k][...], g_r[k][...], m_r[k][...], v_r[k][...])

    shapes = [jax.ShapeDtypeStruct(w.shape, F32) for w in ws]
    outs = pl.pallas_call(
        body, name=name, in_specs=[VMEM_SPEC] * (4 * n), out_specs=[VMEM_SPEC] * (3 * n), out_shape=shapes * 3,
        compiler_params=pltpu.CompilerParams(vmem_limit_bytes=56 << 20),
    )(*ws, *gs, *ms, *vs)
    return outs[:n], outs[n:2 * n], outs[2 * n:]


WEIGHT_NAMES = ("meta_tokens", "ffn1_pre_norm", "ffn1_w_gate", "ffn1_w_up", "ffn1_w_down", "ffn1_post_norm", "mix_pre_norm",
                "w_in", "gla_w_a2", "gla_b_a", "gla_out_norm", "swa_sinks", "swa_out_norm", "w_out", "mix_post_norm",
                "ffn2_pre_norm", "ffn2_w_gate", "ffn2_w_up", "ffn2_w_down", "ffn2_post_norm")
WIN_SHARD = D_IN // N_DEV
WIN_SHARD_PAD = 304
SLAB_VECTORS = ("ffn1_pre", "ffn1_post", "mix_pre", "mix_post", "ffn2_pre", "ffn2_post")
SLAB_ROWS = 32


def kernel(x, meta_tokens, ffn1_pre_norm, ffn1_w_gate, ffn1_w_up, ffn1_w_down, ffn1_post_norm, mix_pre_norm, w_in, gla_w_a2, gla_b_a, gla_out_norm, swa_sinks, swa_out_norm, w_out, mix_post_norm, ffn2_pre_norm, ffn2_w_gate, ffn2_w_up, ffn2_w_down, ffn2_post_norm, loss_target, m_meta_tokens, m_ffn1_pre_norm, m_ffn1_w_gate, m_ffn1_w_up, m_ffn1_w_down, m_ffn1_post_norm, m_mix_pre_norm, m_w_in, m_gla_w_a2, m_gla_b_a, m_gla_out_norm, m_swa_sinks, m_swa_out_norm, m_w_out, m_mix_post_norm, m_ffn2_pre_norm, m_ffn2_w_gate, m_ffn2_w_up, m_ffn2_w_down, m_ffn2_post_norm, v_meta_tokens, v_ffn1_pre_norm, v_ffn1_w_gate, v_ffn1_w_up, v_ffn1_w_down, v_ffn1_post_norm, v_mix_pre_norm, v_w_in, v_gla_w_a2, v_gla_b_a, v_gla_out_norm, v_swa_sinks, v_swa_out_norm, v_w_out, v_mix_post_norm, v_ffn2_pre_norm, v_ffn2_w_gate, v_ffn2_w_up, v_ffn2_w_down, v_ffn2_post_norm):
    given = dict(locals())
    W = {n: given[n] for n in WEIGHT_NAMES}
    M = {n: given["m_" + n] for n in WEIGHT_NAMES}
    V = {n: given["v_" + n] for n in WEIGHT_NAMES}
    dev = _dev_index(*_place_on_mesh())

    def t16(w):
        return w[0].T.astype(BF16)

    small = jnp.concatenate([W["meta_tokens"], jnp.pad(W["gla_w_a2"][0], ((0, 0), (0, 96)))], axis=0)
    wg1, wu1, wd1, small_g, gathered_zeros = _all_gather(
        [t16(W["ffn1_w_gate"]), t16(W["ffn1_w_up"]), W["ffn1_w_down"][0].astype(BF16), small])
    def after_zero(shard, zeros):
        return shard + zeros[0:1, 0:1].astype(shard.dtype)
    win_shard = jnp.pad(t16(W["w_in"]), ((0, WIN_SHARD_PAD - WIN_SHARD), (0, 0)))
    win_shard = after_zero(win_shard, gathered_zeros)
    mid = _exchange_start([win_shard], GATHER, "gather_w_in_start")
    late_shards = [after_zero(W["w_out"][0].astype(BF16), mid[3]), t16(W["ffn2_w_gate"]), t16(W["ffn2_w_up"]),
                   W["ffn2_w_down"][0].astype(BF16)]
    late = _exchange_start(late_shards, GATHER, "gather_late_weights_start")

    def late_weights(what, after):
        if what == "win":
            win_g, = _exchange_wait(mid, GATHER, after, "gather_w_in_wait")
            win_t = win_g.reshape(N_DEV, WIN_SHARD_PAD, D_MODEL)[:, :WIN_SHARD].reshape(D_IN, D_MODEL)
            return dict(win=_win_pad_rows(win_t))
        wout, wg2, wu2, wd2 = _exchange_wait(late, GATHER, after, "gather_late_weights_wait")
        return dict(wout=wout, wg2=wg2, wu2=wu2, wd2=wd2)

    small_g = small_g.reshape(N_DEV, 32, 128)
    meta_full = small_g[:, :N_META].transpose(1, 0, 2).reshape(N_META, D_MODEL)
    wa2_full = small_g[:, N_META:, :32].transpose(1, 0, 2).reshape(16, 256)
    w = dict(
        ffn1_pre=W["ffn1_pre_norm"] + late[3][0, 0], ffn1_post=W["ffn1_post_norm"], mix_pre=W["mix_pre_norm"],
        mix_post=W["mix_post_norm"], ffn2_pre=W["ffn2_pre_norm"], ffn2_post=W["ffn2_post_norm"], b_a=W["gla_b_a"],
        gla_norm=W["gla_out_norm"], sinks=W["swa_sinks"], swa_norm=W["swa_out_norm"], wg1=wg1, wu1=wu1, wd1=wd1,
        wa2=jnp.pad(wa2_full, ((0, 112), (0, 0))))

    in_flight = []

    def on_grads(group, grads):
        parts = []
        for nm, p in grads.items():
            if nm == "win":
                p = _win_unpad_rows(p).reshape(N_DEV, WIN_SHARD, D_MODEL)
                p = jnp.pad(p, ((0, 0), (0, WIN_SHARD_PAD - WIN_SHARD), (0, 0))).reshape(N_DEV * WIN_SHARD_PAD, D_MODEL)
            parts.append(p)
        kind = SCATTER if group == "ffn2" else SCATTER_CHIPS
        if kind == SCATTER_CHIPS:
            parts = [_sibling_reduce(p, "pair_" + group + "_" + nm) for nm, p in zip(grads, parts)]
        started = _exchange_start(parts, kind, "scatter_" + group + "_start")
        in_flight.append((group, list(grads), started, kind))
        return started[3]

    front = jnp.concatenate([jnp.zeros((PAD_ROWS, D_MODEL), F32), meta_full], axis=0)
    loss, dh0, g = _local_step(x[0], loss_target[0], front, w, late_weights, on_grads)
    grad_x = dh0[BLK:][None]

    packed = jnp.concatenate([g["b_a"][0:1], g["gla_norm"][0:1], g["sinks"][0:1], g["swa_norm"][0:1]], axis=1)
    slab = jnp.concatenate([g[k][0:1] for k in SLAB_VECTORS] + [packed, jnp.full((1, D_MODEL), loss, F32),
                           g["wa2"][:16].reshape(4, D_MODEL), jnp.zeros((4, D_MODEL), F32), dh0[PAD_ROWS:BLK]], axis=0)
    tot = _all_reduce_small(slab + in_flight[-1][2][3][0, 0])
    loss = tot[7, 0]
    small_grads = dict(
        ffn1_pre_norm=tot[0:1], ffn1_post_norm=tot[1:2], mix_pre_norm=tot[2:3], mix_post_norm=tot[3:4],
        ffn2_pre_norm=tot[4:5], ffn2_post_norm=tot[5:6], gla_b_a=tot[6:7, 0:256], gla_out_norm=tot[6:7, 256:384],
        swa_sinks=tot[6:7, 384:392], swa_out_norm=tot[6:7, 512:1024],
        gla_w_a2=lax.dynamic_slice_in_dim(tot[8:12].reshape(16, 256), dev * 32, 32, axis=1)[None],
        meta_tokens=lax.dynamic_slice_in_dim(tot[16:32], dev * 128, 128, axis=1))

    big = dict(wg1=("ffn1_w_gate", True), wu1=("ffn1_w_up", True), wd1=("ffn1_w_down", False), win=("w_in", True),
               wout=("w_out", False), wg2=("ffn2_w_gate", True), wu2=("ffn2_w_up", True), wd2=("ffn2_w_down", False))
    grads = dict(small_grads)
    delta, new_m, new_v = {}, {}, {}
    names = [n for n in WEIGHT_NAMES if n not in [full for full, _ in big.values()]]
    two_d = lambda a: a.reshape(-1, a.shape[-1])
    d_, m_, v_ = _adamw([two_d(W[n]) for n in names], [two_d(grads[n]) for n in names],
                        [two_d(M[n]) for n in names], [two_d(V[n]) for n in names], "adamw_small")
    for k, n in enumerate(names):
        delta[n], new_m[n], new_v[n] = d_[k].reshape(W[n].shape), m_[k].reshape(W[n].shape), v_[k].reshape(W[n].shape)

    before_wait = d_[0] + in_flight[-1][2][3][0, 0]
    for group, shorts, started, kind in in_flight:
        lands = _exchange_wait(started, kind, before_wait, "scatter_" + group + "_wait")
        blocks = 4 if kind == SCATTER_CHIPS else N_DEV
        for short, land in zip(shorts, lands):
            n, transposed = big[short]
            to_slab = (lambda a: a[0].T) if transposed else (lambda a: a[0])
            from_slab = (lambda a: a.T[None]) if transposed else (lambda a: a[None])
            if short == "win":
                g_slab = _sum_partials([land], "sum_" + n, blocks)[0][:WIN_SHARD]
                d_, m_, v_ = _adamw([to_slab(W[n])], [g_slab], [to_slab(M[n])], [to_slab(V[n])], "adamw_" + n)
                d_, m_, v_ = d_[0], m_[0], v_[0]
            else:
                g_slab, d_, m_, v_ = _sum_adamw(land, to_slab(W[n]), to_slab(M[n]), to_slab(V[n]), blocks, "adamw_" + n)
            grads[n], delta[n], new_m[n], new_v[n] = from_slab(g_slab), from_slab(d_), from_slab(m_), from_slab(v_)
            before_wait = d_
    return (loss, grad_x, *[grads[n] for n in WEIGHT_NAMES], *[delta[n] for n in WEIGHT_NAMES],
            *[new_m[n] for n in WEIGHT_NAMES], *[new_v[n] for n in WEIGHT_NAMES])
```

```python
import math

import jax
import jax.numpy as jnp
from jax import lax
from jax.experimental import pallas as pl
from jax.experimental.pallas import tpu as pltpu

F32, BF16 = jnp.float32, jnp.bfloat16

D_MODEL = 1024
D_FF = 2816
N_META = 16
BLK = 128
PAD_ROWS = BLK - N_META
GLA_DK = 64
SWA_HD = 64
SWA_HEADS = 8
GLA_TAU = 16.0
NORM_EPS = 1e-6
NEG_INF = -1e30
ROPE_THETA = 10000.0
P_GQ, P_GK, P_GV, P_GG, P_SQ, P_SK, P_SV, P_GA, P_END = 0, 256, 512, 1024, 1536, 2048, 2176, 2304, 2432
D_IN = 2320
IN_SPLITS = (256, 256, 512, 512, 16, 512, 128, 128)
FF_TILE = 2816
WGRAD_TILE_MAX = 2432
N_DEV = 8
MESH = pl.DeviceIdType.MESH

ADAM_LR, ADAM_B1, ADAM_B2, ADAM_EPS, ADAM_WD, ADAM_STEP = 0.001, 0.9, 0.999, 1e-08, 0.01, 10

V7X_VMEM_BYTES = 64 << 20
VMEM_SPEC = pl.BlockSpec(memory_space=pltpu.VMEM)
SMEM_SPEC = pl.BlockSpec(memory_space=pltpu.SMEM)
ANY_SPEC = pl.BlockSpec(memory_space=pl.ANY)


def _params(semantics, vmem_mb=56):
    return pltpu.CompilerParams(dimension_semantics=semantics, vmem_limit_bytes=vmem_mb << 20)


def _row_tile(rows):
    return 416 if rows % 416 == 0 else BLK


def _nn(a, b):
    return lax.dot_general(a, b, (((1,), (0,)), ((), ())), preferred_element_type=F32)


def _nt(a, b):
    return lax.dot_general(a, b, (((1,), (1,)), ((), ())), preferred_element_type=F32)


def _tn(a, b):
    return lax.dot_general(a, b, (((0,), (0,)), ((), ())), preferred_element_type=F32)


def _rms(x):
    r = lax.rsqrt(jnp.mean(x * x, axis=-1, keepdims=True) + NORM_EPS)
    return x * r, r


def _rms_bwd(xn, r, w, dy):
    g = dy * w
    return r * (g - xn * jnp.mean(g * xn, axis=-1, keepdims=True))


def _sigmoid(x):
    return 1.0 / (1.0 + jnp.exp(-x))


def _colsum(x):
    return jnp.sum(x, axis=0, keepdims=True)


def _split_bf16(x):
    hi = x.astype(BF16)
    lo = (x - hi.astype(F32)).astype(BF16)
    return hi, lo


def _tri(lower):
    r = lax.broadcasted_iota(jnp.int32, (BLK, BLK), 0)
    c = lax.broadcasted_iota(jnp.int32, (BLK, BLK), 1)
    return (r >= c) if lower else (c >= r)


def _half_mask(width, half):
    lane = lax.broadcasted_iota(jnp.int32, (1, width), 1)
    return ((lane % 128) < 64) if half == 0 else ((lane % 128) >= 64)


def _rot_half(x):
    w = x.shape[-1]
    lane = lax.broadcasted_iota(jnp.int32, (1, w), 1)
    return jnp.where((lane % SWA_HD) < SWA_HD // 2, -pltpu.roll(x, w - SWA_HD // 2, 1), pltpu.roll(x, SWA_HD // 2, 1))


def _row_spec(tm, cols):
    return pl.BlockSpec((tm, cols), lambda i: (i, 0))


def _acc_spec(cols):
    return pl.BlockSpec((8, cols), lambda i: (0, 0))


def _acc_add(ref, first, value):
    @pl.when(first)
    def _():
        ref[...] = jnp.zeros_like(ref)
    ref[0:1, :] += value


def _behind_front(ref, i, tm, front):
    blk = ref[...]
    return jnp.where(i == 0, jnp.concatenate([front, blk[0:tm - BLK]], axis=0), blk)


def _ffn_fwd(h, gpre, wg_t, wu_t, wd, gpost, tgt=None, front=None):
    with_loss, with_front = tgt is not None, front is not None
    rows = h.shape[0] + (BLK if with_front else 0)
    tm = _row_tile(rows)
    nf = D_FF // FF_TILE

    def body(*refs):
        refs = list(refs)
        h_ref, gpre_ref, wg_ref, wu_ref, wd_ref, gpost_ref = refs[:6]
        del refs[:6]
        front_ref = refs.pop(0) if with_front else None
        t_ref = refs.pop(0) if with_loss else None
        h0_ref = refs.pop(0) if with_front else None
        ho_ref, a_ref, b_ref, s_ref, f_ref = refs[:5]
        dy_ref, loss_ref = refs[5:7] if with_loss else (None, None)
        acc = refs[-1]
        i = pl.program_id(0)
        if with_front:
            h_in = _behind_front(h_ref, i, tm, front_ref[...])
            h0_ref[...] = h_in
        else:
            h_in = h_ref[...]
        hn, _ = _rms(h_in)
        n16 = (hn * gpre_ref[...]).astype(BF16)
        for j in range(nf):
            cols = slice(j * FF_TILE, (j + 1) * FF_TILE)
            a = _nt(n16, wg_ref[cols, :])
            b = _nt(n16, wu_ref[cols, :])
            a_ref[:, cols] = a.astype(BF16)
            b_ref[:, cols] = b.astype(BF16)
            s16 = (a * _sigmoid(a) * b).astype(BF16)
            s_ref[:, cols] = s16
            part = _nn(s16, wd_ref[cols, :])
            if j == 0:
                acc[...] = part
            else:
                acc[...] += part
        f = acc[...]
        f_ref[...] = f
        fn, _ = _rms(f)
        y = h_in + 0.5 * (fn * gpost_ref[...])
        ho_ref[...] = y
        if with_loss:
            row = i * tm + lax.broadcasted_iota(jnp.int32, (tm, 1), 0)
            err = jnp.where(row >= BLK, y - _behind_front(t_ref, i, tm, jnp.zeros((BLK, D_MODEL), F32)), 0.0)
            dy_ref[...] = err * (1.0 / D_MODEL)
            part = 0.5 * jnp.sum(jnp.sum(err * err, axis=-1, keepdims=True) * (1.0 / D_MODEL), axis=0, keepdims=True)

            @pl.when(i == 0)
            def _():
                loss_ref[...] = jnp.zeros_like(loss_ref)
            loss_ref[...] += part

    row_f32 = _row_spec(tm, D_MODEL)
    behind = pl.BlockSpec((pl.Element(tm), pl.Element(D_MODEL)),
                          lambda i: (pl.multiple_of(jnp.maximum(i * tm - BLK, 0), math.gcd(tm, BLK)), 0))
    in_specs = [behind if with_front else row_f32, VMEM_SPEC, VMEM_SPEC, VMEM_SPEC, VMEM_SPEC, VMEM_SPEC]
    out_specs = [row_f32, _row_spec(tm, D_FF), _row_spec(tm, D_FF), _row_spec(tm, D_FF), row_f32]
    out_shape = [jax.ShapeDtypeStruct((rows, D_MODEL), F32), jax.ShapeDtypeStruct((rows, D_FF), BF16),
                 jax.ShapeDtypeStruct((rows, D_FF), BF16), jax.ShapeDtypeStruct((rows, D_FF), BF16),
                 jax.ShapeDtypeStruct((rows, D_MODEL), F32)]
    args = [h, gpre, wg_t, wu_t, wd, gpost]
    if with_front:
        in_specs.append(VMEM_SPEC)
        args.append(front)
        out_specs.insert(0, row_f32)
        out_shape.insert(0, jax.ShapeDtypeStruct((rows, D_MODEL), F32))
    if with_loss:
        in_specs.append(behind)
        args.append(tgt)
        out_specs += [row_f32, pl.BlockSpec((8, 128), lambda i: (0, 0))]
        out_shape += [jax.ShapeDtypeStruct((rows, D_MODEL), F32), jax.ShapeDtypeStruct((8, 128), F32)]
    return pl.pallas_call(
        body, name="ffn_fwd_loss" if with_loss else "ffn_fwd", grid=(rows // tm,),
        in_specs=in_specs, out_specs=out_specs, out_shape=out_shape,
        scratch_shapes=[pltpu.VMEM((tm, D_MODEL), F32)],
        compiler_params=_params(("arbitrary",)),
    )(*args)


def _ffn_bwd_act(dh_out, h, a, b, f, gpre, gpost, wg_t, wu_t, wd, name):
    rows = h.shape[0]
    tm = _row_tile(rows)
    nf = D_FF // FF_TILE

    def body(dho_ref, h_ref, a_ref, b_ref, f_ref, gpre_ref, gpost_ref, wg_ref, wu_ref, wd_ref,
             dh_ref, da_ref, db_ref, df_ref, n_ref, dgpre_ref, dgpost_ref, acc):
        first = pl.program_id(0) == 0
        dho = dho_ref[...]
        drr = 0.5 * dho
        fn, rf = _rms(f_ref[...])
        _acc_add(dgpost_ref, first, _colsum(drr * fn))
        df16 = _rms_bwd(fn, rf, gpost_ref[...], drr).astype(BF16)
        df_ref[...] = df16
        hn, rh = _rms(h_ref[...])
        n_ref[...] = (hn * gpre_ref[...]).astype(BF16)
        for j in range(nf):
            cols = slice(j * FF_TILE, (j + 1) * FF_TILE)
            ds = _nt(df16, wd_ref[cols, :])
            av = a_ref[:, cols].astype(F32)
            bv = b_ref[:, cols].astype(F32)
            sg = _sigmoid(av)
            db16 = (ds * (av * sg)).astype(BF16)
            da16 = (ds * bv * (sg * (1.0 + av * (1.0 - sg)))).astype(BF16)
            da_ref[:, cols] = da16
            db_ref[:, cols] = db16
            part = _nn(da16, wg_ref[cols, :]) + _nn(db16, wu_ref[cols, :])
            if j == 0:
                acc[...] = part
            else:
                acc[...] += part
        dn = acc[...]
        _acc_add(dgpre_ref, first, _colsum(dn * hn))
        dh_ref[...] = dho + _rms_bwd(hn, rh, gpre_ref[...], dn)

    row_f32 = _row_spec(tm, D_MODEL)
    row_ff = _row_spec(tm, D_FF)
    return pl.pallas_call(
        body, name=name, grid=(rows // tm,),
        in_specs=[row_f32, row_f32, row_ff, row_ff, row_f32, VMEM_SPEC, VMEM_SPEC, VMEM_SPEC, VMEM_SPEC, VMEM_SPEC],
        out_specs=[row_f32, row_ff, row_ff, row_f32, row_f32, _acc_spec(D_MODEL), _acc_spec(D_MODEL)],
        out_shape=[jax.ShapeDtypeStruct((rows, D_MODEL), F32), jax.ShapeDtypeStruct((rows, D_FF), BF16),
                   jax.ShapeDtypeStruct((rows, D_FF), BF16), jax.ShapeDtypeStruct((rows, D_MODEL), BF16),
                   jax.ShapeDtypeStruct((rows, D_MODEL), BF16), jax.ShapeDtypeStruct((8, D_MODEL), F32),
                   jax.ShapeDtypeStruct((8, D_MODEL), F32)],
        scratch_shapes=[pltpu.VMEM((tm, D_MODEL), F32)],
        compiler_params=_params(("arbitrary",), vmem_mb=62),
    )(dh_out, h, a, b, f, gpre, gpost, wg_t, wu_t, wd)


def _wgrad(lhs, rhs, name, after=None):
    rows, width = lhs.shape
    tm = rows if rows % 1664 == 0 else BLK
    tf = 256 if width % 256 == 0 else 128
    nr = rows // tm

    def body(l_ref, r_ref, *rest):
        o_ref, acc = rest[-2:]
        i = pl.program_id(1)
        part = _tn(l_ref[...], r_ref[...])

        @pl.when(i == 0)
        def _():
            acc[...] = part

        @pl.when(i > 0)
        def _():
            acc[...] += part

        @pl.when(i == nr - 1)
        def _():
            o_ref[...] = acc[...].astype(BF16)

    l_spec = pl.BlockSpec((tm, tf), lambda j, i: (i, j))
    r_spec = pl.BlockSpec((tm, D_MODEL), lambda j, i: (i, 0))
    return pl.pallas_call(
        body, name=name, grid=(width // tf, nr),
        in_specs=[l_spec, r_spec] + ([] if after is None else [ANY_SPEC]),
        out_specs=pl.BlockSpec((tf, D_MODEL), lambda j, i: (j, 0)),
        out_shape=jax.ShapeDtypeStruct((width, D_MODEL), BF16),
        scratch_shapes=[pltpu.VMEM((tf, D_MODEL), F32)],
        compiler_params=_params(("arbitrary", "arbitrary")),
    )(lhs, rhs, *([] if after is None else [after]))


def _chunk_cumsum(x, lower):
    tri = jnp.where(_tri(lower), 1.0, 0.0).astype(BF16)
    hi, lo = _split_bf16(x)
    return _nn(tri, hi) + _nn(tri, lo)


def _mix_in(h, g, win_p, wa2_p, b_a, cos, sin):
    rows = h.shape[0]
    tm = 640 if rows % 640 == 0 else BLK

    def body(h_ref, g_ref, win_ref, wa2_ref, ba_ref, cos_ref, sin_ref,
             gq_ref, gk_ref, gv_ref, gg_ref, sq_ref, sk_ref, sv_ref, ga_ref, loga_ref, bc_ref, n_ref):
        hn, _ = _rms(h_ref[...])
        n16 = (hn * g_ref[...]).astype(BF16)
        n_ref[...] = n16
        proj = _nt(n16, win_ref[...])
        gq_ref[...] = proj[:, P_GQ:P_GK]
        gk_ref[...] = proj[:, P_GK:P_GV]
        gv_ref[...] = proj[:, P_GV:P_GG].astype(BF16)
        gg_ref[...] = proj[:, P_GG:P_SQ]
        c1, s1 = cos_ref[...], sin_ref[...]
        c4 = jnp.concatenate([c1, c1, c1, c1], axis=1)
        s4 = jnp.concatenate([s1, s1, s1, s1], axis=1)
        sq = proj[:, P_SQ:P_SK]
        sk = proj[:, P_SK:P_SV]
        sq_ref[...] = (sq * c4 + _rot_half(sq) * s4).astype(BF16)
        sk_ref[...] = (sk * c1 + _rot_half(sk) * s1).astype(BF16)
        sv_ref[...] = proj[:, P_SV:P_GA].astype(BF16)
        ga = proj[:, P_GA:P_END]
        ga_ref[...] = ga
        z = _nn(ga, wa2_ref[...]) + ba_ref[...]
        loga = (jnp.minimum(z, 0.0) - jnp.log(1.0 + jnp.exp(-jnp.abs(z)))) * (1.0 / GLA_TAU)
        loga_ref[...] = loga
        for c in range(tm // BLK):
            rs = slice(c * BLK, (c + 1) * BLK)
            bc_ref[rs, :] = _chunk_cumsum(loga[rs, :], True)

    f32 = lambda c: jax.ShapeDtypeStruct((rows, c), F32)
    b16 = lambda c: jax.ShapeDtypeStruct((rows, c), BF16)
    rs = lambda c: _row_spec(tm, c)
    return pl.pallas_call(
        body, name="mix_in", grid=(rows // tm,),
        in_specs=[rs(D_MODEL), VMEM_SPEC, VMEM_SPEC, VMEM_SPEC, VMEM_SPEC, rs(128), rs(128)],
        out_specs=[rs(256), rs(256), rs(512), rs(512), rs(512), rs(128), rs(128), rs(128), rs(256), rs(256), rs(D_MODEL)],
        out_shape=[f32(256), f32(256), b16(512), f32(512), b16(512), b16(128), b16(128), f32(128), f32(256), f32(256),
                   b16(D_MODEL)],
        compiler_params=_params(("arbitrary",)),
    )(h, g, win_p, wa2_p, b_a, cos, sin)


def _gla_factors(q, k, bc):
    bm = bc[BLK // 2 - 1:BLK // 2, :]
    bl = bc[BLK - 1:BLK, :]
    e_q, e_k, e_qe, e_kd = jnp.exp(bc - bm), jnp.exp(bm - bc), jnp.exp(bc), jnp.exp(bl - bc)
    return (q * e_q, k * e_k, q * e_qe, k * e_kd), (e_q, e_k, e_qe, e_kd), jnp.exp(bl)


def _gla_fwd(gq, gk, gv, gg, bc, wgn):
    rows = gq.shape[0]
    nc = rows // BLK
    scale = GLA_DK ** -0.5

    def body(q_ref, k_ref, v_ref, gg_ref, bc_ref, wgn_ref, o_ref, cat_ref, sp_ref, st):
        @pl.when(pl.program_id(0) == 0)
        def _():
            st[...] = jnp.zeros_like(st)
        low = _tri(True)
        wgn_v = wgn_ref[...]
        for p in range(2):
            sl = slice(128 * p, 128 * p + 128)
            (qt, kt, qe, kd), _, ebl = _gla_factors(q_ref[:, sl] * scale, k_ref[:, sl], bc_ref[:, sl])
            s_prev = st[p]
            sp_ref[0, p] = s_prev
            s16 = s_prev.astype(BF16)
            qt16 = qt.astype(BF16)
            s_new = s_prev * ebl
            for hh in range(2):
                hs = slice(128 * (2 * p + hh), 128 * (2 * p + hh) + 128)
                lm = _half_mask(128, hh)
                vh = v_ref[:, hs]
                pm = jnp.where(low, _nt(qt16, jnp.where(lm, kt, 0.0).astype(BF16)), 0.0)
                o = _nn(pm.astype(BF16), vh) + _nt(jnp.where(lm, qe, 0.0).astype(BF16), s16)
                s_new = s_new + _tn(vh, jnp.where(lm, kd, 0.0).astype(BF16))
                o_ref[:, hs] = o
                on, _ = _rms(o)
                gate = gg_ref[:, hs]
                cat_ref[:, hs] = (on * wgn_v * (gate * _sigmoid(gate))).astype(BF16)
            st[p] = s_new

    rs = lambda c: _row_spec(BLK, c)
    return pl.pallas_call(
        body, name="gla_fwd", grid=(nc,),
        in_specs=[rs(256), rs(256), rs(512), rs(512), rs(256), VMEM_SPEC],
        out_specs=[rs(512), rs(512), pl.BlockSpec((1, 2, 128, 128), lambda i: (i, 0, 0, 0))],
        out_shape=[jax.ShapeDtypeStruct((rows, 512), F32), jax.ShapeDtypeStruct((rows, 512), BF16),
                   jax.ShapeDtypeStruct((nc, 2, 128, 128), F32)],
        scratch_shapes=[pltpu.VMEM((2, 128, 128), F32)],
        compiler_params=_params(("arbitrary",)),
    )(gq, gk, gv, gg, bc, wgn)


def _gla_bwd(dcat, o_all, gq, gk, gv, gg, bc, sp, wgn):
    rows = gq.shape[0]
    nc = rows // BLK
    scale = GLA_DK ** -0.5

    def body(dc_ref, o_ref, q_ref, k_ref, v_ref, gg_ref, bc_ref, sp_ref, wgn_ref,
             dq_ref, dk_ref, dv_ref, dgg_ref, dla_ref, dwgn_ref, dst):
        first = pl.program_id(0) == 0

        @pl.when(first)
        def _():
            dst[...] = jnp.zeros_like(dst)
        low, upp = _tri(True), _tri(False)
        last_row = lax.broadcasted_iota(jnp.int32, (BLK, 1), 0) == BLK - 1
        wgn_v = wgn_ref[...]
        dwgn = jnp.zeros((1, 128), F32)
        for p in range(2):
            sl = slice(128 * p, 128 * p + 128)
            (qt, kt, qe, kd), (e_q, e_k, e_qe, e_kd), ebl = _gla_factors(
                q_ref[:, sl] * scale, k_ref[:, sl], bc_ref[:, sl])
            s_prev = sp_ref[0, p]
            s16 = s_prev.astype(BF16)
            ds_next = dst[p]
            ds16 = ds_next.astype(BF16)
            qt16 = qt.astype(BF16)
            ds_new = ds_next * ebl
            dqt = jnp.zeros((BLK, 128), F32)
            dkt = jnp.zeros((BLK, 128), F32)
            dqe = jnp.zeros((BLK, 128), F32)
            dkd = jnp.zeros((BLK, 128), F32)
            for hh in range(2):
                hs = slice(128 * (2 * p + hh), 128 * (2 * p + hh) + 128)
                lm = _half_mask(128, hh)
                on, ro = _rms(o_ref[:, hs])
                gate = gg_ref[:, hs]
                sg = _sigmoid(gate)
                si = gate * sg
                dog = dc_ref[:, hs]
                dwgn = dwgn + _colsum(dog * si * on)
                dgg_ref[:, hs] = dog * (on * wgn_v) * (sg * (1.0 + gate * (1.0 - sg)))
                do16 = _rms_bwd(on, ro, wgn_v, dog * si).astype(BF16)
                vh = v_ref[:, hs]
                ktm16 = jnp.where(lm, kt, 0.0).astype(BF16)
                qtm16 = jnp.where(lm, qt, 0.0).astype(BF16)
                qem16 = jnp.where(lm, qe, 0.0).astype(BF16)
                kdm16 = jnp.where(lm, kd, 0.0).astype(BF16)
                p_t = jnp.where(upp, _nt(ktm16, qt16), 0.0)
                dp_t = jnp.where(upp, _nt(vh, do16), 0.0)
                dp = jnp.where(low, _nt(do16, vh), 0.0)
                dv_ref[:, hs] = _nn(p_t.astype(BF16), do16) + _nt(kdm16, ds16)
                dqt = dqt + _nn(dp.astype(BF16), ktm16)
                dkt = dkt + _nn(dp_t.astype(BF16), qtm16)
                dqe = dqe + jnp.where(lm, _nn(do16, s16), 0.0)
                dkd = dkd + jnp.where(lm, _nn(vh, ds16), 0.0)
                ds_new = ds_new + _tn(do16, qem16)
            debl = _colsum(ds_next * s_prev)
            dq_ref[:, sl] = (dqt * e_q + dqe * e_qe) * scale
            dk_ref[:, sl] = dkt * e_k + dkd * e_kd
            dkd_kd = dkd * kd
            db = dqt * qt - dkt * kt + dqe * qe - dkd_kd
            db = jnp.where(last_row, db + (_colsum(dkd_kd) + debl * ebl), db)
            dla_ref[:, sl] = _chunk_cumsum(db, False)
            dst[p] = ds_new
        _acc_add(dwgn_ref, first, dwgn)

    rev = lambda c: pl.BlockSpec((BLK, c), lambda i: (nc - 1 - i, 0))
    f32 = lambda c: jax.ShapeDtypeStruct((rows, c), F32)
    return pl.pallas_call(
        body, name="gla_bwd", grid=(nc,),
        in_specs=[rev(512), rev(512), rev(256), rev(256), rev(512), rev(512), rev(256),
                  pl.BlockSpec((1, 2, 128, 128), lambda i: (nc - 1 - i, 0, 0, 0)), VMEM_SPEC],
        out_specs=[rev(256), rev(256), rev(512), rev(512), rev(256), _acc_spec(128)],
        out_shape=[f32(256), f32(256), f32(512), f32(512), f32(256), jax.ShapeDtypeStruct((8, 128), F32)],
        scratch_shapes=[pltpu.VMEM((2, 128, 128), F32)],
        compiler_params=_params(("arbitrary",)),
    )(dcat, o_all, gq, gk, gv, gg, bc, sp, wgn)


def _swa_masks(i):
    t = lax.broadcasted_iota(jnp.int32, (BLK, BLK), 0)
    c = lax.broadcasted_iota(jnp.int32, (BLK, BLK), 1)
    own_side = c <= t
    band_ok = i >= jnp.where(own_side, 1, 2)
    meta_ok = (c % N_META) <= jnp.where(i >= 1, N_META, t - PAD_ROWS)
    return own_side, band_ok, meta_ok, c // N_META


def _swa_blocks(ref, i):
    prev = pl.multiple_of(jnp.maximum(i - 1, 0) * BLK, BLK)
    own = pl.multiple_of(i * BLK, BLK)
    return jnp.concatenate([ref[pl.ds(prev, BLK), :], ref[pl.ds(own, BLK), :]], axis=0), prev, own


def _swa_meta_operand(ref):
    blk = ref[0:BLK, :]
    swapped = pltpu.roll(blk, 64, 1)
    lo = jnp.where(_half_mask(128, 0), blk, swapped)
    hi = jnp.where(_half_mask(128, 1), blk, swapped)
    meta = jnp.concatenate([lo, lo, hi, hi], axis=1)[PAD_ROWS:BLK, :]
    tiled = jnp.concatenate([meta] * SWA_HEADS, axis=0)
    j = lax.broadcasted_iota(jnp.int32, tiled.shape, 0)
    lane = lax.broadcasted_iota(jnp.int32, tiled.shape, 1)
    return jnp.where(j // N_META == lane // SWA_HD, tiled, jnp.zeros_like(tiled))


def _swa_meta_fold(acc):
    out = jnp.zeros((N_META, 128), F32)
    for hd in range(SWA_HEADS):
        half, kv = hd % 2, hd // 4
        piece = acc[N_META * hd:N_META * (hd + 1), 128 * (hd // 2):128 * (hd // 2) + 128]
        piece = jnp.where(_half_mask(128, half), piece, 0.0)
        out = out + (piece if half == kv else pltpu.roll(piece, 64, 1))
    return out


def _by_head(group, per_head):
    out = jnp.zeros((BLK, BLK), F32)
    for hd, v in enumerate(per_head):
        out = jnp.where(group == hd, v, out)
    return out


def _place(x, kv):
    if kv == 0:
        lo = jnp.where(_half_mask(128, 0), x, jnp.zeros_like(x))
        return lo, pltpu.roll(lo, 64, 1)
    hi = jnp.where(_half_mask(128, 1), x, jnp.zeros_like(x))
    return pltpu.roll(hi, 64, 1), hi


def _swa_fwd(sq, sk, sv, sinks, wn):
    rows = sq.shape[0]
    nb = rows // BLK
    scale = SWA_HD ** -0.5

    def body(q_ref, k_ref, v_ref, sink_ref, wn_ref, o_ref, cat_ref, lse_ref, kp, vp):
        i = pl.program_id(0)

        @pl.when(i == 0)
        def _():
            kp[...] = _swa_meta_operand(k_ref)
            vp[...] = _swa_meta_operand(v_ref)
        own_side, band_ok, meta_ok, group = _swa_masks(i)
        k2, _, _ = _swa_blocks(k_ref, i)
        v2, _, _ = _swa_blocks(v_ref, i)
        kz = (_place(k2, 0), _place(k2, 1))
        vz = (_place(v2, 0), _place(v2, 1))
        q_all = q_ref[...]
        s_meta = jnp.where(meta_ok, _nt(q_all, kp[...]) * scale, NEG_INF)
        s_band, m = [], []
        for hd in range(SWA_HEADS):
            kv, half = hd // 4, hd % 2
            q_pair = q_all[:, 128 * (hd // 2):128 * (hd // 2) + 128]
            s2 = _nt(q_pair, kz[kv][half])
            s = jnp.where(band_ok, jnp.where(own_side, s2[:, BLK:], s2[:, :BLK]) * scale, NEG_INF)
            top = jnp.maximum(jnp.max(s, axis=-1, keepdims=True),
                              jnp.max(jnp.where(group == hd, s_meta, NEG_INF), axis=-1, keepdims=True))
            s_band.append(s)
            m.append(jnp.maximum(top, sink_ref[0, hd]))
        e_meta = jnp.exp(s_meta - _by_head(group, m))
        o_meta = _nn(e_meta.astype(BF16), vp[...])
        outs = []
        for pr in range(4):
            o_pair = o_meta[:, 128 * pr:128 * pr + 128]
            rden = []
            for half in range(2):
                hd = 2 * pr + half
                kv = hd // 4
                e = jnp.exp(s_band[hd] - m[hd])
                den = (jnp.sum(e, axis=-1, keepdims=True)
                       + jnp.sum(jnp.where(group == hd, e_meta, 0.0), axis=-1, keepdims=True)
                       + jnp.exp(sink_ref[0, hd] - m[hd]))
                lse_ref[:, hd:hd + 1] = m[hd] + jnp.log(den)
                rden.append(1.0 / den)
                e2 = jnp.concatenate([jnp.where(own_side, 0.0, e), jnp.where(own_side, e, 0.0)], axis=1).astype(BF16)
                o_pair = o_pair + _nn(e2, vz[kv][half])
            outs.append(o_pair * jnp.where(_half_mask(128, 0), rden[0], rden[1]))
        o = jnp.concatenate(outs, axis=1)
        o_ref[...] = o
        on, _ = _rms(o)
        cat_ref[...] = (on * wn_ref[...]).astype(BF16)

    return pl.pallas_call(
        body, name="swa_fwd", grid=(nb,),
        in_specs=[_row_spec(BLK, 512), VMEM_SPEC, VMEM_SPEC, SMEM_SPEC, VMEM_SPEC],
        out_specs=[_row_spec(BLK, 512), _row_spec(BLK, 512), _row_spec(BLK, SWA_HEADS)],
        out_shape=[jax.ShapeDtypeStruct((rows, 512), F32), jax.ShapeDtypeStruct((rows, 512), BF16),
                   jax.ShapeDtypeStruct((rows, SWA_HEADS), F32)],
        scratch_shapes=[pltpu.VMEM((BLK, 512), BF16), pltpu.VMEM((BLK, 512), BF16)],
        compiler_params=_params(("arbitrary",)),
    )(sq, sk, sv, sinks, wn)


def _swa_bwd(dcat, o_all, sq, sk, sv, lse, sinks, wn):
    rows = sq.shape[0]
    nb = rows // BLK
    scale = SWA_HD ** -0.5

    def body(dc_ref, o_ref, q_ref, k_ref, v_ref, lse_ref, sink_ref, wn_ref, dq_ref, dk_ref, dv_ref, dsink_ref, dwn_ref,
             kp, vp, dkp, dvp):
        i = pl.program_id(0)
        first = i == 0

        @pl.when(first)
        def _():
            dk_ref[...] = jnp.zeros_like(dk_ref)
            dv_ref[...] = jnp.zeros_like(dv_ref)
            dkp[...] = jnp.zeros_like(dkp)
            dvp[...] = jnp.zeros_like(dvp)
            kp[...] = _swa_meta_operand(k_ref)
            vp[...] = _swa_meta_operand(v_ref)
        own_side, band_ok, meta_ok, group = _swa_masks(i)
        k2, prev, own = _swa_blocks(k_ref, i)
        v2, _, _ = _swa_blocks(v_ref, i)
        kz = (_place(k2, 0), _place(k2, 1))
        vz = (_place(v2, 0), _place(v2, 1))
        o = o_ref[...]
        on, ro = _rms(o)
        dc = dc_ref[...]
        _acc_add(dwn_ref, first, _colsum(dc * on))
        do = _rms_bwd(on, ro, wn_ref[...], dc)
        do_o = do * o
        do16 = do.astype(BF16)
        q_all = q_ref[...]
        lse = [lse_ref[:, hd:hd + 1] for hd in range(SWA_HEADS)]
        delta = [jnp.sum(jnp.where(_half_mask(128, hd % 2), do_o[:, 128 * (hd // 2):128 * (hd // 2) + 128], 0.0),
                         axis=-1, keepdims=True) for hd in range(SWA_HEADS)]
        s_meta = jnp.where(meta_ok, _nt(q_all, kp[...]) * scale, NEG_INF)
        p_meta = jnp.exp(s_meta - _by_head(group, lse))
        ds_meta16 = (p_meta * (_nt(do16, vp[...]) - _by_head(group, delta)) * scale).astype(BF16)
        dq_meta = _nn(ds_meta16, kp[...])
        dkp[...] += _tn(ds_meta16, q_all)
        dvp[...] += _tn(p_meta.astype(BF16), do16)
        own2 = jnp.concatenate([own_side.astype(jnp.int32)] * 2, axis=0) > 0
        ok2 = jnp.concatenate([band_ok.astype(jnp.int32)] * 2, axis=0) > 0

        def window(x2):
            return jnp.where(own2, x2[:, BLK:], x2[:, :BLK])

        def unwindow(x):
            return jnp.concatenate([jnp.where(own2, 0.0, x), jnp.where(own2, x, 0.0)], axis=1).astype(BF16)
        lane8 = lax.broadcasted_iota(jnp.int32, (1, 128), 1)
        dsink = jnp.zeros((1, 128), F32)
        dq_pairs = [dq_meta[:, 128 * pr:128 * pr + 128] for pr in range(4)]
        dk2 = [[None, None], [None, None]]
        dv2 = [[None, None], [None, None]]
        for kv in range(2):
            for half in range(2):
                heads, pairs = (4 * kv + half, 4 * kv + 2 + half), (2 * kv, 2 * kv + 1)
                q_s = jnp.concatenate([q_all[:, 128 * pr:128 * pr + 128] for pr in pairs], axis=0)
                do_s = jnp.concatenate([do16[:, 128 * pr:128 * pr + 128] for pr in pairs], axis=0)
                lse_s = jnp.concatenate([lse[hd] for hd in heads], axis=0)
                delta_s = jnp.concatenate([delta[hd] for hd in heads], axis=0)
                s = jnp.where(ok2, window(_nt(q_s, kz[kv][half])) * scale, NEG_INF)
                prob = jnp.exp(s - lse_s)
                for hd in heads:
                    dsink = dsink + jnp.where(lane8 == hd, -jnp.sum(jnp.exp(sink_ref[0, hd] - lse[hd]) * delta[hd]), 0.0)
                ds2 = unwindow(prob * (window(_nt(do_s, vz[kv][half])) - delta_s) * scale)
                dq_s = _nn(ds2, kz[kv][half])
                dq_pairs[pairs[0]] = dq_pairs[pairs[0]] + dq_s[:BLK]
                dq_pairs[pairs[1]] = dq_pairs[pairs[1]] + dq_s[BLK:]
                dk2[kv][half] = _tn(ds2, q_s)
                dv2[kv][half] = _tn(unwindow(prob), do_s)
        dq_ref[...] = jnp.concatenate(dq_pairs, axis=1)
        _acc_add(dsink_ref, first, dsink)
        for ref, acc2 in ((dk_ref, dk2), (dv_ref, dv2)):
            tot = jnp.zeros((2 * BLK, 128), F32)
            for kv in range(2):
                for half in range(2):
                    part = jnp.where(_half_mask(128, half), acc2[kv][half], 0.0)
                    tot = tot + (part if half == kv else pltpu.roll(part, 64, 1))
            ref[pl.ds(prev, BLK), :] += tot[:BLK]
            ref[pl.ds(own, BLK), :] += tot[BLK:]

        @pl.when(i == nb - 1)
        def _():
            dk_ref[PAD_ROWS:BLK, :] += _swa_meta_fold(dkp[...])
            dv_ref[PAD_ROWS:BLK, :] += _swa_meta_fold(dvp[...])

    full = pl.BlockSpec((rows, 128), lambda i: (0, 0))
    return pl.pallas_call(
        body, name="swa_bwd", grid=(nb,),
        in_specs=[_row_spec(BLK, 512), _row_spec(BLK, 512), _row_spec(BLK, 512), VMEM_SPEC, VMEM_SPEC,
                  _row_spec(BLK, SWA_HEADS), SMEM_SPEC, VMEM_SPEC],
        out_specs=[_row_spec(BLK, 512), full, full, _acc_spec(128), _acc_spec(512)],
        out_shape=[jax.ShapeDtypeStruct((rows, 512), F32), jax.ShapeDtypeStruct((rows, 128), F32),
                   jax.ShapeDtypeStruct((rows, 128), F32), jax.ShapeDtypeStruct((8, 128), F32),
                   jax.ShapeDtypeStruct((8, 512), F32)],
        scratch_shapes=[pltpu.VMEM((BLK, 512), BF16), pltpu.VMEM((BLK, 512), BF16),
                        pltpu.VMEM((BLK, 512), F32), pltpu.VMEM((BLK, 512), F32)],
        compiler_params=_params(("arbitrary",)),
    )(dcat, o_all, sq, sk, sv, lse, sinks, wn)


def _mix_out(h, cat_g, cat_s, wout, gpost):
    rows = h.shape[0]
    tm = _row_tile(rows)

    def body(h_ref, cg_ref, cs_ref, w_ref, g_ref, ho_ref, m_ref):
        m = _nn(cg_ref[...], w_ref[0:512, :]) + _nn(cs_ref[...], w_ref[512:1024, :])
        m_ref[...] = m
        mn, _ = _rms(m)
        ho_ref[...] = h_ref[...] + mn * g_ref[...]

    row_f32 = _row_spec(tm, D_MODEL)
    return pl.pallas_call(
        body, name="mix_out", grid=(rows // tm,),
        in_specs=[row_f32, _row_spec(tm, 512), _row_spec(tm, 512), VMEM_SPEC, VMEM_SPEC],
        out_specs=[row_f32, row_f32],
        out_shape=[jax.ShapeDtypeStruct((rows, D_MODEL), F32), jax.ShapeDtypeStruct((rows, D_MODEL), F32)],
        compiler_params=_params(("arbitrary",)),
    )(h, cat_g, cat_s, wout, gpost)


def _mix_out_bwd(dh, m, wout, gpost):
    rows = dh.shape[0]
    tm = _row_tile(rows)

    def body(dh_ref, m_ref, w_ref, g_ref, dcg_ref, dcs_ref, dm_ref, dg_ref):
        first = pl.program_id(0) == 0
        dhv = dh_ref[...]
        mn, rm = _rms(m_ref[...])
        _acc_add(dg_ref, first, _colsum(dhv * mn))
        dm16 = _rms_bwd(mn, rm, g_ref[...], dhv).astype(BF16)
        dm_ref[...] = dm16
        dcat = _nt(dm16, w_ref[...])
        dcg_ref[...] = dcat[:, 0:512]
        dcs_ref[...] = dcat[:, 512:1024]

    row_f32 = _row_spec(tm, D_MODEL)
    return pl.pallas_call(
        body, name="mix_out_bwd", grid=(rows // tm,),
        in_specs=[row_f32, row_f32, VMEM_SPEC, VMEM_SPEC],
        out_specs=[_row_spec(tm, 512), _row_spec(tm, 512), row_f32, _acc_spec(D_MODEL)],
        out_shape=[jax.ShapeDtypeStruct((rows, 512), F32), jax.ShapeDtypeStruct((rows, 512), F32),
                   jax.ShapeDtypeStruct((rows, D_MODEL), BF16), jax.ShapeDtypeStruct((8, D_MODEL), F32)],
        compiler_params=_params(("arbitrary",)),
    )(dh, m, wout, gpost)


def _mix_in_bwd(dh_out, h, g, win_p, wa2_p, cos, sin, loga, ga, dgq, dgk, dgv, dgg, dsq, dsk, dsv, dloga):
    rows = h.shape[0]
    tm = _row_tile(rows)

    def body(dho_ref, h_ref, g_ref, win_ref, wa2_ref, cos_ref, sin_ref, loga_ref, ga_ref,
             dgq_ref, dgk_ref, dgv_ref, dgg_ref, dsq_ref, dsk_ref, dsv_ref, dla_ref,
             dh_ref, dproj_ref, dwa2_ref, dg_ref, dba_ref):
        first = pl.program_id(0) == 0
        dz = dla_ref[...] * (1.0 / GLA_TAU) * (1.0 - jnp.exp(GLA_TAU * loga_ref[...]))
        _acc_add(dba_ref, first, _colsum(dz))
        dga = _nt(dz, wa2_ref[...])
        pa = _tn(ga_ref[...], dz)
        c1, s1 = cos_ref[...], sin_ref[...]
        c4 = jnp.concatenate([c1, c1, c1, c1], axis=1)
        s4 = jnp.concatenate([s1, s1, s1, s1], axis=1)
        dq_r, dk_r = dsq_ref[...], dsk_ref[...]
        dsq = dq_r * c4 - _rot_half(dq_r * s4)
        dsk = dk_r * c1 - _rot_half(dk_r * s1)
        dproj16 = jnp.concatenate(
            [dgq_ref[...], dgk_ref[...], dgv_ref[...], dgg_ref[...], dsq, dsk, dsv_ref[...], dga], axis=1).astype(BF16)
        dproj_ref[...] = dproj16
        dn = _nn(dproj16, win_ref[...])

        @pl.when(first)
        def _():
            dwa2_ref[...] = pa

        @pl.when(jnp.logical_not(first))
        def _():
            dwa2_ref[...] += pa
        hn, rh = _rms(h_ref[...])
        _acc_add(dg_ref, first, _colsum(dn * hn))
        dh_ref[...] = dho_ref[...] + _rms_bwd(hn, rh, g_ref[...], dn)

    rs = lambda c: _row_spec(tm, c)
    return pl.pallas_call(
        body, name="mix_in_bwd", grid=(rows // tm,),
        in_specs=[rs(D_MODEL), rs(D_MODEL), VMEM_SPEC, VMEM_SPEC, VMEM_SPEC, rs(128), rs(128), rs(256), rs(128),
                  rs(256), rs(256), rs(512), rs(512), rs(512), rs(128), rs(128), rs(256)],
        out_specs=[rs(D_MODEL), rs(P_END), pl.BlockSpec((128, 256), lambda i: (0, 0)), _acc_spec(D_MODEL), _acc_spec(256)],
        out_shape=[jax.ShapeDtypeStruct((rows, D_MODEL), F32), jax.ShapeDtypeStruct((rows, P_END), BF16),
                   jax.ShapeDtypeStruct((128, 256), F32), jax.ShapeDtypeStruct((8, D_MODEL), F32),
                   jax.ShapeDtypeStruct((8, 256), F32)],
        compiler_params=_params(("arbitrary",)),
    )(dh_out, h, g, win_p, wa2_p, cos, sin, loga, ga, dgq, dgk, dgv, dgg, dsq, dsk, dsv, dloga)


def _rope_tables(rows):
    pos = (jnp.arange(rows, dtype=jnp.int32) - PAD_ROWS).astype(F32)
    inv_freq = 1.0 / (ROPE_THETA ** (jnp.arange(0, SWA_HD, 2, dtype=F32) / SWA_HD))
    ang = pos[:, None] * inv_freq[None, :]
    return jnp.tile(jnp.cos(ang), (1, 4)), jnp.tile(jnp.sin(ang), (1, 4))


def _local_step(x, tgt, front, w, late_weights=None, on_grads=None, on_small=None):
    cos, sin = _rope_tables(x.shape[0] + BLK)
    g = {}

    def tell(group, names):
        for nm in names:
            g[nm] = grads_now[nm]
        return None if on_grads is None else on_grads(group, {nm: grads_now[nm] for nm in names})

    h0, h1, a1, b1, s1, f1 = _ffn_fwd(x, w["ffn1_pre"], w["wg1"], w["wu1"], w["wd1"], w["ffn1_post"], front=front)
    if late_weights is not None:
        w = {**w, **late_weights("win", f1)}
    gq, gk, gv, gg, sq, sk, sv, ga, loga, bc, n2 = _mix_in(h1, w["mix_pre"], w["win"], w["wa2"], w["b_a"], cos, sin)
    o_g, cat_g, sp = _gla_fwd(gq, gk, gv, gg, bc, w["gla_norm"])
    o_s, cat_s, lse = _swa_fwd(sq, sk, sv, w["sinks"], w["swa_norm"])
    if late_weights is not None:
        w = {**w, **late_weights("rest", lse)}
    h2, m = _mix_out(h1, cat_g, cat_s, w["wout"], w["mix_post"])
    h3, a2, b2, s2, f2, dy, loss = _ffn_fwd(h2, w["ffn2_pre"], w["wg2"], w["wu2"], w["wd2"], w["ffn2_post"], tgt)
    del h3
    dh2, da, db, df, n3, g["ffn2_pre"], g["ffn2_post"] = _ffn_bwd_act(
        dy, h2, a2, b2, f2, w["ffn2_pre"], w["ffn2_post"], w["wg2"], w["wu2"], w["wd2"], "ffn2_bwd_act")
    grads_now = dict(wd2=_wgrad(s2, df, "ffn2_wgrad_down"), wg2=_wgrad(da, n3, "ffn2_wgrad_gate"),
                     wu2=_wgrad(db, n3, "ffn2_wgrad_up"))
    tok = tell("ffn2", ("wd2", "wg2", "wu2"))
    dcg, dcs, dm, g["mix_post"] = _mix_out_bwd(dh2, m, w["wout"], w["mix_post"] + (0.0 if tok is None else tok[0, 0]))
    dsq, dsk, dsv, g["sinks"], g["swa_norm"] = _swa_bwd(dcs, o_s, sq, sk, sv, lse, w["sinks"], w["swa_norm"])
    dgq, dgk, dgv, dgg, dloga, g["gla_norm"] = _gla_bwd(dcg, o_g, gq, gk, gv, gg, bc, sp, w["gla_norm"])
    dh1, dproj, g["wa2"], g["mix_pre"], g["b_a"] = _mix_in_bwd(
        dh2, h1, w["mix_pre"], w["win"], w["wa2"], cos, sin, loga, ga, dgq, dgk, dgv, dgg, dsq, dsk, dsv, dloga)
    dh0, da, db, df, n1, g["ffn1_pre"], g["ffn1_post"] = _ffn_bwd_act(
        dh1, h0, a1, b1, f1, w["ffn1_pre"], w["ffn1_post"], w["wg1"], w["wu1"], w["wd1"], "ffn1_bwd_act")
    tok = None if on_small is None else on_small(loss[0, 0], dh0, g)
    grads_now = dict(wd1=_wgrad(s1, df, "ffn1_wgrad_down", after=tok))
    tok = tell("ffn1_down", ("wd1",))
    grads_now = dict(wg1=_wgrad(da, n1, "ffn1_wgrad_gate", after=tok))
    tok = tell("ffn1_gate", ("wg1",))
    grads_now = dict(wu1=_wgrad(db, n1, "ffn1_wgrad_up", after=tok))
    tok = tell("ffn1_up", ("wu1",))
    grads_now = dict(win=_wgrad(dproj, n2, "win_wgrad", after=tok),
                     wout=jnp.concatenate([_wgrad(cat_g, dm, "wout_wgrad_gla", after=tok),
                                           _wgrad(cat_s, dm, "wout_wgrad_swa", after=tok)], axis=0))
    tell("mix", ("wout", "win"))
    return loss[0, 0], dh0, g


def _win_pad_rows(win_t):
    pad = jnp.zeros((P_END - P_GA - 16, win_t.shape[1]), win_t.dtype)
    return jnp.concatenate([win_t[0:1536], win_t[1552:2320], win_t[1536:1552], pad], axis=0)


def _win_unpad_rows(win_p):
    return jnp.concatenate([win_p[0:1536], win_p[P_GA:P_GA + 16], win_p[1536:P_GA]], axis=0)


def _place_on_mesh():
    return lax.axis_index("x"), lax.axis_index("y"), lax.axis_index("c")


def _dev_index(px, py, pc):
    return 4 * px + 2 * py + pc


def _other_devices(x, y, c):
    flip = lambda v, f: 1 - v if f else v
    return [(flip(x, fx), flip(y, fy), flip(c, fc)) for fx in (0, 1) for fy in (0, 1) for fc in (0, 1)][1:]


def _all_gather(shards):
    n = len(shards)

    def body(*refs):
        ins, outs = refs[:n], refs[n:2 * n]
        zeros_ref, send_sems, recv_sems, local_sems = refs[2 * n:]
        zeros_ref[...] = jnp.zeros_like(zeros_ref)
        x, y, c = _place_on_mesh()
        me, sibling = (x, y, c), (x, y, 1 - c)
        chips = [(1 - x, y), (x, 1 - y), (1 - x, 1 - y)]

        def rows(k, px, py, pc):
            r = ins[k].shape[0]
            return outs[k].at[pl.ds(pl.multiple_of(_dev_index(px, py, pc) * r, 8), r), :]

        def copy(k, slot, block, to, src=None):
            return pltpu.make_async_remote_copy(
                src_ref=rows(k, *block) if src is None else src, dst_ref=rows(k, *block),
                send_sem=send_sems.at[k, slot], recv_sem=recv_sems.at[k, slot], device_id=to, device_id_type=MESH)

        local = [pltpu.make_async_copy(ins[k], rows(k, *me), local_sems.at[k]) for k in range(n)]
        sends = []
        for k in range(n):
            local[k].start()
            sends.append(copy(k, 0, me, sibling, src=ins[k]))
            sends += [copy(k, 1 + j, me, (*chip, c), src=ins[k]) for j, chip in enumerate(chips)]
        for cp in sends:
            cp.start()
        for k in range(n):
            for j, chip in enumerate(chips):
                copy(k, 1 + j, (*chip, c), me).wait_recv()
                passed = copy(k, 4 + j, (*chip, c), sibling)
                passed.start()
                sends.append(passed)
        for k in range(n):
            copy(k, 0, sibling, me).wait_recv()
            for j, chip in enumerate(chips):
                copy(k, 4 + j, (*chip, 1 - c), me).wait_recv()
        for cp in sends:
            cp.wait_send()
        for cp in local:
            cp.wait()

    return pl.pallas_call(
        body, name="all_gather_weights",
        in_specs=[ANY_SPEC] * n, out_specs=[ANY_SPEC] * n + [VMEM_SPEC],
        out_shape=[jax.ShapeDtypeStruct((N_DEV * s.shape[0], s.shape[1]), s.dtype) for s in shards]
        + [jax.ShapeDtypeStruct((8, 128), F32)],
        scratch_shapes=[pltpu.SemaphoreType.DMA((n, 7)), pltpu.SemaphoreType.DMA((n, 7)), pltpu.SemaphoreType.DMA((n,))],
    )(*shards)


HBM_SPEC = pl.BlockSpec(memory_space=pltpu.HBM)
SEM_SPEC = pl.BlockSpec(memory_space=pltpu.SEMAPHORE)
DATAFLOW = pltpu.SideEffectType.DATAFLOW_SIDE_EFFECTING


GATHER, SCATTER, SCATTER_CHIPS = "gather", "scatter", "scatter among chips"


def _exchange_peers(kind):
    x, y, c = _place_on_mesh()
    if kind == SCATTER_CHIPS:
        peers = [(1 - x, y, c), (x, 1 - y, c), (1 - x, 1 - y, c)]
        return peers, [2 * p[0] + p[1] for p in peers], 2 * x + y, 4
    peers = _other_devices(x, y, c)
    return peers, [_dev_index(*p) for p in peers], _dev_index(x, y, c), N_DEV


def _exchange_copies(srcs, lands, send_sems, recv_sems, own_sems, kind, arriving):
    peers, theirs, me, blocks = _exchange_peers(kind)
    remote, local = [], []
    for k, (src, land) in enumerate(zip(srcs, lands)):
        r = land.shape[0] // blocks

        def block(ref, d):
            return ref.at[pl.ds(pl.multiple_of(d * r, 8), r), :]

        for f, (peer, him) in enumerate(zip(peers, theirs)):
            mine, his = (him, me) if arriving else (me, him)
            sem = len(peers) * k + f
            remote.append(pltpu.make_async_remote_copy(
                src_ref=src if kind == GATHER else block(src, his), dst_ref=block(land, mine),
                send_sem=send_sems.at[sem], recv_sem=recv_sems.at[sem], device_id=peer, device_id_type=MESH))
        local.append(pltpu.make_async_copy(src if kind == GATHER else block(src, me), block(land, me), own_sems.at[k]))
    return remote, local


def _exchange_start(srcs, kind, name):
    n = len(srcs)
    lands = [lax.empty((N_DEV * s.shape[0], s.shape[1]) if kind == GATHER else s.shape, s.dtype) for s in srcs]
    sems = (3 if kind == SCATTER_CHIPS else 7) * n

    def body(*refs):
        remote, local = _exchange_copies(refs[:n], refs[n:2 * n], *refs[2 * n:2 * n + 3], kind, False)
        for cp in remote + local:
            cp.start()
        refs[-1][...] = jnp.zeros_like(refs[-1])

    both = list(srcs) + list(lands)
    outs = pl.pallas_call(
        body, name=name,
        out_shape=(pltpu.SemaphoreType.DMA((sems,)), pltpu.SemaphoreType.DMA((sems,)), pltpu.SemaphoreType.DMA((n,)),
                   *[pltpu.HBM(a.shape, a.dtype) for a in both], jax.ShapeDtypeStruct((8, 128), F32)),
        in_specs=[HBM_SPEC] * (2 * n), out_specs=(SEM_SPEC, SEM_SPEC, SEM_SPEC, *[HBM_SPEC] * (2 * n), VMEM_SPEC),
        input_output_aliases={i: 3 + i for i in range(2 * n)},
        compiler_params=pltpu.CompilerParams(has_side_effects=DATAFLOW),
    )(*[pltpu.with_memory_space_constraint(a, pltpu.HBM) for a in both])
    return outs[0:3], outs[3:3 + n], outs[3 + n:3 + 2 * n], outs[-1]


def _exchange_wait(started, kind, after, name):
    sems, srcs, lands, _ = started
    n = len(srcs)

    def body(*refs):
        args = (refs[:n], refs[n:2 * n], *refs[2 * n:2 * n + 3], kind)
        going, local = _exchange_copies(*args, False)
        for cp in going:
            cp.wait_send()
        for cp in local:
            cp.wait()
        for cp in _exchange_copies(*args, True)[0]:
            cp.wait_recv()

    both = list(srcs) + list(lands)
    outs = pl.pallas_call(
        body, name=name, out_shape=[pltpu.HBM(a.shape, a.dtype) for a in both],
        in_specs=[HBM_SPEC] * (2 * n) + [SEM_SPEC, SEM_SPEC, SEM_SPEC, ANY_SPEC], out_specs=[HBM_SPEC] * (2 * n),
        input_output_aliases={i: i for i in range(2 * n)},
        compiler_params=pltpu.CompilerParams(has_side_effects=DATAFLOW),
    )(*both, *sems, after)
    return outs[n:]


def _sibling_reduce(part, name):
    r, cols = part.shape[0] // N_DEV, part.shape[1]

    def swap(p_ref, got_ref, send_sems, recv_sems):
        x, y, c = _place_on_mesh()
        copies = [pltpu.make_async_remote_copy(
            src_ref=p_ref.at[pl.ds(pl.multiple_of((2 * j + 1 - c) * r, 8), r), :], dst_ref=got_ref.at[pl.ds(j * r, r), :],
            send_sem=send_sems.at[j], recv_sem=recv_sems.at[j], device_id=(x, y, 1 - c), device_id_type=MESH)
            for j in range(4)]
        for cp in copies:
            cp.start()
        for cp in copies:
            cp.wait()

    got = pl.pallas_call(
        swap, name=name + "_swap", in_specs=[ANY_SPEC], out_specs=ANY_SPEC,
        out_shape=jax.ShapeDtypeStruct((4 * r, cols), part.dtype),
        scratch_shapes=[pltpu.SemaphoreType.DMA((4,)), pltpu.SemaphoreType.DMA((4,))],
    )(part)

    def add(c_ref, mine_ref, got_ref, o_ref):
        del c_ref
        o_ref[...] = (mine_ref[...].astype(F32) + got_ref[...].astype(F32)).astype(o_ref.dtype)

    core = lax.axis_index("c").astype(jnp.int32).reshape(1)
    return pl.pallas_call(
        add, name=name + "_add",
        grid_spec=pltpu.PrefetchScalarGridSpec(
            num_scalar_prefetch=1, grid=(4,),
            in_specs=[pl.BlockSpec((r, cols), lambda j, c_ref: (2 * j + c_ref[0], 0)),
                      pl.BlockSpec((r, cols), lambda j, c_ref: (j, 0))],
            out_specs=pl.BlockSpec((r, cols), lambda j, c_ref: (j, 0))),
        out_shape=jax.ShapeDtypeStruct((4 * r, cols), part.dtype),
        compiler_params=_params(("arbitrary",)),
    )(core, part, got)


def _sum_partials(parts, name, blocks=N_DEV):
    n = len(parts)

    def body(*refs):
        ins, outs = refs[:n], refs[n:]
        first = pl.program_id(0) == 0
        for i_ref, o_ref in zip(ins, outs):
            v = i_ref[...].astype(F32)

            @pl.when(first)
            def _():
                o_ref[...] = v

            @pl.when(jnp.logical_not(first))
            def _():
                o_ref[...] += v

    shapes = [(p.shape[0] // blocks, p.shape[1]) for p in parts]
    return pl.pallas_call(
        body, name=name, grid=(blocks,),
        in_specs=[pl.BlockSpec(s, lambda j: (j, 0)) for s in shapes],
        out_specs=[pl.BlockSpec(s, lambda j: (0, 0)) for s in shapes],
        out_shape=[jax.ShapeDtypeStruct(s, F32) for s in shapes],
        compiler_params=_params(("arbitrary",)),
    )(*parts)


def _adamw_update(w, g, m, v):
    m = ADAM_B1 * m + (1.0 - ADAM_B1) * g
    v = ADAM_B2 * v + (1.0 - ADAM_B2) * (g * g)
    m_hat = m * (1.0 / (1.0 - ADAM_B1 ** ADAM_STEP))
    v_hat = v * (1.0 / (1.0 - ADAM_B2 ** ADAM_STEP))
    return -ADAM_LR * (m_hat / (jnp.sqrt(v_hat) + ADAM_EPS) + ADAM_WD * w), m, v


def _sum_adamw(parts, w, m, v, blocks, name):
    shape = w.shape

    def body(p_ref, w_ref, m_ref, v_ref, g_ref, d_ref, mo_ref, vo_ref):
        j = pl.program_id(0)
        part = p_ref[...].astype(F32)

        @pl.when(j == 0)
        def _():
            g_ref[...] = part

        @pl.when(j > 0)
        def _():
            g_ref[...] += part

        @pl.when(j == blocks - 1)
        def _():
            d_ref[...], mo_ref[...], vo_ref[...] = _adamw_update(w_ref[...], g_ref[...], m_ref[...], v_ref[...])

    held = pl.BlockSpec(shape, lambda j: (0, 0))
    return pl.pallas_call(
        body, name=name, grid=(blocks,),
        in_specs=[pl.BlockSpec(shape, lambda j: (j, 0)), held, held, held],
        out_specs=[held] * 4, out_shape=[jax.ShapeDtypeStruct(shape, F32)] * 4,
        compiler_params=_params(("arbitrary",)),
    )(parts, w, m, v)


def _adamw(ws, gs, ms, vs, name):
    n = len(ws)

    def body(*refs):
        w_r, g_r, m_r, v_r = refs[:n], refs[n:2 * n], refs[2 * n:3 * n], refs[3 * n:4 * n]
        d_o, m_o, v_o = refs[4 * n:5 * n], refs[5 * n:6 * n], refs[6 * n:7 * n]
        for k in range(n):
            d_o[k][...], m_o[k][...], v_o[k][...] = _adamw_update(w_r[k][...], g_r[k][...], m_r[k][...], v_r[k][...])

    shapes = [jax.ShapeDtypeStruct(w.shape, F32) for w in ws]
    outs = pl.pallas_call(
        body, name=name, in_specs=[VMEM_SPEC] * (4 * n), out_specs=[VMEM_SPEC] * (3 * n), out_shape=shapes * 3,
        compiler_params=pltpu.CompilerParams(vmem_limit_bytes=56 << 20),
    )(*ws, *gs, *ms, *vs)
    return outs[:n], outs[n:2 * n], outs[2 * n:]


WEIGHT_NAMES = ("meta_tokens", "ffn1_pre_norm", "ffn1_w_gate", "ffn1_w_up", "ffn1_w_down", "ffn1_post_norm", "mix_pre_norm",
                "w_in", "gla_w_a2", "gla_b_a", "gla_out_norm", "swa_sinks", "swa_out_norm", "w_out", "mix_post_norm",
                "ffn2_pre_norm", "ffn2_w_gate", "ffn2_w_up", "ffn2_w_down", "ffn2_post_norm")
WIN_SHARD = D_IN // N_DEV
WIN_SHARD_PAD = 304
SLAB_VECTORS = ("ffn1_pre", "ffn1_post", "mix_pre", "mix_post", "ffn2_pre", "ffn2_post")
SLAB_ROWS = 32


def kernel(x, meta_tokens, ffn1_pre_norm, ffn1_w_gate, ffn1_w_up, ffn1_w_down, ffn1_post_norm, mix_pre_norm, w_in, gla_w_a2, gla_b_a, gla_out_norm, swa_sinks, swa_out_norm, w_out, mix_post_norm, ffn2_pre_norm, ffn2_w_gate, ffn2_w_up, ffn2_w_down, ffn2_post_norm, loss_target, m_meta_tokens, m_ffn1_pre_norm, m_ffn1_w_gate, m_ffn1_w_up, m_ffn1_w_down, m_ffn1_post_norm, m_mix_pre_norm, m_w_in, m_gla_w_a2, m_gla_b_a, m_gla_out_norm, m_swa_sinks, m_swa_out_norm, m_w_out, m_mix_post_norm, m_ffn2_pre_norm, m_ffn2_w_gate, m_ffn2_w_up, m_ffn2_w_down, m_ffn2_post_norm, v_meta_tokens, v_ffn1_pre_norm, v_ffn1_w_gate, v_ffn1_w_up, v_ffn1_w_down, v_ffn1_post_norm, v_mix_pre_norm, v_w_in, v_gla_w_a2, v_gla_b_a, v_gla_out_norm, v_swa_sinks, v_swa_out_norm, v_w_out, v_mix_post_norm, v_ffn2_pre_norm, v_ffn2_w_gate, v_ffn2_w_up, v_ffn2_w_down, v_ffn2_post_norm):
    given = dict(locals())
    W = {n: given[n] for n in WEIGHT_NAMES}
    M = {n: given["m_" + n] for n in WEIGHT_NAMES}
    V = {n: given["v_" + n] for n in WEIGHT_NAMES}
    dev = _dev_index(*_place_on_mesh())

    def t16(w):
        return w[0].T.astype(BF16)

    small = jnp.concatenate([W["meta_tokens"], jnp.pad(W["gla_w_a2"][0], ((0, 0), (0, 96)))], axis=0)
    wg1, wu1, wd1, small_g, gathered_zeros = _all_gather(
        [t16(W["ffn1_w_gate"]), t16(W["ffn1_w_up"]), W["ffn1_w_down"][0].astype(BF16), small])
    def after_zero(shard, zeros):
        return shard + zeros[0:1, 0:1].astype(shard.dtype)
    win_shard = jnp.pad(t16(W["w_in"]), ((0, WIN_SHARD_PAD - WIN_SHARD), (0, 0)))
    win_shard = after_zero(win_shard, gathered_zeros)
    mid = _exchange_start([win_shard], GATHER, "gather_w_in_start")
    late_shards = [after_zero(W["w_out"][0].astype(BF16), mid[3]), t16(W["ffn2_w_gate"]), t16(W["ffn2_w_up"]),
                   W["ffn2_w_down"][0].astype(BF16)]
    late = _exchange_start(late_shards, GATHER, "gather_late_weights_start")

    def late_weights(what, after):
        if what == "win":
            win_g, = _exchange_wait(mid, GATHER, after, "gather_w_in_wait")
            win_t = win_g.reshape(N_DEV, WIN_SHARD_PAD, D_MODEL)[:, :WIN_SHARD].reshape(D_IN, D_MODEL)
            return dict(win=_win_pad_rows(win_t))
        wout, wg2, wu2, wd2 = _exchange_wait(late, GATHER, after, "gather_late_weights_wait")
        return dict(wout=wout, wg2=wg2, wu2=wu2, wd2=wd2)

    small_g = small_g.reshape(N_DEV, 32, 128)
    meta_full = small_g[:, :N_META].transpose(1, 0, 2).reshape(N_META, D_MODEL)
    wa2_full = small_g[:, N_META:, :32].transpose(1, 0, 2).reshape(16, 256)
    w = dict(
        ffn1_pre=W["ffn1_pre_norm"] + late[3][0, 0], ffn1_post=W["ffn1_post_norm"], mix_pre=W["mix_pre_norm"],
        mix_post=W["mix_post_norm"], ffn2_pre=W["ffn2_pre_norm"], ffn2_post=W["ffn2_post_norm"], b_a=W["gla_b_a"],
        gla_norm=W["gla_out_norm"], sinks=W["swa_sinks"], swa_norm=W["swa_out_norm"], wg1=wg1, wu1=wu1, wd1=wd1,
        wa2=jnp.pad(wa2_full, ((0, 112), (0, 0))))

    in_flight = []

    def on_grads(group, grads):
        parts = []
        for nm, p in grads.items():
            if nm == "win":
                p = _win_unpad_rows(p).reshape(N_DEV, WIN_SHARD, D_MODEL)
                p = jnp.pad(p, ((0, 0), (0, WIN_SHARD_PAD - WIN_SHARD), (0, 0))).reshape(N_DEV * WIN_SHARD_PAD, D_MODEL)
            parts.append(p)
        kind = SCATTER if group == "ffn2" else SCATTER_CHIPS
        if kind == SCATTER_CHIPS:
            parts = [_sibling_reduce(p, "pair_" + group + "_" + nm) for nm, p in zip(grads, parts)]
        started = _exchange_start(parts, kind, "scatter_" + group + "_start")
        in_flight.append((group, list(grads), started, kind))
        return started[3]

    small_flight = []

    def on_small(loss, dh0, g):
        packed = jnp.concatenate([g["b_a"][0:1], g["gla_norm"][0:1], g["sinks"][0:1], g["swa_norm"][0:1]], axis=1)
        slab = jnp.concatenate([g[k][0:1] for k in SLAB_VECTORS] + [packed, jnp.full((1, D_MODEL), loss, F32),
                               g["wa2"][:16].reshape(4, D_MODEL), jnp.zeros((4, D_MODEL), F32), dh0[PAD_ROWS:BLK]], axis=0)
        small_flight.append(_exchange_start([slab], GATHER, "gather_small_grads_start"))
        return small_flight[0][3]

    front = jnp.concatenate([jnp.zeros((PAD_ROWS, D_MODEL), F32), meta_full], axis=0)
    loss, dh0, g = _local_step(x[0], loss_target[0], front, w, late_weights, on_grads, on_small)
    grad_x = dh0[BLK:][None]

    land, = _exchange_wait(small_flight[0], GATHER, in_flight[-1][2][3], "gather_small_grads_wait")
    tot = _sum_partials([land], "sum_small_grads")[0]
    loss = tot[7, 0]
    small_grads = dict(
        ffn1_pre_norm=tot[0:1], ffn1_post_norm=tot[1:2], mix_pre_norm=tot[2:3], mix_post_norm=tot[3:4],
        ffn2_pre_norm=tot[4:5], ffn2_post_norm=tot[5:6], gla_b_a=tot[6:7, 0:256], gla_out_norm=tot[6:7, 256:384],
        swa_sinks=tot[6:7, 384:392], swa_out_norm=tot[6:7, 512:1024],
        gla_w_a2=lax.dynamic_slice_in_dim(tot[8:12].reshape(16, 256), dev * 32, 32, axis=1)[None],
        meta_tokens=lax.dynamic_slice_in_dim(tot[16:32], dev * 128, 128, axis=1))

    big = dict(wg1=("ffn1_w_gate", True), wu1=("ffn1_w_up", True), wd1=("ffn1_w_down", False), win=("w_in", True),
               wout=("w_out", False), wg2=("ffn2_w_gate", True), wu2=("ffn2_w_up", True), wd2=("ffn2_w_down", False))
    grads = dict(small_grads)
    delta, new_m, new_v = {}, {}, {}
    names = [n for n in WEIGHT_NAMES if n not in [full for full, _ in big.values()]]
    two_d = lambda a: a.reshape(-1, a.shape[-1])
    d_, m_, v_ = _adamw([two_d(W[n]) for n in names], [two_d(grads[n]) for n in names],
                        [two_d(M[n]) for n in names], [two_d(V[n]) for n in names], "adamw_small")
    for k, n in enumerate(names):
        delta[n], new_m[n], new_v[n] = d_[k].reshape(W[n].shape), m_[k].reshape(W[n].shape), v_[k].reshape(W[n].shape)

    before_wait = d_[0] + in_flight[-1][2][3][0, 0]
    for group, shorts, started, kind in in_flight:
        lands = _exchange_wait(started, kind, before_wait, "scatter_" + group + "_wait")
        blocks = 4 if kind == SCATTER_CHIPS else N_DEV
        for short, land in zip(shorts, lands):
            n, transposed = big[short]
            to_slab = (lambda a: a[0].T) if transposed else (lambda a: a[0])
            from_slab = (lambda a: a.T[None]) if transposed else (lambda a: a[None])
            if short == "win":
                g_slab = _sum_partials([land], "sum_" + n, blocks)[0][:WIN_SHARD]
                d_, m_, v_ = _adamw([to_slab(W[n])], [g_slab], [to_slab(M[n])], [to_slab(V[n])], "adamw_" + n)
                d_, m_, v_ = d_[0], m_[0], v_[0]
            else:
                g_slab, d_, m_, v_ = _sum_adamw(land, to_slab(W[n]), to_slab(M[n]), to_slab(V[n]), blocks, "adamw_" + n)
            grads[n], delta[n], new_m[n], new_v[n] = from_slab(g_slab), from_slab(d_), from_slab(m_), from_slab(v_)
            before_wait = d_
    return (loss, grad_x, *[grads[n] for n in WEIGHT_NAMES], *[delta[n] for n in WEIGHT_NAMES],
            *[new_m[n] for n in WEIGHT_NAMES], *[new_v[n] for n in WEIGHT_NAMES])
```

```python
import math

import jax
import jax.numpy as jnp
from jax import lax
from jax.experimental import pallas as pl
from jax.experimental.pallas import tpu as pltpu

F32, BF16 = jnp.float32, jnp.bfloat16

D_MODEL = 1024
D_FF = 2816
N_META = 16
BLK = 128
PAD_ROWS = BLK - N_META
GLA_DK = 64
SWA_HD = 64
SWA_HEADS = 8
GLA_TAU = 16.0
NORM_EPS = 1e-6
NEG_INF = -1e30
ROPE_THETA = 10000.0
P_GQ, P_GK, P_GV, P_GG, P_SQ, P_SK, P_SV, P_GA, P_END = 0, 256, 512, 1024, 1536, 2048, 2176, 2304, 2432
D_IN = 2320
IN_SPLITS = (256, 256, 512, 512, 16, 512, 128, 128)
FF_TILE = 2816
WGRAD_TILE_MAX = 2432
N_DEV = 8
MESH = pl.DeviceIdType.MESH

ADAM_LR, ADAM_B1, ADAM_B2, ADAM_EPS, ADAM_WD, ADAM_STEP = 0.001, 0.9, 0.999, 1e-08, 0.01, 10

V7X_VMEM_BYTES = 64 << 20
VMEM_SPEC = pl.BlockSpec(memory_space=pltpu.VMEM)
SMEM_SPEC = pl.BlockSpec(memory_space=pltpu.SMEM)
ANY_SPEC = pl.BlockSpec(memory_space=pl.ANY)


def _params(semantics, vmem_mb=56):
    return pltpu.CompilerParams(dimension_semantics=semantics, vmem_limit_bytes=vmem_mb << 20)


def _row_tile(rows):
    return 416 if rows % 416 == 0 else BLK


def _nn(a, b):
    return lax.dot_general(a, b, (((1,), (0,)), ((), ())), preferred_element_type=F32)


def _nt(a, b):
    return lax.dot_general(a, b, (((1,), (1,)), ((), ())), preferred_element_type=F32)


def _tn(a, b):
    return lax.dot_general(a, b, (((0,), (0,)), ((), ())), preferred_element_type=F32)


def _rms(x):
    r = lax.rsqrt(jnp.mean(x * x, axis=-1, keepdims=True) + NORM_EPS)
    return x * r, r


def _rms_bwd(xn, r, w, dy):
    g = dy * w
    return r * (g - xn * jnp.mean(g * xn, axis=-1, keepdims=True))


def _sigmoid(x):
    return 1.0 / (1.0 + jnp.exp(-x))


def _colsum(x):
    return jnp.sum(x, axis=0, keepdims=True)


def _split_bf16(x):
    hi = x.astype(BF16)
    lo = (x - hi.astype(F32)).astype(BF16)
    return hi, lo


def _tri(lower):
    r = lax.broadcasted_iota(jnp.int32, (BLK, BLK), 0)
    c = lax.broadcasted_iota(jnp.int32, (BLK, BLK), 1)
    return (r >= c) if lower else (c >= r)


def _half_mask(width, half):
    lane = lax.broadcasted_iota(jnp.int32, (1, width), 1)
    return ((lane % 128) < 64) if half == 0 else ((lane % 128) >= 64)


def _rot_half(x):
    w = x.shape[-1]
    lane = lax.broadcasted_iota(jnp.int32, (1, w), 1)
    return jnp.where((lane % SWA_HD) < SWA_HD // 2, -pltpu.roll(x, w - SWA_HD // 2, 1), pltpu.roll(x, SWA_HD // 2, 1))


def _row_spec(tm, cols):
    return pl.BlockSpec((tm, cols), lambda i: (i, 0))


def _acc_spec(cols):
    return pl.BlockSpec((8, cols), lambda i: (0, 0))


def _acc_add(ref, first, value):
    @pl.when(first)
    def _():
        ref[...] = jnp.zeros_like(ref)
    ref[0:1, :] += value


def _behind_front(ref, i, tm, front):
    blk = ref[...]
    return jnp.where(i == 0, jnp.concatenate([front, blk[0:tm - BLK]], axis=0), blk)


def _ffn_fwd(h, gpre, wg_t, wu_t, wd, gpost, tgt=None, front=None):
    with_loss, with_front = tgt is not None, front is not None
    rows = h.shape[0] + (BLK if with_front else 0)
    tm = _row_tile(rows)
    nf = D_FF // FF_TILE

    def body(*refs):
        refs = list(refs)
        h_ref, gpre_ref, wg_ref, wu_ref, wd_ref, gpost_ref = refs[:6]
        del refs[:6]
        front_ref = refs.pop(0) if with_front else None
        t_ref = refs.pop(0) if with_loss else None
        h0_ref = refs.pop(0) if with_front else None
        ho_ref, a_ref, b_ref, s_ref, f_ref = refs[:5]
        dy_ref, loss_ref = refs[5:7] if with_loss else (None, None)
        acc = refs[-1]
        i = pl.program_id(0)
        if with_front:
            h_in = _behind_front(h_ref, i, tm, front_ref[...])
            h0_ref[...] = h_in
        else:
            h_in = h_ref[...]
        hn, _ = _rms(h_in)
        n16 = (hn * gpre_ref[...]).astype(BF16)
        for j in range(nf):
            cols = slice(j * FF_TILE, (j + 1) * FF_TILE)
            a = _nt(n16, wg_ref[cols, :])
            b = _nt(n16, wu_ref[cols, :])
            a_ref[:, cols] = a.astype(BF16)
            b_ref[:, cols] = b.astype(BF16)
            s16 = (a * _sigmoid(a) * b).astype(BF16)
            s_ref[:, cols] = s16
            part = _nn(s16, wd_ref[cols, :])
            if j == 0:
                acc[...] = part
            else:
                acc[...] += part
        f = acc[...]
        f_ref[...] = f
        fn, _ = _rms(f)
        y = h_in + 0.5 * (fn * gpost_ref[...])
        ho_ref[...] = y
        if with_loss:
            row = i * tm + lax.broadcasted_iota(jnp.int32, (tm, 1), 0)
            err = jnp.where(row >= BLK, y - _behind_front(t_ref, i, tm, jnp.zeros((BLK, D_MODEL), F32)), 0.0)
            dy_ref[...] = err * (1.0 / D_MODEL)
            part = 0.5 * jnp.sum(jnp.sum(err * err, axis=-1, keepdims=True) * (1.0 / D_MODEL), axis=0, keepdims=True)

            @pl.when(i == 0)
            def _():
                loss_ref[...] = jnp.zeros_like(loss_ref)
            loss_ref[...] += part

    row_f32 = _row_spec(tm, D_MODEL)
    behind = pl.BlockSpec((pl.Element(tm), pl.Element(D_MODEL)),
                          lambda i: (pl.multiple_of(jnp.maximum(i * tm - BLK, 0), math.gcd(tm, BLK)), 0))
    in_specs = [behind if with_front else row_f32, VMEM_SPEC, VMEM_SPEC, VMEM_SPEC, VMEM_SPEC, VMEM_SPEC]
    out_specs = [row_f32, _row_spec(tm, D_FF), _row_spec(tm, D_FF), _row_spec(tm, D_FF), row_f32]
    out_shape = [jax.ShapeDtypeStruct((rows, D_MODEL), F32), jax.ShapeDtypeStruct((rows, D_FF), BF16),
                 jax.ShapeDtypeStruct((rows, D_FF), BF16), jax.ShapeDtypeStruct((rows, D_FF), BF16),
                 jax.ShapeDtypeStruct((rows, D_MODEL), F32)]
    args = [h, gpre, wg_t, wu_t, wd, gpost]
    if with_front:
        in_specs.append(VMEM_SPEC)
        args.append(front)
        out_specs.insert(0, row_f32)
        out_shape.insert(0, jax.ShapeDtypeStruct((rows, D_MODEL), F32))
    if with_loss:
        in_specs.append(behind)
        args.append(tgt)
        out_specs += [row_f32, pl.BlockSpec((8, 128), lambda i: (0, 0))]
        out_shape += [jax.ShapeDtypeStruct((rows, D_MODEL), F32), jax.ShapeDtypeStruct((8, 128), F32)]
    return pl.pallas_call(
        body, name="ffn_fwd_loss" if with_loss else "ffn_fwd", grid=(rows // tm,),
        in_specs=in_specs, out_specs=out_specs, out_shape=out_shape,
        scratch_shapes=[pltpu.VMEM((tm, D_MODEL), F32)],
        compiler_params=_params(("arbitrary",)),
    )(*args)


def _ffn_bwd_act(dh_out, h, a, b, f, gpre, gpost, wg_t, wu_t, wd, name):
    rows = h.shape[0]
    tm = _row_tile(rows)
    nf = D_FF // FF_TILE

    def body(dho_ref, h_ref, a_ref, b_ref, f_ref, gpre_ref, gpost_ref, wg_ref, wu_ref, wd_ref,
             dh_ref, da_ref, db_ref, df_ref, n_ref, dgpre_ref, dgpost_ref, acc):
        first = pl.program_id(0) == 0
        dho = dho_ref[...]
        drr = 0.5 * dho
        fn, rf = _rms(f_ref[...])
        _acc_add(dgpost_ref, first, _colsum(drr * fn))
        df16 = _rms_bwd(fn, rf, gpost_ref[...], drr).astype(BF16)
        df_ref[...] = df16
        hn, rh = _rms(h_ref[...])
        n_ref[...] = (hn * gpre_ref[...]).astype(BF16)
        for j in range(nf):
            cols = slice(j * FF_TILE, (j + 1) * FF_TILE)
            ds = _nt(df16, wd_ref[cols, :])
            av = a_ref[:, cols].astype(F32)
            bv = b_ref[:, cols].astype(F32)
            sg = _sigmoid(av)
            db16 = (ds * (av * sg)).astype(BF16)
            da16 = (ds * bv * (sg * (1.0 + av * (1.0 - sg)))).astype(BF16)
            da_ref[:, cols] = da16
            db_ref[:, cols] = db16
            part = _nn(da16, wg_ref[cols, :]) + _nn(db16, wu_ref[cols, :])
            if j == 0:
                acc[...] = part
            else:
                acc[...] += part
        dn = acc[...]
        _acc_add(dgpre_ref, first, _colsum(dn * hn))
        dh_ref[...] = dho + _rms_bwd(hn, rh, gpre_ref[...], dn)

    row_f32 = _row_spec(tm, D_MODEL)
    row_ff = _row_spec(tm, D_FF)
    return pl.pallas_call(
        body, name=name, grid=(rows // tm,),
        in_specs=[row_f32, row_f32, row_ff, row_ff, row_f32, VMEM_SPEC, VMEM_SPEC, VMEM_SPEC, VMEM_SPEC, VMEM_SPEC],
        out_specs=[row_f32, row_ff, row_ff, row_f32, row_f32, _acc_spec(D_MODEL), _acc_spec(D_MODEL)],
        out_shape=[jax.ShapeDtypeStruct((rows, D_MODEL), F32), jax.ShapeDtypeStruct((rows, D_FF), BF16),
                   jax.ShapeDtypeStruct((rows, D_FF), BF16), jax.ShapeDtypeStruct((rows, D_MODEL), BF16),
                   jax.ShapeDtypeStruct((rows, D_MODEL), BF16), jax.ShapeDtypeStruct((8, D_MODEL), F32),
                   jax.ShapeDtypeStruct((8, D_MODEL), F32)],
        scratch_shapes=[pltpu.VMEM((tm, D_MODEL), F32)],
        compiler_params=_params(("arbitrary",), vmem_mb=62),
    )(dh_out, h, a, b, f, gpre, gpost, wg_t, wu_t, wd)


def _wgrad(lhs, rhs, name, after=None):
    rows, width = lhs.shape
    tm = rows if rows % 1664 == 0 else BLK
    tf = 256 if width % 256 == 0 else 128
    nr = rows // tm

    def body(l_ref, r_ref, *rest):
        o_ref, acc = rest[-2:]
        i = pl.program_id(1)
        part = _tn(l_ref[...], r_ref[...])

        @pl.when(i == 0)
        def _():
            acc[...] = part

        @pl.when(i > 0)
        def _():
            acc[...] += part

        @pl.when(i == nr - 1)
        def _():
            o_ref[...] = acc[...].astype(BF16)

    l_spec = pl.BlockSpec((tm, tf), lambda j, i: (i, j))
    r_spec = pl.BlockSpec((tm, D_MODEL), lambda j, i: (i, 0))
    return pl.pallas_call(
        body, name=name, grid=(width // tf, nr),
        in_specs=[l_spec, r_spec] + ([] if after is None else [ANY_SPEC]),
        out_specs=pl.BlockSpec((tf, D_MODEL), lambda j, i: (j, 0)),
        out_shape=jax.ShapeDtypeStruct((width, D_MODEL), BF16),
        scratch_shapes=[pltpu.VMEM((tf, D_MODEL), F32)],
        compiler_params=_params(("arbitrary", "arbitrary")),
    )(lhs, rhs, *([] if after is None else [after]))


def _chunk_cumsum(x, lower):
    tri = jnp.where(_tri(lower), 1.0, 0.0).astype(BF16)
    hi, lo = _split_bf16(x)
    return _nn(tri, hi) + _nn(tri, lo)


def _mix_in(h, g, win_p, wa2_p, b_a, cos, sin):
    rows = h.shape[0]
    tm = 640 if rows % 640 == 0 else BLK

    def body(h_ref, g_ref, win_ref, wa2_ref, ba_ref, cos_ref, sin_ref,
             gq_ref, gk_ref, gv_ref, gg_ref, sq_ref, sk_ref, sv_ref, ga_ref, loga_ref, bc_ref, n_ref):
        hn, _ = _rms(h_ref[...])
        n16 = (hn * g_ref[...]).astype(BF16)
        n_ref[...] = n16
        proj = _nt(n16, win_ref[...])
        gq_ref[...] = proj[:, P_GQ:P_GK]
        gk_ref[...] = proj[:, P_GK:P_GV]
        gv_ref[...] = proj[:, P_GV:P_GG].astype(BF16)
        gg_ref[...] = proj[:, P_GG:P_SQ]
        c1, s1 = cos_ref[...], sin_ref[...]
        c4 = jnp.concatenate([c1, c1, c1, c1], axis=1)
        s4 = jnp.concatenate([s1, s1, s1, s1], axis=1)
        sq = proj[:, P_SQ:P_SK]
        sk = proj[:, P_SK:P_SV]
        sq_ref[...] = (sq * c4 + _rot_half(sq) * s4).astype(BF16)
        sk_ref[...] = (sk * c1 + _rot_half(sk) * s1).astype(BF16)
        sv_ref[...] = proj[:, P_SV:P_GA].astype(BF16)
        ga = proj[:, P_GA:P_END]
        ga_ref[...] = ga
        z = _nn(ga, wa2_ref[...]) + ba_ref[...]
        loga = (jnp.minimum(z, 0.0) - jnp.log(1.0 + jnp.exp(-jnp.abs(z)))) * (1.0 / GLA_TAU)
        loga_ref[...] = loga
        for c in range(tm // BLK):
            rs = slice(c * BLK, (c + 1) * BLK)
            bc_ref[rs, :] = _chunk_cumsum(loga[rs, :], True)

    f32 = lambda c: jax.ShapeDtypeStruct((rows, c), F32)
    b16 = lambda c: jax.ShapeDtypeStruct((rows, c), BF16)
    rs = lambda c: _row_spec(tm, c)
    return pl.pallas_call(
        body, name="mix_in", grid=(rows // tm,),
        in_specs=[rs(D_MODEL), VMEM_SPEC, VMEM_SPEC, VMEM_SPEC, VMEM_SPEC, rs(128), rs(128)],
        out_specs=[rs(256), rs(256), rs(512), rs(512), rs(512), rs(128), rs(128), rs(128), rs(256), rs(256), rs(D_MODEL)],
        out_shape=[f32(256), f32(256), b16(512), f32(512), b16(512), b16(128), b16(128), f32(128), f32(256), f32(256),
                   b16(D_MODEL)],
        compiler_params=_params(("arbitrary",)),
    )(h, g, win_p, wa2_p, b_a, cos, sin)


def _gla_factors(q, k, bc):
    bm = bc[BLK // 2 - 1:BLK // 2, :]
    bl = bc[BLK - 1:BLK, :]
    e_q, e_k, e_qe, e_kd = jnp.exp(bc - bm), jnp.exp(bm - bc), jnp.exp(bc), jnp.exp(bl - bc)
    return (q * e_q, k * e_k, q * e_qe, k * e_kd), (e_q, e_k, e_qe, e_kd), jnp.exp(bl)


def _gla_fwd(gq, gk, gv, gg, bc, wgn):
    rows = gq.shape[0]
    nc = rows // BLK
    scale = GLA_DK ** -0.5

    def body(q_ref, k_ref, v_ref, gg_ref, bc_ref, wgn_ref, o_ref, cat_ref, sp_ref, st):
        @pl.when(pl.program_id(0) == 0)
        def _():
            st[...] = jnp.zeros_like(st)
        low = _tri(True)
        wgn_v = wgn_ref[...]
        for p in range(2):
            sl = slice(128 * p, 128 * p + 128)
            (qt, kt, qe, kd), _, ebl = _gla_factors(q_ref[:, sl] * scale, k_ref[:, sl], bc_ref[:, sl])
            s_prev = st[p]
            sp_ref[0, p] = s_prev
            s16 = s_prev.astype(BF16)
            qt16 = qt.astype(BF16)
            s_new = s_prev * ebl
            for hh in range(2):
                hs = slice(128 * (2 * p + hh), 128 * (2 * p + hh) + 128)
                lm = _half_mask(128, hh)
                vh = v_ref[:, hs]
                pm = jnp.where(low, _nt(qt16, jnp.where(lm, kt, 0.0).astype(BF16)), 0.0)
                o = _nn(pm.astype(BF16), vh) + _nt(jnp.where(lm, qe, 0.0).astype(BF16), s16)
                s_new = s_new + _tn(vh, jnp.where(lm, kd, 0.0).astype(BF16))
                o_ref[:, hs] = o
                on, _ = _rms(o)
                gate = gg_ref[:, hs]
                cat_ref[:, hs] = (on * wgn_v * (gate * _sigmoid(gate))).astype(BF16)
            st[p] = s_new

    rs = lambda c: _row_spec(BLK, c)
    return pl.pallas_call(
        body, name="gla_fwd", grid=(nc,),
        in_specs=[rs(256), rs(256), rs(512), rs(512), rs(256), VMEM_SPEC],
        out_specs=[rs(512), rs(512), pl.BlockSpec((1, 2, 128, 128), lambda i: (i, 0, 0, 0))],
        out_shape=[jax.ShapeDtypeStruct((rows, 512), F32), jax.ShapeDtypeStruct((rows, 512), BF16),
                   jax.ShapeDtypeStruct((nc, 2, 128, 128), F32)],
        scratch_shapes=[pltpu.VMEM((2, 128, 128), F32)],
        compiler_params=_params(("arbitrary",)),
    )(gq, gk, gv, gg, bc, wgn)


def _gla_bwd(dcat, o_all, gq, gk, gv, gg, bc, sp, wgn):
    rows = gq.shape[0]
    nc = rows // BLK
    scale = GLA_DK ** -0.5

    def body(dc_ref, o_ref, q_ref, k_ref, v_ref, gg_ref, bc_ref, sp_ref, wgn_ref,
             dq_ref, dk_ref, dv_ref, dgg_ref, dla_ref, dwgn_ref, dst):
        first = pl.program_id(0) == 0

        @pl.when(first)
        def _():
            dst[...] = jnp.zeros_like(dst)
        low, upp = _tri(True), _tri(False)
        last_row = lax.broadcasted_iota(jnp.int32, (BLK, 1), 0) == BLK - 1
        wgn_v = wgn_ref[...]
        dwgn = jnp.zeros((1, 128), F32)
        for p in range(2):
            sl = slice(128 * p, 128 * p + 128)
            (qt, kt, qe, kd), (e_q, e_k, e_qe, e_kd), ebl = _gla_factors(
                q_ref[:, sl] * scale, k_ref[:, sl], bc_ref[:, sl])
            s_prev = sp_ref[0, p]
            s16 = s_prev.astype(BF16)
            ds_next = dst[p]
            ds16 = ds_next.astype(BF16)
            qt16 = qt.astype(BF16)
            ds_new = ds_next * ebl
            dqt = jnp.zeros((BLK, 128), F32)
            dkt = jnp.zeros((BLK, 128), F32)
            dqe = jnp.zeros((BLK, 128), F32)
            dkd = jnp.zeros((BLK, 128), F32)
            for hh in range(2):
                hs = slice(128 * (2 * p + hh), 128 * (2 * p + hh) + 128)
                lm = _half_mask(128, hh)
                on, ro = _rms(o_ref[:, hs])
                gate = gg_ref[:, hs]
                sg = _sigmoid(gate)
                si = gate * sg
                dog = dc_ref[:, hs]
                dwgn = dwgn + _colsum(dog * si * on)
                dgg_ref[:, hs] = dog * (on * wgn_v) * (sg * (1.0 + gate * (1.0 - sg)))
                do16 = _rms_bwd(on, ro, wgn_v, dog * si).astype(BF16)
                vh = v_ref[:, hs]
                ktm16 = jnp.where(lm, kt, 0.0).astype(BF16)
                qtm16 = jnp.where(lm, qt, 0.0).astype(BF16)
                qem16 = jnp.where(lm, qe, 0.0).astype(BF16)
                kdm16 = jnp.where(lm, kd, 0.0).astype(BF16)
                p_t = jnp.where(upp, _nt(ktm16, qt16), 0.0)
                dp_t = jnp.where(upp, _nt(vh, do16), 0.0)
                dp = jnp.where(low, _nt(do16, vh), 0.0)
                dv_ref[:, hs] = _nn(p_t.astype(BF16), do16) + _nt(kdm16, ds16)
                dqt = dqt + _nn(dp.astype(BF16), ktm16)
                dkt = dkt + _nn(dp_t.astype(BF16), qtm16)
                dqe = dqe + jnp.where(lm, _nn(do16, s16), 0.0)
                dkd = dkd + jnp.where(lm, _nn(vh, ds16), 0.0)
                ds_new = ds_new + _tn(do16, qem16)
            debl = _colsum(ds_next * s_prev)
            dq_ref[:, sl] = (dqt * e_q + dqe * e_qe) * scale
            dk_ref[:, sl] = dkt * e_k + dkd * e_kd
            dkd_kd = dkd * kd
            db = dqt * qt - dkt * kt + dqe * qe - dkd_kd
            db = jnp.where(last_row, db + (_colsum(dkd_kd) + debl * ebl), db)
            dla_ref[:, sl] = _chunk_cumsum(db, False)
            dst[p] = ds_new
        _acc_add(dwgn_ref, first, dwgn)

    rev = lambda c: pl.BlockSpec((BLK, c), lambda i: (nc - 1 - i, 0))
    f32 = lambda c: jax.ShapeDtypeStruct((rows, c), F32)
    return pl.pallas_call(
        body, name="gla_bwd", grid=(nc,),
        in_specs=[rev(512), rev(512), rev(256), rev(256), rev(512), rev(512), rev(256),
                  pl.BlockSpec((1, 2, 128, 128), lambda i: (nc - 1 - i, 0, 0, 0)), VMEM_SPEC],
        out_specs=[rev(256), rev(256), rev(512), rev(512), rev(256), _acc_spec(128)],
        out_shape=[f32(256), f32(256), f32(512), f32(512), f32(256), jax.ShapeDtypeStruct((8, 128), F32)],
        scratch_shapes=[pltpu.VMEM((2, 128, 128), F32)],
        compiler_params=_params(("arbitrary",)),
    )(dcat, o_all, gq, gk, gv, gg, bc, sp, wgn)


def _swa_masks(i):
    t = lax.broadcasted_iota(jnp.int32, (BLK, BLK), 0)
    c = lax.broadcasted_iota(jnp.int32, (BLK, BLK), 1)
    own_side = c <= t
    band_ok = i >= jnp.where(own_side, 1, 2)
    meta_ok = (c % N_META) <= jnp.where(i >= 1, N_META, t - PAD_ROWS)
    return own_side, band_ok, meta_ok, c // N_META


def _swa_blocks(ref, i):
    prev = pl.multiple_of(jnp.maximum(i - 1, 0) * BLK, BLK)
    own = pl.multiple_of(i * BLK, BLK)
    return jnp.concatenate([ref[pl.ds(prev, BLK), :], ref[pl.ds(own, BLK), :]], axis=0), prev, own


def _swa_meta_operand(ref):
    blk = ref[0:BLK, :]
    swapped = pltpu.roll(blk, 64, 1)
    lo = jnp.where(_half_mask(128, 0), blk, swapped)
    hi = jnp.where(_half_mask(128, 1), blk, swapped)
    meta = jnp.concatenate([lo, lo, hi, hi], axis=1)[PAD_ROWS:BLK, :]
    tiled = jnp.concatenate([meta] * SWA_HEADS, axis=0)
    j = lax.broadcasted_iota(jnp.int32, tiled.shape, 0)
    lane = lax.broadcasted_iota(jnp.int32, tiled.shape, 1)
    return jnp.where(j // N_META == lane // SWA_HD, tiled, jnp.zeros_like(tiled))


def _swa_meta_fold(acc):
    out = jnp.zeros((N_META, 128), F32)
    for hd in range(SWA_HEADS):
        half, kv = hd % 2, hd // 4
        piece = acc[N_META * hd:N_META * (hd + 1), 128 * (hd // 2):128 * (hd // 2) + 128]
        piece = jnp.where(_half_mask(128, half), piece, 0.0)
        out = out + (piece if half == kv else pltpu.roll(piece, 64, 1))
    return out


def _by_head(group, per_head):
    out = jnp.zeros((BLK, BLK), F32)
    for hd, v in enumerate(per_head):
        out = jnp.where(group == hd, v, out)
    return out


def _place(x, kv):
    if kv == 0:
        lo = jnp.where(_half_mask(128, 0), x, jnp.zeros_like(x))
        return lo, pltpu.roll(lo, 64, 1)
    hi = jnp.where(_half_mask(128, 1), x, jnp.zeros_like(x))
    return pltpu.roll(hi, 64, 1), hi


def _swa_fwd(sq, sk, sv, sinks, wn):
    rows = sq.shape[0]
    nb = rows // BLK
    scale = SWA_HD ** -0.5

    def body(q_ref, k_ref, v_ref, sink_ref, wn_ref, o_ref, cat_ref, lse_ref, kp, vp):
        i = pl.program_id(0)

        @pl.when(i == 0)
        def _():
            kp[...] = _swa_meta_operand(k_ref)
            vp[...] = _swa_meta_operand(v_ref)
        own_side, band_ok, meta_ok, group = _swa_masks(i)
        k2, _, _ = _swa_blocks(k_ref, i)
        v2, _, _ = _swa_blocks(v_ref, i)
        kz = (_place(k2, 0), _place(k2, 1))
        vz = (_place(v2, 0), _place(v2, 1))
        q_all = q_ref[...]
        s_meta = jnp.where(meta_ok, _nt(q_all, kp[...]) * scale, NEG_INF)
        s_band, m = [], []
        for hd in range(SWA_HEADS):
            kv, half = hd // 4, hd % 2
            q_pair = q_all[:, 128 * (hd // 2):128 * (hd // 2) + 128]
            s2 = _nt(q_pair, kz[kv][half])
            s = jnp.where(band_ok, jnp.where(own_side, s2[:, BLK:], s2[:, :BLK]) * scale, NEG_INF)
            top = jnp.maximum(jnp.max(s, axis=-1, keepdims=True),
                              jnp.max(jnp.where(group == hd, s_meta, NEG_INF), axis=-1, keepdims=True))
            s_band.append(s)
            m.append(jnp.maximum(top, sink_ref[0, hd]))
        e_meta = jnp.exp(s_meta - _by_head(group, m))
        o_meta = _nn(e_meta.astype(BF16), vp[...])
        outs = []
        for pr in range(4):
            o_pair = o_meta[:, 128 * pr:128 * pr + 128]
            rden = []
            for half in range(2):
                hd = 2 * pr + half
                kv = hd // 4
                e = jnp.exp(s_band[hd] - m[hd])
                den = (jnp.sum(e, axis=-1, keepdims=True)
                       + jnp.sum(jnp.where(group == hd, e_meta, 0.0), axis=-1, keepdims=True)
                       + jnp.exp(sink_ref[0, hd] - m[hd]))
                lse_ref[:, hd:hd + 1] = m[hd] + jnp.log(den)
                rden.append(1.0 / den)
                e2 = jnp.concatenate([jnp.where(own_side, 0.0, e), jnp.where(own_side, e, 0.0)], axis=1).astype(BF16)
                o_pair = o_pair + _nn(e2, vz[kv][half])
            outs.append(o_pair * jnp.where(_half_mask(128, 0), rden[0], rden[1]))
        o = jnp.concatenate(outs, axis=1)
        o_ref[...] = o
        on, _ = _rms(o)
        cat_ref[...] = (on * wn_ref[...]).astype(BF16)

    return pl.pallas_call(
        body, name="swa_fwd", grid=(nb,),
        in_specs=[_row_spec(BLK, 512), VMEM_SPEC, VMEM_SPEC, SMEM_SPEC, VMEM_SPEC],
        out_specs=[_row_spec(BLK, 512), _row_spec(BLK, 512), _row_spec(BLK, SWA_HEADS)],
        out_shape=[jax.ShapeDtypeStruct((rows, 512), F32), jax.ShapeDtypeStruct((rows, 512), BF16),
                   jax.ShapeDtypeStruct((rows, SWA_HEADS), F32)],
        scratch_shapes=[pltpu.VMEM((BLK, 512), BF16), pltpu.VMEM((BLK, 512), BF16)],
        compiler_params=_params(("arbitrary",)),
    )(sq, sk, sv, sinks, wn)


def _swa_bwd(dcat, o_all, sq, sk, sv, lse, sinks, wn):
    rows = sq.shape[0]
    nb = rows // BLK
    scale = SWA_HD ** -0.5

    def body(dc_ref, o_ref, q_ref, k_ref, v_ref, lse_ref, sink_ref, wn_ref, dq_ref, dk_ref, dv_ref, dsink_ref, dwn_ref,
             kp, vp, dkp, dvp):
        i = pl.program_id(0)
        first = i == 0

        @pl.when(first)
        def _():
            dk_ref[...] = jnp.zeros_like(dk_ref)
            dv_ref[...] = jnp.zeros_like(dv_ref)
            dkp[...] = jnp.zeros_like(dkp)
            dvp[...] = jnp.zeros_like(dvp)
            kp[...] = _swa_meta_operand(k_ref)
            vp[...] = _swa_meta_operand(v_ref)
        own_side, band_ok, meta_ok, group = _swa_masks(i)
        k2, prev, own = _swa_blocks(k_ref, i)
        v2, _, _ = _swa_blocks(v_ref, i)
        kz = (_place(k2, 0), _place(k2, 1))
        vz = (_place(v2, 0), _place(v2, 1))
        o = o_ref[...]
        on, ro = _rms(o)
        dc = dc_ref[...]
        _acc_add(dwn_ref, first, _colsum(dc * on))
        do = _rms_bwd(on, ro, wn_ref[...], dc)
        do_o = do * o
        do16 = do.astype(BF16)
        q_all = q_ref[...]
        lse = [lse_ref[:, hd:hd + 1] for hd in range(SWA_HEADS)]
        delta = [jnp.sum(jnp.where(_half_mask(128, hd % 2), do_o[:, 128 * (hd // 2):128 * (hd // 2) + 128], 0.0),
                         axis=-1, keepdims=True) for hd in range(SWA_HEADS)]
        s_meta = jnp.where(meta_ok, _nt(q_all, kp[...]) * scale, NEG_INF)
        p_meta = jnp.exp(s_meta - _by_head(group, lse))
        ds_meta16 = (p_meta * (_nt(do16, vp[...]) - _by_head(group, delta)) * scale).astype(BF16)
        dq_meta = _nn(ds_meta16, kp[...])
        dkp[...] += _tn(ds_meta16, q_all)
        dvp[...] += _tn(p_meta.astype(BF16), do16)
        own2 = jnp.concatenate([own_side.astype(jnp.int32)] * 2, axis=0) > 0
        ok2 = jnp.concatenate([band_ok.astype(jnp.int32)] * 2, axis=0) > 0

        def window(x2):
            return jnp.where(own2, x2[:, BLK:], x2[:, :BLK])

        def unwindow(x):
            return jnp.concatenate([jnp.where(own2, 0.0, x), jnp.where(own2, x, 0.0)], axis=1).astype(BF16)
        lane8 = lax.broadcasted_iota(jnp.int32, (1, 128), 1)
        dsink = jnp.zeros((1, 128), F32)
        dq_pairs = [dq_meta[:, 128 * pr:128 * pr + 128] for pr in range(4)]
        dk2 = [[None, None], [None, None]]
        dv2 = [[None, None], [None, None]]
        for kv in range(2):
            for half in range(2):
                heads, pairs = (4 * kv + half, 4 * kv + 2 + half), (2 * kv, 2 * kv + 1)
                q_s = jnp.concatenate([q_all[:, 128 * pr:128 * pr + 128] for pr in pairs], axis=0)
                do_s = jnp.concatenate([do16[:, 128 * pr:128 * pr + 128] for pr in pairs], axis=0)
                lse_s = jnp.concatenate([lse[hd] for hd in heads], axis=0)
                delta_s = jnp.concatenate([delta[hd] for hd in heads], axis=0)
                s = jnp.where(ok2, window(_nt(q_s, kz[kv][half])) * scale, NEG_INF)
                prob = jnp.exp(s - lse_s)
                for hd in heads:
                    dsink = dsink + jnp.where(lane8 == hd, -jnp.sum(jnp.exp(sink_ref[0, hd] - lse[hd]) * delta[hd]), 0.0)
                ds2 = unwindow(prob * (window(_nt(do_s, vz[kv][half])) - delta_s) * scale)
                dq_s = _nn(ds2, kz[kv][half])
                dq_pairs[pairs[0]] = dq_pairs[pairs[0]] + dq_s[:BLK]
                dq_pairs[pairs[1]] = dq_pairs[pairs[1]] + dq_s[BLK:]
                dk2[kv][half] = _tn(ds2, q_s)
                dv2[kv][half] = _tn(unwindow(prob), do_s)
        dq_ref[...] = jnp.concatenate(dq_pairs, axis=1)
        _acc_add(dsink_ref, first, dsink)
        for ref, acc2 in ((dk_ref, dk2), (dv_ref, dv2)):
            tot = jnp.zeros((2 * BLK, 128), F32)
            for kv in range(2):
                for half in range(2):
                    part = jnp.where(_half_mask(128, half), acc2[kv][half], 0.0)
                    tot = tot + (part if half == kv else pltpu.roll(part, 64, 1))
            ref[pl.ds(prev, BLK), :] += tot[:BLK]
            ref[pl.ds(own, BLK), :] += tot[BLK:]

        @pl.when(i == nb - 1)
        def _():
            dk_ref[PAD_ROWS:BLK, :] += _swa_meta_fold(dkp[...])
            dv_ref[PAD_ROWS:BLK, :] += _swa_meta_fold(dvp[...])

    full = pl.BlockSpec((rows, 128), lambda i: (0, 0))
    return pl.pallas_call(
        body, name="swa_bwd", grid=(nb,),
        in_specs=[_row_spec(BLK, 512), _row_spec(BLK, 512), _row_spec(BLK, 512), VMEM_SPEC, VMEM_SPEC,
                  _row_spec(BLK, SWA_HEADS), SMEM_SPEC, VMEM_SPEC],
        out_specs=[_row_spec(BLK, 512), full, full, _acc_spec(128), _acc_spec(512)],
        out_shape=[jax.ShapeDtypeStruct((rows, 512), F32), jax.ShapeDtypeStruct((rows, 128), F32),
                   jax.ShapeDtypeStruct((rows, 128), F32), jax.ShapeDtypeStruct((8, 128), F32),
                   jax.ShapeDtypeStruct((8, 512), F32)],
        scratch_shapes=[pltpu.VMEM((BLK, 512), BF16), pltpu.VMEM((BLK, 512), BF16),
                        pltpu.VMEM((BLK, 512), F32), pltpu.VMEM((BLK, 512), F32)],
        compiler_params=_params(("arbitrary",)),
    )(dcat, o_all, sq, sk, sv, lse, sinks, wn)


def _mix_out(h, cat_g, cat_s, wout, gpost):
    rows = h.shape[0]
    tm = _row_tile(rows)

    def body(h_ref, cg_ref, cs_ref, w_ref, g_ref, ho_ref, m_ref):
        m = _nn(cg_ref[...], w_ref[0:512, :]) + _nn(cs_ref[...], w_ref[512:1024, :])
        m_ref[...] = m
        mn, _ = _rms(m)
        ho_ref[...] = h_ref[...] + mn * g_ref[...]

    row_f32 = _row_spec(tm, D_MODEL)
    return pl.pallas_call(
        body, name="mix_out", grid=(rows // tm,),
        in_specs=[row_f32, _row_spec(tm, 512), _row_spec(tm, 512), VMEM_SPEC, VMEM_SPEC],
        out_specs=[row_f32, row_f32],
        out_shape=[jax.ShapeDtypeStruct((rows, D_MODEL), F32), jax.ShapeDtypeStruct((rows, D_MODEL), F32)],
        compiler_params=_params(("arbitrary",)),
    )(h, cat_g, cat_s, wout, gpost)


def _mix_out_bwd(dh, m, wout, gpost):
    rows = dh.shape[0]
    tm = _row_tile(rows)

    def body(dh_ref, m_ref, w_ref, g_ref, dcg_ref, dcs_ref, dm_ref, dg_ref):
        first = pl.program_id(0) == 0
        dhv = dh_ref[...]
        mn, rm = _rms(m_ref[...])
        _acc_add(dg_ref, first, _colsum(dhv * mn))
        dm16 = _rms_bwd(mn, rm, g_ref[...], dhv).astype(BF16)
        dm_ref[...] = dm16
        dcat = _nt(dm16, w_ref[...])
        dcg_ref[...] = dcat[:, 0:512]
        dcs_ref[...] = dcat[:, 512:1024]

    row_f32 = _row_spec(tm, D_MODEL)
    return pl.pallas_call(
        body, name="mix_out_bwd", grid=(rows // tm,),
        in_specs=[row_f32, row_f32, VMEM_SPEC, VMEM_SPEC],
        out_specs=[_row_spec(tm, 512), _row_spec(tm, 512), row_f32, _acc_spec(D_MODEL)],
        out_shape=[jax.ShapeDtypeStruct((rows, 512), F32), jax.ShapeDtypeStruct((rows, 512), F32),
                   jax.ShapeDtypeStruct((rows, D_MODEL), BF16), jax.ShapeDtypeStruct((8, D_MODEL), F32)],
        compiler_params=_params(("arbitrary",)),
    )(dh, m, wout, gpost)


def _mix_in_bwd(dh_out, h, g, win_p, wa2_p, cos, sin, loga, ga, dgq, dgk, dgv, dgg, dsq, dsk, dsv, dloga):
    rows = h.shape[0]
    tm = _row_tile(rows)

    def body(dho_ref, h_ref, g_ref, win_ref, wa2_ref, cos_ref, sin_ref, loga_ref, ga_ref,
             dgq_ref, dgk_ref, dgv_ref, dgg_ref, dsq_ref, dsk_ref, dsv_ref, dla_ref,
             dh_ref, dproj_ref, dwa2_ref, dg_ref, dba_ref):
        first = pl.program_id(0) == 0
        dz = dla_ref[...] * (1.0 / GLA_TAU) * (1.0 - jnp.exp(GLA_TAU * loga_ref[...]))
        _acc_add(dba_ref, first, _colsum(dz))
        dga = _nt(dz, wa2_ref[...])
        pa = _tn(ga_ref[...], dz)
        c1, s1 = cos_ref[...], sin_ref[...]
        c4 = jnp.concatenate([c1, c1, c1, c1], axis=1)
        s4 = jnp.concatenate([s1, s1, s1, s1], axis=1)
        dq_r, dk_r = dsq_ref[...], dsk_ref[...]
        dsq = dq_r * c4 - _rot_half(dq_r * s4)
        dsk = dk_r * c1 - _rot_half(dk_r * s1)
        dproj16 = jnp.concatenate(
            [dgq_ref[...], dgk_ref[...], dgv_ref[...], dgg_ref[...], dsq, dsk, dsv_ref[...], dga], axis=1).astype(BF16)
        dproj_ref[...] = dproj16
        dn = _nn(dproj16, win_ref[...])

        @pl.when(first)
        def _():
            dwa2_ref[...] = pa

        @pl.when(jnp.logical_not(first))
        def _():
            dwa2_ref[...] += pa
        hn, rh = _rms(h_ref[...])
        _acc_add(dg_ref, first, _colsum(dn * hn))
        dh_ref[...] = dho_ref[...] + _rms_bwd(hn, rh, g_ref[...], dn)

    rs = lambda c: _row_spec(tm, c)
    return pl.pallas_call(
        body, name="mix_in_bwd", grid=(rows // tm,),
        in_specs=[rs(D_MODEL), rs(D_MODEL), VMEM_SPEC, VMEM_SPEC, VMEM_SPEC, rs(128), rs(128), rs(256), rs(128),
                  rs(256), rs(256), rs(512), rs(512), rs(512), rs(128), rs(128), rs(256)],
        out_specs=[rs(D_MODEL), rs(P_END), pl.BlockSpec((128, 256), lambda i: (0, 0)), _acc_spec(D_MODEL), _acc_spec(256)],
        out_shape=[jax.ShapeDtypeStruct((rows, D_MODEL), F32), jax.ShapeDtypeStruct((rows, P_END), BF16),
                   jax.ShapeDtypeStruct((128, 256), F32), jax.ShapeDtypeStruct((8, D_MODEL), F32),
                   jax.ShapeDtypeStruct((8, 256), F32)],
        compiler_params=_params(("arbitrary",)),
    )(dh_out, h, g, win_p, wa2_p, cos, sin, loga, ga, dgq, dgk, dgv, dgg, dsq, dsk, dsv, dloga)


def _rope_tables(rows):
    pos = (jnp.arange(rows, dtype=jnp.int32) - PAD_ROWS).astype(F32)
    inv_freq = 1.0 / (ROPE_THETA ** (jnp.arange(0, SWA_HD, 2, dtype=F32) / SWA_HD))
    ang = pos[:, None] * inv_freq[None, :]
    return jnp.tile(jnp.cos(ang), (1, 4)), jnp.tile(jnp.sin(ang), (1, 4))


def _local_step(x, tgt, front, w, late_weights=None, on_grads=None, on_small=None):
    cos, sin = _rope_tables(x.shape[0] + BLK)
    g = {}

    def tell(group, names):
        for nm in names:
            g[nm] = grads_now[nm]
        return None if on_grads is None else on_grads(group, {nm: grads_now[nm] for nm in names})

    h0, h1, a1, b1, s1, f1 = _ffn_fwd(x, w["ffn1_pre"], w["wg1"], w["wu1"], w["wd1"], w["ffn1_post"], front=front)
    if late_weights is not None:
        w = {**w, **late_weights("win", f1)}
    gq, gk, gv, gg, sq, sk, sv, ga, loga, bc, n2 = _mix_in(h1, w["mix_pre"], w["win"], w["wa2"], w["b_a"], cos, sin)
    o_g, cat_g, sp = _gla_fwd(gq, gk, gv, gg, bc, w["gla_norm"])
    o_s, cat_s, lse = _swa_fwd(sq, sk, sv, w["sinks"], w["swa_norm"])
    if late_weights is not None:
        w = {**w, **late_weights("rest", lse)}
    h2, m = _mix_out(h1, cat_g, cat_s, w["wout"], w["mix_post"])
    h3, a2, b2, s2, f2, dy, loss = _ffn_fwd(h2, w["ffn2_pre"], w["wg2"], w["wu2"], w["wd2"], w["ffn2_post"], tgt)
    del h3
    dh2, da, db, df, n3, g["ffn2_pre"], g["ffn2_post"] = _ffn_bwd_act(
        dy, h2, a2, b2, f2, w["ffn2_pre"], w["ffn2_post"], w["wg2"], w["wu2"], w["wd2"], "ffn2_bwd_act")
    grads_now = dict(wd2=_wgrad(s2, df, "ffn2_wgrad_down"), wg2=_wgrad(da, n3, "ffn2_wgrad_gate"),
                     wu2=_wgrad(db, n3, "ffn2_wgrad_up"))
    tok = tell("ffn2", ("wd2", "wg2", "wu2"))
    dcg, dcs, dm, g["mix_post"] = _mix_out_bwd(dh2, m, w["wout"], w["mix_post"] + (0.0 if tok is None else tok[0, 0]))
    dsq, dsk, dsv, g["sinks"], g["swa_norm"] = _swa_bwd(dcs, o_s, sq, sk, sv, lse, w["sinks"], w["swa_norm"])
    dgq, dgk, dgv, dgg, dloga, g["gla_norm"] = _gla_bwd(dcg, o_g, gq, gk, gv, gg, bc, sp, w["gla_norm"])
    dh1, dproj, g["wa2"], g["mix_pre"], g["b_a"] = _mix_in_bwd(
        dh2, h1, w["mix_pre"], w["win"], w["wa2"], cos, sin, loga, ga, dgq, dgk, dgv, dgg, dsq, dsk, dsv, dloga)
    dh0, da, db, df, n1, g["ffn1_pre"], g["ffn1_post"] = _ffn_bwd_act(
        dh1, h0, a1, b1, f1, w["ffn1_pre"], w["ffn1_post"], w["wg1"], w["wu1"], w["wd1"], "ffn1_bwd_act")
    tok = None if on_small is None else on_small(loss[0, 0], dh0, g)
    grads_now = dict(wd1=_wgrad(s1, df, "ffn1_wgrad_down", after=tok))
    tok = tell("ffn1_down", ("wd1",))
    grads_now = dict(wg1=_wgrad(da, n1, "ffn1_wgrad_gate", after=tok))
    tok = tell("ffn1_gate", ("wg1",))
    grads_now = dict(wu1=_wgrad(db, n1, "ffn1_wgrad_up", after=tok))
    tok = tell("ffn1_up", ("wu1",))
    grads_now = dict(win=_wgrad(dproj, n2, "win_wgrad", after=tok),
                     wout=jnp.concatenate([_wgrad(cat_g, dm, "wout_wgrad_gla", after=tok),
                                           _wgrad(cat_s, dm, "wout_wgrad_swa", after=tok)], axis=0))
    tell("mix", ("wout", "win"))
    return loss[0, 0], dh0, g


def _win_pad_rows(win_t):
    pad = jnp.zeros((P_END - P_GA - 16, win_t.shape[1]), win_t.dtype)
    return jnp.concatenate([win_t[0:1536], win_t[1552:2320], win_t[1536:1552], pad], axis=0)


def _win_unpad_rows(win_p):
    return jnp.concatenate([win_p[0:1536], win_p[P_GA:P_GA + 16], win_p[1536:P_GA]], axis=0)


def _place_on_mesh():
    return lax.axis_index("x"), lax.axis_index("y"), lax.axis_index("c")


def _dev_index(px, py, pc):
    return 4 * px + 2 * py + pc


def _other_devices(x, y, c):
    flip = lambda v, f: 1 - v if f else v
    return [(flip(x, fx), flip(y, fy), flip(c, fc)) for fx in (0, 1) for fy in (0, 1) for fc in (0, 1)][1:]


def _all_gather(shards):
    n = len(shards)

    def body(*refs):
        ins, outs = refs[:n], refs[n:2 * n]
        zeros_ref, send_sems, recv_sems, local_sems = refs[2 * n:]
        zeros_ref[...] = jnp.zeros_like(zeros_ref)
        x, y, c = _place_on_mesh()
        me, sibling = (x, y, c), (x, y, 1 - c)
        chips = [(1 - x, y), (x, 1 - y), (1 - x, 1 - y)]

        def rows(k, px, py, pc):
            r = ins[k].shape[0]
            return outs[k].at[pl.ds(pl.multiple_of(_dev_index(px, py, pc) * r, 8), r), :]

        def copy(k, slot, block, to, src=None):
            return pltpu.make_async_remote_copy(
                src_ref=rows(k, *block) if src is None else src, dst_ref=rows(k, *block),
                send_sem=send_sems.at[k, slot], recv_sem=recv_sems.at[k, slot], device_id=to, device_id_type=MESH)

        local = [pltpu.make_async_copy(ins[k], rows(k, *me), local_sems.at[k]) for k in range(n)]
        sends = []
        for k in range(n):
            local[k].start()
            sends.append(copy(k, 0, me, sibling, src=ins[k]))
            sends += [copy(k, 1 + j, me, (*chip, c), src=ins[k]) for j, chip in enumerate(chips)]
        for cp in sends:
            cp.start()
        for k in range(n):
            for j, chip in enumerate(chips):
                copy(k, 1 + j, (*chip, c), me).wait_recv()
                passed = copy(k, 4 + j, (*chip, c), sibling)
                passed.start()
                sends.append(passed)
        for k in range(n):
            copy(k, 0, sibling, me).wait_recv()
            for j, chip in enumerate(chips):
                copy(k, 4 + j, (*chip, 1 - c), me).wait_recv()
        for cp in sends:
            cp.wait_send()
        for cp in local:
            cp.wait()

    return pl.pallas_call(
        body, name="all_gather_weights",
        in_specs=[ANY_SPEC] * n, out_specs=[ANY_SPEC] * n + [VMEM_SPEC],
        out_shape=[jax.ShapeDtypeStruct((N_DEV * s.shape[0], s.shape[1]), s.dtype) for s in shards]
        + [jax.ShapeDtypeStruct((8, 128), F32)],
        scratch_shapes=[pltpu.SemaphoreType.DMA((n, 7)), pltpu.SemaphoreType.DMA((n, 7)), pltpu.SemaphoreType.DMA((n,))],
    )(*shards)


HBM_SPEC = pl.BlockSpec(memory_space=pltpu.HBM)
SEM_SPEC = pl.BlockSpec(memory_space=pltpu.SEMAPHORE)
DATAFLOW = pltpu.SideEffectType.DATAFLOW_SIDE_EFFECTING


GATHER, SCATTER, SCATTER_CHIPS = "gather", "scatter", "scatter among chips"


def _exchange_peers(kind):
    x, y, c = _place_on_mesh()
    if kind == SCATTER_CHIPS:
        peers = [(1 - x, y, c), (x, 1 - y, c), (1 - x, 1 - y, c)]
        return peers, [2 * p[0] + p[1] for p in peers], 2 * x + y, 4
    peers = _other_devices(x, y, c)
    return peers, [_dev_index(*p) for p in peers], _dev_index(x, y, c), N_DEV


def _exchange_copies(srcs, lands, send_sems, recv_sems, own_sems, kind, arriving):
    peers, theirs, me, blocks = _exchange_peers(kind)
    remote, local = [], []
    for k, (src, land) in enumerate(zip(srcs, lands)):
        r = land.shape[0] // blocks

        def block(ref, d):
            return ref.at[pl.ds(pl.multiple_of(d * r, 8), r), :]

        for f, (peer, him) in enumerate(zip(peers, theirs)):
            mine, his = (him, me) if arriving else (me, him)
            sem = len(peers) * k + f
            remote.append(pltpu.make_async_remote_copy(
                src_ref=src if kind == GATHER else block(src, his), dst_ref=block(land, mine),
                send_sem=send_sems.at[sem], recv_sem=recv_sems.at[sem], device_id=peer, device_id_type=MESH))
        local.append(pltpu.make_async_copy(src if kind == GATHER else block(src, me), block(land, me), own_sems.at[k]))
    return remote, local


def _exchange_start(srcs, kind, name):
    n = len(srcs)
    lands = [lax.empty((N_DEV * s.shape[0], s.shape[1]) if kind == GATHER else s.shape, s.dtype) for s in srcs]
    sems = (3 if kind == SCATTER_CHIPS else 7) * n

    def body(*refs):
        remote, local = _exchange_copies(refs[:n], refs[n:2 * n], *refs[2 * n:2 * n + 3], kind, False)
        for cp in remote + local:
            cp.start()
        refs[-1][...] = jnp.zeros_like(refs[-1])

    both = list(srcs) + list(lands)
    outs = pl.pallas_call(
        body, name=name,
        out_shape=(pltpu.SemaphoreType.DMA((sems,)), pltpu.SemaphoreType.DMA((sems,)), pltpu.SemaphoreType.DMA((n,)),
                   *[pltpu.HBM(a.shape, a.dtype) for a in both], jax.ShapeDtypeStruct((8, 128), F32)),
        in_specs=[HBM_SPEC] * (2 * n), out_specs=(SEM_SPEC, SEM_SPEC, SEM_SPEC, *[HBM_SPEC] * (2 * n), VMEM_SPEC),
        input_output_aliases={i: 3 + i for i in range(2 * n)},
        compiler_params=pltpu.CompilerParams(has_side_effects=DATAFLOW),
    )(*[pltpu.with_memory_space_constraint(a, pltpu.HBM) for a in both])
    return outs[0:3], outs[3:3 + n], outs[3 + n:3 + 2 * n], outs[-1]


def _exchange_wait(started, kind, after, name):
    sems, srcs, lands, _ = started
    n = len(srcs)

    def body(*refs):
        args = (refs[:n], refs[n:2 * n], *refs[2 * n:2 * n + 3], kind)
        going, local = _exchange_copies(*args, False)
        for cp in going:
            cp.wait_send()
        for cp in local:
            cp.wait()
        for cp in _exchange_copies(*args, True)[0]:
            cp.wait_recv()

    both = list(srcs) + list(lands)
    outs = pl.pallas_call(
        body, name=name, out_shape=[pltpu.HBM(a.shape, a.dtype) for a in both],
        in_specs=[HBM_SPEC] * (2 * n) + [SEM_SPEC, SEM_SPEC, SEM_SPEC, ANY_SPEC], out_specs=[HBM_SPEC] * (2 * n),
        input_output_aliases={i: i for i in range(2 * n)},
        compiler_params=pltpu.CompilerParams(has_side_effects=DATAFLOW),
    )(*both, *sems, after)
    return outs[n:]


def _sibling_reduce(part, name):
    r, cols = part.shape[0] // N_DEV, part.shape[1]

    def body(p_ref, o_ref, mine, got, send_sems, recv_sems, own_sems):
        x, y, c = _place_on_mesh()

        def block(d):
            return p_ref.at[pl.ds(pl.multiple_of(d * r, 8), r), :]
        swaps = [pltpu.make_async_remote_copy(
            src_ref=block(2 * j + 1 - c), dst_ref=got.at[j], send_sem=send_sems.at[j], recv_sem=recv_sems.at[j],
            device_id=(x, y, 1 - c), device_id_type=MESH) for j in range(4)]
        keeps = [pltpu.make_async_copy(block(2 * j + c), mine.at[j], own_sems.at[j]) for j in range(4)]
        for cp in swaps + keeps:
            cp.start()
        for j in range(4):
            keeps[j].wait()
            swaps[j].wait()
            o_ref[pl.ds(j * r, r), :] = (mine[j].astype(F32) + got[j].astype(F32)).astype(o_ref.dtype)

    return pl.pallas_call(
        body, name=name, in_specs=[ANY_SPEC], out_specs=VMEM_SPEC,
        out_shape=jax.ShapeDtypeStruct((4 * r, cols), part.dtype),
        scratch_shapes=[pltpu.VMEM((4, r, cols), part.dtype), pltpu.VMEM((4, r, cols), part.dtype),
                        pltpu.SemaphoreType.DMA((4,)), pltpu.SemaphoreType.DMA((4,)), pltpu.SemaphoreType.DMA((4,))],
        compiler_params=pltpu.CompilerParams(vmem_limit_bytes=32 << 20),
    )(part)


def _sum_partials(parts, name, blocks=N_DEV):
    n = len(parts)

    def body(*refs):
        ins, outs = refs[:n], refs[n:]
        first = pl.program_id(0) == 0
        for i_ref, o_ref in zip(ins, outs):
            v = i_ref[...].astype(F32)

            @pl.when(first)
            def _():
                o_ref[...] = v

            @pl.when(jnp.logical_not(first))
            def _():
                o_ref[...] += v

    shapes = [(p.shape[0] // blocks, p.shape[1]) for p in parts]
    return pl.pallas_call(
        body, name=name, grid=(blocks,),
        in_specs=[pl.BlockSpec(s, lambda j: (j, 0)) for s in shapes],
        out_specs=[pl.BlockSpec(s, lambda j: (0, 0)) for s in shapes],
        out_shape=[jax.ShapeDtypeStruct(s, F32) for s in shapes],
        compiler_params=_params(("arbitrary",)),
    )(*parts)


def _adamw_update(w, g, m, v):
    m = ADAM_B1 * m + (1.0 - ADAM_B1) * g
    v = ADAM_B2 * v + (1.0 - ADAM_B2) * (g * g)
    m_hat = m * (1.0 / (1.0 - ADAM_B1 ** ADAM_STEP))
    v_hat = v * (1.0 / (1.0 - ADAM_B2 ** ADAM_STEP))
    return -ADAM_LR * (m_hat / (jnp.sqrt(v_hat) + ADAM_EPS) + ADAM_WD * w), m, v


def _sum_adamw(parts, w, m, v, blocks, name):
    shape = w.shape

    def body(p_ref, w_ref, m_ref, v_ref, g_ref, d_ref, mo_ref, vo_ref):
        j = pl.program_id(0)
        part = p_ref[...].astype(F32)

        @pl.when(j == 0)
        def _():
            g_ref[...] = part

        @pl.when(j > 0)
        def _():
            g_ref[...] += part

        @pl.when(j == blocks - 1)
        def _():
            d_ref[...], mo_ref[...], vo_ref[...] = _adamw_update(w_ref[...], g_ref[...], m_ref[...], v_ref[...])

    held = pl.BlockSpec(shape, lambda j: (0, 0))
    return pl.pallas_call(
        body, name=name, grid=(blocks,),
        in_specs=[pl.BlockSpec(shape, lambda j: (j, 0)), held, held, held],
        out_specs=[held] * 4, out_shape=[jax.ShapeDtypeStruct(shape, F32)] * 4,
        compiler_params=_params(("arbitrary",)),
    )(parts, w, m, v)


def _adamw(ws, gs, ms, vs, name):
    n = len(ws)

    def body(*refs):
        w_r, g_r, m_r, v_r = refs[:n], refs[n:2 * n], refs[2 * n:3 * n], refs[3 * n:4 * n]
        d_o, m_o, v_o = refs[4 * n:5 * n], refs[5 * n:6 * n], refs[6 * n:7 * n]
        for k in range(n):
            d_o[k][...], m_o[k][...], v_o[k][...] = _adamw_update(w_r[k][...], g_r[k][...], m_r[k][...], v_r[k][...])

    shapes = [jax.ShapeDtypeStruct(w.shape, F32) for w in ws]
    outs = pl.pallas_call(
        body, name=name, in_specs=[VMEM_SPEC] * (4 * n), out_specs=[VMEM_SPEC] * (3 * n), out_shape=shapes * 3,
        compiler_params=pltpu.CompilerParams(vmem_limit_bytes=56 << 20),
    )(*ws, *gs, *ms, *vs)
    return outs[:n], outs[n:2 * n], outs[2 * n:]


WEIGHT_NAMES = ("meta_tokens", "ffn1_pre_norm", "ffn1_w_gate", "ffn1_w_up", "ffn1_w_down", "ffn1_post_norm", "mix_pre_norm",
                "w_in", "gla_w_a2", "gla_b_a", "gla_out_norm", "swa_sinks", "swa_out_norm", "w_out", "mix_post_norm",
                "ffn2_pre_norm", "ffn2_w_gate", "ffn2_w_up", "ffn2_w_down", "ffn2_post_norm")
WIN_SHARD = D_IN // N_DEV
WIN_SHARD_PAD = 304
SLAB_VECTORS = ("ffn1_pre", "ffn1_post", "mix_pre", "mix_post", "ffn2_pre", "ffn2_post")
SLAB_ROWS = 32


def kernel(x, meta_tokens, ffn1_pre_norm, ffn1_w_gate, ffn1_w_up, ffn1_w_down, ffn1_post_norm, mix_pre_norm, w_in, gla_w_a2, gla_b_a, gla_out_norm, swa_sinks, swa_out_norm, w_out, mix_post_norm, ffn2_pre_norm, ffn2_w_gate, ffn2_w_up, ffn2_w_down, ffn2_post_norm, loss_target, m_meta_tokens, m_ffn1_pre_norm, m_ffn1_w_gate, m_ffn1_w_up, m_ffn1_w_down, m_ffn1_post_norm, m_mix_pre_norm, m_w_in, m_gla_w_a2, m_gla_b_a, m_gla_out_norm, m_swa_sinks, m_swa_out_norm, m_w_out, m_mix_post_norm, m_ffn2_pre_norm, m_ffn2_w_gate, m_ffn2_w_up, m_ffn2_w_down, m_ffn2_post_norm, v_meta_tokens, v_ffn1_pre_norm, v_ffn1_w_gate, v_ffn1_w_up, v_ffn1_w_down, v_ffn1_post_norm, v_mix_pre_norm, v_w_in, v_gla_w_a2, v_gla_b_a, v_gla_out_norm, v_swa_sinks, v_swa_out_norm, v_w_out, v_mix_post_norm, v_ffn2_pre_norm, v_ffn2_w_gate, v_ffn2_w_up, v_ffn2_w_down, v_ffn2_post_norm):
    given = dict(locals())
    W = {n: given[n] for n in WEIGHT_NAMES}
    M = {n: given["m_" + n] for n in WEIGHT_NAMES}
    V = {n: given["v_" + n] for n in WEIGHT_NAMES}
    dev = _dev_index(*_place_on_mesh())

    def t16(w):
        return w[0].T.astype(BF16)

    small = jnp.concatenate([W["meta_tokens"], jnp.pad(W["gla_w_a2"][0], ((0, 0), (0, 96)))], axis=0)
    wg1, wu1, wd1, small_g, gathered_zeros = _all_gather(
        [t16(W["ffn1_w_gate"]), t16(W["ffn1_w_up"]), W["ffn1_w_down"][0].astype(BF16), small])
    def after_zero(shard, zeros):
        return shard + zeros[0:1, 0:1].astype(shard.dtype)
    win_shard = jnp.pad(t16(W["w_in"]), ((0, WIN_SHARD_PAD - WIN_SHARD), (0, 0)))
    win_shard = after_zero(win_shard, gathered_zeros)
    mid = _exchange_start([win_shard], GATHER, "gather_w_in_start")
    late_shards = [after_zero(W["w_out"][0].astype(BF16), mid[3]), t16(W["ffn2_w_gate"]), t16(W["ffn2_w_up"]),
                   W["ffn2_w_down"][0].astype(BF16)]
    late = _exchange_start(late_shards, GATHER, "gather_late_weights_start")

    def late_weights(what, after):
        if what == "win":
            win_g, = _exchange_wait(mid, GATHER, after, "gather_w_in_wait")
            win_t = win_g.reshape(N_DEV, WIN_SHARD_PAD, D_MODEL)[:, :WIN_SHARD].reshape(D_IN, D_MODEL)
            return dict(win=_win_pad_rows(win_t))
        wout, wg2, wu2, wd2 = _exchange_wait(late, GATHER, after, "gather_late_weights_wait")
        return dict(wout=wout, wg2=wg2, wu2=wu2, wd2=wd2)

    small_g = small_g.reshape(N_DEV, 32, 128)
    meta_full = small_g[:, :N_META].transpose(1, 0, 2).reshape(N_META, D_MODEL)
    wa2_full = small_g[:, N_META:, :32].transpose(1, 0, 2).reshape(16, 256)
    w = dict(
        ffn1_pre=W["ffn1_pre_norm"] + late[3][0, 0], ffn1_post=W["ffn1_post_norm"], mix_pre=W["mix_pre_norm"],
        mix_post=W["mix_post_norm"], ffn2_pre=W["ffn2_pre_norm"], ffn2_post=W["ffn2_post_norm"], b_a=W["gla_b_a"],
        gla_norm=W["gla_out_norm"], sinks=W["swa_sinks"], swa_norm=W["swa_out_norm"], wg1=wg1, wu1=wu1, wd1=wd1,
        wa2=jnp.pad(wa2_full, ((0, 112), (0, 0))))

    in_flight = []

    def on_grads(group, grads):
        parts = []
        for nm, p in grads.items():
            if nm == "win":
                p = _win_unpad_rows(p).reshape(N_DEV, WIN_SHARD, D_MODEL)
                p = jnp.pad(p, ((0, 0), (0, WIN_SHARD_PAD - WIN_SHARD), (0, 0))).reshape(N_DEV * WIN_SHARD_PAD, D_MODEL)
            parts.append(p)
        kind = SCATTER if group == "ffn2" else SCATTER_CHIPS
        if kind == SCATTER_CHIPS:
            parts = [_sibling_reduce(p, "pair_" + group + "_" + nm) for nm, p in zip(grads, parts)]
        started = _exchange_start(parts, kind, "scatter_" + group + "_start")
        in_flight.append((group, list(grads), started, kind))
        return started[3]

    small_flight = []

    def on_small(loss, dh0, g):
        packed = jnp.concatenate([g["b_a"][0:1], g["gla_norm"][0:1], g["sinks"][0:1], g["swa_norm"][0:1]], axis=1)
        slab = jnp.concatenate([g[k][0:1] for k in SLAB_VECTORS] + [packed, jnp.full((1, D_MODEL), loss, F32),
                               g["wa2"][:16].reshape(4, D_MODEL), jnp.zeros((4, D_MODEL), F32), dh0[PAD_ROWS:BLK]], axis=0)
        small_flight.append(_exchange_start([slab], GATHER, "gather_small_grads_start"))
        return small_flight[0][3]

    front = jnp.concatenate([jnp.zeros((PAD_ROWS, D_MODEL), F32), meta_full], axis=0)
    loss, dh0, g = _local_step(x[0], loss_target[0], front, w, late_weights, on_grads, on_small)
    grad_x = dh0[BLK:][None]

    land, = _exchange_wait(small_flight[0], GATHER, in_flight[-1][2][3], "gather_small_grads_wait")
    tot = _sum_partials([land], "sum_small_grads")[0]
    loss = tot[7, 0]
    small_grads = dict(
        ffn1_pre_norm=tot[0:1], ffn1_post_norm=tot[1:2], mix_pre_norm=tot[2:3], mix_post_norm=tot[3:4],
        ffn2_pre_norm=tot[4:5], ffn2_post_norm=tot[5:6], gla_b_a=tot[6:7, 0:256], gla_out_norm=tot[6:7, 256:384],
        swa_sinks=tot[6:7, 384:392], swa_out_norm=tot[6:7, 512:1024],
        gla_w_a2=lax.dynamic_slice_in_dim(tot[8:12].reshape(16, 256), dev * 32, 32, axis=1)[None],
        meta_tokens=lax.dynamic_slice_in_dim(tot[16:32], dev * 128, 128, axis=1))

    big = dict(wg1=("ffn1_w_gate", True), wu1=("ffn1_w_up", True), wd1=("ffn1_w_down", False), win=("w_in", True),
               wout=("w_out", False), wg2=("ffn2_w_gate", True), wu2=("ffn2_w_up", True), wd2=("ffn2_w_down", False))
    grads = dict(small_grads)
    delta, new_m, new_v = {}, {}, {}
    names = [n for n in WEIGHT_NAMES if n not in [full for full, _ in big.values()]]
    two_d = lambda a: a.reshape(-1, a.shape[-1])
    d_, m_, v_ = _adamw([two_d(W[n]) for n in names], [two_d(grads[n]) for n in names],
                        [two_d(M[n]) for n in names], [two_d(V[n]) for n in names], "adamw_small")
    for k, n in enumerate(names):
        delta[n], new_m[n], new_v[n] = d_[k].reshape(W[n].shape), m_[k].reshape(W[n].shape), v_[k].reshape(W[n].shape)

    before_wait = d_[0] + in_flight[-1][2][3][0, 0]
    for group, shorts, started, kind in in_flight:
        lands = _exchange_wait(started, kind, before_wait, "scatter_" + group + "_wait")
        blocks = 4 if kind == SCATTER_CHIPS else N_DEV
        for short, land in zip(shorts, lands):
            n, transposed = big[short]
            to_slab = (lambda a: a[0].T) if transposed else (lambda a: a[0])
            from_slab = (lambda a: a.T[None]) if transposed else (lambda a: a[None])
            if short == "win":
                g_slab = _sum_partials([land], "sum_" + n, blocks)[0][:WIN_SHARD]
                d_, m_, v_ = _adamw([to_slab(W[n])], [g_slab], [to_slab(M[n])], [to_slab(V[n])], "adamw_" + n)
                d_, m_, v_ = d_[0], m_[0], v_[0]
            else:
                g_slab, d_, m_, v_ = _sum_adamw(land, to_slab(W[n]), to_slab(M[n]), to_slab(V[n]), blocks, "adamw_" + n)
            grads[n], delta[n], new_m[n], new_v[n] = from_slab(g_slab), from_slab(d_), from_slab(m_), from_slab(v_)
            before_wait = d_
    return (loss, grad_x, *[grads[n] for n in WEIGHT_NAMES], *[delta[n] for n in WEIGHT_NAMES],
            *[new_m[n] for n in WEIGHT_NAMES], *[new_v[n] for n in WEIGHT_NAMES])
```

```python
import math

import jax
import jax.numpy as jnp
from jax import lax
from jax.experimental import pallas as pl
from jax.experimental.pallas import tpu as pltpu

F32, BF16 = jnp.float32, jnp.bfloat16

D_MODEL = 1024
D_FF = 2816
N_META = 16
BLK = 128
PAD_ROWS = BLK - N_META
GLA_DK = 64
SWA_HD = 64
SWA_HEADS = 8
GLA_TAU = 16.0
NORM_EPS = 1e-6
NEG_INF = -1e30
ROPE_THETA = 10000.0
P_GQ, P_GK, P_GV, P_GG, P_SQ, P_SK, P_SV, P_GA, P_END = 0, 256, 512, 1024, 1536, 2048, 2176, 2304, 2432
D_IN = 2320
IN_SPLITS = (256, 256, 512, 512, 16, 512, 128, 128)
FF_TILE = 2816
WGRAD_TILE_MAX = 2432
N_DEV = 8
MESH = pl.DeviceIdType.MESH

ADAM_LR, ADAM_B1, ADAM_B2, ADAM_EPS, ADAM_WD, ADAM_STEP = 0.001, 0.9, 0.999, 1e-08, 0.01, 10

V7X_VMEM_BYTES = 64 << 20
VMEM_SPEC = pl.BlockSpec(memory_space=pltpu.VMEM)
SMEM_SPEC = pl.BlockSpec(memory_space=pltpu.SMEM)
ANY_SPEC = pl.BlockSpec(memory_space=pl.ANY)


def _params(semantics, vmem_mb=56):
    return pltpu.CompilerParams(dimension_semantics=semantics, vmem_limit_bytes=vmem_mb << 20)


def _row_tile(rows):
    return 416 if rows % 416 == 0 else BLK


def _blocks_per_step(blocks):
    return 5 if blocks % 5 == 0 else 1


def _nn(a, b):
    return lax.dot_general(a, b, (((1,), (0,)), ((), ())), preferred_element_type=F32)


def _nt(a, b):
    return lax.dot_general(a, b, (((1,), (1,)), ((), ())), preferred_element_type=F32)


def _tn(a, b):
    return lax.dot_general(a, b, (((0,), (0,)), ((), ())), preferred_element_type=F32)


def _rms(x):
    r = lax.rsqrt(jnp.mean(x * x, axis=-1, keepdims=True) + NORM_EPS)
    return x * r, r


def _rms_bwd(xn, r, w, dy):
    g = dy * w
    return r * (g - xn * jnp.mean(g * xn, axis=-1, keepdims=True))


def _sigmoid(x):
    return 1.0 / (1.0 + jnp.exp(-x))


def _colsum(x):
    return jnp.sum(x, axis=0, keepdims=True)


def _split_bf16(x):
    hi = x.astype(BF16)
    lo = (x - hi.astype(F32)).astype(BF16)
    return hi, lo


def _tri(lower):
    r = lax.broadcasted_iota(jnp.int32, (BLK, BLK), 0)
    c = lax.broadcasted_iota(jnp.int32, (BLK, BLK), 1)
    return (r >= c) if lower else (c >= r)


def _half_mask(width, half):
    lane = lax.broadcasted_iota(jnp.int32, (1, width), 1)
    return ((lane % 128) < 64) if half == 0 else ((lane % 128) >= 64)


def _rot_half(x):
    w = x.shape[-1]
    lane = lax.broadcasted_iota(jnp.int32, (1, w), 1)
    return jnp.where((lane % SWA_HD) < SWA_HD // 2, -pltpu.roll(x, w - SWA_HD // 2, 1), pltpu.roll(x, SWA_HD // 2, 1))


def _row_spec(tm, cols):
    return pl.BlockSpec((tm, cols), lambda i: (i, 0))


def _acc_spec(cols):
    return pl.BlockSpec((8, cols), lambda i: (0, 0))


def _acc_add(ref, first, value):
    @pl.when(first)
    def _():
        ref[...] = jnp.zeros_like(ref)
    ref[0:1, :] += value


def _behind_front(ref, i, tm, front):
    blk = ref[...]
    return jnp.where(i == 0, jnp.concatenate([front, blk[0:tm - BLK]], axis=0), blk)


def _ffn_fwd(h, gpre, wg_t, wu_t, wd, gpost, tgt=None, front=None):
    with_loss, with_front = tgt is not None, front is not None
    rows = h.shape[0] + (BLK if with_front else 0)
    tm = _row_tile(rows)
    nf = D_FF // FF_TILE

    def body(*refs):
        refs = list(refs)
        h_ref, gpre_ref, wg_ref, wu_ref, wd_ref, gpost_ref = refs[:6]
        del refs[:6]
        front_ref = refs.pop(0) if with_front else None
        t_ref = refs.pop(0) if with_loss else None
        h0_ref = refs.pop(0) if with_front else None
        ho_ref, a_ref, b_ref, s_ref, f_ref = refs[:5]
        dy_ref, loss_ref = refs[5:7] if with_loss else (None, None)
        acc = refs[-1]
        i = pl.program_id(0)
        if with_front:
            h_in = _behind_front(h_ref, i, tm, front_ref[...])
            h0_ref[...] = h_in
        else:
            h_in = h_ref[...]
        hn, _ = _rms(h_in)
        n16 = (hn * gpre_ref[...]).astype(BF16)
        for j in range(nf):
            cols = slice(j * FF_TILE, (j + 1) * FF_TILE)
            a = _nt(n16, wg_ref[cols, :])
            b = _nt(n16, wu_ref[cols, :])
            a_ref[:, cols] = a.astype(BF16)
            b_ref[:, cols] = b.astype(BF16)
            s16 = (a * _sigmoid(a) * b).astype(BF16)
            s_ref[:, cols] = s16
            part = _nn(s16, wd_ref[cols, :])
            if j == 0:
                acc[...] = part
            else:
                acc[...] += part
        f = acc[...]
        f_ref[...] = f
        fn, _ = _rms(f)
        y = h_in + 0.5 * (fn * gpost_ref[...])
        ho_ref[...] = y
        if with_loss:
            row = i * tm + lax.broadcasted_iota(jnp.int32, (tm, 1), 0)
            err = jnp.where(row >= BLK, y - _behind_front(t_ref, i, tm, jnp.zeros((BLK, D_MODEL), F32)), 0.0)
            dy_ref[...] = err * (1.0 / D_MODEL)
            part = 0.5 * jnp.sum(jnp.sum(err * err, axis=-1, keepdims=True) * (1.0 / D_MODEL), axis=0, keepdims=True)

            @pl.when(i == 0)
            def _():
                loss_ref[...] = jnp.zeros_like(loss_ref)
            loss_ref[...] += part

    row_f32 = _row_spec(tm, D_MODEL)
    behind = pl.BlockSpec((pl.Element(tm), pl.Element(D_MODEL)),
                          lambda i: (pl.multiple_of(jnp.maximum(i * tm - BLK, 0), math.gcd(tm, BLK)), 0))
    in_specs = [behind if with_front else row_f32, VMEM_SPEC, VMEM_SPEC, VMEM_SPEC, VMEM_SPEC, VMEM_SPEC]
    out_specs = [row_f32, _row_spec(tm, D_FF), _row_spec(tm, D_FF), _row_spec(tm, D_FF), row_f32]
    out_shape = [jax.ShapeDtypeStruct((rows, D_MODEL), F32), jax.ShapeDtypeStruct((rows, D_FF), BF16),
                 jax.ShapeDtypeStruct((rows, D_FF), BF16), jax.ShapeDtypeStruct((rows, D_FF), BF16),
                 jax.ShapeDtypeStruct((rows, D_MODEL), F32)]
    args = [h, gpre, wg_t, wu_t, wd, gpost]
    if with_front:
        in_specs.append(VMEM_SPEC)
        args.append(front)
        out_specs.insert(0, row_f32)
        out_shape.insert(0, jax.ShapeDtypeStruct((rows, D_MODEL), F32))
    if with_loss:
        in_specs.append(behind)
        args.append(tgt)
        out_specs += [row_f32, pl.BlockSpec((8, 128), lambda i: (0, 0))]
        out_shape += [jax.ShapeDtypeStruct((rows, D_MODEL), F32), jax.ShapeDtypeStruct((8, 128), F32)]
    return pl.pallas_call(
        body, name="ffn_fwd_loss" if with_loss else "ffn_fwd", grid=(rows // tm,),
        in_specs=in_specs, out_specs=out_specs, out_shape=out_shape,
        scratch_shapes=[pltpu.VMEM((tm, D_MODEL), F32)],
        compiler_params=_params(("arbitrary",)),
    )(*args)


def _ffn_bwd_act(dh_out, h, a, b, f, gpre, gpost, wg_t, wu_t, wd, name):
    rows = h.shape[0]
    tm = _row_tile(rows)
    nf = D_FF // FF_TILE

    def body(dho_ref, h_ref, a_ref, b_ref, f_ref, gpre_ref, gpost_ref, wg_ref, wu_ref, wd_ref,
             dh_ref, da_ref, db_ref, df_ref, n_ref, dgpre_ref, dgpost_ref, acc):
        first = pl.program_id(0) == 0
        dho = dho_ref[...]
        drr = 0.5 * dho
        fn, rf = _rms(f_ref[...])
        _acc_add(dgpost_ref, first, _colsum(drr * fn))
        df16 = _rms_bwd(fn, rf, gpost_ref[...], drr).astype(BF16)
        df_ref[...] = df16
        hn, rh = _rms(h_ref[...])
        n_ref[...] = (hn * gpre_ref[...]).astype(BF16)
        for j in range(nf):
            cols = slice(j * FF_TILE, (j + 1) * FF_TILE)
            ds = _nt(df16, wd_ref[cols, :])
            av = a_ref[:, cols].astype(F32)
            bv = b_ref[:, cols].astype(F32)
            sg = _sigmoid(av)
            db16 = (ds * (av * sg)).astype(BF16)
            da16 = (ds * bv * (sg * (1.0 + av * (1.0 - sg)))).astype(BF16)
            da_ref[:, cols] = da16
            db_ref[:, cols] = db16
            part = _nn(da16, wg_ref[cols, :]) + _nn(db16, wu_ref[cols, :])
            if j == 0:
                acc[...] = part
            else:
                acc[...] += part
        dn = acc[...]
        _acc_add(dgpre_ref, first, _colsum(dn * hn))
        dh_ref[...] = dho + _rms_bwd(hn, rh, gpre_ref[...], dn)

    row_f32 = _row_spec(tm, D_MODEL)
    row_ff = _row_spec(tm, D_FF)
    return pl.pallas_call(
        body, name=name, grid=(rows // tm,),
        in_specs=[row_f32, row_f32, row_ff, row_ff, row_f32, VMEM_SPEC, VMEM_SPEC, VMEM_SPEC, VMEM_SPEC, VMEM_SPEC],
        out_specs=[row_f32, row_ff, row_ff, row_f32, row_f32, _acc_spec(D_MODEL), _acc_spec(D_MODEL)],
        out_shape=[jax.ShapeDtypeStruct((rows, D_MODEL), F32), jax.ShapeDtypeStruct((rows, D_FF), BF16),
                   jax.ShapeDtypeStruct((rows, D_FF), BF16), jax.ShapeDtypeStruct((rows, D_MODEL), BF16),
                   jax.ShapeDtypeStruct((rows, D_MODEL), BF16), jax.ShapeDtypeStruct((8, D_MODEL), F32),
                   jax.ShapeDtypeStruct((8, D_MODEL), F32)],
        scratch_shapes=[pltpu.VMEM((tm, D_MODEL), F32)],
        compiler_params=_params(("arbitrary",), vmem_mb=62),
    )(dh_out, h, a, b, f, gpre, gpost, wg_t, wu_t, wd)


def _wgrad(lhs, rhs, name, after=None):
    rows, width = lhs.shape
    tm = rows if rows % 1664 == 0 else BLK
    tf = 256 if width % 256 == 0 else 128
    nr = rows // tm

    def body(l_ref, r_ref, *rest):
        o_ref, acc = rest[-2:]
        i = pl.program_id(1)
        part = _tn(l_ref[...], r_ref[...])

        @pl.when(i == 0)
        def _():
            acc[...] = part

        @pl.when(i > 0)
        def _():
            acc[...] += part

        @pl.when(i == nr - 1)
        def _():
            o_ref[...] = acc[...].astype(BF16)

    l_spec = pl.BlockSpec((tm, tf), lambda j, i: (i, j))
    r_spec = pl.BlockSpec((tm, D_MODEL), lambda j, i: (i, 0))
    return pl.pallas_call(
        body, name=name, grid=(width // tf, nr),
        in_specs=[l_spec, r_spec] + ([] if after is None else [ANY_SPEC]),
        out_specs=pl.BlockSpec((tf, D_MODEL), lambda j, i: (j, 0)),
        out_shape=jax.ShapeDtypeStruct((width, D_MODEL), BF16),
        scratch_shapes=[pltpu.VMEM((tf, D_MODEL), F32)],
        compiler_params=_params(("arbitrary", "arbitrary")),
    )(lhs, rhs, *([] if after is None else [after]))


def _chunk_cumsum(x, lower):
    tri = jnp.where(_tri(lower), 1.0, 0.0).astype(BF16)
    hi, lo = _split_bf16(x)
    return _nn(tri, hi) + _nn(tri, lo)


def _mix_in(h, g, win_p, wa2_p, b_a, cos, sin):
    rows = h.shape[0]
    tm = 640 if rows % 640 == 0 else BLK

    def body(h_ref, g_ref, win_ref, wa2_ref, ba_ref, cos_ref, sin_ref,
             gq_ref, gk_ref, gv_ref, gg_ref, sq_ref, sk_ref, sv_ref, ga_ref, loga_ref, bc_ref, n_ref):
        hn, _ = _rms(h_ref[...])
        n16 = (hn * g_ref[...]).astype(BF16)
        n_ref[...] = n16
        proj = _nt(n16, win_ref[...])
        gq_ref[...] = proj[:, P_GQ:P_GK]
        gk_ref[...] = proj[:, P_GK:P_GV]
        gv_ref[...] = proj[:, P_GV:P_GG].astype(BF16)
        gg_ref[...] = proj[:, P_GG:P_SQ]
        c1, s1 = cos_ref[...], sin_ref[...]
        c4 = jnp.concatenate([c1, c1, c1, c1], axis=1)
        s4 = jnp.concatenate([s1, s1, s1, s1], axis=1)
        sq = proj[:, P_SQ:P_SK]
        sk = proj[:, P_SK:P_SV]
        sq_ref[...] = (sq * c4 + _rot_half(sq) * s4).astype(BF16)
        sk_ref[...] = (sk * c1 + _rot_half(sk) * s1).astype(BF16)
        sv_ref[...] = proj[:, P_SV:P_GA].astype(BF16)
        ga = proj[:, P_GA:P_END]
        ga_ref[...] = ga
        z = _nn(ga, wa2_ref[...]) + ba_ref[...]
        loga = (jnp.minimum(z, 0.0) - jnp.log(1.0 + jnp.exp(-jnp.abs(z)))) * (1.0 / GLA_TAU)
        loga_ref[...] = loga
        for c in range(tm // BLK):
            rs = slice(c * BLK, (c + 1) * BLK)
            bc_ref[rs, :] = _chunk_cumsum(loga[rs, :], True)

    f32 = lambda c: jax.ShapeDtypeStruct((rows, c), F32)
    b16 = lambda c: jax.ShapeDtypeStruct((rows, c), BF16)
    rs = lambda c: _row_spec(tm, c)
    return pl.pallas_call(
        body, name="mix_in", grid=(rows // tm,),
        in_specs=[rs(D_MODEL), VMEM_SPEC, VMEM_SPEC, VMEM_SPEC, VMEM_SPEC, rs(128), rs(128)],
        out_specs=[rs(256), rs(256), rs(512), rs(512), rs(512), rs(128), rs(128), rs(128), rs(256), rs(256), rs(D_MODEL)],
        out_shape=[f32(256), f32(256), b16(512), f32(512), b16(512), b16(128), b16(128), f32(128), f32(256), f32(256),
                   b16(D_MODEL)],
        compiler_params=_params(("arbitrary",)),
    )(h, g, win_p, wa2_p, b_a, cos, sin)


def _gla_factors(q, k, bc):
    bm = bc[BLK // 2 - 1:BLK // 2, :]
    bl = bc[BLK - 1:BLK, :]
    e_q, e_k, e_qe, e_kd = jnp.exp(bc - bm), jnp.exp(bm - bc), jnp.exp(bc), jnp.exp(bl - bc)
    return (q * e_q, k * e_k, q * e_qe, k * e_kd), (e_q, e_k, e_qe, e_kd), jnp.exp(bl)


def _gla_fwd(gq, gk, gv, gg, bc, wgn):
    rows = gq.shape[0]
    nc = rows // BLK
    per_step = _blocks_per_step(nc)
    scale = GLA_DK ** -0.5

    def body(q_ref, k_ref, v_ref, gg_ref, bc_ref, wgn_ref, o_ref, cat_ref, sp_ref, st):
        @pl.when(pl.program_id(0) == 0)
        def _():
            st[...] = jnp.zeros_like(st)
        low = _tri(True)
        wgn_v = wgn_ref[...]

        def chunk(c, carry):
            rr = pl.ds(pl.multiple_of(c * BLK, BLK), BLK)
            for p in range(2):
                sl = slice(128 * p, 128 * p + 128)
                (qt, kt, qe, kd), _, ebl = _gla_factors(q_ref[rr, sl] * scale, k_ref[rr, sl], bc_ref[rr, sl])
                s_prev = st[p]
                sp_ref[c, p] = s_prev
                s16 = s_prev.astype(BF16)
                qt16 = qt.astype(BF16)
                s_new = s_prev * ebl
                for hh in range(2):
                    hs = slice(128 * (2 * p + hh), 128 * (2 * p + hh) + 128)
                    lm = _half_mask(128, hh)
                    vh = v_ref[rr, hs]
                    pm = jnp.where(low, _nt(qt16, jnp.where(lm, kt, 0.0).astype(BF16)), 0.0)
                    o = _nn(pm.astype(BF16), vh) + _nt(jnp.where(lm, qe, 0.0).astype(BF16), s16)
                    s_new = s_new + _tn(vh, jnp.where(lm, kd, 0.0).astype(BF16))
                    o_ref[rr, hs] = o
                    on, _ = _rms(o)
                    gate = gg_ref[rr, hs]
                    cat_ref[rr, hs] = (on * wgn_v * (gate * _sigmoid(gate))).astype(BF16)
                st[p] = s_new
            return carry
        lax.fori_loop(0, per_step, chunk, 0)

    rs = lambda c: _row_spec(per_step * BLK, c)
    return pl.pallas_call(
        body, name="gla_fwd", grid=(nc // per_step,),
        in_specs=[rs(256), rs(256), rs(512), rs(512), rs(256), VMEM_SPEC],
        out_specs=[rs(512), rs(512), pl.BlockSpec((per_step, 2, 128, 128), lambda i: (i, 0, 0, 0))],
        out_shape=[jax.ShapeDtypeStruct((rows, 512), F32), jax.ShapeDtypeStruct((rows, 512), BF16),
                   jax.ShapeDtypeStruct((nc, 2, 128, 128), F32)],
        scratch_shapes=[pltpu.VMEM((2, 128, 128), F32)],
        compiler_params=_params(("arbitrary",)),
    )(gq, gk, gv, gg, bc, wgn)


def _gla_bwd(dcat, o_all, gq, gk, gv, gg, bc, sp, wgn):
    rows = gq.shape[0]
    nc = rows // BLK
    per_step = _blocks_per_step(nc)
    steps = nc // per_step
    scale = GLA_DK ** -0.5

    def body(dc_ref, o_ref, q_ref, k_ref, v_ref, gg_ref, bc_ref, sp_ref, wgn_ref,
             dq_ref, dk_ref, dv_ref, dgg_ref, dla_ref, dwgn_ref, dst):
        first = pl.program_id(0) == 0

        @pl.when(first)
        def _():
            dst[...] = jnp.zeros_like(dst)
        low, upp = _tri(True), _tri(False)
        last_row = lax.broadcasted_iota(jnp.int32, (BLK, 1), 0) == BLK - 1
        wgn_v = wgn_ref[...]

        def chunk(c, dwgn):
            rr = pl.ds(pl.multiple_of((per_step - 1 - c) * BLK, BLK), BLK)
            for p in range(2):
                sl = slice(128 * p, 128 * p + 128)
                (qt, kt, qe, kd), (e_q, e_k, e_qe, e_kd), ebl = _gla_factors(
                    q_ref[rr, sl] * scale, k_ref[rr, sl], bc_ref[rr, sl])
                s_prev = sp_ref[per_step - 1 - c, p]
                s16 = s_prev.astype(BF16)
                ds_next = dst[p]
                ds16 = ds_next.astype(BF16)
                qt16 = qt.astype(BF16)
                ds_new = ds_next * ebl
                dqt = jnp.zeros((BLK, 128), F32)
                dkt = jnp.zeros((BLK, 128), F32)
                dqe = jnp.zeros((BLK, 128), F32)
                dkd = jnp.zeros((BLK, 128), F32)
                for hh in range(2):
                    hs = slice(128 * (2 * p + hh), 128 * (2 * p + hh) + 128)
                    lm = _half_mask(128, hh)
                    on, ro = _rms(o_ref[rr, hs])
                    gate = gg_ref[rr, hs]
                    sg = _sigmoid(gate)
                    si = gate * sg
                    dog = dc_ref[rr, hs]
                    dwgn = dwgn + _colsum(dog * si * on)
                    dgg_ref[rr, hs] = dog * (on * wgn_v) * (sg * (1.0 + gate * (1.0 - sg)))
                    do16 = _rms_bwd(on, ro, wgn_v, dog * si).astype(BF16)
                    vh = v_ref[rr, hs]
                    ktm16 = jnp.where(lm, kt, 0.0).astype(BF16)
                    qtm16 = jnp.where(lm, qt, 0.0).astype(BF16)
                    qem16 = jnp.where(lm, qe, 0.0).astype(BF16)
                    kdm16 = jnp.where(lm, kd, 0.0).astype(BF16)
                    p_t = jnp.where(upp, _nt(ktm16, qt16), 0.0)
                    dp_t = jnp.where(upp, _nt(vh, do16), 0.0)
                    dp = jnp.where(low, _nt(do16, vh), 0.0)
                    dv_ref[rr, hs] = _nn(p_t.astype(BF16), do16) + _nt(kdm16, ds16)
                    dqt = dqt + _nn(dp.astype(BF16), ktm16)
                    dkt = dkt + _nn(dp_t.astype(BF16), qtm16)
                    dqe = dqe + jnp.where(lm, _nn(do16, s16), 0.0)
                    dkd = dkd + jnp.where(lm, _nn(vh, ds16), 0.0)
                    ds_new = ds_new + _tn(do16, qem16)
                debl = _colsum(ds_next * s_prev)
                dq_ref[rr, sl] = (dqt * e_q + dqe * e_qe) * scale
                dk_ref[rr, sl] = dkt * e_k + dkd * e_kd
                dkd_kd = dkd * kd
                db = dqt * qt - dkt * kt + dqe * qe - dkd_kd
                db = jnp.where(last_row, db + (_colsum(dkd_kd) + debl * ebl), db)
                dla_ref[rr, sl] = _chunk_cumsum(db, False)
                dst[p] = ds_new
            return dwgn
        dwgn = lax.fori_loop(0, per_step, chunk, jnp.zeros((1, 128), F32))
        _acc_add(dwgn_ref, first, dwgn)

    rev = lambda c: pl.BlockSpec((per_step * BLK, c), lambda i: (steps - 1 - i, 0))
    f32 = lambda c: jax.ShapeDtypeStruct((rows, c), F32)
    return pl.pallas_call(
        body, name="gla_bwd", grid=(steps,),
        in_specs=[rev(512), rev(512), rev(256), rev(256), rev(512), rev(512), rev(256),
                  pl.BlockSpec((per_step, 2, 128, 128), lambda i: (steps - 1 - i, 0, 0, 0)), VMEM_SPEC],
        out_specs=[rev(256), rev(256), rev(512), rev(512), rev(256), _acc_spec(128)],
        out_shape=[f32(256), f32(256), f32(512), f32(512), f32(256), jax.ShapeDtypeStruct((8, 128), F32)],
        scratch_shapes=[pltpu.VMEM((2, 128, 128), F32)],
        compiler_params=_params(("arbitrary",)),
    )(dcat, o_all, gq, gk, gv, gg, bc, sp, wgn)


def _swa_masks(i):
    t = lax.broadcasted_iota(jnp.int32, (BLK, BLK), 0)
    c = lax.broadcasted_iota(jnp.int32, (BLK, BLK), 1)
    own_side = c <= t
    band_ok = i >= jnp.where(own_side, 1, 2)
    meta_ok = (c % N_META) <= jnp.where(i >= 1, N_META, t - PAD_ROWS)
    return own_side, band_ok, meta_ok, c // N_META


def _swa_blocks(ref, i):
    prev = pl.multiple_of(jnp.maximum(i - 1, 0) * BLK, BLK)
    own = pl.multiple_of(i * BLK, BLK)
    return jnp.concatenate([ref[pl.ds(prev, BLK), :], ref[pl.ds(own, BLK), :]], axis=0), prev, own


def _swa_meta_operand(ref):
    blk = ref[0:BLK, :]
    swapped = pltpu.roll(blk, 64, 1)
    lo = jnp.where(_half_mask(128, 0), blk, swapped)
    hi = jnp.where(_half_mask(128, 1), blk, swapped)
    meta = jnp.concatenate([lo, lo, hi, hi], axis=1)[PAD_ROWS:BLK, :]
    tiled = jnp.concatenate([meta] * SWA_HEADS, axis=0)
    j = lax.broadcasted_iota(jnp.int32, tiled.shape, 0)
    lane = lax.broadcasted_iota(jnp.int32, tiled.shape, 1)
    return jnp.where(j // N_META == lane // SWA_HD, tiled, jnp.zeros_like(tiled))


def _swa_meta_fold(acc):
    out = jnp.zeros((N_META, 128), F32)
    for hd in range(SWA_HEADS):
        half, kv = hd % 2, hd // 4
        piece = acc[N_META * hd:N_META * (hd + 1), 128 * (hd // 2):128 * (hd // 2) + 128]
        piece = jnp.where(_half_mask(128, half), piece, 0.0)
        out = out + (piece if half == kv else pltpu.roll(piece, 64, 1))
    return out


def _by_head(group, per_head):
    out = jnp.zeros((BLK, BLK), F32)
    for hd, v in enumerate(per_head):
        out = jnp.where(group == hd, v, out)
    return out


def _place(x, kv):
    if kv == 0:
        lo = jnp.where(_half_mask(128, 0), x, jnp.zeros_like(x))
        return lo, pltpu.roll(lo, 64, 1)
    hi = jnp.where(_half_mask(128, 1), x, jnp.zeros_like(x))
    return pltpu.roll(hi, 64, 1), hi


def _swa_fwd(sq, sk, sv, sinks, wn):
    rows = sq.shape[0]
    nb = rows // BLK
    per_step = _blocks_per_step(nb)
    scale = SWA_HD ** -0.5

    def body(q_ref, k_ref, v_ref, sink_ref, wn_ref, o_ref, cat_ref, lse_ref, kp, vp):
        step = pl.program_id(0)

        @pl.when(step == 0)
        def _():
            kp[...] = _swa_meta_operand(k_ref)
            vp[...] = _swa_meta_operand(v_ref)

        def one_block(c, carry):
            i = step * per_step + c
            rr = pl.ds(pl.multiple_of(c * BLK, BLK), BLK)
            own_side, band_ok, meta_ok, group = _swa_masks(i)
            k2, _, _ = _swa_blocks(k_ref, i)
            v2, _, _ = _swa_blocks(v_ref, i)
            kz = (_place(k2, 0), _place(k2, 1))
            vz = (_place(v2, 0), _place(v2, 1))
            q_all = q_ref[rr, :]
            s_meta = jnp.where(meta_ok, _nt(q_all, kp[...]) * scale, NEG_INF)
            s_band, m = [], []
            for hd in range(SWA_HEADS):
                kv, half = hd // 4, hd % 2
                q_pair = q_all[:, 128 * (hd // 2):128 * (hd // 2) + 128]
                s2 = _nt(q_pair, kz[kv][half])
                s = jnp.where(band_ok, jnp.where(own_side, s2[:, BLK:], s2[:, :BLK]) * scale, NEG_INF)
                top = jnp.maximum(jnp.max(s, axis=-1, keepdims=True),
                                  jnp.max(jnp.where(group == hd, s_meta, NEG_INF), axis=-1, keepdims=True))
                s_band.append(s)
                m.append(jnp.maximum(top, sink_ref[0, hd]))
            e_meta = jnp.exp(s_meta - _by_head(group, m))
            o_meta = _nn(e_meta.astype(BF16), vp[...])
            outs = []
            for pr in range(4):
                o_pair = o_meta[:, 128 * pr:128 * pr + 128]
                rden = []
                for half in range(2):
                    hd = 2 * pr + half
                    kv = hd // 4
                    e = jnp.exp(s_band[hd] - m[hd])
                    den = (jnp.sum(e, axis=-1, keepdims=True)
                           + jnp.sum(jnp.where(group == hd, e_meta, 0.0), axis=-1, keepdims=True)
                           + jnp.exp(sink_ref[0, hd] - m[hd]))
                    lse_ref[rr, hd:hd + 1] = m[hd] + jnp.log(den)
                    rden.append(1.0 / den)
                    e2 = jnp.concatenate([jnp.where(own_side, 0.0, e), jnp.where(own_side, e, 0.0)], axis=1).astype(BF16)
                    o_pair = o_pair + _nn(e2, vz[kv][half])
                outs.append(o_pair * jnp.where(_half_mask(128, 0), rden[0], rden[1]))
            o = jnp.concatenate(outs, axis=1)
            o_ref[rr, :] = o
            on, _ = _rms(o)
            cat_ref[rr, :] = (on * wn_ref[...]).astype(BF16)
            return carry
        lax.fori_loop(0, per_step, one_block, 0)

    return pl.pallas_call(
        body, name="swa_fwd", grid=(nb // per_step,),
        in_specs=[_row_spec(per_step * BLK, 512), VMEM_SPEC, VMEM_SPEC, SMEM_SPEC, VMEM_SPEC],
        out_specs=[_row_spec(per_step * BLK, 512), _row_spec(per_step * BLK, 512), _row_spec(per_step * BLK, SWA_HEADS)],
        out_shape=[jax.ShapeDtypeStruct((rows, 512), F32), jax.ShapeDtypeStruct((rows, 512), BF16),
                   jax.ShapeDtypeStruct((rows, SWA_HEADS), F32)],
        scratch_shapes=[pltpu.VMEM((BLK, 512), BF16), pltpu.VMEM((BLK, 512), BF16)],
        compiler_params=_params(("arbitrary",)),
    )(sq, sk, sv, sinks, wn)


def _swa_bwd(dcat, o_all, sq, sk, sv, lse, sinks, wn):
    rows = sq.shape[0]
    nb = rows // BLK
    per_step = _blocks_per_step(nb)
    steps = nb // per_step
    scale = SWA_HD ** -0.5

    def body(dc_ref, o_ref, q_ref, k_ref, v_ref, lse_ref, sink_ref, wn_ref, dq_ref, dk_ref, dv_ref, dsink_ref, dwn_ref,
             kp, vp, dkp, dvp):
        step = pl.program_id(0)

        @pl.when(step == 0)
        def _():
            dk_ref[...] = jnp.zeros_like(dk_ref)
            dv_ref[...] = jnp.zeros_like(dv_ref)
            dkp[...] = jnp.zeros_like(dkp)
            dvp[...] = jnp.zeros_like(dvp)
            kp[...] = _swa_meta_operand(k_ref)
            vp[...] = _swa_meta_operand(v_ref)

        def one_block(c, carry):
            i = step * per_step + c
            rr = pl.ds(pl.multiple_of(c * BLK, BLK), BLK)
            first = i == 0
            own_side, band_ok, meta_ok, group = _swa_masks(i)
            k2, prev, own = _swa_blocks(k_ref, i)
            v2, _, _ = _swa_blocks(v_ref, i)
            kz = (_place(k2, 0), _place(k2, 1))
            vz = (_place(v2, 0), _place(v2, 1))
            o = o_ref[rr, :]
            on, ro = _rms(o)
            dc = dc_ref[rr, :]
            _acc_add(dwn_ref, first, _colsum(dc * on))
            do = _rms_bwd(on, ro, wn_ref[...], dc)
            do_o = do * o
            do16 = do.astype(BF16)
            q_all = q_ref[rr, :]
            lse = [lse_ref[rr, hd:hd + 1] for hd in range(SWA_HEADS)]
            delta = [jnp.sum(jnp.where(_half_mask(128, hd % 2), do_o[:, 128 * (hd // 2):128 * (hd // 2) + 128], 0.0),
                             axis=-1, keepdims=True) for hd in range(SWA_HEADS)]
            s_meta = jnp.where(meta_ok, _nt(q_all, kp[...]) * scale, NEG_INF)
            p_meta = jnp.exp(s_meta - _by_head(group, lse))
            ds_meta16 = (p_meta * (_nt(do16, vp[...]) - _by_head(group, delta)) * scale).astype(BF16)
            dq_meta = _nn(ds_meta16, kp[...])
            dkp[...] += _tn(ds_meta16, q_all)
            dvp[...] += _tn(p_meta.astype(BF16), do16)
            own2 = jnp.concatenate([own_side.astype(jnp.int32)] * 2, axis=0) > 0
            ok2 = jnp.concatenate([band_ok.astype(jnp.int32)] * 2, axis=0) > 0

            def window(x2):
                return jnp.where(own2, x2[:, BLK:], x2[:, :BLK])

            def unwindow(x):
                return jnp.concatenate([jnp.where(own2, 0.0, x), jnp.where(own2, x, 0.0)], axis=1).astype(BF16)
            lane8 = lax.broadcasted_iota(jnp.int32, (1, 128), 1)
            dsink = jnp.zeros((1, 128), F32)
            dq_pairs = [dq_meta[:, 128 * pr:128 * pr + 128] for pr in range(4)]
            dk2 = [[None, None], [None, None]]
            dv2 = [[None, None], [None, None]]
            for kv in range(2):
                for half in range(2):
                    heads, pairs = (4 * kv + half, 4 * kv + 2 + half), (2 * kv, 2 * kv + 1)
                    q_s = jnp.concatenate([q_all[:, 128 * pr:128 * pr + 128] for pr in pairs], axis=0)
                    do_s = jnp.concatenate([do16[:, 128 * pr:128 * pr + 128] for pr in pairs], axis=0)
                    lse_s = jnp.concatenate([lse[hd] for hd in heads], axis=0)
                    delta_s = jnp.concatenate([delta[hd] for hd in heads], axis=0)
                    s = jnp.where(ok2, window(_nt(q_s, kz[kv][half])) * scale, NEG_INF)
                    prob = jnp.exp(s - lse_s)
                    for hd in heads:
                        dsink = dsink + jnp.where(lane8 == hd, -jnp.sum(jnp.exp(sink_ref[0, hd] - lse[hd]) * delta[hd]), 0.0)
                    ds2 = unwindow(prob * (window(_nt(do_s, vz[kv][half])) - delta_s) * scale)
                    dq_s = _nn(ds2, kz[kv][half])
                    dq_pairs[pairs[0]] = dq_pairs[pairs[0]] + dq_s[:BLK]
                    dq_pairs[pairs[1]] = dq_pairs[pairs[1]] + dq_s[BLK:]
                    dk2[kv][half] = _tn(ds2, q_s)
                    dv2[kv][half] = _tn(unwindow(prob), do_s)
            dq_ref[rr, :] = jnp.concatenate(dq_pairs, axis=1)
            _acc_add(dsink_ref, first, dsink)
            for ref, acc2 in ((dk_ref, dk2), (dv_ref, dv2)):
                tot = jnp.zeros((2 * BLK, 128), F32)
                for kv in range(2):
                    for half in range(2):
                        part = jnp.where(_half_mask(128, half), acc2[kv][half], 0.0)
                        tot = tot + (part if half == kv else pltpu.roll(part, 64, 1))
                ref[pl.ds(prev, BLK), :] += tot[:BLK]
                ref[pl.ds(own, BLK), :] += tot[BLK:]
            return carry
        lax.fori_loop(0, per_step, one_block, 0)

        @pl.when(step == steps - 1)
        def _():
            dk_ref[PAD_ROWS:BLK, :] += _swa_meta_fold(dkp[...])
            dv_ref[PAD_ROWS:BLK, :] += _swa_meta_fold(dvp[...])

    full = pl.BlockSpec((rows, 128), lambda i: (0, 0))
    blocks = lambda cols: _row_spec(per_step * BLK, cols)
    return pl.pallas_call(
        body, name="swa_bwd", grid=(steps,),
        in_specs=[blocks(512), blocks(512), blocks(512), VMEM_SPEC, VMEM_SPEC, blocks(SWA_HEADS), SMEM_SPEC, VMEM_SPEC],
        out_specs=[blocks(512), full, full, _acc_spec(128), _acc_spec(512)],
        out_shape=[jax.ShapeDtypeStruct((rows, 512), F32), jax.ShapeDtypeStruct((rows, 128), F32),
                   jax.ShapeDtypeStruct((rows, 128), F32), jax.ShapeDtypeStruct((8, 128), F32),
                   jax.ShapeDtypeStruct((8, 512), F32)],
        scratch_shapes=[pltpu.VMEM((BLK, 512), BF16), pltpu.VMEM((BLK, 512), BF16),
                        pltpu.VMEM((BLK, 512), F32), pltpu.VMEM((BLK, 512), F32)],
        compiler_params=_params(("arbitrary",)),
    )(dcat, o_all, sq, sk, sv, lse, sinks, wn)


def _mix_out(h, cat_g, cat_s, wout, gpost):
    rows = h.shape[0]
    tm = _row_tile(rows)

    def body(h_ref, cg_ref, cs_ref, w_ref, g_ref, ho_ref, m_ref):
        m = _nn(cg_ref[...], w_ref[0:512, :]) + _nn(cs_ref[...], w_ref[512:1024, :])
        m_ref[...] = m
        mn, _ = _rms(m)
        ho_ref[...] = h_ref[...] + mn * g_ref[...]

    row_f32 = _row_spec(tm, D_MODEL)
    return pl.pallas_call(
        body, name="mix_out", grid=(rows // tm,),
        in_specs=[row_f32, _row_spec(tm, 512), _row_spec(tm, 512), VMEM_SPEC, VMEM_SPEC],
        out_specs=[row_f32, row_f32],
        out_shape=[jax.ShapeDtypeStruct((rows, D_MODEL), F32), jax.ShapeDtypeStruct((rows, D_MODEL), F32)],
        compiler_params=_params(("arbitrary",)),
    )(h, cat_g, cat_s, wout, gpost)


def _mix_out_bwd(dh, m, wout, gpost):
    rows = dh.shape[0]
    tm = _row_tile(rows)

    def body(dh_ref, m_ref, w_ref, g_ref, dcg_ref, dcs_ref, dm_ref, dg_ref):
        first = pl.program_id(0) == 0
        dhv = dh_ref[...]
        mn, rm = _rms(m_ref[...])
        _acc_add(dg_ref, first, _colsum(dhv * mn))
        dm16 = _rms_bwd(mn, rm, g_ref[...], dhv).astype(BF16)
        dm_ref[...] = dm16
        dcat = _nt(dm16, w_ref[...])
        dcg_ref[...] = dcat[:, 0:512]
        dcs_ref[...] = dcat[:, 512:1024]

    row_f32 = _row_spec(tm, D_MODEL)
    return pl.pallas_call(
        body, name="mix_out_bwd", grid=(rows // tm,),
        in_specs=[row_f32, row_f32, VMEM_SPEC, VMEM_SPEC],
        out_specs=[_row_spec(tm, 512), _row_spec(tm, 512), row_f32, _acc_spec(D_MODEL)],
        out_shape=[jax.ShapeDtypeStruct((rows, 512), F32), jax.ShapeDtypeStruct((rows, 512), F32),
                   jax.ShapeDtypeStruct((rows, D_MODEL), BF16), jax.ShapeDtypeStruct((8, D_MODEL), F32)],
        compiler_params=_params(("arbitrary",)),
    )(dh, m, wout, gpost)


def _mix_in_bwd(dh_out, h, g, win_p, wa2_p, cos, sin, loga, ga, dgq, dgk, dgv, dgg, dsq, dsk, dsv, dloga):
    rows = h.shape[0]
    tm = _row_tile(rows)

    def body(dho_ref, h_ref, g_ref, win_ref, wa2_ref, cos_ref, sin_ref, loga_ref, ga_ref,
             dgq_ref, dgk_ref, dgv_ref, dgg_ref, dsq_ref, dsk_ref, dsv_ref, dla_ref,
             dh_ref, dproj_ref, dwa2_ref, dg_ref, dba_ref):
        first = pl.program_id(0) == 0
        dz = dla_ref[...] * (1.0 / GLA_TAU) * (1.0 - jnp.exp(GLA_TAU * loga_ref[...]))
        _acc_add(dba_ref, first, _colsum(dz))
        dga = _nt(dz, wa2_ref[...])
        pa = _tn(ga_ref[...], dz)
        c1, s1 = cos_ref[...], sin_ref[...]
        c4 = jnp.concatenate([c1, c1, c1, c1], axis=1)
        s4 = jnp.concatenate([s1, s1, s1, s1], axis=1)
        dq_r, dk_r = dsq_ref[...], dsk_ref[...]
        dsq = dq_r * c4 - _rot_half(dq_r * s4)
        dsk = dk_r * c1 - _rot_half(dk_r * s1)
        dproj16 = jnp.concatenate(
            [dgq_ref[...], dgk_ref[...], dgv_ref[...], dgg_ref[...], dsq, dsk, dsv_ref[...], dga], axis=1).astype(BF16)
        dproj_ref[...] = dproj16
        dn = _nn(dproj16, win_ref[...])

        @pl.when(first)
        def _():
            dwa2_ref[...] = pa

        @pl.when(jnp.logical_not(first))
        def _():
            dwa2_ref[...] += pa
        hn, rh = _rms(h_ref[...])
        _acc_add(dg_ref, first, _colsum(dn * hn))
        dh_ref[...] = dho_ref[...] + _rms_bwd(hn, rh, g_ref[...], dn)

    rs = lambda c: _row_spec(tm, c)
    return pl.pallas_call(
        body, name="mix_in_bwd", grid=(rows // tm,),
        in_specs=[rs(D_MODEL), rs(D_MODEL), VMEM_SPEC, VMEM_SPEC, VMEM_SPEC, rs(128), rs(128), rs(256), rs(128),
                  rs(256), rs(256), rs(512), rs(512), rs(512), rs(128), rs(128), rs(256)],
        out_specs=[rs(D_MODEL), rs(P_END), pl.BlockSpec((128, 256), lambda i: (0, 0)), _acc_spec(D_MODEL), _acc_spec(256)],
        out_shape=[jax.ShapeDtypeStruct((rows, D_MODEL), F32), jax.ShapeDtypeStruct((rows, P_END), BF16),
                   jax.ShapeDtypeStruct((128, 256), F32), jax.ShapeDtypeStruct((8, D_MODEL), F32),
                   jax.ShapeDtypeStruct((8, 256), F32)],
        compiler_params=_params(("arbitrary",)),
    )(dh_out, h, g, win_p, wa2_p, cos, sin, loga, ga, dgq, dgk, dgv, dgg, dsq, dsk, dsv, dloga)


def _rope_tables(rows):
    pos = (jnp.arange(rows, dtype=jnp.int32) - PAD_ROWS).astype(F32)
    inv_freq = 1.0 / (ROPE_THETA ** (jnp.arange(0, SWA_HD, 2, dtype=F32) / SWA_HD))
    ang = pos[:, None] * inv_freq[None, :]
    return jnp.tile(jnp.cos(ang), (1, 4)), jnp.tile(jnp.sin(ang), (1, 4))


def _local_step(x, tgt, front, w, late_weights=None, on_grads=None, on_small=None):
    cos, sin = _rope_tables(x.shape[0] + BLK)
    g = {}

    def tell(group, names):
        for nm in names:
            g[nm] = grads_now[nm]
        return None if on_grads is None else on_grads(group, {nm: grads_now[nm] for nm in names})

    h0, h1, a1, b1, s1, f1 = _ffn_fwd(x, w["ffn1_pre"], w["wg1"], w["wu1"], w["wd1"], w["ffn1_post"], front=front)
    if late_weights is not None:
        w = {**w, **late_weights("win", f1)}
    gq, gk, gv, gg, sq, sk, sv, ga, loga, bc, n2 = _mix_in(h1, w["mix_pre"], w["win"], w["wa2"], w["b_a"], cos, sin)
    o_g, cat_g, sp = _gla_fwd(gq, gk, gv, gg, bc, w["gla_norm"])
    o_s, cat_s, lse = _swa_fwd(sq, sk, sv, w["sinks"], w["swa_norm"])
    if late_weights is not None:
        w = {**w, **late_weights("rest", lse)}
    h2, m = _mix_out(h1, cat_g, cat_s, w["wout"], w["mix_post"])
    h3, a2, b2, s2, f2, dy, loss = _ffn_fwd(h2, w["ffn2_pre"], w["wg2"], w["wu2"], w["wd2"], w["ffn2_post"], tgt)
    del h3
    dh2, da, db, df, n3, g["ffn2_pre"], g["ffn2_post"] = _ffn_bwd_act(
        dy, h2, a2, b2, f2, w["ffn2_pre"], w["ffn2_post"], w["wg2"], w["wu2"], w["wd2"], "ffn2_bwd_act")
    grads_now = dict(wd2=_wgrad(s2, df, "ffn2_wgrad_down"), wg2=_wgrad(da, n3, "ffn2_wgrad_gate"),
                     wu2=_wgrad(db, n3, "ffn2_wgrad_up"))
    tok = tell("ffn2", ("wd2", "wg2", "wu2"))
    dcg, dcs, dm, g["mix_post"] = _mix_out_bwd(dh2, m, w["wout"], w["mix_post"] + (0.0 if tok is None else tok[0, 0]))
    dsq, dsk, dsv, g["sinks"], g["swa_norm"] = _swa_bwd(dcs, o_s, sq, sk, sv, lse, w["sinks"], w["swa_norm"])
    dgq, dgk, dgv, dgg, dloga, g["gla_norm"] = _gla_bwd(dcg, o_g, gq, gk, gv, gg, bc, sp, w["gla_norm"])
    dh1, dproj, g["wa2"], g["mix_pre"], g["b_a"] = _mix_in_bwd(
        dh2, h1, w["mix_pre"], w["win"], w["wa2"], cos, sin, loga, ga, dgq, dgk, dgv, dgg, dsq, dsk, dsv, dloga)
    dh0, da, db, df, n1, g["ffn1_pre"], g["ffn1_post"] = _ffn_bwd_act(
        dh1, h0, a1, b1, f1, w["ffn1_pre"], w["ffn1_post"], w["wg1"], w["wu1"], w["wd1"], "ffn1_bwd_act")
    tok = None if on_small is None else on_small(loss[0, 0], dh0, g)
    grads_now = dict(wd1=_wgrad(s1, df, "ffn1_wgrad_down", after=tok))
    tok = tell("ffn1_down", ("wd1",))
    grads_now = dict(wg1=_wgrad(da, n1, "ffn1_wgrad_gate", after=tok))
    tok = tell("ffn1_gate", ("wg1",))
    grads_now = dict(wu1=_wgrad(db, n1, "ffn1_wgrad_up", after=tok))
    tok = tell("ffn1_up", ("wu1",))
    grads_now = dict(win=_wgrad(dproj, n2, "win_wgrad", after=tok),
                     wout=jnp.concatenate([_wgrad(cat_g, dm, "wout_wgrad_gla", after=tok),
                                           _wgrad(cat_s, dm, "wout_wgrad_swa", after=tok)], axis=0))
    tell("mix", ("wout", "win"))
    return loss[0, 0], dh0, g


def _win_pad_rows(win_t):
    pad = jnp.zeros((P_END - P_GA - 16, win_t.shape[1]), win_t.dtype)
    return jnp.concatenate([win_t[0:1536], win_t[1552:2320], win_t[1536:1552], pad], axis=0)


def _win_unpad_rows(win_p):
    return jnp.concatenate([win_p[0:1536], win_p[P_GA:P_GA + 16], win_p[1536:P_GA]], axis=0)


def _place_on_mesh():
    return lax.axis_index("x"), lax.axis_index("y"), lax.axis_index("c")


def _dev_index(px, py, pc):
    return 4 * px + 2 * py + pc


def _other_devices(x, y, c):
    flip = lambda v, f: 1 - v if f else v
    return [(flip(x, fx), flip(y, fy), flip(c, fc)) for fx in (0, 1) for fy in (0, 1) for fc in (0, 1)][1:]


def _all_gather(shards):
    n = len(shards)

    def body(*refs):
        ins, outs = refs[:n], refs[n:2 * n]
        zeros_ref, send_sems, recv_sems, local_sems = refs[2 * n:]
        zeros_ref[...] = jnp.zeros_like(zeros_ref)
        x, y, c = _place_on_mesh()
        me, sibling = (x, y, c), (x, y, 1 - c)
        chips = [(1 - x, y), (x, 1 - y), (1 - x, 1 - y)]

        def rows(k, px, py, pc):
            r = ins[k].shape[0]
            return outs[k].at[pl.ds(pl.multiple_of(_dev_index(px, py, pc) * r, 8), r), :]

        def copy(k, slot, block, to, src=None):
            return pltpu.make_async_remote_copy(
                src_ref=rows(k, *block) if src is None else src, dst_ref=rows(k, *block),
                send_sem=send_sems.at[k, slot], recv_sem=recv_sems.at[k, slot], device_id=to, device_id_type=MESH)

        local = [pltpu.make_async_copy(ins[k], rows(k, *me), local_sems.at[k]) for k in range(n)]
        sends = []
        for k in range(n):
            local[k].start()
            sends.append(copy(k, 0, me, sibling, src=ins[k]))
            sends += [copy(k, 1 + j, me, (*chip, c), src=ins[k]) for j, chip in enumerate(chips)]
        for cp in sends:
            cp.start()
        for k in range(n):
            for j, chip in enumerate(chips):
                copy(k, 1 + j, (*chip, c), me).wait_recv()
                passed = copy(k, 4 + j, (*chip, c), sibling)
                passed.start()
                sends.append(passed)
        for k in range(n):
            copy(k, 0, sibling, me).wait_recv()
            for j, chip in enumerate(chips):
                copy(k, 4 + j, (*chip, 1 - c), me).wait_recv()
        for cp in sends:
            cp.wait_send()
        for cp in local:
            cp.wait()

    return pl.pallas_call(
        body, name="all_gather_weights",
        in_specs=[ANY_SPEC] * n, out_specs=[ANY_SPEC] * n + [VMEM_SPEC],
        out_shape=[jax.ShapeDtypeStruct((N_DEV * s.shape[0], s.shape[1]), s.dtype) for s in shards]
        + [jax.ShapeDtypeStruct((8, 128), F32)],
        scratch_shapes=[pltpu.SemaphoreType.DMA((n, 7)), pltpu.SemaphoreType.DMA((n, 7)), pltpu.SemaphoreType.DMA((n,))],
    )(*shards)


HBM_SPEC = pl.BlockSpec(memory_space=pltpu.HBM)
SEM_SPEC = pl.BlockSpec(memory_space=pltpu.SEMAPHORE)
DATAFLOW = pltpu.SideEffectType.DATAFLOW_SIDE_EFFECTING


GATHER, SCATTER, SCATTER_CHIPS = "gather", "scatter", "scatter among chips"


def _exchange_peers(kind):
    x, y, c = _place_on_mesh()
    if kind == SCATTER_CHIPS:
        peers = [(1 - x, y, c), (x, 1 - y, c), (1 - x, 1 - y, c)]
        return peers, [2 * p[0] + p[1] for p in peers], 2 * x + y, 4
    peers = _other_devices(x, y, c)
    return peers, [_dev_index(*p) for p in peers], _dev_index(x, y, c), N_DEV


def _exchange_copies(srcs, lands, send_sems, recv_sems, own_sems, kind, arriving):
    peers, theirs, me, blocks = _exchange_peers(kind)
    remote, local = [], []
    for k, (src, land) in enumerate(zip(srcs, lands)):
        r = land.shape[0] // blocks

        def block(ref, d):
            return ref.at[pl.ds(pl.multiple_of(d * r, 8), r), :]

        for f, (peer, him) in enumerate(zip(peers, theirs)):
            mine, his = (him, me) if arriving else (me, him)
            sem = len(peers) * k + f
            remote.append(pltpu.make_async_remote_copy(
                src_ref=src if kind == GATHER else block(src, his), dst_ref=block(land, mine),
                send_sem=send_sems.at[sem], recv_sem=recv_sems.at[sem], device_id=peer, device_id_type=MESH))
        local.append(pltpu.make_async_copy(src if kind == GATHER else block(src, me), block(land, me), own_sems.at[k]))
    return remote, local


def _exchange_start(srcs, kind, name):
    n = len(srcs)
    lands = [lax.empty((N_DEV * s.shape[0], s.shape[1]) if kind == GATHER else s.shape, s.dtype) for s in srcs]
    sems = (3 if kind == SCATTER_CHIPS else 7) * n

    def body(*refs):
        remote, local = _exchange_copies(refs[:n], refs[n:2 * n], *refs[2 * n:2 * n + 3], kind, False)
        for cp in remote + local:
            cp.start()
        refs[-1][...] = jnp.zeros_like(refs[-1])

    both = list(srcs) + list(lands)
    outs = pl.pallas_call(
        body, name=name,
        out_shape=(pltpu.SemaphoreType.DMA((sems,)), pltpu.SemaphoreType.DMA((sems,)), pltpu.SemaphoreType.DMA((n,)),
                   *[pltpu.HBM(a.shape, a.dtype) for a in both], jax.ShapeDtypeStruct((8, 128), F32)),
        in_specs=[HBM_SPEC] * (2 * n), out_specs=(SEM_SPEC, SEM_SPEC, SEM_SPEC, *[HBM_SPEC] * (2 * n), VMEM_SPEC),
        input_output_aliases={i: 3 + i for i in range(2 * n)},
        compiler_params=pltpu.CompilerParams(has_side_effects=DATAFLOW),
    )(*[pltpu.with_memory_space_constraint(a, pltpu.HBM) for a in both])
    return outs[0:3], outs[3:3 + n], outs[3 + n:3 + 2 * n], outs[-1]


def _exchange_wait(started, kind, after, name):
    sems, srcs, lands, _ = started
    n = len(srcs)

    def body(*refs):
        args = (refs[:n], refs[n:2 * n], *refs[2 * n:2 * n + 3], kind)
        going, local = _exchange_copies(*args, False)
        for cp in going:
            cp.wait_send()
        for cp in local:
            cp.wait()
        for cp in _exchange_copies(*args, True)[0]:
            cp.wait_recv()

    both = list(srcs) + list(lands)
    outs = pl.pallas_call(
        body, name=name, out_shape=[pltpu.HBM(a.shape, a.dtype) for a in both],
        in_specs=[HBM_SPEC] * (2 * n) + [SEM_SPEC, SEM_SPEC, SEM_SPEC, ANY_SPEC], out_specs=[HBM_SPEC] * (2 * n),
        input_output_aliases={i: i for i in range(2 * n)},
        compiler_params=pltpu.CompilerParams(has_side_effects=DATAFLOW),
    )(*both, *sems, after)
    return outs[n:]


def _sibling_reduce(part, name):
    r, cols = part.shape[0] // N_DEV, part.shape[1]

    def body(p_ref, o_ref, mine, got, send_sems, recv_sems, own_sems):
        x, y, c = _place_on_mesh()

        def block(d):
            return p_ref.at[pl.ds(pl.multiple_of(d * r, 8), r), :]
        swaps = [pltpu.make_async_remote_copy(
            src_ref=block(2 * j + 1 - c), dst_ref=got.at[j], send_sem=send_sems.at[j], recv_sem=recv_sems.at[j],
            device_id=(x, y, 1 - c), device_id_type=MESH) for j in range(4)]
        keeps = [pltpu.make_async_copy(block(2 * j + c), mine.at[j], own_sems.at[j]) for j in range(4)]
        for cp in swaps + keeps:
            cp.start()
        for j in range(4):
            keeps[j].wait()
            swaps[j].wait()
            o_ref[pl.ds(j * r, r), :] = (mine[j].astype(F32) + got[j].astype(F32)).astype(o_ref.dtype)

    return pl.pallas_call(
        body, name=name, in_specs=[ANY_SPEC], out_specs=VMEM_SPEC,
        out_shape=jax.ShapeDtypeStruct((4 * r, cols), part.dtype),
        scratch_shapes=[pltpu.VMEM((4, r, cols), part.dtype), pltpu.VMEM((4, r, cols), part.dtype),
                        pltpu.SemaphoreType.DMA((4,)), pltpu.SemaphoreType.DMA((4,)), pltpu.SemaphoreType.DMA((4,))],
        compiler_params=pltpu.CompilerParams(vmem_limit_bytes=32 << 20),
    )(part)


def _sum_partials(parts, name, blocks=N_DEV):
    n = len(parts)

    def body(*refs):
        ins, outs = refs[:n], refs[n:]
        first = pl.program_id(0) == 0
        for i_ref, o_ref in zip(ins, outs):
            v = i_ref[...].astype(F32)

            @pl.when(first)
            def _():
                o_ref[...] = v

            @pl.when(jnp.logical_not(first))
            def _():
                o_ref[...] += v

    shapes = [(p.shape[0] // blocks, p.shape[1]) for p in parts]
    return pl.pallas_call(
        body, name=name, grid=(blocks,),
        in_specs=[pl.BlockSpec(s, lambda j: (j, 0)) for s in shapes],
        out_specs=[pl.BlockSpec(s, lambda j: (0, 0)) for s in shapes],
        out_shape=[jax.ShapeDtypeStruct(s, F32) for s in shapes],
        compiler_params=_params(("arbitrary",)),
    )(*parts)


def _adamw_update(w, g, m, v):
    m = ADAM_B1 * m + (1.0 - ADAM_B1) * g
    v = ADAM_B2 * v + (1.0 - ADAM_B2) * (g * g)
    m_hat = m * (1.0 / (1.0 - ADAM_B1 ** ADAM_STEP))
    v_hat = v * (1.0 / (1.0 - ADAM_B2 ** ADAM_STEP))
    return -ADAM_LR * (m_hat / (jnp.sqrt(v_hat) + ADAM_EPS) + ADAM_WD * w), m, v


def _sum_adamw(parts, w, m, v, blocks, name):
    shape = w.shape

    def body(p_ref, w_ref, m_ref, v_ref, g_ref, d_ref, mo_ref, vo_ref):
        j = pl.program_id(0)
        part = p_ref[...].astype(F32)

        @pl.when(j == 0)
        def _():
            g_ref[...] = part

        @pl.when(j > 0)
        def _():
            g_ref[...] += part

        @pl.when(j == blocks - 1)
        def _():
            d_ref[...], mo_ref[...], vo_ref[...] = _adamw_update(w_ref[...], g_ref[...], m_ref[...], v_ref[...])

    held = pl.BlockSpec(shape, lambda j: (0, 0))
    return pl.pallas_call(
        body, name=name, grid=(blocks,),
        in_specs=[pl.BlockSpec(shape, lambda j: (j, 0)), held, held, held],
        out_specs=[held] * 4, out_shape=[jax.ShapeDtypeStruct(shape, F32)] * 4,
        compiler_params=_params(("arbitrary",)),
    )(parts, w, m, v)


def _adamw(ws, gs, ms, vs, name):
    n = len(ws)

    def body(*refs):
        w_r, g_r, m_r, v_r = refs[:n], refs[n:2 * n], refs[2 * n:3 * n], refs[3 * n:4 * n]
        d_o, m_o, v_o = refs[4 * n:5 * n], refs[5 * n:6 * n], refs[6 * n:7 * n]
        for k in range(n):
            d_o[k][...], m_o[k][...], v_o[k][...] = _adamw_update(w_r[k][...], g_r[k][...], m_r[k][...], v_r[k][...])

    shapes = [jax.ShapeDtypeStruct(w.shape, F32) for w in ws]
    outs = pl.pallas_call(
        body, name=name, in_specs=[VMEM_SPEC] * (4 * n), out_specs=[VMEM_SPEC] * (3 * n), out_shape=shapes * 3,
        compiler_params=pltpu.CompilerParams(vmem_limit_bytes=56 << 20),
    )(*ws, *gs, *ms, *vs)
    return outs[:n], outs[n:2 * n], outs[2 * n:]


WEIGHT_NAMES = ("meta_tokens", "ffn1_pre_norm", "ffn1_w_gate", "ffn1_w_up", "ffn1_w_down", "ffn1_post_norm", "mix_pre_norm",
                "w_in", "gla_w_a2", "gla_b_a", "gla_out_norm", "swa_sinks", "swa_out_norm", "w_out", "mix_post_norm",
                "ffn2_pre_norm", "ffn2_w_gate", "ffn2_w_up", "ffn2_w_down", "ffn2_post_norm")
WIN_SHARD = D_IN // N_DEV
WIN_SHARD_PAD = 304
SLAB_VECTORS = ("ffn1_pre", "ffn1_post", "mix_pre", "mix_post", "ffn2_pre", "ffn2_post")
SLAB_ROWS = 32


def kernel(x, meta_tokens, ffn1_pre_norm, ffn1_w_gate, ffn1_w_up, ffn1_w_down, ffn1_post_norm, mix_pre_norm, w_in, gla_w_a2, gla_b_a, gla_out_norm, swa_sinks, swa_out_norm, w_out, mix_post_norm, ffn2_pre_norm, ffn2_w_gate, ffn2_w_up, ffn2_w_down, ffn2_post_norm, loss_target, m_meta_tokens, m_ffn1_pre_norm, m_ffn1_w_gate, m_ffn1_w_up, m_ffn1_w_down, m_ffn1_post_norm, m_mix_pre_norm, m_w_in, m_gla_w_a2, m_gla_b_a, m_gla_out_norm, m_swa_sinks, m_swa_out_norm, m_w_out, m_mix_post_norm, m_ffn2_pre_norm, m_ffn2_w_gate, m_ffn2_w_up, m_ffn2_w_down, m_ffn2_post_norm, v_meta_tokens, v_ffn1_pre_norm, v_ffn1_w_gate, v_ffn1_w_up, v_ffn1_w_down, v_ffn1_post_norm, v_mix_pre_norm, v_w_in, v_gla_w_a2, v_gla_b_a, v_gla_out_norm, v_swa_sinks, v_swa_out_norm, v_w_out, v_mix_post_norm, v_ffn2_pre_norm, v_ffn2_w_gate, v_ffn2_w_up, v_ffn2_w_down, v_ffn2_post_norm):
    given = dict(locals())
    W = {n: given[n] for n in WEIGHT_NAMES}
    M = {n: given["m_" + n] for n in WEIGHT_NAMES}
    V = {n: given["v_" + n] for n in WEIGHT_NAMES}
    dev = _dev_index(*_place_on_mesh())

    def t16(w):
        return w[0].T.astype(BF16)

    small = jnp.concatenate([W["meta_tokens"], jnp.pad(W["gla_w_a2"][0], ((0, 0), (0, 96)))], axis=0)
    wg1, wu1, wd1, small_g, gathered_zeros = _all_gather(
        [t16(W["ffn1_w_gate"]), t16(W["ffn1_w_up"]), W["ffn1_w_down"][0].astype(BF16), small])
    def after_zero(shard, zeros):
        return shard + zeros[0:1, 0:1].astype(shard.dtype)
    win_shard = jnp.pad(t16(W["w_in"]), ((0, WIN_SHARD_PAD - WIN_SHARD), (0, 0)))
    win_shard = after_zero(win_shard, gathered_zeros)
    mid = _exchange_start([win_shard], GATHER, "gather_w_in_start")
    late_shards = [after_zero(W["w_out"][0].astype(BF16), mid[3]), t16(W["ffn2_w_gate"]), t16(W["ffn2_w_up"]),
                   W["ffn2_w_down"][0].astype(BF16)]
    late = _exchange_start(late_shards, GATHER, "gather_late_weights_start")

    def late_weights(what, after):
        if what == "win":
            win_g, = _exchange_wait(mid, GATHER, after, "gather_w_in_wait")
            win_t = win_g.reshape(N_DEV, WIN_SHARD_PAD, D_MODEL)[:, :WIN_SHARD].reshape(D_IN, D_MODEL)
            return dict(win=_win_pad_rows(win_t))
        wout, wg2, wu2, wd2 = _exchange_wait(late, GATHER, after, "gather_late_weights_wait")
        return dict(wout=wout, wg2=wg2, wu2=wu2, wd2=wd2)

    small_g = small_g.reshape(N_DEV, 32, 128)
    meta_full = small_g[:, :N_META].transpose(1, 0, 2).reshape(N_META, D_MODEL)
    wa2_full = small_g[:, N_META:, :32].transpose(1, 0, 2).reshape(16, 256)
    w = dict(
        ffn1_pre=W["ffn1_pre_norm"] + late[3][0, 0], ffn1_post=W["ffn1_post_norm"], mix_pre=W["mix_pre_norm"],
        mix_post=W["mix_post_norm"], ffn2_pre=W["ffn2_pre_norm"], ffn2_post=W["ffn2_post_norm"], b_a=W["gla_b_a"],
        gla_norm=W["gla_out_norm"], sinks=W["swa_sinks"], swa_norm=W["swa_out_norm"], wg1=wg1, wu1=wu1, wd1=wd1,
        wa2=jnp.pad(wa2_full, ((0, 112), (0, 0))))

    in_flight = []

    def on_grads(group, grads):
        parts = []
        for nm, p in grads.items():
            if nm == "win":
                p = _win_unpad_rows(p).reshape(N_DEV, WIN_SHARD, D_MODEL)
                p = jnp.pad(p, ((0, 0), (0, WIN_SHARD_PAD - WIN_SHARD), (0, 0))).reshape(N_DEV * WIN_SHARD_PAD, D_MODEL)
            parts.append(p)
        kind = SCATTER if group == "ffn2" else SCATTER_CHIPS
        if kind == SCATTER_CHIPS:
            parts = [_sibling_reduce(p, "pair_" + group + "_" + nm) for nm, p in zip(grads, parts)]
        started = _exchange_start(parts, kind, "scatter_" + group + "_start")
        in_flight.append((group, list(grads), started, kind))
        return started[3]

    small_flight = []

    def on_small(loss, dh0, g):
        packed = jnp.concatenate([g["b_a"][0:1], g["gla_norm"][0:1], g["sinks"][0:1], g["swa_norm"][0:1]], axis=1)
        slab = jnp.concatenate([g[k][0:1] for k in SLAB_VECTORS] + [packed, jnp.full((1, D_MODEL), loss, F32),
                               g["wa2"][:16].reshape(4, D_MODEL), jnp.zeros((4, D_MODEL), F32), dh0[PAD_ROWS:BLK]], axis=0)
        small_flight.append(_exchange_start([slab], GATHER, "gather_small_grads_start"))
        return small_flight[0][3]

    front = jnp.concatenate([jnp.zeros((PAD_ROWS, D_MODEL), F32), meta_full], axis=0)
    loss, dh0, g = _local_step(x[0], loss_target[0], front, w, late_weights, on_grads, on_small)
    grad_x = dh0[BLK:][None]

    land, = _exchange_wait(small_flight[0], GATHER, in_flight[-1][2][3], "gather_small_grads_wait")
    tot = _sum_partials([land], "sum_small_grads")[0]
    loss = tot[7, 0]
    small_grads = dict(
        ffn1_pre_norm=tot[0:1], ffn1_post_norm=tot[1:2], mix_pre_norm=tot[2:3], mix_post_norm=tot[3:4],
        ffn2_pre_norm=tot[4:5], ffn2_post_norm=tot[5:6], gla_b_a=tot[6:7, 0:256], gla_out_norm=tot[6:7, 256:384],
        swa_sinks=tot[6:7, 384:392], swa_out_norm=tot[6:7, 512:1024],
        gla_w_a2=lax.dynamic_slice_in_dim(tot[8:12].reshape(16, 256), dev * 32, 32, axis=1)[None],
        meta_tokens=lax.dynamic_slice_in_dim(tot[16:32], dev * 128, 128, axis=1))

    big = dict(wg1=("ffn1_w_gate", True), wu1=("ffn1_w_up", True), wd1=("ffn1_w_down", False), win=("w_in", True),
               wout=("w_out", False), wg2=("ffn2_w_gate", True), wu2=("ffn2_w_up", True), wd2=("ffn2_w_down", False))
    grads = dict(small_grads)
    delta, new_m, new_v = {}, {}, {}
    names = [n for n in WEIGHT_NAMES if n not in [full for full, _ in big.values()]]
    two_d = lambda a: a.reshape(-1, a.shape[-1])
    d_, m_, v_ = _adamw([two_d(W[n]) for n in names], [two_d(grads[n]) for n in names],
                        [two_d(M[n]) for n in names], [two_d(V[n]) for n in names], "adamw_small")
    for k, n in enumerate(names):
        delta[n], new_m[n], new_v[n] = d_[k].reshape(W[n].shape), m_[k].reshape(W[n].shape), v_[k].reshape(W[n].shape)

    before_wait = d_[0] + in_flight[-1][2][3][0, 0]
    for group, shorts, started, kind in in_flight:
        lands = _exchange_wait(started, kind, before_wait, "scatter_" + group + "_wait")
        blocks = 4 if kind == SCATTER_CHIPS else N_DEV
        for short, land in zip(shorts, lands):
            n, transposed = big[short]
            to_slab = (lambda a: a[0].T) if transposed else (lambda a: a[0])
            from_slab = (lambda a: a.T[None]) if transposed else (lambda a: a[None])
            if short == "win":
                g_slab = _sum_partials([land], "sum_" + n, blocks)[0][:WIN_SHARD]
                d_, m_, v_ = _adamw([to_slab(W[n])], [g_slab], [to_slab(M[n])], [to_slab(V[n])], "adamw_" + n)
                d_, m_, v_ = d_[0], m_[0], v_[0]
            else:
                g_slab, d_, m_, v_ = _sum_adamw(land, to_slab(W[n]), to_slab(M[n]), to_slab(V[n]), blocks, "adamw_" + n)
            grads[n], delta[n], new_m[n], new_v[n] = from_slab(g_slab), from_slab(d_), from_slab(m_), from_slab(v_)
            before_wait = d_
    return (loss, grad_x, *[grads[n] for n in WEIGHT_NAMES], *[delta[n] for n in WEIGHT_NAMES],
            *[new_m[n] for n in WEIGHT_NAMES], *[new_v[n] for n in WEIGHT_NAMES])
```

```python
import math

import jax
import jax.numpy as jnp
from jax import lax
from jax.experimental import pallas as pl
from jax.experimental.pallas import tpu as pltpu

F32, BF16 = jnp.float32, jnp.bfloat16

D_MODEL = 1024
D_FF = 2816
N_META = 16
BLK = 128
PAD_ROWS = BLK - N_META
GLA_DK = 64
SWA_HD = 64
SWA_HEADS = 8
GLA_TAU = 16.0
NORM_EPS = 1e-6
NEG_INF = -1e30
ROPE_THETA = 10000.0
P_GQ, P_GK, P_GV, P_GG, P_SQ, P_SK, P_SV, P_GA, P_END = 0, 256, 512, 1024, 1536, 2048, 2176, 2304, 2432
D_IN = 2320
IN_SPLITS = (256, 256, 512, 512, 16, 512, 128, 128)
FF_TILE = 2816
WGRAD_TILE_MAX = 2432
N_DEV = 8
MESH = pl.DeviceIdType.MESH

ADAM_LR, ADAM_B1, ADAM_B2, ADAM_EPS, ADAM_WD, ADAM_STEP = 0.001, 0.9, 0.999, 1e-08, 0.01, 10

V7X_VMEM_BYTES = 64 << 20
VMEM_SPEC = pl.BlockSpec(memory_space=pltpu.VMEM)
SMEM_SPEC = pl.BlockSpec(memory_space=pltpu.SMEM)
ANY_SPEC = pl.BlockSpec(memory_space=pl.ANY)


def _params(semantics, vmem_mb=56):
    return pltpu.CompilerParams(dimension_semantics=semantics, vmem_limit_bytes=vmem_mb << 20)


def _row_tile(rows):
    return 416 if rows % 416 == 0 else BLK


def _blocks_per_step(blocks):
    return 5 if blocks % 5 == 0 else 1


def _nn(a, b):
    return lax.dot_general(a, b, (((1,), (0,)), ((), ())), preferred_element_type=F32)


def _nt(a, b):
    return lax.dot_general(a, b, (((1,), (1,)), ((), ())), preferred_element_type=F32)


def _tn(a, b):
    return lax.dot_general(a, b, (((0,), (0,)), ((), ())), preferred_element_type=F32)


def _rms(x):
    r = lax.rsqrt(jnp.mean(x * x, axis=-1, keepdims=True) + NORM_EPS)
    return x * r, r


def _rms_bwd(xn, r, w, dy):
    g = dy * w
    return r * (g - xn * jnp.mean(g * xn, axis=-1, keepdims=True))


def _sigmoid(x):
    return 1.0 / (1.0 + jnp.exp(-x))


def _colsum(x):
    return jnp.sum(x, axis=0, keepdims=True)


def _split_bf16(x):
    hi = x.astype(BF16)
    lo = (x - hi.astype(F32)).astype(BF16)
    return hi, lo


def _tri(lower):
    r = lax.broadcasted_iota(jnp.int32, (BLK, BLK), 0)
    c = lax.broadcasted_iota(jnp.int32, (BLK, BLK), 1)
    return (r >= c) if lower else (c >= r)


def _half_mask(width, half):
    lane = lax.broadcasted_iota(jnp.int32, (1, width), 1)
    return ((lane % 128) < 64) if half == 0 else ((lane % 128) >= 64)


def _rot_half(x):
    w = x.shape[-1]
    lane = lax.broadcasted_iota(jnp.int32, (1, w), 1)
    return jnp.where((lane % SWA_HD) < SWA_HD // 2, -pltpu.roll(x, w - SWA_HD // 2, 1), pltpu.roll(x, SWA_HD // 2, 1))


def _row_spec(tm, cols):
    return pl.BlockSpec((tm, cols), lambda i: (i, 0))


def _acc_spec(cols):
    return pl.BlockSpec((8, cols), lambda i: (0, 0))


def _acc_add(ref, first, value):
    @pl.when(first)
    def _():
        ref[...] = jnp.zeros_like(ref)
    ref[0:1, :] += value


def _behind_front(ref, i, tm, front):
    blk = ref[...]
    return jnp.where(i == 0, jnp.concatenate([front, blk[0:tm - BLK]], axis=0), blk)


def _ffn_fwd(h, gpre, wg_t, wu_t, wd, gpost, tgt=None, front=None):
    with_loss, with_front = tgt is not None, front is not None
    rows = h.shape[0] + (BLK if with_front else 0)
    tm = _row_tile(rows)
    nf = D_FF // FF_TILE

    def body(*refs):
        refs = list(refs)
        h_ref, gpre_ref, wg_ref, wu_ref, wd_ref, gpost_ref = refs[:6]
        del refs[:6]
        front_ref = refs.pop(0) if with_front else None
        t_ref = refs.pop(0) if with_loss else None
        h0_ref = refs.pop(0) if with_front else None
        ho_ref, a_ref, b_ref, s_ref, f_ref = refs[:5]
        dy_ref, loss_ref = refs[5:7] if with_loss else (None, None)
        acc = refs[-1]
        i = pl.program_id(0)
        if with_front:
            h_in = _behind_front(h_ref, i, tm, front_ref[...])
            h0_ref[...] = h_in
        else:
            h_in = h_ref[...]
        hn, _ = _rms(h_in)
        n16 = (hn * gpre_ref[...]).astype(BF16)
        for j in range(nf):
            cols = slice(j * FF_TILE, (j + 1) * FF_TILE)
            a = _nt(n16, wg_ref[cols, :])
            b = _nt(n16, wu_ref[cols, :])
            a_ref[:, cols] = a.astype(BF16)
            b_ref[:, cols] = b.astype(BF16)
            s16 = (a * _sigmoid(a) * b).astype(BF16)
            s_ref[:, cols] = s16
            part = _nn(s16, wd_ref[cols, :])
            if j == 0:
                acc[...] = part
            else:
                acc[...] += part
        f = acc[...]
        f_ref[...] = f
        fn, _ = _rms(f)
        y = h_in + 0.5 * (fn * gpost_ref[...])
        ho_ref[...] = y
        if with_loss:
            row = i * tm + lax.broadcasted_iota(jnp.int32, (tm, 1), 0)
            err = jnp.where(row >= BLK, y - _behind_front(t_ref, i, tm, jnp.zeros((BLK, D_MODEL), F32)), 0.0)
            dy_ref[...] = err * (1.0 / D_MODEL)
            part = 0.5 * jnp.sum(jnp.sum(err * err, axis=-1, keepdims=True) * (1.0 / D_MODEL), axis=0, keepdims=True)

            @pl.when(i == 0)
            def _():
                loss_ref[...] = jnp.zeros_like(loss_ref)
            loss_ref[...] += part

    row_f32 = _row_spec(tm, D_MODEL)
    behind = pl.BlockSpec((pl.Element(tm), pl.Element(D_MODEL)),
                          lambda i: (pl.multiple_of(jnp.maximum(i * tm - BLK, 0), math.gcd(tm, BLK)), 0))
    in_specs = [behind if with_front else row_f32, VMEM_SPEC, VMEM_SPEC, VMEM_SPEC, VMEM_SPEC, VMEM_SPEC]
    out_specs = [row_f32, _row_spec(tm, D_FF), _row_spec(tm, D_FF), _row_spec(tm, D_FF), row_f32]
    out_shape = [jax.ShapeDtypeStruct((rows, D_MODEL), F32), jax.ShapeDtypeStruct((rows, D_FF), BF16),
                 jax.ShapeDtypeStruct((rows, D_FF), BF16), jax.ShapeDtypeStruct((rows, D_FF), BF16),
                 jax.ShapeDtypeStruct((rows, D_MODEL), F32)]
    args = [h, gpre, wg_t, wu_t, wd, gpost]
    if with_front:
        in_specs.append(VMEM_SPEC)
        args.append(front)
        out_specs.insert(0, row_f32)
        out_shape.insert(0, jax.ShapeDtypeStruct((rows, D_MODEL), F32))
    if with_loss:
        in_specs.append(behind)
        args.append(tgt)
        out_specs += [row_f32, pl.BlockSpec((8, 128), lambda i: (0, 0))]
        out_shape += [jax.ShapeDtypeStruct((rows, D_MODEL), F32), jax.ShapeDtypeStruct((8, 128), F32)]
    return pl.pallas_call(
        body, name="ffn_fwd_loss" if with_loss else "ffn_fwd", grid=(rows // tm,),
        in_specs=in_specs, out_specs=out_specs, out_shape=out_shape,
        scratch_shapes=[pltpu.VMEM((tm, D_MODEL), F32)],
        compiler_params=_params(("arbitrary",)),
    )(*args)


def _ffn_bwd_act(dh_out, h, a, b, f, gpre, gpost, wg_t, wu_t, wd, name):
    rows = h.shape[0]
    tm = _row_tile(rows)
    nf = D_FF // FF_TILE

    def body(dho_ref, h_ref, a_ref, b_ref, f_ref, gpre_ref, gpost_ref, wg_ref, wu_ref, wd_ref,
             dh_ref, da_ref, db_ref, df_ref, n_ref, dgpre_ref, dgpost_ref, acc):
        first = pl.program_id(0) == 0
        dho = dho_ref[...]
        drr = 0.5 * dho
        fn, rf = _rms(f_ref[...])
        _acc_add(dgpost_ref, first, _colsum(drr * fn))
        df16 = _rms_bwd(fn, rf, gpost_ref[...], drr).astype(BF16)
        df_ref[...] = df16
        hn, rh = _rms(h_ref[...])
        n_ref[...] = (hn * gpre_ref[...]).astype(BF16)
        for j in range(nf):
            cols = slice(j * FF_TILE, (j + 1) * FF_TILE)
            ds = _nt(df16, wd_ref[cols, :])
            av = a_ref[:, cols].astype(F32)
            bv = b_ref[:, cols].astype(F32)
            sg = _sigmoid(av)
            db16 = (ds * (av * sg)).astype(BF16)
            da16 = (ds * bv * (sg * (1.0 + av * (1.0 - sg)))).astype(BF16)
            da_ref[:, cols] = da16
            db_ref[:, cols] = db16
            part = _nn(da16, wg_ref[cols, :]) + _nn(db16, wu_ref[cols, :])
            if j == 0:
                acc[...] = part
            else:
                acc[...] += part
        dn = acc[...]
        _acc_add(dgpre_ref, first, _colsum(dn * hn))
        dh_ref[...] = dho + _rms_bwd(hn, rh, gpre_ref[...], dn)

    row_f32 = _row_spec(tm, D_MODEL)
    row_ff = _row_spec(tm, D_FF)
    return pl.pallas_call(
        body, name=name, grid=(rows // tm,),
        in_specs=[row_f32, row_f32, row_ff, row_ff, row_f32, VMEM_SPEC, VMEM_SPEC, VMEM_SPEC, VMEM_SPEC, VMEM_SPEC],
        out_specs=[row_f32, row_ff, row_ff, row_f32, row_f32, _acc_spec(D_MODEL), _acc_spec(D_MODEL)],
        out_shape=[jax.ShapeDtypeStruct((rows, D_MODEL), F32), jax.ShapeDtypeStruct((rows, D_FF), BF16),
                   jax.ShapeDtypeStruct((rows, D_FF), BF16), jax.ShapeDtypeStruct((rows, D_MODEL), BF16),
                   jax.ShapeDtypeStruct((rows, D_MODEL), BF16), jax.ShapeDtypeStruct((8, D_MODEL), F32),
                   jax.ShapeDtypeStruct((8, D_MODEL), F32)],
        scratch_shapes=[pltpu.VMEM((tm, D_MODEL), F32)],
        compiler_params=_params(("arbitrary",), vmem_mb=62),
    )(dh_out, h, a, b, f, gpre, gpost, wg_t, wu_t, wd)


def _wgrad(lhs, rhs, name, after=None):
    rows, width = lhs.shape
    tm = rows if rows % 1664 == 0 else BLK
    tf = 256 if width % 256 == 0 else 128
    nr = rows // tm

    def body(l_ref, r_ref, *rest):
        o_ref, acc = rest[-2:]
        i = pl.program_id(1)
        part = _tn(l_ref[...], r_ref[...])

        @pl.when(i == 0)
        def _():
            acc[...] = part

        @pl.when(i > 0)
        def _():
            acc[...] += part

        @pl.when(i == nr - 1)
        def _():
            o_ref[...] = acc[...].astype(BF16)

    l_spec = pl.BlockSpec((tm, tf), lambda j, i: (i, j))
    r_spec = pl.BlockSpec((tm, D_MODEL), lambda j, i: (i, 0))
    return pl.pallas_call(
        body, name=name, grid=(width // tf, nr),
        in_specs=[l_spec, r_spec] + ([] if after is None else [ANY_SPEC]),
        out_specs=pl.BlockSpec((tf, D_MODEL), lambda j, i: (j, 0)),
        out_shape=jax.ShapeDtypeStruct((width, D_MODEL), BF16),
        scratch_shapes=[pltpu.VMEM((tf, D_MODEL), F32)],
        compiler_params=_params(("arbitrary", "arbitrary")),
    )(lhs, rhs, *([] if after is None else [after]))


def _chunk_cumsum(x, lower):
    tri = jnp.where(_tri(lower), 1.0, 0.0).astype(BF16)
    hi, lo = _split_bf16(x)
    return _nn(tri, hi) + _nn(tri, lo)


def _mix_in(h, g, win_p, wa2_p, b_a, cos, sin):
    rows = h.shape[0]
    tm = 640 if rows % 640 == 0 else BLK

    def body(h_ref, g_ref, win_ref, wa2_ref, ba_ref, cos_ref, sin_ref,
             gq_ref, gk_ref, gv_ref, gg_ref, sq_ref, sk_ref, sv_ref, ga_ref, loga_ref, bc_ref, n_ref):
        hn, _ = _rms(h_ref[...])
        n16 = (hn * g_ref[...]).astype(BF16)
        n_ref[...] = n16
        proj = _nt(n16, win_ref[...])
        gq_ref[...] = proj[:, P_GQ:P_GK]
        gk_ref[...] = proj[:, P_GK:P_GV]
        gv_ref[...] = proj[:, P_GV:P_GG].astype(BF16)
        gg_ref[...] = proj[:, P_GG:P_SQ]
        c1, s1 = cos_ref[...], sin_ref[...]
        c4 = jnp.concatenate([c1, c1, c1, c1], axis=1)
        s4 = jnp.concatenate([s1, s1, s1, s1], axis=1)
        sq = proj[:, P_SQ:P_SK]
        sk = proj[:, P_SK:P_SV]
        sq_ref[...] = (sq * c4 + _rot_half(sq) * s4).astype(BF16)
        sk_ref[...] = (sk * c1 + _rot_half(sk) * s1).astype(BF16)
        sv_ref[...] = proj[:, P_SV:P_GA].astype(BF16)
        ga = proj[:, P_GA:P_END]
        ga_ref[...] = ga
        z = _nn(ga, wa2_ref[...]) + ba_ref[...]
        loga = (jnp.minimum(z, 0.0) - jnp.log(1.0 + jnp.exp(-jnp.abs(z)))) * (1.0 / GLA_TAU)
        loga_ref[...] = loga
        for c in range(tm // BLK):
            rs = slice(c * BLK, (c + 1) * BLK)
            bc_ref[rs, :] = _chunk_cumsum(loga[rs, :], True)

    f32 = lambda c: jax.ShapeDtypeStruct((rows, c), F32)
    b16 = lambda c: jax.ShapeDtypeStruct((rows, c), BF16)
    rs = lambda c: _row_spec(tm, c)
    return pl.pallas_call(
        body, name="mix_in", grid=(rows // tm,),
        in_specs=[rs(D_MODEL), VMEM_SPEC, VMEM_SPEC, VMEM_SPEC, VMEM_SPEC, rs(128), rs(128)],
        out_specs=[rs(256), rs(256), rs(512), rs(512), rs(512), rs(128), rs(128), rs(128), rs(256), rs(256), rs(D_MODEL)],
        out_shape=[f32(256), f32(256), b16(512), f32(512), b16(512), b16(128), b16(128), f32(128), f32(256), f32(256),
                   b16(D_MODEL)],
        compiler_params=_params(("arbitrary",)),
    )(h, g, win_p, wa2_p, b_a, cos, sin)


def _gla_factors(q, k, bc):
    bm = bc[BLK // 2 - 1:BLK // 2, :]
    bl = bc[BLK - 1:BLK, :]
    e_q, e_k, e_qe, e_kd = jnp.exp(bc - bm), jnp.exp(bm - bc), jnp.exp(bc), jnp.exp(bl - bc)
    return (q * e_q, k * e_k, q * e_qe, k * e_kd), (e_q, e_k, e_qe, e_kd), jnp.exp(bl)


def _gla_fwd(gq, gk, gv, gg, bc, wgn):
    rows = gq.shape[0]
    nc = rows // BLK
    per_step = _blocks_per_step(nc)
    scale = GLA_DK ** -0.5

    def body(q_ref, k_ref, v_ref, gg_ref, bc_ref, wgn_ref, o_ref, cat_ref, sp_ref, st):
        @pl.when(pl.program_id(0) == 0)
        def _():
            st[...] = jnp.zeros_like(st)
        low = _tri(True)
        wgn_v = wgn_ref[...]

        def chunk(c, carry):
            rr = pl.ds(pl.multiple_of(c * BLK, BLK), BLK)
            for p in range(2):
                sl = slice(128 * p, 128 * p + 128)
                (qt, kt, qe, kd), _, ebl = _gla_factors(q_ref[rr, sl] * scale, k_ref[rr, sl], bc_ref[rr, sl])
                s_prev = st[p]
                sp_ref[c, p] = s_prev
                s16 = s_prev.astype(BF16)
                qt16 = qt.astype(BF16)
                s_new = s_prev * ebl
                for hh in range(2):
                    hs = slice(128 * (2 * p + hh), 128 * (2 * p + hh) + 128)
                    lm = _half_mask(128, hh)
                    vh = v_ref[rr, hs]
                    pm = jnp.where(low, _nt(qt16, jnp.where(lm, kt, 0.0).astype(BF16)), 0.0)
                    o = _nn(pm.astype(BF16), vh) + _nt(jnp.where(lm, qe, 0.0).astype(BF16), s16)
                    s_new = s_new + _tn(vh, jnp.where(lm, kd, 0.0).astype(BF16))
                    o_ref[rr, hs] = o
                    on, _ = _rms(o)
                    gate = gg_ref[rr, hs]
                    cat_ref[rr, hs] = (on * wgn_v * (gate * _sigmoid(gate))).astype(BF16)
                st[p] = s_new
            return carry
        lax.fori_loop(0, per_step, chunk, 0)

    rs = lambda c: _row_spec(per_step * BLK, c)
    return pl.pallas_call(
        body, name="gla_fwd", grid=(nc // per_step,),
        in_specs=[rs(256), rs(256), rs(512), rs(512), rs(256), VMEM_SPEC],
        out_specs=[rs(512), rs(512), pl.BlockSpec((per_step, 2, 128, 128), lambda i: (i, 0, 0, 0))],
        out_shape=[jax.ShapeDtypeStruct((rows, 512), F32), jax.ShapeDtypeStruct((rows, 512), BF16),
                   jax.ShapeDtypeStruct((nc, 2, 128, 128), F32)],
        scratch_shapes=[pltpu.VMEM((2, 128, 128), F32)],
        compiler_params=_params(("arbitrary",)),
    )(gq, gk, gv, gg, bc, wgn)


def _gla_bwd(dcat, o_all, gq, gk, gv, gg, bc, sp, wgn):
    rows = gq.shape[0]
    nc = rows // BLK
    per_step = _blocks_per_step(nc)
    steps = nc // per_step
    scale = GLA_DK ** -0.5

    def body(dc_ref, o_ref, q_ref, k_ref, v_ref, gg_ref, bc_ref, sp_ref, wgn_ref,
             dq_ref, dk_ref, dv_ref, dgg_ref, dla_ref, dwgn_ref, dst):
        first = pl.program_id(0) == 0

        @pl.when(first)
        def _():
            dst[...] = jnp.zeros_like(dst)
        low, upp = _tri(True), _tri(False)
        last_row = lax.broadcasted_iota(jnp.int32, (BLK, 1), 0) == BLK - 1
        wgn_v = wgn_ref[...]

        def chunk(c, dwgn):
            rr = pl.ds(pl.multiple_of((per_step - 1 - c) * BLK, BLK), BLK)
            for p in range(2):
                sl = slice(128 * p, 128 * p + 128)
                (qt, kt, qe, kd), (e_q, e_k, e_qe, e_kd), ebl = _gla_factors(
                    q_ref[rr, sl] * scale, k_ref[rr, sl], bc_ref[rr, sl])
                s_prev = sp_ref[per_step - 1 - c, p]
                s16 = s_prev.astype(BF16)
                ds_next = dst[p]
                ds16 = ds_next.astype(BF16)
                qt16 = qt.astype(BF16)
                ds_new = ds_next * ebl
                dqt = jnp.zeros((BLK, 128), F32)
                dkt = jnp.zeros((BLK, 128), F32)
                dqe = jnp.zeros((BLK, 128), F32)
                dkd = jnp.zeros((BLK, 128), F32)
                for hh in range(2):
                    hs = slice(128 * (2 * p + hh), 128 * (2 * p + hh) + 128)
                    lm = _half_mask(128, hh)
                    on, ro = _rms(o_ref[rr, hs])
                    gate = gg_ref[rr, hs]
                    sg = _sigmoid(gate)
                    si = gate * sg
                    dog = dc_ref[rr, hs]
                    dwgn = dwgn + _colsum(dog * si * on)
                    dgg_ref[rr, hs] = dog * (on * wgn_v) * (sg * (1.0 + gate * (1.0 - sg)))
                    do16 = _rms_bwd(on, ro, wgn_v, dog * si).astype(BF16)
                    vh = v_ref[rr, hs]
                    ktm16 = jnp.where(lm, kt, 0.0).astype(BF16)
                    qtm16 = jnp.where(lm, qt, 0.0).astype(BF16)
                    qem16 = jnp.where(lm, qe, 0.0).astype(BF16)
                    kdm16 = jnp.where(lm, kd, 0.0).astype(BF16)
                    p_t = jnp.where(upp, _nt(ktm16, qt16), 0.0)
                    dp_t = jnp.where(upp, _nt(vh, do16), 0.0)
                    dp = jnp.where(low, _nt(do16, vh), 0.0)
                    dv_ref[rr, hs] = _nn(p_t.astype(BF16), do16) + _nt(kdm16, ds16)
                    dqt = dqt + _nn(dp.astype(BF16), ktm16)
                    dkt = dkt + _nn(dp_t.astype(BF16), qtm16)
                    dqe = dqe + jnp.where(lm, _nn(do16, s16), 0.0)
                    dkd = dkd + jnp.where(lm, _nn(vh, ds16), 0.0)
                    ds_new = ds_new + _tn(do16, qem16)
                debl = _colsum(ds_next * s_prev)
                dq_ref[rr, sl] = (dqt * e_q + dqe * e_qe) * scale
                dk_ref[rr, sl] = dkt * e_k + dkd * e_kd
                dkd_kd = dkd * kd
                db = dqt * qt - dkt * kt + dqe * qe - dkd_kd
                db = jnp.where(last_row, db + (_colsum(dkd_kd) + debl * ebl), db)
                dla_ref[rr, sl] = _chunk_cumsum(db, False)
                dst[p] = ds_new
            return dwgn
        dwgn = lax.fori_loop(0, per_step, chunk, jnp.zeros((1, 128), F32))
        _acc_add(dwgn_ref, first, dwgn)

    rev = lambda c: pl.BlockSpec((per_step * BLK, c), lambda i: (steps - 1 - i, 0))
    f32 = lambda c: jax.ShapeDtypeStruct((rows, c), F32)
    return pl.pallas_call(
        body, name="gla_bwd", grid=(steps,),
        in_specs=[rev(512), rev(512), rev(256), rev(256), rev(512), rev(512), rev(256),
                  pl.BlockSpec((per_step, 2, 128, 128), lambda i: (steps - 1 - i, 0, 0, 0)), VMEM_SPEC],
        out_specs=[rev(256), rev(256), rev(512), rev(512), rev(256), _acc_spec(128)],
        out_shape=[f32(256), f32(256), f32(512), f32(512), f32(256), jax.ShapeDtypeStruct((8, 128), F32)],
        scratch_shapes=[pltpu.VMEM((2, 128, 128), F32)],
        compiler_params=_params(("arbitrary",)),
    )(dcat, o_all, gq, gk, gv, gg, bc, sp, wgn)


def _swa_masks(i):
    t = lax.broadcasted_iota(jnp.int32, (BLK, BLK), 0)
    c = lax.broadcasted_iota(jnp.int32, (BLK, BLK), 1)
    own_side = c <= t
    band_ok = i >= jnp.where(own_side, 1, 2)
    meta_ok = (c % N_META) <= jnp.where(i >= 1, N_META, t - PAD_ROWS)
    return own_side, band_ok, meta_ok, c // N_META


def _swa_blocks(ref, i):
    prev = pl.multiple_of(jnp.maximum(i - 1, 0) * BLK, BLK)
    own = pl.multiple_of(i * BLK, BLK)
    return jnp.concatenate([ref[pl.ds(prev, BLK), :], ref[pl.ds(own, BLK), :]], axis=0), prev, own


def _swa_meta_operand(ref):
    blk = ref[0:BLK, :]
    swapped = pltpu.roll(blk, 64, 1)
    lo = jnp.where(_half_mask(128, 0), blk, swapped)
    hi = jnp.where(_half_mask(128, 1), blk, swapped)
    meta = jnp.concatenate([lo, lo, hi, hi], axis=1)[PAD_ROWS:BLK, :]
    tiled = jnp.concatenate([meta] * SWA_HEADS, axis=0)
    j = lax.broadcasted_iota(jnp.int32, tiled.shape, 0)
    lane = lax.broadcasted_iota(jnp.int32, tiled.shape, 1)
    return jnp.where(j // N_META == lane // SWA_HD, tiled, jnp.zeros_like(tiled))


def _swa_meta_fold(acc):
    out = jnp.zeros((N_META, 128), F32)
    for hd in range(SWA_HEADS):
        half, kv = hd % 2, hd // 4
        piece = acc[N_META * hd:N_META * (hd + 1), 128 * (hd // 2):128 * (hd // 2) + 128]
        piece = jnp.where(_half_mask(128, half), piece, 0.0)
        out = out + (piece if half == kv else pltpu.roll(piece, 64, 1))
    return out


def _by_head(group, per_head):
    out = jnp.zeros((BLK, BLK), F32)
    for hd, v in enumerate(per_head):
        out = jnp.where(group == hd, v, out)
    return out


def _place(x, kv):
    if kv == 0:
        lo = jnp.where(_half_mask(128, 0), x, jnp.zeros_like(x))
        return lo, pltpu.roll(lo, 64, 1)
    hi = jnp.where(_half_mask(128, 1), x, jnp.zeros_like(x))
    return pltpu.roll(hi, 64, 1), hi


def _swa_fwd(sq, sk, sv, sinks, wn):
    rows = sq.shape[0]
    nb = rows // BLK
    per_step = _blocks_per_step(nb)
    scale = SWA_HD ** -0.5

    def body(q_ref, k_ref, v_ref, sink_ref, wn_ref, o_ref, cat_ref, lse_ref, kp, vp):
        step = pl.program_id(0)

        @pl.when(step == 0)
        def _():
            kp[...] = _swa_meta_operand(k_ref)
            vp[...] = _swa_meta_operand(v_ref)

        def one_block(c, carry):
            i = step * per_step + c
            rr = pl.ds(pl.multiple_of(c * BLK, BLK), BLK)
            own_side, band_ok, meta_ok, group = _swa_masks(i)
            k2, _, _ = _swa_blocks(k_ref, i)
            v2, _, _ = _swa_blocks(v_ref, i)
            kz = (_place(k2, 0), _place(k2, 1))
            vz = (_place(v2, 0), _place(v2, 1))
            q_all = q_ref[rr, :]
            s_meta = jnp.where(meta_ok, _nt(q_all, kp[...]) * scale, NEG_INF)
            s_band, m = [], []
            for hd in range(SWA_HEADS):
                kv, half = hd // 4, hd % 2
                q_pair = q_all[:, 128 * (hd // 2):128 * (hd // 2) + 128]
                s2 = _nt(q_pair, kz[kv][half])
                s = jnp.where(band_ok, jnp.where(own_side, s2[:, BLK:], s2[:, :BLK]) * scale, NEG_INF)
                top = jnp.maximum(jnp.max(s, axis=-1, keepdims=True),
                                  jnp.max(jnp.where(group == hd, s_meta, NEG_INF), axis=-1, keepdims=True))
                s_band.append(s)
                m.append(jnp.maximum(top, sink_ref[0, hd]))
            e_meta = jnp.exp(s_meta - _by_head(group, m))
            o_meta = _nn(e_meta.astype(BF16), vp[...])
            outs = []
            for pr in range(4):
                o_pair = o_meta[:, 128 * pr:128 * pr + 128]
                rden = []
                for half in range(2):
                    hd = 2 * pr + half
                    kv = hd // 4
                    e = jnp.exp(s_band[hd] - m[hd])
                    den = (jnp.sum(e, axis=-1, keepdims=True)
                           + jnp.sum(jnp.where(group == hd, e_meta, 0.0), axis=-1, keepdims=True)
                           + jnp.exp(sink_ref[0, hd] - m[hd]))
                    lse_ref[rr, hd:hd + 1] = m[hd] + jnp.log(den)
                    rden.append(1.0 / den)
                    e2 = jnp.concatenate([jnp.where(own_side, 0.0, e), jnp.where(own_side, e, 0.0)], axis=1).astype(BF16)
                    o_pair = o_pair + _nn(e2, vz[kv][half])
                outs.append(o_pair * jnp.where(_half_mask(128, 0), rden[0], rden[1]))
            o = jnp.concatenate(outs, axis=1)
            o_ref[rr, :] = o
            on, _ = _rms(o)
            cat_ref[rr, :] = (on * wn_ref[...]).astype(BF16)
            return carry
        lax.fori_loop(0, per_step, one_block, 0)

    return pl.pallas_call(
        body, name="swa_fwd", grid=(nb // per_step,),
        in_specs=[_row_spec(per_step * BLK, 512), VMEM_SPEC, VMEM_SPEC, SMEM_SPEC, VMEM_SPEC],
        out_specs=[_row_spec(per_step * BLK, 512), _row_spec(per_step * BLK, 512), _row_spec(per_step * BLK, SWA_HEADS)],
        out_shape=[jax.ShapeDtypeStruct((rows, 512), F32), jax.ShapeDtypeStruct((rows, 512), BF16),
                   jax.ShapeDtypeStruct((rows, SWA_HEADS), F32)],
        scratch_shapes=[pltpu.VMEM((BLK, 512), BF16), pltpu.VMEM((BLK, 512), BF16)],
        compiler_params=_params(("arbitrary",)),
    )(sq, sk, sv, sinks, wn)


def _swa_bwd(dcat, o_all, sq, sk, sv, lse, sinks, wn):
    rows = sq.shape[0]
    nb = rows // BLK
    per_step = _blocks_per_step(nb)
    steps = nb // per_step
    scale = SWA_HD ** -0.5

    def body(dc_ref, o_ref, q_ref, k_ref, v_ref, lse_ref, sink_ref, wn_ref, dq_ref, dk_ref, dv_ref, dsink_ref, dwn_ref,
             kp, vp, dkp, dvp):
        step = pl.program_id(0)

        @pl.when(step == 0)
        def _():
            dk_ref[...] = jnp.zeros_like(dk_ref)
            dv_ref[...] = jnp.zeros_like(dv_ref)
            dkp[...] = jnp.zeros_like(dkp)
            dvp[...] = jnp.zeros_like(dvp)
            kp[...] = _swa_meta_operand(k_ref)
            vp[...] = _swa_meta_operand(v_ref)

        def one_block(c, carry):
            i = step * per_step + c
            rr = pl.ds(pl.multiple_of(c * BLK, BLK), BLK)
            first = i == 0
            own_side, band_ok, meta_ok, group = _swa_masks(i)
            k2, prev, own = _swa_blocks(k_ref, i)
            v2, _, _ = _swa_blocks(v_ref, i)
            kz = (_place(k2, 0), _place(k2, 1))
            vz = (_place(v2, 0), _place(v2, 1))
            o = o_ref[rr, :]
            on, ro = _rms(o)
            dc = dc_ref[rr, :]
            _acc_add(dwn_ref, first, _colsum(dc * on))
            do = _rms_bwd(on, ro, wn_ref[...], dc)
            do_o = do * o
            do16 = do.astype(BF16)
            q_all = q_ref[rr, :]
            lse = [lse_ref[rr, hd:hd + 1] for hd in range(SWA_HEADS)]
            delta = [jnp.sum(jnp.where(_half_mask(128, hd % 2), do_o[:, 128 * (hd // 2):128 * (hd // 2) + 128], 0.0),
                             axis=-1, keepdims=True) for hd in range(SWA_HEADS)]
            s_meta = jnp.where(meta_ok, _nt(q_all, kp[...]) * scale, NEG_INF)
            p_meta = jnp.exp(s_meta - _by_head(group, lse))
            ds_meta16 = (p_meta * (_nt(do16, vp[...]) - _by_head(group, delta)) * scale).astype(BF16)
            dq_meta = _nn(ds_meta16, kp[...])
            dkp[...] += _tn(ds_meta16, q_all)
            dvp[...] += _tn(p_meta.astype(BF16), do16)
            own2 = jnp.concatenate([own_side.astype(jnp.int32)] * 2, axis=0) > 0
            ok2 = jnp.concatenate([band_ok.astype(jnp.int32)] * 2, axis=0) > 0

            def window(x2):
                return jnp.where(own2, x2[:, BLK:], x2[:, :BLK])

            def unwindow(x):
                return jnp.concatenate([jnp.where(own2, 0.0, x), jnp.where(own2, x, 0.0)], axis=1).astype(BF16)
            lane8 = lax.broadcasted_iota(jnp.int32, (1, 128), 1)
            dsink = jnp.zeros((1, 128), F32)
            dq_pairs = [dq_meta[:, 128 * pr:128 * pr + 128] for pr in range(4)]
            dk2 = [[None, None], [None, None]]
            dv2 = [[None, None], [None, None]]
            for kv in range(2):
                for half in range(2):
                    heads, pairs = (4 * kv + half, 4 * kv + 2 + half), (2 * kv, 2 * kv + 1)
                    q_s = jnp.concatenate([q_all[:, 128 * pr:128 * pr + 128] for pr in pairs], axis=0)
                    do_s = jnp.concatenate([do16[:, 128 * pr:128 * pr + 128] for pr in pairs], axis=0)
                    lse_s = jnp.concatenate([lse[hd] for hd in heads], axis=0)
                    delta_s = jnp.concatenate([delta[hd] for hd in heads], axis=0)
                    s = jnp.where(ok2, window(_nt(q_s, kz[kv][half])) * scale, NEG_INF)
                    prob = jnp.exp(s - lse_s)
                    for hd in heads:
                        dsink = dsink + jnp.where(lane8 == hd, -jnp.sum(jnp.exp(sink_ref[0, hd] - lse[hd]) * delta[hd]), 0.0)
                    ds2 = unwindow(prob * (window(_nt(do_s, vz[kv][half])) - delta_s) * scale)
                    dq_s = _nn(ds2, kz[kv][half])
                    dq_pairs[pairs[0]] = dq_pairs[pairs[0]] + dq_s[:BLK]
                    dq_pairs[pairs[1]] = dq_pairs[pairs[1]] + dq_s[BLK:]
                    dk2[kv][half] = _tn(ds2, q_s)
                    dv2[kv][half] = _tn(unwindow(prob), do_s)
            dq_ref[rr, :] = jnp.concatenate(dq_pairs, axis=1)
            _acc_add(dsink_ref, first, dsink)
            for ref, acc2 in ((dk_ref, dk2), (dv_ref, dv2)):
                tot = jnp.zeros((2 * BLK, 128), F32)
                for kv in range(2):
                    for half in range(2):
                        part = jnp.where(_half_mask(128, half), acc2[kv][half], 0.0)
                        tot = tot + (part if half == kv else pltpu.roll(part, 64, 1))
                ref[pl.ds(prev, BLK), :] += tot[:BLK]
                ref[pl.ds(own, BLK), :] += tot[BLK:]
            return carry
        lax.fori_loop(0, per_step, one_block, 0)

        @pl.when(step == steps - 1)
        def _():
            dk_ref[PAD_ROWS:BLK, :] += _swa_meta_fold(dkp[...])
            dv_ref[PAD_ROWS:BLK, :] += _swa_meta_fold(dvp[...])

    full = pl.BlockSpec((rows, 128), lambda i: (0, 0))
    blocks = lambda cols: _row_spec(per_step * BLK, cols)
    return pl.pallas_call(
        body, name="swa_bwd", grid=(steps,),
        in_specs=[blocks(512), blocks(512), blocks(512), VMEM_SPEC, VMEM_SPEC, blocks(SWA_HEADS), SMEM_SPEC, VMEM_SPEC],
        out_specs=[blocks(512), full, full, _acc_spec(128), _acc_spec(512)],
        out_shape=[jax.ShapeDtypeStruct((rows, 512), F32), jax.ShapeDtypeStruct((rows, 128), F32),
                   jax.ShapeDtypeStruct((rows, 128), F32), jax.ShapeDtypeStruct((8, 128), F32),
                   jax.ShapeDtypeStruct((8, 512), F32)],
        scratch_shapes=[pltpu.VMEM((BLK, 512), BF16), pltpu.VMEM((BLK, 512), BF16),
                        pltpu.VMEM((BLK, 512), F32), pltpu.VMEM((BLK, 512), F32)],
        compiler_params=_params(("arbitrary",)),
    )(dcat, o_all, sq, sk, sv, lse, sinks, wn)


def _mix_out(h, cat_g, cat_s, wout, gpost):
    rows = h.shape[0]
    tm = _row_tile(rows)

    def body(h_ref, cg_ref, cs_ref, w_ref, g_ref, ho_ref, m_ref):
        m = _nn(cg_ref[...], w_ref[0:512, :]) + _nn(cs_ref[...], w_ref[512:1024, :])
        m_ref[...] = m
        mn, _ = _rms(m)
        ho_ref[...] = h_ref[...] + mn * g_ref[...]

    row_f32 = _row_spec(tm, D_MODEL)
    return pl.pallas_call(
        body, name="mix_out", grid=(rows // tm,),
        in_specs=[row_f32, _row_spec(tm, 512), _row_spec(tm, 512), VMEM_SPEC, VMEM_SPEC],
        out_specs=[row_f32, row_f32],
        out_shape=[jax.ShapeDtypeStruct((rows, D_MODEL), F32), jax.ShapeDtypeStruct((rows, D_MODEL), F32)],
        compiler_params=_params(("arbitrary",)),
    )(h, cat_g, cat_s, wout, gpost)


def _mix_out_bwd(dh, m, wout, gpost):
    rows = dh.shape[0]
    tm = _row_tile(rows)

    def body(dh_ref, m_ref, w_ref, g_ref, dcg_ref, dcs_ref, dm_ref, dg_ref):
        first = pl.program_id(0) == 0
        dhv = dh_ref[...]
        mn, rm = _rms(m_ref[...])
        _acc_add(dg_ref, first, _colsum(dhv * mn))
        dm16 = _rms_bwd(mn, rm, g_ref[...], dhv).astype(BF16)
        dm_ref[...] = dm16
        dcat = _nt(dm16, w_ref[...])
        dcg_ref[...] = dcat[:, 0:512]
        dcs_ref[...] = dcat[:, 512:1024]

    row_f32 = _row_spec(tm, D_MODEL)
    return pl.pallas_call(
        body, name="mix_out_bwd", grid=(rows // tm,),
        in_specs=[row_f32, row_f32, VMEM_SPEC, VMEM_SPEC],
        out_specs=[_row_spec(tm, 512), _row_spec(tm, 512), row_f32, _acc_spec(D_MODEL)],
        out_shape=[jax.ShapeDtypeStruct((rows, 512), F32), jax.ShapeDtypeStruct((rows, 512), F32),
                   jax.ShapeDtypeStruct((rows, D_MODEL), BF16), jax.ShapeDtypeStruct((8, D_MODEL), F32)],
        compiler_params=_params(("arbitrary",)),
    )(dh, m, wout, gpost)


def _mix_in_bwd(dh_out, h, g, win_p, wa2_p, cos, sin, loga, ga, dgq, dgk, dgv, dgg, dsq, dsk, dsv, dloga):
    rows = h.shape[0]
    tm = _row_tile(rows)

    def body(dho_ref, h_ref, g_ref, win_ref, wa2_ref, cos_ref, sin_ref, loga_ref, ga_ref,
             dgq_ref, dgk_ref, dgv_ref, dgg_ref, dsq_ref, dsk_ref, dsv_ref, dla_ref,
             dh_ref, dproj_ref, dwa2_ref, dg_ref, dba_ref):
        first = pl.program_id(0) == 0
        dz = dla_ref[...] * (1.0 / GLA_TAU) * (1.0 - jnp.exp(GLA_TAU * loga_ref[...]))
        _acc_add(dba_ref, first, _colsum(dz))
        dga = _nt(dz, wa2_ref[...])
        pa = _tn(ga_ref[...], dz)
        c1, s1 = cos_ref[...], sin_ref[...]
        c4 = jnp.concatenate([c1, c1, c1, c1], axis=1)
        s4 = jnp.concatenate([s1, s1, s1, s1], axis=1)
        dq_r, dk_r = dsq_ref[...], dsk_ref[...]
        dsq = dq_r * c4 - _rot_half(dq_r * s4)
        dsk = dk_r * c1 - _rot_half(dk_r * s1)
        dproj16 = jnp.concatenate(
            [dgq_ref[...], dgk_ref[...], dgv_ref[...], dgg_ref[...], dsq, dsk, dsv_ref[...], dga], axis=1).astype(BF16)
        dproj_ref[...] = dproj16
        dn = _nn(dproj16, win_ref[...])

        @pl.when(first)
        def _():
            dwa2_ref[...] = pa

        @pl.when(jnp.logical_not(first))
        def _():
            dwa2_ref[...] += pa
        hn, rh = _rms(h_ref[...])
        _acc_add(dg_ref, first, _colsum(dn * hn))
        dh_ref[...] = dho_ref[...] + _rms_bwd(hn, rh, g_ref[...], dn)

    rs = lambda c: _row_spec(tm, c)
    return pl.pallas_call(
        body, name="mix_in_bwd", grid=(rows // tm,),
        in_specs=[rs(D_MODEL), rs(D_MODEL), VMEM_SPEC, VMEM_SPEC, VMEM_SPEC, rs(128), rs(128), rs(256), rs(128),
                  rs(256), rs(256), rs(512), rs(512), rs(512), rs(128), rs(128), rs(256)],
        out_specs=[rs(D_MODEL), rs(P_END), pl.BlockSpec((128, 256), lambda i: (0, 0)), _acc_spec(D_MODEL), _acc_spec(256)],
        out_shape=[jax.ShapeDtypeStruct((rows, D_MODEL), F32), jax.ShapeDtypeStruct((rows, P_END), BF16),
                   jax.ShapeDtypeStruct((128, 256), F32), jax.ShapeDtypeStruct((8, D_MODEL), F32),
                   jax.ShapeDtypeStruct((8, 256), F32)],
        compiler_params=_params(("arbitrary",)),
    )(dh_out, h, g, win_p, wa2_p, cos, sin, loga, ga, dgq, dgk, dgv, dgg, dsq, dsk, dsv, dloga)


def _rope_tables(rows):
    pos = (jnp.arange(rows, dtype=jnp.int32) - PAD_ROWS).astype(F32)
    inv_freq = 1.0 / (ROPE_THETA ** (jnp.arange(0, SWA_HD, 2, dtype=F32) / SWA_HD))
    ang = pos[:, None] * inv_freq[None, :]
    return jnp.tile(jnp.cos(ang), (1, 4)), jnp.tile(jnp.sin(ang), (1, 4))


def _local_step(x, tgt, front, w, late_weights=None, on_grads=None, on_small=None):
    cos, sin = _rope_tables(x.shape[0] + BLK)
    g = {}

    def tell(group, names):
        for nm in names:
            g[nm] = grads_now[nm]
        return None if on_grads is None else on_grads(group, {nm: grads_now[nm] for nm in names})

    h0, h1, a1, b1, s1, f1 = _ffn_fwd(x, w["ffn1_pre"], w["wg1"], w["wu1"], w["wd1"], w["ffn1_post"], front=front)
    if late_weights is not None:
        w = {**w, **late_weights("win", f1)}
    gq, gk, gv, gg, sq, sk, sv, ga, loga, bc, n2 = _mix_in(h1, w["mix_pre"], w["win"], w["wa2"], w["b_a"], cos, sin)
    o_g, cat_g, sp = _gla_fwd(gq, gk, gv, gg, bc, w["gla_norm"])
    o_s, cat_s, lse = _swa_fwd(sq, sk, sv, w["sinks"], w["swa_norm"])
    if late_weights is not None:
        w = {**w, **late_weights("rest", lse)}
    h2, m = _mix_out(h1, cat_g, cat_s, w["wout"], w["mix_post"])
    h3, a2, b2, s2, f2, dy, loss = _ffn_fwd(h2, w["ffn2_pre"], w["wg2"], w["wu2"], w["wd2"], w["ffn2_post"], tgt)
    del h3
    dh2, da, db, df, n3, g["ffn2_pre"], g["ffn2_post"] = _ffn_bwd_act(
        dy, h2, a2, b2, f2, w["ffn2_pre"], w["ffn2_post"], w["wg2"], w["wu2"], w["wd2"], "ffn2_bwd_act")
    grads_now = dict(wd2=_wgrad(s2, df, "ffn2_wgrad_down"), wg2=_wgrad(da, n3, "ffn2_wgrad_gate"),
                     wu2=_wgrad(db, n3, "ffn2_wgrad_up"))
    tok = tell("ffn2", ("wd2", "wg2", "wu2"))
    dcg, dcs, dm, g["mix_post"] = _mix_out_bwd(dh2, m, w["wout"], w["mix_post"] + (0.0 if tok is None else tok[0, 0]))
    dsq, dsk, dsv, g["sinks"], g["swa_norm"] = _swa_bwd(dcs, o_s, sq, sk, sv, lse, w["sinks"], w["swa_norm"])
    dgq, dgk, dgv, dgg, dloga, g["gla_norm"] = _gla_bwd(dcg, o_g, gq, gk, gv, gg, bc, sp, w["gla_norm"])
    dh1, dproj, g["wa2"], g["mix_pre"], g["b_a"] = _mix_in_bwd(
        dh2, h1, w["mix_pre"], w["win"], w["wa2"], cos, sin, loga, ga, dgq, dgk, dgv, dgg, dsq, dsk, dsv, dloga)
    dh0, da, db, df, n1, g["ffn1_pre"], g["ffn1_post"] = _ffn_bwd_act(
        dh1, h0, a1, b1, f1, w["ffn1_pre"], w["ffn1_post"], w["wg1"], w["wu1"], w["wd1"], "ffn1_bwd_act")
    tok = None if on_small is None else on_small(loss[0, 0], dh0, g)
    grads_now = dict(wd1=_wgrad(s1, df, "ffn1_wgrad_down", after=tok))
    tok = tell("ffn1_down", ("wd1",))
    grads_now = dict(wg1=_wgrad(da, n1, "ffn1_wgrad_gate", after=tok))
    tok = tell("ffn1_gate", ("wg1",))
    grads_now = dict(wu1=_wgrad(db, n1, "ffn1_wgrad_up", after=tok))
    tok = tell("ffn1_up", ("wu1",))
    grads_now = dict(win=_wgrad(dproj, n2, "win_wgrad", after=tok),
                     wout=jnp.concatenate([_wgrad(cat_g, dm, "wout_wgrad_gla", after=tok),
                                           _wgrad(cat_s, dm, "wout_wgrad_swa", after=tok)], axis=0))
    tell("mix", ("wout", "win"))
    return loss[0, 0], dh0, g


def _win_pad_rows(win_t):
    pad = jnp.zeros((P_END - P_GA - 16, win_t.shape[1]), win_t.dtype)
    return jnp.concatenate([win_t[0:1536], win_t[1552:2320], win_t[1536:1552], pad], axis=0)


def _win_unpad_rows(win_p):
    return jnp.concatenate([win_p[0:1536], win_p[P_GA:P_GA + 16], win_p[1536:P_GA]], axis=0)


def _place_on_mesh():
    return lax.axis_index("x"), lax.axis_index("y"), lax.axis_index("c")


def _dev_index(px, py, pc):
    return 4 * px + 2 * py + pc


def _other_devices(x, y, c):
    flip = lambda v, f: 1 - v if f else v
    return [(flip(x, fx), flip(y, fy), flip(c, fc)) for fx in (0, 1) for fy in (0, 1) for fc in (0, 1)][1:]


def _all_gather(shards):
    n = len(shards)

    def body(*refs):
        ins, outs = refs[:n], refs[n:2 * n]
        zeros_ref, send_sems, recv_sems, local_sems = refs[2 * n:]
        zeros_ref[...] = jnp.zeros_like(zeros_ref)
        x, y, c = _place_on_mesh()
        me, sibling = (x, y, c), (x, y, 1 - c)
        chips = [(1 - x, y), (x, 1 - y), (1 - x, 1 - y)]

        def rows(k, px, py, pc):
            r = ins[k].shape[0]
            return outs[k].at[pl.ds(pl.multiple_of(_dev_index(px, py, pc) * r, 8), r), :]

        def copy(k, slot, block, to, src=None):
            return pltpu.make_async_remote_copy(
                src_ref=rows(k, *block) if src is None else src, dst_ref=rows(k, *block),
                send_sem=send_sems.at[k, slot], recv_sem=recv_sems.at[k, slot], device_id=to, device_id_type=MESH)

        local = [pltpu.make_async_copy(ins[k], rows(k, *me), local_sems.at[k]) for k in range(n)]
        sends = []
        for k in range(n):
            local[k].start()
            sends.append(copy(k, 0, me, sibling, src=ins[k]))
            sends += [copy(k, 1 + j, me, (*chip, c), src=ins[k]) for j, chip in enumerate(chips)]
        for cp in sends:
            cp.start()
        for k in range(n):
            for j, chip in enumerate(chips):
                copy(k, 1 + j, (*chip, c), me).wait_recv()
                passed = copy(k, 4 + j, (*chip, c), sibling)
                passed.start()
                sends.append(passed)
        for k in range(n):
            copy(k, 0, sibling, me).wait_recv()
            for j, chip in enumerate(chips):
                copy(k, 4 + j, (*chip, 1 - c), me).wait_recv()
        for cp in sends:
            cp.wait_send()
        for cp in local:
            cp.wait()

    return pl.pallas_call(
        body, name="all_gather_weights",
        in_specs=[ANY_SPEC] * n, out_specs=[ANY_SPEC] * n + [VMEM_SPEC],
        out_shape=[jax.ShapeDtypeStruct((N_DEV * s.shape[0], s.shape[1]), s.dtype) for s in shards]
        + [jax.ShapeDtypeStruct((8, 128), F32)],
        scratch_shapes=[pltpu.SemaphoreType.DMA((n, 7)), pltpu.SemaphoreType.DMA((n, 7)), pltpu.SemaphoreType.DMA((n,))],
    )(*shards)


HBM_SPEC = pl.BlockSpec(memory_space=pltpu.HBM)
SEM_SPEC = pl.BlockSpec(memory_space=pltpu.SEMAPHORE)
DATAFLOW = pltpu.SideEffectType.DATAFLOW_SIDE_EFFECTING


GATHER, SCATTER, SCATTER_CHIPS = "gather", "scatter", "scatter among chips"


def _exchange_peers(kind):
    x, y, c = _place_on_mesh()
    if kind == SCATTER_CHIPS:
        peers = [(1 - x, y, c), (x, 1 - y, c), (1 - x, 1 - y, c)]
        return peers, [2 * p[0] + p[1] for p in peers], 2 * x + y, 4
    peers = _other_devices(x, y, c)
    return peers, [_dev_index(*p) for p in peers], _dev_index(x, y, c), N_DEV


def _exchange_copies(srcs, lands, send_sems, recv_sems, own_sems, kind, arriving):
    peers, theirs, me, blocks = _exchange_peers(kind)
    remote, local = [], []
    for k, (src, land) in enumerate(zip(srcs, lands)):
        r = land.shape[0] // blocks

        def block(ref, d):
            return ref.at[pl.ds(pl.multiple_of(d * r, 8), r), :]

        for f, (peer, him) in enumerate(zip(peers, theirs)):
            mine, his = (him, me) if arriving else (me, him)
            sem = len(peers) * k + f
            remote.append(pltpu.make_async_remote_copy(
                src_ref=src if kind == GATHER else block(src, his), dst_ref=block(land, mine),
                send_sem=send_sems.at[sem], recv_sem=recv_sems.at[sem], device_id=peer, device_id_type=MESH))
        local.append(pltpu.make_async_copy(src if kind == GATHER else block(src, me), block(land, me), own_sems.at[k]))
    return remote, local


def _exchange_start(srcs, kind, name):
    n = len(srcs)
    lands = [lax.empty((N_DEV * s.shape[0], s.shape[1]) if kind == GATHER else s.shape, s.dtype) for s in srcs]
    sems = (3 if kind == SCATTER_CHIPS else 7) * n

    def body(*refs):
        remote, local = _exchange_copies(refs[:n], refs[n:2 * n], *refs[2 * n:2 * n + 3], kind, False)
        for cp in remote + local:
            cp.start()
        refs[-1][...] = jnp.zeros_like(refs[-1])

    both = list(srcs) + list(lands)
    outs = pl.pallas_call(
        body, name=name,
        out_shape=(pltpu.SemaphoreType.DMA((sems,)), pltpu.SemaphoreType.DMA((sems,)), pltpu.SemaphoreType.DMA((n,)),
                   *[pltpu.HBM(a.shape, a.dtype) for a in both], jax.ShapeDtypeStruct((8, 128), F32)),
        in_specs=[HBM_SPEC] * (2 * n), out_specs=(SEM_SPEC, SEM_SPEC, SEM_SPEC, *[HBM_SPEC] * (2 * n), VMEM_SPEC),
        input_output_aliases={i: 3 + i for i in range(2 * n)},
        compiler_params=pltpu.CompilerParams(has_side_effects=DATAFLOW),
    )(*[pltpu.with_memory_space_constraint(a, pltpu.HBM) for a in both])
    return outs[0:3], outs[3:3 + n], outs[3 + n:3 + 2 * n], outs[-1]


def _exchange_wait(started, kind, after, name):
    sems, srcs, lands, _ = started
    n = len(srcs)

    def body(*refs):
        args = (refs[:n], refs[n:2 * n], *refs[2 * n:2 * n + 3], kind)
        going, local = _exchange_copies(*args, False)
        for cp in going:
            cp.wait_send()
        for cp in local:
            cp.wait()
        for cp in _exchange_copies(*args, True)[0]:
            cp.wait_recv()

    both = list(srcs) + list(lands)
    outs = pl.pallas_call(
        body, name=name, out_shape=[pltpu.HBM(a.shape, a.dtype) for a in both],
        in_specs=[HBM_SPEC] * (2 * n) + [SEM_SPEC, SEM_SPEC, SEM_SPEC, ANY_SPEC], out_specs=[HBM_SPEC] * (2 * n),
        input_output_aliases={i: i for i in range(2 * n)},
        compiler_params=pltpu.CompilerParams(has_side_effects=DATAFLOW),
    )(*both, *sems, after)
    return outs[n:]


def _sibling_reduce(part, name):
    r, cols = part.shape[0] // N_DEV, part.shape[1]

    def body(p_ref, o_ref, mine, got, send_sems, recv_sems, own_sems):
        x, y, c = _place_on_mesh()

        def block(d):
            return p_ref.at[pl.ds(pl.multiple_of(d * r, 8), r), :]
        swaps = [pltpu.make_async_remote_copy(
            src_ref=block(2 * j + 1 - c), dst_ref=got.at[j], send_sem=send_sems.at[j], recv_sem=recv_sems.at[j],
            device_id=(x, y, 1 - c), device_id_type=MESH) for j in range(4)]
        keeps = [pltpu.make_async_copy(block(2 * j + c), mine.at[j], own_sems.at[j]) for j in range(4)]
        for cp in swaps + keeps:
            cp.start()
        for j in range(4):
            keeps[j].wait()
            swaps[j].wait()
            o_ref[pl.ds(j * r, r), :] = (mine[j].astype(F32) + got[j].astype(F32)).astype(o_ref.dtype)

    return pl.pallas_call(
        body, name=name, in_specs=[ANY_SPEC], out_specs=VMEM_SPEC,
        out_shape=jax.ShapeDtypeStruct((4 * r, cols), part.dtype),
        scratch_shapes=[pltpu.VMEM((4, r, cols), part.dtype), pltpu.VMEM((4, r, cols), part.dtype),
                        pltpu.SemaphoreType.DMA((4,)), pltpu.SemaphoreType.DMA((4,)), pltpu.SemaphoreType.DMA((4,))],
        compiler_params=pltpu.CompilerParams(vmem_limit_bytes=32 << 20),
    )(part)


def _sum_partials(parts, name, blocks=N_DEV):
    n = len(parts)

    def body(*refs):
        ins, outs = refs[:n], refs[n:]
        first = pl.program_id(0) == 0
        for i_ref, o_ref in zip(ins, outs):
            v = i_ref[...].astype(F32)

            @pl.when(first)
            def _():
                o_ref[...] = v

            @pl.when(jnp.logical_not(first))
            def _():
                o_ref[...] += v

    shapes = [(p.shape[0] // blocks, p.shape[1]) for p in parts]
    return pl.pallas_call(
        body, name=name, grid=(blocks,),
        in_specs=[pl.BlockSpec(s, lambda j: (j, 0)) for s in shapes],
        out_specs=[pl.BlockSpec(s, lambda j: (0, 0)) for s in shapes],
        out_shape=[jax.ShapeDtypeStruct(s, F32) for s in shapes],
        compiler_params=_params(("arbitrary",)),
    )(*parts)


def _adamw_update(w, g, m, v):
    m = ADAM_B1 * m + (1.0 - ADAM_B1) * g
    v = ADAM_B2 * v + (1.0 - ADAM_B2) * (g * g)
    m_hat = m * (1.0 / (1.0 - ADAM_B1 ** ADAM_STEP))
    v_hat = v * (1.0 / (1.0 - ADAM_B2 ** ADAM_STEP))
    return -ADAM_LR * (m_hat / (jnp.sqrt(v_hat) + ADAM_EPS) + ADAM_WD * w), m, v


def _sum_adamw(parts, w, m, v, blocks, name, transposed):
    shape = (w.shape[1], w.shape[0]) if transposed else w.shape

    def body(p_ref, w_ref, m_ref, v_ref, g_ref, d_ref, mo_ref, vo_ref):
        j = pl.program_id(0)
        part = p_ref[...].astype(F32)

        @pl.when(j == 0)
        def _():
            g_ref[...] = part

        @pl.when(j > 0)
        def _():
            g_ref[...] += part

        @pl.when(j == blocks - 1)
        def _():
            slab = (lambda ref: ref[...].T) if transposed else (lambda ref: ref[...])
            d_ref[...], mo_ref[...], vo_ref[...] = _adamw_update(slab(w_ref), g_ref[...], slab(m_ref), slab(v_ref))

    held = pl.BlockSpec(shape, lambda j: (0, 0))
    given = pl.BlockSpec(w.shape, lambda j: (0, 0))
    return pl.pallas_call(
        body, name=name, grid=(blocks,),
        in_specs=[pl.BlockSpec(shape, lambda j: (j, 0)), given, given, given],
        out_specs=[held] * 4, out_shape=[jax.ShapeDtypeStruct(shape, F32)] * 4,
        compiler_params=_params(("arbitrary",)),
    )(parts, w, m, v)


def _adamw(ws, gs, ms, vs, name):
    n = len(ws)

    def body(*refs):
        w_r, g_r, m_r, v_r = refs[:n], refs[n:2 * n], refs[2 * n:3 * n], refs[3 * n:4 * n]
        d_o, m_o, v_o = refs[4 * n:5 * n], refs[5 * n:6 * n], refs[6 * n:7 * n]
        for k in range(n):
            d_o[k][...], m_o[k][...], v_o[k][...] = _adamw_update(w_r[k][...], g_r[k][...], m_r[k][...], v_r[k][...])

    shapes = [jax.ShapeDtypeStruct(w.shape, F32) for w in ws]
    outs = pl.pallas_call(
        body, name=name, in_specs=[VMEM_SPEC] * (4 * n), out_specs=[VMEM_SPEC] * (3 * n), out_shape=shapes * 3,
        compiler_params=pltpu.CompilerParams(vmem_limit_bytes=56 << 20),
    )(*ws, *gs, *ms, *vs)
    return outs[:n], outs[n:2 * n], outs[2 * n:]


WEIGHT_NAMES = ("meta_tokens", "ffn1_pre_norm", "ffn1_w_gate", "ffn1_w_up", "ffn1_w_down", "ffn1_post_norm", "mix_pre_norm",
                "w_in", "gla_w_a2", "gla_b_a", "gla_out_norm", "swa_sinks", "swa_out_norm", "w_out", "mix_post_norm",
                "ffn2_pre_norm", "ffn2_w_gate", "ffn2_w_up", "ffn2_w_down", "ffn2_post_norm")
WIN_SHARD = D_IN // N_DEV
WIN_SHARD_PAD = 304
SLAB_VECTORS = ("ffn1_pre", "ffn1_post", "mix_pre", "mix_post", "ffn2_pre", "ffn2_post")
SLAB_ROWS = 32


def kernel(x, meta_tokens, ffn1_pre_norm, ffn1_w_gate, ffn1_w_up, ffn1_w_down, ffn1_post_norm, mix_pre_norm, w_in, gla_w_a2, gla_b_a, gla_out_norm, swa_sinks, swa_out_norm, w_out, mix_post_norm, ffn2_pre_norm, ffn2_w_gate, ffn2_w_up, ffn2_w_down, ffn2_post_norm, loss_target, m_meta_tokens, m_ffn1_pre_norm, m_ffn1_w_gate, m_ffn1_w_up, m_ffn1_w_down, m_ffn1_post_norm, m_mix_pre_norm, m_w_in, m_gla_w_a2, m_gla_b_a, m_gla_out_norm, m_swa_sinks, m_swa_out_norm, m_w_out, m_mix_post_norm, m_ffn2_pre_norm, m_ffn2_w_gate, m_ffn2_w_up, m_ffn2_w_down, m_ffn2_post_norm, v_meta_tokens, v_ffn1_pre_norm, v_ffn1_w_gate, v_ffn1_w_up, v_ffn1_w_down, v_ffn1_post_norm, v_mix_pre_norm, v_w_in, v_gla_w_a2, v_gla_b_a, v_gla_out_norm, v_swa_sinks, v_swa_out_norm, v_w_out, v_mix_post_norm, v_ffn2_pre_norm, v_ffn2_w_gate, v_ffn2_w_up, v_ffn2_w_down, v_ffn2_post_norm):
    given = dict(locals())
    W = {n: given[n] for n in WEIGHT_NAMES}
    M = {n: given["m_" + n] for n in WEIGHT_NAMES}
    V = {n: given["v_" + n] for n in WEIGHT_NAMES}
    dev = _dev_index(*_place_on_mesh())

    def t16(w):
        return w[0].T.astype(BF16)

    small = jnp.concatenate([W["meta_tokens"], jnp.pad(W["gla_w_a2"][0], ((0, 0), (0, 96)))], axis=0)
    wg1, wu1, wd1, small_g, gathered_zeros = _all_gather(
        [t16(W["ffn1_w_gate"]), t16(W["ffn1_w_up"]), W["ffn1_w_down"][0].astype(BF16), small])
    def after_zero(shard, zeros):
        return shard + zeros[0:1, 0:1].astype(shard.dtype)
    win_shard = jnp.pad(t16(W["w_in"]), ((0, WIN_SHARD_PAD - WIN_SHARD), (0, 0)))
    win_shard = after_zero(win_shard, gathered_zeros)
    mid = _exchange_start([win_shard], GATHER, "gather_w_in_start")
    late_shards = [after_zero(W["w_out"][0].astype(BF16), mid[3]), t16(W["ffn2_w_gate"]), t16(W["ffn2_w_up"]),
                   W["ffn2_w_down"][0].astype(BF16)]
    late = _exchange_start(late_shards, GATHER, "gather_late_weights_start")

    def late_weights(what, after):
        if what == "win":
            win_g, = _exchange_wait(mid, GATHER, after, "gather_w_in_wait")
            win_t = win_g.reshape(N_DEV, WIN_SHARD_PAD, D_MODEL)[:, :WIN_SHARD].reshape(D_IN, D_MODEL)
            return dict(win=_win_pad_rows(win_t))
        wout, wg2, wu2, wd2 = _exchange_wait(late, GATHER, after, "gather_late_weights_wait")
        return dict(wout=wout, wg2=wg2, wu2=wu2, wd2=wd2)

    small_g = small_g.reshape(N_DEV, 32, 128)
    meta_full = small_g[:, :N_META].transpose(1, 0, 2).reshape(N_META, D_MODEL)
    wa2_full = small_g[:, N_META:, :32].transpose(1, 0, 2).reshape(16, 256)
    w = dict(
        ffn1_pre=W["ffn1_pre_norm"] + late[3][0, 0], ffn1_post=W["ffn1_post_norm"], mix_pre=W["mix_pre_norm"],
        mix_post=W["mix_post_norm"], ffn2_pre=W["ffn2_pre_norm"], ffn2_post=W["ffn2_post_norm"], b_a=W["gla_b_a"],
        gla_norm=W["gla_out_norm"], sinks=W["swa_sinks"], swa_norm=W["swa_out_norm"], wg1=wg1, wu1=wu1, wd1=wd1,
        wa2=jnp.pad(wa2_full, ((0, 112), (0, 0))))

    in_flight = []

    def on_grads(group, grads):
        parts = []
        for nm, p in grads.items():
            if nm == "win":
                p = _win_unpad_rows(p).reshape(N_DEV, WIN_SHARD, D_MODEL)
                p = jnp.pad(p, ((0, 0), (0, WIN_SHARD_PAD - WIN_SHARD), (0, 0))).reshape(N_DEV * WIN_SHARD_PAD, D_MODEL)
            parts.append(p)
        kind = SCATTER if group == "ffn2" else SCATTER_CHIPS
        if kind == SCATTER_CHIPS:
            parts = [_sibling_reduce(p, "pair_" + group + "_" + nm) for nm, p in zip(grads, parts)]
        started = _exchange_start(parts, kind, "scatter_" + group + "_start")
        in_flight.append((group, list(grads), started, kind))
        return started[3]

    small_flight = []

    def on_small(loss, dh0, g):
        packed = jnp.concatenate([g["b_a"][0:1], g["gla_norm"][0:1], g["sinks"][0:1], g["swa_norm"][0:1]], axis=1)
        slab = jnp.concatenate([g[k][0:1] for k in SLAB_VECTORS] + [packed, jnp.full((1, D_MODEL), loss, F32),
                               g["wa2"][:16].reshape(4, D_MODEL), jnp.zeros((4, D_MODEL), F32), dh0[PAD_ROWS:BLK]], axis=0)
        small_flight.append(_exchange_start([slab], GATHER, "gather_small_grads_start"))
        return small_flight[0][3]

    front = jnp.concatenate([jnp.zeros((PAD_ROWS, D_MODEL), F32), meta_full], axis=0)
    loss, dh0, g = _local_step(x[0], loss_target[0], front, w, late_weights, on_grads, on_small)
    grad_x = dh0[BLK:][None]

    land, = _exchange_wait(small_flight[0], GATHER, in_flight[-1][2][3], "gather_small_grads_wait")
    tot = _sum_partials([land], "sum_small_grads")[0]
    loss = tot[7, 0]
    small_grads = dict(
        ffn1_pre_norm=tot[0:1], ffn1_post_norm=tot[1:2], mix_pre_norm=tot[2:3], mix_post_norm=tot[3:4],
        ffn2_pre_norm=tot[4:5], ffn2_post_norm=tot[5:6], gla_b_a=tot[6:7, 0:256], gla_out_norm=tot[6:7, 256:384],
        swa_sinks=tot[6:7, 384:392], swa_out_norm=tot[6:7, 512:1024],
        gla_w_a2=lax.dynamic_slice_in_dim(tot[8:12].reshape(16, 256), dev * 32, 32, axis=1)[None],
        meta_tokens=lax.dynamic_slice_in_dim(tot[16:32], dev * 128, 128, axis=1))

    big = dict(wg1=("ffn1_w_gate", True), wu1=("ffn1_w_up", True), wd1=("ffn1_w_down", False), win=("w_in", True),
               wout=("w_out", False), wg2=("ffn2_w_gate", True), wu2=("ffn2_w_up", True), wd2=("ffn2_w_down", False))
    grads = dict(small_grads)
    delta, new_m, new_v = {}, {}, {}
    names = [n for n in WEIGHT_NAMES if n not in [full for full, _ in big.values()]]
    two_d = lambda a: a.reshape(-1, a.shape[-1])
    d_, m_, v_ = _adamw([two_d(W[n]) for n in names], [two_d(grads[n]) for n in names],
                        [two_d(M[n]) for n in names], [two_d(V[n]) for n in names], "adamw_small")
    for k, n in enumerate(names):
        delta[n], new_m[n], new_v[n] = d_[k].reshape(W[n].shape), m_[k].reshape(W[n].shape), v_[k].reshape(W[n].shape)

    before_wait = d_[0] + in_flight[-1][2][3][0, 0]
    for group, shorts, started, kind in in_flight:
        lands = _exchange_wait(started, kind, before_wait, "scatter_" + group + "_wait")
        blocks = 4 if kind == SCATTER_CHIPS else N_DEV
        for short, land in zip(shorts, lands):
            n, transposed = big[short]
            to_slab = (lambda a: a[0].T) if transposed else (lambda a: a[0])
            from_slab = (lambda a: a.T[None]) if transposed else (lambda a: a[None])
            if short == "win":
                g_slab = _sum_partials([land], "sum_" + n, blocks)[0][:WIN_SHARD]
                d_, m_, v_ = _adamw([to_slab(W[n])], [g_slab], [to_slab(M[n])], [to_slab(V[n])], "adamw_" + n)
                d_, m_, v_ = d_[0], m_[0], v_[0]
            else:
                g_slab, d_, m_, v_ = _sum_adamw(land, W[n][0], M[n][0], V[n][0], blocks, "adamw_" + n, transposed)
            grads[n], delta[n], new_m[n], new_v[n] = from_slab(g_slab), from_slab(d_), from_slab(m_), from_slab(v_)
            before_wait = d_
    return (loss, grad_x, *[grads[n] for n in WEIGHT_NAMES], *[delta[n] for n in WEIGHT_NAMES],
            *[new_m[n] for n in WEIGHT_NAMES], *[new_v[n] for n in WEIGHT_NAMES])
```

```python
import math

import jax
import jax.numpy as jnp
from jax import lax
from jax.experimental import pallas as pl
from jax.experimental.pallas import tpu as pltpu

F32, BF16 = jnp.float32, jnp.bfloat16

D_MODEL = 1024
D_FF = 2816
N_META = 16
BLK = 128
PAD_ROWS = BLK - N_META
GLA_DK = 64
SWA_HD = 64
SWA_HEADS = 8
GLA_TAU = 16.0
NORM_EPS = 1e-6
NEG_INF = -1e30
ROPE_THETA = 10000.0
P_GQ, P_GK, P_GV, P_GG, P_SQ, P_SK, P_SV, P_GA, P_END = 0, 256, 512, 1024, 1536, 2048, 2176, 2304, 2432
D_IN = 2320
IN_SPLITS = (256, 256, 512, 512, 16, 512, 128, 128)
FF_TILE = 2816
WGRAD_TILE_MAX = 2432
N_DEV = 8
MESH = pl.DeviceIdType.MESH

ADAM_LR, ADAM_B1, ADAM_B2, ADAM_EPS, ADAM_WD, ADAM_STEP = 0.001, 0.9, 0.999, 1e-08, 0.01, 10

V7X_VMEM_BYTES = 64 << 20
VMEM_SPEC = pl.BlockSpec(memory_space=pltpu.VMEM)
SMEM_SPEC = pl.BlockSpec(memory_space=pltpu.SMEM)
ANY_SPEC = pl.BlockSpec(memory_space=pl.ANY)


def _params(semantics, vmem_mb=56):
    return pltpu.CompilerParams(dimension_semantics=semantics, vmem_limit_bytes=vmem_mb << 20)


def _row_tile(rows):
    return 416 if rows % 416 == 0 else BLK


def _blocks_per_step(blocks):
    return 5 if blocks % 5 == 0 else 1


def _nn(a, b):
    return lax.dot_general(a, b, (((1,), (0,)), ((), ())), preferred_element_type=F32)


def _nt(a, b):
    return lax.dot_general(a, b, (((1,), (1,)), ((), ())), preferred_element_type=F32)


def _tn(a, b):
    return lax.dot_general(a, b, (((0,), (0,)), ((), ())), preferred_element_type=F32)


def _rms(x):
    r = lax.rsqrt(jnp.mean(x * x, axis=-1, keepdims=True) + NORM_EPS)
    return x * r, r


def _rms_bwd(xn, r, w, dy):
    g = dy * w
    return r * (g - xn * jnp.mean(g * xn, axis=-1, keepdims=True))


def _sigmoid(x):
    return 1.0 / (1.0 + jnp.exp(-x))


def _colsum(x):
    return jnp.sum(x, axis=0, keepdims=True)


def _split_bf16(x):
    hi = x.astype(BF16)
    lo = (x - hi.astype(F32)).astype(BF16)
    return hi, lo


def _tri(lower):
    r = lax.broadcasted_iota(jnp.int32, (BLK, BLK), 0)
    c = lax.broadcasted_iota(jnp.int32, (BLK, BLK), 1)
    return (r >= c) if lower else (c >= r)


def _half_mask(width, half):
    lane = lax.broadcasted_iota(jnp.int32, (1, width), 1)
    return ((lane % 128) < 64) if half == 0 else ((lane % 128) >= 64)


def _rot_half(x):
    w = x.shape[-1]
    lane = lax.broadcasted_iota(jnp.int32, (1, w), 1)
    return jnp.where((lane % SWA_HD) < SWA_HD // 2, -pltpu.roll(x, w - SWA_HD // 2, 1), pltpu.roll(x, SWA_HD // 2, 1))


def _row_spec(tm, cols):
    return pl.BlockSpec((tm, cols), lambda i: (i, 0))


def _acc_spec(cols):
    return pl.BlockSpec((8, cols), lambda i: (0, 0))


def _acc_add(ref, first, value):
    @pl.when(first)
    def _():
        ref[...] = jnp.zeros_like(ref)
    ref[0:1, :] += value


def _behind_spec(tm):
    return pl.BlockSpec((pl.Element(tm), pl.Element(D_MODEL)),
                        lambda i: (pl.multiple_of(jnp.maximum(i * tm - BLK, 0), math.gcd(tm, BLK)), 0))


def _behind_front(ref, i, tm, front):
    blk = ref[...]
    return jnp.where(i == 0, jnp.concatenate([front, blk[0:tm - BLK]], axis=0), blk)


def _ffn_fwd(h, gpre, wg_t, wu_t, wd, gpost, tgt=None, front=None):
    with_loss, with_front = tgt is not None, front is not None
    rows = h.shape[0] + (BLK if with_front else 0)
    tm = _row_tile(rows)
    nf = D_FF // FF_TILE

    def body(*refs):
        refs = list(refs)
        h_ref, gpre_ref, wg_ref, wu_ref, wd_ref, gpost_ref = refs[:6]
        del refs[:6]
        front_ref = refs.pop(0) if with_front else None
        t_ref = refs.pop(0) if with_loss else None
        ho_ref, a_ref, b_ref, s_ref, f_ref = refs[:5]
        dy_ref, loss_ref = refs[5:7] if with_loss else (None, None)
        acc = refs[-1]
        i = pl.program_id(0)
        h_in = _behind_front(h_ref, i, tm, front_ref[...]) if with_front else h_ref[...]
        hn, _ = _rms(h_in)
        n16 = (hn * gpre_ref[...]).astype(BF16)
        for j in range(nf):
            cols = slice(j * FF_TILE, (j + 1) * FF_TILE)
            a = _nt(n16, wg_ref[cols, :])
            b = _nt(n16, wu_ref[cols, :])
            a_ref[:, cols] = a.astype(BF16)
            b_ref[:, cols] = b.astype(BF16)
            s16 = (a * _sigmoid(a) * b).astype(BF16)
            s_ref[:, cols] = s16
            part = _nn(s16, wd_ref[cols, :])
            if j == 0:
                acc[...] = part
            else:
                acc[...] += part
        f = acc[...]
        f_ref[...] = f
        fn, _ = _rms(f)
        y = h_in + 0.5 * (fn * gpost_ref[...])
        ho_ref[...] = y
        if with_loss:
            row = i * tm + lax.broadcasted_iota(jnp.int32, (tm, 1), 0)
            err = jnp.where(row >= BLK, y - _behind_front(t_ref, i, tm, jnp.zeros((BLK, D_MODEL), F32)), 0.0)
            dy_ref[...] = err * (1.0 / D_MODEL)
            part = 0.5 * jnp.sum(jnp.sum(err * err, axis=-1, keepdims=True) * (1.0 / D_MODEL), axis=0, keepdims=True)

            @pl.when(i == 0)
            def _():
                loss_ref[...] = jnp.zeros_like(loss_ref)
            loss_ref[...] += part

    row_f32 = _row_spec(tm, D_MODEL)
    behind = _behind_spec(tm)
    in_specs = [behind if with_front else row_f32, VMEM_SPEC, VMEM_SPEC, VMEM_SPEC, VMEM_SPEC, VMEM_SPEC]
    out_specs = [row_f32, _row_spec(tm, D_FF), _row_spec(tm, D_FF), _row_spec(tm, D_FF), row_f32]
    out_shape = [jax.ShapeDtypeStruct((rows, D_MODEL), F32), jax.ShapeDtypeStruct((rows, D_FF), BF16),
                 jax.ShapeDtypeStruct((rows, D_FF), BF16), jax.ShapeDtypeStruct((rows, D_FF), BF16),
                 jax.ShapeDtypeStruct((rows, D_MODEL), F32)]
    args = [h, gpre, wg_t, wu_t, wd, gpost]
    if with_front:
        in_specs.append(VMEM_SPEC)
        args.append(front)
    if with_loss:
        in_specs.append(behind)
        args.append(tgt)
        out_specs += [row_f32, pl.BlockSpec((8, 128), lambda i: (0, 0))]
        out_shape += [jax.ShapeDtypeStruct((rows, D_MODEL), F32), jax.ShapeDtypeStruct((8, 128), F32)]
    return pl.pallas_call(
        body, name="ffn_fwd_loss" if with_loss else "ffn_fwd", grid=(rows // tm,),
        in_specs=in_specs, out_specs=out_specs, out_shape=out_shape,
        scratch_shapes=[pltpu.VMEM((tm, D_MODEL), F32)],
        compiler_params=_params(("arbitrary",)),
    )(*args)


def _ffn_bwd_act(dh_out, h, a, b, f, gpre, gpost, wg_t, wu_t, wd, name, front=None):
    with_front = front is not None
    rows = dh_out.shape[0]
    tm = _row_tile(rows)
    nf = D_FF // FF_TILE

    def body(dho_ref, h_ref, a_ref, b_ref, f_ref, gpre_ref, gpost_ref, wg_ref, wu_ref, wd_ref, *rest):
        front_ref = rest[0] if with_front else None
        dh_ref, da_ref, db_ref, df_ref, n_ref, dgpre_ref, dgpost_ref, acc = rest[-8:]
        first = pl.program_id(0) == 0
        dho = dho_ref[...]
        drr = 0.5 * dho
        fn, rf = _rms(f_ref[...])
        _acc_add(dgpost_ref, first, _colsum(drr * fn))
        df16 = _rms_bwd(fn, rf, gpost_ref[...], drr).astype(BF16)
        df_ref[...] = df16
        h_in = _behind_front(h_ref, pl.program_id(0), tm, front_ref[...]) if with_front else h_ref[...]
        hn, rh = _rms(h_in)
        n_ref[...] = (hn * gpre_ref[...]).astype(BF16)
        for j in range(nf):
            cols = slice(j * FF_TILE, (j + 1) * FF_TILE)
            ds = _nt(df16, wd_ref[cols, :])
            av = a_ref[:, cols].astype(F32)
            bv = b_ref[:, cols].astype(F32)
            sg = _sigmoid(av)
            db16 = (ds * (av * sg)).astype(BF16)
            da16 = (ds * bv * (sg * (1.0 + av * (1.0 - sg)))).astype(BF16)
            da_ref[:, cols] = da16
            db_ref[:, cols] = db16
            part = _nn(da16, wg_ref[cols, :]) + _nn(db16, wu_ref[cols, :])
            if j == 0:
                acc[...] = part
            else:
                acc[...] += part
        dn = acc[...]
        _acc_add(dgpre_ref, first, _colsum(dn * hn))
        dh_ref[...] = dho + _rms_bwd(hn, rh, gpre_ref[...], dn)

    row_f32 = _row_spec(tm, D_MODEL)
    row_ff = _row_spec(tm, D_FF)
    return pl.pallas_call(
        body, name=name, grid=(rows // tm,),
        in_specs=[row_f32, _behind_spec(tm) if with_front else row_f32, row_ff, row_ff, row_f32,
                  VMEM_SPEC, VMEM_SPEC, VMEM_SPEC, VMEM_SPEC, VMEM_SPEC] + ([VMEM_SPEC] if with_front else []),
        out_specs=[row_f32, row_ff, row_ff, row_f32, row_f32, _acc_spec(D_MODEL), _acc_spec(D_MODEL)],
        out_shape=[jax.ShapeDtypeStruct((rows, D_MODEL), F32), jax.ShapeDtypeStruct((rows, D_FF), BF16),
                   jax.ShapeDtypeStruct((rows, D_FF), BF16), jax.ShapeDtypeStruct((rows, D_MODEL), BF16),
                   jax.ShapeDtypeStruct((rows, D_MODEL), BF16), jax.ShapeDtypeStruct((8, D_MODEL), F32),
                   jax.ShapeDtypeStruct((8, D_MODEL), F32)],
        scratch_shapes=[pltpu.VMEM((tm, D_MODEL), F32)],
        compiler_params=_params(("arbitrary",), vmem_mb=62),
    )(dh_out, h, a, b, f, gpre, gpost, wg_t, wu_t, wd, *([front] if with_front else []))


def _wgrad(lhs, rhs, name, after=None):
    rows, width = lhs.shape
    tm = rows if rows % 1664 == 0 else BLK
    tf = 256 if width % 256 == 0 else 128
    nr = rows // tm

    def body(l_ref, r_ref, *rest):
        o_ref, acc = rest[-2:]
        i = pl.program_id(1)
        part = _tn(l_ref[...], r_ref[...])

        @pl.when(i == 0)
        def _():
            acc[...] = part

        @pl.when(i > 0)
        def _():
            acc[...] += part

        @pl.when(i == nr - 1)
        def _():
            o_ref[...] = acc[...].astype(BF16)

    l_spec = pl.BlockSpec((tm, tf), lambda j, i: (i, j))
    r_spec = pl.BlockSpec((tm, D_MODEL), lambda j, i: (i, 0))
    return pl.pallas_call(
        body, name=name, grid=(width // tf, nr),
        in_specs=[l_spec, r_spec] + ([] if after is None else [ANY_SPEC]),
        out_specs=pl.BlockSpec((tf, D_MODEL), lambda j, i: (j, 0)),
        out_shape=jax.ShapeDtypeStruct((width, D_MODEL), BF16),
        scratch_shapes=[pltpu.VMEM((tf, D_MODEL), F32)],
        compiler_params=_params(("arbitrary", "arbitrary")),
    )(lhs, rhs, *([] if after is None else [after]))


def _chunk_cumsum(x, lower):
    tri = jnp.where(_tri(lower), 1.0, 0.0).astype(BF16)
    hi, lo = _split_bf16(x)
    return _nn(tri, hi) + _nn(tri, lo)


def _mix_in(h, g, win_p, wa2_p, b_a, cos, sin):
    rows = h.shape[0]
    tm = 640 if rows % 640 == 0 else BLK

    def body(h_ref, g_ref, win_ref, wa2_ref, ba_ref, cos_ref, sin_ref,
             gq_ref, gk_ref, gv_ref, gg_ref, sq_ref, sk_ref, sv_ref, ga_ref, loga_ref, bc_ref, n_ref):
        hn, _ = _rms(h_ref[...])
        n16 = (hn * g_ref[...]).astype(BF16)
        n_ref[...] = n16
        proj = _nt(n16, win_ref[...])
        gq_ref[...] = proj[:, P_GQ:P_GK]
        gk_ref[...] = proj[:, P_GK:P_GV]
        gv_ref[...] = proj[:, P_GV:P_GG].astype(BF16)
        gg_ref[...] = proj[:, P_GG:P_SQ]
        c1, s1 = cos_ref[...], sin_ref[...]
        c4 = jnp.concatenate([c1, c1, c1, c1], axis=1)
        s4 = jnp.concatenate([s1, s1, s1, s1], axis=1)
        sq = proj[:, P_SQ:P_SK]
        sk = proj[:, P_SK:P_SV]
        sq_ref[...] = (sq * c4 + _rot_half(sq) * s4).astype(BF16)
        sk_ref[...] = (sk * c1 + _rot_half(sk) * s1).astype(BF16)
        sv_ref[...] = proj[:, P_SV:P_GA].astype(BF16)
        ga = proj[:, P_GA:P_END]
        ga_ref[...] = ga
        z = _nn(ga, wa2_ref[...]) + ba_ref[...]
        loga = (jnp.minimum(z, 0.0) - jnp.log(1.0 + jnp.exp(-jnp.abs(z)))) * (1.0 / GLA_TAU)
        loga_ref[...] = loga
        for c in range(tm // BLK):
            rs = slice(c * BLK, (c + 1) * BLK)
            bc_ref[rs, :] = _chunk_cumsum(loga[rs, :], True)

    f32 = lambda c: jax.ShapeDtypeStruct((rows, c), F32)
    b16 = lambda c: jax.ShapeDtypeStruct((rows, c), BF16)
    rs = lambda c: _row_spec(tm, c)
    return pl.pallas_call(
        body, name="mix_in", grid=(rows // tm,),
        in_specs=[rs(D_MODEL), VMEM_SPEC, VMEM_SPEC, VMEM_SPEC, VMEM_SPEC, rs(128), rs(128)],
        out_specs=[rs(256), rs(256), rs(512), rs(512), rs(512), rs(128), rs(128), rs(128), rs(256), rs(256), rs(D_MODEL)],
        out_shape=[f32(256), f32(256), b16(512), f32(512), b16(512), b16(128), b16(128), f32(128), f32(256), f32(256),
                   b16(D_MODEL)],
        compiler_params=_params(("arbitrary",)),
    )(h, g, win_p, wa2_p, b_a, cos, sin)


def _gla_factors(q, k, bc):
    bm = bc[BLK // 2 - 1:BLK // 2, :]
    bl = bc[BLK - 1:BLK, :]
    e_q, e_k, e_qe, e_kd = jnp.exp(bc - bm), jnp.exp(bm - bc), jnp.exp(bc), jnp.exp(bl - bc)
    return (q * e_q, k * e_k, q * e_qe, k * e_kd), (e_q, e_k, e_qe, e_kd), jnp.exp(bl)


def _gla_fwd(gq, gk, gv, gg, bc, wgn):
    rows = gq.shape[0]
    nc = rows // BLK
    per_step = _blocks_per_step(nc)
    scale = GLA_DK ** -0.5

    def body(q_ref, k_ref, v_ref, gg_ref, bc_ref, wgn_ref, o_ref, cat_ref, sp_ref, st):
        @pl.when(pl.program_id(0) == 0)
        def _():
            st[...] = jnp.zeros_like(st)
        low = _tri(True)
        wgn_v = wgn_ref[...]

        def chunk(c, carry):
            rr = pl.ds(pl.multiple_of(c * BLK, BLK), BLK)
            for p in range(2):
                sl = slice(128 * p, 128 * p + 128)
                (qt, kt, qe, kd), _, ebl = _gla_factors(q_ref[rr, sl] * scale, k_ref[rr, sl], bc_ref[rr, sl])
                s_prev = st[p]
                sp_ref[c, p] = s_prev
                s16 = s_prev.astype(BF16)
                qt16 = qt.astype(BF16)
                s_new = s_prev * ebl
                for hh in range(2):
                    hs = slice(128 * (2 * p + hh), 128 * (2 * p + hh) + 128)
                    lm = _half_mask(128, hh)
                    vh = v_ref[rr, hs]
                    pm = jnp.where(low, _nt(qt16, jnp.where(lm, kt, 0.0).astype(BF16)), 0.0)
                    o = _nn(pm.astype(BF16), vh) + _nt(jnp.where(lm, qe, 0.0).astype(BF16), s16)
                    s_new = s_new + _tn(vh, jnp.where(lm, kd, 0.0).astype(BF16))
                    o_ref[rr, hs] = o
                    on, _ = _rms(o)
                    gate = gg_ref[rr, hs]
                    cat_ref[rr, hs] = (on * wgn_v * (gate * _sigmoid(gate))).astype(BF16)
                st[p] = s_new
            return carry
        lax.fori_loop(0, per_step, chunk, 0)

    rs = lambda c: _row_spec(per_step * BLK, c)
    return pl.pallas_call(
        body, name="gla_fwd", grid=(nc // per_step,),
        in_specs=[rs(256), rs(256), rs(512), rs(512), rs(256), VMEM_SPEC],
        out_specs=[rs(512), rs(512), pl.BlockSpec((per_step, 2, 128, 128), lambda i: (i, 0, 0, 0))],
        out_shape=[jax.ShapeDtypeStruct((rows, 512), F32), jax.ShapeDtypeStruct((rows, 512), BF16),
                   jax.ShapeDtypeStruct((nc, 2, 128, 128), F32)],
        scratch_shapes=[pltpu.VMEM((2, 128, 128), F32)],
        compiler_params=_params(("arbitrary",)),
    )(gq, gk, gv, gg, bc, wgn)


def _gla_bwd(dcat, o_all, gq, gk, gv, gg, bc, sp, wgn):
    rows = gq.shape[0]
    nc = rows // BLK
    per_step = _blocks_per_step(nc)
    steps = nc // per_step
    scale = GLA_DK ** -0.5

    def body(dc_ref, o_ref, q_ref, k_ref, v_ref, gg_ref, bc_ref, sp_ref, wgn_ref,
             dq_ref, dk_ref, dv_ref, dgg_ref, dla_ref, dwgn_ref, dst):
        first = pl.program_id(0) == 0

        @pl.when(first)
        def _():
            dst[...] = jnp.zeros_like(dst)
        low, upp = _tri(True), _tri(False)
        last_row = lax.broadcasted_iota(jnp.int32, (BLK, 1), 0) == BLK - 1
        wgn_v = wgn_ref[...]

        def chunk(c, dwgn):
            rr = pl.ds(pl.multiple_of((per_step - 1 - c) * BLK, BLK), BLK)
            for p in range(2):
                sl = slice(128 * p, 128 * p + 128)
                (qt, kt, qe, kd), (e_q, e_k, e_qe, e_kd), ebl = _gla_factors(
                    q_ref[rr, sl] * scale, k_ref[rr, sl], bc_ref[rr, sl])
                s_prev = sp_ref[per_step - 1 - c, p]
                s16 = s_prev.astype(BF16)
                ds_next = dst[p]
                ds16 = ds_next.astype(BF16)
                qt16 = qt.astype(BF16)
                ds_new = ds_next * ebl
                dqt = jnp.zeros((BLK, 128), F32)
                dkt = jnp.zeros((BLK, 128), F32)
                dqe = jnp.zeros((BLK, 128), F32)
                dkd = jnp.zeros((BLK, 128), F32)
                for hh in range(2):
                    hs = slice(128 * (2 * p + hh), 128 * (2 * p + hh) + 128)
                    lm = _half_mask(128, hh)
                    on, ro = _rms(o_ref[rr, hs])
                    gate = gg_ref[rr, hs]
                    sg = _sigmoid(gate)
                    si = gate * sg
                    dog = dc_ref[rr, hs]
                    dwgn = dwgn + _colsum(dog * si * on)
                    dgg_ref[rr, hs] = dog * (on * wgn_v) * (sg * (1.0 + gate * (1.0 - sg)))
                    do16 = _rms_bwd(on, ro, wgn_v, dog * si).astype(BF16)
                    vh = v_ref[rr, hs]
                    ktm16 = jnp.where(lm, kt, 0.0).astype(BF16)
                    qtm16 = jnp.where(lm, qt, 0.0).astype(BF16)
                    qem16 = jnp.where(lm, qe, 0.0).astype(BF16)
                    kdm16 = jnp.where(lm, kd, 0.0).astype(BF16)
                    p_t = jnp.where(upp, _nt(ktm16, qt16), 0.0)
                    dp_t = jnp.where(upp, _nt(vh, do16), 0.0)
                    dp = jnp.where(low, _nt(do16, vh), 0.0)
                    dv_ref[rr, hs] = _nn(p_t.astype(BF16), do16) + _nt(kdm16, ds16)
                    dqt = dqt + _nn(dp.astype(BF16), ktm16)
                    dkt = dkt + _nn(dp_t.astype(BF16), qtm16)
                    dqe = dqe + jnp.where(lm, _nn(do16, s16), 0.0)
                    dkd = dkd + jnp.where(lm, _nn(vh, ds16), 0.0)
                    ds_new = ds_new + _tn(do16, qem16)
                debl = _colsum(ds_next * s_prev)
                dq_ref[rr, sl] = (dqt * e_q + dqe * e_qe) * scale
                dk_ref[rr, sl] = dkt * e_k + dkd * e_kd
                dkd_kd = dkd * kd
                db = dqt * qt - dkt * kt + dqe * qe - dkd_kd
                db = jnp.where(last_row, db + (_colsum(dkd_kd) + debl * ebl), db)
                dla_ref[rr, sl] = _chunk_cumsum(db, False)
                dst[p] = ds_new
            return dwgn
        dwgn = lax.fori_loop(0, per_step, chunk, jnp.zeros((1, 128), F32))
        _acc_add(dwgn_ref, first, dwgn)

    rev = lambda c: pl.BlockSpec((per_step * BLK, c), lambda i: (steps - 1 - i, 0))
    f32 = lambda c: jax.ShapeDtypeStruct((rows, c), F32)
    return pl.pallas_call(
        body, name="gla_bwd", grid=(steps,),
        in_specs=[rev(512), rev(512), rev(256), rev(256), rev(512), rev(512), rev(256),
                  pl.BlockSpec((per_step, 2, 128, 128), lambda i: (steps - 1 - i, 0, 0, 0)), VMEM_SPEC],
        out_specs=[rev(256), rev(256), rev(512), rev(512), rev(256), _acc_spec(128)],
        out_shape=[f32(256), f32(256), f32(512), f32(512), f32(256), jax.ShapeDtypeStruct((8, 128), F32)],
        scratch_shapes=[pltpu.VMEM((2, 128, 128), F32)],
        compiler_params=_params(("arbitrary",)),
    )(dcat, o_all, gq, gk, gv, gg, bc, sp, wgn)


def _swa_masks(i):
    t = lax.broadcasted_iota(jnp.int32, (BLK, BLK), 0)
    c = lax.broadcasted_iota(jnp.int32, (BLK, BLK), 1)
    own_side = c <= t
    band_ok = i >= jnp.where(own_side, 1, 2)
    meta_ok = (c % N_META) <= jnp.where(i >= 1, N_META, t - PAD_ROWS)
    return own_side, band_ok, meta_ok, c // N_META


def _swa_blocks(ref, i):
    prev = pl.multiple_of(jnp.maximum(i - 1, 0) * BLK, BLK)
    own = pl.multiple_of(i * BLK, BLK)
    return jnp.concatenate([ref[pl.ds(prev, BLK), :], ref[pl.ds(own, BLK), :]], axis=0), prev, own


def _swa_meta_operand(ref):
    blk = ref[0:BLK, :]
    swapped = pltpu.roll(blk, 64, 1)
    lo = jnp.where(_half_mask(128, 0), blk, swapped)
    hi = jnp.where(_half_mask(128, 1), blk, swapped)
    meta = jnp.concatenate([lo, lo, hi, hi], axis=1)[PAD_ROWS:BLK, :]
    tiled = jnp.concatenate([meta] * SWA_HEADS, axis=0)
    j = lax.broadcasted_iota(jnp.int32, tiled.shape, 0)
    lane = lax.broadcasted_iota(jnp.int32, tiled.shape, 1)
    return jnp.where(j // N_META == lane // SWA_HD, tiled, jnp.zeros_like(tiled))


def _swa_meta_fold(acc):
    out = jnp.zeros((N_META, 128), F32)
    for hd in range(SWA_HEADS):
        half, kv = hd % 2, hd // 4
        piece = acc[N_META * hd:N_META * (hd + 1), 128 * (hd // 2):128 * (hd // 2) + 128]
        piece = jnp.where(_half_mask(128, half), piece, 0.0)
        out = out + (piece if half == kv else pltpu.roll(piece, 64, 1))
    return out


def _by_head(group, per_head):
    out = jnp.zeros((BLK, BLK), F32)
    for hd, v in enumerate(per_head):
        out = jnp.where(group == hd, v, out)
    return out


def _place(x, kv):
    if kv == 0:
        lo = jnp.where(_half_mask(128, 0), x, jnp.zeros_like(x))
        return lo, pltpu.roll(lo, 64, 1)
    hi = jnp.where(_half_mask(128, 1), x, jnp.zeros_like(x))
    return pltpu.roll(hi, 64, 1), hi


def _swa_fwd(sq, sk, sv, sinks, wn):
    rows = sq.shape[0]
    nb = rows // BLK
    per_step = _blocks_per_step(nb)
    scale = SWA_HD ** -0.5

    def body(q_ref, k_ref, v_ref, sink_ref, wn_ref, o_ref, cat_ref, lse_ref, kp, vp):
        step = pl.program_id(0)

        @pl.when(step == 0)
        def _():
            kp[...] = _swa_meta_operand(k_ref)
            vp[...] = _swa_meta_operand(v_ref)

        def one_block(c, carry):
            i = step * per_step + c
            rr = pl.ds(pl.multiple_of(c * BLK, BLK), BLK)
            own_side, band_ok, meta_ok, group = _swa_masks(i)
            k2, _, _ = _swa_blocks(k_ref, i)
            v2, _, _ = _swa_blocks(v_ref, i)
            kz = (_place(k2, 0), _place(k2, 1))
            vz = (_place(v2, 0), _place(v2, 1))
            q_all = q_ref[rr, :]
            s_meta = jnp.where(meta_ok, _nt(q_all, kp[...]) * scale, NEG_INF)
            s_band, m = [], []
            for hd in range(SWA_HEADS):
                kv, half = hd // 4, hd % 2
                q_pair = q_all[:, 128 * (hd // 2):128 * (hd // 2) + 128]
                s2 = _nt(q_pair, kz[kv][half])
                s = jnp.where(band_ok, jnp.where(own_side, s2[:, BLK:], s2[:, :BLK]) * scale, NEG_INF)
                top = jnp.maximum(jnp.max(s, axis=-1, keepdims=True),
                                  jnp.max(jnp.where(group == hd, s_meta, NEG_INF), axis=-1, keepdims=True))
                s_band.append(s)
                m.append(jnp.maximum(top, sink_ref[0, hd]))
            e_meta = jnp.exp(s_meta - _by_head(group, m))
            o_meta = _nn(e_meta.astype(BF16), vp[...])
            outs = []
            for pr in range(4):
                o_pair = o_meta[:, 128 * pr:128 * pr + 128]
                rden = []
                for half in range(2):
                    hd = 2 * pr + half
                    kv = hd // 4
                    e = jnp.exp(s_band[hd] - m[hd])
                    den = (jnp.sum(e, axis=-1, keepdims=True)
                           + jnp.sum(jnp.where(group == hd, e_meta, 0.0), axis=-1, keepdims=True)
                           + jnp.exp(sink_ref[0, hd] - m[hd]))
                    lse_ref[rr, hd:hd + 1] = m[hd] + jnp.log(den)
                    rden.append(1.0 / den)
                    e2 = jnp.concatenate([jnp.where(own_side, 0.0, e), jnp.where(own_side, e, 0.0)], axis=1).astype(BF16)
                    o_pair = o_pair + _nn(e2, vz[kv][half])
                outs.append(o_pair * jnp.where(_half_mask(128, 0), rden[0], rden[1]))
            o = jnp.concatenate(outs, axis=1)
            o_ref[rr, :] = o
            on, _ = _rms(o)
            cat_ref[rr, :] = (on * wn_ref[...]).astype(BF16)
            return carry
        lax.fori_loop(0, per_step, one_block, 0)

    return pl.pallas_call(
        body, name="swa_fwd", grid=(nb // per_step,),
        in_specs=[_row_spec(per_step * BLK, 512), VMEM_SPEC, VMEM_SPEC, SMEM_SPEC, VMEM_SPEC],
        out_specs=[_row_spec(per_step * BLK, 512), _row_spec(per_step * BLK, 512), _row_spec(per_step * BLK, SWA_HEADS)],
        out_shape=[jax.ShapeDtypeStruct((rows, 512), F32), jax.ShapeDtypeStruct((rows, 512), BF16),
                   jax.ShapeDtypeStruct((rows, SWA_HEADS), F32)],
        scratch_shapes=[pltpu.VMEM((BLK, 512), BF16), pltpu.VMEM((BLK, 512), BF16)],
        compiler_params=_params(("arbitrary",)),
    )(sq, sk, sv, sinks, wn)


def _swa_bwd(dcat, o_all, sq, sk, sv, lse, sinks, wn):
    rows = sq.shape[0]
    nb = rows // BLK
    per_step = _blocks_per_step(nb)
    steps = nb // per_step
    scale = SWA_HD ** -0.5

    def body(dc_ref, o_ref, q_ref, k_ref, v_ref, lse_ref, sink_ref, wn_ref, dq_ref, dk_ref, dv_ref, dsink_ref, dwn_ref,
             kp, vp, dkp, dvp):
        step = pl.program_id(0)

        @pl.when(step == 0)
        def _():
            dk_ref[...] = jnp.zeros_like(dk_ref)
            dv_ref[...] = jnp.zeros_like(dv_ref)
            dkp[...] = jnp.zeros_like(dkp)
            dvp[...] = jnp.zeros_like(dvp)
            kp[...] = _swa_meta_operand(k_ref)
            vp[...] = _swa_meta_operand(v_ref)

        def one_block(c, carry):
            i = step * per_step + c
            rr = pl.ds(pl.multiple_of(c * BLK, BLK), BLK)
            first = i == 0
            own_side, band_ok, meta_ok, group = _swa_masks(i)
            k2, prev, own = _swa_blocks(k_ref, i)
            v2, _, _ = _swa_blocks(v_ref, i)
            kz = (_place(k2, 0), _place(k2, 1))
            vz = (_place(v2, 0), _place(v2, 1))
            o = o_ref[rr, :]
            on, ro = _rms(o)
            dc = dc_ref[rr, :]
            _acc_add(dwn_ref, first, _colsum(dc * on))
            do = _rms_bwd(on, ro, wn_ref[...], dc)
            do_o = do * o
            do16 = do.astype(BF16)
            q_all = q_ref[rr, :]
            lse = [lse_ref[rr, hd:hd + 1] for hd in range(SWA_HEADS)]
            delta = [jnp.sum(jnp.where(_half_mask(128, hd % 2), do_o[:, 128 * (hd // 2):128 * (hd // 2) + 128], 0.0),
                             axis=-1, keepdims=True) for hd in range(SWA_HEADS)]
            s_meta = jnp.where(meta_ok, _nt(q_all, kp[...]) * scale, NEG_INF)
            p_meta = jnp.exp(s_meta - _by_head(group, lse))
            ds_meta16 = (p_meta * (_nt(do16, vp[...]) - _by_head(group, delta)) * scale).astype(BF16)
            dq_meta = _nn(ds_meta16, kp[...])
            dkp[...] += _tn(ds_meta16, q_all)
            dvp[...] += _tn(p_meta.astype(BF16), do16)
            own2 = jnp.concatenate([own_side.astype(jnp.int32)] * 2, axis=0) > 0
            ok2 = jnp.concatenate([band_ok.astype(jnp.int32)] * 2, axis=0) > 0

            def window(x2):
                return jnp.where(own2, x2[:, BLK:], x2[:, :BLK])

            def unwindow(x):
                return jnp.concatenate([jnp.where(own2, 0.0, x), jnp.where(own2, x, 0.0)], axis=1).astype(BF16)
            lane8 = lax.broadcasted_iota(jnp.int32, (1, 128), 1)
            dsink = jnp.zeros((1, 128), F32)
            dq_pairs = [dq_meta[:, 128 * pr:128 * pr + 128] for pr in range(4)]
            dk2 = [[None, None], [None, None]]
            dv2 = [[None, None], [None, None]]
            for kv in range(2):
                for half in range(2):
                    heads, pairs = (4 * kv + half, 4 * kv + 2 + half), (2 * kv, 2 * kv + 1)
                    q_s = jnp.concatenate([q_all[:, 128 * pr:128 * pr + 128] for pr in pairs], axis=0)
                    do_s = jnp.concatenate([do16[:, 128 * pr:128 * pr + 128] for pr in pairs], axis=0)
                    lse_s = jnp.concatenate([lse[hd] for hd in heads], axis=0)
                    delta_s = jnp.concatenate([delta[hd] for hd in heads], axis=0)
                    s = jnp.where(ok2, window(_nt(q_s, kz[kv][half])) * scale, NEG_INF)
                    prob = jnp.exp(s - lse_s)
                    for hd in heads:
                        dsink = dsink + jnp.where(lane8 == hd, -jnp.sum(jnp.exp(sink_ref[0, hd] - lse[hd]) * delta[hd]), 0.0)
                    ds2 = unwindow(prob * (window(_nt(do_s, vz[kv][half])) - delta_s) * scale)
                    dq_s = _nn(ds2, kz[kv][half])
                    dq_pairs[pairs[0]] = dq_pairs[pairs[0]] + dq_s[:BLK]
                    dq_pairs[pairs[1]] = dq_pairs[pairs[1]] + dq_s[BLK:]
                    dk2[kv][half] = _tn(ds2, q_s)
                    dv2[kv][half] = _tn(unwindow(prob), do_s)
            dq_ref[rr, :] = jnp.concatenate(dq_pairs, axis=1)
            _acc_add(dsink_ref, first, dsink)
            for ref, acc2 in ((dk_ref, dk2), (dv_ref, dv2)):
                tot = jnp.zeros((2 * BLK, 128), F32)
                for kv in range(2):
                    for half in range(2):
                        part = jnp.where(_half_mask(128, half), acc2[kv][half], 0.0)
                        tot = tot + (part if half == kv else pltpu.roll(part, 64, 1))
                ref[pl.ds(prev, BLK), :] += tot[:BLK]
                ref[pl.ds(own, BLK), :] += tot[BLK:]
            return carry
        lax.fori_loop(0, per_step, one_block, 0)

        @pl.when(step == steps - 1)
        def _():
            dk_ref[PAD_ROWS:BLK, :] += _swa_meta_fold(dkp[...])
            dv_ref[PAD_ROWS:BLK, :] += _swa_meta_fold(dvp[...])

    full = pl.BlockSpec((rows, 128), lambda i: (0, 0))
    blocks = lambda cols: _row_spec(per_step * BLK, cols)
    return pl.pallas_call(
        body, name="swa_bwd", grid=(steps,),
        in_specs=[blocks(512), blocks(512), blocks(512), VMEM_SPEC, VMEM_SPEC, blocks(SWA_HEADS), SMEM_SPEC, VMEM_SPEC],
        out_specs=[blocks(512), full, full, _acc_spec(128), _acc_spec(512)],
        out_shape=[jax.ShapeDtypeStruct((rows, 512), F32), jax.ShapeDtypeStruct((rows, 128), F32),
                   jax.ShapeDtypeStruct((rows, 128), F32), jax.ShapeDtypeStruct((8, 128), F32),
                   jax.ShapeDtypeStruct((8, 512), F32)],
        scratch_shapes=[pltpu.VMEM((BLK, 512), BF16), pltpu.VMEM((BLK, 512), BF16),
                        pltpu.VMEM((BLK, 512), F32), pltpu.VMEM((BLK, 512), F32)],
        compiler_params=_params(("arbitrary",)),
    )(dcat, o_all, sq, sk, sv, lse, sinks, wn)


def _mix_out(h, cat_g, cat_s, wout, gpost):
    rows = h.shape[0]
    tm = _row_tile(rows)

    def body(h_ref, cg_ref, cs_ref, w_ref, g_ref, ho_ref, m_ref):
        m = _nn(cg_ref[...], w_ref[0:512, :]) + _nn(cs_ref[...], w_ref[512:1024, :])
        m_ref[...] = m
        mn, _ = _rms(m)
        ho_ref[...] = h_ref[...] + mn * g_ref[...]

    row_f32 = _row_spec(tm, D_MODEL)
    return pl.pallas_call(
        body, name="mix_out", grid=(rows // tm,),
        in_specs=[row_f32, _row_spec(tm, 512), _row_spec(tm, 512), VMEM_SPEC, VMEM_SPEC],
        out_specs=[row_f32, row_f32],
        out_shape=[jax.ShapeDtypeStruct((rows, D_MODEL), F32), jax.ShapeDtypeStruct((rows, D_MODEL), F32)],
        compiler_params=_params(("arbitrary",)),
    )(h, cat_g, cat_s, wout, gpost)


def _mix_out_bwd(dh, m, wout, gpost):
    rows = dh.shape[0]
    tm = _row_tile(rows)

    def body(dh_ref, m_ref, w_ref, g_ref, dcg_ref, dcs_ref, dm_ref, dg_ref):
        first = pl.program_id(0) == 0
        dhv = dh_ref[...]
        mn, rm = _rms(m_ref[...])
        _acc_add(dg_ref, first, _colsum(dhv * mn))
        dm16 = _rms_bwd(mn, rm, g_ref[...], dhv).astype(BF16)
        dm_ref[...] = dm16
        dcat = _nt(dm16, w_ref[...])
        dcg_ref[...] = dcat[:, 0:512]
        dcs_ref[...] = dcat[:, 512:1024]

    row_f32 = _row_spec(tm, D_MODEL)
    return pl.pallas_call(
        body, name="mix_out_bwd", grid=(rows // tm,),
        in_specs=[row_f32, row_f32, VMEM_SPEC, VMEM_SPEC],
        out_specs=[_row_spec(tm, 512), _row_spec(tm, 512), row_f32, _acc_spec(D_MODEL)],
        out_shape=[jax.ShapeDtypeStruct((rows, 512), F32), jax.ShapeDtypeStruct((rows, 512), F32),
                   jax.ShapeDtypeStruct((rows, D_MODEL), BF16), jax.ShapeDtypeStruct((8, D_MODEL), F32)],
        compiler_params=_params(("arbitrary",)),
    )(dh, m, wout, gpost)


def _mix_in_bwd(dh_out, h, g, win_p, wa2_p, cos, sin, loga, ga, dgq, dgk, dgv, dgg, dsq, dsk, dsv, dloga):
    rows = h.shape[0]
    tm = _row_tile(rows)

    def body(dho_ref, h_ref, g_ref, win_ref, wa2_ref, cos_ref, sin_ref, loga_ref, ga_ref,
             dgq_ref, dgk_ref, dgv_ref, dgg_ref, dsq_ref, dsk_ref, dsv_ref, dla_ref,
             dh_ref, dproj_ref, dwa2_ref, dg_ref, dba_ref):
        first = pl.program_id(0) == 0
        dz = dla_ref[...] * (1.0 / GLA_TAU) * (1.0 - jnp.exp(GLA_TAU * loga_ref[...]))
        _acc_add(dba_ref, first, _colsum(dz))
        dga = _nt(dz, wa2_ref[...])
        pa = _tn(ga_ref[...], dz)
        c1, s1 = cos_ref[...], sin_ref[...]
        c4 = jnp.concatenate([c1, c1, c1, c1], axis=1)
        s4 = jnp.concatenate([s1, s1, s1, s1], axis=1)
        dq_r, dk_r = dsq_ref[...], dsk_ref[...]
        dsq = dq_r * c4 - _rot_half(dq_r * s4)
        dsk = dk_r * c1 - _rot_half(dk_r * s1)
        dproj16 = jnp.concatenate(
            [dgq_ref[...], dgk_ref[...], dgv_ref[...], dgg_ref[...], dsq, dsk, dsv_ref[...], dga], axis=1).astype(BF16)
        dproj_ref[...] = dproj16
        dn = _nn(dproj16, win_ref[...])

        @pl.when(first)
        def _():
            dwa2_ref[...] = pa

        @pl.when(jnp.logical_not(first))
        def _():
            dwa2_ref[...] += pa
        hn, rh = _rms(h_ref[...])
        _acc_add(dg_ref, first, _colsum(dn * hn))
        dh_ref[...] = dho_ref[...] + _rms_bwd(hn, rh, g_ref[...], dn)

    rs = lambda c: _row_spec(tm, c)
    return pl.pallas_call(
        body, name="mix_in_bwd", grid=(rows // tm,),
        in_specs=[rs(D_MODEL), rs(D_MODEL), VMEM_SPEC, VMEM_SPEC, VMEM_SPEC, rs(128), rs(128), rs(256), rs(128),
                  rs(256), rs(256), rs(512), rs(512), rs(512), rs(128), rs(128), rs(256)],
        out_specs=[rs(D_MODEL), rs(P_END), pl.BlockSpec((128, 256), lambda i: (0, 0)), _acc_spec(D_MODEL), _acc_spec(256)],
        out_shape=[jax.ShapeDtypeStruct((rows, D_MODEL), F32), jax.ShapeDtypeStruct((rows, P_END), BF16),
                   jax.ShapeDtypeStruct((128, 256), F32), jax.ShapeDtypeStruct((8, D_MODEL), F32),
                   jax.ShapeDtypeStruct((8, 256), F32)],
        compiler_params=_params(("arbitrary",)),
    )(dh_out, h, g, win_p, wa2_p, cos, sin, loga, ga, dgq, dgk, dgv, dgg, dsq, dsk, dsv, dloga)


def _rope_tables(rows):
    pos = (jnp.arange(rows, dtype=jnp.int32) - PAD_ROWS).astype(F32)
    inv_freq = 1.0 / (ROPE_THETA ** (jnp.arange(0, SWA_HD, 2, dtype=F32) / SWA_HD))
    ang = pos[:, None] * inv_freq[None, :]
    return jnp.tile(jnp.cos(ang), (1, 4)), jnp.tile(jnp.sin(ang), (1, 4))


def _local_step(x, tgt, front, w, late_weights=None, on_grads=None, on_small=None):
    cos, sin = _rope_tables(x.shape[0] + BLK)
    g = {}

    def tell(group, names):
        for nm in names:
            g[nm] = grads_now[nm]
        return None if on_grads is None else on_grads(group, {nm: grads_now[nm] for nm in names})

    h1, a1, b1, s1, f1 = _ffn_fwd(x, w["ffn1_pre"], w["wg1"], w["wu1"], w["wd1"], w["ffn1_post"], front=front)
    if late_weights is not None:
        w = {**w, **late_weights("win", f1)}
    gq, gk, gv, gg, sq, sk, sv, ga, loga, bc, n2 = _mix_in(h1, w["mix_pre"], w["win"], w["wa2"], w["b_a"], cos, sin)
    o_g, cat_g, sp = _gla_fwd(gq, gk, gv, gg, bc, w["gla_norm"])
    o_s, cat_s, lse = _swa_fwd(sq, sk, sv, w["sinks"], w["swa_norm"])
    if late_weights is not None:
        w = {**w, **late_weights("rest", lse)}
    h2, m = _mix_out(h1, cat_g, cat_s, w["wout"], w["mix_post"])
    h3, a2, b2, s2, f2, dy, loss = _ffn_fwd(h2, w["ffn2_pre"], w["wg2"], w["wu2"], w["wd2"], w["ffn2_post"], tgt)
    del h3
    dh2, da, db, df, n3, g["ffn2_pre"], g["ffn2_post"] = _ffn_bwd_act(
        dy, h2, a2, b2, f2, w["ffn2_pre"], w["ffn2_post"], w["wg2"], w["wu2"], w["wd2"], "ffn2_bwd_act")
    grads_now = dict(wd2=_wgrad(s2, df, "ffn2_wgrad_down"), wg2=_wgrad(da, n3, "ffn2_wgrad_gate"),
                     wu2=_wgrad(db, n3, "ffn2_wgrad_up"))
    tok = tell("ffn2", ("wd2", "wg2", "wu2"))
    dcg, dcs, dm, g["mix_post"] = _mix_out_bwd(dh2, m, w["wout"], w["mix_post"] + (0.0 if tok is None else tok[0, 0]))
    dsq, dsk, dsv, g["sinks"], g["swa_norm"] = _swa_bwd(dcs, o_s, sq, sk, sv, lse, w["sinks"], w["swa_norm"])
    dgq, dgk, dgv, dgg, dloga, g["gla_norm"] = _gla_bwd(dcg, o_g, gq, gk, gv, gg, bc, sp, w["gla_norm"])
    dh1, dproj, g["wa2"], g["mix_pre"], g["b_a"] = _mix_in_bwd(
        dh2, h1, w["mix_pre"], w["win"], w["wa2"], cos, sin, loga, ga, dgq, dgk, dgv, dgg, dsq, dsk, dsv, dloga)
    dh0, da, db, df, n1, g["ffn1_pre"], g["ffn1_post"] = _ffn_bwd_act(
        dh1, x, a1, b1, f1, w["ffn1_pre"], w["ffn1_post"], w["wg1"], w["wu1"], w["wd1"], "ffn1_bwd_act", front=front)
    tok = None if on_small is None else on_small(loss[0, 0], dh0, g)
    grads_now = dict(wd1=_wgrad(s1, df, "ffn1_wgrad_down", after=tok))
    tok = tell("ffn1_down", ("wd1",))
    grads_now = dict(wg1=_wgrad(da, n1, "ffn1_wgrad_gate", after=tok))
    tok = tell("ffn1_gate", ("wg1",))
    grads_now = dict(wu1=_wgrad(db, n1, "ffn1_wgrad_up", after=tok))
    tok = tell("ffn1_up", ("wu1",))
    grads_now = dict(win=_wgrad(dproj, n2, "win_wgrad", after=tok),
                     wout=jnp.concatenate([_wgrad(cat_g, dm, "wout_wgrad_gla", after=tok),
                                           _wgrad(cat_s, dm, "wout_wgrad_swa", after=tok)], axis=0))
    tell("mix", ("wout", "win"))
    return loss[0, 0], dh0, g


def _win_pad_rows(win_t):
    pad = jnp.zeros((P_END - P_GA - 16, win_t.shape[1]), win_t.dtype)
    return jnp.concatenate([win_t[0:1536], win_t[1552:2320], win_t[1536:1552], pad], axis=0)


def _win_unpad_rows(win_p):
    return jnp.concatenate([win_p[0:1536], win_p[P_GA:P_GA + 16], win_p[1536:P_GA]], axis=0)


def _place_on_mesh():
    return lax.axis_index("x"), lax.axis_index("y"), lax.axis_index("c")


def _dev_index(px, py, pc):
    return 4 * px + 2 * py + pc


def _other_devices(x, y, c):
    flip = lambda v, f: 1 - v if f else v
    return [(flip(x, fx), flip(y, fy), flip(c, fc)) for fx in (0, 1) for fy in (0, 1) for fc in (0, 1)][1:]


def _all_gather(shards):
    n = len(shards)

    def body(*refs):
        ins, outs = refs[:n], refs[n:2 * n]
        zeros_ref, send_sems, recv_sems, local_sems = refs[2 * n:]
        zeros_ref[...] = jnp.zeros_like(zeros_ref)
        x, y, c = _place_on_mesh()
        me, sibling = (x, y, c), (x, y, 1 - c)
        chips = [(1 - x, y), (x, 1 - y), (1 - x, 1 - y)]

        def rows(k, px, py, pc):
            r = ins[k].shape[0]
            return outs[k].at[pl.ds(pl.multiple_of(_dev_index(px, py, pc) * r, 8), r), :]

        def copy(k, slot, block, to, src=None):
            return pltpu.make_async_remote_copy(
                src_ref=rows(k, *block) if src is None else src, dst_ref=rows(k, *block),
                send_sem=send_sems.at[k, slot], recv_sem=recv_sems.at[k, slot], device_id=to, device_id_type=MESH)

        local = [pltpu.make_async_copy(ins[k], rows(k, *me), local_sems.at[k]) for k in range(n)]
        sends = []
        for k in range(n):
            local[k].start()
            sends.append(copy(k, 0, me, sibling, src=ins[k]))
            sends += [copy(k, 1 + j, me, (*chip, c), src=ins[k]) for j, chip in enumerate(chips)]
        for cp in sends:
            cp.start()
        for k in range(n):
            for j, chip in enumerate(chips):
                copy(k, 1 + j, (*chip, c), me).wait_recv()
                passed = copy(k, 4 + j, (*chip, c), sibling)
                passed.start()
                sends.append(passed)
        for k in range(n):
            copy(k, 0, sibling, me).wait_recv()
            for j, chip in enumerate(chips):
                copy(k, 4 + j, (*chip, 1 - c), me).wait_recv()
        for cp in sends:
            cp.wait_send()
        for cp in local:
            cp.wait()

    return pl.pallas_call(
        body, name="all_gather_weights",
        in_specs=[ANY_SPEC] * n, out_specs=[ANY_SPEC] * n + [VMEM_SPEC],
        out_shape=[jax.ShapeDtypeStruct((N_DEV * s.shape[0], s.shape[1]), s.dtype) for s in shards]
        + [jax.ShapeDtypeStruct((8, 128), F32)],
        scratch_shapes=[pltpu.SemaphoreType.DMA((n, 7)), pltpu.SemaphoreType.DMA((n, 7)), pltpu.SemaphoreType.DMA((n,))],
    )(*shards)


HBM_SPEC = pl.BlockSpec(memory_space=pltpu.HBM)
SEM_SPEC = pl.BlockSpec(memory_space=pltpu.SEMAPHORE)
DATAFLOW = pltpu.SideEffectType.DATAFLOW_SIDE_EFFECTING


GATHER, SCATTER, SCATTER_CHIPS = "gather", "scatter", "scatter among chips"


def _exchange_peers(kind):
    x, y, c = _place_on_mesh()
    if kind == SCATTER_CHIPS:
        peers = [(1 - x, y, c), (x, 1 - y, c), (1 - x, 1 - y, c)]
        return peers, [2 * p[0] + p[1] for p in peers], 2 * x + y, 4
    peers = _other_devices(x, y, c)
    return peers, [_dev_index(*p) for p in peers], _dev_index(x, y, c), N_DEV


def _exchange_copies(srcs, lands, send_sems, recv_sems, own_sems, kind, arriving):
    peers, theirs, me, blocks = _exchange_peers(kind)
    remote, local = [], []
    for k, (src, land) in enumerate(zip(srcs, lands)):
        r = land.shape[0] // blocks

        def block(ref, d):
            return ref.at[pl.ds(pl.multiple_of(d * r, 8), r), :]

        for f, (peer, him) in enumerate(zip(peers, theirs)):
            mine, his = (him, me) if arriving else (me, him)
            sem = len(peers) * k + f
            remote.append(pltpu.make_async_remote_copy(
                src_ref=src if kind == GATHER else block(src, his), dst_ref=block(land, mine),
                send_sem=send_sems.at[sem], recv_sem=recv_sems.at[sem], device_id=peer, device_id_type=MESH))
        local.append(pltpu.make_async_copy(src if kind == GATHER else block(src, me), block(land, me), own_sems.at[k]))
    return remote, local


def _exchange_start(srcs, kind, name):
    n = len(srcs)
    lands = [lax.empty((N_DEV * s.shape[0], s.shape[1]) if kind == GATHER else s.shape, s.dtype) for s in srcs]
    sems = (3 if kind == SCATTER_CHIPS else 7) * n

    def body(*refs):
        remote, local = _exchange_copies(refs[:n], refs[n:2 * n], *refs[2 * n:2 * n + 3], kind, False)
        for cp in remote + local:
            cp.start()
        refs[-1][...] = jnp.zeros_like(refs[-1])

    both = list(srcs) + list(lands)
    outs = pl.pallas_call(
        body, name=name,
        out_shape=(pltpu.SemaphoreType.DMA((sems,)), pltpu.SemaphoreType.DMA((sems,)), pltpu.SemaphoreType.DMA((n,)),
                   *[pltpu.HBM(a.shape, a.dtype) for a in both], jax.ShapeDtypeStruct((8, 128), F32)),
        in_specs=[HBM_SPEC] * (2 * n), out_specs=(SEM_SPEC, SEM_SPEC, SEM_SPEC, *[HBM_SPEC] * (2 * n), VMEM_SPEC),
        input_output_aliases={i: 3 + i for i in range(2 * n)},
        compiler_params=pltpu.CompilerParams(has_side_effects=DATAFLOW),
    )(*[pltpu.with_memory_space_constraint(a, pltpu.HBM) for a in both])
    return outs[0:3], outs[3:3 + n], outs[3 + n:3 + 2 * n], outs[-1]


def _exchange_wait(started, kind, after, name):
    sems, srcs, lands, _ = started
    n = len(srcs)

    def body(*refs):
        args = (refs[:n], refs[n:2 * n], *refs[2 * n:2 * n + 3], kind)
        going, local = _exchange_copies(*args, False)
        for cp in going:
            cp.wait_send()
        for cp in local:
            cp.wait()
        for cp in _exchange_copies(*args, True)[0]:
            cp.wait_recv()

    both = list(srcs) + list(lands)
    outs = pl.pallas_call(
        body, name=name, out_shape=[pltpu.HBM(a.shape, a.dtype) for a in both],
        in_specs=[HBM_SPEC] * (2 * n) + [SEM_SPEC, SEM_SPEC, SEM_SPEC, ANY_SPEC], out_specs=[HBM_SPEC] * (2 * n),
        input_output_aliases={i: i for i in range(2 * n)},
        compiler_params=pltpu.CompilerParams(has_side_effects=DATAFLOW),
    )(*both, *sems, after)
    return outs[n:]


def _sibling_reduce(part, name):
    r, cols = part.shape[0] // N_DEV, part.shape[1]

    def body(p_ref, o_ref, mine, got, send_sems, recv_sems, own_sems):
        x, y, c = _place_on_mesh()

        def block(d):
            return p_ref.at[pl.ds(pl.multiple_of(d * r, 8), r), :]
        swaps = [pltpu.make_async_remote_copy(
            src_ref=block(2 * j + 1 - c), dst_ref=got.at[j], send_sem=send_sems.at[j], recv_sem=recv_sems.at[j],
            device_id=(x, y, 1 - c), device_id_type=MESH) for j in range(4)]
        keeps = [pltpu.make_async_copy(block(2 * j + c), mine.at[j], own_sems.at[j]) for j in range(4)]
        for cp in swaps + keeps:
            cp.start()
        for j in range(4):
            keeps[j].wait()
            swaps[j].wait()
            o_ref[pl.ds(j * r, r), :] = (mine[j].astype(F32) + got[j].astype(F32)).astype(o_ref.dtype)

    return pl.pallas_call(
        body, name=name, in_specs=[ANY_SPEC], out_specs=VMEM_SPEC,
        out_shape=jax.ShapeDtypeStruct((4 * r, cols), part.dtype),
        scratch_shapes=[pltpu.VMEM((4, r, cols), part.dtype), pltpu.VMEM((4, r, cols), part.dtype),
                        pltpu.SemaphoreType.DMA((4,)), pltpu.SemaphoreType.DMA((4,)), pltpu.SemaphoreType.DMA((4,))],
        compiler_params=pltpu.CompilerParams(vmem_limit_bytes=32 << 20),
    )(part)


def _sum_partials(parts, name, blocks=N_DEV):
    n = len(parts)

    def body(*refs):
        ins, outs = refs[:n], refs[n:]
        first = pl.program_id(0) == 0
        for i_ref, o_ref in zip(ins, outs):
            v = i_ref[...].astype(F32)

            @pl.when(first)
            def _():
                o_ref[...] = v

            @pl.when(jnp.logical_not(first))
            def _():
                o_ref[...] += v

    shapes = [(p.shape[0] // blocks, p.shape[1]) for p in parts]
    return pl.pallas_call(
        body, name=name, grid=(blocks,),
        in_specs=[pl.BlockSpec(s, lambda j: (j, 0)) for s in shapes],
        out_specs=[pl.BlockSpec(s, lambda j: (0, 0)) for s in shapes],
        out_shape=[jax.ShapeDtypeStruct(s, F32) for s in shapes],
        compiler_params=_params(("arbitrary",)),
    )(*parts)


def _adamw_update(w, g, m, v):
    m = ADAM_B1 * m + (1.0 - ADAM_B1) * g
    v = ADAM_B2 * v + (1.0 - ADAM_B2) * (g * g)
    m_hat = m * (1.0 / (1.0 - ADAM_B1 ** ADAM_STEP))
    v_hat = v * (1.0 / (1.0 - ADAM_B2 ** ADAM_STEP))
    return -ADAM_LR * (m_hat / (jnp.sqrt(v_hat) + ADAM_EPS) + ADAM_WD * w), m, v


def _sum_adamw(parts, w, m, v, blocks, name):
    shape = w.shape

    def body(p_ref, w_ref, m_ref, v_ref, g_ref, d_ref, mo_ref, vo_ref):
        j = pl.program_id(0)
        part = p_ref[...].astype(F32)

        @pl.when(j == 0)
        def _():
            g_ref[...] = part

        @pl.when(j > 0)
        def _():
            g_ref[...] += part

        @pl.when(j == blocks - 1)
        def _():
            d_ref[...], mo_ref[...], vo_ref[...] = _adamw_update(w_ref[...], g_ref[...], m_ref[...], v_ref[...])

    held = pl.BlockSpec(shape, lambda j: (0, 0))
    return pl.pallas_call(
        body, name=name, grid=(blocks,),
        in_specs=[pl.BlockSpec(shape, lambda j: (j, 0)), held, held, held],
        out_specs=[held] * 4, out_shape=[jax.ShapeDtypeStruct(shape, F32)] * 4,
        compiler_params=_params(("arbitrary",)),
    )(parts, w, m, v)


def _adamw(ws, gs, ms, vs, name):
    n = len(ws)

    def body(*refs):
        w_r, g_r, m_r, v_r = refs[:n], refs[n:2 * n], refs[2 * n:3 * n], refs[3 * n:4 * n]
        d_o, m_o, v_o = refs[4 * n:5 * n], refs[5 * n:6 * n], refs[6 * n:7 * n]
        for k in range(n):
            d_o[k][...], m_o[k][...], v_o[k][...] = _adamw_update(w_r[k][...], g_r[k][...], m_r[k][...], v_r[k][...])

    shapes = [jax.ShapeDtypeStruct(w.shape, F32) for w in ws]
    outs = pl.pallas_call(
        body, name=name, in_specs=[VMEM_SPEC] * (4 * n), out_specs=[VMEM_SPEC] * (3 * n), out_shape=shapes * 3,
        compiler_params=pltpu.CompilerParams(vmem_limit_bytes=56 << 20),
    )(*ws, *gs, *ms, *vs)
    return outs[:n], outs[n:2 * n], outs[2 * n:]


WEIGHT_NAMES = ("meta_tokens", "ffn1_pre_norm", "ffn1_w_gate", "ffn1_w_up", "ffn1_w_down", "ffn1_post_norm", "mix_pre_norm",
                "w_in", "gla_w_a2", "gla_b_a", "gla_out_norm", "swa_sinks", "swa_out_norm", "w_out", "mix_post_norm",
                "ffn2_pre_norm", "ffn2_w_gate", "ffn2_w_up", "ffn2_w_down", "ffn2_post_norm")
WIN_SHARD = D_IN // N_DEV
WIN_SHARD_PAD = 304
SLAB_VECTORS = ("ffn1_pre", "ffn1_post", "mix_pre", "mix_post", "ffn2_pre", "ffn2_post")
SLAB_ROWS = 32


def kernel(x, meta_tokens, ffn1_pre_norm, ffn1_w_gate, ffn1_w_up, ffn1_w_down, ffn1_post_norm, mix_pre_norm, w_in, gla_w_a2, gla_b_a, gla_out_norm, swa_sinks, swa_out_norm, w_out, mix_post_norm, ffn2_pre_norm, ffn2_w_gate, ffn2_w_up, ffn2_w_down, ffn2_post_norm, loss_target, m_meta_tokens, m_ffn1_pre_norm, m_ffn1_w_gate, m_ffn1_w_up, m_ffn1_w_down, m_ffn1_post_norm, m_mix_pre_norm, m_w_in, m_gla_w_a2, m_gla_b_a, m_gla_out_norm, m_swa_sinks, m_swa_out_norm, m_w_out, m_mix_post_norm, m_ffn2_pre_norm, m_ffn2_w_gate, m_ffn2_w_up, m_ffn2_w_down, m_ffn2_post_norm, v_meta_tokens, v_ffn1_pre_norm, v_ffn1_w_gate, v_ffn1_w_up, v_ffn1_w_down, v_ffn1_post_norm, v_mix_pre_norm, v_w_in, v_gla_w_a2, v_gla_b_a, v_gla_out_norm, v_swa_sinks, v_swa_out_norm, v_w_out, v_mix_post_norm, v_ffn2_pre_norm, v_ffn2_w_gate, v_ffn2_w_up, v_ffn2_w_down, v_ffn2_post_norm):
    given = dict(locals())
    W = {n: given[n] for n in WEIGHT_NAMES}
    M = {n: given["m_" + n] for n in WEIGHT_NAMES}
    V = {n: given["v_" + n] for n in WEIGHT_NAMES}
    dev = _dev_index(*_place_on_mesh())

    def t16(w):
        return w[0].T.astype(BF16)

    small = jnp.concatenate([W["meta_tokens"], jnp.pad(W["gla_w_a2"][0], ((0, 0), (0, 96)))], axis=0)
    wg1, wu1, wd1, small_g, gathered_zeros = _all_gather(
        [t16(W["ffn1_w_gate"]), t16(W["ffn1_w_up"]), W["ffn1_w_down"][0].astype(BF16), small])
    def after_zero(shard, zeros):
        return shard + zeros[0:1, 0:1].astype(shard.dtype)
    win_shard = jnp.pad(t16(W["w_in"]), ((0, WIN_SHARD_PAD - WIN_SHARD), (0, 0)))
    win_shard = after_zero(win_shard, gathered_zeros)
    mid = _exchange_start([win_shard], GATHER, "gather_w_in_start")
    late_shards = [after_zero(W["w_out"][0].astype(BF16), mid[3]), t16(W["ffn2_w_gate"]), t16(W["ffn2_w_up"]),
                   W["ffn2_w_down"][0].astype(BF16)]
    late = _exchange_start(late_shards, GATHER, "gather_late_weights_start")

    def late_weights(what, after):
        if what == "win":
            win_g, = _exchange_wait(mid, GATHER, after, "gather_w_in_wait")
            win_t = win_g.reshape(N_DEV, WIN_SHARD_PAD, D_MODEL)[:, :WIN_SHARD].reshape(D_IN, D_MODEL)
            return dict(win=_win_pad_rows(win_t))
        wout, wg2, wu2, wd2 = _exchange_wait(late, GATHER, after, "gather_late_weights_wait")
        return dict(wout=wout, wg2=wg2, wu2=wu2, wd2=wd2)

    small_g = small_g.reshape(N_DEV, 32, 128)
    meta_full = small_g[:, :N_META].transpose(1, 0, 2).reshape(N_META, D_MODEL)
    wa2_full = small_g[:, N_META:, :32].transpose(1, 0, 2).reshape(16, 256)
    w = dict(
        ffn1_pre=W["ffn1_pre_norm"] + late[3][0, 0], ffn1_post=W["ffn1_post_norm"], mix_pre=W["mix_pre_norm"],
        mix_post=W["mix_post_norm"], ffn2_pre=W["ffn2_pre_norm"], ffn2_post=W["ffn2_post_norm"], b_a=W["gla_b_a"],
        gla_norm=W["gla_out_norm"], sinks=W["swa_sinks"], swa_norm=W["swa_out_norm"], wg1=wg1, wu1=wu1, wd1=wd1,
        wa2=jnp.pad(wa2_full, ((0, 112), (0, 0))))

    in_flight = []

    def on_grads(group, grads):
        parts = []
        for nm, p in grads.items():
            if nm == "win":
                p = _win_unpad_rows(p).reshape(N_DEV, WIN_SHARD, D_MODEL)
                p = jnp.pad(p, ((0, 0), (0, WIN_SHARD_PAD - WIN_SHARD), (0, 0))).reshape(N_DEV * WIN_SHARD_PAD, D_MODEL)
            parts.append(p)
        kind = SCATTER if group == "ffn2" else SCATTER_CHIPS
        if kind == SCATTER_CHIPS:
            parts = [_sibling_reduce(p, "pair_" + group + "_" + nm) for nm, p in zip(grads, parts)]
        started = _exchange_start(parts, kind, "scatter_" + group + "_start")
        in_flight.append((group, list(grads), started, kind))
        return started[3]

    small_flight = []

    def on_small(loss, dh0, g):
        packed = jnp.concatenate([g["b_a"][0:1], g["gla_norm"][0:1], g["sinks"][0:1], g["swa_norm"][0:1]], axis=1)
        slab = jnp.concatenate([g[k][0:1] for k in SLAB_VECTORS] + [packed, jnp.full((1, D_MODEL), loss, F32),
                               g["wa2"][:16].reshape(4, D_MODEL), jnp.zeros((4, D_MODEL), F32), dh0[PAD_ROWS:BLK]], axis=0)
        small_flight.append(_exchange_start([slab], GATHER, "gather_small_grads_start"))
        return small_flight[0][3]

    front = jnp.concatenate([jnp.zeros((PAD_ROWS, D_MODEL), F32), meta_full], axis=0)
    loss, dh0, g = _local_step(x[0], loss_target[0], front, w, late_weights, on_grads, on_small)
    grad_x = dh0[BLK:][None]

    land, = _exchange_wait(small_flight[0], GATHER, in_flight[-1][2][3], "gather_small_grads_wait")
    tot = _sum_partials([land], "sum_small_grads")[0]
    loss = tot[7, 0]
    small_grads = dict(
        ffn1_pre_norm=tot[0:1], ffn1_post_norm=tot[1:2], mix_pre_norm=tot[2:3], mix_post_norm=tot[3:4],
        ffn2_pre_norm=tot[4:5], ffn2_post_norm=tot[5:6], gla_b_a=tot[6:7, 0:256], gla_out_norm=tot[6:7, 256:384],
        swa_sinks=tot[6:7, 384:392], swa_out_norm=tot[6:7, 512:1024],
        gla_w_a2=lax.dynamic_slice_in_dim(tot[8:12].reshape(16, 256), dev * 32, 32, axis=1)[None],
        meta_tokens=lax.dynamic_slice_in_dim(tot[16:32], dev * 128, 128, axis=1))

    big = dict(wg1=("ffn1_w_gate", True), wu1=("ffn1_w_up", True), wd1=("ffn1_w_down", False), win=("w_in", True),
               wout=("w_out", False), wg2=("ffn2_w_gate", True), wu2=("ffn2_w_up", True), wd2=("ffn2_w_down", False))
    grads = dict(small_grads)
    delta, new_m, new_v = {}, {}, {}
    names = [n for n in WEIGHT_NAMES if n not in [full for full, _ in big.values()]]
    two_d = lambda a: a.reshape(-1, a.shape[-1])
    d_, m_, v_ = _adamw([two_d(W[n]) for n in names], [two_d(grads[n]) for n in names],
                        [two_d(M[n]) for n in names], [two_d(V[n]) for n in names], "adamw_small")
    for k, n in enumerate(names):
        delta[n], new_m[n], new_v[n] = d_[k].reshape(W[n].shape), m_[k].reshape(W[n].shape), v_[k].reshape(W[n].shape)

    before_wait = d_[0] + in_flight[-1][2][3][0, 0]
    for group, shorts, started, kind in in_flight:
        lands = _exchange_wait(started, kind, before_wait, "scatter_" + group + "_wait")
        blocks = 4 if kind == SCATTER_CHIPS else N_DEV
        for short, land in zip(shorts, lands):
            n, transposed = big[short]
            to_slab = (lambda a: a[0].T) if transposed else (lambda a: a[0])
            from_slab = (lambda a: a.T[None]) if transposed else (lambda a: a[None])
            if short == "win":
                g_slab = _sum_partials([land], "sum_" + n, blocks)[0][:WIN_SHARD]
                d_, m_, v_ = _adamw([to_slab(W[n])], [g_slab], [to_slab(M[n])], [to_slab(V[n])], "adamw_" + n)
                d_, m_, v_ = d_[0], m_[0], v_[0]
            else:
                g_slab, d_, m_, v_ = _sum_adamw(land, to_slab(W[n]), to_slab(M[n]), to_slab(V[n]), blocks, "adamw_" + n)
            grads[n], delta[n], new_m[n], new_v[n] = from_slab(g_slab), from_slab(d_), from_slab(m_), from_slab(v_)
            before_wait = d_
    return (loss, grad_x, *[grads[n] for n in WEIGHT_NAMES], *[delta[n] for n in WEIGHT_NAMES],
            *[new_m[n] for n in WEIGHT_NAMES], *[new_v[n] for n in WEIGHT_NAMES])
```

```python
import math

import jax
import jax.numpy as jnp
from jax import lax
from jax.experimental import pallas as pl
from jax.experimental.pallas import tpu as pltpu

F32, BF16 = jnp.float32, jnp.bfloat16

D_MODEL = 1024
D_FF = 2816
N_META = 16
BLK = 128
PAD_ROWS = BLK - N_META
GLA_DK = 64
SWA_HD = 64
SWA_HEADS = 8
GLA_TAU = 16.0
NORM_EPS = 1e-6
NEG_INF = -1e30
ROPE_THETA = 10000.0
P_GQ, P_GK, P_GV, P_GG, P_SQ, P_SK, P_SV, P_GA, P_END = 0, 256, 512, 1024, 1536, 2048, 2176, 2304, 2432
D_IN = 2320
IN_SPLITS = (256, 256, 512, 512, 16, 512, 128, 128)
FF_TILE = 2816
WGRAD_TILE_MAX = 2432
N_DEV = 8
MESH = pl.DeviceIdType.MESH

ADAM_LR, ADAM_B1, ADAM_B2, ADAM_EPS, ADAM_WD, ADAM_STEP = 0.001, 0.9, 0.999, 1e-08, 0.01, 10

V7X_VMEM_BYTES = 64 << 20
VMEM_SPEC = pl.BlockSpec(memory_space=pltpu.VMEM)
SMEM_SPEC = pl.BlockSpec(memory_space=pltpu.SMEM)
ANY_SPEC = pl.BlockSpec(memory_space=pl.ANY)


def _params(semantics, vmem_mb=56):
    return pltpu.CompilerParams(dimension_semantics=semantics, vmem_limit_bytes=vmem_mb << 20)


def _row_tile(rows):
    return 416 if rows % 416 == 0 else BLK


def _blocks_per_step(blocks):
    return 5 if blocks % 5 == 0 else 1


def _nn(a, b):
    return lax.dot_general(a, b, (((1,), (0,)), ((), ())), preferred_element_type=F32)


def _nt(a, b):
    return lax.dot_general(a, b, (((1,), (1,)), ((), ())), preferred_element_type=F32)


def _tn(a, b):
    return lax.dot_general(a, b, (((0,), (0,)), ((), ())), preferred_element_type=F32)


def _rms(x):
    r = lax.rsqrt(jnp.mean(x * x, axis=-1, keepdims=True) + NORM_EPS)
    return x * r, r


def _rms_bwd(xn, r, w, dy):
    g = dy * w
    return r * (g - xn * jnp.mean(g * xn, axis=-1, keepdims=True))


def _sigmoid(x):
    return 1.0 / (1.0 + jnp.exp(-x))


def _colsum(x):
    return jnp.sum(x, axis=0, keepdims=True)


def _split_bf16(x):
    hi = x.astype(BF16)
    lo = (x - hi.astype(F32)).astype(BF16)
    return hi, lo


def _tri(lower):
    r = lax.broadcasted_iota(jnp.int32, (BLK, BLK), 0)
    c = lax.broadcasted_iota(jnp.int32, (BLK, BLK), 1)
    return (r >= c) if lower else (c >= r)


def _half_mask(width, half):
    lane = lax.broadcasted_iota(jnp.int32, (1, width), 1)
    return ((lane % 128) < 64) if half == 0 else ((lane % 128) >= 64)


def _rot_half(x):
    w = x.shape[-1]
    lane = lax.broadcasted_iota(jnp.int32, (1, w), 1)
    return jnp.where((lane % SWA_HD) < SWA_HD // 2, -pltpu.roll(x, w - SWA_HD // 2, 1), pltpu.roll(x, SWA_HD // 2, 1))


def _row_spec(tm, cols):
    return pl.BlockSpec((tm, cols), lambda i: (i, 0))


def _acc_spec(cols):
    return pl.BlockSpec((8, cols), lambda i: (0, 0))


def _acc_add(ref, first, value):
    @pl.when(first)
    def _():
        ref[...] = jnp.zeros_like(ref)
    ref[0:1, :] += value


def _behind_spec(tm):
    return pl.BlockSpec((pl.Element(tm), pl.Element(D_MODEL)),
                        lambda i: (pl.multiple_of(jnp.maximum(i * tm - BLK, 0), math.gcd(tm, BLK)), 0))


def _behind_front(ref, i, tm, front):
    blk = ref[...]
    return jnp.where(i == 0, jnp.concatenate([front, blk[0:tm - BLK]], axis=0), blk)


def _ffn_fwd(h, gpre, wg_t, wu_t, wd, gpost, tgt=None, front=None, mixed=None):
    with_loss, with_front, with_mixed = tgt is not None, front is not None, mixed is not None
    rows = h.shape[0] + (BLK if with_front else 0)
    tm = _row_tile(rows)
    nf = D_FF // FF_TILE

    def body(*refs):
        refs = list(refs)
        h_ref, gpre_ref, wg_ref, wu_ref, wd_ref, gpost_ref = refs[:6]
        del refs[:6]
        front_ref = refs.pop(0) if with_front else None
        cg_ref, cs_ref, wo_ref, gm_ref = (refs.pop(0), refs.pop(0), refs.pop(0), refs.pop(0)) if with_mixed else (None,) * 4
        t_ref = refs.pop(0) if with_loss else None
        hm_ref, m_ref = (refs.pop(0), refs.pop(0)) if with_mixed else (None, None)
        ho_ref = None if with_loss else refs.pop(0)
        a_ref, b_ref, s_ref, f_ref = refs[:4]
        dy_ref, loss_ref = refs[4:6] if with_loss else (None, None)
        acc = refs[-1]
        i = pl.program_id(0)
        h_in = _behind_front(h_ref, i, tm, front_ref[...]) if with_front else h_ref[...]
        if with_mixed:
            m = _nn(cg_ref[...], wo_ref[0:512, :]) + _nn(cs_ref[...], wo_ref[512:1024, :])
            m_ref[...] = m
            mn, _ = _rms(m)
            h_in = h_in + mn * gm_ref[...]
            hm_ref[...] = h_in
        hn, _ = _rms(h_in)
        n16 = (hn * gpre_ref[...]).astype(BF16)
        for j in range(nf):
            cols = slice(j * FF_TILE, (j + 1) * FF_TILE)
            a = _nt(n16, wg_ref[cols, :])
            b = _nt(n16, wu_ref[cols, :])
            a_ref[:, cols] = a.astype(BF16)
            b_ref[:, cols] = b.astype(BF16)
            s16 = (a * _sigmoid(a) * b).astype(BF16)
            s_ref[:, cols] = s16
            part = _nn(s16, wd_ref[cols, :])
            if j == 0:
                acc[...] = part
            else:
                acc[...] += part
        f = acc[...]
        f_ref[...] = f
        fn, _ = _rms(f)
        y = h_in + 0.5 * (fn * gpost_ref[...])
        if not with_loss:
            ho_ref[...] = y
        else:
            row = i * tm + lax.broadcasted_iota(jnp.int32, (tm, 1), 0)
            err = jnp.where(row >= BLK, y - _behind_front(t_ref, i, tm, jnp.zeros((BLK, D_MODEL), F32)), 0.0)
            dy_ref[...] = err * (1.0 / D_MODEL)
            part = 0.5 * jnp.sum(jnp.sum(err * err, axis=-1, keepdims=True) * (1.0 / D_MODEL), axis=0, keepdims=True)

            @pl.when(i == 0)
            def _():
                loss_ref[...] = jnp.zeros_like(loss_ref)
            loss_ref[...] += part

    row_f32 = _row_spec(tm, D_MODEL)
    behind = _behind_spec(tm)
    in_specs = [behind if with_front else row_f32, VMEM_SPEC, VMEM_SPEC, VMEM_SPEC, VMEM_SPEC, VMEM_SPEC]
    wide, full = jax.ShapeDtypeStruct((rows, D_FF), BF16), jax.ShapeDtypeStruct((rows, D_MODEL), F32)
    out_specs = [_row_spec(tm, D_FF), _row_spec(tm, D_FF), _row_spec(tm, D_FF), row_f32]
    out_shape = [wide, wide, wide, full]
    args = [h, gpre, wg_t, wu_t, wd, gpost]
    if not with_loss:
        out_specs.insert(0, row_f32)
        out_shape.insert(0, full)
    if with_front:
        in_specs.append(VMEM_SPEC)
        args.append(front)
    if with_mixed:
        in_specs += [_row_spec(tm, 512), _row_spec(tm, 512), VMEM_SPEC, VMEM_SPEC]
        args += list(mixed)
        out_specs = [row_f32, row_f32] + out_specs
        out_shape = [full, full] + out_shape
    if with_loss:
        in_specs.append(behind)
        args.append(tgt)
        out_specs += [row_f32, pl.BlockSpec((8, 128), lambda i: (0, 0))]
        out_shape += [jax.ShapeDtypeStruct((rows, D_MODEL), F32), jax.ShapeDtypeStruct((8, 128), F32)]
    return pl.pallas_call(
        body, name="ffn_fwd_loss" if with_loss else "ffn_fwd", grid=(rows // tm,),
        in_specs=in_specs, out_specs=out_specs, out_shape=out_shape,
        scratch_shapes=[pltpu.VMEM((tm, D_MODEL), F32)],
        compiler_params=_params(("arbitrary",), vmem_mb=62 if with_mixed else 56),
    )(*args)


def _ffn_bwd_act(dh_out, h, a, b, f, gpre, gpost, wg_t, wu_t, wd, name, front=None):
    with_front = front is not None
    rows = dh_out.shape[0]
    tm = _row_tile(rows)
    nf = D_FF // FF_TILE

    def body(dho_ref, h_ref, a_ref, b_ref, f_ref, gpre_ref, gpost_ref, wg_ref, wu_ref, wd_ref, *rest):
        front_ref = rest[0] if with_front else None
        dh_ref, da_ref, db_ref, df_ref, n_ref, dgpre_ref, dgpost_ref, acc = rest[-8:]
        first = pl.program_id(0) == 0
        dho = dho_ref[...]
        drr = 0.5 * dho
        fn, rf = _rms(f_ref[...])
        _acc_add(dgpost_ref, first, _colsum(drr * fn))
        df16 = _rms_bwd(fn, rf, gpost_ref[...], drr).astype(BF16)
        df_ref[...] = df16
        h_in = _behind_front(h_ref, pl.program_id(0), tm, front_ref[...]) if with_front else h_ref[...]
        hn, rh = _rms(h_in)
        n_ref[...] = (hn * gpre_ref[...]).astype(BF16)
        for j in range(nf):
            cols = slice(j * FF_TILE, (j + 1) * FF_TILE)
            ds = _nt(df16, wd_ref[cols, :])
            av = a_ref[:, cols].astype(F32)
            bv = b_ref[:, cols].astype(F32)
            sg = _sigmoid(av)
            db16 = (ds * (av * sg)).astype(BF16)
            da16 = (ds * bv * (sg * (1.0 + av * (1.0 - sg)))).astype(BF16)
            da_ref[:, cols] = da16
            db_ref[:, cols] = db16
            part = _nn(da16, wg_ref[cols, :]) + _nn(db16, wu_ref[cols, :])
            if j == 0:
                acc[...] = part
            else:
                acc[...] += part
        dn = acc[...]
        _acc_add(dgpre_ref, first, _colsum(dn * hn))
        dh_ref[...] = dho + _rms_bwd(hn, rh, gpre_ref[...], dn)

    row_f32 = _row_spec(tm, D_MODEL)
    row_ff = _row_spec(tm, D_FF)
    return pl.pallas_call(
        body, name=name, grid=(rows // tm,),
        in_specs=[row_f32, _behind_spec(tm) if with_front else row_f32, row_ff, row_ff, row_f32,
                  VMEM_SPEC, VMEM_SPEC, VMEM_SPEC, VMEM_SPEC, VMEM_SPEC] + ([VMEM_SPEC] if with_front else []),
        out_specs=[row_f32, row_ff, row_ff, row_f32, row_f32, _acc_spec(D_MODEL), _acc_spec(D_MODEL)],
        out_shape=[jax.ShapeDtypeStruct((rows, D_MODEL), F32), jax.ShapeDtypeStruct((rows, D_FF), BF16),
                   jax.ShapeDtypeStruct((rows, D_FF), BF16), jax.ShapeDtypeStruct((rows, D_MODEL), BF16),
                   jax.ShapeDtypeStruct((rows, D_MODEL), BF16), jax.ShapeDtypeStruct((8, D_MODEL), F32),
                   jax.ShapeDtypeStruct((8, D_MODEL), F32)],
        scratch_shapes=[pltpu.VMEM((tm, D_MODEL), F32)],
        compiler_params=_params(("arbitrary",), vmem_mb=62),
    )(dh_out, h, a, b, f, gpre, gpost, wg_t, wu_t, wd, *([front] if with_front else []))


def _wgrad(lhs, rhs, name, after=None):
    rows, width = lhs.shape
    tm = rows if rows % 1664 == 0 else BLK
    tf = 256 if width % 256 == 0 else 128
    nr = rows // tm

    def body(l_ref, r_ref, *rest):
        o_ref, acc = rest[-2:]
        i = pl.program_id(1)
        part = _tn(l_ref[...], r_ref[...])

        @pl.when(i == 0)
        def _():
            acc[...] = part

        @pl.when(i > 0)
        def _():
            acc[...] += part

        @pl.when(i == nr - 1)
        def _():
            o_ref[...] = acc[...].astype(BF16)

    l_spec = pl.BlockSpec((tm, tf), lambda j, i: (i, j))
    r_spec = pl.BlockSpec((tm, D_MODEL), lambda j, i: (i, 0))
    return pl.pallas_call(
        body, name=name, grid=(width // tf, nr),
        in_specs=[l_spec, r_spec] + ([] if after is None else [ANY_SPEC]),
        out_specs=pl.BlockSpec((tf, D_MODEL), lambda j, i: (j, 0)),
        out_shape=jax.ShapeDtypeStruct((width, D_MODEL), BF16),
        scratch_shapes=[pltpu.VMEM((tf, D_MODEL), F32)],
        compiler_params=_params(("arbitrary", "arbitrary")),
    )(lhs, rhs, *([] if after is None else [after]))


def _chunk_cumsum(x, lower):
    tri = jnp.where(_tri(lower), 1.0, 0.0).astype(BF16)
    hi, lo = _split_bf16(x)
    return _nn(tri, hi) + _nn(tri, lo)


def _mix_in(h, g, win_p, wa2_p, b_a, cos, sin):
    rows = h.shape[0]
    tm = 640 if rows % 640 == 0 else BLK

    def body(h_ref, g_ref, win_ref, wa2_ref, ba_ref, cos_ref, sin_ref,
             gq_ref, gk_ref, gv_ref, gg_ref, sq_ref, sk_ref, sv_ref, ga_ref, loga_ref, bc_ref, n_ref):
        hn, _ = _rms(h_ref[...])
        n16 = (hn * g_ref[...]).astype(BF16)
        n_ref[...] = n16
        proj = _nt(n16, win_ref[...])
        gq_ref[...] = proj[:, P_GQ:P_GK]
        gk_ref[...] = proj[:, P_GK:P_GV]
        gv_ref[...] = proj[:, P_GV:P_GG].astype(BF16)
        gg_ref[...] = proj[:, P_GG:P_SQ]
        c1, s1 = cos_ref[...], sin_ref[...]
        c4 = jnp.concatenate([c1, c1, c1, c1], axis=1)
        s4 = jnp.concatenate([s1, s1, s1, s1], axis=1)
        sq = proj[:, P_SQ:P_SK]
        sk = proj[:, P_SK:P_SV]
        sq_ref[...] = (sq * c4 + _rot_half(sq) * s4).astype(BF16)
        sk_ref[...] = (sk * c1 + _rot_half(sk) * s1).astype(BF16)
        sv_ref[...] = proj[:, P_SV:P_GA].astype(BF16)
        ga = proj[:, P_GA:P_END]
        ga_ref[...] = ga
        z = _nn(ga, wa2_ref[...]) + ba_ref[...]
        loga = (jnp.minimum(z, 0.0) - jnp.log(1.0 + jnp.exp(-jnp.abs(z)))) * (1.0 / GLA_TAU)
        loga_ref[...] = loga
        for c in range(tm // BLK):
            rs = slice(c * BLK, (c + 1) * BLK)
            bc_ref[rs, :] = _chunk_cumsum(loga[rs, :], True)

    f32 = lambda c: jax.ShapeDtypeStruct((rows, c), F32)
    b16 = lambda c: jax.ShapeDtypeStruct((rows, c), BF16)
    rs = lambda c: _row_spec(tm, c)
    return pl.pallas_call(
        body, name="mix_in", grid=(rows // tm,),
        in_specs=[rs(D_MODEL), VMEM_SPEC, VMEM_SPEC, VMEM_SPEC, VMEM_SPEC, rs(128), rs(128)],
        out_specs=[rs(256), rs(256), rs(512), rs(512), rs(512), rs(128), rs(128), rs(128), rs(256), rs(256), rs(D_MODEL)],
        out_shape=[f32(256), f32(256), b16(512), f32(512), b16(512), b16(128), b16(128), f32(128), f32(256), f32(256),
                   b16(D_MODEL)],
        compiler_params=_params(("arbitrary",)),
    )(h, g, win_p, wa2_p, b_a, cos, sin)


def _gla_factors(q, k, bc):
    bm = bc[BLK // 2 - 1:BLK // 2, :]
    bl = bc[BLK - 1:BLK, :]
    e_q, e_k, e_qe, e_kd = jnp.exp(bc - bm), jnp.exp(bm - bc), jnp.exp(bc), jnp.exp(bl - bc)
    return (q * e_q, k * e_k, q * e_qe, k * e_kd), (e_q, e_k, e_qe, e_kd), jnp.exp(bl)


def _gla_fwd(gq, gk, gv, gg, bc, wgn):
    rows = gq.shape[0]
    nc = rows // BLK
    per_step = _blocks_per_step(nc)
    scale = GLA_DK ** -0.5

    def body(q_ref, k_ref, v_ref, gg_ref, bc_ref, wgn_ref, o_ref, cat_ref, sp_ref, st):
        @pl.when(pl.program_id(0) == 0)
        def _():
            st[...] = jnp.zeros_like(st)
        low = _tri(True)
        wgn_v = wgn_ref[...]

        def chunk(c, carry):
            rr = pl.ds(pl.multiple_of(c * BLK, BLK), BLK)
            for p in range(2):
                sl = slice(128 * p, 128 * p + 128)
                (qt, kt, qe, kd), _, ebl = _gla_factors(q_ref[rr, sl] * scale, k_ref[rr, sl], bc_ref[rr, sl])
                s_prev = st[p]
                sp_ref[c, p] = s_prev
                s16 = s_prev.astype(BF16)
                qt16 = qt.astype(BF16)
                s_new = s_prev * ebl
                for hh in range(2):
                    hs = slice(128 * (2 * p + hh), 128 * (2 * p + hh) + 128)
                    lm = _half_mask(128, hh)
                    vh = v_ref[rr, hs]
                    pm = jnp.where(low, _nt(qt16, jnp.where(lm, kt, 0.0).astype(BF16)), 0.0)
                    o = _nn(pm.astype(BF16), vh) + _nt(jnp.where(lm, qe, 0.0).astype(BF16), s16)
                    s_new = s_new + _tn(vh, jnp.where(lm, kd, 0.0).astype(BF16))
                    o_ref[rr, hs] = o
                    on, _ = _rms(o)
                    gate = gg_ref[rr, hs]
                    cat_ref[rr, hs] = (on * wgn_v * (gate * _sigmoid(gate))).astype(BF16)
                st[p] = s_new
            return carry
        lax.fori_loop(0, per_step, chunk, 0)

    rs = lambda c: _row_spec(per_step * BLK, c)
    return pl.pallas_call(
        body, name="gla_fwd", grid=(nc // per_step,),
        in_specs=[rs(256), rs(256), rs(512), rs(512), rs(256), VMEM_SPEC],
        out_specs=[rs(512), rs(512), pl.BlockSpec((per_step, 2, 128, 128), lambda i: (i, 0, 0, 0))],
        out_shape=[jax.ShapeDtypeStruct((rows, 512), F32), jax.ShapeDtypeStruct((rows, 512), BF16),
                   jax.ShapeDtypeStruct((nc, 2, 128, 128), F32)],
        scratch_shapes=[pltpu.VMEM((2, 128, 128), F32)],
        compiler_params=_params(("arbitrary",)),
    )(gq, gk, gv, gg, bc, wgn)


def _gla_bwd(dcat, o_all, gq, gk, gv, gg, bc, sp, wgn):
    rows = gq.shape[0]
    nc = rows // BLK
    per_step = _blocks_per_step(nc)
    steps = nc // per_step
    scale = GLA_DK ** -0.5

    def body(dc_ref, o_ref, q_ref, k_ref, v_ref, gg_ref, bc_ref, sp_ref, wgn_ref,
             dq_ref, dk_ref, dv_ref, dgg_ref, dla_ref, dwgn_ref, dst):
        first = pl.program_id(0) == 0

        @pl.when(first)
        def _():
            dst[...] = jnp.zeros_like(dst)
        low, upp = _tri(True), _tri(False)
        last_row = lax.broadcasted_iota(jnp.int32, (BLK, 1), 0) == BLK - 1
        wgn_v = wgn_ref[...]

        def chunk(c, dwgn):
            rr = pl.ds(pl.multiple_of((per_step - 1 - c) * BLK, BLK), BLK)
            for p in range(2):
                sl = slice(128 * p, 128 * p + 128)
                (qt, kt, qe, kd), (e_q, e_k, e_qe, e_kd), ebl = _gla_factors(
                    q_ref[rr, sl] * scale, k_ref[rr, sl], bc_ref[rr, sl])
                s_prev = sp_ref[per_step - 1 - c, p]
                s16 = s_prev.astype(BF16)
                ds_next = dst[p]
                ds16 = ds_next.astype(BF16)
                qt16 = qt.astype(BF16)
                ds_new = ds_next * ebl
                dqt = jnp.zeros((BLK, 128), F32)
                dkt = jnp.zeros((BLK, 128), F32)
                dqe = jnp.zeros((BLK, 128), F32)
                dkd = jnp.zeros((BLK, 128), F32)
                for hh in range(2):
                    hs = slice(128 * (2 * p + hh), 128 * (2 * p + hh) + 128)
                    lm = _half_mask(128, hh)
                    on, ro = _rms(o_ref[rr, hs])
                    gate = gg_ref[rr, hs]
                    sg = _sigmoid(gate)
                    si = gate * sg
                    dog = dc_ref[rr, hs]
                    dwgn = dwgn + _colsum(dog * si * on)
                    dgg_ref[rr, hs] = dog * (on * wgn_v) * (sg * (1.0 + gate * (1.0 - sg)))
                    do16 = _rms_bwd(on, ro, wgn_v, dog * si).astype(BF16)
                    vh = v_ref[rr, hs]
                    ktm16 = jnp.where(lm, kt, 0.0).astype(BF16)
                    qtm16 = jnp.where(lm, qt, 0.0).astype(BF16)
                    qem16 = jnp.where(lm, qe, 0.0).astype(BF16)
                    kdm16 = jnp.where(lm, kd, 0.0).astype(BF16)
                    p_t = jnp.where(upp, _nt(ktm16, qt16), 0.0)
                    dp_t = jnp.where(upp, _nt(vh, do16), 0.0)
                    dp = jnp.where(low, _nt(do16, vh), 0.0)
                    dv_ref[rr, hs] = _nn(p_t.astype(BF16), do16) + _nt(kdm16, ds16)
                    dqt = dqt + _nn(dp.astype(BF16), ktm16)
                    dkt = dkt + _nn(dp_t.astype(BF16), qtm16)
                    dqe = dqe + jnp.where(lm, _nn(do16, s16), 0.0)
                    dkd = dkd + jnp.where(lm, _nn(vh, ds16), 0.0)
                    ds_new = ds_new + _tn(do16, qem16)
                debl = _colsum(ds_next * s_prev)
                dq_ref[rr, sl] = (dqt * e_q + dqe * e_qe) * scale
                dk_ref[rr, sl] = dkt * e_k + dkd * e_kd
                dkd_kd = dkd * kd
                db = dqt * qt - dkt * kt + dqe * qe - dkd_kd
                db = jnp.where(last_row, db + (_colsum(dkd_kd) + debl * ebl), db)
                dla_ref[rr, sl] = _chunk_cumsum(db, False)
                dst[p] = ds_new
            return dwgn
        dwgn = lax.fori_loop(0, per_step, chunk, jnp.zeros((1, 128), F32))
        _acc_add(dwgn_ref, first, dwgn)

    rev = lambda c: pl.BlockSpec((per_step * BLK, c), lambda i: (steps - 1 - i, 0))
    f32 = lambda c: jax.ShapeDtypeStruct((rows, c), F32)
    return pl.pallas_call(
        body, name="gla_bwd", grid=(steps,),
        in_specs=[rev(512), rev(512), rev(256), rev(256), rev(512), rev(512), rev(256),
                  pl.BlockSpec((per_step, 2, 128, 128), lambda i: (steps - 1 - i, 0, 0, 0)), VMEM_SPEC],
        out_specs=[rev(256), rev(256), rev(512), rev(512), rev(256), _acc_spec(128)],
        out_shape=[f32(256), f32(256), f32(512), f32(512), f32(256), jax.ShapeDtypeStruct((8, 128), F32)],
        scratch_shapes=[pltpu.VMEM((2, 128, 128), F32)],
        compiler_params=_params(("arbitrary",)),
    )(dcat, o_all, gq, gk, gv, gg, bc, sp, wgn)


def _swa_masks(i):
    t = lax.broadcasted_iota(jnp.int32, (BLK, BLK), 0)
    c = lax.broadcasted_iota(jnp.int32, (BLK, BLK), 1)
    own_side = c <= t
    band_ok = i >= jnp.where(own_side, 1, 2)
    meta_ok = (c % N_META) <= jnp.where(i >= 1, N_META, t - PAD_ROWS)
    return own_side, band_ok, meta_ok, c // N_META


def _swa_blocks(ref, i):
    prev = pl.multiple_of(jnp.maximum(i - 1, 0) * BLK, BLK)
    own = pl.multiple_of(i * BLK, BLK)
    return jnp.concatenate([ref[pl.ds(prev, BLK), :], ref[pl.ds(own, BLK), :]], axis=0), prev, own


def _swa_meta_operand(ref):
    blk = ref[0:BLK, :]
    swapped = pltpu.roll(blk, 64, 1)
    lo = jnp.where(_half_mask(128, 0), blk, swapped)
    hi = jnp.where(_half_mask(128, 1), blk, swapped)
    meta = jnp.concatenate([lo, lo, hi, hi], axis=1)[PAD_ROWS:BLK, :]
    tiled = jnp.concatenate([meta] * SWA_HEADS, axis=0)
    j = lax.broadcasted_iota(jnp.int32, tiled.shape, 0)
    lane = lax.broadcasted_iota(jnp.int32, tiled.shape, 1)
    return jnp.where(j // N_META == lane // SWA_HD, tiled, jnp.zeros_like(tiled))


def _swa_meta_fold(acc):
    out = jnp.zeros((N_META, 128), F32)
    for hd in range(SWA_HEADS):
        half, kv = hd % 2, hd // 4
        piece = acc[N_META * hd:N_META * (hd + 1), 128 * (hd // 2):128 * (hd // 2) + 128]
        piece = jnp.where(_half_mask(128, half), piece, 0.0)
        out = out + (piece if half == kv else pltpu.roll(piece, 64, 1))
    return out


def _by_head(group, per_head):
    out = jnp.zeros((BLK, BLK), F32)
    for hd, v in enumerate(per_head):
        out = jnp.where(group == hd, v, out)
    return out


def _place(x, kv):
    if kv == 0:
        lo = jnp.where(_half_mask(128, 0), x, jnp.zeros_like(x))
        return lo, pltpu.roll(lo, 64, 1)
    hi = jnp.where(_half_mask(128, 1), x, jnp.zeros_like(x))
    return pltpu.roll(hi, 64, 1), hi


def _swa_fwd(sq, sk, sv, sinks, wn):
    rows = sq.shape[0]
    nb = rows // BLK
    per_step = _blocks_per_step(nb)
    scale = SWA_HD ** -0.5

    def body(q_ref, k_ref, v_ref, sink_ref, wn_ref, o_ref, cat_ref, lse_ref, kp, vp):
        step = pl.program_id(0)

        @pl.when(step == 0)
        def _():
            kp[...] = _swa_meta_operand(k_ref)
            vp[...] = _swa_meta_operand(v_ref)

        def one_block(c, carry):
            i = step * per_step + c
            rr = pl.ds(pl.multiple_of(c * BLK, BLK), BLK)
            own_side, band_ok, meta_ok, group = _swa_masks(i)
            k2, _, _ = _swa_blocks(k_ref, i)
            v2, _, _ = _swa_blocks(v_ref, i)
            kz = (_place(k2, 0), _place(k2, 1))
            vz = (_place(v2, 0), _place(v2, 1))
            q_all = q_ref[rr, :]
            s_meta = jnp.where(meta_ok, _nt(q_all, kp[...]) * scale, NEG_INF)
            s_band, m = [], []
            for hd in range(SWA_HEADS):
                kv, half = hd // 4, hd % 2
                q_pair = q_all[:, 128 * (hd // 2):128 * (hd // 2) + 128]
                s2 = _nt(q_pair, kz[kv][half])
                s = jnp.where(band_ok, jnp.where(own_side, s2[:, BLK:], s2[:, :BLK]) * scale, NEG_INF)
                top = jnp.maximum(jnp.max(s, axis=-1, keepdims=True),
                                  jnp.max(jnp.where(group == hd, s_meta, NEG_INF), axis=-1, keepdims=True))
                s_band.append(s)
                m.append(jnp.maximum(top, sink_ref[0, hd]))
            e_meta = jnp.exp(s_meta - _by_head(group, m))
            o_meta = _nn(e_meta.astype(BF16), vp[...])
            outs = []
            for pr in range(4):
                o_pair = o_meta[:, 128 * pr:128 * pr + 128]
                rden = []
                for half in range(2):
                    hd = 2 * pr + half
                    kv = hd // 4
                    e = jnp.exp(s_band[hd] - m[hd])
                    den = (jnp.sum(e, axis=-1, keepdims=True)
                           + jnp.sum(jnp.where(group == hd, e_meta, 0.0), axis=-1, keepdims=True)
                           + jnp.exp(sink_ref[0, hd] - m[hd]))
                    lse_ref[rr, hd:hd + 1] = m[hd] + jnp.log(den)
                    rden.append(1.0 / den)
                    e2 = jnp.concatenate([jnp.where(own_side, 0.0, e), jnp.where(own_side, e, 0.0)], axis=1).astype(BF16)
                    o_pair = o_pair + _nn(e2, vz[kv][half])
                outs.append(o_pair * jnp.where(_half_mask(128, 0), rden[0], rden[1]))
            o = jnp.concatenate(outs, axis=1)
            o_ref[rr, :] = o
            on, _ = _rms(o)
            cat_ref[rr, :] = (on * wn_ref[...]).astype(BF16)
            return carry
        lax.fori_loop(0, per_step, one_block, 0)

    return pl.pallas_call(
        body, name="swa_fwd", grid=(nb // per_step,),
        in_specs=[_row_spec(per_step * BLK, 512), VMEM_SPEC, VMEM_SPEC, SMEM_SPEC, VMEM_SPEC],
        out_specs=[_row_spec(per_step * BLK, 512), _row_spec(per_step * BLK, 512), _row_spec(per_step * BLK, SWA_HEADS)],
        out_shape=[jax.ShapeDtypeStruct((rows, 512), F32), jax.ShapeDtypeStruct((rows, 512), BF16),
                   jax.ShapeDtypeStruct((rows, SWA_HEADS), F32)],
        scratch_shapes=[pltpu.VMEM((BLK, 512), BF16), pltpu.VMEM((BLK, 512), BF16)],
        compiler_params=_params(("arbitrary",)),
    )(sq, sk, sv, sinks, wn)


def _swa_bwd(dcat, o_all, sq, sk, sv, lse, sinks, wn):
    rows = sq.shape[0]
    nb = rows // BLK
    per_step = _blocks_per_step(nb)
    steps = nb // per_step
    scale = SWA_HD ** -0.5

    def body(dc_ref, o_ref, q_ref, k_ref, v_ref, lse_ref, sink_ref, wn_ref, dq_ref, dk_ref, dv_ref, dsink_ref, dwn_ref,
             kp, vp, dkp, dvp):
        step = pl.program_id(0)

        @pl.when(step == 0)
        def _():
            dk_ref[...] = jnp.zeros_like(dk_ref)
            dv_ref[...] = jnp.zeros_like(dv_ref)
            dkp[...] = jnp.zeros_like(dkp)
            dvp[...] = jnp.zeros_like(dvp)
            kp[...] = _swa_meta_operand(k_ref)
            vp[...] = _swa_meta_operand(v_ref)

        def one_block(c, carry):
            i = step * per_step + c
            rr = pl.ds(pl.multiple_of(c * BLK, BLK), BLK)
            first = i == 0
            own_side, band_ok, meta_ok, group = _swa_masks(i)
            k2, prev, own = _swa_blocks(k_ref, i)
            v2, _, _ = _swa_blocks(v_ref, i)
            kz = (_place(k2, 0), _place(k2, 1))
            vz = (_place(v2, 0), _place(v2, 1))
            o = o_ref[rr, :]
            on, ro = _rms(o)
            dc = dc_ref[rr, :]
            _acc_add(dwn_ref, first, _colsum(dc * on))
            do = _rms_bwd(on, ro, wn_ref[...], dc)
            do_o = do * o
            do16 = do.astype(BF16)
            q_all = q_ref[rr, :]
            lse = [lse_ref[rr, hd:hd + 1] for hd in range(SWA_HEADS)]
            delta = [jnp.sum(jnp.where(_half_mask(128, hd % 2), do_o[:, 128 * (hd // 2):128 * (hd // 2) + 128], 0.0),
                             axis=-1, keepdims=True) for hd in range(SWA_HEADS)]
            s_meta = jnp.where(meta_ok, _nt(q_all, kp[...]) * scale, NEG_INF)
            p_meta = jnp.exp(s_meta - _by_head(group, lse))
            ds_meta16 = (p_meta * (_nt(do16, vp[...]) - _by_head(group, delta)) * scale).astype(BF16)
            dq_meta = _nn(ds_meta16, kp[...])
            dkp[...] += _tn(ds_meta16, q_all)
            dvp[...] += _tn(p_meta.astype(BF16), do16)
            own2 = jnp.concatenate([own_side.astype(jnp.int32)] * 2, axis=0) > 0
            ok2 = jnp.concatenate([band_ok.astype(jnp.int32)] * 2, axis=0) > 0

            def window(x2):
                return jnp.where(own2, x2[:, BLK:], x2[:, :BLK])

            def unwindow(x):
                return jnp.concatenate([jnp.where(own2, 0.0, x), jnp.where(own2, x, 0.0)], axis=1).astype(BF16)
            lane8 = lax.broadcasted_iota(jnp.int32, (1, 128), 1)
            dsink = jnp.zeros((1, 128), F32)
            dq_pairs = [dq_meta[:, 128 * pr:128 * pr + 128] for pr in range(4)]
            dk2 = [[None, None], [None, None]]
            dv2 = [[None, None], [None, None]]
            for kv in range(2):
                for half in range(2):
                    heads, pairs = (4 * kv + half, 4 * kv + 2 + half), (2 * kv, 2 * kv + 1)
                    q_s = jnp.concatenate([q_all[:, 128 * pr:128 * pr + 128] for pr in pairs], axis=0)
                    do_s = jnp.concatenate([do16[:, 128 * pr:128 * pr + 128] for pr in pairs], axis=0)
                    lse_s = jnp.concatenate([lse[hd] for hd in heads], axis=0)
                    delta_s = jnp.concatenate([delta[hd] for hd in heads], axis=0)
                    s = jnp.where(ok2, window(_nt(q_s, kz[kv][half])) * scale, NEG_INF)
                    prob = jnp.exp(s - lse_s)
                    for hd in heads:
                        dsink = dsink + jnp.where(lane8 == hd, -jnp.sum(jnp.exp(sink_ref[0, hd] - lse[hd]) * delta[hd]), 0.0)
                    ds2 = unwindow(prob * (window(_nt(do_s, vz[kv][half])) - delta_s) * scale)
                    dq_s = _nn(ds2, kz[kv][half])
                    dq_pairs[pairs[0]] = dq_pairs[pairs[0]] + dq_s[:BLK]
                    dq_pairs[pairs[1]] = dq_pairs[pairs[1]] + dq_s[BLK:]
                    dk2[kv][half] = _tn(ds2, q_s)
                    dv2[kv][half] = _tn(unwindow(prob), do_s)
            dq_ref[rr, :] = jnp.concatenate(dq_pairs, axis=1)
            _acc_add(dsink_ref, first, dsink)
            for ref, acc2 in ((dk_ref, dk2), (dv_ref, dv2)):
                tot = jnp.zeros((2 * BLK, 128), F32)
                for kv in range(2):
                    for half in range(2):
                        part = jnp.where(_half_mask(128, half), acc2[kv][half], 0.0)
                        tot = tot + (part if half == kv else pltpu.roll(part, 64, 1))
                ref[pl.ds(prev, BLK), :] += tot[:BLK]
                ref[pl.ds(own, BLK), :] += tot[BLK:]
            return carry
        lax.fori_loop(0, per_step, one_block, 0)

        @pl.when(step == steps - 1)
        def _():
            dk_ref[PAD_ROWS:BLK, :] += _swa_meta_fold(dkp[...])
            dv_ref[PAD_ROWS:BLK, :] += _swa_meta_fold(dvp[...])

    full = pl.BlockSpec((rows, 128), lambda i: (0, 0))
    blocks = lambda cols: _row_spec(per_step * BLK, cols)
    return pl.pallas_call(
        body, name="swa_bwd", grid=(steps,),
        in_specs=[blocks(512), blocks(512), blocks(512), VMEM_SPEC, VMEM_SPEC, blocks(SWA_HEADS), SMEM_SPEC, VMEM_SPEC],
        out_specs=[blocks(512), full, full, _acc_spec(128), _acc_spec(512)],
        out_shape=[jax.ShapeDtypeStruct((rows, 512), F32), jax.ShapeDtypeStruct((rows, 128), F32),
                   jax.ShapeDtypeStruct((rows, 128), F32), jax.ShapeDtypeStruct((8, 128), F32),
                   jax.ShapeDtypeStruct((8, 512), F32)],
        scratch_shapes=[pltpu.VMEM((BLK, 512), BF16), pltpu.VMEM((BLK, 512), BF16),
                        pltpu.VMEM((BLK, 512), F32), pltpu.VMEM((BLK, 512), F32)],
        compiler_params=_params(("arbitrary",)),
    )(dcat, o_all, sq, sk, sv, lse, sinks, wn)


def _mix_out_bwd(dh, m, wout, gpost):
    rows = dh.shape[0]
    tm = _row_tile(rows)

    def body(dh_ref, m_ref, w_ref, g_ref, dcg_ref, dcs_ref, dm_ref, dg_ref):
        first = pl.program_id(0) == 0
        dhv = dh_ref[...]
        mn, rm = _rms(m_ref[...])
        _acc_add(dg_ref, first, _colsum(dhv * mn))
        dm16 = _rms_bwd(mn, rm, g_ref[...], dhv).astype(BF16)
        dm_ref[...] = dm16
        dcat = _nt(dm16, w_ref[...])
        dcg_ref[...] = dcat[:, 0:512]
        dcs_ref[...] = dcat[:, 512:1024]

    row_f32 = _row_spec(tm, D_MODEL)
    return pl.pallas_call(
        body, name="mix_out_bwd", grid=(rows // tm,),
        in_specs=[row_f32, row_f32, VMEM_SPEC, VMEM_SPEC],
        out_specs=[_row_spec(tm, 512), _row_spec(tm, 512), row_f32, _acc_spec(D_MODEL)],
        out_shape=[jax.ShapeDtypeStruct((rows, 512), F32), jax.ShapeDtypeStruct((rows, 512), F32),
                   jax.ShapeDtypeStruct((rows, D_MODEL), BF16), jax.ShapeDtypeStruct((8, D_MODEL), F32)],
        compiler_params=_params(("arbitrary",)),
    )(dh, m, wout, gpost)


def _mix_in_bwd(dh_out, h, g, win_p, wa2_p, cos, sin, loga, ga, dgq, dgk, dgv, dgg, dsq, dsk, dsv, dloga):
    rows = h.shape[0]
    tm = _row_tile(rows)

    def body(dho_ref, h_ref, g_ref, win_ref, wa2_ref, cos_ref, sin_ref, loga_ref, ga_ref,
             dgq_ref, dgk_ref, dgv_ref, dgg_ref, dsq_ref, dsk_ref, dsv_ref, dla_ref,
             dh_ref, dproj_ref, dwa2_ref, dg_ref, dba_ref):
        first = pl.program_id(0) == 0
        dz = dla_ref[...] * (1.0 / GLA_TAU) * (1.0 - jnp.exp(GLA_TAU * loga_ref[...]))
        _acc_add(dba_ref, first, _colsum(dz))
        dga = _nt(dz, wa2_ref[...])
        pa = _tn(ga_ref[...], dz)
        c1, s1 = cos_ref[...], sin_ref[...]
        c4 = jnp.concatenate([c1, c1, c1, c1], axis=1)
        s4 = jnp.concatenate([s1, s1, s1, s1], axis=1)
        dq_r, dk_r = dsq_ref[...], dsk_ref[...]
        dsq = dq_r * c4 - _rot_half(dq_r * s4)
        dsk = dk_r * c1 - _rot_half(dk_r * s1)
        dproj16 = jnp.concatenate(
            [dgq_ref[...], dgk_ref[...], dgv_ref[...], dgg_ref[...], dsq, dsk, dsv_ref[...], dga], axis=1).astype(BF16)
        dproj_ref[...] = dproj16
        dn = _nn(dproj16, win_ref[...])

        @pl.when(first)
        def _():
            dwa2_ref[...] = pa

        @pl.when(jnp.logical_not(first))
        def _():
            dwa2_ref[...] += pa
        hn, rh = _rms(h_ref[...])
        _acc_add(dg_ref, first, _colsum(dn * hn))
        dh_ref[...] = dho_ref[...] + _rms_bwd(hn, rh, g_ref[...], dn)

    rs = lambda c: _row_spec(tm, c)
    return pl.pallas_call(
        body, name="mix_in_bwd", grid=(rows // tm,),
        in_specs=[rs(D_MODEL), rs(D_MODEL), VMEM_SPEC, VMEM_SPEC, VMEM_SPEC, rs(128), rs(128), rs(256), rs(128),
                  rs(256), rs(256), rs(512), rs(512), rs(512), rs(128), rs(128), rs(256)],
        out_specs=[rs(D_MODEL), rs(P_END), pl.BlockSpec((128, 256), lambda i: (0, 0)), _acc_spec(D_MODEL), _acc_spec(256)],
        out_shape=[jax.ShapeDtypeStruct((rows, D_MODEL), F32), jax.ShapeDtypeStruct((rows, P_END), BF16),
                   jax.ShapeDtypeStruct((128, 256), F32), jax.ShapeDtypeStruct((8, D_MODEL), F32),
                   jax.ShapeDtypeStruct((8, 256), F32)],
        compiler_params=_params(("arbitrary",)),
    )(dh_out, h, g, win_p, wa2_p, cos, sin, loga, ga, dgq, dgk, dgv, dgg, dsq, dsk, dsv, dloga)


def _rope_tables(rows):
    pos = (jnp.arange(rows, dtype=jnp.int32) - PAD_ROWS).astype(F32)
    inv_freq = 1.0 / (ROPE_THETA ** (jnp.arange(0, SWA_HD, 2, dtype=F32) / SWA_HD))
    ang = pos[:, None] * inv_freq[None, :]
    return jnp.tile(jnp.cos(ang), (1, 4)), jnp.tile(jnp.sin(ang), (1, 4))


def _local_step(x, tgt, front, w, late_weights=None, on_grads=None, on_small=None):
    cos, sin = _rope_tables(x.shape[0] + BLK)
    g = {}

    def tell(group, names):
        for nm in names:
            g[nm] = grads_now[nm]
        return None if on_grads is None else on_grads(group, {nm: grads_now[nm] for nm in names})

    h1, a1, b1, s1, f1 = _ffn_fwd(x, w["ffn1_pre"], w["wg1"], w["wu1"], w["wd1"], w["ffn1_post"], front=front)
    if late_weights is not None:
        w = {**w, **late_weights("win", f1)}
    gq, gk, gv, gg, sq, sk, sv, ga, loga, bc, n2 = _mix_in(h1, w["mix_pre"], w["win"], w["wa2"], w["b_a"], cos, sin)
    o_g, cat_g, sp = _gla_fwd(gq, gk, gv, gg, bc, w["gla_norm"])
    o_s, cat_s, lse = _swa_fwd(sq, sk, sv, w["sinks"], w["swa_norm"])
    if late_weights is not None:
        w = {**w, **late_weights("rest", lse)}
    h2, m, a2, b2, s2, f2, dy, loss = _ffn_fwd(h1, w["ffn2_pre"], w["wg2"], w["wu2"], w["wd2"], w["ffn2_post"], tgt,
                                               mixed=(cat_g, cat_s, w["wout"], w["mix_post"]))
    dh2, da, db, df, n3, g["ffn2_pre"], g["ffn2_post"] = _ffn_bwd_act(
        dy, h2, a2, b2, f2, w["ffn2_pre"], w["ffn2_post"], w["wg2"], w["wu2"], w["wd2"], "ffn2_bwd_act")
    grads_now = dict(wd2=_wgrad(s2, df, "ffn2_wgrad_down"), wg2=_wgrad(da, n3, "ffn2_wgrad_gate"),
                     wu2=_wgrad(db, n3, "ffn2_wgrad_up"))
    tok = tell("ffn2", ("wd2", "wg2", "wu2"))
    dcg, dcs, dm, g["mix_post"] = _mix_out_bwd(dh2, m, w["wout"], w["mix_post"] + (0.0 if tok is None else tok[0, 0]))
    dsq, dsk, dsv, g["sinks"], g["swa_norm"] = _swa_bwd(dcs, o_s, sq, sk, sv, lse, w["sinks"], w["swa_norm"])
    dgq, dgk, dgv, dgg, dloga, g["gla_norm"] = _gla_bwd(dcg, o_g, gq, gk, gv, gg, bc, sp, w["gla_norm"])
    dh1, dproj, g["wa2"], g["mix_pre"], g["b_a"] = _mix_in_bwd(
        dh2, h1, w["mix_pre"], w["win"], w["wa2"], cos, sin, loga, ga, dgq, dgk, dgv, dgg, dsq, dsk, dsv, dloga)
    dh0, da, db, df, n1, g["ffn1_pre"], g["ffn1_post"] = _ffn_bwd_act(
        dh1, x, a1, b1, f1, w["ffn1_pre"], w["ffn1_post"], w["wg1"], w["wu1"], w["wd1"], "ffn1_bwd_act", front=front)
    tok = None if on_small is None else on_small(loss[0, 0], dh0, g)
    grads_now = dict(wd1=_wgrad(s1, df, "ffn1_wgrad_down", after=tok))
    tok = tell("ffn1_down", ("wd1",))
    grads_now = dict(wg1=_wgrad(da, n1, "ffn1_wgrad_gate", after=tok))
    tok = tell("ffn1_gate", ("wg1",))
    grads_now = dict(wu1=_wgrad(db, n1, "ffn1_wgrad_up", after=tok))
    tok = tell("ffn1_up", ("wu1",))
    grads_now = dict(win=_wgrad(dproj, n2, "win_wgrad", after=tok),
                     wout=jnp.concatenate([_wgrad(cat_g, dm, "wout_wgrad_gla", after=tok),
                                           _wgrad(cat_s, dm, "wout_wgrad_swa", after=tok)], axis=0))
    tell("mix", ("wout", "win"))
    return loss[0, 0], dh0, g


def _win_pad_rows(win_t):
    pad = jnp.zeros((P_END - P_GA - 16, win_t.shape[1]), win_t.dtype)
    return jnp.concatenate([win_t[0:1536], win_t[1552:2320], win_t[1536:1552], pad], axis=0)


def _win_unpad_rows(win_p):
    return jnp.concatenate([win_p[0:1536], win_p[P_GA:P_GA + 16], win_p[1536:P_GA]], axis=0)


def _place_on_mesh():
    return lax.axis_index("x"), lax.axis_index("y"), lax.axis_index("c")


def _dev_index(px, py, pc):
    return 4 * px + 2 * py + pc


def _other_devices(x, y, c):
    flip = lambda v, f: 1 - v if f else v
    return [(flip(x, fx), flip(y, fy), flip(c, fc)) for fx in (0, 1) for fy in (0, 1) for fc in (0, 1)][1:]


def _all_gather(shards):
    n = len(shards)

    def body(*refs):
        ins, outs = refs[:n], refs[n:2 * n]
        zeros_ref, send_sems, recv_sems, local_sems = refs[2 * n:]
        zeros_ref[...] = jnp.zeros_like(zeros_ref)
        x, y, c = _place_on_mesh()
        me, sibling = (x, y, c), (x, y, 1 - c)
        chips = [(1 - x, y), (x, 1 - y), (1 - x, 1 - y)]

        def rows(k, px, py, pc):
            r = ins[k].shape[0]
            return outs[k].at[pl.ds(pl.multiple_of(_dev_index(px, py, pc) * r, 8), r), :]

        def copy(k, slot, block, to, src=None):
            return pltpu.make_async_remote_copy(
                src_ref=rows(k, *block) if src is None else src, dst_ref=rows(k, *block),
                send_sem=send_sems.at[k, slot], recv_sem=recv_sems.at[k, slot], device_id=to, device_id_type=MESH)

        local = [pltpu.make_async_copy(ins[k], rows(k, *me), local_sems.at[k]) for k in range(n)]
        sends = []
        for k in range(n):
            local[k].start()
            sends.append(copy(k, 0, me, sibling, src=ins[k]))
            sends += [copy(k, 1 + j, me, (*chip, c), src=ins[k]) for j, chip in enumerate(chips)]
        for cp in sends:
            cp.start()
        for k in range(n):
            for j, chip in enumerate(chips):
                copy(k, 1 + j, (*chip, c), me).wait_recv()
                passed = copy(k, 4 + j, (*chip, c), sibling)
                passed.start()
                sends.append(passed)
        for k in range(n):
            copy(k, 0, sibling, me).wait_recv()
            for j, chip in enumerate(chips):
                copy(k, 4 + j, (*chip, 1 - c), me).wait_recv()
        for cp in sends:
            cp.wait_send()
        for cp in local:
            cp.wait()

    return pl.pallas_call(
        body, name="all_gather_weights",
        in_specs=[ANY_SPEC] * n, out_specs=[ANY_SPEC] * n + [VMEM_SPEC],
        out_shape=[jax.ShapeDtypeStruct((N_DEV * s.shape[0], s.shape[1]), s.dtype) for s in shards]
        + [jax.ShapeDtypeStruct((8, 128), F32)],
        scratch_shapes=[pltpu.SemaphoreType.DMA((n, 7)), pltpu.SemaphoreType.DMA((n, 7)), pltpu.SemaphoreType.DMA((n,))],
    )(*shards)


HBM_SPEC = pl.BlockSpec(memory_space=pltpu.HBM)
SEM_SPEC = pl.BlockSpec(memory_space=pltpu.SEMAPHORE)
DATAFLOW = pltpu.SideEffectType.DATAFLOW_SIDE_EFFECTING


GATHER, SCATTER, SCATTER_CHIPS = "gather", "scatter", "scatter among chips"


def _exchange_peers(kind):
    x, y, c = _place_on_mesh()
    if kind == SCATTER_CHIPS:
        peers = [(1 - x, y, c), (x, 1 - y, c), (1 - x, 1 - y, c)]
        return peers, [2 * p[0] + p[1] for p in peers], 2 * x + y, 4
    peers = _other_devices(x, y, c)
    return peers, [_dev_index(*p) for p in peers], _dev_index(x, y, c), N_DEV


def _exchange_copies(srcs, lands, send_sems, recv_sems, own_sems, kind, arriving):
    peers, theirs, me, blocks = _exchange_peers(kind)
    remote, local = [], []
    for k, (src, land) in enumerate(zip(srcs, lands)):
        r = land.shape[0] // blocks

        def block(ref, d):
            return ref.at[pl.ds(pl.multiple_of(d * r, 8), r), :]

        for f, (peer, him) in enumerate(zip(peers, theirs)):
            mine, his = (him, me) if arriving else (me, him)
            sem = len(peers) * k + f
            remote.append(pltpu.make_async_remote_copy(
                src_ref=src if kind == GATHER else block(src, his), dst_ref=block(land, mine),
                send_sem=send_sems.at[sem], recv_sem=recv_sems.at[sem], device_id=peer, device_id_type=MESH))
        local.append(pltpu.make_async_copy(src if kind == GATHER else block(src, me), block(land, me), own_sems.at[k]))
    return remote, local


def _exchange_start(srcs, kind, name):
    n = len(srcs)
    lands = [lax.empty((N_DEV * s.shape[0], s.shape[1]) if kind == GATHER else s.shape, s.dtype) for s in srcs]
    sems = (3 if kind == SCATTER_CHIPS else 7) * n

    def body(*refs):
        remote, local = _exchange_copies(refs[:n], refs[n:2 * n], *refs[2 * n:2 * n + 3], kind, False)
        for cp in remote + local:
            cp.start()
        refs[-1][...] = jnp.zeros_like(refs[-1])

    both = list(srcs) + list(lands)
    outs = pl.pallas_call(
        body, name=name,
        out_shape=(pltpu.SemaphoreType.DMA((sems,)), pltpu.SemaphoreType.DMA((sems,)), pltpu.SemaphoreType.DMA((n,)),
                   *[pltpu.HBM(a.shape, a.dtype) for a in both], jax.ShapeDtypeStruct((8, 128), F32)),
        in_specs=[HBM_SPEC] * (2 * n), out_specs=(SEM_SPEC, SEM_SPEC, SEM_SPEC, *[HBM_SPEC] * (2 * n), VMEM_SPEC),
        input_output_aliases={i: 3 + i for i in range(2 * n)},
        compiler_params=pltpu.CompilerParams(has_side_effects=DATAFLOW),
    )(*[pltpu.with_memory_space_constraint(a, pltpu.HBM) for a in both])
    return outs[0:3], outs[3:3 + n], outs[3 + n:3 + 2 * n], outs[-1]


def _exchange_wait(started, kind, after, name):
    sems, srcs, lands, _ = started
    n = len(srcs)

    def body(*refs):
        args = (refs[:n], refs[n:2 * n], *refs[2 * n:2 * n + 3], kind)
        going, local = _exchange_copies(*args, False)
        for cp in going:
            cp.wait_send()
        for cp in local:
            cp.wait()
        for cp in _exchange_copies(*args, True)[0]:
            cp.wait_recv()

    both = list(srcs) + list(lands)
    outs = pl.pallas_call(
        body, name=name, out_shape=[pltpu.HBM(a.shape, a.dtype) for a in both],
        in_specs=[HBM_SPEC] * (2 * n) + [SEM_SPEC, SEM_SPEC, SEM_SPEC, ANY_SPEC], out_specs=[HBM_SPEC] * (2 * n),
        input_output_aliases={i: i for i in range(2 * n)},
        compiler_params=pltpu.CompilerParams(has_side_effects=DATAFLOW),
    )(*both, *sems, after)
    return outs[n:]


def _sibling_reduce(part, name):
    r, cols = part.shape[0] // N_DEV, part.shape[1]

    def body(p_ref, o_ref, mine, got, send_sems, recv_sems, own_sems):
        x, y, c = _place_on_mesh()

        def block(d):
            return p_ref.at[pl.ds(pl.multiple_of(d * r, 8), r), :]
        swaps = [pltpu.make_async_remote_copy(
            src_ref=block(2 * j + 1 - c), dst_ref=got.at[j], send_sem=send_sems.at[j], recv_sem=recv_sems.at[j],
            device_id=(x, y, 1 - c), device_id_type=MESH) for j in range(4)]
        keeps = [pltpu.make_async_copy(block(2 * j + c), mine.at[j], own_sems.at[j]) for j in range(4)]
        for cp in swaps + keeps:
            cp.start()
        for j in range(4):
            keeps[j].wait()
            swaps[j].wait()
            o_ref[pl.ds(j * r, r), :] = (mine[j].astype(F32) + got[j].astype(F32)).astype(o_ref.dtype)

    return pl.pallas_call(
        body, name=name, in_specs=[ANY_SPEC], out_specs=VMEM_SPEC,
        out_shape=jax.ShapeDtypeStruct((4 * r, cols), part.dtype),
        scratch_shapes=[pltpu.VMEM((4, r, cols), part.dtype), pltpu.VMEM((4, r, cols), part.dtype),
                        pltpu.SemaphoreType.DMA((4,)), pltpu.SemaphoreType.DMA((4,)), pltpu.SemaphoreType.DMA((4,))],
        compiler_params=pltpu.CompilerParams(vmem_limit_bytes=32 << 20),
    )(part)


def _sum_partials(parts, name, blocks=N_DEV):
    n = len(parts)

    def body(*refs):
        ins, outs = refs[:n], refs[n:]
        first = pl.program_id(0) == 0
        for i_ref, o_ref in zip(ins, outs):
            v = i_ref[...].astype(F32)

            @pl.when(first)
            def _():
                o_ref[...] = v

            @pl.when(jnp.logical_not(first))
            def _():
                o_ref[...] += v

    shapes = [(p.shape[0] // blocks, p.shape[1]) for p in parts]
    return pl.pallas_call(
        body, name=name, grid=(blocks,),
        in_specs=[pl.BlockSpec(s, lambda j: (j, 0)) for s in shapes],
        out_specs=[pl.BlockSpec(s, lambda j: (0, 0)) for s in shapes],
        out_shape=[jax.ShapeDtypeStruct(s, F32) for s in shapes],
        compiler_params=_params(("arbitrary",)),
    )(*parts)


def _adamw_update(w, g, m, v):
    m = ADAM_B1 * m + (1.0 - ADAM_B1) * g
    v = ADAM_B2 * v + (1.0 - ADAM_B2) * (g * g)
    m_hat = m * (1.0 / (1.0 - ADAM_B1 ** ADAM_STEP))
    v_hat = v * (1.0 / (1.0 - ADAM_B2 ** ADAM_STEP))
    return -ADAM_LR * (m_hat / (jnp.sqrt(v_hat) + ADAM_EPS) + ADAM_WD * w), m, v


def _sum_adamw(parts, w, m, v, blocks, name):
    shape = w.shape

    def body(p_ref, w_ref, m_ref, v_ref, g_ref, d_ref, mo_ref, vo_ref):
        j = pl.program_id(0)
        part = p_ref[...].astype(F32)

        @pl.when(j == 0)
        def _():
            g_ref[...] = part

        @pl.when(j > 0)
        def _():
            g_ref[...] += part

        @pl.when(j == blocks - 1)
        def _():
            d_ref[...], mo_ref[...], vo_ref[...] = _adamw_update(w_ref[...], g_ref[...], m_ref[...], v_ref[...])

    held = pl.BlockSpec(shape, lambda j: (0, 0))
    return pl.pallas_call(
        body, name=name, grid=(blocks,),
        in_specs=[pl.BlockSpec(shape, lambda j: (j, 0)), held, held, held],
        out_specs=[held] * 4, out_shape=[jax.ShapeDtypeStruct(shape, F32)] * 4,
        compiler_params=_params(("arbitrary",)),
    )(parts, w, m, v)


def _adamw(ws, gs, ms, vs, name):
    n = len(ws)

    def body(*refs):
        w_r, g_r, m_r, v_r = refs[:n], refs[n:2 * n], refs[2 * n:3 * n], refs[3 * n:4 * n]
        d_o, m_o, v_o = refs[4 * n:5 * n], refs[5 * n:6 * n], refs[6 * n:7 * n]
        for k in range(n):
            d_o[k][...], m_o[k][...], v_o[k][...] = _adamw_update(w_r[k][...], g_r[k][...], m_r[k][...], v_r[k][...])

    shapes = [jax.ShapeDtypeStruct(w.shape, F32) for w in ws]
    outs = pl.pallas_call(
        body, name=name, in_specs=[VMEM_SPEC] * (4 * n), out_specs=[VMEM_SPEC] * (3 * n), out_shape=shapes * 3,
        compiler_params=pltpu.CompilerParams(vmem_limit_bytes=56 << 20),
    )(*ws, *gs, *ms, *vs)
    return outs[:n], outs[n:2 * n], outs[2 * n:]


WEIGHT_NAMES = ("meta_tokens", "ffn1_pre_norm", "ffn1_w_gate", "ffn1_w_up", "ffn1_w_down", "ffn1_post_norm", "mix_pre_norm",
                "w_in", "gla_w_a2", "gla_b_a", "gla_out_norm", "swa_sinks", "swa_out_norm", "w_out", "mix_post_norm",
                "ffn2_pre_norm", "ffn2_w_gate", "ffn2_w_up", "ffn2_w_down", "ffn2_post_norm")
WIN_SHARD = D_IN // N_DEV
WIN_SHARD_PAD = 304
SLAB_VECTORS = ("ffn1_pre", "ffn1_post", "mix_pre", "mix_post", "ffn2_pre", "ffn2_post")
SLAB_ROWS = 32


def kernel(x, meta_tokens, ffn1_pre_norm, ffn1_w_gate, ffn1_w_up, ffn1_w_down, ffn1_post_norm, mix_pre_norm, w_in, gla_w_a2, gla_b_a, gla_out_norm, swa_sinks, swa_out_norm, w_out, mix_post_norm, ffn2_pre_norm, ffn2_w_gate, ffn2_w_up, ffn2_w_down, ffn2_post_norm, loss_target, m_meta_tokens, m_ffn1_pre_norm, m_ffn1_w_gate, m_ffn1_w_up, m_ffn1_w_down, m_ffn1_post_norm, m_mix_pre_norm, m_w_in, m_gla_w_a2, m_gla_b_a, m_gla_out_norm, m_swa_sinks, m_swa_out_norm, m_w_out, m_mix_post_norm, m_ffn2_pre_norm, m_ffn2_w_gate, m_ffn2_w_up, m_ffn2_w_down, m_ffn2_post_norm, v_meta_tokens, v_ffn1_pre_norm, v_ffn1_w_gate, v_ffn1_w_up, v_ffn1_w_down, v_ffn1_post_norm, v_mix_pre_norm, v_w_in, v_gla_w_a2, v_gla_b_a, v_gla_out_norm, v_swa_sinks, v_swa_out_norm, v_w_out, v_mix_post_norm, v_ffn2_pre_norm, v_ffn2_w_gate, v_ffn2_w_up, v_ffn2_w_down, v_ffn2_post_norm):
    given = dict(locals())
    W = {n: given[n] for n in WEIGHT_NAMES}
    M = {n: given["m_" + n] for n in WEIGHT_NAMES}
    V = {n: given["v_" + n] for n in WEIGHT_NAMES}
    dev = _dev_index(*_place_on_mesh())

    def t16(w):
        return w[0].T.astype(BF16)

    small = jnp.concatenate([W["meta_tokens"], jnp.pad(W["gla_w_a2"][0], ((0, 0), (0, 96)))], axis=0)
    wg1, wu1, wd1, small_g, gathered_zeros = _all_gather(
        [t16(W["ffn1_w_gate"]), t16(W["ffn1_w_up"]), W["ffn1_w_down"][0].astype(BF16), small])
    def after_zero(shard, zeros):
        return shard + zeros[0:1, 0:1].astype(shard.dtype)
    win_shard = jnp.pad(t16(W["w_in"]), ((0, WIN_SHARD_PAD - WIN_SHARD), (0, 0)))
    win_shard = after_zero(win_shard, gathered_zeros)
    mid = _exchange_start([win_shard], GATHER, "gather_w_in_start")
    late_shards = [after_zero(W["w_out"][0].astype(BF16), mid[3]), t16(W["ffn2_w_gate"]), t16(W["ffn2_w_up"]),
                   W["ffn2_w_down"][0].astype(BF16)]
    late = _exchange_start(late_shards, GATHER, "gather_late_weights_start")

    def late_weights(what, after):
        if what == "win":
            win_g, = _exchange_wait(mid, GATHER, after, "gather_w_in_wait")
            win_t = win_g.reshape(N_DEV, WIN_SHARD_PAD, D_MODEL)[:, :WIN_SHARD].reshape(D_IN, D_MODEL)
            return dict(win=_win_pad_rows(win_t))
        wout, wg2, wu2, wd2 = _exchange_wait(late, GATHER, after, "gather_late_weights_wait")
        return dict(wout=wout, wg2=wg2, wu2=wu2, wd2=wd2)

    small_g = small_g.reshape(N_DEV, 32, 128)
    meta_full = small_g[:, :N_META].transpose(1, 0, 2).reshape(N_META, D_MODEL)
    wa2_full = small_g[:, N_META:, :32].transpose(1, 0, 2).reshape(16, 256)
    w = dict(
        ffn1_pre=W["ffn1_pre_norm"] + late[3][0, 0], ffn1_post=W["ffn1_post_norm"], mix_pre=W["mix_pre_norm"],
        mix_post=W["mix_post_norm"], ffn2_pre=W["ffn2_pre_norm"], ffn2_post=W["ffn2_post_norm"], b_a=W["gla_b_a"],
        gla_norm=W["gla_out_norm"], sinks=W["swa_sinks"], swa_norm=W["swa_out_norm"], wg1=wg1, wu1=wu1, wd1=wd1,
        wa2=jnp.pad(wa2_full, ((0, 112), (0, 0))))

    in_flight = []

    def on_grads(group, grads):
        parts = []
        for nm, p in grads.items():
            if nm == "win":
                p = _win_unpad_rows(p).reshape(N_DEV, WIN_SHARD, D_MODEL)
                p = jnp.pad(p, ((0, 0), (0, WIN_SHARD_PAD - WIN_SHARD), (0, 0))).reshape(N_DEV * WIN_SHARD_PAD, D_MODEL)
            parts.append(p)
        kind = SCATTER if group == "ffn2" else SCATTER_CHIPS
        if kind == SCATTER_CHIPS:
            parts = [_sibling_reduce(p, "pair_" + group + "_" + nm) for nm, p in zip(grads, parts)]
        started = _exchange_start(parts, kind, "scatter_" + group + "_start")
        in_flight.append((group, list(grads), started, kind))
        return started[3]

    small_flight = []

    def on_small(loss, dh0, g):
        packed = jnp.concatenate([g["b_a"][0:1], g["gla_norm"][0:1], g["sinks"][0:1], g["swa_norm"][0:1]], axis=1)
        slab = jnp.concatenate([g[k][0:1] for k in SLAB_VECTORS] + [packed, jnp.full((1, D_MODEL), loss, F32),
                               g["wa2"][:16].reshape(4, D_MODEL), jnp.zeros((4, D_MODEL), F32), dh0[PAD_ROWS:BLK]], axis=0)
        small_flight.append(_exchange_start([slab], GATHER, "gather_small_grads_start"))
        return small_flight[0][3]

    front = jnp.concatenate([jnp.zeros((PAD_ROWS, D_MODEL), F32), meta_full], axis=0)
    loss, dh0, g = _local_step(x[0], loss_target[0], front, w, late_weights, on_grads, on_small)
    grad_x = dh0[BLK:][None]

    land, = _exchange_wait(small_flight[0], GATHER, in_flight[-1][2][3], "gather_small_grads_wait")
    tot = _sum_partials([land], "sum_small_grads")[0]
    loss = tot[7, 0]
    small_grads = dict(
        ffn1_pre_norm=tot[0:1], ffn1_post_norm=tot[1:2], mix_pre_norm=tot[2:3], mix_post_norm=tot[3:4],
        ffn2_pre_norm=tot[4:5], ffn2_post_norm=tot[5:6], gla_b_a=tot[6:7, 0:256], gla_out_norm=tot[6:7, 256:384],
        swa_sinks=tot[6:7, 384:392], swa_out_norm=tot[6:7, 512:1024],
        gla_w_a2=lax.dynamic_slice_in_dim(tot[8:12].reshape(16, 256), dev * 32, 32, axis=1)[None],
        meta_tokens=lax.dynamic_slice_in_dim(tot[16:32], dev * 128, 128, axis=1))

    big = dict(wg1=("ffn1_w_gate", True), wu1=("ffn1_w_up", True), wd1=("ffn1_w_down", False), win=("w_in", True),
               wout=("w_out", False), wg2=("ffn2_w_gate", True), wu2=("ffn2_w_up", True), wd2=("ffn2_w_down", False))
    grads = dict(small_grads)
    delta, new_m, new_v = {}, {}, {}
    names = [n for n in WEIGHT_NAMES if n not in [full for full, _ in big.values()]]
    two_d = lambda a: a.reshape(-1, a.shape[-1])
    d_, m_, v_ = _adamw([two_d(W[n]) for n in names], [two_d(grads[n]) for n in names],
                        [two_d(M[n]) for n in names], [two_d(V[n]) for n in names], "adamw_small")
    for k, n in enumerate(names):
        delta[n], new_m[n], new_v[n] = d_[k].reshape(W[n].shape), m_[k].reshape(W[n].shape), v_[k].reshape(W[n].shape)

    before_wait = d_[0] + in_flight[-1][2][3][0, 0]
    for group, shorts, started, kind in in_flight:
        lands = _exchange_wait(started, kind, before_wait, "scatter_" + group + "_wait")
        blocks = 4 if kind == SCATTER_CHIPS else N_DEV
        for short, land in zip(shorts, lands):
            n, transposed = big[short]
            to_slab = (lambda a: a[0].T) if transposed else (lambda a: a[0])
            from_slab = (lambda a: a.T[None]) if transposed else (lambda a: a[None])
            if short == "win":
                g_slab = _sum_partials([land], "sum_" + n, blocks)[0][:WIN_SHARD]
                d_, m_, v_ = _adamw([to_slab(W[n])], [g_slab], [to_slab(M[n])], [to_slab(V[n])], "adamw_" + n)
                d_, m_, v_ = d_[0], m_[0], v_[0]
            else:
                g_slab, d_, m_, v_ = _sum_adamw(land, to_slab(W[n]), to_slab(M[n]), to_slab(V[n]), blocks, "adamw_" + n)
            grads[n], delta[n], new_m[n], new_v[n] = from_slab(g_slab), from_slab(d_), from_slab(m_), from_slab(v_)
            before_wait = d_
    return (loss, grad_x, *[grads[n] for n in WEIGHT_NAMES], *[delta[n] for n in WEIGHT_NAMES],
            *[new_m[n] for n in WEIGHT_NAMES], *[new_v[n] for n in WEIGHT_NAMES])
```

```python
import math

import jax
import jax.numpy as jnp
from jax import lax
from jax.experimental import pallas as pl
from jax.experimental.pallas import tpu as pltpu

F32, BF16 = jnp.float32, jnp.bfloat16

D_MODEL = 1024
D_FF = 2816
N_META = 16
BLK = 128
PAD_ROWS = BLK - N_META
GLA_DK = 64
SWA_HD = 64
SWA_HEADS = 8
GLA_TAU = 16.0
NORM_EPS = 1e-6
NEG_INF = -1e30
ROPE_THETA = 10000.0
P_GQ, P_GK, P_GV, P_GG, P_SQ, P_SK, P_SV, P_GA, P_END = 0, 256, 512, 1024, 1536, 2048, 2176, 2304, 2432
D_IN = 2320
IN_SPLITS = (256, 256, 512, 512, 16, 512, 128, 128)
FF_TILE = 2816
WGRAD_TILE_MAX = 2432
N_DEV = 8
MESH = pl.DeviceIdType.MESH

ADAM_LR, ADAM_B1, ADAM_B2, ADAM_EPS, ADAM_WD, ADAM_STEP = 0.001, 0.9, 0.999, 1e-08, 0.01, 10

V7X_VMEM_BYTES = 64 << 20
VMEM_SPEC = pl.BlockSpec(memory_space=pltpu.VMEM)
SMEM_SPEC = pl.BlockSpec(memory_space=pltpu.SMEM)
ANY_SPEC = pl.BlockSpec(memory_space=pl.ANY)


def _params(semantics, vmem_mb=56):
    return pltpu.CompilerParams(dimension_semantics=semantics, vmem_limit_bytes=vmem_mb << 20)


def _row_tile(rows):
    return 416 if rows % 416 == 0 else BLK


def _blocks_per_step(blocks):
    return 5 if blocks % 5 == 0 else 1


def _nn(a, b):
    return lax.dot_general(a, b, (((1,), (0,)), ((), ())), preferred_element_type=F32)


def _nt(a, b):
    return lax.dot_general(a, b, (((1,), (1,)), ((), ())), preferred_element_type=F32)


def _tn(a, b):
    return lax.dot_general(a, b, (((0,), (0,)), ((), ())), preferred_element_type=F32)


def _rms(x):
    r = lax.rsqrt(jnp.mean(x * x, axis=-1, keepdims=True) + NORM_EPS)
    return x * r, r


def _rms_bwd(xn, r, w, dy):
    g = dy * w
    return r * (g - xn * jnp.mean(g * xn, axis=-1, keepdims=True))


def _sigmoid(x):
    return 1.0 / (1.0 + jnp.exp(-x))


def _colsum(x):
    return jnp.sum(x, axis=0, keepdims=True)


def _split_bf16(x):
    hi = x.astype(BF16)
    lo = (x - hi.astype(F32)).astype(BF16)
    return hi, lo


def _tri(lower):
    r = lax.broadcasted_iota(jnp.int32, (BLK, BLK), 0)
    c = lax.broadcasted_iota(jnp.int32, (BLK, BLK), 1)
    return (r >= c) if lower else (c >= r)


def _half_mask(width, half):
    lane = lax.broadcasted_iota(jnp.int32, (1, width), 1)
    return ((lane % 128) < 64) if half == 0 else ((lane % 128) >= 64)


def _rot_half(x):
    w = x.shape[-1]
    lane = lax.broadcasted_iota(jnp.int32, (1, w), 1)
    return jnp.where((lane % SWA_HD) < SWA_HD // 2, -pltpu.roll(x, w - SWA_HD // 2, 1), pltpu.roll(x, SWA_HD // 2, 1))


def _row_spec(tm, cols):
    return pl.BlockSpec((tm, cols), lambda i: (i, 0))


def _acc_spec(cols):
    return pl.BlockSpec((8, cols), lambda i: (0, 0))


def _acc_add(ref, first, value):
    @pl.when(first)
    def _():
        ref[...] = jnp.zeros_like(ref)
    ref[0:1, :] += value


def _behind_spec(tm):
    return pl.BlockSpec((pl.Element(tm), pl.Element(D_MODEL)),
                        lambda i: (pl.multiple_of(jnp.maximum(i * tm - BLK, 0), math.gcd(tm, BLK)), 0))


def _behind_front(ref, i, tm, front):
    blk = ref[...]
    return jnp.where(i == 0, jnp.concatenate([front, blk[0:tm - BLK]], axis=0), blk)


def _ffn_fwd(h, gpre, wg_t, wu_t, wd, gpost, tgt=None, front=None, mixed=None):
    with_loss, with_front, with_mixed = tgt is not None, front is not None, mixed is not None
    rows = h.shape[0] + (BLK if with_front else 0)
    tm = _row_tile(rows)
    nf = D_FF // FF_TILE

    def body(*refs):
        refs = list(refs)
        h_ref, gpre_ref, wg_ref, wu_ref, wd_ref, gpost_ref = refs[:6]
        del refs[:6]
        front_ref = refs.pop(0) if with_front else None
        cg_ref, cs_ref, wo_ref, gm_ref = (refs.pop(0), refs.pop(0), refs.pop(0), refs.pop(0)) if with_mixed else (None,) * 4
        t_ref = refs.pop(0) if with_loss else None
        hm_ref, m_ref = (refs.pop(0), refs.pop(0)) if with_mixed else (None, None)
        ho_ref = None if with_loss else refs.pop(0)
        a_ref, b_ref, s_ref, f_ref = refs[:4]
        dy_ref, loss_ref = refs[4:6] if with_loss else (None, None)
        acc = refs[-1]
        i = pl.program_id(0)
        h_in = _behind_front(h_ref, i, tm, front_ref[...]) if with_front else h_ref[...]
        if with_mixed:
            m = _nn(cg_ref[...], wo_ref[0:512, :]) + _nn(cs_ref[...], wo_ref[512:1024, :])
            m_ref[...] = m
            mn, _ = _rms(m)
            h_in = h_in + mn * gm_ref[...]
            hm_ref[...] = h_in
        hn, _ = _rms(h_in)
        n16 = (hn * gpre_ref[...]).astype(BF16)
        for j in range(nf):
            cols = slice(j * FF_TILE, (j + 1) * FF_TILE)
            a = _nt(n16, wg_ref[cols, :])
            b = _nt(n16, wu_ref[cols, :])
            a_ref[:, cols] = a.astype(BF16)
            b_ref[:, cols] = b.astype(BF16)
            s16 = (a * _sigmoid(a) * b).astype(BF16)
            s_ref[:, cols] = s16
            part = _nn(s16, wd_ref[cols, :])
            if j == 0:
                acc[...] = part
            else:
                acc[...] += part
        f = acc[...]
        f_ref[...] = f
        fn, _ = _rms(f)
        y = h_in + 0.5 * (fn * gpost_ref[...])
        if not with_loss:
            ho_ref[...] = y
        else:
            row = i * tm + lax.broadcasted_iota(jnp.int32, (tm, 1), 0)
            err = jnp.where(row >= BLK, y - _behind_front(t_ref, i, tm, jnp.zeros((BLK, D_MODEL), F32)), 0.0)
            dy_ref[...] = err * (1.0 / D_MODEL)
            part = 0.5 * jnp.sum(jnp.sum(err * err, axis=-1, keepdims=True) * (1.0 / D_MODEL), axis=0, keepdims=True)

            @pl.when(i == 0)
            def _():
                loss_ref[...] = jnp.zeros_like(loss_ref)
            loss_ref[...] += part

    row_f32 = _row_spec(tm, D_MODEL)
    behind = _behind_spec(tm)
    in_specs = [behind if with_front else row_f32, VMEM_SPEC, VMEM_SPEC, VMEM_SPEC, VMEM_SPEC, VMEM_SPEC]
    wide, full = jax.ShapeDtypeStruct((rows, D_FF), BF16), jax.ShapeDtypeStruct((rows, D_MODEL), F32)
    out_specs = [_row_spec(tm, D_FF), _row_spec(tm, D_FF), _row_spec(tm, D_FF), row_f32]
    out_shape = [wide, wide, wide, full]
    args = [h, gpre, wg_t, wu_t, wd, gpost]
    if not with_loss:
        out_specs.insert(0, row_f32)
        out_shape.insert(0, full)
    if with_front:
        in_specs.append(VMEM_SPEC)
        args.append(front)
    if with_mixed:
        in_specs += [_row_spec(tm, 512), _row_spec(tm, 512), VMEM_SPEC, VMEM_SPEC]
        args += list(mixed)
        out_specs = [row_f32, row_f32] + out_specs
        out_shape = [full, full] + out_shape
    if with_loss:
        in_specs.append(behind)
        args.append(tgt)
        out_specs += [row_f32, pl.BlockSpec((8, 128), lambda i: (0, 0))]
        out_shape += [jax.ShapeDtypeStruct((rows, D_MODEL), F32), jax.ShapeDtypeStruct((8, 128), F32)]
    return pl.pallas_call(
        body, name="ffn_fwd_loss" if with_loss else "ffn_fwd", grid=(rows // tm,),
        in_specs=in_specs, out_specs=out_specs, out_shape=out_shape,
        scratch_shapes=[pltpu.VMEM((tm, D_MODEL), F32)],
        compiler_params=_params(("arbitrary",), vmem_mb=62 if with_mixed else 56),
    )(*args)


def _ffn_bwd_act(dh_out, h, a, b, f, gpre, gpost, wg_t, wu_t, wd, name, front=None):
    with_front = front is not None
    rows = dh_out.shape[0]
    tm = _row_tile(rows)
    nf = D_FF // FF_TILE

    def body(dho_ref, h_ref, a_ref, b_ref, f_ref, gpre_ref, gpost_ref, wg_ref, wu_ref, wd_ref, *rest):
        front_ref = rest[0] if with_front else None
        dh_ref, da_ref, db_ref, df_ref, n_ref, dgpre_ref, dgpost_ref, acc = rest[-8:]
        first = pl.program_id(0) == 0
        dho = dho_ref[...]
        drr = 0.5 * dho
        fn, rf = _rms(f_ref[...])
        _acc_add(dgpost_ref, first, _colsum(drr * fn))
        df16 = _rms_bwd(fn, rf, gpost_ref[...], drr).astype(BF16)
        df_ref[...] = df16
        h_in = _behind_front(h_ref, pl.program_id(0), tm, front_ref[...]) if with_front else h_ref[...]
        hn, rh = _rms(h_in)
        n_ref[...] = (hn * gpre_ref[...]).astype(BF16)
        for j in range(nf):
            cols = slice(j * FF_TILE, (j + 1) * FF_TILE)
            ds = _nt(df16, wd_ref[cols, :])
            av = a_ref[:, cols].astype(F32)
            bv = b_ref[:, cols].astype(F32)
            sg = _sigmoid(av)
            db16 = (ds * (av * sg)).astype(BF16)
            da16 = (ds * bv * (sg * (1.0 + av * (1.0 - sg)))).astype(BF16)
            da_ref[:, cols] = da16
            db_ref[:, cols] = db16
            part = _nn(da16, wg_ref[cols, :]) + _nn(db16, wu_ref[cols, :])
            if j == 0:
                acc[...] = part
            else:
                acc[...] += part
        dn = acc[...]
        _acc_add(dgpre_ref, first, _colsum(dn * hn))
        dh_ref[...] = dho + _rms_bwd(hn, rh, gpre_ref[...], dn)

    row_f32 = _row_spec(tm, D_MODEL)
    row_ff = _row_spec(tm, D_FF)
    return pl.pallas_call(
        body, name=name, grid=(rows // tm,),
        in_specs=[row_f32, _behind_spec(tm) if with_front else row_f32, row_ff, row_ff, row_f32,
                  VMEM_SPEC, VMEM_SPEC, VMEM_SPEC, VMEM_SPEC, VMEM_SPEC] + ([VMEM_SPEC] if with_front else []),
        out_specs=[row_f32, row_ff, row_ff, row_f32, row_f32, _acc_spec(D_MODEL), _acc_spec(D_MODEL)],
        out_shape=[jax.ShapeDtypeStruct((rows, D_MODEL), F32), jax.ShapeDtypeStruct((rows, D_FF), BF16),
                   jax.ShapeDtypeStruct((rows, D_FF), BF16), jax.ShapeDtypeStruct((rows, D_MODEL), BF16),
                   jax.ShapeDtypeStruct((rows, D_MODEL), BF16), jax.ShapeDtypeStruct((8, D_MODEL), F32),
                   jax.ShapeDtypeStruct((8, D_MODEL), F32)],
        scratch_shapes=[pltpu.VMEM((tm, D_MODEL), F32)],
        compiler_params=_params(("arbitrary",), vmem_mb=62),
    )(dh_out, h, a, b, f, gpre, gpost, wg_t, wu_t, wd, *([front] if with_front else []))


def _wgrad(lhs, rhs, name, after=None):
    rows, width = lhs.shape
    tm = rows if rows % 1664 == 0 else BLK
    tf = 256 if width % 256 == 0 else 128
    nr = rows // tm

    def body(l_ref, r_ref, *rest):
        o_ref, acc = rest[-2:]
        i = pl.program_id(1)
        part = _tn(l_ref[...], r_ref[...])

        @pl.when(i == 0)
        def _():
            acc[...] = part

        @pl.when(i > 0)
        def _():
            acc[...] += part

        @pl.when(i == nr - 1)
        def _():
            o_ref[...] = acc[...].astype(BF16)

    l_spec = pl.BlockSpec((tm, tf), lambda j, i: (i, j))
    r_spec = pl.BlockSpec((tm, D_MODEL), lambda j, i: (i, 0))
    return pl.pallas_call(
        body, name=name, grid=(width // tf, nr),
        in_specs=[l_spec, r_spec] + ([] if after is None else [ANY_SPEC]),
        out_specs=pl.BlockSpec((tf, D_MODEL), lambda j, i: (j, 0)),
        out_shape=jax.ShapeDtypeStruct((width, D_MODEL), BF16),
        scratch_shapes=[pltpu.VMEM((tf, D_MODEL), F32)],
        compiler_params=_params(("arbitrary", "arbitrary")),
    )(lhs, rhs, *([] if after is None else [after]))


def _chunk_cumsum(x, lower):
    tri = jnp.where(_tri(lower), 1.0, 0.0).astype(BF16)
    hi, lo = _split_bf16(x)
    return _nn(tri, hi) + _nn(tri, lo)


def _mix_in(h, g, win_p, wa2_p, b_a, cos, sin):
    rows = h.shape[0]
    tm = 640 if rows % 640 == 0 else BLK

    def body(h_ref, g_ref, win_ref, wa2_ref, ba_ref, cos_ref, sin_ref,
             gq_ref, gk_ref, gv_ref, gg_ref, sq_ref, sk_ref, sv_ref, ga_ref, loga_ref, bc_ref, n_ref):
        hn, _ = _rms(h_ref[...])
        n16 = (hn * g_ref[...]).astype(BF16)
        n_ref[...] = n16
        proj = _nt(n16, win_ref[...])
        gq_ref[...] = proj[:, P_GQ:P_GK]
        gk_ref[...] = proj[:, P_GK:P_GV]
        gv_ref[...] = proj[:, P_GV:P_GG].astype(BF16)
        gg_ref[...] = proj[:, P_GG:P_SQ]
        c1, s1 = cos_ref[...], sin_ref[...]
        c4 = jnp.concatenate([c1, c1, c1, c1], axis=1)
        s4 = jnp.concatenate([s1, s1, s1, s1], axis=1)
        sq = proj[:, P_SQ:P_SK]
        sk = proj[:, P_SK:P_SV]
        sq_ref[...] = (sq * c4 + _rot_half(sq) * s4).astype(BF16)
        sk_ref[...] = (sk * c1 + _rot_half(sk) * s1).astype(BF16)
        sv_ref[...] = proj[:, P_SV:P_GA].astype(BF16)
        ga = proj[:, P_GA:P_END]
        ga_ref[...] = ga
        z = _nn(ga, wa2_ref[...]) + ba_ref[...]
        loga = (jnp.minimum(z, 0.0) - jnp.log(1.0 + jnp.exp(-jnp.abs(z)))) * (1.0 / GLA_TAU)
        loga_ref[...] = loga
        for c in range(tm // BLK):
            rs = slice(c * BLK, (c + 1) * BLK)
            bc_ref[rs, :] = _chunk_cumsum(loga[rs, :], True)

    f32 = lambda c: jax.ShapeDtypeStruct((rows, c), F32)
    b16 = lambda c: jax.ShapeDtypeStruct((rows, c), BF16)
    rs = lambda c: _row_spec(tm, c)
    return pl.pallas_call(
        body, name="mix_in", grid=(rows // tm,),
        in_specs=[rs(D_MODEL), VMEM_SPEC, VMEM_SPEC, VMEM_SPEC, VMEM_SPEC, rs(128), rs(128)],
        out_specs=[rs(256), rs(256), rs(512), rs(512), rs(512), rs(128), rs(128), rs(128), rs(256), rs(256), rs(D_MODEL)],
        out_shape=[f32(256), f32(256), b16(512), f32(512), b16(512), b16(128), b16(128), f32(128), f32(256), f32(256),
                   b16(D_MODEL)],
        compiler_params=_params(("arbitrary",)),
    )(h, g, win_p, wa2_p, b_a, cos, sin)


def _side_by_side(parts, name):
    steps, per_step = parts[0]["steps"], parts[0]["per_step"]
    assert all((p["steps"], p["per_step"]) == (steps, per_step) for p in parts)
    counts = [[len(p[key]) for p in parts] for key in ("in_specs", "out_specs", "scratch_shapes")]

    def body(*refs):
        groups, pos = [], 0
        for kind in counts:
            groups.append([])
            for n in kind:
                groups[-1].append(refs[pos:pos + n])
                pos += n
        programs = [p["program"](*groups[0][k], *groups[1][k], *groups[2][k]) for k, p in enumerate(parts)]

        def blocks(c, carries):
            return tuple(block(c, carry) for (block, _, _), carry in zip(programs, carries))
        carries = lax.fori_loop(0, per_step, blocks, tuple(first for _, first, _ in programs))
        for (_, _, finish), carry in zip(programs, carries):
            finish(carry)

    outs = pl.pallas_call(
        body, name=name, grid=(steps,),
        in_specs=[s for p in parts for s in p["in_specs"]], out_specs=[s for p in parts for s in p["out_specs"]],
        out_shape=[s for p in parts for s in p["out_shape"]],
        scratch_shapes=[s for p in parts for s in p["scratch_shapes"]],
        compiler_params=_params(("arbitrary",)),
    )(*[a for p in parts for a in p["args"]])
    split, pos = [], 0
    for n in counts[1]:
        split.append(outs[pos:pos + n])
        pos += n
    return split


def _gla_factors(q, k, bc):
    bm = bc[BLK // 2 - 1:BLK // 2, :]
    bl = bc[BLK - 1:BLK, :]
    e_q, e_k, e_qe, e_kd = jnp.exp(bc - bm), jnp.exp(bm - bc), jnp.exp(bc), jnp.exp(bl - bc)
    return (q * e_q, k * e_k, q * e_qe, k * e_kd), (e_q, e_k, e_qe, e_kd), jnp.exp(bl)


def _gla_fwd(gq, gk, gv, gg, bc, wgn):
    rows = gq.shape[0]
    nc = rows // BLK
    per_step = _blocks_per_step(nc)
    scale = GLA_DK ** -0.5

    def body(q_ref, k_ref, v_ref, gg_ref, bc_ref, wgn_ref, o_ref, cat_ref, sp_ref, st):
        @pl.when(pl.program_id(0) == 0)
        def _():
            st[...] = jnp.zeros_like(st)
        low = _tri(True)
        wgn_v = wgn_ref[...]

        def chunk(c, carry):
            rr = pl.ds(pl.multiple_of(c * BLK, BLK), BLK)
            for p in range(2):
                sl = slice(128 * p, 128 * p + 128)
                (qt, kt, qe, kd), _, ebl = _gla_factors(q_ref[rr, sl] * scale, k_ref[rr, sl], bc_ref[rr, sl])
                s_prev = st[p]
                sp_ref[c, p] = s_prev
                s16 = s_prev.astype(BF16)
                qt16 = qt.astype(BF16)
                s_new = s_prev * ebl
                for hh in range(2):
                    hs = slice(128 * (2 * p + hh), 128 * (2 * p + hh) + 128)
                    lm = _half_mask(128, hh)
                    vh = v_ref[rr, hs]
                    pm = jnp.where(low, _nt(qt16, jnp.where(lm, kt, 0.0).astype(BF16)), 0.0)
                    o = _nn(pm.astype(BF16), vh) + _nt(jnp.where(lm, qe, 0.0).astype(BF16), s16)
                    s_new = s_new + _tn(vh, jnp.where(lm, kd, 0.0).astype(BF16))
                    o_ref[rr, hs] = o
                    on, _ = _rms(o)
                    gate = gg_ref[rr, hs]
                    cat_ref[rr, hs] = (on * wgn_v * (gate * _sigmoid(gate))).astype(BF16)
                st[p] = s_new
            return carry
        return chunk, 0, lambda carry: None

    rs = lambda c: _row_spec(per_step * BLK, c)
    return dict(
        program=body, steps=nc // per_step, per_step=per_step,
        in_specs=[rs(256), rs(256), rs(512), rs(512), rs(256), VMEM_SPEC],
        out_specs=[rs(512), rs(512), pl.BlockSpec((per_step, 2, 128, 128), lambda i: (i, 0, 0, 0))],
        out_shape=[jax.ShapeDtypeStruct((rows, 512), F32), jax.ShapeDtypeStruct((rows, 512), BF16),
                   jax.ShapeDtypeStruct((nc, 2, 128, 128), F32)],
        scratch_shapes=[pltpu.VMEM((2, 128, 128), F32)],
        args=(gq, gk, gv, gg, bc, wgn))


def _gla_bwd(dcat, o_all, gq, gk, gv, gg, bc, sp, wgn):
    rows = gq.shape[0]
    nc = rows // BLK
    per_step = _blocks_per_step(nc)
    steps = nc // per_step
    scale = GLA_DK ** -0.5

    def body(dc_ref, o_ref, q_ref, k_ref, v_ref, gg_ref, bc_ref, sp_ref, wgn_ref,
             dq_ref, dk_ref, dv_ref, dgg_ref, dla_ref, dwgn_ref, dst):
        first = pl.program_id(0) == 0

        @pl.when(first)
        def _():
            dst[...] = jnp.zeros_like(dst)
        low, upp = _tri(True), _tri(False)
        last_row = lax.broadcasted_iota(jnp.int32, (BLK, 1), 0) == BLK - 1
        wgn_v = wgn_ref[...]

        def chunk(c, dwgn):
            rr = pl.ds(pl.multiple_of((per_step - 1 - c) * BLK, BLK), BLK)
            for p in range(2):
                sl = slice(128 * p, 128 * p + 128)
                (qt, kt, qe, kd), (e_q, e_k, e_qe, e_kd), ebl = _gla_factors(
                    q_ref[rr, sl] * scale, k_ref[rr, sl], bc_ref[rr, sl])
                s_prev = sp_ref[per_step - 1 - c, p]
                s16 = s_prev.astype(BF16)
                ds_next = dst[p]
                ds16 = ds_next.astype(BF16)
                qt16 = qt.astype(BF16)
                ds_new = ds_next * ebl
                dqt = jnp.zeros((BLK, 128), F32)
                dkt = jnp.zeros((BLK, 128), F32)
                dqe = jnp.zeros((BLK, 128), F32)
                dkd = jnp.zeros((BLK, 128), F32)
                for hh in range(2):
                    hs = slice(128 * (2 * p + hh), 128 * (2 * p + hh) + 128)
                    lm = _half_mask(128, hh)
                    on, ro = _rms(o_ref[rr, hs])
                    gate = gg_ref[rr, hs]
                    sg = _sigmoid(gate)
                    si = gate * sg
                    dog = dc_ref[rr, hs]
                    dwgn = dwgn + _colsum(dog * si * on)
                    dgg_ref[rr, hs] = dog * (on * wgn_v) * (sg * (1.0 + gate * (1.0 - sg)))
                    do16 = _rms_bwd(on, ro, wgn_v, dog * si).astype(BF16)
                    vh = v_ref[rr, hs]
                    ktm16 = jnp.where(lm, kt, 0.0).astype(BF16)
                    qtm16 = jnp.where(lm, qt, 0.0).astype(BF16)
                    qem16 = jnp.where(lm, qe, 0.0).astype(BF16)
                    kdm16 = jnp.where(lm, kd, 0.0).astype(BF16)
                    p_t = jnp.where(upp, _nt(ktm16, qt16), 0.0)
                    dp_t = jnp.where(upp, _nt(vh, do16), 0.0)
                    dp = jnp.where(low, _nt(do16, vh), 0.0)
                    dv_ref[rr, hs] = _nn(p_t.astype(BF16), do16) + _nt(kdm16, ds16)
                    dqt = dqt + _nn(dp.astype(BF16), ktm16)
                    dkt = dkt + _nn(dp_t.astype(BF16), qtm16)
                    dqe = dqe + jnp.where(lm, _nn(do16, s16), 0.0)
                    dkd = dkd + jnp.where(lm, _nn(vh, ds16), 0.0)
                    ds_new = ds_new + _tn(do16, qem16)
                debl = _colsum(ds_next * s_prev)
                dq_ref[rr, sl] = (dqt * e_q + dqe * e_qe) * scale
                dk_ref[rr, sl] = dkt * e_k + dkd * e_kd
                dkd_kd = dkd * kd
                db = dqt * qt - dkt * kt + dqe * qe - dkd_kd
                db = jnp.where(last_row, db + (_colsum(dkd_kd) + debl * ebl), db)
                dla_ref[rr, sl] = _chunk_cumsum(db, False)
                dst[p] = ds_new
            return dwgn

        def finish(dwgn):
            _acc_add(dwgn_ref, first, dwgn)
        return chunk, jnp.zeros((1, 128), F32), finish

    rev = lambda c: pl.BlockSpec((per_step * BLK, c), lambda i: (steps - 1 - i, 0))
    f32 = lambda c: jax.ShapeDtypeStruct((rows, c), F32)
    return dict(
        program=body, steps=steps, per_step=per_step,
        in_specs=[rev(512), rev(512), rev(256), rev(256), rev(512), rev(512), rev(256),
                  pl.BlockSpec((per_step, 2, 128, 128), lambda i: (steps - 1 - i, 0, 0, 0)), VMEM_SPEC],
        out_specs=[rev(256), rev(256), rev(512), rev(512), rev(256), _acc_spec(128)],
        out_shape=[f32(256), f32(256), f32(512), f32(512), f32(256), jax.ShapeDtypeStruct((8, 128), F32)],
        scratch_shapes=[pltpu.VMEM((2, 128, 128), F32)],
        args=(dcat, o_all, gq, gk, gv, gg, bc, sp, wgn))


def _swa_masks(i):
    t = lax.broadcasted_iota(jnp.int32, (BLK, BLK), 0)
    c = lax.broadcasted_iota(jnp.int32, (BLK, BLK), 1)
    own_side = c <= t
    band_ok = i >= jnp.where(own_side, 1, 2)
    meta_ok = (c % N_META) <= jnp.where(i >= 1, N_META, t - PAD_ROWS)
    return own_side, band_ok, meta_ok, c // N_META


def _swa_blocks(ref, i):
    prev = pl.multiple_of(jnp.maximum(i - 1, 0) * BLK, BLK)
    own = pl.multiple_of(i * BLK, BLK)
    return jnp.concatenate([ref[pl.ds(prev, BLK), :], ref[pl.ds(own, BLK), :]], axis=0), prev, own


def _swa_meta_operand(ref):
    blk = ref[0:BLK, :]
    swapped = pltpu.roll(blk, 64, 1)
    lo = jnp.where(_half_mask(128, 0), blk, swapped)
    hi = jnp.where(_half_mask(128, 1), blk, swapped)
    meta = jnp.concatenate([lo, lo, hi, hi], axis=1)[PAD_ROWS:BLK, :]
    tiled = jnp.concatenate([meta] * SWA_HEADS, axis=0)
    j = lax.broadcasted_iota(jnp.int32, tiled.shape, 0)
    lane = lax.broadcasted_iota(jnp.int32, tiled.shape, 1)
    return jnp.where(j // N_META == lane // SWA_HD, tiled, jnp.zeros_like(tiled))


def _swa_meta_fold(acc):
    out = jnp.zeros((N_META, 128), F32)
    for hd in range(SWA_HEADS):
        half, kv = hd % 2, hd // 4
        piece = acc[N_META * hd:N_META * (hd + 1), 128 * (hd // 2):128 * (hd // 2) + 128]
        piece = jnp.where(_half_mask(128, half), piece, 0.0)
        out = out + (piece if half == kv else pltpu.roll(piece, 64, 1))
    return out


def _by_head(group, per_head):
    out = jnp.zeros((BLK, BLK), F32)
    for hd, v in enumerate(per_head):
        out = jnp.where(group == hd, v, out)
    return out


def _place(x, kv):
    if kv == 0:
        lo = jnp.where(_half_mask(128, 0), x, jnp.zeros_like(x))
        return lo, pltpu.roll(lo, 64, 1)
    hi = jnp.where(_half_mask(128, 1), x, jnp.zeros_like(x))
    return pltpu.roll(hi, 64, 1), hi


def _swa_fwd(sq, sk, sv, sinks, wn):
    rows = sq.shape[0]
    nb = rows // BLK
    per_step = _blocks_per_step(nb)
    scale = SWA_HD ** -0.5

    def body(q_ref, k_ref, v_ref, sink_ref, wn_ref, o_ref, cat_ref, lse_ref, kp, vp):
        step = pl.program_id(0)

        @pl.when(step == 0)
        def _():
            kp[...] = _swa_meta_operand(k_ref)
            vp[...] = _swa_meta_operand(v_ref)

        def one_block(c, carry):
            i = step * per_step + c
            rr = pl.ds(pl.multiple_of(c * BLK, BLK), BLK)
            own_side, band_ok, meta_ok, group = _swa_masks(i)
            k2, _, _ = _swa_blocks(k_ref, i)
            v2, _, _ = _swa_blocks(v_ref, i)
            kz = (_place(k2, 0), _place(k2, 1))
            vz = (_place(v2, 0), _place(v2, 1))
            q_all = q_ref[rr, :]
            s_meta = jnp.where(meta_ok, _nt(q_all, kp[...]) * scale, NEG_INF)
            s_band, m = [], []
            for hd in range(SWA_HEADS):
                kv, half = hd // 4, hd % 2
                q_pair = q_all[:, 128 * (hd // 2):128 * (hd // 2) + 128]
                s2 = _nt(q_pair, kz[kv][half])
                s = jnp.where(band_ok, jnp.where(own_side, s2[:, BLK:], s2[:, :BLK]) * scale, NEG_INF)
                top = jnp.maximum(jnp.max(s, axis=-1, keepdims=True),
                                  jnp.max(jnp.where(group == hd, s_meta, NEG_INF), axis=-1, keepdims=True))
                s_band.append(s)
                m.append(jnp.maximum(top, sink_ref[0, hd]))
            e_meta = jnp.exp(s_meta - _by_head(group, m))
            o_meta = _nn(e_meta.astype(BF16), vp[...])
            outs = []
            for pr in range(4):
                o_pair = o_meta[:, 128 * pr:128 * pr + 128]
                rden = []
                for half in range(2):
                    hd = 2 * pr + half
                    kv = hd // 4
                    e = jnp.exp(s_band[hd] - m[hd])
                    den = (jnp.sum(e, axis=-1, keepdims=True)
                           + jnp.sum(jnp.where(group == hd, e_meta, 0.0), axis=-1, keepdims=True)
                           + jnp.exp(sink_ref[0, hd] - m[hd]))
                    lse_ref[rr, hd:hd + 1] = m[hd] + jnp.log(den)
                    rden.append(1.0 / den)
                    e2 = jnp.concatenate([jnp.where(own_side, 0.0, e), jnp.where(own_side, e, 0.0)], axis=1).astype(BF16)
                    o_pair = o_pair + _nn(e2, vz[kv][half])
                outs.append(o_pair * jnp.where(_half_mask(128, 0), rden[0], rden[1]))
            o = jnp.concatenate(outs, axis=1)
            o_ref[rr, :] = o
            on, _ = _rms(o)
            cat_ref[rr, :] = (on * wn_ref[...]).astype(BF16)
            return carry
        return one_block, 0, lambda carry: None

    return dict(
        program=body, steps=nb // per_step, per_step=per_step,
        in_specs=[_row_spec(per_step * BLK, 512), VMEM_SPEC, VMEM_SPEC, SMEM_SPEC, VMEM_SPEC],
        out_specs=[_row_spec(per_step * BLK, 512), _row_spec(per_step * BLK, 512), _row_spec(per_step * BLK, SWA_HEADS)],
        out_shape=[jax.ShapeDtypeStruct((rows, 512), F32), jax.ShapeDtypeStruct((rows, 512), BF16),
                   jax.ShapeDtypeStruct((rows, SWA_HEADS), F32)],
        scratch_shapes=[pltpu.VMEM((BLK, 512), BF16), pltpu.VMEM((BLK, 512), BF16)],
        args=(sq, sk, sv, sinks, wn))


def _swa_bwd(dcat, o_all, sq, sk, sv, lse, sinks, wn):
    rows = sq.shape[0]
    nb = rows // BLK
    per_step = _blocks_per_step(nb)
    steps = nb // per_step
    scale = SWA_HD ** -0.5

    def body(dc_ref, o_ref, q_ref, k_ref, v_ref, lse_ref, sink_ref, wn_ref, dq_ref, dk_ref, dv_ref, dsink_ref, dwn_ref,
             kp, vp, dkp, dvp):
        step = pl.program_id(0)

        @pl.when(step == 0)
        def _():
            dk_ref[...] = jnp.zeros_like(dk_ref)
            dv_ref[...] = jnp.zeros_like(dv_ref)
            dkp[...] = jnp.zeros_like(dkp)
            dvp[...] = jnp.zeros_like(dvp)
            kp[...] = _swa_meta_operand(k_ref)
            vp[...] = _swa_meta_operand(v_ref)

        def one_block(c, carry):
            i = step * per_step + c
            rr = pl.ds(pl.multiple_of(c * BLK, BLK), BLK)
            first = i == 0
            own_side, band_ok, meta_ok, group = _swa_masks(i)
            k2, prev, own = _swa_blocks(k_ref, i)
            v2, _, _ = _swa_blocks(v_ref, i)
            kz = (_place(k2, 0), _place(k2, 1))
            vz = (_place(v2, 0), _place(v2, 1))
            o = o_ref[rr, :]
            on, ro = _rms(o)
            dc = dc_ref[rr, :]
            _acc_add(dwn_ref, first, _colsum(dc * on))
            do = _rms_bwd(on, ro, wn_ref[...], dc)
            do_o = do * o
            do16 = do.astype(BF16)
            q_all = q_ref[rr, :]
            lse = [lse_ref[rr, hd:hd + 1] for hd in range(SWA_HEADS)]
            delta = [jnp.sum(jnp.where(_half_mask(128, hd % 2), do_o[:, 128 * (hd // 2):128 * (hd // 2) + 128], 0.0),
                             axis=-1, keepdims=True) for hd in range(SWA_HEADS)]
            s_meta = jnp.where(meta_ok, _nt(q_all, kp[...]) * scale, NEG_INF)
            p_meta = jnp.exp(s_meta - _by_head(group, lse))
            ds_meta16 = (p_meta * (_nt(do16, vp[...]) - _by_head(group, delta)) * scale).astype(BF16)
            dq_meta = _nn(ds_meta16, kp[...])
            dkp[...] += _tn(ds_meta16, q_all)
            dvp[...] += _tn(p_meta.astype(BF16), do16)
            own2 = jnp.concatenate([own_side.astype(jnp.int32)] * 2, axis=0) > 0
            ok2 = jnp.concatenate([band_ok.astype(jnp.int32)] * 2, axis=0) > 0

            def window(x2):
                return jnp.where(own2, x2[:, BLK:], x2[:, :BLK])

            def unwindow(x):
                return jnp.concatenate([jnp.where(own2, 0.0, x), jnp.where(own2, x, 0.0)], axis=1).astype(BF16)
            lane8 = lax.broadcasted_iota(jnp.int32, (1, 128), 1)
            dsink = jnp.zeros((1, 128), F32)
            dq_pairs = [dq_meta[:, 128 * pr:128 * pr + 128] for pr in range(4)]
            dk2 = [[None, None], [None, None]]
            dv2 = [[None, None], [None, None]]
            for kv in range(2):
                for half in range(2):
                    heads, pairs = (4 * kv + half, 4 * kv + 2 + half), (2 * kv, 2 * kv + 1)
                    q_s = jnp.concatenate([q_all[:, 128 * pr:128 * pr + 128] for pr in pairs], axis=0)
                    do_s = jnp.concatenate([do16[:, 128 * pr:128 * pr + 128] for pr in pairs], axis=0)
                    lse_s = jnp.concatenate([lse[hd] for hd in heads], axis=0)
                    delta_s = jnp.concatenate([delta[hd] for hd in heads], axis=0)
                    s = jnp.where(ok2, window(_nt(q_s, kz[kv][half])) * scale, NEG_INF)
                    prob = jnp.exp(s - lse_s)
                    for hd in heads:
                        dsink = dsink + jnp.where(lane8 == hd, -jnp.sum(jnp.exp(sink_ref[0, hd] - lse[hd]) * delta[hd]), 0.0)
                    ds2 = unwindow(prob * (window(_nt(do_s, vz[kv][half])) - delta_s) * scale)
                    dq_s = _nn(ds2, kz[kv][half])
                    dq_pairs[pairs[0]] = dq_pairs[pairs[0]] + dq_s[:BLK]
                    dq_pairs[pairs[1]] = dq_pairs[pairs[1]] + dq_s[BLK:]
                    dk2[kv][half] = _tn(ds2, q_s)
                    dv2[kv][half] = _tn(unwindow(prob), do_s)
            dq_ref[rr, :] = jnp.concatenate(dq_pairs, axis=1)
            _acc_add(dsink_ref, first, dsink)
            for ref, acc2 in ((dk_ref, dk2), (dv_ref, dv2)):
                tot = jnp.zeros((2 * BLK, 128), F32)
                for kv in range(2):
                    for half in range(2):
                        part = jnp.where(_half_mask(128, half), acc2[kv][half], 0.0)
                        tot = tot + (part if half == kv else pltpu.roll(part, 64, 1))
                ref[pl.ds(prev, BLK), :] += tot[:BLK]
                ref[pl.ds(own, BLK), :] += tot[BLK:]
            return carry

        def finish(carry):
            del carry

            @pl.when(step == steps - 1)
            def _():
                dk_ref[PAD_ROWS:BLK, :] += _swa_meta_fold(dkp[...])
                dv_ref[PAD_ROWS:BLK, :] += _swa_meta_fold(dvp[...])
        return one_block, jnp.zeros((1, 128), F32), finish

    full = pl.BlockSpec((rows, 128), lambda i: (0, 0))
    blocks = lambda cols: _row_spec(per_step * BLK, cols)
    return dict(
        program=body, steps=steps, per_step=per_step,
        in_specs=[blocks(512), blocks(512), blocks(512), VMEM_SPEC, VMEM_SPEC, blocks(SWA_HEADS), SMEM_SPEC, VMEM_SPEC],
        out_specs=[blocks(512), full, full, _acc_spec(128), _acc_spec(512)],
        out_shape=[jax.ShapeDtypeStruct((rows, 512), F32), jax.ShapeDtypeStruct((rows, 128), F32),
                   jax.ShapeDtypeStruct((rows, 128), F32), jax.ShapeDtypeStruct((8, 128), F32),
                   jax.ShapeDtypeStruct((8, 512), F32)],
        scratch_shapes=[pltpu.VMEM((BLK, 512), BF16), pltpu.VMEM((BLK, 512), BF16),
                        pltpu.VMEM((BLK, 512), F32), pltpu.VMEM((BLK, 512), F32)],
        args=(dcat, o_all, sq, sk, sv, lse, sinks, wn))


def _mix_out_bwd(dh, m, wout, gpost):
    rows = dh.shape[0]
    tm = _row_tile(rows)

    def body(dh_ref, m_ref, w_ref, g_ref, dcg_ref, dcs_ref, dm_ref, dg_ref):
        first = pl.program_id(0) == 0
        dhv = dh_ref[...]
        mn, rm = _rms(m_ref[...])
        _acc_add(dg_ref, first, _colsum(dhv * mn))
        dm16 = _rms_bwd(mn, rm, g_ref[...], dhv).astype(BF16)
        dm_ref[...] = dm16
        dcat = _nt(dm16, w_ref[...])
        dcg_ref[...] = dcat[:, 0:512]
        dcs_ref[...] = dcat[:, 512:1024]

    row_f32 = _row_spec(tm, D_MODEL)
    return pl.pallas_call(
        body, name="mix_out_bwd", grid=(rows // tm,),
        in_specs=[row_f32, row_f32, VMEM_SPEC, VMEM_SPEC],
        out_specs=[_row_spec(tm, 512), _row_spec(tm, 512), row_f32, _acc_spec(D_MODEL)],
        out_shape=[jax.ShapeDtypeStruct((rows, 512), F32), jax.ShapeDtypeStruct((rows, 512), F32),
                   jax.ShapeDtypeStruct((rows, D_MODEL), BF16), jax.ShapeDtypeStruct((8, D_MODEL), F32)],
        compiler_params=_params(("arbitrary",)),
    )(dh, m, wout, gpost)


def _mix_in_bwd(dh_out, h, g, win_p, wa2_p, cos, sin, loga, ga, dgq, dgk, dgv, dgg, dsq, dsk, dsv, dloga):
    rows = h.shape[0]
    tm = _row_tile(rows)

    def body(dho_ref, h_ref, g_ref, win_ref, wa2_ref, cos_ref, sin_ref, loga_ref, ga_ref,
             dgq_ref, dgk_ref, dgv_ref, dgg_ref, dsq_ref, dsk_ref, dsv_ref, dla_ref,
             dh_ref, dproj_ref, dwa2_ref, dg_ref, dba_ref):
        first = pl.program_id(0) == 0
        dz = dla_ref[...] * (1.0 / GLA_TAU) * (1.0 - jnp.exp(GLA_TAU * loga_ref[...]))
        _acc_add(dba_ref, first, _colsum(dz))
        dga = _nt(dz, wa2_ref[...])
        pa = _tn(ga_ref[...], dz)
        c1, s1 = cos_ref[...], sin_ref[...]
        c4 = jnp.concatenate([c1, c1, c1, c1], axis=1)
        s4 = jnp.concatenate([s1, s1, s1, s1], axis=1)
        dq_r, dk_r = dsq_ref[...], dsk_ref[...]
        dsq = dq_r * c4 - _rot_half(dq_r * s4)
        dsk = dk_r * c1 - _rot_half(dk_r * s1)
        dproj16 = jnp.concatenate(
            [dgq_ref[...], dgk_ref[...], dgv_ref[...], dgg_ref[...], dsq, dsk, dsv_ref[...], dga], axis=1).astype(BF16)
        dproj_ref[...] = dproj16
        dn = _nn(dproj16, win_ref[...])

        @pl.when(first)
        def _():
            dwa2_ref[...] = pa

        @pl.when(jnp.logical_not(first))
        def _():
            dwa2_ref[...] += pa
        hn, rh = _rms(h_ref[...])
        _acc_add(dg_ref, first, _colsum(dn * hn))
        dh_ref[...] = dho_ref[...] + _rms_bwd(hn, rh, g_ref[...], dn)

    rs = lambda c: _row_spec(tm, c)
    return pl.pallas_call(
        body, name="mix_in_bwd", grid=(rows // tm,),
        in_specs=[rs(D_MODEL), rs(D_MODEL), VMEM_SPEC, VMEM_SPEC, VMEM_SPEC, rs(128), rs(128), rs(256), rs(128),
                  rs(256), rs(256), rs(512), rs(512), rs(512), rs(128), rs(128), rs(256)],
        out_specs=[rs(D_MODEL), rs(P_END), pl.BlockSpec((128, 256), lambda i: (0, 0)), _acc_spec(D_MODEL), _acc_spec(256)],
        out_shape=[jax.ShapeDtypeStruct((rows, D_MODEL), F32), jax.ShapeDtypeStruct((rows, P_END), BF16),
                   jax.ShapeDtypeStruct((128, 256), F32), jax.ShapeDtypeStruct((8, D_MODEL), F32),
                   jax.ShapeDtypeStruct((8, 256), F32)],
        compiler_params=_params(("arbitrary",)),
    )(dh_out, h, g, win_p, wa2_p, cos, sin, loga, ga, dgq, dgk, dgv, dgg, dsq, dsk, dsv, dloga)


def _rope_tables(rows):
    pos = (jnp.arange(rows, dtype=jnp.int32) - PAD_ROWS).astype(F32)
    inv_freq = 1.0 / (ROPE_THETA ** (jnp.arange(0, SWA_HD, 2, dtype=F32) / SWA_HD))
    ang = pos[:, None] * inv_freq[None, :]
    return jnp.tile(jnp.cos(ang), (1, 4)), jnp.tile(jnp.sin(ang), (1, 4))


def _local_step(x, tgt, front, w, late_weights=None, on_grads=None, on_small=None):
    cos, sin = _rope_tables(x.shape[0] + BLK)
    g = {}

    def tell(group, names):
        for nm in names:
            g[nm] = grads_now[nm]
        return None if on_grads is None else on_grads(group, {nm: grads_now[nm] for nm in names})

    h1, a1, b1, s1, f1 = _ffn_fwd(x, w["ffn1_pre"], w["wg1"], w["wu1"], w["wd1"], w["ffn1_post"], front=front)
    if late_weights is not None:
        w = {**w, **late_weights("win", f1)}
    gq, gk, gv, gg, sq, sk, sv, ga, loga, bc, n2 = _mix_in(h1, w["mix_pre"], w["win"], w["wa2"], w["b_a"], cos, sin)
    (o_g, cat_g, sp), (o_s, cat_s, lse) = _side_by_side(
        [_gla_fwd(gq, gk, gv, gg, bc, w["gla_norm"]), _swa_fwd(sq, sk, sv, w["sinks"], w["swa_norm"])], "attention_fwd")
    if late_weights is not None:
        w = {**w, **late_weights("rest", lse)}
    h2, m, a2, b2, s2, f2, dy, loss = _ffn_fwd(h1, w["ffn2_pre"], w["wg2"], w["wu2"], w["wd2"], w["ffn2_post"], tgt,
                                               mixed=(cat_g, cat_s, w["wout"], w["mix_post"]))
    dh2, da, db, df, n3, g["ffn2_pre"], g["ffn2_post"] = _ffn_bwd_act(
        dy, h2, a2, b2, f2, w["ffn2_pre"], w["ffn2_post"], w["wg2"], w["wu2"], w["wd2"], "ffn2_bwd_act")
    grads_now = dict(wd2=_wgrad(s2, df, "ffn2_wgrad_down"), wg2=_wgrad(da, n3, "ffn2_wgrad_gate"),
                     wu2=_wgrad(db, n3, "ffn2_wgrad_up"))
    tok = tell("ffn2", ("wd2", "wg2", "wu2"))
    dcg, dcs, dm, g["mix_post"] = _mix_out_bwd(dh2, m, w["wout"], w["mix_post"] + (0.0 if tok is None else tok[0, 0]))
    (dgq, dgk, dgv, dgg, dloga, g["gla_norm"]), (dsq, dsk, dsv, g["sinks"], g["swa_norm"]) = _side_by_side(
        [_gla_bwd(dcg, o_g, gq, gk, gv, gg, bc, sp, w["gla_norm"]),
         _swa_bwd(dcs, o_s, sq, sk, sv, lse, w["sinks"], w["swa_norm"])], "attention_bwd")
    dh1, dproj, g["wa2"], g["mix_pre"], g["b_a"] = _mix_in_bwd(
        dh2, h1, w["mix_pre"], w["win"], w["wa2"], cos, sin, loga, ga, dgq, dgk, dgv, dgg, dsq, dsk, dsv, dloga)
    dh0, da, db, df, n1, g["ffn1_pre"], g["ffn1_post"] = _ffn_bwd_act(
        dh1, x, a1, b1, f1, w["ffn1_pre"], w["ffn1_post"], w["wg1"], w["wu1"], w["wd1"], "ffn1_bwd_act", front=front)
    tok = None if on_small is None else on_small(loss[0, 0], dh0, g)
    grads_now = dict(wd1=_wgrad(s1, df, "ffn1_wgrad_down", after=tok))
    tok = tell("ffn1_down", ("wd1",))
    grads_now = dict(wg1=_wgrad(da, n1, "ffn1_wgrad_gate", after=tok))
    tok = tell("ffn1_gate", ("wg1",))
    grads_now = dict(wu1=_wgrad(db, n1, "ffn1_wgrad_up", after=tok))
    tok = tell("ffn1_up", ("wu1",))
    grads_now = dict(win=_wgrad(dproj, n2, "win_wgrad", after=tok),
                     wout=jnp.concatenate([_wgrad(cat_g, dm, "wout_wgrad_gla", after=tok),
                                           _wgrad(cat_s, dm, "wout_wgrad_swa", after=tok)], axis=0))
    tell("mix", ("wout", "win"))
    return loss[0, 0], dh0, g


def _win_pad_rows(win_t):
    pad = jnp.zeros((P_END - P_GA - 16, win_t.shape[1]), win_t.dtype)
    return jnp.concatenate([win_t[0:1536], win_t[1552:2320], win_t[1536:1552], pad], axis=0)


def _win_unpad_rows(win_p):
    return jnp.concatenate([win_p[0:1536], win_p[P_GA:P_GA + 16], win_p[1536:P_GA]], axis=0)


def _place_on_mesh():
    return lax.axis_index("x"), lax.axis_index("y"), lax.axis_index("c")


def _dev_index(px, py, pc):
    return 4 * px + 2 * py + pc


def _other_devices(x, y, c):
    flip = lambda v, f: 1 - v if f else v
    return [(flip(x, fx), flip(y, fy), flip(c, fc)) for fx in (0, 1) for fy in (0, 1) for fc in (0, 1)][1:]


def _all_gather(shards):
    n = len(shards)

    def body(*refs):
        ins, outs = refs[:n], refs[n:2 * n]
        zeros_ref, send_sems, recv_sems, local_sems = refs[2 * n:]
        zeros_ref[...] = jnp.zeros_like(zeros_ref)
        x, y, c = _place_on_mesh()
        me, sibling = (x, y, c), (x, y, 1 - c)
        chips = [(1 - x, y), (x, 1 - y), (1 - x, 1 - y)]

        def rows(k, px, py, pc):
            r = ins[k].shape[0]
            return outs[k].at[pl.ds(pl.multiple_of(_dev_index(px, py, pc) * r, 8), r), :]

        def copy(k, slot, block, to, src=None):
            return pltpu.make_async_remote_copy(
                src_ref=rows(k, *block) if src is None else src, dst_ref=rows(k, *block),
                send_sem=send_sems.at[k, slot], recv_sem=recv_sems.at[k, slot], device_id=to, device_id_type=MESH)

        local = [pltpu.make_async_copy(ins[k], rows(k, *me), local_sems.at[k]) for k in range(n)]
        sends = []
        for k in range(n):
            local[k].start()
            sends.append(copy(k, 0, me, sibling, src=ins[k]))
            sends += [copy(k, 1 + j, me, (*chip, c), src=ins[k]) for j, chip in enumerate(chips)]
        for cp in sends:
            cp.start()
        for k in range(n):
            for j, chip in enumerate(chips):
                copy(k, 1 + j, (*chip, c), me).wait_recv()
                passed = copy(k, 4 + j, (*chip, c), sibling)
                passed.start()
                sends.append(passed)
        for k in range(n):
            copy(k, 0, sibling, me).wait_recv()
            for j, chip in enumerate(chips):
                copy(k, 4 + j, (*chip, 1 - c), me).wait_recv()
        for cp in sends:
            cp.wait_send()
        for cp in local:
            cp.wait()

    return pl.pallas_call(
        body, name="all_gather_weights",
        in_specs=[ANY_SPEC] * n, out_specs=[ANY_SPEC] * n + [VMEM_SPEC],
        out_shape=[jax.ShapeDtypeStruct((N_DEV * s.shape[0], s.shape[1]), s.dtype) for s in shards]
        + [jax.ShapeDtypeStruct((8, 128), F32)],
        scratch_shapes=[pltpu.SemaphoreType.DMA((n, 7)), pltpu.SemaphoreType.DMA((n, 7)), pltpu.SemaphoreType.DMA((n,))],
    )(*shards)


HBM_SPEC = pl.BlockSpec(memory_space=pltpu.HBM)
SEM_SPEC = pl.BlockSpec(memory_space=pltpu.SEMAPHORE)
DATAFLOW = pltpu.SideEffectType.DATAFLOW_SIDE_EFFECTING


GATHER, SCATTER, SCATTER_CHIPS = "gather", "scatter", "scatter among chips"


def _exchange_peers(kind):
    x, y, c = _place_on_mesh()
    if kind == SCATTER_CHIPS:
        peers = [(1 - x, y, c), (x, 1 - y, c), (1 - x, 1 - y, c)]
        return peers, [2 * p[0] + p[1] for p in peers], 2 * x + y, 4
    peers = _other_devices(x, y, c)
    return peers, [_dev_index(*p) for p in peers], _dev_index(x, y, c), N_DEV


def _exchange_copies(srcs, lands, send_sems, recv_sems, own_sems, kind, arriving):
    peers, theirs, me, blocks = _exchange_peers(kind)
    remote, local = [], []
    for k, (src, land) in enumerate(zip(srcs, lands)):
        r = land.shape[0] // blocks

        def block(ref, d):
            return ref.at[pl.ds(pl.multiple_of(d * r, 8), r), :]

        for f, (peer, him) in enumerate(zip(peers, theirs)):
            mine, his = (him, me) if arriving else (me, him)
            sem = len(peers) * k + f
            remote.append(pltpu.make_async_remote_copy(
                src_ref=src if kind == GATHER else block(src, his), dst_ref=block(land, mine),
                send_sem=send_sems.at[sem], recv_sem=recv_sems.at[sem], device_id=peer, device_id_type=MESH))
        local.append(pltpu.make_async_copy(src if kind == GATHER else block(src, me), block(land, me), own_sems.at[k]))
    return remote, local


def _exchange_start(srcs, kind, name):
    n = len(srcs)
    lands = [lax.empty((N_DEV * s.shape[0], s.shape[1]) if kind == GATHER else s.shape, s.dtype) for s in srcs]
    sems = (3 if kind == SCATTER_CHIPS else 7) * n

    def body(*refs):
        remote, local = _exchange_copies(refs[:n], refs[n:2 * n], *refs[2 * n:2 * n + 3], kind, False)
        for cp in remote + local:
            cp.start()
        refs[-1][...] = jnp.zeros_like(refs[-1])

    both = list(srcs) + list(lands)
    outs = pl.pallas_call(
        body, name=name,
        out_shape=(pltpu.SemaphoreType.DMA((sems,)), pltpu.SemaphoreType.DMA((sems,)), pltpu.SemaphoreType.DMA((n,)),
                   *[pltpu.HBM(a.shape, a.dtype) for a in both], jax.ShapeDtypeStruct((8, 128), F32)),
        in_specs=[HBM_SPEC] * (2 * n), out_specs=(SEM_SPEC, SEM_SPEC, SEM_SPEC, *[HBM_SPEC] * (2 * n), VMEM_SPEC),
        input_output_aliases={i: 3 + i for i in range(2 * n)},
        compiler_params=pltpu.CompilerParams(has_side_effects=DATAFLOW),
    )(*[pltpu.with_memory_space_constraint(a, pltpu.HBM) for a in both])
    return outs[0:3], outs[3:3 + n], outs[3 + n:3 + 2 * n], outs[-1]


def _exchange_wait(started, kind, after, name):
    sems, srcs, lands, _ = started
    n = len(srcs)

    def body(*refs):
        args = (refs[:n], refs[n:2 * n], *refs[2 * n:2 * n + 3], kind)
        going, local = _exchange_copies(*args, False)
        for cp in going:
            cp.wait_send()
        for cp in local:
            cp.wait()
        for cp in _exchange_copies(*args, True)[0]:
            cp.wait_recv()

    both = list(srcs) + list(lands)
    outs = pl.pallas_call(
        body, name=name, out_shape=[pltpu.HBM(a.shape, a.dtype) for a in both],
        in_specs=[HBM_SPEC] * (2 * n) + [SEM_SPEC, SEM_SPEC, SEM_SPEC, ANY_SPEC], out_specs=[HBM_SPEC] * (2 * n),
        input_output_aliases={i: i for i in range(2 * n)},
        compiler_params=pltpu.CompilerParams(has_side_effects=DATAFLOW),
    )(*both, *sems, after)
    return outs[n:]


def _sibling_reduce(part, name):
    r, cols = part.shape[0] // N_DEV, part.shape[1]

    def body(p_ref, o_ref, mine, got, send_sems, recv_sems, own_sems):
        x, y, c = _place_on_mesh()

        def block(d):
            return p_ref.at[pl.ds(pl.multiple_of(d * r, 8), r), :]
        swaps = [pltpu.make_async_remote_copy(
            src_ref=block(2 * j + 1 - c), dst_ref=got.at[j], send_sem=send_sems.at[j], recv_sem=recv_sems.at[j],
            device_id=(x, y, 1 - c), device_id_type=MESH) for j in range(4)]
        keeps = [pltpu.make_async_copy(block(2 * j + c), mine.at[j], own_sems.at[j]) for j in range(4)]
        for cp in swaps + keeps:
            cp.start()
        for j in range(4):
            keeps[j].wait()
            swaps[j].wait()
            o_ref[pl.ds(j * r, r), :] = (mine[j].astype(F32) + got[j].astype(F32)).astype(o_ref.dtype)

    return pl.pallas_call(
        body, name=name, in_specs=[ANY_SPEC], out_specs=VMEM_SPEC,
        out_shape=jax.ShapeDtypeStruct((4 * r, cols), part.dtype),
        scratch_shapes=[pltpu.VMEM((4, r, cols), part.dtype), pltpu.VMEM((4, r, cols), part.dtype),
                        pltpu.SemaphoreType.DMA((4,)), pltpu.SemaphoreType.DMA((4,)), pltpu.SemaphoreType.DMA((4,))],
        compiler_params=pltpu.CompilerParams(vmem_limit_bytes=32 << 20),
    )(part)


def _sum_partials(parts, name, blocks=N_DEV):
    n = len(parts)

    def body(*refs):
        ins, outs = refs[:n], refs[n:]
        first = pl.program_id(0) == 0
        for i_ref, o_ref in zip(ins, outs):
            v = i_ref[...].astype(F32)

            @pl.when(first)
            def _():
                o_ref[...] = v

            @pl.when(jnp.logical_not(first))
            def _():
                o_ref[...] += v

    shapes = [(p.shape[0] // blocks, p.shape[1]) for p in parts]
    return pl.pallas_call(
        body, name=name, grid=(blocks,),
        in_specs=[pl.BlockSpec(s, lambda j: (j, 0)) for s in shapes],
        out_specs=[pl.BlockSpec(s, lambda j: (0, 0)) for s in shapes],
        out_shape=[jax.ShapeDtypeStruct(s, F32) for s in shapes],
        compiler_params=_params(("arbitrary",)),
    )(*parts)


def _adamw_update(w, g, m, v):
    m = ADAM_B1 * m + (1.0 - ADAM_B1) * g
    v = ADAM_B2 * v + (1.0 - ADAM_B2) * (g * g)
    m_hat = m * (1.0 / (1.0 - ADAM_B1 ** ADAM_STEP))
    v_hat = v * (1.0 / (1.0 - ADAM_B2 ** ADAM_STEP))
    return -ADAM_LR * (m_hat / (jnp.sqrt(v_hat) + ADAM_EPS) + ADAM_WD * w), m, v


def _sum_adamw(parts, w, m, v, blocks, name):
    shape = w.shape

    def body(p_ref, w_ref, m_ref, v_ref, g_ref, d_ref, mo_ref, vo_ref):
        j = pl.program_id(0)
        part = p_ref[...].astype(F32)

        @pl.when(j == 0)
        def _():
            g_ref[...] = part

        @pl.when(j > 0)
        def _():
            g_ref[...] += part

        @pl.when(j == blocks - 1)
        def _():
            d_ref[...], mo_ref[...], vo_ref[...] = _adamw_update(w_ref[...], g_ref[...], m_ref[...], v_ref[...])

    held = pl.BlockSpec(shape, lambda j: (0, 0))
    return pl.pallas_call(
        body, name=name, grid=(blocks,),
        in_specs=[pl.BlockSpec(shape, lambda j: (j, 0)), held, held, held],
        out_specs=[held] * 4, out_shape=[jax.ShapeDtypeStruct(shape, F32)] * 4,
        compiler_params=_params(("arbitrary",)),
    )(parts, w, m, v)


def _adamw(ws, gs, ms, vs, name):
    n = len(ws)

    def body(*refs):
        w_r, g_r, m_r, v_r = refs[:n], refs[n:2 * n], refs[2 * n:3 * n], refs[3 * n:4 * n]
        d_o, m_o, v_o = refs[4 * n:5 * n], refs[5 * n:6 * n], refs[6 * n:7 * n]
        for k in range(n):
            d_o[k][...], m_o[k][...], v_o[k][...] = _adamw_update(w_r[k][...], g_r[k][...], m_r[k][...], v_r[k][...])

    shapes = [jax.ShapeDtypeStruct(w.shape, F32) for w in ws]
    outs = pl.pallas_call(
        body, name=name, in_specs=[VMEM_SPEC] * (4 * n), out_specs=[VMEM_SPEC] * (3 * n), out_shape=shapes * 3,
        compiler_params=pltpu.CompilerParams(vmem_limit_bytes=56 << 20),
    )(*ws, *gs, *ms, *vs)
    return outs[:n], outs[n:2 * n], outs[2 * n:]


WEIGHT_NAMES = ("meta_tokens", "ffn1_pre_norm", "ffn1_w_gate", "ffn1_w_up", "ffn1_w_down", "ffn1_post_norm", "mix_pre_norm",
                "w_in", "gla_w_a2", "gla_b_a", "gla_out_norm", "swa_sinks", "swa_out_norm", "w_out", "mix_post_norm",
                "ffn2_pre_norm", "ffn2_w_gate", "ffn2_w_up", "ffn2_w_down", "ffn2_post_norm")
WIN_SHARD = D_IN // N_DEV
WIN_SHARD_PAD = 304
SLAB_VECTORS = ("ffn1_pre", "ffn1_post", "mix_pre", "mix_post", "ffn2_pre", "ffn2_post")
SLAB_ROWS = 32


def kernel(x, meta_tokens, ffn1_pre_norm, ffn1_w_gate, ffn1_w_up, ffn1_w_down, ffn1_post_norm, mix_pre_norm, w_in, gla_w_a2, gla_b_a, gla_out_norm, swa_sinks, swa_out_norm, w_out, mix_post_norm, ffn2_pre_norm, ffn2_w_gate, ffn2_w_up, ffn2_w_down, ffn2_post_norm, loss_target, m_meta_tokens, m_ffn1_pre_norm, m_ffn1_w_gate, m_ffn1_w_up, m_ffn1_w_down, m_ffn1_post_norm, m_mix_pre_norm, m_w_in, m_gla_w_a2, m_gla_b_a, m_gla_out_norm, m_swa_sinks, m_swa_out_norm, m_w_out, m_mix_post_norm, m_ffn2_pre_norm, m_ffn2_w_gate, m_ffn2_w_up, m_ffn2_w_down, m_ffn2_post_norm, v_meta_tokens, v_ffn1_pre_norm, v_ffn1_w_gate, v_ffn1_w_up, v_ffn1_w_down, v_ffn1_post_norm, v_mix_pre_norm, v_w_in, v_gla_w_a2, v_gla_b_a, v_gla_out_norm, v_swa_sinks, v_swa_out_norm, v_w_out, v_mix_post_norm, v_ffn2_pre_norm, v_ffn2_w_gate, v_ffn2_w_up, v_ffn2_w_down, v_ffn2_post_norm):
    given = dict(locals())
    W = {n: given[n] for n in WEIGHT_NAMES}
    M = {n: given["m_" + n] for n in WEIGHT_NAMES}
    V = {n: given["v_" + n] for n in WEIGHT_NAMES}
    dev = _dev_index(*_place_on_mesh())

    def t16(w):
        return w[0].T.astype(BF16)

    small = jnp.concatenate([W["meta_tokens"], jnp.pad(W["gla_w_a2"][0], ((0, 0), (0, 96)))], axis=0)
    wg1, wu1, wd1, small_g, gathered_zeros = _all_gather(
        [t16(W["ffn1_w_gate"]), t16(W["ffn1_w_up"]), W["ffn1_w_down"][0].astype(BF16), small])
    def after_zero(shard, zeros):
        return shard + zeros[0:1, 0:1].astype(shard.dtype)
    win_shard = jnp.pad(t16(W["w_in"]), ((0, WIN_SHARD_PAD - WIN_SHARD), (0, 0)))
    win_shard = after_zero(win_shard, gathered_zeros)
    mid = _exchange_start([win_shard], GATHER, "gather_w_in_start")
    late_shards = [after_zero(W["w_out"][0].astype(BF16), mid[3]), t16(W["ffn2_w_gate"]), t16(W["ffn2_w_up"]),
                   W["ffn2_w_down"][0].astype(BF16)]
    late = _exchange_start(late_shards, GATHER, "gather_late_weights_start")

    def late_weights(what, after):
        if what == "win":
            win_g, = _exchange_wait(mid, GATHER, after, "gather_w_in_wait")
            win_t = win_g.reshape(N_DEV, WIN_SHARD_PAD, D_MODEL)[:, :WIN_SHARD].reshape(D_IN, D_MODEL)
            return dict(win=_win_pad_rows(win_t))
        wout, wg2, wu2, wd2 = _exchange_wait(late, GATHER, after, "gather_late_weights_wait")
        return dict(wout=wout, wg2=wg2, wu2=wu2, wd2=wd2)

    small_g = small_g.reshape(N_DEV, 32, 128)
    meta_full = small_g[:, :N_META].transpose(1, 0, 2).reshape(N_META, D_MODEL)
    wa2_full = small_g[:, N_META:, :32].transpose(1, 0, 2).reshape(16, 256)
    w = dict(
        ffn1_pre=W["ffn1_pre_norm"] + late[3][0, 0], ffn1_post=W["ffn1_post_norm"], mix_pre=W["mix_pre_norm"],
        mix_post=W["mix_post_norm"], ffn2_pre=W["ffn2_pre_norm"], ffn2_post=W["ffn2_post_norm"], b_a=W["gla_b_a"],
        gla_norm=W["gla_out_norm"], sinks=W["swa_sinks"], swa_norm=W["swa_out_norm"], wg1=wg1, wu1=wu1, wd1=wd1,
        wa2=jnp.pad(wa2_full, ((0, 112), (0, 0))))

    in_flight = []

    def on_grads(group, grads):
        parts = []
        for nm, p in grads.items():
            if nm == "win":
                p = _win_unpad_rows(p).reshape(N_DEV, WIN_SHARD, D_MODEL)
                p = jnp.pad(p, ((0, 0), (0, WIN_SHARD_PAD - WIN_SHARD), (0, 0))).reshape(N_DEV * WIN_SHARD_PAD, D_MODEL)
            parts.append(p)
        kind = SCATTER if group == "ffn2" else SCATTER_CHIPS
        if kind == SCATTER_CHIPS:
            parts = [_sibling_reduce(p, "pair_" + group + "_" + nm) for nm, p in zip(grads, parts)]
        started = _exchange_start(parts, kind, "scatter_" + group + "_start")
        in_flight.append((group, list(grads), started, kind))
        return started[3]

    small_flight = []

    def on_small(loss, dh0, g):
        packed = jnp.concatenate([g["b_a"][0:1], g["gla_norm"][0:1], g["sinks"][0:1], g["swa_norm"][0:1]], axis=1)
        slab = jnp.concatenate([g[k][0:1] for k in SLAB_VECTORS] + [packed, jnp.full((1, D_MODEL), loss, F32),
                               g["wa2"][:16].reshape(4, D_MODEL), jnp.zeros((4, D_MODEL), F32), dh0[PAD_ROWS:BLK]], axis=0)
        small_flight.append(_exchange_start([slab], GATHER, "gather_small_grads_start"))
        return small_flight[0][3]

    front = jnp.concatenate([jnp.zeros((PAD_ROWS, D_MODEL), F32), meta_full], axis=0)
    loss, dh0, g = _local_step(x[0], loss_target[0], front, w, late_weights, on_grads, on_small)
    grad_x = dh0[BLK:][None]

    land, = _exchange_wait(small_flight[0], GATHER, in_flight[-1][2][3], "gather_small_grads_wait")
    tot = _sum_partials([land], "sum_small_grads")[0]
    loss = tot[7, 0]
    small_grads = dict(
        ffn1_pre_norm=tot[0:1], ffn1_post_norm=tot[1:2], mix_pre_norm=tot[2:3], mix_post_norm=tot[3:4],
        ffn2_pre_norm=tot[4:5], ffn2_post_norm=tot[5:6], gla_b_a=tot[6:7, 0:256], gla_out_norm=tot[6:7, 256:384],
        swa_sinks=tot[6:7, 384:392], swa_out_norm=tot[6:7, 512:1024],
        gla_w_a2=lax.dynamic_slice_in_dim(tot[8:12].reshape(16, 256), dev * 32, 32, axis=1)[None],
        meta_tokens=lax.dynamic_slice_in_dim(tot[16:32], dev * 128, 128, axis=1))

    big = dict(wg1=("ffn1_w_gate", True), wu1=("ffn1_w_up", True), wd1=("ffn1_w_down", False), win=("w_in", True),
               wout=("w_out", False), wg2=("ffn2_w_gate", True), wu2=("ffn2_w_up", True), wd2=("ffn2_w_down", False))
    grads = dict(small_grads)
    delta, new_m, new_v = {}, {}, {}
    names = [n for n in WEIGHT_NAMES if n not in [full for full, _ in big.values()]]
    two_d = lambda a: a.reshape(-1, a.shape[-1])
    d_, m_, v_ = _adamw([two_d(W[n]) for n in names], [two_d(grads[n]) for n in names],
                        [two_d(M[n]) for n in names], [two_d(V[n]) for n in names], "adamw_small")
    for k, n in enumerate(names):
        delta[n], new_m[n], new_v[n] = d_[k].reshape(W[n].shape), m_[k].reshape(W[n].shape), v_[k].reshape(W[n].shape)

    before_wait = d_[0] + in_flight[-1][2][3][0, 0]
    for group, shorts, started, kind in in_flight:
        lands = _exchange_wait(started, kind, before_wait, "scatter_" + group + "_wait")
        blocks = 4 if kind == SCATTER_CHIPS else N_DEV
        for short, land in zip(shorts, lands):
            n, transposed = big[short]
            to_slab = (lambda a: a[0].T) if transposed else (lambda a: a[0])
            from_slab = (lambda a: a.T[None]) if transposed else (lambda a: a[None])
            if short == "win":
                g_slab = _sum_partials([land], "sum_" + n, blocks)[0][:WIN_SHARD]
                d_, m_, v_ = _adamw([to_slab(W[n])], [g_slab], [to_slab(M[n])], [to_slab(V[n])], "adamw_" + n)
                d_, m_, v_ = d_[0], m_[0], v_[0]
            else:
                g_slab, d_, m_, v_ = _sum_adamw(land, to_slab(W[n]), to_slab(M[n]), to_slab(V[n]), blocks, "adamw_" + n)
            grads[n], delta[n], new_m[n], new_v[n] = from_slab(g_slab), from_slab(d_), from_slab(m_), from_slab(v_)
            before_wait = d_
    return (loss, grad_x, *[grads[n] for n in WEIGHT_NAMES], *[delta[n] for n in WEIGHT_NAMES],
            *[new_m[n] for n in WEIGHT_NAMES], *[new_v[n] for n in WEIGHT_NAMES])
```

```python
import math

import jax
import jax.numpy as jnp
from jax import lax
from jax.experimental import pallas as pl
from jax.experimental.pallas import tpu as pltpu

F32, BF16 = jnp.float32, jnp.bfloat16

D_MODEL = 1024
D_FF = 2816
N_META = 16
BLK = 128
PAD_ROWS = BLK - N_META
GLA_DK = 64
SWA_HD = 64
SWA_HEADS = 8
GLA_TAU = 16.0
NORM_EPS = 1e-6
NEG_INF = -1e30
ROPE_THETA = 10000.0
P_GQ, P_GK, P_GV, P_GG, P_SQ, P_SK, P_SV, P_GA, P_END = 0, 256, 512, 1024, 1536, 2048, 2176, 2304, 2432
D_IN = 2320
IN_SPLITS = (256, 256, 512, 512, 16, 512, 128, 128)
FF_TILE = 2816
WGRAD_TILE_MAX = 2432
N_DEV = 8
MESH = pl.DeviceIdType.MESH

ADAM_LR, ADAM_B1, ADAM_B2, ADAM_EPS, ADAM_WD, ADAM_STEP = 0.001, 0.9, 0.999, 1e-08, 0.01, 10

V7X_VMEM_BYTES = 64 << 20
VMEM_SPEC = pl.BlockSpec(memory_space=pltpu.VMEM)
SMEM_SPEC = pl.BlockSpec(memory_space=pltpu.SMEM)
ANY_SPEC = pl.BlockSpec(memory_space=pl.ANY)


def _params(semantics, vmem_mb=56):
    return pltpu.CompilerParams(dimension_semantics=semantics, vmem_limit_bytes=vmem_mb << 20)


def _row_tile(rows):
    return 416 if rows % 416 == 0 else BLK


def _blocks_per_step(blocks):
    return 5 if blocks % 5 == 0 else 1


def _nn(a, b):
    return lax.dot_general(a, b, (((1,), (0,)), ((), ())), preferred_element_type=F32)


def _nt(a, b):
    return lax.dot_general(a, b, (((1,), (1,)), ((), ())), preferred_element_type=F32)


def _tn(a, b):
    return lax.dot_general(a, b, (((0,), (0,)), ((), ())), preferred_element_type=F32)


def _rms(x):
    r = lax.rsqrt(jnp.mean(x * x, axis=-1, keepdims=True) + NORM_EPS)
    return x * r, r


def _rms_bwd(xn, r, w, dy):
    g = dy * w
    return r * (g - xn * jnp.mean(g * xn, axis=-1, keepdims=True))


def _sigmoid(x):
    return 1.0 / (1.0 + jnp.exp(-x))


def _colsum(x):
    return jnp.sum(x, axis=0, keepdims=True)


def _split_bf16(x):
    hi = x.astype(BF16)
    lo = (x - hi.astype(F32)).astype(BF16)
    return hi, lo


def _tri(lower):
    r = lax.broadcasted_iota(jnp.int32, (BLK, BLK), 0)
    c = lax.broadcasted_iota(jnp.int32, (BLK, BLK), 1)
    return (r >= c) if lower else (c >= r)


def _half_mask(width, half):
    lane = lax.broadcasted_iota(jnp.int32, (1, width), 1)
    return ((lane % 128) < 64) if half == 0 else ((lane % 128) >= 64)


def _rot_half(x):
    w = x.shape[-1]
    lane = lax.broadcasted_iota(jnp.int32, (1, w), 1)
    return jnp.where((lane % SWA_HD) < SWA_HD // 2, -pltpu.roll(x, w - SWA_HD // 2, 1), pltpu.roll(x, SWA_HD // 2, 1))


def _row_spec(tm, cols):
    return pl.BlockSpec((tm, cols), lambda i: (i, 0))


def _acc_spec(cols):
    return pl.BlockSpec((8, cols), lambda i: (0, 0))


def _acc_add(ref, first, value):
    @pl.when(first)
    def _():
        ref[...] = jnp.zeros_like(ref)
    ref[0:1, :] += value


def _behind_spec(tm):
    return pl.BlockSpec((pl.Element(tm), pl.Element(D_MODEL)),
                        lambda i: (pl.multiple_of(jnp.maximum(i * tm - BLK, 0), math.gcd(tm, BLK)), 0))


def _behind_front(ref, i, tm, front):
    blk = ref[...]
    return jnp.where(i == 0, jnp.concatenate([front, blk[0:tm - BLK]], axis=0), blk)


def _ffn_fwd(h, gpre, wg_t, wu_t, wd, gpost, tgt=None, front=None, mixed=None):
    with_loss, with_front, with_mixed = tgt is not None, front is not None, mixed is not None
    rows = h.shape[0] + (BLK if with_front else 0)
    tm = _row_tile(rows)
    nf = D_FF // FF_TILE

    def body(*refs):
        refs = list(refs)
        h_ref, gpre_ref, wg_ref, wu_ref, wd_ref, gpost_ref = refs[:6]
        del refs[:6]
        front_ref = refs.pop(0) if with_front else None
        cg_ref, cs_ref, wo_ref, gm_ref = (refs.pop(0), refs.pop(0), refs.pop(0), refs.pop(0)) if with_mixed else (None,) * 4
        t_ref = refs.pop(0) if with_loss else None
        hm_ref, m_ref = (refs.pop(0), refs.pop(0)) if with_mixed else (None, None)
        ho_ref = None if with_loss else refs.pop(0)
        a_ref, b_ref, s_ref, f_ref = refs[:4]
        dy_ref, loss_ref = refs[4:6] if with_loss else (None, None)
        acc = refs[-1]
        i = pl.program_id(0)
        h_in = _behind_front(h_ref, i, tm, front_ref[...]) if with_front else h_ref[...]
        if with_mixed:
            m = _nn(cg_ref[...], wo_ref[0:512, :]) + _nn(cs_ref[...], wo_ref[512:1024, :])
            m_ref[...] = m
            mn, _ = _rms(m)
            h_in = h_in + mn * gm_ref[...]
            hm_ref[...] = h_in
        hn, _ = _rms(h_in)
        n16 = (hn * gpre_ref[...]).astype(BF16)
        for j in range(nf):
            cols = slice(j * FF_TILE, (j + 1) * FF_TILE)
            a = _nt(n16, wg_ref[cols, :])
            b = _nt(n16, wu_ref[cols, :])
            a_ref[:, cols] = a.astype(BF16)
            b_ref[:, cols] = b.astype(BF16)
            s16 = (a * _sigmoid(a) * b).astype(BF16)
            s_ref[:, cols] = s16
            part = _nn(s16, wd_ref[cols, :])
            if j == 0:
                acc[...] = part
            else:
                acc[...] += part
        f = acc[...]
        f_ref[...] = f
        fn, _ = _rms(f)
        y = h_in + 0.5 * (fn * gpost_ref[...])
        if not with_loss:
            ho_ref[...] = y
        else:
            row = i * tm + lax.broadcasted_iota(jnp.int32, (tm, 1), 0)
            err = jnp.where(row >= BLK, y - _behind_front(t_ref, i, tm, jnp.zeros((BLK, D_MODEL), F32)), 0.0)
            dy_ref[...] = err * (1.0 / D_MODEL)
            part = 0.5 * jnp.sum(jnp.sum(err * err, axis=-1, keepdims=True) * (1.0 / D_MODEL), axis=0, keepdims=True)

            @pl.when(i == 0)
            def _():
                loss_ref[...] = jnp.zeros_like(loss_ref)
            loss_ref[...] += part

    row_f32 = _row_spec(tm, D_MODEL)
    behind = _behind_spec(tm)
    in_specs = [behind if with_front else row_f32, VMEM_SPEC, VMEM_SPEC, VMEM_SPEC, VMEM_SPEC, VMEM_SPEC]
    wide, full = jax.ShapeDtypeStruct((rows, D_FF), BF16), jax.ShapeDtypeStruct((rows, D_MODEL), F32)
    out_specs = [_row_spec(tm, D_FF), _row_spec(tm, D_FF), _row_spec(tm, D_FF), row_f32]
    out_shape = [wide, wide, wide, full]
    args = [h, gpre, wg_t, wu_t, wd, gpost]
    if not with_loss:
        out_specs.insert(0, row_f32)
        out_shape.insert(0, full)
    if with_front:
        in_specs.append(VMEM_SPEC)
        args.append(front)
    if with_mixed:
        in_specs += [_row_spec(tm, 512), _row_spec(tm, 512), VMEM_SPEC, VMEM_SPEC]
        args += list(mixed)
        out_specs = [row_f32, row_f32] + out_specs
        out_shape = [full, full] + out_shape
    if with_loss:
        in_specs.append(behind)
        args.append(tgt)
        out_specs += [row_f32, pl.BlockSpec((8, 128), lambda i: (0, 0))]
        out_shape += [jax.ShapeDtypeStruct((rows, D_MODEL), F32), jax.ShapeDtypeStruct((8, 128), F32)]
    return pl.pallas_call(
        body, name="ffn_fwd_loss" if with_loss else "ffn_fwd", grid=(rows // tm,),
        in_specs=in_specs, out_specs=out_specs, out_shape=out_shape,
        scratch_shapes=[pltpu.VMEM((tm, D_MODEL), F32)],
        compiler_params=_params(("arbitrary",), vmem_mb=62 if with_mixed else 56),
    )(*args)


def _ffn_bwd_act(dh_out, h, a, b, f, gpre, gpost, wg_t, wu_t, wd, name, front=None):
    with_front = front is not None
    rows = dh_out.shape[0]
    tm = _row_tile(rows)
    nf = D_FF // FF_TILE

    def body(dho_ref, h_ref, a_ref, b_ref, f_ref, gpre_ref, gpost_ref, wg_ref, wu_ref, wd_ref, *rest):
        front_ref = rest[0] if with_front else None
        dh_ref, da_ref, db_ref, df_ref, n_ref, dgpre_ref, dgpost_ref, acc = rest[-8:]
        first = pl.program_id(0) == 0
        dho = dho_ref[...]
        drr = 0.5 * dho
        fn, rf = _rms(f_ref[...])
        _acc_add(dgpost_ref, first, _colsum(drr * fn))
        df16 = _rms_bwd(fn, rf, gpost_ref[...], drr).astype(BF16)
        df_ref[...] = df16
        h_in = _behind_front(h_ref, pl.program_id(0), tm, front_ref[...]) if with_front else h_ref[...]
        hn, rh = _rms(h_in)
        n_ref[...] = (hn * gpre_ref[...]).astype(BF16)
        for j in range(nf):
            cols = slice(j * FF_TILE, (j + 1) * FF_TILE)
            ds = _nt(df16, wd_ref[cols, :])
            av = a_ref[:, cols].astype(F32)
            bv = b_ref[:, cols].astype(F32)
            sg = _sigmoid(av)
            db16 = (ds * (av * sg)).astype(BF16)
            da16 = (ds * bv * (sg * (1.0 + av * (1.0 - sg)))).astype(BF16)
            da_ref[:, cols] = da16
            db_ref[:, cols] = db16
            part = _nn(da16, wg_ref[cols, :]) + _nn(db16, wu_ref[cols, :])
            if j == 0:
                acc[...] = part
            else:
                acc[...] += part
        dn = acc[...]
        _acc_add(dgpre_ref, first, _colsum(dn * hn))
        dh_ref[...] = dho + _rms_bwd(hn, rh, gpre_ref[...], dn)

    row_f32 = _row_spec(tm, D_MODEL)
    row_ff = _row_spec(tm, D_FF)
    return pl.pallas_call(
        body, name=name, grid=(rows // tm,),
        in_specs=[row_f32, _behind_spec(tm) if with_front else row_f32, row_ff, row_ff, row_f32,
                  VMEM_SPEC, VMEM_SPEC, VMEM_SPEC, VMEM_SPEC, VMEM_SPEC] + ([VMEM_SPEC] if with_front else []),
        out_specs=[row_f32, row_ff, row_ff, row_f32, row_f32, _acc_spec(D_MODEL), _acc_spec(D_MODEL)],
        out_shape=[jax.ShapeDtypeStruct((rows, D_MODEL), F32), jax.ShapeDtypeStruct((rows, D_FF), BF16),
                   jax.ShapeDtypeStruct((rows, D_FF), BF16), jax.ShapeDtypeStruct((rows, D_MODEL), BF16),
                   jax.ShapeDtypeStruct((rows, D_MODEL), BF16), jax.ShapeDtypeStruct((8, D_MODEL), F32),
                   jax.ShapeDtypeStruct((8, D_MODEL), F32)],
        scratch_shapes=[pltpu.VMEM((tm, D_MODEL), F32)],
        compiler_params=_params(("arbitrary",), vmem_mb=62),
    )(dh_out, h, a, b, f, gpre, gpost, wg_t, wu_t, wd, *([front] if with_front else []))


def _wgrad(lhs, rhs, name, after=None):
    rows, width = lhs.shape
    tf = 256 if width % 256 == 0 else 128
    pieces = 5 if rows % 80 == 0 else 3 if rows % 48 == 0 else 1
    piece = rows // pieces

    def body(l_ref, r_hbm, *rest):
        o_ref, r_all, sems = rest[-3:]
        j = pl.program_id(0)

        def fetch(c):
            part = pl.ds(c * piece, piece)
            return pltpu.make_async_copy(r_hbm.at[part, :], r_all.at[part, :], sems.at[c])

        @pl.when(j == 0)
        def _():
            for c in range(pieces):
                fetch(c).start()
            total = None
            for c in range(pieces):
                fetch(c).wait()
                part = _tn(l_ref[c * piece:(c + 1) * piece, :], r_all[c * piece:(c + 1) * piece, :])
                total = part if total is None else total + part
            o_ref[...] = total.astype(BF16)

        @pl.when(j > 0)
        def _():
            o_ref[...] = _tn(l_ref[...], r_all[...]).astype(BF16)

    return pl.pallas_call(
        body, name=name, grid=(width // tf,),
        in_specs=[pl.BlockSpec((rows, tf), lambda j: (0, j)), ANY_SPEC] + ([] if after is None else [ANY_SPEC]),
        out_specs=pl.BlockSpec((tf, D_MODEL), lambda j: (j, 0)),
        out_shape=jax.ShapeDtypeStruct((width, D_MODEL), BF16),
        scratch_shapes=[pltpu.VMEM((rows, D_MODEL), BF16), pltpu.SemaphoreType.DMA((pieces,))],
        compiler_params=_params(("arbitrary",)),
    )(lhs, rhs, *([] if after is None else [after]))


def _chunk_cumsum(x, lower):
    tri = jnp.where(_tri(lower), 1.0, 0.0).astype(BF16)
    hi, lo = _split_bf16(x)
    return _nn(tri, hi) + _nn(tri, lo)


def _mix_in(h, g, win_p, wa2_p, b_a, cos, sin):
    rows = h.shape[0]
    tm = 640 if rows % 640 == 0 else BLK

    def body(h_ref, g_ref, win_ref, wa2_ref, ba_ref, cos_ref, sin_ref,
             gq_ref, gk_ref, gv_ref, gg_ref, sq_ref, sk_ref, sv_ref, ga_ref, loga_ref, bc_ref, n_ref):
        hn, _ = _rms(h_ref[...])
        n16 = (hn * g_ref[...]).astype(BF16)
        n_ref[...] = n16
        proj = _nt(n16, win_ref[...])
        gq_ref[...] = proj[:, P_GQ:P_GK]
        gk_ref[...] = proj[:, P_GK:P_GV]
        gv_ref[...] = proj[:, P_GV:P_GG].astype(BF16)
        gg_ref[...] = proj[:, P_GG:P_SQ]
        c1, s1 = cos_ref[...], sin_ref[...]
        c4 = jnp.concatenate([c1, c1, c1, c1], axis=1)
        s4 = jnp.concatenate([s1, s1, s1, s1], axis=1)
        sq = proj[:, P_SQ:P_SK]
        sk = proj[:, P_SK:P_SV]
        sq_ref[...] = (sq * c4 + _rot_half(sq) * s4).astype(BF16)
        sk_ref[...] = (sk * c1 + _rot_half(sk) * s1).astype(BF16)
        sv_ref[...] = proj[:, P_SV:P_GA].astype(BF16)
        ga = proj[:, P_GA:P_END]
        ga_ref[...] = ga
        z = _nn(ga, wa2_ref[...]) + ba_ref[...]
        loga = (jnp.minimum(z, 0.0) - jnp.log(1.0 + jnp.exp(-jnp.abs(z)))) * (1.0 / GLA_TAU)
        loga_ref[...] = loga
        for c in range(tm // BLK):
            rs = slice(c * BLK, (c + 1) * BLK)
            bc_ref[rs, :] = _chunk_cumsum(loga[rs, :], True)

    f32 = lambda c: jax.ShapeDtypeStruct((rows, c), F32)
    b16 = lambda c: jax.ShapeDtypeStruct((rows, c), BF16)
    rs = lambda c: _row_spec(tm, c)
    return pl.pallas_call(
        body, name="mix_in", grid=(rows // tm,),
        in_specs=[rs(D_MODEL), VMEM_SPEC, VMEM_SPEC, VMEM_SPEC, VMEM_SPEC, rs(128), rs(128)],
        out_specs=[rs(256), rs(256), rs(512), rs(512), rs(512), rs(128), rs(128), rs(128), rs(256), rs(256), rs(D_MODEL)],
        out_shape=[f32(256), f32(256), b16(512), f32(512), b16(512), b16(128), b16(128), f32(128), f32(256), f32(256),
                   b16(D_MODEL)],
        compiler_params=_params(("arbitrary",)),
    )(h, g, win_p, wa2_p, b_a, cos, sin)


def _side_by_side(parts, name):
    steps, per_step = parts[0]["steps"], parts[0]["per_step"]
    assert all((p["steps"], p["per_step"]) == (steps, per_step) for p in parts)
    counts = [[len(p[key]) for p in parts] for key in ("in_specs", "out_specs", "scratch_shapes")]

    def body(*refs):
        groups, pos = [], 0
        for kind in counts:
            groups.append([])
            for n in kind:
                groups[-1].append(refs[pos:pos + n])
                pos += n
        programs = [p["program"](*groups[0][k], *groups[1][k], *groups[2][k]) for k, p in enumerate(parts)]

        def blocks(c, carries):
            return tuple(block(c, carry) for (block, _, _), carry in zip(programs, carries))
        carries = lax.fori_loop(0, per_step, blocks, tuple(first for _, first, _ in programs))
        for (_, _, finish), carry in zip(programs, carries):
            finish(carry)

    outs = pl.pallas_call(
        body, name=name, grid=(steps,),
        in_specs=[s for p in parts for s in p["in_specs"]], out_specs=[s for p in parts for s in p["out_specs"]],
        out_shape=[s for p in parts for s in p["out_shape"]],
        scratch_shapes=[s for p in parts for s in p["scratch_shapes"]],
        compiler_params=_params(("arbitrary",)),
    )(*[a for p in parts for a in p["args"]])
    split, pos = [], 0
    for n in counts[1]:
        split.append(outs[pos:pos + n])
        pos += n
    return split


def _gla_factors(q, k, bc):
    bm = bc[BLK // 2 - 1:BLK // 2, :]
    bl = bc[BLK - 1:BLK, :]
    e_q, e_k, e_qe, e_kd = jnp.exp(bc - bm), jnp.exp(bm - bc), jnp.exp(bc), jnp.exp(bl - bc)
    return (q * e_q, k * e_k, q * e_qe, k * e_kd), (e_q, e_k, e_qe, e_kd), jnp.exp(bl)


def _gla_fwd(gq, gk, gv, gg, bc, wgn):
    rows = gq.shape[0]
    nc = rows // BLK
    per_step = _blocks_per_step(nc)
    scale = GLA_DK ** -0.5

    def body(q_ref, k_ref, v_ref, gg_ref, bc_ref, wgn_ref, o_ref, cat_ref, sp_ref, st):
        @pl.when(pl.program_id(0) == 0)
        def _():
            st[...] = jnp.zeros_like(st)
        low = _tri(True)
        wgn_v = wgn_ref[...]

        def chunk(c, carry):
            rr = pl.ds(pl.multiple_of(c * BLK, BLK), BLK)
            for p in range(2):
                sl = slice(128 * p, 128 * p + 128)
                (qt, kt, qe, kd), _, ebl = _gla_factors(q_ref[rr, sl] * scale, k_ref[rr, sl], bc_ref[rr, sl])
                s_prev = st[p]
                sp_ref[c, p] = s_prev
                s16 = s_prev.astype(BF16)
                qt16 = qt.astype(BF16)
                s_new = s_prev * ebl
                for hh in range(2):
                    hs = slice(128 * (2 * p + hh), 128 * (2 * p + hh) + 128)
                    lm = _half_mask(128, hh)
                    vh = v_ref[rr, hs]
                    pm = jnp.where(low, _nt(qt16, jnp.where(lm, kt, 0.0).astype(BF16)), 0.0)
                    o = _nn(pm.astype(BF16), vh) + _nt(jnp.where(lm, qe, 0.0).astype(BF16), s16)
                    s_new = s_new + _tn(vh, jnp.where(lm, kd, 0.0).astype(BF16))
                    o_ref[rr, hs] = o
                    on, _ = _rms(o)
                    gate = gg_ref[rr, hs]
                    cat_ref[rr, hs] = (on * wgn_v * (gate * _sigmoid(gate))).astype(BF16)
                st[p] = s_new
            return carry
        return chunk, 0, lambda carry: None

    rs = lambda c: _row_spec(per_step * BLK, c)
    return dict(
        program=body, steps=nc // per_step, per_step=per_step,
        in_specs=[rs(256), rs(256), rs(512), rs(512), rs(256), VMEM_SPEC],
        out_specs=[rs(512), rs(512), pl.BlockSpec((per_step, 2, 128, 128), lambda i: (i, 0, 0, 0))],
        out_shape=[jax.ShapeDtypeStruct((rows, 512), F32), jax.ShapeDtypeStruct((rows, 512), BF16),
                   jax.ShapeDtypeStruct((nc, 2, 128, 128), F32)],
        scratch_shapes=[pltpu.VMEM((2, 128, 128), F32)],
        args=(gq, gk, gv, gg, bc, wgn))


def _gla_bwd(dcat, o_all, gq, gk, gv, gg, bc, sp, wgn):
    rows = gq.shape[0]
    nc = rows // BLK
    per_step = _blocks_per_step(nc)
    steps = nc // per_step
    scale = GLA_DK ** -0.5

    def body(dc_ref, o_ref, q_ref, k_ref, v_ref, gg_ref, bc_ref, sp_ref, wgn_ref,
             dq_ref, dk_ref, dv_ref, dgg_ref, dla_ref, dwgn_ref, dst):
        first = pl.program_id(0) == 0

        @pl.when(first)
        def _():
            dst[...] = jnp.zeros_like(dst)
        low, upp = _tri(True), _tri(False)
        last_row = lax.broadcasted_iota(jnp.int32, (BLK, 1), 0) == BLK - 1
        wgn_v = wgn_ref[...]

        def chunk(c, dwgn):
            rr = pl.ds(pl.multiple_of((per_step - 1 - c) * BLK, BLK), BLK)
            for p in range(2):
                sl = slice(128 * p, 128 * p + 128)
                (qt, kt, qe, kd), (e_q, e_k, e_qe, e_kd), ebl = _gla_factors(
                    q_ref[rr, sl] * scale, k_ref[rr, sl], bc_ref[rr, sl])
                s_prev = sp_ref[per_step - 1 - c, p]
                s16 = s_prev.astype(BF16)
                ds_next = dst[p]
                ds16 = ds_next.astype(BF16)
                qt16 = qt.astype(BF16)
                ds_new = ds_next * ebl
                dqt = jnp.zeros((BLK, 128), F32)
                dkt = jnp.zeros((BLK, 128), F32)
                dqe = jnp.zeros((BLK, 128), F32)
                dkd = jnp.zeros((BLK, 128), F32)
                for hh in range(2):
                    hs = slice(128 * (2 * p + hh), 128 * (2 * p + hh) + 128)
                    lm = _half_mask(128, hh)
                    on, ro = _rms(o_ref[rr, hs])
                    gate = gg_ref[rr, hs]
                    sg = _sigmoid(gate)
                    si = gate * sg
                    dog = dc_ref[rr, hs]
                    dwgn = dwgn + _colsum(dog * si * on)
                    dgg_ref[rr, hs] = dog * (on * wgn_v) * (sg * (1.0 + gate * (1.0 - sg)))
                    do16 = _rms_bwd(on, ro, wgn_v, dog * si).astype(BF16)
                    vh = v_ref[rr, hs]
                    ktm16 = jnp.where(lm, kt, 0.0).astype(BF16)
                    qtm16 = jnp.where(lm, qt, 0.0).astype(BF16)
                    qem16 = jnp.where(lm, qe, 0.0).astype(BF16)
                    kdm16 = jnp.where(lm, kd, 0.0).astype(BF16)
                    p_t = jnp.where(upp, _nt(ktm16, qt16), 0.0)
                    dp_t = jnp.where(upp, _nt(vh, do16), 0.0)
                    dp = jnp.where(low, _nt(do16, vh), 0.0)
                    dv_ref[rr, hs] = _nn(p_t.astype(BF16), do16) + _nt(kdm16, ds16)
                    dqt = dqt + _nn(dp.astype(BF16), ktm16)
                    dkt = dkt + _nn(dp_t.astype(BF16), qtm16)
                    dqe = dqe + jnp.where(lm, _nn(do16, s16), 0.0)
                    dkd = dkd + jnp.where(lm, _nn(vh, ds16), 0.0)
                    ds_new = ds_new + _tn(do16, qem16)
                debl = _colsum(ds_next * s_prev)
                dq_ref[rr, sl] = (dqt * e_q + dqe * e_qe) * scale
                dk_ref[rr, sl] = dkt * e_k + dkd * e_kd
                dkd_kd = dkd * kd
                db = dqt * qt - dkt * kt + dqe * qe - dkd_kd
                db = jnp.where(last_row, db + (_colsum(dkd_kd) + debl * ebl), db)
                dla_ref[rr, sl] = _chunk_cumsum(db, False)
                dst[p] = ds_new
            return dwgn

        def finish(dwgn):
            _acc_add(dwgn_ref, first, dwgn)
        return chunk, jnp.zeros((1, 128), F32), finish

    rev = lambda c: pl.BlockSpec((per_step * BLK, c), lambda i: (steps - 1 - i, 0))
    f32 = lambda c: jax.ShapeDtypeStruct((rows, c), F32)
    return dict(
        program=body, steps=steps, per_step=per_step,
        in_specs=[rev(512), rev(512), rev(256), rev(256), rev(512), rev(512), rev(256),
                  pl.BlockSpec((per_step, 2, 128, 128), lambda i: (steps - 1 - i, 0, 0, 0)), VMEM_SPEC],
        out_specs=[rev(256), rev(256), rev(512), rev(512), rev(256), _acc_spec(128)],
        out_shape=[f32(256), f32(256), f32(512), f32(512), f32(256), jax.ShapeDtypeStruct((8, 128), F32)],
        scratch_shapes=[pltpu.VMEM((2, 128, 128), F32)],
        args=(dcat, o_all, gq, gk, gv, gg, bc, sp, wgn))


def _swa_masks(i):
    t = lax.broadcasted_iota(jnp.int32, (BLK, BLK), 0)
    c = lax.broadcasted_iota(jnp.int32, (BLK, BLK), 1)
    own_side = c <= t
    band_ok = i >= jnp.where(own_side, 1, 2)
    meta_ok = (c % N_META) <= jnp.where(i >= 1, N_META, t - PAD_ROWS)
    return own_side, band_ok, meta_ok, c // N_META


def _swa_blocks(ref, i):
    prev = pl.multiple_of(jnp.maximum(i - 1, 0) * BLK, BLK)
    own = pl.multiple_of(i * BLK, BLK)
    return jnp.concatenate([ref[pl.ds(prev, BLK), :], ref[pl.ds(own, BLK), :]], axis=0), prev, own


def _swa_meta_operand(ref):
    blk = ref[0:BLK, :]
    swapped = pltpu.roll(blk, 64, 1)
    lo = jnp.where(_half_mask(128, 0), blk, swapped)
    hi = jnp.where(_half_mask(128, 1), blk, swapped)
    meta = jnp.concatenate([lo, lo, hi, hi], axis=1)[PAD_ROWS:BLK, :]
    tiled = jnp.concatenate([meta] * SWA_HEADS, axis=0)
    j = lax.broadcasted_iota(jnp.int32, tiled.shape, 0)
    lane = lax.broadcasted_iota(jnp.int32, tiled.shape, 1)
    return jnp.where(j // N_META == lane // SWA_HD, tiled, jnp.zeros_like(tiled))


def _swa_meta_fold(acc):
    out = jnp.zeros((N_META, 128), F32)
    for hd in range(SWA_HEADS):
        half, kv = hd % 2, hd // 4
        piece = acc[N_META * hd:N_META * (hd + 1), 128 * (hd // 2):128 * (hd // 2) + 128]
        piece = jnp.where(_half_mask(128, half), piece, 0.0)
        out = out + (piece if half == kv else pltpu.roll(piece, 64, 1))
    return out


def _by_head(group, per_head):
    out = jnp.zeros((BLK, BLK), F32)
    for hd, v in enumerate(per_head):
        out = jnp.where(group == hd, v, out)
    return out


def _place(x, kv):
    if kv == 0:
        lo = jnp.where(_half_mask(128, 0), x, jnp.zeros_like(x))
        return lo, pltpu.roll(lo, 64, 1)
    hi = jnp.where(_half_mask(128, 1), x, jnp.zeros_like(x))
    return pltpu.roll(hi, 64, 1), hi


def _swa_fwd(sq, sk, sv, sinks, wn):
    rows = sq.shape[0]
    nb = rows // BLK
    per_step = _blocks_per_step(nb)
    scale = SWA_HD ** -0.5

    def body(q_ref, k_ref, v_ref, sink_ref, wn_ref, o_ref, cat_ref, lse_ref, kp, vp):
        step = pl.program_id(0)

        @pl.when(step == 0)
        def _():
            kp[...] = _swa_meta_operand(k_ref)
            vp[...] = _swa_meta_operand(v_ref)

        def one_block(c, carry):
            i = step * per_step + c
            rr = pl.ds(pl.multiple_of(c * BLK, BLK), BLK)
            own_side, band_ok, meta_ok, group = _swa_masks(i)
            k2, _, _ = _swa_blocks(k_ref, i)
            v2, _, _ = _swa_blocks(v_ref, i)
            kz = (_place(k2, 0), _place(k2, 1))
            vz = (_place(v2, 0), _place(v2, 1))
            q_all = q_ref[rr, :]
            s_meta = jnp.where(meta_ok, _nt(q_all, kp[...]) * scale, NEG_INF)
            s_band, m = [], []
            for hd in range(SWA_HEADS):
                kv, half = hd // 4, hd % 2
                q_pair = q_all[:, 128 * (hd // 2):128 * (hd // 2) + 128]
                s2 = _nt(q_pair, kz[kv][half])
                s = jnp.where(band_ok, jnp.where(own_side, s2[:, BLK:], s2[:, :BLK]) * scale, NEG_INF)
                top = jnp.maximum(jnp.max(s, axis=-1, keepdims=True),
                                  jnp.max(jnp.where(group == hd, s_meta, NEG_INF), axis=-1, keepdims=True))
                s_band.append(s)
                m.append(jnp.maximum(top, sink_ref[0, hd]))
            e_meta = jnp.exp(s_meta - _by_head(group, m))
            o_meta = _nn(e_meta.astype(BF16), vp[...])
            outs = []
            for pr in range(4):
                o_pair = o_meta[:, 128 * pr:128 * pr + 128]
                rden = []
                for half in range(2):
                    hd = 2 * pr + half
                    kv = hd // 4
                    e = jnp.exp(s_band[hd] - m[hd])
                    den = (jnp.sum(e, axis=-1, keepdims=True)
                           + jnp.sum(jnp.where(group == hd, e_meta, 0.0), axis=-1, keepdims=True)
                           + jnp.exp(sink_ref[0, hd] - m[hd]))
                    lse_ref[rr, hd:hd + 1] = m[hd] + jnp.log(den)
                    rden.append(1.0 / den)
                    e2 = jnp.concatenate([jnp.where(own_side, 0.0, e), jnp.where(own_side, e, 0.0)], axis=1).astype(BF16)
                    o_pair = o_pair + _nn(e2, vz[kv][half])
                outs.append(o_pair * jnp.where(_half_mask(128, 0), rden[0], rden[1]))
            o = jnp.concatenate(outs, axis=1)
            o_ref[rr, :] = o
            on, _ = _rms(o)
            cat_ref[rr, :] = (on * wn_ref[...]).astype(BF16)
            return carry
        return one_block, 0, lambda carry: None

    return dict(
        program=body, steps=nb // per_step, per_step=per_step,
        in_specs=[_row_spec(per_step * BLK, 512), VMEM_SPEC, VMEM_SPEC, SMEM_SPEC, VMEM_SPEC],
        out_specs=[_row_spec(per_step * BLK, 512), _row_spec(per_step * BLK, 512), _row_spec(per_step * BLK, SWA_HEADS)],
        out_shape=[jax.ShapeDtypeStruct((rows, 512), F32), jax.ShapeDtypeStruct((rows, 512), BF16),
                   jax.ShapeDtypeStruct((rows, SWA_HEADS), F32)],
        scratch_shapes=[pltpu.VMEM((BLK, 512), BF16), pltpu.VMEM((BLK, 512), BF16)],
        args=(sq, sk, sv, sinks, wn))


def _swa_bwd(dcat, o_all, sq, sk, sv, lse, sinks, wn):
    rows = sq.shape[0]
    nb = rows // BLK
    per_step = _blocks_per_step(nb)
    steps = nb // per_step
    scale = SWA_HD ** -0.5

    def body(dc_ref, o_ref, q_ref, k_ref, v_ref, lse_ref, sink_ref, wn_ref, dq_ref, dk_ref, dv_ref, dsink_ref, dwn_ref,
             kp, vp, dkp, dvp):
        step = pl.program_id(0)

        @pl.when(step == 0)
        def _():
            dk_ref[...] = jnp.zeros_like(dk_ref)
            dv_ref[...] = jnp.zeros_like(dv_ref)
            dkp[...] = jnp.zeros_like(dkp)
            dvp[...] = jnp.zeros_like(dvp)
            kp[...] = _swa_meta_operand(k_ref)
            vp[...] = _swa_meta_operand(v_ref)

        def one_block(c, carry):
            i = step * per_step + c
            rr = pl.ds(pl.multiple_of(c * BLK, BLK), BLK)
            first = i == 0
            own_side, band_ok, meta_ok, group = _swa_masks(i)
            k2, prev, own = _swa_blocks(k_ref, i)
            v2, _, _ = _swa_blocks(v_ref, i)
            kz = (_place(k2, 0), _place(k2, 1))
            vz = (_place(v2, 0), _place(v2, 1))
            o = o_ref[rr, :]
            on, ro = _rms(o)
            dc = dc_ref[rr, :]
            _acc_add(dwn_ref, first, _colsum(dc * on))
            do = _rms_bwd(on, ro, wn_ref[...], dc)
            do_o = do * o
            do16 = do.astype(BF16)
            q_all = q_ref[rr, :]
            lse = [lse_ref[rr, hd:hd + 1] for hd in range(SWA_HEADS)]
            delta = [jnp.sum(jnp.where(_half_mask(128, hd % 2), do_o[:, 128 * (hd // 2):128 * (hd // 2) + 128], 0.0),
                             axis=-1, keepdims=True) for hd in range(SWA_HEADS)]
            s_meta = jnp.where(meta_ok, _nt(q_all, kp[...]) * scale, NEG_INF)
            p_meta = jnp.exp(s_meta - _by_head(group, lse))
            ds_meta16 = (p_meta * (_nt(do16, vp[...]) - _by_head(group, delta)) * scale).astype(BF16)
            dq_meta = _nn(ds_meta16, kp[...])
            dkp[...] += _tn(ds_meta16, q_all)
            dvp[...] += _tn(p_meta.astype(BF16), do16)
            own2 = jnp.concatenate([own_side.astype(jnp.int32)] * 2, axis=0) > 0
            ok2 = jnp.concatenate([band_ok.astype(jnp.int32)] * 2, axis=0) > 0

            def window(x2):
                return jnp.where(own2, x2[:, BLK:], x2[:, :BLK])

            def unwindow(x):
                return jnp.concatenate([jnp.where(own2, 0.0, x), jnp.where(own2, x, 0.0)], axis=1).astype(BF16)
            lane8 = lax.broadcasted_iota(jnp.int32, (1, 128), 1)
            dsink = jnp.zeros((1, 128), F32)
            dq_pairs = [dq_meta[:, 128 * pr:128 * pr + 128] for pr in range(4)]
            dk2 = [[None, None], [None, None]]
            dv2 = [[None, None], [None, None]]
            for kv in range(2):
                for half in range(2):
                    heads, pairs = (4 * kv + half, 4 * kv + 2 + half), (2 * kv, 2 * kv + 1)
                    q_s = jnp.concatenate([q_all[:, 128 * pr:128 * pr + 128] for pr in pairs], axis=0)
                    do_s = jnp.concatenate([do16[:, 128 * pr:128 * pr + 128] for pr in pairs], axis=0)
                    lse_s = jnp.concatenate([lse[hd] for hd in heads], axis=0)
                    delta_s = jnp.concatenate([delta[hd] for hd in heads], axis=0)
                    s = jnp.where(ok2, window(_nt(q_s, kz[kv][half])) * scale, NEG_INF)
                    prob = jnp.exp(s - lse_s)
                    for hd in heads:
                        dsink = dsink + jnp.where(lane8 == hd, -jnp.sum(jnp.exp(sink_ref[0, hd] - lse[hd]) * delta[hd]), 0.0)
                    ds2 = unwindow(prob * (window(_nt(do_s, vz[kv][half])) - delta_s) * scale)
                    dq_s = _nn(ds2, kz[kv][half])
                    dq_pairs[pairs[0]] = dq_pairs[pairs[0]] + dq_s[:BLK]
                    dq_pairs[pairs[1]] = dq_pairs[pairs[1]] + dq_s[BLK:]
                    dk2[kv][half] = _tn(ds2, q_s)
                    dv2[kv][half] = _tn(unwindow(prob), do_s)
            dq_ref[rr, :] = jnp.concatenate(dq_pairs, axis=1)
            _acc_add(dsink_ref, first, dsink)
            for ref, acc2 in ((dk_ref, dk2), (dv_ref, dv2)):
                tot = jnp.zeros((2 * BLK, 128), F32)
                for kv in range(2):
                    for half in range(2):
                        part = jnp.where(_half_mask(128, half), acc2[kv][half], 0.0)
                        tot = tot + (part if half == kv else pltpu.roll(part, 64, 1))
                ref[pl.ds(prev, BLK), :] += tot[:BLK]
                ref[pl.ds(own, BLK), :] += tot[BLK:]
            return carry

        def finish(carry):
            del carry

            @pl.when(step == steps - 1)
            def _():
                dk_ref[PAD_ROWS:BLK, :] += _swa_meta_fold(dkp[...])
                dv_ref[PAD_ROWS:BLK, :] += _swa_meta_fold(dvp[...])
        return one_block, jnp.zeros((1, 128), F32), finish

    full = pl.BlockSpec((rows, 128), lambda i: (0, 0))
    blocks = lambda cols: _row_spec(per_step * BLK, cols)
    return dict(
        program=body, steps=steps, per_step=per_step,
        in_specs=[blocks(512), blocks(512), blocks(512), VMEM_SPEC, VMEM_SPEC, blocks(SWA_HEADS), SMEM_SPEC, VMEM_SPEC],
        out_specs=[blocks(512), full, full, _acc_spec(128), _acc_spec(512)],
        out_shape=[jax.ShapeDtypeStruct((rows, 512), F32), jax.ShapeDtypeStruct((rows, 128), F32),
                   jax.ShapeDtypeStruct((rows, 128), F32), jax.ShapeDtypeStruct((8, 128), F32),
                   jax.ShapeDtypeStruct((8, 512), F32)],
        scratch_shapes=[pltpu.VMEM((BLK, 512), BF16), pltpu.VMEM((BLK, 512), BF16),
                        pltpu.VMEM((BLK, 512), F32), pltpu.VMEM((BLK, 512), F32)],
        args=(dcat, o_all, sq, sk, sv, lse, sinks, wn))


def _mix_out_bwd(dh, m, wout, gpost):
    rows = dh.shape[0]
    tm = _row_tile(rows)

    def body(dh_ref, m_ref, w_ref, g_ref, dcg_ref, dcs_ref, dm_ref, dg_ref):
        first = pl.program_id(0) == 0
        dhv = dh_ref[...]
        mn, rm = _rms(m_ref[...])
        _acc_add(dg_ref, first, _colsum(dhv * mn))
        dm16 = _rms_bwd(mn, rm, g_ref[...], dhv).astype(BF16)
        dm_ref[...] = dm16
        dcat = _nt(dm16, w_ref[...])
        dcg_ref[...] = dcat[:, 0:512]
        dcs_ref[...] = dcat[:, 512:1024]

    row_f32 = _row_spec(tm, D_MODEL)
    return pl.pallas_call(
        body, name="mix_out_bwd", grid=(rows // tm,),
        in_specs=[row_f32, row_f32, VMEM_SPEC, VMEM_SPEC],
        out_specs=[_row_spec(tm, 512), _row_spec(tm, 512), row_f32, _acc_spec(D_MODEL)],
        out_shape=[jax.ShapeDtypeStruct((rows, 512), F32), jax.ShapeDtypeStruct((rows, 512), F32),
                   jax.ShapeDtypeStruct((rows, D_MODEL), BF16), jax.ShapeDtypeStruct((8, D_MODEL), F32)],
        compiler_params=_params(("arbitrary",)),
    )(dh, m, wout, gpost)


def _mix_in_bwd(dh_out, h, g, win_p, wa2_p, cos, sin, loga, ga, dgq, dgk, dgv, dgg, dsq, dsk, dsv, dloga):
    rows = h.shape[0]
    tm = _row_tile(rows)

    def body(dho_ref, h_ref, g_ref, win_ref, wa2_ref, cos_ref, sin_ref, loga_ref, ga_ref,
             dgq_ref, dgk_ref, dgv_ref, dgg_ref, dsq_ref, dsk_ref, dsv_ref, dla_ref,
             dh_ref, dproj_ref, dwa2_ref, dg_ref, dba_ref):
        first = pl.program_id(0) == 0
        dz = dla_ref[...] * (1.0 / GLA_TAU) * (1.0 - jnp.exp(GLA_TAU * loga_ref[...]))
        _acc_add(dba_ref, first, _colsum(dz))
        dga = _nt(dz, wa2_ref[...])
        pa = _tn(ga_ref[...], dz)
        c1, s1 = cos_ref[...], sin_ref[...]
        c4 = jnp.concatenate([c1, c1, c1, c1], axis=1)
        s4 = jnp.concatenate([s1, s1, s1, s1], axis=1)
        dq_r, dk_r = dsq_ref[...], dsk_ref[...]
        dsq = dq_r * c4 - _rot_half(dq_r * s4)
        dsk = dk_r * c1 - _rot_half(dk_r * s1)
        dproj16 = jnp.concatenate(
            [dgq_ref[...], dgk_ref[...], dgv_ref[...], dgg_ref[...], dsq, dsk, dsv_ref[...], dga], axis=1).astype(BF16)
        dproj_ref[...] = dproj16
        dn = _nn(dproj16, win_ref[...])

        @pl.when(first)
        def _():
            dwa2_ref[...] = pa

        @pl.when(jnp.logical_not(first))
        def _():
            dwa2_ref[...] += pa
        hn, rh = _rms(h_ref[...])
        _acc_add(dg_ref, first, _colsum(dn * hn))
        dh_ref[...] = dho_ref[...] + _rms_bwd(hn, rh, g_ref[...], dn)

    rs = lambda c: _row_spec(tm, c)
    return pl.pallas_call(
        body, name="mix_in_bwd", grid=(rows // tm,),
        in_specs=[rs(D_MODEL), rs(D_MODEL), VMEM_SPEC, VMEM_SPEC, VMEM_SPEC, rs(128), rs(128), rs(256), rs(128),
                  rs(256), rs(256), rs(512), rs(512), rs(512), rs(128), rs(128), rs(256)],
        out_specs=[rs(D_MODEL), rs(P_END), pl.BlockSpec((128, 256), lambda i: (0, 0)), _acc_spec(D_MODEL), _acc_spec(256)],
        out_shape=[jax.ShapeDtypeStruct((rows, D_MODEL), F32), jax.ShapeDtypeStruct((rows, P_END), BF16),
                   jax.ShapeDtypeStruct((128, 256), F32), jax.ShapeDtypeStruct((8, D_MODEL), F32),
                   jax.ShapeDtypeStruct((8, 256), F32)],
        compiler_params=_params(("arbitrary",)),
    )(dh_out, h, g, win_p, wa2_p, cos, sin, loga, ga, dgq, dgk, dgv, dgg, dsq, dsk, dsv, dloga)


def _rope_tables(rows):
    pos = (jnp.arange(rows, dtype=jnp.int32) - PAD_ROWS).astype(F32)
    inv_freq = 1.0 / (ROPE_THETA ** (jnp.arange(0, SWA_HD, 2, dtype=F32) / SWA_HD))
    ang = pos[:, None] * inv_freq[None, :]
    return jnp.tile(jnp.cos(ang), (1, 4)), jnp.tile(jnp.sin(ang), (1, 4))


def _local_step(x, tgt, front, w, late_weights=None, on_grads=None, on_small=None):
    cos, sin = _rope_tables(x.shape[0] + BLK)
    g = {}

    def tell(group, names):
        for nm in names:
            g[nm] = grads_now[nm]
        return None if on_grads is None else on_grads(group, {nm: grads_now[nm] for nm in names})

    h1, a1, b1, s1, f1 = _ffn_fwd(x, w["ffn1_pre"], w["wg1"], w["wu1"], w["wd1"], w["ffn1_post"], front=front)
    if late_weights is not None:
        w = {**w, **late_weights("win", f1)}
    gq, gk, gv, gg, sq, sk, sv, ga, loga, bc, n2 = _mix_in(h1, w["mix_pre"], w["win"], w["wa2"], w["b_a"], cos, sin)
    (o_g, cat_g, sp), (o_s, cat_s, lse) = _side_by_side(
        [_gla_fwd(gq, gk, gv, gg, bc, w["gla_norm"]), _swa_fwd(sq, sk, sv, w["sinks"], w["swa_norm"])], "attention_fwd")
    if late_weights is not None:
        w = {**w, **late_weights("rest", lse)}
    h2, m, a2, b2, s2, f2, dy, loss = _ffn_fwd(h1, w["ffn2_pre"], w["wg2"], w["wu2"], w["wd2"], w["ffn2_post"], tgt,
                                               mixed=(cat_g, cat_s, w["wout"], w["mix_post"]))
    dh2, da, db, df, n3, g["ffn2_pre"], g["ffn2_post"] = _ffn_bwd_act(
        dy, h2, a2, b2, f2, w["ffn2_pre"], w["ffn2_post"], w["wg2"], w["wu2"], w["wd2"], "ffn2_bwd_act")
    grads_now = dict(wd2=_wgrad(s2, df, "ffn2_wgrad_down"), wg2=_wgrad(da, n3, "ffn2_wgrad_gate"),
                     wu2=_wgrad(db, n3, "ffn2_wgrad_up"))
    tok = tell("ffn2", ("wd2", "wg2", "wu2"))
    dcg, dcs, dm, g["mix_post"] = _mix_out_bwd(dh2, m, w["wout"], w["mix_post"] + (0.0 if tok is None else tok[0, 0]))
    (dgq, dgk, dgv, dgg, dloga, g["gla_norm"]), (dsq, dsk, dsv, g["sinks"], g["swa_norm"]) = _side_by_side(
        [_gla_bwd(dcg, o_g, gq, gk, gv, gg, bc, sp, w["gla_norm"]),
         _swa_bwd(dcs, o_s, sq, sk, sv, lse, w["sinks"], w["swa_norm"])], "attention_bwd")
    dh1, dproj, g["wa2"], g["mix_pre"], g["b_a"] = _mix_in_bwd(
        dh2, h1, w["mix_pre"], w["win"], w["wa2"], cos, sin, loga, ga, dgq, dgk, dgv, dgg, dsq, dsk, dsv, dloga)
    dh0, da, db, df, n1, g["ffn1_pre"], g["ffn1_post"] = _ffn_bwd_act(
        dh1, x, a1, b1, f1, w["ffn1_pre"], w["ffn1_post"], w["wg1"], w["wu1"], w["wd1"], "ffn1_bwd_act", front=front)
    tok = None if on_small is None else on_small(loss[0, 0], dh0, g)
    grads_now = dict(wd1=_wgrad(s1, df, "ffn1_wgrad_down", after=tok))
    tok = tell("ffn1_down", ("wd1",))
    grads_now = dict(wg1=_wgrad(da, n1, "ffn1_wgrad_gate", after=tok))
    tok = tell("ffn1_gate", ("wg1",))
    grads_now = dict(wu1=_wgrad(db, n1, "ffn1_wgrad_up", after=tok))
    tok = tell("ffn1_up", ("wu1",))
    grads_now = dict(win=_wgrad(dproj, n2, "win_wgrad", after=tok),
                     wout=jnp.concatenate([_wgrad(cat_g, dm, "wout_wgrad_gla", after=tok),
                                           _wgrad(cat_s, dm, "wout_wgrad_swa", after=tok)], axis=0))
    tell("mix", ("wout", "win"))
    return loss[0, 0], dh0, g


def _win_pad_rows(win_t):
    pad = jnp.zeros((P_END - P_GA - 16, win_t.shape[1]), win_t.dtype)
    return jnp.concatenate([win_t[0:1536], win_t[1552:2320], win_t[1536:1552], pad], axis=0)


def _win_unpad_rows(win_p):
    return jnp.concatenate([win_p[0:1536], win_p[P_GA:P_GA + 16], win_p[1536:P_GA]], axis=0)


def _place_on_mesh():
    return lax.axis_index("x"), lax.axis_index("y"), lax.axis_index("c")


def _dev_index(px, py, pc):
    return 4 * px + 2 * py + pc


def _other_devices(x, y, c):
    flip = lambda v, f: 1 - v if f else v
    return [(flip(x, fx), flip(y, fy), flip(c, fc)) for fx in (0, 1) for fy in (0, 1) for fc in (0, 1)][1:]


def _all_gather(shards):
    n = len(shards)

    def body(*refs):
        ins, outs = refs[:n], refs[n:2 * n]
        zeros_ref, send_sems, recv_sems, local_sems = refs[2 * n:]
        zeros_ref[...] = jnp.zeros_like(zeros_ref)
        x, y, c = _place_on_mesh()
        me, sibling = (x, y, c), (x, y, 1 - c)
        chips = [(1 - x, y), (x, 1 - y), (1 - x, 1 - y)]

        def rows(k, px, py, pc):
            r = ins[k].shape[0]
            return outs[k].at[pl.ds(pl.multiple_of(_dev_index(px, py, pc) * r, 8), r), :]

        def copy(k, slot, block, to, src=None):
            return pltpu.make_async_remote_copy(
                src_ref=rows(k, *block) if src is None else src, dst_ref=rows(k, *block),
                send_sem=send_sems.at[k, slot], recv_sem=recv_sems.at[k, slot], device_id=to, device_id_type=MESH)

        local = [pltpu.make_async_copy(ins[k], rows(k, *me), local_sems.at[k]) for k in range(n)]
        sends = []
        for k in range(n):
            local[k].start()
            sends.append(copy(k, 0, me, sibling, src=ins[k]))
            sends += [copy(k, 1 + j, me, (*chip, c), src=ins[k]) for j, chip in enumerate(chips)]
        for cp in sends:
            cp.start()
        for k in range(n):
            for j, chip in enumerate(chips):
                copy(k, 1 + j, (*chip, c), me).wait_recv()
                passed = copy(k, 4 + j, (*chip, c), sibling)
                passed.start()
                sends.append(passed)
        for k in range(n):
            copy(k, 0, sibling, me).wait_recv()
            for j, chip in enumerate(chips):
                copy(k, 4 + j, (*chip, 1 - c), me).wait_recv()
        for cp in sends:
            cp.wait_send()
        for cp in local:
            cp.wait()

    return pl.pallas_call(
        body, name="all_gather_weights",
        in_specs=[ANY_SPEC] * n, out_specs=[ANY_SPEC] * n + [VMEM_SPEC],
        out_shape=[jax.ShapeDtypeStruct((N_DEV * s.shape[0], s.shape[1]), s.dtype) for s in shards]
        + [jax.ShapeDtypeStruct((8, 128), F32)],
        scratch_shapes=[pltpu.SemaphoreType.DMA((n, 7)), pltpu.SemaphoreType.DMA((n, 7)), pltpu.SemaphoreType.DMA((n,))],
    )(*shards)


HBM_SPEC = pl.BlockSpec(memory_space=pltpu.HBM)
SEM_SPEC = pl.BlockSpec(memory_space=pltpu.SEMAPHORE)
DATAFLOW = pltpu.SideEffectType.DATAFLOW_SIDE_EFFECTING


GATHER, SCATTER, SCATTER_CHIPS = "gather", "scatter", "scatter among chips"


def _exchange_peers(kind):
    x, y, c = _place_on_mesh()
    if kind == SCATTER_CHIPS:
        peers = [(1 - x, y, c), (x, 1 - y, c), (1 - x, 1 - y, c)]
        return peers, [2 * p[0] + p[1] for p in peers], 2 * x + y, 4
    peers = _other_devices(x, y, c)
    return peers, [_dev_index(*p) for p in peers], _dev_index(x, y, c), N_DEV


def _exchange_copies(srcs, lands, send_sems, recv_sems, own_sems, kind, arriving):
    peers, theirs, me, blocks = _exchange_peers(kind)
    remote, local = [], []
    for k, (src, land) in enumerate(zip(srcs, lands)):
        r = land.shape[0] // blocks

        def block(ref, d):
            return ref.at[pl.ds(pl.multiple_of(d * r, 8), r), :]

        for f, (peer, him) in enumerate(zip(peers, theirs)):
            mine, his = (him, me) if arriving else (me, him)
            sem = len(peers) * k + f
            remote.append(pltpu.make_async_remote_copy(
                src_ref=src if kind == GATHER else block(src, his), dst_ref=block(land, mine),
                send_sem=send_sems.at[sem], recv_sem=recv_sems.at[sem], device_id=peer, device_id_type=MESH))
        local.append(pltpu.make_async_copy(src if kind == GATHER else block(src, me), block(land, me), own_sems.at[k]))
    return remote, local


def _exchange_start(srcs, kind, name):
    n = len(srcs)
    lands = [lax.empty((N_DEV * s.shape[0], s.shape[1]) if kind == GATHER else s.shape, s.dtype) for s in srcs]
    sems = (3 if kind == SCATTER_CHIPS else 7) * n

    def body(*refs):
        remote, local = _exchange_copies(refs[:n], refs[n:2 * n], *refs[2 * n:2 * n + 3], kind, False)
        for cp in remote + local:
            cp.start()
        refs[-1][...] = jnp.zeros_like(refs[-1])

    both = list(srcs) + list(lands)
    outs = pl.pallas_call(
        body, name=name,
        out_shape=(pltpu.SemaphoreType.DMA((sems,)), pltpu.SemaphoreType.DMA((sems,)), pltpu.SemaphoreType.DMA((n,)),
                   *[pltpu.HBM(a.shape, a.dtype) for a in both], jax.ShapeDtypeStruct((8, 128), F32)),
        in_specs=[HBM_SPEC] * (2 * n), out_specs=(SEM_SPEC, SEM_SPEC, SEM_SPEC, *[HBM_SPEC] * (2 * n), VMEM_SPEC),
        input_output_aliases={i: 3 + i for i in range(2 * n)},
        compiler_params=pltpu.CompilerParams(has_side_effects=DATAFLOW),
    )(*[pltpu.with_memory_space_constraint(a, pltpu.HBM) for a in both])
    return outs[0:3], outs[3:3 + n], outs[3 + n:3 + 2 * n], outs[-1]


def _exchange_wait(started, kind, after, name):
    sems, srcs, lands, _ = started
    n = len(srcs)

    def body(*refs):
        args = (refs[:n], refs[n:2 * n], *refs[2 * n:2 * n + 3], kind)
        going, local = _exchange_copies(*args, False)
        for cp in going:
            cp.wait_send()
        for cp in local:
            cp.wait()
        for cp in _exchange_copies(*args, True)[0]:
            cp.wait_recv()

    both = list(srcs) + list(lands)
    outs = pl.pallas_call(
        body, name=name, out_shape=[pltpu.HBM(a.shape, a.dtype) for a in both],
        in_specs=[HBM_SPEC] * (2 * n) + [SEM_SPEC, SEM_SPEC, SEM_SPEC, ANY_SPEC], out_specs=[HBM_SPEC] * (2 * n),
        input_output_aliases={i: i for i in range(2 * n)},
        compiler_params=pltpu.CompilerParams(has_side_effects=DATAFLOW),
    )(*both, *sems, after)
    return outs[n:]


def _sibling_reduce(part, name):
    r, cols = part.shape[0] // N_DEV, part.shape[1]

    def body(p_ref, o_ref, mine, got, send_sems, recv_sems, own_sems):
        x, y, c = _place_on_mesh()

        def block(d):
            return p_ref.at[pl.ds(pl.multiple_of(d * r, 8), r), :]
        swaps = [pltpu.make_async_remote_copy(
            src_ref=block(2 * j + 1 - c), dst_ref=got.at[j], send_sem=send_sems.at[j], recv_sem=recv_sems.at[j],
            device_id=(x, y, 1 - c), device_id_type=MESH) for j in range(4)]
        keeps = [pltpu.make_async_copy(block(2 * j + c), mine.at[j], own_sems.at[j]) for j in range(4)]
        for cp in swaps + keeps:
            cp.start()
        for j in range(4):
            keeps[j].wait()
            swaps[j].wait()
            o_ref[pl.ds(j * r, r), :] = (mine[j].astype(F32) + got[j].astype(F32)).astype(o_ref.dtype)

    return pl.pallas_call(
        body, name=name, in_specs=[ANY_SPEC], out_specs=VMEM_SPEC,
        out_shape=jax.ShapeDtypeStruct((4 * r, cols), part.dtype),
        scratch_shapes=[pltpu.VMEM((4, r, cols), part.dtype), pltpu.VMEM((4, r, cols), part.dtype),
                        pltpu.SemaphoreType.DMA((4,)), pltpu.SemaphoreType.DMA((4,)), pltpu.SemaphoreType.DMA((4,))],
        compiler_params=pltpu.CompilerParams(vmem_limit_bytes=32 << 20),
    )(part)


def _sum_partials(parts, name, blocks=N_DEV):
    n = len(parts)

    def body(*refs):
        ins, outs = refs[:n], refs[n:]
        first = pl.program_id(0) == 0
        for i_ref, o_ref in zip(ins, outs):
            v = i_ref[...].astype(F32)

            @pl.when(first)
            def _():
                o_ref[...] = v

            @pl.when(jnp.logical_not(first))
            def _():
                o_ref[...] += v

    shapes = [(p.shape[0] // blocks, p.shape[1]) for p in parts]
    return pl.pallas_call(
        body, name=name, grid=(blocks,),
        in_specs=[pl.BlockSpec(s, lambda j: (j, 0)) for s in shapes],
        out_specs=[pl.BlockSpec(s, lambda j: (0, 0)) for s in shapes],
        out_shape=[jax.ShapeDtypeStruct(s, F32) for s in shapes],
        compiler_params=_params(("arbitrary",)),
    )(*parts)


def _adamw_update(w, g, m, v):
    m = ADAM_B1 * m + (1.0 - ADAM_B1) * g
    v = ADAM_B2 * v + (1.0 - ADAM_B2) * (g * g)
    m_hat = m * (1.0 / (1.0 - ADAM_B1 ** ADAM_STEP))
    v_hat = v * (1.0 / (1.0 - ADAM_B2 ** ADAM_STEP))
    return -ADAM_LR * (m_hat / (jnp.sqrt(v_hat) + ADAM_EPS) + ADAM_WD * w), m, v


def _sum_adamw(parts, w, m, v, blocks, name):
    shape = w.shape

    def body(p_ref, w_ref, m_ref, v_ref, g_ref, d_ref, mo_ref, vo_ref):
        j = pl.program_id(0)
        part = p_ref[...].astype(F32)

        @pl.when(j == 0)
        def _():
            g_ref[...] = part

        @pl.when(j > 0)
        def _():
            g_ref[...] += part

        @pl.when(j == blocks - 1)
        def _():
            d_ref[...], mo_ref[...], vo_ref[...] = _adamw_update(w_ref[...], g_ref[...], m_ref[...], v_ref[...])

    held = pl.BlockSpec(shape, lambda j: (0, 0))
    return pl.pallas_call(
        body, name=name, grid=(blocks,),
        in_specs=[pl.BlockSpec(shape, lambda j: (j, 0)), held, held, held],
        out_specs=[held] * 4, out_shape=[jax.ShapeDtypeStruct(shape, F32)] * 4,
        compiler_params=_params(("arbitrary",)),
    )(parts, w, m, v)


def _adamw(ws, gs, ms, vs, name):
    n = len(ws)

    def body(*refs):
        w_r, g_r, m_r, v_r = refs[:n], refs[n:2 * n], refs[2 * n:3 * n], refs[3 * n:4 * n]
        d_o, m_o, v_o = refs[4 * n:5 * n], refs[5 * n:6 * n], refs[6 * n:7 * n]
        for k in range(n):
            d_o[k][...], m_o[k][...], v_o[k][...] = _adamw_update(w_r[k][...], g_r[k][...], m_r[k][...], v_r[k][...])

    shapes = [jax.ShapeDtypeStruct(w.shape, F32) for w in ws]
    outs = pl.pallas_call(
        body, name=name, in_specs=[VMEM_SPEC] * (4 * n), out_specs=[VMEM_SPEC] * (3 * n), out_shape=shapes * 3,
        compiler_params=pltpu.CompilerParams(vmem_limit_bytes=56 << 20),
    )(*ws, *gs, *ms, *vs)
    return outs[:n], outs[n:2 * n], outs[2 * n:]


WEIGHT_NAMES = ("meta_tokens", "ffn1_pre_norm", "ffn1_w_gate", "ffn1_w_up", "ffn1_w_down", "ffn1_post_norm", "mix_pre_norm",
                "w_in", "gla_w_a2", "gla_b_a", "gla_out_norm", "swa_sinks", "swa_out_norm", "w_out", "mix_post_norm",
                "ffn2_pre_norm", "ffn2_w_gate", "ffn2_w_up", "ffn2_w_down", "ffn2_post_norm")
WIN_SHARD = D_IN // N_DEV
WIN_SHARD_PAD = 304
SLAB_VECTORS = ("ffn1_pre", "ffn1_post", "mix_pre", "mix_post", "ffn2_pre", "ffn2_post")
SLAB_ROWS = 32


def kernel(x, meta_tokens, ffn1_pre_norm, ffn1_w_gate, ffn1_w_up, ffn1_w_down, ffn1_post_norm, mix_pre_norm, w_in, gla_w_a2, gla_b_a, gla_out_norm, swa_sinks, swa_out_norm, w_out, mix_post_norm, ffn2_pre_norm, ffn2_w_gate, ffn2_w_up, ffn2_w_down, ffn2_post_norm, loss_target, m_meta_tokens, m_ffn1_pre_norm, m_ffn1_w_gate, m_ffn1_w_up, m_ffn1_w_down, m_ffn1_post_norm, m_mix_pre_norm, m_w_in, m_gla_w_a2, m_gla_b_a, m_gla_out_norm, m_swa_sinks, m_swa_out_norm, m_w_out, m_mix_post_norm, m_ffn2_pre_norm, m_ffn2_w_gate, m_ffn2_w_up, m_ffn2_w_down, m_ffn2_post_norm, v_meta_tokens, v_ffn1_pre_norm, v_ffn1_w_gate, v_ffn1_w_up, v_ffn1_w_down, v_ffn1_post_norm, v_mix_pre_norm, v_w_in, v_gla_w_a2, v_gla_b_a, v_gla_out_norm, v_swa_sinks, v_swa_out_norm, v_w_out, v_mix_post_norm, v_ffn2_pre_norm, v_ffn2_w_gate, v_ffn2_w_up, v_ffn2_w_down, v_ffn2_post_norm):
    given = dict(locals())
    W = {n: given[n] for n in WEIGHT_NAMES}
    M = {n: given["m_" + n] for n in WEIGHT_NAMES}
    V = {n: given["v_" + n] for n in WEIGHT_NAMES}
    dev = _dev_index(*_place_on_mesh())

    def t16(w):
        return w[0].T.astype(BF16)

    small = jnp.concatenate([W["meta_tokens"], jnp.pad(W["gla_w_a2"][0], ((0, 0), (0, 96)))], axis=0)
    wg1, wu1, wd1, small_g, gathered_zeros = _all_gather(
        [t16(W["ffn1_w_gate"]), t16(W["ffn1_w_up"]), W["ffn1_w_down"][0].astype(BF16), small])
    def after_zero(shard, zeros):
        return shard + zeros[0:1, 0:1].astype(shard.dtype)
    win_shard = jnp.pad(t16(W["w_in"]), ((0, WIN_SHARD_PAD - WIN_SHARD), (0, 0)))
    win_shard = after_zero(win_shard, gathered_zeros)
    mid = _exchange_start([win_shard], GATHER, "gather_w_in_start")
    late_shards = [after_zero(W["w_out"][0].astype(BF16), mid[3]), t16(W["ffn2_w_gate"]), t16(W["ffn2_w_up"]),
                   W["ffn2_w_down"][0].astype(BF16)]
    late = _exchange_start(late_shards, GATHER, "gather_late_weights_start")

    def late_weights(what, after):
        if what == "win":
            win_g, = _exchange_wait(mid, GATHER, after, "gather_w_in_wait")
            win_t = win_g.reshape(N_DEV, WIN_SHARD_PAD, D_MODEL)[:, :WIN_SHARD].reshape(D_IN, D_MODEL)
            return dict(win=_win_pad_rows(win_t))
        wout, wg2, wu2, wd2 = _exchange_wait(late, GATHER, after, "gather_late_weights_wait")
        return dict(wout=wout, wg2=wg2, wu2=wu2, wd2=wd2)

    small_g = small_g.reshape(N_DEV, 32, 128)
    meta_full = small_g[:, :N_META].transpose(1, 0, 2).reshape(N_META, D_MODEL)
    wa2_full = small_g[:, N_META:, :32].transpose(1, 0, 2).reshape(16, 256)
    w = dict(
        ffn1_pre=W["ffn1_pre_norm"] + late[3][0, 0], ffn1_post=W["ffn1_post_norm"], mix_pre=W["mix_pre_norm"],
        mix_post=W["mix_post_norm"], ffn2_pre=W["ffn2_pre_norm"], ffn2_post=W["ffn2_post_norm"], b_a=W["gla_b_a"],
        gla_norm=W["gla_out_norm"], sinks=W["swa_sinks"], swa_norm=W["swa_out_norm"], wg1=wg1, wu1=wu1, wd1=wd1,
        wa2=jnp.pad(wa2_full, ((0, 112), (0, 0))))

    in_flight = []

    def on_grads(group, grads):
        parts = []
        for nm, p in grads.items():
            if nm == "win":
                p = _win_unpad_rows(p).reshape(N_DEV, WIN_SHARD, D_MODEL)
                p = jnp.pad(p, ((0, 0), (0, WIN_SHARD_PAD - WIN_SHARD), (0, 0))).reshape(N_DEV * WIN_SHARD_PAD, D_MODEL)
            parts.append(p)
        kind = SCATTER if group == "ffn2" else SCATTER_CHIPS
        if kind == SCATTER_CHIPS:
            parts = [_sibling_reduce(p, "pair_" + group + "_" + nm) for nm, p in zip(grads, parts)]
        started = _exchange_start(parts, kind, "scatter_" + group + "_start")
        in_flight.append((group, list(grads), started, kind))
        return started[3]

    small_flight = []

    def on_small(loss, dh0, g):
        packed = jnp.concatenate([g["b_a"][0:1], g["gla_norm"][0:1], g["sinks"][0:1], g["swa_norm"][0:1]], axis=1)
        slab = jnp.concatenate([g[k][0:1] for k in SLAB_VECTORS] + [packed, jnp.full((1, D_MODEL), loss, F32),
                               g["wa2"][:16].reshape(4, D_MODEL), jnp.zeros((4, D_MODEL), F32), dh0[PAD_ROWS:BLK]], axis=0)
        small_flight.append(_exchange_start([slab], GATHER, "gather_small_grads_start"))
        return small_flight[0][3]

    front = jnp.concatenate([jnp.zeros((PAD_ROWS, D_MODEL), F32), meta_full], axis=0)
    loss, dh0, g = _local_step(x[0], loss_target[0], front, w, late_weights, on_grads, on_small)
    grad_x = dh0[BLK:][None]

    land, = _exchange_wait(small_flight[0], GATHER, in_flight[-1][2][3], "gather_small_grads_wait")
    tot = _sum_partials([land], "sum_small_grads")[0]
    loss = tot[7, 0]
    small_grads = dict(
        ffn1_pre_norm=tot[0:1], ffn1_post_norm=tot[1:2], mix_pre_norm=tot[2:3], mix_post_norm=tot[3:4],
        ffn2_pre_norm=tot[4:5], ffn2_post_norm=tot[5:6], gla_b_a=tot[6:7, 0:256], gla_out_norm=tot[6:7, 256:384],
        swa_sinks=tot[6:7, 384:392], swa_out_norm=tot[6:7, 512:1024],
        gla_w_a2=lax.dynamic_slice_in_dim(tot[8:12].reshape(16, 256), dev * 32, 32, axis=1)[None],
        meta_tokens=lax.dynamic_slice_in_dim(tot[16:32], dev * 128, 128, axis=1))

    big = dict(wg1=("ffn1_w_gate", True), wu1=("ffn1_w_up", True), wd1=("ffn1_w_down", False), win=("w_in", True),
               wout=("w_out", False), wg2=("ffn2_w_gate", True), wu2=("ffn2_w_up", True), wd2=("ffn2_w_down", False))
    grads = dict(small_grads)
    delta, new_m, new_v = {}, {}, {}
    names = [n for n in WEIGHT_NAMES if n not in [full for full, _ in big.values()]]
    two_d = lambda a: a.reshape(-1, a.shape[-1])
    d_, m_, v_ = _adamw([two_d(W[n]) for n in names], [two_d(grads[n]) for n in names],
                        [two_d(M[n]) for n in names], [two_d(V[n]) for n in names], "adamw_small")
    for k, n in enumerate(names):
        delta[n], new_m[n], new_v[n] = d_[k].reshape(W[n].shape), m_[k].reshape(W[n].shape), v_[k].reshape(W[n].shape)

    before_wait = d_[0] + in_flight[-1][2][3][0, 0]
    for group, shorts, started, kind in in_flight:
        lands = _exchange_wait(started, kind, before_wait, "scatter_" + group + "_wait")
        blocks = 4 if kind == SCATTER_CHIPS else N_DEV
        for short, land in zip(shorts, lands):
            n, transposed = big[short]
            to_slab = (lambda a: a[0].T) if transposed else (lambda a: a[0])
            from_slab = (lambda a: a.T[None]) if transposed else (lambda a: a[None])
            if short == "win":
                g_slab = _sum_partials([land], "sum_" + n, blocks)[0][:WIN_SHARD]
                d_, m_, v_ = _adamw([to_slab(W[n])], [g_slab], [to_slab(M[n])], [to_slab(V[n])], "adamw_" + n)
                d_, m_, v_ = d_[0], m_[0], v_[0]
            else:
                g_slab, d_, m_, v_ = _sum_adamw(land, to_slab(W[n]), to_slab(M[n]), to_slab(V[n]), blocks, "adamw_" + n)
            grads[n], delta[n], new_m[n], new_v[n] = from_slab(g_slab), from_slab(d_), from_slab(m_), from_slab(v_)
            before_wait = d_
    return (loss, grad_x, *[grads[n] for n in WEIGHT_NAMES], *[delta[n] for n in WEIGHT_NAMES],
            *[new_m[n] for n in WEIGHT_NAMES], *[new_v[n] for n in WEIGHT_NAMES])
```

```python
import math

import jax
import jax.numpy as jnp
from jax import lax
from jax.experimental import pallas as pl
from jax.experimental.pallas import tpu as pltpu

F32, BF16 = jnp.float32, jnp.bfloat16

D_MODEL = 1024
D_FF = 2816
N_META = 16
BLK = 128
PAD_ROWS = BLK - N_META
GLA_DK = 64
SWA_HD = 64
SWA_HEADS = 8
GLA_TAU = 16.0
NORM_EPS = 1e-6
NEG_INF = -1e30
ROPE_THETA = 10000.0
P_GQ, P_GK, P_GV, P_GG, P_SQ, P_SK, P_SV, P_GA, P_END = 0, 256, 512, 1024, 1536, 2048, 2176, 2304, 2432
D_IN = 2320
IN_SPLITS = (256, 256, 512, 512, 16, 512, 128, 128)
FF_TILE = 2816
WGRAD_TILE_MAX = 2432
N_DEV = 8
MESH = pl.DeviceIdType.MESH

ADAM_LR, ADAM_B1, ADAM_B2, ADAM_EPS, ADAM_WD, ADAM_STEP = 0.001, 0.9, 0.999, 1e-08, 0.01, 10

V7X_VMEM_BYTES = 64 << 20
VMEM_SPEC = pl.BlockSpec(memory_space=pltpu.VMEM)
SMEM_SPEC = pl.BlockSpec(memory_space=pltpu.SMEM)
ANY_SPEC = pl.BlockSpec(memory_space=pl.ANY)


def _params(semantics, vmem_mb=56):
    return pltpu.CompilerParams(dimension_semantics=semantics, vmem_limit_bytes=vmem_mb << 20)


def _row_tile(rows):
    return 416 if rows % 416 == 0 else BLK


def _blocks_per_step(blocks):
    return 5 if blocks % 5 == 0 else 1


def _nn(a, b):
    return lax.dot_general(a, b, (((1,), (0,)), ((), ())), preferred_element_type=F32)


def _nt(a, b):
    return lax.dot_general(a, b, (((1,), (1,)), ((), ())), preferred_element_type=F32)


def _tn(a, b):
    return lax.dot_general(a, b, (((0,), (0,)), ((), ())), preferred_element_type=F32)


def _rms(x):
    r = lax.rsqrt(jnp.mean(x * x, axis=-1, keepdims=True) + NORM_EPS)
    return x * r, r


def _rms_bwd(xn, r, w, dy):
    g = dy * w
    return r * (g - xn * jnp.mean(g * xn, axis=-1, keepdims=True))


def _sigmoid(x):
    return 1.0 / (1.0 + jnp.exp(-x))


def _colsum(x):
    return jnp.sum(x, axis=0, keepdims=True)


def _split_bf16(x):
    hi = x.astype(BF16)
    lo = (x - hi.astype(F32)).astype(BF16)
    return hi, lo


def _tri(lower):
    r = lax.broadcasted_iota(jnp.int32, (BLK, BLK), 0)
    c = lax.broadcasted_iota(jnp.int32, (BLK, BLK), 1)
    return (r >= c) if lower else (c >= r)


def _half_mask(width, half):
    lane = lax.broadcasted_iota(jnp.int32, (1, width), 1)
    return ((lane % 128) < 64) if half == 0 else ((lane % 128) >= 64)


def _rot_half(x):
    w = x.shape[-1]
    lane = lax.broadcasted_iota(jnp.int32, (1, w), 1)
    return jnp.where((lane % SWA_HD) < SWA_HD // 2, -pltpu.roll(x, w - SWA_HD // 2, 1), pltpu.roll(x, SWA_HD // 2, 1))


def _row_spec(tm, cols):
    return pl.BlockSpec((tm, cols), lambda i: (i, 0))


def _acc_spec(cols):
    return pl.BlockSpec((8, cols), lambda i: (0, 0))


def _acc_add(ref, first, value):
    @pl.when(first)
    def _():
        ref[...] = jnp.zeros_like(ref)
    ref[0:1, :] += value


def _behind_spec(tm):
    return pl.BlockSpec((pl.Element(tm), pl.Element(D_MODEL)),
                        lambda i: (pl.multiple_of(jnp.maximum(i * tm - BLK, 0), math.gcd(tm, BLK)), 0))


def _behind_front(ref, i, tm, front):
    blk = ref[...]
    return jnp.where(i == 0, jnp.concatenate([front, blk[0:tm - BLK]], axis=0), blk)


def _ffn_fwd(h, gpre, wg_t, wu_t, wd, gpost, tgt=None, front=None, mixed=None):
    with_loss, with_front, with_mixed = tgt is not None, front is not None, mixed is not None
    rows = h.shape[0] + (BLK if with_front else 0)
    tm = _row_tile(rows)
    nf = D_FF // FF_TILE

    def body(*refs):
        refs = list(refs)
        h_ref, gpre_ref, wg_ref, wu_ref, wd_ref, gpost_ref = refs[:6]
        del refs[:6]
        front_ref = refs.pop(0) if with_front else None
        cg_ref, cs_ref, wo_ref, gm_ref = (refs.pop(0), refs.pop(0), refs.pop(0), refs.pop(0)) if with_mixed else (None,) * 4
        t_ref = refs.pop(0) if with_loss else None
        hm_ref, m_ref = (refs.pop(0), refs.pop(0)) if with_mixed else (None, None)
        ho_ref = None if with_loss else refs.pop(0)
        a_ref, b_ref, s_ref, f_ref = refs[:4]
        dy_ref, loss_ref = refs[4:6] if with_loss else (None, None)
        acc = refs[-1]
        i = pl.program_id(0)
        h_in = _behind_front(h_ref, i, tm, front_ref[...]) if with_front else h_ref[...]
        if with_mixed:
            m = _nn(cg_ref[...], wo_ref[0:512, :]) + _nn(cs_ref[...], wo_ref[512:1024, :])
            m_ref[...] = m
            mn, _ = _rms(m)
            h_in = h_in + mn * gm_ref[...]
            hm_ref[...] = h_in
        hn, _ = _rms(h_in)
        n16 = (hn * gpre_ref[...]).astype(BF16)
        for j in range(nf):
            cols = slice(j * FF_TILE, (j + 1) * FF_TILE)
            a = _nt(n16, wg_ref[cols, :])
            b = _nt(n16, wu_ref[cols, :])
            a_ref[:, cols] = a.astype(BF16)
            b_ref[:, cols] = b.astype(BF16)
            s16 = (a * _sigmoid(a) * b).astype(BF16)
            s_ref[:, cols] = s16
            part = _nn(s16, wd_ref[cols, :])
            if j == 0:
                acc[...] = part
            else:
                acc[...] += part
        f = acc[...]
        f_ref[...] = f
        fn, _ = _rms(f)
        y = h_in + 0.5 * (fn * gpost_ref[...])
        if not with_loss:
            ho_ref[...] = y
        else:
            row = i * tm + lax.broadcasted_iota(jnp.int32, (tm, 1), 0)
            err = jnp.where(row >= BLK, y - _behind_front(t_ref, i, tm, jnp.zeros((BLK, D_MODEL), F32)), 0.0)
            dy_ref[...] = err * (1.0 / D_MODEL)
            part = 0.5 * jnp.sum(jnp.sum(err * err, axis=-1, keepdims=True) * (1.0 / D_MODEL), axis=0, keepdims=True)

            @pl.when(i == 0)
            def _():
                loss_ref[...] = jnp.zeros_like(loss_ref)
            loss_ref[...] += part

    row_f32 = _row_spec(tm, D_MODEL)
    behind = _behind_spec(tm)
    in_specs = [behind if with_front else row_f32, VMEM_SPEC, VMEM_SPEC, VMEM_SPEC, VMEM_SPEC, VMEM_SPEC]
    wide, full = jax.ShapeDtypeStruct((rows, D_FF), BF16), jax.ShapeDtypeStruct((rows, D_MODEL), F32)
    out_specs = [_row_spec(tm, D_FF), _row_spec(tm, D_FF), _row_spec(tm, D_FF), row_f32]
    out_shape = [wide, wide, wide, full]
    args = [h, gpre, wg_t, wu_t, wd, gpost]
    if not with_loss:
        out_specs.insert(0, row_f32)
        out_shape.insert(0, full)
    if with_front:
        in_specs.append(VMEM_SPEC)
        args.append(front)
    if with_mixed:
        in_specs += [_row_spec(tm, 512), _row_spec(tm, 512), VMEM_SPEC, VMEM_SPEC]
        args += list(mixed)
        out_specs = [row_f32, row_f32] + out_specs
        out_shape = [full, full] + out_shape
    if with_loss:
        in_specs.append(behind)
        args.append(tgt)
        out_specs += [row_f32, pl.BlockSpec((8, 128), lambda i: (0, 0))]
        out_shape += [jax.ShapeDtypeStruct((rows, D_MODEL), F32), jax.ShapeDtypeStruct((8, 128), F32)]
    return pl.pallas_call(
        body, name="ffn_fwd_loss" if with_loss else "ffn_fwd", grid=(rows // tm,),
        in_specs=in_specs, out_specs=out_specs, out_shape=out_shape,
        scratch_shapes=[pltpu.VMEM((tm, D_MODEL), F32)],
        compiler_params=_params(("arbitrary",), vmem_mb=62 if with_mixed else 56),
    )(*args)


def _ffn_bwd_act(dh_out, h, a, b, f, gpre, gpost, wg_t, wu_t, wd, name, front=None):
    with_front = front is not None
    rows = dh_out.shape[0]
    tm = _row_tile(rows)
    nf = D_FF // FF_TILE

    def body(dho_ref, h_ref, a_ref, b_ref, f_ref, gpre_ref, gpost_ref, wg_ref, wu_ref, wd_ref, *rest):
        front_ref = rest[0] if with_front else None
        dh_ref, da_ref, db_ref, df_ref, n_ref, dgpre_ref, dgpost_ref, acc = rest[-8:]
        first = pl.program_id(0) == 0
        dho = dho_ref[...]
        drr = 0.5 * dho
        fn, rf = _rms(f_ref[...])
        _acc_add(dgpost_ref, first, _colsum(drr * fn))
        df16 = _rms_bwd(fn, rf, gpost_ref[...], drr).astype(BF16)
        df_ref[...] = df16
        h_in = _behind_front(h_ref, pl.program_id(0), tm, front_ref[...]) if with_front else h_ref[...]
        hn, rh = _rms(h_in)
        n_ref[...] = (hn * gpre_ref[...]).astype(BF16)
        for j in range(nf):
            cols = slice(j * FF_TILE, (j + 1) * FF_TILE)
            ds = _nt(df16, wd_ref[cols, :])
            av = a_ref[:, cols].astype(F32)
            bv = b_ref[:, cols].astype(F32)
            sg = _sigmoid(av)
            db16 = (ds * (av * sg)).astype(BF16)
            da16 = (ds * bv * (sg * (1.0 + av * (1.0 - sg)))).astype(BF16)
            da_ref[:, cols] = da16
            db_ref[:, cols] = db16
            part = _nn(da16, wg_ref[cols, :]) + _nn(db16, wu_ref[cols, :])
            if j == 0:
                acc[...] = part
            else:
                acc[...] += part
        dn = acc[...]
        _acc_add(dgpre_ref, first, _colsum(dn * hn))
        dh_ref[...] = dho + _rms_bwd(hn, rh, gpre_ref[...], dn)

    row_f32 = _row_spec(tm, D_MODEL)
    row_ff = _row_spec(tm, D_FF)
    return pl.pallas_call(
        body, name=name, grid=(rows // tm,),
        in_specs=[row_f32, _behind_spec(tm) if with_front else row_f32, row_ff, row_ff, row_f32,
                  VMEM_SPEC, VMEM_SPEC, VMEM_SPEC, VMEM_SPEC, VMEM_SPEC] + ([VMEM_SPEC] if with_front else []),
        out_specs=[row_f32, row_ff, row_ff, row_f32, row_f32, _acc_spec(D_MODEL), _acc_spec(D_MODEL)],
        out_shape=[jax.ShapeDtypeStruct((rows, D_MODEL), F32), jax.ShapeDtypeStruct((rows, D_FF), BF16),
                   jax.ShapeDtypeStruct((rows, D_FF), BF16), jax.ShapeDtypeStruct((rows, D_MODEL), BF16),
                   jax.ShapeDtypeStruct((rows, D_MODEL), BF16), jax.ShapeDtypeStruct((8, D_MODEL), F32),
                   jax.ShapeDtypeStruct((8, D_MODEL), F32)],
        scratch_shapes=[pltpu.VMEM((tm, D_MODEL), F32)],
        compiler_params=_params(("arbitrary",), vmem_mb=62),
    )(dh_out, h, a, b, f, gpre, gpost, wg_t, wu_t, wd, *([front] if with_front else []))


def _wgrad(lhs, rhs, name, after=None):
    rows, width = lhs.shape
    tf = 256 if width % 256 == 0 else 128
    pieces = 5 if rows % 80 == 0 else 3 if rows % 48 == 0 else 1
    piece = rows // pieces
    tiles, slots = width // tf, 3

    def body(l_hbm, r_hbm, *rest):
        o_ref, l_buf, r_all, l_sems, r_sems = rest[-5:]
        j = pl.program_id(0)

        def fetch_r(c):
            part = pl.ds(c * piece, piece)
            return pltpu.make_async_copy(r_hbm.at[part, :], r_all.at[part, :], r_sems.at[c])

        def fetch_l(tile):
            slot = tile % slots
            start = tile * tf if isinstance(tile, int) else pl.multiple_of(tile * tf, 128)
            return pltpu.make_async_copy(l_hbm.at[:, pl.ds(start, tf)], l_buf.at[slot], l_sems.at[slot])

        @pl.when(j == 0)
        def _():
            for tile in range(min(slots - 1, tiles)):
                fetch_l(tile).start()
            for c in range(pieces):
                fetch_r(c).start()

        @pl.when(j + slots - 1 < tiles)
        def _():
            fetch_l(j + slots - 1).start()
        fetch_l(j).wait()
        lhs_tile = l_buf.at[j % slots]

        @pl.when(j == 0)
        def _():
            total = None
            for c in range(pieces):
                fetch_r(c).wait()
                part = _tn(lhs_tile[c * piece:(c + 1) * piece, :], r_all[c * piece:(c + 1) * piece, :])
                total = part if total is None else total + part
            o_ref[...] = total.astype(BF16)

        @pl.when(j > 0)
        def _():
            o_ref[...] = _tn(lhs_tile[...], r_all[...]).astype(BF16)

    return pl.pallas_call(
        body, name=name, grid=(tiles,),
        in_specs=[ANY_SPEC, ANY_SPEC] + ([] if after is None else [ANY_SPEC]),
        out_specs=pl.BlockSpec((tf, D_MODEL), lambda j: (j, 0)),
        out_shape=jax.ShapeDtypeStruct((width, D_MODEL), BF16),
        scratch_shapes=[pltpu.VMEM((slots, rows, tf), BF16), pltpu.VMEM((rows, D_MODEL), BF16),
                        pltpu.SemaphoreType.DMA((slots,)), pltpu.SemaphoreType.DMA((pieces,))],
        compiler_params=_params(("arbitrary",)),
    )(lhs, rhs, *([] if after is None else [after]))


def _chunk_cumsum(x, lower):
    tri = jnp.where(_tri(lower), 1.0, 0.0).astype(BF16)
    hi, lo = _split_bf16(x)
    return _nn(tri, hi) + _nn(tri, lo)


def _mix_in(h, g, win_p, wa2_p, b_a, cos, sin):
    rows = h.shape[0]
    tm = 640 if rows % 640 == 0 else BLK

    def body(h_ref, g_ref, win_ref, wa2_ref, ba_ref, cos_ref, sin_ref,
             gq_ref, gk_ref, gv_ref, gg_ref, sq_ref, sk_ref, sv_ref, ga_ref, loga_ref, bc_ref, n_ref):
        hn, _ = _rms(h_ref[...])
        n16 = (hn * g_ref[...]).astype(BF16)
        n_ref[...] = n16
        proj = _nt(n16, win_ref[...])
        gq_ref[...] = proj[:, P_GQ:P_GK]
        gk_ref[...] = proj[:, P_GK:P_GV]
        gv_ref[...] = proj[:, P_GV:P_GG].astype(BF16)
        gg_ref[...] = proj[:, P_GG:P_SQ]
        c1, s1 = cos_ref[...], sin_ref[...]
        c4 = jnp.concatenate([c1, c1, c1, c1], axis=1)
        s4 = jnp.concatenate([s1, s1, s1, s1], axis=1)
        sq = proj[:, P_SQ:P_SK]
        sk = proj[:, P_SK:P_SV]
        sq_ref[...] = (sq * c4 + _rot_half(sq) * s4).astype(BF16)
        sk_ref[...] = (sk * c1 + _rot_half(sk) * s1).astype(BF16)
        sv_ref[...] = proj[:, P_SV:P_GA].astype(BF16)
        ga = proj[:, P_GA:P_END]
        ga_ref[...] = ga
        z = _nn(ga, wa2_ref[...]) + ba_ref[...]
        loga = (jnp.minimum(z, 0.0) - jnp.log(1.0 + jnp.exp(-jnp.abs(z)))) * (1.0 / GLA_TAU)
        loga_ref[...] = loga
        for c in range(tm // BLK):
            rs = slice(c * BLK, (c + 1) * BLK)
            bc_ref[rs, :] = _chunk_cumsum(loga[rs, :], True)

    f32 = lambda c: jax.ShapeDtypeStruct((rows, c), F32)
    b16 = lambda c: jax.ShapeDtypeStruct((rows, c), BF16)
    rs = lambda c: _row_spec(tm, c)
    return pl.pallas_call(
        body, name="mix_in", grid=(rows // tm,),
        in_specs=[rs(D_MODEL), VMEM_SPEC, VMEM_SPEC, VMEM_SPEC, VMEM_SPEC, rs(128), rs(128)],
        out_specs=[rs(256), rs(256), rs(512), rs(512), rs(512), rs(128), rs(128), rs(128), rs(256), rs(256), rs(D_MODEL)],
        out_shape=[f32(256), f32(256), b16(512), f32(512), b16(512), b16(128), b16(128), f32(128), f32(256), f32(256),
                   b16(D_MODEL)],
        compiler_params=_params(("arbitrary",)),
    )(h, g, win_p, wa2_p, b_a, cos, sin)


def _side_by_side(parts, name):
    steps, per_step = parts[0]["steps"], parts[0]["per_step"]
    assert all((p["steps"], p["per_step"]) == (steps, per_step) for p in parts)
    counts = [[len(p[key]) for p in parts] for key in ("in_specs", "out_specs", "scratch_shapes")]

    def body(*refs):
        groups, pos = [], 0
        for kind in counts:
            groups.append([])
            for n in kind:
                groups[-1].append(refs[pos:pos + n])
                pos += n
        programs = [p["program"](*groups[0][k], *groups[1][k], *groups[2][k]) for k, p in enumerate(parts)]

        def blocks(c, carries):
            return tuple(block(c, carry) for (block, _, _), carry in zip(programs, carries))
        carries = lax.fori_loop(0, per_step, blocks, tuple(first for _, first, _ in programs))
        for (_, _, finish), carry in zip(programs, carries):
            finish(carry)

    outs = pl.pallas_call(
        body, name=name, grid=(steps,),
        in_specs=[s for p in parts for s in p["in_specs"]], out_specs=[s for p in parts for s in p["out_specs"]],
        out_shape=[s for p in parts for s in p["out_shape"]],
        scratch_shapes=[s for p in parts for s in p["scratch_shapes"]],
        compiler_params=_params(("arbitrary",)),
    )(*[a for p in parts for a in p["args"]])
    split, pos = [], 0
    for n in counts[1]:
        split.append(outs[pos:pos + n])
        pos += n
    return split


def _gla_factors(q, k, bc):
    bm = bc[BLK // 2 - 1:BLK // 2, :]
    bl = bc[BLK - 1:BLK, :]
    e_q, e_k, e_qe, e_kd = jnp.exp(bc - bm), jnp.exp(bm - bc), jnp.exp(bc), jnp.exp(bl - bc)
    return (q * e_q, k * e_k, q * e_qe, k * e_kd), (e_q, e_k, e_qe, e_kd), jnp.exp(bl)


def _gla_fwd(gq, gk, gv, gg, bc, wgn):
    rows = gq.shape[0]
    nc = rows // BLK
    per_step = _blocks_per_step(nc)
    scale = GLA_DK ** -0.5

    def body(q_ref, k_ref, v_ref, gg_ref, bc_ref, wgn_ref, o_ref, cat_ref, sp_ref, st):
        @pl.when(pl.program_id(0) == 0)
        def _():
            st[...] = jnp.zeros_like(st)
        low = _tri(True)
        wgn_v = wgn_ref[...]

        def chunk(c, carry):
            rr = pl.ds(pl.multiple_of(c * BLK, BLK), BLK)
            for p in range(2):
                sl = slice(128 * p, 128 * p + 128)
                (qt, kt, qe, kd), _, ebl = _gla_factors(q_ref[rr, sl] * scale, k_ref[rr, sl], bc_ref[rr, sl])
                s_prev = st[p]
                sp_ref[c, p] = s_prev
                s16 = s_prev.astype(BF16)
                qt16 = qt.astype(BF16)
                s_new = s_prev * ebl
                for hh in range(2):
                    hs = slice(128 * (2 * p + hh), 128 * (2 * p + hh) + 128)
                    lm = _half_mask(128, hh)
                    vh = v_ref[rr, hs]
                    pm = jnp.where(low, _nt(qt16, jnp.where(lm, kt, 0.0).astype(BF16)), 0.0)
                    o = _nn(pm.astype(BF16), vh) + _nt(jnp.where(lm, qe, 0.0).astype(BF16), s16)
                    s_new = s_new + _tn(vh, jnp.where(lm, kd, 0.0).astype(BF16))
                    o_ref[rr, hs] = o
                    on, _ = _rms(o)
                    gate = gg_ref[rr, hs]
                    cat_ref[rr, hs] = (on * wgn_v * (gate * _sigmoid(gate))).astype(BF16)
                st[p] = s_new
            return carry
        return chunk, 0, lambda carry: None

    rs = lambda c: _row_spec(per_step * BLK, c)
    return dict(
        program=body, steps=nc // per_step, per_step=per_step,
        in_specs=[rs(256), rs(256), rs(512), rs(512), rs(256), VMEM_SPEC],
        out_specs=[rs(512), rs(512), pl.BlockSpec((per_step, 2, 128, 128), lambda i: (i, 0, 0, 0))],
        out_shape=[jax.ShapeDtypeStruct((rows, 512), F32), jax.ShapeDtypeStruct((rows, 512), BF16),
                   jax.ShapeDtypeStruct((nc, 2, 128, 128), F32)],
        scratch_shapes=[pltpu.VMEM((2, 128, 128), F32)],
        args=(gq, gk, gv, gg, bc, wgn))


def _gla_bwd(dcat, o_all, gq, gk, gv, gg, bc, sp, wgn):
    rows = gq.shape[0]
    nc = rows // BLK
    per_step = _blocks_per_step(nc)
    steps = nc // per_step
    scale = GLA_DK ** -0.5

    def body(dc_ref, o_ref, q_ref, k_ref, v_ref, gg_ref, bc_ref, sp_ref, wgn_ref,
             dq_ref, dk_ref, dv_ref, dgg_ref, dla_ref, dwgn_ref, dst):
        first = pl.program_id(0) == 0

        @pl.when(first)
        def _():
            dst[...] = jnp.zeros_like(dst)
        low, upp = _tri(True), _tri(False)
        last_row = lax.broadcasted_iota(jnp.int32, (BLK, 1), 0) == BLK - 1
        wgn_v = wgn_ref[...]

        def chunk(c, dwgn):
            rr = pl.ds(pl.multiple_of((per_step - 1 - c) * BLK, BLK), BLK)
            for p in range(2):
                sl = slice(128 * p, 128 * p + 128)
                (qt, kt, qe, kd), (e_q, e_k, e_qe, e_kd), ebl = _gla_factors(
                    q_ref[rr, sl] * scale, k_ref[rr, sl], bc_ref[rr, sl])
                s_prev = sp_ref[per_step - 1 - c, p]
                s16 = s_prev.astype(BF16)
                ds_next = dst[p]
                ds16 = ds_next.astype(BF16)
                qt16 = qt.astype(BF16)
                ds_new = ds_next * ebl
                dqt = jnp.zeros((BLK, 128), F32)
                dkt = jnp.zeros((BLK, 128), F32)
                dqe = jnp.zeros((BLK, 128), F32)
                dkd = jnp.zeros((BLK, 128), F32)
                for hh in range(2):
                    hs = slice(128 * (2 * p + hh), 128 * (2 * p + hh) + 128)
                    lm = _half_mask(128, hh)
                    on, ro = _rms(o_ref[rr, hs])
                    gate = gg_ref[rr, hs]
                    sg = _sigmoid(gate)
                    si = gate * sg
                    dog = dc_ref[rr, hs]
                    dwgn = dwgn + _colsum(dog * si * on)
                    dgg_ref[rr, hs] = dog * (on * wgn_v) * (sg * (1.0 + gate * (1.0 - sg)))
                    do16 = _rms_bwd(on, ro, wgn_v, dog * si).astype(BF16)
                    vh = v_ref[rr, hs]
                    ktm16 = jnp.where(lm, kt, 0.0).astype(BF16)
                    qtm16 = jnp.where(lm, qt, 0.0).astype(BF16)
                    qem16 = jnp.where(lm, qe, 0.0).astype(BF16)
                    kdm16 = jnp.where(lm, kd, 0.0).astype(BF16)
                    p_t = jnp.where(upp, _nt(ktm16, qt16), 0.0)
                    dp_t = jnp.where(upp, _nt(vh, do16), 0.0)
                    dp = jnp.where(low, _nt(do16, vh), 0.0)
                    dv_ref[rr, hs] = _nn(p_t.astype(BF16), do16) + _nt(kdm16, ds16)
                    dqt = dqt + _nn(dp.astype(BF16), ktm16)
                    dkt = dkt + _nn(dp_t.astype(BF16), qtm16)
                    dqe = dqe + jnp.where(lm, _nn(do16, s16), 0.0)
                    dkd = dkd + jnp.where(lm, _nn(vh, ds16), 0.0)
                    ds_new = ds_new + _tn(do16, qem16)
                debl = _colsum(ds_next * s_prev)
                dq_ref[rr, sl] = (dqt * e_q + dqe * e_qe) * scale
                dk_ref[rr, sl] = dkt * e_k + dkd * e_kd
                dkd_kd = dkd * kd
                db = dqt * qt - dkt * kt + dqe * qe - dkd_kd
                db = jnp.where(last_row, db + (_colsum(dkd_kd) + debl * ebl), db)
                dla_ref[rr, sl] = _chunk_cumsum(db, False)
                dst[p] = ds_new
            return dwgn

        def finish(dwgn):
            _acc_add(dwgn_ref, first, dwgn)
        return chunk, jnp.zeros((1, 128), F32), finish

    rev = lambda c: pl.BlockSpec((per_step * BLK, c), lambda i: (steps - 1 - i, 0))
    f32 = lambda c: jax.ShapeDtypeStruct((rows, c), F32)
    return dict(
        program=body, steps=steps, per_step=per_step,
        in_specs=[rev(512), rev(512), rev(256), rev(256), rev(512), rev(512), rev(256),
                  pl.BlockSpec((per_step, 2, 128, 128), lambda i: (steps - 1 - i, 0, 0, 0)), VMEM_SPEC],
        out_specs=[rev(256), rev(256), rev(512), rev(512), rev(256), _acc_spec(128)],
        out_shape=[f32(256), f32(256), f32(512), f32(512), f32(256), jax.ShapeDtypeStruct((8, 128), F32)],
        scratch_shapes=[pltpu.VMEM((2, 128, 128), F32)],
        args=(dcat, o_all, gq, gk, gv, gg, bc, sp, wgn))


def _swa_masks(i):
    t = lax.broadcasted_iota(jnp.int32, (BLK, BLK), 0)
    c = lax.broadcasted_iota(jnp.int32, (BLK, BLK), 1)
    own_side = c <= t
    band_ok = i >= jnp.where(own_side, 1, 2)
    meta_ok = (c % N_META) <= jnp.where(i >= 1, N_META, t - PAD_ROWS)
    return own_side, band_ok, meta_ok, c // N_META


def _swa_blocks(ref, i):
    prev = pl.multiple_of(jnp.maximum(i - 1, 0) * BLK, BLK)
    own = pl.multiple_of(i * BLK, BLK)
    return jnp.concatenate([ref[pl.ds(prev, BLK), :], ref[pl.ds(own, BLK), :]], axis=0), prev, own


def _swa_meta_operand(ref):
    blk = ref[0:BLK, :]
    swapped = pltpu.roll(blk, 64, 1)
    lo = jnp.where(_half_mask(128, 0), blk, swapped)
    hi = jnp.where(_half_mask(128, 1), blk, swapped)
    meta = jnp.concatenate([lo, lo, hi, hi], axis=1)[PAD_ROWS:BLK, :]
    tiled = jnp.concatenate([meta] * SWA_HEADS, axis=0)
    j = lax.broadcasted_iota(jnp.int32, tiled.shape, 0)
    lane = lax.broadcasted_iota(jnp.int32, tiled.shape, 1)
    return jnp.where(j // N_META == lane // SWA_HD, tiled, jnp.zeros_like(tiled))


def _swa_meta_fold(acc):
    out = jnp.zeros((N_META, 128), F32)
    for hd in range(SWA_HEADS):
        half, kv = hd % 2, hd // 4
        piece = acc[N_META * hd:N_META * (hd + 1), 128 * (hd // 2):128 * (hd // 2) + 128]
        piece = jnp.where(_half_mask(128, half), piece, 0.0)
        out = out + (piece if half == kv else pltpu.roll(piece, 64, 1))
    return out


def _by_head(group, per_head):
    out = jnp.zeros((BLK, BLK), F32)
    for hd, v in enumerate(per_head):
        out = jnp.where(group == hd, v, out)
    return out


def _place(x, kv):
    if kv == 0:
        lo = jnp.where(_half_mask(128, 0), x, jnp.zeros_like(x))
        return lo, pltpu.roll(lo, 64, 1)
    hi = jnp.where(_half_mask(128, 1), x, jnp.zeros_like(x))
    return pltpu.roll(hi, 64, 1), hi


def _swa_fwd(sq, sk, sv, sinks, wn):
    rows = sq.shape[0]
    nb = rows // BLK
    per_step = _blocks_per_step(nb)
    scale = SWA_HD ** -0.5

    def body(q_ref, k_ref, v_ref, sink_ref, wn_ref, o_ref, cat_ref, lse_ref, kp, vp):
        step = pl.program_id(0)

        @pl.when(step == 0)
        def _():
            kp[...] = _swa_meta_operand(k_ref)
            vp[...] = _swa_meta_operand(v_ref)

        def one_block(c, carry):
            i = step * per_step + c
            rr = pl.ds(pl.multiple_of(c * BLK, BLK), BLK)
            own_side, band_ok, meta_ok, group = _swa_masks(i)
            k2, _, _ = _swa_blocks(k_ref, i)
            v2, _, _ = _swa_blocks(v_ref, i)
            kz = (_place(k2, 0), _place(k2, 1))
            vz = (_place(v2, 0), _place(v2, 1))
            q_all = q_ref[rr, :]
            s_meta = jnp.where(meta_ok, _nt(q_all, kp[...]) * scale, NEG_INF)
            s_band, m = [], []
            for hd in range(SWA_HEADS):
                kv, half = hd // 4, hd % 2
                q_pair = q_all[:, 128 * (hd // 2):128 * (hd // 2) + 128]
                s2 = _nt(q_pair, kz[kv][half])
                s = jnp.where(band_ok, jnp.where(own_side, s2[:, BLK:], s2[:, :BLK]) * scale, NEG_INF)
                top = jnp.maximum(jnp.max(s, axis=-1, keepdims=True),
                                  jnp.max(jnp.where(group == hd, s_meta, NEG_INF), axis=-1, keepdims=True))
                s_band.append(s)
                m.append(jnp.maximum(top, sink_ref[0, hd]))
            e_meta = jnp.exp(s_meta - _by_head(group, m))
            o_meta = _nn(e_meta.astype(BF16), vp[...])
            outs = []
            for pr in range(4):
                o_pair = o_meta[:, 128 * pr:128 * pr + 128]
                rden = []
                for half in range(2):
                    hd = 2 * pr + half
                    kv = hd // 4
                    e = jnp.exp(s_band[hd] - m[hd])
                    den = (jnp.sum(e, axis=-1, keepdims=True)
                           + jnp.sum(jnp.where(group == hd, e_meta, 0.0), axis=-1, keepdims=True)
                           + jnp.exp(sink_ref[0, hd] - m[hd]))
                    lse_ref[rr, hd:hd + 1] = m[hd] + jnp.log(den)
                    rden.append(1.0 / den)
                    e2 = jnp.concatenate([jnp.where(own_side, 0.0, e), jnp.where(own_side, e, 0.0)], axis=1).astype(BF16)
                    o_pair = o_pair + _nn(e2, vz[kv][half])
                outs.append(o_pair * jnp.where(_half_mask(128, 0), rden[0], rden[1]))
            o = jnp.concatenate(outs, axis=1)
            o_ref[rr, :] = o
            on, _ = _rms(o)
            cat_ref[rr, :] = (on * wn_ref[...]).astype(BF16)
            return carry
        return one_block, 0, lambda carry: None

    return dict(
        program=body, steps=nb // per_step, per_step=per_step,
        in_specs=[_row_spec(per_step * BLK, 512), VMEM_SPEC, VMEM_SPEC, SMEM_SPEC, VMEM_SPEC],
        out_specs=[_row_spec(per_step * BLK, 512), _row_spec(per_step * BLK, 512), _row_spec(per_step * BLK, SWA_HEADS)],
        out_shape=[jax.ShapeDtypeStruct((rows, 512), F32), jax.ShapeDtypeStruct((rows, 512), BF16),
                   jax.ShapeDtypeStruct((rows, SWA_HEADS), F32)],
        scratch_shapes=[pltpu.VMEM((BLK, 512), BF16), pltpu.VMEM((BLK, 512), BF16)],
        args=(sq, sk, sv, sinks, wn))


def _swa_bwd(dcat, o_all, sq, sk, sv, lse, sinks, wn):
    rows = sq.shape[0]
    nb = rows // BLK
    per_step = _blocks_per_step(nb)
    steps = nb // per_step
    scale = SWA_HD ** -0.5

    def body(dc_ref, o_ref, q_ref, k_ref, v_ref, lse_ref, sink_ref, wn_ref, dq_ref, dk_ref, dv_ref, dsink_ref, dwn_ref,
             kp, vp, dkp, dvp):
        step = pl.program_id(0)

        @pl.when(step == 0)
        def _():
            dk_ref[...] = jnp.zeros_like(dk_ref)
            dv_ref[...] = jnp.zeros_like(dv_ref)
            dkp[...] = jnp.zeros_like(dkp)
            dvp[...] = jnp.zeros_like(dvp)
            kp[...] = _swa_meta_operand(k_ref)
            vp[...] = _swa_meta_operand(v_ref)

        def one_block(c, carry):
            i = step * per_step + c
            rr = pl.ds(pl.multiple_of(c * BLK, BLK), BLK)
            first = i == 0
            own_side, band_ok, meta_ok, group = _swa_masks(i)
            k2, prev, own = _swa_blocks(k_ref, i)
            v2, _, _ = _swa_blocks(v_ref, i)
            kz = (_place(k2, 0), _place(k2, 1))
            vz = (_place(v2, 0), _place(v2, 1))
            o = o_ref[rr, :]
            on, ro = _rms(o)
            dc = dc_ref[rr, :]
            _acc_add(dwn_ref, first, _colsum(dc * on))
            do = _rms_bwd(on, ro, wn_ref[...], dc)
            do_o = do * o
            do16 = do.astype(BF16)
            q_all = q_ref[rr, :]
            lse = [lse_ref[rr, hd:hd + 1] for hd in range(SWA_HEADS)]
            delta = [jnp.sum(jnp.where(_half_mask(128, hd % 2), do_o[:, 128 * (hd // 2):128 * (hd // 2) + 128], 0.0),
                             axis=-1, keepdims=True) for hd in range(SWA_HEADS)]
            s_meta = jnp.where(meta_ok, _nt(q_all, kp[...]) * scale, NEG_INF)
            p_meta = jnp.exp(s_meta - _by_head(group, lse))
            ds_meta16 = (p_meta * (_nt(do16, vp[...]) - _by_head(group, delta)) * scale).astype(BF16)
            dq_meta = _nn(ds_meta16, kp[...])
            dkp[...] += _tn(ds_meta16, q_all)
            dvp[...] += _tn(p_meta.astype(BF16), do16)
            own2 = jnp.concatenate([own_side.astype(jnp.int32)] * 2, axis=0) > 0
            ok2 = jnp.concatenate([band_ok.astype(jnp.int32)] * 2, axis=0) > 0

            def window(x2):
                return jnp.where(own2, x2[:, BLK:], x2[:, :BLK])

            def unwindow(x):
                return jnp.concatenate([jnp.where(own2, 0.0, x), jnp.where(own2, x, 0.0)], axis=1).astype(BF16)
            lane8 = lax.broadcasted_iota(jnp.int32, (1, 128), 1)
            dsink = jnp.zeros((1, 128), F32)
            dq_pairs = [dq_meta[:, 128 * pr:128 * pr + 128] for pr in range(4)]
            dk2 = [[None, None], [None, None]]
            dv2 = [[None, None], [None, None]]
            for kv in range(2):
                for half in range(2):
                    heads, pairs = (4 * kv + half, 4 * kv + 2 + half), (2 * kv, 2 * kv + 1)
                    q_s = jnp.concatenate([q_all[:, 128 * pr:128 * pr + 128] for pr in pairs], axis=0)
                    do_s = jnp.concatenate([do16[:, 128 * pr:128 * pr + 128] for pr in pairs], axis=0)
                    lse_s = jnp.concatenate([lse[hd] for hd in heads], axis=0)
                    delta_s = jnp.concatenate([delta[hd] for hd in heads], axis=0)
                    s = jnp.where(ok2, window(_nt(q_s, kz[kv][half])) * scale, NEG_INF)
                    prob = jnp.exp(s - lse_s)
                    for hd in heads:
                        dsink = dsink + jnp.where(lane8 == hd, -jnp.sum(jnp.exp(sink_ref[0, hd] - lse[hd]) * delta[hd]), 0.0)
                    ds2 = unwindow(prob * (window(_nt(do_s, vz[kv][half])) - delta_s) * scale)
                    dq_s = _nn(ds2, kz[kv][half])
                    dq_pairs[pairs[0]] = dq_pairs[pairs[0]] + dq_s[:BLK]
                    dq_pairs[pairs[1]] = dq_pairs[pairs[1]] + dq_s[BLK:]
                    dk2[kv][half] = _tn(ds2, q_s)
                    dv2[kv][half] = _tn(unwindow(prob), do_s)
            dq_ref[rr, :] = jnp.concatenate(dq_pairs, axis=1)
            _acc_add(dsink_ref, first, dsink)
            for ref, acc2 in ((dk_ref, dk2), (dv_ref, dv2)):
                tot = jnp.zeros((2 * BLK, 128), F32)
                for kv in range(2):
                    for half in range(2):
                        part = jnp.where(_half_mask(128, half), acc2[kv][half], 0.0)
                        tot = tot + (part if half == kv else pltpu.roll(part, 64, 1))
                ref[pl.ds(prev, BLK), :] += tot[:BLK]
                ref[pl.ds(own, BLK), :] += tot[BLK:]
            return carry

        def finish(carry):
            del carry

            @pl.when(step == steps - 1)
            def _():
                dk_ref[PAD_ROWS:BLK, :] += _swa_meta_fold(dkp[...])
                dv_ref[PAD_ROWS:BLK, :] += _swa_meta_fold(dvp[...])
        return one_block, jnp.zeros((1, 128), F32), finish

    full = pl.BlockSpec((rows, 128), lambda i: (0, 0))
    blocks = lambda cols: _row_spec(per_step * BLK, cols)
    return dict(
        program=body, steps=steps, per_step=per_step,
        in_specs=[blocks(512), blocks(512), blocks(512), VMEM_SPEC, VMEM_SPEC, blocks(SWA_HEADS), SMEM_SPEC, VMEM_SPEC],
        out_specs=[blocks(512), full, full, _acc_spec(128), _acc_spec(512)],
        out_shape=[jax.ShapeDtypeStruct((rows, 512), F32), jax.ShapeDtypeStruct((rows, 128), F32),
                   jax.ShapeDtypeStruct((rows, 128), F32), jax.ShapeDtypeStruct((8, 128), F32),
                   jax.ShapeDtypeStruct((8, 512), F32)],
        scratch_shapes=[pltpu.VMEM((BLK, 512), BF16), pltpu.VMEM((BLK, 512), BF16),
                        pltpu.VMEM((BLK, 512), F32), pltpu.VMEM((BLK, 512), F32)],
        args=(dcat, o_all, sq, sk, sv, lse, sinks, wn))


def _mix_out_bwd(dh, m, wout, gpost):
    rows = dh.shape[0]
    tm = _row_tile(rows)

    def body(dh_ref, m_ref, w_ref, g_ref, dcg_ref, dcs_ref, dm_ref, dg_ref):
        first = pl.program_id(0) == 0
        dhv = dh_ref[...]
        mn, rm = _rms(m_ref[...])
        _acc_add(dg_ref, first, _colsum(dhv * mn))
        dm16 = _rms_bwd(mn, rm, g_ref[...], dhv).astype(BF16)
        dm_ref[...] = dm16
        dcat = _nt(dm16, w_ref[...])
        dcg_ref[...] = dcat[:, 0:512]
        dcs_ref[...] = dcat[:, 512:1024]

    row_f32 = _row_spec(tm, D_MODEL)
    return pl.pallas_call(
        body, name="mix_out_bwd", grid=(rows // tm,),
        in_specs=[row_f32, row_f32, VMEM_SPEC, VMEM_SPEC],
        out_specs=[_row_spec(tm, 512), _row_spec(tm, 512), row_f32, _acc_spec(D_MODEL)],
        out_shape=[jax.ShapeDtypeStruct((rows, 512), F32), jax.ShapeDtypeStruct((rows, 512), F32),
                   jax.ShapeDtypeStruct((rows, D_MODEL), BF16), jax.ShapeDtypeStruct((8, D_MODEL), F32)],
        compiler_params=_params(("arbitrary",)),
    )(dh, m, wout, gpost)


def _mix_in_bwd(dh_out, h, g, win_p, wa2_p, cos, sin, loga, ga, dgq, dgk, dgv, dgg, dsq, dsk, dsv, dloga):
    rows = h.shape[0]
    tm = _row_tile(rows)

    def body(dho_ref, h_ref, g_ref, win_ref, wa2_ref, cos_ref, sin_ref, loga_ref, ga_ref,
             dgq_ref, dgk_ref, dgv_ref, dgg_ref, dsq_ref, dsk_ref, dsv_ref, dla_ref,
             dh_ref, dproj_ref, dwa2_ref, dg_ref, dba_ref):
        first = pl.program_id(0) == 0
        dz = dla_ref[...] * (1.0 / GLA_TAU) * (1.0 - jnp.exp(GLA_TAU * loga_ref[...]))
        _acc_add(dba_ref, first, _colsum(dz))
        dga = _nt(dz, wa2_ref[...])
        pa = _tn(ga_ref[...], dz)
        c1, s1 = cos_ref[...], sin_ref[...]
        c4 = jnp.concatenate([c1, c1, c1, c1], axis=1)
        s4 = jnp.concatenate([s1, s1, s1, s1], axis=1)
        dq_r, dk_r = dsq_ref[...], dsk_ref[...]
        dsq = dq_r * c4 - _rot_half(dq_r * s4)
        dsk = dk_r * c1 - _rot_half(dk_r * s1)
        dproj16 = jnp.concatenate(
            [dgq_ref[...], dgk_ref[...], dgv_ref[...], dgg_ref[...], dsq, dsk, dsv_ref[...], dga], axis=1).astype(BF16)
        dproj_ref[...] = dproj16
        dn = _nn(dproj16, win_ref[...])

        @pl.when(first)
        def _():
            dwa2_ref[...] = pa

        @pl.when(jnp.logical_not(first))
        def _():
            dwa2_ref[...] += pa
        hn, rh = _rms(h_ref[...])
        _acc_add(dg_ref, first, _colsum(dn * hn))
        dh_ref[...] = dho_ref[...] + _rms_bwd(hn, rh, g_ref[...], dn)

    rs = lambda c: _row_spec(tm, c)
    return pl.pallas_call(
        body, name="mix_in_bwd", grid=(rows // tm,),
        in_specs=[rs(D_MODEL), rs(D_MODEL), VMEM_SPEC, VMEM_SPEC, VMEM_SPEC, rs(128), rs(128), rs(256), rs(128),
                  rs(256), rs(256), rs(512), rs(512), rs(512), rs(128), rs(128), rs(256)],
        out_specs=[rs(D_MODEL), rs(P_END), pl.BlockSpec((128, 256), lambda i: (0, 0)), _acc_spec(D_MODEL), _acc_spec(256)],
        out_shape=[jax.ShapeDtypeStruct((rows, D_MODEL), F32), jax.ShapeDtypeStruct((rows, P_END), BF16),
                   jax.ShapeDtypeStruct((128, 256), F32), jax.ShapeDtypeStruct((8, D_MODEL), F32),
                   jax.ShapeDtypeStruct((8, 256), F32)],
        compiler_params=_params(("arbitrary",)),
    )(dh_out, h, g, win_p, wa2_p, cos, sin, loga, ga, dgq, dgk, dgv, dgg, dsq, dsk, dsv, dloga)


def _rope_tables(rows):
    pos = (jnp.arange(rows, dtype=jnp.int32) - PAD_ROWS).astype(F32)
    inv_freq = 1.0 / (ROPE_THETA ** (jnp.arange(0, SWA_HD, 2, dtype=F32) / SWA_HD))
    ang = pos[:, None] * inv_freq[None, :]
    return jnp.tile(jnp.cos(ang), (1, 4)), jnp.tile(jnp.sin(ang), (1, 4))


def _local_step(x, tgt, front, w, late_weights=None, on_grads=None, on_small=None):
    cos, sin = _rope_tables(x.shape[0] + BLK)
    g = {}

    def tell(group, names):
        for nm in names:
            g[nm] = grads_now[nm]
        return None if on_grads is None else on_grads(group, {nm: grads_now[nm] for nm in names})

    h1, a1, b1, s1, f1 = _ffn_fwd(x, w["ffn1_pre"], w["wg1"], w["wu1"], w["wd1"], w["ffn1_post"], front=front)
    if late_weights is not None:
        w = {**w, **late_weights("win", f1)}
    gq, gk, gv, gg, sq, sk, sv, ga, loga, bc, n2 = _mix_in(h1, w["mix_pre"], w["win"], w["wa2"], w["b_a"], cos, sin)
    (o_g, cat_g, sp), (o_s, cat_s, lse) = _side_by_side(
        [_gla_fwd(gq, gk, gv, gg, bc, w["gla_norm"]), _swa_fwd(sq, sk, sv, w["sinks"], w["swa_norm"])], "attention_fwd")
    if late_weights is not None:
        w = {**w, **late_weights("rest", lse)}
    h2, m, a2, b2, s2, f2, dy, loss = _ffn_fwd(h1, w["ffn2_pre"], w["wg2"], w["wu2"], w["wd2"], w["ffn2_post"], tgt,
                                               mixed=(cat_g, cat_s, w["wout"], w["mix_post"]))
    dh2, da, db, df, n3, g["ffn2_pre"], g["ffn2_post"] = _ffn_bwd_act(
        dy, h2, a2, b2, f2, w["ffn2_pre"], w["ffn2_post"], w["wg2"], w["wu2"], w["wd2"], "ffn2_bwd_act")
    grads_now = dict(wd2=_wgrad(s2, df, "ffn2_wgrad_down"), wg2=_wgrad(da, n3, "ffn2_wgrad_gate"),
                     wu2=_wgrad(db, n3, "ffn2_wgrad_up"))
    tok = tell("ffn2", ("wd2", "wg2", "wu2"))
    dcg, dcs, dm, g["mix_post"] = _mix_out_bwd(dh2, m, w["wout"], w["mix_post"] + (0.0 if tok is None else tok[0, 0]))
    (dgq, dgk, dgv, dgg, dloga, g["gla_norm"]), (dsq, dsk, dsv, g["sinks"], g["swa_norm"]) = _side_by_side(
        [_gla_bwd(dcg, o_g, gq, gk, gv, gg, bc, sp, w["gla_norm"]),
         _swa_bwd(dcs, o_s, sq, sk, sv, lse, w["sinks"], w["swa_norm"])], "attention_bwd")
    dh1, dproj, g["wa2"], g["mix_pre"], g["b_a"] = _mix_in_bwd(
        dh2, h1, w["mix_pre"], w["win"], w["wa2"], cos, sin, loga, ga, dgq, dgk, dgv, dgg, dsq, dsk, dsv, dloga)
    dh0, da, db, df, n1, g["ffn1_pre"], g["ffn1_post"] = _ffn_bwd_act(
        dh1, x, a1, b1, f1, w["ffn1_pre"], w["ffn1_post"], w["wg1"], w["wu1"], w["wd1"], "ffn1_bwd_act", front=front)
    tok = None if on_small is None else on_small(loss[0, 0], dh0, g)
    grads_now = dict(wd1=_wgrad(s1, df, "ffn1_wgrad_down", after=tok))
    tok = tell("ffn1_down", ("wd1",))
    grads_now = dict(wg1=_wgrad(da, n1, "ffn1_wgrad_gate", after=tok))
    tok = tell("ffn1_gate", ("wg1",))
    grads_now = dict(wu1=_wgrad(db, n1, "ffn1_wgrad_up", after=tok))
    tok = tell("ffn1_up", ("wu1",))
    grads_now = dict(win=_wgrad(dproj, n2, "win_wgrad", after=tok),
                     wout=jnp.concatenate([_wgrad(cat_g, dm, "wout_wgrad_gla", after=tok),
                                           _wgrad(cat_s, dm, "wout_wgrad_swa", after=tok)], axis=0))
    tell("mix", ("wout", "win"))
    return loss[0, 0], dh0, g


def _win_pad_rows(win_t):
    pad = jnp.zeros((P_END - P_GA - 16, win_t.shape[1]), win_t.dtype)
    return jnp.concatenate([win_t[0:1536], win_t[1552:2320], win_t[1536:1552], pad], axis=0)


def _win_unpad_rows(win_p):
    return jnp.concatenate([win_p[0:1536], win_p[P_GA:P_GA + 16], win_p[1536:P_GA]], axis=0)


def _place_on_mesh():
    return lax.axis_index("x"), lax.axis_index("y"), lax.axis_index("c")


def _dev_index(px, py, pc):
    return 4 * px + 2 * py + pc


def _other_devices(x, y, c):
    flip = lambda v, f: 1 - v if f else v
    return [(flip(x, fx), flip(y, fy), flip(c, fc)) for fx in (0, 1) for fy in (0, 1) for fc in (0, 1)][1:]


def _all_gather(shards):
    n = len(shards)

    def body(*refs):
        ins, outs = refs[:n], refs[n:2 * n]
        zeros_ref, send_sems, recv_sems, local_sems = refs[2 * n:]
        zeros_ref[...] = jnp.zeros_like(zeros_ref)
        x, y, c = _place_on_mesh()
        me, sibling = (x, y, c), (x, y, 1 - c)
        chips = [(1 - x, y), (x, 1 - y), (1 - x, 1 - y)]

        def rows(k, px, py, pc):
            r = ins[k].shape[0]
            return outs[k].at[pl.ds(pl.multiple_of(_dev_index(px, py, pc) * r, 8), r), :]

        def copy(k, slot, block, to, src=None):
            return pltpu.make_async_remote_copy(
                src_ref=rows(k, *block) if src is None else src, dst_ref=rows(k, *block),
                send_sem=send_sems.at[k, slot], recv_sem=recv_sems.at[k, slot], device_id=to, device_id_type=MESH)

        local = [pltpu.make_async_copy(ins[k], rows(k, *me), local_sems.at[k]) for k in range(n)]
        sends = []
        for k in range(n):
            local[k].start()
            sends.append(copy(k, 0, me, sibling, src=ins[k]))
            sends += [copy(k, 1 + j, me, (*chip, c), src=ins[k]) for j, chip in enumerate(chips)]
        for cp in sends:
            cp.start()
        for k in range(n):
            for j, chip in enumerate(chips):
                copy(k, 1 + j, (*chip, c), me).wait_recv()
                passed = copy(k, 4 + j, (*chip, c), sibling)
                passed.start()
                sends.append(passed)
        for k in range(n):
            copy(k, 0, sibling, me).wait_recv()
            for j, chip in enumerate(chips):
                copy(k, 4 + j, (*chip, 1 - c), me).wait_recv()
        for cp in sends:
            cp.wait_send()
        for cp in local:
            cp.wait()

    return pl.pallas_call(
        body, name="all_gather_weights",
        in_specs=[ANY_SPEC] * n, out_specs=[ANY_SPEC] * n + [VMEM_SPEC],
        out_shape=[jax.ShapeDtypeStruct((N_DEV * s.shape[0], s.shape[1]), s.dtype) for s in shards]
        + [jax.ShapeDtypeStruct((8, 128), F32)],
        scratch_shapes=[pltpu.SemaphoreType.DMA((n, 7)), pltpu.SemaphoreType.DMA((n, 7)), pltpu.SemaphoreType.DMA((n,))],
    )(*shards)


HBM_SPEC = pl.BlockSpec(memory_space=pltpu.HBM)
SEM_SPEC = pl.BlockSpec(memory_space=pltpu.SEMAPHORE)
DATAFLOW = pltpu.SideEffectType.DATAFLOW_SIDE_EFFECTING


GATHER, SCATTER, SCATTER_CHIPS = "gather", "scatter", "scatter among chips"


def _exchange_peers(kind):
    x, y, c = _place_on_mesh()
    if kind == SCATTER_CHIPS:
        peers = [(1 - x, y, c), (x, 1 - y, c), (1 - x, 1 - y, c)]
        return peers, [2 * p[0] + p[1] for p in peers], 2 * x + y, 4
    peers = _other_devices(x, y, c)
    return peers, [_dev_index(*p) for p in peers], _dev_index(x, y, c), N_DEV


def _exchange_copies(srcs, lands, send_sems, recv_sems, own_sems, kind, arriving):
    peers, theirs, me, blocks = _exchange_peers(kind)
    remote, local = [], []
    for k, (src, land) in enumerate(zip(srcs, lands)):
        r = land.shape[0] // blocks

        def block(ref, d):
            return ref.at[pl.ds(pl.multiple_of(d * r, 8), r), :]

        for f, (peer, him) in enumerate(zip(peers, theirs)):
            mine, his = (him, me) if arriving else (me, him)
            sem = len(peers) * k + f
            remote.append(pltpu.make_async_remote_copy(
                src_ref=src if kind == GATHER else block(src, his), dst_ref=block(land, mine),
                send_sem=send_sems.at[sem], recv_sem=recv_sems.at[sem], device_id=peer, device_id_type=MESH))
        local.append(pltpu.make_async_copy(src if kind == GATHER else block(src, me), block(land, me), own_sems.at[k]))
    return remote, local


def _exchange_start(srcs, kind, name):
    n = len(srcs)
    lands = [lax.empty((N_DEV * s.shape[0], s.shape[1]) if kind == GATHER else s.shape, s.dtype) for s in srcs]
    sems = (3 if kind == SCATTER_CHIPS else 7) * n

    def body(*refs):
        remote, local = _exchange_copies(refs[:n], refs[n:2 * n], *refs[2 * n:2 * n + 3], kind, False)
        for cp in remote + local:
            cp.start()
        refs[-1][...] = jnp.zeros_like(refs[-1])

    both = list(srcs) + list(lands)
    outs = pl.pallas_call(
        body, name=name,
        out_shape=(pltpu.SemaphoreType.DMA((sems,)), pltpu.SemaphoreType.DMA((sems,)), pltpu.SemaphoreType.DMA((n,)),
                   *[pltpu.HBM(a.shape, a.dtype) for a in both], jax.ShapeDtypeStruct((8, 128), F32)),
        in_specs=[HBM_SPEC] * (2 * n), out_specs=(SEM_SPEC, SEM_SPEC, SEM_SPEC, *[HBM_SPEC] * (2 * n), VMEM_SPEC),
        input_output_aliases={i: 3 + i for i in range(2 * n)},
        compiler_params=pltpu.CompilerParams(has_side_effects=DATAFLOW),
    )(*[pltpu.with_memory_space_constraint(a, pltpu.HBM) for a in both])
    return outs[0:3], outs[3:3 + n], outs[3 + n:3 + 2 * n], outs[-1]


def _exchange_wait(started, kind, after, name):
    sems, srcs, lands, _ = started
    n = len(srcs)

    def body(*refs):
        args = (refs[:n], refs[n:2 * n], *refs[2 * n:2 * n + 3], kind)
        going, local = _exchange_copies(*args, False)
        for cp in going:
            cp.wait_send()
        for cp in local:
            cp.wait()
        for cp in _exchange_copies(*args, True)[0]:
            cp.wait_recv()

    both = list(srcs) + list(lands)
    outs = pl.pallas_call(
        body, name=name, out_shape=[pltpu.HBM(a.shape, a.dtype) for a in both],
        in_specs=[HBM_SPEC] * (2 * n) + [SEM_SPEC, SEM_SPEC, SEM_SPEC, ANY_SPEC], out_specs=[HBM_SPEC] * (2 * n),
        input_output_aliases={i: i for i in range(2 * n)},
        compiler_params=pltpu.CompilerParams(has_side_effects=DATAFLOW),
    )(*both, *sems, after)
    return outs[n:]


def _sibling_reduce(part, name):
    r, cols = part.shape[0] // N_DEV, part.shape[1]

    def body(p_ref, o_ref, mine, got, send_sems, recv_sems, own_sems):
        x, y, c = _place_on_mesh()

        def block(d):
            return p_ref.at[pl.ds(pl.multiple_of(d * r, 8), r), :]
        swaps = [pltpu.make_async_remote_copy(
            src_ref=block(2 * j + 1 - c), dst_ref=got.at[j], send_sem=send_sems.at[j], recv_sem=recv_sems.at[j],
            device_id=(x, y, 1 - c), device_id_type=MESH) for j in range(4)]
        keeps = [pltpu.make_async_copy(block(2 * j + c), mine.at[j], own_sems.at[j]) for j in range(4)]
        for cp in swaps + keeps:
            cp.start()
        for j in range(4):
            keeps[j].wait()
            swaps[j].wait()
            o_ref[pl.ds(j * r, r), :] = (mine[j].astype(F32) + got[j].astype(F32)).astype(o_ref.dtype)

    return pl.pallas_call(
        body, name=name, in_specs=[ANY_SPEC], out_specs=VMEM_SPEC,
        out_shape=jax.ShapeDtypeStruct((4 * r, cols), part.dtype),
        scratch_shapes=[pltpu.VMEM((4, r, cols), part.dtype), pltpu.VMEM((4, r, cols), part.dtype),
                        pltpu.SemaphoreType.DMA((4,)), pltpu.SemaphoreType.DMA((4,)), pltpu.SemaphoreType.DMA((4,))],
        compiler_params=pltpu.CompilerParams(vmem_limit_bytes=32 << 20),
    )(part)


def _sum_partials(parts, name, blocks=N_DEV):
    n = len(parts)

    def body(*refs):
        ins, outs = refs[:n], refs[n:]
        first = pl.program_id(0) == 0
        for i_ref, o_ref in zip(ins, outs):
            v = i_ref[...].astype(F32)

            @pl.when(first)
            def _():
                o_ref[...] = v

            @pl.when(jnp.logical_not(first))
            def _():
                o_ref[...] += v

    shapes = [(p.shape[0] // blocks, p.shape[1]) for p in parts]
    return pl.pallas_call(
        body, name=name, grid=(blocks,),
        in_specs=[pl.BlockSpec(s, lambda j: (j, 0)) for s in shapes],
        out_specs=[pl.BlockSpec(s, lambda j: (0, 0)) for s in shapes],
        out_shape=[jax.ShapeDtypeStruct(s, F32) for s in shapes],
        compiler_params=_params(("arbitrary",)),
    )(*parts)


def _adamw_update(w, g, m, v):
    m = ADAM_B1 * m + (1.0 - ADAM_B1) * g
    v = ADAM_B2 * v + (1.0 - ADAM_B2) * (g * g)
    m_hat = m * (1.0 / (1.0 - ADAM_B1 ** ADAM_STEP))
    v_hat = v * (1.0 / (1.0 - ADAM_B2 ** ADAM_STEP))
    return -ADAM_LR * (m_hat / (jnp.sqrt(v_hat) + ADAM_EPS) + ADAM_WD * w), m, v


def _sum_adamw(parts, w, m, v, blocks, name):
    shape = w.shape

    def body(p_ref, w_ref, m_ref, v_ref, g_ref, d_ref, mo_ref, vo_ref):
        j = pl.program_id(0)
        part = p_ref[...].astype(F32)

        @pl.when(j == 0)
        def _():
            g_ref[...] = part

        @pl.when(j > 0)
        def _():
            g_ref[...] += part

        @pl.when(j == blocks - 1)
        def _():
            d_ref[...], mo_ref[...], vo_ref[...] = _adamw_update(w_ref[...], g_ref[...], m_ref[...], v_ref[...])

    held = pl.BlockSpec(shape, lambda j: (0, 0))
    return pl.pallas_call(
        body, name=name, grid=(blocks,),
        in_specs=[pl.BlockSpec(shape, lambda j: (j, 0)), held, held, held],
        out_specs=[held] * 4, out_shape=[jax.ShapeDtypeStruct(shape, F32)] * 4,
        compiler_params=_params(("arbitrary",)),
    )(parts, w, m, v)


def _adamw(ws, gs, ms, vs, name):
    n = len(ws)

    def body(*refs):
        w_r, g_r, m_r, v_r = refs[:n], refs[n:2 * n], refs[2 * n:3 * n], refs[3 * n:4 * n]
        d_o, m_o, v_o = refs[4 * n:5 * n], refs[5 * n:6 * n], refs[6 * n:7 * n]
        for k in range(n):
            d_o[k][...], m_o[k][...], v_o[k][...] = _adamw_update(w_r[k][...], g_r[k][...], m_r[k][...], v_r[k][...])

    shapes = [jax.ShapeDtypeStruct(w.shape, F32) for w in ws]
    outs = pl.pallas_call(
        body, name=name, in_specs=[VMEM_SPEC] * (4 * n), out_specs=[VMEM_SPEC] * (3 * n), out_shape=shapes * 3,
        compiler_params=pltpu.CompilerParams(vmem_limit_bytes=56 << 20),
    )(*ws, *gs, *ms, *vs)
    return outs[:n], outs[n:2 * n], outs[2 * n:]


WEIGHT_NAMES = ("meta_tokens", "ffn1_pre_norm", "ffn1_w_gate", "ffn1_w_up", "ffn1_w_down", "ffn1_post_norm", "mix_pre_norm",
                "w_in", "gla_w_a2", "gla_b_a", "gla_out_norm", "swa_sinks", "swa_out_norm", "w_out", "mix_post_norm",
                "ffn2_pre_norm", "ffn2_w_gate", "ffn2_w_up", "ffn2_w_down", "ffn2_post_norm")
WIN_SHARD = D_IN // N_DEV
WIN_SHARD_PAD = 304
SLAB_VECTORS = ("ffn1_pre", "ffn1_post", "mix_pre", "mix_post", "ffn2_pre", "ffn2_post")
SLAB_ROWS = 32


def kernel(x, meta_tokens, ffn1_pre_norm, ffn1_w_gate, ffn1_w_up, ffn1_w_down, ffn1_post_norm, mix_pre_norm, w_in, gla_w_a2, gla_b_a, gla_out_norm, swa_sinks, swa_out_norm, w_out, mix_post_norm, ffn2_pre_norm, ffn2_w_gate, ffn2_w_up, ffn2_w_down, ffn2_post_norm, loss_target, m_meta_tokens, m_ffn1_pre_norm, m_ffn1_w_gate, m_ffn1_w_up, m_ffn1_w_down, m_ffn1_post_norm, m_mix_pre_norm, m_w_in, m_gla_w_a2, m_gla_b_a, m_gla_out_norm, m_swa_sinks, m_swa_out_norm, m_w_out, m_mix_post_norm, m_ffn2_pre_norm, m_ffn2_w_gate, m_ffn2_w_up, m_ffn2_w_down, m_ffn2_post_norm, v_meta_tokens, v_ffn1_pre_norm, v_ffn1_w_gate, v_ffn1_w_up, v_ffn1_w_down, v_ffn1_post_norm, v_mix_pre_norm, v_w_in, v_gla_w_a2, v_gla_b_a, v_gla_out_norm, v_swa_sinks, v_swa_out_norm, v_w_out, v_mix_post_norm, v_ffn2_pre_norm, v_ffn2_w_gate, v_ffn2_w_up, v_ffn2_w_down, v_ffn2_post_norm):
    given = dict(locals())
    W = {n: given[n] for n in WEIGHT_NAMES}
    M = {n: given["m_" + n] for n in WEIGHT_NAMES}
    V = {n: given["v_" + n] for n in WEIGHT_NAMES}
    dev = _dev_index(*_place_on_mesh())

    def t16(w):
        return w[0].T.astype(BF16)

    small = jnp.concatenate([W["meta_tokens"], jnp.pad(W["gla_w_a2"][0], ((0, 0), (0, 96)))], axis=0)
    wg1, wu1, wd1, small_g, gathered_zeros = _all_gather(
        [t16(W["ffn1_w_gate"]), t16(W["ffn1_w_up"]), W["ffn1_w_down"][0].astype(BF16), small])
    def after_zero(shard, zeros):
        return shard + zeros[0:1, 0:1].astype(shard.dtype)
    win_shard = jnp.pad(t16(W["w_in"]), ((0, WIN_SHARD_PAD - WIN_SHARD), (0, 0)))
    win_shard = after_zero(win_shard, gathered_zeros)
    mid = _exchange_start([win_shard], GATHER, "gather_w_in_start")
    late_shards = [after_zero(W["w_out"][0].astype(BF16), mid[3]), t16(W["ffn2_w_gate"]), t16(W["ffn2_w_up"]),
                   W["ffn2_w_down"][0].astype(BF16)]
    late = _exchange_start(late_shards, GATHER, "gather_late_weights_start")

    def late_weights(what, after):
        if what == "win":
            win_g, = _exchange_wait(mid, GATHER, after, "gather_w_in_wait")
            win_t = win_g.reshape(N_DEV, WIN_SHARD_PAD, D_MODEL)[:, :WIN_SHARD].reshape(D_IN, D_MODEL)
            return dict(win=_win_pad_rows(win_t))
        wout, wg2, wu2, wd2 = _exchange_wait(late, GATHER, after, "gather_late_weights_wait")
        return dict(wout=wout, wg2=wg2, wu2=wu2, wd2=wd2)

    small_g = small_g.reshape(N_DEV, 32, 128)
    meta_full = small_g[:, :N_META].transpose(1, 0, 2).reshape(N_META, D_MODEL)
    wa2_full = small_g[:, N_META:, :32].transpose(1, 0, 2).reshape(16, 256)
    w = dict(
        ffn1_pre=W["ffn1_pre_norm"] + late[3][0, 0], ffn1_post=W["ffn1_post_norm"], mix_pre=W["mix_pre_norm"],
        mix_post=W["mix_post_norm"], ffn2_pre=W["ffn2_pre_norm"], ffn2_post=W["ffn2_post_norm"], b_a=W["gla_b_a"],
        gla_norm=W["gla_out_norm"], sinks=W["swa_sinks"], swa_norm=W["swa_out_norm"], wg1=wg1, wu1=wu1, wd1=wd1,
        wa2=jnp.pad(wa2_full, ((0, 112), (0, 0))))

    in_flight = []

    def on_grads(group, grads):
        parts = []
        for nm, p in grads.items():
            if nm == "win":
                p = _win_unpad_rows(p).reshape(N_DEV, WIN_SHARD, D_MODEL)
                p = jnp.pad(p, ((0, 0), (0, WIN_SHARD_PAD - WIN_SHARD), (0, 0))).reshape(N_DEV * WIN_SHARD_PAD, D_MODEL)
            parts.append(p)
        kind = SCATTER if group == "ffn2" else SCATTER_CHIPS
        if kind == SCATTER_CHIPS:
            parts = [_sibling_reduce(p, "pair_" + group + "_" + nm) for nm, p in zip(grads, parts)]
        started = _exchange_start(parts, kind, "scatter_" + group + "_start")
        in_flight.append((group, list(grads), started, kind))
        return started[3]

    small_flight = []

    def on_small(loss, dh0, g):
        packed = jnp.concatenate([g["b_a"][0:1], g["gla_norm"][0:1], g["sinks"][0:1], g["swa_norm"][0:1]], axis=1)
        slab = jnp.concatenate([g[k][0:1] for k in SLAB_VECTORS] + [packed, jnp.full((1, D_MODEL), loss, F32),
                               g["wa2"][:16].reshape(4, D_MODEL), jnp.zeros((4, D_MODEL), F32), dh0[PAD_ROWS:BLK]], axis=0)
        small_flight.append(_exchange_start([slab], GATHER, "gather_small_grads_start"))
        return small_flight[0][3]

    front = jnp.concatenate([jnp.zeros((PAD_ROWS, D_MODEL), F32), meta_full], axis=0)
    loss, dh0, g = _local_step(x[0], loss_target[0], front, w, late_weights, on_grads, on_small)
    grad_x = dh0[BLK:][None]

    land, = _exchange_wait(small_flight[0], GATHER, in_flight[-1][2][3], "gather_small_grads_wait")
    tot = _sum_partials([land], "sum_small_grads")[0]
    loss = tot[7, 0]
    small_grads = dict(
        ffn1_pre_norm=tot[0:1], ffn1_post_norm=tot[1:2], mix_pre_norm=tot[2:3], mix_post_norm=tot[3:4],
        ffn2_pre_norm=tot[4:5], ffn2_post_norm=tot[5:6], gla_b_a=tot[6:7, 0:256], gla_out_norm=tot[6:7, 256:384],
        swa_sinks=tot[6:7, 384:392], swa_out_norm=tot[6:7, 512:1024],
        gla_w_a2=lax.dynamic_slice_in_dim(tot[8:12].reshape(16, 256), dev * 32, 32, axis=1)[None],
        meta_tokens=lax.dynamic_slice_in_dim(tot[16:32], dev * 128, 128, axis=1))

    big = dict(wg1=("ffn1_w_gate", True), wu1=("ffn1_w_up", True), wd1=("ffn1_w_down", False), win=("w_in", True),
               wout=("w_out", False), wg2=("ffn2_w_gate", True), wu2=("ffn2_w_up", True), wd2=("ffn2_w_down", False))
    grads = dict(small_grads)
    delta, new_m, new_v = {}, {}, {}
    names = [n for n in WEIGHT_NAMES if n not in [full for full, _ in big.values()]]
    two_d = lambda a: a.reshape(-1, a.shape[-1])
    d_, m_, v_ = _adamw([two_d(W[n]) for n in names], [two_d(grads[n]) for n in names],
                        [two_d(M[n]) for n in names], [two_d(V[n]) for n in names], "adamw_small")
    for k, n in enumerate(names):
        delta[n], new_m[n], new_v[n] = d_[k].reshape(W[n].shape), m_[k].reshape(W[n].shape), v_[k].reshape(W[n].shape)

    before_wait = d_[0] + in_flight[-1][2][3][0, 0]
    for group, shorts, started, kind in in_flight:
        lands = _exchange_wait(started, kind, before_wait, "scatter_" + group + "_wait")
        blocks = 4 if kind == SCATTER_CHIPS else N_DEV
        for short, land in zip(shorts, lands):
            n, transposed = big[short]
            to_slab = (lambda a: a[0].T) if transposed else (lambda a: a[0])
            from_slab = (lambda a: a.T[None]) if transposed else (lambda a: a[None])
            if short == "win":
                g_slab = _sum_partials([land], "sum_" + n, blocks)[0][:WIN_SHARD]
                d_, m_, v_ = _adamw([to_slab(W[n])], [g_slab], [to_slab(M[n])], [to_slab(V[n])], "adamw_" + n)
                d_, m_, v_ = d_[0], m_[0], v_[0]
            else:
                g_slab, d_, m_, v_ = _sum_adamw(land, to_slab(W[n]), to_slab(M[n]), to_slab(V[n]), blocks, "adamw_" + n)
            grads[n], delta[n], new_m[n], new_v[n] = from_slab(g_slab), from_slab(d_), from_slab(m_), from_slab(v_)
            before_wait = d_
    return (loss, grad_x, *[grads[n] for n in WEIGHT_NAMES], *[delta[n] for n in WEIGHT_NAMES],
            *[new_m[n] for n in WEIGHT_NAMES], *[new_v[n] for n in WEIGHT_NAMES])
```

```python
import math

import jax
import jax.numpy as jnp
from jax import lax
from jax.experimental import pallas as pl
from jax.experimental.pallas import tpu as pltpu

F32, BF16 = jnp.float32, jnp.bfloat16

D_MODEL = 1024
D_FF = 2816
N_META = 16
BLK = 128
PAD_ROWS = BLK - N_META
GLA_DK = 64
SWA_HD = 64
SWA_HEADS = 8
GLA_TAU = 16.0
NORM_EPS = 1e-6
NEG_INF = -1e30
ROPE_THETA = 10000.0
P_GQ, P_GK, P_GV, P_GG, P_SQ, P_SK, P_SV, P_GA, P_END = 0, 256, 512, 1024, 1536, 2048, 2176, 2304, 2432
D_IN = 2320
IN_SPLITS = (256, 256, 512, 512, 16, 512, 128, 128)
FF_TILE = 2816
WGRAD_TILE_MAX = 2432
N_DEV = 8
MESH = pl.DeviceIdType.MESH

ADAM_LR, ADAM_B1, ADAM_B2, ADAM_EPS, ADAM_WD, ADAM_STEP = 0.001, 0.9, 0.999, 1e-08, 0.01, 10

V7X_VMEM_BYTES = 64 << 20
VMEM_SPEC = pl.BlockSpec(memory_space=pltpu.VMEM)
SMEM_SPEC = pl.BlockSpec(memory_space=pltpu.SMEM)
ANY_SPEC = pl.BlockSpec(memory_space=pl.ANY)


def _params(semantics, vmem_mb=56):
    return pltpu.CompilerParams(dimension_semantics=semantics, vmem_limit_bytes=vmem_mb << 20)


def _row_tile(rows):
    return 416 if rows % 416 == 0 else BLK


def _blocks_per_step(blocks):
    return 5 if blocks % 5 == 0 else 1


def _nn(a, b):
    return lax.dot_general(a, b, (((1,), (0,)), ((), ())), preferred_element_type=F32)


def _nt(a, b):
    return lax.dot_general(a, b, (((1,), (1,)), ((), ())), preferred_element_type=F32)


def _tn(a, b):
    return lax.dot_general(a, b, (((0,), (0,)), ((), ())), preferred_element_type=F32)


def _rms(x):
    r = lax.rsqrt(jnp.mean(x * x, axis=-1, keepdims=True) + NORM_EPS)
    return x * r, r


def _rms_bwd(xn, r, w, dy):
    g = dy * w
    return r * (g - xn * jnp.mean(g * xn, axis=-1, keepdims=True))


def _sigmoid(x):
    return 1.0 / (1.0 + jnp.exp(-x))


def _colsum(x):
    return jnp.sum(x, axis=0, keepdims=True)


def _split_bf16(x):
    hi = x.astype(BF16)
    lo = (x - hi.astype(F32)).astype(BF16)
    return hi, lo


def _tri(lower):
    r = lax.broadcasted_iota(jnp.int32, (BLK, BLK), 0)
    c = lax.broadcasted_iota(jnp.int32, (BLK, BLK), 1)
    return (r >= c) if lower else (c >= r)


def _half_mask(width, half):
    lane = lax.broadcasted_iota(jnp.int32, (1, width), 1)
    return ((lane % 128) < 64) if half == 0 else ((lane % 128) >= 64)


def _rot_half(x):
    w = x.shape[-1]
    lane = lax.broadcasted_iota(jnp.int32, (1, w), 1)
    return jnp.where((lane % SWA_HD) < SWA_HD // 2, -pltpu.roll(x, w - SWA_HD // 2, 1), pltpu.roll(x, SWA_HD // 2, 1))


def _row_spec(tm, cols):
    return pl.BlockSpec((tm, cols), lambda i: (i, 0))


def _acc_spec(cols):
    return pl.BlockSpec((8, cols), lambda i: (0, 0))


def _acc_add(ref, first, value):
    @pl.when(first)
    def _():
        ref[...] = jnp.zeros_like(ref)
    ref[0:1, :] += value


def _behind_spec(tm):
    return pl.BlockSpec((pl.Element(tm), pl.Element(D_MODEL)),
                        lambda i: (pl.multiple_of(jnp.maximum(i * tm - BLK, 0), math.gcd(tm, BLK)), 0))


def _behind_front(ref, i, tm, front):
    blk = ref[...]
    return jnp.where(i == 0, jnp.concatenate([front, blk[0:tm - BLK]], axis=0), blk)


def _ffn_fwd(h, gpre, wg_t, wu_t, wd, gpost, tgt=None, front=None, mixed=None):
    with_loss, with_front, with_mixed = tgt is not None, front is not None, mixed is not None
    rows = h.shape[0] + (BLK if with_front else 0)
    tm = _row_tile(rows)
    nf = D_FF // FF_TILE

    def body(*refs):
        refs = list(refs)
        h_ref, gpre_ref, wg_ref, wu_ref, wd_ref, gpost_ref = refs[:6]
        del refs[:6]
        front_ref = refs.pop(0) if with_front else None
        cg_ref, cs_ref, wo_ref, gm_ref = (refs.pop(0), refs.pop(0), refs.pop(0), refs.pop(0)) if with_mixed else (None,) * 4
        t_ref = refs.pop(0) if with_loss else None
        hm_ref, m_ref = (refs.pop(0), refs.pop(0)) if with_mixed else (None, None)
        ho_ref = None if with_loss else refs.pop(0)
        a_ref, b_ref, s_ref, f_ref = refs[:4]
        dy_ref, loss_ref = refs[4:6] if with_loss else (None, None)
        acc = refs[-1]
        i = pl.program_id(0)
        h_in = _behind_front(h_ref, i, tm, front_ref[...]) if with_front else h_ref[...]
        if with_mixed:
            m = _nn(cg_ref[...], wo_ref[0:512, :]) + _nn(cs_ref[...], wo_ref[512:1024, :])
            m_ref[...] = m
            mn, _ = _rms(m)
            h_in = h_in + mn * gm_ref[...]
            hm_ref[...] = h_in
        hn, _ = _rms(h_in)
        n16 = (hn * gpre_ref[...]).astype(BF16)
        for j in range(nf):
            cols = slice(j * FF_TILE, (j + 1) * FF_TILE)
            a = _nt(n16, wg_ref[cols, :])
            b = _nt(n16, wu_ref[cols, :])
            a_ref[:, cols] = a.astype(BF16)
            b_ref[:, cols] = b.astype(BF16)
            s16 = (a * _sigmoid(a) * b).astype(BF16)
            s_ref[:, cols] = s16
            part = _nn(s16, wd_ref[cols, :])
            if j == 0:
                acc[...] = part
            else:
                acc[...] += part
        f = acc[...]
        f_ref[...] = f
        fn, _ = _rms(f)
        y = h_in + 0.5 * (fn * gpost_ref[...])
        if not with_loss:
            ho_ref[...] = y
        else:
            row = i * tm + lax.broadcasted_iota(jnp.int32, (tm, 1), 0)
            err = jnp.where(row >= BLK, y - _behind_front(t_ref, i, tm, jnp.zeros((BLK, D_MODEL), F32)), 0.0)
            dy_ref[...] = err * (1.0 / D_MODEL)
            part = 0.5 * jnp.sum(jnp.sum(err * err, axis=-1, keepdims=True) * (1.0 / D_MODEL), axis=0, keepdims=True)

            @pl.when(i == 0)
            def _():
                loss_ref[...] = jnp.zeros_like(loss_ref)
            loss_ref[...] += part

    row_f32 = _row_spec(tm, D_MODEL)
    behind = _behind_spec(tm)
    in_specs = [behind if with_front else row_f32, VMEM_SPEC, VMEM_SPEC, VMEM_SPEC, VMEM_SPEC, VMEM_SPEC]
    wide, full = jax.ShapeDtypeStruct((rows, D_FF), BF16), jax.ShapeDtypeStruct((rows, D_MODEL), F32)
    out_specs = [_row_spec(tm, D_FF), _row_spec(tm, D_FF), _row_spec(tm, D_FF), row_f32]
    out_shape = [wide, wide, wide, full]
    args = [h, gpre, wg_t, wu_t, wd, gpost]
    if not with_loss:
        out_specs.insert(0, row_f32)
        out_shape.insert(0, full)
    if with_front:
        in_specs.append(VMEM_SPEC)
        args.append(front)
    if with_mixed:
        in_specs += [_row_spec(tm, 512), _row_spec(tm, 512), VMEM_SPEC, VMEM_SPEC]
        args += list(mixed)
        out_specs = [row_f32, row_f32] + out_specs
        out_shape = [full, full] + out_shape
    if with_loss:
        in_specs.append(behind)
        args.append(tgt)
        out_specs += [row_f32, pl.BlockSpec((8, 128), lambda i: (0, 0))]
        out_shape += [jax.ShapeDtypeStruct((rows, D_MODEL), F32), jax.ShapeDtypeStruct((8, 128), F32)]
    return pl.pallas_call(
        body, name="ffn_fwd_loss" if with_loss else "ffn_fwd", grid=(rows // tm,),
        in_specs=in_specs, out_specs=out_specs, out_shape=out_shape,
        scratch_shapes=[pltpu.VMEM((tm, D_MODEL), F32)],
        compiler_params=_params(("arbitrary",), vmem_mb=62 if with_mixed else 56),
    )(*args)


def _ffn_bwd_act(dh_out, h, a, b, f, gpre, gpost, wg_t, wu_t, wd, name, front=None):
    with_front = front is not None
    rows = dh_out.shape[0]
    tm = _row_tile(rows)
    nf = D_FF // FF_TILE

    def body(dho_ref, h_ref, a_ref, b_ref, f_ref, gpre_ref, gpost_ref, wg_ref, wu_ref, wd_ref, *rest):
        front_ref = rest[0] if with_front else None
        dh_ref, da_ref, db_ref, df_ref, n_ref, dgpre_ref, dgpost_ref, acc = rest[-8:]
        first = pl.program_id(0) == 0
        dho = dho_ref[...]
        drr = 0.5 * dho
        fn, rf = _rms(f_ref[...])
        _acc_add(dgpost_ref, first, _colsum(drr * fn))
        df16 = _rms_bwd(fn, rf, gpost_ref[...], drr).astype(BF16)
        df_ref[...] = df16
        h_in = _behind_front(h_ref, pl.program_id(0), tm, front_ref[...]) if with_front else h_ref[...]
        hn, rh = _rms(h_in)
        n_ref[...] = (hn * gpre_ref[...]).astype(BF16)
        for j in range(nf):
            cols = slice(j * FF_TILE, (j + 1) * FF_TILE)
            ds = _nt(df16, wd_ref[cols, :])
            av = a_ref[:, cols].astype(F32)
            bv = b_ref[:, cols].astype(F32)
            sg = _sigmoid(av)
            db16 = (ds * (av * sg)).astype(BF16)
            da16 = (ds * bv * (sg * (1.0 + av * (1.0 - sg)))).astype(BF16)
            da_ref[:, cols] = da16
            db_ref[:, cols] = db16
            part = _nn(da16, wg_ref[cols, :]) + _nn(db16, wu_ref[cols, :])
            if j == 0:
                acc[...] = part
            else:
                acc[...] += part
        dn = acc[...]
        _acc_add(dgpre_ref, first, _colsum(dn * hn))
        dh_ref[...] = dho + _rms_bwd(hn, rh, gpre_ref[...], dn)

    row_f32 = _row_spec(tm, D_MODEL)
    row_ff = _row_spec(tm, D_FF)
    return pl.pallas_call(
        body, name=name, grid=(rows // tm,),
        in_specs=[row_f32, _behind_spec(tm) if with_front else row_f32, row_ff, row_ff, row_f32,
                  VMEM_SPEC, VMEM_SPEC, VMEM_SPEC, VMEM_SPEC, VMEM_SPEC] + ([VMEM_SPEC] if with_front else []),
        out_specs=[row_f32, row_ff, row_ff, row_f32, row_f32, _acc_spec(D_MODEL), _acc_spec(D_MODEL)],
        out_shape=[jax.ShapeDtypeStruct((rows, D_MODEL), F32), jax.ShapeDtypeStruct((rows, D_FF), BF16),
                   jax.ShapeDtypeStruct((rows, D_FF), BF16), jax.ShapeDtypeStruct((rows, D_MODEL), BF16),
                   jax.ShapeDtypeStruct((rows, D_MODEL), BF16), jax.ShapeDtypeStruct((8, D_MODEL), F32),
                   jax.ShapeDtypeStruct((8, D_MODEL), F32)],
        scratch_shapes=[pltpu.VMEM((tm, D_MODEL), F32)],
        compiler_params=_params(("arbitrary",), vmem_mb=62),
    )(dh_out, h, a, b, f, gpre, gpost, wg_t, wu_t, wd, *([front] if with_front else []))


def _wgrad(lhs, rhs, name, after=None):
    rows, width = lhs.shape
    tf = 256 if width % 256 == 0 else 128
    pieces = 5 if rows % 80 == 0 else 3 if rows % 48 == 0 else 1
    piece = rows // pieces
    tiles, slots = width // tf, 3

    def body(l_hbm, r_hbm, *rest):
        o_ref, l_buf, r_all, l_sems, r_sems = rest[-5:]
        j = pl.program_id(0)

        def fetch_r(c):
            part = pl.ds(c * piece, piece)
            return pltpu.make_async_copy(r_hbm.at[part, :], r_all.at[part, :], r_sems.at[c])

        def fetch_l(tile):
            slot = tile % slots
            start = tile * tf if isinstance(tile, int) else pl.multiple_of(tile * tf, 128)
            return pltpu.make_async_copy(l_hbm.at[:, pl.ds(start, tf)], l_buf.at[slot], l_sems.at[slot])

        @pl.when(j == 0)
        def _():
            for tile in range(min(slots - 1, tiles)):
                fetch_l(tile).start()
            for c in range(pieces):
                fetch_r(c).start()

        @pl.when(j + slots - 1 < tiles)
        def _():
            fetch_l(j + slots - 1).start()
        fetch_l(j).wait()
        lhs_tile = l_buf.at[j % slots]

        @pl.when(j == 0)
        def _():
            total = None
            for c in range(pieces):
                fetch_r(c).wait()
                part = _tn(lhs_tile[c * piece:(c + 1) * piece, :], r_all[c * piece:(c + 1) * piece, :])
                total = part if total is None else total + part
            o_ref[...] = total.astype(BF16)

        @pl.when(j > 0)
        def _():
            o_ref[...] = _tn(lhs_tile[...], r_all[...]).astype(BF16)

    return pl.pallas_call(
        body, name=name, grid=(tiles,),
        in_specs=[ANY_SPEC, ANY_SPEC] + ([] if after is None else [ANY_SPEC]),
        out_specs=pl.BlockSpec((tf, D_MODEL), lambda j: (j, 0)),
        out_shape=jax.ShapeDtypeStruct((width, D_MODEL), BF16),
        scratch_shapes=[pltpu.VMEM((slots, rows, tf), BF16), pltpu.VMEM((rows, D_MODEL), BF16),
                        pltpu.SemaphoreType.DMA((slots,)), pltpu.SemaphoreType.DMA((pieces,))],
        compiler_params=_params(("arbitrary",)),
    )(lhs, rhs, *([] if after is None else [after]))


def _chunk_cumsum(x, lower):
    tri = jnp.where(_tri(lower), 1.0, 0.0).astype(BF16)
    hi, lo = _split_bf16(x)
    return _nn(tri, hi) + _nn(tri, lo)


def _mix_in(h, g, win_p, wa2_p, b_a, cos, sin):
    rows = h.shape[0]
    tm = 640 if rows % 640 == 0 else BLK

    def body(h_ref, g_ref, win_ref, wa2_ref, ba_ref, cos_ref, sin_ref,
             gq_ref, gk_ref, gv_ref, gg_ref, sq_ref, sk_ref, sv_ref, ga_ref, loga_ref, bc_ref, n_ref):
        hn, _ = _rms(h_ref[...])
        n16 = (hn * g_ref[...]).astype(BF16)
        n_ref[...] = n16
        proj = _nt(n16, win_ref[...])
        gq_ref[...] = proj[:, P_GQ:P_GK]
        gk_ref[...] = proj[:, P_GK:P_GV]
        gv_ref[...] = proj[:, P_GV:P_GG].astype(BF16)
        gg_ref[...] = proj[:, P_GG:P_SQ]
        c1, s1 = cos_ref[...], sin_ref[...]
        c4 = jnp.concatenate([c1, c1, c1, c1], axis=1)
        s4 = jnp.concatenate([s1, s1, s1, s1], axis=1)
        sq = proj[:, P_SQ:P_SK]
        sk = proj[:, P_SK:P_SV]
        sq_ref[...] = (sq * c4 + _rot_half(sq) * s4).astype(BF16)
        sk_ref[...] = (sk * c1 + _rot_half(sk) * s1).astype(BF16)
        sv_ref[...] = proj[:, P_SV:P_GA].astype(BF16)
        ga = proj[:, P_GA:P_END]
        ga_ref[...] = ga
        z = _nn(ga, wa2_ref[...]) + ba_ref[...]
        loga = (jnp.minimum(z, 0.0) - jnp.log(1.0 + jnp.exp(-jnp.abs(z)))) * (1.0 / GLA_TAU)
        loga_ref[...] = loga
        for c in range(tm // BLK):
            rs = slice(c * BLK, (c + 1) * BLK)
            bc_ref[rs, :] = _chunk_cumsum(loga[rs, :], True)

    f32 = lambda c: jax.ShapeDtypeStruct((rows, c), F32)
    b16 = lambda c: jax.ShapeDtypeStruct((rows, c), BF16)
    rs = lambda c: _row_spec(tm, c)
    return pl.pallas_call(
        body, name="mix_in", grid=(rows // tm,),
        in_specs=[rs(D_MODEL), VMEM_SPEC, VMEM_SPEC, VMEM_SPEC, VMEM_SPEC, rs(128), rs(128)],
        out_specs=[rs(256), rs(256), rs(512), rs(512), rs(512), rs(128), rs(128), rs(128), rs(256), rs(256), rs(D_MODEL)],
        out_shape=[f32(256), f32(256), b16(512), f32(512), b16(512), b16(128), b16(128), f32(128), f32(256), f32(256),
                   b16(D_MODEL)],
        compiler_params=_params(("arbitrary",)),
    )(h, g, win_p, wa2_p, b_a, cos, sin)


def _side_by_side(parts, name):
    steps, per_step = parts[0]["steps"], parts[0]["per_step"]
    assert all((p["steps"], p["per_step"]) == (steps, per_step) for p in parts)
    counts = [[len(p[key]) for p in parts] for key in ("in_specs", "out_specs", "scratch_shapes")]

    def body(*refs):
        groups, pos = [], 0
        for kind in counts:
            groups.append([])
            for n in kind:
                groups[-1].append(refs[pos:pos + n])
                pos += n
        programs = [p["program"](*groups[0][k], *groups[1][k], *groups[2][k]) for k, p in enumerate(parts)]

        def blocks(c, carries):
            return tuple(block(c, carry) for (block, _, _), carry in zip(programs, carries))
        carries = lax.fori_loop(0, per_step, blocks, tuple(first for _, first, _ in programs))
        for (_, _, finish), carry in zip(programs, carries):
            finish(carry)

    outs = pl.pallas_call(
        body, name=name, grid=(steps,),
        in_specs=[s for p in parts for s in p["in_specs"]], out_specs=[s for p in parts for s in p["out_specs"]],
        out_shape=[s for p in parts for s in p["out_shape"]],
        scratch_shapes=[s for p in parts for s in p["scratch_shapes"]],
        compiler_params=_params(("arbitrary",)),
    )(*[a for p in parts for a in p["args"]])
    split, pos = [], 0
    for n in counts[1]:
        split.append(outs[pos:pos + n])
        pos += n
    return split


def _gla_factors(q, k, bc):
    bm = bc[BLK // 2 - 1:BLK // 2, :]
    bl = bc[BLK - 1:BLK, :]
    e_q, e_k, e_qe, e_kd = jnp.exp(bc - bm), jnp.exp(bm - bc), jnp.exp(bc), jnp.exp(bl - bc)
    return (q * e_q, k * e_k, q * e_qe, k * e_kd), (e_q, e_k, e_qe, e_kd), jnp.exp(bl)


def _gla_fwd(gq, gk, gv, gg, bc, wgn):
    rows = gq.shape[0]
    nc = rows // BLK
    per_step = _blocks_per_step(nc)
    scale = GLA_DK ** -0.5

    def body(q_ref, k_ref, v_ref, gg_ref, bc_ref, wgn_ref, o_ref, cat_ref, sp_ref, st):
        @pl.when(pl.program_id(0) == 0)
        def _():
            st[...] = jnp.zeros_like(st)
        low = _tri(True)
        wgn_v = wgn_ref[...]

        def chunk(c, carry):
            rr = pl.ds(pl.multiple_of(c * BLK, BLK), BLK)
            for p in range(2):
                sl = slice(128 * p, 128 * p + 128)
                (qt, kt, qe, kd), _, ebl = _gla_factors(q_ref[rr, sl] * scale, k_ref[rr, sl], bc_ref[rr, sl])
                s_prev = st[p]
                sp_ref[c, p] = s_prev
                s16 = s_prev.astype(BF16)
                qt16 = qt.astype(BF16)
                s_new = s_prev * ebl
                for hh in range(2):
                    hs = slice(128 * (2 * p + hh), 128 * (2 * p + hh) + 128)
                    lm = _half_mask(128, hh)
                    vh = v_ref[rr, hs]
                    pm = jnp.where(low, _nt(qt16, jnp.where(lm, kt, 0.0).astype(BF16)), 0.0)
                    o = _nn(pm.astype(BF16), vh) + _nt(jnp.where(lm, qe, 0.0).astype(BF16), s16)
                    s_new = s_new + _tn(vh, jnp.where(lm, kd, 0.0).astype(BF16))
                    o_ref[rr, hs] = o
                    on, _ = _rms(o)
                    gate = gg_ref[rr, hs]
                    cat_ref[rr, hs] = (on * wgn_v * (gate * _sigmoid(gate))).astype(BF16)
                st[p] = s_new
            return carry
        return chunk, 0, lambda carry: None

    rs = lambda c: _row_spec(per_step * BLK, c)
    return dict(
        program=body, steps=nc // per_step, per_step=per_step,
        in_specs=[rs(256), rs(256), rs(512), rs(512), rs(256), VMEM_SPEC],
        out_specs=[rs(512), rs(512), pl.BlockSpec((per_step, 2, 128, 128), lambda i: (i, 0, 0, 0))],
        out_shape=[jax.ShapeDtypeStruct((rows, 512), F32), jax.ShapeDtypeStruct((rows, 512), BF16),
                   jax.ShapeDtypeStruct((nc, 2, 128, 128), F32)],
        scratch_shapes=[pltpu.VMEM((2, 128, 128), F32)],
        args=(gq, gk, gv, gg, bc, wgn))


def _gla_bwd(dcat, o_all, gq, gk, gv, gg, bc, sp, wgn):
    rows = gq.shape[0]
    nc = rows // BLK
    per_step = _blocks_per_step(nc)
    steps = nc // per_step
    scale = GLA_DK ** -0.5

    def body(dc_ref, o_ref, q_ref, k_ref, v_ref, gg_ref, bc_ref, sp_ref, wgn_ref,
             dq_ref, dk_ref, dv_ref, dgg_ref, dla_ref, dwgn_ref, dst):
        first = pl.program_id(0) == 0

        @pl.when(first)
        def _():
            dst[...] = jnp.zeros_like(dst)
        low, upp = _tri(True), _tri(False)
        last_row = lax.broadcasted_iota(jnp.int32, (BLK, 1), 0) == BLK - 1
        wgn_v = wgn_ref[...]

        def chunk(c, dwgn):
            rr = pl.ds(pl.multiple_of((per_step - 1 - c) * BLK, BLK), BLK)
            for p in range(2):
                sl = slice(128 * p, 128 * p + 128)
                (qt, kt, qe, kd), (e_q, e_k, e_qe, e_kd), ebl = _gla_factors(
                    q_ref[rr, sl] * scale, k_ref[rr, sl], bc_ref[rr, sl])
                s_prev = sp_ref[per_step - 1 - c, p]
                s16 = s_prev.astype(BF16)
                ds_next = dst[p]
                ds16 = ds_next.astype(BF16)
                qt16 = qt.astype(BF16)
                ds_new = ds_next * ebl
                dqt = jnp.zeros((BLK, 128), F32)
                dkt = jnp.zeros((BLK, 128), F32)
                dqe = jnp.zeros((BLK, 128), F32)
                dkd = jnp.zeros((BLK, 128), F32)
                for hh in range(2):
                    hs = slice(128 * (2 * p + hh), 128 * (2 * p + hh) + 128)
                    lm = _half_mask(128, hh)
                    on, ro = _rms(o_ref[rr, hs])
                    gate = gg_ref[rr, hs]
                    sg = _sigmoid(gate)
                    si = gate * sg
                    dog = dc_ref[rr, hs]
                    dwgn = dwgn + _colsum(dog * si * on)
                    dgg_ref[rr, hs] = dog * (on * wgn_v) * (sg * (1.0 + gate * (1.0 - sg)))
                    do16 = _rms_bwd(on, ro, wgn_v, dog * si).astype(BF16)
                    vh = v_ref[rr, hs]
                    ktm16 = jnp.where(lm, kt, 0.0).astype(BF16)
                    qtm16 = jnp.where(lm, qt, 0.0).astype(BF16)
                    qem16 = jnp.where(lm, qe, 0.0).astype(BF16)
                    kdm16 = jnp.where(lm, kd, 0.0).astype(BF16)
                    p_t = jnp.where(upp, _nt(ktm16, qt16), 0.0)
                    dp_t = jnp.where(upp, _nt(vh, do16), 0.0)
                    dp = jnp.where(low, _nt(do16, vh), 0.0)
                    dv_ref[rr, hs] = _nn(p_t.astype(BF16), do16) + _nt(kdm16, ds16)
                    dqt = dqt + _nn(dp.astype(BF16), ktm16)
                    dkt = dkt + _nn(dp_t.astype(BF16), qtm16)
                    dqe = dqe + jnp.where(lm, _nn(do16, s16), 0.0)
                    dkd = dkd + jnp.where(lm, _nn(vh, ds16), 0.0)
                    ds_new = ds_new + _tn(do16, qem16)
                debl = _colsum(ds_next * s_prev)
                dq_ref[rr, sl] = (dqt * e_q + dqe * e_qe) * scale
                dk_ref[rr, sl] = dkt * e_k + dkd * e_kd
                dkd_kd = dkd * kd
                db = dqt * qt - dkt * kt + dqe * qe - dkd_kd
                db = jnp.where(last_row, db + (_colsum(dkd_kd) + debl * ebl), db)
                dla_ref[rr, sl] = _chunk_cumsum(db, False)
                dst[p] = ds_new
            return dwgn

        def finish(dwgn):
            _acc_add(dwgn_ref, first, dwgn)
        return chunk, jnp.zeros((1, 128), F32), finish

    rev = lambda c: pl.BlockSpec((per_step * BLK, c), lambda i: (steps - 1 - i, 0))
    f32 = lambda c: jax.ShapeDtypeStruct((rows, c), F32)
    return dict(
        program=body, steps=steps, per_step=per_step,
        in_specs=[rev(512), rev(512), rev(256), rev(256), rev(512), rev(512), rev(256),
                  pl.BlockSpec((per_step, 2, 128, 128), lambda i: (steps - 1 - i, 0, 0, 0)), VMEM_SPEC],
        out_specs=[rev(256), rev(256), rev(512), rev(512), rev(256), _acc_spec(128)],
        out_shape=[f32(256), f32(256), f32(512), f32(512), f32(256), jax.ShapeDtypeStruct((8, 128), F32)],
        scratch_shapes=[pltpu.VMEM((2, 128, 128), F32)],
        args=(dcat, o_all, gq, gk, gv, gg, bc, sp, wgn))


def _swa_masks(i):
    t = lax.broadcasted_iota(jnp.int32, (BLK, BLK), 0)
    c = lax.broadcasted_iota(jnp.int32, (BLK, BLK), 1)
    own_side = c <= t
    band_ok = i >= jnp.where(own_side, 1, 2)
    meta_ok = (c % N_META) <= jnp.where(i >= 1, N_META, t - PAD_ROWS)
    return own_side, band_ok, meta_ok, c // N_META


def _swa_blocks(ref, i):
    prev = pl.multiple_of(jnp.maximum(i - 1, 0) * BLK, BLK)
    own = pl.multiple_of(i * BLK, BLK)
    return jnp.concatenate([ref[pl.ds(prev, BLK), :], ref[pl.ds(own, BLK), :]], axis=0), prev, own


def _swa_meta_operand(ref):
    blk = ref[0:BLK, :]
    swapped = pltpu.roll(blk, 64, 1)
    lo = jnp.where(_half_mask(128, 0), blk, swapped)
    hi = jnp.where(_half_mask(128, 1), blk, swapped)
    meta = jnp.concatenate([lo, lo, hi, hi], axis=1)[PAD_ROWS:BLK, :]
    tiled = jnp.concatenate([meta] * SWA_HEADS, axis=0)
    j = lax.broadcasted_iota(jnp.int32, tiled.shape, 0)
    lane = lax.broadcasted_iota(jnp.int32, tiled.shape, 1)
    return jnp.where(j // N_META == lane // SWA_HD, tiled, jnp.zeros_like(tiled))


def _swa_meta_fold(acc):
    out = jnp.zeros((N_META, 128), F32)
    for hd in range(SWA_HEADS):
        half, kv = hd % 2, hd // 4
        piece = acc[N_META * hd:N_META * (hd + 1), 128 * (hd // 2):128 * (hd // 2) + 128]
        piece = jnp.where(_half_mask(128, half), piece, 0.0)
        out = out + (piece if half == kv else pltpu.roll(piece, 64, 1))
    return out


def _by_head(group, per_head):
    out = jnp.zeros((BLK, BLK), F32)
    for hd, v in enumerate(per_head):
        out = jnp.where(group == hd, v, out)
    return out


def _place(x, kv):
    if kv == 0:
        lo = jnp.where(_half_mask(128, 0), x, jnp.zeros_like(x))
        return lo, pltpu.roll(lo, 64, 1)
    hi = jnp.where(_half_mask(128, 1), x, jnp.zeros_like(x))
    return pltpu.roll(hi, 64, 1), hi


def _swa_fwd(sq, sk, sv, sinks, wn):
    rows = sq.shape[0]
    nb = rows // BLK
    per_step = _blocks_per_step(nb)
    scale = SWA_HD ** -0.5

    def body(q_ref, k_ref, v_ref, sink_ref, wn_ref, o_ref, cat_ref, lse_ref, kp, vp):
        step = pl.program_id(0)

        @pl.when(step == 0)
        def _():
            kp[...] = _swa_meta_operand(k_ref)
            vp[...] = _swa_meta_operand(v_ref)

        def one_block(c, carry):
            i = step * per_step + c
            rr = pl.ds(pl.multiple_of(c * BLK, BLK), BLK)
            own_side, band_ok, meta_ok, group = _swa_masks(i)
            k2, _, _ = _swa_blocks(k_ref, i)
            v2, _, _ = _swa_blocks(v_ref, i)
            kz = (_place(k2, 0), _place(k2, 1))
            vz = (_place(v2, 0), _place(v2, 1))
            q_all = q_ref[rr, :]
            s_meta = jnp.where(meta_ok, _nt(q_all, kp[...]) * scale, NEG_INF)
            s_band, m = [], []
            for hd in range(SWA_HEADS):
                kv, half = hd // 4, hd % 2
                q_pair = q_all[:, 128 * (hd // 2):128 * (hd // 2) + 128]
                s2 = _nt(q_pair, kz[kv][half])
                s = jnp.where(band_ok, jnp.where(own_side, s2[:, BLK:], s2[:, :BLK]) * scale, NEG_INF)
                top = jnp.maximum(jnp.max(s, axis=-1, keepdims=True),
                                  jnp.max(jnp.where(group == hd, s_meta, NEG_INF), axis=-1, keepdims=True))
                s_band.append(s)
                m.append(jnp.maximum(top, sink_ref[0, hd]))
            e_meta = jnp.exp(s_meta - _by_head(group, m))
            o_meta = _nn(e_meta.astype(BF16), vp[...])
            outs = []
            for pr in range(4):
                o_pair = o_meta[:, 128 * pr:128 * pr + 128]
                rden = []
                for half in range(2):
                    hd = 2 * pr + half
                    kv = hd // 4
                    e = jnp.exp(s_band[hd] - m[hd])
                    den = (jnp.sum(e, axis=-1, keepdims=True)
                           + jnp.sum(jnp.where(group == hd, e_meta, 0.0), axis=-1, keepdims=True)
                           + jnp.exp(sink_ref[0, hd] - m[hd]))
                    lse_ref[rr, hd:hd + 1] = m[hd] + jnp.log(den)
                    rden.append(1.0 / den)
                    e2 = jnp.concatenate([jnp.where(own_side, 0.0, e), jnp.where(own_side, e, 0.0)], axis=1).astype(BF16)
                    o_pair = o_pair + _nn(e2, vz[kv][half])
                outs.append(o_pair * jnp.where(_half_mask(128, 0), rden[0], rden[1]))
            o = jnp.concatenate(outs, axis=1)
            o_ref[rr, :] = o
            on, _ = _rms(o)
            cat_ref[rr, :] = (on * wn_ref[...]).astype(BF16)
            return carry
        return one_block, 0, lambda carry: None

    return dict(
        program=body, steps=nb // per_step, per_step=per_step,
        in_specs=[_row_spec(per_step * BLK, 512), VMEM_SPEC, VMEM_SPEC, SMEM_SPEC, VMEM_SPEC],
        out_specs=[_row_spec(per_step * BLK, 512), _row_spec(per_step * BLK, 512), _row_spec(per_step * BLK, SWA_HEADS)],
        out_shape=[jax.ShapeDtypeStruct((rows, 512), F32), jax.ShapeDtypeStruct((rows, 512), BF16),
                   jax.ShapeDtypeStruct((rows, SWA_HEADS), F32)],
        scratch_shapes=[pltpu.VMEM((BLK, 512), BF16), pltpu.VMEM((BLK, 512), BF16)],
        args=(sq, sk, sv, sinks, wn))


def _swa_bwd(dcat, o_all, sq, sk, sv, lse, sinks, wn):
    rows = sq.shape[0]
    nb = rows // BLK
    per_step = _blocks_per_step(nb)
    steps = nb // per_step
    scale = SWA_HD ** -0.5

    def body(dc_ref, o_ref, q_ref, k_ref, v_ref, lse_ref, sink_ref, wn_ref, dq_ref, dk_ref, dv_ref, dsink_ref, dwn_ref,
             kp, vp, dkp, dvp):
        step = pl.program_id(0)

        @pl.when(step == 0)
        def _():
            dk_ref[...] = jnp.zeros_like(dk_ref)
            dv_ref[...] = jnp.zeros_like(dv_ref)
            dkp[...] = jnp.zeros_like(dkp)
            dvp[...] = jnp.zeros_like(dvp)
            kp[...] = _swa_meta_operand(k_ref)
            vp[...] = _swa_meta_operand(v_ref)

        def one_block(c, carry):
            i = step * per_step + c
            rr = pl.ds(pl.multiple_of(c * BLK, BLK), BLK)
            first = i == 0
            own_side, band_ok, meta_ok, group = _swa_masks(i)
            k2, prev, own = _swa_blocks(k_ref, i)
            v2, _, _ = _swa_blocks(v_ref, i)
            kz = (_place(k2, 0), _place(k2, 1))
            vz = (_place(v2, 0), _place(v2, 1))
            o = o_ref[rr, :]
            on, ro = _rms(o)
            dc = dc_ref[rr, :]
            _acc_add(dwn_ref, first, _colsum(dc * on))
            do = _rms_bwd(on, ro, wn_ref[...], dc)
            do_o = do * o
            do16 = do.astype(BF16)
            q_all = q_ref[rr, :]
            lse = [lse_ref[rr, hd:hd + 1] for hd in range(SWA_HEADS)]
            delta = [jnp.sum(jnp.where(_half_mask(128, hd % 2), do_o[:, 128 * (hd // 2):128 * (hd // 2) + 128], 0.0),
                             axis=-1, keepdims=True) for hd in range(SWA_HEADS)]
            s_meta = jnp.where(meta_ok, _nt(q_all, kp[...]) * scale, NEG_INF)
            p_meta = jnp.exp(s_meta - _by_head(group, lse))
            ds_meta16 = (p_meta * (_nt(do16, vp[...]) - _by_head(group, delta)) * scale).astype(BF16)
            dq_meta = _nn(ds_meta16, kp[...])
            dkp[...] += _tn(ds_meta16, q_all)
            dvp[...] += _tn(p_meta.astype(BF16), do16)
            own2 = jnp.concatenate([own_side.astype(jnp.int32)] * 2, axis=0) > 0
            ok2 = jnp.concatenate([band_ok.astype(jnp.int32)] * 2, axis=0) > 0

            def window(x2):
                return jnp.where(own2, x2[:, BLK:], x2[:, :BLK])

            def unwindow(x):
                return jnp.concatenate([jnp.where(own2, 0.0, x), jnp.where(own2, x, 0.0)], axis=1).astype(BF16)
            lane8 = lax.broadcasted_iota(jnp.int32, (1, 128), 1)
            dsink = jnp.zeros((1, 128), F32)
            dq_pairs = [dq_meta[:, 128 * pr:128 * pr + 128] for pr in range(4)]
            dk2 = [[None, None], [None, None]]
            dv2 = [[None, None], [None, None]]
            for kv in range(2):
                for half in range(2):
                    heads, pairs = (4 * kv + half, 4 * kv + 2 + half), (2 * kv, 2 * kv + 1)
                    q_s = jnp.concatenate([q_all[:, 128 * pr:128 * pr + 128] for pr in pairs], axis=0)
                    do_s = jnp.concatenate([do16[:, 128 * pr:128 * pr + 128] for pr in pairs], axis=0)
                    lse_s = jnp.concatenate([lse[hd] for hd in heads], axis=0)
                    delta_s = jnp.concatenate([delta[hd] for hd in heads], axis=0)
                    s = jnp.where(ok2, window(_nt(q_s, kz[kv][half])) * scale, NEG_INF)
                    prob = jnp.exp(s - lse_s)
                    for hd in heads:
                        dsink = dsink + jnp.where(lane8 == hd, -jnp.sum(jnp.exp(sink_ref[0, hd] - lse[hd]) * delta[hd]), 0.0)
                    ds2 = unwindow(prob * (window(_nt(do_s, vz[kv][half])) - delta_s) * scale)
                    dq_s = _nn(ds2, kz[kv][half])
                    dq_pairs[pairs[0]] = dq_pairs[pairs[0]] + dq_s[:BLK]
                    dq_pairs[pairs[1]] = dq_pairs[pairs[1]] + dq_s[BLK:]
                    dk2[kv][half] = _tn(ds2, q_s)
                    dv2[kv][half] = _tn(unwindow(prob), do_s)
            dq_ref[rr, :] = jnp.concatenate(dq_pairs, axis=1)
            _acc_add(dsink_ref, first, dsink)
            for ref, acc2 in ((dk_ref, dk2), (dv_ref, dv2)):
                tot = jnp.zeros((2 * BLK, 128), F32)
                for kv in range(2):
                    for half in range(2):
                        part = jnp.where(_half_mask(128, half), acc2[kv][half], 0.0)
                        tot = tot + (part if half == kv else pltpu.roll(part, 64, 1))
                ref[pl.ds(prev, BLK), :] += tot[:BLK]
                ref[pl.ds(own, BLK), :] += tot[BLK:]
            return carry

        def finish(carry):
            del carry

            @pl.when(step == steps - 1)
            def _():
                dk_ref[PAD_ROWS:BLK, :] += _swa_meta_fold(dkp[...])
                dv_ref[PAD_ROWS:BLK, :] += _swa_meta_fold(dvp[...])
        return one_block, jnp.zeros((1, 128), F32), finish

    full = pl.BlockSpec((rows, 128), lambda i: (0, 0))
    blocks = lambda cols: _row_spec(per_step * BLK, cols)
    return dict(
        program=body, steps=steps, per_step=per_step,
        in_specs=[blocks(512), blocks(512), blocks(512), VMEM_SPEC, VMEM_SPEC, blocks(SWA_HEADS), SMEM_SPEC, VMEM_SPEC],
        out_specs=[blocks(512), full, full, _acc_spec(128), _acc_spec(512)],
        out_shape=[jax.ShapeDtypeStruct((rows, 512), F32), jax.ShapeDtypeStruct((rows, 128), F32),
                   jax.ShapeDtypeStruct((rows, 128), F32), jax.ShapeDtypeStruct((8, 128), F32),
                   jax.ShapeDtypeStruct((8, 512), F32)],
        scratch_shapes=[pltpu.VMEM((BLK, 512), BF16), pltpu.VMEM((BLK, 512), BF16),
                        pltpu.VMEM((BLK, 512), F32), pltpu.VMEM((BLK, 512), F32)],
        args=(dcat, o_all, sq, sk, sv, lse, sinks, wn))


def _mix_out_bwd(dh, m, wout, gpost):
    rows = dh.shape[0]
    tm = _row_tile(rows)

    def body(dh_ref, m_ref, w_ref, g_ref, dcg_ref, dcs_ref, dm_ref, dg_ref):
        first = pl.program_id(0) == 0
        dhv = dh_ref[...]
        mn, rm = _rms(m_ref[...])
        _acc_add(dg_ref, first, _colsum(dhv * mn))
        dm16 = _rms_bwd(mn, rm, g_ref[...], dhv).astype(BF16)
        dm_ref[...] = dm16
        dcat = _nt(dm16, w_ref[...])
        dcg_ref[...] = dcat[:, 0:512]
        dcs_ref[...] = dcat[:, 512:1024]

    row_f32 = _row_spec(tm, D_MODEL)
    return pl.pallas_call(
        body, name="mix_out_bwd", grid=(rows // tm,),
        in_specs=[row_f32, row_f32, VMEM_SPEC, VMEM_SPEC],
        out_specs=[_row_spec(tm, 512), _row_spec(tm, 512), row_f32, _acc_spec(D_MODEL)],
        out_shape=[jax.ShapeDtypeStruct((rows, 512), F32), jax.ShapeDtypeStruct((rows, 512), F32),
                   jax.ShapeDtypeStruct((rows, D_MODEL), BF16), jax.ShapeDtypeStruct((8, D_MODEL), F32)],
        compiler_params=_params(("arbitrary",)),
    )(dh, m, wout, gpost)


def _mix_in_bwd(dh_out, h, g, win_p, wa2_p, cos, sin, loga, ga, dgq, dgk, dgv, dgg, dsq, dsk, dsv, dloga):
    rows = h.shape[0]
    tm = _row_tile(rows)

    def body(dho_ref, h_ref, g_ref, win_ref, wa2_ref, cos_ref, sin_ref, loga_ref, ga_ref,
             dgq_ref, dgk_ref, dgv_ref, dgg_ref, dsq_ref, dsk_ref, dsv_ref, dla_ref,
             dh_ref, dproj_ref, dwa2_ref, dg_ref, dba_ref):
        first = pl.program_id(0) == 0
        dz = dla_ref[...] * (1.0 / GLA_TAU) * (1.0 - jnp.exp(GLA_TAU * loga_ref[...]))
        _acc_add(dba_ref, first, _colsum(dz))
        dga = _nt(dz, wa2_ref[...])
        pa = _tn(ga_ref[...], dz)
        c1, s1 = cos_ref[...], sin_ref[...]
        c4 = jnp.concatenate([c1, c1, c1, c1], axis=1)
        s4 = jnp.concatenate([s1, s1, s1, s1], axis=1)
        dq_r, dk_r = dsq_ref[...], dsk_ref[...]
        dsq = dq_r * c4 - _rot_half(dq_r * s4)
        dsk = dk_r * c1 - _rot_half(dk_r * s1)
        dproj16 = jnp.concatenate(
            [dgq_ref[...], dgk_ref[...], dgv_ref[...], dgg_ref[...], dsq, dsk, dsv_ref[...], dga], axis=1).astype(BF16)
        dproj_ref[...] = dproj16
        dn = _nn(dproj16, win_ref[...])

        @pl.when(first)
        def _():
            dwa2_ref[...] = pa

        @pl.when(jnp.logical_not(first))
        def _():
            dwa2_ref[...] += pa
        hn, rh = _rms(h_ref[...])
        _acc_add(dg_ref, first, _colsum(dn * hn))
        dh_ref[...] = dho_ref[...] + _rms_bwd(hn, rh, g_ref[...], dn)

    rs = lambda c: _row_spec(tm, c)
    return pl.pallas_call(
        body, name="mix_in_bwd", grid=(rows // tm,),
        in_specs=[rs(D_MODEL), rs(D_MODEL), VMEM_SPEC, VMEM_SPEC, VMEM_SPEC, rs(128), rs(128), rs(256), rs(128),
                  rs(256), rs(256), rs(512), rs(512), rs(512), rs(128), rs(128), rs(256)],
        out_specs=[rs(D_MODEL), rs(P_END), pl.BlockSpec((128, 256), lambda i: (0, 0)), _acc_spec(D_MODEL), _acc_spec(256)],
        out_shape=[jax.ShapeDtypeStruct((rows, D_MODEL), F32), jax.ShapeDtypeStruct((rows, P_END), BF16),
                   jax.ShapeDtypeStruct((128, 256), F32), jax.ShapeDtypeStruct((8, D_MODEL), F32),
                   jax.ShapeDtypeStruct((8, 256), F32)],
        compiler_params=_params(("arbitrary",)),
    )(dh_out, h, g, win_p, wa2_p, cos, sin, loga, ga, dgq, dgk, dgv, dgg, dsq, dsk, dsv, dloga)


def _rope_tables(rows):
    pos = (jnp.arange(rows, dtype=jnp.int32) - PAD_ROWS).astype(F32)
    inv_freq = 1.0 / (ROPE_THETA ** (jnp.arange(0, SWA_HD, 2, dtype=F32) / SWA_HD))
    ang = pos[:, None] * inv_freq[None, :]
    return jnp.tile(jnp.cos(ang), (1, 4)), jnp.tile(jnp.sin(ang), (1, 4))


def _local_step(x, tgt, front, w, late_weights=None, on_grads=None, on_small=None):
    cos, sin = _rope_tables(x.shape[0] + BLK)
    g = {}

    def tell(group, names):
        for nm in names:
            g[nm] = grads_now[nm]
        return None if on_grads is None else on_grads(group, {nm: grads_now[nm] for nm in names})

    h1, a1, b1, s1, f1 = _ffn_fwd(x, w["ffn1_pre"], w["wg1"], w["wu1"], w["wd1"], w["ffn1_post"], front=front)
    if late_weights is not None:
        w = {**w, **late_weights("win", f1)}
    gq, gk, gv, gg, sq, sk, sv, ga, loga, bc, n2 = _mix_in(h1, w["mix_pre"], w["win"], w["wa2"], w["b_a"], cos, sin)
    (o_g, cat_g, sp), (o_s, cat_s, lse) = _side_by_side(
        [_gla_fwd(gq, gk, gv, gg, bc, w["gla_norm"]), _swa_fwd(sq, sk, sv, w["sinks"], w["swa_norm"])], "attention_fwd")
    if late_weights is not None:
        w = {**w, **late_weights("rest", lse)}
    h2, m, a2, b2, s2, f2, dy, loss = _ffn_fwd(h1, w["ffn2_pre"], w["wg2"], w["wu2"], w["wd2"], w["ffn2_post"], tgt,
                                               mixed=(cat_g, cat_s, w["wout"], w["mix_post"]))
    dh2, da, db, df, n3, g["ffn2_pre"], g["ffn2_post"] = _ffn_bwd_act(
        dy, h2, a2, b2, f2, w["ffn2_pre"], w["ffn2_post"], w["wg2"], w["wu2"], w["wd2"], "ffn2_bwd_act")
    grads_now = dict(wd2=_wgrad(s2, df, "ffn2_wgrad_down"), wg2=_wgrad(da, n3, "ffn2_wgrad_gate"),
                     wu2=_wgrad(db, n3, "ffn2_wgrad_up"))
    tok = tell("ffn2", ("wd2", "wg2", "wu2"))
    dcg, dcs, dm, g["mix_post"] = _mix_out_bwd(dh2, m, w["wout"], w["mix_post"] + (0.0 if tok is None else tok[0, 0]))
    (dgq, dgk, dgv, dgg, dloga, g["gla_norm"]), (dsq, dsk, dsv, g["sinks"], g["swa_norm"]) = _side_by_side(
        [_gla_bwd(dcg, o_g, gq, gk, gv, gg, bc, sp, w["gla_norm"]),
         _swa_bwd(dcs, o_s, sq, sk, sv, lse, w["sinks"], w["swa_norm"])], "attention_bwd")
    dh1, dproj, g["wa2"], g["mix_pre"], g["b_a"] = _mix_in_bwd(
        dh2, h1, w["mix_pre"], w["win"], w["wa2"], cos, sin, loga, ga, dgq, dgk, dgv, dgg, dsq, dsk, dsv, dloga)
    dh0, da, db, df, n1, g["ffn1_pre"], g["ffn1_post"] = _ffn_bwd_act(
        dh1, x, a1, b1, f1, w["ffn1_pre"], w["ffn1_post"], w["wg1"], w["wu1"], w["wd1"], "ffn1_bwd_act", front=front)
    tok = None if on_small is None else on_small(loss[0, 0], dh0, g)
    grads_now = dict(wd1=_wgrad(s1, df, "ffn1_wgrad_down", after=tok))
    tok = tell("ffn1_down", ("wd1",))
    grads_now = dict(wg1=_wgrad(da, n1, "ffn1_wgrad_gate", after=tok))
    tok = tell("ffn1_gate", ("wg1",))
    grads_now = dict(wu1=_wgrad(db, n1, "ffn1_wgrad_up", after=tok))
    tok = tell("ffn1_up", ("wu1",))
    grads_now = dict(win=_wgrad(dproj, n2, "win_wgrad", after=tok),
                     wout=jnp.concatenate([_wgrad(cat_g, dm, "wout_wgrad_gla", after=tok),
                                           _wgrad(cat_s, dm, "wout_wgrad_swa", after=tok)], axis=0))
    tell("mix", ("wout", "win"))
    return loss[0, 0], dh0, g


def _win_pad_rows(win_t):
    pad = jnp.zeros((P_END - P_GA - 16, win_t.shape[1]), win_t.dtype)
    return jnp.concatenate([win_t[0:1536], win_t[1552:2320], win_t[1536:1552], pad], axis=0)


def _win_unpad_rows(win_p):
    return jnp.concatenate([win_p[0:1536], win_p[P_GA:P_GA + 16], win_p[1536:P_GA]], axis=0)


def _place_on_mesh():
    return lax.axis_index("x"), lax.axis_index("y"), lax.axis_index("c")


def _dev_index(px, py, pc):
    return 4 * px + 2 * py + pc


def _other_devices(x, y, c):
    flip = lambda v, f: 1 - v if f else v
    return [(flip(x, fx), flip(y, fy), flip(c, fc)) for fx in (0, 1) for fy in (0, 1) for fc in (0, 1)][1:]


def _all_gather(shards):
    n = len(shards)

    def body(*refs):
        ins, outs = refs[:n], refs[n:2 * n]
        zeros_ref, send_sems, recv_sems, local_sems = refs[2 * n:]
        zeros_ref[...] = jnp.zeros_like(zeros_ref)
        x, y, c = _place_on_mesh()
        me, sibling = (x, y, c), (x, y, 1 - c)
        chips = [(1 - x, y), (x, 1 - y), (1 - x, 1 - y)]

        def rows(k, px, py, pc):
            r = ins[k].shape[0]
            return outs[k].at[pl.ds(pl.multiple_of(_dev_index(px, py, pc) * r, 8), r), :]

        def copy(k, slot, block, to, src=None):
            return pltpu.make_async_remote_copy(
                src_ref=rows(k, *block) if src is None else src, dst_ref=rows(k, *block),
                send_sem=send_sems.at[k, slot], recv_sem=recv_sems.at[k, slot], device_id=to, device_id_type=MESH)

        local = [pltpu.make_async_copy(ins[k], rows(k, *me), local_sems.at[k]) for k in range(n)]
        sends = []
        for k in range(n):
            local[k].start()
            sends.append(copy(k, 0, me, sibling, src=ins[k]))
            sends += [copy(k, 1 + j, me, (*chip, c), src=ins[k]) for j, chip in enumerate(chips)]
        for cp in sends:
            cp.start()
        for k in range(n):
            for j, chip in enumerate(chips):
                copy(k, 1 + j, (*chip, c), me).wait_recv()
                passed = copy(k, 4 + j, (*chip, c), sibling)
                passed.start()
                sends.append(passed)
        for k in range(n):
            copy(k, 0, sibling, me).wait_recv()
            for j, chip in enumerate(chips):
                copy(k, 4 + j, (*chip, 1 - c), me).wait_recv()
        for cp in sends:
            cp.wait_send()
        for cp in local:
            cp.wait()

    return pl.pallas_call(
        body, name="all_gather_weights",
        in_specs=[ANY_SPEC] * n, out_specs=[ANY_SPEC] * n + [VMEM_SPEC],
        out_shape=[jax.ShapeDtypeStruct((N_DEV * s.shape[0], s.shape[1]), s.dtype) for s in shards]
        + [jax.ShapeDtypeStruct((8, 128), F32)],
        scratch_shapes=[pltpu.SemaphoreType.DMA((n, 7)), pltpu.SemaphoreType.DMA((n, 7)), pltpu.SemaphoreType.DMA((n,))],
    )(*shards)


HBM_SPEC = pl.BlockSpec(memory_space=pltpu.HBM)
SEM_SPEC = pl.BlockSpec(memory_space=pltpu.SEMAPHORE)
DATAFLOW = pltpu.SideEffectType.DATAFLOW_SIDE_EFFECTING


GATHER, SCATTER, SCATTER_CHIPS = "gather", "scatter", "scatter among chips"


def _exchange_peers(kind):
    x, y, c = _place_on_mesh()
    if kind == SCATTER_CHIPS:
        peers = [(1 - x, y, c), (x, 1 - y, c), (1 - x, 1 - y, c)]
        return peers, [2 * p[0] + p[1] for p in peers], 2 * x + y, 4
    peers = _other_devices(x, y, c)
    return peers, [_dev_index(*p) for p in peers], _dev_index(x, y, c), N_DEV


def _exchange_copies(srcs, lands, send_sems, recv_sems, own_sems, kind, arriving):
    peers, theirs, me, blocks = _exchange_peers(kind)
    remote, local = [], []
    for k, (src, land) in enumerate(zip(srcs, lands)):
        r = land.shape[0] // blocks

        def block(ref, d):
            return ref.at[pl.ds(pl.multiple_of(d * r, 8), r), :]

        for f, (peer, him) in enumerate(zip(peers, theirs)):
            mine, his = (him, me) if arriving else (me, him)
            sem = len(peers) * k + f
            remote.append(pltpu.make_async_remote_copy(
                src_ref=src if kind == GATHER else block(src, his), dst_ref=block(land, mine),
                send_sem=send_sems.at[sem], recv_sem=recv_sems.at[sem], device_id=peer, device_id_type=MESH))
        local.append(pltpu.make_async_copy(src if kind == GATHER else block(src, me), block(land, me), own_sems.at[k]))
    return remote, local


def _exchange_start(srcs, kind, name):
    n = len(srcs)
    lands = [lax.empty((N_DEV * s.shape[0], s.shape[1]) if kind == GATHER else s.shape, s.dtype) for s in srcs]
    sems = (3 if kind == SCATTER_CHIPS else 7) * n

    def body(*refs):
        remote, local = _exchange_copies(refs[:n], refs[n:2 * n], *refs[2 * n:2 * n + 3], kind, False)
        for cp in remote + local:
            cp.start()
        refs[-1][...] = jnp.zeros_like(refs[-1])

    both = list(srcs) + list(lands)
    outs = pl.pallas_call(
        body, name=name,
        out_shape=(pltpu.SemaphoreType.DMA((sems,)), pltpu.SemaphoreType.DMA((sems,)), pltpu.SemaphoreType.DMA((n,)),
                   *[pltpu.HBM(a.shape, a.dtype) for a in both], jax.ShapeDtypeStruct((8, 128), F32)),
        in_specs=[HBM_SPEC] * (2 * n), out_specs=(SEM_SPEC, SEM_SPEC, SEM_SPEC, *[HBM_SPEC] * (2 * n), VMEM_SPEC),
        input_output_aliases={i: 3 + i for i in range(2 * n)},
        compiler_params=pltpu.CompilerParams(has_side_effects=DATAFLOW),
    )(*[pltpu.with_memory_space_constraint(a, pltpu.HBM) for a in both])
    return outs[0:3], outs[3:3 + n], outs[3 + n:3 + 2 * n], outs[-1]


def _exchange_wait(started, kind, after, name):
    sems, srcs, lands, _ = started
    n = len(srcs)

    def body(*refs):
        args = (refs[:n], refs[n:2 * n], *refs[2 * n:2 * n + 3], kind)
        going, local = _exchange_copies(*args, False)
        for cp in going:
            cp.wait_send()
        for cp in local:
            cp.wait()
        for cp in _exchange_copies(*args, True)[0]:
            cp.wait_recv()

    both = list(srcs) + list(lands)
    outs = pl.pallas_call(
        body, name=name, out_shape=[pltpu.HBM(a.shape, a.dtype) for a in both],
        in_specs=[HBM_SPEC] * (2 * n) + [SEM_SPEC, SEM_SPEC, SEM_SPEC, ANY_SPEC], out_specs=[HBM_SPEC] * (2 * n),
        input_output_aliases={i: i for i in range(2 * n)},
        compiler_params=pltpu.CompilerParams(has_side_effects=DATAFLOW),
    )(*both, *sems, after)
    return outs[n:]


def _sibling_reduce(part, name):
    r, cols = part.shape[0] // N_DEV, part.shape[1]

    def body(p_ref, o_ref, mine, got, send_sems, recv_sems, own_sems):
        x, y, c = _place_on_mesh()

        def block(d):
            return p_ref.at[pl.ds(pl.multiple_of(d * r, 8), r), :]
        swaps = [pltpu.make_async_remote_copy(
            src_ref=block(2 * j + 1 - c), dst_ref=got.at[j], send_sem=send_sems.at[j], recv_sem=recv_sems.at[j],
            device_id=(x, y, 1 - c), device_id_type=MESH) for j in range(4)]
        keeps = [pltpu.make_async_copy(block(2 * j + c), mine.at[j], own_sems.at[j]) for j in range(4)]
        for cp in swaps + keeps:
            cp.start()
        for j in range(4):
            keeps[j].wait()
            swaps[j].wait()
            o_ref[pl.ds(j * r, r), :] = (mine[j].astype(F32) + got[j].astype(F32)).astype(o_ref.dtype)

    return pl.pallas_call(
        body, name=name, in_specs=[ANY_SPEC], out_specs=VMEM_SPEC,
        out_shape=jax.ShapeDtypeStruct((4 * r, cols), part.dtype),
        scratch_shapes=[pltpu.VMEM((4, r, cols), part.dtype), pltpu.VMEM((4, r, cols), part.dtype),
                        pltpu.SemaphoreType.DMA((4,)), pltpu.SemaphoreType.DMA((4,)), pltpu.SemaphoreType.DMA((4,))],
        compiler_params=pltpu.CompilerParams(vmem_limit_bytes=32 << 20),
    )(part)


def _sum_partials(parts, name, blocks=N_DEV):
    n = len(parts)

    def body(*refs):
        ins, outs = refs[:n], refs[n:]
        first = pl.program_id(0) == 0
        for i_ref, o_ref in zip(ins, outs):
            v = i_ref[...].astype(F32)

            @pl.when(first)
            def _():
                o_ref[...] = v

            @pl.when(jnp.logical_not(first))
            def _():
                o_ref[...] += v

    shapes = [(p.shape[0] // blocks, p.shape[1]) for p in parts]
    return pl.pallas_call(
        body, name=name, grid=(blocks,),
        in_specs=[pl.BlockSpec(s, lambda j: (j, 0)) for s in shapes],
        out_specs=[pl.BlockSpec(s, lambda j: (0, 0)) for s in shapes],
        out_shape=[jax.ShapeDtypeStruct(s, F32) for s in shapes],
        compiler_params=_params(("arbitrary",)),
    )(*parts)


def _adamw_update(w, g, m, v):
    m = ADAM_B1 * m + (1.0 - ADAM_B1) * g
    v = ADAM_B2 * v + (1.0 - ADAM_B2) * (g * g)
    m_hat = m * (1.0 / (1.0 - ADAM_B1 ** ADAM_STEP))
    v_hat = v * (1.0 / (1.0 - ADAM_B2 ** ADAM_STEP))
    return -ADAM_LR * (m_hat / (jnp.sqrt(v_hat) + ADAM_EPS) + ADAM_WD * w), m, v


def _sum_adamw(parts, w, m, v, blocks, name):
    shape = w.shape

    def body(p_ref, w_ref, m_ref, v_ref, g_ref, d_ref, mo_ref, vo_ref):
        j = pl.program_id(0)
        part = p_ref[...].astype(F32)

        @pl.when(j == 0)
        def _():
            g_ref[...] = part

        @pl.when(j > 0)
        def _():
            g_ref[...] += part

        @pl.when(j == blocks - 1)
        def _():
            d_ref[...], mo_ref[...], vo_ref[...] = _adamw_update(w_ref[...], g_ref[...], m_ref[...], v_ref[...])

    held = pl.BlockSpec(shape, lambda j: (0, 0))
    return pl.pallas_call(
        body, name=name, grid=(blocks,),
        in_specs=[pl.BlockSpec(shape, lambda j: (j, 0)), held, held, held],
        out_specs=[held] * 4, out_shape=[jax.ShapeDtypeStruct(shape, F32)] * 4,
        compiler_params=_params(("arbitrary",)),
    )(parts, w, m, v)


def _adamw(ws, gs, ms, vs, name):
    n = len(ws)

    def body(*refs):
        w_r, g_r, m_r, v_r = refs[:n], refs[n:2 * n], refs[2 * n:3 * n], refs[3 * n:4 * n]
        d_o, m_o, v_o = refs[4 * n:5 * n], refs[5 * n:6 * n], refs[6 * n:7 * n]
        for k in range(n):
            d_o[k][...], m_o[k][...], v_o[k][...] = _adamw_update(w_r[k][...], g_r[k][...], m_r[k][...], v_r[k][...])

    shapes = [jax.ShapeDtypeStruct(w.shape, F32) for w in ws]
    outs = pl.pallas_call(
        body, name=name, in_specs=[VMEM_SPEC] * (4 * n), out_specs=[VMEM_SPEC] * (3 * n), out_shape=shapes * 3,
        compiler_params=pltpu.CompilerParams(vmem_limit_bytes=56 << 20),
    )(*ws, *gs, *ms, *vs)
    return outs[:n], outs[n:2 * n], outs[2 * n:]


WEIGHT_NAMES = ("meta_tokens", "ffn1_pre_norm", "ffn1_w_gate", "ffn1_w_up", "ffn1_w_down", "ffn1_post_norm", "mix_pre_norm",
                "w_in", "gla_w_a2", "gla_b_a", "gla_out_norm", "swa_sinks", "swa_out_norm", "w_out", "mix_post_norm",
                "ffn2_pre_norm", "ffn2_w_gate", "ffn2_w_up", "ffn2_w_down", "ffn2_post_norm")
WIN_SHARD = D_IN // N_DEV
WIN_SHARD_PAD = 304
SLAB_VECTORS = ("ffn1_pre", "ffn1_post", "mix_pre", "mix_post", "ffn2_pre", "ffn2_post")
SLAB_ROWS = 32


def kernel(x, meta_tokens, ffn1_pre_norm, ffn1_w_gate, ffn1_w_up, ffn1_w_down, ffn1_post_norm, mix_pre_norm, w_in, gla_w_a2, gla_b_a, gla_out_norm, swa_sinks, swa_out_norm, w_out, mix_post_norm, ffn2_pre_norm, ffn2_w_gate, ffn2_w_up, ffn2_w_down, ffn2_post_norm, loss_target, m_meta_tokens, m_ffn1_pre_norm, m_ffn1_w_gate, m_ffn1_w_up, m_ffn1_w_down, m_ffn1_post_norm, m_mix_pre_norm, m_w_in, m_gla_w_a2, m_gla_b_a, m_gla_out_norm, m_swa_sinks, m_swa_out_norm, m_w_out, m_mix_post_norm, m_ffn2_pre_norm, m_ffn2_w_gate, m_ffn2_w_up, m_ffn2_w_down, m_ffn2_post_norm, v_meta_tokens, v_ffn1_pre_norm, v_ffn1_w_gate, v_ffn1_w_up, v_ffn1_w_down, v_ffn1_post_norm, v_mix_pre_norm, v_w_in, v_gla_w_a2, v_gla_b_a, v_gla_out_norm, v_swa_sinks, v_swa_out_norm, v_w_out, v_mix_post_norm, v_ffn2_pre_norm, v_ffn2_w_gate, v_ffn2_w_up, v_ffn2_w_down, v_ffn2_post_norm):
    given = dict(locals())
    W = {n: given[n] for n in WEIGHT_NAMES}
    M = {n: given["m_" + n] for n in WEIGHT_NAMES}
    V = {n: given["v_" + n] for n in WEIGHT_NAMES}
    dev = _dev_index(*_place_on_mesh())

    def t16(w):
        return w[0].T.astype(BF16)

    small = jnp.concatenate([W["meta_tokens"], jnp.pad(W["gla_w_a2"][0], ((0, 0), (0, 96)))], axis=0)
    wg1, wu1, wd1, small_g, gathered_zeros = _all_gather(
        [t16(W["ffn1_w_gate"]), t16(W["ffn1_w_up"]), W["ffn1_w_down"][0].astype(BF16), small])
    def after_zero(shard, zeros):
        return shard + zeros[0:1, 0:1].astype(shard.dtype)
    win_shard = jnp.pad(t16(W["w_in"]), ((0, WIN_SHARD_PAD - WIN_SHARD), (0, 0)))
    win_shard = after_zero(win_shard, gathered_zeros)
    mid = _exchange_start([win_shard], GATHER, "gather_w_in_start")
    late_shards = [after_zero(W["w_out"][0].astype(BF16), mid[3]), t16(W["ffn2_w_gate"]), t16(W["ffn2_w_up"]),
                   W["ffn2_w_down"][0].astype(BF16)]
    late = _exchange_start(late_shards, GATHER, "gather_late_weights_start")

    def late_weights(what, after):
        if what == "win":
            win_g, = _exchange_wait(mid, GATHER, after, "gather_w_in_wait")
            win_t = win_g.reshape(N_DEV, WIN_SHARD_PAD, D_MODEL)[:, :WIN_SHARD].reshape(D_IN, D_MODEL)
            return dict(win=_win_pad_rows(win_t))
        wout, wg2, wu2, wd2 = _exchange_wait(late, GATHER, after, "gather_late_weights_wait")
        return dict(wout=wout, wg2=wg2, wu2=wu2, wd2=wd2)

    small_g = small_g.reshape(N_DEV, 32, 128)
    meta_full = small_g[:, :N_META].transpose(1, 0, 2).reshape(N_META, D_MODEL)
    wa2_full = small_g[:, N_META:, :32].transpose(1, 0, 2).reshape(16, 256)
    w = dict(
        ffn1_pre=W["ffn1_pre_norm"] + late[3][0, 0], ffn1_post=W["ffn1_post_norm"], mix_pre=W["mix_pre_norm"],
        mix_post=W["mix_post_norm"], ffn2_pre=W["ffn2_pre_norm"], ffn2_post=W["ffn2_post_norm"], b_a=W["gla_b_a"],
        gla_norm=W["gla_out_norm"], sinks=W["swa_sinks"], swa_norm=W["swa_out_norm"], wg1=wg1, wu1=wu1, wd1=wd1,
        wa2=jnp.pad(wa2_full, ((0, 112), (0, 0))))

    in_flight = []

    def on_grads(group, grads):
        parts = []
        for nm, p in grads.items():
            if nm == "win":
                p = _win_unpad_rows(p).reshape(N_DEV, WIN_SHARD, D_MODEL)
                p = jnp.pad(p, ((0, 0), (0, WIN_SHARD_PAD - WIN_SHARD), (0, 0))).reshape(N_DEV * WIN_SHARD_PAD, D_MODEL)
            parts.append(p)
        kind = SCATTER if group in ("ffn2", "ffn1_down") else SCATTER_CHIPS
        if kind == SCATTER_CHIPS:
            parts = [_sibling_reduce(p, "pair_" + group + "_" + nm) for nm, p in zip(grads, parts)]
        started = _exchange_start(parts, kind, "scatter_" + group + "_start")
        in_flight.append((group, list(grads), started, kind))
        return started[3]

    small_flight = []

    def on_small(loss, dh0, g):
        packed = jnp.concatenate([g["b_a"][0:1], g["gla_norm"][0:1], g["sinks"][0:1], g["swa_norm"][0:1]], axis=1)
        slab = jnp.concatenate([g[k][0:1] for k in SLAB_VECTORS] + [packed, jnp.full((1, D_MODEL), loss, F32),
                               g["wa2"][:16].reshape(4, D_MODEL), jnp.zeros((4, D_MODEL), F32), dh0[PAD_ROWS:BLK]], axis=0)
        small_flight.append(_exchange_start([slab], GATHER, "gather_small_grads_start"))
        return small_flight[0][3]

    front = jnp.concatenate([jnp.zeros((PAD_ROWS, D_MODEL), F32), meta_full], axis=0)
    loss, dh0, g = _local_step(x[0], loss_target[0], front, w, late_weights, on_grads, on_small)
    grad_x = dh0[BLK:][None]

    land, = _exchange_wait(small_flight[0], GATHER, in_flight[-1][2][3], "gather_small_grads_wait")
    tot = _sum_partials([land], "sum_small_grads")[0]
    loss = tot[7, 0]
    small_grads = dict(
        ffn1_pre_norm=tot[0:1], ffn1_post_norm=tot[1:2], mix_pre_norm=tot[2:3], mix_post_norm=tot[3:4],
        ffn2_pre_norm=tot[4:5], ffn2_post_norm=tot[5:6], gla_b_a=tot[6:7, 0:256], gla_out_norm=tot[6:7, 256:384],
        swa_sinks=tot[6:7, 384:392], swa_out_norm=tot[6:7, 512:1024],
        gla_w_a2=lax.dynamic_slice_in_dim(tot[8:12].reshape(16, 256), dev * 32, 32, axis=1)[None],
        meta_tokens=lax.dynamic_slice_in_dim(tot[16:32], dev * 128, 128, axis=1))

    big = dict(wg1=("ffn1_w_gate", True), wu1=("ffn1_w_up", True), wd1=("ffn1_w_down", False), win=("w_in", True),
               wout=("w_out", False), wg2=("ffn2_w_gate", True), wu2=("ffn2_w_up", True), wd2=("ffn2_w_down", False))
    grads = dict(small_grads)
    delta, new_m, new_v = {}, {}, {}
    names = [n for n in WEIGHT_NAMES if n not in [full for full, _ in big.values()]]
    two_d = lambda a: a.reshape(-1, a.shape[-1])
    d_, m_, v_ = _adamw([two_d(W[n]) for n in names], [two_d(grads[n]) for n in names],
                        [two_d(M[n]) for n in names], [two_d(V[n]) for n in names], "adamw_small")
    for k, n in enumerate(names):
        delta[n], new_m[n], new_v[n] = d_[k].reshape(W[n].shape), m_[k].reshape(W[n].shape), v_[k].reshape(W[n].shape)

    before_wait = d_[0] + in_flight[-1][2][3][0, 0]
    for group, shorts, started, kind in in_flight:
        lands = _exchange_wait(started, kind, before_wait, "scatter_" + group + "_wait")
        blocks = 4 if kind == SCATTER_CHIPS else N_DEV
        for short, land in zip(shorts, lands):
            n, transposed = big[short]
            to_slab = (lambda a: a[0].T) if transposed else (lambda a: a[0])
            from_slab = (lambda a: a.T[None]) if transposed else (lambda a: a[None])
            if short == "win":
                g_slab = _sum_partials([land], "sum_" + n, blocks)[0][:WIN_SHARD]
                d_, m_, v_ = _adamw([to_slab(W[n])], [g_slab], [to_slab(M[n])], [to_slab(V[n])], "adamw_" + n)
                d_, m_, v_ = d_[0], m_[0], v_[0]
            else:
                g_slab, d_, m_, v_ = _sum_adamw(land, to_slab(W[n]), to_slab(M[n]), to_slab(V[n]), blocks, "adamw_" + n)
            grads[n], delta[n], new_m[n], new_v[n] = from_slab(g_slab), from_slab(d_), from_slab(m_), from_slab(v_)
            before_wait = d_
    return (loss, grad_x, *[grads[n] for n in WEIGHT_NAMES], *[delta[n] for n in WEIGHT_NAMES],
            *[new_m[n] for n in WEIGHT_NAMES], *[new_v[n] for n in WEIGHT_NAMES])
```

```python
import math

import jax
import jax.numpy as jnp
from jax import lax
from jax.experimental import pallas as pl
from jax.experimental.pallas import tpu as pltpu

F32, BF16 = jnp.float32, jnp.bfloat16

D_MODEL = 1024
D_FF = 2816
N_META = 16
BLK = 128
PAD_ROWS = BLK - N_META
GLA_DK = 64
SWA_HD = 64
SWA_HEADS = 8
GLA_TAU = 16.0
NORM_EPS = 1e-6
NEG_INF = -1e30
ROPE_THETA = 10000.0
P_GQ, P_GK, P_GV, P_GG, P_SQ, P_SK, P_SV, P_GA, P_END = 0, 256, 512, 1024, 1536, 2048, 2176, 2304, 2432
D_IN = 2320
IN_SPLITS = (256, 256, 512, 512, 16, 512, 128, 128)
FF_TILE = 2816
WGRAD_TILE_MAX = 2432
N_DEV = 8
MESH = pl.DeviceIdType.MESH

ADAM_LR, ADAM_B1, ADAM_B2, ADAM_EPS, ADAM_WD, ADAM_STEP = 0.001, 0.9, 0.999, 1e-08, 0.01, 10

V7X_VMEM_BYTES = 64 << 20
VMEM_SPEC = pl.BlockSpec(memory_space=pltpu.VMEM)
SMEM_SPEC = pl.BlockSpec(memory_space=pltpu.SMEM)
ANY_SPEC = pl.BlockSpec(memory_space=pl.ANY)


def _params(semantics, vmem_mb=56):
    return pltpu.CompilerParams(dimension_semantics=semantics, vmem_limit_bytes=vmem_mb << 20)


def _row_tile(rows):
    return 416 if rows % 416 == 0 else BLK


def _blocks_per_step(blocks):
    return 5 if blocks % 5 == 0 else 1


def _nn(a, b):
    return lax.dot_general(a, b, (((1,), (0,)), ((), ())), preferred_element_type=F32)


def _nt(a, b):
    return lax.dot_general(a, b, (((1,), (1,)), ((), ())), preferred_element_type=F32)


def _tn(a, b):
    return lax.dot_general(a, b, (((0,), (0,)), ((), ())), preferred_element_type=F32)


def _rms(x):
    r = lax.rsqrt(jnp.mean(x * x, axis=-1, keepdims=True) + NORM_EPS)
    return x * r, r


def _rms_bwd(xn, r, w, dy):
    g = dy * w
    return r * (g - xn * jnp.mean(g * xn, axis=-1, keepdims=True))


def _sigmoid(x):
    return 1.0 / (1.0 + jnp.exp(-x))


def _colsum(x):
    return jnp.sum(x, axis=0, keepdims=True)


def _split_bf16(x):
    hi = x.astype(BF16)
    lo = (x - hi.astype(F32)).astype(BF16)
    return hi, lo


def _tri(lower):
    r = lax.broadcasted_iota(jnp.int32, (BLK, BLK), 0)
    c = lax.broadcasted_iota(jnp.int32, (BLK, BLK), 1)
    return (r >= c) if lower else (c >= r)


def _half_mask(width, half):
    lane = lax.broadcasted_iota(jnp.int32, (1, width), 1)
    return ((lane % 128) < 64) if half == 0 else ((lane % 128) >= 64)


def _rot_half(x):
    w = x.shape[-1]
    lane = lax.broadcasted_iota(jnp.int32, (1, w), 1)
    return jnp.where((lane % SWA_HD) < SWA_HD // 2, -pltpu.roll(x, w - SWA_HD // 2, 1), pltpu.roll(x, SWA_HD // 2, 1))


def _row_spec(tm, cols):
    return pl.BlockSpec((tm, cols), lambda i: (i, 0))


def _acc_spec(cols):
    return pl.BlockSpec((8, cols), lambda i: (0, 0))


def _acc_add(ref, first, value):
    @pl.when(first)
    def _():
        ref[...] = jnp.zeros_like(ref)
    ref[0:1, :] += value


def _behind_spec(tm):
    return pl.BlockSpec((pl.Element(tm), pl.Element(D_MODEL)),
                        lambda i: (pl.multiple_of(jnp.maximum(i * tm - BLK, 0), math.gcd(tm, BLK)), 0))


def _behind_front(ref, i, tm, front):
    blk = ref[...]
    return jnp.where(i == 0, jnp.concatenate([front, blk[0:tm - BLK]], axis=0), blk)


def _ffn_fwd(h, gpre, wg_t, wu_t, wd, gpost, tgt=None, front=None, mixed=None):
    with_loss, with_front, with_mixed = tgt is not None, front is not None, mixed is not None
    rows = h.shape[0] + (BLK if with_front else 0)
    tm = _row_tile(rows)
    nf = D_FF // FF_TILE

    def body(*refs):
        refs = list(refs)
        h_ref, gpre_ref, wg_ref, wu_ref, wd_ref, gpost_ref = refs[:6]
        del refs[:6]
        front_ref = refs.pop(0) if with_front else None
        cg_ref, cs_ref, wo_ref, gm_ref = (refs.pop(0), refs.pop(0), refs.pop(0), refs.pop(0)) if with_mixed else (None,) * 4
        t_ref = refs.pop(0) if with_loss else None
        hm_ref, m_ref = (refs.pop(0), refs.pop(0)) if with_mixed else (None, None)
        ho_ref = None if with_loss else refs.pop(0)
        a_ref, b_ref, s_ref, f_ref = refs[:4]
        dy_ref, loss_ref = refs[4:6] if with_loss else (None, None)
        acc = refs[-1]
        i = pl.program_id(0)
        h_in = _behind_front(h_ref, i, tm, front_ref[...]) if with_front else h_ref[...]
        if with_mixed:
            m = _nn(cg_ref[...], wo_ref[0:512, :]) + _nn(cs_ref[...], wo_ref[512:1024, :])
            m_ref[...] = m
            mn, _ = _rms(m)
            h_in = h_in + mn * gm_ref[...]
            hm_ref[...] = h_in
        hn, _ = _rms(h_in)
        n16 = (hn * gpre_ref[...]).astype(BF16)
        for j in range(nf):
            cols = slice(j * FF_TILE, (j + 1) * FF_TILE)
            a = _nt(n16, wg_ref[cols, :])
            b = _nt(n16, wu_ref[cols, :])
            a_ref[:, cols] = a.astype(BF16)
            b_ref[:, cols] = b.astype(BF16)
            s16 = (a * _sigmoid(a) * b).astype(BF16)
            s_ref[:, cols] = s16
            part = _nn(s16, wd_ref[cols, :])
            if j == 0:
                acc[...] = part
            else:
                acc[...] += part
        f = acc[...]
        f_ref[...] = f
        fn, _ = _rms(f)
        y = h_in + 0.5 * (fn * gpost_ref[...])
        if not with_loss:
            ho_ref[...] = y
        else:
            row = i * tm + lax.broadcasted_iota(jnp.int32, (tm, 1), 0)
            err = jnp.where(row >= BLK, y - _behind_front(t_ref, i, tm, jnp.zeros((BLK, D_MODEL), F32)), 0.0)
            dy_ref[...] = err * (1.0 / D_MODEL)
            part = 0.5 * jnp.sum(jnp.sum(err * err, axis=-1, keepdims=True) * (1.0 / D_MODEL), axis=0, keepdims=True)

            @pl.when(i == 0)
            def _():
                loss_ref[...] = jnp.zeros_like(loss_ref)
            loss_ref[...] += part

    row_f32 = _row_spec(tm, D_MODEL)
    behind = _behind_spec(tm)
    in_specs = [behind if with_front else row_f32, VMEM_SPEC, VMEM_SPEC, VMEM_SPEC, VMEM_SPEC, VMEM_SPEC]
    wide, full = jax.ShapeDtypeStruct((rows, D_FF), BF16), jax.ShapeDtypeStruct((rows, D_MODEL), F32)
    out_specs = [_row_spec(tm, D_FF), _row_spec(tm, D_FF), _row_spec(tm, D_FF), row_f32]
    out_shape = [wide, wide, wide, full]
    args = [h, gpre, wg_t, wu_t, wd, gpost]
    if not with_loss:
        out_specs.insert(0, row_f32)
        out_shape.insert(0, full)
    if with_front:
        in_specs.append(VMEM_SPEC)
        args.append(front)
    if with_mixed:
        in_specs += [_row_spec(tm, 512), _row_spec(tm, 512), VMEM_SPEC, VMEM_SPEC]
        args += list(mixed)
        out_specs = [row_f32, row_f32] + out_specs
        out_shape = [full, full] + out_shape
    if with_loss:
        in_specs.append(behind)
        args.append(tgt)
        out_specs += [row_f32, pl.BlockSpec((8, 128), lambda i: (0, 0))]
        out_shape += [jax.ShapeDtypeStruct((rows, D_MODEL), F32), jax.ShapeDtypeStruct((8, 128), F32)]
    return pl.pallas_call(
        body, name="ffn_fwd_loss" if with_loss else "ffn_fwd", grid=(rows // tm,),
        in_specs=in_specs, out_specs=out_specs, out_shape=out_shape,
        scratch_shapes=[pltpu.VMEM((tm, D_MODEL), F32)],
        compiler_params=_params(("arbitrary",), vmem_mb=62 if with_mixed else 56),
    )(*args)


def _ffn_bwd_act(dh_out, h, a, b, f, gpre, gpost, wg_t, wu_t, wd, name, front=None):
    with_front = front is not None
    rows = dh_out.shape[0]
    tm = _row_tile(rows)
    nf = D_FF // FF_TILE

    def body(dho_ref, h_ref, a_ref, b_ref, f_ref, gpre_ref, gpost_ref, wg_ref, wu_ref, wd_ref, *rest):
        front_ref = rest[0] if with_front else None
        dh_ref, da_ref, db_ref, df_ref, n_ref, dgpre_ref, dgpost_ref, acc = rest[-8:]
        first = pl.program_id(0) == 0
        dho = dho_ref[...]
        drr = 0.5 * dho
        fn, rf = _rms(f_ref[...])
        _acc_add(dgpost_ref, first, _colsum(drr * fn))
        df16 = _rms_bwd(fn, rf, gpost_ref[...], drr).astype(BF16)
        df_ref[...] = df16
        h_in = _behind_front(h_ref, pl.program_id(0), tm, front_ref[...]) if with_front else h_ref[...]
        hn, rh = _rms(h_in)
        n_ref[...] = (hn * gpre_ref[...]).astype(BF16)
        for j in range(nf):
            cols = slice(j * FF_TILE, (j + 1) * FF_TILE)
            ds = _nt(df16, wd_ref[cols, :])
            av = a_ref[:, cols].astype(F32)
            bv = b_ref[:, cols].astype(F32)
            sg = _sigmoid(av)
            db16 = (ds * (av * sg)).astype(BF16)
            da16 = (ds * bv * (sg * (1.0 + av * (1.0 - sg)))).astype(BF16)
            da_ref[:, cols] = da16
            db_ref[:, cols] = db16
            part = _nn(da16, wg_ref[cols, :]) + _nn(db16, wu_ref[cols, :])
            if j == 0:
                acc[...] = part
            else:
                acc[...] += part
        dn = acc[...]
        _acc_add(dgpre_ref, first, _colsum(dn * hn))
        dh_ref[...] = dho + _rms_bwd(hn, rh, gpre_ref[...], dn)

    row_f32 = _row_spec(tm, D_MODEL)
    row_ff = _row_spec(tm, D_FF)
    return pl.pallas_call(
        body, name=name, grid=(rows // tm,),
        in_specs=[row_f32, _behind_spec(tm) if with_front else row_f32, row_ff, row_ff, row_f32,
                  VMEM_SPEC, VMEM_SPEC, VMEM_SPEC, VMEM_SPEC, VMEM_SPEC] + ([VMEM_SPEC] if with_front else []),
        out_specs=[row_f32, row_ff, row_ff, row_f32, row_f32, _acc_spec(D_MODEL), _acc_spec(D_MODEL)],
        out_shape=[jax.ShapeDtypeStruct((rows, D_MODEL), F32), jax.ShapeDtypeStruct((rows, D_FF), BF16),
                   jax.ShapeDtypeStruct((rows, D_FF), BF16), jax.ShapeDtypeStruct((rows, D_MODEL), BF16),
                   jax.ShapeDtypeStruct((rows, D_MODEL), BF16), jax.ShapeDtypeStruct((8, D_MODEL), F32),
                   jax.ShapeDtypeStruct((8, D_MODEL), F32)],
        scratch_shapes=[pltpu.VMEM((tm, D_MODEL), F32)],
        compiler_params=_params(("arbitrary",), vmem_mb=62),
    )(dh_out, h, a, b, f, gpre, gpost, wg_t, wu_t, wd, *([front] if with_front else []))


def _wgrad(lhs, rhs, name, after=None):
    rows, width = lhs.shape
    tf = 256 if width % 256 == 0 else 128
    pieces = 5 if rows % 80 == 0 else 3 if rows % 48 == 0 else 1
    piece = rows // pieces
    tiles, slots = width // tf, 3

    def body(l_hbm, r_hbm, *rest):
        o_ref, l_buf, r_all, l_sems, r_sems = rest[-5:]
        j = pl.program_id(0)

        def fetch_r(c):
            part = pl.ds(c * piece, piece)
            return pltpu.make_async_copy(r_hbm.at[part, :], r_all.at[part, :], r_sems.at[c])

        def fetch_l(tile):
            slot = tile % slots
            start = tile * tf if isinstance(tile, int) else pl.multiple_of(tile * tf, 128)
            return pltpu.make_async_copy(l_hbm.at[:, pl.ds(start, tf)], l_buf.at[slot], l_sems.at[slot])

        @pl.when(j == 0)
        def _():
            for tile in range(min(slots - 1, tiles)):
                fetch_l(tile).start()
            for c in range(pieces):
                fetch_r(c).start()

        @pl.when(j + slots - 1 < tiles)
        def _():
            fetch_l(j + slots - 1).start()
        fetch_l(j).wait()
        lhs_tile = l_buf.at[j % slots]

        @pl.when(j == 0)
        def _():
            total = None
            for c in range(pieces):
                fetch_r(c).wait()
                part = _tn(lhs_tile[c * piece:(c + 1) * piece, :], r_all[c * piece:(c + 1) * piece, :])
                total = part if total is None else total + part
            o_ref[...] = total.astype(BF16)

        @pl.when(j > 0)
        def _():
            o_ref[...] = _tn(lhs_tile[...], r_all[...]).astype(BF16)

    return pl.pallas_call(
        body, name=name, grid=(tiles,),
        in_specs=[ANY_SPEC, ANY_SPEC] + ([] if after is None else [ANY_SPEC]),
        out_specs=pl.BlockSpec((tf, D_MODEL), lambda j: (j, 0)),
        out_shape=jax.ShapeDtypeStruct((width, D_MODEL), BF16),
        scratch_shapes=[pltpu.VMEM((slots, rows, tf), BF16), pltpu.VMEM((rows, D_MODEL), BF16),
                        pltpu.SemaphoreType.DMA((slots,)), pltpu.SemaphoreType.DMA((pieces,))],
        compiler_params=_params(("arbitrary",)),
    )(lhs, rhs, *([] if after is None else [after]))


def _chunk_cumsum(x, lower):
    tri = jnp.where(_tri(lower), 1.0, 0.0).astype(BF16)
    hi, lo = _split_bf16(x)
    return _nn(tri, hi) + _nn(tri, lo)


def _mix_in(h, g, win_p, wa2_p, b_a, cos, sin):
    rows = h.shape[0]
    tm = 640 if rows % 640 == 0 else BLK

    def body(h_ref, g_ref, win_ref, wa2_ref, ba_ref, cos_ref, sin_ref,
             gq_ref, gk_ref, gv_ref, gg_ref, sq_ref, sk_ref, sv_ref, ga_ref, loga_ref, bc_ref, n_ref):
        hn, _ = _rms(h_ref[...])
        n16 = (hn * g_ref[...]).astype(BF16)
        n_ref[...] = n16
        proj = _nt(n16, win_ref[...])
        gq_ref[...] = proj[:, P_GQ:P_GK]
        gk_ref[...] = proj[:, P_GK:P_GV]
        gv_ref[...] = proj[:, P_GV:P_GG].astype(BF16)
        gg_ref[...] = proj[:, P_GG:P_SQ]
        c1, s1 = cos_ref[...], sin_ref[...]
        c4 = jnp.concatenate([c1, c1, c1, c1], axis=1)
        s4 = jnp.concatenate([s1, s1, s1, s1], axis=1)
        sq = proj[:, P_SQ:P_SK]
        sk = proj[:, P_SK:P_SV]
        sq_ref[...] = (sq * c4 + _rot_half(sq) * s4).astype(BF16)
        sk_ref[...] = (sk * c1 + _rot_half(sk) * s1).astype(BF16)
        sv_ref[...] = proj[:, P_SV:P_GA].astype(BF16)
        ga = proj[:, P_GA:P_END]
        ga_ref[...] = ga
        z = _nn(ga, wa2_ref[...]) + ba_ref[...]
        loga = (jnp.minimum(z, 0.0) - jnp.log(1.0 + jnp.exp(-jnp.abs(z)))) * (1.0 / GLA_TAU)
        loga_ref[...] = loga
        for c in range(tm // BLK):
            rs = slice(c * BLK, (c + 1) * BLK)
            bc_ref[rs, :] = _chunk_cumsum(loga[rs, :], True)

    f32 = lambda c: jax.ShapeDtypeStruct((rows, c), F32)
    b16 = lambda c: jax.ShapeDtypeStruct((rows, c), BF16)
    rs = lambda c: _row_spec(tm, c)
    return pl.pallas_call(
        body, name="mix_in", grid=(rows // tm,),
        in_specs=[rs(D_MODEL), VMEM_SPEC, VMEM_SPEC, VMEM_SPEC, VMEM_SPEC, rs(128), rs(128)],
        out_specs=[rs(256), rs(256), rs(512), rs(512), rs(512), rs(128), rs(128), rs(128), rs(256), rs(256), rs(D_MODEL)],
        out_shape=[f32(256), f32(256), b16(512), f32(512), b16(512), b16(128), b16(128), f32(128), f32(256), f32(256),
                   b16(D_MODEL)],
        compiler_params=_params(("arbitrary",)),
    )(h, g, win_p, wa2_p, b_a, cos, sin)


def _side_by_side(parts, name):
    steps, per_step = parts[0]["steps"], parts[0]["per_step"]
    assert all((p["steps"], p["per_step"]) == (steps, per_step) for p in parts)
    counts = [[len(p[key]) for p in parts] for key in ("in_specs", "out_specs", "scratch_shapes")]

    def body(*refs):
        groups, pos = [], 0
        for kind in counts:
            groups.append([])
            for n in kind:
                groups[-1].append(refs[pos:pos + n])
                pos += n
        programs = [p["program"](*groups[0][k], *groups[1][k], *groups[2][k]) for k, p in enumerate(parts)]

        def blocks(c, carries):
            return tuple(block(c, carry) for (block, _, _), carry in zip(programs, carries))
        carries = lax.fori_loop(0, per_step, blocks, tuple(first for _, first, _ in programs))
        for (_, _, finish), carry in zip(programs, carries):
            finish(carry)

    outs = pl.pallas_call(
        body, name=name, grid=(steps,),
        in_specs=[s for p in parts for s in p["in_specs"]], out_specs=[s for p in parts for s in p["out_specs"]],
        out_shape=[s for p in parts for s in p["out_shape"]],
        scratch_shapes=[s for p in parts for s in p["scratch_shapes"]],
        compiler_params=_params(("arbitrary",)),
    )(*[a for p in parts for a in p["args"]])
    split, pos = [], 0
    for n in counts[1]:
        split.append(outs[pos:pos + n])
        pos += n
    return split


def _gla_factors(q, k, bc):
    bm = bc[BLK // 2 - 1:BLK // 2, :]
    bl = bc[BLK - 1:BLK, :]
    e_q, e_k, e_qe, e_kd = jnp.exp(bc - bm), jnp.exp(bm - bc), jnp.exp(bc), jnp.exp(bl - bc)
    return (q * e_q, k * e_k, q * e_qe, k * e_kd), (e_q, e_k, e_qe, e_kd), jnp.exp(bl)


def _gla_fwd(gq, gk, gv, gg, bc, wgn):
    rows = gq.shape[0]
    nc = rows // BLK
    per_step = _blocks_per_step(nc)
    scale = GLA_DK ** -0.5

    def body(q_ref, k_ref, v_ref, gg_ref, bc_ref, wgn_ref, o_ref, cat_ref, sp_ref, st):
        @pl.when(pl.program_id(0) == 0)
        def _():
            st[...] = jnp.zeros_like(st)
        low = _tri(True)
        wgn_v = wgn_ref[...]

        def chunk(c, carry):
            rr = pl.ds(pl.multiple_of(c * BLK, BLK), BLK)
            for p in range(2):
                sl = slice(128 * p, 128 * p + 128)
                (qt, kt, qe, kd), _, ebl = _gla_factors(q_ref[rr, sl] * scale, k_ref[rr, sl], bc_ref[rr, sl])
                s_prev = st[p]
                sp_ref[c, p] = s_prev
                s16 = s_prev.astype(BF16)
                qt16 = qt.astype(BF16)
                s_new = s_prev * ebl
                for hh in range(2):
                    hs = slice(128 * (2 * p + hh), 128 * (2 * p + hh) + 128)
                    lm = _half_mask(128, hh)
                    vh = v_ref[rr, hs]
                    pm = jnp.where(low, _nt(qt16, jnp.where(lm, kt, 0.0).astype(BF16)), 0.0)
                    o = _nn(pm.astype(BF16), vh) + _nt(jnp.where(lm, qe, 0.0).astype(BF16), s16)
                    s_new = s_new + _tn(vh, jnp.where(lm, kd, 0.0).astype(BF16))
                    o_ref[rr, hs] = o
                    on, _ = _rms(o)
                    gate = gg_ref[rr, hs]
                    cat_ref[rr, hs] = (on * wgn_v * (gate * _sigmoid(gate))).astype(BF16)
                st[p] = s_new
            return carry
        return chunk, 0, lambda carry: None

    rs = lambda c: _row_spec(per_step * BLK, c)
    return dict(
        program=body, steps=nc // per_step, per_step=per_step,
        in_specs=[rs(256), rs(256), rs(512), rs(512), rs(256), VMEM_SPEC],
        out_specs=[rs(512), rs(512), pl.BlockSpec((per_step, 2, 128, 128), lambda i: (i, 0, 0, 0))],
        out_shape=[jax.ShapeDtypeStruct((rows, 512), F32), jax.ShapeDtypeStruct((rows, 512), BF16),
                   jax.ShapeDtypeStruct((nc, 2, 128, 128), F32)],
        scratch_shapes=[pltpu.VMEM((2, 128, 128), F32)],
        args=(gq, gk, gv, gg, bc, wgn))


def _gla_bwd(dcat, o_all, gq, gk, gv, gg, bc, sp, wgn):
    rows = gq.shape[0]
    nc = rows // BLK
    per_step = _blocks_per_step(nc)
    steps = nc // per_step
    scale = GLA_DK ** -0.5

    def body(dc_ref, o_ref, q_ref, k_ref, v_ref, gg_ref, bc_ref, sp_ref, wgn_ref,
             dq_ref, dk_ref, dv_ref, dgg_ref, dla_ref, dwgn_ref, dst):
        first = pl.program_id(0) == 0

        @pl.when(first)
        def _():
            dst[...] = jnp.zeros_like(dst)
        low, upp = _tri(True), _tri(False)
        last_row = lax.broadcasted_iota(jnp.int32, (BLK, 1), 0) == BLK - 1
        wgn_v = wgn_ref[...]

        def chunk(c, dwgn):
            rr = pl.ds(pl.multiple_of((per_step - 1 - c) * BLK, BLK), BLK)
            for p in range(2):
                sl = slice(128 * p, 128 * p + 128)
                (qt, kt, qe, kd), (e_q, e_k, e_qe, e_kd), ebl = _gla_factors(
                    q_ref[rr, sl] * scale, k_ref[rr, sl], bc_ref[rr, sl])
                s_prev = sp_ref[per_step - 1 - c, p]
                s16 = s_prev.astype(BF16)
                ds_next = dst[p]
                ds16 = ds_next.astype(BF16)
                qt16 = qt.astype(BF16)
                ds_new = ds_next * ebl
                dqt = jnp.zeros((BLK, 128), F32)
                dkt = jnp.zeros((BLK, 128), F32)
                dqe = jnp.zeros((BLK, 128), F32)
                dkd = jnp.zeros((BLK, 128), F32)
                for hh in range(2):
                    hs = slice(128 * (2 * p + hh), 128 * (2 * p + hh) + 128)
                    lm = _half_mask(128, hh)
                    on, ro = _rms(o_ref[rr, hs])
                    gate = gg_ref[rr, hs]
                    sg = _sigmoid(gate)
                    si = gate * sg
                    dog = dc_ref[rr, hs]
                    dwgn = dwgn + _colsum(dog * si * on)
                    dgg_ref[rr, hs] = dog * (on * wgn_v) * (sg * (1.0 + gate * (1.0 - sg)))
                    do16 = _rms_bwd(on, ro, wgn_v, dog * si).astype(BF16)
                    vh = v_ref[rr, hs]
                    ktm16 = jnp.where(lm, kt, 0.0).astype(BF16)
                    qtm16 = jnp.where(lm, qt, 0.0).astype(BF16)
                    qem16 = jnp.where(lm, qe, 0.0).astype(BF16)
                    kdm16 = jnp.where(lm, kd, 0.0).astype(BF16)
                    p_t = jnp.where(upp, _nt(ktm16, qt16), 0.0)
                    dp_t = jnp.where(upp, _nt(vh, do16), 0.0)
                    dp = jnp.where(low, _nt(do16, vh), 0.0)
                    dv_ref[rr, hs] = _nn(p_t.astype(BF16), do16) + _nt(kdm16, ds16)
                    dqt = dqt + _nn(dp.astype(BF16), ktm16)
                    dkt = dkt + _nn(dp_t.astype(BF16), qtm16)
                    dqe = dqe + jnp.where(lm, _nn(do16, s16), 0.0)
                    dkd = dkd + jnp.where(lm, _nn(vh, ds16), 0.0)
                    ds_new = ds_new + _tn(do16, qem16)
                debl = _colsum(ds_next * s_prev)
                dq_ref[rr, sl] = (dqt * e_q + dqe * e_qe) * scale
                dk_ref[rr, sl] = dkt * e_k + dkd * e_kd
                dkd_kd = dkd * kd
                db = dqt * qt - dkt * kt + dqe * qe - dkd_kd
                db = jnp.where(last_row, db + (_colsum(dkd_kd) + debl * ebl), db)
                dla_ref[rr, sl] = _chunk_cumsum(db, False)
                dst[p] = ds_new
            return dwgn

        def finish(dwgn):
            _acc_add(dwgn_ref, first, dwgn)
        return chunk, jnp.zeros((1, 128), F32), finish

    rev = lambda c: pl.BlockSpec((per_step * BLK, c), lambda i: (steps - 1 - i, 0))
    f32 = lambda c: jax.ShapeDtypeStruct((rows, c), F32)
    return dict(
        program=body, steps=steps, per_step=per_step,
        in_specs=[rev(512), rev(512), rev(256), rev(256), rev(512), rev(512), rev(256),
                  pl.BlockSpec((per_step, 2, 128, 128), lambda i: (steps - 1 - i, 0, 0, 0)), VMEM_SPEC],
        out_specs=[rev(256), rev(256), rev(512), rev(512), rev(256), _acc_spec(128)],
        out_shape=[f32(256), f32(256), f32(512), f32(512), f32(256), jax.ShapeDtypeStruct((8, 128), F32)],
        scratch_shapes=[pltpu.VMEM((2, 128, 128), F32)],
        args=(dcat, o_all, gq, gk, gv, gg, bc, sp, wgn))


def _swa_masks(i):
    t = lax.broadcasted_iota(jnp.int32, (BLK, BLK), 0)
    c = lax.broadcasted_iota(jnp.int32, (BLK, BLK), 1)
    own_side = c <= t
    band_ok = i >= jnp.where(own_side, 1, 2)
    meta_ok = (c % N_META) <= jnp.where(i >= 1, N_META, t - PAD_ROWS)
    return own_side, band_ok, meta_ok, c // N_META


def _swa_blocks(ref, i):
    prev = pl.multiple_of(jnp.maximum(i - 1, 0) * BLK, BLK)
    own = pl.multiple_of(i * BLK, BLK)
    return jnp.concatenate([ref[pl.ds(prev, BLK), :], ref[pl.ds(own, BLK), :]], axis=0), prev, own


def _swa_meta_operand(ref):
    blk = ref[0:BLK, :]
    swapped = pltpu.roll(blk, 64, 1)
    lo = jnp.where(_half_mask(128, 0), blk, swapped)
    hi = jnp.where(_half_mask(128, 1), blk, swapped)
    meta = jnp.concatenate([lo, lo, hi, hi], axis=1)[PAD_ROWS:BLK, :]
    tiled = jnp.concatenate([meta] * SWA_HEADS, axis=0)
    j = lax.broadcasted_iota(jnp.int32, tiled.shape, 0)
    lane = lax.broadcasted_iota(jnp.int32, tiled.shape, 1)
    return jnp.where(j // N_META == lane // SWA_HD, tiled, jnp.zeros_like(tiled))


def _swa_meta_fold(acc):
    out = jnp.zeros((N_META, 128), F32)
    for hd in range(SWA_HEADS):
        half, kv = hd % 2, hd // 4
        piece = acc[N_META * hd:N_META * (hd + 1), 128 * (hd // 2):128 * (hd // 2) + 128]
        piece = jnp.where(_half_mask(128, half), piece, 0.0)
        out = out + (piece if half == kv else pltpu.roll(piece, 64, 1))
    return out


def _by_head(group, per_head):
    out = jnp.zeros((BLK, BLK), F32)
    for hd, v in enumerate(per_head):
        out = jnp.where(group == hd, v, out)
    return out


def _place(x, kv):
    if kv == 0:
        lo = jnp.where(_half_mask(128, 0), x, jnp.zeros_like(x))
        return lo, pltpu.roll(lo, 64, 1)
    hi = jnp.where(_half_mask(128, 1), x, jnp.zeros_like(x))
    return pltpu.roll(hi, 64, 1), hi


def _swa_fwd(sq, sk, sv, sinks, wn):
    rows = sq.shape[0]
    nb = rows // BLK
    per_step = _blocks_per_step(nb)
    scale = SWA_HD ** -0.5

    def body(q_ref, k_ref, v_ref, sink_ref, wn_ref, o_ref, cat_ref, lse_ref, kp, vp):
        step = pl.program_id(0)

        @pl.when(step == 0)
        def _():
            kp[...] = _swa_meta_operand(k_ref)
            vp[...] = _swa_meta_operand(v_ref)

        def one_block(c, carry):
            i = step * per_step + c
            rr = pl.ds(pl.multiple_of(c * BLK, BLK), BLK)
            own_side, band_ok, meta_ok, group = _swa_masks(i)
            k2, _, _ = _swa_blocks(k_ref, i)
            v2, _, _ = _swa_blocks(v_ref, i)
            kz = (_place(k2, 0), _place(k2, 1))
            vz = (_place(v2, 0), _place(v2, 1))
            q_all = q_ref[rr, :]
            s_meta = jnp.where(meta_ok, _nt(q_all, kp[...]) * scale, NEG_INF)
            s_band, m = [], []
            for hd in range(SWA_HEADS):
                kv, half = hd // 4, hd % 2
                q_pair = q_all[:, 128 * (hd // 2):128 * (hd // 2) + 128]
                s2 = _nt(q_pair, kz[kv][half])
                s = jnp.where(band_ok, jnp.where(own_side, s2[:, BLK:], s2[:, :BLK]) * scale, NEG_INF)
                top = jnp.maximum(jnp.max(s, axis=-1, keepdims=True),
                                  jnp.max(jnp.where(group == hd, s_meta, NEG_INF), axis=-1, keepdims=True))
                s_band.append(s)
                m.append(jnp.maximum(top, sink_ref[0, hd]))
            e_meta = jnp.exp(s_meta - _by_head(group, m))
            o_meta = _nn(e_meta.astype(BF16), vp[...])
            outs = []
            for pr in range(4):
                o_pair = o_meta[:, 128 * pr:128 * pr + 128]
                rden = []
                for half in range(2):
                    hd = 2 * pr + half
                    kv = hd // 4
                    e = jnp.exp(s_band[hd] - m[hd])
                    den = (jnp.sum(e, axis=-1, keepdims=True)
                           + jnp.sum(jnp.where(group == hd, e_meta, 0.0), axis=-1, keepdims=True)
                           + jnp.exp(sink_ref[0, hd] - m[hd]))
                    lse_ref[rr, hd:hd + 1] = m[hd] + jnp.log(den)
                    rden.append(1.0 / den)
                    e2 = jnp.concatenate([jnp.where(own_side, 0.0, e), jnp.where(own_side, e, 0.0)], axis=1).astype(BF16)
                    o_pair = o_pair + _nn(e2, vz[kv][half])
                outs.append(o_pair * jnp.where(_half_mask(128, 0), rden[0], rden[1]))
            o = jnp.concatenate(outs, axis=1)
            o_ref[rr, :] = o
            on, _ = _rms(o)
            cat_ref[rr, :] = (on * wn_ref[...]).astype(BF16)
            return carry
        return one_block, 0, lambda carry: None

    return dict(
        program=body, steps=nb // per_step, per_step=per_step,
        in_specs=[_row_spec(per_step * BLK, 512), VMEM_SPEC, VMEM_SPEC, SMEM_SPEC, VMEM_SPEC],
        out_specs=[_row_spec(per_step * BLK, 512), _row_spec(per_step * BLK, 512), _row_spec(per_step * BLK, SWA_HEADS)],
        out_shape=[jax.ShapeDtypeStruct((rows, 512), F32), jax.ShapeDtypeStruct((rows, 512), BF16),
                   jax.ShapeDtypeStruct((rows, SWA_HEADS), F32)],
        scratch_shapes=[pltpu.VMEM((BLK, 512), BF16), pltpu.VMEM((BLK, 512), BF16)],
        args=(sq, sk, sv, sinks, wn))


def _swa_bwd(dcat, o_all, sq, sk, sv, lse, sinks, wn):
    rows = sq.shape[0]
    nb = rows // BLK
    per_step = _blocks_per_step(nb)
    steps = nb // per_step
    scale = SWA_HD ** -0.5

    def body(dc_ref, o_ref, q_ref, k_ref, v_ref, lse_ref, sink_ref, wn_ref, dq_ref, dk_ref, dv_ref, dsink_ref, dwn_ref,
             kp, vp, dkp, dvp):
        step = pl.program_id(0)

        @pl.when(step == 0)
        def _():
            dk_ref[...] = jnp.zeros_like(dk_ref)
            dv_ref[...] = jnp.zeros_like(dv_ref)
            dkp[...] = jnp.zeros_like(dkp)
            dvp[...] = jnp.zeros_like(dvp)
            kp[...] = _swa_meta_operand(k_ref)
            vp[...] = _swa_meta_operand(v_ref)

        def one_block(c, carry):
            i = step * per_step + c
            rr = pl.ds(pl.multiple_of(c * BLK, BLK), BLK)
            first = i == 0
            own_side, band_ok, meta_ok, group = _swa_masks(i)
            k2, prev, own = _swa_blocks(k_ref, i)
            v2, _, _ = _swa_blocks(v_ref, i)
            kz = (_place(k2, 0), _place(k2, 1))
            vz = (_place(v2, 0), _place(v2, 1))
            o = o_ref[rr, :]
            on, ro = _rms(o)
            dc = dc_ref[rr, :]
            _acc_add(dwn_ref, first, _colsum(dc * on))
            do = _rms_bwd(on, ro, wn_ref[...], dc)
            do_o = do * o
            do16 = do.astype(BF16)
            q_all = q_ref[rr, :]
            lse = [lse_ref[rr, hd:hd + 1] for hd in range(SWA_HEADS)]
            delta = [jnp.sum(jnp.where(_half_mask(128, hd % 2), do_o[:, 128 * (hd // 2):128 * (hd // 2) + 128], 0.0),
                             axis=-1, keepdims=True) for hd in range(SWA_HEADS)]
            s_meta = jnp.where(meta_ok, _nt(q_all, kp[...]) * scale, NEG_INF)
            p_meta = jnp.exp(s_meta - _by_head(group, lse))
            ds_meta16 = (p_meta * (_nt(do16, vp[...]) - _by_head(group, delta)) * scale).astype(BF16)
            dq_meta = _nn(ds_meta16, kp[...])
            dkp[...] += _tn(ds_meta16, q_all)
            dvp[...] += _tn(p_meta.astype(BF16), do16)
            own2 = jnp.concatenate([own_side.astype(jnp.int32)] * 2, axis=0) > 0
            ok2 = jnp.concatenate([band_ok.astype(jnp.int32)] * 2, axis=0) > 0

            def window(x2):
                return jnp.where(own2, x2[:, BLK:], x2[:, :BLK])

            def unwindow(x):
                return jnp.concatenate([jnp.where(own2, 0.0, x), jnp.where(own2, x, 0.0)], axis=1).astype(BF16)
            lane8 = lax.broadcasted_iota(jnp.int32, (1, 128), 1)
            dsink = jnp.zeros((1, 128), F32)
            dq_pairs = [dq_meta[:, 128 * pr:128 * pr + 128] for pr in range(4)]
            dk2 = [[None, None], [None, None]]
            dv2 = [[None, None], [None, None]]
            for kv in range(2):
                for half in range(2):
                    heads, pairs = (4 * kv + half, 4 * kv + 2 + half), (2 * kv, 2 * kv + 1)
                    q_s = jnp.concatenate([q_all[:, 128 * pr:128 * pr + 128] for pr in pairs], axis=0)
                    do_s = jnp.concatenate([do16[:, 128 * pr:128 * pr + 128] for pr in pairs], axis=0)
                    lse_s = jnp.concatenate([lse[hd] for hd in heads], axis=0)
                    delta_s = jnp.concatenate([delta[hd] for hd in heads], axis=0)
                    s = jnp.where(ok2, window(_nt(q_s, kz[kv][half])) * scale, NEG_INF)
                    prob = jnp.exp(s - lse_s)
                    for hd in heads:
                        dsink = dsink + jnp.where(lane8 == hd, -jnp.sum(jnp.exp(sink_ref[0, hd] - lse[hd]) * delta[hd]), 0.0)
                    ds2 = unwindow(prob * (window(_nt(do_s, vz[kv][half])) - delta_s) * scale)
                    dq_s = _nn(ds2, kz[kv][half])
                    dq_pairs[pairs[0]] = dq_pairs[pairs[0]] + dq_s[:BLK]
                    dq_pairs[pairs[1]] = dq_pairs[pairs[1]] + dq_s[BLK:]
                    dk2[kv][half] = _tn(ds2, q_s)
                    dv2[kv][half] = _tn(unwindow(prob), do_s)
            dq_ref[rr, :] = jnp.concatenate(dq_pairs, axis=1)
            _acc_add(dsink_ref, first, dsink)
            for ref, acc2 in ((dk_ref, dk2), (dv_ref, dv2)):
                tot = jnp.zeros((2 * BLK, 128), F32)
                for kv in range(2):
                    for half in range(2):
                        part = jnp.where(_half_mask(128, half), acc2[kv][half], 0.0)
                        tot = tot + (part if half == kv else pltpu.roll(part, 64, 1))
                ref[pl.ds(prev, BLK), :] += tot[:BLK]
                ref[pl.ds(own, BLK), :] += tot[BLK:]
            return carry

        def finish(carry):
            del carry

            @pl.when(step == steps - 1)
            def _():
                dk_ref[PAD_ROWS:BLK, :] += _swa_meta_fold(dkp[...])
                dv_ref[PAD_ROWS:BLK, :] += _swa_meta_fold(dvp[...])
        return one_block, jnp.zeros((1, 128), F32), finish

    full = pl.BlockSpec((rows, 128), lambda i: (0, 0))
    blocks = lambda cols: _row_spec(per_step * BLK, cols)
    return dict(
        program=body, steps=steps, per_step=per_step,
        in_specs=[blocks(512), blocks(512), blocks(512), VMEM_SPEC, VMEM_SPEC, blocks(SWA_HEADS), SMEM_SPEC, VMEM_SPEC],
        out_specs=[blocks(512), full, full, _acc_spec(128), _acc_spec(512)],
        out_shape=[jax.ShapeDtypeStruct((rows, 512), F32), jax.ShapeDtypeStruct((rows, 128), F32),
                   jax.ShapeDtypeStruct((rows, 128), F32), jax.ShapeDtypeStruct((8, 128), F32),
                   jax.ShapeDtypeStruct((8, 512), F32)],
        scratch_shapes=[pltpu.VMEM((BLK, 512), BF16), pltpu.VMEM((BLK, 512), BF16),
                        pltpu.VMEM((BLK, 512), F32), pltpu.VMEM((BLK, 512), F32)],
        args=(dcat, o_all, sq, sk, sv, lse, sinks, wn))


def _mix_out_bwd(dh, m, wout, gpost):
    rows = dh.shape[0]
    tm = _row_tile(rows)

    def body(dh_ref, m_ref, w_ref, g_ref, dcg_ref, dcs_ref, dm_ref, dg_ref):
        first = pl.program_id(0) == 0
        dhv = dh_ref[...]
        mn, rm = _rms(m_ref[...])
        _acc_add(dg_ref, first, _colsum(dhv * mn))
        dm16 = _rms_bwd(mn, rm, g_ref[...], dhv).astype(BF16)
        dm_ref[...] = dm16
        dcat = _nt(dm16, w_ref[...])
        dcg_ref[...] = dcat[:, 0:512]
        dcs_ref[...] = dcat[:, 512:1024]

    row_f32 = _row_spec(tm, D_MODEL)
    return pl.pallas_call(
        body, name="mix_out_bwd", grid=(rows // tm,),
        in_specs=[row_f32, row_f32, VMEM_SPEC, VMEM_SPEC],
        out_specs=[_row_spec(tm, 512), _row_spec(tm, 512), row_f32, _acc_spec(D_MODEL)],
        out_shape=[jax.ShapeDtypeStruct((rows, 512), F32), jax.ShapeDtypeStruct((rows, 512), F32),
                   jax.ShapeDtypeStruct((rows, D_MODEL), BF16), jax.ShapeDtypeStruct((8, D_MODEL), F32)],
        compiler_params=_params(("arbitrary",)),
    )(dh, m, wout, gpost)


def _mix_in_bwd(dh_out, h, g, win_p, wa2_p, cos, sin, loga, ga, dgq, dgk, dgv, dgg, dsq, dsk, dsv, dloga):
    rows = h.shape[0]
    tm = _row_tile(rows)

    def body(dho_ref, h_ref, g_ref, win_ref, wa2_ref, cos_ref, sin_ref, loga_ref, ga_ref,
             dgq_ref, dgk_ref, dgv_ref, dgg_ref, dsq_ref, dsk_ref, dsv_ref, dla_ref,
             dh_ref, dproj_ref, dwa2_ref, dg_ref, dba_ref):
        first = pl.program_id(0) == 0
        dz = dla_ref[...] * (1.0 / GLA_TAU) * (1.0 - jnp.exp(GLA_TAU * loga_ref[...]))
        _acc_add(dba_ref, first, _colsum(dz))
        dga = _nt(dz, wa2_ref[...])
        pa = _tn(ga_ref[...], dz)
        c1, s1 = cos_ref[...], sin_ref[...]
        c4 = jnp.concatenate([c1, c1, c1, c1], axis=1)
        s4 = jnp.concatenate([s1, s1, s1, s1], axis=1)
        dq_r, dk_r = dsq_ref[...], dsk_ref[...]
        dsq = dq_r * c4 - _rot_half(dq_r * s4)
        dsk = dk_r * c1 - _rot_half(dk_r * s1)
        dproj16 = jnp.concatenate(
            [dgq_ref[...], dgk_ref[...], dgv_ref[...], dgg_ref[...], dsq, dsk, dsv_ref[...], dga], axis=1).astype(BF16)
        dproj_ref[...] = dproj16
        dn = _nn(dproj16, win_ref[...])

        @pl.when(first)
        def _():
            dwa2_ref[...] = pa

        @pl.when(jnp.logical_not(first))
        def _():
            dwa2_ref[...] += pa
        hn, rh = _rms(h_ref[...])
        _acc_add(dg_ref, first, _colsum(dn * hn))
        dh_ref[...] = dho_ref[...] + _rms_bwd(hn, rh, g_ref[...], dn)

    rs = lambda c: _row_spec(tm, c)
    return pl.pallas_call(
        body, name="mix_in_bwd", grid=(rows // tm,),
        in_specs=[rs(D_MODEL), rs(D_MODEL), VMEM_SPEC, VMEM_SPEC, VMEM_SPEC, rs(128), rs(128), rs(256), rs(128),
                  rs(256), rs(256), rs(512), rs(512), rs(512), rs(128), rs(128), rs(256)],
        out_specs=[rs(D_MODEL), rs(P_END), pl.BlockSpec((128, 256), lambda i: (0, 0)), _acc_spec(D_MODEL), _acc_spec(256)],
        out_shape=[jax.ShapeDtypeStruct((rows, D_MODEL), F32), jax.ShapeDtypeStruct((rows, P_END), BF16),
                   jax.ShapeDtypeStruct((128, 256), F32), jax.ShapeDtypeStruct((8, D_MODEL), F32),
                   jax.ShapeDtypeStruct((8, 256), F32)],
        compiler_params=_params(("arbitrary",)),
    )(dh_out, h, g, win_p, wa2_p, cos, sin, loga, ga, dgq, dgk, dgv, dgg, dsq, dsk, dsv, dloga)


def _rope_tables(rows):
    pos = (jnp.arange(rows, dtype=jnp.int32) - PAD_ROWS).astype(F32)
    inv_freq = 1.0 / (ROPE_THETA ** (jnp.arange(0, SWA_HD, 2, dtype=F32) / SWA_HD))
    ang = pos[:, None] * inv_freq[None, :]
    return jnp.tile(jnp.cos(ang), (1, 4)), jnp.tile(jnp.sin(ang), (1, 4))


def _local_step(x, tgt, front, w, late_weights=None, on_grads=None, on_small=None):
    cos, sin = _rope_tables(x.shape[0] + BLK)
    g = {}

    def tell(group, names):
        for nm in names:
            g[nm] = grads_now[nm]
        return None if on_grads is None else on_grads(group, {nm: grads_now[nm] for nm in names})

    h1, a1, b1, s1, f1 = _ffn_fwd(x, w["ffn1_pre"], w["wg1"], w["wu1"], w["wd1"], w["ffn1_post"], front=front)
    if late_weights is not None:
        w = {**w, **late_weights("win", f1)}
    gq, gk, gv, gg, sq, sk, sv, ga, loga, bc, n2 = _mix_in(h1, w["mix_pre"], w["win"], w["wa2"], w["b_a"], cos, sin)
    (o_g, cat_g, sp), (o_s, cat_s, lse) = _side_by_side(
        [_gla_fwd(gq, gk, gv, gg, bc, w["gla_norm"]), _swa_fwd(sq, sk, sv, w["sinks"], w["swa_norm"])], "attention_fwd")
    if late_weights is not None:
        w = {**w, **late_weights("rest", lse)}
    h2, m, a2, b2, s2, f2, dy, loss = _ffn_fwd(h1, w["ffn2_pre"], w["wg2"], w["wu2"], w["wd2"], w["ffn2_post"], tgt,
                                               mixed=(cat_g, cat_s, w["wout"], w["mix_post"]))
    dh2, da, db, df, n3, g["ffn2_pre"], g["ffn2_post"] = _ffn_bwd_act(
        dy, h2, a2, b2, f2, w["ffn2_pre"], w["ffn2_post"], w["wg2"], w["wu2"], w["wd2"], "ffn2_bwd_act")
    grads_now = dict(wd2=_wgrad(s2, df, "ffn2_wgrad_down"), wg2=_wgrad(da, n3, "ffn2_wgrad_gate"),
                     wu2=_wgrad(db, n3, "ffn2_wgrad_up"))
    tok = tell("ffn2", ("wd2", "wg2", "wu2"))
    dcg, dcs, dm, g["mix_post"] = _mix_out_bwd(dh2, m, w["wout"], w["mix_post"] + (0.0 if tok is None else tok[0, 0]))
    (dgq, dgk, dgv, dgg, dloga, g["gla_norm"]), (dsq, dsk, dsv, g["sinks"], g["swa_norm"]) = _side_by_side(
        [_gla_bwd(dcg, o_g, gq, gk, gv, gg, bc, sp, w["gla_norm"]),
         _swa_bwd(dcs, o_s, sq, sk, sv, lse, w["sinks"], w["swa_norm"])], "attention_bwd")
    dh1, dproj, g["wa2"], g["mix_pre"], g["b_a"] = _mix_in_bwd(
        dh2, h1, w["mix_pre"], w["win"], w["wa2"], cos, sin, loga, ga, dgq, dgk, dgv, dgg, dsq, dsk, dsv, dloga)
    dh0, da, db, df, n1, g["ffn1_pre"], g["ffn1_post"] = _ffn_bwd_act(
        dh1, x, a1, b1, f1, w["ffn1_pre"], w["ffn1_post"], w["wg1"], w["wu1"], w["wd1"], "ffn1_bwd_act", front=front)
    tok = None if on_small is None else on_small(loss[0, 0], dh0, g)
    grads_now = dict(wd1=_wgrad(s1, df, "ffn1_wgrad_down", after=tok))
    tok = tell("ffn1_down", ("wd1",))
    grads_now = dict(wg1=_wgrad(da, n1, "ffn1_wgrad_gate", after=tok))
    tok = tell("ffn1_gate", ("wg1",))
    grads_now = dict(wu1=_wgrad(db, n1, "ffn1_wgrad_up", after=tok))
    tok = tell("ffn1_up", ("wu1",))
    grads_now = dict(win=_wgrad(dproj, n2, "win_wgrad", after=tok),
                     wout=jnp.concatenate([_wgrad(cat_g, dm, "wout_wgrad_gla", after=tok),
                                           _wgrad(cat_s, dm, "wout_wgrad_swa", after=tok)], axis=0))
    tell("mix", ("wout", "win"))
    return loss[0, 0], dh0, g


def _win_pad_rows(win_t):
    pad = jnp.zeros((P_END - P_GA - 16, win_t.shape[1]), win_t.dtype)
    return jnp.concatenate([win_t[0:1536], win_t[1552:2320], win_t[1536:1552], pad], axis=0)


def _win_unpad_rows(win_p):
    return jnp.concatenate([win_p[0:1536], win_p[P_GA:P_GA + 16], win_p[1536:P_GA]], axis=0)


def _place_on_mesh():
    return lax.axis_index("x"), lax.axis_index("y"), lax.axis_index("c")


def _dev_index(px, py, pc):
    return 4 * px + 2 * py + pc


def _other_devices(x, y, c):
    flip = lambda v, f: 1 - v if f else v
    return [(flip(x, fx), flip(y, fy), flip(c, fc)) for fx in (0, 1) for fy in (0, 1) for fc in (0, 1)][1:]


def _all_gather(shards):
    n = len(shards)

    def body(*refs):
        ins, outs = refs[:n], refs[n:2 * n]
        zeros_ref, send_sems, recv_sems, local_sems = refs[2 * n:]
        zeros_ref[...] = jnp.zeros_like(zeros_ref)
        x, y, c = _place_on_mesh()
        me, sibling = (x, y, c), (x, y, 1 - c)
        chips = [(1 - x, y), (x, 1 - y), (1 - x, 1 - y)]

        def rows(k, px, py, pc):
            r = ins[k].shape[0]
            return outs[k].at[pl.ds(pl.multiple_of(_dev_index(px, py, pc) * r, 8), r), :]

        def copy(k, slot, block, to, src=None):
            return pltpu.make_async_remote_copy(
                src_ref=rows(k, *block) if src is None else src, dst_ref=rows(k, *block),
                send_sem=send_sems.at[k, slot], recv_sem=recv_sems.at[k, slot], device_id=to, device_id_type=MESH)

        local = [pltpu.make_async_copy(ins[k], rows(k, *me), local_sems.at[k]) for k in range(n)]
        sends = []
        for k in range(n):
            local[k].start()
            sends.append(copy(k, 0, me, sibling, src=ins[k]))
            sends += [copy(k, 1 + j, me, (*chip, c), src=ins[k]) for j, chip in enumerate(chips)]
        for cp in sends:
            cp.start()
        for k in range(n):
            for j, chip in enumerate(chips):
                copy(k, 1 + j, (*chip, c), me).wait_recv()
                passed = copy(k, 4 + j, (*chip, c), sibling)
                passed.start()
                sends.append(passed)
        for k in range(n):
            copy(k, 0, sibling, me).wait_recv()
            for j, chip in enumerate(chips):
                copy(k, 4 + j, (*chip, 1 - c), me).wait_recv()
        for cp in sends:
            cp.wait_send()
        for cp in local:
            cp.wait()

    return pl.pallas_call(
        body, name="all_gather_weights",
        in_specs=[ANY_SPEC] * n, out_specs=[ANY_SPEC] * n + [VMEM_SPEC],
        out_shape=[jax.ShapeDtypeStruct((N_DEV * s.shape[0], s.shape[1]), s.dtype) for s in shards]
        + [jax.ShapeDtypeStruct((8, 128), F32)],
        scratch_shapes=[pltpu.SemaphoreType.DMA((n, 7)), pltpu.SemaphoreType.DMA((n, 7)), pltpu.SemaphoreType.DMA((n,))],
    )(*shards)


HBM_SPEC = pl.BlockSpec(memory_space=pltpu.HBM)
SEM_SPEC = pl.BlockSpec(memory_space=pltpu.SEMAPHORE)
DATAFLOW = pltpu.SideEffectType.DATAFLOW_SIDE_EFFECTING


GATHER, SCATTER, SCATTER_CHIPS = "gather", "scatter", "scatter among chips"


def _exchange_peers(kind):
    x, y, c = _place_on_mesh()
    if kind == SCATTER_CHIPS:
        peers = [(1 - x, y, c), (x, 1 - y, c), (1 - x, 1 - y, c)]
        return peers, [2 * p[0] + p[1] for p in peers], 2 * x + y, 4
    peers = _other_devices(x, y, c)
    return peers, [_dev_index(*p) for p in peers], _dev_index(x, y, c), N_DEV


def _exchange_copies(srcs, lands, send_sems, recv_sems, own_sems, kind, arriving):
    peers, theirs, me, blocks = _exchange_peers(kind)
    remote, local = [], []
    for k, (src, land) in enumerate(zip(srcs, lands)):
        r = land.shape[0] // blocks

        def block(ref, d):
            return ref.at[pl.ds(pl.multiple_of(d * r, 8), r), :]

        for f, (peer, him) in enumerate(zip(peers, theirs)):
            mine, his = (him, me) if arriving else (me, him)
            sem = len(peers) * k + f
            remote.append(pltpu.make_async_remote_copy(
                src_ref=src if kind == GATHER else block(src, his), dst_ref=block(land, mine),
                send_sem=send_sems.at[sem], recv_sem=recv_sems.at[sem], device_id=peer, device_id_type=MESH))
        local.append(pltpu.make_async_copy(src if kind == GATHER else block(src, me), block(land, me), own_sems.at[k]))
    return remote, local


def _exchange_start(srcs, kind, name):
    n = len(srcs)
    lands = [lax.empty((N_DEV * s.shape[0], s.shape[1]) if kind == GATHER else s.shape, s.dtype) for s in srcs]
    sems = (3 if kind == SCATTER_CHIPS else 7) * n

    def body(*refs):
        remote, local = _exchange_copies(refs[:n], refs[n:2 * n], *refs[2 * n:2 * n + 3], kind, False)
        for cp in remote + local:
            cp.start()
        refs[-1][...] = jnp.zeros_like(refs[-1])

    both = list(srcs) + list(lands)
    outs = pl.pallas_call(
        body, name=name,
        out_shape=(pltpu.SemaphoreType.DMA((sems,)), pltpu.SemaphoreType.DMA((sems,)), pltpu.SemaphoreType.DMA((n,)),
                   *[pltpu.HBM(a.shape, a.dtype) for a in both], jax.ShapeDtypeStruct((8, 128), F32)),
        in_specs=[HBM_SPEC] * (2 * n), out_specs=(SEM_SPEC, SEM_SPEC, SEM_SPEC, *[HBM_SPEC] * (2 * n), VMEM_SPEC),
        input_output_aliases={i: 3 + i for i in range(2 * n)},
        compiler_params=pltpu.CompilerParams(has_side_effects=DATAFLOW),
    )(*[pltpu.with_memory_space_constraint(a, pltpu.HBM) for a in both])
    return outs[0:3], outs[3:3 + n], outs[3 + n:3 + 2 * n], outs[-1]


def _exchange_wait(started, kind, after, name):
    sems, srcs, lands, _ = started
    n = len(srcs)

    def body(*refs):
        args = (refs[:n], refs[n:2 * n], *refs[2 * n:2 * n + 3], kind)
        going, local = _exchange_copies(*args, False)
        for cp in going:
            cp.wait_send()
        for cp in local:
            cp.wait()
        for cp in _exchange_copies(*args, True)[0]:
            cp.wait_recv()

    both = list(srcs) + list(lands)
    outs = pl.pallas_call(
        body, name=name, out_shape=[pltpu.HBM(a.shape, a.dtype) for a in both],
        in_specs=[HBM_SPEC] * (2 * n) + [SEM_SPEC, SEM_SPEC, SEM_SPEC, ANY_SPEC], out_specs=[HBM_SPEC] * (2 * n),
        input_output_aliases={i: i for i in range(2 * n)},
        compiler_params=pltpu.CompilerParams(has_side_effects=DATAFLOW),
    )(*both, *sems, after)
    return outs[n:]


def _sibling_reduce(part, name):
    r, cols = part.shape[0] // N_DEV, part.shape[1]

    def body(p_ref, o_ref, mine, got, send_sems, recv_sems, own_sems):
        x, y, c = _place_on_mesh()

        def block(d):
            return p_ref.at[pl.ds(pl.multiple_of(d * r, 8), r), :]
        swaps = [pltpu.make_async_remote_copy(
            src_ref=block(2 * j + 1 - c), dst_ref=got.at[j], send_sem=send_sems.at[j], recv_sem=recv_sems.at[j],
            device_id=(x, y, 1 - c), device_id_type=MESH) for j in range(4)]
        keeps = [pltpu.make_async_copy(block(2 * j + c), mine.at[j], own_sems.at[j]) for j in range(4)]
        for cp in swaps + keeps:
            cp.start()
        for j in range(4):
            keeps[j].wait()
            swaps[j].wait()
            o_ref[pl.ds(j * r, r), :] = (mine[j].astype(F32) + got[j].astype(F32)).astype(o_ref.dtype)

    return pl.pallas_call(
        body, name=name, in_specs=[ANY_SPEC], out_specs=VMEM_SPEC,
        out_shape=jax.ShapeDtypeStruct((4 * r, cols), part.dtype),
        scratch_shapes=[pltpu.VMEM((4, r, cols), part.dtype), pltpu.VMEM((4, r, cols), part.dtype),
                        pltpu.SemaphoreType.DMA((4,)), pltpu.SemaphoreType.DMA((4,)), pltpu.SemaphoreType.DMA((4,))],
        compiler_params=pltpu.CompilerParams(vmem_limit_bytes=32 << 20),
    )(part)


def _sum_partials(parts, name, blocks=N_DEV):
    n = len(parts)

    def body(*refs):
        ins, outs = refs[:n], refs[n:]
        first = pl.program_id(0) == 0
        for i_ref, o_ref in zip(ins, outs):
            v = i_ref[...].astype(F32)

            @pl.when(first)
            def _():
                o_ref[...] = v

            @pl.when(jnp.logical_not(first))
            def _():
                o_ref[...] += v

    shapes = [(p.shape[0] // blocks, p.shape[1]) for p in parts]
    return pl.pallas_call(
        body, name=name, grid=(blocks,),
        in_specs=[pl.BlockSpec(s, lambda j: (j, 0)) for s in shapes],
        out_specs=[pl.BlockSpec(s, lambda j: (0, 0)) for s in shapes],
        out_shape=[jax.ShapeDtypeStruct(s, F32) for s in shapes],
        compiler_params=_params(("arbitrary",)),
    )(*parts)


def _adamw_update(w, g, m, v):
    m = ADAM_B1 * m + (1.0 - ADAM_B1) * g
    v = ADAM_B2 * v + (1.0 - ADAM_B2) * (g * g)
    m_hat = m * (1.0 / (1.0 - ADAM_B1 ** ADAM_STEP))
    v_hat = v * (1.0 / (1.0 - ADAM_B2 ** ADAM_STEP))
    return -ADAM_LR * (m_hat / (jnp.sqrt(v_hat) + ADAM_EPS) + ADAM_WD * w), m, v


def _sum_adamw(parts, w, m, v, blocks, name):
    shape = w.shape

    def body(p_ref, w_ref, m_ref, v_ref, g_ref, d_ref, mo_ref, vo_ref):
        j = pl.program_id(0)
        part = p_ref[...].astype(F32)

        @pl.when(j == 0)
        def _():
            g_ref[...] = part

        @pl.when(j > 0)
        def _():
            g_ref[...] += part

        @pl.when(j == blocks - 1)
        def _():
            d_ref[...], mo_ref[...], vo_ref[...] = _adamw_update(w_ref[...], g_ref[...], m_ref[...], v_ref[...])

    held = pl.BlockSpec(shape, lambda j: (0, 0))
    return pl.pallas_call(
        body, name=name, grid=(blocks,),
        in_specs=[pl.BlockSpec(shape, lambda j: (j, 0)), held, held, held],
        out_specs=[held] * 4, out_shape=[jax.ShapeDtypeStruct(shape, F32)] * 4,
        compiler_params=_params(("arbitrary",)),
    )(parts, w, m, v)


def _adamw(ws, gs, ms, vs, name):
    n = len(ws)

    def body(*refs):
        w_r, g_r, m_r, v_r = refs[:n], refs[n:2 * n], refs[2 * n:3 * n], refs[3 * n:4 * n]
        d_o, m_o, v_o = refs[4 * n:5 * n], refs[5 * n:6 * n], refs[6 * n:7 * n]
        for k in range(n):
            d_o[k][...], m_o[k][...], v_o[k][...] = _adamw_update(w_r[k][...], g_r[k][...], m_r[k][...], v_r[k][...])

    shapes = [jax.ShapeDtypeStruct(w.shape, F32) for w in ws]
    outs = pl.pallas_call(
        body, name=name, in_specs=[VMEM_SPEC] * (4 * n), out_specs=[VMEM_SPEC] * (3 * n), out_shape=shapes * 3,
        compiler_params=pltpu.CompilerParams(vmem_limit_bytes=56 << 20),
    )(*ws, *gs, *ms, *vs)
    return outs[:n], outs[n:2 * n], outs[2 * n:]


WEIGHT_NAMES = ("meta_tokens", "ffn1_pre_norm", "ffn1_w_gate", "ffn1_w_up", "ffn1_w_down", "ffn1_post_norm", "mix_pre_norm",
                "w_in", "gla_w_a2", "gla_b_a", "gla_out_norm", "swa_sinks", "swa_out_norm", "w_out", "mix_post_norm",
                "ffn2_pre_norm", "ffn2_w_gate", "ffn2_w_up", "ffn2_w_down", "ffn2_post_norm")
WIN_SHARD = D_IN // N_DEV
WIN_SHARD_PAD = 304
SLAB_VECTORS = ("ffn1_pre", "ffn1_post", "mix_pre", "mix_post", "ffn2_pre", "ffn2_post")
SLAB_ROWS = 32


def kernel(x, meta_tokens, ffn1_pre_norm, ffn1_w_gate, ffn1_w_up, ffn1_w_down, ffn1_post_norm, mix_pre_norm, w_in, gla_w_a2, gla_b_a, gla_out_norm, swa_sinks, swa_out_norm, w_out, mix_post_norm, ffn2_pre_norm, ffn2_w_gate, ffn2_w_up, ffn2_w_down, ffn2_post_norm, loss_target, m_meta_tokens, m_ffn1_pre_norm, m_ffn1_w_gate, m_ffn1_w_up, m_ffn1_w_down, m_ffn1_post_norm, m_mix_pre_norm, m_w_in, m_gla_w_a2, m_gla_b_a, m_gla_out_norm, m_swa_sinks, m_swa_out_norm, m_w_out, m_mix_post_norm, m_ffn2_pre_norm, m_ffn2_w_gate, m_ffn2_w_up, m_ffn2_w_down, m_ffn2_post_norm, v_meta_tokens, v_ffn1_pre_norm, v_ffn1_w_gate, v_ffn1_w_up, v_ffn1_w_down, v_ffn1_post_norm, v_mix_pre_norm, v_w_in, v_gla_w_a2, v_gla_b_a, v_gla_out_norm, v_swa_sinks, v_swa_out_norm, v_w_out, v_mix_post_norm, v_ffn2_pre_norm, v_ffn2_w_gate, v_ffn2_w_up, v_ffn2_w_down, v_ffn2_post_norm):
    given = dict(locals())
    W = {n: given[n] for n in WEIGHT_NAMES}
    M = {n: given["m_" + n] for n in WEIGHT_NAMES}
    V = {n: given["v_" + n] for n in WEIGHT_NAMES}
    dev = _dev_index(*_place_on_mesh())

    def t16(w):
        return w[0].T.astype(BF16)

    small = jnp.concatenate([W["meta_tokens"], jnp.pad(W["gla_w_a2"][0], ((0, 0), (0, 96)))], axis=0)
    wg1, wu1, wd1, small_g, gathered_zeros = _all_gather(
        [t16(W["ffn1_w_gate"]), t16(W["ffn1_w_up"]), W["ffn1_w_down"][0].astype(BF16), small])
    def after_zero(shard, zeros):
        return shard + zeros[0:1, 0:1].astype(shard.dtype)
    win_shard = jnp.pad(t16(W["w_in"]), ((0, WIN_SHARD_PAD - WIN_SHARD), (0, 0)))
    win_shard = after_zero(win_shard, gathered_zeros)
    mid = _exchange_start([win_shard], GATHER, "gather_w_in_start")
    late_shards = [after_zero(W["w_out"][0].astype(BF16), mid[3]), t16(W["ffn2_w_gate"]), t16(W["ffn2_w_up"]),
                   W["ffn2_w_down"][0].astype(BF16)]
    late = _exchange_start(late_shards, GATHER, "gather_late_weights_start")

    def late_weights(what, after):
        if what == "win":
            win_g, = _exchange_wait(mid, GATHER, after, "gather_w_in_wait")
            win_t = win_g.reshape(N_DEV, WIN_SHARD_PAD, D_MODEL)[:, :WIN_SHARD].reshape(D_IN, D_MODEL)
            return dict(win=_win_pad_rows(win_t))
        wout, wg2, wu2, wd2 = _exchange_wait(late, GATHER, after, "gather_late_weights_wait")
        return dict(wout=wout, wg2=wg2, wu2=wu2, wd2=wd2)

    small_g = small_g.reshape(N_DEV, 32, 128)
    meta_full = small_g[:, :N_META].transpose(1, 0, 2).reshape(N_META, D_MODEL)
    wa2_full = small_g[:, N_META:, :32].transpose(1, 0, 2).reshape(16, 256)
    w = dict(
        ffn1_pre=W["ffn1_pre_norm"] + late[3][0, 0], ffn1_post=W["ffn1_post_norm"], mix_pre=W["mix_pre_norm"],
        mix_post=W["mix_post_norm"], ffn2_pre=W["ffn2_pre_norm"], ffn2_post=W["ffn2_post_norm"], b_a=W["gla_b_a"],
        gla_norm=W["gla_out_norm"], sinks=W["swa_sinks"], swa_norm=W["swa_out_norm"], wg1=wg1, wu1=wu1, wd1=wd1,
        wa2=jnp.pad(wa2_full, ((0, 112), (0, 0))))

    in_flight = []

    def on_grads(group, grads):
        parts = []
        for nm, p in grads.items():
            if nm == "win":
                p = _win_unpad_rows(p).reshape(N_DEV, WIN_SHARD, D_MODEL)
                p = jnp.pad(p, ((0, 0), (0, WIN_SHARD_PAD - WIN_SHARD), (0, 0))).reshape(N_DEV * WIN_SHARD_PAD, D_MODEL)
            parts.append(p)
        kind = SCATTER if group in ("ffn2", "ffn1_down", "ffn1_gate") else SCATTER_CHIPS
        if kind == SCATTER_CHIPS:
            parts = [_sibling_reduce(p, "pair_" + group + "_" + nm) for nm, p in zip(grads, parts)]
        started = _exchange_start(parts, kind, "scatter_" + group + "_start")
        in_flight.append((group, list(grads), started, kind))
        return started[3]

    small_flight = []

    def on_small(loss, dh0, g):
        packed = jnp.concatenate([g["b_a"][0:1], g["gla_norm"][0:1], g["sinks"][0:1], g["swa_norm"][0:1]], axis=1)
        slab = jnp.concatenate([g[k][0:1] for k in SLAB_VECTORS] + [packed, jnp.full((1, D_MODEL), loss, F32),
                               g["wa2"][:16].reshape(4, D_MODEL), jnp.zeros((4, D_MODEL), F32), dh0[PAD_ROWS:BLK]], axis=0)
        small_flight.append(_exchange_start([slab], GATHER, "gather_small_grads_start"))
        return small_flight[0][3]

    front = jnp.concatenate([jnp.zeros((PAD_ROWS, D_MODEL), F32), meta_full], axis=0)
    loss, dh0, g = _local_step(x[0], loss_target[0], front, w, late_weights, on_grads, on_small)
    grad_x = dh0[BLK:][None]

    land, = _exchange_wait(small_flight[0], GATHER, in_flight[-1][2][3], "gather_small_grads_wait")
    tot = _sum_partials([land], "sum_small_grads")[0]
    loss = tot[7, 0]
    small_grads = dict(
        ffn1_pre_norm=tot[0:1], ffn1_post_norm=tot[1:2], mix_pre_norm=tot[2:3], mix_post_norm=tot[3:4],
        ffn2_pre_norm=tot[4:5], ffn2_post_norm=tot[5:6], gla_b_a=tot[6:7, 0:256], gla_out_norm=tot[6:7, 256:384],
        swa_sinks=tot[6:7, 384:392], swa_out_norm=tot[6:7, 512:1024],
        gla_w_a2=lax.dynamic_slice_in_dim(tot[8:12].reshape(16, 256), dev * 32, 32, axis=1)[None],
        meta_tokens=lax.dynamic_slice_in_dim(tot[16:32], dev * 128, 128, axis=1))

    big = dict(wg1=("ffn1_w_gate", True), wu1=("ffn1_w_up", True), wd1=("ffn1_w_down", False), win=("w_in", True),
               wout=("w_out", False), wg2=("ffn2_w_gate", True), wu2=("ffn2_w_up", True), wd2=("ffn2_w_down", False))
    grads = dict(small_grads)
    delta, new_m, new_v = {}, {}, {}
    names = [n for n in WEIGHT_NAMES if n not in [full for full, _ in big.values()]]
    two_d = lambda a: a.reshape(-1, a.shape[-1])
    d_, m_, v_ = _adamw([two_d(W[n]) for n in names], [two_d(grads[n]) for n in names],
                        [two_d(M[n]) for n in names], [two_d(V[n]) for n in names], "adamw_small")
    for k, n in enumerate(names):
        delta[n], new_m[n], new_v[n] = d_[k].reshape(W[n].shape), m_[k].reshape(W[n].shape), v_[k].reshape(W[n].shape)

    before_wait = d_[0] + in_flight[-1][2][3][0, 0]
    for group, shorts, started, kind in in_flight:
        lands = _exchange_wait(started, kind, before_wait, "scatter_" + group + "_wait")
        blocks = 4 if kind == SCATTER_CHIPS else N_DEV
        for short, land in zip(shorts, lands):
            n, transposed = big[short]
            to_slab = (lambda a: a[0].T) if transposed else (lambda a: a[0])
            from_slab = (lambda a: a.T[None]) if transposed else (lambda a: a[None])
            if short == "win":
                g_slab = _sum_partials([land], "sum_" + n, blocks)[0][:WIN_SHARD]
                d_, m_, v_ = _adamw([to_slab(W[n])], [g_slab], [to_slab(M[n])], [to_slab(V[n])], "adamw_" + n)
                d_, m_, v_ = d_[0], m_[0], v_[0]
            else:
                g_slab, d_, m_, v_ = _sum_adamw(land, to_slab(W[n]), to_slab(M[n]), to_slab(V[n]), blocks, "adamw_" + n)
            grads[n], delta[n], new_m[n], new_v[n] = from_slab(g_slab), from_slab(d_), from_slab(m_), from_slab(v_)
            before_wait = d_
    return (loss, grad_x, *[grads[n] for n in WEIGHT_NAMES], *[delta[n] for n in WEIGHT_NAMES],
            *[new_m[n] for n in WEIGHT_NAMES], *[new_v[n] for n in WEIGHT_NAMES])
```

```python
import math

import jax
import jax.numpy as jnp
from jax import lax
from jax.experimental import pallas as pl
from jax.experimental.pallas import tpu as pltpu

F32, BF16 = jnp.float32, jnp.bfloat16

D_MODEL = 1024
D_FF = 2816
N_META = 16
BLK = 128
PAD_ROWS = BLK - N_META
GLA_DK = 64
SWA_HD = 64
SWA_HEADS = 8
GLA_TAU = 16.0
NORM_EPS = 1e-6
NEG_INF = -1e30
ROPE_THETA = 10000.0
P_GQ, P_GK, P_GV, P_GG, P_SQ, P_SK, P_SV, P_GA, P_END = 0, 256, 512, 1024, 1536, 2048, 2176, 2304, 2432
D_IN = 2320
IN_SPLITS = (256, 256, 512, 512, 16, 512, 128, 128)
FF_TILE = 2816
WGRAD_TILE_MAX = 2432
N_DEV = 8
MESH = pl.DeviceIdType.MESH

ADAM_LR, ADAM_B1, ADAM_B2, ADAM_EPS, ADAM_WD, ADAM_STEP = 0.001, 0.9, 0.999, 1e-08, 0.01, 10

V7X_VMEM_BYTES = 64 << 20
VMEM_SPEC = pl.BlockSpec(memory_space=pltpu.VMEM)
SMEM_SPEC = pl.BlockSpec(memory_space=pltpu.SMEM)
ANY_SPEC = pl.BlockSpec(memory_space=pl.ANY)


def _params(semantics, vmem_mb=56):
    return pltpu.CompilerParams(dimension_semantics=semantics, vmem_limit_bytes=vmem_mb << 20)


def _row_tile(rows):
    return 416 if rows % 416 == 0 else BLK


def _blocks_per_step(blocks):
    return 5 if blocks % 5 == 0 else 1


def _nn(a, b):
    return lax.dot_general(a, b, (((1,), (0,)), ((), ())), preferred_element_type=F32)


def _nt(a, b):
    return lax.dot_general(a, b, (((1,), (1,)), ((), ())), preferred_element_type=F32)


def _tn(a, b):
    return lax.dot_general(a, b, (((0,), (0,)), ((), ())), preferred_element_type=F32)


def _rms(x):
    r = lax.rsqrt(jnp.mean(x * x, axis=-1, keepdims=True) + NORM_EPS)
    return x * r, r


def _rms_bwd(xn, r, w, dy):
    g = dy * w
    return r * (g - xn * jnp.mean(g * xn, axis=-1, keepdims=True))


def _sigmoid(x):
    return 1.0 / (1.0 + jnp.exp(-x))


def _colsum(x):
    return jnp.sum(x, axis=0, keepdims=True)


def _split_bf16(x):
    hi = x.astype(BF16)
    lo = (x - hi.astype(F32)).astype(BF16)
    return hi, lo


def _tri(lower):
    r = lax.broadcasted_iota(jnp.int32, (BLK, BLK), 0)
    c = lax.broadcasted_iota(jnp.int32, (BLK, BLK), 1)
    return (r >= c) if lower else (c >= r)


def _half_mask(width, half):
    lane = lax.broadcasted_iota(jnp.int32, (1, width), 1)
    return ((lane % 128) < 64) if half == 0 else ((lane % 128) >= 64)


def _rot_half(x):
    w = x.shape[-1]
    lane = lax.broadcasted_iota(jnp.int32, (1, w), 1)
    return jnp.where((lane % SWA_HD) < SWA_HD // 2, -pltpu.roll(x, w - SWA_HD // 2, 1), pltpu.roll(x, SWA_HD // 2, 1))


def _row_spec(tm, cols):
    return pl.BlockSpec((tm, cols), lambda i: (i, 0))


def _acc_spec(cols):
    return pl.BlockSpec((8, cols), lambda i: (0, 0))


def _acc_add(ref, first, value):
    @pl.when(first)
    def _():
        ref[...] = jnp.zeros_like(ref)
    ref[0:1, :] += value


def _behind_spec(tm):
    return pl.BlockSpec((pl.Element(tm), pl.Element(D_MODEL)),
                        lambda i: (pl.multiple_of(jnp.maximum(i * tm - BLK, 0), math.gcd(tm, BLK)), 0))


def _behind_front(ref, i, tm, front):
    blk = ref[...]
    return jnp.where(i == 0, jnp.concatenate([front, blk[0:tm - BLK]], axis=0), blk)


def _ffn_fwd(h, gpre, wg_t, wu_t, wd, gpost, tgt=None, front=None, mixed=None):
    with_loss, with_front, with_mixed = tgt is not None, front is not None, mixed is not None
    rows = h.shape[0] + (BLK if with_front else 0)
    tm = _row_tile(rows)
    nf = D_FF // FF_TILE

    def body(*refs):
        refs = list(refs)
        h_ref, gpre_ref, wg_ref, wu_ref, wd_ref, gpost_ref = refs[:6]
        del refs[:6]
        front_ref = refs.pop(0) if with_front else None
        cg_ref, cs_ref, wo_ref, gm_ref = (refs.pop(0), refs.pop(0), refs.pop(0), refs.pop(0)) if with_mixed else (None,) * 4
        t_ref = refs.pop(0) if with_loss else None
        hm_ref, m_ref = (refs.pop(0), refs.pop(0)) if with_mixed else (None, None)
        ho_ref = None if with_loss else refs.pop(0)
        a_ref, b_ref, s_ref, f_ref = refs[:4]
        dy_ref, loss_ref = refs[4:6] if with_loss else (None, None)
        acc = refs[-1]
        i = pl.program_id(0)
        h_in = _behind_front(h_ref, i, tm, front_ref[...]) if with_front else h_ref[...]
        if with_mixed:
            m = _nn(cg_ref[...], wo_ref[0:512, :]) + _nn(cs_ref[...], wo_ref[512:1024, :])
            m_ref[...] = m
            mn, _ = _rms(m)
            h_in = h_in + mn * gm_ref[...]
            hm_ref[...] = h_in
        hn, _ = _rms(h_in)
        n16 = (hn * gpre_ref[...]).astype(BF16)
        for j in range(nf):
            cols = slice(j * FF_TILE, (j + 1) * FF_TILE)
            a = _nt(n16, wg_ref[cols, :])
            b = _nt(n16, wu_ref[cols, :])
            a_ref[:, cols] = a.astype(BF16)
            b_ref[:, cols] = b.astype(BF16)
            s16 = (a * _sigmoid(a) * b).astype(BF16)
            s_ref[:, cols] = s16
            part = _nn(s16, wd_ref[cols, :])
            if j == 0:
                acc[...] = part
            else:
                acc[...] += part
        f = acc[...]
        f_ref[...] = f
        fn, _ = _rms(f)
        y = h_in + 0.5 * (fn * gpost_ref[...])
        if not with_loss:
            ho_ref[...] = y
        else:
            row = i * tm + lax.broadcasted_iota(jnp.int32, (tm, 1), 0)
            err = jnp.where(row >= BLK, y - _behind_front(t_ref, i, tm, jnp.zeros((BLK, D_MODEL), F32)), 0.0)
            dy_ref[...] = err * (1.0 / D_MODEL)
            part = 0.5 * jnp.sum(jnp.sum(err * err, axis=-1, keepdims=True) * (1.0 / D_MODEL), axis=0, keepdims=True)

            @pl.when(i == 0)
            def _():
                loss_ref[...] = jnp.zeros_like(loss_ref)
            loss_ref[...] += part

    row_f32 = _row_spec(tm, D_MODEL)
    behind = _behind_spec(tm)
    in_specs = [behind if with_front else row_f32, VMEM_SPEC, VMEM_SPEC, VMEM_SPEC, VMEM_SPEC, VMEM_SPEC]
    wide, full = jax.ShapeDtypeStruct((rows, D_FF), BF16), jax.ShapeDtypeStruct((rows, D_MODEL), F32)
    out_specs = [_row_spec(tm, D_FF), _row_spec(tm, D_FF), _row_spec(tm, D_FF), row_f32]
    out_shape = [wide, wide, wide, full]
    args = [h, gpre, wg_t, wu_t, wd, gpost]
    if not with_loss:
        out_specs.insert(0, row_f32)
        out_shape.insert(0, full)
    if with_front:
        in_specs.append(VMEM_SPEC)
        args.append(front)
    if with_mixed:
        in_specs += [_row_spec(tm, 512), _row_spec(tm, 512), VMEM_SPEC, VMEM_SPEC]
        args += list(mixed)
        out_specs = [row_f32, row_f32] + out_specs
        out_shape = [full, full] + out_shape
    if with_loss:
        in_specs.append(behind)
        args.append(tgt)
        out_specs += [row_f32, pl.BlockSpec((8, 128), lambda i: (0, 0))]
        out_shape += [jax.ShapeDtypeStruct((rows, D_MODEL), F32), jax.ShapeDtypeStruct((8, 128), F32)]
    return pl.pallas_call(
        body, name="ffn_fwd_loss" if with_loss else "ffn_fwd", grid=(rows // tm,),
        in_specs=in_specs, out_specs=out_specs, out_shape=out_shape,
        scratch_shapes=[pltpu.VMEM((tm, D_MODEL), F32)],
        compiler_params=_params(("arbitrary",), vmem_mb=62 if with_mixed else 56),
    )(*args)


def _ffn_bwd_act(dh_out, h, a, b, f, gpre, gpost, wg_t, wu_t, wd, name, front=None):
    with_front = front is not None
    rows = dh_out.shape[0]
    tm = _row_tile(rows)
    nf = D_FF // FF_TILE

    def body(dho_ref, h_ref, a_ref, b_ref, f_ref, gpre_ref, gpost_ref, wg_ref, wu_ref, wd_ref, *rest):
        front_ref = rest[0] if with_front else None
        dh_ref, da_ref, db_ref, df_ref, n_ref, dgpre_ref, dgpost_ref, acc = rest[-8:]
        first = pl.program_id(0) == 0
        dho = dho_ref[...]
        drr = 0.5 * dho
        fn, rf = _rms(f_ref[...])
        _acc_add(dgpost_ref, first, _colsum(drr * fn))
        df16 = _rms_bwd(fn, rf, gpost_ref[...], drr).astype(BF16)
        df_ref[...] = df16
        h_in = _behind_front(h_ref, pl.program_id(0), tm, front_ref[...]) if with_front else h_ref[...]
        hn, rh = _rms(h_in)
        n_ref[...] = (hn * gpre_ref[...]).astype(BF16)
        for j in range(nf):
            cols = slice(j * FF_TILE, (j + 1) * FF_TILE)
            ds = _nt(df16, wd_ref[cols, :])
            av = a_ref[:, cols].astype(F32)
            bv = b_ref[:, cols].astype(F32)
            sg = _sigmoid(av)
            db16 = (ds * (av * sg)).astype(BF16)
            da16 = (ds * bv * (sg * (1.0 + av * (1.0 - sg)))).astype(BF16)
            da_ref[:, cols] = da16
            db_ref[:, cols] = db16
            part = _nn(da16, wg_ref[cols, :]) + _nn(db16, wu_ref[cols, :])
            if j == 0:
                acc[...] = part
            else:
                acc[...] += part
        dn = acc[...]
        _acc_add(dgpre_ref, first, _colsum(dn * hn))
        dh_ref[...] = dho + _rms_bwd(hn, rh, gpre_ref[...], dn)

    row_f32 = _row_spec(tm, D_MODEL)
    row_ff = _row_spec(tm, D_FF)
    return pl.pallas_call(
        body, name=name, grid=(rows // tm,),
        in_specs=[row_f32, _behind_spec(tm) if with_front else row_f32, row_ff, row_ff, row_f32,
                  VMEM_SPEC, VMEM_SPEC, VMEM_SPEC, VMEM_SPEC, VMEM_SPEC] + ([VMEM_SPEC] if with_front else []),
        out_specs=[row_f32, row_ff, row_ff, row_f32, row_f32, _acc_spec(D_MODEL), _acc_spec(D_MODEL)],
        out_shape=[jax.ShapeDtypeStruct((rows, D_MODEL), F32), jax.ShapeDtypeStruct((rows, D_FF), BF16),
                   jax.ShapeDtypeStruct((rows, D_FF), BF16), jax.ShapeDtypeStruct((rows, D_MODEL), BF16),
                   jax.ShapeDtypeStruct((rows, D_MODEL), BF16), jax.ShapeDtypeStruct((8, D_MODEL), F32),
                   jax.ShapeDtypeStruct((8, D_MODEL), F32)],
        scratch_shapes=[pltpu.VMEM((tm, D_MODEL), F32)],
        compiler_params=_params(("arbitrary",), vmem_mb=62),
    )(dh_out, h, a, b, f, gpre, gpost, wg_t, wu_t, wd, *([front] if with_front else []))


def _wgrad(lhs, rhs, name, after=None):
    rows, width = lhs.shape
    tf = 256 if width % 256 == 0 else 128
    pieces = 5 if rows % 80 == 0 else 3 if rows % 48 == 0 else 1
    piece = rows // pieces
    tiles, slots = width // tf, 3

    def body(l_hbm, r_hbm, *rest):
        o_ref, l_buf, r_all, l_sems, r_sems = rest[-5:]
        j = pl.program_id(0)

        def fetch_r(c):
            part = pl.ds(c * piece, piece)
            return pltpu.make_async_copy(r_hbm.at[part, :], r_all.at[part, :], r_sems.at[c])

        def fetch_l(tile):
            slot = tile % slots
            start = tile * tf if isinstance(tile, int) else pl.multiple_of(tile * tf, 128)
            return pltpu.make_async_copy(l_hbm.at[:, pl.ds(start, tf)], l_buf.at[slot], l_sems.at[slot])

        @pl.when(j == 0)
        def _():
            for tile in range(min(slots - 1, tiles)):
                fetch_l(tile).start()
            for c in range(pieces):
                fetch_r(c).start()

        @pl.when(j + slots - 1 < tiles)
        def _():
            fetch_l(j + slots - 1).start()
        fetch_l(j).wait()
        lhs_tile = l_buf.at[j % slots]

        @pl.when(j == 0)
        def _():
            total = None
            for c in range(pieces):
                fetch_r(c).wait()
                part = _tn(lhs_tile[c * piece:(c + 1) * piece, :], r_all[c * piece:(c + 1) * piece, :])
                total = part if total is None else total + part
            o_ref[...] = total.astype(BF16)

        @pl.when(j > 0)
        def _():
            o_ref[...] = _tn(lhs_tile[...], r_all[...]).astype(BF16)

    return pl.pallas_call(
        body, name=name, grid=(tiles,),
        in_specs=[ANY_SPEC, ANY_SPEC] + ([] if after is None else [ANY_SPEC]),
        out_specs=pl.BlockSpec((tf, D_MODEL), lambda j: (j, 0)),
        out_shape=jax.ShapeDtypeStruct((width, D_MODEL), BF16),
        scratch_shapes=[pltpu.VMEM((slots, rows, tf), BF16), pltpu.VMEM((rows, D_MODEL), BF16),
                        pltpu.SemaphoreType.DMA((slots,)), pltpu.SemaphoreType.DMA((pieces,))],
        compiler_params=_params(("arbitrary",)),
    )(lhs, rhs, *([] if after is None else [after]))


def _chunk_cumsum(x, lower):
    tri = jnp.where(_tri(lower), 1.0, 0.0).astype(BF16)
    hi, lo = _split_bf16(x)
    return _nn(tri, hi) + _nn(tri, lo)


def _mix_in(h, g, win_p, wa2_p, b_a, cos, sin):
    rows = h.shape[0]
    tm = 640 if rows % 640 == 0 else BLK

    def body(h_ref, g_ref, win_ref, wa2_ref, ba_ref, cos_ref, sin_ref,
             gq_ref, gk_ref, gv_ref, gg_ref, sq_ref, sk_ref, sv_ref, ga_ref, loga_ref, bc_ref, n_ref):
        hn, _ = _rms(h_ref[...])
        n16 = (hn * g_ref[...]).astype(BF16)
        n_ref[...] = n16
        proj = _nt(n16, win_ref[...])
        gq_ref[...] = proj[:, P_GQ:P_GK]
        gk_ref[...] = proj[:, P_GK:P_GV]
        gv_ref[...] = proj[:, P_GV:P_GG].astype(BF16)
        gg_ref[...] = proj[:, P_GG:P_SQ]
        c1, s1 = cos_ref[...], sin_ref[...]
        c4 = jnp.concatenate([c1, c1, c1, c1], axis=1)
        s4 = jnp.concatenate([s1, s1, s1, s1], axis=1)
        sq = proj[:, P_SQ:P_SK]
        sk = proj[:, P_SK:P_SV]
        sq_ref[...] = (sq * c4 + _rot_half(sq) * s4).astype(BF16)
        sk_ref[...] = (sk * c1 + _rot_half(sk) * s1).astype(BF16)
        sv_ref[...] = proj[:, P_SV:P_GA].astype(BF16)
        ga = proj[:, P_GA:P_END]
        ga_ref[...] = ga
        z = _nn(ga, wa2_ref[...]) + ba_ref[...]
        loga = (jnp.minimum(z, 0.0) - jnp.log(1.0 + jnp.exp(-jnp.abs(z)))) * (1.0 / GLA_TAU)
        loga_ref[...] = loga
        for c in range(tm // BLK):
            rs = slice(c * BLK, (c + 1) * BLK)
            bc_ref[rs, :] = _chunk_cumsum(loga[rs, :], True)

    f32 = lambda c: jax.ShapeDtypeStruct((rows, c), F32)
    b16 = lambda c: jax.ShapeDtypeStruct((rows, c), BF16)
    rs = lambda c: _row_spec(tm, c)
    return pl.pallas_call(
        body, name="mix_in", grid=(rows // tm,),
        in_specs=[rs(D_MODEL), VMEM_SPEC, VMEM_SPEC, VMEM_SPEC, VMEM_SPEC, rs(128), rs(128)],
        out_specs=[rs(256), rs(256), rs(512), rs(512), rs(512), rs(128), rs(128), rs(128), rs(256), rs(256), rs(D_MODEL)],
        out_shape=[f32(256), f32(256), b16(512), f32(512), b16(512), b16(128), b16(128), f32(128), f32(256), f32(256),
                   b16(D_MODEL)],
        compiler_params=_params(("arbitrary",)),
    )(h, g, win_p, wa2_p, b_a, cos, sin)


def _side_by_side(parts, name):
    steps, per_step = parts[0]["steps"], parts[0]["per_step"]
    assert all((p["steps"], p["per_step"]) == (steps, per_step) for p in parts)
    counts = [[len(p[key]) for p in parts] for key in ("in_specs", "out_specs", "scratch_shapes")]

    def body(*refs):
        groups, pos = [], 0
        for kind in counts:
            groups.append([])
            for n in kind:
                groups[-1].append(refs[pos:pos + n])
                pos += n
        programs = [p["program"](*groups[0][k], *groups[1][k], *groups[2][k]) for k, p in enumerate(parts)]

        def blocks(c, carries):
            return tuple(block(c, carry) for (block, _, _), carry in zip(programs, carries))
        carries = lax.fori_loop(0, per_step, blocks, tuple(first for _, first, _ in programs))
        for (_, _, finish), carry in zip(programs, carries):
            finish(carry)

    outs = pl.pallas_call(
        body, name=name, grid=(steps,),
        in_specs=[s for p in parts for s in p["in_specs"]], out_specs=[s for p in parts for s in p["out_specs"]],
        out_shape=[s for p in parts for s in p["out_shape"]],
        scratch_shapes=[s for p in parts for s in p["scratch_shapes"]],
        compiler_params=_params(("arbitrary",)),
    )(*[a for p in parts for a in p["args"]])
    split, pos = [], 0
    for n in counts[1]:
        split.append(outs[pos:pos + n])
        pos += n
    return split


def _gla_factors(q, k, bc):
    bm = bc[BLK // 2 - 1:BLK // 2, :]
    bl = bc[BLK - 1:BLK, :]
    e_q, e_k, e_qe, e_kd = jnp.exp(bc - bm), jnp.exp(bm - bc), jnp.exp(bc), jnp.exp(bl - bc)
    return (q * e_q, k * e_k, q * e_qe, k * e_kd), (e_q, e_k, e_qe, e_kd), jnp.exp(bl)


def _gla_fwd(gq, gk, gv, gg, bc, wgn):
    rows = gq.shape[0]
    nc = rows // BLK
    per_step = _blocks_per_step(nc)
    scale = GLA_DK ** -0.5

    def body(q_ref, k_ref, v_ref, gg_ref, bc_ref, wgn_ref, o_ref, cat_ref, sp_ref, st):
        @pl.when(pl.program_id(0) == 0)
        def _():
            st[...] = jnp.zeros_like(st)
        low = _tri(True)
        wgn_v = wgn_ref[...]

        def chunk(c, carry):
            rr = pl.ds(pl.multiple_of(c * BLK, BLK), BLK)
            for p in range(2):
                sl = slice(128 * p, 128 * p + 128)
                (qt, kt, qe, kd), _, ebl = _gla_factors(q_ref[rr, sl] * scale, k_ref[rr, sl], bc_ref[rr, sl])
                s_prev = st[p]
                sp_ref[c, p] = s_prev
                s16 = s_prev.astype(BF16)
                qt16 = qt.astype(BF16)
                s_new = s_prev * ebl
                for hh in range(2):
                    hs = slice(128 * (2 * p + hh), 128 * (2 * p + hh) + 128)
                    lm = _half_mask(128, hh)
                    vh = v_ref[rr, hs]
                    pm = jnp.where(low, _nt(qt16, jnp.where(lm, kt, 0.0).astype(BF16)), 0.0)
                    o = _nn(pm.astype(BF16), vh) + _nt(jnp.where(lm, qe, 0.0).astype(BF16), s16)
                    s_new = s_new + _tn(vh, jnp.where(lm, kd, 0.0).astype(BF16))
                    o_ref[rr, hs] = o
                    on, _ = _rms(o)
                    gate = gg_ref[rr, hs]
                    cat_ref[rr, hs] = (on * wgn_v * (gate * _sigmoid(gate))).astype(BF16)
                st[p] = s_new
            return carry
        return chunk, 0, lambda carry: None

    rs = lambda c: _row_spec(per_step * BLK, c)
    return dict(
        program=body, steps=nc // per_step, per_step=per_step,
        in_specs=[rs(256), rs(256), rs(512), rs(512), rs(256), VMEM_SPEC],
        out_specs=[rs(512), rs(512), pl.BlockSpec((per_step, 2, 128, 128), lambda i: (i, 0, 0, 0))],
        out_shape=[jax.ShapeDtypeStruct((rows, 512), F32), jax.ShapeDtypeStruct((rows, 512), BF16),
                   jax.ShapeDtypeStruct((nc, 2, 128, 128), F32)],
        scratch_shapes=[pltpu.VMEM((2, 128, 128), F32)],
        args=(gq, gk, gv, gg, bc, wgn))


def _gla_bwd(dcat, o_all, gq, gk, gv, gg, bc, sp, wgn):
    rows = gq.shape[0]
    nc = rows // BLK
    per_step = _blocks_per_step(nc)
    steps = nc // per_step
    scale = GLA_DK ** -0.5

    def body(dc_ref, o_ref, q_ref, k_ref, v_ref, gg_ref, bc_ref, sp_ref, wgn_ref,
             dq_ref, dk_ref, dv_ref, dgg_ref, dla_ref, dwgn_ref, dst):
        first = pl.program_id(0) == 0

        @pl.when(first)
        def _():
            dst[...] = jnp.zeros_like(dst)
        low, upp = _tri(True), _tri(False)
        last_row = lax.broadcasted_iota(jnp.int32, (BLK, 1), 0) == BLK - 1
        wgn_v = wgn_ref[...]

        def chunk(c, dwgn):
            rr = pl.ds(pl.multiple_of((per_step - 1 - c) * BLK, BLK), BLK)
            for p in range(2):
                sl = slice(128 * p, 128 * p + 128)
                (qt, kt, qe, kd), (e_q, e_k, e_qe, e_kd), ebl = _gla_factors(
                    q_ref[rr, sl] * scale, k_ref[rr, sl], bc_ref[rr, sl])
                s_prev = sp_ref[per_step - 1 - c, p]
                s16 = s_prev.astype(BF16)
                ds_next = dst[p]
                ds16 = ds_next.astype(BF16)
                qt16 = qt.astype(BF16)
                ds_new = ds_next * ebl
                dqt = jnp.zeros((BLK, 128), F32)
                dkt = jnp.zeros((BLK, 128), F32)
                dqe = jnp.zeros((BLK, 128), F32)
                dkd = jnp.zeros((BLK, 128), F32)
                for hh in range(2):
                    hs = slice(128 * (2 * p + hh), 128 * (2 * p + hh) + 128)
                    lm = _half_mask(128, hh)
                    on, ro = _rms(o_ref[rr, hs])
                    gate = gg_ref[rr, hs]
                    sg = _sigmoid(gate)
                    si = gate * sg
                    dog = dc_ref[rr, hs]
                    dwgn = dwgn + _colsum(dog * si * on)
                    dgg_ref[rr, hs] = dog * (on * wgn_v) * (sg * (1.0 + gate * (1.0 - sg)))
                    do16 = _rms_bwd(on, ro, wgn_v, dog * si).astype(BF16)
                    vh = v_ref[rr, hs]
                    ktm16 = jnp.where(lm, kt, 0.0).astype(BF16)
                    qtm16 = jnp.where(lm, qt, 0.0).astype(BF16)
                    qem16 = jnp.where(lm, qe, 0.0).astype(BF16)
                    kdm16 = jnp.where(lm, kd, 0.0).astype(BF16)
                    p_t = jnp.where(upp, _nt(ktm16, qt16), 0.0)
                    dp_t = jnp.where(upp, _nt(vh, do16), 0.0)
                    dp = jnp.where(low, _nt(do16, vh), 0.0)
                    dv_ref[rr, hs] = _nn(p_t.astype(BF16), do16) + _nt(kdm16, ds16)
                    dqt = dqt + _nn(dp.astype(BF16), ktm16)
                    dkt = dkt + _nn(dp_t.astype(BF16), qtm16)
                    dqe = dqe + jnp.where(lm, _nn(do16, s16), 0.0)
                    dkd = dkd + jnp.where(lm, _nn(vh, ds16), 0.0)
                    ds_new = ds_new + _tn(do16, qem16)
                debl = _colsum(ds_next * s_prev)
                dq_ref[rr, sl] = (dqt * e_q + dqe * e_qe) * scale
                dk_ref[rr, sl] = dkt * e_k + dkd * e_kd
                dkd_kd = dkd * kd
                db = dqt * qt - dkt * kt + dqe * qe - dkd_kd
                db = jnp.where(last_row, db + (_colsum(dkd_kd) + debl * ebl), db)
                dla_ref[rr, sl] = _chunk_cumsum(db, False)
                dst[p] = ds_new
            return dwgn

        def finish(dwgn):
            _acc_add(dwgn_ref, first, dwgn)
        return chunk, jnp.zeros((1, 128), F32), finish

    rev = lambda c: pl.BlockSpec((per_step * BLK, c), lambda i: (steps - 1 - i, 0))
    f32 = lambda c: jax.ShapeDtypeStruct((rows, c), F32)
    return dict(
        program=body, steps=steps, per_step=per_step,
        in_specs=[rev(512), rev(512), rev(256), rev(256), rev(512), rev(512), rev(256),
                  pl.BlockSpec((per_step, 2, 128, 128), lambda i: (steps - 1 - i, 0, 0, 0)), VMEM_SPEC],
        out_specs=[rev(256), rev(256), rev(512), rev(512), rev(256), _acc_spec(128)],
        out_shape=[f32(256), f32(256), f32(512), f32(512), f32(256), jax.ShapeDtypeStruct((8, 128), F32)],
        scratch_shapes=[pltpu.VMEM((2, 128, 128), F32)],
        args=(dcat, o_all, gq, gk, gv, gg, bc, sp, wgn))


def _swa_masks(i):
    t = lax.broadcasted_iota(jnp.int32, (BLK, BLK), 0)
    c = lax.broadcasted_iota(jnp.int32, (BLK, BLK), 1)
    own_side = c <= t
    band_ok = i >= jnp.where(own_side, 1, 2)
    meta_ok = (c % N_META) <= jnp.where(i >= 1, N_META, t - PAD_ROWS)
    return own_side, band_ok, meta_ok, c // N_META


def _swa_blocks(ref, i):
    prev = pl.multiple_of(jnp.maximum(i - 1, 0) * BLK, BLK)
    own = pl.multiple_of(i * BLK, BLK)
    return jnp.concatenate([ref[pl.ds(prev, BLK), :], ref[pl.ds(own, BLK), :]], axis=0), prev, own


def _swa_meta_operand(ref):
    blk = ref[0:BLK, :]
    swapped = pltpu.roll(blk, 64, 1)
    lo = jnp.where(_half_mask(128, 0), blk, swapped)
    hi = jnp.where(_half_mask(128, 1), blk, swapped)
    meta = jnp.concatenate([lo, lo, hi, hi], axis=1)[PAD_ROWS:BLK, :]
    tiled = jnp.concatenate([meta] * SWA_HEADS, axis=0)
    j = lax.broadcasted_iota(jnp.int32, tiled.shape, 0)
    lane = lax.broadcasted_iota(jnp.int32, tiled.shape, 1)
    return jnp.where(j // N_META == lane // SWA_HD, tiled, jnp.zeros_like(tiled))


def _swa_meta_fold(acc):
    out = jnp.zeros((N_META, 128), F32)
    for hd in range(SWA_HEADS):
        half, kv = hd % 2, hd // 4
        piece = acc[N_META * hd:N_META * (hd + 1), 128 * (hd // 2):128 * (hd // 2) + 128]
        piece = jnp.where(_half_mask(128, half), piece, 0.0)
        out = out + (piece if half == kv else pltpu.roll(piece, 64, 1))
    return out


def _by_head(group, per_head):
    out = jnp.zeros((BLK, BLK), F32)
    for hd, v in enumerate(per_head):
        out = jnp.where(group == hd, v, out)
    return out


def _place(x, kv):
    if kv == 0:
        lo = jnp.where(_half_mask(128, 0), x, jnp.zeros_like(x))
        return lo, pltpu.roll(lo, 64, 1)
    hi = jnp.where(_half_mask(128, 1), x, jnp.zeros_like(x))
    return pltpu.roll(hi, 64, 1), hi


def _swa_fwd(sq, sk, sv, sinks, wn):
    rows = sq.shape[0]
    nb = rows // BLK
    per_step = _blocks_per_step(nb)
    scale = SWA_HD ** -0.5

    def body(q_ref, k_ref, v_ref, sink_ref, wn_ref, o_ref, cat_ref, lse_ref, kp, vp):
        step = pl.program_id(0)

        @pl.when(step == 0)
        def _():
            kp[...] = _swa_meta_operand(k_ref)
            vp[...] = _swa_meta_operand(v_ref)

        def one_block(c, carry):
            i = step * per_step + c
            rr = pl.ds(pl.multiple_of(c * BLK, BLK), BLK)
            own_side, band_ok, meta_ok, group = _swa_masks(i)
            k2, _, _ = _swa_blocks(k_ref, i)
            v2, _, _ = _swa_blocks(v_ref, i)
            kz = (_place(k2, 0), _place(k2, 1))
            vz = (_place(v2, 0), _place(v2, 1))
            q_all = q_ref[rr, :]
            s_meta = jnp.where(meta_ok, _nt(q_all, kp[...]) * scale, NEG_INF)
            s_band, m = [], []
            for hd in range(SWA_HEADS):
                kv, half = hd // 4, hd % 2
                q_pair = q_all[:, 128 * (hd // 2):128 * (hd // 2) + 128]
                s2 = _nt(q_pair, kz[kv][half])
                s = jnp.where(band_ok, jnp.where(own_side, s2[:, BLK:], s2[:, :BLK]) * scale, NEG_INF)
                top = jnp.maximum(jnp.max(s, axis=-1, keepdims=True),
                                  jnp.max(jnp.where(group == hd, s_meta, NEG_INF), axis=-1, keepdims=True))
                s_band.append(s)
                m.append(jnp.maximum(top, sink_ref[0, hd]))
            e_meta = jnp.exp(s_meta - _by_head(group, m))
            o_meta = _nn(e_meta.astype(BF16), vp[...])
            outs = []
            for pr in range(4):
                o_pair = o_meta[:, 128 * pr:128 * pr + 128]
                rden = []
                for half in range(2):
                    hd = 2 * pr + half
                    kv = hd // 4
                    e = jnp.exp(s_band[hd] - m[hd])
                    den = (jnp.sum(e, axis=-1, keepdims=True)
                           + jnp.sum(jnp.where(group == hd, e_meta, 0.0), axis=-1, keepdims=True)
                           + jnp.exp(sink_ref[0, hd] - m[hd]))
                    lse_ref[rr, hd:hd + 1] = m[hd] + jnp.log(den)
                    rden.append(1.0 / den)
                    e2 = jnp.concatenate([jnp.where(own_side, 0.0, e), jnp.where(own_side, e, 0.0)], axis=1).astype(BF16)
                    o_pair = o_pair + _nn(e2, vz[kv][half])
                outs.append(o_pair * jnp.where(_half_mask(128, 0), rden[0], rden[1]))
            o = jnp.concatenate(outs, axis=1)
            o_ref[rr, :] = o
            on, _ = _rms(o)
            cat_ref[rr, :] = (on * wn_ref[...]).astype(BF16)
            return carry
        return one_block, 0, lambda carry: None

    return dict(
        program=body, steps=nb // per_step, per_step=per_step,
        in_specs=[_row_spec(per_step * BLK, 512), VMEM_SPEC, VMEM_SPEC, SMEM_SPEC, VMEM_SPEC],
        out_specs=[_row_spec(per_step * BLK, 512), _row_spec(per_step * BLK, 512), _row_spec(per_step * BLK, SWA_HEADS)],
        out_shape=[jax.ShapeDtypeStruct((rows, 512), F32), jax.ShapeDtypeStruct((rows, 512), BF16),
                   jax.ShapeDtypeStruct((rows, SWA_HEADS), F32)],
        scratch_shapes=[pltpu.VMEM((BLK, 512), BF16), pltpu.VMEM((BLK, 512), BF16)],
        args=(sq, sk, sv, sinks, wn))


def _swa_bwd(dcat, o_all, sq, sk, sv, lse, sinks, wn):
    rows = sq.shape[0]
    nb = rows // BLK
    per_step = _blocks_per_step(nb)
    steps = nb // per_step
    scale = SWA_HD ** -0.5

    def body(dc_ref, o_ref, q_ref, k_ref, v_ref, lse_ref, sink_ref, wn_ref, dq_ref, dk_ref, dv_ref, dsink_ref, dwn_ref,
             kp, vp, dkp, dvp):
        step = pl.program_id(0)

        @pl.when(step == 0)
        def _():
            dk_ref[...] = jnp.zeros_like(dk_ref)
            dv_ref[...] = jnp.zeros_like(dv_ref)
            dkp[...] = jnp.zeros_like(dkp)
            dvp[...] = jnp.zeros_like(dvp)
            kp[...] = _swa_meta_operand(k_ref)
            vp[...] = _swa_meta_operand(v_ref)

        def one_block(c, carry):
            i = step * per_step + c
            rr = pl.ds(pl.multiple_of(c * BLK, BLK), BLK)
            first = i == 0
            own_side, band_ok, meta_ok, group = _swa_masks(i)
            k2, prev, own = _swa_blocks(k_ref, i)
            v2, _, _ = _swa_blocks(v_ref, i)
            kz = (_place(k2, 0), _place(k2, 1))
            vz = (_place(v2, 0), _place(v2, 1))
            o = o_ref[rr, :]
            on, ro = _rms(o)
            dc = dc_ref[rr, :]
            _acc_add(dwn_ref, first, _colsum(dc * on))
            do = _rms_bwd(on, ro, wn_ref[...], dc)
            do_o = do * o
            do16 = do.astype(BF16)
            q_all = q_ref[rr, :]
            lse = [lse_ref[rr, hd:hd + 1] for hd in range(SWA_HEADS)]
            delta = [jnp.sum(jnp.where(_half_mask(128, hd % 2), do_o[:, 128 * (hd // 2):128 * (hd // 2) + 128], 0.0),
                             axis=-1, keepdims=True) for hd in range(SWA_HEADS)]
            s_meta = jnp.where(meta_ok, _nt(q_all, kp[...]) * scale, NEG_INF)
            p_meta = jnp.exp(s_meta - _by_head(group, lse))
            ds_meta16 = (p_meta * (_nt(do16, vp[...]) - _by_head(group, delta)) * scale).astype(BF16)
            dq_meta = _nn(ds_meta16, kp[...])
            dkp[...] += _tn(ds_meta16, q_all)
            dvp[...] += _tn(p_meta.astype(BF16), do16)
            own2 = jnp.concatenate([own_side.astype(jnp.int32)] * 2, axis=0) > 0
            ok2 = jnp.concatenate([band_ok.astype(jnp.int32)] * 2, axis=0) > 0

            def window(x2):
                return jnp.where(own2, x2[:, BLK:], x2[:, :BLK])

            def unwindow(x):
                return jnp.concatenate([jnp.where(own2, 0.0, x), jnp.where(own2, x, 0.0)], axis=1).astype(BF16)
            lane8 = lax.broadcasted_iota(jnp.int32, (1, 128), 1)
            dsink = jnp.zeros((1, 128), F32)
            dq_pairs = [dq_meta[:, 128 * pr:128 * pr + 128] for pr in range(4)]
            dk2 = [[None, None], [None, None]]
            dv2 = [[None, None], [None, None]]
            for kv in range(2):
                for half in range(2):
                    heads, pairs = (4 * kv + half, 4 * kv + 2 + half), (2 * kv, 2 * kv + 1)
                    q_s = jnp.concatenate([q_all[:, 128 * pr:128 * pr + 128] for pr in pairs], axis=0)
                    do_s = jnp.concatenate([do16[:, 128 * pr:128 * pr + 128] for pr in pairs], axis=0)
                    lse_s = jnp.concatenate([lse[hd] for hd in heads], axis=0)
                    delta_s = jnp.concatenate([delta[hd] for hd in heads], axis=0)
                    s = jnp.where(ok2, window(_nt(q_s, kz[kv][half])) * scale, NEG_INF)
                    prob = jnp.exp(s - lse_s)
                    for hd in heads:
                        dsink = dsink + jnp.where(lane8 == hd, -jnp.sum(jnp.exp(sink_ref[0, hd] - lse[hd]) * delta[hd]), 0.0)
                    ds2 = unwindow(prob * (window(_nt(do_s, vz[kv][half])) - delta_s) * scale)
                    dq_s = _nn(ds2, kz[kv][half])
                    dq_pairs[pairs[0]] = dq_pairs[pairs[0]] + dq_s[:BLK]
                    dq_pairs[pairs[1]] = dq_pairs[pairs[1]] + dq_s[BLK:]
                    dk2[kv][half] = _tn(ds2, q_s)
                    dv2[kv][half] = _tn(unwindow(prob), do_s)
            dq_ref[rr, :] = jnp.concatenate(dq_pairs, axis=1)
            _acc_add(dsink_ref, first, dsink)
            for ref, acc2 in ((dk_ref, dk2), (dv_ref, dv2)):
                tot = jnp.zeros((2 * BLK, 128), F32)
                for kv in range(2):
                    for half in range(2):
                        part = jnp.where(_half_mask(128, half), acc2[kv][half], 0.0)
                        tot = tot + (part if half == kv else pltpu.roll(part, 64, 1))
                ref[pl.ds(prev, BLK), :] += tot[:BLK]
                ref[pl.ds(own, BLK), :] += tot[BLK:]
            return carry

        def finish(carry):
            del carry

            @pl.when(step == steps - 1)
            def _():
                dk_ref[PAD_ROWS:BLK, :] += _swa_meta_fold(dkp[...])
                dv_ref[PAD_ROWS:BLK, :] += _swa_meta_fold(dvp[...])
        return one_block, jnp.zeros((1, 128), F32), finish

    full = pl.BlockSpec((rows, 128), lambda i: (0, 0))
    blocks = lambda cols: _row_spec(per_step * BLK, cols)
    return dict(
        program=body, steps=steps, per_step=per_step,
        in_specs=[blocks(512), blocks(512), blocks(512), VMEM_SPEC, VMEM_SPEC, blocks(SWA_HEADS), SMEM_SPEC, VMEM_SPEC],
        out_specs=[blocks(512), full, full, _acc_spec(128), _acc_spec(512)],
        out_shape=[jax.ShapeDtypeStruct((rows, 512), F32), jax.ShapeDtypeStruct((rows, 128), F32),
                   jax.ShapeDtypeStruct((rows, 128), F32), jax.ShapeDtypeStruct((8, 128), F32),
                   jax.ShapeDtypeStruct((8, 512), F32)],
        scratch_shapes=[pltpu.VMEM((BLK, 512), BF16), pltpu.VMEM((BLK, 512), BF16),
                        pltpu.VMEM((BLK, 512), F32), pltpu.VMEM((BLK, 512), F32)],
        args=(dcat, o_all, sq, sk, sv, lse, sinks, wn))


def _mix_out_bwd(dh, m, wout, gpost):
    rows = dh.shape[0]
    tm = _row_tile(rows)

    def body(dh_ref, m_ref, w_ref, g_ref, dcg_ref, dcs_ref, dm_ref, dg_ref):
        first = pl.program_id(0) == 0
        dhv = dh_ref[...]
        mn, rm = _rms(m_ref[...])
        _acc_add(dg_ref, first, _colsum(dhv * mn))
        dm16 = _rms_bwd(mn, rm, g_ref[...], dhv).astype(BF16)
        dm_ref[...] = dm16
        dcat = _nt(dm16, w_ref[...])
        dcg_ref[...] = dcat[:, 0:512]
        dcs_ref[...] = dcat[:, 512:1024]

    row_f32 = _row_spec(tm, D_MODEL)
    return pl.pallas_call(
        body, name="mix_out_bwd", grid=(rows // tm,),
        in_specs=[row_f32, row_f32, VMEM_SPEC, VMEM_SPEC],
        out_specs=[_row_spec(tm, 512), _row_spec(tm, 512), row_f32, _acc_spec(D_MODEL)],
        out_shape=[jax.ShapeDtypeStruct((rows, 512), F32), jax.ShapeDtypeStruct((rows, 512), F32),
                   jax.ShapeDtypeStruct((rows, D_MODEL), BF16), jax.ShapeDtypeStruct((8, D_MODEL), F32)],
        compiler_params=_params(("arbitrary",)),
    )(dh, m, wout, gpost)


def _mix_in_bwd(dh_out, h, g, win_p, wa2_p, cos, sin, loga, ga, dgq, dgk, dgv, dgg, dsq, dsk, dsv, dloga):
    rows = h.shape[0]
    tm = _row_tile(rows)

    def body(dho_ref, h_ref, g_ref, win_ref, wa2_ref, cos_ref, sin_ref, loga_ref, ga_ref,
             dgq_ref, dgk_ref, dgv_ref, dgg_ref, dsq_ref, dsk_ref, dsv_ref, dla_ref,
             dh_ref, dproj_ref, dwa2_ref, dg_ref, dba_ref):
        first = pl.program_id(0) == 0
        dz = dla_ref[...] * (1.0 / GLA_TAU) * (1.0 - jnp.exp(GLA_TAU * loga_ref[...]))
        _acc_add(dba_ref, first, _colsum(dz))
        dga = _nt(dz, wa2_ref[...])
        pa = _tn(ga_ref[...], dz)
        c1, s1 = cos_ref[...], sin_ref[...]
        c4 = jnp.concatenate([c1, c1, c1, c1], axis=1)
        s4 = jnp.concatenate([s1, s1, s1, s1], axis=1)
        dq_r, dk_r = dsq_ref[...], dsk_ref[...]
        dsq = dq_r * c4 - _rot_half(dq_r * s4)
        dsk = dk_r * c1 - _rot_half(dk_r * s1)
        dproj16 = jnp.concatenate(
            [dgq_ref[...], dgk_ref[...], dgv_ref[...], dgg_ref[...], dsq, dsk, dsv_ref[...], dga], axis=1).astype(BF16)
        dproj_ref[...] = dproj16
        dn = _nn(dproj16, win_ref[...])

        @pl.when(first)
        def _():
            dwa2_ref[...] = pa

        @pl.when(jnp.logical_not(first))
        def _():
            dwa2_ref[...] += pa
        hn, rh = _rms(h_ref[...])
        _acc_add(dg_ref, first, _colsum(dn * hn))
        dh_ref[...] = dho_ref[...] + _rms_bwd(hn, rh, g_ref[...], dn)

    rs = lambda c: _row_spec(tm, c)
    return pl.pallas_call(
        body, name="mix_in_bwd", grid=(rows // tm,),
        in_specs=[rs(D_MODEL), rs(D_MODEL), VMEM_SPEC, VMEM_SPEC, VMEM_SPEC, rs(128), rs(128), rs(256), rs(128),
                  rs(256), rs(256), rs(512), rs(512), rs(512), rs(128), rs(128), rs(256)],
        out_specs=[rs(D_MODEL), rs(P_END), pl.BlockSpec((128, 256), lambda i: (0, 0)), _acc_spec(D_MODEL), _acc_spec(256)],
        out_shape=[jax.ShapeDtypeStruct((rows, D_MODEL), F32), jax.ShapeDtypeStruct((rows, P_END), BF16),
                   jax.ShapeDtypeStruct((128, 256), F32), jax.ShapeDtypeStruct((8, D_MODEL), F32),
                   jax.ShapeDtypeStruct((8, 256), F32)],
        compiler_params=_params(("arbitrary",)),
    )(dh_out, h, g, win_p, wa2_p, cos, sin, loga, ga, dgq, dgk, dgv, dgg, dsq, dsk, dsv, dloga)


def _rope_tables(rows):
    pos = (jnp.arange(rows, dtype=jnp.int32) - PAD_ROWS).astype(F32)
    inv_freq = 1.0 / (ROPE_THETA ** (jnp.arange(0, SWA_HD, 2, dtype=F32) / SWA_HD))
    ang = pos[:, None] * inv_freq[None, :]
    return jnp.tile(jnp.cos(ang), (1, 4)), jnp.tile(jnp.sin(ang), (1, 4))


def _local_step(x, tgt, front, w, late_weights=None, on_grads=None, on_small=None):
    cos, sin = _rope_tables(x.shape[0] + BLK)
    g = {}

    def tell(group, names):
        for nm in names:
            g[nm] = grads_now[nm]
        return None if on_grads is None else on_grads(group, {nm: grads_now[nm] for nm in names})

    h1, a1, b1, s1, f1 = _ffn_fwd(x, w["ffn1_pre"], w["wg1"], w["wu1"], w["wd1"], w["ffn1_post"], front=front)
    if late_weights is not None:
        w = {**w, **late_weights("win", f1)}
    gq, gk, gv, gg, sq, sk, sv, ga, loga, bc, n2 = _mix_in(h1, w["mix_pre"], w["win"], w["wa2"], w["b_a"], cos, sin)
    (o_g, cat_g, sp), (o_s, cat_s, lse) = _side_by_side(
        [_gla_fwd(gq, gk, gv, gg, bc, w["gla_norm"]), _swa_fwd(sq, sk, sv, w["sinks"], w["swa_norm"])], "attention_fwd")
    if late_weights is not None:
        w = {**w, **late_weights("rest", lse)}
    h2, m, a2, b2, s2, f2, dy, loss = _ffn_fwd(h1, w["ffn2_pre"], w["wg2"], w["wu2"], w["wd2"], w["ffn2_post"], tgt,
                                               mixed=(cat_g, cat_s, w["wout"], w["mix_post"]))
    dh2, da, db, df, n3, g["ffn2_pre"], g["ffn2_post"] = _ffn_bwd_act(
        dy, h2, a2, b2, f2, w["ffn2_pre"], w["ffn2_post"], w["wg2"], w["wu2"], w["wd2"], "ffn2_bwd_act")
    grads_now = dict(wd2=_wgrad(s2, df, "ffn2_wgrad_down"), wg2=_wgrad(da, n3, "ffn2_wgrad_gate"),
                     wu2=_wgrad(db, n3, "ffn2_wgrad_up"))
    tok = tell("ffn2", ("wd2", "wg2", "wu2"))
    dcg, dcs, dm, g["mix_post"] = _mix_out_bwd(dh2, m, w["wout"], w["mix_post"] + (0.0 if tok is None else tok[0, 0]))
    (dgq, dgk, dgv, dgg, dloga, g["gla_norm"]), (dsq, dsk, dsv, g["sinks"], g["swa_norm"]) = _side_by_side(
        [_gla_bwd(dcg, o_g, gq, gk, gv, gg, bc, sp, w["gla_norm"]),
         _swa_bwd(dcs, o_s, sq, sk, sv, lse, w["sinks"], w["swa_norm"])], "attention_bwd")
    dh1, dproj, g["wa2"], g["mix_pre"], g["b_a"] = _mix_in_bwd(
        dh2, h1, w["mix_pre"], w["win"], w["wa2"], cos, sin, loga, ga, dgq, dgk, dgv, dgg, dsq, dsk, dsv, dloga)
    dh0, da, db, df, n1, g["ffn1_pre"], g["ffn1_post"] = _ffn_bwd_act(
        dh1, x, a1, b1, f1, w["ffn1_pre"], w["ffn1_post"], w["wg1"], w["wu1"], w["wd1"], "ffn1_bwd_act", front=front)
    tok = None if on_small is None else on_small(loss[0, 0], dh0, g)
    grads_now = dict(wd1=_wgrad(s1, df, "ffn1_wgrad_down", after=tok))
    tok = tell("ffn1_down", ("wd1",))
    grads_now = dict(wg1=_wgrad(da, n1, "ffn1_wgrad_gate", after=tok))
    tok = tell("ffn1_gate", ("wg1",))
    grads_now = dict(wu1=_wgrad(db, n1, "ffn1_wgrad_up", after=tok))
    tok = tell("ffn1_up", ("wu1",))
    grads_now = dict(win=_wgrad(dproj, n2, "win_wgrad", after=tok),
                     wout=jnp.concatenate([_wgrad(cat_g, dm, "wout_wgrad_gla", after=tok),
                                           _wgrad(cat_s, dm, "wout_wgrad_swa", after=tok)], axis=0))
    tell("mix", ("wout", "win"))
    return loss[0, 0], dh0, g


def _win_pad_rows(win_t):
    pad = jnp.zeros((P_END - P_GA - 16, win_t.shape[1]), win_t.dtype)
    return jnp.concatenate([win_t[0:1536], win_t[1552:2320], win_t[1536:1552], pad], axis=0)


def _win_unpad_rows(win_p):
    return jnp.concatenate([win_p[0:1536], win_p[P_GA:P_GA + 16], win_p[1536:P_GA]], axis=0)


def _place_on_mesh():
    return lax.axis_index("x"), lax.axis_index("y"), lax.axis_index("c")


def _dev_index(px, py, pc):
    return 4 * px + 2 * py + pc


def _other_devices(x, y, c):
    flip = lambda v, f: 1 - v if f else v
    return [(flip(x, fx), flip(y, fy), flip(c, fc)) for fx in (0, 1) for fy in (0, 1) for fc in (0, 1)][1:]


def _all_gather(shards):
    n = len(shards)

    def body(*refs):
        ins, outs = refs[:n], refs[n:2 * n]
        zeros_ref, send_sems, recv_sems, local_sems = refs[2 * n:]
        zeros_ref[...] = jnp.zeros_like(zeros_ref)
        x, y, c = _place_on_mesh()
        me, sibling = (x, y, c), (x, y, 1 - c)
        chips = [(1 - x, y), (x, 1 - y), (1 - x, 1 - y)]

        def rows(k, px, py, pc):
            r = ins[k].shape[0]
            return outs[k].at[pl.ds(pl.multiple_of(_dev_index(px, py, pc) * r, 8), r), :]

        def copy(k, slot, block, to, src=None):
            return pltpu.make_async_remote_copy(
                src_ref=rows(k, *block) if src is None else src, dst_ref=rows(k, *block),
                send_sem=send_sems.at[k, slot], recv_sem=recv_sems.at[k, slot], device_id=to, device_id_type=MESH)

        local = [pltpu.make_async_copy(ins[k], rows(k, *me), local_sems.at[k]) for k in range(n)]
        sends = []
        for k in range(n):
            local[k].start()
            sends.append(copy(k, 0, me, sibling, src=ins[k]))
            sends += [copy(k, 1 + j, me, (*chip, c), src=ins[k]) for j, chip in enumerate(chips)]
        for cp in sends:
            cp.start()
        for k in range(n):
            for j, chip in enumerate(chips):
                copy(k, 1 + j, (*chip, c), me).wait_recv()
                passed = copy(k, 4 + j, (*chip, c), sibling)
                passed.start()
                sends.append(passed)
        for k in range(n):
            copy(k, 0, sibling, me).wait_recv()
            for j, chip in enumerate(chips):
                copy(k, 4 + j, (*chip, 1 - c), me).wait_recv()
        for cp in sends:
            cp.wait_send()
        for cp in local:
            cp.wait()

    return pl.pallas_call(
        body, name="all_gather_weights",
        in_specs=[ANY_SPEC] * n, out_specs=[ANY_SPEC] * n + [VMEM_SPEC],
        out_shape=[jax.ShapeDtypeStruct((N_DEV * s.shape[0], s.shape[1]), s.dtype) for s in shards]
        + [jax.ShapeDtypeStruct((8, 128), F32)],
        scratch_shapes=[pltpu.SemaphoreType.DMA((n, 7)), pltpu.SemaphoreType.DMA((n, 7)), pltpu.SemaphoreType.DMA((n,))],
    )(*shards)


HBM_SPEC = pl.BlockSpec(memory_space=pltpu.HBM)
SEM_SPEC = pl.BlockSpec(memory_space=pltpu.SEMAPHORE)
DATAFLOW = pltpu.SideEffectType.DATAFLOW_SIDE_EFFECTING


GATHER, SCATTER, SCATTER_CHIPS = "gather", "scatter", "scatter among chips"


def _exchange_peers(kind):
    x, y, c = _place_on_mesh()
    if kind == SCATTER_CHIPS:
        peers = [(1 - x, y, c), (x, 1 - y, c), (1 - x, 1 - y, c)]
        return peers, [2 * p[0] + p[1] for p in peers], 2 * x + y, 4
    peers = _other_devices(x, y, c)
    return peers, [_dev_index(*p) for p in peers], _dev_index(x, y, c), N_DEV


def _exchange_copies(srcs, lands, send_sems, recv_sems, own_sems, kind, arriving):
    peers, theirs, me, blocks = _exchange_peers(kind)
    remote, local = [], []
    for k, (src, land) in enumerate(zip(srcs, lands)):
        r = land.shape[0] // blocks

        def block(ref, d):
            return ref.at[pl.ds(pl.multiple_of(d * r, 8), r), :]

        for f, (peer, him) in enumerate(zip(peers, theirs)):
            mine, his = (him, me) if arriving else (me, him)
            sem = len(peers) * k + f
            remote.append(pltpu.make_async_remote_copy(
                src_ref=src if kind == GATHER else block(src, his), dst_ref=block(land, mine),
                send_sem=send_sems.at[sem], recv_sem=recv_sems.at[sem], device_id=peer, device_id_type=MESH))
        local.append(pltpu.make_async_copy(src if kind == GATHER else block(src, me), block(land, me), own_sems.at[k]))
    return remote, local


def _exchange_start(srcs, kind, name):
    n = len(srcs)
    lands = [lax.empty((N_DEV * s.shape[0], s.shape[1]) if kind == GATHER else s.shape, s.dtype) for s in srcs]
    sems = (3 if kind == SCATTER_CHIPS else 7) * n

    def body(*refs):
        remote, local = _exchange_copies(refs[:n], refs[n:2 * n], *refs[2 * n:2 * n + 3], kind, False)
        for cp in remote + local:
            cp.start()
        refs[-1][...] = jnp.zeros_like(refs[-1])

    both = list(srcs) + list(lands)
    outs = pl.pallas_call(
        body, name=name,
        out_shape=(pltpu.SemaphoreType.DMA((sems,)), pltpu.SemaphoreType.DMA((sems,)), pltpu.SemaphoreType.DMA((n,)),
                   *[pltpu.HBM(a.shape, a.dtype) for a in both], jax.ShapeDtypeStruct((8, 128), F32)),
        in_specs=[HBM_SPEC] * (2 * n), out_specs=(SEM_SPEC, SEM_SPEC, SEM_SPEC, *[HBM_SPEC] * (2 * n), VMEM_SPEC),
        input_output_aliases={i: 3 + i for i in range(2 * n)},
        compiler_params=pltpu.CompilerParams(has_side_effects=DATAFLOW),
    )(*[pltpu.with_memory_space_constraint(a, pltpu.HBM) for a in both])
    return outs[0:3], outs[3:3 + n], outs[3 + n:3 + 2 * n], outs[-1]


def _exchange_wait(started, kind, after, name):
    sems, srcs, lands, _ = started
    n = len(srcs)

    def body(*refs):
        args = (refs[:n], refs[n:2 * n], *refs[2 * n:2 * n + 3], kind)
        going, local = _exchange_copies(*args, False)
        for cp in going:
            cp.wait_send()
        for cp in local:
            cp.wait()
        for cp in _exchange_copies(*args, True)[0]:
            cp.wait_recv()

    both = list(srcs) + list(lands)
    outs = pl.pallas_call(
        body, name=name, out_shape=[pltpu.HBM(a.shape, a.dtype) for a in both],
        in_specs=[HBM_SPEC] * (2 * n) + [SEM_SPEC, SEM_SPEC, SEM_SPEC, ANY_SPEC], out_specs=[HBM_SPEC] * (2 * n),
        input_output_aliases={i: i for i in range(2 * n)},
        compiler_params=pltpu.CompilerParams(has_side_effects=DATAFLOW),
    )(*both, *sems, after)
    return outs[n:]


def _sibling_reduce(part, name):
    r, cols = part.shape[0] // N_DEV, part.shape[1]

    def body(p_ref, o_ref, mine, got, send_sems, recv_sems, own_sems):
        x, y, c = _place_on_mesh()

        def block(d):
            return p_ref.at[pl.ds(pl.multiple_of(d * r, 8), r), :]
        swaps = [pltpu.make_async_remote_copy(
            src_ref=block(2 * j + 1 - c), dst_ref=got.at[j], send_sem=send_sems.at[j], recv_sem=recv_sems.at[j],
            device_id=(x, y, 1 - c), device_id_type=MESH) for j in range(4)]
        keeps = [pltpu.make_async_copy(block(2 * j + c), mine.at[j], own_sems.at[j]) for j in range(4)]
        for cp in swaps + keeps:
            cp.start()
        for j in range(4):
            keeps[j].wait()
            swaps[j].wait()
            o_ref[pl.ds(j * r, r), :] = (mine[j].astype(F32) + got[j].astype(F32)).astype(o_ref.dtype)

    return pl.pallas_call(
        body, name=name, in_specs=[ANY_SPEC], out_specs=VMEM_SPEC,
        out_shape=jax.ShapeDtypeStruct((4 * r, cols), part.dtype),
        scratch_shapes=[pltpu.VMEM((4, r, cols), part.dtype), pltpu.VMEM((4, r, cols), part.dtype),
                        pltpu.SemaphoreType.DMA((4,)), pltpu.SemaphoreType.DMA((4,)), pltpu.SemaphoreType.DMA((4,))],
        compiler_params=pltpu.CompilerParams(vmem_limit_bytes=32 << 20),
    )(part)


def _sum_partials(parts, name, blocks=N_DEV):
    n = len(parts)

    def body(*refs):
        ins, outs = refs[:n], refs[n:]
        first = pl.program_id(0) == 0
        for i_ref, o_ref in zip(ins, outs):
            v = i_ref[...].astype(F32)

            @pl.when(first)
            def _():
                o_ref[...] = v

            @pl.when(jnp.logical_not(first))
            def _():
                o_ref[...] += v

    shapes = [(p.shape[0] // blocks, p.shape[1]) for p in parts]
    return pl.pallas_call(
        body, name=name, grid=(blocks,),
        in_specs=[pl.BlockSpec(s, lambda j: (j, 0)) for s in shapes],
        out_specs=[pl.BlockSpec(s, lambda j: (0, 0)) for s in shapes],
        out_shape=[jax.ShapeDtypeStruct(s, F32) for s in shapes],
        compiler_params=_params(("arbitrary",)),
    )(*parts)


def _adamw_update(w, g, m, v):
    m = ADAM_B1 * m + (1.0 - ADAM_B1) * g
    v = ADAM_B2 * v + (1.0 - ADAM_B2) * (g * g)
    m_hat = m * (1.0 / (1.0 - ADAM_B1 ** ADAM_STEP))
    v_hat = v * (1.0 / (1.0 - ADAM_B2 ** ADAM_STEP))
    return -ADAM_LR * (m_hat / (jnp.sqrt(v_hat) + ADAM_EPS) + ADAM_WD * w), m, v


def _sum_adamw(parts, w, m, v, blocks, name):
    shape = w.shape

    def body(p_ref, w_ref, m_ref, v_ref, g_ref, d_ref, mo_ref, vo_ref):
        j = pl.program_id(0)
        part = p_ref[...].astype(F32)

        @pl.when(j == 0)
        def _():
            g_ref[...] = part

        @pl.when(j > 0)
        def _():
            g_ref[...] += part

        @pl.when(j == blocks - 1)
        def _():
            d_ref[...], mo_ref[...], vo_ref[...] = _adamw_update(w_ref[...], g_ref[...], m_ref[...], v_ref[...])

    held = pl.BlockSpec(shape, lambda j: (0, 0))
    return pl.pallas_call(
        body, name=name, grid=(blocks,),
        in_specs=[pl.BlockSpec(shape, lambda j: (j, 0)), held, held, held],
        out_specs=[held] * 4, out_shape=[jax.ShapeDtypeStruct(shape, F32)] * 4,
        compiler_params=_params(("arbitrary",)),
    )(parts, w, m, v)


def _adamw(ws, gs, ms, vs, name):
    n = len(ws)

    def body(*refs):
        w_r, g_r, m_r, v_r = refs[:n], refs[n:2 * n], refs[2 * n:3 * n], refs[3 * n:4 * n]
        d_o, m_o, v_o = refs[4 * n:5 * n], refs[5 * n:6 * n], refs[6 * n:7 * n]
        for k in range(n):
            d_o[k][...], m_o[k][...], v_o[k][...] = _adamw_update(w_r[k][...], g_r[k][...], m_r[k][...], v_r[k][...])

    shapes = [jax.ShapeDtypeStruct(w.shape, F32) for w in ws]
    outs = pl.pallas_call(
        body, name=name, in_specs=[VMEM_SPEC] * (4 * n), out_specs=[VMEM_SPEC] * (3 * n), out_shape=shapes * 3,
        compiler_params=pltpu.CompilerParams(vmem_limit_bytes=56 << 20),
    )(*ws, *gs, *ms, *vs)
    return outs[:n], outs[n:2 * n], outs[2 * n:]


WEIGHT_NAMES = ("meta_tokens", "ffn1_pre_norm", "ffn1_w_gate", "ffn1_w_up", "ffn1_w_down", "ffn1_post_norm", "mix_pre_norm",
                "w_in", "gla_w_a2", "gla_b_a", "gla_out_norm", "swa_sinks", "swa_out_norm", "w_out", "mix_post_norm",
                "ffn2_pre_norm", "ffn2_w_gate", "ffn2_w_up", "ffn2_w_down", "ffn2_post_norm")
WIN_SHARD = D_IN // N_DEV
WIN_SHARD_PAD = 304
SLAB_VECTORS = ("ffn1_pre", "ffn1_post", "mix_pre", "mix_post", "ffn2_pre", "ffn2_post")
SLAB_ROWS = 32


def kernel(x, meta_tokens, ffn1_pre_norm, ffn1_w_gate, ffn1_w_up, ffn1_w_down, ffn1_post_norm, mix_pre_norm, w_in, gla_w_a2, gla_b_a, gla_out_norm, swa_sinks, swa_out_norm, w_out, mix_post_norm, ffn2_pre_norm, ffn2_w_gate, ffn2_w_up, ffn2_w_down, ffn2_post_norm, loss_target, m_meta_tokens, m_ffn1_pre_norm, m_ffn1_w_gate, m_ffn1_w_up, m_ffn1_w_down, m_ffn1_post_norm, m_mix_pre_norm, m_w_in, m_gla_w_a2, m_gla_b_a, m_gla_out_norm, m_swa_sinks, m_swa_out_norm, m_w_out, m_mix_post_norm, m_ffn2_pre_norm, m_ffn2_w_gate, m_ffn2_w_up, m_ffn2_w_down, m_ffn2_post_norm, v_meta_tokens, v_ffn1_pre_norm, v_ffn1_w_gate, v_ffn1_w_up, v_ffn1_w_down, v_ffn1_post_norm, v_mix_pre_norm, v_w_in, v_gla_w_a2, v_gla_b_a, v_gla_out_norm, v_swa_sinks, v_swa_out_norm, v_w_out, v_mix_post_norm, v_ffn2_pre_norm, v_ffn2_w_gate, v_ffn2_w_up, v_ffn2_w_down, v_ffn2_post_norm):
    given = dict(locals())
    W = {n: given[n] for n in WEIGHT_NAMES}
    M = {n: given["m_" + n] for n in WEIGHT_NAMES}
    V = {n: given["v_" + n] for n in WEIGHT_NAMES}
    dev = _dev_index(*_place_on_mesh())

    def t16(w):
        return w[0].T.astype(BF16)

    small = jnp.concatenate([W["meta_tokens"], jnp.pad(W["gla_w_a2"][0], ((0, 0), (0, 96)))], axis=0)
    wg1, wu1, wd1, small_g, gathered_zeros = _all_gather(
        [t16(W["ffn1_w_gate"]), t16(W["ffn1_w_up"]), W["ffn1_w_down"][0].astype(BF16), small])
    def after_zero(shard, zeros):
        return shard + zeros[0:1, 0:1].astype(shard.dtype)
    win_shard = jnp.pad(t16(W["w_in"]), ((0, WIN_SHARD_PAD - WIN_SHARD), (0, 0)))
    win_shard = after_zero(win_shard, gathered_zeros)
    mid = _exchange_start([win_shard], GATHER, "gather_w_in_start")
    late_shards = [after_zero(W["w_out"][0].astype(BF16), mid[3]), t16(W["ffn2_w_gate"]), t16(W["ffn2_w_up"]),
                   W["ffn2_w_down"][0].astype(BF16)]
    late = _exchange_start(late_shards, GATHER, "gather_late_weights_start")

    def late_weights(what, after):
        if what == "win":
            win_g, = _exchange_wait(mid, GATHER, after, "gather_w_in_wait")
            win_t = win_g.reshape(N_DEV, WIN_SHARD_PAD, D_MODEL)[:, :WIN_SHARD].reshape(D_IN, D_MODEL)
            return dict(win=_win_pad_rows(win_t))
        wout, wg2, wu2, wd2 = _exchange_wait(late, GATHER, after, "gather_late_weights_wait")
        return dict(wout=wout, wg2=wg2, wu2=wu2, wd2=wd2)

    small_g = small_g.reshape(N_DEV, 32, 128)
    meta_full = small_g[:, :N_META].transpose(1, 0, 2).reshape(N_META, D_MODEL)
    wa2_full = small_g[:, N_META:, :32].transpose(1, 0, 2).reshape(16, 256)
    w = dict(
        ffn1_pre=W["ffn1_pre_norm"] + late[3][0, 0], ffn1_post=W["ffn1_post_norm"], mix_pre=W["mix_pre_norm"],
        mix_post=W["mix_post_norm"], ffn2_pre=W["ffn2_pre_norm"], ffn2_post=W["ffn2_post_norm"], b_a=W["gla_b_a"],
        gla_norm=W["gla_out_norm"], sinks=W["swa_sinks"], swa_norm=W["swa_out_norm"], wg1=wg1, wu1=wu1, wd1=wd1,
        wa2=jnp.pad(wa2_full, ((0, 112), (0, 0))))

    in_flight = []

    def on_grads(group, grads):
        parts = []
        for nm, p in grads.items():
            if nm == "win":
                p = _win_unpad_rows(p).reshape(N_DEV, WIN_SHARD, D_MODEL)
                p = jnp.pad(p, ((0, 0), (0, WIN_SHARD_PAD - WIN_SHARD), (0, 0))).reshape(N_DEV * WIN_SHARD_PAD, D_MODEL)
            parts.append(p)
        kind = SCATTER_CHIPS if group == "mix" else SCATTER
        if kind == SCATTER_CHIPS:
            parts = [_sibling_reduce(p, "pair_" + group + "_" + nm) for nm, p in zip(grads, parts)]
        started = _exchange_start(parts, kind, "scatter_" + group + "_start")
        in_flight.append((group, list(grads), started, kind))
        return started[3]

    small_flight = []

    def on_small(loss, dh0, g):
        packed = jnp.concatenate([g["b_a"][0:1], g["gla_norm"][0:1], g["sinks"][0:1], g["swa_norm"][0:1]], axis=1)
        slab = jnp.concatenate([g[k][0:1] for k in SLAB_VECTORS] + [packed, jnp.full((1, D_MODEL), loss, F32),
                               g["wa2"][:16].reshape(4, D_MODEL), jnp.zeros((4, D_MODEL), F32), dh0[PAD_ROWS:BLK]], axis=0)
        small_flight.append(_exchange_start([slab], GATHER, "gather_small_grads_start"))
        return small_flight[0][3]

    front = jnp.concatenate([jnp.zeros((PAD_ROWS, D_MODEL), F32), meta_full], axis=0)
    loss, dh0, g = _local_step(x[0], loss_target[0], front, w, late_weights, on_grads, on_small)
    grad_x = dh0[BLK:][None]

    land, = _exchange_wait(small_flight[0], GATHER, in_flight[-1][2][3], "gather_small_grads_wait")
    tot = _sum_partials([land], "sum_small_grads")[0]
    loss = tot[7, 0]
    small_grads = dict(
        ffn1_pre_norm=tot[0:1], ffn1_post_norm=tot[1:2], mix_pre_norm=tot[2:3], mix_post_norm=tot[3:4],
        ffn2_pre_norm=tot[4:5], ffn2_post_norm=tot[5:6], gla_b_a=tot[6:7, 0:256], gla_out_norm=tot[6:7, 256:384],
        swa_sinks=tot[6:7, 384:392], swa_out_norm=tot[6:7, 512:1024],
        gla_w_a2=lax.dynamic_slice_in_dim(tot[8:12].reshape(16, 256), dev * 32, 32, axis=1)[None],
        meta_tokens=lax.dynamic_slice_in_dim(tot[16:32], dev * 128, 128, axis=1))

    big = dict(wg1=("ffn1_w_gate", True), wu1=("ffn1_w_up", True), wd1=("ffn1_w_down", False), win=("w_in", True),
               wout=("w_out", False), wg2=("ffn2_w_gate", True), wu2=("ffn2_w_up", True), wd2=("ffn2_w_down", False))
    grads = dict(small_grads)
    delta, new_m, new_v = {}, {}, {}
    names = [n for n in WEIGHT_NAMES if n not in [full for full, _ in big.values()]]
    two_d = lambda a: a.reshape(-1, a.shape[-1])
    d_, m_, v_ = _adamw([two_d(W[n]) for n in names], [two_d(grads[n]) for n in names],
                        [two_d(M[n]) for n in names], [two_d(V[n]) for n in names], "adamw_small")
    for k, n in enumerate(names):
        delta[n], new_m[n], new_v[n] = d_[k].reshape(W[n].shape), m_[k].reshape(W[n].shape), v_[k].reshape(W[n].shape)

    before_wait = d_[0] + in_flight[-1][2][3][0, 0]
    for group, shorts, started, kind in in_flight:
        lands = _exchange_wait(started, kind, before_wait, "scatter_" + group + "_wait")
        blocks = 4 if kind == SCATTER_CHIPS else N_DEV
        for short, land in zip(shorts, lands):
            n, transposed = big[short]
            to_slab = (lambda a: a[0].T) if transposed else (lambda a: a[0])
            from_slab = (lambda a: a.T[None]) if transposed else (lambda a: a[None])
            if short == "win":
                g_slab = _sum_partials([land], "sum_" + n, blocks)[0][:WIN_SHARD]
                d_, m_, v_ = _adamw([to_slab(W[n])], [g_slab], [to_slab(M[n])], [to_slab(V[n])], "adamw_" + n)
                d_, m_, v_ = d_[0], m_[0], v_[0]
            else:
                g_slab, d_, m_, v_ = _sum_adamw(land, to_slab(W[n]), to_slab(M[n]), to_slab(V[n]), blocks, "adamw_" + n)
            grads[n], delta[n], new_m[n], new_v[n] = from_slab(g_slab), from_slab(d_), from_slab(m_), from_slab(v_)
            before_wait = d_
    return (loss, grad_x, *[grads[n] for n in WEIGHT_NAMES], *[delta[n] for n in WEIGHT_NAMES],
            *[new_m[n] for n in WEIGHT_NAMES], *[new_v[n] for n in WEIGHT_NAMES])
```

```python
import math

import jax
import jax.numpy as jnp
from jax import lax
from jax.experimental import pallas as pl
from jax.experimental.pallas import tpu as pltpu

F32, BF16 = jnp.float32, jnp.bfloat16

D_MODEL = 1024
D_FF = 2816
N_META = 16
BLK = 128
PAD_ROWS = BLK - N_META
GLA_DK = 64
SWA_HD = 64
SWA_HEADS = 8
GLA_TAU = 16.0
NORM_EPS = 1e-6
NEG_INF = -1e30
ROPE_THETA = 10000.0
P_GQ, P_GK, P_GV, P_GG, P_SQ, P_SK, P_SV, P_GA, P_END = 0, 256, 512, 1024, 1536, 2048, 2176, 2304, 2432
D_IN = 2320
IN_SPLITS = (256, 256, 512, 512, 16, 512, 128, 128)
FF_TILE = 2816
WGRAD_TILE_MAX = 2432
N_DEV = 8
MESH = pl.DeviceIdType.MESH

ADAM_LR, ADAM_B1, ADAM_B2, ADAM_EPS, ADAM_WD, ADAM_STEP = 0.001, 0.9, 0.999, 1e-08, 0.01, 10

V7X_VMEM_BYTES = 64 << 20
VMEM_SPEC = pl.BlockSpec(memory_space=pltpu.VMEM)
SMEM_SPEC = pl.BlockSpec(memory_space=pltpu.SMEM)
ANY_SPEC = pl.BlockSpec(memory_space=pl.ANY)


def _params(semantics, vmem_mb=56):
    return pltpu.CompilerParams(dimension_semantics=semantics, vmem_limit_bytes=vmem_mb << 20)


def _row_tile(rows):
    return 416 if rows % 416 == 0 else BLK


def _blocks_per_step(blocks):
    return 5 if blocks % 5 == 0 else 1


def _nn(a, b):
    return lax.dot_general(a, b, (((1,), (0,)), ((), ())), preferred_element_type=F32)


def _nt(a, b):
    return lax.dot_general(a, b, (((1,), (1,)), ((), ())), preferred_element_type=F32)


def _tn(a, b):
    return lax.dot_general(a, b, (((0,), (0,)), ((), ())), preferred_element_type=F32)


def _rms(x):
    r = lax.rsqrt(jnp.mean(x * x, axis=-1, keepdims=True) + NORM_EPS)
    return x * r, r


def _rms_bwd(xn, r, w, dy):
    g = dy * w
    return r * (g - xn * jnp.mean(g * xn, axis=-1, keepdims=True))


def _sigmoid(x):
    return 1.0 / (1.0 + jnp.exp(-x))


def _colsum(x):
    return jnp.sum(x, axis=0, keepdims=True)


def _split_bf16(x):
    hi = x.astype(BF16)
    lo = (x - hi.astype(F32)).astype(BF16)
    return hi, lo


def _tri(lower):
    r = lax.broadcasted_iota(jnp.int32, (BLK, BLK), 0)
    c = lax.broadcasted_iota(jnp.int32, (BLK, BLK), 1)
    return (r >= c) if lower else (c >= r)


def _half_mask(width, half):
    lane = lax.broadcasted_iota(jnp.int32, (1, width), 1)
    return ((lane % 128) < 64) if half == 0 else ((lane % 128) >= 64)


def _rot_half(x):
    w = x.shape[-1]
    lane = lax.broadcasted_iota(jnp.int32, (1, w), 1)
    return jnp.where((lane % SWA_HD) < SWA_HD // 2, -pltpu.roll(x, w - SWA_HD // 2, 1), pltpu.roll(x, SWA_HD // 2, 1))


def _row_spec(tm, cols):
    return pl.BlockSpec((tm, cols), lambda i: (i, 0))


def _acc_spec(cols):
    return pl.BlockSpec((8, cols), lambda i: (0, 0))


def _acc_add(ref, first, value):
    @pl.when(first)
    def _():
        ref[...] = jnp.zeros_like(ref)
    ref[0:1, :] += value


def _behind_spec(tm):
    return pl.BlockSpec((pl.Element(tm), pl.Element(D_MODEL)),
                        lambda i: (pl.multiple_of(jnp.maximum(i * tm - BLK, 0), math.gcd(tm, BLK)), 0))


def _behind_front(ref, i, tm, front):
    blk = ref[...]
    return jnp.where(i == 0, jnp.concatenate([front, blk[0:tm - BLK]], axis=0), blk)


def _ffn_fwd(h, gpre, wg_t, wu_t, wd, gpost, tgt=None, front=None, mixed=None):
    with_loss, with_front, with_mixed = tgt is not None, front is not None, mixed is not None
    rows = h.shape[0] + (BLK if with_front else 0)
    tm = _row_tile(rows)
    nf = D_FF // FF_TILE

    def body(*refs):
        refs = list(refs)
        h_ref, gpre_ref, wg_ref, wu_ref, wd_ref, gpost_ref = refs[:6]
        del refs[:6]
        front_ref = refs.pop(0) if with_front else None
        cg_ref, cs_ref, wo_ref, gm_ref = (refs.pop(0), refs.pop(0), refs.pop(0), refs.pop(0)) if with_mixed else (None,) * 4
        t_ref = refs.pop(0) if with_loss else None
        hm_ref, m_ref = (refs.pop(0), refs.pop(0)) if with_mixed else (None, None)
        ho_ref = None if with_loss else refs.pop(0)
        a_ref, b_ref, s_ref, f_ref = refs[:4]
        dy_ref, loss_ref = refs[4:6] if with_loss else (None, None)
        acc = refs[-1]
        i = pl.program_id(0)
        h_in = _behind_front(h_ref, i, tm, front_ref[...]) if with_front else h_ref[...]
        if with_mixed:
            m = _nn(cg_ref[...], wo_ref[0:512, :]) + _nn(cs_ref[...], wo_ref[512:1024, :])
            m_ref[...] = m
            mn, _ = _rms(m)
            h_in = h_in + mn * gm_ref[...]
            hm_ref[...] = h_in
        hn, _ = _rms(h_in)
        n16 = (hn * gpre_ref[...]).astype(BF16)
        for j in range(nf):
            cols = slice(j * FF_TILE, (j + 1) * FF_TILE)
            a = _nt(n16, wg_ref[cols, :])
            b = _nt(n16, wu_ref[cols, :])
            a_ref[:, cols] = a.astype(BF16)
            b_ref[:, cols] = b.astype(BF16)
            s16 = (a * _sigmoid(a) * b).astype(BF16)
            s_ref[:, cols] = s16
            part = _nn(s16, wd_ref[cols, :])
            if j == 0:
                acc[...] = part
            else:
                acc[...] += part
        f = acc[...]
        f_ref[...] = f
        fn, _ = _rms(f)
        y = h_in + 0.5 * (fn * gpost_ref[...])
        if not with_loss:
            ho_ref[...] = y
        else:
            row = i * tm + lax.broadcasted_iota(jnp.int32, (tm, 1), 0)
            err = jnp.where(row >= BLK, y - _behind_front(t_ref, i, tm, jnp.zeros((BLK, D_MODEL), F32)), 0.0)
            dy_ref[...] = err * (1.0 / D_MODEL)
            part = 0.5 * jnp.sum(jnp.sum(err * err, axis=-1, keepdims=True) * (1.0 / D_MODEL), axis=0, keepdims=True)

            @pl.when(i == 0)
            def _():
                loss_ref[...] = jnp.zeros_like(loss_ref)
            loss_ref[...] += part

    row_f32 = _row_spec(tm, D_MODEL)
    behind = _behind_spec(tm)
    in_specs = [behind if with_front else row_f32, VMEM_SPEC, VMEM_SPEC, VMEM_SPEC, VMEM_SPEC, VMEM_SPEC]
    wide, full = jax.ShapeDtypeStruct((rows, D_FF), BF16), jax.ShapeDtypeStruct((rows, D_MODEL), F32)
    out_specs = [_row_spec(tm, D_FF), _row_spec(tm, D_FF), _row_spec(tm, D_FF), row_f32]
    out_shape = [wide, wide, wide, full]
    args = [h, gpre, wg_t, wu_t, wd, gpost]
    if not with_loss:
        out_specs.insert(0, row_f32)
        out_shape.insert(0, full)
    if with_front:
        in_specs.append(VMEM_SPEC)
        args.append(front)
    if with_mixed:
        in_specs += [_row_spec(tm, 512), _row_spec(tm, 512), VMEM_SPEC, VMEM_SPEC]
        args += list(mixed)
        out_specs = [row_f32, row_f32] + out_specs
        out_shape = [full, full] + out_shape
    if with_loss:
        in_specs.append(behind)
        args.append(tgt)
        out_specs += [row_f32, pl.BlockSpec((8, 128), lambda i: (0, 0))]
        out_shape += [jax.ShapeDtypeStruct((rows, D_MODEL), F32), jax.ShapeDtypeStruct((8, 128), F32)]
    return pl.pallas_call(
        body, name="ffn_fwd_loss" if with_loss else "ffn_fwd", grid=(rows // tm,),
        in_specs=in_specs, out_specs=out_specs, out_shape=out_shape,
        scratch_shapes=[pltpu.VMEM((tm, D_MODEL), F32)],
        compiler_params=_params(("arbitrary",), vmem_mb=62 if with_mixed else 56),
    )(*args)


def _ffn_bwd_act(dh_out, h, a, b, f, gpre, gpost, wg_t, wu_t, wd, name, front=None):
    with_front = front is not None
    rows = dh_out.shape[0]
    tm = _row_tile(rows)
    nf = D_FF // FF_TILE

    def body(dho_ref, h_ref, a_ref, b_ref, f_ref, gpre_ref, gpost_ref, wg_ref, wu_ref, wd_ref, *rest):
        front_ref = rest[0] if with_front else None
        dh_ref, da_ref, db_ref, df_ref, n_ref, dgpre_ref, dgpost_ref, acc = rest[-8:]
        first = pl.program_id(0) == 0
        dho = dho_ref[...]
        drr = 0.5 * dho
        fn, rf = _rms(f_ref[...])
        _acc_add(dgpost_ref, first, _colsum(drr * fn))
        df16 = _rms_bwd(fn, rf, gpost_ref[...], drr).astype(BF16)
        df_ref[...] = df16
        h_in = _behind_front(h_ref, pl.program_id(0), tm, front_ref[...]) if with_front else h_ref[...]
        hn, rh = _rms(h_in)
        n_ref[...] = (hn * gpre_ref[...]).astype(BF16)
        for j in range(nf):
            cols = slice(j * FF_TILE, (j + 1) * FF_TILE)
            ds = _nt(df16, wd_ref[cols, :])
            av = a_ref[:, cols].astype(F32)
            bv = b_ref[:, cols].astype(F32)
            sg = _sigmoid(av)
            db16 = (ds * (av * sg)).astype(BF16)
            da16 = (ds * bv * (sg * (1.0 + av * (1.0 - sg)))).astype(BF16)
            da_ref[:, cols] = da16
            db_ref[:, cols] = db16
            part = _nn(da16, wg_ref[cols, :]) + _nn(db16, wu_ref[cols, :])
            if j == 0:
                acc[...] = part
            else:
                acc[...] += part
        dn = acc[...]
        _acc_add(dgpre_ref, first, _colsum(dn * hn))
        dh_ref[...] = dho + _rms_bwd(hn, rh, gpre_ref[...], dn)

    row_f32 = _row_spec(tm, D_MODEL)
    row_ff = _row_spec(tm, D_FF)
    return pl.pallas_call(
        body, name=name, grid=(rows // tm,),
        in_specs=[row_f32, _behind_spec(tm) if with_front else row_f32, row_ff, row_ff, row_f32,
                  VMEM_SPEC, VMEM_SPEC, VMEM_SPEC, VMEM_SPEC, VMEM_SPEC] + ([VMEM_SPEC] if with_front else []),
        out_specs=[row_f32, row_ff, row_ff, row_f32, row_f32, _acc_spec(D_MODEL), _acc_spec(D_MODEL)],
        out_shape=[jax.ShapeDtypeStruct((rows, D_MODEL), F32), jax.ShapeDtypeStruct((rows, D_FF), BF16),
                   jax.ShapeDtypeStruct((rows, D_FF), BF16), jax.ShapeDtypeStruct((rows, D_MODEL), BF16),
                   jax.ShapeDtypeStruct((rows, D_MODEL), BF16), jax.ShapeDtypeStruct((8, D_MODEL), F32),
                   jax.ShapeDtypeStruct((8, D_MODEL), F32)],
        scratch_shapes=[pltpu.VMEM((tm, D_MODEL), F32)],
        compiler_params=_params(("arbitrary",), vmem_mb=62),
    )(dh_out, h, a, b, f, gpre, gpost, wg_t, wu_t, wd, *([front] if with_front else []))


def _wgrad(lhs, rhs, name, after=None):
    rows, width = lhs.shape
    tf = 256 if width % 256 == 0 else 128
    pieces = 5 if rows % 80 == 0 else 3 if rows % 48 == 0 else 1
    piece = rows // pieces
    tiles, slots = width // tf, 3

    def body(l_hbm, r_hbm, *rest):
        o_ref, l_buf, r_all, l_sems, r_sems = rest[-5:]
        j = pl.program_id(0)

        def fetch_r(c):
            part = pl.ds(c * piece, piece)
            return pltpu.make_async_copy(r_hbm.at[part, :], r_all.at[part, :], r_sems.at[c])

        def fetch_l(tile):
            slot = tile % slots
            start = tile * tf if isinstance(tile, int) else pl.multiple_of(tile * tf, 128)
            return pltpu.make_async_copy(l_hbm.at[:, pl.ds(start, tf)], l_buf.at[slot], l_sems.at[slot])

        @pl.when(j == 0)
        def _():
            for tile in range(min(slots - 1, tiles)):
                fetch_l(tile).start()
            for c in range(pieces):
                fetch_r(c).start()

        @pl.when(j + slots - 1 < tiles)
        def _():
            fetch_l(j + slots - 1).start()
        fetch_l(j).wait()
        lhs_tile = l_buf.at[j % slots]

        @pl.when(j == 0)
        def _():
            total = None
            for c in range(pieces):
                fetch_r(c).wait()
                part = _tn(lhs_tile[c * piece:(c + 1) * piece, :], r_all[c * piece:(c + 1) * piece, :])
                total = part if total is None else total + part
            o_ref[...] = total.astype(BF16)

        @pl.when(j > 0)
        def _():
            o_ref[...] = _tn(lhs_tile[...], r_all[...]).astype(BF16)

    return pl.pallas_call(
        body, name=name, grid=(tiles,),
        in_specs=[ANY_SPEC, ANY_SPEC] + ([] if after is None else [ANY_SPEC]),
        out_specs=pl.BlockSpec((tf, D_MODEL), lambda j: (j, 0)),
        out_shape=jax.ShapeDtypeStruct((width, D_MODEL), BF16),
        scratch_shapes=[pltpu.VMEM((slots, rows, tf), BF16), pltpu.VMEM((rows, D_MODEL), BF16),
                        pltpu.SemaphoreType.DMA((slots,)), pltpu.SemaphoreType.DMA((pieces,))],
        compiler_params=_params(("arbitrary",)),
    )(lhs, rhs, *([] if after is None else [after]))


def _chunk_cumsum(x, lower):
    tri = jnp.where(_tri(lower), 1.0, 0.0).astype(BF16)
    hi, lo = _split_bf16(x)
    return _nn(tri, hi) + _nn(tri, lo)


def _mix_in(h, g, win_p, wa2_p, b_a, cos, sin):
    rows = h.shape[0]
    tm = 640 if rows % 640 == 0 else BLK

    def body(h_ref, g_ref, win_ref, wa2_ref, ba_ref, cos_ref, sin_ref,
             gq_ref, gk_ref, gv_ref, gg_ref, sq_ref, sk_ref, sv_ref, ga_ref, loga_ref, bc_ref, n_ref):
        hn, _ = _rms(h_ref[...])
        n16 = (hn * g_ref[...]).astype(BF16)
        n_ref[...] = n16
        proj = _nt(n16, win_ref[...])
        gq_ref[...] = proj[:, P_GQ:P_GK]
        gk_ref[...] = proj[:, P_GK:P_GV]
        gv_ref[...] = proj[:, P_GV:P_GG].astype(BF16)
        gg_ref[...] = proj[:, P_GG:P_SQ]
        c1, s1 = cos_ref[...], sin_ref[...]
        c4 = jnp.concatenate([c1, c1, c1, c1], axis=1)
        s4 = jnp.concatenate([s1, s1, s1, s1], axis=1)
        sq = proj[:, P_SQ:P_SK]
        sk = proj[:, P_SK:P_SV]
        sq_ref[...] = (sq * c4 + _rot_half(sq) * s4).astype(BF16)
        sk_ref[...] = (sk * c1 + _rot_half(sk) * s1).astype(BF16)
        sv_ref[...] = proj[:, P_SV:P_GA].astype(BF16)
        ga = proj[:, P_GA:P_END]
        ga_ref[...] = ga
        z = _nn(ga, wa2_ref[...]) + ba_ref[...]
        loga = (jnp.minimum(z, 0.0) - jnp.log(1.0 + jnp.exp(-jnp.abs(z)))) * (1.0 / GLA_TAU)
        loga_ref[...] = loga
        for c in range(tm // BLK):
            rs = slice(c * BLK, (c + 1) * BLK)
            bc_ref[rs, :] = _chunk_cumsum(loga[rs, :], True)

    f32 = lambda c: jax.ShapeDtypeStruct((rows, c), F32)
    b16 = lambda c: jax.ShapeDtypeStruct((rows, c), BF16)
    rs = lambda c: _row_spec(tm, c)
    return pl.pallas_call(
        body, name="mix_in", grid=(rows // tm,),
        in_specs=[rs(D_MODEL), VMEM_SPEC, VMEM_SPEC, VMEM_SPEC, VMEM_SPEC, rs(128), rs(128)],
        out_specs=[rs(256), rs(256), rs(512), rs(512), rs(512), rs(128), rs(128), rs(128), rs(256), rs(256), rs(D_MODEL)],
        out_shape=[f32(256), f32(256), b16(512), f32(512), b16(512), b16(128), b16(128), f32(128), f32(256), f32(256),
                   b16(D_MODEL)],
        compiler_params=_params(("arbitrary",)),
    )(h, g, win_p, wa2_p, b_a, cos, sin)


def _side_by_side(parts, name):
    steps, per_step = parts[0]["steps"], parts[0]["per_step"]
    assert all((p["steps"], p["per_step"]) == (steps, per_step) for p in parts)
    counts = [[len(p[key]) for p in parts] for key in ("in_specs", "out_specs", "scratch_shapes")]

    def body(*refs):
        groups, pos = [], 0
        for kind in counts:
            groups.append([])
            for n in kind:
                groups[-1].append(refs[pos:pos + n])
                pos += n
        programs = [p["program"](*groups[0][k], *groups[1][k], *groups[2][k]) for k, p in enumerate(parts)]

        def blocks(c, carries):
            return tuple(block(c, carry) for (block, _, _), carry in zip(programs, carries))
        carries = lax.fori_loop(0, per_step, blocks, tuple(first for _, first, _ in programs))
        for (_, _, finish), carry in zip(programs, carries):
            finish(carry)

    outs = pl.pallas_call(
        body, name=name, grid=(steps,),
        in_specs=[s for p in parts for s in p["in_specs"]], out_specs=[s for p in parts for s in p["out_specs"]],
        out_shape=[s for p in parts for s in p["out_shape"]],
        scratch_shapes=[s for p in parts for s in p["scratch_shapes"]],
        compiler_params=_params(("arbitrary",)),
    )(*[a for p in parts for a in p["args"]])
    split, pos = [], 0
    for n in counts[1]:
        split.append(outs[pos:pos + n])
        pos += n
    return split


def _gla_factors(q, k, bc):
    bm = bc[BLK // 2 - 1:BLK // 2, :]
    bl = bc[BLK - 1:BLK, :]
    e_q, e_k, e_qe, e_kd = jnp.exp(bc - bm), jnp.exp(bm - bc), jnp.exp(bc), jnp.exp(bl - bc)
    return (q * e_q, k * e_k, q * e_qe, k * e_kd), (e_q, e_k, e_qe, e_kd), jnp.exp(bl)


def _gla_fwd(gq, gk, gv, gg, bc, wgn):
    rows = gq.shape[0]
    nc = rows // BLK
    per_step = _blocks_per_step(nc)
    scale = GLA_DK ** -0.5

    def body(q_ref, k_ref, v_ref, gg_ref, bc_ref, wgn_ref, o_ref, cat_ref, sp_ref, st):
        @pl.when(pl.program_id(0) == 0)
        def _():
            st[...] = jnp.zeros_like(st)
        low = _tri(True)
        wgn_v = wgn_ref[...]

        def chunk(c, carry):
            rr = pl.ds(pl.multiple_of(c * BLK, BLK), BLK)
            for p in range(2):
                sl = slice(128 * p, 128 * p + 128)
                (qt, kt, qe, kd), _, ebl = _gla_factors(q_ref[rr, sl] * scale, k_ref[rr, sl], bc_ref[rr, sl])
                s_prev = st[p]
                sp_ref[c, p] = s_prev
                s16 = s_prev.astype(BF16)
                qt16 = qt.astype(BF16)
                s_new = s_prev * ebl
                for hh in range(2):
                    hs = slice(128 * (2 * p + hh), 128 * (2 * p + hh) + 128)
                    lm = _half_mask(128, hh)
                    vh = v_ref[rr, hs]
                    pm = jnp.where(low, _nt(qt16, jnp.where(lm, kt, 0.0).astype(BF16)), 0.0)
                    o = _nn(pm.astype(BF16), vh) + _nt(jnp.where(lm, qe, 0.0).astype(BF16), s16)
                    s_new = s_new + _tn(vh, jnp.where(lm, kd, 0.0).astype(BF16))
                    o_ref[rr, hs] = o
                    on, _ = _rms(o)
                    gate = gg_ref[rr, hs]
                    cat_ref[rr, hs] = (on * wgn_v * (gate * _sigmoid(gate))).astype(BF16)
                st[p] = s_new
            return carry
        return chunk, 0, lambda carry: None

    rs = lambda c: _row_spec(per_step * BLK, c)
    return dict(
        program=body, steps=nc // per_step, per_step=per_step,
        in_specs=[rs(256), rs(256), rs(512), rs(512), rs(256), VMEM_SPEC],
        out_specs=[rs(512), rs(512), pl.BlockSpec((per_step, 2, 128, 128), lambda i: (i, 0, 0, 0))],
        out_shape=[jax.ShapeDtypeStruct((rows, 512), F32), jax.ShapeDtypeStruct((rows, 512), BF16),
                   jax.ShapeDtypeStruct((nc, 2, 128, 128), F32)],
        scratch_shapes=[pltpu.VMEM((2, 128, 128), F32)],
        args=(gq, gk, gv, gg, bc, wgn))


def _gla_bwd(dcat, o_all, gq, gk, gv, gg, bc, sp, wgn):
    rows = gq.shape[0]
    nc = rows // BLK
    per_step = _blocks_per_step(nc)
    steps = nc // per_step
    scale = GLA_DK ** -0.5

    def body(dc_ref, o_ref, q_ref, k_ref, v_ref, gg_ref, bc_ref, sp_ref, wgn_ref,
             dq_ref, dk_ref, dv_ref, dgg_ref, dla_ref, dwgn_ref, dst):
        first = pl.program_id(0) == 0

        @pl.when(first)
        def _():
            dst[...] = jnp.zeros_like(dst)
        low, upp = _tri(True), _tri(False)
        last_row = lax.broadcasted_iota(jnp.int32, (BLK, 1), 0) == BLK - 1
        wgn_v = wgn_ref[...]

        def chunk(c, dwgn):
            rr = pl.ds(pl.multiple_of((per_step - 1 - c) * BLK, BLK), BLK)
            for p in range(2):
                sl = slice(128 * p, 128 * p + 128)
                (qt, kt, qe, kd), (e_q, e_k, e_qe, e_kd), ebl = _gla_factors(
                    q_ref[rr, sl] * scale, k_ref[rr, sl], bc_ref[rr, sl])
                s_prev = sp_ref[per_step - 1 - c, p]
                s16 = s_prev.astype(BF16)
                ds_next = dst[p]
                ds16 = ds_next.astype(BF16)
                qt16 = qt.astype(BF16)
                ds_new = ds_next * ebl
                dqt = jnp.zeros((BLK, 128), F32)
                dkt = jnp.zeros((BLK, 128), F32)
                dqe = jnp.zeros((BLK, 128), F32)
                dkd = jnp.zeros((BLK, 128), F32)
                for hh in range(2):
                    hs = slice(128 * (2 * p + hh), 128 * (2 * p + hh) + 128)
                    lm = _half_mask(128, hh)
                    on, ro = _rms(o_ref[rr, hs])
                    gate = gg_ref[rr, hs]
                    sg = _sigmoid(gate)
                    si = gate * sg
                    dog = dc_ref[rr, hs]
                    dwgn = dwgn + _colsum(dog * si * on)
                    dgg_ref[rr, hs] = dog * (on * wgn_v) * (sg * (1.0 + gate * (1.0 - sg)))
                    do16 = _rms_bwd(on, ro, wgn_v, dog * si).astype(BF16)
                    vh = v_ref[rr, hs]
                    ktm16 = jnp.where(lm, kt, 0.0).astype(BF16)
                    qtm16 = jnp.where(lm, qt, 0.0).astype(BF16)
                    qem16 = jnp.where(lm, qe, 0.0).astype(BF16)
                    kdm16 = jnp.where(lm, kd, 0.0).astype(BF16)
                    p_t = jnp.where(upp, _nt(ktm16, qt16), 0.0)
                    dp_t = jnp.where(upp, _nt(vh, do16), 0.0)
                    dp = jnp.where(low, _nt(do16, vh), 0.0)
                    dv_ref[rr, hs] = _nn(p_t.astype(BF16), do16) + _nt(kdm16, ds16)
                    dqt = dqt + _nn(dp.astype(BF16), ktm16)
                    dkt = dkt + _nn(dp_t.astype(BF16), qtm16)
                    dqe = dqe + jnp.where(lm, _nn(do16, s16), 0.0)
                    dkd = dkd + jnp.where(lm, _nn(vh, ds16), 0.0)
                    ds_new = ds_new + _tn(do16, qem16)
                debl = _colsum(ds_next * s_prev)
                dq_ref[rr, sl] = (dqt * e_q + dqe * e_qe) * scale
                dk_ref[rr, sl] = dkt * e_k + dkd * e_kd
                dkd_kd = dkd * kd
                db = dqt * qt - dkt * kt + dqe * qe - dkd_kd
                db = jnp.where(last_row, db + (_colsum(dkd_kd) + debl * ebl), db)
                dla_ref[rr, sl] = _chunk_cumsum(db, False)
                dst[p] = ds_new
            return dwgn

        def finish(dwgn):
            _acc_add(dwgn_ref, first, dwgn)
        return chunk, jnp.zeros((1, 128), F32), finish

    rev = lambda c: pl.BlockSpec((per_step * BLK, c), lambda i: (steps - 1 - i, 0))
    f32 = lambda c: jax.ShapeDtypeStruct((rows, c), F32)
    return dict(
        program=body, steps=steps, per_step=per_step,
        in_specs=[rev(512), rev(512), rev(256), rev(256), rev(512), rev(512), rev(256),
                  pl.BlockSpec((per_step, 2, 128, 128), lambda i: (steps - 1 - i, 0, 0, 0)), VMEM_SPEC],
        out_specs=[rev(256), rev(256), rev(512), rev(512), rev(256), _acc_spec(128)],
        out_shape=[f32(256), f32(256), f32(512), f32(512), f32(256), jax.ShapeDtypeStruct((8, 128), F32)],
        scratch_shapes=[pltpu.VMEM((2, 128, 128), F32)],
        args=(dcat, o_all, gq, gk, gv, gg, bc, sp, wgn))


def _swa_masks(i):
    t = lax.broadcasted_iota(jnp.int32, (BLK, BLK), 0)
    c = lax.broadcasted_iota(jnp.int32, (BLK, BLK), 1)
    own_side = c <= t
    band_ok = i >= jnp.where(own_side, 1, 2)
    meta_ok = (c % N_META) <= jnp.where(i >= 1, N_META, t - PAD_ROWS)
    return own_side, band_ok, meta_ok, c // N_META


def _swa_blocks(ref, i):
    prev = pl.multiple_of(jnp.maximum(i - 1, 0) * BLK, BLK)
    own = pl.multiple_of(i * BLK, BLK)
    return jnp.concatenate([ref[pl.ds(prev, BLK), :], ref[pl.ds(own, BLK), :]], axis=0), prev, own


def _swa_meta_operand(ref):
    blk = ref[0:BLK, :]
    swapped = pltpu.roll(blk, 64, 1)
    lo = jnp.where(_half_mask(128, 0), blk, swapped)
    hi = jnp.where(_half_mask(128, 1), blk, swapped)
    meta = jnp.concatenate([lo, lo, hi, hi], axis=1)[PAD_ROWS:BLK, :]
    tiled = jnp.concatenate([meta] * SWA_HEADS, axis=0)
    j = lax.broadcasted_iota(jnp.int32, tiled.shape, 0)
    lane = lax.broadcasted_iota(jnp.int32, tiled.shape, 1)
    return jnp.where(j // N_META == lane // SWA_HD, tiled, jnp.zeros_like(tiled))


def _swa_meta_fold(acc):
    out = jnp.zeros((N_META, 128), F32)
    for hd in range(SWA_HEADS):
        half, kv = hd % 2, hd // 4
        piece = acc[N_META * hd:N_META * (hd + 1), 128 * (hd // 2):128 * (hd // 2) + 128]
        piece = jnp.where(_half_mask(128, half), piece, 0.0)
        out = out + (piece if half == kv else pltpu.roll(piece, 64, 1))
    return out


def _by_head(group, per_head):
    out = jnp.zeros((BLK, BLK), F32)
    for hd, v in enumerate(per_head):
        out = jnp.where(group == hd, v, out)
    return out


def _place(x, kv):
    if kv == 0:
        lo = jnp.where(_half_mask(128, 0), x, jnp.zeros_like(x))
        return lo, pltpu.roll(lo, 64, 1)
    hi = jnp.where(_half_mask(128, 1), x, jnp.zeros_like(x))
    return pltpu.roll(hi, 64, 1), hi


def _swa_fwd(sq, sk, sv, sinks, wn):
    rows = sq.shape[0]
    nb = rows // BLK
    per_step = _blocks_per_step(nb)
    scale = SWA_HD ** -0.5

    def body(q_ref, k_ref, v_ref, sink_ref, wn_ref, o_ref, cat_ref, lse_ref, kp, vp):
        step = pl.program_id(0)

        @pl.when(step == 0)
        def _():
            kp[...] = _swa_meta_operand(k_ref)
            vp[...] = _swa_meta_operand(v_ref)

        def one_block(c, carry):
            i = step * per_step + c
            rr = pl.ds(pl.multiple_of(c * BLK, BLK), BLK)
            own_side, band_ok, meta_ok, group = _swa_masks(i)
            k2, _, _ = _swa_blocks(k_ref, i)
            v2, _, _ = _swa_blocks(v_ref, i)
            kz = (_place(k2, 0), _place(k2, 1))
            vz = (_place(v2, 0), _place(v2, 1))
            q_all = q_ref[rr, :]
            s_meta = jnp.where(meta_ok, _nt(q_all, kp[...]) * scale, NEG_INF)
            s_band, m = [], []
            for hd in range(SWA_HEADS):
                kv, half = hd // 4, hd % 2
                q_pair = q_all[:, 128 * (hd // 2):128 * (hd // 2) + 128]
                s2 = _nt(q_pair, kz[kv][half])
                s = jnp.where(band_ok, jnp.where(own_side, s2[:, BLK:], s2[:, :BLK]) * scale, NEG_INF)
                top = jnp.maximum(jnp.max(s, axis=-1, keepdims=True),
                                  jnp.max(jnp.where(group == hd, s_meta, NEG_INF), axis=-1, keepdims=True))
                s_band.append(s)
                m.append(jnp.maximum(top, sink_ref[0, hd]))
            e_meta = jnp.exp(s_meta - _by_head(group, m))
            o_meta = _nn(e_meta.astype(BF16), vp[...])
            outs = []
            for pr in range(4):
                o_pair = o_meta[:, 128 * pr:128 * pr + 128]
                rden = []
                for half in range(2):
                    hd = 2 * pr + half
                    kv = hd // 4
                    e = jnp.exp(s_band[hd] - m[hd])
                    den = (jnp.sum(e, axis=-1, keepdims=True)
                           + jnp.sum(jnp.where(group == hd, e_meta, 0.0), axis=-1, keepdims=True)
                           + jnp.exp(sink_ref[0, hd] - m[hd]))
                    lse_ref[rr, hd:hd + 1] = m[hd] + jnp.log(den)
                    rden.append(1.0 / den)
                    e2 = jnp.concatenate([jnp.where(own_side, 0.0, e), jnp.where(own_side, e, 0.0)], axis=1).astype(BF16)
                    o_pair = o_pair + _nn(e2, vz[kv][half])
                outs.append(o_pair * jnp.where(_half_mask(128, 0), rden[0], rden[1]))
            o = jnp.concatenate(outs, axis=1)
            o_ref[rr, :] = o
            on, _ = _rms(o)
            cat_ref[rr, :] = (on * wn_ref[...]).astype(BF16)
            return carry
        return one_block, 0, lambda carry: None

    return dict(
        program=body, steps=nb // per_step, per_step=per_step,
        in_specs=[_row_spec(per_step * BLK, 512), VMEM_SPEC, VMEM_SPEC, SMEM_SPEC, VMEM_SPEC],
        out_specs=[_row_spec(per_step * BLK, 512), _row_spec(per_step * BLK, 512), _row_spec(per_step * BLK, SWA_HEADS)],
        out_shape=[jax.ShapeDtypeStruct((rows, 512), F32), jax.ShapeDtypeStruct((rows, 512), BF16),
                   jax.ShapeDtypeStruct((rows, SWA_HEADS), F32)],
        scratch_shapes=[pltpu.VMEM((BLK, 512), BF16), pltpu.VMEM((BLK, 512), BF16)],
        args=(sq, sk, sv, sinks, wn))


def _swa_bwd(dcat, o_all, sq, sk, sv, lse, sinks, wn):
    rows = sq.shape[0]
    nb = rows // BLK
    per_step = _blocks_per_step(nb)
    steps = nb // per_step
    scale = SWA_HD ** -0.5

    def body(dc_ref, o_ref, q_ref, k_ref, v_ref, lse_ref, sink_ref, wn_ref, dq_ref, dk_ref, dv_ref, dsink_ref, dwn_ref,
             kp, vp, dkp, dvp):
        step = pl.program_id(0)

        @pl.when(step == 0)
        def _():
            dk_ref[...] = jnp.zeros_like(dk_ref)
            dv_ref[...] = jnp.zeros_like(dv_ref)
            dkp[...] = jnp.zeros_like(dkp)
            dvp[...] = jnp.zeros_like(dvp)
            kp[...] = _swa_meta_operand(k_ref)
            vp[...] = _swa_meta_operand(v_ref)

        def one_block(c, carry):
            i = step * per_step + c
            rr = pl.ds(pl.multiple_of(c * BLK, BLK), BLK)
            first = i == 0
            own_side, band_ok, meta_ok, group = _swa_masks(i)
            k2, prev, own = _swa_blocks(k_ref, i)
            v2, _, _ = _swa_blocks(v_ref, i)
            kz = (_place(k2, 0), _place(k2, 1))
            vz = (_place(v2, 0), _place(v2, 1))
            o = o_ref[rr, :]
            on, ro = _rms(o)
            dc = dc_ref[rr, :]
            _acc_add(dwn_ref, first, _colsum(dc * on))
            do = _rms_bwd(on, ro, wn_ref[...], dc)
            do_o = do * o
            do16 = do.astype(BF16)
            q_all = q_ref[rr, :]
            lse = [lse_ref[rr, hd:hd + 1] for hd in range(SWA_HEADS)]
            delta = [jnp.sum(jnp.where(_half_mask(128, hd % 2), do_o[:, 128 * (hd // 2):128 * (hd // 2) + 128], 0.0),
                             axis=-1, keepdims=True) for hd in range(SWA_HEADS)]
            s_meta = jnp.where(meta_ok, _nt(q_all, kp[...]) * scale, NEG_INF)
            p_meta = jnp.exp(s_meta - _by_head(group, lse))
            ds_meta16 = (p_meta * (_nt(do16, vp[...]) - _by_head(group, delta)) * scale).astype(BF16)
            dq_meta = _nn(ds_meta16, kp[...])
            dkp[...] += _tn(ds_meta16, q_all)
            dvp[...] += _tn(p_meta.astype(BF16), do16)
            own2 = jnp.concatenate([own_side.astype(jnp.int32)] * 2, axis=0) > 0
            ok2 = jnp.concatenate([band_ok.astype(jnp.int32)] * 2, axis=0) > 0

            def window(x2):
                return jnp.where(own2, x2[:, BLK:], x2[:, :BLK])

            def unwindow(x):
                return jnp.concatenate([jnp.where(own2, 0.0, x), jnp.where(own2, x, 0.0)], axis=1).astype(BF16)
            lane8 = lax.broadcasted_iota(jnp.int32, (1, 128), 1)
            dsink = jnp.zeros((1, 128), F32)
            dq_pairs = [dq_meta[:, 128 * pr:128 * pr + 128] for pr in range(4)]
            dk2 = [[None, None], [None, None]]
            dv2 = [[None, None], [None, None]]
            for kv in range(2):
                for half in range(2):
                    heads, pairs = (4 * kv + half, 4 * kv + 2 + half), (2 * kv, 2 * kv + 1)
                    q_s = jnp.concatenate([q_all[:, 128 * pr:128 * pr + 128] for pr in pairs], axis=0)
                    do_s = jnp.concatenate([do16[:, 128 * pr:128 * pr + 128] for pr in pairs], axis=0)
                    lse_s = jnp.concatenate([lse[hd] for hd in heads], axis=0)
                    delta_s = jnp.concatenate([delta[hd] for hd in heads], axis=0)
                    s = jnp.where(ok2, window(_nt(q_s, kz[kv][half])) * scale, NEG_INF)
                    prob = jnp.exp(s - lse_s)
                    for hd in heads:
                        dsink = dsink + jnp.where(lane8 == hd, -jnp.sum(jnp.exp(sink_ref[0, hd] - lse[hd]) * delta[hd]), 0.0)
                    ds2 = unwindow(prob * (window(_nt(do_s, vz[kv][half])) - delta_s) * scale)
                    dq_s = _nn(ds2, kz[kv][half])
                    dq_pairs[pairs[0]] = dq_pairs[pairs[0]] + dq_s[:BLK]
                    dq_pairs[pairs[1]] = dq_pairs[pairs[1]] + dq_s[BLK:]
                    dk2[kv][half] = _tn(ds2, q_s)
                    dv2[kv][half] = _tn(unwindow(prob), do_s)
            dq_ref[rr, :] = jnp.concatenate(dq_pairs, axis=1)
            _acc_add(dsink_ref, first, dsink)
            for ref, acc2 in ((dk_ref, dk2), (dv_ref, dv2)):
                tot = jnp.zeros((2 * BLK, 128), F32)
                for kv in range(2):
                    for half in range(2):
                        part = jnp.where(_half_mask(128, half), acc2[kv][half], 0.0)
                        tot = tot + (part if half == kv else pltpu.roll(part, 64, 1))
                ref[pl.ds(prev, BLK), :] += tot[:BLK]
                ref[pl.ds(own, BLK), :] += tot[BLK:]
            return carry

        def finish(carry):
            del carry

            @pl.when(step == steps - 1)
            def _():
                dk_ref[PAD_ROWS:BLK, :] += _swa_meta_fold(dkp[...])
                dv_ref[PAD_ROWS:BLK, :] += _swa_meta_fold(dvp[...])
        return one_block, jnp.zeros((1, 128), F32), finish

    full = pl.BlockSpec((rows, 128), lambda i: (0, 0))
    blocks = lambda cols: _row_spec(per_step * BLK, cols)
    return dict(
        program=body, steps=steps, per_step=per_step,
        in_specs=[blocks(512), blocks(512), blocks(512), VMEM_SPEC, VMEM_SPEC, blocks(SWA_HEADS), SMEM_SPEC, VMEM_SPEC],
        out_specs=[blocks(512), full, full, _acc_spec(128), _acc_spec(512)],
        out_shape=[jax.ShapeDtypeStruct((rows, 512), F32), jax.ShapeDtypeStruct((rows, 128), F32),
                   jax.ShapeDtypeStruct((rows, 128), F32), jax.ShapeDtypeStruct((8, 128), F32),
                   jax.ShapeDtypeStruct((8, 512), F32)],
        scratch_shapes=[pltpu.VMEM((BLK, 512), BF16), pltpu.VMEM((BLK, 512), BF16),
                        pltpu.VMEM((BLK, 512), F32), pltpu.VMEM((BLK, 512), F32)],
        args=(dcat, o_all, sq, sk, sv, lse, sinks, wn))


def _mix_out_bwd(dh, m, wout, gpost):
    rows = dh.shape[0]
    tm = _row_tile(rows)

    def body(dh_ref, m_ref, w_ref, g_ref, dcg_ref, dcs_ref, dm_ref, dg_ref):
        first = pl.program_id(0) == 0
        dhv = dh_ref[...]
        mn, rm = _rms(m_ref[...])
        _acc_add(dg_ref, first, _colsum(dhv * mn))
        dm16 = _rms_bwd(mn, rm, g_ref[...], dhv).astype(BF16)
        dm_ref[...] = dm16
        dcat = _nt(dm16, w_ref[...])
        dcg_ref[...] = dcat[:, 0:512]
        dcs_ref[...] = dcat[:, 512:1024]

    row_f32 = _row_spec(tm, D_MODEL)
    return pl.pallas_call(
        body, name="mix_out_bwd", grid=(rows // tm,),
        in_specs=[row_f32, row_f32, VMEM_SPEC, VMEM_SPEC],
        out_specs=[_row_spec(tm, 512), _row_spec(tm, 512), row_f32, _acc_spec(D_MODEL)],
        out_shape=[jax.ShapeDtypeStruct((rows, 512), F32), jax.ShapeDtypeStruct((rows, 512), F32),
                   jax.ShapeDtypeStruct((rows, D_MODEL), BF16), jax.ShapeDtypeStruct((8, D_MODEL), F32)],
        compiler_params=_params(("arbitrary",)),
    )(dh, m, wout, gpost)


def _mix_in_bwd(dh_out, h, g, win_p, wa2_p, cos, sin, loga, ga, dgq, dgk, dgv, dgg, dsq, dsk, dsv, dloga):
    rows = h.shape[0]
    tm = _row_tile(rows)

    def body(dho_ref, h_ref, g_ref, win_ref, wa2_ref, cos_ref, sin_ref, loga_ref, ga_ref,
             dgq_ref, dgk_ref, dgv_ref, dgg_ref, dsq_ref, dsk_ref, dsv_ref, dla_ref,
             dh_ref, dproj_ref, dwa2_ref, dg_ref, dba_ref):
        first = pl.program_id(0) == 0
        dz = dla_ref[...] * (1.0 / GLA_TAU) * (1.0 - jnp.exp(GLA_TAU * loga_ref[...]))
        _acc_add(dba_ref, first, _colsum(dz))
        dga = _nt(dz, wa2_ref[...])
        pa = _tn(ga_ref[...], dz)
        c1, s1 = cos_ref[...], sin_ref[...]
        c4 = jnp.concatenate([c1, c1, c1, c1], axis=1)
        s4 = jnp.concatenate([s1, s1, s1, s1], axis=1)
        dq_r, dk_r = dsq_ref[...], dsk_ref[...]
        dsq = dq_r * c4 - _rot_half(dq_r * s4)
        dsk = dk_r * c1 - _rot_half(dk_r * s1)
        dproj16 = jnp.concatenate(
            [dgq_ref[...], dgk_ref[...], dgv_ref[...], dgg_ref[...], dsq, dsk, dsv_ref[...], dga], axis=1).astype(BF16)
        dproj_ref[...] = dproj16
        dn = _nn(dproj16, win_ref[...])

        @pl.when(first)
        def _():
            dwa2_ref[...] = pa

        @pl.when(jnp.logical_not(first))
        def _():
            dwa2_ref[...] += pa
        hn, rh = _rms(h_ref[...])
        _acc_add(dg_ref, first, _colsum(dn * hn))
        dh_ref[...] = dho_ref[...] + _rms_bwd(hn, rh, g_ref[...], dn)

    rs = lambda c: _row_spec(tm, c)
    return pl.pallas_call(
        body, name="mix_in_bwd", grid=(rows // tm,),
        in_specs=[rs(D_MODEL), rs(D_MODEL), VMEM_SPEC, VMEM_SPEC, VMEM_SPEC, rs(128), rs(128), rs(256), rs(128),
                  rs(256), rs(256), rs(512), rs(512), rs(512), rs(128), rs(128), rs(256)],
        out_specs=[rs(D_MODEL), rs(P_END), pl.BlockSpec((128, 256), lambda i: (0, 0)), _acc_spec(D_MODEL), _acc_spec(256)],
        out_shape=[jax.ShapeDtypeStruct((rows, D_MODEL), F32), jax.ShapeDtypeStruct((rows, P_END), BF16),
                   jax.ShapeDtypeStruct((128, 256), F32), jax.ShapeDtypeStruct((8, D_MODEL), F32),
                   jax.ShapeDtypeStruct((8, 256), F32)],
        compiler_params=_params(("arbitrary",)),
    )(dh_out, h, g, win_p, wa2_p, cos, sin, loga, ga, dgq, dgk, dgv, dgg, dsq, dsk, dsv, dloga)


def _rope_tables(rows):
    pos = (jnp.arange(rows, dtype=jnp.int32) - PAD_ROWS).astype(F32)
    inv_freq = 1.0 / (ROPE_THETA ** (jnp.arange(0, SWA_HD, 2, dtype=F32) / SWA_HD))
    ang = pos[:, None] * inv_freq[None, :]
    return jnp.tile(jnp.cos(ang), (1, 4)), jnp.tile(jnp.sin(ang), (1, 4))


def _local_step(x, tgt, front, w, late_weights=None, on_grads=None, on_small=None):
    cos, sin = _rope_tables(x.shape[0] + BLK)
    g = {}

    def tell(group, names):
        for nm in names:
            g[nm] = grads_now[nm]
        return None if on_grads is None else on_grads(group, {nm: grads_now[nm] for nm in names})

    h1, a1, b1, s1, f1 = _ffn_fwd(x, w["ffn1_pre"], w["wg1"], w["wu1"], w["wd1"], w["ffn1_post"], front=front)
    if late_weights is not None:
        w = {**w, **late_weights("win", f1)}
    gq, gk, gv, gg, sq, sk, sv, ga, loga, bc, n2 = _mix_in(h1, w["mix_pre"], w["win"], w["wa2"], w["b_a"], cos, sin)
    (o_g, cat_g, sp), (o_s, cat_s, lse) = _side_by_side(
        [_gla_fwd(gq, gk, gv, gg, bc, w["gla_norm"]), _swa_fwd(sq, sk, sv, w["sinks"], w["swa_norm"])], "attention_fwd")
    if late_weights is not None:
        w = {**w, **late_weights("rest", lse)}
    h2, m, a2, b2, s2, f2, dy, loss = _ffn_fwd(h1, w["ffn2_pre"], w["wg2"], w["wu2"], w["wd2"], w["ffn2_post"], tgt,
                                               mixed=(cat_g, cat_s, w["wout"], w["mix_post"]))
    dh2, da, db, df, n3, g["ffn2_pre"], g["ffn2_post"] = _ffn_bwd_act(
        dy, h2, a2, b2, f2, w["ffn2_pre"], w["ffn2_post"], w["wg2"], w["wu2"], w["wd2"], "ffn2_bwd_act")
    grads_now = dict(wd2=_wgrad(s2, df, "ffn2_wgrad_down"), wg2=_wgrad(da, n3, "ffn2_wgrad_gate"),
                     wu2=_wgrad(db, n3, "ffn2_wgrad_up"))
    tok = tell("ffn2", ("wd2", "wg2", "wu2"))
    dcg, dcs, dm, g["mix_post"] = _mix_out_bwd(dh2, m, w["wout"], w["mix_post"] + (0.0 if tok is None else tok[0, 0]))
    (dgq, dgk, dgv, dgg, dloga, g["gla_norm"]), (dsq, dsk, dsv, g["sinks"], g["swa_norm"]) = _side_by_side(
        [_gla_bwd(dcg, o_g, gq, gk, gv, gg, bc, sp, w["gla_norm"]),
         _swa_bwd(dcs, o_s, sq, sk, sv, lse, w["sinks"], w["swa_norm"])], "attention_bwd")
    dh1, dproj, g["wa2"], g["mix_pre"], g["b_a"] = _mix_in_bwd(
        dh2, h1, w["mix_pre"], w["win"], w["wa2"], cos, sin, loga, ga, dgq, dgk, dgv, dgg, dsq, dsk, dsv, dloga)
    dh0, da, db, df, n1, g["ffn1_pre"], g["ffn1_post"] = _ffn_bwd_act(
        dh1, x, a1, b1, f1, w["ffn1_pre"], w["ffn1_post"], w["wg1"], w["wu1"], w["wd1"], "ffn1_bwd_act", front=front)
    tok = None if on_small is None else on_small(loss[0, 0], dh0, g)
    grads_now = dict(wd1=_wgrad(s1, df, "ffn1_wgrad_down", after=tok))
    tok = tell("ffn1_down", ("wd1",))
    grads_now = dict(wg1=_wgrad(da, n1, "ffn1_wgrad_gate", after=tok))
    tok = tell("ffn1_gate", ("wg1",))
    grads_now = dict(wu1=_wgrad(db, n1, "ffn1_wgrad_up", after=tok))
    tok = tell("ffn1_up", ("wu1",))
    grads_now = dict(win=_wgrad(dproj, n2, "win_wgrad", after=tok),
                     wout=jnp.concatenate([_wgrad(cat_g, dm, "wout_wgrad_gla", after=tok),
                                           _wgrad(cat_s, dm, "wout_wgrad_swa", after=tok)], axis=0))
    tell("mix", ("wout", "win"))
    return loss[0, 0], dh0, g


def _win_pad_rows(win_t):
    pad = jnp.zeros((P_END - P_GA - 16, win_t.shape[1]), win_t.dtype)
    return jnp.concatenate([win_t[0:1536], win_t[1552:2320], win_t[1536:1552], pad], axis=0)


def _win_unpad_rows(win_p):
    return jnp.concatenate([win_p[0:1536], win_p[P_GA:P_GA + 16], win_p[1536:P_GA]], axis=0)


def _place_on_mesh():
    return lax.axis_index("x"), lax.axis_index("y"), lax.axis_index("c")


def _dev_index(px, py, pc):
    return 4 * px + 2 * py + pc


def _other_devices(x, y, c):
    flip = lambda v, f: 1 - v if f else v
    return [(flip(x, fx), flip(y, fy), flip(c, fc)) for fx in (0, 1) for fy in (0, 1) for fc in (0, 1)][1:]


def _all_gather(shards):
    n = len(shards)

    def body(*refs):
        ins, outs = refs[:n], refs[n:2 * n]
        zeros_ref, send_sems, recv_sems, local_sems = refs[2 * n:]
        zeros_ref[...] = jnp.zeros_like(zeros_ref)
        x, y, c = _place_on_mesh()
        me, sibling = (x, y, c), (x, y, 1 - c)
        chips = [(1 - x, y), (x, 1 - y), (1 - x, 1 - y)]

        def rows(k, px, py, pc):
            r = ins[k].shape[0]
            return outs[k].at[pl.ds(pl.multiple_of(_dev_index(px, py, pc) * r, 8), r), :]

        def copy(k, slot, block, to, src=None):
            return pltpu.make_async_remote_copy(
                src_ref=rows(k, *block) if src is None else src, dst_ref=rows(k, *block),
                send_sem=send_sems.at[k, slot], recv_sem=recv_sems.at[k, slot], device_id=to, device_id_type=MESH)

        local = [pltpu.make_async_copy(ins[k], rows(k, *me), local_sems.at[k]) for k in range(n)]
        sends = []
        for k in range(n):
            local[k].start()
            sends.append(copy(k, 0, me, sibling, src=ins[k]))
            sends += [copy(k, 1 + j, me, (*chip, c), src=ins[k]) for j, chip in enumerate(chips)]
        for cp in sends:
            cp.start()
        for k in range(n):
            for j, chip in enumerate(chips):
                copy(k, 1 + j, (*chip, c), me).wait_recv()
                passed = copy(k, 4 + j, (*chip, c), sibling)
                passed.start()
                sends.append(passed)
        for k in range(n):
            copy(k, 0, sibling, me).wait_recv()
            for j, chip in enumerate(chips):
                copy(k, 4 + j, (*chip, 1 - c), me).wait_recv()
        for cp in sends:
            cp.wait_send()
        for cp in local:
            cp.wait()

    return pl.pallas_call(
        body, name="all_gather_weights",
        in_specs=[ANY_SPEC] * n, out_specs=[ANY_SPEC] * n + [VMEM_SPEC],
        out_shape=[jax.ShapeDtypeStruct((N_DEV * s.shape[0], s.shape[1]), s.dtype) for s in shards]
        + [jax.ShapeDtypeStruct((8, 128), F32)],
        scratch_shapes=[pltpu.SemaphoreType.DMA((n, 7)), pltpu.SemaphoreType.DMA((n, 7)), pltpu.SemaphoreType.DMA((n,))],
    )(*shards)


HBM_SPEC = pl.BlockSpec(memory_space=pltpu.HBM)
SEM_SPEC = pl.BlockSpec(memory_space=pltpu.SEMAPHORE)
DATAFLOW = pltpu.SideEffectType.DATAFLOW_SIDE_EFFECTING


GATHER, SCATTER, SCATTER_CHIPS = "gather", "scatter", "scatter among chips"


def _exchange_peers(kind):
    x, y, c = _place_on_mesh()
    if kind == SCATTER_CHIPS:
        peers = [(1 - x, y, c), (x, 1 - y, c), (1 - x, 1 - y, c)]
        return peers, [2 * p[0] + p[1] for p in peers], 2 * x + y, 4
    peers = _other_devices(x, y, c)
    return peers, [_dev_index(*p) for p in peers], _dev_index(x, y, c), N_DEV


def _exchange_copies(srcs, lands, send_sems, recv_sems, own_sems, kind, arriving):
    peers, theirs, me, blocks = _exchange_peers(kind)
    remote, local = [], []
    for k, (src, land) in enumerate(zip(srcs, lands)):
        r = land.shape[0] // blocks

        def block(ref, d):
            return ref.at[pl.ds(pl.multiple_of(d * r, 8), r), :]

        for f, (peer, him) in enumerate(zip(peers, theirs)):
            mine, his = (him, me) if arriving else (me, him)
            sem = len(peers) * k + f
            remote.append(pltpu.make_async_remote_copy(
                src_ref=src if kind == GATHER else block(src, his), dst_ref=block(land, mine),
                send_sem=send_sems.at[sem], recv_sem=recv_sems.at[sem], device_id=peer, device_id_type=MESH))
        local.append(pltpu.make_async_copy(src if kind == GATHER else block(src, me), block(land, me), own_sems.at[k]))
    return remote, local


def _exchange_start(srcs, kind, name):
    n = len(srcs)
    lands = [lax.empty((N_DEV * s.shape[0], s.shape[1]) if kind == GATHER else s.shape, s.dtype) for s in srcs]
    sems = (3 if kind == SCATTER_CHIPS else 7) * n

    def body(*refs):
        remote, local = _exchange_copies(refs[:n], refs[n:2 * n], *refs[2 * n:2 * n + 3], kind, False)
        for cp in remote + local:
            cp.start()
        refs[-1][...] = jnp.zeros_like(refs[-1])

    both = list(srcs) + list(lands)
    outs = pl.pallas_call(
        body, name=name,
        out_shape=(pltpu.SemaphoreType.DMA((sems,)), pltpu.SemaphoreType.DMA((sems,)), pltpu.SemaphoreType.DMA((n,)),
                   *[pltpu.HBM(a.shape, a.dtype) for a in both], jax.ShapeDtypeStruct((8, 128), F32)),
        in_specs=[HBM_SPEC] * (2 * n), out_specs=(SEM_SPEC, SEM_SPEC, SEM_SPEC, *[HBM_SPEC] * (2 * n), VMEM_SPEC),
        input_output_aliases={i: 3 + i for i in range(2 * n)},
        compiler_params=pltpu.CompilerParams(has_side_effects=DATAFLOW),
    )(*[pltpu.with_memory_space_constraint(a, pltpu.HBM) for a in both])
    return outs[0:3], outs[3:3 + n], outs[3 + n:3 + 2 * n], outs[-1]


def _exchange_wait(started, kind, after, name):
    sems, srcs, lands, _ = started
    n = len(srcs)

    def body(*refs):
        args = (refs[:n], refs[n:2 * n], *refs[2 * n:2 * n + 3], kind)
        going, local = _exchange_copies(*args, False)
        for cp in going:
            cp.wait_send()
        for cp in local:
            cp.wait()
        for cp in _exchange_copies(*args, True)[0]:
            cp.wait_recv()

    both = list(srcs) + list(lands)
    outs = pl.pallas_call(
        body, name=name, out_shape=[pltpu.HBM(a.shape, a.dtype) for a in both],
        in_specs=[HBM_SPEC] * (2 * n) + [SEM_SPEC, SEM_SPEC, SEM_SPEC, ANY_SPEC], out_specs=[HBM_SPEC] * (2 * n),
        input_output_aliases={i: i for i in range(2 * n)},
        compiler_params=pltpu.CompilerParams(has_side_effects=DATAFLOW),
    )(*both, *sems, after)
    return outs[n:]


def _sibling_reduce(part, name):
    r, cols = part.shape[0] // N_DEV, part.shape[1]

    def body(p_ref, o_ref, mine, got, send_sems, recv_sems, own_sems):
        x, y, c = _place_on_mesh()

        def block(d):
            return p_ref.at[pl.ds(pl.multiple_of(d * r, 8), r), :]
        swaps = [pltpu.make_async_remote_copy(
            src_ref=block(2 * j + 1 - c), dst_ref=got.at[j], send_sem=send_sems.at[j], recv_sem=recv_sems.at[j],
            device_id=(x, y, 1 - c), device_id_type=MESH) for j in range(4)]
        keeps = [pltpu.make_async_copy(block(2 * j + c), mine.at[j], own_sems.at[j]) for j in range(4)]
        for cp in swaps + keeps:
            cp.start()
        for j in range(4):
            keeps[j].wait()
            swaps[j].wait()
            o_ref[pl.ds(j * r, r), :] = (mine[j].astype(F32) + got[j].astype(F32)).astype(o_ref.dtype)

    return pl.pallas_call(
        body, name=name, in_specs=[ANY_SPEC], out_specs=VMEM_SPEC,
        out_shape=jax.ShapeDtypeStruct((4 * r, cols), part.dtype),
        scratch_shapes=[pltpu.VMEM((4, r, cols), part.dtype), pltpu.VMEM((4, r, cols), part.dtype),
                        pltpu.SemaphoreType.DMA((4,)), pltpu.SemaphoreType.DMA((4,)), pltpu.SemaphoreType.DMA((4,))],
        compiler_params=pltpu.CompilerParams(vmem_limit_bytes=32 << 20),
    )(part)


def _sum_partials(parts, name, blocks=N_DEV):
    n = len(parts)

    def body(*refs):
        ins, outs = refs[:n], refs[n:]
        first = pl.program_id(0) == 0
        for i_ref, o_ref in zip(ins, outs):
            v = i_ref[...].astype(F32)

            @pl.when(first)
            def _():
                o_ref[...] = v

            @pl.when(jnp.logical_not(first))
            def _():
                o_ref[...] += v

    shapes = [(p.shape[0] // blocks, p.shape[1]) for p in parts]
    return pl.pallas_call(
        body, name=name, grid=(blocks,),
        in_specs=[pl.BlockSpec(s, lambda j: (j, 0)) for s in shapes],
        out_specs=[pl.BlockSpec(s, lambda j: (0, 0)) for s in shapes],
        out_shape=[jax.ShapeDtypeStruct(s, F32) for s in shapes],
        compiler_params=_params(("arbitrary",)),
    )(*parts)


def _adamw_update(w, g, m, v):
    m = ADAM_B1 * m + (1.0 - ADAM_B1) * g
    v = ADAM_B2 * v + (1.0 - ADAM_B2) * (g * g)
    m_hat = m * (1.0 / (1.0 - ADAM_B1 ** ADAM_STEP))
    v_hat = v * (1.0 / (1.0 - ADAM_B2 ** ADAM_STEP))
    return -ADAM_LR * (m_hat / (jnp.sqrt(v_hat) + ADAM_EPS) + ADAM_WD * w), m, v


def _sum_adamw(parts, w, m, v, blocks, name):
    shape = w.shape

    def body(p_ref, w_ref, m_ref, v_ref, g_ref, d_ref, mo_ref, vo_ref):
        j = pl.program_id(0)
        part = p_ref[...].astype(F32)

        @pl.when(j == 0)
        def _():
            g_ref[...] = part

        @pl.when(j > 0)
        def _():
            g_ref[...] += part

        @pl.when(j == blocks - 1)
        def _():
            d_ref[...], mo_ref[...], vo_ref[...] = _adamw_update(w_ref[...], g_ref[...], m_ref[...], v_ref[...])

    held = pl.BlockSpec(shape, lambda j: (0, 0))
    return pl.pallas_call(
        body, name=name, grid=(blocks,),
        in_specs=[pl.BlockSpec(shape, lambda j: (j, 0)), held, held, held],
        out_specs=[held] * 4, out_shape=[jax.ShapeDtypeStruct(shape, F32)] * 4,
        compiler_params=_params(("arbitrary",)),
    )(parts, w, m, v)


def _adamw(ws, gs, ms, vs, name):
    n = len(ws)

    def body(*refs):
        w_r, g_r, m_r, v_r = refs[:n], refs[n:2 * n], refs[2 * n:3 * n], refs[3 * n:4 * n]
        d_o, m_o, v_o = refs[4 * n:5 * n], refs[5 * n:6 * n], refs[6 * n:7 * n]
        for k in range(n):
            d_o[k][...], m_o[k][...], v_o[k][...] = _adamw_update(w_r[k][...], g_r[k][...], m_r[k][...], v_r[k][...])

    shapes = [jax.ShapeDtypeStruct(w.shape, F32) for w in ws]
    outs = pl.pallas_call(
        body, name=name, in_specs=[VMEM_SPEC] * (4 * n), out_specs=[VMEM_SPEC] * (3 * n), out_shape=shapes * 3,
        compiler_params=pltpu.CompilerParams(vmem_limit_bytes=56 << 20),
    )(*ws, *gs, *ms, *vs)
    return outs[:n], outs[n:2 * n], outs[2 * n:]


WEIGHT_NAMES = ("meta_tokens", "ffn1_pre_norm", "ffn1_w_gate", "ffn1_w_up", "ffn1_w_down", "ffn1_post_norm", "mix_pre_norm",
                "w_in", "gla_w_a2", "gla_b_a", "gla_out_norm", "swa_sinks", "swa_out_norm", "w_out", "mix_post_norm",
                "ffn2_pre_norm", "ffn2_w_gate", "ffn2_w_up", "ffn2_w_down", "ffn2_post_norm")
WIN_SHARD = D_IN // N_DEV
WIN_SHARD_PAD = 304
SLAB_VECTORS = ("ffn1_pre", "ffn1_post", "mix_pre", "mix_post", "ffn2_pre", "ffn2_post")
SLAB_ROWS = 32


def kernel(x, meta_tokens, ffn1_pre_norm, ffn1_w_gate, ffn1_w_up, ffn1_w_down, ffn1_post_norm, mix_pre_norm, w_in, gla_w_a2, gla_b_a, gla_out_norm, swa_sinks, swa_out_norm, w_out, mix_post_norm, ffn2_pre_norm, ffn2_w_gate, ffn2_w_up, ffn2_w_down, ffn2_post_norm, loss_target, m_meta_tokens, m_ffn1_pre_norm, m_ffn1_w_gate, m_ffn1_w_up, m_ffn1_w_down, m_ffn1_post_norm, m_mix_pre_norm, m_w_in, m_gla_w_a2, m_gla_b_a, m_gla_out_norm, m_swa_sinks, m_swa_out_norm, m_w_out, m_mix_post_norm, m_ffn2_pre_norm, m_ffn2_w_gate, m_ffn2_w_up, m_ffn2_w_down, m_ffn2_post_norm, v_meta_tokens, v_ffn1_pre_norm, v_ffn1_w_gate, v_ffn1_w_up, v_ffn1_w_down, v_ffn1_post_norm, v_mix_pre_norm, v_w_in, v_gla_w_a2, v_gla_b_a, v_gla_out_norm, v_swa_sinks, v_swa_out_norm, v_w_out, v_mix_post_norm, v_ffn2_pre_norm, v_ffn2_w_gate, v_ffn2_w_up, v_ffn2_w_down, v_ffn2_post_norm):
    given = dict(locals())
    W = {n: given[n] for n in WEIGHT_NAMES}
    M = {n: given["m_" + n] for n in WEIGHT_NAMES}
    V = {n: given["v_" + n] for n in WEIGHT_NAMES}
    dev = _dev_index(*_place_on_mesh())

    def t16(w):
        return w[0].T.astype(BF16)

    small = jnp.concatenate([W["meta_tokens"], jnp.pad(W["gla_w_a2"][0], ((0, 0), (0, 96)))], axis=0)
    wg1, wu1, wd1, small_g, gathered_zeros = _all_gather(
        [t16(W["ffn1_w_gate"]), t16(W["ffn1_w_up"]), W["ffn1_w_down"][0].astype(BF16), small])
    def after_zero(shard, zeros):
        return shard + zeros[0:1, 0:1].astype(shard.dtype)
    win_shard = jnp.pad(t16(W["w_in"]), ((0, WIN_SHARD_PAD - WIN_SHARD), (0, 0)))
    win_shard = after_zero(win_shard, gathered_zeros)
    mid = _exchange_start([win_shard], GATHER, "gather_w_in_start")
    late_shards = [after_zero(W["w_out"][0].astype(BF16), mid[3]), t16(W["ffn2_w_gate"]), t16(W["ffn2_w_up"]),
                   W["ffn2_w_down"][0].astype(BF16)]
    late = _exchange_start(late_shards, GATHER, "gather_late_weights_start")

    def late_weights(what, after):
        if what == "win":
            win_g, = _exchange_wait(mid, GATHER, after, "gather_w_in_wait")
            win_t = win_g.reshape(N_DEV, WIN_SHARD_PAD, D_MODEL)[:, :WIN_SHARD].reshape(D_IN, D_MODEL)
            return dict(win=_win_pad_rows(win_t))
        wout, wg2, wu2, wd2 = _exchange_wait(late, GATHER, after, "gather_late_weights_wait")
        return dict(wout=wout, wg2=wg2, wu2=wu2, wd2=wd2)

    small_g = small_g.reshape(N_DEV, 32, 128)
    meta_full = small_g[:, :N_META].transpose(1, 0, 2).reshape(N_META, D_MODEL)
    wa2_full = small_g[:, N_META:, :32].transpose(1, 0, 2).reshape(16, 256)
    w = dict(
        ffn1_pre=W["ffn1_pre_norm"] + late[3][0, 0], ffn1_post=W["ffn1_post_norm"], mix_pre=W["mix_pre_norm"],
        mix_post=W["mix_post_norm"], ffn2_pre=W["ffn2_pre_norm"], ffn2_post=W["ffn2_post_norm"], b_a=W["gla_b_a"],
        gla_norm=W["gla_out_norm"], sinks=W["swa_sinks"], swa_norm=W["swa_out_norm"], wg1=wg1, wu1=wu1, wd1=wd1,
        wa2=jnp.pad(wa2_full, ((0, 112), (0, 0))))

    in_flight = []

    def on_grads(group, grads):
        parts = []
        for nm, p in grads.items():
            if nm == "win":
                p = _win_unpad_rows(p).reshape(N_DEV, WIN_SHARD, D_MODEL)
                p = jnp.pad(p, ((0, 0), (0, WIN_SHARD_PAD - WIN_SHARD), (0, 0))).reshape(N_DEV * WIN_SHARD_PAD, D_MODEL)
            parts.append(p)
        kind = SCATTER
        if kind == SCATTER_CHIPS:
            parts = [_sibling_reduce(p, "pair_" + group + "_" + nm) for nm, p in zip(grads, parts)]
        started = _exchange_start(parts, kind, "scatter_" + group + "_start")
        in_flight.append((group, list(grads), started, kind))
        return started[3]

    small_flight = []

    def on_small(loss, dh0, g):
        packed = jnp.concatenate([g["b_a"][0:1], g["gla_norm"][0:1], g["sinks"][0:1], g["swa_norm"][0:1]], axis=1)
        slab = jnp.concatenate([g[k][0:1] for k in SLAB_VECTORS] + [packed, jnp.full((1, D_MODEL), loss, F32),
                               g["wa2"][:16].reshape(4, D_MODEL), jnp.zeros((4, D_MODEL), F32), dh0[PAD_ROWS:BLK]], axis=0)
        small_flight.append(_exchange_start([slab], GATHER, "gather_small_grads_start"))
        return small_flight[0][3]

    front = jnp.concatenate([jnp.zeros((PAD_ROWS, D_MODEL), F32), meta_full], axis=0)
    loss, dh0, g = _local_step(x[0], loss_target[0], front, w, late_weights, on_grads, on_small)
    grad_x = dh0[BLK:][None]

    land, = _exchange_wait(small_flight[0], GATHER, in_flight[-1][2][3], "gather_small_grads_wait")
    tot = _sum_partials([land], "sum_small_grads")[0]
    loss = tot[7, 0]
    small_grads = dict(
        ffn1_pre_norm=tot[0:1], ffn1_post_norm=tot[1:2], mix_pre_norm=tot[2:3], mix_post_norm=tot[3:4],
        ffn2_pre_norm=tot[4:5], ffn2_post_norm=tot[5:6], gla_b_a=tot[6:7, 0:256], gla_out_norm=tot[6:7, 256:384],
        swa_sinks=tot[6:7, 384:392], swa_out_norm=tot[6:7, 512:1024],
        gla_w_a2=lax.dynamic_slice_in_dim(tot[8:12].reshape(16, 256), dev * 32, 32, axis=1)[None],
        meta_tokens=lax.dynamic_slice_in_dim(tot[16:32], dev * 128, 128, axis=1))

    big = dict(wg1=("ffn1_w_gate", True), wu1=("ffn1_w_up", True), wd1=("ffn1_w_down", False), win=("w_in", True),
               wout=("w_out", False), wg2=("ffn2_w_gate", True), wu2=("ffn2_w_up", True), wd2=("ffn2_w_down", False))
    grads = dict(small_grads)
    delta, new_m, new_v = {}, {}, {}
    names = [n for n in WEIGHT_NAMES if n not in [full for full, _ in big.values()]]
    two_d = lambda a: a.reshape(-1, a.shape[-1])
    d_, m_, v_ = _adamw([two_d(W[n]) for n in names], [two_d(grads[n]) for n in names],
                        [two_d(M[n]) for n in names], [two_d(V[n]) for n in names], "adamw_small")
    for k, n in enumerate(names):
        delta[n], new_m[n], new_v[n] = d_[k].reshape(W[n].shape), m_[k].reshape(W[n].shape), v_[k].reshape(W[n].shape)

    before_wait = d_[0] + in_flight[-1][2][3][0, 0]
    for group, shorts, started, kind in in_flight:
        lands = _exchange_wait(started, kind, before_wait, "scatter_" + group + "_wait")
        blocks = 4 if kind == SCATTER_CHIPS else N_DEV
        for short, land in zip(shorts, lands):
            n, transposed = big[short]
            to_slab = (lambda a: a[0].T) if transposed else (lambda a: a[0])
            from_slab = (lambda a: a.T[None]) if transposed else (lambda a: a[None])
            if short == "win":
                g_slab = _sum_partials([land], "sum_" + n, blocks)[0][:WIN_SHARD]
                d_, m_, v_ = _adamw([to_slab(W[n])], [g_slab], [to_slab(M[n])], [to_slab(V[n])], "adamw_" + n)
                d_, m_, v_ = d_[0], m_[0], v_[0]
            else:
                g_slab, d_, m_, v_ = _sum_adamw(land, to_slab(W[n]), to_slab(M[n]), to_slab(V[n]), blocks, "adamw_" + n)
            grads[n], delta[n], new_m[n], new_v[n] = from_slab(g_slab), from_slab(d_), from_slab(m_), from_slab(v_)
            before_wait = d_
    return (loss, grad_x, *[grads[n] for n in WEIGHT_NAMES], *[delta[n] for n in WEIGHT_NAMES],
            *[new_m[n] for n in WEIGHT_NAMES], *[new_v[n] for n in WEIGHT_NAMES])
```

```python
import math

import jax
import jax.numpy as jnp
from jax import lax
from jax.experimental import pallas as pl
from jax.experimental.pallas import tpu as pltpu

F32, BF16 = jnp.float32, jnp.bfloat16

D_MODEL = 1024
D_FF = 2816
N_META = 16
BLK = 128
PAD_ROWS = BLK - N_META
GLA_DK = 64
SWA_HD = 64
SWA_HEADS = 8
GLA_TAU = 16.0
NORM_EPS = 1e-6
NEG_INF = -1e30
ROPE_THETA = 10000.0
P_GQ, P_GK, P_GV, P_GG, P_SQ, P_SK, P_SV, P_GA, P_END = 0, 256, 512, 1024, 1536, 2048, 2176, 2304, 2432
D_IN = 2320
IN_SPLITS = (256, 256, 512, 512, 16, 512, 128, 128)
FF_TILE = 2816
WGRAD_TILE_MAX = 2432
N_DEV = 8
MESH = pl.DeviceIdType.MESH

ADAM_LR, ADAM_B1, ADAM_B2, ADAM_EPS, ADAM_WD, ADAM_STEP = 0.001, 0.9, 0.999, 1e-08, 0.01, 10

V7X_VMEM_BYTES = 64 << 20
VMEM_SPEC = pl.BlockSpec(memory_space=pltpu.VMEM)
SMEM_SPEC = pl.BlockSpec(memory_space=pltpu.SMEM)
ANY_SPEC = pl.BlockSpec(memory_space=pl.ANY)


def _params(semantics, vmem_mb=56):
    return pltpu.CompilerParams(dimension_semantics=semantics, vmem_limit_bytes=vmem_mb << 20)


def _row_tile(rows):
    return 416 if rows % 416 == 0 else BLK


def _blocks_per_step(blocks):
    return 5 if blocks % 5 == 0 else 1


def _nn(a, b):
    return lax.dot_general(a, b, (((1,), (0,)), ((), ())), preferred_element_type=F32)


def _nt(a, b):
    return lax.dot_general(a, b, (((1,), (1,)), ((), ())), preferred_element_type=F32)


def _tn(a, b):
    return lax.dot_general(a, b, (((0,), (0,)), ((), ())), preferred_element_type=F32)


def _rms(x):
    r = lax.rsqrt(jnp.mean(x * x, axis=-1, keepdims=True) + NORM_EPS)
    return x * r, r


def _rms_bwd(xn, r, w, dy):
    g = dy * w
    return r * (g - xn * jnp.mean(g * xn, axis=-1, keepdims=True))


def _sigmoid(x):
    return 1.0 / (1.0 + jnp.exp(-x))


def _colsum(x):
    return jnp.sum(x, axis=0, keepdims=True)


def _split_bf16(x):
    hi = x.astype(BF16)
    lo = (x - hi.astype(F32)).astype(BF16)
    return hi, lo


def _tri(lower):
    r = lax.broadcasted_iota(jnp.int32, (BLK, BLK), 0)
    c = lax.broadcasted_iota(jnp.int32, (BLK, BLK), 1)
    return (r >= c) if lower else (c >= r)


def _half_mask(width, half):
    lane = lax.broadcasted_iota(jnp.int32, (1, width), 1)
    return ((lane % 128) < 64) if half == 0 else ((lane % 128) >= 64)


def _rot_half(x):
    w = x.shape[-1]
    lane = lax.broadcasted_iota(jnp.int32, (1, w), 1)
    return jnp.where((lane % SWA_HD) < SWA_HD // 2, -pltpu.roll(x, w - SWA_HD // 2, 1), pltpu.roll(x, SWA_HD // 2, 1))


def _row_spec(tm, cols):
    return pl.BlockSpec((tm, cols), lambda i: (i, 0))


def _acc_spec(cols):
    return pl.BlockSpec((8, cols), lambda i: (0, 0))


def _acc_add(ref, first, value):
    @pl.when(first)
    def _():
        ref[...] = jnp.zeros_like(ref)
    ref[0:1, :] += value


def _behind_spec(tm):
    return pl.BlockSpec((pl.Element(tm), pl.Element(D_MODEL)),
                        lambda i: (pl.multiple_of(jnp.maximum(i * tm - BLK, 0), math.gcd(tm, BLK)), 0))


def _behind_front(ref, i, tm, front):
    blk = ref[...]
    return jnp.where(i == 0, jnp.concatenate([front, blk[0:tm - BLK]], axis=0), blk)


def _ffn_fwd(h, gpre, wg_t, wu_t, wd, gpost, tgt=None, front=None, mixed=None):
    with_loss, with_front, with_mixed = tgt is not None, front is not None, mixed is not None
    rows = h.shape[0] + (BLK if with_front else 0)
    tm = _row_tile(rows)
    nf = D_FF // FF_TILE

    def body(*refs):
        refs = list(refs)
        h_ref, gpre_ref, wg_ref, wu_ref, wd_ref, gpost_ref = refs[:6]
        del refs[:6]
        front_ref = refs.pop(0) if with_front else None
        cg_ref, cs_ref, wo_ref, gm_ref = (refs.pop(0), refs.pop(0), refs.pop(0), refs.pop(0)) if with_mixed else (None,) * 4
        t_ref = refs.pop(0) if with_loss else None
        hm_ref, m_ref = (refs.pop(0), refs.pop(0)) if with_mixed else (None, None)
        ho_ref = None if with_loss else refs.pop(0)
        a_ref, b_ref, s_ref, f_ref = refs[:4]
        dy_ref, loss_ref = refs[4:6] if with_loss else (None, None)
        acc = refs[-1]
        i = pl.program_id(0)
        h_in = _behind_front(h_ref, i, tm, front_ref[...]) if with_front else h_ref[...]
        if with_mixed:
            m = _nn(cg_ref[...], wo_ref[0:512, :]) + _nn(cs_ref[...], wo_ref[512:1024, :])
            m_ref[...] = m
            mn, _ = _rms(m)
            h_in = h_in + mn * gm_ref[...]
            hm_ref[...] = h_in
        hn, _ = _rms(h_in)
        n16 = (hn * gpre_ref[...]).astype(BF16)
        for j in range(nf):
            cols = slice(j * FF_TILE, (j + 1) * FF_TILE)
            a = _nt(n16, wg_ref[cols, :])
            b = _nt(n16, wu_ref[cols, :])
            a_ref[:, cols] = a.astype(BF16)
            b_ref[:, cols] = b.astype(BF16)
            s16 = (a * _sigmoid(a) * b).astype(BF16)
            s_ref[:, cols] = s16
            part = _nn(s16, wd_ref[cols, :])
            if j == 0:
                acc[...] = part
            else:
                acc[...] += part
        f = acc[...]
        f_ref[...] = f
        fn, _ = _rms(f)
        y = h_in + 0.5 * (fn * gpost_ref[...])
        if not with_loss:
            ho_ref[...] = y
        else:
            row = i * tm + lax.broadcasted_iota(jnp.int32, (tm, 1), 0)
            err = jnp.where(row >= BLK, y - _behind_front(t_ref, i, tm, jnp.zeros((BLK, D_MODEL), F32)), 0.0)
            dy_ref[...] = err * (1.0 / D_MODEL)
            part = 0.5 * jnp.sum(jnp.sum(err * err, axis=-1, keepdims=True) * (1.0 / D_MODEL), axis=0, keepdims=True)

            @pl.when(i == 0)
            def _():
                loss_ref[...] = jnp.zeros_like(loss_ref)
            loss_ref[...] += part

    row_f32 = _row_spec(tm, D_MODEL)
    behind = _behind_spec(tm)
    in_specs = [behind if with_front else row_f32, VMEM_SPEC, VMEM_SPEC, VMEM_SPEC, VMEM_SPEC, VMEM_SPEC]
    wide, full = jax.ShapeDtypeStruct((rows, D_FF), BF16), jax.ShapeDtypeStruct((rows, D_MODEL), F32)
    out_specs = [_row_spec(tm, D_FF), _row_spec(tm, D_FF), _row_spec(tm, D_FF), row_f32]
    out_shape = [wide, wide, wide, full]
    args = [h, gpre, wg_t, wu_t, wd, gpost]
    if not with_loss:
        out_specs.insert(0, row_f32)
        out_shape.insert(0, full)
    if with_front:
        in_specs.append(VMEM_SPEC)
        args.append(front)
    if with_mixed:
        in_specs += [_row_spec(tm, 512), _row_spec(tm, 512), VMEM_SPEC, VMEM_SPEC]
        args += list(mixed)
        out_specs = [row_f32, row_f32] + out_specs
        out_shape = [full, full] + out_shape
    if with_loss:
        in_specs.append(behind)
        args.append(tgt)
        out_specs += [row_f32, pl.BlockSpec((8, 128), lambda i: (0, 0))]
        out_shape += [jax.ShapeDtypeStruct((rows, D_MODEL), F32), jax.ShapeDtypeStruct((8, 128), F32)]
    return pl.pallas_call(
        body, name="ffn_fwd_loss" if with_loss else "ffn_fwd", grid=(rows // tm,),
        in_specs=in_specs, out_specs=out_specs, out_shape=out_shape,
        scratch_shapes=[pltpu.VMEM((tm, D_MODEL), F32)],
        compiler_params=_params(("arbitrary",), vmem_mb=62 if with_mixed else 56),
    )(*args)


def _ffn_bwd_act(dh_out, h, a, b, f, gpre, gpost, wg_t, wu_t, wd, name, front=None):
    with_front = front is not None
    rows = dh_out.shape[0]
    tm = _row_tile(rows)
    nf = D_FF // FF_TILE

    def body(dho_ref, h_ref, a_ref, b_ref, f_ref, gpre_ref, gpost_ref, wg_ref, wu_ref, wd_ref, *rest):
        front_ref = rest[0] if with_front else None
        dh_ref, da_ref, db_ref, df_ref, n_ref, dgpre_ref, dgpost_ref, acc = rest[-8:]
        first = pl.program_id(0) == 0
        dho = dho_ref[...]
        drr = 0.5 * dho
        fn, rf = _rms(f_ref[...])
        _acc_add(dgpost_ref, first, _colsum(drr * fn))
        df16 = _rms_bwd(fn, rf, gpost_ref[...], drr).astype(BF16)
        df_ref[...] = df16
        h_in = _behind_front(h_ref, pl.program_id(0), tm, front_ref[...]) if with_front else h_ref[...]
        hn, rh = _rms(h_in)
        n_ref[...] = (hn * gpre_ref[...]).astype(BF16)
        for j in range(nf):
            cols = slice(j * FF_TILE, (j + 1) * FF_TILE)
            ds = _nt(df16, wd_ref[cols, :])
            av = a_ref[:, cols].astype(F32)
            bv = b_ref[:, cols].astype(F32)
            sg = _sigmoid(av)
            db16 = (ds * (av * sg)).astype(BF16)
            da16 = (ds * bv * (sg * (1.0 + av * (1.0 - sg)))).astype(BF16)
            da_ref[:, cols] = da16
            db_ref[:, cols] = db16
            part = _nn(da16, wg_ref[cols, :]) + _nn(db16, wu_ref[cols, :])
            if j == 0:
                acc[...] = part
            else:
                acc[...] += part
        dn = acc[...]
        _acc_add(dgpre_ref, first, _colsum(dn * hn))
        dh_ref[...] = dho + _rms_bwd(hn, rh, gpre_ref[...], dn)

    row_f32 = _row_spec(tm, D_MODEL)
    row_ff = _row_spec(tm, D_FF)
    return pl.pallas_call(
        body, name=name, grid=(rows // tm,),
        in_specs=[row_f32, _behind_spec(tm) if with_front else row_f32, row_ff, row_ff, row_f32,
                  VMEM_SPEC, VMEM_SPEC, VMEM_SPEC, VMEM_SPEC, VMEM_SPEC] + ([VMEM_SPEC] if with_front else []),
        out_specs=[row_f32, row_ff, row_ff, row_f32, row_f32, _acc_spec(D_MODEL), _acc_spec(D_MODEL)],
        out_shape=[jax.ShapeDtypeStruct((rows, D_MODEL), F32), jax.ShapeDtypeStruct((rows, D_FF), BF16),
                   jax.ShapeDtypeStruct((rows, D_FF), BF16), jax.ShapeDtypeStruct((rows, D_MODEL), BF16),
                   jax.ShapeDtypeStruct((rows, D_MODEL), BF16), jax.ShapeDtypeStruct((8, D_MODEL), F32),
                   jax.ShapeDtypeStruct((8, D_MODEL), F32)],
        scratch_shapes=[pltpu.VMEM((tm, D_MODEL), F32)],
        compiler_params=_params(("arbitrary",), vmem_mb=62),
    )(dh_out, h, a, b, f, gpre, gpost, wg_t, wu_t, wd, *([front] if with_front else []))


def _wgrad(lhs, rhs, name, after=None):
    rows, width = lhs.shape
    tf = 256 if width % 256 == 0 else 128
    pieces = 5 if rows % 80 == 0 else 3 if rows % 48 == 0 else 1
    piece = rows // pieces
    tiles, slots = width // tf, 3

    def body(l_hbm, r_hbm, *rest):
        o_ref, l_buf, r_all, l_sems, r_sems = rest[-5:]
        j = pl.program_id(0)

        def fetch_r(c):
            part = pl.ds(c * piece, piece)
            return pltpu.make_async_copy(r_hbm.at[part, :], r_all.at[part, :], r_sems.at[c])

        def fetch_l(tile):
            slot = tile % slots
            start = tile * tf if isinstance(tile, int) else pl.multiple_of(tile * tf, 128)
            return pltpu.make_async_copy(l_hbm.at[:, pl.ds(start, tf)], l_buf.at[slot], l_sems.at[slot])

        @pl.when(j == 0)
        def _():
            for tile in range(min(slots - 1, tiles)):
                fetch_l(tile).start()
            for c in range(pieces):
                fetch_r(c).start()

        @pl.when(j + slots - 1 < tiles)
        def _():
            fetch_l(j + slots - 1).start()
        fetch_l(j).wait()
        lhs_tile = l_buf.at[j % slots]

        @pl.when(j == 0)
        def _():
            total = None
            for c in range(pieces):
                fetch_r(c).wait()
                part = _tn(lhs_tile[c * piece:(c + 1) * piece, :], r_all[c * piece:(c + 1) * piece, :])
                total = part if total is None else total + part
            o_ref[...] = total.astype(BF16)

        @pl.when(j > 0)
        def _():
            o_ref[...] = _tn(lhs_tile[...], r_all[...]).astype(BF16)

    return pl.pallas_call(
        body, name=name, grid=(tiles,),
        in_specs=[ANY_SPEC, ANY_SPEC] + ([] if after is None else [ANY_SPEC]),
        out_specs=pl.BlockSpec((tf, D_MODEL), lambda j: (j, 0)),
        out_shape=jax.ShapeDtypeStruct((width, D_MODEL), BF16),
        scratch_shapes=[pltpu.VMEM((slots, rows, tf), BF16), pltpu.VMEM((rows, D_MODEL), BF16),
                        pltpu.SemaphoreType.DMA((slots,)), pltpu.SemaphoreType.DMA((pieces,))],
        compiler_params=_params(("arbitrary",)),
    )(lhs, rhs, *([] if after is None else [after]))


def _chunk_cumsum(x, lower):
    tri = jnp.where(_tri(lower), 1.0, 0.0).astype(BF16)
    hi, lo = _split_bf16(x)
    return _nn(tri, hi) + _nn(tri, lo)


def _mix_in(h, g, win_p, wa2_p, b_a, cos, sin):
    rows = h.shape[0]
    tm = 640 if rows % 640 == 0 else BLK

    def body(h_ref, g_ref, win_ref, wa2_ref, ba_ref, cos_ref, sin_ref,
             gq_ref, gk_ref, gv_ref, gg_ref, sq_ref, sk_ref, sv_ref, ga_ref, loga_ref, bc_ref, n_ref):
        hn, _ = _rms(h_ref[...])
        n16 = (hn * g_ref[...]).astype(BF16)
        n_ref[...] = n16
        proj = _nt(n16, win_ref[...])
        gq_ref[...] = proj[:, P_GQ:P_GK]
        gk_ref[...] = proj[:, P_GK:P_GV]
        gv_ref[...] = proj[:, P_GV:P_GG].astype(BF16)
        gg_ref[...] = proj[:, P_GG:P_SQ]
        c1, s1 = cos_ref[...], sin_ref[...]
        c4 = jnp.concatenate([c1, c1, c1, c1], axis=1)
        s4 = jnp.concatenate([s1, s1, s1, s1], axis=1)
        sq = proj[:, P_SQ:P_SK]
        sk = proj[:, P_SK:P_SV]
        sq_ref[...] = (sq * c4 + _rot_half(sq) * s4).astype(BF16)
        sk_ref[...] = (sk * c1 + _rot_half(sk) * s1).astype(BF16)
        sv_ref[...] = proj[:, P_SV:P_GA].astype(BF16)
        ga = proj[:, P_GA:P_END]
        ga_ref[...] = ga
        z = _nn(ga, wa2_ref[...]) + ba_ref[...]
        loga = (jnp.minimum(z, 0.0) - jnp.log(1.0 + jnp.exp(-jnp.abs(z)))) * (1.0 / GLA_TAU)
        loga_ref[...] = loga
        for c in range(tm // BLK):
            rs = slice(c * BLK, (c + 1) * BLK)
            bc_ref[rs, :] = _chunk_cumsum(loga[rs, :], True)

    f32 = lambda c: jax.ShapeDtypeStruct((rows, c), F32)
    b16 = lambda c: jax.ShapeDtypeStruct((rows, c), BF16)
    rs = lambda c: _row_spec(tm, c)
    return pl.pallas_call(
        body, name="mix_in", grid=(rows // tm,),
        in_specs=[rs(D_MODEL), VMEM_SPEC, VMEM_SPEC, VMEM_SPEC, VMEM_SPEC, rs(128), rs(128)],
        out_specs=[rs(256), rs(256), rs(512), rs(512), rs(512), rs(128), rs(128), rs(128), rs(256), rs(256), rs(D_MODEL)],
        out_shape=[f32(256), f32(256), b16(512), f32(512), b16(512), b16(128), b16(128), f32(128), f32(256), f32(256),
                   b16(D_MODEL)],
        compiler_params=_params(("arbitrary",)),
    )(h, g, win_p, wa2_p, b_a, cos, sin)


def _side_by_side(parts, name):
    steps, per_step = parts[0]["steps"], parts[0]["per_step"]
    assert all((p["steps"], p["per_step"]) == (steps, per_step) for p in parts)
    counts = [[len(p[key]) for p in parts] for key in ("in_specs", "out_specs", "scratch_shapes")]

    def body(*refs):
        groups, pos = [], 0
        for kind in counts:
            groups.append([])
            for n in kind:
                groups[-1].append(refs[pos:pos + n])
                pos += n
        programs = [p["program"](*groups[0][k], *groups[1][k], *groups[2][k]) for k, p in enumerate(parts)]

        def blocks(c, carries):
            return tuple(block(c, carry) for (block, _, _), carry in zip(programs, carries))
        carries = lax.fori_loop(0, per_step, blocks, tuple(first for _, first, _ in programs))
        for (_, _, finish), carry in zip(programs, carries):
            finish(carry)

    outs = pl.pallas_call(
        body, name=name, grid=(steps,),
        in_specs=[s for p in parts for s in p["in_specs"]], out_specs=[s for p in parts for s in p["out_specs"]],
        out_shape=[s for p in parts for s in p["out_shape"]],
        scratch_shapes=[s for p in parts for s in p["scratch_shapes"]],
        compiler_params=_params(("arbitrary",)),
    )(*[a for p in parts for a in p["args"]])
    split, pos = [], 0
    for n in counts[1]:
        split.append(outs[pos:pos + n])
        pos += n
    return split


def _gla_factors(q, k, bc):
    bm = bc[BLK // 2 - 1:BLK // 2, :]
    bl = bc[BLK - 1:BLK, :]
    e_q, e_k, e_qe, e_kd = jnp.exp(bc - bm), jnp.exp(bm - bc), jnp.exp(bc), jnp.exp(bl - bc)
    return (q * e_q, k * e_k, q * e_qe, k * e_kd), (e_q, e_k, e_qe, e_kd), jnp.exp(bl)


def _gla_fwd(gq, gk, gv, gg, bc, wgn):
    rows = gq.shape[0]
    nc = rows // BLK
    per_step = _blocks_per_step(nc)
    scale = GLA_DK ** -0.5

    def body(q_ref, k_ref, v_ref, gg_ref, bc_ref, wgn_ref, o_ref, cat_ref, sp_ref, st):
        @pl.when(pl.program_id(0) == 0)
        def _():
            st[...] = jnp.zeros_like(st)
        low = _tri(True)
        wgn_v = wgn_ref[...]

        def chunk(c, carry):
            rr = pl.ds(pl.multiple_of(c * BLK, BLK), BLK)
            for p in range(2):
                sl = slice(128 * p, 128 * p + 128)
                (qt, kt, qe, kd), _, ebl = _gla_factors(q_ref[rr, sl] * scale, k_ref[rr, sl], bc_ref[rr, sl])
                s_prev = st[p]
                sp_ref[c, p] = s_prev
                s16 = s_prev.astype(BF16)
                qt16 = qt.astype(BF16)
                s_new = s_prev * ebl
                for hh in range(2):
                    hs = slice(128 * (2 * p + hh), 128 * (2 * p + hh) + 128)
                    lm = _half_mask(128, hh)
                    vh = v_ref[rr, hs]
                    pm = jnp.where(low, _nt(qt16, jnp.where(lm, kt, 0.0).astype(BF16)), 0.0)
                    o = _nn(pm.astype(BF16), vh) + _nt(jnp.where(lm, qe, 0.0).astype(BF16), s16)
                    s_new = s_new + _tn(vh, jnp.where(lm, kd, 0.0).astype(BF16))
                    o_ref[rr, hs] = o
                    on, _ = _rms(o)
                    gate = gg_ref[rr, hs]
                    cat_ref[rr, hs] = (on * wgn_v * (gate * _sigmoid(gate))).astype(BF16)
                st[p] = s_new
            return carry
        return chunk, 0, lambda carry: None

    rs = lambda c: _row_spec(per_step * BLK, c)
    return dict(
        program=body, steps=nc // per_step, per_step=per_step,
        in_specs=[rs(256), rs(256), rs(512), rs(512), rs(256), VMEM_SPEC],
        out_specs=[rs(512), rs(512), pl.BlockSpec((per_step, 2, 128, 128), lambda i: (i, 0, 0, 0))],
        out_shape=[jax.ShapeDtypeStruct((rows, 512), F32), jax.ShapeDtypeStruct((rows, 512), BF16),
                   jax.ShapeDtypeStruct((nc, 2, 128, 128), F32)],
        scratch_shapes=[pltpu.VMEM((2, 128, 128), F32)],
        args=(gq, gk, gv, gg, bc, wgn))


def _gla_bwd(dcat, o_all, gq, gk, gv, gg, bc, sp, wgn):
    rows = gq.shape[0]
    nc = rows // BLK
    per_step = _blocks_per_step(nc)
    steps = nc // per_step
    scale = GLA_DK ** -0.5

    def body(dc_ref, o_ref, q_ref, k_ref, v_ref, gg_ref, bc_ref, sp_ref, wgn_ref,
             dq_ref, dk_ref, dv_ref, dgg_ref, dla_ref, dwgn_ref, dst):
        first = pl.program_id(0) == 0

        @pl.when(first)
        def _():
            dst[...] = jnp.zeros_like(dst)
        low, upp = _tri(True), _tri(False)
        last_row = lax.broadcasted_iota(jnp.int32, (BLK, 1), 0) == BLK - 1
        wgn_v = wgn_ref[...]

        def chunk(c, dwgn):
            rr = pl.ds(pl.multiple_of((per_step - 1 - c) * BLK, BLK), BLK)
            for p in range(2):
                sl = slice(128 * p, 128 * p + 128)
                (qt, kt, qe, kd), (e_q, e_k, e_qe, e_kd), ebl = _gla_factors(
                    q_ref[rr, sl] * scale, k_ref[rr, sl], bc_ref[rr, sl])
                s_prev = sp_ref[per_step - 1 - c, p]
                s16 = s_prev.astype(BF16)
                ds_next = dst[p]
                ds16 = ds_next.astype(BF16)
                qt16 = qt.astype(BF16)
                ds_new = ds_next * ebl
                dqt = jnp.zeros((BLK, 128), F32)
                dkt = jnp.zeros((BLK, 128), F32)
                dqe = jnp.zeros((BLK, 128), F32)
                dkd = jnp.zeros((BLK, 128), F32)
                for hh in range(2):
                    hs = slice(128 * (2 * p + hh), 128 * (2 * p + hh) + 128)
                    lm = _half_mask(128, hh)
                    on, ro = _rms(o_ref[rr, hs])
                    gate = gg_ref[rr, hs]
                    sg = _sigmoid(gate)
                    si = gate * sg
                    dog = dc_ref[rr, hs]
                    dwgn = dwgn + _colsum(dog * si * on)
                    dgg_ref[rr, hs] = dog * (on * wgn_v) * (sg * (1.0 + gate * (1.0 - sg)))
                    do16 = _rms_bwd(on, ro, wgn_v, dog * si).astype(BF16)
                    vh = v_ref[rr, hs]
                    ktm16 = jnp.where(lm, kt, 0.0).astype(BF16)
                    qtm16 = jnp.where(lm, qt, 0.0).astype(BF16)
                    qem16 = jnp.where(lm, qe, 0.0).astype(BF16)
                    kdm16 = jnp.where(lm, kd, 0.0).astype(BF16)
                    p_t = jnp.where(upp, _nt(ktm16, qt16), 0.0)
                    dp_t = jnp.where(upp, _nt(vh, do16), 0.0)
                    dp = jnp.where(low, _nt(do16, vh), 0.0)
                    dv_ref[rr, hs] = _nn(p_t.astype(BF16), do16) + _nt(kdm16, ds16)
                    dqt = dqt + _nn(dp.astype(BF16), ktm16)
                    dkt = dkt + _nn(dp_t.astype(BF16), qtm16)
                    dqe = dqe + jnp.where(lm, _nn(do16, s16), 0.0)
                    dkd = dkd + jnp.where(lm, _nn(vh, ds16), 0.0)
                    ds_new = ds_new + _tn(do16, qem16)
                debl = _colsum(ds_next * s_prev)
                dq_ref[rr, sl] = (dqt * e_q + dqe * e_qe) * scale
                dk_ref[rr, sl] = dkt * e_k + dkd * e_kd
                dkd_kd = dkd * kd
                db = dqt * qt - dkt * kt + dqe * qe - dkd_kd
                db = jnp.where(last_row, db + (_colsum(dkd_kd) + debl * ebl), db)
                dla_ref[rr, sl] = _chunk_cumsum(db, False)
                dst[p] = ds_new
            return dwgn

        def finish(dwgn):
            _acc_add(dwgn_ref, first, dwgn)
        return chunk, jnp.zeros((1, 128), F32), finish

    rev = lambda c: pl.BlockSpec((per_step * BLK, c), lambda i: (steps - 1 - i, 0))
    f32 = lambda c: jax.ShapeDtypeStruct((rows, c), F32)
    return dict(
        program=body, steps=steps, per_step=per_step,
        in_specs=[rev(512), rev(512), rev(256), rev(256), rev(512), rev(512), rev(256),
                  pl.BlockSpec((per_step, 2, 128, 128), lambda i: (steps - 1 - i, 0, 0, 0)), VMEM_SPEC],
        out_specs=[rev(256), rev(256), rev(512), rev(512), rev(256), _acc_spec(128)],
        out_shape=[f32(256), f32(256), f32(512), f32(512), f32(256), jax.ShapeDtypeStruct((8, 128), F32)],
        scratch_shapes=[pltpu.VMEM((2, 128, 128), F32)],
        args=(dcat, o_all, gq, gk, gv, gg, bc, sp, wgn))


def _swa_masks(i):
    t = lax.broadcasted_iota(jnp.int32, (BLK, BLK), 0)
    c = lax.broadcasted_iota(jnp.int32, (BLK, BLK), 1)
    own_side = c <= t
    band_ok = i >= jnp.where(own_side, 1, 2)
    meta_ok = (c % N_META) <= jnp.where(i >= 1, N_META, t - PAD_ROWS)
    return own_side, band_ok, meta_ok, c // N_META


def _swa_blocks(ref, i):
    prev = pl.multiple_of(jnp.maximum(i - 1, 0) * BLK, BLK)
    own = pl.multiple_of(i * BLK, BLK)
    return jnp.concatenate([ref[pl.ds(prev, BLK), :], ref[pl.ds(own, BLK), :]], axis=0), prev, own


def _swa_meta_operand(ref):
    blk = ref[0:BLK, :]
    swapped = pltpu.roll(blk, 64, 1)
    lo = jnp.where(_half_mask(128, 0), blk, swapped)
    hi = jnp.where(_half_mask(128, 1), blk, swapped)
    meta = jnp.concatenate([lo, lo, hi, hi], axis=1)[PAD_ROWS:BLK, :]
    tiled = jnp.concatenate([meta] * SWA_HEADS, axis=0)
    j = lax.broadcasted_iota(jnp.int32, tiled.shape, 0)
    lane = lax.broadcasted_iota(jnp.int32, tiled.shape, 1)
    return jnp.where(j // N_META == lane // SWA_HD, tiled, jnp.zeros_like(tiled))


def _swa_meta_fold(acc):
    out = jnp.zeros((N_META, 128), F32)
    for hd in range(SWA_HEADS):
        half, kv = hd % 2, hd // 4
        piece = acc[N_META * hd:N_META * (hd + 1), 128 * (hd // 2):128 * (hd // 2) + 128]
        piece = jnp.where(_half_mask(128, half), piece, 0.0)
        out = out + (piece if half == kv else pltpu.roll(piece, 64, 1))
    return out


def _by_head(group, per_head):
    out = jnp.zeros((BLK, BLK), F32)
    for hd, v in enumerate(per_head):
        out = jnp.where(group == hd, v, out)
    return out


def _place(x, kv):
    if kv == 0:
        lo = jnp.where(_half_mask(128, 0), x, jnp.zeros_like(x))
        return lo, pltpu.roll(lo, 64, 1)
    hi = jnp.where(_half_mask(128, 1), x, jnp.zeros_like(x))
    return pltpu.roll(hi, 64, 1), hi


def _swa_fwd(sq, sk, sv, sinks, wn):
    rows = sq.shape[0]
    nb = rows // BLK
    per_step = _blocks_per_step(nb)
    scale = SWA_HD ** -0.5

    def body(q_ref, k_ref, v_ref, sink_ref, wn_ref, o_ref, cat_ref, lse_ref, kp, vp):
        step = pl.program_id(0)

        @pl.when(step == 0)
        def _():
            kp[...] = _swa_meta_operand(k_ref)
            vp[...] = _swa_meta_operand(v_ref)

        def one_block(c, carry):
            i = step * per_step + c
            rr = pl.ds(pl.multiple_of(c * BLK, BLK), BLK)
            own_side, band_ok, meta_ok, group = _swa_masks(i)
            k2, _, _ = _swa_blocks(k_ref, i)
            v2, _, _ = _swa_blocks(v_ref, i)
            kz = (_place(k2, 0), _place(k2, 1))
            vz = (_place(v2, 0), _place(v2, 1))
            q_all = q_ref[rr, :]
            s_meta = jnp.where(meta_ok, _nt(q_all, kp[...]) * scale, NEG_INF)
            s_band, m = [], []
            for hd in range(SWA_HEADS):
                kv, half = hd // 4, hd % 2
                q_pair = q_all[:, 128 * (hd // 2):128 * (hd // 2) + 128]
                s2 = _nt(q_pair, kz[kv][half])
                s = jnp.where(band_ok, jnp.where(own_side, s2[:, BLK:], s2[:, :BLK]) * scale, NEG_INF)
                top = jnp.maximum(jnp.max(s, axis=-1, keepdims=True),
                                  jnp.max(jnp.where(group == hd, s_meta, NEG_INF), axis=-1, keepdims=True))
                s_band.append(s)
                m.append(jnp.maximum(top, sink_ref[0, hd]))
            e_meta = jnp.exp(s_meta - _by_head(group, m))
            o_meta = _nn(e_meta.astype(BF16), vp[...])
            outs = []
            for pr in range(4):
                o_pair = o_meta[:, 128 * pr:128 * pr + 128]
                rden = []
                for half in range(2):
                    hd = 2 * pr + half
                    kv = hd // 4
                    e = jnp.exp(s_band[hd] - m[hd])
                    den = (jnp.sum(e, axis=-1, keepdims=True)
                           + jnp.sum(jnp.where(group == hd, e_meta, 0.0), axis=-1, keepdims=True)
                           + jnp.exp(sink_ref[0, hd] - m[hd]))
                    lse_ref[rr, hd:hd + 1] = m[hd] + jnp.log(den)
                    rden.append(1.0 / den)
                    e2 = jnp.concatenate([jnp.where(own_side, 0.0, e), jnp.where(own_side, e, 0.0)], axis=1).astype(BF16)
                    o_pair = o_pair + _nn(e2, vz[kv][half])
                outs.append(o_pair * jnp.where(_half_mask(128, 0), rden[0], rden[1]))
            o = jnp.concatenate(outs, axis=1)
            o_ref[rr, :] = o
            on, _ = _rms(o)
            cat_ref[rr, :] = (on * wn_ref[...]).astype(BF16)
            return carry
        return one_block, 0, lambda carry: None

    return dict(
        program=body, steps=nb // per_step, per_step=per_step,
        in_specs=[_row_spec(per_step * BLK, 512), VMEM_SPEC, VMEM_SPEC, SMEM_SPEC, VMEM_SPEC],
        out_specs=[_row_spec(per_step * BLK, 512), _row_spec(per_step * BLK, 512), _row_spec(per_step * BLK, SWA_HEADS)],
        out_shape=[jax.ShapeDtypeStruct((rows, 512), F32), jax.ShapeDtypeStruct((rows, 512), BF16),
                   jax.ShapeDtypeStruct((rows, SWA_HEADS), F32)],
        scratch_shapes=[pltpu.VMEM((BLK, 512), BF16), pltpu.VMEM((BLK, 512), BF16)],
        args=(sq, sk, sv, sinks, wn))


def _swa_bwd(dcat, o_all, sq, sk, sv, lse, sinks, wn):
    rows = sq.shape[0]
    nb = rows // BLK
    per_step = _blocks_per_step(nb)
    steps = nb // per_step
    scale = SWA_HD ** -0.5

    def body(dc_ref, o_ref, q_ref, k_ref, v_ref, lse_ref, sink_ref, wn_ref, dq_ref, dk_ref, dv_ref, dsink_ref, dwn_ref,
             kp, vp, dkp, dvp):
        step = pl.program_id(0)

        @pl.when(step == 0)
        def _():
            dk_ref[...] = jnp.zeros_like(dk_ref)
            dv_ref[...] = jnp.zeros_like(dv_ref)
            dkp[...] = jnp.zeros_like(dkp)
            dvp[...] = jnp.zeros_like(dvp)
            kp[...] = _swa_meta_operand(k_ref)
            vp[...] = _swa_meta_operand(v_ref)

        def one_block(c, carry):
            i = step * per_step + c
            rr = pl.ds(pl.multiple_of(c * BLK, BLK), BLK)
            first = i == 0
            own_side, band_ok, meta_ok, group = _swa_masks(i)
            k2, prev, own = _swa_blocks(k_ref, i)
            v2, _, _ = _swa_blocks(v_ref, i)
            kz = (_place(k2, 0), _place(k2, 1))
            vz = (_place(v2, 0), _place(v2, 1))
            o = o_ref[rr, :]
            on, ro = _rms(o)
            dc = dc_ref[rr, :]
            _acc_add(dwn_ref, first, _colsum(dc * on))
            do = _rms_bwd(on, ro, wn_ref[...], dc)
            do_o = do * o
            do16 = do.astype(BF16)
            q_all = q_ref[rr, :]
            lse = [lse_ref[rr, hd:hd + 1] for hd in range(SWA_HEADS)]
            delta = [jnp.sum(jnp.where(_half_mask(128, hd % 2), do_o[:, 128 * (hd // 2):128 * (hd // 2) + 128], 0.0),
                             axis=-1, keepdims=True) for hd in range(SWA_HEADS)]
            s_meta = jnp.where(meta_ok, _nt(q_all, kp[...]) * scale, NEG_INF)
            p_meta = jnp.exp(s_meta - _by_head(group, lse))
            ds_meta16 = (p_meta * (_nt(do16, vp[...]) - _by_head(group, delta)) * scale).astype(BF16)
            dq_meta = _nn(ds_meta16, kp[...])
            dkp[...] += _tn(ds_meta16, q_all)
            dvp[...] += _tn(p_meta.astype(BF16), do16)
            own2 = jnp.concatenate([own_side.astype(jnp.int32)] * 2, axis=0) > 0
            ok2 = jnp.concatenate([band_ok.astype(jnp.int32)] * 2, axis=0) > 0

            def window(x2):
                return jnp.where(own2, x2[:, BLK:], x2[:, :BLK])

            def unwindow(x):
                return jnp.concatenate([jnp.where(own2, 0.0, x), jnp.where(own2, x, 0.0)], axis=1).astype(BF16)
            lane8 = lax.broadcasted_iota(jnp.int32, (1, 128), 1)
            dsink = jnp.zeros((1, 128), F32)
            dq_pairs = [dq_meta[:, 128 * pr:128 * pr + 128] for pr in range(4)]
            dk2 = [[None, None], [None, None]]
            dv2 = [[None, None], [None, None]]
            for kv in range(2):
                for half in range(2):
                    heads, pairs = (4 * kv + half, 4 * kv + 2 + half), (2 * kv, 2 * kv + 1)
                    q_s = jnp.concatenate([q_all[:, 128 * pr:128 * pr + 128] for pr in pairs], axis=0)
                    do_s = jnp.concatenate([do16[:, 128 * pr:128 * pr + 128] for pr in pairs], axis=0)
                    lse_s = jnp.concatenate([lse[hd] for hd in heads], axis=0)
                    delta_s = jnp.concatenate([delta[hd] for hd in heads], axis=0)
                    s = jnp.where(ok2, window(_nt(q_s, kz[kv][half])) * scale, NEG_INF)
                    prob = jnp.exp(s - lse_s)
                    for hd in heads:
                        dsink = dsink + jnp.where(lane8 == hd, -jnp.sum(jnp.exp(sink_ref[0, hd] - lse[hd]) * delta[hd]), 0.0)
                    ds2 = unwindow(prob * (window(_nt(do_s, vz[kv][half])) - delta_s) * scale)
                    dq_s = _nn(ds2, kz[kv][half])
                    dq_pairs[pairs[0]] = dq_pairs[pairs[0]] + dq_s[:BLK]
                    dq_pairs[pairs[1]] = dq_pairs[pairs[1]] + dq_s[BLK:]
                    dk2[kv][half] = _tn(ds2, q_s)
                    dv2[kv][half] = _tn(unwindow(prob), do_s)
            dq_ref[rr, :] = jnp.concatenate(dq_pairs, axis=1)
            _acc_add(dsink_ref, first, dsink)
            for ref, acc2 in ((dk_ref, dk2), (dv_ref, dv2)):
                tot = jnp.zeros((2 * BLK, 128), F32)
                for kv in range(2):
                    for half in range(2):
                        part = jnp.where(_half_mask(128, half), acc2[kv][half], 0.0)
                        tot = tot + (part if half == kv else pltpu.roll(part, 64, 1))
                ref[pl.ds(prev, BLK), :] += tot[:BLK]
                ref[pl.ds(own, BLK), :] += tot[BLK:]
            return carry

        def finish(carry):
            del carry

            @pl.when(step == steps - 1)
            def _():
                dk_ref[PAD_ROWS:BLK, :] += _swa_meta_fold(dkp[...])
                dv_ref[PAD_ROWS:BLK, :] += _swa_meta_fold(dvp[...])
        return one_block, jnp.zeros((1, 128), F32), finish

    full = pl.BlockSpec((rows, 128), lambda i: (0, 0))
    blocks = lambda cols: _row_spec(per_step * BLK, cols)
    return dict(
        program=body, steps=steps, per_step=per_step,
        in_specs=[blocks(512), blocks(512), blocks(512), VMEM_SPEC, VMEM_SPEC, blocks(SWA_HEADS), SMEM_SPEC, VMEM_SPEC],
        out_specs=[blocks(512), full, full, _acc_spec(128), _acc_spec(512)],
        out_shape=[jax.ShapeDtypeStruct((rows, 512), F32), jax.ShapeDtypeStruct((rows, 128), F32),
                   jax.ShapeDtypeStruct((rows, 128), F32), jax.ShapeDtypeStruct((8, 128), F32),
                   jax.ShapeDtypeStruct((8, 512), F32)],
        scratch_shapes=[pltpu.VMEM((BLK, 512), BF16), pltpu.VMEM((BLK, 512), BF16),
                        pltpu.VMEM((BLK, 512), F32), pltpu.VMEM((BLK, 512), F32)],
        args=(dcat, o_all, sq, sk, sv, lse, sinks, wn))


def _mix_out_bwd(dh, m, wout, gpost):
    rows = dh.shape[0]
    tm = _row_tile(rows)

    def body(dh_ref, m_ref, w_ref, g_ref, dcg_ref, dcs_ref, dm_ref, dg_ref):
        first = pl.program_id(0) == 0
        dhv = dh_ref[...]
        mn, rm = _rms(m_ref[...])
        _acc_add(dg_ref, first, _colsum(dhv * mn))
        dm16 = _rms_bwd(mn, rm, g_ref[...], dhv).astype(BF16)
        dm_ref[...] = dm16
        dcat = _nt(dm16, w_ref[...])
        dcg_ref[...] = dcat[:, 0:512]
        dcs_ref[...] = dcat[:, 512:1024]

    row_f32 = _row_spec(tm, D_MODEL)
    return pl.pallas_call(
        body, name="mix_out_bwd", grid=(rows // tm,),
        in_specs=[row_f32, row_f32, VMEM_SPEC, VMEM_SPEC],
        out_specs=[_row_spec(tm, 512), _row_spec(tm, 512), row_f32, _acc_spec(D_MODEL)],
        out_shape=[jax.ShapeDtypeStruct((rows, 512), F32), jax.ShapeDtypeStruct((rows, 512), F32),
                   jax.ShapeDtypeStruct((rows, D_MODEL), BF16), jax.ShapeDtypeStruct((8, D_MODEL), F32)],
        compiler_params=_params(("arbitrary",)),
    )(dh, m, wout, gpost)


def _mix_in_bwd(dh_out, h, g, win_p, wa2_p, cos, sin, loga, ga, dgq, dgk, dgv, dgg, dsq, dsk, dsv, dloga):
    rows = h.shape[0]
    tm = _row_tile(rows)

    def body(dho_ref, h_ref, g_ref, win_ref, wa2_ref, cos_ref, sin_ref, loga_ref, ga_ref,
             dgq_ref, dgk_ref, dgv_ref, dgg_ref, dsq_ref, dsk_ref, dsv_ref, dla_ref,
             dh_ref, dproj_ref, dwa2_ref, dg_ref, dba_ref):
        first = pl.program_id(0) == 0
        dz = dla_ref[...] * (1.0 / GLA_TAU) * (1.0 - jnp.exp(GLA_TAU * loga_ref[...]))
        _acc_add(dba_ref, first, _colsum(dz))
        dga = _nt(dz, wa2_ref[...])
        pa = _tn(ga_ref[...], dz)
        c1, s1 = cos_ref[...], sin_ref[...]
        c4 = jnp.concatenate([c1, c1, c1, c1], axis=1)
        s4 = jnp.concatenate([s1, s1, s1, s1], axis=1)
        dq_r, dk_r = dsq_ref[...], dsk_ref[...]
        dsq = dq_r * c4 - _rot_half(dq_r * s4)
        dsk = dk_r * c1 - _rot_half(dk_r * s1)
        dproj16 = jnp.concatenate(
            [dgq_ref[...], dgk_ref[...], dgv_ref[...], dgg_ref[...], dsq, dsk, dsv_ref[...], dga], axis=1).astype(BF16)
        dproj_ref[...] = dproj16
        dn = _nn(dproj16, win_ref[...])

        @pl.when(first)
        def _():
            dwa2_ref[...] = pa

        @pl.when(jnp.logical_not(first))
        def _():
            dwa2_ref[...] += pa
        hn, rh = _rms(h_ref[...])
        _acc_add(dg_ref, first, _colsum(dn * hn))
        dh_ref[...] = dho_ref[...] + _rms_bwd(hn, rh, g_ref[...], dn)

    rs = lambda c: _row_spec(tm, c)
    return pl.pallas_call(
        body, name="mix_in_bwd", grid=(rows // tm,),
        in_specs=[rs(D_MODEL), rs(D_MODEL), VMEM_SPEC, VMEM_SPEC, VMEM_SPEC, rs(128), rs(128), rs(256), rs(128),
                  rs(256), rs(256), rs(512), rs(512), rs(512), rs(128), rs(128), rs(256)],
        out_specs=[rs(D_MODEL), rs(P_END), pl.BlockSpec((128, 256), lambda i: (0, 0)), _acc_spec(D_MODEL), _acc_spec(256)],
        out_shape=[jax.ShapeDtypeStruct((rows, D_MODEL), F32), jax.ShapeDtypeStruct((rows, P_END), BF16),
                   jax.ShapeDtypeStruct((128, 256), F32), jax.ShapeDtypeStruct((8, D_MODEL), F32),
                   jax.ShapeDtypeStruct((8, 256), F32)],
        compiler_params=_params(("arbitrary",)),
    )(dh_out, h, g, win_p, wa2_p, cos, sin, loga, ga, dgq, dgk, dgv, dgg, dsq, dsk, dsv, dloga)


def _rope_tables(rows):
    pos = (jnp.arange(rows, dtype=jnp.int32) - PAD_ROWS).astype(F32)
    inv_freq = 1.0 / (ROPE_THETA ** (jnp.arange(0, SWA_HD, 2, dtype=F32) / SWA_HD))
    ang = pos[:, None] * inv_freq[None, :]
    return jnp.tile(jnp.cos(ang), (1, 4)), jnp.tile(jnp.sin(ang), (1, 4))


def _local_step(x, tgt, front, w, late_weights=None, on_grads=None, on_small=None):
    cos, sin = _rope_tables(x.shape[0] + BLK)
    g = {}

    def tell(group, names):
        for nm in names:
            g[nm] = grads_now[nm]
        return None if on_grads is None else on_grads(group, {nm: grads_now[nm] for nm in names})

    h1, a1, b1, s1, f1 = _ffn_fwd(x, w["ffn1_pre"], w["wg1"], w["wu1"], w["wd1"], w["ffn1_post"], front=front)
    if late_weights is not None:
        w = {**w, **late_weights("win", f1)}
    gq, gk, gv, gg, sq, sk, sv, ga, loga, bc, n2 = _mix_in(h1, w["mix_pre"], w["win"], w["wa2"], w["b_a"], cos, sin)
    (o_g, cat_g, sp), (o_s, cat_s, lse) = _side_by_side(
        [_gla_fwd(gq, gk, gv, gg, bc, w["gla_norm"]), _swa_fwd(sq, sk, sv, w["sinks"], w["swa_norm"])], "attention_fwd")
    if late_weights is not None:
        w = {**w, **late_weights("rest", lse)}
    h2, m, a2, b2, s2, f2, dy, loss = _ffn_fwd(h1, w["ffn2_pre"], w["wg2"], w["wu2"], w["wd2"], w["ffn2_post"], tgt,
                                               mixed=(cat_g, cat_s, w["wout"], w["mix_post"]))
    dh2, da, db, df, n3, g["ffn2_pre"], g["ffn2_post"] = _ffn_bwd_act(
        dy, h2, a2, b2, f2, w["ffn2_pre"], w["ffn2_post"], w["wg2"], w["wu2"], w["wd2"], "ffn2_bwd_act")
    grads_now = dict(wd2=_wgrad(s2, df, "ffn2_wgrad_down"), wg2=_wgrad(da, n3, "ffn2_wgrad_gate"),
                     wu2=_wgrad(db, n3, "ffn2_wgrad_up"))
    tok = tell("ffn2", ("wd2", "wg2", "wu2"))
    dcg, dcs, dm, g["mix_post"] = _mix_out_bwd(dh2, m, w["wout"], w["mix_post"] + (0.0 if tok is None else tok[0, 0]))
    (dgq, dgk, dgv, dgg, dloga, g["gla_norm"]), (dsq, dsk, dsv, g["sinks"], g["swa_norm"]) = _side_by_side(
        [_gla_bwd(dcg, o_g, gq, gk, gv, gg, bc, sp, w["gla_norm"]),
         _swa_bwd(dcs, o_s, sq, sk, sv, lse, w["sinks"], w["swa_norm"])], "attention_bwd")
    dh1, dproj, g["wa2"], g["mix_pre"], g["b_a"] = _mix_in_bwd(
        dh2, h1, w["mix_pre"], w["win"], w["wa2"], cos, sin, loga, ga, dgq, dgk, dgv, dgg, dsq, dsk, dsv, dloga)
    dh0, da, db, df, n1, g["ffn1_pre"], g["ffn1_post"] = _ffn_bwd_act(
        dh1, x, a1, b1, f1, w["ffn1_pre"], w["ffn1_post"], w["wg1"], w["wu1"], w["wd1"], "ffn1_bwd_act", front=front)
    tok = None if on_small is None else on_small(loss[0, 0], dh0, g)
    grads_now = dict(wd1=_wgrad(s1, df, "ffn1_wgrad_down", after=tok))
    tok = tell("ffn1_down", ("wd1",))
    grads_now = dict(wg1=_wgrad(da, n1, "ffn1_wgrad_gate", after=tok))
    tok = tell("ffn1_gate", ("wg1",))
    grads_now = dict(wu1=_wgrad(db, n1, "ffn1_wgrad_up", after=tok))
    tok = tell("ffn1_up", ("wu1",))
    grads_now = dict(wout=jnp.concatenate([_wgrad(cat_g, dm, "wout_wgrad_gla", after=tok),
                                           _wgrad(cat_s, dm, "wout_wgrad_swa", after=tok)], axis=0))
    tok = tell("wout", ("wout",))
    grads_now = dict(win=_wgrad(dproj, n2, "win_wgrad", after=tok))
    tell("mix", ("win",))
    return loss[0, 0], dh0, g


def _win_pad_rows(win_t):
    pad = jnp.zeros((P_END - P_GA - 16, win_t.shape[1]), win_t.dtype)
    return jnp.concatenate([win_t[0:1536], win_t[1552:2320], win_t[1536:1552], pad], axis=0)


def _win_unpad_rows(win_p):
    return jnp.concatenate([win_p[0:1536], win_p[P_GA:P_GA + 16], win_p[1536:P_GA]], axis=0)


def _place_on_mesh():
    return lax.axis_index("x"), lax.axis_index("y"), lax.axis_index("c")


def _dev_index(px, py, pc):
    return 4 * px + 2 * py + pc


def _other_devices(x, y, c):
    flip = lambda v, f: 1 - v if f else v
    return [(flip(x, fx), flip(y, fy), flip(c, fc)) for fx in (0, 1) for fy in (0, 1) for fc in (0, 1)][1:]


def _all_gather(shards):
    n = len(shards)

    def body(*refs):
        ins, outs = refs[:n], refs[n:2 * n]
        zeros_ref, send_sems, recv_sems, local_sems = refs[2 * n:]
        zeros_ref[...] = jnp.zeros_like(zeros_ref)
        x, y, c = _place_on_mesh()
        me, sibling = (x, y, c), (x, y, 1 - c)
        chips = [(1 - x, y), (x, 1 - y), (1 - x, 1 - y)]

        def rows(k, px, py, pc):
            r = ins[k].shape[0]
            return outs[k].at[pl.ds(pl.multiple_of(_dev_index(px, py, pc) * r, 8), r), :]

        def copy(k, slot, block, to, src=None):
            return pltpu.make_async_remote_copy(
                src_ref=rows(k, *block) if src is None else src, dst_ref=rows(k, *block),
                send_sem=send_sems.at[k, slot], recv_sem=recv_sems.at[k, slot], device_id=to, device_id_type=MESH)

        local = [pltpu.make_async_copy(ins[k], rows(k, *me), local_sems.at[k]) for k in range(n)]
        sends = []
        for k in range(n):
            local[k].start()
            sends.append(copy(k, 0, me, sibling, src=ins[k]))
            sends += [copy(k, 1 + j, me, (*chip, c), src=ins[k]) for j, chip in enumerate(chips)]
        for cp in sends:
            cp.start()
        for k in range(n):
            for j, chip in enumerate(chips):
                copy(k, 1 + j, (*chip, c), me).wait_recv()
                passed = copy(k, 4 + j, (*chip, c), sibling)
                passed.start()
                sends.append(passed)
        for k in range(n):
            copy(k, 0, sibling, me).wait_recv()
            for j, chip in enumerate(chips):
                copy(k, 4 + j, (*chip, 1 - c), me).wait_recv()
        for cp in sends:
            cp.wait_send()
        for cp in local:
            cp.wait()

    return pl.pallas_call(
        body, name="all_gather_weights",
        in_specs=[ANY_SPEC] * n, out_specs=[ANY_SPEC] * n + [VMEM_SPEC],
        out_shape=[jax.ShapeDtypeStruct((N_DEV * s.shape[0], s.shape[1]), s.dtype) for s in shards]
        + [jax.ShapeDtypeStruct((8, 128), F32)],
        scratch_shapes=[pltpu.SemaphoreType.DMA((n, 7)), pltpu.SemaphoreType.DMA((n, 7)), pltpu.SemaphoreType.DMA((n,))],
    )(*shards)


HBM_SPEC = pl.BlockSpec(memory_space=pltpu.HBM)
SEM_SPEC = pl.BlockSpec(memory_space=pltpu.SEMAPHORE)
DATAFLOW = pltpu.SideEffectType.DATAFLOW_SIDE_EFFECTING


GATHER, SCATTER, SCATTER_CHIPS = "gather", "scatter", "scatter among chips"


def _exchange_peers(kind):
    x, y, c = _place_on_mesh()
    if kind == SCATTER_CHIPS:
        peers = [(1 - x, y, c), (x, 1 - y, c), (1 - x, 1 - y, c)]
        return peers, [2 * p[0] + p[1] for p in peers], 2 * x + y, 4
    peers = _other_devices(x, y, c)
    return peers, [_dev_index(*p) for p in peers], _dev_index(x, y, c), N_DEV


def _exchange_copies(srcs, lands, send_sems, recv_sems, own_sems, kind, arriving):
    peers, theirs, me, blocks = _exchange_peers(kind)
    remote, local = [], []
    for k, (src, land) in enumerate(zip(srcs, lands)):
        r = land.shape[0] // blocks

        def block(ref, d):
            return ref.at[pl.ds(pl.multiple_of(d * r, 8), r), :]

        for f, (peer, him) in enumerate(zip(peers, theirs)):
            mine, his = (him, me) if arriving else (me, him)
            sem = len(peers) * k + f
            remote.append(pltpu.make_async_remote_copy(
                src_ref=src if kind == GATHER else block(src, his), dst_ref=block(land, mine),
                send_sem=send_sems.at[sem], recv_sem=recv_sems.at[sem], device_id=peer, device_id_type=MESH))
        local.append(pltpu.make_async_copy(src if kind == GATHER else block(src, me), block(land, me), own_sems.at[k]))
    return remote, local


def _exchange_start(srcs, kind, name):
    n = len(srcs)
    lands = [lax.empty((N_DEV * s.shape[0], s.shape[1]) if kind == GATHER else s.shape, s.dtype) for s in srcs]
    sems = (3 if kind == SCATTER_CHIPS else 7) * n

    def body(*refs):
        remote, local = _exchange_copies(refs[:n], refs[n:2 * n], *refs[2 * n:2 * n + 3], kind, False)
        for cp in remote + local:
            cp.start()
        refs[-1][...] = jnp.zeros_like(refs[-1])

    both = list(srcs) + list(lands)
    outs = pl.pallas_call(
        body, name=name,
        out_shape=(pltpu.SemaphoreType.DMA((sems,)), pltpu.SemaphoreType.DMA((sems,)), pltpu.SemaphoreType.DMA((n,)),
                   *[pltpu.HBM(a.shape, a.dtype) for a in both], jax.ShapeDtypeStruct((8, 128), F32)),
        in_specs=[HBM_SPEC] * (2 * n), out_specs=(SEM_SPEC, SEM_SPEC, SEM_SPEC, *[HBM_SPEC] * (2 * n), VMEM_SPEC),
        input_output_aliases={i: 3 + i for i in range(2 * n)},
        compiler_params=pltpu.CompilerParams(has_side_effects=DATAFLOW),
    )(*[pltpu.with_memory_space_constraint(a, pltpu.HBM) for a in both])
    return outs[0:3], outs[3:3 + n], outs[3 + n:3 + 2 * n], outs[-1]


def _exchange_wait(started, kind, after, name):
    sems, srcs, lands, _ = started
    n = len(srcs)

    def body(*refs):
        args = (refs[:n], refs[n:2 * n], *refs[2 * n:2 * n + 3], kind)
        going, local = _exchange_copies(*args, False)
        for cp in going:
            cp.wait_send()
        for cp in local:
            cp.wait()
        for cp in _exchange_copies(*args, True)[0]:
            cp.wait_recv()

    both = list(srcs) + list(lands)
    outs = pl.pallas_call(
        body, name=name, out_shape=[pltpu.HBM(a.shape, a.dtype) for a in both],
        in_specs=[HBM_SPEC] * (2 * n) + [SEM_SPEC, SEM_SPEC, SEM_SPEC, ANY_SPEC], out_specs=[HBM_SPEC] * (2 * n),
        input_output_aliases={i: i for i in range(2 * n)},
        compiler_params=pltpu.CompilerParams(has_side_effects=DATAFLOW),
    )(*both, *sems, after)
    return outs[n:]


def _sibling_reduce(part, name):
    r, cols = part.shape[0] // N_DEV, part.shape[1]

    def body(p_ref, o_ref, mine, got, send_sems, recv_sems, own_sems):
        x, y, c = _place_on_mesh()

        def block(d):
            return p_ref.at[pl.ds(pl.multiple_of(d * r, 8), r), :]
        swaps = [pltpu.make_async_remote_copy(
            src_ref=block(2 * j + 1 - c), dst_ref=got.at[j], send_sem=send_sems.at[j], recv_sem=recv_sems.at[j],
            device_id=(x, y, 1 - c), device_id_type=MESH) for j in range(4)]
        keeps = [pltpu.make_async_copy(block(2 * j + c), mine.at[j], own_sems.at[j]) for j in range(4)]
        for cp in swaps + keeps:
            cp.start()
        for j in range(4):
            keeps[j].wait()
            swaps[j].wait()
            o_ref[pl.ds(j * r, r), :] = (mine[j].astype(F32) + got[j].astype(F32)).astype(o_ref.dtype)

    return pl.pallas_call(
        body, name=name, in_specs=[ANY_SPEC], out_specs=VMEM_SPEC,
        out_shape=jax.ShapeDtypeStruct((4 * r, cols), part.dtype),
        scratch_shapes=[pltpu.VMEM((4, r, cols), part.dtype), pltpu.VMEM((4, r, cols), part.dtype),
                        pltpu.SemaphoreType.DMA((4,)), pltpu.SemaphoreType.DMA((4,)), pltpu.SemaphoreType.DMA((4,))],
        compiler_params=pltpu.CompilerParams(vmem_limit_bytes=32 << 20),
    )(part)


def _sum_partials(parts, name, blocks=N_DEV):
    n = len(parts)

    def body(*refs):
        ins, outs = refs[:n], refs[n:]
        first = pl.program_id(0) == 0
        for i_ref, o_ref in zip(ins, outs):
            v = i_ref[...].astype(F32)

            @pl.when(first)
            def _():
                o_ref[...] = v

            @pl.when(jnp.logical_not(first))
            def _():
                o_ref[...] += v

    shapes = [(p.shape[0] // blocks, p.shape[1]) for p in parts]
    return pl.pallas_call(
        body, name=name, grid=(blocks,),
        in_specs=[pl.BlockSpec(s, lambda j: (j, 0)) for s in shapes],
        out_specs=[pl.BlockSpec(s, lambda j: (0, 0)) for s in shapes],
        out_shape=[jax.ShapeDtypeStruct(s, F32) for s in shapes],
        compiler_params=_params(("arbitrary",)),
    )(*parts)


def _adamw_update(w, g, m, v):
    m = ADAM_B1 * m + (1.0 - ADAM_B1) * g
    v = ADAM_B2 * v + (1.0 - ADAM_B2) * (g * g)
    m_hat = m * (1.0 / (1.0 - ADAM_B1 ** ADAM_STEP))
    v_hat = v * (1.0 / (1.0 - ADAM_B2 ** ADAM_STEP))
    return -ADAM_LR * (m_hat / (jnp.sqrt(v_hat) + ADAM_EPS) + ADAM_WD * w), m, v


def _sum_adamw(parts, w, m, v, blocks, name):
    shape = w.shape

    def body(p_ref, w_ref, m_ref, v_ref, g_ref, d_ref, mo_ref, vo_ref):
        j = pl.program_id(0)
        part = p_ref[...].astype(F32)

        @pl.when(j == 0)
        def _():
            g_ref[...] = part

        @pl.when(j > 0)
        def _():
            g_ref[...] += part

        @pl.when(j == blocks - 1)
        def _():
            d_ref[...], mo_ref[...], vo_ref[...] = _adamw_update(w_ref[...], g_ref[...], m_ref[...], v_ref[...])

    held = pl.BlockSpec(shape, lambda j: (0, 0))
    return pl.pallas_call(
        body, name=name, grid=(blocks,),
        in_specs=[pl.BlockSpec(shape, lambda j: (j, 0)), held, held, held],
        out_specs=[held] * 4, out_shape=[jax.ShapeDtypeStruct(shape, F32)] * 4,
        compiler_params=_params(("arbitrary",)),
    )(parts, w, m, v)


def _adamw(ws, gs, ms, vs, name):
    n = len(ws)

    def body(*refs):
        w_r, g_r, m_r, v_r = refs[:n], refs[n:2 * n], refs[2 * n:3 * n], refs[3 * n:4 * n]
        d_o, m_o, v_o = refs[4 * n:5 * n], refs[5 * n:6 * n], refs[6 * n:7 * n]
        for k in range(n):
            d_o[k][...], m_o[k][...], v_o[k][...] = _adamw_update(w_r[k][...], g_r[k][...], m_r[k][...], v_r[k][...])

    shapes = [jax.ShapeDtypeStruct(w.shape, F32) for w in ws]
    outs = pl.pallas_call(
        body, name=name, in_specs=[VMEM_SPEC] * (4 * n), out_specs=[VMEM_SPEC] * (3 * n), out_shape=shapes * 3,
        compiler_params=pltpu.CompilerParams(vmem_limit_bytes=56 << 20),
    )(*ws, *gs, *ms, *vs)
    return outs[:n], outs[n:2 * n], outs[2 * n:]


WEIGHT_NAMES = ("meta_tokens", "ffn1_pre_norm", "ffn1_w_gate", "ffn1_w_up", "ffn1_w_down", "ffn1_post_norm", "mix_pre_norm",
                "w_in", "gla_w_a2", "gla_b_a", "gla_out_norm", "swa_sinks", "swa_out_norm", "w_out", "mix_post_norm",
                "ffn2_pre_norm", "ffn2_w_gate", "ffn2_w_up", "ffn2_w_down", "ffn2_post_norm")
WIN_SHARD = D_IN // N_DEV
WIN_SHARD_PAD = 304
SLAB_VECTORS = ("ffn1_pre", "ffn1_post", "mix_pre", "mix_post", "ffn2_pre", "ffn2_post")
SLAB_ROWS = 32


def kernel(x, meta_tokens, ffn1_pre_norm, ffn1_w_gate, ffn1_w_up, ffn1_w_down, ffn1_post_norm, mix_pre_norm, w_in, gla_w_a2, gla_b_a, gla_out_norm, swa_sinks, swa_out_norm, w_out, mix_post_norm, ffn2_pre_norm, ffn2_w_gate, ffn2_w_up, ffn2_w_down, ffn2_post_norm, loss_target, m_meta_tokens, m_ffn1_pre_norm, m_ffn1_w_gate, m_ffn1_w_up, m_ffn1_w_down, m_ffn1_post_norm, m_mix_pre_norm, m_w_in, m_gla_w_a2, m_gla_b_a, m_gla_out_norm, m_swa_sinks, m_swa_out_norm, m_w_out, m_mix_post_norm, m_ffn2_pre_norm, m_ffn2_w_gate, m_ffn2_w_up, m_ffn2_w_down, m_ffn2_post_norm, v_meta_tokens, v_ffn1_pre_norm, v_ffn1_w_gate, v_ffn1_w_up, v_ffn1_w_down, v_ffn1_post_norm, v_mix_pre_norm, v_w_in, v_gla_w_a2, v_gla_b_a, v_gla_out_norm, v_swa_sinks, v_swa_out_norm, v_w_out, v_mix_post_norm, v_ffn2_pre_norm, v_ffn2_w_gate, v_ffn2_w_up, v_ffn2_w_down, v_ffn2_post_norm):
    given = dict(locals())
    W = {n: given[n] for n in WEIGHT_NAMES}
    M = {n: given["m_" + n] for n in WEIGHT_NAMES}
    V = {n: given["v_" + n] for n in WEIGHT_NAMES}
    dev = _dev_index(*_place_on_mesh())

    def t16(w):
        return w[0].T.astype(BF16)

    small = jnp.concatenate([W["meta_tokens"], jnp.pad(W["gla_w_a2"][0], ((0, 0), (0, 96)))], axis=0)
    wg1, wu1, wd1, small_g, gathered_zeros = _all_gather(
        [t16(W["ffn1_w_gate"]), t16(W["ffn1_w_up"]), W["ffn1_w_down"][0].astype(BF16), small])
    def after_zero(shard, zeros):
        return shard + zeros[0:1, 0:1].astype(shard.dtype)
    win_shard = jnp.pad(t16(W["w_in"]), ((0, WIN_SHARD_PAD - WIN_SHARD), (0, 0)))
    win_shard = after_zero(win_shard, gathered_zeros)
    mid = _exchange_start([win_shard], GATHER, "gather_w_in_start")
    late_shards = [after_zero(W["w_out"][0].astype(BF16), mid[3]), t16(W["ffn2_w_gate"]), t16(W["ffn2_w_up"]),
                   W["ffn2_w_down"][0].astype(BF16)]
    late = _exchange_start(late_shards, GATHER, "gather_late_weights_start")

    def late_weights(what, after):
        if what == "win":
            win_g, = _exchange_wait(mid, GATHER, after, "gather_w_in_wait")
            win_t = win_g.reshape(N_DEV, WIN_SHARD_PAD, D_MODEL)[:, :WIN_SHARD].reshape(D_IN, D_MODEL)
            return dict(win=_win_pad_rows(win_t))
        wout, wg2, wu2, wd2 = _exchange_wait(late, GATHER, after, "gather_late_weights_wait")
        return dict(wout=wout, wg2=wg2, wu2=wu2, wd2=wd2)

    small_g = small_g.reshape(N_DEV, 32, 128)
    meta_full = small_g[:, :N_META].transpose(1, 0, 2).reshape(N_META, D_MODEL)
    wa2_full = small_g[:, N_META:, :32].transpose(1, 0, 2).reshape(16, 256)
    w = dict(
        ffn1_pre=W["ffn1_pre_norm"] + late[3][0, 0], ffn1_post=W["ffn1_post_norm"], mix_pre=W["mix_pre_norm"],
        mix_post=W["mix_post_norm"], ffn2_pre=W["ffn2_pre_norm"], ffn2_post=W["ffn2_post_norm"], b_a=W["gla_b_a"],
        gla_norm=W["gla_out_norm"], sinks=W["swa_sinks"], swa_norm=W["swa_out_norm"], wg1=wg1, wu1=wu1, wd1=wd1,
        wa2=jnp.pad(wa2_full, ((0, 112), (0, 0))))

    in_flight = []

    def on_grads(group, grads):
        parts = []
        for nm, p in grads.items():
            if nm == "win":
                p = _win_unpad_rows(p).reshape(N_DEV, WIN_SHARD, D_MODEL)
                p = jnp.pad(p, ((0, 0), (0, WIN_SHARD_PAD - WIN_SHARD), (0, 0))).reshape(N_DEV * WIN_SHARD_PAD, D_MODEL)
            parts.append(p)
        kind = SCATTER_CHIPS if group == "mix" else SCATTER
        if kind == SCATTER_CHIPS:
            parts = [_sibling_reduce(p, "pair_" + group + "_" + nm) for nm, p in zip(grads, parts)]
        started = _exchange_start(parts, kind, "scatter_" + group + "_start")
        in_flight.append((group, list(grads), started, kind))
        return started[3]

    small_flight = []

    def on_small(loss, dh0, g):
        packed = jnp.concatenate([g["b_a"][0:1], g["gla_norm"][0:1], g["sinks"][0:1], g["swa_norm"][0:1]], axis=1)
        slab = jnp.concatenate([g[k][0:1] for k in SLAB_VECTORS] + [packed, jnp.full((1, D_MODEL), loss, F32),
                               g["wa2"][:16].reshape(4, D_MODEL), jnp.zeros((4, D_MODEL), F32), dh0[PAD_ROWS:BLK]], axis=0)
        small_flight.append(_exchange_start([slab], GATHER, "gather_small_grads_start"))
        return small_flight[0][3]

    front = jnp.concatenate([jnp.zeros((PAD_ROWS, D_MODEL), F32), meta_full], axis=0)
    loss, dh0, g = _local_step(x[0], loss_target[0], front, w, late_weights, on_grads, on_small)
    grad_x = dh0[BLK:][None]

    land, = _exchange_wait(small_flight[0], GATHER, in_flight[-1][2][3], "gather_small_grads_wait")
    tot = _sum_partials([land], "sum_small_grads")[0]
    loss = tot[7, 0]
    small_grads = dict(
        ffn1_pre_norm=tot[0:1], ffn1_post_norm=tot[1:2], mix_pre_norm=tot[2:3], mix_post_norm=tot[3:4],
        ffn2_pre_norm=tot[4:5], ffn2_post_norm=tot[5:6], gla_b_a=tot[6:7, 0:256], gla_out_norm=tot[6:7, 256:384],
        swa_sinks=tot[6:7, 384:392], swa_out_norm=tot[6:7, 512:1024],
        gla_w_a2=lax.dynamic_slice_in_dim(tot[8:12].reshape(16, 256), dev * 32, 32, axis=1)[None],
        meta_tokens=lax.dynamic_slice_in_dim(tot[16:32], dev * 128, 128, axis=1))

    big = dict(wg1=("ffn1_w_gate", True), wu1=("ffn1_w_up", True), wd1=("ffn1_w_down", False), win=("w_in", True),
               wout=("w_out", False), wg2=("ffn2_w_gate", True), wu2=("ffn2_w_up", True), wd2=("ffn2_w_down", False))
    grads = dict(small_grads)
    delta, new_m, new_v = {}, {}, {}
    names = [n for n in WEIGHT_NAMES if n not in [full for full, _ in big.values()]]
    two_d = lambda a: a.reshape(-1, a.shape[-1])
    d_, m_, v_ = _adamw([two_d(W[n]) for n in names], [two_d(grads[n]) for n in names],
                        [two_d(M[n]) for n in names], [two_d(V[n]) for n in names], "adamw_small")
    for k, n in enumerate(names):
        delta[n], new_m[n], new_v[n] = d_[k].reshape(W[n].shape), m_[k].reshape(W[n].shape), v_[k].reshape(W[n].shape)

    before_wait = d_[0] + in_flight[-1][2][3][0, 0]
    for group, shorts, started, kind in in_flight:
        lands = _exchange_wait(started, kind, before_wait, "scatter_" + group + "_wait")
        blocks = 4 if kind == SCATTER_CHIPS else N_DEV
        for short, land in zip(shorts, lands):
            n, transposed = big[short]
            to_slab = (lambda a: a[0].T) if transposed else (lambda a: a[0])
            from_slab = (lambda a: a.T[None]) if transposed else (lambda a: a[None])
            if short == "win":
                g_slab = _sum_partials([land], "sum_" + n, blocks)[0][:WIN_SHARD]
                d_, m_, v_ = _adamw([to_slab(W[n])], [g_slab], [to_slab(M[n])], [to_slab(V[n])], "adamw_" + n)
                d_, m_, v_ = d_[0], m_[0], v_[0]
            else:
                g_slab, d_, m_, v_ = _sum_adamw(land, to_slab(W[n]), to_slab(M[n]), to_slab(V[n]), blocks, "adamw_" + n)
            grads[n], delta[n], new_m[n], new_v[n] = from_slab(g_slab), from_slab(d_), from_slab(m_), from_slab(v_)
            before_wait = d_
    return (loss, grad_x, *[grads[n] for n in WEIGHT_NAMES], *[delta[n] for n in WEIGHT_NAMES],
            *[new_m[n] for n in WEIGHT_NAMES], *[new_v[n] for n in WEIGHT_NAMES])
```

```python
import math

import jax
import jax.numpy as jnp
from jax import lax
from jax.experimental import pallas as pl
from jax.experimental.pallas import tpu as pltpu

F32, BF16 = jnp.float32, jnp.bfloat16

D_MODEL = 1024
D_FF = 2816
N_META = 16
BLK = 128
PAD_ROWS = BLK - N_META
GLA_DK = 64
SWA_HD = 64
SWA_HEADS = 8
GLA_TAU = 16.0
NORM_EPS = 1e-6
NEG_INF = -1e30
ROPE_THETA = 10000.0
P_GQ, P_GK, P_GV, P_GG, P_SQ, P_SK, P_SV, P_GA, P_END = 0, 256, 512, 1024, 1536, 2048, 2176, 2304, 2432
D_IN = 2320
IN_SPLITS = (256, 256, 512, 512, 16, 512, 128, 128)
FF_TILE = 2816
WGRAD_TILE_MAX = 2432
N_DEV = 8
MESH = pl.DeviceIdType.MESH

ADAM_LR, ADAM_B1, ADAM_B2, ADAM_EPS, ADAM_WD, ADAM_STEP = 0.001, 0.9, 0.999, 1e-08, 0.01, 10

V7X_VMEM_BYTES = 64 << 20
VMEM_SPEC = pl.BlockSpec(memory_space=pltpu.VMEM)
SMEM_SPEC = pl.BlockSpec(memory_space=pltpu.SMEM)
ANY_SPEC = pl.BlockSpec(memory_space=pl.ANY)


def _params(semantics, vmem_mb=56):
    return pltpu.CompilerParams(dimension_semantics=semantics, vmem_limit_bytes=vmem_mb << 20)


def _row_tile(rows):
    return 416 if rows % 416 == 0 else BLK


def _blocks_per_step(blocks):
    return 5 if blocks % 5 == 0 else 1


def _nn(a, b):
    return lax.dot_general(a, b, (((1,), (0,)), ((), ())), preferred_element_type=F32)


def _nt(a, b):
    return lax.dot_general(a, b, (((1,), (1,)), ((), ())), preferred_element_type=F32)


def _tn(a, b):
    return lax.dot_general(a, b, (((0,), (0,)), ((), ())), preferred_element_type=F32)


def _rms(x):
    r = lax.rsqrt(jnp.mean(x * x, axis=-1, keepdims=True) + NORM_EPS)
    return x * r, r


def _rms_bwd(xn, r, w, dy):
    g = dy * w
    return r * (g - xn * jnp.mean(g * xn, axis=-1, keepdims=True))


def _sigmoid(x):
    return 1.0 / (1.0 + jnp.exp(-x))


def _colsum(x):
    return jnp.sum(x, axis=0, keepdims=True)


def _split_bf16(x):
    hi = x.astype(BF16)
    lo = (x - hi.astype(F32)).astype(BF16)
    return hi, lo


def _tri(lower):
    r = lax.broadcasted_iota(jnp.int32, (BLK, BLK), 0)
    c = lax.broadcasted_iota(jnp.int32, (BLK, BLK), 1)
    return (r >= c) if lower else (c >= r)


def _half_mask(width, half):
    lane = lax.broadcasted_iota(jnp.int32, (1, width), 1)
    return ((lane % 128) < 64) if half == 0 else ((lane % 128) >= 64)


def _rot_half(x):
    w = x.shape[-1]
    lane = lax.broadcasted_iota(jnp.int32, (1, w), 1)
    return jnp.where((lane % SWA_HD) < SWA_HD // 2, -pltpu.roll(x, w - SWA_HD // 2, 1), pltpu.roll(x, SWA_HD // 2, 1))


def _row_spec(tm, cols):
    return pl.BlockSpec((tm, cols), lambda i: (i, 0))


def _acc_spec(cols):
    return pl.BlockSpec((8, cols), lambda i: (0, 0))


def _acc_add(ref, first, value):
    @pl.when(first)
    def _():
        ref[...] = jnp.zeros_like(ref)
    ref[0:1, :] += value


def _behind_spec(tm):
    return pl.BlockSpec((pl.Element(tm), pl.Element(D_MODEL)),
                        lambda i: (pl.multiple_of(jnp.maximum(i * tm - BLK, 0), math.gcd(tm, BLK)), 0))


def _behind_front(ref, i, tm, front):
    blk = ref[...]
    return jnp.where(i == 0, jnp.concatenate([front, blk[0:tm - BLK]], axis=0), blk)


def _ffn_fwd(h, gpre, wg_t, wu_t, wd, gpost, tgt=None, front=None, mixed=None):
    with_loss, with_front, with_mixed = tgt is not None, front is not None, mixed is not None
    rows = h.shape[0] + (BLK if with_front else 0)
    tm = _row_tile(rows)
    nf = D_FF // FF_TILE

    def body(*refs):
        refs = list(refs)
        h_ref, gpre_ref, wg_ref, wu_ref, wd_ref, gpost_ref = refs[:6]
        del refs[:6]
        front_ref = refs.pop(0) if with_front else None
        cg_ref, cs_ref, wo_ref, gm_ref = (refs.pop(0), refs.pop(0), refs.pop(0), refs.pop(0)) if with_mixed else (None,) * 4
        t_ref = refs.pop(0) if with_loss else None
        hm_ref, m_ref = (refs.pop(0), refs.pop(0)) if with_mixed else (None, None)
        ho_ref = None if with_loss else refs.pop(0)
        a_ref, b_ref, s_ref, f_ref = refs[:4]
        dy_ref, loss_ref = refs[4:6] if with_loss else (None, None)
        acc = refs[-1]
        i = pl.program_id(0)
        h_in = _behind_front(h_ref, i, tm, front_ref[...]) if with_front else h_ref[...]
        if with_mixed:
            m = _nn(cg_ref[...], wo_ref[0:512, :]) + _nn(cs_ref[...], wo_ref[512:1024, :])
            m_ref[...] = m
            mn, _ = _rms(m)
            h_in = h_in + mn * gm_ref[...]
            hm_ref[...] = h_in
        hn, _ = _rms(h_in)
        n16 = (hn * gpre_ref[...]).astype(BF16)
        for j in range(nf):
            cols = slice(j * FF_TILE, (j + 1) * FF_TILE)
            a = _nt(n16, wg_ref[cols, :])
            b = _nt(n16, wu_ref[cols, :])
            a_ref[:, cols] = a.astype(BF16)
            b_ref[:, cols] = b.astype(BF16)
            s16 = (a * _sigmoid(a) * b).astype(BF16)
            s_ref[:, cols] = s16
            part = _nn(s16, wd_ref[cols, :])
            if j == 0:
                acc[...] = part
            else:
                acc[...] += part
        f = acc[...]
        f_ref[...] = f
        fn, _ = _rms(f)
        y = h_in + 0.5 * (fn * gpost_ref[...])
        if not with_loss:
            ho_ref[...] = y
        else:
            row = i * tm + lax.broadcasted_iota(jnp.int32, (tm, 1), 0)
            err = jnp.where(row >= BLK, y - _behind_front(t_ref, i, tm, jnp.zeros((BLK, D_MODEL), F32)), 0.0)
            dy_ref[...] = err * (1.0 / D_MODEL)
            part = 0.5 * jnp.sum(jnp.sum(err * err, axis=-1, keepdims=True) * (1.0 / D_MODEL), axis=0, keepdims=True)

            @pl.when(i == 0)
            def _():
                loss_ref[...] = jnp.zeros_like(loss_ref)
            loss_ref[...] += part

    row_f32 = _row_spec(tm, D_MODEL)
    behind = _behind_spec(tm)
    in_specs = [behind if with_front else row_f32, VMEM_SPEC, VMEM_SPEC, VMEM_SPEC, VMEM_SPEC, VMEM_SPEC]
    wide, full = jax.ShapeDtypeStruct((rows, D_FF), BF16), jax.ShapeDtypeStruct((rows, D_MODEL), F32)
    out_specs = [_row_spec(tm, D_FF), _row_spec(tm, D_FF), _row_spec(tm, D_FF), row_f32]
    out_shape = [wide, wide, wide, full]
    args = [h, gpre, wg_t, wu_t, wd, gpost]
    if not with_loss:
        out_specs.insert(0, row_f32)
        out_shape.insert(0, full)
    if with_front:
        in_specs.append(VMEM_SPEC)
        args.append(front)
    if with_mixed:
        in_specs += [_row_spec(tm, 512), _row_spec(tm, 512), VMEM_SPEC, VMEM_SPEC]
        args += list(mixed)
        out_specs = [row_f32, row_f32] + out_specs
        out_shape = [full, full] + out_shape
    if with_loss:
        in_specs.append(behind)
        args.append(tgt)
        out_specs += [row_f32, pl.BlockSpec((8, 128), lambda i: (0, 0))]
        out_shape += [jax.ShapeDtypeStruct((rows, D_MODEL), F32), jax.ShapeDtypeStruct((8, 128), F32)]
    return pl.pallas_call(
        body, name="ffn_fwd_loss" if with_loss else "ffn_fwd", grid=(rows // tm,),
        in_specs=in_specs, out_specs=out_specs, out_shape=out_shape,
        scratch_shapes=[pltpu.VMEM((tm, D_MODEL), F32)],
        compiler_params=_params(("arbitrary",), vmem_mb=62 if with_mixed else 56),
    )(*args)


def _ffn_bwd_act(dh_out, h, a, b, f, gpre, gpost, wg_t, wu_t, wd, name, front=None):
    with_front = front is not None
    rows = dh_out.shape[0]
    tm = _row_tile(rows)
    nf = D_FF // FF_TILE

    def body(dho_ref, h_ref, a_ref, b_ref, f_ref, gpre_ref, gpost_ref, wg_ref, wu_ref, wd_ref, *rest):
        front_ref = rest[0] if with_front else None
        dh_ref, da_ref, db_ref, df_ref, n_ref, dgpre_ref, dgpost_ref, acc = rest[-8:]
        first = pl.program_id(0) == 0
        dho = dho_ref[...]
        drr = 0.5 * dho
        fn, rf = _rms(f_ref[...])
        _acc_add(dgpost_ref, first, _colsum(drr * fn))
        df16 = _rms_bwd(fn, rf, gpost_ref[...], drr).astype(BF16)
        df_ref[...] = df16
        h_in = _behind_front(h_ref, pl.program_id(0), tm, front_ref[...]) if with_front else h_ref[...]
        hn, rh = _rms(h_in)
        n_ref[...] = (hn * gpre_ref[...]).astype(BF16)
        for j in range(nf):
            cols = slice(j * FF_TILE, (j + 1) * FF_TILE)
            ds = _nt(df16, wd_ref[cols, :])
            av = a_ref[:, cols].astype(F32)
            bv = b_ref[:, cols].astype(F32)
            sg = _sigmoid(av)
            db16 = (ds * (av * sg)).astype(BF16)
            da16 = (ds * bv * (sg * (1.0 + av * (1.0 - sg)))).astype(BF16)
            da_ref[:, cols] = da16
            db_ref[:, cols] = db16
            part = _nn(da16, wg_ref[cols, :]) + _nn(db16, wu_ref[cols, :])
            if j == 0:
                acc[...] = part
            else:
                acc[...] += part
        dn = acc[...]
        _acc_add(dgpre_ref, first, _colsum(dn * hn))
        dh_ref[...] = dho + _rms_bwd(hn, rh, gpre_ref[...], dn)

    row_f32 = _row_spec(tm, D_MODEL)
    row_ff = _row_spec(tm, D_FF)
    return pl.pallas_call(
        body, name=name, grid=(rows // tm,),
        in_specs=[row_f32, _behind_spec(tm) if with_front else row_f32, row_ff, row_ff, row_f32,
                  VMEM_SPEC, VMEM_SPEC, VMEM_SPEC, VMEM_SPEC, VMEM_SPEC] + ([VMEM_SPEC] if with_front else []),
        out_specs=[row_f32, row_ff, row_ff, row_f32, row_f32, _acc_spec(D_MODEL), _acc_spec(D_MODEL)],
        out_shape=[jax.ShapeDtypeStruct((rows, D_MODEL), F32), jax.ShapeDtypeStruct((rows, D_FF), BF16),
                   jax.ShapeDtypeStruct((rows, D_FF), BF16), jax.ShapeDtypeStruct((rows, D_MODEL), BF16),
                   jax.ShapeDtypeStruct((rows, D_MODEL), BF16), jax.ShapeDtypeStruct((8, D_MODEL), F32),
                   jax.ShapeDtypeStruct((8, D_MODEL), F32)],
        scratch_shapes=[pltpu.VMEM((tm, D_MODEL), F32)],
        compiler_params=_params(("arbitrary",), vmem_mb=62),
    )(dh_out, h, a, b, f, gpre, gpost, wg_t, wu_t, wd, *([front] if with_front else []))


def _wgrad(lhs, rhs, name, after=None, carry=None):
    rows, width = lhs.shape
    n_carry = 4
    carry_rows = 0 if carry is None else (carry.shape[0] - BLK) // n_carry
    tf = 256 if width % 256 == 0 else 128
    pieces = 5 if rows % 80 == 0 else 3 if rows % 48 == 0 else 1
    piece = rows // pieces
    tiles, slots = width // tf, 3

    def body(l_hbm, r_hbm, *rest):
        if carry is None:
            o_ref, l_buf, r_all, l_sems, r_sems = rest[-5:]
        else:
            c_hbm = rest[-8]
            o_ref, c_out, l_buf, r_all, l_sems, r_sems, c_sems = rest[-7:]
        j = pl.program_id(0)

        def move_c(c):
            return pltpu.make_async_copy(c_hbm.at[pl.ds(BLK + c * carry_rows, carry_rows), :],
                                         c_out.at[pl.ds(c * carry_rows, carry_rows), :], c_sems.at[c])

        def fetch_r(c):
            part = pl.ds(c * piece, piece)
            return pltpu.make_async_copy(r_hbm.at[part, :], r_all.at[part, :], r_sems.at[c])

        def fetch_l(tile):
            slot = tile % slots
            start = tile * tf if isinstance(tile, int) else pl.multiple_of(tile * tf, 128)
            return pltpu.make_async_copy(l_hbm.at[:, pl.ds(start, tf)], l_buf.at[slot], l_sems.at[slot])

        @pl.when(j == 0)
        def _():
            for tile in range(min(slots - 1, tiles)):
                fetch_l(tile).start()
            for c in range(pieces):
                fetch_r(c).start()
            if carry is not None:
                for c in range(n_carry):
                    move_c(c).start()

        @pl.when(j + slots - 1 < tiles)
        def _():
            fetch_l(j + slots - 1).start()
        fetch_l(j).wait()
        lhs_tile = l_buf.at[j % slots]

        @pl.when(j == 0)
        def _():
            total = None
            for c in range(pieces):
                fetch_r(c).wait()
                part = _tn(lhs_tile[c * piece:(c + 1) * piece, :], r_all[c * piece:(c + 1) * piece, :])
                total = part if total is None else total + part
            o_ref[...] = total.astype(BF16)

        @pl.when(j > 0)
        def _():
            o_ref[...] = _tn(lhs_tile[...], r_all[...]).astype(BF16)

        if carry is not None:
            @pl.when(j == tiles - 1)
            def _():
                for c in range(n_carry):
                    move_c(c).wait()

    out_spec = pl.BlockSpec((tf, D_MODEL), lambda j: (j, 0))
    out_shape = jax.ShapeDtypeStruct((width, D_MODEL), BF16)
    scratch = [pltpu.VMEM((slots, rows, tf), BF16), pltpu.VMEM((rows, D_MODEL), BF16),
               pltpu.SemaphoreType.DMA((slots,)), pltpu.SemaphoreType.DMA((pieces,))]
    if carry is not None:
        out_spec, scratch = [out_spec, ANY_SPEC], scratch + [pltpu.SemaphoreType.DMA((n_carry,))]
        out_shape = [out_shape, jax.ShapeDtypeStruct((n_carry * carry_rows, carry.shape[1]), carry.dtype)]
    return pl.pallas_call(
        body, name=name, grid=(tiles,),
        in_specs=[ANY_SPEC, ANY_SPEC] + ([] if after is None else [ANY_SPEC]) + ([] if carry is None else [ANY_SPEC]),
        out_specs=out_spec, out_shape=out_shape, scratch_shapes=scratch,
        compiler_params=_params(("arbitrary",)),
    )(lhs, rhs, *([] if after is None else [after]), *([] if carry is None else [carry]))


def _chunk_cumsum(x, lower):
    tri = jnp.where(_tri(lower), 1.0, 0.0).astype(BF16)
    hi, lo = _split_bf16(x)
    return _nn(tri, hi) + _nn(tri, lo)


def _mix_in(h, g, win_p, wa2_p, b_a, cos, sin):
    rows = h.shape[0]
    tm = 640 if rows % 640 == 0 else BLK

    def body(h_ref, g_ref, win_ref, wa2_ref, ba_ref, cos_ref, sin_ref,
             gq_ref, gk_ref, gv_ref, gg_ref, sq_ref, sk_ref, sv_ref, ga_ref, loga_ref, bc_ref, n_ref):
        hn, _ = _rms(h_ref[...])
        n16 = (hn * g_ref[...]).astype(BF16)
        n_ref[...] = n16
        proj = _nt(n16, win_ref[...])
        gq_ref[...] = proj[:, P_GQ:P_GK]
        gk_ref[...] = proj[:, P_GK:P_GV]
        gv_ref[...] = proj[:, P_GV:P_GG].astype(BF16)
        gg_ref[...] = proj[:, P_GG:P_SQ]
        c1, s1 = cos_ref[...], sin_ref[...]
        c4 = jnp.concatenate([c1, c1, c1, c1], axis=1)
        s4 = jnp.concatenate([s1, s1, s1, s1], axis=1)
        sq = proj[:, P_SQ:P_SK]
        sk = proj[:, P_SK:P_SV]
        sq_ref[...] = (sq * c4 + _rot_half(sq) * s4).astype(BF16)
        sk_ref[...] = (sk * c1 + _rot_half(sk) * s1).astype(BF16)
        sv_ref[...] = proj[:, P_SV:P_GA].astype(BF16)
        ga = proj[:, P_GA:P_END]
        ga_ref[...] = ga
        z = _nn(ga, wa2_ref[...]) + ba_ref[...]
        loga = (jnp.minimum(z, 0.0) - jnp.log(1.0 + jnp.exp(-jnp.abs(z)))) * (1.0 / GLA_TAU)
        loga_ref[...] = loga
        for c in range(tm // BLK):
            rs = slice(c * BLK, (c + 1) * BLK)
            bc_ref[rs, :] = _chunk_cumsum(loga[rs, :], True)

    f32 = lambda c: jax.ShapeDtypeStruct((rows, c), F32)
    b16 = lambda c: jax.ShapeDtypeStruct((rows, c), BF16)
    rs = lambda c: _row_spec(tm, c)
    return pl.pallas_call(
        body, name="mix_in", grid=(rows // tm,),
        in_specs=[rs(D_MODEL), VMEM_SPEC, VMEM_SPEC, VMEM_SPEC, VMEM_SPEC, rs(128), rs(128)],
        out_specs=[rs(256), rs(256), rs(512), rs(512), rs(512), rs(128), rs(128), rs(128), rs(256), rs(256), rs(D_MODEL)],
        out_shape=[f32(256), f32(256), b16(512), f32(512), b16(512), b16(128), b16(128), f32(128), f32(256), f32(256),
                   b16(D_MODEL)],
        compiler_params=_params(("arbitrary",)),
    )(h, g, win_p, wa2_p, b_a, cos, sin)


def _side_by_side(parts, name):
    steps, per_step = parts[0]["steps"], parts[0]["per_step"]
    assert all((p["steps"], p["per_step"]) == (steps, per_step) for p in parts)
    counts = [[len(p[key]) for p in parts] for key in ("in_specs", "out_specs", "scratch_shapes")]

    def body(*refs):
        groups, pos = [], 0
        for kind in counts:
            groups.append([])
            for n in kind:
                groups[-1].append(refs[pos:pos + n])
                pos += n
        programs = [p["program"](*groups[0][k], *groups[1][k], *groups[2][k]) for k, p in enumerate(parts)]

        def blocks(c, carries):
            return tuple(block(c, carry) for (block, _, _), carry in zip(programs, carries))
        carries = lax.fori_loop(0, per_step, blocks, tuple(first for _, first, _ in programs))
        for (_, _, finish), carry in zip(programs, carries):
            finish(carry)

    outs = pl.pallas_call(
        body, name=name, grid=(steps,),
        in_specs=[s for p in parts for s in p["in_specs"]], out_specs=[s for p in parts for s in p["out_specs"]],
        out_shape=[s for p in parts for s in p["out_shape"]],
        scratch_shapes=[s for p in parts for s in p["scratch_shapes"]],
        compiler_params=_params(("arbitrary",)),
    )(*[a for p in parts for a in p["args"]])
    split, pos = [], 0
    for n in counts[1]:
        split.append(outs[pos:pos + n])
        pos += n
    return split


def _gla_factors(q, k, bc):
    bm = bc[BLK // 2 - 1:BLK // 2, :]
    bl = bc[BLK - 1:BLK, :]
    e_q, e_k, e_qe, e_kd = jnp.exp(bc - bm), jnp.exp(bm - bc), jnp.exp(bc), jnp.exp(bl - bc)
    return (q * e_q, k * e_k, q * e_qe, k * e_kd), (e_q, e_k, e_qe, e_kd), jnp.exp(bl)


def _gla_fwd(gq, gk, gv, gg, bc, wgn):
    rows = gq.shape[0]
    nc = rows // BLK
    per_step = _blocks_per_step(nc)
    scale = GLA_DK ** -0.5

    def body(q_ref, k_ref, v_ref, gg_ref, bc_ref, wgn_ref, o_ref, cat_ref, sp_ref, st):
        @pl.when(pl.program_id(0) == 0)
        def _():
            st[...] = jnp.zeros_like(st)
        low = _tri(True)
        wgn_v = wgn_ref[...]

        def chunk(c, carry):
            rr = pl.ds(pl.multiple_of(c * BLK, BLK), BLK)
            for p in range(2):
                sl = slice(128 * p, 128 * p + 128)
                (qt, kt, qe, kd), _, ebl = _gla_factors(q_ref[rr, sl] * scale, k_ref[rr, sl], bc_ref[rr, sl])
                s_prev = st[p]
                sp_ref[c, p] = s_prev
                s16 = s_prev.astype(BF16)
                qt16 = qt.astype(BF16)
                s_new = s_prev * ebl
                for hh in range(2):
                    hs = slice(128 * (2 * p + hh), 128 * (2 * p + hh) + 128)
                    lm = _half_mask(128, hh)
                    vh = v_ref[rr, hs]
                    pm = jnp.where(low, _nt(qt16, jnp.where(lm, kt, 0.0).astype(BF16)), 0.0)
                    o = _nn(pm.astype(BF16), vh) + _nt(jnp.where(lm, qe, 0.0).astype(BF16), s16)
                    s_new = s_new + _tn(vh, jnp.where(lm, kd, 0.0).astype(BF16))
                    o_ref[rr, hs] = o
                    on, _ = _rms(o)
                    gate = gg_ref[rr, hs]
                    cat_ref[rr, hs] = (on * wgn_v * (gate * _sigmoid(gate))).astype(BF16)
                st[p] = s_new
            return carry
        return chunk, 0, lambda carry: None

    rs = lambda c: _row_spec(per_step * BLK, c)
    return dict(
        program=body, steps=nc // per_step, per_step=per_step,
        in_specs=[rs(256), rs(256), rs(512), rs(512), rs(256), VMEM_SPEC],
        out_specs=[rs(512), rs(512), pl.BlockSpec((per_step, 2, 128, 128), lambda i: (i, 0, 0, 0))],
        out_shape=[jax.ShapeDtypeStruct((rows, 512), F32), jax.ShapeDtypeStruct((rows, 512), BF16),
                   jax.ShapeDtypeStruct((nc, 2, 128, 128), F32)],
        scratch_shapes=[pltpu.VMEM((2, 128, 128), F32)],
        args=(gq, gk, gv, gg, bc, wgn))


def _gla_bwd(dcat, o_all, gq, gk, gv, gg, bc, sp, wgn):
    rows = gq.shape[0]
    nc = rows // BLK
    per_step = _blocks_per_step(nc)
    steps = nc // per_step
    scale = GLA_DK ** -0.5

    def body(dc_ref, o_ref, q_ref, k_ref, v_ref, gg_ref, bc_ref, sp_ref, wgn_ref,
             dq_ref, dk_ref, dv_ref, dgg_ref, dla_ref, dwgn_ref, dst):
        first = pl.program_id(0) == 0

        @pl.when(first)
        def _():
            dst[...] = jnp.zeros_like(dst)
        low, upp = _tri(True), _tri(False)
        last_row = lax.broadcasted_iota(jnp.int32, (BLK, 1), 0) == BLK - 1
        wgn_v = wgn_ref[...]

        def chunk(c, dwgn):
            rr = pl.ds(pl.multiple_of((per_step - 1 - c) * BLK, BLK), BLK)
            for p in range(2):
                sl = slice(128 * p, 128 * p + 128)
                (qt, kt, qe, kd), (e_q, e_k, e_qe, e_kd), ebl = _gla_factors(
                    q_ref[rr, sl] * scale, k_ref[rr, sl], bc_ref[rr, sl])
                s_prev = sp_ref[per_step - 1 - c, p]
                s16 = s_prev.astype(BF16)
                ds_next = dst[p]
                ds16 = ds_next.astype(BF16)
                qt16 = qt.astype(BF16)
                ds_new = ds_next * ebl
                dqt = jnp.zeros((BLK, 128), F32)
                dkt = jnp.zeros((BLK, 128), F32)
                dqe = jnp.zeros((BLK, 128), F32)
                dkd = jnp.zeros((BLK, 128), F32)
                for hh in range(2):
                    hs = slice(128 * (2 * p + hh), 128 * (2 * p + hh) + 128)
                    lm = _half_mask(128, hh)
                    on, ro = _rms(o_ref[rr, hs])
                    gate = gg_ref[rr, hs]
                    sg = _sigmoid(gate)
                    si = gate * sg
                    dog = dc_ref[rr, hs]
                    dwgn = dwgn + _colsum(dog * si * on)
                    dgg_ref[rr, hs] = dog * (on * wgn_v) * (sg * (1.0 + gate * (1.0 - sg)))
                    do16 = _rms_bwd(on, ro, wgn_v, dog * si).astype(BF16)
                    vh = v_ref[rr, hs]
                    ktm16 = jnp.where(lm, kt, 0.0).astype(BF16)
                    qtm16 = jnp.where(lm, qt, 0.0).astype(BF16)
                    qem16 = jnp.where(lm, qe, 0.0).astype(BF16)
                    kdm16 = jnp.where(lm, kd, 0.0).astype(BF16)
                    p_t = jnp.where(upp, _nt(ktm16, qt16), 0.0)
                    dp_t = jnp.where(upp, _nt(vh, do16), 0.0)
                    dp = jnp.where(low, _nt(do16, vh), 0.0)
                    dv_ref[rr, hs] = _nn(p_t.astype(BF16), do16) + _nt(kdm16, ds16)
                    dqt = dqt + _nn(dp.astype(BF16), ktm16)
                    dkt = dkt + _nn(dp_t.astype(BF16), qtm16)
                    dqe = dqe + jnp.where(lm, _nn(do16, s16), 0.0)
                    dkd = dkd + jnp.where(lm, _nn(vh, ds16), 0.0)
                    ds_new = ds_new + _tn(do16, qem16)
                debl = _colsum(ds_next * s_prev)
                dq_ref[rr, sl] = (dqt * e_q + dqe * e_qe) * scale
                dk_ref[rr, sl] = dkt * e_k + dkd * e_kd
                dkd_kd = dkd * kd
                db = dqt * qt - dkt * kt + dqe * qe - dkd_kd
                db = jnp.where(last_row, db + (_colsum(dkd_kd) + debl * ebl), db)
                dla_ref[rr, sl] = _chunk_cumsum(db, False)
                dst[p] = ds_new
            return dwgn

        def finish(dwgn):
            _acc_add(dwgn_ref, first, dwgn)
        return chunk, jnp.zeros((1, 128), F32), finish

    rev = lambda c: pl.BlockSpec((per_step * BLK, c), lambda i: (steps - 1 - i, 0))
    f32 = lambda c: jax.ShapeDtypeStruct((rows, c), F32)
    return dict(
        program=body, steps=steps, per_step=per_step,
        in_specs=[rev(512), rev(512), rev(256), rev(256), rev(512), rev(512), rev(256),
                  pl.BlockSpec((per_step, 2, 128, 128), lambda i: (steps - 1 - i, 0, 0, 0)), VMEM_SPEC],
        out_specs=[rev(256), rev(256), rev(512), rev(512), rev(256), _acc_spec(128)],
        out_shape=[f32(256), f32(256), f32(512), f32(512), f32(256), jax.ShapeDtypeStruct((8, 128), F32)],
        scratch_shapes=[pltpu.VMEM((2, 128, 128), F32)],
        args=(dcat, o_all, gq, gk, gv, gg, bc, sp, wgn))


def _swa_masks(i):
    t = lax.broadcasted_iota(jnp.int32, (BLK, BLK), 0)
    c = lax.broadcasted_iota(jnp.int32, (BLK, BLK), 1)
    own_side = c <= t
    band_ok = i >= jnp.where(own_side, 1, 2)
    meta_ok = (c % N_META) <= jnp.where(i >= 1, N_META, t - PAD_ROWS)
    return own_side, band_ok, meta_ok, c // N_META


def _swa_blocks(ref, i):
    prev = pl.multiple_of(jnp.maximum(i - 1, 0) * BLK, BLK)
    own = pl.multiple_of(i * BLK, BLK)
    return jnp.concatenate([ref[pl.ds(prev, BLK), :], ref[pl.ds(own, BLK), :]], axis=0), prev, own


def _swa_meta_operand(ref):
    blk = ref[0:BLK, :]
    swapped = pltpu.roll(blk, 64, 1)
    lo = jnp.where(_half_mask(128, 0), blk, swapped)
    hi = jnp.where(_half_mask(128, 1), blk, swapped)
    meta = jnp.concatenate([lo, lo, hi, hi], axis=1)[PAD_ROWS:BLK, :]
    tiled = jnp.concatenate([meta] * SWA_HEADS, axis=0)
    j = lax.broadcasted_iota(jnp.int32, tiled.shape, 0)
    lane = lax.broadcasted_iota(jnp.int32, tiled.shape, 1)
    return jnp.where(j // N_META == lane // SWA_HD, tiled, jnp.zeros_like(tiled))


def _swa_meta_fold(acc):
    out = jnp.zeros((N_META, 128), F32)
    for hd in range(SWA_HEADS):
        half, kv = hd % 2, hd // 4
        piece = acc[N_META * hd:N_META * (hd + 1), 128 * (hd // 2):128 * (hd // 2) + 128]
        piece = jnp.where(_half_mask(128, half), piece, 0.0)
        out = out + (piece if half == kv else pltpu.roll(piece, 64, 1))
    return out


def _by_head(group, per_head):
    out = jnp.zeros((BLK, BLK), F32)
    for hd, v in enumerate(per_head):
        out = jnp.where(group == hd, v, out)
    return out


def _place(x, kv):
    if kv == 0:
        lo = jnp.where(_half_mask(128, 0), x, jnp.zeros_like(x))
        return lo, pltpu.roll(lo, 64, 1)
    hi = jnp.where(_half_mask(128, 1), x, jnp.zeros_like(x))
    return pltpu.roll(hi, 64, 1), hi


def _swa_fwd(sq, sk, sv, sinks, wn):
    rows = sq.shape[0]
    nb = rows // BLK
    per_step = _blocks_per_step(nb)
    scale = SWA_HD ** -0.5

    def body(q_ref, k_ref, v_ref, sink_ref, wn_ref, o_ref, cat_ref, lse_ref, kp, vp):
        step = pl.program_id(0)

        @pl.when(step == 0)
        def _():
            kp[...] = _swa_meta_operand(k_ref)
            vp[...] = _swa_meta_operand(v_ref)

        def one_block(c, carry):
            i = step * per_step + c
            rr = pl.ds(pl.multiple_of(c * BLK, BLK), BLK)
            own_side, band_ok, meta_ok, group = _swa_masks(i)
            k2, _, _ = _swa_blocks(k_ref, i)
            v2, _, _ = _swa_blocks(v_ref, i)
            kz = (_place(k2, 0), _place(k2, 1))
            vz = (_place(v2, 0), _place(v2, 1))
            q_all = q_ref[rr, :]
            s_meta = jnp.where(meta_ok, _nt(q_all, kp[...]) * scale, NEG_INF)
            s_band, m = [], []
            for hd in range(SWA_HEADS):
                kv, half = hd // 4, hd % 2
                q_pair = q_all[:, 128 * (hd // 2):128 * (hd // 2) + 128]
                s2 = _nt(q_pair, kz[kv][half])
                s = jnp.where(band_ok, jnp.where(own_side, s2[:, BLK:], s2[:, :BLK]) * scale, NEG_INF)
                top = jnp.maximum(jnp.max(s, axis=-1, keepdims=True),
                                  jnp.max(jnp.where(group == hd, s_meta, NEG_INF), axis=-1, keepdims=True))
                s_band.append(s)
                m.append(jnp.maximum(top, sink_ref[0, hd]))
            e_meta = jnp.exp(s_meta - _by_head(group, m))
            o_meta = _nn(e_meta.astype(BF16), vp[...])
            outs = []
            for pr in range(4):
                o_pair = o_meta[:, 128 * pr:128 * pr + 128]
                rden = []
                for half in range(2):
                    hd = 2 * pr + half
                    kv = hd // 4
                    e = jnp.exp(s_band[hd] - m[hd])
                    den = (jnp.sum(e, axis=-1, keepdims=True)
                           + jnp.sum(jnp.where(group == hd, e_meta, 0.0), axis=-1, keepdims=True)
                           + jnp.exp(sink_ref[0, hd] - m[hd]))
                    lse_ref[rr, hd:hd + 1] = m[hd] + jnp.log(den)
                    rden.append(1.0 / den)
                    e2 = jnp.concatenate([jnp.where(own_side, 0.0, e), jnp.where(own_side, e, 0.0)], axis=1).astype(BF16)
                    o_pair = o_pair + _nn(e2, vz[kv][half])
                outs.append(o_pair * jnp.where(_half_mask(128, 0), rden[0], rden[1]))
            o = jnp.concatenate(outs, axis=1)
            o_ref[rr, :] = o
            on, _ = _rms(o)
            cat_ref[rr, :] = (on * wn_ref[...]).astype(BF16)
            return carry
        return one_block, 0, lambda carry: None

    return dict(
        program=body, steps=nb // per_step, per_step=per_step,
        in_specs=[_row_spec(per_step * BLK, 512), VMEM_SPEC, VMEM_SPEC, SMEM_SPEC, VMEM_SPEC],
        out_specs=[_row_spec(per_step * BLK, 512), _row_spec(per_step * BLK, 512), _row_spec(per_step * BLK, SWA_HEADS)],
        out_shape=[jax.ShapeDtypeStruct((rows, 512), F32), jax.ShapeDtypeStruct((rows, 512), BF16),
                   jax.ShapeDtypeStruct((rows, SWA_HEADS), F32)],
        scratch_shapes=[pltpu.VMEM((BLK, 512), BF16), pltpu.VMEM((BLK, 512), BF16)],
        args=(sq, sk, sv, sinks, wn))


def _swa_bwd(dcat, o_all, sq, sk, sv, lse, sinks, wn):
    rows = sq.shape[0]
    nb = rows // BLK
    per_step = _blocks_per_step(nb)
    steps = nb // per_step
    scale = SWA_HD ** -0.5

    def body(dc_ref, o_ref, q_ref, k_ref, v_ref, lse_ref, sink_ref, wn_ref, dq_ref, dk_ref, dv_ref, dsink_ref, dwn_ref,
             kp, vp, dkp, dvp):
        step = pl.program_id(0)

        @pl.when(step == 0)
        def _():
            dk_ref[...] = jnp.zeros_like(dk_ref)
            dv_ref[...] = jnp.zeros_like(dv_ref)
            dkp[...] = jnp.zeros_like(dkp)
            dvp[...] = jnp.zeros_like(dvp)
            kp[...] = _swa_meta_operand(k_ref)
            vp[...] = _swa_meta_operand(v_ref)

        def one_block(c, carry):
            i = step * per_step + c
            rr = pl.ds(pl.multiple_of(c * BLK, BLK), BLK)
            first = i == 0
            own_side, band_ok, meta_ok, group = _swa_masks(i)
            k2, prev, own = _swa_blocks(k_ref, i)
            v2, _, _ = _swa_blocks(v_ref, i)
            kz = (_place(k2, 0), _place(k2, 1))
            vz = (_place(v2, 0), _place(v2, 1))
            o = o_ref[rr, :]
            on, ro = _rms(o)
            dc = dc_ref[rr, :]
            _acc_add(dwn_ref, first, _colsum(dc * on))
            do = _rms_bwd(on, ro, wn_ref[...], dc)
            do_o = do * o
            do16 = do.astype(BF16)
            q_all = q_ref[rr, :]
            lse = [lse_ref[rr, hd:hd + 1] for hd in range(SWA_HEADS)]
            delta = [jnp.sum(jnp.where(_half_mask(128, hd % 2), do_o[:, 128 * (hd // 2):128 * (hd // 2) + 128], 0.0),
                             axis=-1, keepdims=True) for hd in range(SWA_HEADS)]
            s_meta = jnp.where(meta_ok, _nt(q_all, kp[...]) * scale, NEG_INF)
            p_meta = jnp.exp(s_meta - _by_head(group, lse))
            ds_meta16 = (p_meta * (_nt(do16, vp[...]) - _by_head(group, delta)) * scale).astype(BF16)
            dq_meta = _nn(ds_meta16, kp[...])
            dkp[...] += _tn(ds_meta16, q_all)
            dvp[...] += _tn(p_meta.astype(BF16), do16)
            own2 = jnp.concatenate([own_side.astype(jnp.int32)] * 2, axis=0) > 0
            ok2 = jnp.concatenate([band_ok.astype(jnp.int32)] * 2, axis=0) > 0

            def window(x2):
                return jnp.where(own2, x2[:, BLK:], x2[:, :BLK])

            def unwindow(x):
                return jnp.concatenate([jnp.where(own2, 0.0, x), jnp.where(own2, x, 0.0)], axis=1).astype(BF16)
            lane8 = lax.broadcasted_iota(jnp.int32, (1, 128), 1)
            dsink = jnp.zeros((1, 128), F32)
            dq_pairs = [dq_meta[:, 128 * pr:128 * pr + 128] for pr in range(4)]
            dk2 = [[None, None], [None, None]]
            dv2 = [[None, None], [None, None]]
            for kv in range(2):
                for half in range(2):
                    heads, pairs = (4 * kv + half, 4 * kv + 2 + half), (2 * kv, 2 * kv + 1)
                    q_s = jnp.concatenate([q_all[:, 128 * pr:128 * pr + 128] for pr in pairs], axis=0)
                    do_s = jnp.concatenate([do16[:, 128 * pr:128 * pr + 128] for pr in pairs], axis=0)
                    lse_s = jnp.concatenate([lse[hd] for hd in heads], axis=0)
                    delta_s = jnp.concatenate([delta[hd] for hd in heads], axis=0)
                    s = jnp.where(ok2, window(_nt(q_s, kz[kv][half])) * scale, NEG_INF)
                    prob = jnp.exp(s - lse_s)
                    for hd in heads:
                        dsink = dsink + jnp.where(lane8 == hd, -jnp.sum(jnp.exp(sink_ref[0, hd] - lse[hd]) * delta[hd]), 0.0)
                    ds2 = unwindow(prob * (window(_nt(do_s, vz[kv][half])) - delta_s) * scale)
                    dq_s = _nn(ds2, kz[kv][half])
                    dq_pairs[pairs[0]] = dq_pairs[pairs[0]] + dq_s[:BLK]
                    dq_pairs[pairs[1]] = dq_pairs[pairs[1]] + dq_s[BLK:]
                    dk2[kv][half] = _tn(ds2, q_s)
                    dv2[kv][half] = _tn(unwindow(prob), do_s)
            dq_ref[rr, :] = jnp.concatenate(dq_pairs, axis=1)
            _acc_add(dsink_ref, first, dsink)
            for ref, acc2 in ((dk_ref, dk2), (dv_ref, dv2)):
                tot = jnp.zeros((2 * BLK, 128), F32)
                for kv in range(2):
                    for half in range(2):
                        part = jnp.where(_half_mask(128, half), acc2[kv][half], 0.0)
                        tot = tot + (part if half == kv else pltpu.roll(part, 64, 1))
                ref[pl.ds(prev, BLK), :] += tot[:BLK]
                ref[pl.ds(own, BLK), :] += tot[BLK:]
            return carry

        def finish(carry):
            del carry

            @pl.when(step == steps - 1)
            def _():
                dk_ref[PAD_ROWS:BLK, :] += _swa_meta_fold(dkp[...])
                dv_ref[PAD_ROWS:BLK, :] += _swa_meta_fold(dvp[...])
        return one_block, jnp.zeros((1, 128), F32), finish

    full = pl.BlockSpec((rows, 128), lambda i: (0, 0))
    blocks = lambda cols: _row_spec(per_step * BLK, cols)
    return dict(
        program=body, steps=steps, per_step=per_step,
        in_specs=[blocks(512), blocks(512), blocks(512), VMEM_SPEC, VMEM_SPEC, blocks(SWA_HEADS), SMEM_SPEC, VMEM_SPEC],
        out_specs=[blocks(512), full, full, _acc_spec(128), _acc_spec(512)],
        out_shape=[jax.ShapeDtypeStruct((rows, 512), F32), jax.ShapeDtypeStruct((rows, 128), F32),
                   jax.ShapeDtypeStruct((rows, 128), F32), jax.ShapeDtypeStruct((8, 128), F32),
                   jax.ShapeDtypeStruct((8, 512), F32)],
        scratch_shapes=[pltpu.VMEM((BLK, 512), BF16), pltpu.VMEM((BLK, 512), BF16),
                        pltpu.VMEM((BLK, 512), F32), pltpu.VMEM((BLK, 512), F32)],
        args=(dcat, o_all, sq, sk, sv, lse, sinks, wn))


def _mix_out_bwd(dh, m, wout, gpost):
    rows = dh.shape[0]
    tm = _row_tile(rows)

    def body(dh_ref, m_ref, w_ref, g_ref, dcg_ref, dcs_ref, dm_ref, dg_ref):
        first = pl.program_id(0) == 0
        dhv = dh_ref[...]
        mn, rm = _rms(m_ref[...])
        _acc_add(dg_ref, first, _colsum(dhv * mn))
        dm16 = _rms_bwd(mn, rm, g_ref[...], dhv).astype(BF16)
        dm_ref[...] = dm16
        dcat = _nt(dm16, w_ref[...])
        dcg_ref[...] = dcat[:, 0:512]
        dcs_ref[...] = dcat[:, 512:1024]

    row_f32 = _row_spec(tm, D_MODEL)
    return pl.pallas_call(
        body, name="mix_out_bwd", grid=(rows // tm,),
        in_specs=[row_f32, row_f32, VMEM_SPEC, VMEM_SPEC],
        out_specs=[_row_spec(tm, 512), _row_spec(tm, 512), row_f32, _acc_spec(D_MODEL)],
        out_shape=[jax.ShapeDtypeStruct((rows, 512), F32), jax.ShapeDtypeStruct((rows, 512), F32),
                   jax.ShapeDtypeStruct((rows, D_MODEL), BF16), jax.ShapeDtypeStruct((8, D_MODEL), F32)],
        compiler_params=_params(("arbitrary",)),
    )(dh, m, wout, gpost)


def _mix_in_bwd(dh_out, h, g, win_p, wa2_p, cos, sin, loga, ga, dgq, dgk, dgv, dgg, dsq, dsk, dsv, dloga):
    rows = h.shape[0]
    tm = _row_tile(rows)

    def body(dho_ref, h_ref, g_ref, win_ref, wa2_ref, cos_ref, sin_ref, loga_ref, ga_ref,
             dgq_ref, dgk_ref, dgv_ref, dgg_ref, dsq_ref, dsk_ref, dsv_ref, dla_ref,
             dh_ref, dproj_ref, dwa2_ref, dg_ref, dba_ref):
        first = pl.program_id(0) == 0
        dz = dla_ref[...] * (1.0 / GLA_TAU) * (1.0 - jnp.exp(GLA_TAU * loga_ref[...]))
        _acc_add(dba_ref, first, _colsum(dz))
        dga = _nt(dz, wa2_ref[...])
        pa = _tn(ga_ref[...], dz)
        c1, s1 = cos_ref[...], sin_ref[...]
        c4 = jnp.concatenate([c1, c1, c1, c1], axis=1)
        s4 = jnp.concatenate([s1, s1, s1, s1], axis=1)
        dq_r, dk_r = dsq_ref[...], dsk_ref[...]
        dsq = dq_r * c4 - _rot_half(dq_r * s4)
        dsk = dk_r * c1 - _rot_half(dk_r * s1)
        dproj16 = jnp.concatenate(
            [dgq_ref[...], dgk_ref[...], dgv_ref[...], dgg_ref[...], dsq, dsk, dsv_ref[...], dga], axis=1).astype(BF16)
        dproj_ref[...] = dproj16
        dn = _nn(dproj16, win_ref[...])

        @pl.when(first)
        def _():
            dwa2_ref[...] = pa

        @pl.when(jnp.logical_not(first))
        def _():
            dwa2_ref[...] += pa
        hn, rh = _rms(h_ref[...])
        _acc_add(dg_ref, first, _colsum(dn * hn))
        dh_ref[...] = dho_ref[...] + _rms_bwd(hn, rh, g_ref[...], dn)

    rs = lambda c: _row_spec(tm, c)
    return pl.pallas_call(
        body, name="mix_in_bwd", grid=(rows // tm,),
        in_specs=[rs(D_MODEL), rs(D_MODEL), VMEM_SPEC, VMEM_SPEC, VMEM_SPEC, rs(128), rs(128), rs(256), rs(128),
                  rs(256), rs(256), rs(512), rs(512), rs(512), rs(128), rs(128), rs(256)],
        out_specs=[rs(D_MODEL), rs(P_END), pl.BlockSpec((128, 256), lambda i: (0, 0)), _acc_spec(D_MODEL), _acc_spec(256)],
        out_shape=[jax.ShapeDtypeStruct((rows, D_MODEL), F32), jax.ShapeDtypeStruct((rows, P_END), BF16),
                   jax.ShapeDtypeStruct((128, 256), F32), jax.ShapeDtypeStruct((8, D_MODEL), F32),
                   jax.ShapeDtypeStruct((8, 256), F32)],
        compiler_params=_params(("arbitrary",)),
    )(dh_out, h, g, win_p, wa2_p, cos, sin, loga, ga, dgq, dgk, dgv, dgg, dsq, dsk, dsv, dloga)


def _rope_tables(rows):
    pos = (jnp.arange(rows, dtype=jnp.int32) - PAD_ROWS).astype(F32)
    inv_freq = 1.0 / (ROPE_THETA ** (jnp.arange(0, SWA_HD, 2, dtype=F32) / SWA_HD))
    ang = pos[:, None] * inv_freq[None, :]
    return jnp.tile(jnp.cos(ang), (1, 4)), jnp.tile(jnp.sin(ang), (1, 4))


def _local_step(x, tgt, front, w, late_weights=None, on_grads=None, on_small=None):
    cos, sin = _rope_tables(x.shape[0] + BLK)
    g = {}

    def tell(group, names):
        for nm in names:
            g[nm] = grads_now[nm]
        return None if on_grads is None else on_grads(group, {nm: grads_now[nm] for nm in names})

    h1, a1, b1, s1, f1 = _ffn_fwd(x, w["ffn1_pre"], w["wg1"], w["wu1"], w["wd1"], w["ffn1_post"], front=front)
    if late_weights is not None:
        w = {**w, **late_weights("win", f1)}
    gq, gk, gv, gg, sq, sk, sv, ga, loga, bc, n2 = _mix_in(h1, w["mix_pre"], w["win"], w["wa2"], w["b_a"], cos, sin)
    (o_g, cat_g, sp), (o_s, cat_s, lse) = _side_by_side(
        [_gla_fwd(gq, gk, gv, gg, bc, w["gla_norm"]), _swa_fwd(sq, sk, sv, w["sinks"], w["swa_norm"])], "attention_fwd")
    if late_weights is not None:
        w = {**w, **late_weights("rest", lse)}
    h2, m, a2, b2, s2, f2, dy, loss = _ffn_fwd(h1, w["ffn2_pre"], w["wg2"], w["wu2"], w["wd2"], w["ffn2_post"], tgt,
                                               mixed=(cat_g, cat_s, w["wout"], w["mix_post"]))
    dh2, da, db, df, n3, g["ffn2_pre"], g["ffn2_post"] = _ffn_bwd_act(
        dy, h2, a2, b2, f2, w["ffn2_pre"], w["ffn2_post"], w["wg2"], w["wu2"], w["wd2"], "ffn2_bwd_act")
    grads_now = dict(wd2=_wgrad(s2, df, "ffn2_wgrad_down"), wg2=_wgrad(da, n3, "ffn2_wgrad_gate"),
                     wu2=_wgrad(db, n3, "ffn2_wgrad_up"))
    tok = tell("ffn2", ("wd2", "wg2", "wu2"))
    dcg, dcs, dm, g["mix_post"] = _mix_out_bwd(dh2, m, w["wout"], w["mix_post"] + (0.0 if tok is None else tok[0, 0]))
    (dgq, dgk, dgv, dgg, dloga, g["gla_norm"]), (dsq, dsk, dsv, g["sinks"], g["swa_norm"]) = _side_by_side(
        [_gla_bwd(dcg, o_g, gq, gk, gv, gg, bc, sp, w["gla_norm"]),
         _swa_bwd(dcs, o_s, sq, sk, sv, lse, w["sinks"], w["swa_norm"])], "attention_bwd")
    dh1, dproj, g["wa2"], g["mix_pre"], g["b_a"] = _mix_in_bwd(
        dh2, h1, w["mix_pre"], w["win"], w["wa2"], cos, sin, loga, ga, dgq, dgk, dgv, dgg, dsq, dsk, dsv, dloga)
    dh0, da, db, df, n1, g["ffn1_pre"], g["ffn1_post"] = _ffn_bwd_act(
        dh1, x, a1, b1, f1, w["ffn1_pre"], w["ffn1_post"], w["wg1"], w["wu1"], w["wd1"], "ffn1_bwd_act", front=front)
    tok = None if on_small is None else on_small(loss[0, 0], dh0, g)
    gd1, g["grad_x"] = _wgrad(s1, df, "ffn1_wgrad_down", after=tok, carry=dh0)
    grads_now = dict(wd1=gd1)
    tok = tell("ffn1_down", ("wd1",))
    grads_now = dict(wg1=_wgrad(da, n1, "ffn1_wgrad_gate", after=tok))
    tok = tell("ffn1_gate", ("wg1",))
    grads_now = dict(wu1=_wgrad(db, n1, "ffn1_wgrad_up", after=tok))
    tok = tell("ffn1_up", ("wu1",))
    grads_now = dict(wout=jnp.concatenate([_wgrad(cat_g, dm, "wout_wgrad_gla", after=tok),
                                           _wgrad(cat_s, dm, "wout_wgrad_swa", after=tok)], axis=0))
    tok = tell("wout", ("wout",))
    grads_now = dict(win=_wgrad(dproj, n2, "win_wgrad", after=tok))
    tell("mix", ("win",))
    return loss[0, 0], dh0, g


def _win_pad_rows(win_t):
    pad = jnp.zeros((P_END - P_GA - 16, win_t.shape[1]), win_t.dtype)
    return jnp.concatenate([win_t[0:1536], win_t[1552:2320], win_t[1536:1552], pad], axis=0)


def _win_unpad_rows(win_p):
    return jnp.concatenate([win_p[0:1536], win_p[P_GA:P_GA + 16], win_p[1536:P_GA]], axis=0)


def _place_on_mesh():
    return lax.axis_index("x"), lax.axis_index("y"), lax.axis_index("c")


def _dev_index(px, py, pc):
    return 4 * px + 2 * py + pc


def _other_devices(x, y, c):
    flip = lambda v, f: 1 - v if f else v
    return [(flip(x, fx), flip(y, fy), flip(c, fc)) for fx in (0, 1) for fy in (0, 1) for fc in (0, 1)][1:]


def _all_gather(shards):
    n = len(shards)

    def body(*refs):
        ins, outs = refs[:n], refs[n:2 * n]
        zeros_ref, send_sems, recv_sems, local_sems = refs[2 * n:]
        zeros_ref[...] = jnp.zeros_like(zeros_ref)
        x, y, c = _place_on_mesh()
        me, sibling = (x, y, c), (x, y, 1 - c)
        chips = [(1 - x, y), (x, 1 - y), (1 - x, 1 - y)]

        def rows(k, px, py, pc):
            r = ins[k].shape[0]
            return outs[k].at[pl.ds(pl.multiple_of(_dev_index(px, py, pc) * r, 8), r), :]

        def copy(k, slot, block, to, src=None):
            return pltpu.make_async_remote_copy(
                src_ref=rows(k, *block) if src is None else src, dst_ref=rows(k, *block),
                send_sem=send_sems.at[k, slot], recv_sem=recv_sems.at[k, slot], device_id=to, device_id_type=MESH)

        local = [pltpu.make_async_copy(ins[k], rows(k, *me), local_sems.at[k]) for k in range(n)]
        sends = []
        for k in range(n):
            local[k].start()
            sends.append(copy(k, 0, me, sibling, src=ins[k]))
            sends += [copy(k, 1 + j, me, (*chip, c), src=ins[k]) for j, chip in enumerate(chips)]
        for cp in sends:
            cp.start()
        for k in range(n):
            for j, chip in enumerate(chips):
                copy(k, 1 + j, (*chip, c), me).wait_recv()
                passed = copy(k, 4 + j, (*chip, c), sibling)
                passed.start()
                sends.append(passed)
        for k in range(n):
            copy(k, 0, sibling, me).wait_recv()
            for j, chip in enumerate(chips):
                copy(k, 4 + j, (*chip, 1 - c), me).wait_recv()
        for cp in sends:
            cp.wait_send()
        for cp in local:
            cp.wait()

    return pl.pallas_call(
        body, name="all_gather_weights",
        in_specs=[ANY_SPEC] * n, out_specs=[ANY_SPEC] * n + [VMEM_SPEC],
        out_shape=[jax.ShapeDtypeStruct((N_DEV * s.shape[0], s.shape[1]), s.dtype) for s in shards]
        + [jax.ShapeDtypeStruct((8, 128), F32)],
        scratch_shapes=[pltpu.SemaphoreType.DMA((n, 7)), pltpu.SemaphoreType.DMA((n, 7)), pltpu.SemaphoreType.DMA((n,))],
    )(*shards)


HBM_SPEC = pl.BlockSpec(memory_space=pltpu.HBM)
SEM_SPEC = pl.BlockSpec(memory_space=pltpu.SEMAPHORE)
DATAFLOW = pltpu.SideEffectType.DATAFLOW_SIDE_EFFECTING


GATHER, SCATTER, SCATTER_CHIPS = "gather", "scatter", "scatter among chips"


def _exchange_peers(kind):
    x, y, c = _place_on_mesh()
    if kind == SCATTER_CHIPS:
        peers = [(1 - x, y, c), (x, 1 - y, c), (1 - x, 1 - y, c)]
        return peers, [2 * p[0] + p[1] for p in peers], 2 * x + y, 4
    peers = _other_devices(x, y, c)
    return peers, [_dev_index(*p) for p in peers], _dev_index(x, y, c), N_DEV


def _exchange_copies(srcs, lands, send_sems, recv_sems, own_sems, kind, arriving):
    peers, theirs, me, blocks = _exchange_peers(kind)
    remote, local = [], []
    for k, (src, land) in enumerate(zip(srcs, lands)):
        r = land.shape[0] // blocks

        def block(ref, d):
            return ref.at[pl.ds(pl.multiple_of(d * r, 8), r), :]

        for f, (peer, him) in enumerate(zip(peers, theirs)):
            mine, his = (him, me) if arriving else (me, him)
            sem = len(peers) * k + f
            remote.append(pltpu.make_async_remote_copy(
                src_ref=src if kind == GATHER else block(src, his), dst_ref=block(land, mine),
                send_sem=send_sems.at[sem], recv_sem=recv_sems.at[sem], device_id=peer, device_id_type=MESH))
        local.append(pltpu.make_async_copy(src if kind == GATHER else block(src, me), block(land, me), own_sems.at[k]))
    return remote, local


def _exchange_start(srcs, kind, name):
    n = len(srcs)
    lands = [lax.empty((N_DEV * s.shape[0], s.shape[1]) if kind == GATHER else s.shape, s.dtype) for s in srcs]
    sems = (3 if kind == SCATTER_CHIPS else 7) * n

    def body(*refs):
        remote, local = _exchange_copies(refs[:n], refs[n:2 * n], *refs[2 * n:2 * n + 3], kind, False)
        for cp in remote + local:
            cp.start()
        refs[-1][...] = jnp.zeros_like(refs[-1])

    both = list(srcs) + list(lands)
    outs = pl.pallas_call(
        body, name=name,
        out_shape=(pltpu.SemaphoreType.DMA((sems,)), pltpu.SemaphoreType.DMA((sems,)), pltpu.SemaphoreType.DMA((n,)),
                   *[pltpu.HBM(a.shape, a.dtype) for a in both], jax.ShapeDtypeStruct((8, 128), F32)),
        in_specs=[HBM_SPEC] * (2 * n), out_specs=(SEM_SPEC, SEM_SPEC, SEM_SPEC, *[HBM_SPEC] * (2 * n), VMEM_SPEC),
        input_output_aliases={i: 3 + i for i in range(2 * n)},
        compiler_params=pltpu.CompilerParams(has_side_effects=DATAFLOW),
    )(*[pltpu.with_memory_space_constraint(a, pltpu.HBM) for a in both])
    return outs[0:3], outs[3:3 + n], outs[3 + n:3 + 2 * n], outs[-1]


def _exchange_wait(started, kind, after, name):
    sems, srcs, lands, _ = started
    n = len(srcs)

    def body(*refs):
        args = (refs[:n], refs[n:2 * n], *refs[2 * n:2 * n + 3], kind)
        going, local = _exchange_copies(*args, False)
        for cp in going:
            cp.wait_send()
        for cp in local:
            cp.wait()
        for cp in _exchange_copies(*args, True)[0]:
            cp.wait_recv()

    both = list(srcs) + list(lands)
    outs = pl.pallas_call(
        body, name=name, out_shape=[pltpu.HBM(a.shape, a.dtype) for a in both],
        in_specs=[HBM_SPEC] * (2 * n) + [SEM_SPEC, SEM_SPEC, SEM_SPEC, ANY_SPEC], out_specs=[HBM_SPEC] * (2 * n),
        input_output_aliases={i: i for i in range(2 * n)},
        compiler_params=pltpu.CompilerParams(has_side_effects=DATAFLOW),
    )(*both, *sems, after)
    return outs[n:]


def _sibling_reduce(part, name):
    r, cols = part.shape[0] // N_DEV, part.shape[1]

    def body(p_ref, o_ref, mine, got, send_sems, recv_sems, own_sems):
        x, y, c = _place_on_mesh()

        def block(d):
            return p_ref.at[pl.ds(pl.multiple_of(d * r, 8), r), :]
        swaps = [pltpu.make_async_remote_copy(
            src_ref=block(2 * j + 1 - c), dst_ref=got.at[j], send_sem=send_sems.at[j], recv_sem=recv_sems.at[j],
            device_id=(x, y, 1 - c), device_id_type=MESH) for j in range(4)]
        keeps = [pltpu.make_async_copy(block(2 * j + c), mine.at[j], own_sems.at[j]) for j in range(4)]
        for cp in swaps + keeps:
            cp.start()
        for j in range(4):
            keeps[j].wait()
            swaps[j].wait()
            o_ref[pl.ds(j * r, r), :] = (mine[j].astype(F32) + got[j].astype(F32)).astype(o_ref.dtype)

    return pl.pallas_call(
        body, name=name, in_specs=[ANY_SPEC], out_specs=VMEM_SPEC,
        out_shape=jax.ShapeDtypeStruct((4 * r, cols), part.dtype),
        scratch_shapes=[pltpu.VMEM((4, r, cols), part.dtype), pltpu.VMEM((4, r, cols), part.dtype),
                        pltpu.SemaphoreType.DMA((4,)), pltpu.SemaphoreType.DMA((4,)), pltpu.SemaphoreType.DMA((4,))],
        compiler_params=pltpu.CompilerParams(vmem_limit_bytes=32 << 20),
    )(part)


def _sum_partials(parts, name, blocks=N_DEV):
    n = len(parts)

    def body(*refs):
        ins, outs = refs[:n], refs[n:]
        first = pl.program_id(0) == 0
        for i_ref, o_ref in zip(ins, outs):
            v = i_ref[...].astype(F32)

            @pl.when(first)
            def _():
                o_ref[...] = v

            @pl.when(jnp.logical_not(first))
            def _():
                o_ref[...] += v

    shapes = [(p.shape[0] // blocks, p.shape[1]) for p in parts]
    return pl.pallas_call(
        body, name=name, grid=(blocks,),
        in_specs=[pl.BlockSpec(s, lambda j: (j, 0)) for s in shapes],
        out_specs=[pl.BlockSpec(s, lambda j: (0, 0)) for s in shapes],
        out_shape=[jax.ShapeDtypeStruct(s, F32) for s in shapes],
        compiler_params=_params(("arbitrary",)),
    )(*parts)


def _adamw_update(w, g, m, v):
    m = ADAM_B1 * m + (1.0 - ADAM_B1) * g
    v = ADAM_B2 * v + (1.0 - ADAM_B2) * (g * g)
    m_hat = m * (1.0 / (1.0 - ADAM_B1 ** ADAM_STEP))
    v_hat = v * (1.0 / (1.0 - ADAM_B2 ** ADAM_STEP))
    return -ADAM_LR * (m_hat / (jnp.sqrt(v_hat) + ADAM_EPS) + ADAM_WD * w), m, v


def _sum_adamw(parts, w, m, v, blocks, name):
    shape = w.shape

    def body(p_ref, w_ref, m_ref, v_ref, g_ref, d_ref, mo_ref, vo_ref):
        j = pl.program_id(0)
        part = p_ref[...].astype(F32)

        @pl.when(j == 0)
        def _():
            g_ref[...] = part

        @pl.when(j > 0)
        def _():
            g_ref[...] += part

        @pl.when(j == blocks - 1)
        def _():
            d_ref[...], mo_ref[...], vo_ref[...] = _adamw_update(w_ref[...], g_ref[...], m_ref[...], v_ref[...])

    held = pl.BlockSpec(shape, lambda j: (0, 0))
    return pl.pallas_call(
        body, name=name, grid=(blocks,),
        in_specs=[pl.BlockSpec(shape, lambda j: (j, 0)), held, held, held],
        out_specs=[held] * 4, out_shape=[jax.ShapeDtypeStruct(shape, F32)] * 4,
        compiler_params=_params(("arbitrary",)),
    )(parts, w, m, v)


def _adamw(ws, gs, ms, vs, name):
    n = len(ws)

    def body(*refs):
        w_r, g_r, m_r, v_r = refs[:n], refs[n:2 * n], refs[2 * n:3 * n], refs[3 * n:4 * n]
        d_o, m_o, v_o = refs[4 * n:5 * n], refs[5 * n:6 * n], refs[6 * n:7 * n]
        for k in range(n):
            d_o[k][...], m_o[k][...], v_o[k][...] = _adamw_update(w_r[k][...], g_r[k][...], m_r[k][...], v_r[k][...])

    shapes = [jax.ShapeDtypeStruct(w.shape, F32) for w in ws]
    outs = pl.pallas_call(
        body, name=name, in_specs=[VMEM_SPEC] * (4 * n), out_specs=[VMEM_SPEC] * (3 * n), out_shape=shapes * 3,
        compiler_params=pltpu.CompilerParams(vmem_limit_bytes=56 << 20),
    )(*ws, *gs, *ms, *vs)
    return outs[:n], outs[n:2 * n], outs[2 * n:]


WEIGHT_NAMES = ("meta_tokens", "ffn1_pre_norm", "ffn1_w_gate", "ffn1_w_up", "ffn1_w_down", "ffn1_post_norm", "mix_pre_norm",
                "w_in", "gla_w_a2", "gla_b_a", "gla_out_norm", "swa_sinks", "swa_out_norm", "w_out", "mix_post_norm",
                "ffn2_pre_norm", "ffn2_w_gate", "ffn2_w_up", "ffn2_w_down", "ffn2_post_norm")
WIN_SHARD = D_IN // N_DEV
WIN_SHARD_PAD = 304
SLAB_VECTORS = ("ffn1_pre", "ffn1_post", "mix_pre", "mix_post", "ffn2_pre", "ffn2_post")
SLAB_ROWS = 32


def kernel(x, meta_tokens, ffn1_pre_norm, ffn1_w_gate, ffn1_w_up, ffn1_w_down, ffn1_post_norm, mix_pre_norm, w_in, gla_w_a2, gla_b_a, gla_out_norm, swa_sinks, swa_out_norm, w_out, mix_post_norm, ffn2_pre_norm, ffn2_w_gate, ffn2_w_up, ffn2_w_down, ffn2_post_norm, loss_target, m_meta_tokens, m_ffn1_pre_norm, m_ffn1_w_gate, m_ffn1_w_up, m_ffn1_w_down, m_ffn1_post_norm, m_mix_pre_norm, m_w_in, m_gla_w_a2, m_gla_b_a, m_gla_out_norm, m_swa_sinks, m_swa_out_norm, m_w_out, m_mix_post_norm, m_ffn2_pre_norm, m_ffn2_w_gate, m_ffn2_w_up, m_ffn2_w_down, m_ffn2_post_norm, v_meta_tokens, v_ffn1_pre_norm, v_ffn1_w_gate, v_ffn1_w_up, v_ffn1_w_down, v_ffn1_post_norm, v_mix_pre_norm, v_w_in, v_gla_w_a2, v_gla_b_a, v_gla_out_norm, v_swa_sinks, v_swa_out_norm, v_w_out, v_mix_post_norm, v_ffn2_pre_norm, v_ffn2_w_gate, v_ffn2_w_up, v_ffn2_w_down, v_ffn2_post_norm):
    given = dict(locals())
    W = {n: given[n] for n in WEIGHT_NAMES}
    M = {n: given["m_" + n] for n in WEIGHT_NAMES}
    V = {n: given["v_" + n] for n in WEIGHT_NAMES}
    dev = _dev_index(*_place_on_mesh())

    def t16(w):
        return w[0].T.astype(BF16)

    small = jnp.concatenate([W["meta_tokens"], jnp.pad(W["gla_w_a2"][0], ((0, 0), (0, 96)))], axis=0)
    wg1, wu1, wd1, small_g, gathered_zeros = _all_gather(
        [t16(W["ffn1_w_gate"]), t16(W["ffn1_w_up"]), W["ffn1_w_down"][0].astype(BF16), small])
    def after_zero(shard, zeros):
        return shard + zeros[0:1, 0:1].astype(shard.dtype)
    win_shard = jnp.pad(t16(W["w_in"]), ((0, WIN_SHARD_PAD - WIN_SHARD), (0, 0)))
    win_shard = after_zero(win_shard, gathered_zeros)
    mid = _exchange_start([win_shard], GATHER, "gather_w_in_start")
    late_shards = [after_zero(W["w_out"][0].astype(BF16), mid[3]), t16(W["ffn2_w_gate"]), t16(W["ffn2_w_up"]),
                   W["ffn2_w_down"][0].astype(BF16)]
    late = _exchange_start(late_shards, GATHER, "gather_late_weights_start")

    def late_weights(what, after):
        if what == "win":
            win_g, = _exchange_wait(mid, GATHER, after, "gather_w_in_wait")
            win_t = win_g.reshape(N_DEV, WIN_SHARD_PAD, D_MODEL)[:, :WIN_SHARD].reshape(D_IN, D_MODEL)
            return dict(win=_win_pad_rows(win_t))
        wout, wg2, wu2, wd2 = _exchange_wait(late, GATHER, after, "gather_late_weights_wait")
        return dict(wout=wout, wg2=wg2, wu2=wu2, wd2=wd2)

    small_g = small_g.reshape(N_DEV, 32, 128)
    meta_full = small_g[:, :N_META].transpose(1, 0, 2).reshape(N_META, D_MODEL)
    wa2_full = small_g[:, N_META:, :32].transpose(1, 0, 2).reshape(16, 256)
    w = dict(
        ffn1_pre=W["ffn1_pre_norm"] + late[3][0, 0], ffn1_post=W["ffn1_post_norm"], mix_pre=W["mix_pre_norm"],
        mix_post=W["mix_post_norm"], ffn2_pre=W["ffn2_pre_norm"], ffn2_post=W["ffn2_post_norm"], b_a=W["gla_b_a"],
        gla_norm=W["gla_out_norm"], sinks=W["swa_sinks"], swa_norm=W["swa_out_norm"], wg1=wg1, wu1=wu1, wd1=wd1,
        wa2=jnp.pad(wa2_full, ((0, 112), (0, 0))))

    in_flight = []

    def on_grads(group, grads):
        parts = []
        for nm, p in grads.items():
            if nm == "win":
                p = _win_unpad_rows(p).reshape(N_DEV, WIN_SHARD, D_MODEL)
                p = jnp.pad(p, ((0, 0), (0, WIN_SHARD_PAD - WIN_SHARD), (0, 0))).reshape(N_DEV * WIN_SHARD_PAD, D_MODEL)
            parts.append(p)
        kind = SCATTER_CHIPS if group == "mix" else SCATTER
        if kind == SCATTER_CHIPS:
            parts = [_sibling_reduce(p, "pair_" + group + "_" + nm) for nm, p in zip(grads, parts)]
        started = _exchange_start(parts, kind, "scatter_" + group + "_start")
        in_flight.append((group, list(grads), started, kind))
        return started[3]

    small_flight = []

    def on_small(loss, dh0, g):
        packed = jnp.concatenate([g["b_a"][0:1], g["gla_norm"][0:1], g["sinks"][0:1], g["swa_norm"][0:1]], axis=1)
        slab = jnp.concatenate([g[k][0:1] for k in SLAB_VECTORS] + [packed, jnp.full((1, D_MODEL), loss, F32),
                               g["wa2"][:16].reshape(4, D_MODEL), jnp.zeros((4, D_MODEL), F32), dh0[PAD_ROWS:BLK]], axis=0)
        small_flight.append(_exchange_start([slab], GATHER, "gather_small_grads_start"))
        return small_flight[0][3]

    front = jnp.concatenate([jnp.zeros((PAD_ROWS, D_MODEL), F32), meta_full], axis=0)
    loss, dh0, g = _local_step(x[0], loss_target[0], front, w, late_weights, on_grads, on_small)
    grad_x = g["grad_x"][None]

    land, = _exchange_wait(small_flight[0], GATHER, in_flight[-1][2][3], "gather_small_grads_wait")
    tot = _sum_partials([land], "sum_small_grads")[0]
    loss = tot[7, 0]
    small_grads = dict(
        ffn1_pre_norm=tot[0:1], ffn1_post_norm=tot[1:2], mix_pre_norm=tot[2:3], mix_post_norm=tot[3:4],
        ffn2_pre_norm=tot[4:5], ffn2_post_norm=tot[5:6], gla_b_a=tot[6:7, 0:256], gla_out_norm=tot[6:7, 256:384],
        swa_sinks=tot[6:7, 384:392], swa_out_norm=tot[6:7, 512:1024],
        gla_w_a2=lax.dynamic_slice_in_dim(tot[8:12].reshape(16, 256), dev * 32, 32, axis=1)[None],
        meta_tokens=lax.dynamic_slice_in_dim(tot[16:32], dev * 128, 128, axis=1))

    big = dict(wg1=("ffn1_w_gate", True), wu1=("ffn1_w_up", True), wd1=("ffn1_w_down", False), win=("w_in", True),
               wout=("w_out", False), wg2=("ffn2_w_gate", True), wu2=("ffn2_w_up", True), wd2=("ffn2_w_down", False))
    grads = dict(small_grads)
    delta, new_m, new_v = {}, {}, {}
    names = [n for n in WEIGHT_NAMES if n not in [full for full, _ in big.values()]]
    two_d = lambda a: a.reshape(-1, a.shape[-1])
    d_, m_, v_ = _adamw([two_d(W[n]) for n in names], [two_d(grads[n]) for n in names],
                        [two_d(M[n]) for n in names], [two_d(V[n]) for n in names], "adamw_small")
    for k, n in enumerate(names):
        delta[n], new_m[n], new_v[n] = d_[k].reshape(W[n].shape), m_[k].reshape(W[n].shape), v_[k].reshape(W[n].shape)

    before_wait = d_[0] + in_flight[-1][2][3][0, 0]
    for group, shorts, started, kind in in_flight:
        lands = _exchange_wait(started, kind, before_wait, "scatter_" + group + "_wait")
        blocks = 4 if kind == SCATTER_CHIPS else N_DEV
        for short, land in zip(shorts, lands):
            n, transposed = big[short]
            to_slab = (lambda a: a[0].T) if transposed else (lambda a: a[0])
            from_slab = (lambda a: a.T[None]) if transposed else (lambda a: a[None])
            if short == "win":
                g_slab = _sum_partials([land], "sum_" + n, blocks)[0][:WIN_SHARD]
                d_, m_, v_ = _adamw([to_slab(W[n])], [g_slab], [to_slab(M[n])], [to_slab(V[n])], "adamw_" + n)
                d_, m_, v_ = d_[0], m_[0], v_[0]
            else:
                g_slab, d_, m_, v_ = _sum_adamw(land, to_slab(W[n]), to_slab(M[n]), to_slab(V[n]), blocks, "adamw_" + n)
            grads[n], delta[n], new_m[n], new_v[n] = from_slab(g_slab), from_slab(d_), from_slab(m_), from_slab(v_)
            before_wait = d_
    return (loss, grad_x, *[grads[n] for n in WEIGHT_NAMES], *[delta[n] for n in WEIGHT_NAMES],
            *[new_m[n] for n in WEIGHT_NAMES], *[new_v[n] for n in WEIGHT_NAMES])
```
